```python
import math
import jax, jax.numpy as jnp
from jax import lax
import numpy as np

D_MODEL = 2048
BATCH = 8
SEQ = 4096
DEPTH = 1

N_MEM = 256
NORM_EPS = 1e-6

SSD_WIDTH = D_MODEL // 2
SSD_HEAD_DIM = 64
SSD_HEADS = SSD_WIDTH // SSD_HEAD_DIM
SSD_GROUPS = 2
SSD_HEADS_PER_GROUP = SSD_HEADS // SSD_GROUPS
SSD_STATE = 128
SSD_CONV = 4
SSD_CHUNK = 128
SSD_CONV_DIM = SSD_WIDTH + 2 * SSD_GROUPS * SSD_STATE
SSD_IN = SSD_WIDTH + SSD_CONV_DIM + SSD_HEADS

RWKV_WIDTH = D_MODEL - SSD_WIDTH
RWKV_HEAD_DIM = 64
RWKV_HEADS = RWKV_WIDTH // RWKV_HEAD_DIM
RWKV_DECAY_RANK = 96
RWKV_AAA_RANK = 96
RWKV_GATE_RANK = 256
RWKV_IN = 3 * RWKV_WIDTH + RWKV_DECAY_RANK + RWKV_AAA_RANK + RWKV_GATE_RANK
RWKV_LN_EPS = 64e-5

MIX_WIDTH = SSD_WIDTH + RWKV_WIDTH
D_IN = SSD_IN + RWKV_IN

XATTN_HEADS = 4
XATTN_HEAD_DIM = D_MODEL // XATTN_HEADS

D_FF = 4 * D_MODEL

kernel_name = "hymba_ssd_rwkv7_memxattn_block"


def rms_norm(x, g, eps=NORM_EPS):
    xf = x.astype(jnp.float32)
    y = xf * lax.rsqrt(jnp.mean(xf * xf, axis=-1, keepdims=True) + eps)
    return (y * g).astype(x.dtype)


def causal_depthwise_conv(u, w, b):
    y = lax.conv_general_dilated(
        u, w, window_strides=(1,), padding=[(w.shape[0] - 1, 0)],
        dimension_numbers=("NWC", "WIO", "NWC"), feature_group_count=u.shape[-1])
    return y + b


def ssd_mixer(u, conv_w, conv_b, dt_bias, a_log, d_skip, norm_g):
    f32 = jnp.float32
    bsz, seq, _ = u.shape
    G, E, P, N, Q = SSD_GROUPS, SSD_HEADS_PER_GROUP, SSD_HEAD_DIM, SSD_STATE, SSD_CHUNK
    nc = seq // Q
    z, xbc, dt = jnp.split(u, [SSD_WIDTH, SSD_WIDTH + SSD_CONV_DIM], axis=-1)
    xbc = jax.nn.silu(causal_depthwise_conv(xbc, conv_w, conv_b))
    xs, bm, cm = jnp.split(xbc, [SSD_WIDTH, SSD_WIDTH + G * N], axis=-1)
    xs = xs.astype(f32).reshape(bsz, nc, Q, G, E, P)
    bm = bm.astype(f32).reshape(bsz, nc, Q, G, N)
    cm = cm.astype(f32).reshape(bsz, nc, Q, G, N)
    dt = jax.nn.softplus(dt.astype(f32) + dt_bias.astype(f32))
    a = -jnp.exp(a_log.astype(f32))
    dt_c = dt.reshape(bsz, nc, Q, G, E)
    xdt = xs * dt_c[..., None]
    da = jnp.transpose(dt_c * a.reshape(G, E), (0, 1, 3, 4, 2))
    cs = jnp.cumsum(da, axis=-1)
    causal = jnp.tril(jnp.ones((Q, Q), dtype=bool))
    seg = cs[..., :, None] - cs[..., None, :]
    lmat = jnp.where(causal, jnp.exp(jnp.where(causal, seg, 0.0)), 0.0)
    cb = jnp.einsum('bclgn,bcsgn->bcgls', cm, bm)
    y_diag = jnp.einsum('bcgls,bcgels,bcsgep->bclgep', cb, lmat, xdt)
    decay_to_end = jnp.exp(cs[..., -1:] - cs)
    chunk_states = jnp.einsum('bcsgn,bcges,bcsgep->bcgepn', bm, decay_to_end, xdt)
    chunk_decay = jnp.exp(cs[..., -1])

    def carry_state(h, inp):
        st, dec = inp
        return h * dec[..., None, None] + st, h

    h0 = jnp.zeros((bsz, G, E, P, N), f32)
    _, start_states = lax.scan(carry_state, h0,
                               (jnp.moveaxis(chunk_states, 1, 0), jnp.moveaxis(chunk_decay, 1, 0)))
    start_states = jnp.moveaxis(start_states, 0, 1)
    y_off = jnp.einsum('bclgn,bcgepn,bcgel->bclgep', cm, start_states, jnp.exp(cs))
    y = y_diag + y_off + xs * d_skip.astype(f32).reshape(G, E, 1)
    y = y.reshape(bsz, seq, SSD_WIDTH) * jax.nn.silu(z.astype(f32))
    yg = y.reshape(bsz, seq, G, SSD_WIDTH // G)
    yg = yg * lax.rsqrt(jnp.mean(yg * yg, axis=-1, keepdims=True) + NORM_EPS)
    y = yg.reshape(bsz, seq, SSD_WIDTH) * norm_g
    return y.astype(u.dtype)


def rwkv7_mixer(u, mu, w0, w2, a0, a2, g2, k_k, k_a, r_k, ln_w, ln_b):
    f32 = jnp.float32
    bsz, seq, _ = u.shape
    H, N, W = RWKV_HEADS, RWKV_HEAD_DIM, RWKV_WIDTH
    uf = u.astype(f32)
    u_prev = jnp.pad(uf, ((0, 0), (1, 0), (0, 0)))[:, :-1]
    uf = uf + (u_prev - uf) * mu
    r, k, v, pw, pa, pg = jnp.split(
        uf, [W, 2 * W, 3 * W, 3 * W + RWKV_DECAY_RANK,
             3 * W + RWKV_DECAY_RANK + RWKV_AAA_RANK], axis=-1)
    w_log = -jax.nn.softplus(-(w0 + jnp.tanh(pw) @ w2)) - 0.5
    decay = jnp.exp(-jnp.exp(w_log))
    iclr = jax.nn.sigmoid(a0 + pa @ a2)
    gate = jax.nn.sigmoid(pg) @ g2
    heads = lambda t: t.reshape(bsz, seq, H, N)
    kk = heads(k * k_k)
    kk = kk / jnp.maximum(jnp.sqrt(jnp.sum(kk * kk, axis=-1, keepdims=True)), 1e-12)
    k = k * (1.0 + (iclr - 1.0) * k_a)
    r, k, v, decay, iclr = heads(r), heads(k), heads(v), heads(decay), heads(iclr)

    def step(state, inp):
        r_t, w_t, k_t, v_t, kk_t, a_t = inp
        sa = jnp.einsum('bhij,bhj->bhi', state, -kk_t)
        state = (state * w_t[:, :, None, :]
                 + sa[..., None] * (kk_t * a_t)[:, :, None, :]
                 + v_t[..., None] * k_t[:, :, None, :])
        return state, jnp.einsum('bhij,bhj->bhi', state, r_t)

    seq_first = lambda t: jnp.moveaxis(t, 1, 0)
    s0 = jnp.zeros((bsz, H, N, N), f32)
    _, y = lax.scan(step, s0, (seq_first(r), seq_first(decay), seq_first(k),
                               seq_first(v), seq_first(kk), seq_first(iclr)))
    y = jnp.moveaxis(y, 0, 1)
    mean = jnp.mean(y, axis=-1, keepdims=True)
    var = jnp.mean(jnp.square(y - mean), axis=-1, keepdims=True)
    y = ((y - mean) * lax.rsqrt(var + RWKV_LN_EPS)).reshape(bsz, seq, W) * ln_w + ln_b
    bonus = jnp.sum(r * k * r_k, axis=-1, keepdims=True) * v
    y = (y + bonus.reshape(bsz, seq, W)) * gate
    return y.astype(u.dtype)


def memory_cross_attention(h, m, wq, wk, wv, wo):
    bsz, seq, _ = h.shape
    q = (h @ wq).reshape(bsz, seq, XATTN_HEADS, XATTN_HEAD_DIM)
    k = (m @ wk).reshape(bsz, m.shape[1], XATTN_HEADS, XATTN_HEAD_DIM)
    v = (m @ wv).reshape(bsz, m.shape[1], XATTN_HEADS, XATTN_HEAD_DIM)
    scores = jnp.einsum('bshd,bmhd->bhsm', q, k).astype(jnp.float32) * (XATTN_HEAD_DIM ** -0.5)
    p = jax.nn.softmax(scores, axis=-1).astype(v.dtype)
    o = jnp.einsum('bhsm,bmhd->bshd', p, v).reshape(bsz, seq, D_MODEL)
    return o @ wo


def _fwd_setup_inputs(seed: int = 0) -> dict:
    key = jax.random.key(seed)
    ks = iter(jax.random.split(key, 40))
    nrm = lambda shape, scale: jax.random.normal(next(ks), shape, jnp.float32) * scale
    uni = lambda shape, lo, hi: jax.random.uniform(next(ks), shape, jnp.float32, lo, hi)
    L = DEPTH
    dt0 = jnp.exp(uni((L, SSD_HEADS), math.log(1e-3), math.log(1e-1)))
    return {
        "x": nrm((BATCH, SEQ, D_MODEL), 1.0),
        "mem": nrm((BATCH, N_MEM, D_MODEL), 1.0),
        "norm_mix_g": 1.0 + nrm((L, D_MODEL), 0.02),
        "w_in": nrm((L, D_MODEL, D_IN), D_MODEL ** -0.5),
        "ssd_conv_w": nrm((L, SSD_CONV, 1, SSD_CONV_DIM), SSD_CONV ** -0.5),
        "ssd_conv_b": nrm((L, SSD_CONV_DIM), 0.01),
        "ssd_dt_bias": dt0 + jnp.log(-jnp.expm1(-dt0)),
        "ssd_a_log": jnp.log(uni((L, SSD_HEADS), 1.0, 16.0)),
        "ssd_d": 1.0 + nrm((L, SSD_HEADS), 0.1),
        "ssd_norm_g": 1.0 + nrm((L, SSD_WIDTH), 0.02),
        "rwkv_mu": uni((L, RWKV_IN), 0.0, 1.0),
        "rwkv_w0": uni((L, RWKV_WIDTH), -6.0, -1.0),
        "rwkv_w2": nrm((L, RWKV_DECAY_RANK, RWKV_WIDTH), 0.5 * RWKV_DECAY_RANK ** -0.5),
        "rwkv_a0": nrm((L, RWKV_WIDTH), 0.1),
        "rwkv_a2": nrm((L, RWKV_AAA_RANK, RWKV_WIDTH), RWKV_AAA_RANK ** -0.5),
        "rwkv_g2": nrm((L, RWKV_GATE_RANK, RWKV_WIDTH), RWKV_GATE_RANK ** -0.5),
        "rwkv_k_k": 0.85 + nrm((L, RWKV_WIDTH), 0.05),
        "rwkv_k_a": 1.0 + nrm((L, RWKV_WIDTH), 0.05),
        "rwkv_r_k": nrm((L, RWKV_HEADS, RWKV_HEAD_DIM), 0.1),
        "rwkv_ln_w": 1.0 + nrm((L, RWKV_WIDTH), 0.02),
        "rwkv_ln_b": nrm((L, RWKV_WIDTH), 0.01),
        "w_out": nrm((L, MIX_WIDTH, D_MODEL), MIX_WIDTH ** -0.5),
        "norm_x_g": 1.0 + nrm((L, D_MODEL), 0.02),
        "norm_mem_g": 1.0 + nrm((L, D_MODEL), 0.02),
        "xattn_wq": nrm((L, D_MODEL, D_MODEL), D_MODEL ** -0.5),
        "xattn_wk": nrm((L, D_MODEL, D_MODEL), D_MODEL ** -0.5),
        "xattn_wv": nrm((L, D_MODEL, D_MODEL), D_MODEL ** -0.5),
        "xattn_wo": nrm((L, D_MODEL, D_MODEL), D_MODEL ** -0.5),
        "norm_ffn_g": 1.0 + nrm((L, D_MODEL), 0.02),
        "ffn_w1": nrm((L, D_MODEL, D_FF), D_MODEL ** -0.5),
        "ffn_w2": nrm((L, D_FF, D_MODEL), D_FF ** -0.5),
        "final_norm_g": 1.0 + nrm((D_MODEL,), 0.02),
    }


def _fwd_reference(x, mem, norm_mix_g, w_in, ssd_conv_w, ssd_conv_b, ssd_dt_bias, ssd_a_log,
              ssd_d, ssd_norm_g, rwkv_mu, rwkv_w0, rwkv_w2, rwkv_a0, rwkv_a2, rwkv_g2,
              rwkv_k_k, rwkv_k_a, rwkv_r_k, rwkv_ln_w, rwkv_ln_b, w_out, norm_x_g,
              norm_mem_g, xattn_wq, xattn_wk, xattn_wv, xattn_wo, norm_ffn_g, ffn_w1,
              ffn_w2, final_norm_g):
    for l in range(DEPTH):
        h = rms_norm(x, norm_mix_g[l])
        u = h @ w_in[l]
        y_ssd = ssd_mixer(u[..., :SSD_IN], ssd_conv_w[l], ssd_conv_b[l], ssd_dt_bias[l],
                          ssd_a_log[l], ssd_d[l], ssd_norm_g[l])
        y_rwkv = rwkv7_mixer(u[..., SSD_IN:], rwkv_mu[l], rwkv_w0[l], rwkv_w2[l], rwkv_a0[l],
                             rwkv_a2[l], rwkv_g2[l], rwkv_k_k[l], rwkv_k_a[l], rwkv_r_k[l],
                             rwkv_ln_w[l], rwkv_ln_b[l])
        x = x + jnp.concatenate([y_ssd, y_rwkv], axis=-1) @ w_out[l]
        h = rms_norm(x, norm_x_g[l])
        m = rms_norm(mem, norm_mem_g[l])
        x = x + memory_cross_attention(h, m, xattn_wq[l], xattn_wk[l], xattn_wv[l], xattn_wo[l])
        h = rms_norm(x, norm_ffn_g[l])
        x = x + jnp.square(jax.nn.relu(h @ ffn_w1[l])) @ ffn_w2[l]
    return rms_norm(x, final_norm_g)


import jax as _jax
import jax.numpy as _jnp

TWIN_FORMAT = 'train_step'
FWD_PARAMS = ['x', 'mem', 'norm_mix_g', 'w_in', 'ssd_conv_w', 'ssd_conv_b', 'ssd_dt_bias', 'ssd_a_log', 'ssd_d', 'ssd_norm_g', 'rwkv_mu', 'rwkv_w0', 'rwkv_w2', 'rwkv_a0', 'rwkv_a2', 'rwkv_g2', 'rwkv_k_k', 'rwkv_k_a', 'rwkv_r_k', 'rwkv_ln_w', 'rwkv_ln_b', 'w_out', 'norm_x_g', 'norm_mem_g', 'xattn_wq', 'xattn_wk', 'xattn_wv', 'xattn_wo', 'norm_ffn_g', 'ffn_w1', 'ffn_w2', 'final_norm_g']
TWIN_WEIGHTS = ['norm_mix_g', 'w_in', 'ssd_conv_w', 'ssd_conv_b', 'ssd_dt_bias', 'ssd_a_log', 'ssd_d', 'ssd_norm_g', 'rwkv_mu', 'rwkv_w0', 'rwkv_w2', 'rwkv_a0', 'rwkv_a2', 'rwkv_g2', 'rwkv_k_k', 'rwkv_k_a', 'rwkv_r_k', 'rwkv_ln_w', 'rwkv_ln_b', 'w_out', 'norm_x_g', 'norm_mem_g', 'xattn_wq', 'xattn_wk', 'xattn_wv', 'xattn_wo', 'norm_ffn_g', 'ffn_w1', 'ffn_w2', 'final_norm_g']
TWIN_DIFF_INPUT = 'x'
TWIN_INPUTS = ['x', 'mem', 'norm_mix_g', 'w_in', 'ssd_conv_w', 'ssd_conv_b', 'ssd_dt_bias', 'ssd_a_log', 'ssd_d', 'ssd_norm_g', 'rwkv_mu', 'rwkv_w0', 'rwkv_w2', 'rwkv_a0', 'rwkv_a2', 'rwkv_g2', 'rwkv_k_k', 'rwkv_k_a', 'rwkv_r_k', 'rwkv_ln_w', 'rwkv_ln_b', 'w_out', 'norm_x_g', 'norm_mem_g', 'xattn_wq', 'xattn_wk', 'xattn_wv', 'xattn_wo', 'norm_ffn_g', 'ffn_w1', 'ffn_w2', 'final_norm_g', 'loss_target', 'm_norm_mix_g', 'm_w_in', 'm_ssd_conv_w', 'm_ssd_conv_b', 'm_ssd_dt_bias', 'm_ssd_a_log', 'm_ssd_d', 'm_ssd_norm_g', 'm_rwkv_mu', 'm_rwkv_w0', 'm_rwkv_w2', 'm_rwkv_a0', 'm_rwkv_a2', 'm_rwkv_g2', 'm_rwkv_k_k', 'm_rwkv_k_a', 'm_rwkv_r_k', 'm_rwkv_ln_w', 'm_rwkv_ln_b', 'm_w_out', 'm_norm_x_g', 'm_norm_mem_g', 'm_xattn_wq', 'm_xattn_wk', 'm_xattn_wv', 'm_xattn_wo', 'm_norm_ffn_g', 'm_ffn_w1', 'm_ffn_w2', 'm_final_norm_g', 'v_norm_mix_g', 'v_w_in', 'v_ssd_conv_w', 'v_ssd_conv_b', 'v_ssd_dt_bias', 'v_ssd_a_log', 'v_ssd_d', 'v_ssd_norm_g', 'v_rwkv_mu', 'v_rwkv_w0', 'v_rwkv_w2', 'v_rwkv_a0', 'v_rwkv_a2', 'v_rwkv_g2', 'v_rwkv_k_k', 'v_rwkv_k_a', 'v_rwkv_r_k', 'v_rwkv_ln_w', 'v_rwkv_ln_b', 'v_w_out', 'v_norm_x_g', 'v_norm_mem_g', 'v_xattn_wq', 'v_xattn_wk', 'v_xattn_wv', 'v_xattn_wo', 'v_norm_ffn_g', 'v_ffn_w1', 'v_ffn_w2', 'v_final_norm_g']
TWIN_OUTPUTS = ['loss', 'grad_x', 'grad_norm_mix_g', 'grad_w_in', 'grad_ssd_conv_w', 'grad_ssd_conv_b', 'grad_ssd_dt_bias', 'grad_ssd_a_log', 'grad_ssd_d', 'grad_ssd_norm_g', 'grad_rwkv_mu', 'grad_rwkv_w0', 'grad_rwkv_w2', 'grad_rwkv_a0', 'grad_rwkv_a2', 'grad_rwkv_g2', 'grad_rwkv_k_k', 'grad_rwkv_k_a', 'grad_rwkv_r_k', 'grad_rwkv_ln_w', 'grad_rwkv_ln_b', 'grad_w_out', 'grad_norm_x_g', 'grad_norm_mem_g', 'grad_xattn_wq', 'grad_xattn_wk', 'grad_xattn_wv', 'grad_xattn_wo', 'grad_norm_ffn_g', 'grad_ffn_w1', 'grad_ffn_w2', 'grad_final_norm_g', 'delta_norm_mix_g', 'delta_w_in', 'delta_ssd_conv_w', 'delta_ssd_conv_b', 'delta_ssd_dt_bias', 'delta_ssd_a_log', 'delta_ssd_d', 'delta_ssd_norm_g', 'delta_rwkv_mu', 'delta_rwkv_w0', 'delta_rwkv_w2', 'delta_rwkv_a0', 'delta_rwkv_a2', 'delta_rwkv_g2', 'delta_rwkv_k_k', 'delta_rwkv_k_a', 'delta_rwkv_r_k', 'delta_rwkv_ln_w', 'delta_rwkv_ln_b', 'delta_w_out', 'delta_norm_x_g', 'delta_norm_mem_g', 'delta_xattn_wq', 'delta_xattn_wk', 'delta_xattn_wv', 'delta_xattn_wo', 'delta_norm_ffn_g', 'delta_ffn_w1', 'delta_ffn_w2', 'delta_final_norm_g', 'new_m_norm_mix_g', 'new_m_w_in', 'new_m_ssd_conv_w', 'new_m_ssd_conv_b', 'new_m_ssd_dt_bias', 'new_m_ssd_a_log', 'new_m_ssd_d', 'new_m_ssd_norm_g', 'new_m_rwkv_mu', 'new_m_rwkv_w0', 'new_m_rwkv_w2', 'new_m_rwkv_a0', 'new_m_rwkv_a2', 'new_m_rwkv_g2', 'new_m_rwkv_k_k', 'new_m_rwkv_k_a', 'new_m_rwkv_r_k', 'new_m_rwkv_ln_w', 'new_m_rwkv_ln_b', 'new_m_w_out', 'new_m_norm_x_g', 'new_m_norm_mem_g', 'new_m_xattn_wq', 'new_m_xattn_wk', 'new_m_xattn_wv', 'new_m_xattn_wo', 'new_m_norm_ffn_g', 'new_m_ffn_w1', 'new_m_ffn_w2', 'new_m_final_norm_g', 'new_v_norm_mix_g', 'new_v_w_in', 'new_v_ssd_conv_w', 'new_v_ssd_conv_b', 'new_v_ssd_dt_bias', 'new_v_ssd_a_log', 'new_v_ssd_d', 'new_v_ssd_norm_g', 'new_v_rwkv_mu', 'new_v_rwkv_w0', 'new_v_rwkv_w2', 'new_v_rwkv_a0', 'new_v_rwkv_a2', 'new_v_rwkv_g2', 'new_v_rwkv_k_k', 'new_v_rwkv_k_a', 'new_v_rwkv_r_k', 'new_v_rwkv_ln_w', 'new_v_rwkv_ln_b', 'new_v_w_out', 'new_v_norm_x_g', 'new_v_norm_mem_g', 'new_v_xattn_wq', 'new_v_xattn_wk', 'new_v_xattn_wv', 'new_v_xattn_wo', 'new_v_norm_ffn_g', 'new_v_ffn_w1', 'new_v_ffn_w2', 'new_v_final_norm_g']
TWIN_LEAF_KINDS = {'loss': 'loss', 'grad_x': 'grad_x', 'grad_norm_mix_g': 'grad_w', 'grad_w_in': 'grad_w', 'grad_ssd_conv_w': 'grad_w', 'grad_ssd_conv_b': 'grad_w', 'grad_ssd_dt_bias': 'grad_w', 'grad_ssd_a_log': 'grad_w', 'grad_ssd_d': 'grad_w', 'grad_ssd_norm_g': 'grad_w', 'grad_rwkv_mu': 'grad_w', 'grad_rwkv_w0': 'grad_w', 'grad_rwkv_w2': 'grad_w', 'grad_rwkv_a0': 'grad_w', 'grad_rwkv_a2': 'grad_w', 'grad_rwkv_g2': 'grad_w', 'grad_rwkv_k_k': 'grad_w', 'grad_rwkv_k_a': 'grad_w', 'grad_rwkv_r_k': 'grad_w', 'grad_rwkv_ln_w': 'grad_w', 'grad_rwkv_ln_b': 'grad_w', 'grad_w_out': 'grad_w', 'grad_norm_x_g': 'grad_w', 'grad_norm_mem_g': 'grad_w', 'grad_xattn_wq': 'grad_w', 'grad_xattn_wk': 'grad_w', 'grad_xattn_wv': 'grad_w', 'grad_xattn_wo': 'grad_w', 'grad_norm_ffn_g': 'grad_w', 'grad_ffn_w1': 'grad_w', 'grad_ffn_w2': 'grad_w', 'grad_final_norm_g': 'grad_w', 'delta_norm_mix_g': 'delta_w', 'delta_w_in': 'delta_w', 'delta_ssd_conv_w': 'delta_w', 'delta_ssd_conv_b': 'delta_w', 'delta_ssd_dt_bias': 'delta_w', 'delta_ssd_a_log': 'delta_w', 'delta_ssd_d': 'delta_w', 'delta_ssd_norm_g': 'delta_w', 'delta_rwkv_mu': 'delta_w', 'delta_rwkv_w0': 'delta_w', 'delta_rwkv_w2': 'delta_w', 'delta_rwkv_a0': 'delta_w', 'delta_rwkv_a2': 'delta_w', 'delta_rwkv_g2': 'delta_w', 'delta_rwkv_k_k': 'delta_w', 'delta_rwkv_k_a': 'delta_w', 'delta_rwkv_r_k': 'delta_w', 'delta_rwkv_ln_w': 'delta_w', 'delta_rwkv_ln_b': 'delta_w', 'delta_w_out': 'delta_w', 'delta_norm_x_g': 'delta_w', 'delta_norm_mem_g': 'delta_w', 'delta_xattn_wq': 'delta_w', 'delta_xattn_wk': 'delta_w', 'delta_xattn_wv': 'delta_w', 'delta_xattn_wo': 'delta_w', 'delta_norm_ffn_g': 'delta_w', 'delta_ffn_w1': 'delta_w', 'delta_ffn_w2': 'delta_w', 'delta_final_norm_g': 'delta_w', 'new_m_norm_mix_g': 'new_m', 'new_m_w_in': 'new_m', 'new_m_ssd_conv_w': 'new_m', 'new_m_ssd_conv_b': 'new_m', 'new_m_ssd_dt_bias': 'new_m', 'new_m_ssd_a_log': 'new_m', 'new_m_ssd_d': 'new_m', 'new_m_ssd_norm_g': 'new_m', 'new_m_rwkv_mu': 'new_m', 'new_m_rwkv_w0': 'new_m', 'new_m_rwkv_w2': 'new_m', 'new_m_rwkv_a0': 'new_m', 'new_m_rwkv_a2': 'new_m', 'new_m_rwkv_g2': 'new_m', 'new_m_rwkv_k_k': 'new_m', 'new_m_rwkv_k_a': 'new_m', 'new_m_rwkv_r_k': 'new_m', 'new_m_rwkv_ln_w': 'new_m', 'new_m_rwkv_ln_b': 'new_m', 'new_m_w_out': 'new_m', 'new_m_norm_x_g': 'new_m', 'new_m_norm_mem_g': 'new_m', 'new_m_xattn_wq': 'new_m', 'new_m_xattn_wk': 'new_m', 'new_m_xattn_wv': 'new_m', 'new_m_xattn_wo': 'new_m', 'new_m_norm_ffn_g': 'new_m', 'new_m_ffn_w1': 'new_m', 'new_m_ffn_w2': 'new_m', 'new_m_final_norm_g': 'new_m', 'new_v_norm_mix_g': 'new_v', 'new_v_w_in': 'new_v', 'new_v_ssd_conv_w': 'new_v', 'new_v_ssd_conv_b': 'new_v', 'new_v_ssd_dt_bias': 'new_v', 'new_v_ssd_a_log': 'new_v', 'new_v_ssd_d': 'new_v', 'new_v_ssd_norm_g': 'new_v', 'new_v_rwkv_mu': 'new_v', 'new_v_rwkv_w0': 'new_v', 'new_v_rwkv_w2': 'new_v', 'new_v_rwkv_a0': 'new_v', 'new_v_rwkv_a2': 'new_v', 'new_v_rwkv_g2': 'new_v', 'new_v_rwkv_k_k': 'new_v', 'new_v_rwkv_k_a': 'new_v', 'new_v_rwkv_r_k': 'new_v', 'new_v_rwkv_ln_w': 'new_v', 'new_v_rwkv_ln_b': 'new_v', 'new_v_w_out': 'new_v', 'new_v_norm_x_g': 'new_v', 'new_v_norm_mem_g': 'new_v', 'new_v_xattn_wq': 'new_v', 'new_v_xattn_wk': 'new_v', 'new_v_xattn_wv': 'new_v', 'new_v_xattn_wo': 'new_v', 'new_v_norm_ffn_g': 'new_v', 'new_v_ffn_w1': 'new_v', 'new_v_ffn_w2': 'new_v', 'new_v_final_norm_g': 'new_v'}


def _forward(args):
    return _fwd_reference(*[args[k] for k in FWD_PARAMS])


def _output_shape():
    def fwd():
        inp = _fwd_setup_inputs(0)
        return _fwd_reference(*[inp[k] for k in FWD_PARAMS])
    out = _jax.eval_shape(fwd)
    return out.shape, out.dtype

N_MICROBATCH = 1
ADAM_LR = 0.001
ADAM_B1 = 0.9
ADAM_B2 = 0.999
ADAM_EPS = 1e-08
ADAM_WD = 0.01
ADAM_STEP = 10
PER_EXAMPLE_BATCH_AXIS = {'x': 0, 'mem': 0, 'loss_target': 0}
SHARED_INPUTS = []
_WEIGHT_DTYPES = {'norm_mix_g': _jnp.float32, 'w_in': _jnp.float32, 'ssd_conv_w': _jnp.float32, 'ssd_conv_b': _jnp.float32, 'ssd_dt_bias': _jnp.float32, 'ssd_a_log': _jnp.float32, 'ssd_d': _jnp.float32, 'ssd_norm_g': _jnp.float32, 'rwkv_mu': _jnp.float32, 'rwkv_w0': _jnp.float32, 'rwkv_w2': _jnp.float32, 'rwkv_a0': _jnp.float32, 'rwkv_a2': _jnp.float32, 'rwkv_g2': _jnp.float32, 'rwkv_k_k': _jnp.float32, 'rwkv_k_a': _jnp.float32, 'rwkv_r_k': _jnp.float32, 'rwkv_ln_w': _jnp.float32, 'rwkv_ln_b': _jnp.float32, 'w_out': _jnp.float32, 'norm_x_g': _jnp.float32, 'norm_mem_g': _jnp.float32, 'xattn_wq': _jnp.float32, 'xattn_wk': _jnp.float32, 'xattn_wv': _jnp.float32, 'xattn_wo': _jnp.float32, 'norm_ffn_g': _jnp.float32, 'ffn_w1': _jnp.float32, 'ffn_w2': _jnp.float32, 'final_norm_g': _jnp.float32}
MOMENT_SCALE = {'norm_mix_g': 9.637612e-02, 'w_in': 5.694770e-02, 'ssd_conv_w': 6.492432e-02, 'ssd_conv_b': 9.203472e-02, 'ssd_dt_bias': 1.948267e-01, 'ssd_a_log': 2.045979e-01, 'ssd_d': 5.376066e-01, 'ssd_norm_g': 7.825994e-02, 'rwkv_mu': 7.105764e-02, 'rwkv_w0': 1.685688e-02, 'rwkv_w2': 1.985362e-03, 'rwkv_a0': 1.768519e-02, 'rwkv_a2': 1.538601e-02, 'rwkv_g2': 3.980057e-02, 'rwkv_k_k': 4.241477e-02, 'rwkv_k_a': 4.970380e-02, 'rwkv_r_k': 9.390583e-02, 'rwkv_ln_w': 4.339390e-02, 'rwkv_ln_b': 6.315919e-02, 'w_out': 6.097313e-02, 'norm_x_g': 7.922768e-03, 'norm_mem_g': 1.142359e-02, 'xattn_wq': 7.566994e-03, 'xattn_wk': 7.566733e-03, 'xattn_wv': 7.873047e-03, 'xattn_wo': 7.842228e-03, 'norm_ffn_g': 7.016085e-02, 'ffn_w1': 3.532124e-02, 'ffn_w2': 7.056102e-02, 'final_norm_g': 1.611898e+01}


def _to_microbatches(a, axis):
    t = _jnp.moveaxis(a, axis, 0)
    t = t.reshape((N_MICROBATCH, t.shape[0] // N_MICROBATCH) + t.shape[1:])
    return _jnp.moveaxis(t, 1, axis + 1)


def setup_inputs(seed: int = 0) -> dict:
    inp = _fwd_setup_inputs(seed)
    key = _jax.random.fold_in(_jax.random.key(seed), 7919)
    shape, _ = _output_shape()
    out = dict(inp)
    out["loss_target"] = _jax.random.normal(_jax.random.fold_in(key, 0), shape, _jnp.float32)
    for i, name in enumerate(TWIN_WEIGHTS):
        w = inp[name].astype(_jnp.float32)
        if MOMENT_SCALE is None:
            s = _jnp.sqrt(_jnp.mean(_jnp.square(w)) + 1e-30)
        else:
            s = MOMENT_SCALE[name]
        km, kv = _jax.random.split(_jax.random.fold_in(key, i + 1))
        out[name] = w
        out["m_" + name] = s * _jax.random.normal(km, w.shape, _jnp.float32)
        out["v_" + name] = (s * s) * _jax.random.uniform(kv, w.shape, _jnp.float32, 0.5, 1.5)
    if N_MICROBATCH > 1:
        for name, axis in PER_EXAMPLE_BATCH_AXIS.items():
            out[name] = _to_microbatches(out[name], axis)
    return {'x': out['x'], 'mem': out['mem'], 'norm_mix_g': out['norm_mix_g'], 'w_in': out['w_in'], 'ssd_conv_w': out['ssd_conv_w'], 'ssd_conv_b': out['ssd_conv_b'], 'ssd_dt_bias': out['ssd_dt_bias'], 'ssd_a_log': out['ssd_a_log'], 'ssd_d': out['ssd_d'], 'ssd_norm_g': out['ssd_norm_g'], 'rwkv_mu': out['rwkv_mu'], 'rwkv_w0': out['rwkv_w0'], 'rwkv_w2': out['rwkv_w2'], 'rwkv_a0': out['rwkv_a0'], 'rwkv_a2': out['rwkv_a2'], 'rwkv_g2': out['rwkv_g2'], 'rwkv_k_k': out['rwkv_k_k'], 'rwkv_k_a': out['rwkv_k_a'], 'rwkv_r_k': out['rwkv_r_k'], 'rwkv_ln_w': out['rwkv_ln_w'], 'rwkv_ln_b': out['rwkv_ln_b'], 'w_out': out['w_out'], 'norm_x_g': out['norm_x_g'], 'norm_mem_g': out['norm_mem_g'], 'xattn_wq': out['xattn_wq'], 'xattn_wk': out['xattn_wk'], 'xattn_wv': out['xattn_wv'], 'xattn_wo': out['xattn_wo'], 'norm_ffn_g': out['norm_ffn_g'], 'ffn_w1': out['ffn_w1'], 'ffn_w2': out['ffn_w2'], 'final_norm_g': out['final_norm_g'], 'loss_target': out['loss_target'], 'm_norm_mix_g': out['m_norm_mix_g'], 'm_w_in': out['m_w_in'], 'm_ssd_conv_w': out['m_ssd_conv_w'], 'm_ssd_conv_b': out['m_ssd_conv_b'], 'm_ssd_dt_bias': out['m_ssd_dt_bias'], 'm_ssd_a_log': out['m_ssd_a_log'], 'm_ssd_d': out['m_ssd_d'], 'm_ssd_norm_g': out['m_ssd_norm_g'], 'm_rwkv_mu': out['m_rwkv_mu'], 'm_rwkv_w0': out['m_rwkv_w0'], 'm_rwkv_w2': out['m_rwkv_w2'], 'm_rwkv_a0': out['m_rwkv_a0'], 'm_rwkv_a2': out['m_rwkv_a2'], 'm_rwkv_g2': out['m_rwkv_g2'], 'm_rwkv_k_k': out['m_rwkv_k_k'], 'm_rwkv_k_a': out['m_rwkv_k_a'], 'm_rwkv_r_k': out['m_rwkv_r_k'], 'm_rwkv_ln_w': out['m_rwkv_ln_w'], 'm_rwkv_ln_b': out['m_rwkv_ln_b'], 'm_w_out': out['m_w_out'], 'm_norm_x_g': out['m_norm_x_g'], 'm_norm_mem_g': out['m_norm_mem_g'], 'm_xattn_wq': out['m_xattn_wq'], 'm_xattn_wk': out['m_xattn_wk'], 'm_xattn_wv': out['m_xattn_wv'], 'm_xattn_wo': out['m_xattn_wo'], 'm_norm_ffn_g': out['m_norm_ffn_g'], 'm_ffn_w1': out['m_ffn_w1'], 'm_ffn_w2': out['m_ffn_w2'], 'm_final_norm_g': out['m_final_norm_g'], 'v_norm_mix_g': out['v_norm_mix_g'], 'v_w_in': out['v_w_in'], 'v_ssd_conv_w': out['v_ssd_conv_w'], 'v_ssd_conv_b': out['v_ssd_conv_b'], 'v_ssd_dt_bias': out['v_ssd_dt_bias'], 'v_ssd_a_log': out['v_ssd_a_log'], 'v_ssd_d': out['v_ssd_d'], 'v_ssd_norm_g': out['v_ssd_norm_g'], 'v_rwkv_mu': out['v_rwkv_mu'], 'v_rwkv_w0': out['v_rwkv_w0'], 'v_rwkv_w2': out['v_rwkv_w2'], 'v_rwkv_a0': out['v_rwkv_a0'], 'v_rwkv_a2': out['v_rwkv_a2'], 'v_rwkv_g2': out['v_rwkv_g2'], 'v_rwkv_k_k': out['v_rwkv_k_k'], 'v_rwkv_k_a': out['v_rwkv_k_a'], 'v_rwkv_r_k': out['v_rwkv_r_k'], 'v_rwkv_ln_w': out['v_rwkv_ln_w'], 'v_rwkv_ln_b': out['v_rwkv_ln_b'], 'v_w_out': out['v_w_out'], 'v_norm_x_g': out['v_norm_x_g'], 'v_norm_mem_g': out['v_norm_mem_g'], 'v_xattn_wq': out['v_xattn_wq'], 'v_xattn_wk': out['v_xattn_wk'], 'v_xattn_wv': out['v_xattn_wv'], 'v_xattn_wo': out['v_xattn_wo'], 'v_norm_ffn_g': out['v_norm_ffn_g'], 'v_ffn_w1': out['v_ffn_w1'], 'v_ffn_w2': out['v_ffn_w2'], 'v_final_norm_g': out['v_final_norm_g']}


def _loss(weights, diff, rest, loss_target):
    with _jax.named_scope("forward"):
        args = {**rest, TWIN_DIFF_INPUT: diff, **{k: w.astype(_WEIGHT_DTYPES[k]) for k, w in weights.items()}}
        y = _forward(args)
    with _jax.named_scope("loss_head"):
        err = _jnp.square(y.astype(_jnp.float32) - loss_target)
        return 0.5 * _jnp.sum(_jnp.mean(err, axis=-1)) if err.ndim else 0.5 * err


def _adamw(w, g, m, v):
    m = ADAM_B1 * m + (1.0 - ADAM_B1) * g
    v = ADAM_B2 * v + (1.0 - ADAM_B2) * _jnp.square(g)
    m_hat = m / (1.0 - ADAM_B1 ** ADAM_STEP)
    v_hat = v / (1.0 - ADAM_B2 ** ADAM_STEP)
    delta = -ADAM_LR * (m_hat / (_jnp.sqrt(v_hat) + ADAM_EPS) + ADAM_WD * w)
    return delta, m, v


def reference(x, mem, norm_mix_g, w_in, ssd_conv_w, ssd_conv_b, ssd_dt_bias, ssd_a_log, ssd_d, ssd_norm_g, rwkv_mu, rwkv_w0, rwkv_w2, rwkv_a0, rwkv_a2, rwkv_g2, rwkv_k_k, rwkv_k_a, rwkv_r_k, rwkv_ln_w, rwkv_ln_b, w_out, norm_x_g, norm_mem_g, xattn_wq, xattn_wk, xattn_wv, xattn_wo, norm_ffn_g, ffn_w1, ffn_w2, final_norm_g, loss_target, m_norm_mix_g, m_w_in, m_ssd_conv_w, m_ssd_conv_b, m_ssd_dt_bias, m_ssd_a_log, m_ssd_d, m_ssd_norm_g, m_rwkv_mu, m_rwkv_w0, m_rwkv_w2, m_rwkv_a0, m_rwkv_a2, m_rwkv_g2, m_rwkv_k_k, m_rwkv_k_a, m_rwkv_r_k, m_rwkv_ln_w, m_rwkv_ln_b, m_w_out, m_norm_x_g, m_norm_mem_g, m_xattn_wq, m_xattn_wk, m_xattn_wv, m_xattn_wo, m_norm_ffn_g, m_ffn_w1, m_ffn_w2, m_final_norm_g, v_norm_mix_g, v_w_in, v_ssd_conv_w, v_ssd_conv_b, v_ssd_dt_bias, v_ssd_a_log, v_ssd_d, v_ssd_norm_g, v_rwkv_mu, v_rwkv_w0, v_rwkv_w2, v_rwkv_a0, v_rwkv_a2, v_rwkv_g2, v_rwkv_k_k, v_rwkv_k_a, v_rwkv_r_k, v_rwkv_ln_w, v_rwkv_ln_b, v_w_out, v_norm_x_g, v_norm_mem_g, v_xattn_wq, v_xattn_wk, v_xattn_wv, v_xattn_wo, v_norm_ffn_g, v_ffn_w1, v_ffn_w2, v_final_norm_g):
    given = dict(x=x, mem=mem, norm_mix_g=norm_mix_g, w_in=w_in, ssd_conv_w=ssd_conv_w, ssd_conv_b=ssd_conv_b, ssd_dt_bias=ssd_dt_bias, ssd_a_log=ssd_a_log, ssd_d=ssd_d, ssd_norm_g=ssd_norm_g, rwkv_mu=rwkv_mu, rwkv_w0=rwkv_w0, rwkv_w2=rwkv_w2, rwkv_a0=rwkv_a0, rwkv_a2=rwkv_a2, rwkv_g2=rwkv_g2, rwkv_k_k=rwkv_k_k, rwkv_k_a=rwkv_k_a, rwkv_r_k=rwkv_r_k, rwkv_ln_w=rwkv_ln_w, rwkv_ln_b=rwkv_ln_b, w_out=w_out, norm_x_g=norm_x_g, norm_mem_g=norm_mem_g, xattn_wq=xattn_wq, xattn_wk=xattn_wk, xattn_wv=xattn_wv, xattn_wo=xattn_wo, norm_ffn_g=norm_ffn_g, ffn_w1=ffn_w1, ffn_w2=ffn_w2, final_norm_g=final_norm_g, loss_target=loss_target, m_norm_mix_g=m_norm_mix_g, m_w_in=m_w_in, m_ssd_conv_w=m_ssd_conv_w, m_ssd_conv_b=m_ssd_conv_b, m_ssd_dt_bias=m_ssd_dt_bias, m_ssd_a_log=m_ssd_a_log, m_ssd_d=m_ssd_d, m_ssd_norm_g=m_ssd_norm_g, m_rwkv_mu=m_rwkv_mu, m_rwkv_w0=m_rwkv_w0, m_rwkv_w2=m_rwkv_w2, m_rwkv_a0=m_rwkv_a0, m_rwkv_a2=m_rwkv_a2, m_rwkv_g2=m_rwkv_g2, m_rwkv_k_k=m_rwkv_k_k, m_rwkv_k_a=m_rwkv_k_a, m_rwkv_r_k=m_rwkv_r_k, m_rwkv_ln_w=m_rwkv_ln_w, m_rwkv_ln_b=m_rwkv_ln_b, m_w_out=m_w_out, m_norm_x_g=m_norm_x_g, m_norm_mem_g=m_norm_mem_g, m_xattn_wq=m_xattn_wq, m_xattn_wk=m_xattn_wk, m_xattn_wv=m_xattn_wv, m_xattn_wo=m_xattn_wo, m_norm_ffn_g=m_norm_ffn_g, m_ffn_w1=m_ffn_w1, m_ffn_w2=m_ffn_w2, m_final_norm_g=m_final_norm_g, v_norm_mix_g=v_norm_mix_g, v_w_in=v_w_in, v_ssd_conv_w=v_ssd_conv_w, v_ssd_conv_b=v_ssd_conv_b, v_ssd_dt_bias=v_ssd_dt_bias, v_ssd_a_log=v_ssd_a_log, v_ssd_d=v_ssd_d, v_ssd_norm_g=v_ssd_norm_g, v_rwkv_mu=v_rwkv_mu, v_rwkv_w0=v_rwkv_w0, v_rwkv_w2=v_rwkv_w2, v_rwkv_a0=v_rwkv_a0, v_rwkv_a2=v_rwkv_a2, v_rwkv_g2=v_rwkv_g2, v_rwkv_k_k=v_rwkv_k_k, v_rwkv_k_a=v_rwkv_k_a, v_rwkv_r_k=v_rwkv_r_k, v_rwkv_ln_w=v_rwkv_ln_w, v_rwkv_ln_b=v_rwkv_ln_b, v_w_out=v_w_out, v_norm_x_g=v_norm_x_g, v_norm_mem_g=v_norm_mem_g, v_xattn_wq=v_xattn_wq, v_xattn_wk=v_xattn_wk, v_xattn_wv=v_xattn_wv, v_xattn_wo=v_xattn_wo, v_norm_ffn_g=v_norm_ffn_g, v_ffn_w1=v_ffn_w1, v_ffn_w2=v_ffn_w2, v_final_norm_g=v_final_norm_g)
    weights = {n: given[n] for n in TWIN_WEIGHTS}
    shared = {n: given[n] for n in SHARED_INPUTS}
    per_example = {n: given[n] for n in ['x', 'mem']}
    grad_fn = _jax.value_and_grad(_loss, argnums=(0, 1))

    def one_microbatch(ex, loss_target):
        ex = dict(ex)
        diff = ex.pop(TWIN_DIFF_INPUT)
        return grad_fn(weights, diff, {**shared, **ex}, loss_target)

    if N_MICROBATCH == 1:
        loss, (grad_w, grad_x) = one_microbatch(per_example, given["loss_target"])
    else:
        def body(carry, xs):
            loss_sum, grad_sum = carry
            l_k, (gw_k, gx_k) = one_microbatch(xs[0], xs[1])
            with _jax.named_scope("update"):
                return (loss_sum + l_k, _jax.tree.map(_jnp.add, grad_sum, gw_k)), gx_k

        init = (_jnp.zeros((), _jnp.float32), _jax.tree.map(_jnp.zeros_like, weights))
        (loss, grad_w), grad_x = _jax.lax.scan(body, init, (per_example, given["loss_target"]))
    with _jax.named_scope("update"):
        delta_w, new_m, new_v = {}, {}, {}
        for n in TWIN_WEIGHTS:
            delta_w[n], new_m[n], new_v[n] = _adamw(weights[n], grad_w[n], given["m_" + n], given["v_" + n])
    return (loss, grad_x, *[grad_w[n] for n in TWIN_WEIGHTS], *[delta_w[n] for n in TWIN_WEIGHTS],
            *[new_m[n] for n in TWIN_WEIGHTS], *[new_v[n] for n in TWIN_WEIGHTS])
```

```python
import functools

import jax
import jax.numpy as jnp
from jax import lax
from jax.experimental import pallas as pl
from jax.experimental.pallas import tpu as pltpu

F32 = jnp.float32
BF16 = jnp.bfloat16
HIGHEST = lax.Precision.HIGHEST

N_DEV = 8
D_MODEL = 2048
NORM_EPS = 1e-6
SSD_WIDTH = 1024
SSD_CONV_DIM = 1536
SSD_HEADS = 16
SSD_HEAD_DIM = 64
SSD_STATE = 128
SSD_CHUNK = 128
SSD_HEADS_PER_GROUP = 8
RWKV_WIDTH = 1024
RWKV_HEADS = 16
RWKV_HEAD_DIM = 64
RWKV_LN_EPS = 64e-5
RWKV_CHUNK = 64
RWKV_HEADS_PER_STEP = 4
XATTN_HEADS = 4
XATTN_HEAD_DIM = 512
D_FF = 8192
LANES = 128
SUBLANES = 8
VMEM_LIMIT = 56 * 1024 * 1024

ADAM_LR = 0.001
ADAM_B1 = 0.9
ADAM_B2 = 0.999
ADAM_EPS = 1e-08
ADAM_WD = 0.01
ADAM_STEP = 10

_DN = {"nn": ((1,), (0,)), "nt": ((1,), (1,)), "tn": ((0,), (0,))}


def _dg(a, b, mode, precision=None):
    return lax.dot_general(a, b, (_DN[mode], ((), ())), precision=precision, preferred_element_type=F32)


@functools.partial(jax.custom_vjp, nondiff_argnums=(2,))
def bdot(a, b, mode):
    return _dg(a.astype(BF16), b.astype(BF16), mode)


def _bdot_fwd(a, b, mode):
    return bdot(a, b, mode), (a, b)


def _bdot_bwd(mode, res, g):
    a, b = res
    ab, bb, gb = a.astype(BF16), b.astype(BF16), g.astype(BF16)
    if mode == "nn":
        da, db = _dg(gb, bb, "nt"), _dg(ab, gb, "tn")
    elif mode == "nt":
        da, db = _dg(gb, bb, "nn"), _dg(gb, ab, "tn")
    else:
        da, db = _dg(bb, gb, "nt"), _dg(ab, gb, "nn")
    return da.astype(a.dtype), db.astype(b.dtype)


bdot.defvjp(_bdot_fwd, _bdot_bwd)


def fdot(a, b, mode):
    return _dg(a, b, mode, precision=HIGHEST)


def _split3(x):
    hi = x.astype(BF16)
    r1 = x - hi.astype(F32)
    mid = r1.astype(BF16)
    lo = (r1 - mid.astype(F32)).astype(BF16)
    return hi, mid, lo


def _dot01(x, m01):
    hi, mid, lo = _split3(x)
    return _dg(hi, m01, "nn") + _dg(mid, m01, "nn") + _dg(lo, m01, "nn")


def _head_indicator(width, heads, transpose):
    hd = width // heads
    shape = (LANES, width) if transpose else (width, LANES)
    lane = lax.broadcasted_iota(jnp.int32, shape, 1 if not transpose else 0)
    pos = lax.broadcasted_iota(jnp.int32, shape, 0 if not transpose else 1)
    return ((pos >= lane * hd) & (pos < lane * hd + hd)).astype(BF16)


@jax.custom_vjp
def head_sum(x):
    w = x.shape[-1]
    e = _head_indicator(w, w // RWKV_HEAD_DIM, False)
    et = _head_indicator(w, w // RWKV_HEAD_DIM, True)
    return _dot01(_dot01(x, e), et)


head_sum.defvjp(lambda x: (head_sum(x), None), lambda _, g: (head_sum(g),))


def rmsnorm_fn(x, g):
    y = x * lax.rsqrt(jnp.mean(x * x, axis=-1, keepdims=True) + NORM_EPS)
    return ((y * g).astype(BF16),)


def cast_fn(x):
    return (x.astype(BF16),)


def relu2_fn(a):
    return (jnp.square(jnp.maximum(a, 0.0)).astype(BF16),)


def ssd_pre_fn(xbc, xbc1, xbc2, xbc3, dt_raw, conv_w, conv_b, dt_bias):
    c = conv_w[3:4] * xbc + conv_w[2:3] * xbc1 + conv_w[1:2] * xbc2 + conv_w[0:1] * xbc3 + conv_b
    act = c * jax.nn.sigmoid(c)
    dt = jax.nn.softplus(dt_raw + dt_bias)
    return act[:, :SSD_WIDTH], act[:, SSD_WIDTH:SSD_WIDTH + 256], act[:, SSD_WIDTH + 256:], dt


def ssd_post_fn(yscan, z, norm_g):
    y = yscan * (z * jax.nn.sigmoid(z))
    half = SSD_WIDTH // 2
    parts = []
    for g in range(2):
        yg = y[:, g * half:(g + 1) * half]
        parts.append(yg * lax.rsqrt(jnp.mean(yg * yg, axis=-1, keepdims=True) + NORM_EPS))
    return ((jnp.concatenate(parts, axis=-1) * norm_g).astype(BF16),)


def rwkv_pre_fn(rkv, rkv_p, pg, pg_p, pwa, pwa_p, mu_rkv, mu_pg, mu_pwa, w0, w2p, a0, a2p, g2, k_k, k_a):
    w = RWKV_WIDTH
    rkv = rkv + (rkv_p - rkv) * mu_rkv
    pg = pg + (pg_p - pg) * mu_pg
    pwa = pwa + (pwa_p - pwa) * mu_pwa
    r, k, v = rkv[:, :w], rkv[:, w:2 * w], rkv[:, 2 * w:]
    pw, pa = pwa[:, :LANES], pwa[:, LANES:]
    w_log = -jax.nn.softplus(-(w0 + bdot(jnp.tanh(pw), w2p, "nn"))) - 0.5
    lw = -jnp.exp(w_log)
    iclr = jax.nn.sigmoid(a0 + bdot(pa, a2p, "nn"))
    gate = bdot(jax.nn.sigmoid(pg), g2, "nn")
    kk = k * k_k
    kap = kk / jnp.maximum(jnp.sqrt(head_sum(kk * kk)), 1e-12)
    k_mod = k * (1.0 + (iclr - 1.0) * k_a)
    return r, lw, k_mod, v, kap, kap * iclr, gate


def rwkv_post_fn(ys, r, k_mod, v, gate, ln_w, ln_b, r_k):
    inv_n = 1.0 / RWKV_HEAD_DIM
    mean = head_sum(ys) * inv_n
    yc = ys - mean
    var = head_sum(yc * yc) * inv_n
    yn = yc * lax.rsqrt(var + RWKV_LN_EPS) * ln_w + ln_b
    bonus = head_sum(r * k_mod * r_k) * v
    return (((yn + bonus) * gate).astype(BF16),)


def attn_fn(q, kx, vx):
    outs = []
    for h in range(XATTN_HEADS):
        sl = slice(h * XATTN_HEAD_DIM, (h + 1) * XATTN_HEAD_DIM)
        s = bdot(q[:, sl], kx[:, sl], "nt") * (XATTN_HEAD_DIM ** -0.5)
        s = s - jnp.max(s, axis=-1, keepdims=True)
        p = jnp.exp(s)
        p = p / jnp.sum(p, axis=-1, keepdims=True)
        outs.append(bdot(p, vx[:, sl], "nn"))
    return (jnp.concatenate(outs, axis=-1).astype(BF16),)


def loss_fn(x, tgt, g):
    y = x * lax.rsqrt(jnp.mean(x * x, axis=-1, keepdims=True) + NORM_EPS) * g
    err = jnp.square(y - tgt)
    return 0.5 * jnp.sum(jnp.mean(err, axis=-1, keepdims=True), axis=0, keepdims=True)


def _tri_masks(n):
    row = lax.broadcasted_iota(jnp.int32, (n, n), 0)
    col = lax.broadcasted_iota(jnp.int32, (n, n), 1)
    return col <= row, col < row, row == col


def rwkv_chunk_fn(st0, r, lw, k, v, kap, b):
    c = r.shape[0]
    incl, strict, diag = _tri_masks(c)
    cum = fdot(incl.astype(F32), lw, "nn")
    g_in = jnp.exp(cum)
    g_prev = jnp.exp(cum - lw)
    g_inv = jnp.exp(-cum)
    g_end = jnp.exp(cum[c - 1:c, :] - cum)
    kap_t, k_t, b_t, r_t = kap * g_prev, k * g_inv, b * g_inv, r * g_in
    a_ub = jnp.where(strict, bdot(kap_t, b_t, "nt"), 0.0)
    a_vk = jnp.where(strict, bdot(kap_t, k_t, "nt"), 0.0)
    rhs = -(bdot(kap_t, st0, "nn") + bdot(a_vk, v, "nn"))
    eye = diag.astype(F32)
    m = -a_ub
    inv = eye + m
    n = 1
    while n * 2 < c:
        m = fdot(m, m, "nn")
        inv = fdot(inv, eye + m, "nn")
        n *= 2
    u = fdot(inv, rhs, "nn")
    y = (bdot(r_t, st0, "nn")
         + bdot(jnp.where(incl, bdot(r_t, k_t, "nt"), 0.0), v, "nn")
         + bdot(jnp.where(incl, bdot(r_t, b_t, "nt"), 0.0), u, "nn"))
    g_full = jnp.exp(fdot(lw, jnp.ones((c, st0.shape[1]), F32), "tn"))
    st1 = g_full * st0 + bdot(k * g_end, v, "tn") + bdot(b * g_end, u, "tn")
    return y, st1


def ssd_chunk_fn(group, h0, xs, bm, cm, dt, a_log, d_skip):
    q = xs.shape[0]
    causal, _, _ = _tri_masks(q)
    a_row = -jnp.exp(a_log)
    cs_all = fdot(causal.astype(F32), dt * a_row, "nn")
    cs_t = cs_all.T
    cb = bdot(cm, bm, "nt")
    ys, hs = [], []
    p = SSD_HEAD_DIM
    for e in range(SSD_HEADS_PER_GROUP):
        hl = group * SSD_HEADS_PER_GROUP + e
        cs = cs_all[:, hl:hl + 1]
        seg = cs - cs_t[hl:hl + 1, :]
        lmat = jnp.where(causal, jnp.exp(jnp.where(causal, seg, 0.0)), 0.0)
        x_h = xs[:, e * p:(e + 1) * p]
        xdt = x_h * dt[:, hl:hl + 1]
        h0_e = h0[e * p:(e + 1) * p, :]
        cl = cs[q - 1:q, :]
        y = bdot(cb * lmat, xdt, "nn") + bdot(cm, h0_e, "nt") * jnp.exp(cs) + x_h * d_skip[:, hl:hl + 1]
        hs.append(h0_e * jnp.exp(cl) + bdot(xdt * jnp.exp(cl - cs), bm, "tn"))
        ys.append(y)
    return jnp.concatenate(ys, axis=-1), jnp.concatenate(hs, axis=0)


class Rows:
    def __init__(self, arr, w=None, cb=0, shifts=()):
        self.arr, self.w, self.cb, self.shifts = arr, (arr.shape[1] if w is None else w), cb, tuple(shifts)


def _as_rows(x):
    return x if isinstance(x, Rows) else Rows(x)


def _shift_down(x, halo, k):
    rolled = pltpu.roll(x, k, 0)
    first = rolled[0:SUBLANES]
    rid = lax.broadcasted_iota(jnp.int32, first.shape, 0)
    patched = jnp.where(rid < k, pltpu.roll(halo, k, 0), first)
    return jnp.concatenate([patched, rolled[SUBLANES:]], axis=0)


def _shift_up(g, carry, k):
    tm = g.shape[0]
    rolled = pltpu.roll(g, tm - k, 0)
    last = rolled[tm - SUBLANES:]
    rid = lax.broadcasted_iota(jnp.int32, last.shape, 0)
    patched = jnp.where(rid >= SUBLANES - k, pltpu.roll(carry, SUBLANES - k, 0), last)
    return jnp.concatenate([rolled[:tm - SUBLANES], patched], axis=0)


def _params():
    return pltpu.CompilerParams(vmem_limit_bytes=VMEM_LIMIT)


def _load_rows(refs, pos, rins, first_block):
    vals = []
    for r in rins:
        x = refs[pos][...].astype(F32) if refs[pos].dtype != F32 else refs[pos][...]
        pos += 1
        vals.append(x)
        if r.shifts:
            halo = refs[pos][...]
            pos += 1
            halo = jnp.where(first_block, jnp.zeros_like(halo), halo)
            for k in r.shifts:
                vals.append(_shift_down(x, halo, k))
    return vals, pos


def _row_specs(rins, tm, blk):
    specs, args = [], []
    for r in rins:
        specs.append(pl.BlockSpec((tm, r.w), lambda i, cb=r.cb: (blk(i), cb)))
        args.append(r.arr)
        if r.shifts:
            per = tm // SUBLANES
            specs.append(pl.BlockSpec((SUBLANES, r.w), lambda i, cb=r.cb: (jnp.maximum(blk(i) * per - 1, 0), cb)))
            args.append(r.arr)
    return specs, args


def rowwise_fwd(name, fn, rins, params, outs, tm):
    rins = [_as_rows(r) for r in rins]
    t = rins[0].arr.shape[0]
    tm = min(tm, t)
    nb = t // tm
    specs, args = _row_specs(rins, tm, lambda i: i)
    for p in params:
        specs.append(pl.BlockSpec(p.shape, lambda i: (0, 0)))
        args.append(p)
    n_in = len(args)

    def body(*refs):
        vals, pos = _load_rows(refs, 0, rins, pl.program_id(0) == 0)
        pv = [refs[pos + j][...] for j in range(len(params))]
        res = fn(*vals, *pv)
        for o_ref, o in zip(refs[n_in:], res):
            o_ref[...] = o.astype(o_ref.dtype)

    return pl.pallas_call(
        body, name=name, grid=(nb,), in_specs=specs,
        out_specs=[pl.BlockSpec((tm, w), lambda i: (i, 0)) for w, _ in outs],
        out_shape=[jax.ShapeDtypeStruct((t, w), dt) for w, dt in outs],
        compiler_params=_params(),
    )(*args)


def rowwise_bwd(name, fn, rins, params, cts, tm, grad_dtypes, row_add=None):
    rins = [_as_rows(r) for r in rins]
    cts = [[_as_rows(c) for c in lst] for lst in cts]
    row_add = [_as_rows(a) for a in (row_add or [])]
    t = rins[0].arr.shape[0]
    tm = min(tm, t)
    nb = t // tm
    rev = lambda i: nb - 1 - i
    specs, args = _row_specs(rins, tm, rev)
    for p in params:
        specs.append(pl.BlockSpec(p.shape, lambda i: (0, 0)))
        args.append(p)
    flat_cts = [c for lst in cts for c in lst] + row_add
    for c in flat_cts:
        specs.append(pl.BlockSpec((tm, c.w), lambda i, cb=c.cb: (rev(i), cb)))
        args.append(c.arr)
    n_in = len(args)
    want = [i for i, d in enumerate(grad_dtypes) if d is not None]
    out_specs = [pl.BlockSpec((tm, rins[i].w), lambda i_: (rev(i_), 0)) for i in want]
    out_shape = [jax.ShapeDtypeStruct((t, rins[i].w), grad_dtypes[i]) for i in want]
    out_specs += [pl.BlockSpec(p.shape, lambda i: (0, 0)) for p in params]
    out_shape += [jax.ShapeDtypeStruct(p.shape, F32) for p in params]
    n_out = len(out_shape)
    scratch = [pltpu.VMEM((SUBLANES, r.w), F32) for r in rins for _ in r.shifts]

    def body(*refs):
        i = pl.program_id(0)
        vals, pos = _load_rows(refs, 0, rins, rev(i) == 0)
        pv = [refs[pos + j][...] for j in range(len(params))]
        pos += len(params)
        outs, vjp = jax.vjp(fn, *vals, *pv)
        ct_vals = []
        for o, lst in zip(outs, cts):
            acc = None
            for _ in lst:
                cv = refs[pos][...].astype(F32)
                pos += 1
                acc = cv if acc is None else acc + cv
            ct_vals.append(acc.astype(o.dtype))
        adds = [refs[pos + j][...].astype(F32) for j in range(len(row_add))]
        grads = vjp(tuple(ct_vals))
        out_refs = refs[n_in:n_in + n_out]
        carry_refs = refs[n_in + n_out:]

        @pl.when(i == 0)
        def _():
            for cr in carry_refs:
                cr[...] = jnp.zeros_like(cr)
            for pr in out_refs[len(want):]:
                pr[...] = jnp.zeros_like(pr)

        gi, ci, oi = 0, 0, 0
        for idx, r in enumerate(rins):
            d = grads[gi]
            gi += 1
            for k in r.shifts:
                dk = grads[gi]
                gi += 1
                d = d + _shift_up(dk, carry_refs[ci][...], k)
                carry_refs[ci][...] = dk[0:SUBLANES]
                ci += 1
            if idx == 0:
                for a in adds:
                    d = d + a
            if grad_dtypes[idx] is not None:
                out_refs[oi][...] = d.astype(out_refs[oi].dtype)
                oi += 1
        for pr, gp in zip(out_refs[len(want):], grads[gi:]):
            pr[...] += gp

    res = pl.pallas_call(
        body, name=name, grid=(nb,), in_specs=specs, out_specs=out_specs, out_shape=out_shape,
        scratch_shapes=scratch, compiler_params=_params(),
    )(*args)
    return res[:len(want)], res[len(want):]


def _pick(n, pref):
    for c in pref:
        if n % c == 0:
            return c
    return n


def mm(name, a, b, mode, out_dtype=F32, res=None, b_slabs=None, out_slabs=None):
    if mode == "tn":
        k_dim, m_dim = a.shape
    else:
        m_dim, k_dim = a.shape
    if b_slabs:
        n_dim = b.shape[0] * b.shape[2] if mode == "nn" else b.shape[1]
    else:
        n_dim = b.shape[0] if mode == "nt" else b.shape[1]
    n_slabs = out_slabs or (b_slabs if (b_slabs and mode == "nn") else 1)
    k_slabs = b_slabs if (b_slabs and mode == "nt") else 1
    tm = _pick(m_dim, (512, 256, 128))
    tn = _pick(n_dim // n_slabs, (1024, 768, 512, 384, 256, 128))
    tk = _pick(k_dim // k_slabs, (512, 256, 128))
    nji = n_dim // n_slabs // tn
    nki = k_dim // k_slabs // tk
    nblk = lambda js, j: js * nji + j
    kblk = lambda ks, k: ks * nki + k
    if mode == "tn":
        a_spec = pl.BlockSpec((tk, tm), lambda i, js, j, ks, k: (kblk(ks, k), i))
    else:
        a_spec = pl.BlockSpec((tm, tk), lambda i, js, j, ks, k: (i, kblk(ks, k)))
    if b_slabs and mode == "nn":
        b_spec = pl.BlockSpec((None, tk, tn), lambda i, js, j, ks, k: (js, k, j))
    elif b_slabs and mode == "nt":
        b_spec = pl.BlockSpec((None, tn, tk), lambda i, js, j, ks, k: (ks, nblk(js, j), k))
    elif mode == "nt":
        b_spec = pl.BlockSpec((tn, tk), lambda i, js, j, ks, k: (nblk(js, j), kblk(ks, k)))
    else:
        b_spec = pl.BlockSpec((tk, tn), lambda i, js, j, ks, k: (kblk(ks, k), nblk(js, j)))
    specs, args = [a_spec, b_spec], [a, b]
    if res is not None:
        specs.append(pl.BlockSpec((tm, tn), lambda i, js, j, ks, k: (i, nblk(js, j))))
        args.append(res)
    if out_slabs:
        o_spec = pl.BlockSpec((None, tm, tn), lambda i, js, j, ks, k: (js, i, j))
        o_shape = jax.ShapeDtypeStruct((out_slabs, m_dim, n_dim // out_slabs), out_dtype)
    else:
        o_spec = pl.BlockSpec((tm, tn), lambda i, js, j, ks, k: (i, nblk(js, j)))
        o_shape = jax.ShapeDtypeStruct((m_dim, n_dim), out_dtype)

    def body(*refs):
        a_ref, b_ref = refs[0], refs[1]
        o_ref, acc_ref = refs[-2], refs[-1]
        ks, kk = pl.program_id(3), pl.program_id(4)

        @pl.when((ks == 0) & (kk == 0))
        def _():
            acc_ref[...] = jnp.zeros_like(acc_ref)

        acc_ref[...] += _dg(a_ref[...].astype(BF16), b_ref[...].astype(BF16), mode)

        @pl.when((ks == k_slabs - 1) & (kk == nki - 1))
        def _():
            out = acc_ref[...]
            if res is not None:
                out = out + refs[2][...].astype(F32)
            o_ref[...] = out.astype(o_ref.dtype)

    return pl.pallas_call(
        body, name=name, grid=(m_dim // tm, n_slabs, nji, k_slabs, nki), in_specs=specs, out_specs=o_spec,
        out_shape=o_shape, scratch_shapes=[pltpu.VMEM((tm, tn), F32)],
        compiler_params=pltpu.CompilerParams(
            dimension_semantics=("parallel", "parallel", "parallel", "arbitrary", "arbitrary"),
            vmem_limit_bytes=VMEM_LIMIT),
    )(*args)


def rwkv_scan_fwd(r, lw, k, v, kap, b):
    t = r.shape[0]
    c, hps, hd = min(RWKV_CHUNK, t), RWKV_HEADS_PER_STEP, RWKV_HEAD_DIM
    nc, ng, wl = t // c, RWKV_HEADS // hps, hps * hd
    spec = pl.BlockSpec((c, wl), lambda g, ci: (ci, g))

    def body(r_ref, lw_ref, k_ref, v_ref, kap_ref, b_ref, y_ref, ck_ref, st_ref):
        @pl.when(pl.program_id(1) == 0)
        def _():
            st_ref[...] = jnp.zeros_like(st_ref)

        ck_ref[...] = st_ref[...]
        ins = [x[...] for x in (r_ref, lw_ref, k_ref, v_ref, kap_ref, b_ref)]
        ys = []
        for h in range(hps):
            sl = slice(h * hd, (h + 1) * hd)
            y, st1 = rwkv_chunk_fn(st_ref[sl, :], *[x[:, sl] for x in ins])
            st_ref[sl, :] = st1
            ys.append(y)
        y_ref[...] = jnp.concatenate(ys, axis=-1)

    return pl.pallas_call(
        body, name="rwkv_scan_fwd", grid=(ng, nc), in_specs=[spec] * 6,
        out_specs=[spec, pl.BlockSpec((None, wl, hd), lambda g, ci: (ci, g, 0))],
        out_shape=[jax.ShapeDtypeStruct((t, RWKV_WIDTH), F32), jax.ShapeDtypeStruct((nc, RWKV_WIDTH, hd), F32)],
        scratch_shapes=[pltpu.VMEM((wl, hd), F32)], compiler_params=_params(),
    )(r, lw, k, v, kap, b)


def rwkv_scan_bwd(r, lw, k, v, kap, b, ck, dy):
    t = r.shape[0]
    c, hps, hd = min(RWKV_CHUNK, t), RWKV_HEADS_PER_STEP, RWKV_HEAD_DIM
    nc, ng, wl = t // c, RWKV_HEADS // hps, hps * hd
    spec = pl.BlockSpec((c, wl), lambda g, ci: (nc - 1 - ci, g))

    def body(r_ref, lw_ref, k_ref, v_ref, kap_ref, b_ref, ck_ref, dy_ref, *rest):
        out_refs, dst_ref = rest[:6], rest[6]

        @pl.when(pl.program_id(1) == 0)
        def _():
            dst_ref[...] = jnp.zeros_like(dst_ref)

        ins = [x[...] for x in (r_ref, lw_ref, k_ref, v_ref, kap_ref, b_ref)]
        dyv = dy_ref[...]
        parts = [[] for _ in range(6)]
        for h in range(hps):
            sl = slice(h * hd, (h + 1) * hd)
            _, vjp = jax.vjp(rwkv_chunk_fn, ck_ref[sl, :], *[x[:, sl] for x in ins])
            grads = vjp((dyv[:, sl], dst_ref[sl, :]))
            dst_ref[sl, :] = grads[0]
            for j in range(6):
                parts[j].append(grads[1 + j])
        for j in range(6):
            out_refs[j][...] = jnp.concatenate(parts[j], axis=-1)

    return pl.pallas_call(
        body, name="rwkv_scan_bwd", grid=(ng, nc),
        in_specs=[spec] * 6 + [pl.BlockSpec((None, wl, hd), lambda g, ci: (nc - 1 - ci, g, 0)), spec],
        out_specs=[spec] * 6, out_shape=[jax.ShapeDtypeStruct((t, RWKV_WIDTH), F32)] * 6,
        scratch_shapes=[pltpu.VMEM((wl, hd), F32)], compiler_params=_params(),
    )(r, lw, k, v, kap, b, ck, dy)


def _ssd_specs(q, blk):
    gw = SSD_WIDTH // 2
    return [pl.BlockSpec((q, gw), lambda g, ci: (blk(ci), g)),
            pl.BlockSpec((q, SSD_STATE), lambda g, ci: (blk(ci), g)),
            pl.BlockSpec((q, SSD_STATE), lambda g, ci: (blk(ci), g)),
            pl.BlockSpec((q, LANES), lambda g, ci: (blk(ci), 0)),
            pl.BlockSpec((1, LANES), lambda g, ci: (0, 0)),
            pl.BlockSpec((1, LANES), lambda g, ci: (0, 0))]


def ssd_scan_fwd(xs, bm, cm, dt, a_log, d_skip):
    t = xs.shape[0]
    q = min(SSD_CHUNK, t)
    nc, gw = t // q, SSD_WIDTH // 2

    def body(xs_ref, bm_ref, cm_ref, dt_ref, al_ref, d_ref, y_ref, ck_ref, h_ref):
        @pl.when(pl.program_id(1) == 0)
        def _():
            h_ref[...] = jnp.zeros_like(h_ref)

        ck_ref[...] = h_ref[...]
        args = (h_ref[...], xs_ref[...], bm_ref[...], cm_ref[...], dt_ref[...], al_ref[...], d_ref[...])
        g = pl.program_id(0)

        @pl.when(g == 0)
        def _():
            y, h1 = ssd_chunk_fn(0, *args)
            y_ref[...] = y
            h_ref[...] = h1

        @pl.when(g == 1)
        def _():
            y, h1 = ssd_chunk_fn(1, *args)
            y_ref[...] = y
            h_ref[...] = h1

    return pl.pallas_call(
        body, name="ssd_scan_fwd", grid=(2, nc), in_specs=_ssd_specs(q, lambda ci: ci),
        out_specs=[pl.BlockSpec((q, gw), lambda g, ci: (ci, g)),
                   pl.BlockSpec((None, gw, SSD_STATE), lambda g, ci: (ci, g, 0))],
        out_shape=[jax.ShapeDtypeStruct((t, SSD_WIDTH), F32), jax.ShapeDtypeStruct((nc, SSD_WIDTH, SSD_STATE), F32)],
        scratch_shapes=[pltpu.VMEM((gw, SSD_STATE), F32)], compiler_params=_params(),
    )(xs, bm, cm, dt, a_log, d_skip)


def ssd_scan_bwd(xs, bm, cm, dt, a_log, d_skip, ck, dy):
    t = xs.shape[0]
    q = min(SSD_CHUNK, t)
    nc, gw = t // q, SSD_WIDTH // 2
    rev = lambda ci: nc - 1 - ci

    def body(xs_ref, bm_ref, cm_ref, dt_ref, al_ref, d_ref, ck_ref, dy_ref,
             dxs_ref, dbm_ref, dcm_ref, ddt_ref, dal_ref, dd_ref, dh_ref):
        g, ci = pl.program_id(0), pl.program_id(1)

        @pl.when(ci == 0)
        def _():
            dh_ref[...] = jnp.zeros_like(dh_ref)

        @pl.when((ci == 0) & (g == 0))
        def _():
            dal_ref[...] = jnp.zeros_like(dal_ref)
            dd_ref[...] = jnp.zeros_like(dd_ref)

        args = (ck_ref[...], xs_ref[...], bm_ref[...], cm_ref[...], dt_ref[...], al_ref[...], d_ref[...])

        def run(group):
            _, vjp = jax.vjp(functools.partial(ssd_chunk_fn, group), *args)
            dh0, dxs, dbm, dcm, ddt, dal, dd = vjp((dy_ref[...], dh_ref[...]))
            dh_ref[...] = dh0
            dxs_ref[...] = dxs
            dbm_ref[...] = dbm
            dcm_ref[...] = dcm
            ddt_ref[...] = ddt
            dal_ref[...] += dal
            dd_ref[...] += dd

        pl.when(g == 0)(lambda: run(0))
        pl.when(g == 1)(lambda: run(1))

    in_specs = _ssd_specs(q, rev) + [pl.BlockSpec((None, gw, SSD_STATE), lambda g, ci: (rev(ci), g, 0)),
                                     pl.BlockSpec((q, gw), lambda g, ci: (rev(ci), g))]
    return pl.pallas_call(
        body, name="ssd_scan_bwd", grid=(2, nc), in_specs=in_specs,
        out_specs=[pl.BlockSpec((q, gw), lambda g, ci: (rev(ci), g)),
                   pl.BlockSpec((q, SSD_STATE), lambda g, ci: (rev(ci), g)),
                   pl.BlockSpec((q, SSD_STATE), lambda g, ci: (rev(ci), g)),
                   pl.BlockSpec((None, q, LANES), lambda g, ci: (g, rev(ci), 0)),
                   pl.BlockSpec((1, LANES), lambda g, ci: (0, 0)),
                   pl.BlockSpec((1, LANES), lambda g, ci: (0, 0))],
        out_shape=[jax.ShapeDtypeStruct((t, SSD_WIDTH), F32), jax.ShapeDtypeStruct((t, 2 * SSD_STATE), F32),
                   jax.ShapeDtypeStruct((t, 2 * SSD_STATE), F32), jax.ShapeDtypeStruct((2, t, LANES), F32),
                   jax.ShapeDtypeStruct((1, LANES), F32), jax.ShapeDtypeStruct((1, LANES), F32)],
        scratch_shapes=[pltpu.VMEM((gw, SSD_STATE), F32)], compiler_params=_params(),
    )(xs, bm, cm, dt, a_log, d_skip, ck, dy)


def loss_and_grad(x, tgt, g, tm):
    t, d = x.shape
    tm = min(tm, t)
    nb = t // tm

    def body(x_ref, t_ref, g_ref, loss_ref, dx_ref, dg_ref):
        @pl.when(pl.program_id(0) == 0)
        def _():
            loss_ref[...] = jnp.zeros_like(loss_ref)
            dg_ref[...] = jnp.zeros_like(dg_ref)

        val, vjp = jax.vjp(loss_fn, x_ref[...], t_ref[...], g_ref[...])
        dx, _, dg = vjp(jnp.ones((1, 1), F32))
        loss_ref[...] += jnp.broadcast_to(val, loss_ref.shape)
        dx_ref[...] = dx
        dg_ref[...] += dg

    row = pl.BlockSpec((tm, d), lambda i: (i, 0))
    one = pl.BlockSpec((1, d), lambda i: (0, 0))
    return pl.pallas_call(
        body, name="loss_and_grad", grid=(nb,), in_specs=[row, row, one],
        out_specs=[pl.BlockSpec((SUBLANES, LANES), lambda i: (0, 0)), row, one],
        out_shape=[jax.ShapeDtypeStruct((SUBLANES, LANES), F32), jax.ShapeDtypeStruct((t, d), F32),
                   jax.ShapeDtypeStruct((1, d), F32)],
        compiler_params=_params(),
    )(x, tgt, g)


def adamw(name, recv, w, m, v):
    rows, cols = w.shape
    recv_block_bytes = 4 * 1024 * 1024
    tm = _pick(rows, [c for c in (256, 128, 64, 32, 16, 8) if N_DEV * c * cols * 4 <= recv_block_bytes])
    c1 = 1.0 / (1.0 - ADAM_B1 ** ADAM_STEP)
    c2 = 1.0 / (1.0 - ADAM_B2 ** ADAM_STEP)

    def body(recv_ref, w_ref, m_ref, v_ref, g_ref, d_ref, nm_ref, nv_ref):
        g = recv_ref[0]
        for p in range(1, N_DEV):
            g = g + recv_ref[p]
        nm = ADAM_B1 * m_ref[...] + (1.0 - ADAM_B1) * g
        nv = ADAM_B2 * v_ref[...] + (1.0 - ADAM_B2) * jnp.square(g)
        g_ref[...] = g
        nm_ref[...] = nm
        nv_ref[...] = nv
        d_ref[...] = -ADAM_LR * ((nm * c1) / (jnp.sqrt(nv * c2) + ADAM_EPS) + ADAM_WD * w_ref[...])

    blk = pl.BlockSpec((tm, cols), lambda i: (i, 0))
    return pl.pallas_call(
        body, name=name, grid=(rows // tm,),
        in_specs=[pl.BlockSpec((N_DEV, tm, cols), lambda i: (0, i, 0)), blk, blk, blk],
        out_specs=[blk] * 4, out_shape=[jax.ShapeDtypeStruct((rows, cols), F32)] * 4,
        compiler_params=_params(),
    )(recv, w, m, v)


def _mesh_pos():
    return lax.axis_index("x"), lax.axis_index("y"), lax.axis_index("c")


def _peer(pos, mask):
    x, y, c = pos
    return (1 - x if mask & 4 else x, 1 - y if mask & 2 else y, 1 - c if mask & 1 else c)


def _linear(pos):
    return 4 * pos[0] + 2 * pos[1] + pos[2]


def _exchange(name, xs, scatter):
    n = len(xs)
    any_spec = pl.BlockSpec(memory_space=pl.ANY)

    def body(*refs):
        ins, outs = refs[:n], refs[n:2 * n]
        send_sems, recv_sems, local_sems = refs[2 * n:]
        me = _mesh_pos()
        me_lin = _linear(me)
        local_copies, remote_copies = [], []
        for ti in range(n):
            src_mine = ins[ti].at[me_lin] if scatter else ins[ti]
            local = pltpu.make_async_copy(src_mine, outs[ti].at[me_lin], local_sems.at[ti])
            local.start()
            local_copies.append(local)
            for mask in range(1, N_DEV):
                peer = _peer(me, mask)
                src = ins[ti].at[_linear(peer)] if scatter else ins[ti]
                cp = pltpu.make_async_remote_copy(
                    src_ref=src, dst_ref=outs[ti].at[me_lin], send_sem=send_sems.at[ti, mask - 1],
                    recv_sem=recv_sems.at[ti, mask - 1], device_id=peer, device_id_type=pl.DeviceIdType.MESH)
                cp.start()
                remote_copies.append(cp)
        for ti in range(n):
            for mask in range(1, N_DEV):
                peer = _peer(me, mask)
                src = ins[ti].at[_linear(peer)] if scatter else ins[ti]
                pltpu.make_async_remote_copy(
                    src_ref=src, dst_ref=outs[ti].at[_linear(peer)], send_sem=send_sems.at[ti, mask - 1],
                    recv_sem=recv_sems.at[ti, mask - 1], device_id=peer, device_id_type=pl.DeviceIdType.MESH).wait_recv()
        for cp in remote_copies:
            cp.wait_send()
        for cp in local_copies:
            cp.wait()

    out_shape = [jax.ShapeDtypeStruct(x.shape if scatter else (N_DEV,) + x.shape, x.dtype) for x in xs]
    return pl.pallas_call(
        body, name=name, in_specs=[any_spec] * n, out_specs=[any_spec] * n, out_shape=out_shape,
        scratch_shapes=[pltpu.SemaphoreType.DMA((n, N_DEV - 1)), pltpu.SemaphoreType.DMA((n, N_DEV - 1)),
                        pltpu.SemaphoreType.DMA((n,))],
        compiler_params=pltpu.CompilerParams(has_side_effects=True),
    )(*xs)


def all_gather_many(name, xs):
    return _exchange(name, xs, scatter=False)


def scatter_many(name, xs):
    return _exchange(name, xs, scatter=True)


_Z = (0, 1024)
_XBC = (1024, 2560)
_DT = (2560, 2576)
_RKV = (2576, 5648)
_PW = (5648, 5744)
_PA = (5744, 5840)
_PG = (5840, 6096)
D_IN = 6096

_SMALL = ("norm_mix_g", "ssd_conv_b", "ssd_dt_bias", "ssd_a_log", "ssd_d", "ssd_norm_g", "rwkv_mu", "rwkv_w0",
          "rwkv_a0", "rwkv_k_k", "rwkv_k_a", "rwkv_r_k", "rwkv_ln_w", "rwkv_ln_b", "norm_x_g", "norm_mem_g",
          "norm_ffn_g", "final_norm_g")
_WEIGHTS = ("norm_mix_g", "w_in", "ssd_conv_w", "ssd_conv_b", "ssd_dt_bias", "ssd_a_log", "ssd_d", "ssd_norm_g",
            "rwkv_mu", "rwkv_w0", "rwkv_w2", "rwkv_a0", "rwkv_a2", "rwkv_g2", "rwkv_k_k", "rwkv_k_a", "rwkv_r_k",
            "rwkv_ln_w", "rwkv_ln_b", "w_out", "norm_x_g", "norm_mem_g", "xattn_wq", "xattn_wk", "xattn_wv",
            "xattn_wo", "norm_ffn_g", "ffn_w1", "ffn_w2", "final_norm_g")


def _pad_lanes(x, width=LANES):
    return jnp.pad(x, ((0, 0), (0, width - x.shape[1])))


def _pack_small(vals):
    flat = jnp.concatenate([vals[n].reshape(-1) for n in _SMALL])
    rows = -(-flat.shape[0] // (LANES * SUBLANES)) * SUBLANES
    return jnp.pad(flat, (0, rows * LANES - flat.shape[0])).reshape(rows, LANES)


def _unpack_small(packed, shapes):
    flat = packed.reshape(-1)
    out, pos = {}, 0
    for n in _SMALL:
        size = 1
        for s in shapes[n]:
            size *= s
        out[n] = flat[pos:pos + size].reshape(shapes[n])
        pos += size
    return out


def _cols(w, rng):
    return w[:, rng[0]:rng[1]]


def kernel(x, mem, norm_mix_g, w_in, ssd_conv_w, ssd_conv_b, ssd_dt_bias, ssd_a_log, ssd_d, ssd_norm_g, rwkv_mu, rwkv_w0, rwkv_w2, rwkv_a0, rwkv_a2, rwkv_g2, rwkv_k_k, rwkv_k_a, rwkv_r_k, rwkv_ln_w, rwkv_ln_b, w_out, norm_x_g, norm_mem_g, xattn_wq, xattn_wk, xattn_wv, xattn_wo, norm_ffn_g, ffn_w1, ffn_w2, final_norm_g, loss_target, m_norm_mix_g, m_w_in, m_ssd_conv_w, m_ssd_conv_b, m_ssd_dt_bias, m_ssd_a_log, m_ssd_d, m_ssd_norm_g, m_rwkv_mu, m_rwkv_w0, m_rwkv_w2, m_rwkv_a0, m_rwkv_a2, m_rwkv_g2, m_rwkv_k_k, m_rwkv_k_a, m_rwkv_r_k, m_rwkv_ln_w, m_rwkv_ln_b, m_w_out, m_norm_x_g, m_norm_mem_g, m_xattn_wq, m_xattn_wk, m_xattn_wv, m_xattn_wo, m_norm_ffn_g, m_ffn_w1, m_ffn_w2, m_final_norm_g, v_norm_mix_g, v_w_in, v_ssd_conv_w, v_ssd_conv_b, v_ssd_dt_bias, v_ssd_a_log, v_ssd_d, v_ssd_norm_g, v_rwkv_mu, v_rwkv_w0, v_rwkv_w2, v_rwkv_a0, v_rwkv_a2, v_rwkv_g2, v_rwkv_k_k, v_rwkv_k_a, v_rwkv_r_k, v_rwkv_ln_w, v_rwkv_ln_b, v_w_out, v_norm_x_g, v_norm_mem_g, v_xattn_wq, v_xattn_wk, v_xattn_wv, v_xattn_wo, v_norm_ffn_g, v_ffn_w1, v_ffn_w2, v_final_norm_g):
    given = dict(locals())
    wts = {n: given[n] for n in _WEIGHTS}
    mom_m = {n: given["m_" + n] for n in _WEIGHTS}
    mom_v = {n: given["v_" + n] for n in _WEIGHTS}
    d = D_MODEL
    xt, memt, tgt = x[0], mem[0], loss_target[0]
    tm = 256

    big = {"w_in": w_in[0], "w_out": w_out[0], "xattn_wq": xattn_wq[0], "xattn_wk": xattn_wk[0],
           "xattn_wv": xattn_wv[0], "xattn_wo": xattn_wo[0], "ffn_w1": ffn_w1[0], "ffn_w2": ffn_w2[0]}
    small_sh = {"ssd_conv_w": ssd_conv_w.reshape(4, -1), "rwkv_w2": rwkv_w2[0], "rwkv_a2": rwkv_a2[0],
                "rwkv_g2": rwkv_g2[0]}
    send = [rowwise_fwd("cast_" + n, cast_fn, [a], [], [(a.shape[1], BF16)], 256)[0] for n, a in big.items()]
    send += list(small_sh.values())
    gathered = all_gather_many("gather_weights", send)
    g_big = dict(zip(big, gathered[:len(big)]))
    g_small = dict(zip(small_sh, gathered[len(big):]))

    w_in_full = jnp.transpose(g_big["w_in"], (1, 0, 2)).reshape(d, D_IN)
    w_z, w_xbc, w_rkv, w_pg = (_cols(w_in_full, r) for r in (_Z, _XBC, _RKV, _PG))
    w_sm = jnp.concatenate([_pad_lanes(_cols(w_in_full, r)) for r in (_PW, _PA, _DT)], axis=1)
    w_out_f = g_big["w_out"].reshape(d, d)
    wq_f, wk_f, wv_f, wo_f = (g_big[n].reshape(d, d) for n in ("xattn_wq", "xattn_wk", "xattn_wv", "xattn_wo"))
    w1_s = g_big["ffn_w1"]
    w2_f = g_big["ffn_w2"].reshape(D_FF, d)
    unshard_cols = lambda g: jnp.transpose(g, (1, 0, 2)).reshape(g.shape[1], -1)
    conv_w_f = unshard_cols(g_small["ssd_conv_w"])
    pad_rows = lambda a: jnp.pad(a, ((0, LANES - a.shape[0]), (0, 0)))
    w2p, a2p = pad_rows(unshard_cols(g_small["rwkv_w2"])), pad_rows(unshard_cols(g_small["rwkv_a2"]))
    g2_f = unshard_cols(g_small["rwkv_g2"])

    mu = rwkv_mu
    mu_rkv, mu_pg = mu[:, :3072], mu[:, 3264:3520]
    mu_pwa = jnp.concatenate([_pad_lanes(mu[:, 3072:3168]), _pad_lanes(mu[:, 3168:3264])], axis=1)
    dt_bias_p, a_log_p, d_p = _pad_lanes(ssd_dt_bias), _pad_lanes(ssd_a_log), _pad_lanes(ssd_d)
    r_k_row = rwkv_r_k.reshape(1, RWKV_WIDTH)
    g_final = final_norm_g.reshape(1, d)

    (h1,) = rowwise_fwd("norm_mix", rmsnorm_fn, [xt], [norm_mix_g], [(d, BF16)], tm)
    u_z = mm("in_z", h1, w_z, "nn")
    u_xbc = mm("in_xbc", h1, w_xbc, "nn")
    u_rkv = mm("in_rkv", h1, w_rkv, "nn")
    u_pg = mm("in_pg", h1, w_pg, "nn")
    u_sm = mm("in_small", h1, w_sm, "nn")

    ssd_pre_rows = lambda: [Rows(u_xbc, shifts=(1, 2, 3)), Rows(u_sm, LANES, 2)]
    ssd_pre_params = [conv_w_f, ssd_conv_b, dt_bias_p]
    xs, bm, cm, dt = rowwise_fwd("ssd_pre", ssd_pre_fn, ssd_pre_rows(), ssd_pre_params,
                                 [(SSD_WIDTH, F32), (256, F32), (256, F32), (LANES, F32)], tm)
    y_scan, ssd_ck = ssd_scan_fwd(xs, bm, cm, dt, a_log_p, d_p)
    (y_ssd,) = rowwise_fwd("ssd_post", ssd_post_fn, [y_scan, u_z], [ssd_norm_g], [(SSD_WIDTH, BF16)], tm)

    rwkv_pre_rows = lambda: [Rows(u_rkv, shifts=(1,)), Rows(u_pg, shifts=(1,)), Rows(u_sm, 2 * LANES, 0, shifts=(1,))]
    rwkv_pre_params = [mu_rkv, mu_pg, mu_pwa, rwkv_w0, w2p, rwkv_a0, a2p, g2_f, rwkv_k_k, rwkv_k_a]
    r_, lw_, k_, v_, kap_, b_, gate_ = rowwise_fwd("rwkv_pre", rwkv_pre_fn, rwkv_pre_rows(), rwkv_pre_params,
                                                   [(RWKV_WIDTH, F32)] * 7, 128)
    ys_r, rwkv_ck = rwkv_scan_fwd(r_, lw_, k_, v_, kap_, b_)
    rwkv_post_params = [rwkv_ln_w, rwkv_ln_b, r_k_row]
    (y_rwkv,) = rowwise_fwd("rwkv_post", rwkv_post_fn, [ys_r, r_, k_, v_, gate_], rwkv_post_params,
                            [(RWKV_WIDTH, BF16)], tm)
    ycat = jnp.concatenate([y_ssd, y_rwkv], axis=1)
    x1 = mm("out_proj", ycat, w_out_f, "nn", res=xt)

    (h2,) = rowwise_fwd("norm_x", rmsnorm_fn, [x1], [norm_x_g], [(d, BF16)], tm)
    (mn,) = rowwise_fwd("norm_mem", rmsnorm_fn, [memt], [norm_mem_g], [(d, BF16)], tm)
    q = mm("xattn_q", h2, wq_f, "nn", out_dtype=BF16)
    kx = mm("xattn_k", mn, wk_f, "nn")
    vx = mm("xattn_v", mn, wv_f, "nn")
    (o,) = rowwise_fwd("xattn", attn_fn, [q], [kx, vx], [(d, BF16)], tm)
    x2 = mm("xattn_o", o, wo_f, "nn", res=x1)

    (h3,) = rowwise_fwd("norm_ffn", rmsnorm_fn, [x2], [norm_ffn_g], [(d, BF16)], tm)
    a_ffn = mm("ffn_1", h3, w1_s, "nn", b_slabs=N_DEV)
    (hid,) = rowwise_fwd("relu2", relu2_fn, [a_ffn], [], [(D_FF, BF16)], tm)
    x3 = mm("ffn_2", hid, w2_f, "nn", res=x2)

    loss_blk, dx3, dg_final = loss_and_grad(x3, tgt, g_final, tm)

    grads = {}
    grads["ffn_w2"] = mm("d_ffn_w2", hid, dx3, "tn").reshape(N_DEV, D_FF // N_DEV, d)
    d_hid = mm("d_hid", dx3, w2_f, "nt")
    (da,), _ = rowwise_bwd("relu2_bwd", relu2_fn, [a_ffn], [], [[d_hid]], tm, [BF16])
    grads["ffn_w1"] = mm("d_ffn_w1", h3, da, "tn", out_slabs=N_DEV)
    dh3 = mm("d_h3", da, w1_s, "nt", b_slabs=N_DEV)
    (dx2,), (dg_ffn,) = rowwise_bwd("norm_ffn_bwd", rmsnorm_fn, [x2], [norm_ffn_g], [[dh3]], tm, [F32], row_add=[dx3])

    grads["xattn_wo"] = mm("d_wo", o, dx2, "tn").reshape(N_DEV, d // N_DEV, d)
    d_o = mm("d_o", dx2, wo_f, "nt")
    (dq,), (dkx, dvx) = rowwise_bwd("xattn_bwd", attn_fn, [q], [kx, vx], [[d_o]], tm, [BF16])
    grads["xattn_wq"] = mm("d_wq", h2, dq, "tn").reshape(N_DEV, d // N_DEV, d)
    dh2 = mm("d_h2", dq, wq_f, "nt")
    grads["xattn_wk"] = mm("d_wk", mn, dkx, "tn").reshape(N_DEV, d // N_DEV, d)
    grads["xattn_wv"] = mm("d_wv", mn, dvx, "tn").reshape(N_DEV, d // N_DEV, d)
    dmn = mm("d_mn_v", dvx, wv_f, "nt", res=mm("d_mn_k", dkx, wk_f, "nt"))
    _, (dg_mem,) = rowwise_bwd("norm_mem_bwd", rmsnorm_fn, [memt], [norm_mem_g], [[dmn]], tm, [None])
    (dx1,), (dg_x,) = rowwise_bwd("norm_x_bwd", rmsnorm_fn, [x1], [norm_x_g], [[dh2]], tm, [F32], row_add=[dx2])

    grads["w_out"] = mm("d_w_out", ycat, dx1, "tn").reshape(N_DEV, d // N_DEV, d)
    d_ycat = mm("d_ycat", dx1, w_out_f, "nt")

    (d_ys, d_r1, d_k1, d_v1, d_gate), (dln_w, dln_b, dr_k) = rowwise_bwd(
        "rwkv_post_bwd", rwkv_post_fn, [ys_r, r_, k_, v_, gate_], rwkv_post_params,
        [[Rows(d_ycat, RWKV_WIDTH, 1)]], tm, [F32] * 5)
    d_r2, d_lw, d_k2, d_v2, d_kap, d_b = rwkv_scan_bwd(r_, lw_, k_, v_, kap_, b_, rwkv_ck, d_ys)
    (du_rkv, du_pg, du_pwa), rwkv_pg = rowwise_bwd(
        "rwkv_pre_bwd", rwkv_pre_fn, rwkv_pre_rows(), rwkv_pre_params,
        [[d_r1, d_r2], [d_lw], [d_k1, d_k2], [d_v1, d_v2], [d_kap], [d_b], [d_gate]], 128, [BF16] * 3)
    dmu_rkv, dmu_pg, dmu_pwa, dw0, dw2p, da0, da2p, dg2, dk_k, dk_a = rwkv_pg

    (d_yscan, du_z), (dssd_norm_g,) = rowwise_bwd("ssd_post_bwd", ssd_post_fn, [y_scan, u_z], [ssd_norm_g],
                                                  [[Rows(d_ycat, SSD_WIDTH, 0)]], tm, [F32, BF16])
    dxs, dbm, dcm, ddt2, da_log_p, dd_p = ssd_scan_bwd(xs, bm, cm, dt, a_log_p, d_p, ssd_ck, d_yscan)
    (du_xbc, du_dt), (dconv_w, dconv_b, ddt_bias_p) = rowwise_bwd(
        "ssd_pre_bwd", ssd_pre_fn, ssd_pre_rows(), ssd_pre_params,
        [[dxs], [dbm], [dcm], [ddt2[0], ddt2[1]]], tm, [BF16, BF16])
    du_sm = jnp.concatenate([du_pwa, du_dt], axis=1)

    dw_z = mm("d_w_z", h1, du_z, "tn")
    dw_xbc = mm("d_w_xbc", h1, du_xbc, "tn")
    dw_rkv = mm("d_w_rkv", h1, du_rkv, "tn")
    dw_pg = mm("d_w_pg", h1, du_pg, "tn")
    dw_sm = mm("d_w_small", h1, du_sm, "tn")
    dh1 = mm("d_h1_z", du_z, w_z, "nt")
    dh1 = mm("d_h1_xbc", du_xbc, w_xbc, "nt", res=dh1)
    dh1 = mm("d_h1_rkv", du_rkv, w_rkv, "nt", res=dh1)
    dh1 = mm("d_h1_pg", du_pg, w_pg, "nt", res=dh1)
    dh1 = mm("d_h1_small", du_sm, w_sm, "nt", res=dh1)
    (dx,), (dg_mix,) = rowwise_bwd("norm_mix_bwd", rmsnorm_fn, [xt], [norm_mix_g], [[dh1]], tm, [F32], row_add=[dx1])

    dw_in_full = jnp.concatenate([dw_z, dw_xbc, dw_sm[:, 256:272], dw_rkv, dw_sm[:, 0:96], dw_sm[:, 128:224], dw_pg], axis=1)
    to_slabs = lambda g: jnp.transpose(g.reshape(g.shape[0], N_DEV, -1), (1, 0, 2))
    grads["w_in"] = to_slabs(dw_in_full)
    grads["ssd_conv_w"] = to_slabs(dconv_w)
    grads["rwkv_w2"] = to_slabs(dw2p[:96])
    grads["rwkv_a2"] = to_slabs(da2p[:96])
    grads["rwkv_g2"] = to_slabs(dg2)

    dmu = jnp.concatenate([dmu_rkv, dmu_pwa[:, 0:96], dmu_pwa[:, 128:224], dmu_pg], axis=1)
    small_grads = {
        "norm_mix_g": dg_mix, "ssd_conv_b": dconv_b, "ssd_dt_bias": ddt_bias_p[:, :16], "ssd_a_log": da_log_p[:, :16],
        "ssd_d": dd_p[:, :16], "ssd_norm_g": dssd_norm_g, "rwkv_mu": dmu, "rwkv_w0": dw0, "rwkv_a0": da0,
        "rwkv_k_k": dk_k, "rwkv_k_a": dk_a, "rwkv_r_k": dr_k, "rwkv_ln_w": dln_w, "rwkv_ln_b": dln_b,
        "norm_x_g": dg_x, "norm_mem_g": dg_mem, "norm_ffn_g": dg_ffn, "final_norm_g": dg_final}

    sharded = ("w_in", "ssd_conv_w", "rwkv_w2", "rwkv_a2", "rwkv_g2", "w_out", "xattn_wq", "xattn_wk", "xattn_wv",
               "xattn_wo", "ffn_w1", "ffn_w2")
    received = dict(zip(sharded, scatter_many("scatter_grads", [grads[n] for n in sharded])))
    (small_all,) = all_gather_many("gather_small_grads", [_pack_small(small_grads)])

    out_g, out_d, out_m, out_v = {}, {}, {}, {}
    for n in sharded:
        shape = wts[n].shape
        two_d = lambda a: a.reshape(-1, shape[-1])
        res = adamw("adamw_" + n, received[n].reshape(N_DEV, -1, shape[-1]), two_d(wts[n]), two_d(mom_m[n]), two_d(mom_v[n]))
        out_g[n], out_d[n], out_m[n], out_v[n] = (r.reshape(shape) for r in res)
    res = adamw("adamw_small", small_all, _pack_small(wts), _pack_small(mom_m), _pack_small(mom_v))
    shapes = {n: wts[n].shape for n in _SMALL}
    for dst, packed in zip((out_g, out_d, out_m, out_v), res):
        dst.update(_unpack_small(packed, shapes))

    loss = lax.psum(loss_blk[0, 0], ("x", "y", "c"))
    return (loss, dx[None], *[out_g[n] for n in _WEIGHTS], *[out_d[n] for n in _WEIGHTS],
            *[out_m[n] for n in _WEIGHTS], *[out_v[n] for n in _WEIGHTS])
```

```python
import functools

import jax
import jax.numpy as jnp
from jax import lax
from jax.experimental import pallas as pl
from jax.experimental.pallas import tpu as pltpu

F32 = jnp.float32
BF16 = jnp.bfloat16
HIGHEST = lax.Precision.HIGHEST

N_DEV = 8
D_MODEL = 2048
NORM_EPS = 1e-6
SSD_WIDTH = 1024
SSD_CONV_DIM = 1536
SSD_HEADS = 16
SSD_HEAD_DIM = 64
SSD_STATE = 128
SSD_CHUNK = 128
SSD_HEADS_PER_GROUP = 8
RWKV_WIDTH = 1024
RWKV_HEADS = 16
RWKV_HEAD_DIM = 64
RWKV_LN_EPS = 64e-5
RWKV_CHUNK = 64
RWKV_HEADS_PER_STEP = 16
XATTN_HEADS = 4
XATTN_HEAD_DIM = 512
D_FF = 8192
LANES = 128
SUBLANES = 8
VMEM_LIMIT = 56 * 1024 * 1024

ADAM_LR = 0.001
ADAM_B1 = 0.9
ADAM_B2 = 0.999
ADAM_EPS = 1e-08
ADAM_WD = 0.01
ADAM_STEP = 10

_DN = {"nn": ((1,), (0,)), "nt": ((1,), (1,)), "tn": ((0,), (0,))}


def _dg(a, b, mode, precision=None):
    (ca,), (cb,) = _DN[mode]
    dn = (((ca + 1,), (cb + 1,)), ((0,), (0,))) if a.ndim == 3 else (((ca,), (cb,)), ((), ()))
    return lax.dot_general(a, b, dn, precision=precision, preferred_element_type=F32)


@functools.partial(jax.custom_vjp, nondiff_argnums=(2,))
def bdot(a, b, mode):
    return _dg(a.astype(BF16), b.astype(BF16), mode)


def _bdot_fwd(a, b, mode):
    return bdot(a, b, mode), (a, b)


def _bdot_bwd(mode, res, g):
    a, b = res
    ab, bb, gb = a.astype(BF16), b.astype(BF16), g.astype(BF16)
    if mode == "nn":
        da, db = _dg(gb, bb, "nt"), _dg(ab, gb, "tn")
    elif mode == "nt":
        da, db = _dg(gb, bb, "nn"), _dg(gb, ab, "tn")
    else:
        da, db = _dg(bb, gb, "nt"), _dg(ab, gb, "nn")
    return da.astype(a.dtype), db.astype(b.dtype)


bdot.defvjp(_bdot_fwd, _bdot_bwd)


def fdot(a, b, mode):
    return _dg(a, b, mode, precision=HIGHEST)


def _split3(x):
    hi = x.astype(BF16)
    r1 = x - hi.astype(F32)
    mid = r1.astype(BF16)
    lo = (r1 - mid.astype(F32)).astype(BF16)
    return hi, mid, lo


def _dot01(x, m01):
    hi, mid, lo = _split3(x)
    return _dg(hi, m01, "nn") + _dg(mid, m01, "nn") + _dg(lo, m01, "nn")


def _exact_dot_impl(a, b, mode, exact):
    if exact == "a":
        ae = a.astype(BF16)
        return sum(_dg(ae, part, mode) for part in _split3(b))
    be = b.astype(BF16)
    return sum(_dg(part, be, mode) for part in _split3(a))


@functools.partial(jax.custom_vjp, nondiff_argnums=(2, 3))
def exact_dot(a, b, mode, exact):
    return _exact_dot_impl(a, b, mode, exact)


def _exact_dot_fwd(a, b, mode, exact):
    return _exact_dot_impl(a, b, mode, exact), (a, b)


def _exact_dot_bwd(mode, exact, res, g):
    a, b = res
    if exact == "a":
        db = {"nn": lambda: _exact_dot_impl(a, g, "tn", "a"), "nt": lambda: _exact_dot_impl(g, a, "tn", "b"),
              "tn": lambda: _exact_dot_impl(a, g, "nn", "a")}[mode]()
        return jnp.zeros_like(a), db
    da = {"nn": lambda: _exact_dot_impl(g, b, "nt", "b"), "nt": lambda: _exact_dot_impl(g, b, "nn", "b"),
          "tn": lambda: _exact_dot_impl(b, g, "nt", "a")}[mode]()
    return da, jnp.zeros_like(b)


exact_dot.defvjp(_exact_dot_fwd, _exact_dot_bwd)


def _head_indicator(width, heads, transpose):
    hd = width // heads
    shape = (LANES, width) if transpose else (width, LANES)
    lane = lax.broadcasted_iota(jnp.int32, shape, 1 if not transpose else 0)
    pos = lax.broadcasted_iota(jnp.int32, shape, 0 if not transpose else 1)
    return ((pos >= lane * hd) & (pos < lane * hd + hd)).astype(BF16)


@jax.custom_vjp
def head_sum(x):
    w = x.shape[-1]
    e = _head_indicator(w, w // RWKV_HEAD_DIM, False)
    et = _head_indicator(w, w // RWKV_HEAD_DIM, True)
    return _dot01(_dot01(x, e), et)


head_sum.defvjp(lambda x: (head_sum(x), None), lambda _, g: (head_sum(g),))


def rmsnorm_fn(x, g):
    y = x * lax.rsqrt(jnp.mean(x * x, axis=-1, keepdims=True) + NORM_EPS)
    return ((y * g).astype(BF16),)


def cast_fn(x):
    return (x.astype(BF16),)


def relu2_fn(a):
    return (jnp.square(jnp.maximum(a, 0.0)).astype(BF16),)


def ssd_pre_fn(xbc, xbc1, xbc2, xbc3, dt_raw, conv_w, conv_b, dt_bias):
    c = conv_w[3:4] * xbc + conv_w[2:3] * xbc1 + conv_w[1:2] * xbc2 + conv_w[0:1] * xbc3 + conv_b
    act = c * jax.nn.sigmoid(c)
    dt = jax.nn.softplus(dt_raw + dt_bias)
    return act[:, :SSD_WIDTH], act[:, SSD_WIDTH:SSD_WIDTH + 256], act[:, SSD_WIDTH + 256:], dt


def ssd_post_fn(yscan, z, norm_g):
    y = yscan * (z * jax.nn.sigmoid(z))
    half = SSD_WIDTH // 2
    parts = []
    for g in range(2):
        yg = y[:, g * half:(g + 1) * half]
        parts.append(yg * lax.rsqrt(jnp.mean(yg * yg, axis=-1, keepdims=True) + NORM_EPS))
    return ((jnp.concatenate(parts, axis=-1) * norm_g).astype(BF16),)


def rwkv_pre_fn(rkv, rkv_p, pg, pg_p, pwa, pwa_p, mu_rkv, mu_pg, mu_pwa, w0, w2p, a0, a2p, g2, k_k, k_a):
    w = RWKV_WIDTH
    rkv = rkv + (rkv_p - rkv) * mu_rkv
    pg = pg + (pg_p - pg) * mu_pg
    pwa = pwa + (pwa_p - pwa) * mu_pwa
    r, k, v = rkv[:, :w], rkv[:, w:2 * w], rkv[:, 2 * w:]
    pw, pa = pwa[:, :LANES], pwa[:, LANES:]
    w_log = -jax.nn.softplus(-(w0 + bdot(jnp.tanh(pw), w2p, "nn"))) - 0.5
    lw = -jnp.exp(w_log)
    iclr = jax.nn.sigmoid(a0 + bdot(pa, a2p, "nn"))
    gate = bdot(jax.nn.sigmoid(pg), g2, "nn")
    kk = k * k_k
    kap = kk / jnp.maximum(jnp.sqrt(head_sum(kk * kk)), 1e-12)
    k_mod = k * (1.0 + (iclr - 1.0) * k_a)
    return r, lw, k_mod, v, kap, kap * iclr, gate


def rwkv_post_fn(ys, r, k_mod, v, gate, ln_w, ln_b, r_k):
    inv_n = 1.0 / RWKV_HEAD_DIM
    mean = head_sum(ys) * inv_n
    yc = ys - mean
    var = head_sum(yc * yc) * inv_n
    yn = yc * lax.rsqrt(var + RWKV_LN_EPS) * ln_w + ln_b
    bonus = head_sum(r * k_mod * r_k) * v
    return (((yn + bonus) * gate).astype(BF16),)


def attn_fn(q, kx, vx):
    outs = []
    for h in range(XATTN_HEADS):
        sl = slice(h * XATTN_HEAD_DIM, (h + 1) * XATTN_HEAD_DIM)
        s = bdot(q[:, sl], kx[:, sl], "nt") * (XATTN_HEAD_DIM ** -0.5)
        s = s - jnp.max(s, axis=-1, keepdims=True)
        p = jnp.exp(s)
        p = p / jnp.sum(p, axis=-1, keepdims=True)
        outs.append(bdot(p, vx[:, sl], "nn"))
    return (jnp.concatenate(outs, axis=-1).astype(BF16),)


def loss_fn(x, tgt, g):
    y = x * lax.rsqrt(jnp.mean(x * x, axis=-1, keepdims=True) + NORM_EPS) * g
    err = jnp.square(y - tgt)
    return 0.5 * jnp.sum(jnp.mean(err, axis=-1, keepdims=True), axis=0, keepdims=True)


def _tri_masks(n):
    row = lax.broadcasted_iota(jnp.int32, (n, n), 0)
    col = lax.broadcasted_iota(jnp.int32, (n, n), 1)
    return col <= row, col < row, row == col


def rwkv_chunk_fn(st0, r, lw, k, v, kap, b):
    h, c = r.shape[0], r.shape[1]
    incl, strict, diag = _tri_masks(c)
    cum = exact_dot(jnp.broadcast_to(incl.astype(F32), (h, c, c)), lw, "nn", "a")
    g_in = jnp.exp(cum)
    g_prev = jnp.exp(cum - lw)
    g_inv = jnp.exp(-cum)
    g_end = jnp.exp(cum[:, c - 1:c, :] - cum)
    kap_t, k_t, b_t, r_t = kap * g_prev, k * g_inv, b * g_inv, r * g_in
    a_ub = jnp.where(strict, bdot(kap_t, b_t, "nt"), 0.0)
    a_vk = jnp.where(strict, bdot(kap_t, k_t, "nt"), 0.0)
    rhs = -(bdot(kap_t, st0, "nn") + bdot(a_vk, v, "nn"))
    eye = diag.astype(F32)
    m = -a_ub
    inv = eye + m
    n = 1
    while n * 2 < c:
        m = bdot(m, m, "nn")
        inv = bdot(inv, eye + m, "nn")
        n *= 2
    u = bdot(inv, rhs, "nn")
    y = (bdot(r_t, st0, "nn")
         + bdot(jnp.where(incl, bdot(r_t, k_t, "nt"), 0.0), v, "nn")
         + bdot(jnp.where(incl, bdot(r_t, b_t, "nt"), 0.0), u, "nn"))
    g_full = jnp.exp(exact_dot(lw, jnp.ones((h, c, st0.shape[2]), F32), "tn", "b"))
    st1 = g_full * st0 + bdot(k * g_end, v, "tn") + bdot(b * g_end, u, "tn")
    return y, st1


def ssd_chunk_fn(group, h0, xs, bm, cm, dt, a_log, d_skip):
    q = xs.shape[0]
    causal, _, _ = _tri_masks(q)
    a_row = -jnp.exp(a_log)
    cs_all = exact_dot(causal.astype(F32), dt * a_row, "nn", "a")
    cs_t = cs_all.T
    cb = bdot(cm, bm, "nt")
    ys, hs = [], []
    p = SSD_HEAD_DIM
    for e in range(SSD_HEADS_PER_GROUP):
        hl = group * SSD_HEADS_PER_GROUP + e
        cs = cs_all[:, hl:hl + 1]
        seg = cs - cs_t[hl:hl + 1, :]
        lmat = jnp.where(causal, jnp.exp(jnp.where(causal, seg, 0.0)), 0.0)
        x_h = xs[:, e * p:(e + 1) * p]
        xdt = x_h * dt[:, hl:hl + 1]
        h0_e = h0[e * p:(e + 1) * p, :]
        cl = cs[q - 1:q, :]
        y = bdot(cb * lmat, xdt, "nn") + bdot(cm, h0_e, "nt") * jnp.exp(cs) + x_h * d_skip[:, hl:hl + 1]
        hs.append(h0_e * jnp.exp(cl) + bdot(xdt * jnp.exp(cl - cs), bm, "tn"))
        ys.append(y)
    return jnp.concatenate(ys, axis=-1), jnp.concatenate(hs, axis=0)


class Rows:
    def __init__(self, arr, w=None, cb=0, shifts=()):
        self.arr, self.w, self.cb, self.shifts = arr, (arr.shape[1] if w is None else w), cb, tuple(shifts)


def _as_rows(x):
    return x if isinstance(x, Rows) else Rows(x)


def _shift_down(x, halo, k):
    rolled = pltpu.roll(x, k, 0)
    first = rolled[0:SUBLANES]
    rid = lax.broadcasted_iota(jnp.int32, first.shape, 0)
    patched = jnp.where(rid < k, pltpu.roll(halo, k, 0), first)
    return jnp.concatenate([patched, rolled[SUBLANES:]], axis=0)


def _shift_up(g, carry, k):
    tm = g.shape[0]
    rolled = pltpu.roll(g, tm - k, 0)
    last = rolled[tm - SUBLANES:]
    rid = lax.broadcasted_iota(jnp.int32, last.shape, 0)
    patched = jnp.where(rid >= SUBLANES - k, pltpu.roll(carry, SUBLANES - k, 0), last)
    return jnp.concatenate([rolled[:tm - SUBLANES], patched], axis=0)


def _params():
    return pltpu.CompilerParams(vmem_limit_bytes=VMEM_LIMIT)


def _load_rows(refs, pos, rins, first_block):
    vals = []
    for r in rins:
        x = refs[pos][...].astype(F32) if refs[pos].dtype != F32 else refs[pos][...]
        pos += 1
        vals.append(x)
        if r.shifts:
            halo = refs[pos][...]
            pos += 1
            halo = jnp.where(first_block, jnp.zeros_like(halo), halo)
            for k in r.shifts:
                vals.append(_shift_down(x, halo, k))
    return vals, pos


def _row_specs(rins, tm, blk):
    specs, args = [], []
    for r in rins:
        specs.append(pl.BlockSpec((tm, r.w), lambda i, cb=r.cb: (blk(i), cb)))
        args.append(r.arr)
        if r.shifts:
            per = tm // SUBLANES
            specs.append(pl.BlockSpec((SUBLANES, r.w), lambda i, cb=r.cb: (jnp.maximum(blk(i) * per - 1, 0), cb)))
            args.append(r.arr)
    return specs, args


def rowwise_fwd(name, fn, rins, params, outs, tm):
    rins = [_as_rows(r) for r in rins]
    t = rins[0].arr.shape[0]
    tm = min(tm, t)
    nb = t // tm
    specs, args = _row_specs(rins, tm, lambda i: i)
    for p in params:
        specs.append(pl.BlockSpec(p.shape, lambda i: (0, 0)))
        args.append(p)
    n_in = len(args)

    def body(*refs):
        vals, pos = _load_rows(refs, 0, rins, pl.program_id(0) == 0)
        pv = [refs[pos + j][...] for j in range(len(params))]
        res = fn(*vals, *pv)
        for o_ref, o in zip(refs[n_in:], res):
            o_ref[...] = o.astype(o_ref.dtype)

    return pl.pallas_call(
        body, name=name, grid=(nb,), in_specs=specs,
        out_specs=[pl.BlockSpec((tm, w), lambda i: (i, 0)) for w, _ in outs],
        out_shape=[jax.ShapeDtypeStruct((t, w), dt) for w, dt in outs],
        compiler_params=_params(),
    )(*args)


def rowwise_bwd(name, fn, rins, params, cts, tm, grad_dtypes, row_add=None):
    rins = [_as_rows(r) for r in rins]
    cts = [[_as_rows(c) for c in lst] for lst in cts]
    row_add = [_as_rows(a) for a in (row_add or [])]
    t = rins[0].arr.shape[0]
    tm = min(tm, t)
    nb = t // tm
    rev = lambda i: nb - 1 - i
    specs, args = _row_specs(rins, tm, rev)
    for p in params:
        specs.append(pl.BlockSpec(p.shape, lambda i: (0, 0)))
        args.append(p)
    flat_cts = [c for lst in cts for c in lst] + row_add
    for c in flat_cts:
        specs.append(pl.BlockSpec((tm, c.w), lambda i, cb=c.cb: (rev(i), cb)))
        args.append(c.arr)
    n_in = len(args)
    want = [i for i, d in enumerate(grad_dtypes) if d is not None]
    out_specs = [pl.BlockSpec((tm, rins[i].w), lambda i_: (rev(i_), 0)) for i in want]
    out_shape = [jax.ShapeDtypeStruct((t, rins[i].w), grad_dtypes[i]) for i in want]
    out_specs += [pl.BlockSpec(p.shape, lambda i: (0, 0)) for p in params]
    out_shape += [jax.ShapeDtypeStruct(p.shape, F32) for p in params]
    n_out = len(out_shape)
    scratch = [pltpu.VMEM((SUBLANES, r.w), F32) for r in rins for _ in r.shifts]

    def body(*refs):
        i = pl.program_id(0)
        vals, pos = _load_rows(refs, 0, rins, rev(i) == 0)
        pv = [refs[pos + j][...] for j in range(len(params))]
        pos += len(params)
        outs, vjp = jax.vjp(fn, *vals, *pv)
        ct_vals = []
        for o, lst in zip(outs, cts):
            acc = None
            for _ in lst:
                cv = refs[pos][...].astype(F32)
                pos += 1
                acc = cv if acc is None else acc + cv
            ct_vals.append(acc.astype(o.dtype))
        adds = [refs[pos + j][...].astype(F32) for j in range(len(row_add))]
        grads = vjp(tuple(ct_vals))
        out_refs = refs[n_in:n_in + n_out]
        carry_refs = refs[n_in + n_out:]

        @pl.when(i == 0)
        def _():
            for cr in carry_refs:
                cr[...] = jnp.zeros_like(cr)
            for pr in out_refs[len(want):]:
                pr[...] = jnp.zeros_like(pr)

        gi, ci, oi = 0, 0, 0
        for idx, r in enumerate(rins):
            d = grads[gi]
            gi += 1
            for k in r.shifts:
                dk = grads[gi]
                gi += 1
                d = d + _shift_up(dk, carry_refs[ci][...], k)
                carry_refs[ci][...] = dk[0:SUBLANES]
                ci += 1
            if idx == 0:
                for a in adds:
                    d = d + a
            if grad_dtypes[idx] is not None:
                out_refs[oi][...] = d.astype(out_refs[oi].dtype)
                oi += 1
        for pr, gp in zip(out_refs[len(want):], grads[gi:]):
            pr[...] += gp

    res = pl.pallas_call(
        body, name=name, grid=(nb,), in_specs=specs, out_specs=out_specs, out_shape=out_shape,
        scratch_shapes=scratch, compiler_params=_params(),
    )(*args)
    return res[:len(want)], res[len(want):]


def _pick(n, pref):
    for c in pref:
        if n % c == 0:
            return c
    return n


def mm(name, a, b, mode, out_dtype=F32, res=None, b_slabs=None, out_slabs=None):
    if mode == "tn":
        k_dim, m_dim = a.shape
    else:
        m_dim, k_dim = a.shape
    if b_slabs:
        n_dim = b.shape[0] * b.shape[2] if mode == "nn" else b.shape[1]
    else:
        n_dim = b.shape[0] if mode == "nt" else b.shape[1]
    n_slabs = out_slabs or (b_slabs if (b_slabs and mode == "nn") else 1)
    k_slabs = b_slabs if (b_slabs and mode == "nt") else 1
    tm = _pick(m_dim, (512, 256, 128))
    tn = _pick(n_dim // n_slabs, (1024, 768, 512, 384, 256, 128))
    tk = _pick(k_dim // k_slabs, (512, 256, 128))
    nji = n_dim // n_slabs // tn
    nki = k_dim // k_slabs // tk
    nblk = lambda js, j: js * nji + j
    kblk = lambda ks, k: ks * nki + k
    if mode == "tn":
        a_spec = pl.BlockSpec((tk, tm), lambda i, js, j, ks, k: (kblk(ks, k), i))
    else:
        a_spec = pl.BlockSpec((tm, tk), lambda i, js, j, ks, k: (i, kblk(ks, k)))
    if b_slabs and mode == "nn":
        b_spec = pl.BlockSpec((None, tk, tn), lambda i, js, j, ks, k: (js, k, j))
    elif b_slabs and mode == "nt":
        b_spec = pl.BlockSpec((None, tn, tk), lambda i, js, j, ks, k: (ks, nblk(js, j), k))
    elif mode == "nt":
        b_spec = pl.BlockSpec((tn, tk), lambda i, js, j, ks, k: (nblk(js, j), kblk(ks, k)))
    else:
        b_spec = pl.BlockSpec((tk, tn), lambda i, js, j, ks, k: (kblk(ks, k), nblk(js, j)))
    specs, args = [a_spec, b_spec], [a, b]
    if res is not None:
        specs.append(pl.BlockSpec((tm, tn), lambda i, js, j, ks, k: (i, nblk(js, j))))
        args.append(res)
    if out_slabs:
        o_spec = pl.BlockSpec((None, tm, tn), lambda i, js, j, ks, k: (js, i, j))
        o_shape = jax.ShapeDtypeStruct((out_slabs, m_dim, n_dim // out_slabs), out_dtype)
    else:
        o_spec = pl.BlockSpec((tm, tn), lambda i, js, j, ks, k: (i, nblk(js, j)))
        o_shape = jax.ShapeDtypeStruct((m_dim, n_dim), out_dtype)

    def body(*refs):
        a_ref, b_ref = refs[0], refs[1]
        o_ref, acc_ref = refs[-2], refs[-1]
        ks, kk = pl.program_id(3), pl.program_id(4)

        @pl.when((ks == 0) & (kk == 0))
        def _():
            acc_ref[...] = jnp.zeros_like(acc_ref)

        acc_ref[...] += _dg(a_ref[...].astype(BF16), b_ref[...].astype(BF16), mode)

        @pl.when((ks == k_slabs - 1) & (kk == nki - 1))
        def _():
            out = acc_ref[...]
            if res is not None:
                out = out + refs[2][...].astype(F32)
            o_ref[...] = out.astype(o_ref.dtype)

    return pl.pallas_call(
        body, name=name, grid=(m_dim // tm, n_slabs, nji, k_slabs, nki), in_specs=specs, out_specs=o_spec,
        out_shape=o_shape, scratch_shapes=[pltpu.VMEM((tm, tn), F32)],
        compiler_params=pltpu.CompilerParams(
            dimension_semantics=("parallel", "parallel", "parallel", "arbitrary", "arbitrary"),
            vmem_limit_bytes=VMEM_LIMIT),
    )(*args)


def _stack_lanes(x, n):
    w = x.shape[1] // n
    return jnp.stack([x[:, i * w:(i + 1) * w] for i in range(n)])


def _stack_rows(x, n):
    w = x.shape[0] // n
    return jnp.stack([x[i * w:(i + 1) * w, :] for i in range(n)])


def _with_exchange(body_fn, n_in, n_out, n_scratch, ex, first_step, last_step):
    if ex is None:
        return body_fn

    def body(*refs):
        n = ex.n
        ins, ex_ins = refs[:n_in], refs[n_in:n_in + n]
        outs = refs[n_in + n:n_in + n + n_out]
        ex_outs = refs[n_in + n + n_out:n_in + 2 * n + n_out]
        scratch = refs[n_in + 2 * n + n_out:n_in + 2 * n + n_out + n_scratch]
        sems = refs[n_in + 2 * n + n_out + n_scratch:]
        pl.when(first_step())(lambda: ex.start(ex_ins, ex_outs, sems))
        body_fn(*ins, *outs, *scratch)
        pl.when(last_step())(lambda: ex.finish(ex_ins, ex_outs, sems))

    return body


def _grid2_ends(n0, n1):
    first = lambda: (pl.program_id(0) == 0) & (pl.program_id(1) == 0)
    last = lambda: (pl.program_id(0) == n0 - 1) & (pl.program_id(1) == n1 - 1)
    return first, last


def _ex_parts(ex):
    if ex is None:
        return [], [], [], [], []
    return ex.specs, list(ex.xs), ex.specs, ex.out_shape, ex.scratch


def rwkv_scan_fwd(r, lw, k, v, kap, b, ex=None):
    t = r.shape[0]
    c, hps, hd = min(RWKV_CHUNK, t), RWKV_HEADS_PER_STEP, RWKV_HEAD_DIM
    nc, ng, wl = t // c, RWKV_HEADS // hps, hps * hd
    spec = pl.BlockSpec((c, wl), lambda g, ci: (ci, g))

    def body(r_ref, lw_ref, k_ref, v_ref, kap_ref, b_ref, y_ref, ck_ref, st_ref):
        @pl.when(pl.program_id(1) == 0)
        def _():
            st_ref[...] = jnp.zeros_like(st_ref)

        st = st_ref[...]
        ck_ref[...] = st
        ins = [x[...] for x in (r_ref, lw_ref, k_ref, v_ref, kap_ref, b_ref)]
        y, st1 = rwkv_chunk_fn(_stack_rows(st, hps), *[_stack_lanes(x, hps) for x in ins])
        y_ref[...] = jnp.concatenate([y[h] for h in range(hps)], axis=-1)
        st_ref[...] = jnp.concatenate([st1[h] for h in range(hps)], axis=0)

    ex_in_specs, ex_args, ex_out_specs, ex_out_shape, ex_scratch = _ex_parts(ex)
    res = pl.pallas_call(
        _with_exchange(body, 6, 2, 1, ex, *_grid2_ends(ng, nc)), name="rwkv_scan_fwd", grid=(ng, nc),
        in_specs=[spec] * 6 + ex_in_specs,
        out_specs=[spec, pl.BlockSpec((None, wl, hd), lambda g, ci: (ci, g, 0))] + ex_out_specs,
        out_shape=[jax.ShapeDtypeStruct((t, RWKV_WIDTH), F32),
                   jax.ShapeDtypeStruct((nc, RWKV_WIDTH, hd), F32)] + ex_out_shape,
        scratch_shapes=[pltpu.VMEM((wl, hd), F32)] + ex_scratch,
        compiler_params=pltpu.CompilerParams(vmem_limit_bytes=VMEM_LIMIT, has_side_effects=ex is not None),
    )(r, lw, k, v, kap, b, *ex_args)
    return res[0], res[1], res[2:]


def rwkv_scan_bwd(r, lw, k, v, kap, b, ck, dy, ex=None):
    t = r.shape[0]
    c, hps, hd = min(RWKV_CHUNK, t), RWKV_HEADS_PER_STEP, RWKV_HEAD_DIM
    nc, ng, wl = t // c, RWKV_HEADS // hps, hps * hd
    spec = pl.BlockSpec((c, wl), lambda g, ci: (nc - 1 - ci, g))

    def body(r_ref, lw_ref, k_ref, v_ref, kap_ref, b_ref, ck_ref, dy_ref, *rest):
        out_refs, dst_ref = rest[:6], rest[6]

        @pl.when(pl.program_id(1) == 0)
        def _():
            dst_ref[...] = jnp.zeros_like(dst_ref)

        ins = [x[...] for x in (r_ref, lw_ref, k_ref, v_ref, kap_ref, b_ref)]
        dyv, ck, dst = dy_ref[...], ck_ref[...], dst_ref[...]
        _, vjp = jax.vjp(rwkv_chunk_fn, _stack_rows(ck, hps), *[_stack_lanes(x, hps) for x in ins])
        grads = vjp((_stack_lanes(dyv, hps), _stack_rows(dst, hps)))
        dst_ref[...] = jnp.concatenate([grads[0][h] for h in range(hps)], axis=0)
        for j in range(6):
            out_refs[j][...] = jnp.concatenate([grads[1 + j][h] for h in range(hps)], axis=-1)

    ex_in_specs, ex_args, ex_out_specs, ex_out_shape, ex_scratch = _ex_parts(ex)
    res = pl.pallas_call(
        _with_exchange(body, 8, 6, 1, ex, *_grid2_ends(ng, nc)), name="rwkv_scan_bwd", grid=(ng, nc),
        in_specs=[spec] * 6 + [pl.BlockSpec((None, wl, hd), lambda g, ci: (nc - 1 - ci, g, 0)), spec] + ex_in_specs,
        out_specs=[spec] * 6 + ex_out_specs,
        out_shape=[jax.ShapeDtypeStruct((t, RWKV_WIDTH), F32)] * 6 + ex_out_shape,
        scratch_shapes=[pltpu.VMEM((wl, hd), F32)] + ex_scratch,
        compiler_params=pltpu.CompilerParams(vmem_limit_bytes=VMEM_LIMIT, has_side_effects=ex is not None),
    )(r, lw, k, v, kap, b, ck, dy, *ex_args)
    return res[:6], res[6:]


def _ssd_specs(q, blk):
    gw = SSD_WIDTH // 2
    return [pl.BlockSpec((q, gw), lambda g, ci: (blk(ci), g)),
            pl.BlockSpec((q, SSD_STATE), lambda g, ci: (blk(ci), g)),
            pl.BlockSpec((q, SSD_STATE), lambda g, ci: (blk(ci), g)),
            pl.BlockSpec((q, LANES), lambda g, ci: (blk(ci), 0)),
            pl.BlockSpec((1, LANES), lambda g, ci: (0, 0)),
            pl.BlockSpec((1, LANES), lambda g, ci: (0, 0))]


def ssd_scan_fwd(xs, bm, cm, dt, a_log, d_skip):
    t = xs.shape[0]
    q = min(SSD_CHUNK, t)
    nc, gw = t // q, SSD_WIDTH // 2

    def body(xs_ref, bm_ref, cm_ref, dt_ref, al_ref, d_ref, y_ref, ck_ref, h_ref):
        @pl.when(pl.program_id(1) == 0)
        def _():
            h_ref[...] = jnp.zeros_like(h_ref)

        ck_ref[...] = h_ref[...]
        args = (h_ref[...], xs_ref[...], bm_ref[...], cm_ref[...], dt_ref[...], al_ref[...], d_ref[...])
        g = pl.program_id(0)

        @pl.when(g == 0)
        def _():
            y, h1 = ssd_chunk_fn(0, *args)
            y_ref[...] = y
            h_ref[...] = h1

        @pl.when(g == 1)
        def _():
            y, h1 = ssd_chunk_fn(1, *args)
            y_ref[...] = y
            h_ref[...] = h1

    return pl.pallas_call(
        body, name="ssd_scan_fwd", grid=(2, nc), in_specs=_ssd_specs(q, lambda ci: ci),
        out_specs=[pl.BlockSpec((q, gw), lambda g, ci: (ci, g)),
                   pl.BlockSpec((None, gw, SSD_STATE), lambda g, ci: (ci, g, 0))],
        out_shape=[jax.ShapeDtypeStruct((t, SSD_WIDTH), F32), jax.ShapeDtypeStruct((nc, SSD_WIDTH, SSD_STATE), F32)],
        scratch_shapes=[pltpu.VMEM((gw, SSD_STATE), F32)], compiler_params=_params(),
    )(xs, bm, cm, dt, a_log, d_skip)


def ssd_scan_bwd(xs, bm, cm, dt, a_log, d_skip, ck, dy):
    t = xs.shape[0]
    q = min(SSD_CHUNK, t)
    nc, gw = t // q, SSD_WIDTH // 2
    rev = lambda ci: nc - 1 - ci

    def body(xs_ref, bm_ref, cm_ref, dt_ref, al_ref, d_ref, ck_ref, dy_ref,
             dxs_ref, dbm_ref, dcm_ref, ddt_ref, dal_ref, dd_ref, dh_ref):
        g, ci = pl.program_id(0), pl.program_id(1)

        @pl.when(ci == 0)
        def _():
            dh_ref[...] = jnp.zeros_like(dh_ref)

        @pl.when((ci == 0) & (g == 0))
        def _():
            dal_ref[...] = jnp.zeros_like(dal_ref)
            dd_ref[...] = jnp.zeros_like(dd_ref)

        args = (ck_ref[...], xs_ref[...], bm_ref[...], cm_ref[...], dt_ref[...], al_ref[...], d_ref[...])

        def run(group):
            _, vjp = jax.vjp(functools.partial(ssd_chunk_fn, group), *args)
            dh0, dxs, dbm, dcm, ddt, dal, dd = vjp((dy_ref[...], dh_ref[...]))
            dh_ref[...] = dh0
            dxs_ref[...] = dxs
            dbm_ref[...] = dbm
            dcm_ref[...] = dcm
            ddt_ref[...] = ddt
            dal_ref[...] += dal
            dd_ref[...] += dd

        pl.when(g == 0)(lambda: run(0))
        pl.when(g == 1)(lambda: run(1))

    in_specs = _ssd_specs(q, rev) + [pl.BlockSpec((None, gw, SSD_STATE), lambda g, ci: (rev(ci), g, 0)),
                                     pl.BlockSpec((q, gw), lambda g, ci: (rev(ci), g))]
    return pl.pallas_call(
        body, name="ssd_scan_bwd", grid=(2, nc), in_specs=in_specs,
        out_specs=[pl.BlockSpec((q, gw), lambda g, ci: (rev(ci), g)),
                   pl.BlockSpec((q, SSD_STATE), lambda g, ci: (rev(ci), g)),
                   pl.BlockSpec((q, SSD_STATE), lambda g, ci: (rev(ci), g)),
                   pl.BlockSpec((None, q, LANES), lambda g, ci: (g, rev(ci), 0)),
                   pl.BlockSpec((1, LANES), lambda g, ci: (0, 0)),
                   pl.BlockSpec((1, LANES), lambda g, ci: (0, 0))],
        out_shape=[jax.ShapeDtypeStruct((t, SSD_WIDTH), F32), jax.ShapeDtypeStruct((t, 2 * SSD_STATE), F32),
                   jax.ShapeDtypeStruct((t, 2 * SSD_STATE), F32), jax.ShapeDtypeStruct((2, t, LANES), F32),
                   jax.ShapeDtypeStruct((1, LANES), F32), jax.ShapeDtypeStruct((1, LANES), F32)],
        scratch_shapes=[pltpu.VMEM((gw, SSD_STATE), F32)], compiler_params=_params(),
    )(xs, bm, cm, dt, a_log, d_skip, ck, dy)


def loss_and_grad(x, tgt, g, tm):
    t, d = x.shape
    tm = min(tm, t)
    nb = t // tm

    def body(x_ref, t_ref, g_ref, loss_ref, dx_ref, dg_ref):
        @pl.when(pl.program_id(0) == 0)
        def _():
            loss_ref[...] = jnp.zeros_like(loss_ref)
            dg_ref[...] = jnp.zeros_like(dg_ref)

        val, vjp = jax.vjp(loss_fn, x_ref[...], t_ref[...], g_ref[...])
        dx, _, dg = vjp(jnp.ones((1, 1), F32))
        loss_ref[...] += jnp.broadcast_to(val, loss_ref.shape)
        dx_ref[...] = dx
        dg_ref[...] += dg

    row = pl.BlockSpec((tm, d), lambda i: (i, 0))
    one = pl.BlockSpec((1, d), lambda i: (0, 0))
    return pl.pallas_call(
        body, name="loss_and_grad", grid=(nb,), in_specs=[row, row, one],
        out_specs=[pl.BlockSpec((SUBLANES, LANES), lambda i: (0, 0)), row, one],
        out_shape=[jax.ShapeDtypeStruct((SUBLANES, LANES), F32), jax.ShapeDtypeStruct((t, d), F32),
                   jax.ShapeDtypeStruct((1, d), F32)],
        compiler_params=_params(),
    )(x, tgt, g)


def adamw(name, recv, w, m, v):
    rows, cols = w.shape
    recv_block_bytes = 4 * 1024 * 1024
    tm = _pick(rows, [c for c in (256, 128, 64, 32, 16, 8) if N_DEV * c * cols * 4 <= recv_block_bytes])
    c1 = 1.0 / (1.0 - ADAM_B1 ** ADAM_STEP)
    c2 = 1.0 / (1.0 - ADAM_B2 ** ADAM_STEP)

    def body(recv_ref, w_ref, m_ref, v_ref, g_ref, d_ref, nm_ref, nv_ref):
        g = recv_ref[0].astype(F32)
        for p in range(1, N_DEV):
            g = g + recv_ref[p].astype(F32)
        nm = ADAM_B1 * m_ref[...] + (1.0 - ADAM_B1) * g
        nv = ADAM_B2 * v_ref[...] + (1.0 - ADAM_B2) * jnp.square(g)
        g_ref[...] = g
        nm_ref[...] = nm
        nv_ref[...] = nv
        d_ref[...] = -ADAM_LR * ((nm * c1) / (jnp.sqrt(nv * c2) + ADAM_EPS) + ADAM_WD * w_ref[...])

    blk = pl.BlockSpec((tm, cols), lambda i: (i, 0))
    return pl.pallas_call(
        body, name=name, grid=(rows // tm,),
        in_specs=[pl.BlockSpec((N_DEV, tm, cols), lambda i: (0, i, 0)), blk, blk, blk],
        out_specs=[blk] * 4, out_shape=[jax.ShapeDtypeStruct((rows, cols), F32)] * 4,
        compiler_params=_params(),
    )(recv, w, m, v)


def _mesh_pos():
    return lax.axis_index("x"), lax.axis_index("y"), lax.axis_index("c")


def _peer(pos, mask):
    x, y, c = pos
    return (1 - x if mask & 4 else x, 1 - y if mask & 2 else y, 1 - c if mask & 1 else c)


def _linear(pos):
    return 4 * pos[0] + 2 * pos[1] + pos[2]


class Exchange:
    def __init__(self, xs, scatter):
        self.xs, self.scatter, self.n = list(xs), scatter, len(xs)
        self.out_shape = [jax.ShapeDtypeStruct(x.shape if scatter else (N_DEV,) + x.shape, x.dtype) for x in xs]
        self.specs = [pl.BlockSpec(memory_space=pl.ANY)] * self.n
        self.scratch = [pltpu.SemaphoreType.DMA((self.n, N_DEV - 1)), pltpu.SemaphoreType.DMA((self.n, N_DEV - 1)),
                        pltpu.SemaphoreType.DMA((self.n,))]

    def _copies(self, ins, outs, sems, landing):
        send_sems, recv_sems, local_sems = sems
        me = _mesh_pos()
        me_lin = _linear(me)
        local, remote = [], []
        for ti in range(self.n):
            src_mine = ins[ti].at[me_lin] if self.scatter else ins[ti]
            local.append(pltpu.make_async_copy(src_mine, outs[ti].at[me_lin], local_sems.at[ti]))
            for mask in range(1, N_DEV):
                peer = _peer(me, mask)
                src = ins[ti].at[_linear(peer)] if self.scatter else ins[ti]
                dst = outs[ti].at[_linear(peer) if landing else me_lin]
                remote.append(pltpu.make_async_remote_copy(
                    src_ref=src, dst_ref=dst, send_sem=send_sems.at[ti, mask - 1], recv_sem=recv_sems.at[ti, mask - 1],
                    device_id=peer, device_id_type=pl.DeviceIdType.MESH))
        return local, remote

    def start(self, ins, outs, sems):
        local, remote = self._copies(ins, outs, sems, landing=False)
        for cp in local + remote:
            cp.start()

    def finish(self, ins, outs, sems):
        local, remote = self._copies(ins, outs, sems, landing=True)
        for cp in remote:
            cp.wait_recv()
        for cp in remote:
            cp.wait_send()
        for cp in local:
            cp.wait()


def exchange_call(name, xs, scatter):
    ex = Exchange(xs, scatter)
    n = ex.n

    def body(*refs):
        ins, outs, sems = refs[:n], refs[n:2 * n], refs[2 * n:]
        ex.start(ins, outs, sems)
        ex.finish(ins, outs, sems)

    return pl.pallas_call(
        body, name=name, in_specs=ex.specs, out_specs=ex.specs, out_shape=ex.out_shape, scratch_shapes=ex.scratch,
        compiler_params=pltpu.CompilerParams(has_side_effects=True),
    )(*xs)


def all_gather_many(name, xs):
    return exchange_call(name, xs, scatter=False)


def scatter_many(name, xs):
    return exchange_call(name, xs, scatter=True)


_Z = (0, 1024)
_XBC = (1024, 2560)
_DT = (2560, 2576)
_RKV = (2576, 5648)
_PW = (5648, 5744)
_PA = (5744, 5840)
_PG = (5840, 6096)
D_IN = 6096

_SMALL = ("norm_mix_g", "ssd_conv_b", "ssd_dt_bias", "ssd_a_log", "ssd_d", "ssd_norm_g", "rwkv_mu", "rwkv_w0",
          "rwkv_a0", "rwkv_k_k", "rwkv_k_a", "rwkv_r_k", "rwkv_ln_w", "rwkv_ln_b", "norm_x_g", "norm_mem_g",
          "norm_ffn_g", "final_norm_g")
_WEIGHTS = ("norm_mix_g", "w_in", "ssd_conv_w", "ssd_conv_b", "ssd_dt_bias", "ssd_a_log", "ssd_d", "ssd_norm_g",
            "rwkv_mu", "rwkv_w0", "rwkv_w2", "rwkv_a0", "rwkv_a2", "rwkv_g2", "rwkv_k_k", "rwkv_k_a", "rwkv_r_k",
            "rwkv_ln_w", "rwkv_ln_b", "w_out", "norm_x_g", "norm_mem_g", "xattn_wq", "xattn_wk", "xattn_wv",
            "xattn_wo", "norm_ffn_g", "ffn_w1", "ffn_w2", "final_norm_g")


def _pad_lanes(x, width=LANES):
    return jnp.pad(x, ((0, 0), (0, width - x.shape[1])))


def _pack_small(vals):
    flat = jnp.concatenate([vals[n].reshape(-1) for n in _SMALL])
    rows = -(-flat.shape[0] // (LANES * SUBLANES)) * SUBLANES
    return jnp.pad(flat, (0, rows * LANES - flat.shape[0])).reshape(rows, LANES)


def _unpack_small(packed, shapes):
    flat = packed.reshape(-1)
    out, pos = {}, 0
    for n in _SMALL:
        size = 1
        for s in shapes[n]:
            size *= s
        out[n] = flat[pos:pos + size].reshape(shapes[n])
        pos += size
    return out


def _cols(w, rng):
    return w[:, rng[0]:rng[1]]


def kernel(x, mem, norm_mix_g, w_in, ssd_conv_w, ssd_conv_b, ssd_dt_bias, ssd_a_log, ssd_d, ssd_norm_g, rwkv_mu, rwkv_w0, rwkv_w2, rwkv_a0, rwkv_a2, rwkv_g2, rwkv_k_k, rwkv_k_a, rwkv_r_k, rwkv_ln_w, rwkv_ln_b, w_out, norm_x_g, norm_mem_g, xattn_wq, xattn_wk, xattn_wv, xattn_wo, norm_ffn_g, ffn_w1, ffn_w2, final_norm_g, loss_target, m_norm_mix_g, m_w_in, m_ssd_conv_w, m_ssd_conv_b, m_ssd_dt_bias, m_ssd_a_log, m_ssd_d, m_ssd_norm_g, m_rwkv_mu, m_rwkv_w0, m_rwkv_w2, m_rwkv_a0, m_rwkv_a2, m_rwkv_g2, m_rwkv_k_k, m_rwkv_k_a, m_rwkv_r_k, m_rwkv_ln_w, m_rwkv_ln_b, m_w_out, m_norm_x_g, m_norm_mem_g, m_xattn_wq, m_xattn_wk, m_xattn_wv, m_xattn_wo, m_norm_ffn_g, m_ffn_w1, m_ffn_w2, m_final_norm_g, v_norm_mix_g, v_w_in, v_ssd_conv_w, v_ssd_conv_b, v_ssd_dt_bias, v_ssd_a_log, v_ssd_d, v_ssd_norm_g, v_rwkv_mu, v_rwkv_w0, v_rwkv_w2, v_rwkv_a0, v_rwkv_a2, v_rwkv_g2, v_rwkv_k_k, v_rwkv_k_a, v_rwkv_r_k, v_rwkv_ln_w, v_rwkv_ln_b, v_w_out, v_norm_x_g, v_norm_mem_g, v_xattn_wq, v_xattn_wk, v_xattn_wv, v_xattn_wo, v_norm_ffn_g, v_ffn_w1, v_ffn_w2, v_final_norm_g):
    given = dict(locals())
    wts = {n: given[n] for n in _WEIGHTS}
    mom_m = {n: given["m_" + n] for n in _WEIGHTS}
    mom_v = {n: given["v_" + n] for n in _WEIGHTS}
    d = D_MODEL
    xt, memt, tgt = x[0], mem[0], loss_target[0]
    tm = 256

    big = {"w_in": w_in[0], "w_out": w_out[0], "xattn_wq": xattn_wq[0], "xattn_wk": xattn_wk[0],
           "xattn_wv": xattn_wv[0], "xattn_wo": xattn_wo[0], "ffn_w1": ffn_w1[0], "ffn_w2": ffn_w2[0]}
    small_sh = {"ssd_conv_w": ssd_conv_w.reshape(4, -1), "rwkv_w2": rwkv_w2[0], "rwkv_a2": rwkv_a2[0],
                "rwkv_g2": rwkv_g2[0]}
    cast = {n: rowwise_fwd("cast_" + n, cast_fn, [a], [], [(a.shape[1], BF16)], 256)[0] for n, a in big.items()}
    gathered = all_gather_many("gather_weights", [cast["w_in"]] + list(small_sh.values()))
    g_big = {"w_in": gathered[0]}
    g_small = dict(zip(small_sh, gathered[1:]))
    late = [n for n in big if n != "w_in"]
    late_gather = Exchange([cast[n] for n in late], scatter=False)

    w_in_full = jnp.transpose(g_big["w_in"], (1, 0, 2)).reshape(d, D_IN)
    w_z, w_xbc, w_rkv, w_pg = (_cols(w_in_full, r) for r in (_Z, _XBC, _RKV, _PG))
    w_sm = jnp.concatenate([_pad_lanes(_cols(w_in_full, r)) for r in (_PW, _PA, _DT)], axis=1)
    unshard_cols = lambda g: jnp.transpose(g, (1, 0, 2)).reshape(g.shape[1], -1)
    conv_w_f = unshard_cols(g_small["ssd_conv_w"])
    pad_rows = lambda a: jnp.pad(a, ((0, LANES - a.shape[0]), (0, 0)))
    w2p, a2p = pad_rows(unshard_cols(g_small["rwkv_w2"])), pad_rows(unshard_cols(g_small["rwkv_a2"]))
    g2_f = unshard_cols(g_small["rwkv_g2"])

    mu = rwkv_mu
    mu_rkv, mu_pg = mu[:, :3072], mu[:, 3264:3520]
    mu_pwa = jnp.concatenate([_pad_lanes(mu[:, 3072:3168]), _pad_lanes(mu[:, 3168:3264])], axis=1)
    dt_bias_p, a_log_p, d_p = _pad_lanes(ssd_dt_bias), _pad_lanes(ssd_a_log), _pad_lanes(ssd_d)
    r_k_row = rwkv_r_k.reshape(1, RWKV_WIDTH)
    g_final = final_norm_g.reshape(1, d)

    (h1,) = rowwise_fwd("norm_mix", rmsnorm_fn, [xt], [norm_mix_g], [(d, BF16)], tm)
    u_z = mm("in_z", h1, w_z, "nn")
    u_xbc = mm("in_xbc", h1, w_xbc, "nn")
    u_rkv = mm("in_rkv", h1, w_rkv, "nn")
    u_pg = mm("in_pg", h1, w_pg, "nn")
    u_sm = mm("in_small", h1, w_sm, "nn")

    ssd_pre_rows = lambda: [Rows(u_xbc, shifts=(1, 2, 3)), Rows(u_sm, LANES, 2)]
    ssd_pre_params = [conv_w_f, ssd_conv_b, dt_bias_p]
    xs, bm, cm, dt = rowwise_fwd("ssd_pre", ssd_pre_fn, ssd_pre_rows(), ssd_pre_params,
                                 [(SSD_WIDTH, F32), (256, F32), (256, F32), (LANES, F32)], tm)
    y_scan, ssd_ck = ssd_scan_fwd(xs, bm, cm, dt, a_log_p, d_p)
    (y_ssd,) = rowwise_fwd("ssd_post", ssd_post_fn, [y_scan, u_z], [ssd_norm_g], [(SSD_WIDTH, BF16)], tm)

    rwkv_pre_rows = lambda: [Rows(u_rkv, shifts=(1,)), Rows(u_pg, shifts=(1,)), Rows(u_sm, 2 * LANES, 0, shifts=(1,))]
    rwkv_pre_params = [mu_rkv, mu_pg, mu_pwa, rwkv_w0, w2p, rwkv_a0, a2p, g2_f, rwkv_k_k, rwkv_k_a]
    r_, lw_, k_, v_, kap_, b_, gate_ = rowwise_fwd("rwkv_pre", rwkv_pre_fn, rwkv_pre_rows(), rwkv_pre_params,
                                                   [(RWKV_WIDTH, F32)] * 7, 128)
    ys_r, rwkv_ck, late_w = rwkv_scan_fwd(r_, lw_, k_, v_, kap_, b_, ex=late_gather)
    g_big.update(zip(late, late_w))
    w_out_f = g_big["w_out"].reshape(d, d)
    wq_f, wk_f, wv_f, wo_f = (g_big[n].reshape(d, d) for n in ("xattn_wq", "xattn_wk", "xattn_wv", "xattn_wo"))
    w1_s = g_big["ffn_w1"]
    w2_f = g_big["ffn_w2"].reshape(D_FF, d)
    rwkv_post_params = [rwkv_ln_w, rwkv_ln_b, r_k_row]
    (y_rwkv,) = rowwise_fwd("rwkv_post", rwkv_post_fn, [ys_r, r_, k_, v_, gate_], rwkv_post_params,
                            [(RWKV_WIDTH, BF16)], tm)
    ycat = jnp.concatenate([y_ssd, y_rwkv], axis=1)
    x1 = mm("out_proj", ycat, w_out_f, "nn", res=xt)

    (h2,) = rowwise_fwd("norm_x", rmsnorm_fn, [x1], [norm_x_g], [(d, BF16)], tm)
    (mn,) = rowwise_fwd("norm_mem", rmsnorm_fn, [memt], [norm_mem_g], [(d, BF16)], tm)
    q = mm("xattn_q", h2, wq_f, "nn", out_dtype=BF16)
    kx = mm("xattn_k", mn, wk_f, "nn")
    vx = mm("xattn_v", mn, wv_f, "nn")
    (o,) = rowwise_fwd("xattn", attn_fn, [q], [kx, vx], [(d, BF16)], tm)
    x2 = mm("xattn_o", o, wo_f, "nn", res=x1)

    (h3,) = rowwise_fwd("norm_ffn", rmsnorm_fn, [x2], [norm_ffn_g], [(d, BF16)], tm)
    a_ffn = mm("ffn_1", h3, w1_s, "nn", b_slabs=N_DEV)
    (hid,) = rowwise_fwd("relu2", relu2_fn, [a_ffn], [], [(D_FF, BF16)], tm)
    x3 = mm("ffn_2", hid, w2_f, "nn", res=x2)

    loss_blk, dx3, dg_final = loss_and_grad(x3, tgt, g_final, tm)

    grads = {}
    grads["ffn_w2"] = mm("d_ffn_w2", hid, dx3, "tn", out_dtype=BF16).reshape(N_DEV, D_FF // N_DEV, d)
    d_hid = mm("d_hid", dx3, w2_f, "nt")
    (da,), _ = rowwise_bwd("relu2_bwd", relu2_fn, [a_ffn], [], [[d_hid]], tm, [BF16])
    grads["ffn_w1"] = mm("d_ffn_w1", h3, da, "tn", out_dtype=BF16, out_slabs=N_DEV)
    dh3 = mm("d_h3", da, w1_s, "nt", b_slabs=N_DEV)
    (dx2,), (dg_ffn,) = rowwise_bwd("norm_ffn_bwd", rmsnorm_fn, [x2], [norm_ffn_g], [[dh3]], tm, [F32], row_add=[dx3])

    grads["xattn_wo"] = mm("d_wo", o, dx2, "tn", out_dtype=BF16).reshape(N_DEV, d // N_DEV, d)
    d_o = mm("d_o", dx2, wo_f, "nt")
    (dq,), (dkx, dvx) = rowwise_bwd("xattn_bwd", attn_fn, [q], [kx, vx], [[d_o]], tm, [BF16])
    grads["xattn_wq"] = mm("d_wq", h2, dq, "tn", out_dtype=BF16).reshape(N_DEV, d // N_DEV, d)
    dh2 = mm("d_h2", dq, wq_f, "nt")
    grads["xattn_wk"] = mm("d_wk", mn, dkx, "tn", out_dtype=BF16).reshape(N_DEV, d // N_DEV, d)
    grads["xattn_wv"] = mm("d_wv", mn, dvx, "tn", out_dtype=BF16).reshape(N_DEV, d // N_DEV, d)
    dmn = mm("d_mn_v", dvx, wv_f, "nt", res=mm("d_mn_k", dkx, wk_f, "nt"))
    _, (dg_mem,) = rowwise_bwd("norm_mem_bwd", rmsnorm_fn, [memt], [norm_mem_g], [[dmn]], tm, [None])
    (dx1,), (dg_x,) = rowwise_bwd("norm_x_bwd", rmsnorm_fn, [x1], [norm_x_g], [[dh2]], tm, [F32], row_add=[dx2])

    grads["w_out"] = mm("d_w_out", ycat, dx1, "tn", out_dtype=BF16).reshape(N_DEV, d // N_DEV, d)
    d_ycat = mm("d_ycat", dx1, w_out_f, "nt")

    (d_ys, d_r1, d_k1, d_v1, d_gate), (dln_w, dln_b, dr_k) = rowwise_bwd(
        "rwkv_post_bwd", rwkv_post_fn, [ys_r, r_, k_, v_, gate_], rwkv_post_params,
        [[Rows(d_ycat, RWKV_WIDTH, 1)]], tm, [F32] * 5)
    early = ("ffn_w2", "ffn_w1", "xattn_wo", "xattn_wq", "xattn_wk", "xattn_wv", "w_out")
    (d_r2, d_lw, d_k2, d_v2, d_kap, d_b), early_recv = rwkv_scan_bwd(
        r_, lw_, k_, v_, kap_, b_, rwkv_ck, d_ys, ex=Exchange([grads[n] for n in early], scatter=True))
    received = dict(zip(early, early_recv))
    (du_rkv, du_pg, du_pwa), rwkv_pg = rowwise_bwd(
        "rwkv_pre_bwd", rwkv_pre_fn, rwkv_pre_rows(), rwkv_pre_params,
        [[d_r1, d_r2], [d_lw], [d_k1, d_k2], [d_v1, d_v2], [d_kap], [d_b], [d_gate]], 128, [BF16] * 3)
    dmu_rkv, dmu_pg, dmu_pwa, dw0, dw2p, da0, da2p, dg2, dk_k, dk_a = rwkv_pg

    (d_yscan, du_z), (dssd_norm_g,) = rowwise_bwd("ssd_post_bwd", ssd_post_fn, [y_scan, u_z], [ssd_norm_g],
                                                  [[Rows(d_ycat, SSD_WIDTH, 0)]], tm, [F32, BF16])
    dxs, dbm, dcm, ddt2, da_log_p, dd_p = ssd_scan_bwd(xs, bm, cm, dt, a_log_p, d_p, ssd_ck, d_yscan)
    (du_xbc, du_dt), (dconv_w, dconv_b, ddt_bias_p) = rowwise_bwd(
        "ssd_pre_bwd", ssd_pre_fn, ssd_pre_rows(), ssd_pre_params,
        [[dxs], [dbm], [dcm], [ddt2[0], ddt2[1]]], tm, [BF16, BF16])
    du_sm = jnp.concatenate([du_pwa, du_dt], axis=1)

    dw_z = mm("d_w_z", h1, du_z, "tn", out_dtype=BF16)
    dw_xbc = mm("d_w_xbc", h1, du_xbc, "tn", out_dtype=BF16)
    dw_rkv = mm("d_w_rkv", h1, du_rkv, "tn", out_dtype=BF16)
    dw_pg = mm("d_w_pg", h1, du_pg, "tn", out_dtype=BF16)
    dw_sm = mm("d_w_small", h1, du_sm, "tn", out_dtype=BF16)
    dh1 = mm("d_h1_z", du_z, w_z, "nt")
    dh1 = mm("d_h1_xbc", du_xbc, w_xbc, "nt", res=dh1)
    dh1 = mm("d_h1_rkv", du_rkv, w_rkv, "nt", res=dh1)
    dh1 = mm("d_h1_pg", du_pg, w_pg, "nt", res=dh1)
    dh1 = mm("d_h1_small", du_sm, w_sm, "nt", res=dh1)
    (dx,), (dg_mix,) = rowwise_bwd("norm_mix_bwd", rmsnorm_fn, [xt], [norm_mix_g], [[dh1]], tm, [F32], row_add=[dx1])

    dw_in_full = jnp.concatenate([dw_z, dw_xbc, dw_sm[:, 256:272], dw_rkv, dw_sm[:, 0:96], dw_sm[:, 128:224], dw_pg], axis=1)
    to_slabs = lambda g: jnp.transpose(g.reshape(g.shape[0], N_DEV, -1), (1, 0, 2))
    grads["w_in"] = to_slabs(dw_in_full)
    grads["ssd_conv_w"] = to_slabs(dconv_w)
    grads["rwkv_w2"] = to_slabs(dw2p[:96])
    grads["rwkv_a2"] = to_slabs(da2p[:96])
    grads["rwkv_g2"] = to_slabs(dg2)

    dmu = jnp.concatenate([dmu_rkv, dmu_pwa[:, 0:96], dmu_pwa[:, 128:224], dmu_pg], axis=1)
    small_grads = {
        "norm_mix_g": dg_mix, "ssd_conv_b": dconv_b, "ssd_dt_bias": ddt_bias_p[:, :16], "ssd_a_log": da_log_p[:, :16],
        "ssd_d": dd_p[:, :16], "ssd_norm_g": dssd_norm_g, "rwkv_mu": dmu, "rwkv_w0": dw0, "rwkv_a0": da0,
        "rwkv_k_k": dk_k, "rwkv_k_a": dk_a, "rwkv_r_k": dr_k, "rwkv_ln_w": dln_w, "rwkv_ln_b": dln_b,
        "norm_x_g": dg_x, "norm_mem_g": dg_mem, "norm_ffn_g": dg_ffn, "final_norm_g": dg_final}

    tail = ("w_in", "ssd_conv_w", "rwkv_w2", "rwkv_a2", "rwkv_g2")
    sharded = tail + early
    received.update(zip(tail, scatter_many("scatter_grads", [grads[n] for n in tail])))
    (small_all,) = all_gather_many("gather_small_grads", [_pack_small(small_grads)])

    out_g, out_d, out_m, out_v = {}, {}, {}, {}
    for n in sharded:
        shape = wts[n].shape
        two_d = lambda a: a.reshape(-1, shape[-1])
        res = adamw("adamw_" + n, received[n].reshape(N_DEV, -1, shape[-1]), two_d(wts[n]), two_d(mom_m[n]), two_d(mom_v[n]))
        out_g[n], out_d[n], out_m[n], out_v[n] = (r.reshape(shape) for r in res)
    res = adamw("adamw_small", small_all, _pack_small(wts), _pack_small(mom_m), _pack_small(mom_v))
    shapes = {n: wts[n].shape for n in _SMALL}
    for dst, packed in zip((out_g, out_d, out_m, out_v), res):
        dst.update(_unpack_small(packed, shapes))

    loss = lax.psum(loss_blk[0, 0], ("x", "y", "c"))
    return (loss, dx[None], *[out_g[n] for n in _WEIGHTS], *[out_d[n] for n in _WEIGHTS],
            *[out_m[n] for n in _WEIGHTS], *[out_v[n] for n in _WEIGHTS])
```

```python
import functools

import jax
import jax.numpy as jnp
from jax import lax
from jax.experimental import pallas as pl
from jax.experimental.pallas import tpu as pltpu

F32 = jnp.float32
BF16 = jnp.bfloat16
HIGHEST = lax.Precision.HIGHEST

N_DEV = 8
D_MODEL = 2048
NORM_EPS = 1e-6
SSD_WIDTH = 1024
SSD_CONV_DIM = 1536
SSD_HEADS = 16
SSD_HEAD_DIM = 64
SSD_STATE = 128
SSD_CHUNK = 128
SSD_HEADS_PER_GROUP = 8
RWKV_WIDTH = 1024
RWKV_HEADS = 16
RWKV_HEAD_DIM = 64
RWKV_LN_EPS = 64e-5
RWKV_CHUNK = 64
RWKV_HEADS_PER_STEP = 16
XATTN_HEADS = 4
XATTN_HEAD_DIM = 512
D_FF = 8192
LANES = 128
SUBLANES = 8
VMEM_LIMIT = 56 * 1024 * 1024

ADAM_LR = 0.001
ADAM_B1 = 0.9
ADAM_B2 = 0.999
ADAM_EPS = 1e-08
ADAM_WD = 0.01
ADAM_STEP = 10

_DN = {"nn": ((1,), (0,)), "nt": ((1,), (1,)), "tn": ((0,), (0,))}


def _dg(a, b, mode, precision=None):
    (ca,), (cb,) = _DN[mode]
    dn = (((ca + 1,), (cb + 1,)), ((0,), (0,))) if a.ndim == 3 else (((ca,), (cb,)), ((), ()))
    return lax.dot_general(a, b, dn, precision=precision, preferred_element_type=F32)


@functools.partial(jax.custom_vjp, nondiff_argnums=(2,))
def bdot(a, b, mode):
    return _dg(a.astype(BF16), b.astype(BF16), mode)


def _bdot_fwd(a, b, mode):
    return bdot(a, b, mode), (a, b)


def _bdot_bwd(mode, res, g):
    a, b = res
    ab, bb, gb = a.astype(BF16), b.astype(BF16), g.astype(BF16)
    if mode == "nn":
        da, db = _dg(gb, bb, "nt"), _dg(ab, gb, "tn")
    elif mode == "nt":
        da, db = _dg(gb, bb, "nn"), _dg(gb, ab, "tn")
    else:
        da, db = _dg(bb, gb, "nt"), _dg(ab, gb, "nn")
    return da.astype(a.dtype), db.astype(b.dtype)


bdot.defvjp(_bdot_fwd, _bdot_bwd)


def fdot(a, b, mode):
    return _dg(a, b, mode, precision=HIGHEST)


def _split3(x):
    hi = x.astype(BF16)
    r1 = x - hi.astype(F32)
    mid = r1.astype(BF16)
    lo = (r1 - mid.astype(F32)).astype(BF16)
    return hi, mid, lo


def _dot01(x, m01):
    hi, mid, lo = _split3(x)
    return _dg(hi, m01, "nn") + _dg(mid, m01, "nn") + _dg(lo, m01, "nn")


def _exact_dot_impl(a, b, mode, exact):
    if exact == "a":
        ae = a.astype(BF16)
        return sum(_dg(ae, part, mode) for part in _split3(b))
    be = b.astype(BF16)
    return sum(_dg(part, be, mode) for part in _split3(a))


@functools.partial(jax.custom_vjp, nondiff_argnums=(2, 3))
def exact_dot(a, b, mode, exact):
    return _exact_dot_impl(a, b, mode, exact)


def _exact_dot_fwd(a, b, mode, exact):
    return _exact_dot_impl(a, b, mode, exact), (a, b)


def _exact_dot_bwd(mode, exact, res, g):
    a, b = res
    if exact == "a":
        db = {"nn": lambda: _exact_dot_impl(a, g, "tn", "a"), "nt": lambda: _exact_dot_impl(g, a, "tn", "b"),
              "tn": lambda: _exact_dot_impl(a, g, "nn", "a")}[mode]()
        return jnp.zeros_like(a), db
    da = {"nn": lambda: _exact_dot_impl(g, b, "nt", "b"), "nt": lambda: _exact_dot_impl(g, b, "nn", "b"),
          "tn": lambda: _exact_dot_impl(b, g, "nt", "a")}[mode]()
    return da, jnp.zeros_like(b)


exact_dot.defvjp(_exact_dot_fwd, _exact_dot_bwd)


def _head_indicator(width, heads, transpose):
    hd = width // heads
    shape = (LANES, width) if transpose else (width, LANES)
    lane = lax.broadcasted_iota(jnp.int32, shape, 1 if not transpose else 0)
    pos = lax.broadcasted_iota(jnp.int32, shape, 0 if not transpose else 1)
    return ((pos >= lane * hd) & (pos < lane * hd + hd)).astype(BF16)


@jax.custom_vjp
def head_sum(x):
    w = x.shape[-1]
    e = _head_indicator(w, w // RWKV_HEAD_DIM, False)
    et = _head_indicator(w, w // RWKV_HEAD_DIM, True)
    return _dot01(_dot01(x, e), et)


head_sum.defvjp(lambda x: (head_sum(x), None), lambda _, g: (head_sum(g),))


def rmsnorm_fn(x, g):
    y = x * lax.rsqrt(jnp.mean(x * x, axis=-1, keepdims=True) + NORM_EPS)
    return ((y * g).astype(BF16),)


def cast_fn(x):
    return (x.astype(BF16),)


def relu2_fn(a):
    return (jnp.square(jnp.maximum(a, 0.0)).astype(BF16),)


def ssd_pre_fn(xbc, xbc1, xbc2, xbc3, dt_raw, conv_w, conv_b, dt_bias):
    c = conv_w[3:4] * xbc + conv_w[2:3] * xbc1 + conv_w[1:2] * xbc2 + conv_w[0:1] * xbc3 + conv_b
    act = c * jax.nn.sigmoid(c)
    dt = jax.nn.softplus(dt_raw + dt_bias)
    return act[:, :SSD_WIDTH], act[:, SSD_WIDTH:SSD_WIDTH + 256], act[:, SSD_WIDTH + 256:], dt


def ssd_post_fn(yscan, z, norm_g):
    y = yscan * (z * jax.nn.sigmoid(z))
    half = SSD_WIDTH // 2
    parts = []
    for g in range(2):
        yg = y[:, g * half:(g + 1) * half]
        parts.append(yg * lax.rsqrt(jnp.mean(yg * yg, axis=-1, keepdims=True) + NORM_EPS))
    return ((jnp.concatenate(parts, axis=-1) * norm_g).astype(BF16),)


def rwkv_pre_fn(rkv, rkv_p, pg, pg_p, pwa, pwa_p, mu_rkv, mu_pg, mu_pwa, w0, w2p, a0, a2p, g2, k_k, k_a):
    w = RWKV_WIDTH
    rkv = rkv + (rkv_p - rkv) * mu_rkv
    pg = pg + (pg_p - pg) * mu_pg
    pwa = pwa + (pwa_p - pwa) * mu_pwa
    r, k, v = rkv[:, :w], rkv[:, w:2 * w], rkv[:, 2 * w:]
    pw, pa = pwa[:, :LANES], pwa[:, LANES:]
    w_log = -jax.nn.softplus(-(w0 + bdot(jnp.tanh(pw), w2p, "nn"))) - 0.5
    lw = -jnp.exp(w_log)
    iclr = jax.nn.sigmoid(a0 + bdot(pa, a2p, "nn"))
    gate = bdot(jax.nn.sigmoid(pg), g2, "nn")
    kk = k * k_k
    kap = kk / jnp.maximum(jnp.sqrt(head_sum(kk * kk)), 1e-12)
    k_mod = k * (1.0 + (iclr - 1.0) * k_a)
    return r, lw, k_mod, v, kap, kap * iclr, gate


def rwkv_post_fn(ys, r, k_mod, v, gate, ln_w, ln_b, r_k):
    inv_n = 1.0 / RWKV_HEAD_DIM
    mean = head_sum(ys) * inv_n
    yc = ys - mean
    var = head_sum(yc * yc) * inv_n
    yn = yc * lax.rsqrt(var + RWKV_LN_EPS) * ln_w + ln_b
    bonus = head_sum(r * k_mod * r_k) * v
    return (((yn + bonus) * gate).astype(BF16),)


def attn_fn(q, kx, vx):
    outs = []
    for h in range(XATTN_HEADS):
        sl = slice(h * XATTN_HEAD_DIM, (h + 1) * XATTN_HEAD_DIM)
        s = bdot(q[:, sl], kx[:, sl], "nt") * (XATTN_HEAD_DIM ** -0.5)
        s = s - jnp.max(s, axis=-1, keepdims=True)
        p = jnp.exp(s)
        p = p / jnp.sum(p, axis=-1, keepdims=True)
        outs.append(bdot(p, vx[:, sl], "nn"))
    return (jnp.concatenate(outs, axis=-1).astype(BF16),)


def loss_fn(x, tgt, g):
    y = x * lax.rsqrt(jnp.mean(x * x, axis=-1, keepdims=True) + NORM_EPS) * g
    err = jnp.square(y - tgt)
    return 0.5 * jnp.sum(jnp.mean(err, axis=-1, keepdims=True), axis=0, keepdims=True)


def _tri_masks(n):
    row = lax.broadcasted_iota(jnp.int32, (n, n), 0)
    col = lax.broadcasted_iota(jnp.int32, (n, n), 1)
    return col <= row, col < row, row == col


def rwkv_chunk_fn(st0, r, lw, k, v, kap, b):
    h, c = r.shape[0], r.shape[1]
    incl, strict, diag = _tri_masks(c)
    cum = exact_dot(jnp.broadcast_to(incl.astype(F32), (h, c, c)), lw, "nn", "a")
    g_in = jnp.exp(cum)
    g_prev = jnp.exp(cum - lw)
    g_inv = jnp.exp(-cum)
    g_end = jnp.exp(cum[:, c - 1:c, :] - cum)
    kap_t, k_t, b_t, r_t = kap * g_prev, k * g_inv, b * g_inv, r * g_in
    a_ub = jnp.where(strict, bdot(kap_t, b_t, "nt"), 0.0)
    a_vk = jnp.where(strict, bdot(kap_t, k_t, "nt"), 0.0)
    rhs = -(bdot(kap_t, st0, "nn") + bdot(a_vk, v, "nn"))
    eye = diag.astype(F32)
    m = -a_ub
    inv = eye + m
    n = 1
    while n * 2 < c:
        m = bdot(m, m, "nn")
        inv = bdot(inv, eye + m, "nn")
        n *= 2
    u = bdot(inv, rhs, "nn")
    y = (bdot(r_t, st0, "nn")
         + bdot(jnp.where(incl, bdot(r_t, k_t, "nt"), 0.0), v, "nn")
         + bdot(jnp.where(incl, bdot(r_t, b_t, "nt"), 0.0), u, "nn"))
    g_full = jnp.exp(exact_dot(lw, jnp.ones((h, c, st0.shape[2]), F32), "tn", "b"))
    st1 = g_full * st0 + bdot(k * g_end, v, "tn") + bdot(b * g_end, u, "tn")
    return y, st1


def ssd_chunk_fn(group, h0, xs, bm, cm, dt, a_log, d_skip):
    q = xs.shape[0]
    causal, _, _ = _tri_masks(q)
    a_row = -jnp.exp(a_log)
    cs_all = exact_dot(causal.astype(F32), dt * a_row, "nn", "a")
    cs_t = cs_all.T
    cb = bdot(cm, bm, "nt")
    ys, hs = [], []
    p = SSD_HEAD_DIM
    for e in range(SSD_HEADS_PER_GROUP):
        hl = group * SSD_HEADS_PER_GROUP + e
        cs = cs_all[:, hl:hl + 1]
        seg = cs - cs_t[hl:hl + 1, :]
        lmat = jnp.where(causal, jnp.exp(jnp.where(causal, seg, 0.0)), 0.0)
        x_h = xs[:, e * p:(e + 1) * p]
        xdt = x_h * dt[:, hl:hl + 1]
        h0_e = h0[e * p:(e + 1) * p, :]
        cl = cs[q - 1:q, :]
        y = bdot(cb * lmat, xdt, "nn") + bdot(cm, h0_e, "nt") * jnp.exp(cs) + x_h * d_skip[:, hl:hl + 1]
        hs.append(h0_e * jnp.exp(cl) + bdot(xdt * jnp.exp(cl - cs), bm, "tn"))
        ys.append(y)
    return jnp.concatenate(ys, axis=-1), jnp.concatenate(hs, axis=0)


class Rows:
    def __init__(self, arr, w=None, cb=0, shifts=()):
        self.arr, self.w, self.cb, self.shifts = arr, (arr.shape[1] if w is None else w), cb, tuple(shifts)


def _as_rows(x):
    return x if isinstance(x, Rows) else Rows(x)


def _shift_down(x, halo, k):
    rolled = pltpu.roll(x, k, 0)
    first = rolled[0:SUBLANES]
    rid = lax.broadcasted_iota(jnp.int32, first.shape, 0)
    patched = jnp.where(rid < k, pltpu.roll(halo, k, 0), first)
    return jnp.concatenate([patched, rolled[SUBLANES:]], axis=0)


def _shift_up(g, carry, k):
    tm = g.shape[0]
    rolled = pltpu.roll(g, tm - k, 0)
    last = rolled[tm - SUBLANES:]
    rid = lax.broadcasted_iota(jnp.int32, last.shape, 0)
    patched = jnp.where(rid >= SUBLANES - k, pltpu.roll(carry, SUBLANES - k, 0), last)
    return jnp.concatenate([rolled[:tm - SUBLANES], patched], axis=0)


def _params():
    return pltpu.CompilerParams(vmem_limit_bytes=VMEM_LIMIT)


def _load_rows(refs, pos, rins, first_block):
    vals = []
    for r in rins:
        x = refs[pos][...].astype(F32) if refs[pos].dtype != F32 else refs[pos][...]
        pos += 1
        vals.append(x)
        if r.shifts:
            halo = refs[pos][...]
            pos += 1
            halo = jnp.where(first_block, jnp.zeros_like(halo), halo)
            for k in r.shifts:
                vals.append(_shift_down(x, halo, k))
    return vals, pos


def _row_specs(rins, tm, blk):
    specs, args = [], []
    for r in rins:
        specs.append(pl.BlockSpec((tm, r.w), lambda i, cb=r.cb: (blk(i), cb)))
        args.append(r.arr)
        if r.shifts:
            per = tm // SUBLANES
            specs.append(pl.BlockSpec((SUBLANES, r.w), lambda i, cb=r.cb: (jnp.maximum(blk(i) * per - 1, 0), cb)))
            args.append(r.arr)
    return specs, args


def rowwise_fwd(name, fn, rins, params, outs, tm):
    rins = [_as_rows(r) for r in rins]
    t = rins[0].arr.shape[0]
    tm = min(tm, t)
    nb = t // tm
    specs, args = _row_specs(rins, tm, lambda i: i)
    for p in params:
        specs.append(pl.BlockSpec(p.shape, lambda i: (0, 0)))
        args.append(p)
    n_in = len(args)

    def body(*refs):
        vals, pos = _load_rows(refs, 0, rins, pl.program_id(0) == 0)
        pv = [refs[pos + j][...] for j in range(len(params))]
        res = fn(*vals, *pv)
        for o_ref, o in zip(refs[n_in:], res):
            o_ref[...] = o.astype(o_ref.dtype)

    return pl.pallas_call(
        body, name=name, grid=(nb,), in_specs=specs,
        out_specs=[pl.BlockSpec((tm, w), lambda i: (i, 0)) for w, _ in outs],
        out_shape=[jax.ShapeDtypeStruct((t, w), dt) for w, dt in outs],
        compiler_params=_params(),
    )(*args)


def rowwise_bwd(name, fn, rins, params, cts, tm, grad_dtypes, row_add=None):
    rins = [_as_rows(r) for r in rins]
    cts = [[_as_rows(c) for c in lst] for lst in cts]
    row_add = [_as_rows(a) for a in (row_add or [])]
    t = rins[0].arr.shape[0]
    tm = min(tm, t)
    nb = t // tm
    rev = lambda i: nb - 1 - i
    specs, args = _row_specs(rins, tm, rev)
    for p in params:
        specs.append(pl.BlockSpec(p.shape, lambda i: (0, 0)))
        args.append(p)
    flat_cts = [c for lst in cts for c in lst] + row_add
    for c in flat_cts:
        specs.append(pl.BlockSpec((tm, c.w), lambda i, cb=c.cb: (rev(i), cb)))
        args.append(c.arr)
    n_in = len(args)
    want = [i for i, d in enumerate(grad_dtypes) if d is not None]
    out_specs = [pl.BlockSpec((tm, rins[i].w), lambda i_: (rev(i_), 0)) for i in want]
    out_shape = [jax.ShapeDtypeStruct((t, rins[i].w), grad_dtypes[i]) for i in want]
    out_specs += [pl.BlockSpec(p.shape, lambda i: (0, 0)) for p in params]
    out_shape += [jax.ShapeDtypeStruct(p.shape, F32) for p in params]
    n_out = len(out_shape)
    scratch = [pltpu.VMEM((SUBLANES, r.w), F32) for r in rins for _ in r.shifts]

    def body(*refs):
        i = pl.program_id(0)
        vals, pos = _load_rows(refs, 0, rins, rev(i) == 0)
        pv = [refs[pos + j][...] for j in range(len(params))]
        pos += len(params)
        outs, vjp = jax.vjp(fn, *vals, *pv)
        ct_vals = []
        for o, lst in zip(outs, cts):
            acc = None
            for _ in lst:
                cv = refs[pos][...].astype(F32)
                pos += 1
                acc = cv if acc is None else acc + cv
            ct_vals.append(acc.astype(o.dtype))
        adds = [refs[pos + j][...].astype(F32) for j in range(len(row_add))]
        grads = vjp(tuple(ct_vals))
        out_refs = refs[n_in:n_in + n_out]
        carry_refs = refs[n_in + n_out:]

        @pl.when(i == 0)
        def _():
            for cr in carry_refs:
                cr[...] = jnp.zeros_like(cr)
            for pr in out_refs[len(want):]:
                pr[...] = jnp.zeros_like(pr)

        gi, ci, oi = 0, 0, 0
        for idx, r in enumerate(rins):
            d = grads[gi]
            gi += 1
            for k in r.shifts:
                dk = grads[gi]
                gi += 1
                d = d + _shift_up(dk, carry_refs[ci][...], k)
                carry_refs[ci][...] = dk[0:SUBLANES]
                ci += 1
            if idx == 0:
                for a in adds:
                    d = d + a
            if grad_dtypes[idx] is not None:
                out_refs[oi][...] = d.astype(out_refs[oi].dtype)
                oi += 1
        for pr, gp in zip(out_refs[len(want):], grads[gi:]):
            pr[...] += gp

    res = pl.pallas_call(
        body, name=name, grid=(nb,), in_specs=specs, out_specs=out_specs, out_shape=out_shape,
        scratch_shapes=scratch, compiler_params=_params(),
    )(*args)
    return res[:len(want)], res[len(want):]


def _pick(n, pref):
    for c in pref:
        if n % c == 0:
            return c
    return n


MM_VMEM_BUDGET = 40 * 1024 * 1024
MM_PEAK_FLOPS = 0.9e15
MM_HBM_BYTES_PER_S = 3.0e12
MM_STEP_SECONDS = 0.35e-6


def _mm_tiles(m, n, k, size_a, size_b, size_out, size_res, single_k):
    best = None
    for tk in sorted({c for c in (k, 2048, 1024, 512, 256, 128) if c <= 2048 and k % c == 0}, reverse=True):
        for tm in (1024, 512, 256, 128):
            if m % tm:
                continue
            for tn in (1024, 768, 512, 384, 256, 128):
                if n % tn:
                    continue
                nk = k // tk
                vmem = 2 * (tm * tk * size_a + tk * tn * size_b + tm * tn * (size_out + size_res))
                vmem += tm * tn * 4 * (2 if nk > 1 or not single_k else 1)
                vmem += (tm * tk * 2 if size_a > 2 else 0) + (tk * tn * 2 if size_b > 2 else 0)
                if vmem > MM_VMEM_BUDGET:
                    continue
                steps = (m // tm) * (n // tn) * nk
                a_reads = 1 if (nk == 1 and single_k) else n // tn
                traffic = m * k * size_a * a_reads + k * n * size_b * (m // tm) + m * n * (size_out + size_res)
                cost = max(2.0 * m * n * k / MM_PEAK_FLOPS, traffic / MM_HBM_BYTES_PER_S) + steps * MM_STEP_SECONDS
                if best is None or cost < best[0]:
                    best = (cost, tm, tn, tk)
    return best[1:]


def mm(name, a, b, mode, out_dtype=F32, res=None, b_slabs=None, out_slabs=None, ex=None):
    if mode == "tn":
        k_dim, m_dim = a.shape
    else:
        m_dim, k_dim = a.shape
    if b_slabs:
        n_dim = b.shape[0] * b.shape[2] if mode == "nn" else b.shape[1]
    else:
        n_dim = b.shape[0] if mode == "nt" else b.shape[1]
    n_slabs = out_slabs or (b_slabs if (b_slabs and mode == "nn") else 1)
    k_slabs = b_slabs if (b_slabs and mode == "nt") else 1
    tm, tn, tk = _mm_tiles(m_dim, n_dim // n_slabs, k_dim // k_slabs, a.dtype.itemsize, b.dtype.itemsize,
                           jnp.dtype(out_dtype).itemsize, 0 if res is None else res.dtype.itemsize,
                           single_k=(k_slabs == 1))
    nji = n_dim // n_slabs // tn
    nki = k_dim // k_slabs // tk
    nblk = lambda js, j: js * nji + j
    kblk = lambda ks, k: ks * nki + k
    if mode == "tn":
        a_spec = pl.BlockSpec((tk, tm), lambda i, js, j, ks, k: (kblk(ks, k), i))
    else:
        a_spec = pl.BlockSpec((tm, tk), lambda i, js, j, ks, k: (i, kblk(ks, k)))
    if b_slabs and mode == "nn":
        b_spec = pl.BlockSpec((None, tk, tn), lambda i, js, j, ks, k: (js, k, j))
    elif b_slabs and mode == "nt":
        b_spec = pl.BlockSpec((None, tn, tk), lambda i, js, j, ks, k: (ks, nblk(js, j), k))
    elif mode == "nt":
        b_spec = pl.BlockSpec((tn, tk), lambda i, js, j, ks, k: (nblk(js, j), kblk(ks, k)))
    else:
        b_spec = pl.BlockSpec((tk, tn), lambda i, js, j, ks, k: (kblk(ks, k), nblk(js, j)))
    specs, args = [a_spec, b_spec], [a, b]
    if res is not None:
        specs.append(pl.BlockSpec((tm, tn), lambda i, js, j, ks, k: (i, nblk(js, j))))
        args.append(res)
    if out_slabs:
        o_spec = pl.BlockSpec((None, tm, tn), lambda i, js, j, ks, k: (js, i, j))
        o_shape = jax.ShapeDtypeStruct((out_slabs, m_dim, n_dim // out_slabs), out_dtype)
    else:
        o_spec = pl.BlockSpec((tm, tn), lambda i, js, j, ks, k: (i, nblk(js, j)))
        o_shape = jax.ShapeDtypeStruct((m_dim, n_dim), out_dtype)

    one_k_step = k_slabs * nki == 1
    n_in = len(args)

    def body(*refs):
        a_ref, b_ref, o_ref = refs[0], refs[1], refs[n_in]
        part = _dg(a_ref[...].astype(BF16), b_ref[...].astype(BF16), mode)

        def finish(out):
            if res is not None:
                out = out + refs[2][...].astype(F32)
            o_ref[...] = out.astype(o_ref.dtype)

        if one_k_step:
            finish(part)
            return
        acc_ref = refs[n_in + 1]
        ks, kk = pl.program_id(3), pl.program_id(4)

        @pl.when((ks == 0) & (kk == 0))
        def _():
            acc_ref[...] = part

        @pl.when((ks > 0) | (kk > 0))
        def _():
            acc_ref[...] += part

        pl.when((ks == k_slabs - 1) & (kk == nki - 1))(lambda: finish(acc_ref[...]))

    grid = (m_dim // tm, n_slabs, nji, k_slabs, nki)
    scratch = [] if one_k_step else [pltpu.VMEM((tm, tn), F32)]
    first = lambda: functools.reduce(lambda p, q: p & q, [pl.program_id(ax) == 0 for ax in range(5)])
    last = lambda: functools.reduce(lambda p, q: p & q, [pl.program_id(ax) == grid[ax] - 1 for ax in range(5)])
    ex_in_specs, ex_args, ex_out_specs, ex_out_shape, ex_scratch = _ex_parts(ex)
    out = pl.pallas_call(
        _with_exchange(body, n_in, 1, len(scratch), ex, first, last), name=name, grid=grid,
        in_specs=specs + ex_in_specs, out_specs=[o_spec] + ex_out_specs, out_shape=[o_shape] + ex_out_shape,
        scratch_shapes=scratch + ex_scratch,
        compiler_params=pltpu.CompilerParams(
            dimension_semantics=("parallel", "parallel", "parallel", "arbitrary", "arbitrary") if ex is None
            else ("arbitrary",) * 5,
            vmem_limit_bytes=VMEM_LIMIT, has_side_effects=ex is not None),
    )(*args, *ex_args)
    return out[0] if ex is None else (out[0], out[1:])


def _stack_lanes(x, n):
    w = x.shape[1] // n
    return jnp.stack([x[:, i * w:(i + 1) * w] for i in range(n)])


def _stack_rows(x, n):
    w = x.shape[0] // n
    return jnp.stack([x[i * w:(i + 1) * w, :] for i in range(n)])


def _with_exchange(body_fn, n_in, n_out, n_scratch, ex, first_step, last_step):
    if ex is None:
        return body_fn

    def body(*refs):
        n = ex.n
        ins, ex_ins = refs[:n_in], refs[n_in:n_in + n]
        outs = refs[n_in + n:n_in + n + n_out]
        ex_outs = refs[n_in + n + n_out:n_in + 2 * n + n_out]
        scratch = refs[n_in + 2 * n + n_out:n_in + 2 * n + n_out + n_scratch]
        sems = refs[n_in + 2 * n + n_out + n_scratch:]
        pl.when(first_step())(lambda: ex.start(ex_ins, ex_outs, sems))
        body_fn(*ins, *outs, *scratch)
        pl.when(last_step())(lambda: ex.finish(ex_ins, ex_outs, sems))

    return body


def _grid2_ends(n0, n1):
    first = lambda: (pl.program_id(0) == 0) & (pl.program_id(1) == 0)
    last = lambda: (pl.program_id(0) == n0 - 1) & (pl.program_id(1) == n1 - 1)
    return first, last


def _ex_parts(ex):
    if ex is None:
        return [], [], [], [], []
    return ex.specs, list(ex.xs), ex.specs, ex.out_shape, ex.scratch


def rwkv_scan_fwd(r, lw, k, v, kap, b, ex=None):
    t = r.shape[0]
    c, hps, hd = min(RWKV_CHUNK, t), RWKV_HEADS_PER_STEP, RWKV_HEAD_DIM
    nc, ng, wl = t // c, RWKV_HEADS // hps, hps * hd
    spec = pl.BlockSpec((c, wl), lambda g, ci: (ci, g))

    def body(r_ref, lw_ref, k_ref, v_ref, kap_ref, b_ref, y_ref, ck_ref, st_ref):
        @pl.when(pl.program_id(1) == 0)
        def _():
            st_ref[...] = jnp.zeros_like(st_ref)

        st = st_ref[...]
        ck_ref[...] = st
        ins = [x[...] for x in (r_ref, lw_ref, k_ref, v_ref, kap_ref, b_ref)]
        y, st1 = rwkv_chunk_fn(_stack_rows(st, hps), *[_stack_lanes(x, hps) for x in ins])
        y_ref[...] = jnp.concatenate([y[h] for h in range(hps)], axis=-1)
        st_ref[...] = jnp.concatenate([st1[h] for h in range(hps)], axis=0)

    ex_in_specs, ex_args, ex_out_specs, ex_out_shape, ex_scratch = _ex_parts(ex)
    res = pl.pallas_call(
        _with_exchange(body, 6, 2, 1, ex, *_grid2_ends(ng, nc)), name="rwkv_scan_fwd", grid=(ng, nc),
        in_specs=[spec] * 6 + ex_in_specs,
        out_specs=[spec, pl.BlockSpec((None, wl, hd), lambda g, ci: (ci, g, 0))] + ex_out_specs,
        out_shape=[jax.ShapeDtypeStruct((t, RWKV_WIDTH), F32),
                   jax.ShapeDtypeStruct((nc, RWKV_WIDTH, hd), F32)] + ex_out_shape,
        scratch_shapes=[pltpu.VMEM((wl, hd), F32)] + ex_scratch,
        compiler_params=pltpu.CompilerParams(vmem_limit_bytes=VMEM_LIMIT, has_side_effects=ex is not None),
    )(r, lw, k, v, kap, b, *ex_args)
    return res[0], res[1], res[2:]


def rwkv_scan_bwd(r, lw, k, v, kap, b, ck, dy, ex=None):
    t = r.shape[0]
    c, hps, hd = min(RWKV_CHUNK, t), RWKV_HEADS_PER_STEP, RWKV_HEAD_DIM
    nc, ng, wl = t // c, RWKV_HEADS // hps, hps * hd
    spec = pl.BlockSpec((c, wl), lambda g, ci: (nc - 1 - ci, g))

    def body(r_ref, lw_ref, k_ref, v_ref, kap_ref, b_ref, ck_ref, dy_ref, *rest):
        out_refs, dst_ref = rest[:6], rest[6]

        @pl.when(pl.program_id(1) == 0)
        def _():
            dst_ref[...] = jnp.zeros_like(dst_ref)

        ins = [x[...] for x in (r_ref, lw_ref, k_ref, v_ref, kap_ref, b_ref)]
        dyv, ck, dst = dy_ref[...], ck_ref[...], dst_ref[...]
        _, vjp = jax.vjp(rwkv_chunk_fn, _stack_rows(ck, hps), *[_stack_lanes(x, hps) for x in ins])
        grads = vjp((_stack_lanes(dyv, hps), _stack_rows(dst, hps)))
        dst_ref[...] = jnp.concatenate([grads[0][h] for h in range(hps)], axis=0)
        for j in range(6):
            out_refs[j][...] = jnp.concatenate([grads[1 + j][h] for h in range(hps)], axis=-1)

    ex_in_specs, ex_args, ex_out_specs, ex_out_shape, ex_scratch = _ex_parts(ex)
    res = pl.pallas_call(
        _with_exchange(body, 8, 6, 1, ex, *_grid2_ends(ng, nc)), name="rwkv_scan_bwd", grid=(ng, nc),
        in_specs=[spec] * 6 + [pl.BlockSpec((None, wl, hd), lambda g, ci: (nc - 1 - ci, g, 0)), spec] + ex_in_specs,
        out_specs=[spec] * 6 + ex_out_specs,
        out_shape=[jax.ShapeDtypeStruct((t, RWKV_WIDTH), F32)] * 6 + ex_out_shape,
        scratch_shapes=[pltpu.VMEM((wl, hd), F32)] + ex_scratch,
        compiler_params=pltpu.CompilerParams(vmem_limit_bytes=VMEM_LIMIT, has_side_effects=ex is not None),
    )(r, lw, k, v, kap, b, ck, dy, *ex_args)
    return res[:6], res[6:]


def _ssd_specs(q, blk):
    gw = SSD_WIDTH // 2
    return [pl.BlockSpec((q, gw), lambda g, ci: (blk(ci), g)),
            pl.BlockSpec((q, SSD_STATE), lambda g, ci: (blk(ci), g)),
            pl.BlockSpec((q, SSD_STATE), lambda g, ci: (blk(ci), g)),
            pl.BlockSpec((q, LANES), lambda g, ci: (blk(ci), 0)),
            pl.BlockSpec((1, LANES), lambda g, ci: (0, 0)),
            pl.BlockSpec((1, LANES), lambda g, ci: (0, 0))]


def ssd_scan_fwd(xs, bm, cm, dt, a_log, d_skip):
    t = xs.shape[0]
    q = min(SSD_CHUNK, t)
    nc, gw = t // q, SSD_WIDTH // 2

    def body(xs_ref, bm_ref, cm_ref, dt_ref, al_ref, d_ref, y_ref, ck_ref, h_ref):
        @pl.when(pl.program_id(1) == 0)
        def _():
            h_ref[...] = jnp.zeros_like(h_ref)

        ck_ref[...] = h_ref[...]
        args = (h_ref[...], xs_ref[...], bm_ref[...], cm_ref[...], dt_ref[...], al_ref[...], d_ref[...])
        g = pl.program_id(0)

        @pl.when(g == 0)
        def _():
            y, h1 = ssd_chunk_fn(0, *args)
            y_ref[...] = y
            h_ref[...] = h1

        @pl.when(g == 1)
        def _():
            y, h1 = ssd_chunk_fn(1, *args)
            y_ref[...] = y
            h_ref[...] = h1

    return pl.pallas_call(
        body, name="ssd_scan_fwd", grid=(2, nc), in_specs=_ssd_specs(q, lambda ci: ci),
        out_specs=[pl.BlockSpec((q, gw), lambda g, ci: (ci, g)),
                   pl.BlockSpec((None, gw, SSD_STATE), lambda g, ci: (ci, g, 0))],
        out_shape=[jax.ShapeDtypeStruct((t, SSD_WIDTH), F32), jax.ShapeDtypeStruct((nc, SSD_WIDTH, SSD_STATE), F32)],
        scratch_shapes=[pltpu.VMEM((gw, SSD_STATE), F32)], compiler_params=_params(),
    )(xs, bm, cm, dt, a_log, d_skip)


def ssd_scan_bwd(xs, bm, cm, dt, a_log, d_skip, ck, dy):
    t = xs.shape[0]
    q = min(SSD_CHUNK, t)
    nc, gw = t // q, SSD_WIDTH // 2
    rev = lambda ci: nc - 1 - ci

    def body(xs_ref, bm_ref, cm_ref, dt_ref, al_ref, d_ref, ck_ref, dy_ref,
             dxs_ref, dbm_ref, dcm_ref, ddt_ref, dal_ref, dd_ref, dh_ref):
        g, ci = pl.program_id(0), pl.program_id(1)

        @pl.when(ci == 0)
        def _():
            dh_ref[...] = jnp.zeros_like(dh_ref)

        @pl.when((ci == 0) & (g == 0))
        def _():
            dal_ref[...] = jnp.zeros_like(dal_ref)
            dd_ref[...] = jnp.zeros_like(dd_ref)

        args = (ck_ref[...], xs_ref[...], bm_ref[...], cm_ref[...], dt_ref[...], al_ref[...], d_ref[...])

        def run(group):
            _, vjp = jax.vjp(functools.partial(ssd_chunk_fn, group), *args)
            dh0, dxs, dbm, dcm, ddt, dal, dd = vjp((dy_ref[...], dh_ref[...]))
            dh_ref[...] = dh0
            dxs_ref[...] = dxs
            dbm_ref[...] = dbm
            dcm_ref[...] = dcm
            ddt_ref[...] = ddt
            dal_ref[...] += dal
            dd_ref[...] += dd

        pl.when(g == 0)(lambda: run(0))
        pl.when(g == 1)(lambda: run(1))

    in_specs = _ssd_specs(q, rev) + [pl.BlockSpec((None, gw, SSD_STATE), lambda g, ci: (rev(ci), g, 0)),
                                     pl.BlockSpec((q, gw), lambda g, ci: (rev(ci), g))]
    return pl.pallas_call(
        body, name="ssd_scan_bwd", grid=(2, nc), in_specs=in_specs,
        out_specs=[pl.BlockSpec((q, gw), lambda g, ci: (rev(ci), g)),
                   pl.BlockSpec((q, SSD_STATE), lambda g, ci: (rev(ci), g)),
                   pl.BlockSpec((q, SSD_STATE), lambda g, ci: (rev(ci), g)),
                   pl.BlockSpec((None, q, LANES), lambda g, ci: (g, rev(ci), 0)),
                   pl.BlockSpec((1, LANES), lambda g, ci: (0, 0)),
                   pl.BlockSpec((1, LANES), lambda g, ci: (0, 0))],
        out_shape=[jax.ShapeDtypeStruct((t, SSD_WIDTH), F32), jax.ShapeDtypeStruct((t, 2 * SSD_STATE), F32),
                   jax.ShapeDtypeStruct((t, 2 * SSD_STATE), F32), jax.ShapeDtypeStruct((2, t, LANES), F32),
                   jax.ShapeDtypeStruct((1, LANES), F32), jax.ShapeDtypeStruct((1, LANES), F32)],
        scratch_shapes=[pltpu.VMEM((gw, SSD_STATE), F32)], compiler_params=_params(),
    )(xs, bm, cm, dt, a_log, d_skip, ck, dy)


def loss_and_grad(x, tgt, g, tm):
    t, d = x.shape
    tm = min(tm, t)
    nb = t // tm

    def body(x_ref, t_ref, g_ref, loss_ref, dx_ref, dg_ref):
        @pl.when(pl.program_id(0) == 0)
        def _():
            loss_ref[...] = jnp.zeros_like(loss_ref)
            dg_ref[...] = jnp.zeros_like(dg_ref)

        val, vjp = jax.vjp(loss_fn, x_ref[...], t_ref[...], g_ref[...])
        dx, _, dg = vjp(jnp.ones((1, 1), F32))
        loss_ref[...] += jnp.broadcast_to(val, loss_ref.shape)
        dx_ref[...] = dx
        dg_ref[...] += dg

    row = pl.BlockSpec((tm, d), lambda i: (i, 0))
    one = pl.BlockSpec((1, d), lambda i: (0, 0))
    return pl.pallas_call(
        body, name="loss_and_grad", grid=(nb,), in_specs=[row, row, one],
        out_specs=[pl.BlockSpec((SUBLANES, LANES), lambda i: (0, 0)), row, one],
        out_shape=[jax.ShapeDtypeStruct((SUBLANES, LANES), F32), jax.ShapeDtypeStruct((t, d), F32),
                   jax.ShapeDtypeStruct((1, d), F32)],
        compiler_params=_params(),
    )(x, tgt, g)


def adamw(name, recv, w, m, v):
    rows, cols = w.shape
    recv_block_bytes = 4 * 1024 * 1024
    tm = _pick(rows, [c for c in (256, 128, 64, 32, 16, 8) if N_DEV * c * cols * 4 <= recv_block_bytes])
    c1 = 1.0 / (1.0 - ADAM_B1 ** ADAM_STEP)
    c2 = 1.0 / (1.0 - ADAM_B2 ** ADAM_STEP)

    def body(recv_ref, w_ref, m_ref, v_ref, g_ref, d_ref, nm_ref, nv_ref):
        g = recv_ref[0].astype(F32)
        for p in range(1, N_DEV):
            g = g + recv_ref[p].astype(F32)
        nm = ADAM_B1 * m_ref[...] + (1.0 - ADAM_B1) * g
        nv = ADAM_B2 * v_ref[...] + (1.0 - ADAM_B2) * jnp.square(g)
        g_ref[...] = g
        nm_ref[...] = nm
        nv_ref[...] = nv
        d_ref[...] = -ADAM_LR * ((nm * c1) / (jnp.sqrt(nv * c2) + ADAM_EPS) + ADAM_WD * w_ref[...])

    blk = pl.BlockSpec((tm, cols), lambda i: (i, 0))
    return pl.pallas_call(
        body, name=name, grid=(rows // tm,),
        in_specs=[pl.BlockSpec((N_DEV, tm, cols), lambda i: (0, i, 0)), blk, blk, blk],
        out_specs=[blk] * 4, out_shape=[jax.ShapeDtypeStruct((rows, cols), F32)] * 4,
        compiler_params=_params(),
    )(recv, w, m, v)


def _mesh_pos():
    return lax.axis_index("x"), lax.axis_index("y"), lax.axis_index("c")


def _peer(pos, mask):
    x, y, c = pos
    return (1 - x if mask & 4 else x, 1 - y if mask & 2 else y, 1 - c if mask & 1 else c)


def _linear(pos):
    return 4 * pos[0] + 2 * pos[1] + pos[2]


class Exchange:
    def __init__(self, xs, scatter):
        self.xs, self.scatter, self.n = list(xs), scatter, len(xs)
        self.out_shape = [jax.ShapeDtypeStruct(x.shape if scatter else (N_DEV,) + x.shape, x.dtype) for x in xs]
        self.specs = [pl.BlockSpec(memory_space=pl.ANY)] * self.n
        self.scratch = [pltpu.SemaphoreType.DMA((self.n, N_DEV - 1)), pltpu.SemaphoreType.DMA((self.n, N_DEV - 1)),
                        pltpu.SemaphoreType.DMA((self.n,))]

    def _copies(self, ins, outs, sems, landing):
        send_sems, recv_sems, local_sems = sems
        me = _mesh_pos()
        me_lin = _linear(me)
        local, remote = [], []
        for ti in range(self.n):
            src_mine = ins[ti].at[me_lin] if self.scatter else ins[ti]
            local.append(pltpu.make_async_copy(src_mine, outs[ti].at[me_lin], local_sems.at[ti]))
            for mask in range(1, N_DEV):
                peer = _peer(me, mask)
                src = ins[ti].at[_linear(peer)] if self.scatter else ins[ti]
                dst = outs[ti].at[_linear(peer) if landing else me_lin]
                remote.append(pltpu.make_async_remote_copy(
                    src_ref=src, dst_ref=dst, send_sem=send_sems.at[ti, mask - 1], recv_sem=recv_sems.at[ti, mask - 1],
                    device_id=peer, device_id_type=pl.DeviceIdType.MESH))
        return local, remote

    def start(self, ins, outs, sems):
        local, remote = self._copies(ins, outs, sems, landing=False)
        for cp in local + remote:
            cp.start()

    def finish(self, ins, outs, sems):
        local, remote = self._copies(ins, outs, sems, landing=True)
        for cp in remote:
            cp.wait_recv()
        for cp in remote:
            cp.wait_send()
        for cp in local:
            cp.wait()


def exchange_call(name, xs, scatter):
    ex = Exchange(xs, scatter)
    n = ex.n

    def body(*refs):
        ins, outs, sems = refs[:n], refs[n:2 * n], refs[2 * n:]
        ex.start(ins, outs, sems)
        ex.finish(ins, outs, sems)

    return pl.pallas_call(
        body, name=name, in_specs=ex.specs, out_specs=ex.specs, out_shape=ex.out_shape, scratch_shapes=ex.scratch,
        compiler_params=pltpu.CompilerParams(has_side_effects=True),
    )(*xs)


def all_gather_many(name, xs):
    return exchange_call(name, xs, scatter=False)


def scatter_many(name, xs):
    return exchange_call(name, xs, scatter=True)


_Z = (0, 1024)
_XBC = (1024, 2560)
_DT = (2560, 2576)
_RKV = (2576, 5648)
_PW = (5648, 5744)
_PA = (5744, 5840)
_PG = (5840, 6096)
D_IN = 6096

_SMALL = ("norm_mix_g", "ssd_conv_b", "ssd_dt_bias", "ssd_a_log", "ssd_d", "ssd_norm_g", "rwkv_mu", "rwkv_w0",
          "rwkv_a0", "rwkv_k_k", "rwkv_k_a", "rwkv_r_k", "rwkv_ln_w", "rwkv_ln_b", "norm_x_g", "norm_mem_g",
          "norm_ffn_g", "final_norm_g")
_WEIGHTS = ("norm_mix_g", "w_in", "ssd_conv_w", "ssd_conv_b", "ssd_dt_bias", "ssd_a_log", "ssd_d", "ssd_norm_g",
            "rwkv_mu", "rwkv_w0", "rwkv_w2", "rwkv_a0", "rwkv_a2", "rwkv_g2", "rwkv_k_k", "rwkv_k_a", "rwkv_r_k",
            "rwkv_ln_w", "rwkv_ln_b", "w_out", "norm_x_g", "norm_mem_g", "xattn_wq", "xattn_wk", "xattn_wv",
            "xattn_wo", "norm_ffn_g", "ffn_w1", "ffn_w2", "final_norm_g")


def _pad_lanes(x, width=LANES):
    return jnp.pad(x, ((0, 0), (0, width - x.shape[1])))


def _pack_small(vals):
    flat = jnp.concatenate([vals[n].reshape(-1) for n in _SMALL])
    rows = -(-flat.shape[0] // (LANES * SUBLANES)) * SUBLANES
    return jnp.pad(flat, (0, rows * LANES - flat.shape[0])).reshape(rows, LANES)


def _unpack_small(packed, shapes):
    flat = packed.reshape(-1)
    out, pos = {}, 0
    for n in _SMALL:
        size = 1
        for s in shapes[n]:
            size *= s
        out[n] = flat[pos:pos + size].reshape(shapes[n])
        pos += size
    return out


def _cols(w, rng):
    return w[:, rng[0]:rng[1]]


def kernel(x, mem, norm_mix_g, w_in, ssd_conv_w, ssd_conv_b, ssd_dt_bias, ssd_a_log, ssd_d, ssd_norm_g, rwkv_mu, rwkv_w0, rwkv_w2, rwkv_a0, rwkv_a2, rwkv_g2, rwkv_k_k, rwkv_k_a, rwkv_r_k, rwkv_ln_w, rwkv_ln_b, w_out, norm_x_g, norm_mem_g, xattn_wq, xattn_wk, xattn_wv, xattn_wo, norm_ffn_g, ffn_w1, ffn_w2, final_norm_g, loss_target, m_norm_mix_g, m_w_in, m_ssd_conv_w, m_ssd_conv_b, m_ssd_dt_bias, m_ssd_a_log, m_ssd_d, m_ssd_norm_g, m_rwkv_mu, m_rwkv_w0, m_rwkv_w2, m_rwkv_a0, m_rwkv_a2, m_rwkv_g2, m_rwkv_k_k, m_rwkv_k_a, m_rwkv_r_k, m_rwkv_ln_w, m_rwkv_ln_b, m_w_out, m_norm_x_g, m_norm_mem_g, m_xattn_wq, m_xattn_wk, m_xattn_wv, m_xattn_wo, m_norm_ffn_g, m_ffn_w1, m_ffn_w2, m_final_norm_g, v_norm_mix_g, v_w_in, v_ssd_conv_w, v_ssd_conv_b, v_ssd_dt_bias, v_ssd_a_log, v_ssd_d, v_ssd_norm_g, v_rwkv_mu, v_rwkv_w0, v_rwkv_w2, v_rwkv_a0, v_rwkv_a2, v_rwkv_g2, v_rwkv_k_k, v_rwkv_k_a, v_rwkv_r_k, v_rwkv_ln_w, v_rwkv_ln_b, v_w_out, v_norm_x_g, v_norm_mem_g, v_xattn_wq, v_xattn_wk, v_xattn_wv, v_xattn_wo, v_norm_ffn_g, v_ffn_w1, v_ffn_w2, v_final_norm_g):
    given = dict(locals())
    wts = {n: given[n] for n in _WEIGHTS}
    mom_m = {n: given["m_" + n] for n in _WEIGHTS}
    mom_v = {n: given["v_" + n] for n in _WEIGHTS}
    d = D_MODEL
    xt, memt, tgt = x[0], mem[0], loss_target[0]
    tm = 256

    big = {"w_in": w_in[0], "w_out": w_out[0], "xattn_wq": xattn_wq[0], "xattn_wk": xattn_wk[0],
           "xattn_wv": xattn_wv[0], "xattn_wo": xattn_wo[0], "ffn_w1": ffn_w1[0], "ffn_w2": ffn_w2[0]}
    small_sh = {"ssd_conv_w": ssd_conv_w.reshape(4, -1), "rwkv_w2": rwkv_w2[0], "rwkv_a2": rwkv_a2[0],
                "rwkv_g2": rwkv_g2[0]}
    cast = {n: rowwise_fwd("cast_" + n, cast_fn, [a], [], [(a.shape[1], BF16)], 256)[0] for n, a in big.items()}
    gathered = all_gather_many("gather_weights", [cast["w_in"]] + list(small_sh.values()))
    g_big = {"w_in": gathered[0]}
    g_small = dict(zip(small_sh, gathered[1:]))
    late = [n for n in big if n != "w_in"]
    late_gather = Exchange([cast[n] for n in late], scatter=False)

    w_in_full = jnp.transpose(g_big["w_in"], (1, 0, 2)).reshape(d, D_IN)
    w_z, w_xbc, w_rkv, w_pg = (_cols(w_in_full, r) for r in (_Z, _XBC, _RKV, _PG))
    w_sm = jnp.concatenate([_pad_lanes(_cols(w_in_full, r)) for r in (_PW, _PA, _DT)], axis=1)
    unshard_cols = lambda g: jnp.transpose(g, (1, 0, 2)).reshape(g.shape[1], -1)
    conv_w_f = unshard_cols(g_small["ssd_conv_w"])
    pad_rows = lambda a: jnp.pad(a, ((0, LANES - a.shape[0]), (0, 0)))
    w2p, a2p = pad_rows(unshard_cols(g_small["rwkv_w2"])), pad_rows(unshard_cols(g_small["rwkv_a2"]))
    g2_f = unshard_cols(g_small["rwkv_g2"])

    mu = rwkv_mu
    mu_rkv, mu_pg = mu[:, :3072], mu[:, 3264:3520]
    mu_pwa = jnp.concatenate([_pad_lanes(mu[:, 3072:3168]), _pad_lanes(mu[:, 3168:3264])], axis=1)
    dt_bias_p, a_log_p, d_p = _pad_lanes(ssd_dt_bias), _pad_lanes(ssd_a_log), _pad_lanes(ssd_d)
    r_k_row = rwkv_r_k.reshape(1, RWKV_WIDTH)
    g_final = final_norm_g.reshape(1, d)

    (h1,) = rowwise_fwd("norm_mix", rmsnorm_fn, [xt], [norm_mix_g], [(d, BF16)], tm)
    u_z = mm("in_z", h1, w_z, "nn")
    u_xbc = mm("in_xbc", h1, w_xbc, "nn")
    u_rkv = mm("in_rkv", h1, w_rkv, "nn")
    u_pg = mm("in_pg", h1, w_pg, "nn")
    u_sm = mm("in_small", h1, w_sm, "nn")

    ssd_pre_rows = lambda: [Rows(u_xbc, shifts=(1, 2, 3)), Rows(u_sm, LANES, 2)]
    ssd_pre_params = [conv_w_f, ssd_conv_b, dt_bias_p]
    xs, bm, cm, dt = rowwise_fwd("ssd_pre", ssd_pre_fn, ssd_pre_rows(), ssd_pre_params,
                                 [(SSD_WIDTH, F32), (256, F32), (256, F32), (LANES, F32)], tm)
    y_scan, ssd_ck = ssd_scan_fwd(xs, bm, cm, dt, a_log_p, d_p)
    (y_ssd,) = rowwise_fwd("ssd_post", ssd_post_fn, [y_scan, u_z], [ssd_norm_g], [(SSD_WIDTH, BF16)], tm)

    rwkv_pre_rows = lambda: [Rows(u_rkv, shifts=(1,)), Rows(u_pg, shifts=(1,)), Rows(u_sm, 2 * LANES, 0, shifts=(1,))]
    rwkv_pre_params = [mu_rkv, mu_pg, mu_pwa, rwkv_w0, w2p, rwkv_a0, a2p, g2_f, rwkv_k_k, rwkv_k_a]
    r_, lw_, k_, v_, kap_, b_, gate_ = rowwise_fwd("rwkv_pre", rwkv_pre_fn, rwkv_pre_rows(), rwkv_pre_params,
                                                   [(RWKV_WIDTH, F32)] * 7, 128)
    ys_r, rwkv_ck, late_w = rwkv_scan_fwd(r_, lw_, k_, v_, kap_, b_, ex=late_gather)
    g_big.update(zip(late, late_w))
    w_out_f = g_big["w_out"].reshape(d, d)
    wq_f, wk_f, wv_f, wo_f = (g_big[n].reshape(d, d) for n in ("xattn_wq", "xattn_wk", "xattn_wv", "xattn_wo"))
    w1_s = g_big["ffn_w1"]
    w2_f = g_big["ffn_w2"].reshape(D_FF, d)
    rwkv_post_params = [rwkv_ln_w, rwkv_ln_b, r_k_row]
    (y_rwkv,) = rowwise_fwd("rwkv_post", rwkv_post_fn, [ys_r, r_, k_, v_, gate_], rwkv_post_params,
                            [(RWKV_WIDTH, BF16)], tm)
    ycat = jnp.concatenate([y_ssd, y_rwkv], axis=1)
    x1 = mm("out_proj", ycat, w_out_f, "nn", res=xt)

    (h2,) = rowwise_fwd("norm_x", rmsnorm_fn, [x1], [norm_x_g], [(d, BF16)], tm)
    (mn,) = rowwise_fwd("norm_mem", rmsnorm_fn, [memt], [norm_mem_g], [(d, BF16)], tm)
    q = mm("xattn_q", h2, wq_f, "nn", out_dtype=BF16)
    kx = mm("xattn_k", mn, wk_f, "nn")
    vx = mm("xattn_v", mn, wv_f, "nn")
    (o,) = rowwise_fwd("xattn", attn_fn, [q], [kx, vx], [(d, BF16)], tm)
    x2 = mm("xattn_o", o, wo_f, "nn", res=x1)

    (h3,) = rowwise_fwd("norm_ffn", rmsnorm_fn, [x2], [norm_ffn_g], [(d, BF16)], tm)
    a_ffn = mm("ffn_1", h3, w1_s, "nn", b_slabs=N_DEV)
    (hid,) = rowwise_fwd("relu2", relu2_fn, [a_ffn], [], [(D_FF, BF16)], tm)
    x3 = mm("ffn_2", hid, w2_f, "nn", res=x2)

    loss_blk, dx3, dg_final = loss_and_grad(x3, tgt, g_final, tm)

    grads = {}
    grads["ffn_w2"] = mm("d_ffn_w2", hid, dx3, "tn", out_dtype=BF16).reshape(N_DEV, D_FF // N_DEV, d)
    d_hid = mm("d_hid", dx3, w2_f, "nt")
    (da,), _ = rowwise_bwd("relu2_bwd", relu2_fn, [a_ffn], [], [[d_hid]], tm, [BF16])
    grads["ffn_w1"] = mm("d_ffn_w1", h3, da, "tn", out_dtype=BF16, out_slabs=N_DEV)
    dh3 = mm("d_h3", da, w1_s, "nt", b_slabs=N_DEV)
    (dx2,), (dg_ffn,) = rowwise_bwd("norm_ffn_bwd", rmsnorm_fn, [x2], [norm_ffn_g], [[dh3]], tm, [F32], row_add=[dx3])

    grads["xattn_wo"] = mm("d_wo", o, dx2, "tn", out_dtype=BF16).reshape(N_DEV, d // N_DEV, d)
    d_o = mm("d_o", dx2, wo_f, "nt")
    (dq,), (dkx, dvx) = rowwise_bwd("xattn_bwd", attn_fn, [q], [kx, vx], [[d_o]], tm, [BF16])
    grads["xattn_wq"] = mm("d_wq", h2, dq, "tn", out_dtype=BF16).reshape(N_DEV, d // N_DEV, d)
    dh2 = mm("d_h2", dq, wq_f, "nt")
    grads["xattn_wk"] = mm("d_wk", mn, dkx, "tn", out_dtype=BF16).reshape(N_DEV, d // N_DEV, d)
    grads["xattn_wv"] = mm("d_wv", mn, dvx, "tn", out_dtype=BF16).reshape(N_DEV, d // N_DEV, d)
    dmn = mm("d_mn_v", dvx, wv_f, "nt", res=mm("d_mn_k", dkx, wk_f, "nt"))
    _, (dg_mem,) = rowwise_bwd("norm_mem_bwd", rmsnorm_fn, [memt], [norm_mem_g], [[dmn]], tm, [None])
    (dx1,), (dg_x,) = rowwise_bwd("norm_x_bwd", rmsnorm_fn, [x1], [norm_x_g], [[dh2]], tm, [F32], row_add=[dx2])

    grads["w_out"] = mm("d_w_out", ycat, dx1, "tn", out_dtype=BF16).reshape(N_DEV, d // N_DEV, d)
    d_ycat = mm("d_ycat", dx1, w_out_f, "nt")

    (d_ys, d_r1, d_k1, d_v1, d_gate), (dln_w, dln_b, dr_k) = rowwise_bwd(
        "rwkv_post_bwd", rwkv_post_fn, [ys_r, r_, k_, v_, gate_], rwkv_post_params,
        [[Rows(d_ycat, RWKV_WIDTH, 1)]], tm, [F32] * 5)
    early = ("ffn_w2", "ffn_w1", "xattn_wo", "xattn_wq", "xattn_wk", "xattn_wv", "w_out")
    (d_r2, d_lw, d_k2, d_v2, d_kap, d_b), early_recv = rwkv_scan_bwd(
        r_, lw_, k_, v_, kap_, b_, rwkv_ck, d_ys, ex=Exchange([grads[n] for n in early], scatter=True))
    received = dict(zip(early, early_recv))
    (du_rkv, du_pg, du_pwa), rwkv_pg = rowwise_bwd(
        "rwkv_pre_bwd", rwkv_pre_fn, rwkv_pre_rows(), rwkv_pre_params,
        [[d_r1, d_r2], [d_lw], [d_k1, d_k2], [d_v1, d_v2], [d_kap], [d_b], [d_gate]], 128, [BF16] * 3)
    dmu_rkv, dmu_pg, dmu_pwa, dw0, dw2p, da0, da2p, dg2, dk_k, dk_a = rwkv_pg

    (d_yscan, du_z), (dssd_norm_g,) = rowwise_bwd("ssd_post_bwd", ssd_post_fn, [y_scan, u_z], [ssd_norm_g],
                                                  [[Rows(d_ycat, SSD_WIDTH, 0)]], tm, [F32, BF16])
    dxs, dbm, dcm, ddt2, da_log_p, dd_p = ssd_scan_bwd(xs, bm, cm, dt, a_log_p, d_p, ssd_ck, d_yscan)
    (du_xbc, du_dt), (dconv_w, dconv_b, ddt_bias_p) = rowwise_bwd(
        "ssd_pre_bwd", ssd_pre_fn, ssd_pre_rows(), ssd_pre_params,
        [[dxs], [dbm], [dcm], [ddt2[0], ddt2[1]]], tm, [BF16, BF16])
    du_sm = jnp.concatenate([du_pwa, du_dt], axis=1)

    dw_z = mm("d_w_z", h1, du_z, "tn", out_dtype=BF16)
    dw_xbc = mm("d_w_xbc", h1, du_xbc, "tn", out_dtype=BF16)
    dw_rkv = mm("d_w_rkv", h1, du_rkv, "tn", out_dtype=BF16)
    dw_pg = mm("d_w_pg", h1, du_pg, "tn", out_dtype=BF16)
    dw_sm = mm("d_w_small", h1, du_sm, "tn", out_dtype=BF16)
    dh1 = mm("d_h1_z", du_z, w_z, "nt")
    dh1 = mm("d_h1_xbc", du_xbc, w_xbc, "nt", res=dh1)
    dh1 = mm("d_h1_rkv", du_rkv, w_rkv, "nt", res=dh1)
    dh1 = mm("d_h1_pg", du_pg, w_pg, "nt", res=dh1)
    dh1 = mm("d_h1_small", du_sm, w_sm, "nt", res=dh1)
    (dx,), (dg_mix,) = rowwise_bwd("norm_mix_bwd", rmsnorm_fn, [xt], [norm_mix_g], [[dh1]], tm, [F32], row_add=[dx1])

    dw_in_full = jnp.concatenate([dw_z, dw_xbc, dw_sm[:, 256:272], dw_rkv, dw_sm[:, 0:96], dw_sm[:, 128:224], dw_pg], axis=1)
    to_slabs = lambda g: jnp.transpose(g.reshape(g.shape[0], N_DEV, -1), (1, 0, 2))
    grads["w_in"] = to_slabs(dw_in_full)
    grads["ssd_conv_w"] = to_slabs(dconv_w)
    grads["rwkv_w2"] = to_slabs(dw2p[:96])
    grads["rwkv_a2"] = to_slabs(da2p[:96])
    grads["rwkv_g2"] = to_slabs(dg2)

    dmu = jnp.concatenate([dmu_rkv, dmu_pwa[:, 0:96], dmu_pwa[:, 128:224], dmu_pg], axis=1)
    small_grads = {
        "norm_mix_g": dg_mix, "ssd_conv_b": dconv_b, "ssd_dt_bias": ddt_bias_p[:, :16], "ssd_a_log": da_log_p[:, :16],
        "ssd_d": dd_p[:, :16], "ssd_norm_g": dssd_norm_g, "rwkv_mu": dmu, "rwkv_w0": dw0, "rwkv_a0": da0,
        "rwkv_k_k": dk_k, "rwkv_k_a": dk_a, "rwkv_r_k": dr_k, "rwkv_ln_w": dln_w, "rwkv_ln_b": dln_b,
        "norm_x_g": dg_x, "norm_mem_g": dg_mem, "norm_ffn_g": dg_ffn, "final_norm_g": dg_final}

    tail = ("w_in", "ssd_conv_w", "rwkv_w2", "rwkv_a2", "rwkv_g2")
    sharded = tail + early
    received.update(zip(tail, scatter_many("scatter_grads", [grads[n] for n in tail])))
    (small_all,) = all_gather_many("gather_small_grads", [_pack_small(small_grads)])

    out_g, out_d, out_m, out_v = {}, {}, {}, {}
    for n in sharded:
        shape = wts[n].shape
        two_d = lambda a: a.reshape(-1, shape[-1])
        res = adamw("adamw_" + n, received[n].reshape(N_DEV, -1, shape[-1]), two_d(wts[n]), two_d(mom_m[n]), two_d(mom_v[n]))
        out_g[n], out_d[n], out_m[n], out_v[n] = (r.reshape(shape) for r in res)
    res = adamw("adamw_small", small_all, _pack_small(wts), _pack_small(mom_m), _pack_small(mom_v))
    shapes = {n: wts[n].shape for n in _SMALL}
    for dst, packed in zip((out_g, out_d, out_m, out_v), res):
        dst.update(_unpack_small(packed, shapes))

    loss = lax.psum(loss_blk[0, 0], ("x", "y", "c"))
    return (loss, dx[None], *[out_g[n] for n in _WEIGHTS], *[out_d[n] for n in _WEIGHTS],
            *[out_m[n] for n in _WEIGHTS], *[out_v[n] for n in _WEIGHTS])
```

```python
import functools

import jax
import jax.numpy as jnp
from jax import lax
from jax.experimental import pallas as pl
from jax.experimental.pallas import tpu as pltpu

F32 = jnp.float32
BF16 = jnp.bfloat16
HIGHEST = lax.Precision.HIGHEST

N_DEV = 8
D_MODEL = 2048
NORM_EPS = 1e-6
SSD_WIDTH = 1024
SSD_CONV_DIM = 1536
SSD_HEADS = 16
SSD_HEAD_DIM = 64
SSD_STATE = 128
SSD_CHUNK = 128
SSD_HEADS_PER_GROUP = 8
RWKV_WIDTH = 1024
RWKV_HEADS = 16
RWKV_HEAD_DIM = 64
RWKV_LN_EPS = 64e-5
RWKV_CHUNK = 64
RWKV_HEADS_PER_STEP = 16
XATTN_HEADS = 4
XATTN_HEAD_DIM = 512
D_FF = 8192
LANES = 128
SUBLANES = 8
VMEM_LIMIT = 56 * 1024 * 1024

ADAM_LR = 0.001
ADAM_B1 = 0.9
ADAM_B2 = 0.999
ADAM_EPS = 1e-08
ADAM_WD = 0.01
ADAM_STEP = 10

_DN = {"nn": ((1,), (0,)), "nt": ((1,), (1,)), "tn": ((0,), (0,))}


def _dg(a, b, mode, precision=None):
    (ca,), (cb,) = _DN[mode]
    dn = (((ca + 1,), (cb + 1,)), ((0,), (0,))) if a.ndim == 3 else (((ca,), (cb,)), ((), ()))
    return lax.dot_general(a, b, dn, precision=precision, preferred_element_type=F32)


@functools.partial(jax.custom_vjp, nondiff_argnums=(2,))
def bdot(a, b, mode):
    return _dg(a.astype(BF16), b.astype(BF16), mode)


def _bdot_fwd(a, b, mode):
    return bdot(a, b, mode), (a, b)


def _bdot_bwd(mode, res, g):
    a, b = res
    ab, bb, gb = a.astype(BF16), b.astype(BF16), g.astype(BF16)
    if mode == "nn":
        da, db = _dg(gb, bb, "nt"), _dg(ab, gb, "tn")
    elif mode == "nt":
        da, db = _dg(gb, bb, "nn"), _dg(gb, ab, "tn")
    else:
        da, db = _dg(bb, gb, "nt"), _dg(ab, gb, "nn")
    return da.astype(a.dtype), db.astype(b.dtype)


bdot.defvjp(_bdot_fwd, _bdot_bwd)


def fdot(a, b, mode):
    return _dg(a, b, mode, precision=HIGHEST)


def _split3(x):
    hi = x.astype(BF16)
    r1 = x - hi.astype(F32)
    mid = r1.astype(BF16)
    lo = (r1 - mid.astype(F32)).astype(BF16)
    return hi, mid, lo


def _dot01(x, m01):
    hi, mid, lo = _split3(x)
    return _dg(hi, m01, "nn") + _dg(mid, m01, "nn") + _dg(lo, m01, "nn")


def _exact_dot_impl(a, b, mode, exact):
    if exact == "a":
        ae = a.astype(BF16)
        return sum(_dg(ae, part, mode) for part in _split3(b))
    be = b.astype(BF16)
    return sum(_dg(part, be, mode) for part in _split3(a))


@functools.partial(jax.custom_vjp, nondiff_argnums=(2, 3))
def exact_dot(a, b, mode, exact):
    return _exact_dot_impl(a, b, mode, exact)


def _exact_dot_fwd(a, b, mode, exact):
    return _exact_dot_impl(a, b, mode, exact), (a, b)


def _exact_dot_bwd(mode, exact, res, g):
    a, b = res
    if exact == "a":
        db = {"nn": lambda: _exact_dot_impl(a, g, "tn", "a"), "nt": lambda: _exact_dot_impl(g, a, "tn", "b"),
              "tn": lambda: _exact_dot_impl(a, g, "nn", "a")}[mode]()
        return jnp.zeros_like(a), db
    da = {"nn": lambda: _exact_dot_impl(g, b, "nt", "b"), "nt": lambda: _exact_dot_impl(g, b, "nn", "b"),
          "tn": lambda: _exact_dot_impl(b, g, "nt", "a")}[mode]()
    return da, jnp.zeros_like(b)


exact_dot.defvjp(_exact_dot_fwd, _exact_dot_bwd)


def _head_indicator(width, heads, transpose):
    hd = width // heads
    shape = (LANES, width) if transpose else (width, LANES)
    lane = lax.broadcasted_iota(jnp.int32, shape, 1 if not transpose else 0)
    pos = lax.broadcasted_iota(jnp.int32, shape, 0 if not transpose else 1)
    return ((pos >= lane * hd) & (pos < lane * hd + hd)).astype(BF16)


@jax.custom_vjp
def head_sum(x):
    w = x.shape[-1]
    e = _head_indicator(w, w // RWKV_HEAD_DIM, False)
    et = _head_indicator(w, w // RWKV_HEAD_DIM, True)
    return _dot01(_dot01(x, e), et)


head_sum.defvjp(lambda x: (head_sum(x), None), lambda _, g: (head_sum(g),))


def rmsnorm_fn(x, g):
    y = x * lax.rsqrt(jnp.mean(x * x, axis=-1, keepdims=True) + NORM_EPS)
    return ((y * g).astype(BF16),)


def cast_fn(x):
    return (x.astype(BF16),)


def relu2_fn(a):
    return (jnp.square(jnp.maximum(a, 0.0)).astype(BF16),)


def ssd_pre_fn(xbc, xbc1, xbc2, xbc3, dt_raw, conv_w, conv_b, dt_bias):
    c = conv_w[3:4] * xbc + conv_w[2:3] * xbc1 + conv_w[1:2] * xbc2 + conv_w[0:1] * xbc3 + conv_b
    act = c * jax.nn.sigmoid(c)
    dt = jax.nn.softplus(dt_raw + dt_bias)
    return act[:, :SSD_WIDTH], act[:, SSD_WIDTH:SSD_WIDTH + 256], act[:, SSD_WIDTH + 256:], dt


def ssd_post_fn(yscan, z, norm_g):
    y = yscan * (z * jax.nn.sigmoid(z))
    half = SSD_WIDTH // 2
    parts = []
    for g in range(2):
        yg = y[:, g * half:(g + 1) * half]
        parts.append(yg * lax.rsqrt(jnp.mean(yg * yg, axis=-1, keepdims=True) + NORM_EPS))
    return ((jnp.concatenate(parts, axis=-1) * norm_g).astype(BF16),)


def rwkv_pre_fn(rkv, rkv_p, pg, pg_p, pwa, pwa_p, mu_rkv, mu_pg, mu_pwa, w0, w2p, a0, a2p, g2, k_k, k_a):
    w = RWKV_WIDTH
    rkv = rkv + (rkv_p - rkv) * mu_rkv
    pg = pg + (pg_p - pg) * mu_pg
    pwa = pwa + (pwa_p - pwa) * mu_pwa
    r, k, v = rkv[:, :w], rkv[:, w:2 * w], rkv[:, 2 * w:]
    pw, pa = pwa[:, :LANES], pwa[:, LANES:]
    w_log = -jax.nn.softplus(-(w0 + bdot(jnp.tanh(pw), w2p, "nn"))) - 0.5
    lw = -jnp.exp(w_log)
    iclr = jax.nn.sigmoid(a0 + bdot(pa, a2p, "nn"))
    gate = bdot(jax.nn.sigmoid(pg), g2, "nn")
    kk = k * k_k
    kap = kk / jnp.maximum(jnp.sqrt(head_sum(kk * kk)), 1e-12)
    k_mod = k * (1.0 + (iclr - 1.0) * k_a)
    return r, lw, k_mod, v, kap, kap * iclr, gate


def rwkv_post_fn(ys, r, k_mod, v, gate, ln_w, ln_b, r_k):
    inv_n = 1.0 / RWKV_HEAD_DIM
    mean = head_sum(ys) * inv_n
    yc = ys - mean
    var = head_sum(yc * yc) * inv_n
    yn = yc * lax.rsqrt(var + RWKV_LN_EPS) * ln_w + ln_b
    bonus = head_sum(r * k_mod * r_k) * v
    return (((yn + bonus) * gate).astype(BF16),)


def attn_fn(q, kx, vx):
    outs = []
    for h in range(XATTN_HEADS):
        sl = slice(h * XATTN_HEAD_DIM, (h + 1) * XATTN_HEAD_DIM)
        s = bdot(q[:, sl], kx[:, sl], "nt") * (XATTN_HEAD_DIM ** -0.5)
        s = s - jnp.max(s, axis=-1, keepdims=True)
        p = jnp.exp(s)
        p = p / jnp.sum(p, axis=-1, keepdims=True)
        outs.append(bdot(p, vx[:, sl], "nn"))
    return (jnp.concatenate(outs, axis=-1).astype(BF16),)


def loss_fn(x, tgt, g):
    y = x * lax.rsqrt(jnp.mean(x * x, axis=-1, keepdims=True) + NORM_EPS) * g
    err = jnp.square(y - tgt)
    return 0.5 * jnp.sum(jnp.mean(err, axis=-1, keepdims=True), axis=0, keepdims=True)


def _tri_masks(n):
    row = lax.broadcasted_iota(jnp.int32, (n, n), 0)
    col = lax.broadcasted_iota(jnp.int32, (n, n), 1)
    return col <= row, col < row, row == col


def rwkv_chunk_fn(st0, r, lw, k, v, kap, b):
    h, c = r.shape[0], r.shape[1]
    incl, strict, diag = _tri_masks(c)
    cum = exact_dot(jnp.broadcast_to(incl.astype(F32), (h, c, c)), lw, "nn", "a")
    g_in = jnp.exp(cum)
    g_prev = jnp.exp(cum - lw)
    g_inv = jnp.exp(-cum)
    g_end = jnp.exp(cum[:, c - 1:c, :] - cum)
    kap_t, k_t, b_t, r_t = kap * g_prev, k * g_inv, b * g_inv, r * g_in
    a_ub = jnp.where(strict, bdot(kap_t, b_t, "nt"), 0.0)
    a_vk = jnp.where(strict, bdot(kap_t, k_t, "nt"), 0.0)
    rhs = -(bdot(kap_t, st0, "nn") + bdot(a_vk, v, "nn"))
    eye = diag.astype(F32)
    m = -a_ub
    inv = eye + m
    n = 1
    while n * 2 < c:
        m = bdot(m, m, "nn")
        inv = bdot(inv, eye + m, "nn")
        n *= 2
    u = bdot(inv, rhs, "nn")
    y = (bdot(r_t, st0, "nn")
         + bdot(jnp.where(incl, bdot(r_t, k_t, "nt"), 0.0), v, "nn")
         + bdot(jnp.where(incl, bdot(r_t, b_t, "nt"), 0.0), u, "nn"))
    g_full = jnp.exp(exact_dot(lw, jnp.ones((h, c, st0.shape[2]), F32), "tn", "b"))
    st1 = g_full * st0 + bdot(k * g_end, v, "tn") + bdot(b * g_end, u, "tn")
    return y, st1


def ssd_chunk_fn(group, h0, xs, bm, cm, dt, a_log, d_skip):
    q = xs.shape[0]
    causal, _, _ = _tri_masks(q)
    a_row = -jnp.exp(a_log)
    cs_all = exact_dot(causal.astype(F32), dt * a_row, "nn", "a")
    cs_t = cs_all.T
    cb = bdot(cm, bm, "nt")
    ys, hs = [], []
    p = SSD_HEAD_DIM
    for e in range(SSD_HEADS_PER_GROUP):
        hl = group * SSD_HEADS_PER_GROUP + e
        cs = cs_all[:, hl:hl + 1]
        seg = cs - cs_t[hl:hl + 1, :]
        lmat = jnp.where(causal, jnp.exp(jnp.where(causal, seg, 0.0)), 0.0)
        x_h = xs[:, e * p:(e + 1) * p]
        xdt = x_h * dt[:, hl:hl + 1]
        h0_e = h0[e * p:(e + 1) * p, :]
        cl = cs[q - 1:q, :]
        y = bdot(cb * lmat, xdt, "nn") + bdot(cm, h0_e, "nt") * jnp.exp(cs) + x_h * d_skip[:, hl:hl + 1]
        hs.append(h0_e * jnp.exp(cl) + bdot(xdt * jnp.exp(cl - cs), bm, "tn"))
        ys.append(y)
    return jnp.concatenate(ys, axis=-1), jnp.concatenate(hs, axis=0)


class Rows:
    def __init__(self, arr, w=None, cb=0, shifts=()):
        self.arr, self.w, self.cb, self.shifts = arr, (arr.shape[1] if w is None else w), cb, tuple(shifts)


def _as_rows(x):
    return x if isinstance(x, Rows) else Rows(x)


def _shift_down(x, halo, k):
    rolled = pltpu.roll(x, k, 0)
    first = rolled[0:SUBLANES]
    rid = lax.broadcasted_iota(jnp.int32, first.shape, 0)
    patched = jnp.where(rid < k, pltpu.roll(halo, k, 0), first)
    return jnp.concatenate([patched, rolled[SUBLANES:]], axis=0)


def _shift_up(g, carry, k):
    tm = g.shape[0]
    rolled = pltpu.roll(g, tm - k, 0)
    last = rolled[tm - SUBLANES:]
    rid = lax.broadcasted_iota(jnp.int32, last.shape, 0)
    patched = jnp.where(rid >= SUBLANES - k, pltpu.roll(carry, SUBLANES - k, 0), last)
    return jnp.concatenate([rolled[:tm - SUBLANES], patched], axis=0)


def _params():
    return pltpu.CompilerParams(vmem_limit_bytes=VMEM_LIMIT)


def _load_rows(refs, pos, rins, first_block):
    vals = []
    for r in rins:
        x = refs[pos][...].astype(F32) if refs[pos].dtype != F32 else refs[pos][...]
        pos += 1
        vals.append(x)
        if r.shifts:
            halo = refs[pos][...]
            pos += 1
            halo = jnp.where(first_block, jnp.zeros_like(halo), halo)
            for k in r.shifts:
                vals.append(_shift_down(x, halo, k))
    return vals, pos


def _row_specs(rins, tm, blk):
    specs, args = [], []
    for r in rins:
        specs.append(pl.BlockSpec((tm, r.w), lambda i, cb=r.cb: (blk(i), cb)))
        args.append(r.arr)
        if r.shifts:
            per = tm // SUBLANES
            specs.append(pl.BlockSpec((SUBLANES, r.w), lambda i, cb=r.cb: (jnp.maximum(blk(i) * per - 1, 0), cb)))
            args.append(r.arr)
    return specs, args


def rowwise_fwd(name, fn, rins, params, outs, tm, deps=()):
    rins = [_as_rows(r) for r in rins]
    t = rins[0].arr.shape[0]
    tm = min(tm, t)
    nb = t // tm
    specs, args = _row_specs(rins, tm, lambda i: i)
    for p in params:
        specs.append(pl.BlockSpec(p.shape, lambda i: (0, 0)))
        args.append(p)
    for dep in deps:
        specs.append(pl.BlockSpec(dep.shape, lambda i: (0, 0)))
        args.append(dep)
    n_in = len(args)

    def body(*refs):
        vals, pos = _load_rows(refs, 0, rins, pl.program_id(0) == 0)
        pv = [refs[pos + j][...] for j in range(len(params))]
        res = fn(*vals, *pv)
        for o_ref, o in zip(refs[n_in:], res):
            o_ref[...] = o.astype(o_ref.dtype)

    return pl.pallas_call(
        body, name=name, grid=(nb,), in_specs=specs,
        out_specs=[pl.BlockSpec((tm, w), lambda i: (i, 0)) for w, _ in outs],
        out_shape=[jax.ShapeDtypeStruct((t, w), dt) for w, dt in outs],
        compiler_params=_params(),
    )(*args)


def rowwise_bwd(name, fn, rins, params, cts, tm, grad_dtypes, row_add=None):
    rins = [_as_rows(r) for r in rins]
    cts = [[_as_rows(c) for c in lst] for lst in cts]
    row_add = [_as_rows(a) for a in (row_add or [])]
    t = rins[0].arr.shape[0]
    tm = min(tm, t)
    nb = t // tm
    rev = lambda i: nb - 1 - i
    specs, args = _row_specs(rins, tm, rev)
    for p in params:
        specs.append(pl.BlockSpec(p.shape, lambda i: (0, 0)))
        args.append(p)
    flat_cts = [c for lst in cts for c in lst] + row_add
    for c in flat_cts:
        specs.append(pl.BlockSpec((tm, c.w), lambda i, cb=c.cb: (rev(i), cb)))
        args.append(c.arr)
    n_in = len(args)
    want = [i for i, d in enumerate(grad_dtypes) if d is not None]
    out_specs = [pl.BlockSpec((tm, rins[i].w), lambda i_: (rev(i_), 0)) for i in want]
    out_shape = [jax.ShapeDtypeStruct((t, rins[i].w), grad_dtypes[i]) for i in want]
    out_specs += [pl.BlockSpec(p.shape, lambda i: (0, 0)) for p in params]
    out_shape += [jax.ShapeDtypeStruct(p.shape, F32) for p in params]
    n_out = len(out_shape)
    scratch = [pltpu.VMEM((SUBLANES, r.w), F32) for r in rins for _ in r.shifts]

    def body(*refs):
        i = pl.program_id(0)
        vals, pos = _load_rows(refs, 0, rins, rev(i) == 0)
        pv = [refs[pos + j][...] for j in range(len(params))]
        pos += len(params)
        outs, vjp = jax.vjp(fn, *vals, *pv)
        ct_vals = []
        for o, lst in zip(outs, cts):
            acc = None
            for _ in lst:
                cv = refs[pos][...].astype(F32)
                pos += 1
                acc = cv if acc is None else acc + cv
            ct_vals.append(acc.astype(o.dtype))
        adds = [refs[pos + j][...].astype(F32) for j in range(len(row_add))]
        grads = vjp(tuple(ct_vals))
        out_refs = refs[n_in:n_in + n_out]
        carry_refs = refs[n_in + n_out:]

        @pl.when(i == 0)
        def _():
            for cr in carry_refs:
                cr[...] = jnp.zeros_like(cr)
            for pr in out_refs[len(want):]:
                pr[...] = jnp.zeros_like(pr)

        gi, ci, oi = 0, 0, 0
        for idx, r in enumerate(rins):
            d = grads[gi]
            gi += 1
            for k in r.shifts:
                dk = grads[gi]
                gi += 1
                d = d + _shift_up(dk, carry_refs[ci][...], k)
                carry_refs[ci][...] = dk[0:SUBLANES]
                ci += 1
            if idx == 0:
                for a in adds:
                    d = d + a
            if grad_dtypes[idx] is not None:
                out_refs[oi][...] = d.astype(out_refs[oi].dtype)
                oi += 1
        for pr, gp in zip(out_refs[len(want):], grads[gi:]):
            pr[...] += gp

    res = pl.pallas_call(
        body, name=name, grid=(nb,), in_specs=specs, out_specs=out_specs, out_shape=out_shape,
        scratch_shapes=scratch, compiler_params=_params(),
    )(*args)
    return res[:len(want)], res[len(want):]


def _pick(n, pref):
    for c in pref:
        if n % c == 0:
            return c
    return n


MM_VMEM_BUDGET = 40 * 1024 * 1024
MM_PEAK_FLOPS = 0.9e15
MM_HBM_BYTES_PER_S = 3.0e12
MM_STEP_SECONDS = 0.35e-6


def _mm_tiles(m, n, k, size_a, size_b, size_out, size_res, single_k):
    best = None
    for tk in sorted({c for c in (k, 2048, 1024, 512, 256, 128) if c <= 2048 and k % c == 0}, reverse=True):
        for tm in (1024, 512, 256, 128):
            if m % tm:
                continue
            for tn in (1024, 768, 512, 384, 256, 128):
                if n % tn:
                    continue
                nk = k // tk
                vmem = 2 * (tm * tk * size_a + tk * tn * size_b + tm * tn * (size_out + size_res))
                vmem += tm * tn * 4 * (2 if nk > 1 or not single_k else 1)
                vmem += (tm * tk * 2 if size_a > 2 else 0) + (tk * tn * 2 if size_b > 2 else 0)
                if vmem > MM_VMEM_BUDGET:
                    continue
                steps = (m // tm) * (n // tn) * nk
                a_reads = 1 if (nk == 1 and single_k) else n // tn
                traffic = m * k * size_a * a_reads + k * n * size_b * (m // tm) + m * n * (size_out + size_res)
                cost = max(2.0 * m * n * k / MM_PEAK_FLOPS, traffic / MM_HBM_BYTES_PER_S) + steps * MM_STEP_SECONDS
                if best is None or cost < best[0]:
                    best = (cost, tm, tn, tk)
    return best[1:]


def mm(name, a, b, mode, out_dtype=F32, res=None, b_slabs=None, out_slabs=None, ex=None, dep=None):
    if mode == "tn":
        k_dim, m_dim = a.shape
    else:
        m_dim, k_dim = a.shape
    if b_slabs:
        n_dim = b.shape[0] * b.shape[2] if mode == "nn" else b.shape[1]
    else:
        n_dim = b.shape[0] if mode == "nt" else b.shape[1]
    n_slabs = out_slabs or (b_slabs if (b_slabs and mode == "nn") else 1)
    k_slabs = b_slabs if (b_slabs and mode == "nt") else 1
    tm, tn, tk = _mm_tiles(m_dim, n_dim // n_slabs, k_dim // k_slabs, a.dtype.itemsize, b.dtype.itemsize,
                           jnp.dtype(out_dtype).itemsize, 0 if res is None else res.dtype.itemsize,
                           single_k=(k_slabs == 1))
    nji = n_dim // n_slabs // tn
    nki = k_dim // k_slabs // tk
    nblk = lambda js, j: js * nji + j
    kblk = lambda ks, k: ks * nki + k
    if mode == "tn":
        a_spec = pl.BlockSpec((tk, tm), lambda i, js, j, ks, k: (kblk(ks, k), i))
    else:
        a_spec = pl.BlockSpec((tm, tk), lambda i, js, j, ks, k: (i, kblk(ks, k)))
    if b_slabs and mode == "nn":
        b_spec = pl.BlockSpec((None, tk, tn), lambda i, js, j, ks, k: (js, k, j))
    elif b_slabs and mode == "nt":
        b_spec = pl.BlockSpec((None, tn, tk), lambda i, js, j, ks, k: (ks, nblk(js, j), k))
    elif mode == "nt":
        b_spec = pl.BlockSpec((tn, tk), lambda i, js, j, ks, k: (nblk(js, j), kblk(ks, k)))
    else:
        b_spec = pl.BlockSpec((tk, tn), lambda i, js, j, ks, k: (kblk(ks, k), nblk(js, j)))
    specs, args = [a_spec, b_spec], [a, b]
    if res is not None:
        specs.append(pl.BlockSpec((tm, tn), lambda i, js, j, ks, k: (i, nblk(js, j))))
        args.append(res)
    if dep is not None:
        specs.append(pl.BlockSpec(dep.shape, lambda i, js, j, ks, k: (0, 0)))
        args.append(dep)
    if out_slabs:
        o_spec = pl.BlockSpec((None, tm, tn), lambda i, js, j, ks, k: (js, i, j))
        o_shape = jax.ShapeDtypeStruct((out_slabs, m_dim, n_dim // out_slabs), out_dtype)
    else:
        o_spec = pl.BlockSpec((tm, tn), lambda i, js, j, ks, k: (i, nblk(js, j)))
        o_shape = jax.ShapeDtypeStruct((m_dim, n_dim), out_dtype)

    one_k_step = k_slabs * nki == 1
    n_in = len(args)

    def body(*refs):
        a_ref, b_ref, o_ref = refs[0], refs[1], refs[n_in]
        part = _dg(a_ref[...].astype(BF16), b_ref[...].astype(BF16), mode)

        def finish(out):
            if res is not None:
                out = out + refs[2][...].astype(F32)
            o_ref[...] = out.astype(o_ref.dtype)

        if one_k_step:
            finish(part)
            return
        acc_ref = refs[n_in + 1]
        ks, kk = pl.program_id(3), pl.program_id(4)

        @pl.when((ks == 0) & (kk == 0))
        def _():
            acc_ref[...] = part

        @pl.when((ks > 0) | (kk > 0))
        def _():
            acc_ref[...] += part

        pl.when((ks == k_slabs - 1) & (kk == nki - 1))(lambda: finish(acc_ref[...]))

    grid = (m_dim // tm, n_slabs, nji, k_slabs, nki)
    scratch = [] if one_k_step else [pltpu.VMEM((tm, tn), F32)]
    first = lambda: functools.reduce(lambda p, q: p & q, [pl.program_id(ax) == 0 for ax in range(5)])
    last = lambda: functools.reduce(lambda p, q: p & q, [pl.program_id(ax) == grid[ax] - 1 for ax in range(5)])
    ex_in_specs, ex_args, ex_out_specs, ex_out_shape, ex_scratch = _ex_parts(ex)
    out = pl.pallas_call(
        _with_exchange(body, n_in, 1, len(scratch), ex, first, last), name=name, grid=grid,
        in_specs=specs + ex_in_specs, out_specs=[o_spec] + ex_out_specs, out_shape=[o_shape] + ex_out_shape,
        scratch_shapes=scratch + ex_scratch,
        compiler_params=pltpu.CompilerParams(
            dimension_semantics=("parallel", "parallel", "parallel", "arbitrary", "arbitrary") if ex is None
            else ("arbitrary",) * 5,
            vmem_limit_bytes=VMEM_LIMIT, has_side_effects=ex is not None),
    )(*args, *ex_args)
    return out[0] if ex is None else (out[0], out[1:])


def _stack_lanes(x, n):
    w = x.shape[1] // n
    return jnp.stack([x[:, i * w:(i + 1) * w] for i in range(n)])


def _stack_rows(x, n):
    w = x.shape[0] // n
    return jnp.stack([x[i * w:(i + 1) * w, :] for i in range(n)])


def _with_exchange(body_fn, n_in, n_out, n_scratch, ex, first_step, last_step):
    if ex is None:
        return body_fn

    def body(*refs):
        n = ex.n
        ins, ex_ins = refs[:n_in], refs[n_in:n_in + n]
        outs = refs[n_in + n:n_in + n + n_out]
        ex_outs = refs[n_in + n + n_out:n_in + 2 * n + n_out]
        scratch = refs[n_in + 2 * n + n_out:n_in + 2 * n + n_out + n_scratch]
        sems = refs[n_in + 2 * n + n_out + n_scratch:]
        pl.when(first_step())(lambda: ex.start(ex_ins, ex_outs, sems))
        body_fn(*ins, *outs, *scratch)
        pl.when(last_step())(lambda: ex.finish(ex_ins, ex_outs, sems))

    return body


def _grid2_ends(n0, n1):
    first = lambda: (pl.program_id(0) == 0) & (pl.program_id(1) == 0)
    last = lambda: (pl.program_id(0) == n0 - 1) & (pl.program_id(1) == n1 - 1)
    return first, last


def _ex_parts(ex):
    if ex is None:
        return [], [], [], [], []
    return ex.specs, list(ex.xs), ex.specs, ex.out_shape, ex.scratch


def rwkv_scan_fwd(r, lw, k, v, kap, b, ex=None):
    t = r.shape[0]
    c, hps, hd = min(RWKV_CHUNK, t), RWKV_HEADS_PER_STEP, RWKV_HEAD_DIM
    nc, ng, wl = t // c, RWKV_HEADS // hps, hps * hd
    spec = pl.BlockSpec((c, wl), lambda g, ci: (ci, g))

    def body(r_ref, lw_ref, k_ref, v_ref, kap_ref, b_ref, y_ref, ck_ref, st_ref):
        @pl.when(pl.program_id(1) == 0)
        def _():
            st_ref[...] = jnp.zeros_like(st_ref)

        st = st_ref[...]
        ck_ref[...] = st
        ins = [x[...] for x in (r_ref, lw_ref, k_ref, v_ref, kap_ref, b_ref)]
        y, st1 = rwkv_chunk_fn(_stack_rows(st, hps), *[_stack_lanes(x, hps) for x in ins])
        y_ref[...] = jnp.concatenate([y[h] for h in range(hps)], axis=-1)
        st_ref[...] = jnp.concatenate([st1[h] for h in range(hps)], axis=0)

    ex_in_specs, ex_args, ex_out_specs, ex_out_shape, ex_scratch = _ex_parts(ex)
    res = pl.pallas_call(
        _with_exchange(body, 6, 2, 1, ex, *_grid2_ends(ng, nc)), name="rwkv_scan_fwd", grid=(ng, nc),
        in_specs=[spec] * 6 + ex_in_specs,
        out_specs=[spec, pl.BlockSpec((None, wl, hd), lambda g, ci: (ci, g, 0))] + ex_out_specs,
        out_shape=[jax.ShapeDtypeStruct((t, RWKV_WIDTH), F32),
                   jax.ShapeDtypeStruct((nc, RWKV_WIDTH, hd), F32)] + ex_out_shape,
        scratch_shapes=[pltpu.VMEM((wl, hd), F32)] + ex_scratch,
        compiler_params=pltpu.CompilerParams(vmem_limit_bytes=VMEM_LIMIT, has_side_effects=ex is not None),
    )(r, lw, k, v, kap, b, *ex_args)
    return res[0], res[1], res[2:]


def rwkv_scan_bwd(r, lw, k, v, kap, b, ck, dy, ex=None):
    t = r.shape[0]
    c, hps, hd = min(RWKV_CHUNK, t), RWKV_HEADS_PER_STEP, RWKV_HEAD_DIM
    nc, ng, wl = t // c, RWKV_HEADS // hps, hps * hd
    spec = pl.BlockSpec((c, wl), lambda g, ci: (nc - 1 - ci, g))

    def body(r_ref, lw_ref, k_ref, v_ref, kap_ref, b_ref, ck_ref, dy_ref, *rest):
        out_refs, dst_ref = rest[:6], rest[6]

        @pl.when(pl.program_id(1) == 0)
        def _():
            dst_ref[...] = jnp.zeros_like(dst_ref)

        ins = [x[...] for x in (r_ref, lw_ref, k_ref, v_ref, kap_ref, b_ref)]
        dyv, ck, dst = dy_ref[...], ck_ref[...], dst_ref[...]
        _, vjp = jax.vjp(rwkv_chunk_fn, _stack_rows(ck, hps), *[_stack_lanes(x, hps) for x in ins])
        grads = vjp((_stack_lanes(dyv, hps), _stack_rows(dst, hps)))
        dst_ref[...] = jnp.concatenate([grads[0][h] for h in range(hps)], axis=0)
        for j in range(6):
            out_refs[j][...] = jnp.concatenate([grads[1 + j][h] for h in range(hps)], axis=-1)

    ex_in_specs, ex_args, ex_out_specs, ex_out_shape, ex_scratch = _ex_parts(ex)
    res = pl.pallas_call(
        _with_exchange(body, 8, 6, 1, ex, *_grid2_ends(ng, nc)), name="rwkv_scan_bwd", grid=(ng, nc),
        in_specs=[spec] * 6 + [pl.BlockSpec((None, wl, hd), lambda g, ci: (nc - 1 - ci, g, 0)), spec] + ex_in_specs,
        out_specs=[spec] * 6 + ex_out_specs,
        out_shape=[jax.ShapeDtypeStruct((t, RWKV_WIDTH), F32)] * 6 + ex_out_shape,
        scratch_shapes=[pltpu.VMEM((wl, hd), F32)] + ex_scratch,
        compiler_params=pltpu.CompilerParams(vmem_limit_bytes=VMEM_LIMIT, has_side_effects=ex is not None),
    )(r, lw, k, v, kap, b, ck, dy, *ex_args)
    return res[:6], res[6:]


def _ssd_specs(q, blk):
    gw = SSD_WIDTH // 2
    return [pl.BlockSpec((q, gw), lambda g, ci: (blk(ci), g)),
            pl.BlockSpec((q, SSD_STATE), lambda g, ci: (blk(ci), g)),
            pl.BlockSpec((q, SSD_STATE), lambda g, ci: (blk(ci), g)),
            pl.BlockSpec((q, LANES), lambda g, ci: (blk(ci), 0)),
            pl.BlockSpec((1, LANES), lambda g, ci: (0, 0)),
            pl.BlockSpec((1, LANES), lambda g, ci: (0, 0))]


def ssd_scan_fwd(xs, bm, cm, dt, a_log, d_skip):
    t = xs.shape[0]
    q = min(SSD_CHUNK, t)
    nc, gw = t // q, SSD_WIDTH // 2

    def body(xs_ref, bm_ref, cm_ref, dt_ref, al_ref, d_ref, y_ref, ck_ref, h_ref):
        @pl.when(pl.program_id(1) == 0)
        def _():
            h_ref[...] = jnp.zeros_like(h_ref)

        ck_ref[...] = h_ref[...]
        args = (h_ref[...], xs_ref[...], bm_ref[...], cm_ref[...], dt_ref[...], al_ref[...], d_ref[...])
        g = pl.program_id(0)

        @pl.when(g == 0)
        def _():
            y, h1 = ssd_chunk_fn(0, *args)
            y_ref[...] = y
            h_ref[...] = h1

        @pl.when(g == 1)
        def _():
            y, h1 = ssd_chunk_fn(1, *args)
            y_ref[...] = y
            h_ref[...] = h1

    return pl.pallas_call(
        body, name="ssd_scan_fwd", grid=(2, nc), in_specs=_ssd_specs(q, lambda ci: ci),
        out_specs=[pl.BlockSpec((q, gw), lambda g, ci: (ci, g)),
                   pl.BlockSpec((None, gw, SSD_STATE), lambda g, ci: (ci, g, 0))],
        out_shape=[jax.ShapeDtypeStruct((t, SSD_WIDTH), F32), jax.ShapeDtypeStruct((nc, SSD_WIDTH, SSD_STATE), F32)],
        scratch_shapes=[pltpu.VMEM((gw, SSD_STATE), F32)], compiler_params=_params(),
    )(xs, bm, cm, dt, a_log, d_skip)


def ssd_scan_bwd(xs, bm, cm, dt, a_log, d_skip, ck, dy):
    t = xs.shape[0]
    q = min(SSD_CHUNK, t)
    nc, gw = t // q, SSD_WIDTH // 2
    rev = lambda ci: nc - 1 - ci

    def body(xs_ref, bm_ref, cm_ref, dt_ref, al_ref, d_ref, ck_ref, dy_ref,
             dxs_ref, dbm_ref, dcm_ref, ddt_ref, dal_ref, dd_ref, dh_ref):
        g, ci = pl.program_id(0), pl.program_id(1)

        @pl.when(ci == 0)
        def _():
            dh_ref[...] = jnp.zeros_like(dh_ref)

        @pl.when((ci == 0) & (g == 0))
        def _():
            dal_ref[...] = jnp.zeros_like(dal_ref)
            dd_ref[...] = jnp.zeros_like(dd_ref)

        args = (ck_ref[...], xs_ref[...], bm_ref[...], cm_ref[...], dt_ref[...], al_ref[...], d_ref[...])

        def run(group):
            _, vjp = jax.vjp(functools.partial(ssd_chunk_fn, group), *args)
            dh0, dxs, dbm, dcm, ddt, dal, dd = vjp((dy_ref[...], dh_ref[...]))
            dh_ref[...] = dh0
            dxs_ref[...] = dxs
            dbm_ref[...] = dbm
            dcm_ref[...] = dcm
            ddt_ref[...] = ddt
            dal_ref[...] += dal
            dd_ref[...] += dd

        pl.when(g == 0)(lambda: run(0))
        pl.when(g == 1)(lambda: run(1))

    in_specs = _ssd_specs(q, rev) + [pl.BlockSpec((None, gw, SSD_STATE), lambda g, ci: (rev(ci), g, 0)),
                                     pl.BlockSpec((q, gw), lambda g, ci: (rev(ci), g))]
    return pl.pallas_call(
        body, name="ssd_scan_bwd", grid=(2, nc), in_specs=in_specs,
        out_specs=[pl.BlockSpec((q, gw), lambda g, ci: (rev(ci), g)),
                   pl.BlockSpec((q, SSD_STATE), lambda g, ci: (rev(ci), g)),
                   pl.BlockSpec((q, SSD_STATE), lambda g, ci: (rev(ci), g)),
                   pl.BlockSpec((None, q, LANES), lambda g, ci: (g, rev(ci), 0)),
                   pl.BlockSpec((1, LANES), lambda g, ci: (0, 0)),
                   pl.BlockSpec((1, LANES), lambda g, ci: (0, 0))],
        out_shape=[jax.ShapeDtypeStruct((t, SSD_WIDTH), F32), jax.ShapeDtypeStruct((t, 2 * SSD_STATE), F32),
                   jax.ShapeDtypeStruct((t, 2 * SSD_STATE), F32), jax.ShapeDtypeStruct((2, t, LANES), F32),
                   jax.ShapeDtypeStruct((1, LANES), F32), jax.ShapeDtypeStruct((1, LANES), F32)],
        scratch_shapes=[pltpu.VMEM((gw, SSD_STATE), F32)], compiler_params=_params(),
    )(xs, bm, cm, dt, a_log, d_skip, ck, dy)


def loss_and_grad(x, tgt, g, tm):
    t, d = x.shape
    tm = min(tm, t)
    nb = t // tm

    def body(x_ref, t_ref, g_ref, loss_ref, dx_ref, dg_ref):
        @pl.when(pl.program_id(0) == 0)
        def _():
            loss_ref[...] = jnp.zeros_like(loss_ref)
            dg_ref[...] = jnp.zeros_like(dg_ref)

        val, vjp = jax.vjp(loss_fn, x_ref[...], t_ref[...], g_ref[...])
        dx, _, dg = vjp(jnp.ones((1, 1), F32))
        loss_ref[...] += jnp.broadcast_to(val, loss_ref.shape)
        dx_ref[...] = dx
        dg_ref[...] += dg

    row = pl.BlockSpec((tm, d), lambda i: (i, 0))
    one = pl.BlockSpec((1, d), lambda i: (0, 0))
    return pl.pallas_call(
        body, name="loss_and_grad", grid=(nb,), in_specs=[row, row, one],
        out_specs=[pl.BlockSpec((SUBLANES, LANES), lambda i: (0, 0)), row, one],
        out_shape=[jax.ShapeDtypeStruct((SUBLANES, LANES), F32), jax.ShapeDtypeStruct((t, d), F32),
                   jax.ShapeDtypeStruct((1, d), F32)],
        compiler_params=_params(),
    )(x, tgt, g)


def adamw(name, recv, w, m, v):
    rows, cols = w.shape
    recv_block_bytes = 4 * 1024 * 1024
    tm = _pick(rows, [c for c in (256, 128, 64, 32, 16, 8) if N_DEV * c * cols * 4 <= recv_block_bytes])
    c1 = 1.0 / (1.0 - ADAM_B1 ** ADAM_STEP)
    c2 = 1.0 / (1.0 - ADAM_B2 ** ADAM_STEP)

    def body(recv_ref, w_ref, m_ref, v_ref, g_ref, d_ref, nm_ref, nv_ref):
        g = recv_ref[0].astype(F32)
        for p in range(1, N_DEV):
            g = g + recv_ref[p].astype(F32)
        nm = ADAM_B1 * m_ref[...] + (1.0 - ADAM_B1) * g
        nv = ADAM_B2 * v_ref[...] + (1.0 - ADAM_B2) * jnp.square(g)
        g_ref[...] = g
        nm_ref[...] = nm
        nv_ref[...] = nv
        d_ref[...] = -ADAM_LR * ((nm * c1) / (jnp.sqrt(nv * c2) + ADAM_EPS) + ADAM_WD * w_ref[...])

    blk = pl.BlockSpec((tm, cols), lambda i: (i, 0))
    return pl.pallas_call(
        body, name=name, grid=(rows // tm,),
        in_specs=[pl.BlockSpec((N_DEV, tm, cols), lambda i: (0, i, 0)), blk, blk, blk],
        out_specs=[blk] * 4, out_shape=[jax.ShapeDtypeStruct((rows, cols), F32)] * 4,
        compiler_params=_params(),
    )(recv, w, m, v)


def _mesh_pos():
    return lax.axis_index("x"), lax.axis_index("y"), lax.axis_index("c")


def _peer(pos, mask):
    x, y, c = pos
    return (1 - x if mask & 4 else x, 1 - y if mask & 2 else y, 1 - c if mask & 1 else c)


def _linear(pos):
    return 4 * pos[0] + 2 * pos[1] + pos[2]


class Exchange:
    def __init__(self, xs, scatter):
        self.xs, self.scatter, self.n = list(xs), scatter, len(xs)
        self.out_shape = [jax.ShapeDtypeStruct(x.shape if scatter else (N_DEV,) + x.shape, x.dtype) for x in xs]
        self.specs = [pl.BlockSpec(memory_space=pl.ANY)] * self.n
        peers = N_DEV - 1
        self.scratch = [pltpu.SemaphoreType.DMA((self.n * peers,)), pltpu.SemaphoreType.DMA((self.n * peers,)),
                        pltpu.SemaphoreType.DMA((self.n,))]

    def _copies(self, ins, outs, sems, landing):
        send_sems, recv_sems, local_sems = sems
        me = _mesh_pos()
        me_lin = _linear(me)
        local, remote = [], []
        for ti in range(self.n):
            src_mine = ins[ti].at[me_lin] if self.scatter else ins[ti]
            local.append(pltpu.make_async_copy(src_mine, outs[ti].at[me_lin], local_sems.at[ti]))
            for mask in range(1, N_DEV):
                peer = _peer(me, mask)
                src = ins[ti].at[_linear(peer)] if self.scatter else ins[ti]
                dst = outs[ti].at[_linear(peer) if landing else me_lin]
                remote.append(pltpu.make_async_remote_copy(
                    src_ref=src, dst_ref=dst, send_sem=send_sems.at[ti * (N_DEV - 1) + mask - 1],
                    recv_sem=recv_sems.at[ti * (N_DEV - 1) + mask - 1],
                    device_id=peer, device_id_type=pl.DeviceIdType.MESH))
        return local, remote

    def start(self, ins, outs, sems):
        local, remote = self._copies(ins, outs, sems, landing=False)
        for cp in local + remote:
            cp.start()

    def finish(self, ins, outs, sems):
        local, remote = self._copies(ins, outs, sems, landing=True)
        for cp in remote:
            cp.wait_recv()
        for cp in remote:
            cp.wait_send()
        for cp in local:
            cp.wait()


def exchange_call(name, xs, scatter):
    ex = Exchange(xs, scatter)
    n = ex.n

    def body(*refs):
        ins, outs, sems = refs[:n], refs[n:2 * n], refs[2 * n:]
        ex.start(ins, outs, sems)
        ex.finish(ins, outs, sems)

    return pl.pallas_call(
        body, name=name, in_specs=ex.specs, out_specs=ex.specs, out_shape=ex.out_shape, scratch_shapes=ex.scratch,
        compiler_params=pltpu.CompilerParams(has_side_effects=True),
    )(*xs)


def exchange_start(name, xs, scatter, dep=None):
    ex = Exchange(xs, scatter)
    n = ex.n
    hbm = pl.BlockSpec(memory_space=pltpu.HBM)
    sem = pl.BlockSpec(memory_space=pltpu.SEMAPHORE)
    lands = [lax.empty(s.shape, s.dtype) for s in ex.out_shape]

    n_inputs = 2 * n + (0 if dep is None else 1)

    def body(*refs):
        ins, lnd, sems, token = refs[:n], refs[n:2 * n], refs[n_inputs:n_inputs + 3], refs[-1]
        ex.start(ins, lnd, sems)
        token[...] = jnp.zeros_like(token)

    res = pl.pallas_call(
        body, name=name, in_specs=[hbm] * (2 * n) + ([] if dep is None else [pl.BlockSpec(memory_space=pl.ANY)]),
        out_specs=[sem] * 3 + [hbm] * (2 * n) + [pl.BlockSpec(memory_space=pltpu.VMEM)],
        out_shape=ex.scratch + [pltpu.HBM(x.shape, x.dtype) for x in xs]
        + [pltpu.HBM(s.shape, s.dtype) for s in ex.out_shape] + [jax.ShapeDtypeStruct((SUBLANES, LANES), F32)],
        input_output_aliases={i: 3 + i for i in range(2 * n)},
        compiler_params=pltpu.CompilerParams(has_side_effects=pltpu.SideEffectType.DATAFLOW_SIDE_EFFECTING),
    )(*[pltpu.with_memory_space_constraint(x, pltpu.HBM) for x in list(xs) + lands], *([] if dep is None else [dep]))
    return (ex, res[:3], res[3:3 + n], res[3 + n:3 + 2 * n]), res[-1]


def exchange_wait(name, handles, after):
    ex, sems, srcs, lands = handles
    n = ex.n
    hbm = pl.BlockSpec(memory_space=pltpu.HBM)
    sem = pl.BlockSpec(memory_space=pltpu.SEMAPHORE)

    def body(*refs):
        ins, lnd, sem_refs = refs[:n], refs[n:2 * n], refs[2 * n:2 * n + 3]
        ex.finish(ins, lnd, sem_refs)

    res = pl.pallas_call(
        body, name=name, in_specs=[hbm] * (2 * n) + [sem] * 3 + [pl.BlockSpec(memory_space=pl.ANY)],
        out_specs=[hbm] * (2 * n),
        out_shape=[pltpu.HBM(x.shape, x.dtype) for x in srcs] + [pltpu.HBM(x.shape, x.dtype) for x in lands],
        input_output_aliases={i: i for i in range(2 * n)},
        compiler_params=pltpu.CompilerParams(has_side_effects=pltpu.SideEffectType.DATAFLOW_SIDE_EFFECTING),
    )(*srcs, *lands, *sems, after)
    return res[n:]


def all_gather_many(name, xs):
    return exchange_call(name, xs, scatter=False)


def scatter_many(name, xs):
    return exchange_call(name, xs, scatter=True)


_Z = (0, 1024)
_XBC = (1024, 2560)
_DT = (2560, 2576)
_RKV = (2576, 5648)
_PW = (5648, 5744)
_PA = (5744, 5840)
_PG = (5840, 6096)
D_IN = 6096

_SMALL = ("norm_mix_g", "ssd_conv_b", "ssd_dt_bias", "ssd_a_log", "ssd_d", "ssd_norm_g", "rwkv_mu", "rwkv_w0",
          "rwkv_a0", "rwkv_k_k", "rwkv_k_a", "rwkv_r_k", "rwkv_ln_w", "rwkv_ln_b", "norm_x_g", "norm_mem_g",
          "norm_ffn_g", "final_norm_g")
_WEIGHTS = ("norm_mix_g", "w_in", "ssd_conv_w", "ssd_conv_b", "ssd_dt_bias", "ssd_a_log", "ssd_d", "ssd_norm_g",
            "rwkv_mu", "rwkv_w0", "rwkv_w2", "rwkv_a0", "rwkv_a2", "rwkv_g2", "rwkv_k_k", "rwkv_k_a", "rwkv_r_k",
            "rwkv_ln_w", "rwkv_ln_b", "w_out", "norm_x_g", "norm_mem_g", "xattn_wq", "xattn_wk", "xattn_wv",
            "xattn_wo", "norm_ffn_g", "ffn_w1", "ffn_w2", "final_norm_g")


def _pad_lanes(x, width=LANES):
    return jnp.pad(x, ((0, 0), (0, width - x.shape[1])))


def _pack_small(vals):
    flat = jnp.concatenate([vals[n].reshape(-1) for n in _SMALL])
    rows = -(-flat.shape[0] // (LANES * SUBLANES)) * SUBLANES
    return jnp.pad(flat, (0, rows * LANES - flat.shape[0])).reshape(rows, LANES)


def _unpack_small(packed, shapes):
    flat = packed.reshape(-1)
    out, pos = {}, 0
    for n in _SMALL:
        size = 1
        for s in shapes[n]:
            size *= s
        out[n] = flat[pos:pos + size].reshape(shapes[n])
        pos += size
    return out


def _cols(w, rng):
    return w[:, rng[0]:rng[1]]


def kernel(x, mem, norm_mix_g, w_in, ssd_conv_w, ssd_conv_b, ssd_dt_bias, ssd_a_log, ssd_d, ssd_norm_g, rwkv_mu, rwkv_w0, rwkv_w2, rwkv_a0, rwkv_a2, rwkv_g2, rwkv_k_k, rwkv_k_a, rwkv_r_k, rwkv_ln_w, rwkv_ln_b, w_out, norm_x_g, norm_mem_g, xattn_wq, xattn_wk, xattn_wv, xattn_wo, norm_ffn_g, ffn_w1, ffn_w2, final_norm_g, loss_target, m_norm_mix_g, m_w_in, m_ssd_conv_w, m_ssd_conv_b, m_ssd_dt_bias, m_ssd_a_log, m_ssd_d, m_ssd_norm_g, m_rwkv_mu, m_rwkv_w0, m_rwkv_w2, m_rwkv_a0, m_rwkv_a2, m_rwkv_g2, m_rwkv_k_k, m_rwkv_k_a, m_rwkv_r_k, m_rwkv_ln_w, m_rwkv_ln_b, m_w_out, m_norm_x_g, m_norm_mem_g, m_xattn_wq, m_xattn_wk, m_xattn_wv, m_xattn_wo, m_norm_ffn_g, m_ffn_w1, m_ffn_w2, m_final_norm_g, v_norm_mix_g, v_w_in, v_ssd_conv_w, v_ssd_conv_b, v_ssd_dt_bias, v_ssd_a_log, v_ssd_d, v_ssd_norm_g, v_rwkv_mu, v_rwkv_w0, v_rwkv_w2, v_rwkv_a0, v_rwkv_a2, v_rwkv_g2, v_rwkv_k_k, v_rwkv_k_a, v_rwkv_r_k, v_rwkv_ln_w, v_rwkv_ln_b, v_w_out, v_norm_x_g, v_norm_mem_g, v_xattn_wq, v_xattn_wk, v_xattn_wv, v_xattn_wo, v_norm_ffn_g, v_ffn_w1, v_ffn_w2, v_final_norm_g):
    given = dict(locals())
    wts = {n: given[n] for n in _WEIGHTS}
    mom_m = {n: given["m_" + n] for n in _WEIGHTS}
    mom_v = {n: given["v_" + n] for n in _WEIGHTS}
    d = D_MODEL
    xt, memt, tgt = x[0], mem[0], loss_target[0]
    tm = 256

    big = {"w_in": w_in[0], "w_out": w_out[0], "xattn_wq": xattn_wq[0], "xattn_wk": xattn_wk[0],
           "xattn_wv": xattn_wv[0], "xattn_wo": xattn_wo[0], "ffn_w1": ffn_w1[0], "ffn_w2": ffn_w2[0]}
    small_sh = {"ssd_conv_w": ssd_conv_w.reshape(4, -1), "rwkv_w2": rwkv_w2[0], "rwkv_a2": rwkv_a2[0],
                "rwkv_g2": rwkv_g2[0]}
    cast = {n: rowwise_fwd("cast_" + n, cast_fn, [a], [], [(a.shape[1], BF16)], 256)[0] for n, a in big.items()}
    gathered = all_gather_many("gather_weights", [cast["w_in"]] + list(small_sh.values()))
    g_big = {"w_in": gathered[0]}
    g_small = dict(zip(small_sh, gathered[1:]))
    late_a = ("w_out", "xattn_wq", "xattn_wk", "xattn_wv", "xattn_wo")
    late_b = ("ffn_w1", "ffn_w2")
    gather_a, token_a = exchange_start("gather_attn_start", [cast[n] for n in late_a], scatter=False, dep=gathered[0])
    gather_b, token_b = exchange_start("gather_ffn_start", [cast[n] for n in late_b], scatter=False, dep=token_a)

    w_in_full = jnp.transpose(g_big["w_in"], (1, 0, 2)).reshape(d, D_IN)
    w_z, w_xbc, w_rkv, w_pg = (_cols(w_in_full, r) for r in (_Z, _XBC, _RKV, _PG))
    w_sm = jnp.concatenate([_pad_lanes(_cols(w_in_full, r)) for r in (_PW, _PA, _DT)], axis=1)
    unshard_cols = lambda g: jnp.transpose(g, (1, 0, 2)).reshape(g.shape[1], -1)
    conv_w_f = unshard_cols(g_small["ssd_conv_w"])
    pad_rows = lambda a: jnp.pad(a, ((0, LANES - a.shape[0]), (0, 0)))
    w2p, a2p = pad_rows(unshard_cols(g_small["rwkv_w2"])), pad_rows(unshard_cols(g_small["rwkv_a2"]))
    g2_f = unshard_cols(g_small["rwkv_g2"])

    mu = rwkv_mu
    mu_rkv, mu_pg = mu[:, :3072], mu[:, 3264:3520]
    mu_pwa = jnp.concatenate([_pad_lanes(mu[:, 3072:3168]), _pad_lanes(mu[:, 3168:3264])], axis=1)
    dt_bias_p, a_log_p, d_p = _pad_lanes(ssd_dt_bias), _pad_lanes(ssd_a_log), _pad_lanes(ssd_d)
    r_k_row = rwkv_r_k.reshape(1, RWKV_WIDTH)
    g_final = final_norm_g.reshape(1, d)

    (h1,) = rowwise_fwd("norm_mix", rmsnorm_fn, [xt], [norm_mix_g], [(d, BF16)], tm, deps=[token_b])
    u_z = mm("in_z", h1, w_z, "nn")
    u_xbc = mm("in_xbc", h1, w_xbc, "nn")
    u_rkv = mm("in_rkv", h1, w_rkv, "nn")
    u_pg = mm("in_pg", h1, w_pg, "nn")
    u_sm = mm("in_small", h1, w_sm, "nn")

    ssd_pre_rows = lambda: [Rows(u_xbc, shifts=(1, 2, 3)), Rows(u_sm, LANES, 2)]
    ssd_pre_params = [conv_w_f, ssd_conv_b, dt_bias_p]
    xs, bm, cm, dt = rowwise_fwd("ssd_pre", ssd_pre_fn, ssd_pre_rows(), ssd_pre_params,
                                 [(SSD_WIDTH, F32), (256, F32), (256, F32), (LANES, F32)], tm)
    y_scan, ssd_ck = ssd_scan_fwd(xs, bm, cm, dt, a_log_p, d_p)
    (y_ssd,) = rowwise_fwd("ssd_post", ssd_post_fn, [y_scan, u_z], [ssd_norm_g], [(SSD_WIDTH, BF16)], tm)

    rwkv_pre_rows = lambda: [Rows(u_rkv, shifts=(1,)), Rows(u_pg, shifts=(1,)), Rows(u_sm, 2 * LANES, 0, shifts=(1,))]
    rwkv_pre_params = [mu_rkv, mu_pg, mu_pwa, rwkv_w0, w2p, rwkv_a0, a2p, g2_f, rwkv_k_k, rwkv_k_a]
    r_, lw_, k_, v_, kap_, b_, gate_ = rowwise_fwd("rwkv_pre", rwkv_pre_fn, rwkv_pre_rows(), rwkv_pre_params,
                                                   [(RWKV_WIDTH, F32)] * 7, 128)
    ys_r, rwkv_ck, _ = rwkv_scan_fwd(r_, lw_, k_, v_, kap_, b_)
    g_big.update(zip(late_a, exchange_wait("gather_attn_wait", gather_a, after=ys_r)))
    w_out_f = g_big["w_out"].reshape(d, d)
    wq_f, wk_f, wv_f, wo_f = (g_big[n].reshape(d, d) for n in ("xattn_wq", "xattn_wk", "xattn_wv", "xattn_wo"))
    rwkv_post_params = [rwkv_ln_w, rwkv_ln_b, r_k_row]
    (y_rwkv,) = rowwise_fwd("rwkv_post", rwkv_post_fn, [ys_r, r_, k_, v_, gate_], rwkv_post_params,
                            [(RWKV_WIDTH, BF16)], tm)
    ycat = jnp.concatenate([y_ssd, y_rwkv], axis=1)
    x1 = mm("out_proj", ycat, w_out_f, "nn", res=xt)

    (h2,) = rowwise_fwd("norm_x", rmsnorm_fn, [x1], [norm_x_g], [(d, BF16)], tm)
    (mn,) = rowwise_fwd("norm_mem", rmsnorm_fn, [memt], [norm_mem_g], [(d, BF16)], tm)
    q = mm("xattn_q", h2, wq_f, "nn", out_dtype=BF16)
    kx = mm("xattn_k", mn, wk_f, "nn")
    vx = mm("xattn_v", mn, wv_f, "nn")
    (o,) = rowwise_fwd("xattn", attn_fn, [q], [kx, vx], [(d, BF16)], tm)
    x2 = mm("xattn_o", o, wo_f, "nn", res=x1)

    (h3,) = rowwise_fwd("norm_ffn", rmsnorm_fn, [x2], [norm_ffn_g], [(d, BF16)], tm)
    w1_s, w2_g = exchange_wait("gather_ffn_wait", gather_b, after=h3)
    w2_f = w2_g.reshape(D_FF, d)
    a_ffn = mm("ffn_1", h3, w1_s, "nn", b_slabs=N_DEV)
    (hid,) = rowwise_fwd("relu2", relu2_fn, [a_ffn], [], [(D_FF, BF16)], tm)
    x3 = mm("ffn_2", hid, w2_f, "nn", res=x2)

    loss_blk, dx3, dg_final = loss_and_grad(x3, tgt, g_final, tm)

    grads = {}
    grads["ffn_w2"] = mm("d_ffn_w2", hid, dx3, "tn", out_dtype=BF16).reshape(N_DEV, D_FF // N_DEV, d)
    sc_w2, tok = exchange_start("scatter_ffn_w2_start", [grads["ffn_w2"]], scatter=True)
    d_hid = mm("d_hid", dx3, w2_f, "nt", dep=tok)
    (da,), _ = rowwise_bwd("relu2_bwd", relu2_fn, [a_ffn], [], [[d_hid]], tm, [BF16])
    grads["ffn_w1"] = mm("d_ffn_w1", h3, da, "tn", out_dtype=BF16, out_slabs=N_DEV)
    sc_w1, tok = exchange_start("scatter_ffn_w1_start", [grads["ffn_w1"]], scatter=True)
    dh3 = mm("d_h3", da, w1_s, "nt", b_slabs=N_DEV, dep=tok)
    (dx2,), (dg_ffn,) = rowwise_bwd("norm_ffn_bwd", rmsnorm_fn, [x2], [norm_ffn_g], [[dh3]], tm, [F32], row_add=[dx3])

    grads["xattn_wo"] = mm("d_wo", o, dx2, "tn", out_dtype=BF16).reshape(N_DEV, d // N_DEV, d)
    sc_wo, tok = exchange_start("scatter_wo_start", [grads["xattn_wo"]], scatter=True)
    d_o = mm("d_o", dx2, wo_f, "nt", dep=tok)
    (dq,), (dkx, dvx) = rowwise_bwd("xattn_bwd", attn_fn, [q], [kx, vx], [[d_o]], tm, [BF16])
    grads["xattn_wq"] = mm("d_wq", h2, dq, "tn", out_dtype=BF16).reshape(N_DEV, d // N_DEV, d)
    dh2 = mm("d_h2", dq, wq_f, "nt")
    grads["xattn_wk"] = mm("d_wk", mn, dkx, "tn", out_dtype=BF16).reshape(N_DEV, d // N_DEV, d)
    grads["xattn_wv"] = mm("d_wv", mn, dvx, "tn", out_dtype=BF16).reshape(N_DEV, d // N_DEV, d)
    qkv = ("xattn_wq", "xattn_wk", "xattn_wv")
    sc_qkv, tok = exchange_start("scatter_qkv_start", [grads[n] for n in qkv], scatter=True)
    dmn = mm("d_mn_v", dvx, wv_f, "nt", res=mm("d_mn_k", dkx, wk_f, "nt", dep=tok))
    _, (dg_mem,) = rowwise_bwd("norm_mem_bwd", rmsnorm_fn, [memt], [norm_mem_g], [[dmn]], tm, [None])
    (dx1,), (dg_x,) = rowwise_bwd("norm_x_bwd", rmsnorm_fn, [x1], [norm_x_g], [[dh2]], tm, [F32], row_add=[dx2])

    grads["w_out"] = mm("d_w_out", ycat, dx1, "tn", out_dtype=BF16).reshape(N_DEV, d // N_DEV, d)
    sc_wout, tok = exchange_start("scatter_w_out_start", [grads["w_out"]], scatter=True)
    d_ycat = mm("d_ycat", dx1, w_out_f, "nt", dep=tok)

    (d_ys, d_r1, d_k1, d_v1, d_gate), (dln_w, dln_b, dr_k) = rowwise_bwd(
        "rwkv_post_bwd", rwkv_post_fn, [ys_r, r_, k_, v_, gate_], rwkv_post_params,
        [[Rows(d_ycat, RWKV_WIDTH, 1)]], tm, [F32] * 5)
    (d_r2, d_lw, d_k2, d_v2, d_kap, d_b), _ = rwkv_scan_bwd(r_, lw_, k_, v_, kap_, b_, rwkv_ck, d_ys)
    (du_rkv, du_pg, du_pwa), rwkv_pg = rowwise_bwd(
        "rwkv_pre_bwd", rwkv_pre_fn, rwkv_pre_rows(), rwkv_pre_params,
        [[d_r1, d_r2], [d_lw], [d_k1, d_k2], [d_v1, d_v2], [d_kap], [d_b], [d_gate]], 128, [BF16] * 3)
    dmu_rkv, dmu_pg, dmu_pwa, dw0, dw2p, da0, da2p, dg2, dk_k, dk_a = rwkv_pg

    (d_yscan, du_z), (dssd_norm_g,) = rowwise_bwd("ssd_post_bwd", ssd_post_fn, [y_scan, u_z], [ssd_norm_g],
                                                  [[Rows(d_ycat, SSD_WIDTH, 0)]], tm, [F32, BF16])
    dxs, dbm, dcm, ddt2, da_log_p, dd_p = ssd_scan_bwd(xs, bm, cm, dt, a_log_p, d_p, ssd_ck, d_yscan)
    (du_xbc, du_dt), (dconv_w, dconv_b, ddt_bias_p) = rowwise_bwd(
        "ssd_pre_bwd", ssd_pre_fn, ssd_pre_rows(), ssd_pre_params,
        [[dxs], [dbm], [dcm], [ddt2[0], ddt2[1]]], tm, [BF16, BF16])
    du_sm = jnp.concatenate([du_pwa, du_dt], axis=1)

    dw_z = mm("d_w_z", h1, du_z, "tn", out_dtype=BF16)
    dw_xbc = mm("d_w_xbc", h1, du_xbc, "tn", out_dtype=BF16)
    dw_rkv = mm("d_w_rkv", h1, du_rkv, "tn", out_dtype=BF16)
    dw_pg = mm("d_w_pg", h1, du_pg, "tn", out_dtype=BF16)
    dw_sm = mm("d_w_small", h1, du_sm, "tn", out_dtype=BF16)
    dh1 = mm("d_h1_z", du_z, w_z, "nt")
    dh1 = mm("d_h1_xbc", du_xbc, w_xbc, "nt", res=dh1)
    dh1 = mm("d_h1_rkv", du_rkv, w_rkv, "nt", res=dh1)
    dh1 = mm("d_h1_pg", du_pg, w_pg, "nt", res=dh1)
    dh1 = mm("d_h1_small", du_sm, w_sm, "nt", res=dh1)
    (dx,), (dg_mix,) = rowwise_bwd("norm_mix_bwd", rmsnorm_fn, [xt], [norm_mix_g], [[dh1]], tm, [F32], row_add=[dx1])

    dw_in_full = jnp.concatenate([dw_z, dw_xbc, dw_sm[:, 256:272], dw_rkv, dw_sm[:, 0:96], dw_sm[:, 128:224], dw_pg], axis=1)
    to_slabs = lambda g: jnp.transpose(g.reshape(g.shape[0], N_DEV, -1), (1, 0, 2))
    grads["w_in"] = to_slabs(dw_in_full)
    grads["ssd_conv_w"] = to_slabs(dconv_w)
    grads["rwkv_w2"] = to_slabs(dw2p[:96])
    grads["rwkv_a2"] = to_slabs(da2p[:96])
    grads["rwkv_g2"] = to_slabs(dg2)

    dmu = jnp.concatenate([dmu_rkv, dmu_pwa[:, 0:96], dmu_pwa[:, 128:224], dmu_pg], axis=1)
    small_grads = {
        "norm_mix_g": dg_mix, "ssd_conv_b": dconv_b, "ssd_dt_bias": ddt_bias_p[:, :16], "ssd_a_log": da_log_p[:, :16],
        "ssd_d": dd_p[:, :16], "ssd_norm_g": dssd_norm_g, "rwkv_mu": dmu, "rwkv_w0": dw0, "rwkv_a0": da0,
        "rwkv_k_k": dk_k, "rwkv_k_a": dk_a, "rwkv_r_k": dr_k, "rwkv_ln_w": dln_w, "rwkv_ln_b": dln_b,
        "norm_x_g": dg_x, "norm_mem_g": dg_mem, "norm_ffn_g": dg_ffn, "final_norm_g": dg_final}

    tail = ("w_in", "ssd_conv_w", "rwkv_w2", "rwkv_a2", "rwkv_g2")
    sc_tail, tok = exchange_start("scatter_tail_start", [grads[n] for n in tail], scatter=True)
    received = {}
    for names, handle in ((("ffn_w2",), sc_w2), (("ffn_w1",), sc_w1), (("xattn_wo",), sc_wo), (qkv, sc_qkv),
                          (("w_out",), sc_wout)):
        received.update(zip(names, exchange_wait("scatter_" + names[0] + "_wait", handle, after=tok)))
    (small_all,) = all_gather_many("gather_small_grads", [_pack_small(small_grads)])

    out_g, out_d, out_m, out_v = {}, {}, {}, {}

    def run_adamw(n):
        shape = wts[n].shape
        two_d = lambda a: a.reshape(-1, shape[-1])
        res = adamw("adamw_" + n, received[n].reshape(N_DEV, -1, shape[-1]), two_d(wts[n]), two_d(mom_m[n]), two_d(mom_v[n]))
        out_g[n], out_d[n], out_m[n], out_v[n] = (r.reshape(shape) for r in res)

    for n in ("ffn_w2", "ffn_w1", "xattn_wo") + qkv + ("w_out",):
        run_adamw(n)
    received.update(zip(tail, exchange_wait("scatter_tail_wait", sc_tail, after=out_g["w_out"])))
    for n in tail:
        run_adamw(n)
    res = adamw("adamw_small", small_all, _pack_small(wts), _pack_small(mom_m), _pack_small(mom_v))
    shapes = {n: wts[n].shape for n in _SMALL}
    for dst, packed in zip((out_g, out_d, out_m, out_v), res):
        dst.update(_unpack_small(packed, shapes))

    loss = lax.psum(loss_blk[0, 0], ("x", "y", "c"))
    return (loss, dx[None], *[out_g[n] for n in _WEIGHTS], *[out_d[n] for n in _WEIGHTS],
            *[out_m[n] for n in _WEIGHTS], *[out_v[n] for n in _WEIGHTS])
```

```python
import functools

import jax
import jax.numpy as jnp
from jax import lax
from jax.experimental import pallas as pl
from jax.experimental.pallas import tpu as pltpu

F32 = jnp.float32
BF16 = jnp.bfloat16
HIGHEST = lax.Precision.HIGHEST

N_DEV = 8
D_MODEL = 2048
NORM_EPS = 1e-6
SSD_WIDTH = 1024
SSD_CONV_DIM = 1536
SSD_HEADS = 16
SSD_HEAD_DIM = 64
SSD_STATE = 128
SSD_CHUNK = 128
SSD_HEADS_PER_GROUP = 8
RWKV_WIDTH = 1024
RWKV_HEADS = 16
RWKV_HEAD_DIM = 64
RWKV_LN_EPS = 64e-5
RWKV_CHUNK = 64
RWKV_HEADS_PER_STEP = 16
XATTN_HEADS = 4
XATTN_HEAD_DIM = 512
D_FF = 8192
LANES = 128
SUBLANES = 8
VMEM_LIMIT = 56 * 1024 * 1024

ADAM_LR = 0.001
ADAM_B1 = 0.9
ADAM_B2 = 0.999
ADAM_EPS = 1e-08
ADAM_WD = 0.01
ADAM_STEP = 10

_DN = {"nn": ((1,), (0,)), "nt": ((1,), (1,)), "tn": ((0,), (0,))}


def _dg(a, b, mode, precision=None):
    (ca,), (cb,) = _DN[mode]
    dn = (((ca + 1,), (cb + 1,)), ((0,), (0,))) if a.ndim == 3 else (((ca,), (cb,)), ((), ()))
    return lax.dot_general(a, b, dn, precision=precision, preferred_element_type=F32)


@functools.partial(jax.custom_vjp, nondiff_argnums=(2,))
def bdot(a, b, mode):
    return _dg(a.astype(BF16), b.astype(BF16), mode)


def _bdot_fwd(a, b, mode):
    return bdot(a, b, mode), (a, b)


def _bdot_bwd(mode, res, g):
    a, b = res
    ab, bb, gb = a.astype(BF16), b.astype(BF16), g.astype(BF16)
    if mode == "nn":
        da, db = _dg(gb, bb, "nt"), _dg(ab, gb, "tn")
    elif mode == "nt":
        da, db = _dg(gb, bb, "nn"), _dg(gb, ab, "tn")
    else:
        da, db = _dg(bb, gb, "nt"), _dg(ab, gb, "nn")
    return da.astype(a.dtype), db.astype(b.dtype)


bdot.defvjp(_bdot_fwd, _bdot_bwd)


def fdot(a, b, mode):
    return _dg(a, b, mode, precision=HIGHEST)


def _split3(x):
    hi = x.astype(BF16)
    r1 = x - hi.astype(F32)
    mid = r1.astype(BF16)
    lo = (r1 - mid.astype(F32)).astype(BF16)
    return hi, mid, lo


def _dot01(x, m01):
    hi, mid, lo = _split3(x)
    return _dg(hi, m01, "nn") + _dg(mid, m01, "nn") + _dg(lo, m01, "nn")


def _exact_dot_impl(a, b, mode, exact):
    if exact == "a":
        ae = a.astype(BF16)
        return sum(_dg(ae, part, mode) for part in _split3(b))
    be = b.astype(BF16)
    return sum(_dg(part, be, mode) for part in _split3(a))


@functools.partial(jax.custom_vjp, nondiff_argnums=(2, 3))
def exact_dot(a, b, mode, exact):
    return _exact_dot_impl(a, b, mode, exact)


def _exact_dot_fwd(a, b, mode, exact):
    return _exact_dot_impl(a, b, mode, exact), (a, b)


def _exact_dot_bwd(mode, exact, res, g):
    a, b = res
    if exact == "a":
        db = {"nn": lambda: _exact_dot_impl(a, g, "tn", "a"), "nt": lambda: _exact_dot_impl(g, a, "tn", "b"),
              "tn": lambda: _exact_dot_impl(a, g, "nn", "a")}[mode]()
        return jnp.zeros_like(a), db
    da = {"nn": lambda: _exact_dot_impl(g, b, "nt", "b"), "nt": lambda: _exact_dot_impl(g, b, "nn", "b"),
          "tn": lambda: _exact_dot_impl(b, g, "nt", "a")}[mode]()
    return da, jnp.zeros_like(b)


exact_dot.defvjp(_exact_dot_fwd, _exact_dot_bwd)


def _head_indicator(width, heads, transpose):
    hd = width // heads
    shape = (LANES, width) if transpose else (width, LANES)
    lane = lax.broadcasted_iota(jnp.int32, shape, 1 if not transpose else 0)
    pos = lax.broadcasted_iota(jnp.int32, shape, 0 if not transpose else 1)
    return ((pos >= lane * hd) & (pos < lane * hd + hd)).astype(BF16)


@jax.custom_vjp
def head_sum(x):
    w = x.shape[-1]
    e = _head_indicator(w, w // RWKV_HEAD_DIM, False)
    et = _head_indicator(w, w // RWKV_HEAD_DIM, True)
    return _dot01(_dot01(x, e), et)


head_sum.defvjp(lambda x: (head_sum(x), None), lambda _, g: (head_sum(g),))


def rmsnorm_fn(x, g):
    y = x * lax.rsqrt(jnp.mean(x * x, axis=-1, keepdims=True) + NORM_EPS)
    return ((y * g).astype(BF16),)


def cast_fn(x):
    return (x.astype(BF16),)


def ssd_pre_fn(xbc, xbc1, xbc2, xbc3, dt_raw, conv_w, conv_b, dt_bias):
    c = conv_w[3:4] * xbc + conv_w[2:3] * xbc1 + conv_w[1:2] * xbc2 + conv_w[0:1] * xbc3 + conv_b
    act = c * jax.nn.sigmoid(c)
    dt = jax.nn.softplus(dt_raw + dt_bias)
    return act[:, :SSD_WIDTH], act[:, SSD_WIDTH:SSD_WIDTH + 256], act[:, SSD_WIDTH + 256:], dt


def ssd_post_fn(yscan, z, norm_g):
    y = yscan * (z * jax.nn.sigmoid(z))
    half = SSD_WIDTH // 2
    parts = []
    for g in range(2):
        yg = y[:, g * half:(g + 1) * half]
        parts.append(yg * lax.rsqrt(jnp.mean(yg * yg, axis=-1, keepdims=True) + NORM_EPS))
    return ((jnp.concatenate(parts, axis=-1) * norm_g).astype(BF16),)


def rwkv_pre_fn(rkv, rkv_p, pg, pg_p, pwa, pwa_p, mu_rkv, mu_pg, mu_pwa, w0, w2p, a0, a2p, g2, k_k, k_a):
    w = RWKV_WIDTH
    rkv = rkv + (rkv_p - rkv) * mu_rkv
    pg = pg + (pg_p - pg) * mu_pg
    pwa = pwa + (pwa_p - pwa) * mu_pwa
    r, k, v = rkv[:, :w], rkv[:, w:2 * w], rkv[:, 2 * w:]
    pw, pa = pwa[:, :LANES], pwa[:, LANES:]
    w_log = -jax.nn.softplus(-(w0 + bdot(jnp.tanh(pw), w2p, "nn"))) - 0.5
    lw = -jnp.exp(w_log)
    iclr = jax.nn.sigmoid(a0 + bdot(pa, a2p, "nn"))
    gate = bdot(jax.nn.sigmoid(pg), g2, "nn")
    kk = k * k_k
    kap = kk / jnp.maximum(jnp.sqrt(head_sum(kk * kk)), 1e-12)
    k_mod = k * (1.0 + (iclr - 1.0) * k_a)
    return r, lw, k_mod, v, kap, kap * iclr, gate


def rwkv_post_fn(ys, r, k_mod, v, gate, ln_w, ln_b, r_k):
    inv_n = 1.0 / RWKV_HEAD_DIM
    mean = head_sum(ys) * inv_n
    yc = ys - mean
    var = head_sum(yc * yc) * inv_n
    yn = yc * lax.rsqrt(var + RWKV_LN_EPS) * ln_w + ln_b
    bonus = head_sum(r * k_mod * r_k) * v
    return (((yn + bonus) * gate).astype(BF16),)


def attn_fn(q, kx, vx):
    outs = []
    for h in range(XATTN_HEADS):
        sl = slice(h * XATTN_HEAD_DIM, (h + 1) * XATTN_HEAD_DIM)
        s = bdot(q[:, sl], kx[:, sl], "nt") * (XATTN_HEAD_DIM ** -0.5)
        s = s - jnp.max(s, axis=-1, keepdims=True)
        p = jnp.exp(s)
        p = p / jnp.sum(p, axis=-1, keepdims=True)
        outs.append(bdot(p, vx[:, sl], "nn"))
    return (jnp.concatenate(outs, axis=-1).astype(BF16),)


def loss_fn(x, tgt, g):
    y = x * lax.rsqrt(jnp.mean(x * x, axis=-1, keepdims=True) + NORM_EPS) * g
    err = jnp.square(y - tgt)
    return 0.5 * jnp.sum(jnp.mean(err, axis=-1, keepdims=True), axis=0, keepdims=True)


def _tri_masks(n):
    row = lax.broadcasted_iota(jnp.int32, (n, n), 0)
    col = lax.broadcasted_iota(jnp.int32, (n, n), 1)
    return col <= row, col < row, row == col


def rwkv_chunk_fn(st0, r, lw, k, v, kap, b):
    h, c = r.shape[0], r.shape[1]
    incl, strict, diag = _tri_masks(c)
    cum = exact_dot(jnp.broadcast_to(incl.astype(F32), (h, c, c)), lw, "nn", "a")
    g_in = jnp.exp(cum)
    g_prev = jnp.exp(cum - lw)
    g_inv = jnp.exp(-cum)
    g_end = jnp.exp(cum[:, c - 1:c, :] - cum)
    kap_t, k_t, b_t, r_t = kap * g_prev, k * g_inv, b * g_inv, r * g_in
    a_ub = jnp.where(strict, bdot(kap_t, b_t, "nt"), 0.0)
    a_vk = jnp.where(strict, bdot(kap_t, k_t, "nt"), 0.0)
    rhs = -(bdot(kap_t, st0, "nn") + bdot(a_vk, v, "nn"))
    eye = diag.astype(F32)
    m = -a_ub
    inv = eye + m
    n = 1
    while n * 2 < c:
        m = bdot(m, m, "nn")
        inv = bdot(inv, eye + m, "nn")
        n *= 2
    u = bdot(inv, rhs, "nn")
    y = (bdot(r_t, st0, "nn")
         + bdot(jnp.where(incl, bdot(r_t, k_t, "nt"), 0.0), v, "nn")
         + bdot(jnp.where(incl, bdot(r_t, b_t, "nt"), 0.0), u, "nn"))
    g_full = jnp.exp(exact_dot(lw, jnp.ones((h, c, st0.shape[2]), F32), "tn", "b"))
    st1 = g_full * st0 + bdot(k * g_end, v, "tn") + bdot(b * g_end, u, "tn")
    return y, st1


def ssd_chunk_fn(group, h0, xs, bm, cm, dt, a_log, d_skip):
    q = xs.shape[0]
    causal, _, _ = _tri_masks(q)
    a_row = -jnp.exp(a_log)
    cs_all = exact_dot(causal.astype(F32), dt * a_row, "nn", "a")
    cs_t = cs_all.T
    cb = bdot(cm, bm, "nt")
    ys, hs = [], []
    p = SSD_HEAD_DIM
    for e in range(SSD_HEADS_PER_GROUP):
        hl = group * SSD_HEADS_PER_GROUP + e
        cs = cs_all[:, hl:hl + 1]
        seg = cs - cs_t[hl:hl + 1, :]
        lmat = jnp.where(causal, jnp.exp(jnp.where(causal, seg, 0.0)), 0.0)
        x_h = xs[:, e * p:(e + 1) * p]
        xdt = x_h * dt[:, hl:hl + 1]
        h0_e = h0[e * p:(e + 1) * p, :]
        cl = cs[q - 1:q, :]
        y = bdot(cb * lmat, xdt, "nn") + bdot(cm, h0_e, "nt") * jnp.exp(cs) + x_h * d_skip[:, hl:hl + 1]
        hs.append(h0_e * jnp.exp(cl) + bdot(xdt * jnp.exp(cl - cs), bm, "tn"))
        ys.append(y)
    return jnp.concatenate(ys, axis=-1), jnp.concatenate(hs, axis=0)


class Rows:
    def __init__(self, arr, w=None, cb=0, shifts=()):
        self.arr, self.w, self.cb, self.shifts = arr, (arr.shape[1] if w is None else w), cb, tuple(shifts)


def _as_rows(x):
    return x if isinstance(x, Rows) else Rows(x)


def _shift_down(x, halo, k):
    rolled = pltpu.roll(x, k, 0)
    first = rolled[0:SUBLANES]
    rid = lax.broadcasted_iota(jnp.int32, first.shape, 0)
    patched = jnp.where(rid < k, pltpu.roll(halo, k, 0), first)
    return jnp.concatenate([patched, rolled[SUBLANES:]], axis=0)


def _shift_up(g, carry, k):
    tm = g.shape[0]
    rolled = pltpu.roll(g, tm - k, 0)
    last = rolled[tm - SUBLANES:]
    rid = lax.broadcasted_iota(jnp.int32, last.shape, 0)
    patched = jnp.where(rid >= SUBLANES - k, pltpu.roll(carry, SUBLANES - k, 0), last)
    return jnp.concatenate([rolled[:tm - SUBLANES], patched], axis=0)


def _params():
    return pltpu.CompilerParams(vmem_limit_bytes=VMEM_LIMIT)


def _load_rows(refs, pos, rins, first_block):
    vals = []
    for r in rins:
        x = refs[pos][...].astype(F32) if refs[pos].dtype != F32 else refs[pos][...]
        pos += 1
        vals.append(x)
        if r.shifts:
            halo = refs[pos][...]
            pos += 1
            halo = jnp.where(first_block, jnp.zeros_like(halo), halo)
            for k in r.shifts:
                vals.append(_shift_down(x, halo, k))
    return vals, pos


def _row_specs(rins, tm, blk):
    specs, args = [], []
    for r in rins:
        specs.append(pl.BlockSpec((tm, r.w), lambda i, cb=r.cb: (blk(i), cb)))
        args.append(r.arr)
        if r.shifts:
            per = tm // SUBLANES
            specs.append(pl.BlockSpec((SUBLANES, r.w), lambda i, cb=r.cb: (jnp.maximum(blk(i) * per - 1, 0), cb)))
            args.append(r.arr)
    return specs, args


def rowwise_fwd(name, fn, rins, params, outs, tm, deps=()):
    rins = [_as_rows(r) for r in rins]
    t = rins[0].arr.shape[0]
    tm = min(tm, t)
    nb = t // tm
    specs, args = _row_specs(rins, tm, lambda i: i)
    for p in params:
        specs.append(pl.BlockSpec(p.shape, lambda i: (0, 0)))
        args.append(p)
    for dep in deps:
        specs.append(pl.BlockSpec(dep.shape, lambda i: (0, 0)))
        args.append(dep)
    n_in = len(args)

    def body(*refs):
        vals, pos = _load_rows(refs, 0, rins, pl.program_id(0) == 0)
        pv = [refs[pos + j][...] for j in range(len(params))]
        res = fn(*vals, *pv)
        for o_ref, o in zip(refs[n_in:], res):
            o_ref[...] = o.astype(o_ref.dtype)

    return pl.pallas_call(
        body, name=name, grid=(nb,), in_specs=specs,
        out_specs=[pl.BlockSpec((tm, w), lambda i: (i, 0)) for w, _ in outs],
        out_shape=[jax.ShapeDtypeStruct((t, w), dt) for w, dt in outs],
        compiler_params=_params(),
    )(*args)


def rowwise_bwd(name, fn, rins, params, cts, tm, grad_dtypes, row_add=None):
    rins = [_as_rows(r) for r in rins]
    cts = [[_as_rows(c) for c in lst] for lst in cts]
    row_add = [_as_rows(a) for a in (row_add or [])]
    t = rins[0].arr.shape[0]
    tm = min(tm, t)
    nb = t // tm
    rev = lambda i: nb - 1 - i
    specs, args = _row_specs(rins, tm, rev)
    for p in params:
        specs.append(pl.BlockSpec(p.shape, lambda i: (0, 0)))
        args.append(p)
    flat_cts = [c for lst in cts for c in lst] + row_add
    for c in flat_cts:
        specs.append(pl.BlockSpec((tm, c.w), lambda i, cb=c.cb: (rev(i), cb)))
        args.append(c.arr)
    n_in = len(args)
    want = [i for i, d in enumerate(grad_dtypes) if d is not None]
    out_specs = [pl.BlockSpec((tm, rins[i].w), lambda i_: (rev(i_), 0)) for i in want]
    out_shape = [jax.ShapeDtypeStruct((t, rins[i].w), grad_dtypes[i]) for i in want]
    out_specs += [pl.BlockSpec(p.shape, lambda i: (0, 0)) for p in params]
    out_shape += [jax.ShapeDtypeStruct(p.shape, F32) for p in params]
    n_out = len(out_shape)
    scratch = [pltpu.VMEM((SUBLANES, r.w), F32) for r in rins for _ in r.shifts]

    def body(*refs):
        i = pl.program_id(0)
        vals, pos = _load_rows(refs, 0, rins, rev(i) == 0)
        pv = [refs[pos + j][...] for j in range(len(params))]
        pos += len(params)
        outs, vjp = jax.vjp(fn, *vals, *pv)
        ct_vals = []
        for o, lst in zip(outs, cts):
            acc = None
            for _ in lst:
                cv = refs[pos][...].astype(F32)
                pos += 1
                acc = cv if acc is None else acc + cv
            ct_vals.append(acc.astype(o.dtype))
        adds = [refs[pos + j][...].astype(F32) for j in range(len(row_add))]
        grads = vjp(tuple(ct_vals))
        out_refs = refs[n_in:n_in + n_out]
        carry_refs = refs[n_in + n_out:]

        @pl.when(i == 0)
        def _():
            for cr in carry_refs:
                cr[...] = jnp.zeros_like(cr)
            for pr in out_refs[len(want):]:
                pr[...] = jnp.zeros_like(pr)

        gi, ci, oi = 0, 0, 0
        for idx, r in enumerate(rins):
            d = grads[gi]
            gi += 1
            for k in r.shifts:
                dk = grads[gi]
                gi += 1
                d = d + _shift_up(dk, carry_refs[ci][...], k)
                carry_refs[ci][...] = dk[0:SUBLANES]
                ci += 1
            if idx == 0:
                for a in adds:
                    d = d + a
            if grad_dtypes[idx] is not None:
                out_refs[oi][...] = d.astype(out_refs[oi].dtype)
                oi += 1
        for pr, gp in zip(out_refs[len(want):], grads[gi:]):
            pr[...] += gp

    res = pl.pallas_call(
        body, name=name, grid=(nb,), in_specs=specs, out_specs=out_specs, out_shape=out_shape,
        scratch_shapes=scratch, compiler_params=_params(),
    )(*args)
    return res[:len(want)], res[len(want):]


def _pick(n, pref):
    for c in pref:
        if n % c == 0:
            return c
    return n


MM_VMEM_BUDGET = 40 * 1024 * 1024
MM_PEAK_FLOPS = 0.9e15
MM_HBM_BYTES_PER_S = 3.0e12
MM_STEP_SECONDS = 0.35e-6


def _mm_tiles(m, n, k, size_a, size_b, size_out, size_res, single_k):
    best = None
    for tk in sorted({c for c in (k, 2048, 1024, 512, 256, 128) if c <= 2048 and k % c == 0}, reverse=True):
        for tm in (1024, 512, 256, 128):
            if m % tm:
                continue
            for tn in (1024, 768, 512, 384, 256, 128):
                if n % tn:
                    continue
                nk = k // tk
                vmem = 2 * (tm * tk * size_a + tk * tn * size_b + tm * tn * (size_out + size_res))
                vmem += tm * tn * 4 * (2 if nk > 1 or not single_k else 1)
                vmem += (tm * tk * 2 if size_a > 2 else 0) + (tk * tn * 2 if size_b > 2 else 0)
                if vmem > MM_VMEM_BUDGET:
                    continue
                steps = (m // tm) * (n // tn) * nk
                a_reads = 1 if (nk == 1 and single_k) else n // tn
                traffic = m * k * size_a * a_reads + k * n * size_b * (m // tm) + m * n * (size_out + size_res)
                cost = max(2.0 * m * n * k / MM_PEAK_FLOPS, traffic / MM_HBM_BYTES_PER_S) + steps * MM_STEP_SECONDS
                if best is None or cost < best[0]:
                    best = (cost, tm, tn, tk)
    return best[1:]


def mm(name, a, b, mode, out_dtype=F32, res=None, b_slabs=None, out_slabs=None, dep=None, epi=None, extras=(),
       out_dtypes=None):
    if mode == "tn":
        k_dim, m_dim = a.shape
    else:
        m_dim, k_dim = a.shape
    if b_slabs:
        n_dim = b.shape[0] * b.shape[2] if mode == "nn" else b.shape[1]
    else:
        n_dim = b.shape[0] if mode == "nt" else b.shape[1]
    n_slabs = out_slabs or (b_slabs if (b_slabs and mode == "nn") else 1)
    k_slabs = b_slabs if (b_slabs and mode == "nt") else 1
    if epi is None:
        out_dtypes = [out_dtype]
        if res is None:
            epi = lambda acc: (acc,)
        else:
            extras, epi = [res], lambda acc, r: (acc + r,)
    tm, tn, tk = _mm_tiles(m_dim, n_dim // n_slabs, k_dim // k_slabs, a.dtype.itemsize, b.dtype.itemsize,
                           sum(jnp.dtype(dt).itemsize for dt in out_dtypes), sum(e.dtype.itemsize for e in extras),
                           single_k=(k_slabs == 1))
    nji = n_dim // n_slabs // tn
    nki = k_dim // k_slabs // tk
    nblk = lambda js, j: js * nji + j
    kblk = lambda ks, k: ks * nki + k
    if mode == "tn":
        a_spec = pl.BlockSpec((tk, tm), lambda i, js, j, ks, k: (kblk(ks, k), i))
    else:
        a_spec = pl.BlockSpec((tm, tk), lambda i, js, j, ks, k: (i, kblk(ks, k)))
    if b_slabs and mode == "nn":
        b_spec = pl.BlockSpec((None, tk, tn), lambda i, js, j, ks, k: (js, k, j))
    elif b_slabs and mode == "nt":
        b_spec = pl.BlockSpec((None, tn, tk), lambda i, js, j, ks, k: (ks, nblk(js, j), k))
    elif mode == "nt":
        b_spec = pl.BlockSpec((tn, tk), lambda i, js, j, ks, k: (nblk(js, j), kblk(ks, k)))
    else:
        b_spec = pl.BlockSpec((tk, tn), lambda i, js, j, ks, k: (kblk(ks, k), nblk(js, j)))
    specs, args = [a_spec, b_spec], [a, b]
    for e in extras:
        specs.append(pl.BlockSpec((tm, tn), lambda i, js, j, ks, k: (i, nblk(js, j))))
        args.append(e)
    if dep is not None:
        specs.append(pl.BlockSpec(dep.shape, lambda i, js, j, ks, k: (0, 0)))
        args.append(dep)
    if out_slabs:
        o_specs = [pl.BlockSpec((None, tm, tn), lambda i, js, j, ks, k: (js, i, j))]
        o_shapes = [jax.ShapeDtypeStruct((out_slabs, m_dim, n_dim // out_slabs), out_dtypes[0])]
    else:
        o_specs = [pl.BlockSpec((tm, tn), lambda i, js, j, ks, k: (i, nblk(js, j))) for _ in out_dtypes]
        o_shapes = [jax.ShapeDtypeStruct((m_dim, n_dim), dt) for dt in out_dtypes]

    one_k_step = k_slabs * nki == 1
    n_in, n_out = len(args), len(out_dtypes)

    def body(*refs):
        a_ref, b_ref = refs[0], refs[1]
        part = _dg(a_ref[...].astype(BF16), b_ref[...].astype(BF16), mode)

        def finish(acc):
            outs = epi(acc, *[refs[2 + j][...].astype(F32) for j in range(len(extras))])
            for o_ref, o in zip(refs[n_in:n_in + n_out], outs):
                o_ref[...] = o.astype(o_ref.dtype)

        if one_k_step:
            finish(part)
            return
        acc_ref = refs[n_in + n_out]
        ks, kk = pl.program_id(3), pl.program_id(4)

        @pl.when((ks == 0) & (kk == 0))
        def _():
            acc_ref[...] = part

        @pl.when((ks > 0) | (kk > 0))
        def _():
            acc_ref[...] += part

        pl.when((ks == k_slabs - 1) & (kk == nki - 1))(lambda: finish(acc_ref[...]))

    grid = (m_dim // tm, n_slabs, nji, k_slabs, nki)
    scratch = [] if one_k_step else [pltpu.VMEM((tm, tn), F32)]
    out = pl.pallas_call(
        body, name=name, grid=grid, in_specs=specs, out_specs=o_specs, out_shape=o_shapes, scratch_shapes=scratch,
        compiler_params=pltpu.CompilerParams(
            dimension_semantics=("parallel", "parallel", "parallel", "arbitrary", "arbitrary"),
            vmem_limit_bytes=VMEM_LIMIT),
    )(*args)
    return out[0] if n_out == 1 else out


def _stack_lanes(x, n):
    w = x.shape[1] // n
    return jnp.stack([x[:, i * w:(i + 1) * w] for i in range(n)])


def _stack_rows(x, n):
    w = x.shape[0] // n
    return jnp.stack([x[i * w:(i + 1) * w, :] for i in range(n)])


def rwkv_scan_fwd(r, lw, k, v, kap, b):
    t = r.shape[0]
    c, hps, hd = min(RWKV_CHUNK, t), RWKV_HEADS_PER_STEP, RWKV_HEAD_DIM
    nc, ng, wl = t // c, RWKV_HEADS // hps, hps * hd
    spec = pl.BlockSpec((c, wl), lambda g, ci: (ci, g))

    def body(r_ref, lw_ref, k_ref, v_ref, kap_ref, b_ref, y_ref, ck_ref, st_ref):
        @pl.when(pl.program_id(1) == 0)
        def _():
            st_ref[...] = jnp.zeros_like(st_ref)

        st = st_ref[...]
        ck_ref[...] = st
        ins = [x[...] for x in (r_ref, lw_ref, k_ref, v_ref, kap_ref, b_ref)]
        y, st1 = rwkv_chunk_fn(_stack_rows(st, hps), *[_stack_lanes(x, hps) for x in ins])
        y_ref[...] = jnp.concatenate([y[h] for h in range(hps)], axis=-1)
        st_ref[...] = jnp.concatenate([st1[h] for h in range(hps)], axis=0)

    return pl.pallas_call(
        body, name="rwkv_scan_fwd", grid=(ng, nc), in_specs=[spec] * 6,
        out_specs=[spec, pl.BlockSpec((None, wl, hd), lambda g, ci: (ci, g, 0))],
        out_shape=[jax.ShapeDtypeStruct((t, RWKV_WIDTH), F32), jax.ShapeDtypeStruct((nc, RWKV_WIDTH, hd), F32)],
        scratch_shapes=[pltpu.VMEM((wl, hd), F32)], compiler_params=_params(),
    )(r, lw, k, v, kap, b)


def rwkv_scan_bwd(r, lw, k, v, kap, b, ck, dy):
    t = r.shape[0]
    c, hps, hd = min(RWKV_CHUNK, t), RWKV_HEADS_PER_STEP, RWKV_HEAD_DIM
    nc, ng, wl = t // c, RWKV_HEADS // hps, hps * hd
    spec = pl.BlockSpec((c, wl), lambda g, ci: (nc - 1 - ci, g))

    def body(r_ref, lw_ref, k_ref, v_ref, kap_ref, b_ref, ck_ref, dy_ref, *rest):
        out_refs, dst_ref = rest[:6], rest[6]

        @pl.when(pl.program_id(1) == 0)
        def _():
            dst_ref[...] = jnp.zeros_like(dst_ref)

        ins = [x[...] for x in (r_ref, lw_ref, k_ref, v_ref, kap_ref, b_ref)]
        dyv, ck, dst = dy_ref[...], ck_ref[...], dst_ref[...]
        _, vjp = jax.vjp(rwkv_chunk_fn, _stack_rows(ck, hps), *[_stack_lanes(x, hps) for x in ins])
        grads = vjp((_stack_lanes(dyv, hps), _stack_rows(dst, hps)))
        dst_ref[...] = jnp.concatenate([grads[0][h] for h in range(hps)], axis=0)
        for j in range(6):
            out_refs[j][...] = jnp.concatenate([grads[1 + j][h] for h in range(hps)], axis=-1)

    return pl.pallas_call(
        body, name="rwkv_scan_bwd", grid=(ng, nc),
        in_specs=[spec] * 6 + [pl.BlockSpec((None, wl, hd), lambda g, ci: (nc - 1 - ci, g, 0)), spec],
        out_specs=[spec] * 6, out_shape=[jax.ShapeDtypeStruct((t, RWKV_WIDTH), F32)] * 6,
        scratch_shapes=[pltpu.VMEM((wl, hd), F32)], compiler_params=_params(),
    )(r, lw, k, v, kap, b, ck, dy)


def _ssd_specs(q, blk):
    gw = SSD_WIDTH // 2
    return [pl.BlockSpec((q, gw), lambda g, ci: (blk(ci), g)),
            pl.BlockSpec((q, SSD_STATE), lambda g, ci: (blk(ci), g)),
            pl.BlockSpec((q, SSD_STATE), lambda g, ci: (blk(ci), g)),
            pl.BlockSpec((q, LANES), lambda g, ci: (blk(ci), 0)),
            pl.BlockSpec((1, LANES), lambda g, ci: (0, 0)),
            pl.BlockSpec((1, LANES), lambda g, ci: (0, 0))]


def ssd_scan_fwd(xs, bm, cm, dt, a_log, d_skip):
    t = xs.shape[0]
    q = min(SSD_CHUNK, t)
    nc, gw = t // q, SSD_WIDTH // 2

    def body(xs_ref, bm_ref, cm_ref, dt_ref, al_ref, d_ref, y_ref, ck_ref, h_ref):
        @pl.when(pl.program_id(1) == 0)
        def _():
            h_ref[...] = jnp.zeros_like(h_ref)

        ck_ref[...] = h_ref[...]
        args = (h_ref[...], xs_ref[...], bm_ref[...], cm_ref[...], dt_ref[...], al_ref[...], d_ref[...])
        g = pl.program_id(0)

        @pl.when(g == 0)
        def _():
            y, h1 = ssd_chunk_fn(0, *args)
            y_ref[...] = y
            h_ref[...] = h1

        @pl.when(g == 1)
        def _():
            y, h1 = ssd_chunk_fn(1, *args)
            y_ref[...] = y
            h_ref[...] = h1

    return pl.pallas_call(
        body, name="ssd_scan_fwd", grid=(2, nc), in_specs=_ssd_specs(q, lambda ci: ci),
        out_specs=[pl.BlockSpec((q, gw), lambda g, ci: (ci, g)),
                   pl.BlockSpec((None, gw, SSD_STATE), lambda g, ci: (ci, g, 0))],
        out_shape=[jax.ShapeDtypeStruct((t, SSD_WIDTH), F32), jax.ShapeDtypeStruct((nc, SSD_WIDTH, SSD_STATE), F32)],
        scratch_shapes=[pltpu.VMEM((gw, SSD_STATE), F32)], compiler_params=_params(),
    )(xs, bm, cm, dt, a_log, d_skip)


def ssd_scan_bwd(xs, bm, cm, dt, a_log, d_skip, ck, dy):
    t = xs.shape[0]
    q = min(SSD_CHUNK, t)
    nc, gw = t // q, SSD_WIDTH // 2
    rev = lambda ci: nc - 1 - ci

    def body(xs_ref, bm_ref, cm_ref, dt_ref, al_ref, d_ref, ck_ref, dy_ref,
             dxs_ref, dbm_ref, dcm_ref, ddt_ref, dal_ref, dd_ref, dh_ref):
        g, ci = pl.program_id(0), pl.program_id(1)

        @pl.when(ci == 0)
        def _():
            dh_ref[...] = jnp.zeros_like(dh_ref)

        @pl.when((ci == 0) & (g == 0))
        def _():
            dal_ref[...] = jnp.zeros_like(dal_ref)
            dd_ref[...] = jnp.zeros_like(dd_ref)

        args = (ck_ref[...], xs_ref[...], bm_ref[...], cm_ref[...], dt_ref[...], al_ref[...], d_ref[...])

        def run(group):
            _, vjp = jax.vjp(functools.partial(ssd_chunk_fn, group), *args)
            dh0, dxs, dbm, dcm, ddt, dal, dd = vjp((dy_ref[...], dh_ref[...]))
            dh_ref[...] = dh0
            dxs_ref[...] = dxs
            dbm_ref[...] = dbm
            dcm_ref[...] = dcm
            ddt_ref[...] = ddt
            dal_ref[...] += dal
            dd_ref[...] += dd

        pl.when(g == 0)(lambda: run(0))
        pl.when(g == 1)(lambda: run(1))

    in_specs = _ssd_specs(q, rev) + [pl.BlockSpec((None, gw, SSD_STATE), lambda g, ci: (rev(ci), g, 0)),
                                     pl.BlockSpec((q, gw), lambda g, ci: (rev(ci), g))]
    return pl.pallas_call(
        body, name="ssd_scan_bwd", grid=(2, nc), in_specs=in_specs,
        out_specs=[pl.BlockSpec((q, gw), lambda g, ci: (rev(ci), g)),
                   pl.BlockSpec((q, SSD_STATE), lambda g, ci: (rev(ci), g)),
                   pl.BlockSpec((q, SSD_STATE), lambda g, ci: (rev(ci), g)),
                   pl.BlockSpec((None, q, LANES), lambda g, ci: (g, rev(ci), 0)),
                   pl.BlockSpec((1, LANES), lambda g, ci: (0, 0)),
                   pl.BlockSpec((1, LANES), lambda g, ci: (0, 0))],
        out_shape=[jax.ShapeDtypeStruct((t, SSD_WIDTH), F32), jax.ShapeDtypeStruct((t, 2 * SSD_STATE), F32),
                   jax.ShapeDtypeStruct((t, 2 * SSD_STATE), F32), jax.ShapeDtypeStruct((2, t, LANES), F32),
                   jax.ShapeDtypeStruct((1, LANES), F32), jax.ShapeDtypeStruct((1, LANES), F32)],
        scratch_shapes=[pltpu.VMEM((gw, SSD_STATE), F32)], compiler_params=_params(),
    )(xs, bm, cm, dt, a_log, d_skip, ck, dy)


def loss_and_grad(x, tgt, g, tm):
    t, d = x.shape
    tm = min(tm, t)
    nb = t // tm

    def body(x_ref, t_ref, g_ref, loss_ref, dx_ref, dg_ref):
        @pl.when(pl.program_id(0) == 0)
        def _():
            loss_ref[...] = jnp.zeros_like(loss_ref)
            dg_ref[...] = jnp.zeros_like(dg_ref)

        val, vjp = jax.vjp(loss_fn, x_ref[...], t_ref[...], g_ref[...])
        dx, _, dg = vjp(jnp.ones((1, 1), F32))
        loss_ref[...] += jnp.broadcast_to(val, loss_ref.shape)
        dx_ref[...] = dx
        dg_ref[...] += dg

    row = pl.BlockSpec((tm, d), lambda i: (i, 0))
    one = pl.BlockSpec((1, d), lambda i: (0, 0))
    return pl.pallas_call(
        body, name="loss_and_grad", grid=(nb,), in_specs=[row, row, one],
        out_specs=[pl.BlockSpec((SUBLANES, LANES), lambda i: (0, 0)), row, one],
        out_shape=[jax.ShapeDtypeStruct((SUBLANES, LANES), F32), jax.ShapeDtypeStruct((t, d), F32),
                   jax.ShapeDtypeStruct((1, d), F32)],
        compiler_params=_params(),
    )(x, tgt, g)


def adamw(name, recv, w, m, v):
    rows, cols = w.shape
    recv_block_bytes = 4 * 1024 * 1024
    tm = _pick(rows, [c for c in (256, 128, 64, 32, 16, 8) if N_DEV * c * cols * 4 <= recv_block_bytes])
    c1 = 1.0 / (1.0 - ADAM_B1 ** ADAM_STEP)
    c2 = 1.0 / (1.0 - ADAM_B2 ** ADAM_STEP)

    def body(recv_ref, w_ref, m_ref, v_ref, g_ref, d_ref, nm_ref, nv_ref):
        g = recv_ref[0].astype(F32)
        for p in range(1, N_DEV):
            g = g + recv_ref[p].astype(F32)
        nm = ADAM_B1 * m_ref[...] + (1.0 - ADAM_B1) * g
        nv = ADAM_B2 * v_ref[...] + (1.0 - ADAM_B2) * jnp.square(g)
        g_ref[...] = g
        nm_ref[...] = nm
        nv_ref[...] = nv
        d_ref[...] = -ADAM_LR * ((nm * c1) / (jnp.sqrt(nv * c2) + ADAM_EPS) + ADAM_WD * w_ref[...])

    blk = pl.BlockSpec((tm, cols), lambda i: (i, 0))
    return pl.pallas_call(
        body, name=name, grid=(rows // tm,),
        in_specs=[pl.BlockSpec((N_DEV, tm, cols), lambda i: (0, i, 0)), blk, blk, blk],
        out_specs=[blk] * 4, out_shape=[jax.ShapeDtypeStruct((rows, cols), F32)] * 4,
        compiler_params=_params(),
    )(recv, w, m, v)


def _mesh_pos():
    return lax.axis_index("x"), lax.axis_index("y"), lax.axis_index("c")


def _peer(pos, mask):
    x, y, c = pos
    return (1 - x if mask & 4 else x, 1 - y if mask & 2 else y, 1 - c if mask & 1 else c)


def _linear(pos):
    return 4 * pos[0] + 2 * pos[1] + pos[2]


class Exchange:
    def __init__(self, xs, scatter):
        self.xs, self.scatter, self.n = list(xs), scatter, len(xs)
        self.out_shape = [jax.ShapeDtypeStruct(x.shape if scatter else (N_DEV,) + x.shape, x.dtype) for x in xs]
        self.specs = [pl.BlockSpec(memory_space=pl.ANY)] * self.n
        peers = N_DEV - 1
        self.scratch = [pltpu.SemaphoreType.DMA((self.n * peers,)), pltpu.SemaphoreType.DMA((self.n * peers,)),
                        pltpu.SemaphoreType.DMA((self.n,))]

    def _copies(self, ins, outs, sems, landing):
        send_sems, recv_sems, local_sems = sems
        me = _mesh_pos()
        me_lin = _linear(me)
        local, remote = [], []
        for ti in range(self.n):
            src_mine = ins[ti].at[me_lin] if self.scatter else ins[ti]
            local.append(pltpu.make_async_copy(src_mine, outs[ti].at[me_lin], local_sems.at[ti]))
            for mask in range(1, N_DEV):
                peer = _peer(me, mask)
                src = ins[ti].at[_linear(peer)] if self.scatter else ins[ti]
                dst = outs[ti].at[_linear(peer) if landing else me_lin]
                remote.append(pltpu.make_async_remote_copy(
                    src_ref=src, dst_ref=dst, send_sem=send_sems.at[ti * (N_DEV - 1) + mask - 1],
                    recv_sem=recv_sems.at[ti * (N_DEV - 1) + mask - 1],
                    device_id=peer, device_id_type=pl.DeviceIdType.MESH))
        return local, remote

    def start(self, ins, outs, sems):
        local, remote = self._copies(ins, outs, sems, landing=False)
        for cp in local + remote:
            cp.start()

    def finish(self, ins, outs, sems):
        local, remote = self._copies(ins, outs, sems, landing=True)
        for cp in remote:
            cp.wait_recv()
        for cp in remote:
            cp.wait_send()
        for cp in local:
            cp.wait()


def exchange_call(name, xs, scatter):
    ex = Exchange(xs, scatter)
    n = ex.n

    def body(*refs):
        ins, outs, sems = refs[:n], refs[n:2 * n], refs[2 * n:]
        ex.start(ins, outs, sems)
        ex.finish(ins, outs, sems)

    return pl.pallas_call(
        body, name=name, in_specs=ex.specs, out_specs=ex.specs, out_shape=ex.out_shape, scratch_shapes=ex.scratch,
        compiler_params=pltpu.CompilerParams(has_side_effects=True),
    )(*xs)


def exchange_start(name, xs, scatter, dep=None):
    ex = Exchange(xs, scatter)
    n = ex.n
    hbm = pl.BlockSpec(memory_space=pltpu.HBM)
    sem = pl.BlockSpec(memory_space=pltpu.SEMAPHORE)
    lands = [lax.empty(s.shape, s.dtype) for s in ex.out_shape]

    n_inputs = 2 * n + (0 if dep is None else 1)

    def body(*refs):
        ins, lnd, sems, token = refs[:n], refs[n:2 * n], refs[n_inputs:n_inputs + 3], refs[-1]
        ex.start(ins, lnd, sems)
        token[...] = jnp.zeros_like(token)

    res = pl.pallas_call(
        body, name=name, in_specs=[hbm] * (2 * n) + ([] if dep is None else [pl.BlockSpec(memory_space=pl.ANY)]),
        out_specs=[sem] * 3 + [hbm] * (2 * n) + [pl.BlockSpec(memory_space=pltpu.VMEM)],
        out_shape=ex.scratch + [pltpu.HBM(x.shape, x.dtype) for x in xs]
        + [pltpu.HBM(s.shape, s.dtype) for s in ex.out_shape] + [jax.ShapeDtypeStruct((SUBLANES, LANES), F32)],
        input_output_aliases={i: 3 + i for i in range(2 * n)},
        compiler_params=pltpu.CompilerParams(has_side_effects=pltpu.SideEffectType.DATAFLOW_SIDE_EFFECTING),
    )(*[pltpu.with_memory_space_constraint(x, pltpu.HBM) for x in list(xs) + lands], *([] if dep is None else [dep]))
    return (ex, res[:3], res[3:3 + n], res[3 + n:3 + 2 * n]), res[-1]


def exchange_wait(name, handles, after):
    ex, sems, srcs, lands = handles
    n = ex.n
    hbm = pl.BlockSpec(memory_space=pltpu.HBM)
    sem = pl.BlockSpec(memory_space=pltpu.SEMAPHORE)

    def body(*refs):
        ins, lnd, sem_refs = refs[:n], refs[n:2 * n], refs[2 * n:2 * n + 3]
        ex.finish(ins, lnd, sem_refs)

    res = pl.pallas_call(
        body, name=name, in_specs=[hbm] * (2 * n) + [sem] * 3 + [pl.BlockSpec(memory_space=pl.ANY)],
        out_specs=[hbm] * (2 * n),
        out_shape=[pltpu.HBM(x.shape, x.dtype) for x in srcs] + [pltpu.HBM(x.shape, x.dtype) for x in lands],
        input_output_aliases={i: i for i in range(2 * n)},
        compiler_params=pltpu.CompilerParams(has_side_effects=pltpu.SideEffectType.DATAFLOW_SIDE_EFFECTING),
    )(*srcs, *lands, *sems, after)
    return res[n:]


def all_gather_many(name, xs):
    return exchange_call(name, xs, scatter=False)


def scatter_many(name, xs):
    return exchange_call(name, xs, scatter=True)


_Z = (0, 1024)
_XBC = (1024, 2560)
_DT = (2560, 2576)
_RKV = (2576, 5648)
_PW = (5648, 5744)
_PA = (5744, 5840)
_PG = (5840, 6096)
D_IN = 6096

_SMALL = ("norm_mix_g", "ssd_conv_b", "ssd_dt_bias", "ssd_a_log", "ssd_d", "ssd_norm_g", "rwkv_mu", "rwkv_w0",
          "rwkv_a0", "rwkv_k_k", "rwkv_k_a", "rwkv_r_k", "rwkv_ln_w", "rwkv_ln_b", "norm_x_g", "norm_mem_g",
          "norm_ffn_g", "final_norm_g")
_WEIGHTS = ("norm_mix_g", "w_in", "ssd_conv_w", "ssd_conv_b", "ssd_dt_bias", "ssd_a_log", "ssd_d", "ssd_norm_g",
            "rwkv_mu", "rwkv_w0", "rwkv_w2", "rwkv_a0", "rwkv_a2", "rwkv_g2", "rwkv_k_k", "rwkv_k_a", "rwkv_r_k",
            "rwkv_ln_w", "rwkv_ln_b", "w_out", "norm_x_g", "norm_mem_g", "xattn_wq", "xattn_wk", "xattn_wv",
            "xattn_wo", "norm_ffn_g", "ffn_w1", "ffn_w2", "final_norm_g")


def _pad_lanes(x, width=LANES):
    return jnp.pad(x, ((0, 0), (0, width - x.shape[1])))


def _pack_small(vals):
    flat = jnp.concatenate([vals[n].reshape(-1) for n in _SMALL])
    rows = -(-flat.shape[0] // (LANES * SUBLANES)) * SUBLANES
    return jnp.pad(flat, (0, rows * LANES - flat.shape[0])).reshape(rows, LANES)


def _unpack_small(packed, shapes):
    flat = packed.reshape(-1)
    out, pos = {}, 0
    for n in _SMALL:
        size = 1
        for s in shapes[n]:
            size *= s
        out[n] = flat[pos:pos + size].reshape(shapes[n])
        pos += size
    return out


def _cols(w, rng):
    return w[:, rng[0]:rng[1]]


def kernel(x, mem, norm_mix_g, w_in, ssd_conv_w, ssd_conv_b, ssd_dt_bias, ssd_a_log, ssd_d, ssd_norm_g, rwkv_mu, rwkv_w0, rwkv_w2, rwkv_a0, rwkv_a2, rwkv_g2, rwkv_k_k, rwkv_k_a, rwkv_r_k, rwkv_ln_w, rwkv_ln_b, w_out, norm_x_g, norm_mem_g, xattn_wq, xattn_wk, xattn_wv, xattn_wo, norm_ffn_g, ffn_w1, ffn_w2, final_norm_g, loss_target, m_norm_mix_g, m_w_in, m_ssd_conv_w, m_ssd_conv_b, m_ssd_dt_bias, m_ssd_a_log, m_ssd_d, m_ssd_norm_g, m_rwkv_mu, m_rwkv_w0, m_rwkv_w2, m_rwkv_a0, m_rwkv_a2, m_rwkv_g2, m_rwkv_k_k, m_rwkv_k_a, m_rwkv_r_k, m_rwkv_ln_w, m_rwkv_ln_b, m_w_out, m_norm_x_g, m_norm_mem_g, m_xattn_wq, m_xattn_wk, m_xattn_wv, m_xattn_wo, m_norm_ffn_g, m_ffn_w1, m_ffn_w2, m_final_norm_g, v_norm_mix_g, v_w_in, v_ssd_conv_w, v_ssd_conv_b, v_ssd_dt_bias, v_ssd_a_log, v_ssd_d, v_ssd_norm_g, v_rwkv_mu, v_rwkv_w0, v_rwkv_w2, v_rwkv_a0, v_rwkv_a2, v_rwkv_g2, v_rwkv_k_k, v_rwkv_k_a, v_rwkv_r_k, v_rwkv_ln_w, v_rwkv_ln_b, v_w_out, v_norm_x_g, v_norm_mem_g, v_xattn_wq, v_xattn_wk, v_xattn_wv, v_xattn_wo, v_norm_ffn_g, v_ffn_w1, v_ffn_w2, v_final_norm_g):
    given = dict(locals())
    wts = {n: given[n] for n in _WEIGHTS}
    mom_m = {n: given["m_" + n] for n in _WEIGHTS}
    mom_v = {n: given["v_" + n] for n in _WEIGHTS}
    d = D_MODEL
    xt, memt, tgt = x[0], mem[0], loss_target[0]
    tm = 256

    big = {"w_in": w_in[0], "w_out": w_out[0], "xattn_wq": xattn_wq[0], "xattn_wk": xattn_wk[0],
           "xattn_wv": xattn_wv[0], "xattn_wo": xattn_wo[0], "ffn_w1": ffn_w1[0], "ffn_w2": ffn_w2[0]}
    small_sh = {"ssd_conv_w": ssd_conv_w.reshape(4, -1), "rwkv_w2": rwkv_w2[0], "rwkv_a2": rwkv_a2[0],
                "rwkv_g2": rwkv_g2[0]}
    cast = {n: rowwise_fwd("cast_" + n, cast_fn, [a], [], [(a.shape[1], BF16)], 256)[0] for n, a in big.items()}
    gathered = all_gather_many("gather_weights", [cast["w_in"]] + list(small_sh.values()))
    g_big = {"w_in": gathered[0]}
    g_small = dict(zip(small_sh, gathered[1:]))
    late_a = ("w_out", "xattn_wq", "xattn_wk", "xattn_wv", "xattn_wo")
    late_b = ("ffn_w1", "ffn_w2")
    gather_a, token_a = exchange_start("gather_attn_start", [cast[n] for n in late_a], scatter=False, dep=gathered[0])
    gather_b, token_b = exchange_start("gather_ffn_start", [cast[n] for n in late_b], scatter=False, dep=token_a)

    w_in_full = jnp.transpose(g_big["w_in"], (1, 0, 2)).reshape(d, D_IN)
    w_z, w_xbc, w_rkv, w_pg = (_cols(w_in_full, r) for r in (_Z, _XBC, _RKV, _PG))
    w_sm = jnp.concatenate([_pad_lanes(_cols(w_in_full, r)) for r in (_PW, _PA, _DT)], axis=1)
    unshard_cols = lambda g: jnp.transpose(g, (1, 0, 2)).reshape(g.shape[1], -1)
    conv_w_f = unshard_cols(g_small["ssd_conv_w"])
    pad_rows = lambda a: jnp.pad(a, ((0, LANES - a.shape[0]), (0, 0)))
    w2p, a2p = pad_rows(unshard_cols(g_small["rwkv_w2"])), pad_rows(unshard_cols(g_small["rwkv_a2"]))
    g2_f = unshard_cols(g_small["rwkv_g2"])

    mu = rwkv_mu
    mu_rkv, mu_pg = mu[:, :3072], mu[:, 3264:3520]
    mu_pwa = jnp.concatenate([_pad_lanes(mu[:, 3072:3168]), _pad_lanes(mu[:, 3168:3264])], axis=1)
    dt_bias_p, a_log_p, d_p = _pad_lanes(ssd_dt_bias), _pad_lanes(ssd_a_log), _pad_lanes(ssd_d)
    r_k_row = rwkv_r_k.reshape(1, RWKV_WIDTH)
    g_final = final_norm_g.reshape(1, d)

    (h1,) = rowwise_fwd("norm_mix", rmsnorm_fn, [xt], [norm_mix_g], [(d, BF16)], tm, deps=[token_b])
    u_z = mm("in_z", h1, w_z, "nn")
    u_xbc = mm("in_xbc", h1, w_xbc, "nn")
    u_rkv = mm("in_rkv", h1, w_rkv, "nn")
    u_pg = mm("in_pg", h1, w_pg, "nn")
    u_sm = mm("in_small", h1, w_sm, "nn")

    ssd_pre_rows = lambda: [Rows(u_xbc, shifts=(1, 2, 3)), Rows(u_sm, LANES, 2)]
    ssd_pre_params = [conv_w_f, ssd_conv_b, dt_bias_p]
    xs, bm, cm, dt = rowwise_fwd("ssd_pre", ssd_pre_fn, ssd_pre_rows(), ssd_pre_params,
                                 [(SSD_WIDTH, F32), (256, F32), (256, F32), (LANES, F32)], tm)
    y_scan, ssd_ck = ssd_scan_fwd(xs, bm, cm, dt, a_log_p, d_p)
    (y_ssd,) = rowwise_fwd("ssd_post", ssd_post_fn, [y_scan, u_z], [ssd_norm_g], [(SSD_WIDTH, BF16)], tm)

    rwkv_pre_rows = lambda: [Rows(u_rkv, shifts=(1,)), Rows(u_pg, shifts=(1,)), Rows(u_sm, 2 * LANES, 0, shifts=(1,))]
    rwkv_pre_params = [mu_rkv, mu_pg, mu_pwa, rwkv_w0, w2p, rwkv_a0, a2p, g2_f, rwkv_k_k, rwkv_k_a]
    r_, lw_, k_, v_, kap_, b_, gate_ = rowwise_fwd("rwkv_pre", rwkv_pre_fn, rwkv_pre_rows(), rwkv_pre_params,
                                                   [(RWKV_WIDTH, F32)] * 7, 128)
    ys_r, rwkv_ck = rwkv_scan_fwd(r_, lw_, k_, v_, kap_, b_)
    g_big.update(zip(late_a, exchange_wait("gather_attn_wait", gather_a, after=ys_r)))
    w_out_f = g_big["w_out"].reshape(d, d)
    wq_f, wk_f, wv_f, wo_f = (g_big[n].reshape(d, d) for n in ("xattn_wq", "xattn_wk", "xattn_wv", "xattn_wo"))
    rwkv_post_params = [rwkv_ln_w, rwkv_ln_b, r_k_row]
    (y_rwkv,) = rowwise_fwd("rwkv_post", rwkv_post_fn, [ys_r, r_, k_, v_, gate_], rwkv_post_params,
                            [(RWKV_WIDTH, BF16)], tm)
    ycat = jnp.concatenate([y_ssd, y_rwkv], axis=1)
    x1 = mm("out_proj", ycat, w_out_f, "nn", res=xt)

    (h2,) = rowwise_fwd("norm_x", rmsnorm_fn, [x1], [norm_x_g], [(d, BF16)], tm)
    (mn,) = rowwise_fwd("norm_mem", rmsnorm_fn, [memt], [norm_mem_g], [(d, BF16)], tm)
    q = mm("xattn_q", h2, wq_f, "nn", out_dtype=BF16)
    kx = mm("xattn_k", mn, wk_f, "nn")
    vx = mm("xattn_v", mn, wv_f, "nn")
    (o,) = rowwise_fwd("xattn", attn_fn, [q], [kx, vx], [(d, BF16)], tm)
    x2 = mm("xattn_o", o, wo_f, "nn", res=x1)

    (h3,) = rowwise_fwd("norm_ffn", rmsnorm_fn, [x2], [norm_ffn_g], [(d, BF16)], tm)
    w1_s, w2_g = exchange_wait("gather_ffn_wait", gather_b, after=h3)
    w2_f = w2_g.reshape(D_FF, d)
    relu2_epi = lambda acc: (jnp.square(jnp.maximum(acc, 0.0)), jnp.maximum(acc, 0.0))
    hid, relu_a = mm("ffn_1", h3, w1_s, "nn", b_slabs=N_DEV, epi=relu2_epi, out_dtypes=[BF16, BF16])
    x3 = mm("ffn_2", hid, w2_f, "nn", res=x2)

    loss_blk, dx3, dg_final = loss_and_grad(x3, tgt, g_final, tm)

    grads = {}
    grads["ffn_w2"] = mm("d_ffn_w2", hid, dx3, "tn", out_dtype=BF16).reshape(N_DEV, D_FF // N_DEV, d)
    sc_w2, tok = exchange_start("scatter_ffn_w2_start", [grads["ffn_w2"]], scatter=True)
    da = mm("d_hid", dx3, w2_f, "nt", dep=tok, epi=lambda acc, ra: (2.0 * acc * ra,), extras=[relu_a],
            out_dtypes=[BF16])
    grads["ffn_w1"] = mm("d_ffn_w1", h3, da, "tn", out_dtype=BF16, out_slabs=N_DEV)
    sc_w1, tok = exchange_start("scatter_ffn_w1_start", [grads["ffn_w1"]], scatter=True)
    dh3 = mm("d_h3", da, w1_s, "nt", b_slabs=N_DEV, dep=tok)
    (dx2,), (dg_ffn,) = rowwise_bwd("norm_ffn_bwd", rmsnorm_fn, [x2], [norm_ffn_g], [[dh3]], tm, [F32], row_add=[dx3])

    grads["xattn_wo"] = mm("d_wo", o, dx2, "tn", out_dtype=BF16).reshape(N_DEV, d // N_DEV, d)
    sc_wo, tok = exchange_start("scatter_wo_start", [grads["xattn_wo"]], scatter=True)
    d_o = mm("d_o", dx2, wo_f, "nt", dep=tok)
    (dq,), (dkx, dvx) = rowwise_bwd("xattn_bwd", attn_fn, [q], [kx, vx], [[d_o]], tm, [BF16])
    grads["xattn_wq"] = mm("d_wq", h2, dq, "tn", out_dtype=BF16).reshape(N_DEV, d // N_DEV, d)
    dh2 = mm("d_h2", dq, wq_f, "nt")
    grads["xattn_wk"] = mm("d_wk", mn, dkx, "tn", out_dtype=BF16).reshape(N_DEV, d // N_DEV, d)
    grads["xattn_wv"] = mm("d_wv", mn, dvx, "tn", out_dtype=BF16).reshape(N_DEV, d // N_DEV, d)
    qkv = ("xattn_wq", "xattn_wk", "xattn_wv")
    sc_qkv, tok = exchange_start("scatter_qkv_start", [grads[n] for n in qkv], scatter=True)
    dmn = mm("d_mn_v", dvx, wv_f, "nt", res=mm("d_mn_k", dkx, wk_f, "nt", dep=tok))
    _, (dg_mem,) = rowwise_bwd("norm_mem_bwd", rmsnorm_fn, [memt], [norm_mem_g], [[dmn]], tm, [None])
    (dx1,), (dg_x,) = rowwise_bwd("norm_x_bwd", rmsnorm_fn, [x1], [norm_x_g], [[dh2]], tm, [F32], row_add=[dx2])

    grads["w_out"] = mm("d_w_out", ycat, dx1, "tn", out_dtype=BF16).reshape(N_DEV, d // N_DEV, d)
    sc_wout, tok = exchange_start("scatter_w_out_start", [grads["w_out"]], scatter=True)
    d_ycat = mm("d_ycat", dx1, w_out_f, "nt", dep=tok)

    (d_ys, d_r1, d_k1, d_v1, d_gate), (dln_w, dln_b, dr_k) = rowwise_bwd(
        "rwkv_post_bwd", rwkv_post_fn, [ys_r, r_, k_, v_, gate_], rwkv_post_params,
        [[Rows(d_ycat, RWKV_WIDTH, 1)]], tm, [F32] * 5)
    d_r2, d_lw, d_k2, d_v2, d_kap, d_b = rwkv_scan_bwd(r_, lw_, k_, v_, kap_, b_, rwkv_ck, d_ys)
    (du_rkv, du_pg, du_pwa), rwkv_pg = rowwise_bwd(
        "rwkv_pre_bwd", rwkv_pre_fn, rwkv_pre_rows(), rwkv_pre_params,
        [[d_r1, d_r2], [d_lw], [d_k1, d_k2], [d_v1, d_v2], [d_kap], [d_b], [d_gate]], 128, [BF16] * 3)
    dmu_rkv, dmu_pg, dmu_pwa, dw0, dw2p, da0, da2p, dg2, dk_k, dk_a = rwkv_pg

    (d_yscan, du_z), (dssd_norm_g,) = rowwise_bwd("ssd_post_bwd", ssd_post_fn, [y_scan, u_z], [ssd_norm_g],
                                                  [[Rows(d_ycat, SSD_WIDTH, 0)]], tm, [F32, BF16])
    dxs, dbm, dcm, ddt2, da_log_p, dd_p = ssd_scan_bwd(xs, bm, cm, dt, a_log_p, d_p, ssd_ck, d_yscan)
    (du_xbc, du_dt), (dconv_w, dconv_b, ddt_bias_p) = rowwise_bwd(
        "ssd_pre_bwd", ssd_pre_fn, ssd_pre_rows(), ssd_pre_params,
        [[dxs], [dbm], [dcm], [ddt2[0], ddt2[1]]], tm, [BF16, BF16])
    du_sm = jnp.concatenate([du_pwa, du_dt], axis=1)

    dw_z = mm("d_w_z", h1, du_z, "tn", out_dtype=BF16)
    dw_xbc = mm("d_w_xbc", h1, du_xbc, "tn", out_dtype=BF16)
    dw_rkv = mm("d_w_rkv", h1, du_rkv, "tn", out_dtype=BF16)
    dw_pg = mm("d_w_pg", h1, du_pg, "tn", out_dtype=BF16)
    dw_sm = mm("d_w_small", h1, du_sm, "tn", out_dtype=BF16)
    dw_in_full = jnp.concatenate([dw_z, dw_xbc, dw_sm[:, 256:272], dw_rkv, dw_sm[:, 0:96], dw_sm[:, 128:224], dw_pg], axis=1)
    to_slabs = lambda g: jnp.transpose(g.reshape(g.shape[0], N_DEV, -1), (1, 0, 2))
    grads["w_in"] = to_slabs(dw_in_full)
    grads["ssd_conv_w"] = to_slabs(dconv_w)
    grads["rwkv_w2"] = to_slabs(dw2p[:96])
    grads["rwkv_a2"] = to_slabs(da2p[:96])
    grads["rwkv_g2"] = to_slabs(dg2)
    tail = ("w_in", "ssd_conv_w", "rwkv_w2", "rwkv_a2", "rwkv_g2")
    sc_tail, tok = exchange_start("scatter_tail_start", [grads[n] for n in tail], scatter=True)
    dh1 = mm("d_h1_z", du_z, w_z, "nt", dep=tok)
    dh1 = mm("d_h1_xbc", du_xbc, w_xbc, "nt", res=dh1)
    dh1 = mm("d_h1_rkv", du_rkv, w_rkv, "nt", res=dh1)
    dh1 = mm("d_h1_pg", du_pg, w_pg, "nt", res=dh1)
    dh1 = mm("d_h1_small", du_sm, w_sm, "nt", res=dh1)
    (dx,), (dg_mix,) = rowwise_bwd("norm_mix_bwd", rmsnorm_fn, [xt], [norm_mix_g], [[dh1]], tm, [F32], row_add=[dx1])

    dmu =jnp.concatenate([dmu_rkv, dmu_pwa[:, 0:96], dmu_pwa[:, 128:224], dmu_pg], axis=1)
    small_grads = {
        "norm_mix_g": dg_mix, "ssd_conv_b": dconv_b, "ssd_dt_bias": ddt_bias_p[:, :16], "ssd_a_log": da_log_p[:, :16],
        "ssd_d": dd_p[:, :16], "ssd_norm_g": dssd_norm_g, "rwkv_mu": dmu, "rwkv_w0": dw0, "rwkv_a0": da0,
        "rwkv_k_k": dk_k, "rwkv_k_a": dk_a, "rwkv_r_k": dr_k, "rwkv_ln_w": dln_w, "rwkv_ln_b": dln_b,
        "norm_x_g": dg_x, "norm_mem_g": dg_mem, "norm_ffn_g": dg_ffn, "final_norm_g": dg_final}

    received = {}
    for names, handle in ((("ffn_w2",), sc_w2), (("ffn_w1",), sc_w1), (("xattn_wo",), sc_wo), (qkv, sc_qkv),
                          (("w_out",), sc_wout)):
        received.update(zip(names, exchange_wait("scatter_" + names[0] + "_wait", handle, after=dx)))
    (small_all,) = all_gather_many("gather_small_grads", [_pack_small(small_grads)])

    out_g, out_d, out_m, out_v = {}, {}, {}, {}

    def run_adamw(n):
        shape = wts[n].shape
        two_d = lambda a: a.reshape(-1, shape[-1])
        res = adamw("adamw_" + n, received[n].reshape(N_DEV, -1, shape[-1]), two_d(wts[n]), two_d(mom_m[n]), two_d(mom_v[n]))
        out_g[n], out_d[n], out_m[n], out_v[n] = (r.reshape(shape) for r in res)

    for n in ("ffn_w2", "ffn_w1", "xattn_wo") + qkv + ("w_out",):
        run_adamw(n)
    received.update(zip(tail, exchange_wait("scatter_tail_wait", sc_tail, after=out_g["w_out"])))
    for n in tail:
        run_adamw(n)
    res = adamw("adamw_small", small_all, _pack_small(wts), _pack_small(mom_m), _pack_small(mom_v))
    shapes = {n: wts[n].shape for n in _SMALL}
    for dst, packed in zip((out_g, out_d, out_m, out_v), res):
        dst.update(_unpack_small(packed, shapes))

    loss = lax.psum(loss_blk[0, 0], ("x", "y", "c"))
    return (loss, dx[None], *[out_g[n] for n in _WEIGHTS], *[out_d[n] for n in _WEIGHTS],
            *[out_m[n] for n in _WEIGHTS], *[out_v[n] for n in _WEIGHTS])
```

```python
import functools

import jax
import jax.numpy as jnp
from jax import lax
from jax.experimental import pallas as pl
from jax.experimental.pallas import tpu as pltpu

F32 = jnp.float32
BF16 = jnp.bfloat16
HIGHEST = lax.Precision.HIGHEST

N_DEV = 8
D_MODEL = 2048
NORM_EPS = 1e-6
SSD_WIDTH = 1024
SSD_CONV_DIM = 1536
SSD_HEADS = 16
SSD_HEAD_DIM = 64
SSD_STATE = 128
SSD_CHUNK = 128
SSD_HEADS_PER_GROUP = 8
RWKV_WIDTH = 1024
RWKV_HEADS = 16
RWKV_HEAD_DIM = 64
RWKV_LN_EPS = 64e-5
RWKV_CHUNK = 64
RWKV_HEADS_PER_STEP = 16
XATTN_HEADS = 4
XATTN_HEAD_DIM = 512
D_FF = 8192
LANES = 128
SUBLANES = 8
VMEM_LIMIT = 56 * 1024 * 1024

ADAM_LR = 0.001
ADAM_B1 = 0.9
ADAM_B2 = 0.999
ADAM_EPS = 1e-08
ADAM_WD = 0.01
ADAM_STEP = 10

_DN = {"nn": ((1,), (0,)), "nt": ((1,), (1,)), "tn": ((0,), (0,))}


def _dg(a, b, mode, precision=None):
    (ca,), (cb,) = _DN[mode]
    dn = (((ca + 1,), (cb + 1,)), ((0,), (0,))) if a.ndim == 3 else (((ca,), (cb,)), ((), ()))
    return lax.dot_general(a, b, dn, precision=precision, preferred_element_type=F32)


@functools.partial(jax.custom_vjp, nondiff_argnums=(2,))
def bdot(a, b, mode):
    return _dg(a.astype(BF16), b.astype(BF16), mode)


def _bdot_fwd(a, b, mode):
    return bdot(a, b, mode), (a, b)


def _bdot_bwd(mode, res, g):
    a, b = res
    ab, bb, gb = a.astype(BF16), b.astype(BF16), g.astype(BF16)
    if mode == "nn":
        da, db = _dg(gb, bb, "nt"), _dg(ab, gb, "tn")
    elif mode == "nt":
        da, db = _dg(gb, bb, "nn"), _dg(gb, ab, "tn")
    else:
        da, db = _dg(bb, gb, "nt"), _dg(ab, gb, "nn")
    return da.astype(a.dtype), db.astype(b.dtype)


bdot.defvjp(_bdot_fwd, _bdot_bwd)


def fdot(a, b, mode):
    return _dg(a, b, mode, precision=HIGHEST)


def _split3(x):
    hi = x.astype(BF16)
    r1 = x - hi.astype(F32)
    mid = r1.astype(BF16)
    lo = (r1 - mid.astype(F32)).astype(BF16)
    return hi, mid, lo


def _dot01(x, m01):
    hi, mid, _ = _split3(x)
    return _dg(hi, m01, "nn") + _dg(mid, m01, "nn")


def _exact_dot_impl(a, b, mode, exact):
    if exact == "a":
        ae = a.astype(BF16)
        return sum(_dg(ae, part, mode) for part in _split3(b))
    be = b.astype(BF16)
    return sum(_dg(part, be, mode) for part in _split3(a))


@functools.partial(jax.custom_vjp, nondiff_argnums=(2, 3))
def exact_dot(a, b, mode, exact):
    return _exact_dot_impl(a, b, mode, exact)


def _exact_dot_fwd(a, b, mode, exact):
    return _exact_dot_impl(a, b, mode, exact), (a, b)


def _exact_dot_bwd(mode, exact, res, g):
    a, b = res
    if exact == "a":
        db = {"nn": lambda: _exact_dot_impl(a, g, "tn", "a"), "nt": lambda: _exact_dot_impl(g, a, "tn", "b"),
              "tn": lambda: _exact_dot_impl(a, g, "nn", "a")}[mode]()
        return jnp.zeros_like(a), db
    da = {"nn": lambda: _exact_dot_impl(g, b, "nt", "b"), "nt": lambda: _exact_dot_impl(g, b, "nn", "b"),
          "tn": lambda: _exact_dot_impl(b, g, "nt", "a")}[mode]()
    return da, jnp.zeros_like(b)


exact_dot.defvjp(_exact_dot_fwd, _exact_dot_bwd)


def _head_indicator(width, heads, transpose):
    hd = width // heads
    shape = (LANES, width) if transpose else (width, LANES)
    lane = lax.broadcasted_iota(jnp.int32, shape, 1 if not transpose else 0)
    pos = lax.broadcasted_iota(jnp.int32, shape, 0 if not transpose else 1)
    return ((pos >= lane * hd) & (pos < lane * hd + hd)).astype(BF16)


@jax.custom_vjp
def head_sum(x):
    w = x.shape[-1]
    e = _head_indicator(w, w // RWKV_HEAD_DIM, False)
    et = _head_indicator(w, w // RWKV_HEAD_DIM, True)
    return _dot01(_dot01(x, e), et)


head_sum.defvjp(lambda x: (head_sum(x), None), lambda _, g: (head_sum(g),))


def rmsnorm_fn(x, g):
    y = x * lax.rsqrt(jnp.mean(x * x, axis=-1, keepdims=True) + NORM_EPS)
    return ((y * g).astype(BF16),)


def cast_fn(x):
    return (x.astype(BF16),)


def ssd_pre_fn(xbc, xbc1, xbc2, xbc3, dt_raw, conv_w, conv_b, dt_bias):
    c = conv_w[3:4] * xbc + conv_w[2:3] * xbc1 + conv_w[1:2] * xbc2 + conv_w[0:1] * xbc3 + conv_b
    act = c * jax.nn.sigmoid(c)
    dt = jax.nn.softplus(dt_raw + dt_bias)
    return act[:, :SSD_WIDTH], act[:, SSD_WIDTH:SSD_WIDTH + 256], act[:, SSD_WIDTH + 256:], dt


def ssd_post_fn(yscan, z, norm_g):
    y = yscan * (z * jax.nn.sigmoid(z))
    half = SSD_WIDTH // 2
    parts = []
    for g in range(2):
        yg = y[:, g * half:(g + 1) * half]
        parts.append(yg * lax.rsqrt(jnp.mean(yg * yg, axis=-1, keepdims=True) + NORM_EPS))
    return ((jnp.concatenate(parts, axis=-1) * norm_g).astype(BF16),)


def rwkv_pre_fn(rkv, rkv_p, pg, pg_p, pwa, pwa_p, mu_rkv, mu_pg, mu_pwa, w0, w2p, a0, a2p, g2, k_k, k_a):
    w = RWKV_WIDTH
    rkv = rkv + (rkv_p - rkv) * mu_rkv
    pg = pg + (pg_p - pg) * mu_pg
    pwa = pwa + (pwa_p - pwa) * mu_pwa
    r, k, v = rkv[:, :w], rkv[:, w:2 * w], rkv[:, 2 * w:]
    pw, pa = pwa[:, :LANES], pwa[:, LANES:]
    w_log = -jax.nn.softplus(-(w0 + bdot(jnp.tanh(pw), w2p, "nn"))) - 0.5
    lw = -jnp.exp(w_log)
    iclr = jax.nn.sigmoid(a0 + bdot(pa, a2p, "nn"))
    gate = bdot(jax.nn.sigmoid(pg), g2, "nn")
    kk = k * k_k
    kap = kk / jnp.maximum(jnp.sqrt(head_sum(kk * kk)), 1e-12)
    k_mod = k * (1.0 + (iclr - 1.0) * k_a)
    return r, lw, k_mod, v, kap, kap * iclr, gate


def rwkv_post_fn(ys, r, k_mod, v, gate, ln_w, ln_b, r_k):
    inv_n = 1.0 / RWKV_HEAD_DIM
    mean = head_sum(ys) * inv_n
    yc = ys - mean
    var = head_sum(yc * yc) * inv_n
    yn = yc * lax.rsqrt(var + RWKV_LN_EPS) * ln_w + ln_b
    bonus = head_sum(r * k_mod * r_k) * v
    return (((yn + bonus) * gate).astype(BF16),)


def attn_fn(q, kx, vx):
    outs = []
    for h in range(XATTN_HEADS):
        sl = slice(h * XATTN_HEAD_DIM, (h + 1) * XATTN_HEAD_DIM)
        s = bdot(q[:, sl], kx[:, sl], "nt") * (XATTN_HEAD_DIM ** -0.5)
        s = s - jnp.max(s, axis=-1, keepdims=True)
        p = jnp.exp(s)
        p = p / jnp.sum(p, axis=-1, keepdims=True)
        outs.append(bdot(p, vx[:, sl], "nn"))
    return (jnp.concatenate(outs, axis=-1).astype(BF16),)


def loss_fn(x, tgt, g):
    y = x * lax.rsqrt(jnp.mean(x * x, axis=-1, keepdims=True) + NORM_EPS) * g
    err = jnp.square(y - tgt)
    return 0.5 * jnp.sum(jnp.mean(err, axis=-1, keepdims=True), axis=0, keepdims=True)


def _tri_masks(n):
    row = lax.broadcasted_iota(jnp.int32, (n, n), 0)
    col = lax.broadcasted_iota(jnp.int32, (n, n), 1)
    return col <= row, col < row, row == col


def rwkv_chunk_fn(st0, r, lw, k, v, kap, b):
    h, c = r.shape[0], r.shape[1]
    incl, strict, diag = _tri_masks(c)
    cum = exact_dot(jnp.broadcast_to(incl.astype(F32), (h, c, c)), lw, "nn", "a")
    g_in = jnp.exp(cum)
    g_prev = jnp.exp(cum - lw)
    g_inv = jnp.exp(-cum)
    g_end = jnp.exp(cum[:, c - 1:c, :] - cum)
    kap_t, k_t, b_t, r_t = kap * g_prev, k * g_inv, b * g_inv, r * g_in
    a_ub = jnp.where(strict, bdot(kap_t, b_t, "nt"), 0.0)
    a_vk = jnp.where(strict, bdot(kap_t, k_t, "nt"), 0.0)
    rhs = -(bdot(kap_t, st0, "nn") + bdot(a_vk, v, "nn"))
    eye = diag.astype(F32)
    m = -a_ub
    inv = eye + m
    n = 1
    while n * 2 < c:
        m = bdot(m, m, "nn")
        inv = bdot(inv, eye + m, "nn")
        n *= 2
    u = bdot(inv, rhs, "nn")
    y = (bdot(r_t, st0, "nn")
         + bdot(jnp.where(incl, bdot(r_t, k_t, "nt"), 0.0), v, "nn")
         + bdot(jnp.where(incl, bdot(r_t, b_t, "nt"), 0.0), u, "nn"))
    g_full = jnp.exp(exact_dot(lw, jnp.ones((h, c, st0.shape[2]), F32), "tn", "b"))
    st1 = g_full * st0 + bdot(k * g_end, v, "tn") + bdot(b * g_end, u, "tn")
    return y, st1


def ssd_chunk_fn(group, h0, xs, bm, cm, dt, a_log, d_skip):
    q = xs.shape[0]
    causal, _, _ = _tri_masks(q)
    a_row = -jnp.exp(a_log)
    cs_all = exact_dot(causal.astype(F32), dt * a_row, "nn", "a")
    cs_t = cs_all.T
    cb = bdot(cm, bm, "nt")
    ys, hs = [], []
    p = SSD_HEAD_DIM
    for e in range(SSD_HEADS_PER_GROUP):
        hl = group * SSD_HEADS_PER_GROUP + e
        cs = cs_all[:, hl:hl + 1]
        seg = cs - cs_t[hl:hl + 1, :]
        lmat = jnp.where(causal, jnp.exp(jnp.where(causal, seg, 0.0)), 0.0)
        x_h = xs[:, e * p:(e + 1) * p]
        xdt = x_h * dt[:, hl:hl + 1]
        h0_e = h0[e * p:(e + 1) * p, :]
        cl = cs[q - 1:q, :]
        y = bdot(cb * lmat, xdt, "nn") + bdot(cm, h0_e, "nt") * jnp.exp(cs) + x_h * d_skip[:, hl:hl + 1]
        hs.append(h0_e * jnp.exp(cl) + bdot(xdt * jnp.exp(cl - cs), bm, "tn"))
        ys.append(y)
    return jnp.concatenate(ys, axis=-1), jnp.concatenate(hs, axis=0)


class Rows:
    def __init__(self, arr, w=None, cb=0, shifts=()):
        self.arr, self.w, self.cb, self.shifts = arr, (arr.shape[1] if w is None else w), cb, tuple(shifts)


def _as_rows(x):
    return x if isinstance(x, Rows) else Rows(x)


def _shift_down(x, halo, k):
    rolled = pltpu.roll(x, k, 0)
    first = rolled[0:SUBLANES]
    rid = lax.broadcasted_iota(jnp.int32, first.shape, 0)
    patched = jnp.where(rid < k, pltpu.roll(halo, k, 0), first)
    return jnp.concatenate([patched, rolled[SUBLANES:]], axis=0)


def _shift_up(g, carry, k):
    tm = g.shape[0]
    rolled = pltpu.roll(g, tm - k, 0)
    last = rolled[tm - SUBLANES:]
    rid = lax.broadcasted_iota(jnp.int32, last.shape, 0)
    patched = jnp.where(rid >= SUBLANES - k, pltpu.roll(carry, SUBLANES - k, 0), last)
    return jnp.concatenate([rolled[:tm - SUBLANES], patched], axis=0)


def _params():
    return pltpu.CompilerParams(vmem_limit_bytes=VMEM_LIMIT)


def _load_rows(refs, pos, rins, first_block):
    vals = []
    for r in rins:
        x = refs[pos][...].astype(F32) if refs[pos].dtype != F32 else refs[pos][...]
        pos += 1
        vals.append(x)
        if r.shifts:
            halo = refs[pos][...]
            pos += 1
            halo = jnp.where(first_block, jnp.zeros_like(halo), halo)
            for k in r.shifts:
                vals.append(_shift_down(x, halo, k))
    return vals, pos


def _row_specs(rins, tm, blk):
    specs, args = [], []
    for r in rins:
        specs.append(pl.BlockSpec((tm, r.w), lambda i, cb=r.cb: (blk(i), cb)))
        args.append(r.arr)
        if r.shifts:
            per = tm // SUBLANES
            specs.append(pl.BlockSpec((SUBLANES, r.w), lambda i, cb=r.cb: (jnp.maximum(blk(i) * per - 1, 0), cb)))
            args.append(r.arr)
    return specs, args


def rowwise_fwd(name, fn, rins, params, outs, tm, deps=()):
    rins = [_as_rows(r) for r in rins]
    t = rins[0].arr.shape[0]
    tm = min(tm, t)
    nb = t // tm
    specs, args = _row_specs(rins, tm, lambda i: i)
    for p in params:
        specs.append(pl.BlockSpec(p.shape, lambda i: (0, 0)))
        args.append(p)
    for dep in deps:
        specs.append(pl.BlockSpec(dep.shape, lambda i: (0, 0)))
        args.append(dep)
    n_in = len(args)

    def body(*refs):
        vals, pos = _load_rows(refs, 0, rins, pl.program_id(0) == 0)
        pv = [refs[pos + j][...] for j in range(len(params))]
        res = fn(*vals, *pv)
        for o_ref, o in zip(refs[n_in:], res):
            o_ref[...] = o.astype(o_ref.dtype)

    return pl.pallas_call(
        body, name=name, grid=(nb,), in_specs=specs,
        out_specs=[pl.BlockSpec((tm, w), lambda i: (i, 0)) for w, _ in outs],
        out_shape=[jax.ShapeDtypeStruct((t, w), dt) for w, dt in outs],
        compiler_params=_params(),
    )(*args)


def rowwise_bwd(name, fn, rins, params, cts, tm, grad_dtypes, row_add=None):
    rins = [_as_rows(r) for r in rins]
    cts = [[_as_rows(c) for c in lst] for lst in cts]
    row_add = [_as_rows(a) for a in (row_add or [])]
    t = rins[0].arr.shape[0]
    tm = min(tm, t)
    nb = t // tm
    rev = lambda i: nb - 1 - i
    specs, args = _row_specs(rins, tm, rev)
    for p in params:
        specs.append(pl.BlockSpec(p.shape, lambda i: (0, 0)))
        args.append(p)
    flat_cts = [c for lst in cts for c in lst] + row_add
    for c in flat_cts:
        specs.append(pl.BlockSpec((tm, c.w), lambda i, cb=c.cb: (rev(i), cb)))
        args.append(c.arr)
    n_in = len(args)
    want = [i for i, d in enumerate(grad_dtypes) if d is not None]
    out_specs = [pl.BlockSpec((tm, rins[i].w), lambda i_: (rev(i_), 0)) for i in want]
    out_shape = [jax.ShapeDtypeStruct((t, rins[i].w), grad_dtypes[i]) for i in want]
    out_specs += [pl.BlockSpec(p.shape, lambda i: (0, 0)) for p in params]
    out_shape += [jax.ShapeDtypeStruct(p.shape, F32) for p in params]
    n_out = len(out_shape)
    scratch = [pltpu.VMEM((SUBLANES, r.w), F32) for r in rins for _ in r.shifts]

    def body(*refs):
        i = pl.program_id(0)
        vals, pos = _load_rows(refs, 0, rins, rev(i) == 0)
        pv = [refs[pos + j][...] for j in range(len(params))]
        pos += len(params)
        outs, vjp = jax.vjp(fn, *vals, *pv)
        ct_vals = []
        for o, lst in zip(outs, cts):
            acc = None
            for _ in lst:
                cv = refs[pos][...].astype(F32)
                pos += 1
                acc = cv if acc is None else acc + cv
            ct_vals.append(acc.astype(o.dtype))
        adds = [refs[pos + j][...].astype(F32) for j in range(len(row_add))]
        grads = vjp(tuple(ct_vals))
        out_refs = refs[n_in:n_in + n_out]
        carry_refs = refs[n_in + n_out:]

        @pl.when(i == 0)
        def _():
            for cr in carry_refs:
                cr[...] = jnp.zeros_like(cr)
            for pr in out_refs[len(want):]:
                pr[...] = jnp.zeros_like(pr)

        gi, ci, oi = 0, 0, 0
        for idx, r in enumerate(rins):
            d = grads[gi]
            gi += 1
            for k in r.shifts:
                dk = grads[gi]
                gi += 1
                d = d + _shift_up(dk, carry_refs[ci][...], k)
                carry_refs[ci][...] = dk[0:SUBLANES]
                ci += 1
            if idx == 0:
                for a in adds:
                    d = d + a
            if grad_dtypes[idx] is not None:
                out_refs[oi][...] = d.astype(out_refs[oi].dtype)
                oi += 1
        for pr, gp in zip(out_refs[len(want):], grads[gi:]):
            pr[...] += gp

    res = pl.pallas_call(
        body, name=name, grid=(nb,), in_specs=specs, out_specs=out_specs, out_shape=out_shape,
        scratch_shapes=scratch, compiler_params=_params(),
    )(*args)
    return res[:len(want)], res[len(want):]


def _pick(n, pref):
    for c in pref:
        if n % c == 0:
            return c
    return n


MM_VMEM_BUDGET = 40 * 1024 * 1024
MM_PEAK_FLOPS = 0.9e15
MM_HBM_BYTES_PER_S = 3.0e12
MM_STEP_SECONDS = 0.35e-6


def _mm_tiles(m, n, k, size_a, size_b, size_out, size_res, single_k):
    best = None
    for tk in sorted({c for c in (k, 2048, 1024, 512, 256, 128) if c <= 2048 and k % c == 0}, reverse=True):
        for tm in (1024, 512, 256, 128):
            if m % tm:
                continue
            for tn in (1024, 768, 512, 384, 256, 128):
                if n % tn:
                    continue
                nk = k // tk
                vmem = 2 * (tm * tk * size_a + tk * tn * size_b + tm * tn * (size_out + size_res))
                vmem += tm * tn * 4 * (2 if nk > 1 or not single_k else 1)
                vmem += (tm * tk * 2 if size_a > 2 else 0) + (tk * tn * 2 if size_b > 2 else 0)
                if vmem > MM_VMEM_BUDGET:
                    continue
                steps = (m // tm) * (n // tn) * nk
                a_reads = 1 if (nk == 1 and single_k) else n // tn
                traffic = m * k * size_a * a_reads + k * n * size_b * (m // tm) + m * n * (size_out + size_res)
                cost = max(2.0 * m * n * k / MM_PEAK_FLOPS, traffic / MM_HBM_BYTES_PER_S) + steps * MM_STEP_SECONDS
                if best is None or cost < best[0]:
                    best = (cost, tm, tn, tk)
    return best[1:]


def mm(name, a, b, mode, out_dtype=F32, res=None, b_slabs=None, out_slabs=None, dep=None, epi=None, extras=(),
       out_dtypes=None):
    if mode == "tn":
        k_dim, m_dim = a.shape
    else:
        m_dim, k_dim = a.shape
    if b_slabs:
        n_dim = b.shape[0] * b.shape[2] if mode == "nn" else b.shape[1]
    else:
        n_dim = b.shape[0] if mode == "nt" else b.shape[1]
    n_slabs = out_slabs or (b_slabs if (b_slabs and mode == "nn") else 1)
    k_slabs = b_slabs if (b_slabs and mode == "nt") else 1
    if epi is None:
        out_dtypes = [out_dtype]
        if res is None:
            epi = lambda acc: (acc,)
        else:
            extras, epi = [res], lambda acc, r: (acc + r,)
    tm, tn, tk = _mm_tiles(m_dim, n_dim // n_slabs, k_dim // k_slabs, a.dtype.itemsize, b.dtype.itemsize,
                           sum(jnp.dtype(dt).itemsize for dt in out_dtypes), sum(e.dtype.itemsize for e in extras),
                           single_k=(k_slabs == 1))
    nji = n_dim // n_slabs // tn
    nki = k_dim // k_slabs // tk
    nblk = lambda js, j: js * nji + j
    kblk = lambda ks, k: ks * nki + k
    if mode == "tn":
        a_spec = pl.BlockSpec((tk, tm), lambda i, js, j, ks, k: (kblk(ks, k), i))
    else:
        a_spec = pl.BlockSpec((tm, tk), lambda i, js, j, ks, k: (i, kblk(ks, k)))
    if b_slabs and mode == "nn":
        b_spec = pl.BlockSpec((None, tk, tn), lambda i, js, j, ks, k: (js, k, j))
    elif b_slabs and mode == "nt":
        b_spec = pl.BlockSpec((None, tn, tk), lambda i, js, j, ks, k: (ks, nblk(js, j), k))
    elif mode == "nt":
        b_spec = pl.BlockSpec((tn, tk), lambda i, js, j, ks, k: (nblk(js, j), kblk(ks, k)))
    else:
        b_spec = pl.BlockSpec((tk, tn), lambda i, js, j, ks, k: (kblk(ks, k), nblk(js, j)))
    specs, args = [a_spec, b_spec], [a, b]
    for e in extras:
        specs.append(pl.BlockSpec((tm, tn), lambda i, js, j, ks, k: (i, nblk(js, j))))
        args.append(e)
    if dep is not None:
        specs.append(pl.BlockSpec(dep.shape, lambda i, js, j, ks, k: (0, 0)))
        args.append(dep)
    if out_slabs:
        o_specs = [pl.BlockSpec((None, tm, tn), lambda i, js, j, ks, k: (js, i, j))]
        o_shapes = [jax.ShapeDtypeStruct((out_slabs, m_dim, n_dim // out_slabs), out_dtypes[0])]
    else:
        o_specs = [pl.BlockSpec((tm, tn), lambda i, js, j, ks, k: (i, nblk(js, j))) for _ in out_dtypes]
        o_shapes = [jax.ShapeDtypeStruct((m_dim, n_dim), dt) for dt in out_dtypes]

    one_k_step = k_slabs * nki == 1
    n_in, n_out = len(args), len(out_dtypes)

    def body(*refs):
        a_ref, b_ref = refs[0], refs[1]
        part = _dg(a_ref[...].astype(BF16), b_ref[...].astype(BF16), mode)

        def finish(acc):
            outs = epi(acc, *[refs[2 + j][...].astype(F32) for j in range(len(extras))])
            for o_ref, o in zip(refs[n_in:n_in + n_out], outs):
                o_ref[...] = o.astype(o_ref.dtype)

        if one_k_step:
            finish(part)
            return
        acc_ref = refs[n_in + n_out]
        ks, kk = pl.program_id(3), pl.program_id(4)

        @pl.when((ks == 0) & (kk == 0))
        def _():
            acc_ref[...] = part

        @pl.when((ks > 0) | (kk > 0))
        def _():
            acc_ref[...] += part

        pl.when((ks == k_slabs - 1) & (kk == nki - 1))(lambda: finish(acc_ref[...]))

    grid = (m_dim // tm, n_slabs, nji, k_slabs, nki)
    scratch = [] if one_k_step else [pltpu.VMEM((tm, tn), F32)]
    out = pl.pallas_call(
        body, name=name, grid=grid, in_specs=specs, out_specs=o_specs, out_shape=o_shapes, scratch_shapes=scratch,
        compiler_params=pltpu.CompilerParams(
            dimension_semantics=("parallel", "parallel", "parallel", "arbitrary", "arbitrary"),
            vmem_limit_bytes=VMEM_LIMIT),
    )(*args)
    return out[0] if n_out == 1 else out


def _stack_lanes(x, n):
    w = x.shape[1] // n
    return jnp.stack([x[:, i * w:(i + 1) * w] for i in range(n)])


def _stack_rows(x, n):
    w = x.shape[0] // n
    return jnp.stack([x[i * w:(i + 1) * w, :] for i in range(n)])


def rwkv_scan_fwd(r, lw, k, v, kap, b):
    t = r.shape[0]
    c, hps, hd = min(RWKV_CHUNK, t), RWKV_HEADS_PER_STEP, RWKV_HEAD_DIM
    nc, ng, wl = t // c, RWKV_HEADS // hps, hps * hd
    spec = pl.BlockSpec((c, wl), lambda g, ci: (ci, g))

    def body(r_ref, lw_ref, k_ref, v_ref, kap_ref, b_ref, y_ref, ck_ref, st_ref):
        @pl.when(pl.program_id(1) == 0)
        def _():
            st_ref[...] = jnp.zeros_like(st_ref)

        st = st_ref[...]
        ck_ref[...] = st
        ins = [x[...] for x in (r_ref, lw_ref, k_ref, v_ref, kap_ref, b_ref)]
        y, st1 = rwkv_chunk_fn(_stack_rows(st, hps), *[_stack_lanes(x, hps) for x in ins])
        y_ref[...] = jnp.concatenate([y[h] for h in range(hps)], axis=-1)
        st_ref[...] = jnp.concatenate([st1[h] for h in range(hps)], axis=0)

    return pl.pallas_call(
        body, name="rwkv_scan_fwd", grid=(ng, nc), in_specs=[spec] * 6,
        out_specs=[spec, pl.BlockSpec((None, wl, hd), lambda g, ci: (ci, g, 0))],
        out_shape=[jax.ShapeDtypeStruct((t, RWKV_WIDTH), F32), jax.ShapeDtypeStruct((nc, RWKV_WIDTH, hd), F32)],
        scratch_shapes=[pltpu.VMEM((wl, hd), F32)], compiler_params=_params(),
    )(r, lw, k, v, kap, b)


def rwkv_scan_bwd(r, lw, k, v, kap, b, ck, dy):
    t = r.shape[0]
    c, hps, hd = min(RWKV_CHUNK, t), RWKV_HEADS_PER_STEP, RWKV_HEAD_DIM
    nc, ng, wl = t // c, RWKV_HEADS // hps, hps * hd
    spec = pl.BlockSpec((c, wl), lambda g, ci: (nc - 1 - ci, g))

    def body(r_ref, lw_ref, k_ref, v_ref, kap_ref, b_ref, ck_ref, dy_ref, *rest):
        out_refs, dst_ref = rest[:6], rest[6]

        @pl.when(pl.program_id(1) == 0)
        def _():
            dst_ref[...] = jnp.zeros_like(dst_ref)

        ins = [x[...] for x in (r_ref, lw_ref, k_ref, v_ref, kap_ref, b_ref)]
        dyv, ck, dst = dy_ref[...], ck_ref[...], dst_ref[...]
        _, vjp = jax.vjp(rwkv_chunk_fn, _stack_rows(ck, hps), *[_stack_lanes(x, hps) for x in ins])
        grads = vjp((_stack_lanes(dyv, hps), _stack_rows(dst, hps)))
        dst_ref[...] = jnp.concatenate([grads[0][h] for h in range(hps)], axis=0)
        for j in range(6):
            out_refs[j][...] = jnp.concatenate([grads[1 + j][h] for h in range(hps)], axis=-1)

    return pl.pallas_call(
        body, name="rwkv_scan_bwd", grid=(ng, nc),
        in_specs=[spec] * 6 + [pl.BlockSpec((None, wl, hd), lambda g, ci: (nc - 1 - ci, g, 0)), spec],
        out_specs=[spec] * 6, out_shape=[jax.ShapeDtypeStruct((t, RWKV_WIDTH), F32)] * 6,
        scratch_shapes=[pltpu.VMEM((wl, hd), F32)], compiler_params=_params(),
    )(r, lw, k, v, kap, b, ck, dy)


def _ssd_specs(q, blk):
    gw = SSD_WIDTH // 2
    return [pl.BlockSpec((q, gw), lambda g, ci: (blk(ci), g)),
            pl.BlockSpec((q, SSD_STATE), lambda g, ci: (blk(ci), g)),
            pl.BlockSpec((q, SSD_STATE), lambda g, ci: (blk(ci), g)),
            pl.BlockSpec((q, LANES), lambda g, ci: (blk(ci), 0)),
            pl.BlockSpec((1, LANES), lambda g, ci: (0, 0)),
            pl.BlockSpec((1, LANES), lambda g, ci: (0, 0))]


def ssd_scan_fwd(xs, bm, cm, dt, a_log, d_skip):
    t = xs.shape[0]
    q = min(SSD_CHUNK, t)
    nc, gw = t // q, SSD_WIDTH // 2

    def body(xs_ref, bm_ref, cm_ref, dt_ref, al_ref, d_ref, y_ref, ck_ref, h_ref):
        @pl.when(pl.program_id(1) == 0)
        def _():
            h_ref[...] = jnp.zeros_like(h_ref)

        ck_ref[...] = h_ref[...]
        args = (h_ref[...], xs_ref[...], bm_ref[...], cm_ref[...], dt_ref[...], al_ref[...], d_ref[...])
        g = pl.program_id(0)

        @pl.when(g == 0)
        def _():
            y, h1 = ssd_chunk_fn(0, *args)
            y_ref[...] = y
            h_ref[...] = h1

        @pl.when(g == 1)
        def _():
            y, h1 = ssd_chunk_fn(1, *args)
            y_ref[...] = y
            h_ref[...] = h1

    return pl.pallas_call(
        body, name="ssd_scan_fwd", grid=(2, nc), in_specs=_ssd_specs(q, lambda ci: ci),
        out_specs=[pl.BlockSpec((q, gw), lambda g, ci: (ci, g)),
                   pl.BlockSpec((None, gw, SSD_STATE), lambda g, ci: (ci, g, 0))],
        out_shape=[jax.ShapeDtypeStruct((t, SSD_WIDTH), F32), jax.ShapeDtypeStruct((nc, SSD_WIDTH, SSD_STATE), F32)],
        scratch_shapes=[pltpu.VMEM((gw, SSD_STATE), F32)], compiler_params=_params(),
    )(xs, bm, cm, dt, a_log, d_skip)


def ssd_scan_bwd(xs, bm, cm, dt, a_log, d_skip, ck, dy):
    t = xs.shape[0]
    q = min(SSD_CHUNK, t)
    nc, gw = t // q, SSD_WIDTH // 2
    rev = lambda ci: nc - 1 - ci

    def body(xs_ref, bm_ref, cm_ref, dt_ref, al_ref, d_ref, ck_ref, dy_ref,
             dxs_ref, dbm_ref, dcm_ref, ddt_ref, dal_ref, dd_ref, dh_ref):
        g, ci = pl.program_id(0), pl.program_id(1)

        @pl.when(ci == 0)
        def _():
            dh_ref[...] = jnp.zeros_like(dh_ref)

        @pl.when((ci == 0) & (g == 0))
        def _():
            dal_ref[...] = jnp.zeros_like(dal_ref)
            dd_ref[...] = jnp.zeros_like(dd_ref)

        args = (ck_ref[...], xs_ref[...], bm_ref[...], cm_ref[...], dt_ref[...], al_ref[...], d_ref[...])

        def run(group):
            _, vjp = jax.vjp(functools.partial(ssd_chunk_fn, group), *args)
            dh0, dxs, dbm, dcm, ddt, dal, dd = vjp((dy_ref[...], dh_ref[...]))
            dh_ref[...] = dh0
            dxs_ref[...] = dxs
            dbm_ref[...] = dbm
            dcm_ref[...] = dcm
            ddt_ref[...] = ddt
            dal_ref[...] += dal
            dd_ref[...] += dd

        pl.when(g == 0)(lambda: run(0))
        pl.when(g == 1)(lambda: run(1))

    in_specs = _ssd_specs(q, rev) + [pl.BlockSpec((None, gw, SSD_STATE), lambda g, ci: (rev(ci), g, 0)),
                                     pl.BlockSpec((q, gw), lambda g, ci: (rev(ci), g))]
    return pl.pallas_call(
        body, name="ssd_scan_bwd", grid=(2, nc), in_specs=in_specs,
        out_specs=[pl.BlockSpec((q, gw), lambda g, ci: (rev(ci), g)),
                   pl.BlockSpec((q, SSD_STATE), lambda g, ci: (rev(ci), g)),
                   pl.BlockSpec((q, SSD_STATE), lambda g, ci: (rev(ci), g)),
                   pl.BlockSpec((None, q, LANES), lambda g, ci: (g, rev(ci), 0)),
                   pl.BlockSpec((1, LANES), lambda g, ci: (0, 0)),
                   pl.BlockSpec((1, LANES), lambda g, ci: (0, 0))],
        out_shape=[jax.ShapeDtypeStruct((t, SSD_WIDTH), F32), jax.ShapeDtypeStruct((t, 2 * SSD_STATE), F32),
                   jax.ShapeDtypeStruct((t, 2 * SSD_STATE), F32), jax.ShapeDtypeStruct((2, t, LANES), F32),
                   jax.ShapeDtypeStruct((1, LANES), F32), jax.ShapeDtypeStruct((1, LANES), F32)],
        scratch_shapes=[pltpu.VMEM((gw, SSD_STATE), F32)], compiler_params=_params(),
    )(xs, bm, cm, dt, a_log, d_skip, ck, dy)


def loss_and_grad(x, tgt, g, tm):
    t, d = x.shape
    tm = min(tm, t)
    nb = t // tm

    def body(x_ref, t_ref, g_ref, loss_ref, dx_ref, dg_ref):
        @pl.when(pl.program_id(0) == 0)
        def _():
            loss_ref[...] = jnp.zeros_like(loss_ref)
            dg_ref[...] = jnp.zeros_like(dg_ref)

        val, vjp = jax.vjp(loss_fn, x_ref[...], t_ref[...], g_ref[...])
        dx, _, dg = vjp(jnp.ones((1, 1), F32))
        loss_ref[...] += jnp.broadcast_to(val, loss_ref.shape)
        dx_ref[...] = dx
        dg_ref[...] += dg

    row = pl.BlockSpec((tm, d), lambda i: (i, 0))
    one = pl.BlockSpec((1, d), lambda i: (0, 0))
    return pl.pallas_call(
        body, name="loss_and_grad", grid=(nb,), in_specs=[row, row, one],
        out_specs=[pl.BlockSpec((SUBLANES, LANES), lambda i: (0, 0)), row, one],
        out_shape=[jax.ShapeDtypeStruct((SUBLANES, LANES), F32), jax.ShapeDtypeStruct((t, d), F32),
                   jax.ShapeDtypeStruct((1, d), F32)],
        compiler_params=_params(),
    )(x, tgt, g)


def adamw(name, recv, w, m, v):
    rows, cols = w.shape
    recv_block_bytes = 4 * 1024 * 1024
    tm = _pick(rows, [c for c in (256, 128, 64, 32, 16, 8) if N_DEV * c * cols * 4 <= recv_block_bytes])
    c1 = 1.0 / (1.0 - ADAM_B1 ** ADAM_STEP)
    c2 = 1.0 / (1.0 - ADAM_B2 ** ADAM_STEP)

    def body(recv_ref, w_ref, m_ref, v_ref, g_ref, d_ref, nm_ref, nv_ref):
        g = recv_ref[0].astype(F32)
        for p in range(1, N_DEV):
            g = g + recv_ref[p].astype(F32)
        nm = ADAM_B1 * m_ref[...] + (1.0 - ADAM_B1) * g
        nv = ADAM_B2 * v_ref[...] + (1.0 - ADAM_B2) * jnp.square(g)
        g_ref[...] = g
        nm_ref[...] = nm
        nv_ref[...] = nv
        d_ref[...] = -ADAM_LR * ((nm * c1) / (jnp.sqrt(nv * c2) + ADAM_EPS) + ADAM_WD * w_ref[...])

    blk = pl.BlockSpec((tm, cols), lambda i: (i, 0))
    return pl.pallas_call(
        body, name=name, grid=(rows // tm,),
        in_specs=[pl.BlockSpec((N_DEV, tm, cols), lambda i: (0, i, 0)), blk, blk, blk],
        out_specs=[blk] * 4, out_shape=[jax.ShapeDtypeStruct((rows, cols), F32)] * 4,
        compiler_params=_params(),
    )(recv, w, m, v)


def _mesh_pos():
    return lax.axis_index("x"), lax.axis_index("y"), lax.axis_index("c")


def _peer(pos, mask):
    x, y, c = pos
    return (1 - x if mask & 4 else x, 1 - y if mask & 2 else y, 1 - c if mask & 1 else c)


def _linear(pos):
    return 4 * pos[0] + 2 * pos[1] + pos[2]


class Exchange:
    def __init__(self, xs, scatter):
        self.xs, self.scatter, self.n = list(xs), scatter, len(xs)
        self.out_shape = [jax.ShapeDtypeStruct(x.shape if scatter else (N_DEV,) + x.shape, x.dtype) for x in xs]
        self.specs = [pl.BlockSpec(memory_space=pl.ANY)] * self.n
        peers = N_DEV - 1
        self.scratch = [pltpu.SemaphoreType.DMA((self.n * peers,)), pltpu.SemaphoreType.DMA((self.n * peers,)),
                        pltpu.SemaphoreType.DMA((self.n,))]

    def _copies(self, ins, outs, sems, landing):
        send_sems, recv_sems, local_sems = sems
        me = _mesh_pos()
        me_lin = _linear(me)
        local, remote = [], []
        for ti in range(self.n):
            src_mine = ins[ti].at[me_lin] if self.scatter else ins[ti]
            local.append(pltpu.make_async_copy(src_mine, outs[ti].at[me_lin], local_sems.at[ti]))
            for mask in range(1, N_DEV):
                peer = _peer(me, mask)
                src = ins[ti].at[_linear(peer)] if self.scatter else ins[ti]
                dst = outs[ti].at[_linear(peer) if landing else me_lin]
                remote.append(pltpu.make_async_remote_copy(
                    src_ref=src, dst_ref=dst, send_sem=send_sems.at[ti * (N_DEV - 1) + mask - 1],
                    recv_sem=recv_sems.at[ti * (N_DEV - 1) + mask - 1],
                    device_id=peer, device_id_type=pl.DeviceIdType.MESH))
        return local, remote

    def start(self, ins, outs, sems):
        local, remote = self._copies(ins, outs, sems, landing=False)
        for cp in local + remote:
            cp.start()

    def finish(self, ins, outs, sems):
        local, remote = self._copies(ins, outs, sems, landing=True)
        for cp in remote:
            cp.wait_recv()
        for cp in remote:
            cp.wait_send()
        for cp in local:
            cp.wait()


def exchange_start(name, xs, scatter, dep=None):
    ex = Exchange(xs, scatter)
    n = ex.n
    hbm = pl.BlockSpec(memory_space=pltpu.HBM)
    sem = pl.BlockSpec(memory_space=pltpu.SEMAPHORE)
    lands = [lax.empty(s.shape, s.dtype) for s in ex.out_shape]

    n_inputs = 2 * n + (0 if dep is None else 1)

    def body(*refs):
        ins, lnd, sems, token = refs[:n], refs[n:2 * n], refs[n_inputs:n_inputs + 3], refs[-1]
        ex.start(ins, lnd, sems)
        token[...] = jnp.zeros_like(token)

    res = pl.pallas_call(
        body, name=name, in_specs=[hbm] * (2 * n) + ([] if dep is None else [pl.BlockSpec(memory_space=pl.ANY)]),
        out_specs=[sem] * 3 + [hbm] * (2 * n) + [pl.BlockSpec(memory_space=pltpu.VMEM)],
        out_shape=ex.scratch + [pltpu.HBM(x.shape, x.dtype) for x in xs]
        + [pltpu.HBM(s.shape, s.dtype) for s in ex.out_shape] + [jax.ShapeDtypeStruct((SUBLANES, LANES), F32)],
        input_output_aliases={i: 3 + i for i in range(2 * n)},
        compiler_params=pltpu.CompilerParams(has_side_effects=pltpu.SideEffectType.DATAFLOW_SIDE_EFFECTING),
    )(*[pltpu.with_memory_space_constraint(x, pltpu.HBM) for x in list(xs) + lands], *([] if dep is None else [dep]))
    return (ex, res[:3], res[3:3 + n], res[3 + n:3 + 2 * n]), res[-1]


def exchange_wait(name, handles, after):
    ex, sems, srcs, lands = handles
    n = ex.n
    hbm = pl.BlockSpec(memory_space=pltpu.HBM)
    sem = pl.BlockSpec(memory_space=pltpu.SEMAPHORE)

    def body(*refs):
        ins, lnd, sem_refs = refs[:n], refs[n:2 * n], refs[2 * n:2 * n + 3]
        ex.finish(ins, lnd, sem_refs)

    res = pl.pallas_call(
        body, name=name, in_specs=[hbm] * (2 * n) + [sem] * 3 + [pl.BlockSpec(memory_space=pl.ANY)],
        out_specs=[hbm] * (2 * n),
        out_shape=[pltpu.HBM(x.shape, x.dtype) for x in srcs] + [pltpu.HBM(x.shape, x.dtype) for x in lands],
        input_output_aliases={i: i for i in range(2 * n)},
        compiler_params=pltpu.CompilerParams(has_side_effects=pltpu.SideEffectType.DATAFLOW_SIDE_EFFECTING),
    )(*srcs, *lands, *sems, after)
    return res[n:]


_Z = (0, 1024)
_XBC = (1024, 2560)
_DT = (2560, 2576)
_RKV = (2576, 5648)
_PW = (5648, 5744)
_PA = (5744, 5840)
_PG = (5840, 6096)
D_IN = 6096

_SMALL = ("norm_mix_g", "ssd_conv_b", "ssd_dt_bias", "ssd_a_log", "ssd_d", "ssd_norm_g", "rwkv_mu", "rwkv_w0",
          "rwkv_a0", "rwkv_k_k", "rwkv_k_a", "rwkv_r_k", "rwkv_ln_w", "rwkv_ln_b", "norm_x_g", "norm_mem_g",
          "norm_ffn_g", "final_norm_g")
_WEIGHTS = ("norm_mix_g", "w_in", "ssd_conv_w", "ssd_conv_b", "ssd_dt_bias", "ssd_a_log", "ssd_d", "ssd_norm_g",
            "rwkv_mu", "rwkv_w0", "rwkv_w2", "rwkv_a0", "rwkv_a2", "rwkv_g2", "rwkv_k_k", "rwkv_k_a", "rwkv_r_k",
            "rwkv_ln_w", "rwkv_ln_b", "w_out", "norm_x_g", "norm_mem_g", "xattn_wq", "xattn_wk", "xattn_wv",
            "xattn_wo", "norm_ffn_g", "ffn_w1", "ffn_w2", "final_norm_g")


def _pad_lanes(x, width=LANES):
    return jnp.pad(x, ((0, 0), (0, width - x.shape[1])))


def _pack_small(vals):
    flat = jnp.concatenate([vals[n].reshape(-1) for n in _SMALL])
    rows = -(-flat.shape[0] // (LANES * SUBLANES)) * SUBLANES
    return jnp.pad(flat, (0, rows * LANES - flat.shape[0])).reshape(rows, LANES)


def _unpack_small(packed, shapes):
    flat = packed.reshape(-1)
    out, pos = {}, 0
    for n in _SMALL:
        size = 1
        for s in shapes[n]:
            size *= s
        out[n] = flat[pos:pos + size].reshape(shapes[n])
        pos += size
    return out


def _cols(w, rng):
    return w[:, rng[0]:rng[1]]


def kernel(x, mem, norm_mix_g, w_in, ssd_conv_w, ssd_conv_b, ssd_dt_bias, ssd_a_log, ssd_d, ssd_norm_g, rwkv_mu, rwkv_w0, rwkv_w2, rwkv_a0, rwkv_a2, rwkv_g2, rwkv_k_k, rwkv_k_a, rwkv_r_k, rwkv_ln_w, rwkv_ln_b, w_out, norm_x_g, norm_mem_g, xattn_wq, xattn_wk, xattn_wv, xattn_wo, norm_ffn_g, ffn_w1, ffn_w2, final_norm_g, loss_target, m_norm_mix_g, m_w_in, m_ssd_conv_w, m_ssd_conv_b, m_ssd_dt_bias, m_ssd_a_log, m_ssd_d, m_ssd_norm_g, m_rwkv_mu, m_rwkv_w0, m_rwkv_w2, m_rwkv_a0, m_rwkv_a2, m_rwkv_g2, m_rwkv_k_k, m_rwkv_k_a, m_rwkv_r_k, m_rwkv_ln_w, m_rwkv_ln_b, m_w_out, m_norm_x_g, m_norm_mem_g, m_xattn_wq, m_xattn_wk, m_xattn_wv, m_xattn_wo, m_norm_ffn_g, m_ffn_w1, m_ffn_w2, m_final_norm_g, v_norm_mix_g, v_w_in, v_ssd_conv_w, v_ssd_conv_b, v_ssd_dt_bias, v_ssd_a_log, v_ssd_d, v_ssd_norm_g, v_rwkv_mu, v_rwkv_w0, v_rwkv_w2, v_rwkv_a0, v_rwkv_a2, v_rwkv_g2, v_rwkv_k_k, v_rwkv_k_a, v_rwkv_r_k, v_rwkv_ln_w, v_rwkv_ln_b, v_w_out, v_norm_x_g, v_norm_mem_g, v_xattn_wq, v_xattn_wk, v_xattn_wv, v_xattn_wo, v_norm_ffn_g, v_ffn_w1, v_ffn_w2, v_final_norm_g):
    given = dict(locals())
    wts = {n: given[n] for n in _WEIGHTS}
    mom_m = {n: given["m_" + n] for n in _WEIGHTS}
    mom_v = {n: given["v_" + n] for n in _WEIGHTS}
    d = D_MODEL
    xt, memt, tgt = x[0], mem[0], loss_target[0]
    tm = 256

    big = {"w_in": w_in[0], "w_out": w_out[0], "xattn_wq": xattn_wq[0], "xattn_wk": xattn_wk[0],
           "xattn_wv": xattn_wv[0], "xattn_wo": xattn_wo[0], "ffn_w1": ffn_w1[0], "ffn_w2": ffn_w2[0]}
    small_sh = {"ssd_conv_w": ssd_conv_w.reshape(4, -1), "rwkv_w2": rwkv_w2[0], "rwkv_a2": rwkv_a2[0],
                "rwkv_g2": rwkv_g2[0]}
    cast_one = lambda n, deps=(): rowwise_fwd("cast_" + n, cast_fn, [big[n]], [], [(big[n].shape[1], BF16)], 256,
                                              deps=deps)[0]
    gather_in, token_in = exchange_start("gather_in_start", [cast_one("w_in")] + list(small_sh.values()), scatter=False)
    cast = {n: cast_one(n, deps=[token_in]) for n in big if n != "w_in"}
    late_a = ("w_out", "xattn_wq", "xattn_wk", "xattn_wv", "xattn_wo")
    late_b = ("ffn_w1", "ffn_w2")
    gather_a, token_a = exchange_start("gather_attn_start", [cast[n] for n in late_a], scatter=False, dep=token_in)
    gather_b, token_b = exchange_start("gather_ffn_start", [cast[n] for n in late_b], scatter=False, dep=token_a)
    (h1,) = rowwise_fwd("norm_mix", rmsnorm_fn, [xt], [norm_mix_g], [(d, BF16)], tm, deps=[token_b])
    gathered = exchange_wait("gather_in_wait", gather_in, after=h1)
    g_big = {"w_in": gathered[0]}
    g_small = dict(zip(small_sh, gathered[1:]))

    w_in_full = jnp.transpose(g_big["w_in"], (1, 0, 2)).reshape(d, D_IN)
    w_z, w_xbc, w_rkv, w_pg = (_cols(w_in_full, r) for r in (_Z, _XBC, _RKV, _PG))
    w_sm = jnp.concatenate([_pad_lanes(_cols(w_in_full, r)) for r in (_PW, _PA, _DT)], axis=1)
    unshard_cols = lambda g: jnp.transpose(g, (1, 0, 2)).reshape(g.shape[1], -1)
    conv_w_f = unshard_cols(g_small["ssd_conv_w"])
    pad_rows = lambda a: jnp.pad(a, ((0, LANES - a.shape[0]), (0, 0)))
    w2p, a2p = pad_rows(unshard_cols(g_small["rwkv_w2"])), pad_rows(unshard_cols(g_small["rwkv_a2"]))
    g2_f = unshard_cols(g_small["rwkv_g2"])

    mu = rwkv_mu
    mu_rkv, mu_pg = mu[:, :3072], mu[:, 3264:3520]
    mu_pwa = jnp.concatenate([_pad_lanes(mu[:, 3072:3168]), _pad_lanes(mu[:, 3168:3264])], axis=1)
    dt_bias_p, a_log_p, d_p = _pad_lanes(ssd_dt_bias), _pad_lanes(ssd_a_log), _pad_lanes(ssd_d)
    r_k_row = rwkv_r_k.reshape(1, RWKV_WIDTH)
    g_final = final_norm_g.reshape(1, d)

    u_z = mm("in_z", h1, w_z, "nn")
    u_xbc = mm("in_xbc", h1, w_xbc, "nn")
    u_rkv = mm("in_rkv", h1, w_rkv, "nn")
    u_pg = mm("in_pg", h1, w_pg, "nn")
    u_sm = mm("in_small", h1, w_sm, "nn")

    ssd_pre_rows = lambda: [Rows(u_xbc, shifts=(1, 2, 3)), Rows(u_sm, LANES, 2)]
    ssd_pre_params = [conv_w_f, ssd_conv_b, dt_bias_p]
    xs, bm, cm, dt = rowwise_fwd("ssd_pre", ssd_pre_fn, ssd_pre_rows(), ssd_pre_params,
                                 [(SSD_WIDTH, F32), (256, F32), (256, F32), (LANES, F32)], tm)
    y_scan, ssd_ck = ssd_scan_fwd(xs, bm, cm, dt, a_log_p, d_p)
    (y_ssd,) = rowwise_fwd("ssd_post", ssd_post_fn, [y_scan, u_z], [ssd_norm_g], [(SSD_WIDTH, BF16)], tm)

    rwkv_pre_rows = lambda: [Rows(u_rkv, shifts=(1,)), Rows(u_pg, shifts=(1,)), Rows(u_sm, 2 * LANES, 0, shifts=(1,))]
    rwkv_pre_params = [mu_rkv, mu_pg, mu_pwa, rwkv_w0, w2p, rwkv_a0, a2p, g2_f, rwkv_k_k, rwkv_k_a]
    r_, lw_, k_, v_, kap_, b_, gate_ = rowwise_fwd("rwkv_pre", rwkv_pre_fn, rwkv_pre_rows(), rwkv_pre_params,
                                                   [(RWKV_WIDTH, F32)] * 7, 128)
    ys_r, rwkv_ck = rwkv_scan_fwd(r_, lw_, k_, v_, kap_, b_)
    g_big.update(zip(late_a, exchange_wait("gather_attn_wait", gather_a, after=ys_r)))
    w_out_f = g_big["w_out"].reshape(d, d)
    wq_f, wk_f, wv_f, wo_f = (g_big[n].reshape(d, d) for n in ("xattn_wq", "xattn_wk", "xattn_wv", "xattn_wo"))
    rwkv_post_params = [rwkv_ln_w, rwkv_ln_b, r_k_row]
    (y_rwkv,) = rowwise_fwd("rwkv_post", rwkv_post_fn, [ys_r, r_, k_, v_, gate_], rwkv_post_params,
                            [(RWKV_WIDTH, BF16)], tm)
    ycat = jnp.concatenate([y_ssd, y_rwkv], axis=1)
    x1 = mm("out_proj", ycat, w_out_f, "nn", res=xt)

    (h2,) = rowwise_fwd("norm_x", rmsnorm_fn, [x1], [norm_x_g], [(d, BF16)], tm)
    (mn,) = rowwise_fwd("norm_mem", rmsnorm_fn, [memt], [norm_mem_g], [(d, BF16)], tm)
    q = mm("xattn_q", h2, wq_f, "nn", out_dtype=BF16)
    kx = mm("xattn_k", mn, wk_f, "nn")
    vx = mm("xattn_v", mn, wv_f, "nn")
    (o,) = rowwise_fwd("xattn", attn_fn, [q], [kx, vx], [(d, BF16)], tm)
    x2 = mm("xattn_o", o, wo_f, "nn", res=x1)

    (h3,) = rowwise_fwd("norm_ffn", rmsnorm_fn, [x2], [norm_ffn_g], [(d, BF16)], tm)
    w1_s, w2_g = exchange_wait("gather_ffn_wait", gather_b, after=h3)
    w2_f = w2_g.reshape(D_FF, d)
    relu2_epi = lambda acc: (jnp.square(jnp.maximum(acc, 0.0)), jnp.maximum(acc, 0.0))
    hid, relu_a = mm("ffn_1", h3, w1_s, "nn", b_slabs=N_DEV, epi=relu2_epi, out_dtypes=[BF16, BF16])
    x3 = mm("ffn_2", hid, w2_f, "nn", res=x2)

    loss_blk, dx3, dg_final = loss_and_grad(x3, tgt, g_final, tm)

    grads = {}
    grads["ffn_w2"] = mm("d_ffn_w2", hid, dx3, "tn", out_dtype=BF16).reshape(N_DEV, D_FF // N_DEV, d)
    sc_w2, tok = exchange_start("scatter_ffn_w2_start", [grads["ffn_w2"]], scatter=True)
    da = mm("d_hid", dx3, w2_f, "nt", dep=tok, epi=lambda acc, ra: (2.0 * acc * ra,), extras=[relu_a],
            out_dtypes=[BF16])
    grads["ffn_w1"] = mm("d_ffn_w1", h3, da, "tn", out_dtype=BF16, out_slabs=N_DEV)
    sc_w1, tok = exchange_start("scatter_ffn_w1_start", [grads["ffn_w1"]], scatter=True)
    dh3 = mm("d_h3", da, w1_s, "nt", b_slabs=N_DEV, dep=tok)
    (dx2,), (dg_ffn,) = rowwise_bwd("norm_ffn_bwd", rmsnorm_fn, [x2], [norm_ffn_g], [[dh3]], tm, [F32], row_add=[dx3])

    grads["xattn_wo"] = mm("d_wo", o, dx2, "tn", out_dtype=BF16).reshape(N_DEV, d // N_DEV, d)
    sc_wo, tok = exchange_start("scatter_wo_start", [grads["xattn_wo"]], scatter=True)
    d_o = mm("d_o", dx2, wo_f, "nt", dep=tok)
    (dq,), (dkx, dvx) = rowwise_bwd("xattn_bwd", attn_fn, [q], [kx, vx], [[d_o]], tm, [BF16])
    grads["xattn_wq"] = mm("d_wq", h2, dq, "tn", out_dtype=BF16).reshape(N_DEV, d // N_DEV, d)
    dh2 = mm("d_h2", dq, wq_f, "nt")
    grads["xattn_wk"] = mm("d_wk", mn, dkx, "tn", out_dtype=BF16).reshape(N_DEV, d // N_DEV, d)
    grads["xattn_wv"] = mm("d_wv", mn, dvx, "tn", out_dtype=BF16).reshape(N_DEV, d // N_DEV, d)
    qkv = ("xattn_wq", "xattn_wk", "xattn_wv")
    sc_qkv, tok = exchange_start("scatter_qkv_start", [grads[n] for n in qkv], scatter=True)
    dmn = mm("d_mn_v", dvx, wv_f, "nt", res=mm("d_mn_k", dkx, wk_f, "nt", dep=tok))
    _, (dg_mem,) = rowwise_bwd("norm_mem_bwd", rmsnorm_fn, [memt], [norm_mem_g], [[dmn]], tm, [None])
    (dx1,), (dg_x,) = rowwise_bwd("norm_x_bwd", rmsnorm_fn, [x1], [norm_x_g], [[dh2]], tm, [F32], row_add=[dx2])

    grads["w_out"] = mm("d_w_out", ycat, dx1, "tn", out_dtype=BF16).reshape(N_DEV, d // N_DEV, d)
    sc_wout, tok = exchange_start("scatter_w_out_start", [grads["w_out"]], scatter=True)
    d_ycat = mm("d_ycat", dx1, w_out_f, "nt", dep=tok)

    (d_ys, d_r1, d_k1, d_v1, d_gate), (dln_w, dln_b, dr_k) = rowwise_bwd(
        "rwkv_post_bwd", rwkv_post_fn, [ys_r, r_, k_, v_, gate_], rwkv_post_params,
        [[Rows(d_ycat, RWKV_WIDTH, 1)]], tm, [F32] * 5)
    d_r2, d_lw, d_k2, d_v2, d_kap, d_b = rwkv_scan_bwd(r_, lw_, k_, v_, kap_, b_, rwkv_ck, d_ys)
    (du_rkv, du_pg, du_pwa), rwkv_pg = rowwise_bwd(
        "rwkv_pre_bwd", rwkv_pre_fn, rwkv_pre_rows(), rwkv_pre_params,
        [[d_r1, d_r2], [d_lw], [d_k1, d_k2], [d_v1, d_v2], [d_kap], [d_b], [d_gate]], 128, [BF16] * 3)
    dmu_rkv, dmu_pg, dmu_pwa, dw0, dw2p, da0, da2p, dg2, dk_k, dk_a = rwkv_pg

    (d_yscan, du_z), (dssd_norm_g,) = rowwise_bwd("ssd_post_bwd", ssd_post_fn, [y_scan, u_z], [ssd_norm_g],
                                                  [[Rows(d_ycat, SSD_WIDTH, 0)]], tm, [F32, BF16])
    dxs, dbm, dcm, ddt2, da_log_p, dd_p = ssd_scan_bwd(xs, bm, cm, dt, a_log_p, d_p, ssd_ck, d_yscan)
    (du_xbc, du_dt), (dconv_w, dconv_b, ddt_bias_p) = rowwise_bwd(
        "ssd_pre_bwd", ssd_pre_fn, ssd_pre_rows(), ssd_pre_params,
        [[dxs], [dbm], [dcm], [ddt2[0], ddt2[1]]], tm, [BF16, BF16])
    du_sm = jnp.concatenate([du_pwa, du_dt], axis=1)

    dw_z = mm("d_w_z", h1, du_z, "tn", out_dtype=BF16)
    dw_xbc = mm("d_w_xbc", h1, du_xbc, "tn", out_dtype=BF16)
    dw_rkv = mm("d_w_rkv", h1, du_rkv, "tn", out_dtype=BF16)
    dw_pg = mm("d_w_pg", h1, du_pg, "tn", out_dtype=BF16)
    dw_sm = mm("d_w_small", h1, du_sm, "tn", out_dtype=BF16)
    dw_in_full = jnp.concatenate([dw_z, dw_xbc, dw_sm[:, 256:272], dw_rkv, dw_sm[:, 0:96], dw_sm[:, 128:224], dw_pg], axis=1)
    to_slabs = lambda g: jnp.transpose(g.reshape(g.shape[0], N_DEV, -1), (1, 0, 2))
    grads["w_in"] = to_slabs(dw_in_full)
    grads["ssd_conv_w"] = to_slabs(dconv_w)
    grads["rwkv_w2"] = to_slabs(dw2p[:96])
    grads["rwkv_a2"] = to_slabs(da2p[:96])
    grads["rwkv_g2"] = to_slabs(dg2)
    tail = ("w_in", "ssd_conv_w", "rwkv_w2", "rwkv_a2", "rwkv_g2")
    sc_tail, tok = exchange_start("scatter_tail_start", [grads[n] for n in tail], scatter=True)
    dh1 = mm("d_h1_z", du_z, w_z, "nt", dep=tok)
    dh1 = mm("d_h1_xbc", du_xbc, w_xbc, "nt", res=dh1)
    dh1 = mm("d_h1_rkv", du_rkv, w_rkv, "nt", res=dh1)
    dh1 = mm("d_h1_pg", du_pg, w_pg, "nt", res=dh1)
    dh1 = mm("d_h1_small", du_sm, w_sm, "nt", res=dh1)
    (dx,), (dg_mix,) = rowwise_bwd("norm_mix_bwd", rmsnorm_fn, [xt], [norm_mix_g], [[dh1]], tm, [F32], row_add=[dx1])

    dmu =jnp.concatenate([dmu_rkv, dmu_pwa[:, 0:96], dmu_pwa[:, 128:224], dmu_pg], axis=1)
    small_grads = {
        "norm_mix_g": dg_mix, "ssd_conv_b": dconv_b, "ssd_dt_bias": ddt_bias_p[:, :16], "ssd_a_log": da_log_p[:, :16],
        "ssd_d": dd_p[:, :16], "ssd_norm_g": dssd_norm_g, "rwkv_mu": dmu, "rwkv_w0": dw0, "rwkv_a0": da0,
        "rwkv_k_k": dk_k, "rwkv_k_a": dk_a, "rwkv_r_k": dr_k, "rwkv_ln_w": dln_w, "rwkv_ln_b": dln_b,
        "norm_x_g": dg_x, "norm_mem_g": dg_mem, "norm_ffn_g": dg_ffn, "final_norm_g": dg_final}

    gather_small, tok = exchange_start("gather_small_start", [_pack_small(small_grads)], scatter=False)
    received = {}
    for names, handle in ((("ffn_w2",), sc_w2), (("ffn_w1",), sc_w1), (("xattn_wo",), sc_wo), (qkv, sc_qkv),
                          (("w_out",), sc_wout)):
        received.update(zip(names, exchange_wait("scatter_" + names[0] + "_wait", handle, after=tok)))

    out_g, out_d, out_m, out_v = {}, {}, {}, {}

    def run_adamw(n):
        shape = wts[n].shape
        two_d = lambda a: a.reshape(-1, shape[-1])
        res = adamw("adamw_" + n, received[n].reshape(N_DEV, -1, shape[-1]), two_d(wts[n]), two_d(mom_m[n]), two_d(mom_v[n]))
        out_g[n], out_d[n], out_m[n], out_v[n] = (r.reshape(shape) for r in res)

    for n in ("ffn_w2", "ffn_w1", "xattn_wo") + qkv + ("w_out",):
        run_adamw(n)
    received.update(zip(tail, exchange_wait("scatter_tail_wait", sc_tail, after=out_g["w_out"])))
    for n in tail:
        run_adamw(n)
    (small_all,) = exchange_wait("gather_small_wait", gather_small, after=out_g["w_in"])
    res = adamw("adamw_small", small_all, _pack_small(wts), _pack_small(mom_m), _pack_small(mom_v))
    shapes = {n: wts[n].shape for n in _SMALL}
    for dst, packed in zip((out_g, out_d, out_m, out_v), res):
        dst.update(_unpack_small(packed, shapes))

    loss = lax.psum(loss_blk[0, 0], ("x", "y", "c"))
    return (loss, dx[None], *[out_g[n] for n in _WEIGHTS], *[out_d[n] for n in _WEIGHTS],
            *[out_m[n] for n in _WEIGHTS], *[out_v[n] for n in _WEIGHTS])
```

```python
import functools

import jax
import jax.numpy as jnp
from jax import lax
from jax.experimental import pallas as pl
from jax.experimental.pallas import tpu as pltpu

F32 = jnp.float32
BF16 = jnp.bfloat16
HIGHEST = lax.Precision.HIGHEST

N_DEV = 8
D_MODEL = 2048
NORM_EPS = 1e-6
SSD_WIDTH = 1024
SSD_CONV_DIM = 1536
SSD_HEADS = 16
SSD_HEAD_DIM = 64
SSD_STATE = 128
SSD_CHUNK = 128
SSD_HEADS_PER_GROUP = 8
RWKV_WIDTH = 1024
RWKV_HEADS = 16
RWKV_HEAD_DIM = 64
RWKV_LN_EPS = 64e-5
RWKV_CHUNK = 64
RWKV_HEADS_PER_STEP = 16
XATTN_HEADS = 4
XATTN_HEAD_DIM = 512
D_FF = 8192
LANES = 128
SUBLANES = 8
VMEM_LIMIT = 56 * 1024 * 1024

ADAM_LR = 0.001
ADAM_B1 = 0.9
ADAM_B2 = 0.999
ADAM_EPS = 1e-08
ADAM_WD = 0.01
ADAM_STEP = 10

_DN = {"nn": ((1,), (0,)), "nt": ((1,), (1,)), "tn": ((0,), (0,))}


def _dg(a, b, mode, precision=None):
    (ca,), (cb,) = _DN[mode]
    dn = (((ca + 1,), (cb + 1,)), ((0,), (0,))) if a.ndim == 3 else (((ca,), (cb,)), ((), ()))
    return lax.dot_general(a, b, dn, precision=precision, preferred_element_type=F32)


@functools.partial(jax.custom_vjp, nondiff_argnums=(2,))
def bdot(a, b, mode):
    return _dg(a.astype(BF16), b.astype(BF16), mode)


def _bdot_fwd(a, b, mode):
    return bdot(a, b, mode), (a, b)


def _bdot_bwd(mode, res, g):
    a, b = res
    ab, bb, gb = a.astype(BF16), b.astype(BF16), g.astype(BF16)
    if mode == "nn":
        da, db = _dg(gb, bb, "nt"), _dg(ab, gb, "tn")
    elif mode == "nt":
        da, db = _dg(gb, bb, "nn"), _dg(gb, ab, "tn")
    else:
        da, db = _dg(bb, gb, "nt"), _dg(ab, gb, "nn")
    return da.astype(a.dtype), db.astype(b.dtype)


bdot.defvjp(_bdot_fwd, _bdot_bwd)


def fdot(a, b, mode):
    return _dg(a, b, mode, precision=HIGHEST)


def _split3(x):
    hi = x.astype(BF16)
    r1 = x - hi.astype(F32)
    mid = r1.astype(BF16)
    lo = (r1 - mid.astype(F32)).astype(BF16)
    return hi, mid, lo


def _dot01(x, m01):
    hi, mid, _ = _split3(x)
    return _dg(hi, m01, "nn") + _dg(mid, m01, "nn")


def _exact_dot_impl(a, b, mode, exact):
    if exact == "a":
        ae = a.astype(BF16)
        return sum(_dg(ae, part, mode) for part in _split3(b))
    be = b.astype(BF16)
    return sum(_dg(part, be, mode) for part in _split3(a))


@functools.partial(jax.custom_vjp, nondiff_argnums=(2, 3))
def exact_dot(a, b, mode, exact):
    return _exact_dot_impl(a, b, mode, exact)


def _exact_dot_fwd(a, b, mode, exact):
    return _exact_dot_impl(a, b, mode, exact), (a, b)


def _exact_dot_bwd(mode, exact, res, g):
    a, b = res
    if exact == "a":
        db = {"nn": lambda: _exact_dot_impl(a, g, "tn", "a"), "nt": lambda: _exact_dot_impl(g, a, "tn", "b"),
              "tn": lambda: _exact_dot_impl(a, g, "nn", "a")}[mode]()
        return jnp.zeros_like(a), db
    da = {"nn": lambda: _exact_dot_impl(g, b, "nt", "b"), "nt": lambda: _exact_dot_impl(g, b, "nn", "b"),
          "tn": lambda: _exact_dot_impl(b, g, "nt", "a")}[mode]()
    return da, jnp.zeros_like(b)


exact_dot.defvjp(_exact_dot_fwd, _exact_dot_bwd)


def _head_indicator(width, heads, transpose):
    hd = width // heads
    shape = (LANES, width) if transpose else (width, LANES)
    lane = lax.broadcasted_iota(jnp.int32, shape, 1 if not transpose else 0)
    pos = lax.broadcasted_iota(jnp.int32, shape, 0 if not transpose else 1)
    return ((pos >= lane * hd) & (pos < lane * hd + hd)).astype(BF16)


@jax.custom_vjp
def head_sum(x):
    w = x.shape[-1]
    e = _head_indicator(w, w // RWKV_HEAD_DIM, False)
    et = _head_indicator(w, w // RWKV_HEAD_DIM, True)
    return _dot01(_dot01(x, e), et)


head_sum.defvjp(lambda x: (head_sum(x), None), lambda _, g: (head_sum(g),))


def rmsnorm_fn(x, g):
    y = x * lax.rsqrt(jnp.mean(x * x, axis=-1, keepdims=True) + NORM_EPS)
    return ((y * g).astype(BF16),)


def cast_fn(x):
    return (x.astype(BF16),)


def ssd_pre_fn(xbc, xbc1, xbc2, xbc3, dt_raw, conv_w, conv_b, dt_bias):
    c = conv_w[3:4] * xbc + conv_w[2:3] * xbc1 + conv_w[1:2] * xbc2 + conv_w[0:1] * xbc3 + conv_b
    act = c * jax.nn.sigmoid(c)
    dt = jax.nn.softplus(dt_raw + dt_bias)
    return act[:, :SSD_WIDTH], act[:, SSD_WIDTH:SSD_WIDTH + 256], act[:, SSD_WIDTH + 256:], dt


def ssd_post_fn(yscan, z, norm_g):
    y = yscan * (z * jax.nn.sigmoid(z))
    half = SSD_WIDTH // 2
    parts = []
    for g in range(2):
        yg = y[:, g * half:(g + 1) * half]
        parts.append(yg * lax.rsqrt(jnp.mean(yg * yg, axis=-1, keepdims=True) + NORM_EPS))
    return ((jnp.concatenate(parts, axis=-1) * norm_g).astype(BF16),)


def rwkv_pre_fn(rkv, rkv_p, pg, pg_p, pwa, pwa_p, mu_rkv, mu_pg, mu_pwa, w0, w2p, a0, a2p, g2, k_k, k_a):
    w = RWKV_WIDTH
    rkv = rkv + (rkv_p - rkv) * mu_rkv
    pg = pg + (pg_p - pg) * mu_pg
    pwa = pwa + (pwa_p - pwa) * mu_pwa
    r, k, v = rkv[:, :w], rkv[:, w:2 * w], rkv[:, 2 * w:]
    pw, pa = pwa[:, :LANES], pwa[:, LANES:]
    w_log = -jax.nn.softplus(-(w0 + bdot(jnp.tanh(pw), w2p, "nn"))) - 0.5
    lw = -jnp.exp(w_log)
    iclr = jax.nn.sigmoid(a0 + bdot(pa, a2p, "nn"))
    gate = bdot(jax.nn.sigmoid(pg), g2, "nn")
    kk = k * k_k
    kap = kk / jnp.maximum(jnp.sqrt(head_sum(kk * kk)), 1e-12)
    k_mod = k * (1.0 + (iclr - 1.0) * k_a)
    return r, lw, k_mod, v, kap, kap * iclr, gate


def rwkv_post_fn(ys, r, k_mod, v, gate, ln_w, ln_b, r_k):
    inv_n = 1.0 / RWKV_HEAD_DIM
    mean = head_sum(ys) * inv_n
    yc = ys - mean
    var = head_sum(yc * yc) * inv_n
    yn = yc * lax.rsqrt(var + RWKV_LN_EPS) * ln_w + ln_b
    bonus = head_sum(r * k_mod * r_k) * v
    return (((yn + bonus) * gate).astype(BF16),)


def attn_fn(q, kx, vx):
    outs = []
    for h in range(XATTN_HEADS):
        sl = slice(h * XATTN_HEAD_DIM, (h + 1) * XATTN_HEAD_DIM)
        s = bdot(q[:, sl], kx[:, sl], "nt") * (XATTN_HEAD_DIM ** -0.5)
        s = s - jnp.max(s, axis=-1, keepdims=True)
        p = jnp.exp(s)
        p = p / jnp.sum(p, axis=-1, keepdims=True)
        outs.append(bdot(p, vx[:, sl], "nn"))
    return (jnp.concatenate(outs, axis=-1).astype(BF16),)


def loss_fn(x, tgt, g):
    y = x * lax.rsqrt(jnp.mean(x * x, axis=-1, keepdims=True) + NORM_EPS) * g
    err = jnp.square(y - tgt)
    return 0.5 * jnp.sum(jnp.mean(err, axis=-1, keepdims=True), axis=0, keepdims=True)


def _tri_masks(n):
    row = lax.broadcasted_iota(jnp.int32, (n, n), 0)
    col = lax.broadcasted_iota(jnp.int32, (n, n), 1)
    return col <= row, col < row, row == col


def rwkv_chunk_fn(st0, r, lw, k, v, kap, b):
    h, c = r.shape[0], r.shape[1]
    incl, strict, diag = _tri_masks(c)
    cum = exact_dot(jnp.broadcast_to(incl.astype(F32), (h, c, c)), lw, "nn", "a")
    g_in = jnp.exp(cum)
    g_prev = jnp.exp(cum - lw)
    g_inv = jnp.exp(-cum)
    g_end = jnp.exp(cum[:, c - 1:c, :] - cum)
    kap_t, k_t, b_t, r_t = kap * g_prev, k * g_inv, b * g_inv, r * g_in
    a_ub = jnp.where(strict, bdot(kap_t, b_t, "nt"), 0.0)
    a_vk = jnp.where(strict, bdot(kap_t, k_t, "nt"), 0.0)
    rhs = -(bdot(kap_t, st0, "nn") + bdot(a_vk, v, "nn"))
    eye = diag.astype(F32)
    m = -a_ub
    inv = eye + m
    n = 1
    while n * 2 < c:
        m = bdot(m, m, "nn")
        inv = bdot(inv, eye + m, "nn")
        n *= 2
    u = bdot(inv, rhs, "nn")
    y = (bdot(r_t, st0, "nn")
         + bdot(jnp.where(incl, bdot(r_t, k_t, "nt"), 0.0), v, "nn")
         + bdot(jnp.where(incl, bdot(r_t, b_t, "nt"), 0.0), u, "nn"))
    g_full = jnp.exp(exact_dot(lw, jnp.ones((h, c, st0.shape[2]), F32), "tn", "b"))
    st1 = g_full * st0 + bdot(k * g_end, v, "tn") + bdot(b * g_end, u, "tn")
    return y, st1


def ssd_chunk_fn(group, h0, xs, bm, cm, dt, a_log, d_skip):
    q = xs.shape[0]
    causal, _, _ = _tri_masks(q)
    a_row = -jnp.exp(a_log)
    cs_all = exact_dot(causal.astype(F32), dt * a_row, "nn", "a")
    cs_t = cs_all.T
    cb = bdot(cm, bm, "nt")
    ys, hs = [], []
    p = SSD_HEAD_DIM
    for e in range(SSD_HEADS_PER_GROUP):
        hl = group * SSD_HEADS_PER_GROUP + e
        cs = cs_all[:, hl:hl + 1]
        seg = cs - cs_t[hl:hl + 1, :]
        lmat = jnp.where(causal, jnp.exp(jnp.where(causal, seg, 0.0)), 0.0)
        x_h = xs[:, e * p:(e + 1) * p]
        xdt = x_h * dt[:, hl:hl + 1]
        h0_e = h0[e * p:(e + 1) * p, :]
        cl = cs[q - 1:q, :]
        y = bdot(cb * lmat, xdt, "nn") + bdot(cm, h0_e, "nt") * jnp.exp(cs) + x_h * d_skip[:, hl:hl + 1]
        hs.append(h0_e * jnp.exp(cl) + bdot(xdt * jnp.exp(cl - cs), bm, "tn"))
        ys.append(y)
    return jnp.concatenate(ys, axis=-1), jnp.concatenate(hs, axis=0)


class Rows:
    def __init__(self, arr, w=None, cb=0, shifts=()):
        self.arr, self.w, self.cb, self.shifts = arr, (arr.shape[1] if w is None else w), cb, tuple(shifts)


def _as_rows(x):
    return x if isinstance(x, Rows) else Rows(x)


def _shift_down(x, halo, k):
    rolled = pltpu.roll(x, k, 0)
    first = rolled[0:SUBLANES]
    rid = lax.broadcasted_iota(jnp.int32, first.shape, 0)
    patched = jnp.where(rid < k, pltpu.roll(halo, k, 0), first)
    return jnp.concatenate([patched, rolled[SUBLANES:]], axis=0)


def _shift_up(g, carry, k):
    tm = g.shape[0]
    rolled = pltpu.roll(g, tm - k, 0)
    last = rolled[tm - SUBLANES:]
    rid = lax.broadcasted_iota(jnp.int32, last.shape, 0)
    patched = jnp.where(rid >= SUBLANES - k, pltpu.roll(carry, SUBLANES - k, 0), last)
    return jnp.concatenate([rolled[:tm - SUBLANES], patched], axis=0)


def _params():
    return pltpu.CompilerParams(vmem_limit_bytes=VMEM_LIMIT)


def _load_rows(refs, pos, rins, first_block):
    vals = []
    for r in rins:
        x = refs[pos][...].astype(F32) if refs[pos].dtype != F32 else refs[pos][...]
        pos += 1
        vals.append(x)
        if r.shifts:
            halo = refs[pos][...]
            pos += 1
            halo = jnp.where(first_block, jnp.zeros_like(halo), halo)
            for k in r.shifts:
                vals.append(_shift_down(x, halo, k))
    return vals, pos


def _row_specs(rins, tm, blk):
    specs, args = [], []
    for r in rins:
        specs.append(pl.BlockSpec((tm, r.w), lambda i, cb=r.cb: (blk(i), cb)))
        args.append(r.arr)
        if r.shifts:
            per = tm // SUBLANES
            specs.append(pl.BlockSpec((SUBLANES, r.w), lambda i, cb=r.cb: (jnp.maximum(blk(i) * per - 1, 0), cb)))
            args.append(r.arr)
    return specs, args


def rowwise_fwd(name, fn, rins, params, outs, tm, deps=()):
    rins = [_as_rows(r) for r in rins]
    t = rins[0].arr.shape[0]
    tm = min(tm, t)
    nb = t // tm
    specs, args = _row_specs(rins, tm, lambda i: i)
    for p in params:
        specs.append(pl.BlockSpec(p.shape, lambda i: (0, 0)))
        args.append(p)
    for dep in deps:
        specs.append(pl.BlockSpec(memory_space=pl.ANY))
        args.append(dep)
    n_in = len(args)

    def body(*refs):
        vals, pos = _load_rows(refs, 0, rins, pl.program_id(0) == 0)
        pv = [refs[pos + j][...] for j in range(len(params))]
        res = fn(*vals, *pv)
        for o_ref, o in zip(refs[n_in:], res):
            o_ref[...] = o.astype(o_ref.dtype)

    return pl.pallas_call(
        body, name=name, grid=(nb,), in_specs=specs,
        out_specs=[pl.BlockSpec((tm, w), lambda i: (i, 0)) for w, _ in outs],
        out_shape=[jax.ShapeDtypeStruct((t, w), dt) for w, dt in outs],
        compiler_params=_params(),
    )(*args)


def rowwise_bwd(name, fn, rins, params, cts, tm, grad_dtypes, row_add=None):
    rins = [_as_rows(r) for r in rins]
    cts = [[_as_rows(c) for c in lst] for lst in cts]
    row_add = [_as_rows(a) for a in (row_add or [])]
    t = rins[0].arr.shape[0]
    tm = min(tm, t)
    nb = t // tm
    rev = lambda i: nb - 1 - i
    specs, args = _row_specs(rins, tm, rev)
    for p in params:
        specs.append(pl.BlockSpec(p.shape, lambda i: (0, 0)))
        args.append(p)
    flat_cts = [c for lst in cts for c in lst] + row_add
    for c in flat_cts:
        specs.append(pl.BlockSpec((tm, c.w), lambda i, cb=c.cb: (rev(i), cb)))
        args.append(c.arr)
    n_in = len(args)
    want = [i for i, d in enumerate(grad_dtypes) if d is not None]
    out_specs = [pl.BlockSpec((tm, rins[i].w), lambda i_: (rev(i_), 0)) for i in want]
    out_shape = [jax.ShapeDtypeStruct((t, rins[i].w), grad_dtypes[i]) for i in want]
    out_specs += [pl.BlockSpec(p.shape, lambda i: (0, 0)) for p in params]
    out_shape += [jax.ShapeDtypeStruct(p.shape, F32) for p in params]
    n_out = len(out_shape)
    scratch = [pltpu.VMEM((SUBLANES, r.w), F32) for r in rins for _ in r.shifts]

    def body(*refs):
        i = pl.program_id(0)
        vals, pos = _load_rows(refs, 0, rins, rev(i) == 0)
        pv = [refs[pos + j][...] for j in range(len(params))]
        pos += len(params)
        outs, vjp = jax.vjp(fn, *vals, *pv)
        ct_vals = []
        for o, lst in zip(outs, cts):
            acc = None
            for _ in lst:
                cv = refs[pos][...].astype(F32)
                pos += 1
                acc = cv if acc is None else acc + cv
            ct_vals.append(acc.astype(o.dtype))
        adds = [refs[pos + j][...].astype(F32) for j in range(len(row_add))]
        grads = vjp(tuple(ct_vals))
        out_refs = refs[n_in:n_in + n_out]
        carry_refs = refs[n_in + n_out:]

        @pl.when(i == 0)
        def _():
            for cr in carry_refs:
                cr[...] = jnp.zeros_like(cr)
            for pr in out_refs[len(want):]:
                pr[...] = jnp.zeros_like(pr)

        gi, ci, oi = 0, 0, 0
        for idx, r in enumerate(rins):
            d = grads[gi]
            gi += 1
            for k in r.shifts:
                dk = grads[gi]
                gi += 1
                d = d + _shift_up(dk, carry_refs[ci][...], k)
                carry_refs[ci][...] = dk[0:SUBLANES]
                ci += 1
            if idx == 0:
                for a in adds:
                    d = d + a
            if grad_dtypes[idx] is not None:
                out_refs[oi][...] = d.astype(out_refs[oi].dtype)
                oi += 1
        for pr, gp in zip(out_refs[len(want):], grads[gi:]):
            pr[...] += gp

    res = pl.pallas_call(
        body, name=name, grid=(nb,), in_specs=specs, out_specs=out_specs, out_shape=out_shape,
        scratch_shapes=scratch, compiler_params=_params(),
    )(*args)
    return res[:len(want)], res[len(want):]


def _pick(n, pref):
    for c in pref:
        if n % c == 0:
            return c
    return n


MM_VMEM_BUDGET = 40 * 1024 * 1024
MM_PEAK_FLOPS = 0.9e15
MM_HBM_BYTES_PER_S = 3.0e12
MM_STEP_SECONDS = 0.35e-6


def _mm_tiles(m, n, k, size_a, size_b, size_out, size_res, single_k):
    best = None
    for tk in sorted({c for c in (k, 2048, 1024, 512, 256, 128) if c <= 2048 and k % c == 0}, reverse=True):
        for tm in (1024, 512, 256, 128):
            if m % tm:
                continue
            for tn in (1024, 768, 512, 384, 256, 128):
                if n % tn:
                    continue
                nk = k // tk
                vmem = 2 * (tm * tk * size_a + tk * tn * size_b + tm * tn * (size_out + size_res))
                vmem += tm * tn * 4 * (2 if nk > 1 or not single_k else 1)
                vmem += (tm * tk * 2 if size_a > 2 else 0) + (tk * tn * 2 if size_b > 2 else 0)
                if vmem > MM_VMEM_BUDGET:
                    continue
                steps = (m // tm) * (n // tn) * nk
                a_reads = 1 if (nk == 1 and single_k) else n // tn
                traffic = m * k * size_a * a_reads + k * n * size_b * (m // tm) + m * n * (size_out + size_res)
                cost = max(2.0 * m * n * k / MM_PEAK_FLOPS, traffic / MM_HBM_BYTES_PER_S) + steps * MM_STEP_SECONDS
                if best is None or cost < best[0]:
                    best = (cost, tm, tn, tk)
    return best[1:]


def mm(name, a, b, mode, out_dtype=F32, res=None, b_slabs=None, out_slabs=None, dep=None, epi=None, extras=(),
       out_dtypes=None):
    if mode == "tn":
        k_dim, m_dim = a.shape
    else:
        m_dim, k_dim = a.shape
    if b_slabs:
        n_dim = b.shape[0] * b.shape[2] if mode == "nn" else b.shape[1]
    else:
        n_dim = b.shape[0] if mode == "nt" else b.shape[1]
    n_slabs = out_slabs or (b_slabs if (b_slabs and mode == "nn") else 1)
    k_slabs = b_slabs if (b_slabs and mode == "nt") else 1
    if epi is None:
        out_dtypes = [out_dtype]
        if res is None:
            epi = lambda acc: (acc,)
        else:
            extras, epi = [res], lambda acc, r: (acc + r,)
    tm, tn, tk = _mm_tiles(m_dim, n_dim // n_slabs, k_dim // k_slabs, a.dtype.itemsize, b.dtype.itemsize,
                           sum(jnp.dtype(dt).itemsize for dt in out_dtypes), sum(e.dtype.itemsize for e in extras),
                           single_k=(k_slabs == 1))
    nji = n_dim // n_slabs // tn
    nki = k_dim // k_slabs // tk
    nblk = lambda js, j: js * nji + j
    kblk = lambda ks, k: ks * nki + k
    if mode == "tn":
        a_spec = pl.BlockSpec((tk, tm), lambda i, js, j, ks, k: (kblk(ks, k), i))
    else:
        a_spec = pl.BlockSpec((tm, tk), lambda i, js, j, ks, k: (i, kblk(ks, k)))
    if b_slabs and mode == "nn":
        b_spec = pl.BlockSpec((None, tk, tn), lambda i, js, j, ks, k: (js, k, j))
    elif b_slabs and mode == "nt":
        b_spec = pl.BlockSpec((None, tn, tk), lambda i, js, j, ks, k: (ks, nblk(js, j), k))
    elif mode == "nt":
        b_spec = pl.BlockSpec((tn, tk), lambda i, js, j, ks, k: (nblk(js, j), kblk(ks, k)))
    else:
        b_spec = pl.BlockSpec((tk, tn), lambda i, js, j, ks, k: (kblk(ks, k), nblk(js, j)))
    specs, args = [a_spec, b_spec], [a, b]
    for e in extras:
        specs.append(pl.BlockSpec((tm, tn), lambda i, js, j, ks, k: (i, nblk(js, j))))
        args.append(e)
    if dep is not None:
        specs.append(pl.BlockSpec(memory_space=pl.ANY))
        args.append(dep)
    if out_slabs:
        o_specs = [pl.BlockSpec((None, tm, tn), lambda i, js, j, ks, k: (js, i, j))]
        o_shapes = [jax.ShapeDtypeStruct((out_slabs, m_dim, n_dim // out_slabs), out_dtypes[0])]
    else:
        o_specs = [pl.BlockSpec((tm, tn), lambda i, js, j, ks, k: (i, nblk(js, j))) for _ in out_dtypes]
        o_shapes = [jax.ShapeDtypeStruct((m_dim, n_dim), dt) for dt in out_dtypes]

    one_k_step = k_slabs * nki == 1
    n_in, n_out = len(args), len(out_dtypes)

    def body(*refs):
        a_ref, b_ref = refs[0], refs[1]
        part = _dg(a_ref[...].astype(BF16), b_ref[...].astype(BF16), mode)

        def finish(acc):
            outs = epi(acc, *[refs[2 + j][...].astype(F32) for j in range(len(extras))])
            for o_ref, o in zip(refs[n_in:n_in + n_out], outs):
                o_ref[...] = o.astype(o_ref.dtype)

        if one_k_step:
            finish(part)
            return
        acc_ref = refs[n_in + n_out]
        ks, kk = pl.program_id(3), pl.program_id(4)

        @pl.when((ks == 0) & (kk == 0))
        def _():
            acc_ref[...] = part

        @pl.when((ks > 0) | (kk > 0))
        def _():
            acc_ref[...] += part

        pl.when((ks == k_slabs - 1) & (kk == nki - 1))(lambda: finish(acc_ref[...]))

    grid = (m_dim // tm, n_slabs, nji, k_slabs, nki)
    scratch = [] if one_k_step else [pltpu.VMEM((tm, tn), F32)]
    out = pl.pallas_call(
        body, name=name, grid=grid, in_specs=specs, out_specs=o_specs, out_shape=o_shapes, scratch_shapes=scratch,
        compiler_params=pltpu.CompilerParams(
            dimension_semantics=("parallel", "parallel", "parallel", "arbitrary", "arbitrary"),
            vmem_limit_bytes=VMEM_LIMIT),
    )(*args)
    return out[0] if n_out == 1 else out


def _stack_lanes(x, n):
    w = x.shape[1] // n
    return jnp.stack([x[:, i * w:(i + 1) * w] for i in range(n)])


def _stack_rows(x, n):
    w = x.shape[0] // n
    return jnp.stack([x[i * w:(i + 1) * w, :] for i in range(n)])


def rwkv_scan_fwd(r, lw, k, v, kap, b):
    t = r.shape[0]
    c, hps, hd = min(RWKV_CHUNK, t), RWKV_HEADS_PER_STEP, RWKV_HEAD_DIM
    nc, ng, wl = t // c, RWKV_HEADS // hps, hps * hd
    spec = pl.BlockSpec((c, wl), lambda g, ci: (ci, g))

    def body(r_ref, lw_ref, k_ref, v_ref, kap_ref, b_ref, y_ref, ck_ref, st_ref):
        @pl.when(pl.program_id(1) == 0)
        def _():
            st_ref[...] = jnp.zeros_like(st_ref)

        st = st_ref[...]
        ck_ref[...] = st
        ins = [x[...] for x in (r_ref, lw_ref, k_ref, v_ref, kap_ref, b_ref)]
        y, st1 = rwkv_chunk_fn(_stack_rows(st, hps), *[_stack_lanes(x, hps) for x in ins])
        y_ref[...] = jnp.concatenate([y[h] for h in range(hps)], axis=-1)
        st_ref[...] = jnp.concatenate([st1[h] for h in range(hps)], axis=0)

    return pl.pallas_call(
        body, name="rwkv_scan_fwd", grid=(ng, nc), in_specs=[spec] * 6,
        out_specs=[spec, pl.BlockSpec((None, wl, hd), lambda g, ci: (ci, g, 0))],
        out_shape=[jax.ShapeDtypeStruct((t, RWKV_WIDTH), F32), jax.ShapeDtypeStruct((nc, RWKV_WIDTH, hd), F32)],
        scratch_shapes=[pltpu.VMEM((wl, hd), F32)], compiler_params=_params(),
    )(r, lw, k, v, kap, b)


def rwkv_scan_bwd(r, lw, k, v, kap, b, ck, dy):
    t = r.shape[0]
    c, hps, hd = min(RWKV_CHUNK, t), RWKV_HEADS_PER_STEP, RWKV_HEAD_DIM
    nc, ng, wl = t // c, RWKV_HEADS // hps, hps * hd
    spec = pl.BlockSpec((c, wl), lambda g, ci: (nc - 1 - ci, g))

    def body(r_ref, lw_ref, k_ref, v_ref, kap_ref, b_ref, ck_ref, dy_ref, *rest):
        out_refs, dst_ref = rest[:6], rest[6]

        @pl.when(pl.program_id(1) == 0)
        def _():
            dst_ref[...] = jnp.zeros_like(dst_ref)

        ins = [x[...] for x in (r_ref, lw_ref, k_ref, v_ref, kap_ref, b_ref)]
        dyv, ck, dst = dy_ref[...], ck_ref[...], dst_ref[...]
        _, vjp = jax.vjp(rwkv_chunk_fn, _stack_rows(ck, hps), *[_stack_lanes(x, hps) for x in ins])
        grads = vjp((_stack_lanes(dyv, hps), _stack_rows(dst, hps)))
        dst_ref[...] = jnp.concatenate([grads[0][h] for h in range(hps)], axis=0)
        for j in range(6):
            out_refs[j][...] = jnp.concatenate([grads[1 + j][h] for h in range(hps)], axis=-1)

    return pl.pallas_call(
        body, name="rwkv_scan_bwd", grid=(ng, nc),
        in_specs=[spec] * 6 + [pl.BlockSpec((None, wl, hd), lambda g, ci: (nc - 1 - ci, g, 0)), spec],
        out_specs=[spec] * 6, out_shape=[jax.ShapeDtypeStruct((t, RWKV_WIDTH), F32)] * 6,
        scratch_shapes=[pltpu.VMEM((wl, hd), F32)], compiler_params=_params(),
    )(r, lw, k, v, kap, b, ck, dy)


def _ssd_specs(q, blk):
    gw = SSD_WIDTH // 2
    return [pl.BlockSpec((q, gw), lambda g, ci: (blk(ci), g)),
            pl.BlockSpec((q, SSD_STATE), lambda g, ci: (blk(ci), g)),
            pl.BlockSpec((q, SSD_STATE), lambda g, ci: (blk(ci), g)),
            pl.BlockSpec((q, LANES), lambda g, ci: (blk(ci), 0)),
            pl.BlockSpec((1, LANES), lambda g, ci: (0, 0)),
            pl.BlockSpec((1, LANES), lambda g, ci: (0, 0))]


def ssd_scan_fwd(xs, bm, cm, dt, a_log, d_skip):
    t = xs.shape[0]
    q = min(SSD_CHUNK, t)
    nc, gw = t // q, SSD_WIDTH // 2

    def body(xs_ref, bm_ref, cm_ref, dt_ref, al_ref, d_ref, y_ref, ck_ref, h_ref):
        @pl.when(pl.program_id(1) == 0)
        def _():
            h_ref[...] = jnp.zeros_like(h_ref)

        ck_ref[...] = h_ref[...]
        args = (h_ref[...], xs_ref[...], bm_ref[...], cm_ref[...], dt_ref[...], al_ref[...], d_ref[...])
        g = pl.program_id(0)

        @pl.when(g == 0)
        def _():
            y, h1 = ssd_chunk_fn(0, *args)
            y_ref[...] = y
            h_ref[...] = h1

        @pl.when(g == 1)
        def _():
            y, h1 = ssd_chunk_fn(1, *args)
            y_ref[...] = y
            h_ref[...] = h1

    return pl.pallas_call(
        body, name="ssd_scan_fwd", grid=(2, nc), in_specs=_ssd_specs(q, lambda ci: ci),
        out_specs=[pl.BlockSpec((q, gw), lambda g, ci: (ci, g)),
                   pl.BlockSpec((None, gw, SSD_STATE), lambda g, ci: (ci, g, 0))],
        out_shape=[jax.ShapeDtypeStruct((t, SSD_WIDTH), F32), jax.ShapeDtypeStruct((nc, SSD_WIDTH, SSD_STATE), F32)],
        scratch_shapes=[pltpu.VMEM((gw, SSD_STATE), F32)], compiler_params=_params(),
    )(xs, bm, cm, dt, a_log, d_skip)


def ssd_scan_bwd(xs, bm, cm, dt, a_log, d_skip, ck, dy):
    t = xs.shape[0]
    q = min(SSD_CHUNK, t)
    nc, gw = t // q, SSD_WIDTH // 2
    rev = lambda ci: nc - 1 - ci

    def body(xs_ref, bm_ref, cm_ref, dt_ref, al_ref, d_ref, ck_ref, dy_ref,
             dxs_ref, dbm_ref, dcm_ref, ddt_ref, dal_ref, dd_ref, dh_ref):
        g, ci = pl.program_id(0), pl.program_id(1)

        @pl.when(ci == 0)
        def _():
            dh_ref[...] = jnp.zeros_like(dh_ref)

        @pl.when((ci == 0) & (g == 0))
        def _():
            dal_ref[...] = jnp.zeros_like(dal_ref)
            dd_ref[...] = jnp.zeros_like(dd_ref)

        args = (ck_ref[...], xs_ref[...], bm_ref[...], cm_ref[...], dt_ref[...], al_ref[...], d_ref[...])

        def run(group):
            _, vjp = jax.vjp(functools.partial(ssd_chunk_fn, group), *args)
            dh0, dxs, dbm, dcm, ddt, dal, dd = vjp((dy_ref[...], dh_ref[...]))
            dh_ref[...] = dh0
            dxs_ref[...] = dxs
            dbm_ref[...] = dbm
            dcm_ref[...] = dcm
            ddt_ref[...] = ddt
            dal_ref[...] += dal
            dd_ref[...] += dd

        pl.when(g == 0)(lambda: run(0))
        pl.when(g == 1)(lambda: run(1))

    in_specs = _ssd_specs(q, rev) + [pl.BlockSpec((None, gw, SSD_STATE), lambda g, ci: (rev(ci), g, 0)),
                                     pl.BlockSpec((q, gw), lambda g, ci: (rev(ci), g))]
    return pl.pallas_call(
        body, name="ssd_scan_bwd", grid=(2, nc), in_specs=in_specs,
        out_specs=[pl.BlockSpec((q, gw), lambda g, ci: (rev(ci), g)),
                   pl.BlockSpec((q, SSD_STATE), lambda g, ci: (rev(ci), g)),
                   pl.BlockSpec((q, SSD_STATE), lambda g, ci: (rev(ci), g)),
                   pl.BlockSpec((None, q, LANES), lambda g, ci: (g, rev(ci), 0)),
                   pl.BlockSpec((1, LANES), lambda g, ci: (0, 0)),
                   pl.BlockSpec((1, LANES), lambda g, ci: (0, 0))],
        out_shape=[jax.ShapeDtypeStruct((t, SSD_WIDTH), F32), jax.ShapeDtypeStruct((t, 2 * SSD_STATE), F32),
                   jax.ShapeDtypeStruct((t, 2 * SSD_STATE), F32), jax.ShapeDtypeStruct((2, t, LANES), F32),
                   jax.ShapeDtypeStruct((1, LANES), F32), jax.ShapeDtypeStruct((1, LANES), F32)],
        scratch_shapes=[pltpu.VMEM((gw, SSD_STATE), F32)], compiler_params=_params(),
    )(xs, bm, cm, dt, a_log, d_skip, ck, dy)


def loss_and_grad(x, tgt, g, tm):
    t, d = x.shape
    tm = min(tm, t)
    nb = t // tm

    def body(x_ref, t_ref, g_ref, loss_ref, dx_ref, dg_ref):
        @pl.when(pl.program_id(0) == 0)
        def _():
            loss_ref[...] = jnp.zeros_like(loss_ref)
            dg_ref[...] = jnp.zeros_like(dg_ref)

        val, vjp = jax.vjp(loss_fn, x_ref[...], t_ref[...], g_ref[...])
        dx, _, dg = vjp(jnp.ones((1, 1), F32))
        loss_ref[...] += jnp.broadcast_to(val, loss_ref.shape)
        dx_ref[...] = dx
        dg_ref[...] += dg

    row = pl.BlockSpec((tm, d), lambda i: (i, 0))
    one = pl.BlockSpec((1, d), lambda i: (0, 0))
    return pl.pallas_call(
        body, name="loss_and_grad", grid=(nb,), in_specs=[row, row, one],
        out_specs=[pl.BlockSpec((SUBLANES, LANES), lambda i: (0, 0)), row, one],
        out_shape=[jax.ShapeDtypeStruct((SUBLANES, LANES), F32), jax.ShapeDtypeStruct((t, d), F32),
                   jax.ShapeDtypeStruct((1, d), F32)],
        compiler_params=_params(),
    )(x, tgt, g)


def adamw(name, recv, w, m, v, dep=None):
    rows, cols = w.shape
    recv_block_bytes = 4 * 1024 * 1024
    tm = _pick(rows, [c for c in (256, 128, 64, 32, 16, 8) if N_DEV * c * cols * 4 <= recv_block_bytes])
    c1 = 1.0 / (1.0 - ADAM_B1 ** ADAM_STEP)
    c2 = 1.0 / (1.0 - ADAM_B2 ** ADAM_STEP)

    n_dep = 0 if dep is None else 1

    def body(recv_ref, w_ref, m_ref, v_ref, *rest):
        g_ref, d_ref, nm_ref, nv_ref = rest[n_dep:]
        g = recv_ref[0].astype(F32)
        for p in range(1, N_DEV):
            g = g + recv_ref[p].astype(F32)
        nm = ADAM_B1 * m_ref[...] + (1.0 - ADAM_B1) * g
        nv = ADAM_B2 * v_ref[...] + (1.0 - ADAM_B2) * jnp.square(g)
        g_ref[...] = g
        nm_ref[...] = nm
        nv_ref[...] = nv
        d_ref[...] = -ADAM_LR * ((nm * c1) / (jnp.sqrt(nv * c2) + ADAM_EPS) + ADAM_WD * w_ref[...])

    blk = pl.BlockSpec((tm, cols), lambda i: (i, 0))
    return pl.pallas_call(
        body, name=name, grid=(rows // tm,),
        in_specs=[pl.BlockSpec((N_DEV, tm, cols), lambda i: (0, i, 0)), blk, blk, blk]
        + [pl.BlockSpec(memory_space=pl.ANY)] * n_dep,
        out_specs=[blk] * 4, out_shape=[jax.ShapeDtypeStruct((rows, cols), F32)] * 4,
        compiler_params=_params(),
    )(recv, w, m, v, *([] if dep is None else [dep]))


def _mesh_pos():
    return lax.axis_index("x"), lax.axis_index("y"), lax.axis_index("c")


def _peer(pos, mask):
    x, y, c = pos
    return (1 - x if mask & 4 else x, 1 - y if mask & 2 else y, 1 - c if mask & 1 else c)


def _linear(pos):
    return 4 * pos[0] + 2 * pos[1] + pos[2]


class Exchange:
    def __init__(self, xs, scatter):
        self.xs, self.scatter, self.n = list(xs), scatter, len(xs)
        self.out_shape = [jax.ShapeDtypeStruct(x.shape if scatter else (N_DEV,) + x.shape, x.dtype) for x in xs]
        self.specs = [pl.BlockSpec(memory_space=pl.ANY)] * self.n
        peers = N_DEV - 1
        self.scratch = [pltpu.SemaphoreType.DMA((self.n * peers,)), pltpu.SemaphoreType.DMA((self.n * peers,)),
                        pltpu.SemaphoreType.DMA((self.n,))]

    def _copies(self, ins, outs, sems, landing):
        send_sems, recv_sems, local_sems = sems
        me = _mesh_pos()
        me_lin = _linear(me)
        local, remote = [], []
        for ti in range(self.n):
            src_mine = ins[ti].at[me_lin] if self.scatter else ins[ti]
            local.append(pltpu.make_async_copy(src_mine, outs[ti].at[me_lin], local_sems.at[ti]))
            for mask in range(1, N_DEV):
                peer = _peer(me, mask)
                src = ins[ti].at[_linear(peer)] if self.scatter else ins[ti]
                dst = outs[ti].at[_linear(peer) if landing else me_lin]
                remote.append(pltpu.make_async_remote_copy(
                    src_ref=src, dst_ref=dst, send_sem=send_sems.at[ti * (N_DEV - 1) + mask - 1],
                    recv_sem=recv_sems.at[ti * (N_DEV - 1) + mask - 1],
                    device_id=peer, device_id_type=pl.DeviceIdType.MESH))
        return local, remote

    def start(self, ins, outs, sems):
        local, remote = self._copies(ins, outs, sems, landing=False)
        for cp in local + remote:
            cp.start()

    def finish(self, ins, outs, sems):
        local, remote = self._copies(ins, outs, sems, landing=True)
        for cp in remote:
            cp.wait_recv()
        for cp in remote:
            cp.wait_send()
        for cp in local:
            cp.wait()


def exchange_start(name, xs, scatter, dep=None):
    ex = Exchange(xs, scatter)
    n = ex.n
    hbm = pl.BlockSpec(memory_space=pltpu.HBM)
    sem = pl.BlockSpec(memory_space=pltpu.SEMAPHORE)
    lands = [lax.empty(s.shape, s.dtype) for s in ex.out_shape]

    n_inputs = 2 * n + (0 if dep is None else 1)

    def body(*refs):
        ins, lnd, sems, token = refs[:n], refs[n:2 * n], refs[n_inputs:n_inputs + 3], refs[-1]
        ex.start(ins, lnd, sems)
        token[...] = jnp.zeros_like(token)

    res = pl.pallas_call(
        body, name=name, in_specs=[hbm] * (2 * n) + ([] if dep is None else [pl.BlockSpec(memory_space=pl.ANY)]),
        out_specs=[sem] * 3 + [hbm] * (2 * n) + [pl.BlockSpec(memory_space=pltpu.VMEM)],
        out_shape=ex.scratch + [pltpu.HBM(x.shape, x.dtype) for x in xs]
        + [pltpu.HBM(s.shape, s.dtype) for s in ex.out_shape] + [jax.ShapeDtypeStruct((SUBLANES, LANES), F32)],
        input_output_aliases={i: 3 + i for i in range(2 * n)},
        compiler_params=pltpu.CompilerParams(has_side_effects=pltpu.SideEffectType.DATAFLOW_SIDE_EFFECTING),
    )(*[pltpu.with_memory_space_constraint(x, pltpu.HBM) for x in list(xs) + lands], *([] if dep is None else [dep]))
    return (ex, res[:3], res[3:3 + n], res[3 + n:3 + 2 * n]), res[-1]


def exchange_wait(name, handles, after):
    ex, sems, srcs, lands = handles
    n = ex.n
    hbm = pl.BlockSpec(memory_space=pltpu.HBM)
    sem = pl.BlockSpec(memory_space=pltpu.SEMAPHORE)

    def body(*refs):
        ins, lnd, sem_refs = refs[:n], refs[n:2 * n], refs[2 * n:2 * n + 3]
        ex.finish(ins, lnd, sem_refs)

    res = pl.pallas_call(
        body, name=name, in_specs=[hbm] * (2 * n) + [sem] * 3 + [pl.BlockSpec(memory_space=pl.ANY)],
        out_specs=[hbm] * (2 * n),
        out_shape=[pltpu.HBM(x.shape, x.dtype) for x in srcs] + [pltpu.HBM(x.shape, x.dtype) for x in lands],
        input_output_aliases={i: i for i in range(2 * n)},
        compiler_params=pltpu.CompilerParams(has_side_effects=pltpu.SideEffectType.DATAFLOW_SIDE_EFFECTING),
    )(*srcs, *lands, *sems, after)
    return res[n:]


_Z = (0, 1024)
_XBC = (1024, 2560)
_DT = (2560, 2576)
_RKV = (2576, 5648)
_PW = (5648, 5744)
_PA = (5744, 5840)
_PG = (5840, 6096)
D_IN = 6096

_SMALL = ("norm_mix_g", "ssd_conv_b", "ssd_dt_bias", "ssd_a_log", "ssd_d", "ssd_norm_g", "rwkv_mu", "rwkv_w0",
          "rwkv_a0", "rwkv_k_k", "rwkv_k_a", "rwkv_r_k", "rwkv_ln_w", "rwkv_ln_b", "norm_x_g", "norm_mem_g",
          "norm_ffn_g", "final_norm_g")
_WEIGHTS = ("norm_mix_g", "w_in", "ssd_conv_w", "ssd_conv_b", "ssd_dt_bias", "ssd_a_log", "ssd_d", "ssd_norm_g",
            "rwkv_mu", "rwkv_w0", "rwkv_w2", "rwkv_a0", "rwkv_a2", "rwkv_g2", "rwkv_k_k", "rwkv_k_a", "rwkv_r_k",
            "rwkv_ln_w", "rwkv_ln_b", "w_out", "norm_x_g", "norm_mem_g", "xattn_wq", "xattn_wk", "xattn_wv",
            "xattn_wo", "norm_ffn_g", "ffn_w1", "ffn_w2", "final_norm_g")


def _pad_lanes(x, width=LANES):
    return jnp.pad(x, ((0, 0), (0, width - x.shape[1])))


def _pack_small(vals):
    flat = jnp.concatenate([vals[n].reshape(-1) for n in _SMALL])
    rows = -(-flat.shape[0] // (LANES * SUBLANES)) * SUBLANES
    return jnp.pad(flat, (0, rows * LANES - flat.shape[0])).reshape(rows, LANES)


def _unpack_small(packed, shapes):
    flat = packed.reshape(-1)
    out, pos = {}, 0
    for n in _SMALL:
        size = 1
        for s in shapes[n]:
            size *= s
        out[n] = flat[pos:pos + size].reshape(shapes[n])
        pos += size
    return out


def _cols(w, rng):
    return w[:, rng[0]:rng[1]]


def kernel(x, mem, norm_mix_g, w_in, ssd_conv_w, ssd_conv_b, ssd_dt_bias, ssd_a_log, ssd_d, ssd_norm_g, rwkv_mu, rwkv_w0, rwkv_w2, rwkv_a0, rwkv_a2, rwkv_g2, rwkv_k_k, rwkv_k_a, rwkv_r_k, rwkv_ln_w, rwkv_ln_b, w_out, norm_x_g, norm_mem_g, xattn_wq, xattn_wk, xattn_wv, xattn_wo, norm_ffn_g, ffn_w1, ffn_w2, final_norm_g, loss_target, m_norm_mix_g, m_w_in, m_ssd_conv_w, m_ssd_conv_b, m_ssd_dt_bias, m_ssd_a_log, m_ssd_d, m_ssd_norm_g, m_rwkv_mu, m_rwkv_w0, m_rwkv_w2, m_rwkv_a0, m_rwkv_a2, m_rwkv_g2, m_rwkv_k_k, m_rwkv_k_a, m_rwkv_r_k, m_rwkv_ln_w, m_rwkv_ln_b, m_w_out, m_norm_x_g, m_norm_mem_g, m_xattn_wq, m_xattn_wk, m_xattn_wv, m_xattn_wo, m_norm_ffn_g, m_ffn_w1, m_ffn_w2, m_final_norm_g, v_norm_mix_g, v_w_in, v_ssd_conv_w, v_ssd_conv_b, v_ssd_dt_bias, v_ssd_a_log, v_ssd_d, v_ssd_norm_g, v_rwkv_mu, v_rwkv_w0, v_rwkv_w2, v_rwkv_a0, v_rwkv_a2, v_rwkv_g2, v_rwkv_k_k, v_rwkv_k_a, v_rwkv_r_k, v_rwkv_ln_w, v_rwkv_ln_b, v_w_out, v_norm_x_g, v_norm_mem_g, v_xattn_wq, v_xattn_wk, v_xattn_wv, v_xattn_wo, v_norm_ffn_g, v_ffn_w1, v_ffn_w2, v_final_norm_g):
    given = dict(locals())
    wts = {n: given[n] for n in _WEIGHTS}
    mom_m = {n: given["m_" + n] for n in _WEIGHTS}
    mom_v = {n: given["v_" + n] for n in _WEIGHTS}
    d = D_MODEL
    xt, memt, tgt = x[0], mem[0], loss_target[0]
    tm = 256

    big = {"w_in": w_in[0], "w_out": w_out[0], "xattn_wq": xattn_wq[0], "xattn_wk": xattn_wk[0],
           "xattn_wv": xattn_wv[0], "xattn_wo": xattn_wo[0], "ffn_w1": ffn_w1[0], "ffn_w2": ffn_w2[0]}
    small_sh = {"ssd_conv_w": ssd_conv_w.reshape(4, -1), "rwkv_w2": rwkv_w2[0], "rwkv_a2": rwkv_a2[0],
                "rwkv_g2": rwkv_g2[0]}
    cast_one = lambda n, deps=(): rowwise_fwd("cast_" + n, cast_fn, [big[n]], [], [(big[n].shape[1], BF16)], 256,
                                              deps=deps)[0]
    gather_in, token_in = exchange_start("gather_in_start", [cast_one("w_in")] + list(small_sh.values()), scatter=False)
    cast = {n: cast_one(n, deps=[token_in]) for n in big if n != "w_in"}
    late_a = ("w_out", "xattn_wq", "xattn_wk", "xattn_wv", "xattn_wo")
    late_b = ("ffn_w1", "ffn_w2")
    gather_a, token_a = exchange_start("gather_attn_start", [cast[n] for n in late_a], scatter=False, dep=token_in)
    gather_b, token_b = exchange_start("gather_ffn_start", [cast[n] for n in late_b], scatter=False, dep=token_a)
    (h1,) = rowwise_fwd("norm_mix", rmsnorm_fn, [xt], [norm_mix_g], [(d, BF16)], tm, deps=[token_b])
    gathered = exchange_wait("gather_in_wait", gather_in, after=h1)
    g_big = {"w_in": gathered[0]}
    g_small = dict(zip(small_sh, gathered[1:]))

    w_in_full = jnp.transpose(g_big["w_in"], (1, 0, 2)).reshape(d, D_IN)
    w_z, w_xbc, w_rkv, w_pg = (_cols(w_in_full, r) for r in (_Z, _XBC, _RKV, _PG))
    w_sm = jnp.concatenate([_pad_lanes(_cols(w_in_full, r)) for r in (_PW, _PA, _DT)], axis=1)
    unshard_cols = lambda g: jnp.transpose(g, (1, 0, 2)).reshape(g.shape[1], -1)
    conv_w_f = unshard_cols(g_small["ssd_conv_w"])
    pad_rows = lambda a: jnp.pad(a, ((0, LANES - a.shape[0]), (0, 0)))
    w2p, a2p = pad_rows(unshard_cols(g_small["rwkv_w2"])), pad_rows(unshard_cols(g_small["rwkv_a2"]))
    g2_f = unshard_cols(g_small["rwkv_g2"])

    mu = rwkv_mu
    mu_rkv, mu_pg = mu[:, :3072], mu[:, 3264:3520]
    mu_pwa = jnp.concatenate([_pad_lanes(mu[:, 3072:3168]), _pad_lanes(mu[:, 3168:3264])], axis=1)
    dt_bias_p, a_log_p, d_p = _pad_lanes(ssd_dt_bias), _pad_lanes(ssd_a_log), _pad_lanes(ssd_d)
    r_k_row = rwkv_r_k.reshape(1, RWKV_WIDTH)
    g_final = final_norm_g.reshape(1, d)

    u_z = mm("in_z", h1, w_z, "nn")
    u_xbc = mm("in_xbc", h1, w_xbc, "nn")
    u_rkv = mm("in_rkv", h1, w_rkv, "nn")
    u_pg = mm("in_pg", h1, w_pg, "nn")
    u_sm = mm("in_small", h1, w_sm, "nn")

    ssd_pre_rows = lambda: [Rows(u_xbc, shifts=(1, 2, 3)), Rows(u_sm, LANES, 2)]
    ssd_pre_params = [conv_w_f, ssd_conv_b, dt_bias_p]
    xs, bm, cm, dt = rowwise_fwd("ssd_pre", ssd_pre_fn, ssd_pre_rows(), ssd_pre_params,
                                 [(SSD_WIDTH, F32), (256, F32), (256, F32), (LANES, F32)], tm)
    y_scan, ssd_ck = ssd_scan_fwd(xs, bm, cm, dt, a_log_p, d_p)
    (y_ssd,) = rowwise_fwd("ssd_post", ssd_post_fn, [y_scan, u_z], [ssd_norm_g], [(SSD_WIDTH, BF16)], tm)

    rwkv_pre_rows = lambda: [Rows(u_rkv, shifts=(1,)), Rows(u_pg, shifts=(1,)), Rows(u_sm, 2 * LANES, 0, shifts=(1,))]
    rwkv_pre_params = [mu_rkv, mu_pg, mu_pwa, rwkv_w0, w2p, rwkv_a0, a2p, g2_f, rwkv_k_k, rwkv_k_a]
    r_, lw_, k_, v_, kap_, b_, gate_ = rowwise_fwd("rwkv_pre", rwkv_pre_fn, rwkv_pre_rows(), rwkv_pre_params,
                                                   [(RWKV_WIDTH, F32)] * 7, 128)
    ys_r, rwkv_ck = rwkv_scan_fwd(r_, lw_, k_, v_, kap_, b_)
    g_big.update(zip(late_a, exchange_wait("gather_attn_wait", gather_a, after=ys_r)))
    w_out_f = g_big["w_out"].reshape(d, d)
    wq_f, wk_f, wv_f, wo_f = (g_big[n].reshape(d, d) for n in ("xattn_wq", "xattn_wk", "xattn_wv", "xattn_wo"))
    rwkv_post_params = [rwkv_ln_w, rwkv_ln_b, r_k_row]
    (y_rwkv,) = rowwise_fwd("rwkv_post", rwkv_post_fn, [ys_r, r_, k_, v_, gate_], rwkv_post_params,
                            [(RWKV_WIDTH, BF16)], tm)
    ycat = jnp.concatenate([y_ssd, y_rwkv], axis=1)
    x1 = mm("out_proj", ycat, w_out_f, "nn", res=xt)

    (h2,) = rowwise_fwd("norm_x", rmsnorm_fn, [x1], [norm_x_g], [(d, BF16)], tm)
    (mn,) = rowwise_fwd("norm_mem", rmsnorm_fn, [memt], [norm_mem_g], [(d, BF16)], tm)
    q = mm("xattn_q", h2, wq_f, "nn", out_dtype=BF16)
    kx = mm("xattn_k", mn, wk_f, "nn")
    vx = mm("xattn_v", mn, wv_f, "nn")
    (o,) = rowwise_fwd("xattn", attn_fn, [q], [kx, vx], [(d, BF16)], tm)
    x2 = mm("xattn_o", o, wo_f, "nn", res=x1)

    (h3,) = rowwise_fwd("norm_ffn", rmsnorm_fn, [x2], [norm_ffn_g], [(d, BF16)], tm)
    w1_s, w2_g = exchange_wait("gather_ffn_wait", gather_b, after=h3)
    w2_f = w2_g.reshape(D_FF, d)
    relu2_epi = lambda acc: (jnp.square(jnp.maximum(acc, 0.0)), jnp.maximum(acc, 0.0))
    hid, relu_a = mm("ffn_1", h3, w1_s, "nn", b_slabs=N_DEV, epi=relu2_epi, out_dtypes=[BF16, BF16])
    x3 = mm("ffn_2", hid, w2_f, "nn", res=x2)

    loss_blk, dx3, dg_final = loss_and_grad(x3, tgt, g_final, tm)

    grads = {}
    grads["ffn_w2"] = mm("d_ffn_w2", hid, dx3, "tn", out_dtype=BF16).reshape(N_DEV, D_FF // N_DEV, d)
    sc_w2, tok = exchange_start("scatter_ffn_w2_start", [grads["ffn_w2"]], scatter=True)
    da = mm("d_hid", dx3, w2_f, "nt", dep=tok, epi=lambda acc, ra: (2.0 * acc * ra,), extras=[relu_a],
            out_dtypes=[BF16])
    grads["ffn_w1"] = mm("d_ffn_w1", h3, da, "tn", out_dtype=BF16, out_slabs=N_DEV)
    sc_w1, tok = exchange_start("scatter_ffn_w1_start", [grads["ffn_w1"]], scatter=True)
    dh3 = mm("d_h3", da, w1_s, "nt", b_slabs=N_DEV, dep=tok)
    (dx2,), (dg_ffn,) = rowwise_bwd("norm_ffn_bwd", rmsnorm_fn, [x2], [norm_ffn_g], [[dh3]], tm, [F32], row_add=[dx3])

    grads["xattn_wo"] = mm("d_wo", o, dx2, "tn", out_dtype=BF16).reshape(N_DEV, d // N_DEV, d)
    sc_wo, tok = exchange_start("scatter_wo_start", [grads["xattn_wo"]], scatter=True)
    d_o = mm("d_o", dx2, wo_f, "nt", dep=tok)
    (dq,), (dkx, dvx) = rowwise_bwd("xattn_bwd", attn_fn, [q], [kx, vx], [[d_o]], tm, [BF16])
    grads["xattn_wq"] = mm("d_wq", h2, dq, "tn", out_dtype=BF16).reshape(N_DEV, d // N_DEV, d)
    grads["xattn_wk"] = mm("d_wk", mn, dkx, "tn", out_dtype=BF16).reshape(N_DEV, d // N_DEV, d)
    grads["xattn_wv"] = mm("d_wv", mn, dvx, "tn", out_dtype=BF16).reshape(N_DEV, d // N_DEV, d)
    qkv = ("xattn_wq", "xattn_wk", "xattn_wv")
    sc_qkv, tok = exchange_start("scatter_qkv_start", [grads[n] for n in qkv], scatter=True)
    dmn = mm("d_mn_v", dvx, wv_f, "nt", res=mm("d_mn_k", dkx, wk_f, "nt", dep=tok))
    _, (dg_mem,) = rowwise_bwd("norm_mem_bwd", rmsnorm_fn, [memt], [norm_mem_g], [[dmn]], tm, [None])
    dh2 = mm("d_h2", dq, wq_f, "nt", dep=dg_mem)
    (dx1,), (dg_x,) = rowwise_bwd("norm_x_bwd", rmsnorm_fn, [x1], [norm_x_g], [[dh2]], tm, [F32], row_add=[dx2])

    grads["w_out"] = mm("d_w_out", ycat, dx1, "tn", out_dtype=BF16).reshape(N_DEV, d // N_DEV, d)
    sc_wout, tok = exchange_start("scatter_w_out_start", [grads["w_out"]], scatter=True)
    d_ycat = mm("d_ycat", dx1, w_out_f, "nt", dep=tok)

    (d_ys, d_r1, d_k1, d_v1, d_gate), (dln_w, dln_b, dr_k) = rowwise_bwd(
        "rwkv_post_bwd", rwkv_post_fn, [ys_r, r_, k_, v_, gate_], rwkv_post_params,
        [[Rows(d_ycat, RWKV_WIDTH, 1)]], tm, [F32] * 5)
    d_r2, d_lw, d_k2, d_v2, d_kap, d_b = rwkv_scan_bwd(r_, lw_, k_, v_, kap_, b_, rwkv_ck, d_ys)
    (du_rkv, du_pg, du_pwa), rwkv_pg = rowwise_bwd(
        "rwkv_pre_bwd", rwkv_pre_fn, rwkv_pre_rows(), rwkv_pre_params,
        [[d_r1, d_r2], [d_lw], [d_k1, d_k2], [d_v1, d_v2], [d_kap], [d_b], [d_gate]], 128, [BF16] * 3)
    dmu_rkv, dmu_pg, dmu_pwa, dw0, dw2p, da0, da2p, dg2, dk_k, dk_a = rwkv_pg

    (d_yscan, du_z), (dssd_norm_g,) = rowwise_bwd("ssd_post_bwd", ssd_post_fn, [y_scan, u_z], [ssd_norm_g],
                                                  [[Rows(d_ycat, SSD_WIDTH, 0)]], tm, [F32, BF16])
    dxs, dbm, dcm, ddt2, da_log_p, dd_p = ssd_scan_bwd(xs, bm, cm, dt, a_log_p, d_p, ssd_ck, d_yscan)
    (du_xbc, du_dt), (dconv_w, dconv_b, ddt_bias_p) = rowwise_bwd(
        "ssd_pre_bwd", ssd_pre_fn, ssd_pre_rows(), ssd_pre_params,
        [[dxs], [dbm], [dcm], [ddt2[0], ddt2[1]]], tm, [BF16, BF16])
    du_sm = jnp.concatenate([du_pwa, du_dt], axis=1)

    dw_z = mm("d_w_z", h1, du_z, "tn", out_dtype=BF16)
    dw_xbc = mm("d_w_xbc", h1, du_xbc, "tn", out_dtype=BF16)
    dw_rkv = mm("d_w_rkv", h1, du_rkv, "tn", out_dtype=BF16)
    dw_pg = mm("d_w_pg", h1, du_pg, "tn", out_dtype=BF16)
    dw_sm = mm("d_w_small", h1, du_sm, "tn", out_dtype=BF16)
    dw_in_full = jnp.concatenate([dw_z, dw_xbc, dw_sm[:, 256:272], dw_rkv, dw_sm[:, 0:96], dw_sm[:, 128:224], dw_pg], axis=1)
    to_slabs = lambda g: jnp.transpose(g.reshape(g.shape[0], N_DEV, -1), (1, 0, 2))
    grads["w_in"] = to_slabs(dw_in_full)
    grads["ssd_conv_w"] = to_slabs(dconv_w)
    grads["rwkv_w2"] = to_slabs(dw2p[:96])
    grads["rwkv_a2"] = to_slabs(da2p[:96])
    grads["rwkv_g2"] = to_slabs(dg2)
    tail = ("w_in", "ssd_conv_w", "rwkv_w2", "rwkv_a2", "rwkv_g2")
    sc_tail, tok = exchange_start("scatter_tail_start", [grads[n] for n in tail], scatter=True)
    dh1 = mm("d_h1_z", du_z, w_z, "nt", dep=tok)
    dh1 = mm("d_h1_xbc", du_xbc, w_xbc, "nt", res=dh1)
    dh1 = mm("d_h1_rkv", du_rkv, w_rkv, "nt", res=dh1)
    dh1 = mm("d_h1_pg", du_pg, w_pg, "nt", res=dh1)
    dh1 = mm("d_h1_small", du_sm, w_sm, "nt", res=dh1)
    (dx,), (dg_mix,) = rowwise_bwd("norm_mix_bwd", rmsnorm_fn, [xt], [norm_mix_g], [[dh1]], tm, [F32], row_add=[dx1])

    dmu =jnp.concatenate([dmu_rkv, dmu_pwa[:, 0:96], dmu_pwa[:, 128:224], dmu_pg], axis=1)
    small_grads = {
        "norm_mix_g": dg_mix, "ssd_conv_b": dconv_b, "ssd_dt_bias": ddt_bias_p[:, :16], "ssd_a_log": da_log_p[:, :16],
        "ssd_d": dd_p[:, :16], "ssd_norm_g": dssd_norm_g, "rwkv_mu": dmu, "rwkv_w0": dw0, "rwkv_a0": da0,
        "rwkv_k_k": dk_k, "rwkv_k_a": dk_a, "rwkv_r_k": dr_k, "rwkv_ln_w": dln_w, "rwkv_ln_b": dln_b,
        "norm_x_g": dg_x, "norm_mem_g": dg_mem, "norm_ffn_g": dg_ffn, "final_norm_g": dg_final}

    gather_small, tok = exchange_start("gather_small_start", [_pack_small(small_grads)], scatter=False)
    received = {}
    for names, handle in ((("ffn_w2",), sc_w2), (("ffn_w1",), sc_w1), (("xattn_wo",), sc_wo), (qkv, sc_qkv),
                          (("w_out",), sc_wout)):
        received.update(zip(names, exchange_wait("scatter_" + names[0] + "_wait", handle, after=tok)))

    out_g, out_d, out_m, out_v = {}, {}, {}, {}

    def run_adamw(n, dep):
        shape = wts[n].shape
        two_d = lambda a: a.reshape(-1, shape[-1])
        res = adamw("adamw_" + n, received[n].reshape(N_DEV, -1, shape[-1]), two_d(wts[n]), two_d(mom_m[n]),
                    two_d(mom_v[n]), dep=dep)
        out_g[n], out_d[n], out_m[n], out_v[n] = (r.reshape(shape) for r in res)
        return res[0]

    last = None
    for n in ("ffn_w2", "ffn_w1", "xattn_wo") + qkv + ("w_out",):
        last = run_adamw(n, last)
    received.update(zip(tail, exchange_wait("scatter_tail_wait", sc_tail, after=last)))
    for n in tail:
        last = run_adamw(n, last)
    (small_all,) = exchange_wait("gather_small_wait", gather_small, after=last)
    res = adamw("adamw_small", small_all, _pack_small(wts), _pack_small(mom_m), _pack_small(mom_v))
    shapes = {n: wts[n].shape for n in _SMALL}
    for dst, packed in zip((out_g, out_d, out_m, out_v), res):
        dst.update(_unpack_small(packed, shapes))

    loss = lax.psum(loss_blk[0, 0], ("x", "y", "c"))
    return (loss, dx[None], *[out_g[n] for n in _WEIGHTS], *[out_d[n] for n in _WEIGHTS],
            *[out_m[n] for n in _WEIGHTS], *[out_v[n] for n in _WEIGHTS])
```

```python
import functools

import jax
import jax.numpy as jnp
from jax import lax
from jax.experimental import pallas as pl
from jax.experimental.pallas import tpu as pltpu

F32 = jnp.float32
BF16 = jnp.bfloat16
HIGHEST = lax.Precision.HIGHEST

N_DEV = 8
D_MODEL = 2048
NORM_EPS = 1e-6
SSD_WIDTH = 1024
SSD_CONV_DIM = 1536
SSD_HEADS = 16
SSD_HEAD_DIM = 64
SSD_STATE = 128
SSD_CHUNK = 128
SSD_HEADS_PER_GROUP = 8
RWKV_WIDTH = 1024
RWKV_HEADS = 16
RWKV_HEAD_DIM = 64
RWKV_LN_EPS = 64e-5
RWKV_CHUNK = 64
RWKV_HEADS_PER_STEP = 16
XATTN_HEADS = 4
XATTN_HEAD_DIM = 512
D_FF = 8192
LANES = 128
SUBLANES = 8
VMEM_LIMIT = 56 * 1024 * 1024

ADAM_LR = 0.001
ADAM_B1 = 0.9
ADAM_B2 = 0.999
ADAM_EPS = 1e-08
ADAM_WD = 0.01
ADAM_STEP = 10

_DN = {"nn": ((1,), (0,)), "nt": ((1,), (1,)), "tn": ((0,), (0,))}


def _dg(a, b, mode, precision=None):
    (ca,), (cb,) = _DN[mode]
    dn = (((ca + 1,), (cb + 1,)), ((0,), (0,))) if a.ndim == 3 else (((ca,), (cb,)), ((), ()))
    return lax.dot_general(a, b, dn, precision=precision, preferred_element_type=F32)


@functools.partial(jax.custom_vjp, nondiff_argnums=(2,))
def bdot(a, b, mode):
    return _dg(a.astype(BF16), b.astype(BF16), mode)


def _bdot_fwd(a, b, mode):
    return bdot(a, b, mode), (a, b)


def _bdot_bwd(mode, res, g):
    a, b = res
    ab, bb, gb = a.astype(BF16), b.astype(BF16), g.astype(BF16)
    if mode == "nn":
        da, db = _dg(gb, bb, "nt"), _dg(ab, gb, "tn")
    elif mode == "nt":
        da, db = _dg(gb, bb, "nn"), _dg(gb, ab, "tn")
    else:
        da, db = _dg(bb, gb, "nt"), _dg(ab, gb, "nn")
    return da.astype(a.dtype), db.astype(b.dtype)


bdot.defvjp(_bdot_fwd, _bdot_bwd)


def fdot(a, b, mode):
    return _dg(a, b, mode, precision=HIGHEST)


def _split3(x):
    hi = x.astype(BF16)
    r1 = x - hi.astype(F32)
    mid = r1.astype(BF16)
    lo = (r1 - mid.astype(F32)).astype(BF16)
    return hi, mid, lo


def _dot01(x, m01):
    hi, mid, _ = _split3(x)
    return _dg(hi, m01, "nn") + _dg(mid, m01, "nn")


def _exact_dot_impl(a, b, mode, exact):
    if exact == "a":
        ae = a.astype(BF16)
        return sum(_dg(ae, part, mode) for part in _split3(b))
    be = b.astype(BF16)
    return sum(_dg(part, be, mode) for part in _split3(a))


@functools.partial(jax.custom_vjp, nondiff_argnums=(2, 3))
def exact_dot(a, b, mode, exact):
    return _exact_dot_impl(a, b, mode, exact)


def _exact_dot_fwd(a, b, mode, exact):
    return _exact_dot_impl(a, b, mode, exact), (a, b)


def _exact_dot_bwd(mode, exact, res, g):
    a, b = res
    if exact == "a":
        db = {"nn": lambda: _exact_dot_impl(a, g, "tn", "a"), "nt": lambda: _exact_dot_impl(g, a, "tn", "b"),
              "tn": lambda: _exact_dot_impl(a, g, "nn", "a")}[mode]()
        return jnp.zeros_like(a), db
    da = {"nn": lambda: _exact_dot_impl(g, b, "nt", "b"), "nt": lambda: _exact_dot_impl(g, b, "nn", "b"),
          "tn": lambda: _exact_dot_impl(b, g, "nt", "a")}[mode]()
    return da, jnp.zeros_like(b)


exact_dot.defvjp(_exact_dot_fwd, _exact_dot_bwd)


def _head_indicator(width, heads, transpose):
    hd = width // heads
    shape = (LANES, width) if transpose else (width, LANES)
    lane = lax.broadcasted_iota(jnp.int32, shape, 1 if not transpose else 0)
    pos = lax.broadcasted_iota(jnp.int32, shape, 0 if not transpose else 1)
    return ((pos >= lane * hd) & (pos < lane * hd + hd)).astype(BF16)


@jax.custom_vjp
def head_sum(x):
    w = x.shape[-1]
    e = _head_indicator(w, w // RWKV_HEAD_DIM, False)
    et = _head_indicator(w, w // RWKV_HEAD_DIM, True)
    return _dot01(_dot01(x, e), et)


head_sum.defvjp(lambda x: (head_sum(x), None), lambda _, g: (head_sum(g),))


def rmsnorm_fn(x, g):
    y = x * lax.rsqrt(jnp.mean(x * x, axis=-1, keepdims=True) + NORM_EPS)
    return ((y * g).astype(BF16),)


def cast_fn(x):
    return (x.astype(BF16),)


def ssd_pre_fn(xbc, xbc1, xbc2, xbc3, dt_raw, conv_w, conv_b, dt_bias):
    c = conv_w[3:4] * xbc + conv_w[2:3] * xbc1 + conv_w[1:2] * xbc2 + conv_w[0:1] * xbc3 + conv_b
    act = c * jax.nn.sigmoid(c)
    dt = jax.nn.softplus(dt_raw + dt_bias)
    return act[:, :SSD_WIDTH], act[:, SSD_WIDTH:SSD_WIDTH + 256], act[:, SSD_WIDTH + 256:], dt


def ssd_post_fn(yscan, z, norm_g):
    y = yscan * (z * jax.nn.sigmoid(z))
    half = SSD_WIDTH // 2
    parts = []
    for g in range(2):
        yg = y[:, g * half:(g + 1) * half]
        parts.append(yg * lax.rsqrt(jnp.mean(yg * yg, axis=-1, keepdims=True) + NORM_EPS))
    return ((jnp.concatenate(parts, axis=-1) * norm_g).astype(BF16),)


def rwkv_pre_fn(rkv, rkv_p, pg, pg_p, pwa, pwa_p, mu_rkv, mu_pg, mu_pwa, w0, w2p, a0, a2p, g2, k_k, k_a):
    w = RWKV_WIDTH
    rkv = rkv + (rkv_p - rkv) * mu_rkv
    pg = pg + (pg_p - pg) * mu_pg
    pwa = pwa + (pwa_p - pwa) * mu_pwa
    r, k, v = rkv[:, :w], rkv[:, w:2 * w], rkv[:, 2 * w:]
    pw, pa = pwa[:, :LANES], pwa[:, LANES:]
    w_log = -jax.nn.softplus(-(w0 + bdot(jnp.tanh(pw), w2p, "nn"))) - 0.5
    lw = -jnp.exp(w_log)
    iclr = jax.nn.sigmoid(a0 + bdot(pa, a2p, "nn"))
    gate = bdot(jax.nn.sigmoid(pg), g2, "nn")
    kk = k * k_k
    kap = kk / jnp.maximum(jnp.sqrt(head_sum(kk * kk)), 1e-12)
    k_mod = k * (1.0 + (iclr - 1.0) * k_a)
    return r, lw, k_mod, v, kap, kap * iclr, gate


def rwkv_post_fn(ys, r, k_mod, v, gate, ln_w, ln_b, r_k):
    inv_n = 1.0 / RWKV_HEAD_DIM
    mean = head_sum(ys) * inv_n
    yc = ys - mean
    var = head_sum(yc * yc) * inv_n
    yn = yc * lax.rsqrt(var + RWKV_LN_EPS) * ln_w + ln_b
    bonus = head_sum(r * k_mod * r_k) * v
    return (((yn + bonus) * gate).astype(BF16),)


def attn_fn(q, kx, vx):
    outs = []
    for h in range(XATTN_HEADS):
        sl = slice(h * XATTN_HEAD_DIM, (h + 1) * XATTN_HEAD_DIM)
        s = bdot(q[:, sl], kx[:, sl], "nt") * (XATTN_HEAD_DIM ** -0.5)
        s = s - jnp.max(s, axis=-1, keepdims=True)
        p = jnp.exp(s)
        p = p / jnp.sum(p, axis=-1, keepdims=True)
        outs.append(bdot(p, vx[:, sl], "nn"))
    return (jnp.concatenate(outs, axis=-1).astype(BF16),)


def loss_fn(x, tgt, g):
    y = x * lax.rsqrt(jnp.mean(x * x, axis=-1, keepdims=True) + NORM_EPS) * g
    err = jnp.square(y - tgt)
    return 0.5 * jnp.sum(jnp.mean(err, axis=-1, keepdims=True), axis=0, keepdims=True)


def _tri_masks(n):
    row = lax.broadcasted_iota(jnp.int32, (n, n), 0)
    col = lax.broadcasted_iota(jnp.int32, (n, n), 1)
    return col <= row, col < row, row == col


def rwkv_chunk_fn(st0, r, lw, k, v, kap, b):
    h, c = r.shape[0], r.shape[1]
    incl, strict, diag = _tri_masks(c)
    cum = exact_dot(jnp.broadcast_to(incl.astype(F32), (h, c, c)), lw, "nn", "a")
    g_in = jnp.exp(cum)
    g_prev = jnp.exp(cum - lw)
    g_inv = jnp.exp(-cum)
    g_end = jnp.exp(cum[:, c - 1:c, :] - cum)
    kap_t, k_t, b_t, r_t = kap * g_prev, k * g_inv, b * g_inv, r * g_in
    a_ub = jnp.where(strict, bdot(kap_t, b_t, "nt"), 0.0)
    a_vk = jnp.where(strict, bdot(kap_t, k_t, "nt"), 0.0)
    rhs = -(bdot(kap_t, st0, "nn") + bdot(a_vk, v, "nn"))
    eye = diag.astype(F32)
    m = -a_ub
    inv = eye + m
    n = 1
    while n * 2 < c:
        m = bdot(m, m, "nn")
        inv = bdot(inv, eye + m, "nn")
        n *= 2
    u = bdot(inv, rhs, "nn")
    y = (bdot(r_t, st0, "nn")
         + bdot(jnp.where(incl, bdot(r_t, k_t, "nt"), 0.0), v, "nn")
         + bdot(jnp.where(incl, bdot(r_t, b_t, "nt"), 0.0), u, "nn"))
    g_full = jnp.exp(exact_dot(lw, jnp.ones((h, c, st0.shape[2]), F32), "tn", "b"))
    st1 = g_full * st0 + bdot(k * g_end, v, "tn") + bdot(b * g_end, u, "tn")
    return y, st1


def ssd_chunk_fn(group, h0, xs, bm, cm, dt, a_log, d_skip):
    q = xs.shape[0]
    causal, _, _ = _tri_masks(q)
    a_row = -jnp.exp(a_log)
    cs_all = exact_dot(causal.astype(F32), dt * a_row, "nn", "a")
    cs_t = cs_all.T
    cb = bdot(cm, bm, "nt")
    ys, hs = [], []
    p = SSD_HEAD_DIM
    for e in range(SSD_HEADS_PER_GROUP):
        hl = group * SSD_HEADS_PER_GROUP + e
        cs = cs_all[:, hl:hl + 1]
        seg = cs - cs_t[hl:hl + 1, :]
        lmat = jnp.where(causal, jnp.exp(jnp.where(causal, seg, 0.0)), 0.0)
        x_h = xs[:, e * p:(e + 1) * p]
        xdt = x_h * dt[:, hl:hl + 1]
        h0_e = h0[e * p:(e + 1) * p, :]
        cl = cs[q - 1:q, :]
        y = bdot(cb * lmat, xdt, "nn") + bdot(cm, h0_e, "nt") * jnp.exp(cs) + x_h * d_skip[:, hl:hl + 1]
        hs.append(h0_e * jnp.exp(cl) + bdot(xdt * jnp.exp(cl - cs), bm, "tn"))
        ys.append(y)
    return jnp.concatenate(ys, axis=-1), jnp.concatenate(hs, axis=0)


class Rows:
    def __init__(self, arr, w=None, cb=0, shifts=()):
        self.arr, self.w, self.cb, self.shifts = arr, (arr.shape[1] if w is None else w), cb, tuple(shifts)


def _as_rows(x):
    return x if isinstance(x, Rows) else Rows(x)


def _shift_down(x, halo, k):
    rolled = pltpu.roll(x, k, 0)
    first = rolled[0:SUBLANES]
    rid = lax.broadcasted_iota(jnp.int32, first.shape, 0)
    patched = jnp.where(rid < k, pltpu.roll(halo, k, 0), first)
    return jnp.concatenate([patched, rolled[SUBLANES:]], axis=0)


def _shift_up(g, carry, k):
    tm = g.shape[0]
    rolled = pltpu.roll(g, tm - k, 0)
    last = rolled[tm - SUBLANES:]
    rid = lax.broadcasted_iota(jnp.int32, last.shape, 0)
    patched = jnp.where(rid >= SUBLANES - k, pltpu.roll(carry, SUBLANES - k, 0), last)
    return jnp.concatenate([rolled[:tm - SUBLANES], patched], axis=0)


def _params():
    return pltpu.CompilerParams(vmem_limit_bytes=VMEM_LIMIT)


def _load_rows(refs, pos, rins, first_block):
    vals = []
    for r in rins:
        x = refs[pos][...].astype(F32) if refs[pos].dtype != F32 else refs[pos][...]
        pos += 1
        vals.append(x)
        if r.shifts:
            halo = refs[pos][...]
            pos += 1
            halo = jnp.where(first_block, jnp.zeros_like(halo), halo)
            for k in r.shifts:
                vals.append(_shift_down(x, halo, k))
    return vals, pos


def _row_specs(rins, tm, blk):
    specs, args = [], []
    for r in rins:
        specs.append(pl.BlockSpec((tm, r.w), lambda i, cb=r.cb: (blk(i), cb)))
        args.append(r.arr)
        if r.shifts:
            per = tm // SUBLANES
            specs.append(pl.BlockSpec((SUBLANES, r.w), lambda i, cb=r.cb: (jnp.maximum(blk(i) * per - 1, 0), cb)))
            args.append(r.arr)
    return specs, args


def rowwise_fwd(name, fn, rins, params, outs, tm, deps=()):
    rins = [_as_rows(r) for r in rins]
    t = rins[0].arr.shape[0]
    tm = min(tm, t)
    nb = t // tm
    specs, args = _row_specs(rins, tm, lambda i: i)
    for p in params:
        specs.append(pl.BlockSpec(p.shape, lambda i: (0, 0)))
        args.append(p)
    for dep in deps:
        specs.append(pl.BlockSpec(memory_space=pl.ANY))
        args.append(dep)
    n_in = len(args)

    def body(*refs):
        vals, pos = _load_rows(refs, 0, rins, pl.program_id(0) == 0)
        pv = [refs[pos + j][...] for j in range(len(params))]
        res = fn(*vals, *pv)
        for o_ref, o in zip(refs[n_in:], res):
            o_ref[...] = o.astype(o_ref.dtype)

    return pl.pallas_call(
        body, name=name, grid=(nb,), in_specs=specs,
        out_specs=[pl.BlockSpec((tm, w), lambda i: (i, 0)) for w, _ in outs],
        out_shape=[jax.ShapeDtypeStruct((t, w), dt) for w, dt in outs],
        compiler_params=_params(),
    )(*args)


def rowwise_bwd(name, fn, rins, params, cts, tm, grad_dtypes, row_add=None):
    rins = [_as_rows(r) for r in rins]
    cts = [[_as_rows(c) for c in lst] for lst in cts]
    row_add = [_as_rows(a) for a in (row_add or [])]
    t = rins[0].arr.shape[0]
    tm = min(tm, t)
    nb = t // tm
    rev = lambda i: nb - 1 - i
    specs, args = _row_specs(rins, tm, rev)
    for p in params:
        specs.append(pl.BlockSpec(p.shape, lambda i: (0, 0)))
        args.append(p)
    flat_cts = [c for lst in cts for c in lst] + row_add
    for c in flat_cts:
        specs.append(pl.BlockSpec((tm, c.w), lambda i, cb=c.cb: (rev(i), cb)))
        args.append(c.arr)
    n_in = len(args)
    want = [i for i, d in enumerate(grad_dtypes) if d is not None]
    out_specs = [pl.BlockSpec((tm, rins[i].w), lambda i_: (rev(i_), 0)) for i in want]
    out_shape = [jax.ShapeDtypeStruct((t, rins[i].w), grad_dtypes[i]) for i in want]
    out_specs += [pl.BlockSpec(p.shape, lambda i: (0, 0)) for p in params]
    out_shape += [jax.ShapeDtypeStruct(p.shape, F32) for p in params]
    n_out = len(out_shape)
    scratch = [pltpu.VMEM((SUBLANES, r.w), F32) for r in rins for _ in r.shifts]

    def body(*refs):
        i = pl.program_id(0)
        vals, pos = _load_rows(refs, 0, rins, rev(i) == 0)
        pv = [refs[pos + j][...] for j in range(len(params))]
        pos += len(params)
        outs, vjp = jax.vjp(fn, *vals, *pv)
        ct_vals = []
        for o, lst in zip(outs, cts):
            acc = None
            for _ in lst:
                cv = refs[pos][...].astype(F32)
                pos += 1
                acc = cv if acc is None else acc + cv
            ct_vals.append(acc.astype(o.dtype))
        adds = [refs[pos + j][...].astype(F32) for j in range(len(row_add))]
        grads = vjp(tuple(ct_vals))
        out_refs = refs[n_in:n_in + n_out]
        carry_refs = refs[n_in + n_out:]

        @pl.when(i == 0)
        def _():
            for cr in carry_refs:
                cr[...] = jnp.zeros_like(cr)
            for pr in out_refs[len(want):]:
                pr[...] = jnp.zeros_like(pr)

        gi, ci, oi = 0, 0, 0
        for idx, r in enumerate(rins):
            d = grads[gi]
            gi += 1
            for k in r.shifts:
                dk = grads[gi]
                gi += 1
                d = d + _shift_up(dk, carry_refs[ci][...], k)
                carry_refs[ci][...] = dk[0:SUBLANES]
                ci += 1
            if idx == 0:
                for a in adds:
                    d = d + a
            if grad_dtypes[idx] is not None:
                out_refs[oi][...] = d.astype(out_refs[oi].dtype)
                oi += 1
        for pr, gp in zip(out_refs[len(want):], grads[gi:]):
            pr[...] += gp

    res = pl.pallas_call(
        body, name=name, grid=(nb,), in_specs=specs, out_specs=out_specs, out_shape=out_shape,
        scratch_shapes=scratch, compiler_params=_params(),
    )(*args)
    return res[:len(want)], res[len(want):]


def _pick(n, pref):
    for c in pref:
        if n % c == 0:
            return c
    return n


MM_VMEM_BUDGET = 40 * 1024 * 1024
MM_PEAK_FLOPS = 0.9e15
MM_HBM_BYTES_PER_S = 3.0e12
MM_STEP_SECONDS = 0.35e-6


def _mm_tiles(m, n, k, size_a, size_b, size_out, size_res, single_k):
    best = None
    for tk in sorted({c for c in (k, 2048, 1024, 512, 256, 128) if c <= 2048 and k % c == 0}, reverse=True):
        for tm in (1024, 512, 256, 128):
            if m % tm:
                continue
            for tn in (1024, 768, 512, 384, 256, 128):
                if n % tn:
                    continue
                nk = k // tk
                vmem = 2 * (tm * tk * size_a + tk * tn * size_b + tm * tn * (size_out + size_res))
                vmem += tm * tn * 4 * (2 if nk > 1 or not single_k else 1)
                vmem += (tm * tk * 2 if size_a > 2 else 0) + (tk * tn * 2 if size_b > 2 else 0)
                if vmem > MM_VMEM_BUDGET:
                    continue
                steps = (m // tm) * (n // tn) * nk
                a_reads = 1 if (nk == 1 and single_k) else n // tn
                traffic = m * k * size_a * a_reads + k * n * size_b * (m // tm) + m * n * (size_out + size_res)
                cost = max(2.0 * m * n * k / MM_PEAK_FLOPS, traffic / MM_HBM_BYTES_PER_S) + steps * MM_STEP_SECONDS
                if best is None or cost < best[0]:
                    best = (cost, tm, tn, tk)
    return best[1:]


def mm(name, a, b, mode, out_dtype=F32, res=None, b_slabs=None, out_slabs=None, dep=None, epi=None, extras=(),
       out_dtypes=None):
    if mode == "tn":
        k_dim, m_dim = a.shape
    else:
        m_dim, k_dim = a.shape
    if b_slabs:
        n_dim = b.shape[0] * b.shape[2] if mode == "nn" else b.shape[1]
    else:
        n_dim = b.shape[0] if mode == "nt" else b.shape[1]
    n_slabs = out_slabs or (b_slabs if (b_slabs and mode == "nn") else 1)
    k_slabs = b_slabs if (b_slabs and mode == "nt") else 1
    if epi is None:
        out_dtypes = [out_dtype]
        if res is None:
            epi = lambda acc: (acc,)
        else:
            extras, epi = [res], lambda acc, r: (acc + r,)
    tm, tn, tk = _mm_tiles(m_dim, n_dim // n_slabs, k_dim // k_slabs, a.dtype.itemsize, b.dtype.itemsize,
                           sum(jnp.dtype(dt).itemsize for dt in out_dtypes), sum(e.dtype.itemsize for e in extras),
                           single_k=(k_slabs == 1))
    nji = n_dim // n_slabs // tn
    nki = k_dim // k_slabs // tk
    nblk = lambda js, j: js * nji + j
    kblk = lambda ks, k: ks * nki + k
    if mode == "tn":
        a_spec = pl.BlockSpec((tk, tm), lambda i, js, j, ks, k: (kblk(ks, k), i))
    else:
        a_spec = pl.BlockSpec((tm, tk), lambda i, js, j, ks, k: (i, kblk(ks, k)))
    if b_slabs and mode == "nn":
        b_spec = pl.BlockSpec((None, tk, tn), lambda i, js, j, ks, k: (js, k, j))
    elif b_slabs and mode == "nt":
        b_spec = pl.BlockSpec((None, tn, tk), lambda i, js, j, ks, k: (ks, nblk(js, j), k))
    elif mode == "nt":
        b_spec = pl.BlockSpec((tn, tk), lambda i, js, j, ks, k: (nblk(js, j), kblk(ks, k)))
    else:
        b_spec = pl.BlockSpec((tk, tn), lambda i, js, j, ks, k: (kblk(ks, k), nblk(js, j)))
    specs, args = [a_spec, b_spec], [a, b]
    for e in extras:
        specs.append(pl.BlockSpec((tm, tn), lambda i, js, j, ks, k: (i, nblk(js, j))))
        args.append(e)
    if dep is not None:
        specs.append(pl.BlockSpec(memory_space=pl.ANY))
        args.append(dep)
    if out_slabs:
        o_specs = [pl.BlockSpec((None, tm, tn), lambda i, js, j, ks, k: (js, i, j))]
        o_shapes = [jax.ShapeDtypeStruct((out_slabs, m_dim, n_dim // out_slabs), out_dtypes[0])]
    else:
        o_specs = [pl.BlockSpec((tm, tn), lambda i, js, j, ks, k: (i, nblk(js, j))) for _ in out_dtypes]
        o_shapes = [jax.ShapeDtypeStruct((m_dim, n_dim), dt) for dt in out_dtypes]

    one_k_step = k_slabs * nki == 1
    n_in, n_out = len(args), len(out_dtypes)

    def body(*refs):
        a_ref, b_ref = refs[0], refs[1]
        part = _dg(a_ref[...].astype(BF16), b_ref[...].astype(BF16), mode)

        def finish(acc):
            outs = epi(acc, *[refs[2 + j][...].astype(F32) for j in range(len(extras))])
            for o_ref, o in zip(refs[n_in:n_in + n_out], outs):
                o_ref[...] = o.astype(o_ref.dtype)

        if one_k_step:
            finish(part)
            return
        acc_ref = refs[n_in + n_out]
        ks, kk = pl.program_id(3), pl.program_id(4)

        @pl.when((ks == 0) & (kk == 0))
        def _():
            acc_ref[...] = part

        @pl.when((ks > 0) | (kk > 0))
        def _():
            acc_ref[...] += part

        pl.when((ks == k_slabs - 1) & (kk == nki - 1))(lambda: finish(acc_ref[...]))

    grid = (m_dim // tm, n_slabs, nji, k_slabs, nki)
    scratch = [] if one_k_step else [pltpu.VMEM((tm, tn), F32)]
    out = pl.pallas_call(
        body, name=name, grid=grid, in_specs=specs, out_specs=o_specs, out_shape=o_shapes, scratch_shapes=scratch,
        compiler_params=pltpu.CompilerParams(
            dimension_semantics=("parallel", "parallel", "parallel", "arbitrary", "arbitrary"),
            vmem_limit_bytes=VMEM_LIMIT),
    )(*args)
    return out[0] if n_out == 1 else out


def _stack_lanes(x, n):
    w = x.shape[1] // n
    return jnp.stack([x[:, i * w:(i + 1) * w] for i in range(n)])


def _stack_rows(x, n):
    w = x.shape[0] // n
    return jnp.stack([x[i * w:(i + 1) * w, :] for i in range(n)])


def rwkv_scan_fwd(r, lw, k, v, kap, b):
    t = r.shape[0]
    c, hps, hd = min(RWKV_CHUNK, t), RWKV_HEADS_PER_STEP, RWKV_HEAD_DIM
    nc, ng, wl = t // c, RWKV_HEADS // hps, hps * hd
    spec = pl.BlockSpec((c, wl), lambda g, ci: (ci, g))

    def body(r_ref, lw_ref, k_ref, v_ref, kap_ref, b_ref, y_ref, ck_ref, st_ref):
        @pl.when(pl.program_id(1) == 0)
        def _():
            st_ref[...] = jnp.zeros_like(st_ref)

        st = st_ref[...]
        ck_ref[...] = st
        ins = [x[...] for x in (r_ref, lw_ref, k_ref, v_ref, kap_ref, b_ref)]
        y, st1 = rwkv_chunk_fn(_stack_rows(st, hps), *[_stack_lanes(x, hps) for x in ins])
        y_ref[...] = jnp.concatenate([y[h] for h in range(hps)], axis=-1)
        st_ref[...] = jnp.concatenate([st1[h] for h in range(hps)], axis=0)

    return pl.pallas_call(
        body, name="rwkv_scan_fwd", grid=(ng, nc), in_specs=[spec] * 6,
        out_specs=[spec, pl.BlockSpec((None, wl, hd), lambda g, ci: (ci, g, 0))],
        out_shape=[jax.ShapeDtypeStruct((t, RWKV_WIDTH), F32), jax.ShapeDtypeStruct((nc, RWKV_WIDTH, hd), F32)],
        scratch_shapes=[pltpu.VMEM((wl, hd), F32)], compiler_params=_params(),
    )(r, lw, k, v, kap, b)


def rwkv_scan_bwd(r, lw, k, v, kap, b, ck, dy):
    t = r.shape[0]
    c, hps, hd = min(RWKV_CHUNK, t), RWKV_HEADS_PER_STEP, RWKV_HEAD_DIM
    nc, ng, wl = t // c, RWKV_HEADS // hps, hps * hd
    spec = pl.BlockSpec((c, wl), lambda g, ci: (nc - 1 - ci, g))

    def body(r_ref, lw_ref, k_ref, v_ref, kap_ref, b_ref, ck_ref, dy_ref, *rest):
        out_refs, dst_ref = rest[:6], rest[6]

        @pl.when(pl.program_id(1) == 0)
        def _():
            dst_ref[...] = jnp.zeros_like(dst_ref)

        ins = [x[...] for x in (r_ref, lw_ref, k_ref, v_ref, kap_ref, b_ref)]
        dyv, ck, dst = dy_ref[...], ck_ref[...], dst_ref[...]
        _, vjp = jax.vjp(rwkv_chunk_fn, _stack_rows(ck, hps), *[_stack_lanes(x, hps) for x in ins])
        grads = vjp((_stack_lanes(dyv, hps), _stack_rows(dst, hps)))
        dst_ref[...] = jnp.concatenate([grads[0][h] for h in range(hps)], axis=0)
        for j in range(6):
            out_refs[j][...] = jnp.concatenate([grads[1 + j][h] for h in range(hps)], axis=-1)

    return pl.pallas_call(
        body, name="rwkv_scan_bwd", grid=(ng, nc),
        in_specs=[spec] * 6 + [pl.BlockSpec((None, wl, hd), lambda g, ci: (nc - 1 - ci, g, 0)), spec],
        out_specs=[spec] * 6, out_shape=[jax.ShapeDtypeStruct((t, RWKV_WIDTH), F32)] * 6,
        scratch_shapes=[pltpu.VMEM((wl, hd), F32)], compiler_params=_params(),
    )(r, lw, k, v, kap, b, ck, dy)


def _ssd_specs(q, blk):
    gw = SSD_WIDTH // 2
    return [pl.BlockSpec((q, gw), lambda g, ci: (blk(ci), g)),
            pl.BlockSpec((q, SSD_STATE), lambda g, ci: (blk(ci), g)),
            pl.BlockSpec((q, SSD_STATE), lambda g, ci: (blk(ci), g)),
            pl.BlockSpec((q, LANES), lambda g, ci: (blk(ci), 0)),
            pl.BlockSpec((1, LANES), lambda g, ci: (0, 0)),
            pl.BlockSpec((1, LANES), lambda g, ci: (0, 0))]


def ssd_scan_fwd(xs, bm, cm, dt, a_log, d_skip):
    t = xs.shape[0]
    q = min(SSD_CHUNK, t)
    nc, gw = t // q, SSD_WIDTH // 2

    def body(xs_ref, bm_ref, cm_ref, dt_ref, al_ref, d_ref, y_ref, ck_ref, h_ref):
        @pl.when(pl.program_id(1) == 0)
        def _():
            h_ref[...] = jnp.zeros_like(h_ref)

        ck_ref[...] = h_ref[...]
        args = (h_ref[...], xs_ref[...], bm_ref[...], cm_ref[...], dt_ref[...], al_ref[...], d_ref[...])
        g = pl.program_id(0)

        @pl.when(g == 0)
        def _():
            y, h1 = ssd_chunk_fn(0, *args)
            y_ref[...] = y
            h_ref[...] = h1

        @pl.when(g == 1)
        def _():
            y, h1 = ssd_chunk_fn(1, *args)
            y_ref[...] = y
            h_ref[...] = h1

    return pl.pallas_call(
        body, name="ssd_scan_fwd", grid=(2, nc), in_specs=_ssd_specs(q, lambda ci: ci),
        out_specs=[pl.BlockSpec((q, gw), lambda g, ci: (ci, g)),
                   pl.BlockSpec((None, gw, SSD_STATE), lambda g, ci: (ci, g, 0))],
        out_shape=[jax.ShapeDtypeStruct((t, SSD_WIDTH), F32), jax.ShapeDtypeStruct((nc, SSD_WIDTH, SSD_STATE), F32)],
        scratch_shapes=[pltpu.VMEM((gw, SSD_STATE), F32)], compiler_params=_params(),
    )(xs, bm, cm, dt, a_log, d_skip)


def ssd_scan_bwd(xs, bm, cm, dt, a_log, d_skip, ck, dy):
    t = xs.shape[0]
    q = min(SSD_CHUNK, t)
    nc, gw = t // q, SSD_WIDTH // 2
    rev = lambda ci: nc - 1 - ci

    def body(xs_ref, bm_ref, cm_ref, dt_ref, al_ref, d_ref, ck_ref, dy_ref,
             dxs_ref, dbm_ref, dcm_ref, ddt_ref, dal_ref, dd_ref, dh_ref):
        g, ci = pl.program_id(0), pl.program_id(1)

        @pl.when(ci == 0)
        def _():
            dh_ref[...] = jnp.zeros_like(dh_ref)

        @pl.when((ci == 0) & (g == 0))
        def _():
            dal_ref[...] = jnp.zeros_like(dal_ref)
            dd_ref[...] = jnp.zeros_like(dd_ref)

        args = (ck_ref[...], xs_ref[...], bm_ref[...], cm_ref[...], dt_ref[...], al_ref[...], d_ref[...])

        def run(group):
            _, vjp = jax.vjp(functools.partial(ssd_chunk_fn, group), *args)
            dh0, dxs, dbm, dcm, ddt, dal, dd = vjp((dy_ref[...], dh_ref[...]))
            dh_ref[...] = dh0
            dxs_ref[...] = dxs
            dbm_ref[...] = dbm
            dcm_ref[...] = dcm
            ddt_ref[...] = ddt
            dal_ref[...] += dal
            dd_ref[...] += dd

        pl.when(g == 0)(lambda: run(0))
        pl.when(g == 1)(lambda: run(1))

    in_specs = _ssd_specs(q, rev) + [pl.BlockSpec((None, gw, SSD_STATE), lambda g, ci: (rev(ci), g, 0)),
                                     pl.BlockSpec((q, gw), lambda g, ci: (rev(ci), g))]
    return pl.pallas_call(
        body, name="ssd_scan_bwd", grid=(2, nc), in_specs=in_specs,
        out_specs=[pl.BlockSpec((q, gw), lambda g, ci: (rev(ci), g)),
                   pl.BlockSpec((q, SSD_STATE), lambda g, ci: (rev(ci), g)),
                   pl.BlockSpec((q, SSD_STATE), lambda g, ci: (rev(ci), g)),
                   pl.BlockSpec((None, q, LANES), lambda g, ci: (g, rev(ci), 0)),
                   pl.BlockSpec((1, LANES), lambda g, ci: (0, 0)),
                   pl.BlockSpec((1, LANES), lambda g, ci: (0, 0))],
        out_shape=[jax.ShapeDtypeStruct((t, SSD_WIDTH), F32), jax.ShapeDtypeStruct((t, 2 * SSD_STATE), F32),
                   jax.ShapeDtypeStruct((t, 2 * SSD_STATE), F32), jax.ShapeDtypeStruct((2, t, LANES), F32),
                   jax.ShapeDtypeStruct((1, LANES), F32), jax.ShapeDtypeStruct((1, LANES), F32)],
        scratch_shapes=[pltpu.VMEM((gw, SSD_STATE), F32)], compiler_params=_params(),
    )(xs, bm, cm, dt, a_log, d_skip, ck, dy)


def loss_and_grad(x, tgt, g, tm):
    t, d = x.shape
    tm = min(tm, t)
    nb = t // tm

    def body(x_ref, t_ref, g_ref, loss_ref, dx_ref, dg_ref):
        @pl.when(pl.program_id(0) == 0)
        def _():
            loss_ref[...] = jnp.zeros_like(loss_ref)
            dg_ref[...] = jnp.zeros_like(dg_ref)

        val, vjp = jax.vjp(loss_fn, x_ref[...], t_ref[...], g_ref[...])
        dx, _, dg = vjp(jnp.ones((1, 1), F32))
        loss_ref[...] += jnp.broadcast_to(val, loss_ref.shape)
        dx_ref[...] = dx
        dg_ref[...] += dg

    row = pl.BlockSpec((tm, d), lambda i: (i, 0))
    one = pl.BlockSpec((1, d), lambda i: (0, 0))
    return pl.pallas_call(
        body, name="loss_and_grad", grid=(nb,), in_specs=[row, row, one],
        out_specs=[pl.BlockSpec((SUBLANES, LANES), lambda i: (0, 0)), row, one],
        out_shape=[jax.ShapeDtypeStruct((SUBLANES, LANES), F32), jax.ShapeDtypeStruct((t, d), F32),
                   jax.ShapeDtypeStruct((1, d), F32)],
        compiler_params=_params(),
    )(x, tgt, g)


def adamw(name, recv, w, m, v, dep=None):
    rows, cols = w.shape
    recv_block_bytes = 4 * 1024 * 1024
    tm = _pick(rows, [c for c in (256, 128, 64, 32, 16, 8) if N_DEV * c * cols * 4 <= recv_block_bytes])
    c1 = 1.0 / (1.0 - ADAM_B1 ** ADAM_STEP)
    c2 = 1.0 / (1.0 - ADAM_B2 ** ADAM_STEP)

    n_dep = 0 if dep is None else 1

    def body(recv_ref, w_ref, m_ref, v_ref, *rest):
        g_ref, d_ref, nm_ref, nv_ref = rest[n_dep:]
        g = recv_ref[0].astype(F32)
        for p in range(1, N_DEV):
            g = g + recv_ref[p].astype(F32)
        nm = ADAM_B1 * m_ref[...] + (1.0 - ADAM_B1) * g
        nv = ADAM_B2 * v_ref[...] + (1.0 - ADAM_B2) * jnp.square(g)
        g_ref[...] = g
        nm_ref[...] = nm
        nv_ref[...] = nv
        d_ref[...] = -ADAM_LR * ((nm * c1) / (jnp.sqrt(nv * c2) + ADAM_EPS) + ADAM_WD * w_ref[...])

    blk = pl.BlockSpec((tm, cols), lambda i: (i, 0))
    return pl.pallas_call(
        body, name=name, grid=(rows // tm,),
        in_specs=[pl.BlockSpec((N_DEV, tm, cols), lambda i: (0, i, 0)), blk, blk, blk]
        + [pl.BlockSpec(memory_space=pl.ANY)] * n_dep,
        out_specs=[blk] * 4, out_shape=[jax.ShapeDtypeStruct((rows, cols), F32)] * 4,
        compiler_params=_params(),
    )(recv, w, m, v, *([] if dep is None else [dep]))


def _mesh_pos():
    return lax.axis_index("x"), lax.axis_index("y"), lax.axis_index("c")


def _peer(pos, mask):
    x, y, c = pos
    return (1 - x if mask & 4 else x, 1 - y if mask & 2 else y, 1 - c if mask & 1 else c)


def _linear(pos):
    return 4 * pos[0] + 2 * pos[1] + pos[2]


class Exchange:
    MASKS = {"gather": (1, 2, 3, 4, 5, 6, 7), "scatter": (1, 2, 3, 4, 5, 6, 7), "gather_chips": (1, 2, 4, 6),
             "forward": (2, 4, 6)}

    def __init__(self, xs, kind, lands=None):
        self.kind, self.masks = kind, self.MASKS[kind]
        self.xs = [] if kind == "forward" else list(xs)
        if kind == "forward":
            self.land_shape = [jax.ShapeDtypeStruct(l.shape, l.dtype) for l in lands]
        elif kind == "scatter":
            self.land_shape = [jax.ShapeDtypeStruct(x.shape, x.dtype) for x in xs]
        else:
            self.land_shape = [jax.ShapeDtypeStruct((N_DEV,) + x.shape, x.dtype) for x in xs]
        self.n = len(self.land_shape)
        copies = self.n * len(self.masks)
        self.sems = [pltpu.SemaphoreType.DMA((copies,)), pltpu.SemaphoreType.DMA((copies,)),
                     pltpu.SemaphoreType.DMA((self.n,))]

    def _copies(self, ins, outs, sems, landing):
        send_sems, recv_sems, local_sems = sems
        me = _mesh_pos()
        me_lin = _linear(me)
        local, remote = [], []
        for ti in range(self.n):
            if self.kind != "forward":
                src_mine = ins[ti].at[me_lin] if self.kind == "scatter" else ins[ti]
                local.append(pltpu.make_async_copy(src_mine, outs[ti].at[me_lin], local_sems.at[ti]))
            for j, mask in enumerate(self.masks):
                if self.kind == "forward":
                    peer = _peer(me, 1)
                    src = outs[ti].at[_linear(_peer(me, mask))]
                    dst = outs[ti].at[_linear(_peer(me, mask ^ 1 if landing else mask))]
                else:
                    peer = _peer(me, mask)
                    src = ins[ti].at[_linear(peer)] if self.kind == "scatter" else ins[ti]
                    dst = outs[ti].at[_linear(peer) if landing else me_lin]
                sem_index = ti * len(self.masks) + j
                remote.append(pltpu.make_async_remote_copy(
                    src_ref=src, dst_ref=dst, send_sem=send_sems.at[sem_index], recv_sem=recv_sems.at[sem_index],
                    device_id=peer, device_id_type=pl.DeviceIdType.MESH))
        return local, remote

    def start(self, ins, outs, sems):
        local, remote = self._copies(ins, outs, sems, landing=False)
        for cp in local + remote:
            cp.start()

    def finish(self, ins, outs, sems):
        local, remote = self._copies(ins, outs, sems, landing=True)
        for cp in remote:
            cp.wait_recv()
        for cp in remote:
            cp.wait_send()
        for cp in local:
            cp.wait()


def exchange_start(name, xs, kind, dep=None, lands=None):
    ex = Exchange(xs, kind, lands)
    hbm = pl.BlockSpec(memory_space=pltpu.HBM)
    sem = pl.BlockSpec(memory_space=pltpu.SEMAPHORE)
    if lands is None:
        lands = [lax.empty(s.shape, s.dtype) for s in ex.land_shape]
    n_src, n = len(ex.xs), ex.n
    n_inputs = n_src + n + (0 if dep is None else 1)

    def body(*refs):
        ins, lnd, sems, token = refs[:n_src], refs[n_src:n_src + n], refs[n_inputs:n_inputs + 3], refs[-1]
        ex.start(ins, lnd, sems)
        token[...] = jnp.zeros_like(token)

    res = pl.pallas_call(
        body, name=name, in_specs=[hbm] * (n_src + n) + ([] if dep is None else [pl.BlockSpec(memory_space=pl.ANY)]),
        out_specs=[sem] * 3 + [hbm] * (n_src + n) + [pl.BlockSpec(memory_space=pltpu.VMEM)],
        out_shape=ex.sems + [pltpu.HBM(x.shape, x.dtype) for x in ex.xs]
        + [pltpu.HBM(s.shape, s.dtype) for s in ex.land_shape] + [jax.ShapeDtypeStruct((SUBLANES, LANES), F32)],
        input_output_aliases={i: 3 + i for i in range(n_src + n)},
        compiler_params=pltpu.CompilerParams(has_side_effects=pltpu.SideEffectType.DATAFLOW_SIDE_EFFECTING),
    )(*[pltpu.with_memory_space_constraint(x, pltpu.HBM) for x in ex.xs + list(lands)],
      *([] if dep is None else [dep]))
    return (ex, res[:3], res[3:3 + n_src], res[3 + n_src:3 + n_src + n]), res[-1]


def exchange_wait(name, handles, after):
    ex, sems, srcs, lands = handles
    n_src, n = len(srcs), len(lands)
    hbm = pl.BlockSpec(memory_space=pltpu.HBM)
    sem = pl.BlockSpec(memory_space=pltpu.SEMAPHORE)

    def body(*refs):
        ins, lnd, sem_refs = refs[:n_src], refs[n_src:n_src + n], refs[n_src + n:n_src + n + 3]
        ex.finish(ins, lnd, sem_refs)

    res = pl.pallas_call(
        body, name=name, in_specs=[hbm] * (n_src + n) + [sem] * 3 + [pl.BlockSpec(memory_space=pl.ANY)],
        out_specs=[hbm] * (n_src + n),
        out_shape=[pltpu.HBM(x.shape, x.dtype) for x in srcs] + [pltpu.HBM(x.shape, x.dtype) for x in lands],
        input_output_aliases={i: i for i in range(n_src + n)},
        compiler_params=pltpu.CompilerParams(has_side_effects=pltpu.SideEffectType.DATAFLOW_SIDE_EFFECTING),
    )(*srcs, *lands, *sems, after)
    return res[n_src:]


def forward_start(name, chip_gather, after):
    lands = exchange_wait(name + "_wait", chip_gather, after)
    return exchange_start(name + "_forward_start", [], "forward", lands=lands)


_Z = (0, 1024)
_XBC = (1024, 2560)
_DT = (2560, 2576)
_RKV = (2576, 5648)
_PW = (5648, 5744)
_PA = (5744, 5840)
_PG = (5840, 6096)
D_IN = 6096

_SMALL = ("norm_mix_g", "ssd_conv_b", "ssd_dt_bias", "ssd_a_log", "ssd_d", "ssd_norm_g", "rwkv_mu", "rwkv_w0",
          "rwkv_a0", "rwkv_k_k", "rwkv_k_a", "rwkv_r_k", "rwkv_ln_w", "rwkv_ln_b", "norm_x_g", "norm_mem_g",
          "norm_ffn_g", "final_norm_g")
_WEIGHTS = ("norm_mix_g", "w_in", "ssd_conv_w", "ssd_conv_b", "ssd_dt_bias", "ssd_a_log", "ssd_d", "ssd_norm_g",
            "rwkv_mu", "rwkv_w0", "rwkv_w2", "rwkv_a0", "rwkv_a2", "rwkv_g2", "rwkv_k_k", "rwkv_k_a", "rwkv_r_k",
            "rwkv_ln_w", "rwkv_ln_b", "w_out", "norm_x_g", "norm_mem_g", "xattn_wq", "xattn_wk", "xattn_wv",
            "xattn_wo", "norm_ffn_g", "ffn_w1", "ffn_w2", "final_norm_g")


def _pad_lanes(x, width=LANES):
    return jnp.pad(x, ((0, 0), (0, width - x.shape[1])))


def _pack_small(vals):
    flat = jnp.concatenate([vals[n].reshape(-1) for n in _SMALL])
    rows = -(-flat.shape[0] // (LANES * SUBLANES)) * SUBLANES
    return jnp.pad(flat, (0, rows * LANES - flat.shape[0])).reshape(rows, LANES)


def _unpack_small(packed, shapes):
    flat = packed.reshape(-1)
    out, pos = {}, 0
    for n in _SMALL:
        size = 1
        for s in shapes[n]:
            size *= s
        out[n] = flat[pos:pos + size].reshape(shapes[n])
        pos += size
    return out


def _cols(w, rng):
    return w[:, rng[0]:rng[1]]


def kernel(x, mem, norm_mix_g, w_in, ssd_conv_w, ssd_conv_b, ssd_dt_bias, ssd_a_log, ssd_d, ssd_norm_g, rwkv_mu, rwkv_w0, rwkv_w2, rwkv_a0, rwkv_a2, rwkv_g2, rwkv_k_k, rwkv_k_a, rwkv_r_k, rwkv_ln_w, rwkv_ln_b, w_out, norm_x_g, norm_mem_g, xattn_wq, xattn_wk, xattn_wv, xattn_wo, norm_ffn_g, ffn_w1, ffn_w2, final_norm_g, loss_target, m_norm_mix_g, m_w_in, m_ssd_conv_w, m_ssd_conv_b, m_ssd_dt_bias, m_ssd_a_log, m_ssd_d, m_ssd_norm_g, m_rwkv_mu, m_rwkv_w0, m_rwkv_w2, m_rwkv_a0, m_rwkv_a2, m_rwkv_g2, m_rwkv_k_k, m_rwkv_k_a, m_rwkv_r_k, m_rwkv_ln_w, m_rwkv_ln_b, m_w_out, m_norm_x_g, m_norm_mem_g, m_xattn_wq, m_xattn_wk, m_xattn_wv, m_xattn_wo, m_norm_ffn_g, m_ffn_w1, m_ffn_w2, m_final_norm_g, v_norm_mix_g, v_w_in, v_ssd_conv_w, v_ssd_conv_b, v_ssd_dt_bias, v_ssd_a_log, v_ssd_d, v_ssd_norm_g, v_rwkv_mu, v_rwkv_w0, v_rwkv_w2, v_rwkv_a0, v_rwkv_a2, v_rwkv_g2, v_rwkv_k_k, v_rwkv_k_a, v_rwkv_r_k, v_rwkv_ln_w, v_rwkv_ln_b, v_w_out, v_norm_x_g, v_norm_mem_g, v_xattn_wq, v_xattn_wk, v_xattn_wv, v_xattn_wo, v_norm_ffn_g, v_ffn_w1, v_ffn_w2, v_final_norm_g):
    given = dict(locals())
    wts = {n: given[n] for n in _WEIGHTS}
    mom_m = {n: given["m_" + n] for n in _WEIGHTS}
    mom_v = {n: given["v_" + n] for n in _WEIGHTS}
    d = D_MODEL
    xt, memt, tgt = x[0], mem[0], loss_target[0]
    tm = 256

    big = {"w_in": w_in[0], "w_out": w_out[0], "xattn_wq": xattn_wq[0], "xattn_wk": xattn_wk[0],
           "xattn_wv": xattn_wv[0], "xattn_wo": xattn_wo[0], "ffn_w1": ffn_w1[0], "ffn_w2": ffn_w2[0]}
    small_sh = {"ssd_conv_w": ssd_conv_w.reshape(4, -1), "rwkv_w2": rwkv_w2[0], "rwkv_a2": rwkv_a2[0],
                "rwkv_g2": rwkv_g2[0]}
    cast_one = lambda n, deps=(): rowwise_fwd("cast_" + n, cast_fn, [big[n]], [], [(big[n].shape[1], BF16)], 256,
                                              deps=deps)[0]
    gather_in, token_in = exchange_start("gather_in_start", [cast_one("w_in")] + list(small_sh.values()), "gather_chips")
    cast = {n: cast_one(n, deps=[token_in]) for n in big if n != "w_in"}
    late_a = ("w_out", "xattn_wq", "xattn_wk", "xattn_wv", "xattn_wo")
    late_b = ("ffn_w1", "ffn_w2")
    gather_a, token_a = exchange_start("gather_attn_start", [cast[n] for n in late_a], "gather_chips", dep=token_in)
    gather_b, token_b = exchange_start("gather_ffn_start", [cast[n] for n in late_b], "gather_chips", dep=token_a)
    (h1,) = rowwise_fwd("norm_mix", rmsnorm_fn, [xt], [norm_mix_g], [(d, BF16)], tm, deps=[token_b])
    forward_in, token_in = forward_start("gather_in", gather_in, after=h1)
    gathered = exchange_wait("gather_in_forward_wait", forward_in, after=token_in)
    g_big = {"w_in": gathered[0]}
    g_small = dict(zip(small_sh, gathered[1:]))

    w_in_full = jnp.transpose(g_big["w_in"], (1, 0, 2)).reshape(d, D_IN)
    w_z, w_xbc, w_rkv, w_pg = (_cols(w_in_full, r) for r in (_Z, _XBC, _RKV, _PG))
    w_sm = jnp.concatenate([_pad_lanes(_cols(w_in_full, r)) for r in (_PW, _PA, _DT)], axis=1)
    unshard_cols = lambda g: jnp.transpose(g, (1, 0, 2)).reshape(g.shape[1], -1)
    conv_w_f = unshard_cols(g_small["ssd_conv_w"])
    pad_rows = lambda a: jnp.pad(a, ((0, LANES - a.shape[0]), (0, 0)))
    w2p, a2p = pad_rows(unshard_cols(g_small["rwkv_w2"])), pad_rows(unshard_cols(g_small["rwkv_a2"]))
    g2_f = unshard_cols(g_small["rwkv_g2"])

    mu = rwkv_mu
    mu_rkv, mu_pg = mu[:, :3072], mu[:, 3264:3520]
    mu_pwa = jnp.concatenate([_pad_lanes(mu[:, 3072:3168]), _pad_lanes(mu[:, 3168:3264])], axis=1)
    dt_bias_p, a_log_p, d_p = _pad_lanes(ssd_dt_bias), _pad_lanes(ssd_a_log), _pad_lanes(ssd_d)
    r_k_row = rwkv_r_k.reshape(1, RWKV_WIDTH)
    g_final = final_norm_g.reshape(1, d)

    u_z = mm("in_z", h1, w_z, "nn")
    u_xbc = mm("in_xbc", h1, w_xbc, "nn")
    u_rkv = mm("in_rkv", h1, w_rkv, "nn")
    u_pg = mm("in_pg", h1, w_pg, "nn")
    u_sm = mm("in_small", h1, w_sm, "nn")

    ssd_pre_rows = lambda: [Rows(u_xbc, shifts=(1, 2, 3)), Rows(u_sm, LANES, 2)]
    ssd_pre_params = [conv_w_f, ssd_conv_b, dt_bias_p]
    xs, bm, cm, dt = rowwise_fwd("ssd_pre", ssd_pre_fn, ssd_pre_rows(), ssd_pre_params,
                                 [(SSD_WIDTH, F32), (256, F32), (256, F32), (LANES, F32)], tm)
    y_scan, ssd_ck = ssd_scan_fwd(xs, bm, cm, dt, a_log_p, d_p)
    (y_ssd,) = rowwise_fwd("ssd_post", ssd_post_fn, [y_scan, u_z], [ssd_norm_g], [(SSD_WIDTH, BF16)], tm)

    rwkv_pre_rows = lambda: [Rows(u_rkv, shifts=(1,)), Rows(u_pg, shifts=(1,)), Rows(u_sm, 2 * LANES, 0, shifts=(1,))]
    rwkv_pre_params = [mu_rkv, mu_pg, mu_pwa, rwkv_w0, w2p, rwkv_a0, a2p, g2_f, rwkv_k_k, rwkv_k_a]
    forward_a, token_a = forward_start("gather_attn", gather_a, after=y_ssd)
    r_, lw_, k_, v_, kap_, b_, gate_ = rowwise_fwd("rwkv_pre", rwkv_pre_fn, rwkv_pre_rows(), rwkv_pre_params,
                                                   [(RWKV_WIDTH, F32)] * 7, 128, deps=[token_a])
    ys_r, rwkv_ck = rwkv_scan_fwd(r_, lw_, k_, v_, kap_, b_)
    forward_b, token_b = forward_start("gather_ffn", gather_b, after=ys_r)
    g_big.update(zip(late_a, exchange_wait("gather_attn_forward_wait", forward_a, after=token_b)))
    w_out_f = g_big["w_out"].reshape(d, d)
    wq_f, wk_f, wv_f, wo_f = (g_big[n].reshape(d, d) for n in ("xattn_wq", "xattn_wk", "xattn_wv", "xattn_wo"))
    rwkv_post_params = [rwkv_ln_w, rwkv_ln_b, r_k_row]
    (y_rwkv,) = rowwise_fwd("rwkv_post", rwkv_post_fn, [ys_r, r_, k_, v_, gate_], rwkv_post_params,
                            [(RWKV_WIDTH, BF16)], tm)
    ycat = jnp.concatenate([y_ssd, y_rwkv], axis=1)
    x1 = mm("out_proj", ycat, w_out_f, "nn", res=xt)

    (h2,) = rowwise_fwd("norm_x", rmsnorm_fn, [x1], [norm_x_g], [(d, BF16)], tm)
    (mn,) = rowwise_fwd("norm_mem", rmsnorm_fn, [memt], [norm_mem_g], [(d, BF16)], tm)
    q = mm("xattn_q", h2, wq_f, "nn", out_dtype=BF16)
    kx = mm("xattn_k", mn, wk_f, "nn")
    vx = mm("xattn_v", mn, wv_f, "nn")
    (o,) = rowwise_fwd("xattn", attn_fn, [q], [kx, vx], [(d, BF16)], tm)
    x2 = mm("xattn_o", o, wo_f, "nn", res=x1)

    (h3,) = rowwise_fwd("norm_ffn", rmsnorm_fn, [x2], [norm_ffn_g], [(d, BF16)], tm)
    w1_s, w2_g = exchange_wait("gather_ffn_forward_wait", forward_b, after=h3)
    w2_f = w2_g.reshape(D_FF, d)
    relu2_epi = lambda acc: (jnp.square(jnp.maximum(acc, 0.0)), jnp.maximum(acc, 0.0))
    hid, relu_a = mm("ffn_1", h3, w1_s, "nn", b_slabs=N_DEV, epi=relu2_epi, out_dtypes=[BF16, BF16])
    x3 = mm("ffn_2", hid, w2_f, "nn", res=x2)

    loss_blk, dx3, dg_final = loss_and_grad(x3, tgt, g_final, tm)

    grads = {}
    grads["ffn_w2"] = mm("d_ffn_w2", hid, dx3, "tn", out_dtype=BF16).reshape(N_DEV, D_FF // N_DEV, d)
    sc_w2, tok = exchange_start("scatter_ffn_w2_start", [grads["ffn_w2"]], "scatter")
    da = mm("d_hid", dx3, w2_f, "nt", dep=tok, epi=lambda acc, ra: (2.0 * acc * ra,), extras=[relu_a],
            out_dtypes=[BF16])
    grads["ffn_w1"] = mm("d_ffn_w1", h3, da, "tn", out_dtype=BF16, out_slabs=N_DEV)
    sc_w1, tok = exchange_start("scatter_ffn_w1_start", [grads["ffn_w1"]], "scatter")
    dh3 = mm("d_h3", da, w1_s, "nt", b_slabs=N_DEV, dep=tok)
    (dx2,), (dg_ffn,) = rowwise_bwd("norm_ffn_bwd", rmsnorm_fn, [x2], [norm_ffn_g], [[dh3]], tm, [F32], row_add=[dx3])

    grads["xattn_wo"] = mm("d_wo", o, dx2, "tn", out_dtype=BF16).reshape(N_DEV, d // N_DEV, d)
    sc_wo, tok = exchange_start("scatter_wo_start", [grads["xattn_wo"]], "scatter")
    d_o = mm("d_o", dx2, wo_f, "nt", dep=tok)
    (dq,), (dkx, dvx) = rowwise_bwd("xattn_bwd", attn_fn, [q], [kx, vx], [[d_o]], tm, [BF16])
    grads["xattn_wq"] = mm("d_wq", h2, dq, "tn", out_dtype=BF16).reshape(N_DEV, d // N_DEV, d)
    grads["xattn_wk"] = mm("d_wk", mn, dkx, "tn", out_dtype=BF16).reshape(N_DEV, d // N_DEV, d)
    grads["xattn_wv"] = mm("d_wv", mn, dvx, "tn", out_dtype=BF16).reshape(N_DEV, d // N_DEV, d)
    qkv = ("xattn_wq", "xattn_wk", "xattn_wv")
    sc_qkv, tok = exchange_start("scatter_qkv_start", [grads[n] for n in qkv], "scatter")
    dmn = mm("d_mn_v", dvx, wv_f, "nt", res=mm("d_mn_k", dkx, wk_f, "nt", dep=tok))
    _, (dg_mem,) = rowwise_bwd("norm_mem_bwd", rmsnorm_fn, [memt], [norm_mem_g], [[dmn]], tm, [None])
    dh2 = mm("d_h2", dq, wq_f, "nt", dep=dg_mem)
    (dx1,), (dg_x,) = rowwise_bwd("norm_x_bwd", rmsnorm_fn, [x1], [norm_x_g], [[dh2]], tm, [F32], row_add=[dx2])

    grads["w_out"] = mm("d_w_out", ycat, dx1, "tn", out_dtype=BF16).reshape(N_DEV, d // N_DEV, d)
    sc_wout, tok = exchange_start("scatter_w_out_start", [grads["w_out"]], "scatter")
    d_ycat = mm("d_ycat", dx1, w_out_f, "nt", dep=tok)

    (d_ys, d_r1, d_k1, d_v1, d_gate), (dln_w, dln_b, dr_k) = rowwise_bwd(
        "rwkv_post_bwd", rwkv_post_fn, [ys_r, r_, k_, v_, gate_], rwkv_post_params,
        [[Rows(d_ycat, RWKV_WIDTH, 1)]], tm, [F32] * 5)
    d_r2, d_lw, d_k2, d_v2, d_kap, d_b = rwkv_scan_bwd(r_, lw_, k_, v_, kap_, b_, rwkv_ck, d_ys)
    (du_rkv, du_pg, du_pwa), rwkv_pg = rowwise_bwd(
        "rwkv_pre_bwd", rwkv_pre_fn, rwkv_pre_rows(), rwkv_pre_params,
        [[d_r1, d_r2], [d_lw], [d_k1, d_k2], [d_v1, d_v2], [d_kap], [d_b], [d_gate]], 128, [BF16] * 3)
    dmu_rkv, dmu_pg, dmu_pwa, dw0, dw2p, da0, da2p, dg2, dk_k, dk_a = rwkv_pg

    (d_yscan, du_z), (dssd_norm_g,) = rowwise_bwd("ssd_post_bwd", ssd_post_fn, [y_scan, u_z], [ssd_norm_g],
                                                  [[Rows(d_ycat, SSD_WIDTH, 0)]], tm, [F32, BF16])
    dxs, dbm, dcm, ddt2, da_log_p, dd_p = ssd_scan_bwd(xs, bm, cm, dt, a_log_p, d_p, ssd_ck, d_yscan)
    (du_xbc, du_dt), (dconv_w, dconv_b, ddt_bias_p) = rowwise_bwd(
        "ssd_pre_bwd", ssd_pre_fn, ssd_pre_rows(), ssd_pre_params,
        [[dxs], [dbm], [dcm], [ddt2[0], ddt2[1]]], tm, [BF16, BF16])
    du_sm = jnp.concatenate([du_pwa, du_dt], axis=1)

    dw_z = mm("d_w_z", h1, du_z, "tn", out_dtype=BF16)
    dw_xbc = mm("d_w_xbc", h1, du_xbc, "tn", out_dtype=BF16)
    dw_rkv = mm("d_w_rkv", h1, du_rkv, "tn", out_dtype=BF16)
    dw_pg = mm("d_w_pg", h1, du_pg, "tn", out_dtype=BF16)
    dw_sm = mm("d_w_small", h1, du_sm, "tn", out_dtype=BF16)
    dw_in_full = jnp.concatenate([dw_z, dw_xbc, dw_sm[:, 256:272], dw_rkv, dw_sm[:, 0:96], dw_sm[:, 128:224], dw_pg], axis=1)
    to_slabs = lambda g: jnp.transpose(g.reshape(g.shape[0], N_DEV, -1), (1, 0, 2))
    grads["w_in"] = to_slabs(dw_in_full)
    grads["ssd_conv_w"] = to_slabs(dconv_w)
    grads["rwkv_w2"] = to_slabs(dw2p[:96])
    grads["rwkv_a2"] = to_slabs(da2p[:96])
    grads["rwkv_g2"] = to_slabs(dg2)
    tail = ("w_in", "ssd_conv_w", "rwkv_w2", "rwkv_a2", "rwkv_g2")
    sc_tail, tok = exchange_start("scatter_tail_start", [grads[n] for n in tail], "scatter")
    dh1 = mm("d_h1_z", du_z, w_z, "nt", dep=tok)
    dh1 = mm("d_h1_xbc", du_xbc, w_xbc, "nt", res=dh1)
    dh1 = mm("d_h1_rkv", du_rkv, w_rkv, "nt", res=dh1)
    dh1 = mm("d_h1_pg", du_pg, w_pg, "nt", res=dh1)
    dh1 = mm("d_h1_small", du_sm, w_sm, "nt", res=dh1)
    (dx,), (dg_mix,) = rowwise_bwd("norm_mix_bwd", rmsnorm_fn, [xt], [norm_mix_g], [[dh1]], tm, [F32], row_add=[dx1])

    dmu =jnp.concatenate([dmu_rkv, dmu_pwa[:, 0:96], dmu_pwa[:, 128:224], dmu_pg], axis=1)
    small_grads = {
        "norm_mix_g": dg_mix, "ssd_conv_b": dconv_b, "ssd_dt_bias": ddt_bias_p[:, :16], "ssd_a_log": da_log_p[:, :16],
        "ssd_d": dd_p[:, :16], "ssd_norm_g": dssd_norm_g, "rwkv_mu": dmu, "rwkv_w0": dw0, "rwkv_a0": da0,
        "rwkv_k_k": dk_k, "rwkv_k_a": dk_a, "rwkv_r_k": dr_k, "rwkv_ln_w": dln_w, "rwkv_ln_b": dln_b,
        "norm_x_g": dg_x, "norm_mem_g": dg_mem, "norm_ffn_g": dg_ffn, "final_norm_g": dg_final}

    gather_small, tok = exchange_start("gather_small_start", [_pack_small(small_grads)], "gather")
    received = {}
    for names, handle in ((("ffn_w2",), sc_w2), (("ffn_w1",), sc_w1), (("xattn_wo",), sc_wo), (qkv, sc_qkv),
                          (("w_out",), sc_wout)):
        received.update(zip(names, exchange_wait("scatter_" + names[0] + "_wait", handle, after=tok)))

    out_g, out_d, out_m, out_v = {}, {}, {}, {}

    def run_adamw(n, dep):
        shape = wts[n].shape
        two_d = lambda a: a.reshape(-1, shape[-1])
        res = adamw("adamw_" + n, received[n].reshape(N_DEV, -1, shape[-1]), two_d(wts[n]), two_d(mom_m[n]),
                    two_d(mom_v[n]), dep=dep)
        out_g[n], out_d[n], out_m[n], out_v[n] = (r.reshape(shape) for r in res)
        return res[0]

    last = None
    for n in ("ffn_w2", "ffn_w1", "xattn_wo") + qkv + ("w_out",):
        last = run_adamw(n, last)
    received.update(zip(tail, exchange_wait("scatter_tail_wait", sc_tail, after=last)))
    for n in tail:
        last = run_adamw(n, last)
    (small_all,) = exchange_wait("gather_small_wait", gather_small, after=last)
    res = adamw("adamw_small", small_all, _pack_small(wts), _pack_small(mom_m), _pack_small(mom_v))
    shapes = {n: wts[n].shape for n in _SMALL}
    for dst, packed in zip((out_g, out_d, out_m, out_v), res):
        dst.update(_unpack_small(packed, shapes))

    loss = lax.psum(loss_blk[0, 0], ("x", "y", "c"))
    return (loss, dx[None], *[out_g[n] for n in _WEIGHTS], *[out_d[n] for n in _WEIGHTS],
            *[out_m[n] for n in _WEIGHTS], *[out_v[n] for n in _WEIGHTS])
```

```python
import functools

import jax
import jax.numpy as jnp
from jax import lax
from jax.experimental import pallas as pl
from jax.experimental.pallas import tpu as pltpu

F32 = jnp.float32
BF16 = jnp.bfloat16
HIGHEST = lax.Precision.HIGHEST

N_DEV = 8
D_MODEL = 2048
NORM_EPS = 1e-6
SSD_WIDTH = 1024
SSD_CONV_DIM = 1536
SSD_HEADS = 16
SSD_HEAD_DIM = 64
SSD_STATE = 128
SSD_CHUNK = 128
SSD_HEADS_PER_GROUP = 8
RWKV_WIDTH = 1024
RWKV_HEADS = 16
RWKV_HEAD_DIM = 64
RWKV_LN_EPS = 64e-5
RWKV_CHUNK = 64
RWKV_HEADS_PER_STEP = 16
XATTN_HEADS = 4
XATTN_HEAD_DIM = 512
D_FF = 8192
LANES = 128
SUBLANES = 8
VMEM_LIMIT = 56 * 1024 * 1024

ADAM_LR = 0.001
ADAM_B1 = 0.9
ADAM_B2 = 0.999
ADAM_EPS = 1e-08
ADAM_WD = 0.01
ADAM_STEP = 10

_DN = {"nn": ((1,), (0,)), "nt": ((1,), (1,)), "tn": ((0,), (0,))}


def _dg(a, b, mode, precision=None):
    (ca,), (cb,) = _DN[mode]
    dn = (((ca + 1,), (cb + 1,)), ((0,), (0,))) if a.ndim == 3 else (((ca,), (cb,)), ((), ()))
    return lax.dot_general(a, b, dn, precision=precision, preferred_element_type=F32)


@functools.partial(jax.custom_vjp, nondiff_argnums=(2,))
def bdot(a, b, mode):
    return _dg(a.astype(BF16), b.astype(BF16), mode)


def _bdot_fwd(a, b, mode):
    return bdot(a, b, mode), (a, b)


def _bdot_bwd(mode, res, g):
    a, b = res
    ab, bb, gb = a.astype(BF16), b.astype(BF16), g.astype(BF16)
    if mode == "nn":
        da, db = _dg(gb, bb, "nt"), _dg(ab, gb, "tn")
    elif mode == "nt":
        da, db = _dg(gb, bb, "nn"), _dg(gb, ab, "tn")
    else:
        da, db = _dg(bb, gb, "nt"), _dg(ab, gb, "nn")
    return da.astype(a.dtype), db.astype(b.dtype)


bdot.defvjp(_bdot_fwd, _bdot_bwd)


def fdot(a, b, mode):
    return _dg(a, b, mode, precision=HIGHEST)


def _split3(x):
    hi = x.astype(BF16)
    r1 = x - hi.astype(F32)
    mid = r1.astype(BF16)
    lo = (r1 - mid.astype(F32)).astype(BF16)
    return hi, mid, lo


def _dot01(x, m01):
    hi, mid, _ = _split3(x)
    return _dg(hi, m01, "nn") + _dg(mid, m01, "nn")


def _exact_dot_impl(a, b, mode, exact):
    if exact == "a":
        ae = a.astype(BF16)
        return sum(_dg(ae, part, mode) for part in _split3(b))
    be = b.astype(BF16)
    return sum(_dg(part, be, mode) for part in _split3(a))


@functools.partial(jax.custom_vjp, nondiff_argnums=(2, 3))
def exact_dot(a, b, mode, exact):
    return _exact_dot_impl(a, b, mode, exact)


def _exact_dot_fwd(a, b, mode, exact):
    return _exact_dot_impl(a, b, mode, exact), (a, b)


def _exact_dot_bwd(mode, exact, res, g):
    a, b = res
    if exact == "a":
        db = {"nn": lambda: _exact_dot_impl(a, g, "tn", "a"), "nt": lambda: _exact_dot_impl(g, a, "tn", "b"),
              "tn": lambda: _exact_dot_impl(a, g, "nn", "a")}[mode]()
        return jnp.zeros_like(a), db
    da = {"nn": lambda: _exact_dot_impl(g, b, "nt", "b"), "nt": lambda: _exact_dot_impl(g, b, "nn", "b"),
          "tn": lambda: _exact_dot_impl(b, g, "nt", "a")}[mode]()
    return da, jnp.zeros_like(b)


exact_dot.defvjp(_exact_dot_fwd, _exact_dot_bwd)


def _head_indicator(width, heads, transpose):
    hd = width // heads
    shape = (LANES, width) if transpose else (width, LANES)
    lane = lax.broadcasted_iota(jnp.int32, shape, 1 if not transpose else 0)
    pos = lax.broadcasted_iota(jnp.int32, shape, 0 if not transpose else 1)
    return ((pos >= lane * hd) & (pos < lane * hd + hd)).astype(BF16)


@jax.custom_vjp
def head_sum(x):
    w = x.shape[-1]
    e = _head_indicator(w, w // RWKV_HEAD_DIM, False)
    et = _head_indicator(w, w // RWKV_HEAD_DIM, True)
    return _dot01(_dot01(x, e), et)


head_sum.defvjp(lambda x: (head_sum(x), None), lambda _, g: (head_sum(g),))


def rmsnorm_fn(x, g):
    y = x * lax.rsqrt(jnp.mean(x * x, axis=-1, keepdims=True) + NORM_EPS)
    return ((y * g).astype(BF16),)


def cast_fn(x):
    return (x.astype(BF16),)


def ssd_pre_fn(xbc, xbc1, xbc2, xbc3, dt_raw, conv_w, conv_b, dt_bias):
    c = conv_w[3:4] * xbc + conv_w[2:3] * xbc1 + conv_w[1:2] * xbc2 + conv_w[0:1] * xbc3 + conv_b
    act = c * jax.nn.sigmoid(c)
    dt = jax.nn.softplus(dt_raw + dt_bias)
    return act[:, :SSD_WIDTH], act[:, SSD_WIDTH:SSD_WIDTH + 256], act[:, SSD_WIDTH + 256:], dt


def ssd_post_fn(yscan, z, norm_g):
    y = yscan * (z * jax.nn.sigmoid(z))
    half = SSD_WIDTH // 2
    parts = []
    for g in range(2):
        yg = y[:, g * half:(g + 1) * half]
        parts.append(yg * lax.rsqrt(jnp.mean(yg * yg, axis=-1, keepdims=True) + NORM_EPS))
    return ((jnp.concatenate(parts, axis=-1) * norm_g).astype(BF16),)


def rwkv_pre_fn(rkv, rkv_p, pg, pg_p, pwa, pwa_p, mu_rkv, mu_pg, mu_pwa, w0, w2p, a0, a2p, g2, k_k, k_a):
    w = RWKV_WIDTH
    rkv = rkv + (rkv_p - rkv) * mu_rkv
    pg = pg + (pg_p - pg) * mu_pg
    pwa = pwa + (pwa_p - pwa) * mu_pwa
    r, k, v = rkv[:, :w], rkv[:, w:2 * w], rkv[:, 2 * w:]
    pw, pa = pwa[:, :LANES], pwa[:, LANES:]
    w_log = -jax.nn.softplus(-(w0 + bdot(jnp.tanh(pw), w2p, "nn"))) - 0.5
    lw = -jnp.exp(w_log)
    iclr = jax.nn.sigmoid(a0 + bdot(pa, a2p, "nn"))
    gate = bdot(jax.nn.sigmoid(pg), g2, "nn")
    kk = k * k_k
    kap = kk / jnp.maximum(jnp.sqrt(head_sum(kk * kk)), 1e-12)
    k_mod = k * (1.0 + (iclr - 1.0) * k_a)
    return r, lw, k_mod, v, kap, kap * iclr, gate


def rwkv_post_fn(ys, r, k_mod, v, gate, ln_w, ln_b, r_k):
    inv_n = 1.0 / RWKV_HEAD_DIM
    mean = head_sum(ys) * inv_n
    yc = ys - mean
    var = head_sum(yc * yc) * inv_n
    yn = yc * lax.rsqrt(var + RWKV_LN_EPS) * ln_w + ln_b
    bonus = head_sum(r * k_mod * r_k) * v
    return (((yn + bonus) * gate).astype(BF16),)


def attn_fn(q, kx, vx):
    outs = []
    for h in range(XATTN_HEADS):
        sl = slice(h * XATTN_HEAD_DIM, (h + 1) * XATTN_HEAD_DIM)
        s = bdot(q[:, sl], kx[:, sl], "nt") * (XATTN_HEAD_DIM ** -0.5)
        s = s - jnp.max(s, axis=-1, keepdims=True)
        p = jnp.exp(s)
        p = p / jnp.sum(p, axis=-1, keepdims=True)
        outs.append(bdot(p, vx[:, sl], "nn"))
    return (jnp.concatenate(outs, axis=-1).astype(BF16),)


def loss_fn(x, tgt, g):
    y = x * lax.rsqrt(jnp.mean(x * x, axis=-1, keepdims=True) + NORM_EPS) * g
    err = jnp.square(y - tgt)
    return 0.5 * jnp.sum(jnp.mean(err, axis=-1, keepdims=True), axis=0, keepdims=True)


def _tri_masks(n):
    row = lax.broadcasted_iota(jnp.int32, (n, n), 0)
    col = lax.broadcasted_iota(jnp.int32, (n, n), 1)
    return col <= row, col < row, row == col


def rwkv_chunk_fn(st0, r, lw, k, v, kap, b):
    h, c = r.shape[0], r.shape[1]
    incl, strict, diag = _tri_masks(c)
    cum = exact_dot(jnp.broadcast_to(incl.astype(F32), (h, c, c)), lw, "nn", "a")
    g_in = jnp.exp(cum)
    g_prev = jnp.exp(cum - lw)
    g_inv = jnp.exp(-cum)
    g_end = jnp.exp(cum[:, c - 1:c, :] - cum)
    kap_t, k_t, b_t, r_t = kap * g_prev, k * g_inv, b * g_inv, r * g_in
    a_ub = jnp.where(strict, bdot(kap_t, b_t, "nt"), 0.0)
    a_vk = jnp.where(strict, bdot(kap_t, k_t, "nt"), 0.0)
    rhs = -(bdot(kap_t, st0, "nn") + bdot(a_vk, v, "nn"))
    eye = diag.astype(F32)
    m = -a_ub
    inv = eye + m
    n = 1
    while n * 2 < c:
        m = bdot(m, m, "nn")
        inv = bdot(inv, eye + m, "nn")
        n *= 2
    u = bdot(inv, rhs, "nn")
    y = (bdot(r_t, st0, "nn")
         + bdot(jnp.where(incl, bdot(r_t, k_t, "nt"), 0.0), v, "nn")
         + bdot(jnp.where(incl, bdot(r_t, b_t, "nt"), 0.0), u, "nn"))
    g_full = jnp.exp(exact_dot(lw, jnp.ones((h, c, st0.shape[2]), F32), "tn", "b"))
    st1 = g_full * st0 + bdot(k * g_end, v, "tn") + bdot(b * g_end, u, "tn")
    return y, st1


def ssd_chunk_fn(group, h0, xs, bm, cm, dt, a_log, d_skip):
    q, nh = xs.shape[0], SSD_HEADS_PER_GROUP
    causal, _, _ = _tri_masks(q)
    a_row = -jnp.exp(a_log)
    cs_all = exact_dot(causal.astype(F32), dt * a_row, "nn", "a")
    cs_t = cs_all.T
    lanes = range(group * nh, (group + 1) * nh)
    cs = jnp.stack([cs_all[:, hl:hl + 1] for hl in lanes])
    cs_row = jnp.stack([cs_t[hl:hl + 1, :] for hl in lanes])
    dt_h = jnp.stack([dt[:, hl:hl + 1] for hl in lanes])
    d_h = jnp.stack([d_skip[:, hl:hl + 1] for hl in lanes])
    x = _stack_lanes(xs, nh)
    h0s = _stack_rows(h0, nh)
    lmat = jnp.where(causal, jnp.exp(jnp.where(causal, cs - cs_row, 0.0)), 0.0)
    cb = bdot(cm, bm, "nt")
    xdt = x * dt_h
    cl = cs[:, q - 1:q, :]
    cm_b = jnp.broadcast_to(cm, (nh,) + cm.shape)
    bm_b = jnp.broadcast_to(bm, (nh,) + bm.shape)
    y = bdot(cb * lmat, xdt, "nn") + bdot(cm_b, h0s, "nt") * jnp.exp(cs) + x * d_h
    h1 = h0s * jnp.exp(cl) + bdot(xdt * jnp.exp(cl - cs), bm_b, "tn")
    return jnp.concatenate([y[e] for e in range(nh)], axis=-1), jnp.concatenate([h1[e] for e in range(nh)], axis=0)


class Rows:
    def __init__(self, arr, w=None, cb=0, shifts=()):
        self.arr, self.w, self.cb, self.shifts = arr, (arr.shape[1] if w is None else w), cb, tuple(shifts)


def _as_rows(x):
    return x if isinstance(x, Rows) else Rows(x)


def _shift_down(x, halo, k):
    rolled = pltpu.roll(x, k, 0)
    first = rolled[0:SUBLANES]
    rid = lax.broadcasted_iota(jnp.int32, first.shape, 0)
    patched = jnp.where(rid < k, pltpu.roll(halo, k, 0), first)
    return jnp.concatenate([patched, rolled[SUBLANES:]], axis=0)


def _shift_up(g, carry, k):
    tm = g.shape[0]
    rolled = pltpu.roll(g, tm - k, 0)
    last = rolled[tm - SUBLANES:]
    rid = lax.broadcasted_iota(jnp.int32, last.shape, 0)
    patched = jnp.where(rid >= SUBLANES - k, pltpu.roll(carry, SUBLANES - k, 0), last)
    return jnp.concatenate([rolled[:tm - SUBLANES], patched], axis=0)


def _params():
    return pltpu.CompilerParams(vmem_limit_bytes=VMEM_LIMIT)


def _load_rows(refs, pos, rins, first_block):
    vals = []
    for r in rins:
        x = refs[pos][...].astype(F32) if refs[pos].dtype != F32 else refs[pos][...]
        pos += 1
        vals.append(x)
        if r.shifts:
            halo = refs[pos][...]
            pos += 1
            halo = jnp.where(first_block, jnp.zeros_like(halo), halo)
            for k in r.shifts:
                vals.append(_shift_down(x, halo, k))
    return vals, pos


def _row_specs(rins, tm, blk):
    specs, args = [], []
    for r in rins:
        specs.append(pl.BlockSpec((tm, r.w), lambda i, cb=r.cb: (blk(i), cb)))
        args.append(r.arr)
        if r.shifts:
            per = tm // SUBLANES
            specs.append(pl.BlockSpec((SUBLANES, r.w), lambda i, cb=r.cb: (jnp.maximum(blk(i) * per - 1, 0), cb)))
            args.append(r.arr)
    return specs, args


def rowwise_fwd(name, fn, rins, params, outs, tm, deps=()):
    rins = [_as_rows(r) for r in rins]
    t = rins[0].arr.shape[0]
    tm = min(tm, t)
    nb = t // tm
    specs, args = _row_specs(rins, tm, lambda i: i)
    for p in params:
        specs.append(pl.BlockSpec(p.shape, lambda i: (0, 0)))
        args.append(p)
    for dep in deps:
        specs.append(pl.BlockSpec(memory_space=pl.ANY))
        args.append(dep)
    n_in = len(args)

    def body(*refs):
        vals, pos = _load_rows(refs, 0, rins, pl.program_id(0) == 0)
        pv = [refs[pos + j][...] for j in range(len(params))]
        res = fn(*vals, *pv)
        for o_ref, o in zip(refs[n_in:], res):
            o_ref[...] = o.astype(o_ref.dtype)

    return pl.pallas_call(
        body, name=name, grid=(nb,), in_specs=specs,
        out_specs=[pl.BlockSpec((tm, w), lambda i: (i, 0)) for w, _ in outs],
        out_shape=[jax.ShapeDtypeStruct((t, w), dt) for w, dt in outs],
        compiler_params=_params(),
    )(*args)


def rowwise_bwd(name, fn, rins, params, cts, tm, grad_dtypes, row_add=None):
    rins = [_as_rows(r) for r in rins]
    cts = [[_as_rows(c) for c in lst] for lst in cts]
    row_add = [_as_rows(a) for a in (row_add or [])]
    t = rins[0].arr.shape[0]
    tm = min(tm, t)
    nb = t // tm
    rev = lambda i: nb - 1 - i
    specs, args = _row_specs(rins, tm, rev)
    for p in params:
        specs.append(pl.BlockSpec(p.shape, lambda i: (0, 0)))
        args.append(p)
    flat_cts = [c for lst in cts for c in lst] + row_add
    for c in flat_cts:
        specs.append(pl.BlockSpec((tm, c.w), lambda i, cb=c.cb: (rev(i), cb)))
        args.append(c.arr)
    n_in = len(args)
    want = [i for i, d in enumerate(grad_dtypes) if d is not None]
    out_specs = [pl.BlockSpec((tm, rins[i].w), lambda i_: (rev(i_), 0)) for i in want]
    out_shape = [jax.ShapeDtypeStruct((t, rins[i].w), grad_dtypes[i]) for i in want]
    out_specs += [pl.BlockSpec(p.shape, lambda i: (0, 0)) for p in params]
    out_shape += [jax.ShapeDtypeStruct(p.shape, F32) for p in params]
    n_out = len(out_shape)
    scratch = [pltpu.VMEM((SUBLANES, r.w), F32) for r in rins for _ in r.shifts]

    def body(*refs):
        i = pl.program_id(0)
        vals, pos = _load_rows(refs, 0, rins, rev(i) == 0)
        pv = [refs[pos + j][...] for j in range(len(params))]
        pos += len(params)
        outs, vjp = jax.vjp(fn, *vals, *pv)
        ct_vals = []
        for o, lst in zip(outs, cts):
            acc = None
            for _ in lst:
                cv = refs[pos][...].astype(F32)
                pos += 1
                acc = cv if acc is None else acc + cv
            ct_vals.append(acc.astype(o.dtype))
        adds = [refs[pos + j][...].astype(F32) for j in range(len(row_add))]
        grads = vjp(tuple(ct_vals))
        out_refs = refs[n_in:n_in + n_out]
        carry_refs = refs[n_in + n_out:]

        @pl.when(i == 0)
        def _():
            for cr in carry_refs:
                cr[...] = jnp.zeros_like(cr)
            for pr in out_refs[len(want):]:
                pr[...] = jnp.zeros_like(pr)

        gi, ci, oi = 0, 0, 0
        for idx, r in enumerate(rins):
            d = grads[gi]
            gi += 1
            for k in r.shifts:
                dk = grads[gi]
                gi += 1
                d = d + _shift_up(dk, carry_refs[ci][...], k)
                carry_refs[ci][...] = dk[0:SUBLANES]
                ci += 1
            if idx == 0:
                for a in adds:
                    d = d + a
            if grad_dtypes[idx] is not None:
                out_refs[oi][...] = d.astype(out_refs[oi].dtype)
                oi += 1
        for pr, gp in zip(out_refs[len(want):], grads[gi:]):
            pr[...] += gp

    res = pl.pallas_call(
        body, name=name, grid=(nb,), in_specs=specs, out_specs=out_specs, out_shape=out_shape,
        scratch_shapes=scratch, compiler_params=_params(),
    )(*args)
    return res[:len(want)], res[len(want):]


def _pick(n, pref):
    for c in pref:
        if n % c == 0:
            return c
    return n


MM_VMEM_BUDGET = 40 * 1024 * 1024
MM_PEAK_FLOPS = 0.9e15
MM_HBM_BYTES_PER_S = 3.0e12
MM_STEP_SECONDS = 0.35e-6


def _mm_tiles(m, n, k, size_a, size_b, size_out, size_res, single_k):
    best = None
    for tk in sorted({c for c in (k, 2048, 1024, 512, 256, 128) if c <= 2048 and k % c == 0}, reverse=True):
        for tm in (1024, 512, 256, 128):
            if m % tm:
                continue
            for tn in (1024, 768, 512, 384, 256, 128):
                if n % tn:
                    continue
                nk = k // tk
                vmem = 2 * (tm * tk * size_a + tk * tn * size_b + tm * tn * (size_out + size_res))
                vmem += tm * tn * 4 * (2 if nk > 1 or not single_k else 1)
                vmem += (tm * tk * 2 if size_a > 2 else 0) + (tk * tn * 2 if size_b > 2 else 0)
                if vmem > MM_VMEM_BUDGET:
                    continue
                steps = (m // tm) * (n // tn) * nk
                a_reads = 1 if (nk == 1 and single_k) else n // tn
                traffic = m * k * size_a * a_reads + k * n * size_b * (m // tm) + m * n * (size_out + size_res)
                cost = max(2.0 * m * n * k / MM_PEAK_FLOPS, traffic / MM_HBM_BYTES_PER_S) + steps * MM_STEP_SECONDS
                if best is None or cost < best[0]:
                    best = (cost, tm, tn, tk)
    return best[1:]


def mm(name, a, b, mode, out_dtype=F32, res=None, b_slabs=None, out_slabs=None, dep=None, epi=None, extras=(),
       out_dtypes=None):
    if mode == "tn":
        k_dim, m_dim = a.shape
    else:
        m_dim, k_dim = a.shape
    if b_slabs:
        n_dim = b.shape[0] * b.shape[2] if mode == "nn" else b.shape[1]
    else:
        n_dim = b.shape[0] if mode == "nt" else b.shape[1]
    n_slabs = out_slabs or (b_slabs if (b_slabs and mode == "nn") else 1)
    k_slabs = b_slabs if (b_slabs and mode == "nt") else 1
    if epi is None:
        out_dtypes = [out_dtype]
        if res is None:
            epi = lambda acc: (acc,)
        else:
            extras, epi = [res], lambda acc, r: (acc + r,)
    tm, tn, tk = _mm_tiles(m_dim, n_dim // n_slabs, k_dim // k_slabs, a.dtype.itemsize, b.dtype.itemsize,
                           sum(jnp.dtype(dt).itemsize for dt in out_dtypes), sum(e.dtype.itemsize for e in extras),
                           single_k=(k_slabs == 1))
    nji = n_dim // n_slabs // tn
    nki = k_dim // k_slabs // tk
    nblk = lambda js, j: js * nji + j
    kblk = lambda ks, k: ks * nki + k
    if mode == "tn":
        a_spec = pl.BlockSpec((tk, tm), lambda i, js, j, ks, k: (kblk(ks, k), i))
    else:
        a_spec = pl.BlockSpec((tm, tk), lambda i, js, j, ks, k: (i, kblk(ks, k)))
    if b_slabs and mode == "nn":
        b_spec = pl.BlockSpec((None, tk, tn), lambda i, js, j, ks, k: (js, k, j))
    elif b_slabs and mode == "nt":
        b_spec = pl.BlockSpec((None, tn, tk), lambda i, js, j, ks, k: (ks, nblk(js, j), k))
    elif mode == "nt":
        b_spec = pl.BlockSpec((tn, tk), lambda i, js, j, ks, k: (nblk(js, j), kblk(ks, k)))
    else:
        b_spec = pl.BlockSpec((tk, tn), lambda i, js, j, ks, k: (kblk(ks, k), nblk(js, j)))
    specs, args = [a_spec, b_spec], [a, b]
    for e in extras:
        specs.append(pl.BlockSpec((tm, tn), lambda i, js, j, ks, k: (i, nblk(js, j))))
        args.append(e)
    if dep is not None:
        specs.append(pl.BlockSpec(memory_space=pl.ANY))
        args.append(dep)
    if out_slabs:
        o_specs = [pl.BlockSpec((None, tm, tn), lambda i, js, j, ks, k: (js, i, j))]
        o_shapes = [jax.ShapeDtypeStruct((out_slabs, m_dim, n_dim // out_slabs), out_dtypes[0])]
    else:
        o_specs = [pl.BlockSpec((tm, tn), lambda i, js, j, ks, k: (i, nblk(js, j))) for _ in out_dtypes]
        o_shapes = [jax.ShapeDtypeStruct((m_dim, n_dim), dt) for dt in out_dtypes]

    one_k_step = k_slabs * nki == 1
    n_in, n_out = len(args), len(out_dtypes)

    def body(*refs):
        a_ref, b_ref = refs[0], refs[1]
        part = _dg(a_ref[...].astype(BF16), b_ref[...].astype(BF16), mode)

        def finish(acc):
            outs = epi(acc, *[refs[2 + j][...].astype(F32) for j in range(len(extras))])
            for o_ref, o in zip(refs[n_in:n_in + n_out], outs):
                o_ref[...] = o.astype(o_ref.dtype)

        if one_k_step:
            finish(part)
            return
        acc_ref = refs[n_in + n_out]
        ks, kk = pl.program_id(3), pl.program_id(4)

        @pl.when((ks == 0) & (kk == 0))
        def _():
            acc_ref[...] = part

        @pl.when((ks > 0) | (kk > 0))
        def _():
            acc_ref[...] += part

        pl.when((ks == k_slabs - 1) & (kk == nki - 1))(lambda: finish(acc_ref[...]))

    grid = (m_dim // tm, n_slabs, nji, k_slabs, nki)
    scratch = [] if one_k_step else [pltpu.VMEM((tm, tn), F32)]
    out = pl.pallas_call(
        body, name=name, grid=grid, in_specs=specs, out_specs=o_specs, out_shape=o_shapes, scratch_shapes=scratch,
        compiler_params=pltpu.CompilerParams(
            dimension_semantics=("parallel", "parallel", "parallel", "arbitrary", "arbitrary"),
            vmem_limit_bytes=VMEM_LIMIT),
    )(*args)
    return out[0] if n_out == 1 else out


def _stack_lanes(x, n):
    w = x.shape[1] // n
    return jnp.stack([x[:, i * w:(i + 1) * w] for i in range(n)])


def _stack_rows(x, n):
    w = x.shape[0] // n
    return jnp.stack([x[i * w:(i + 1) * w, :] for i in range(n)])


def rwkv_scan_fwd(r, lw, k, v, kap, b):
    t = r.shape[0]
    c, hps, hd = min(RWKV_CHUNK, t), RWKV_HEADS_PER_STEP, RWKV_HEAD_DIM
    nc, ng, wl = t // c, RWKV_HEADS // hps, hps * hd
    spec = pl.BlockSpec((c, wl), lambda g, ci: (ci, g))

    def body(r_ref, lw_ref, k_ref, v_ref, kap_ref, b_ref, y_ref, ck_ref, st_ref):
        @pl.when(pl.program_id(1) == 0)
        def _():
            st_ref[...] = jnp.zeros_like(st_ref)

        st = st_ref[...]
        ck_ref[...] = st
        ins = [x[...] for x in (r_ref, lw_ref, k_ref, v_ref, kap_ref, b_ref)]
        y, st1 = rwkv_chunk_fn(_stack_rows(st, hps), *[_stack_lanes(x, hps) for x in ins])
        y_ref[...] = jnp.concatenate([y[h] for h in range(hps)], axis=-1)
        st_ref[...] = jnp.concatenate([st1[h] for h in range(hps)], axis=0)

    return pl.pallas_call(
        body, name="rwkv_scan_fwd", grid=(ng, nc), in_specs=[spec] * 6,
        out_specs=[spec, pl.BlockSpec((None, wl, hd), lambda g, ci: (ci, g, 0))],
        out_shape=[jax.ShapeDtypeStruct((t, RWKV_WIDTH), F32), jax.ShapeDtypeStruct((nc, RWKV_WIDTH, hd), F32)],
        scratch_shapes=[pltpu.VMEM((wl, hd), F32)], compiler_params=_params(),
    )(r, lw, k, v, kap, b)


def rwkv_scan_bwd(r, lw, k, v, kap, b, ck, dy):
    t = r.shape[0]
    c, hps, hd = min(RWKV_CHUNK, t), RWKV_HEADS_PER_STEP, RWKV_HEAD_DIM
    nc, ng, wl = t // c, RWKV_HEADS // hps, hps * hd
    spec = pl.BlockSpec((c, wl), lambda g, ci: (nc - 1 - ci, g))

    def body(r_ref, lw_ref, k_ref, v_ref, kap_ref, b_ref, ck_ref, dy_ref, *rest):
        out_refs, dst_ref = rest[:6], rest[6]

        @pl.when(pl.program_id(1) == 0)
        def _():
            dst_ref[...] = jnp.zeros_like(dst_ref)

        ins = [x[...] for x in (r_ref, lw_ref, k_ref, v_ref, kap_ref, b_ref)]
        dyv, ck, dst = dy_ref[...].astype(F32), ck_ref[...], dst_ref[...]
        _, vjp = jax.vjp(rwkv_chunk_fn, _stack_rows(ck, hps), *[_stack_lanes(x, hps) for x in ins])
        grads = vjp((_stack_lanes(dyv, hps), _stack_rows(dst, hps)))
        dst_ref[...] = jnp.concatenate([grads[0][h] for h in range(hps)], axis=0)
        for j in range(6):
            out_refs[j][...] = jnp.concatenate([grads[1 + j][h] for h in range(hps)], axis=-1).astype(BF16)

    return pl.pallas_call(
        body, name="rwkv_scan_bwd", grid=(ng, nc),
        in_specs=[spec] * 6 + [pl.BlockSpec((None, wl, hd), lambda g, ci: (nc - 1 - ci, g, 0)), spec],
        out_specs=[spec] * 6, out_shape=[jax.ShapeDtypeStruct((t, RWKV_WIDTH), BF16)] * 6,
        scratch_shapes=[pltpu.VMEM((wl, hd), F32)], compiler_params=_params(),
    )(r, lw, k, v, kap, b, ck, dy)


def _ssd_specs(q, blk):
    gw = SSD_WIDTH // 2
    return [pl.BlockSpec((q, gw), lambda g, ci: (blk(ci), g)),
            pl.BlockSpec((q, SSD_STATE), lambda g, ci: (blk(ci), g)),
            pl.BlockSpec((q, SSD_STATE), lambda g, ci: (blk(ci), g)),
            pl.BlockSpec((q, LANES), lambda g, ci: (blk(ci), 0)),
            pl.BlockSpec((1, LANES), lambda g, ci: (0, 0)),
            pl.BlockSpec((1, LANES), lambda g, ci: (0, 0))]


def ssd_scan_fwd(xs, bm, cm, dt, a_log, d_skip):
    t = xs.shape[0]
    q = min(SSD_CHUNK, t)
    nc, gw = t // q, SSD_WIDTH // 2

    def body(xs_ref, bm_ref, cm_ref, dt_ref, al_ref, d_ref, y_ref, ck_ref, h_ref):
        @pl.when(pl.program_id(1) == 0)
        def _():
            h_ref[...] = jnp.zeros_like(h_ref)

        ck_ref[...] = h_ref[...]
        args = (h_ref[...], xs_ref[...], bm_ref[...], cm_ref[...], dt_ref[...], al_ref[...], d_ref[...])
        g = pl.program_id(0)

        @pl.when(g == 0)
        def _():
            y, h1 = ssd_chunk_fn(0, *args)
            y_ref[...] = y
            h_ref[...] = h1

        @pl.when(g == 1)
        def _():
            y, h1 = ssd_chunk_fn(1, *args)
            y_ref[...] = y
            h_ref[...] = h1

    return pl.pallas_call(
        body, name="ssd_scan_fwd", grid=(2, nc), in_specs=_ssd_specs(q, lambda ci: ci),
        out_specs=[pl.BlockSpec((q, gw), lambda g, ci: (ci, g)),
                   pl.BlockSpec((None, gw, SSD_STATE), lambda g, ci: (ci, g, 0))],
        out_shape=[jax.ShapeDtypeStruct((t, SSD_WIDTH), F32), jax.ShapeDtypeStruct((nc, SSD_WIDTH, SSD_STATE), F32)],
        scratch_shapes=[pltpu.VMEM((gw, SSD_STATE), F32)], compiler_params=_params(),
    )(xs, bm, cm, dt, a_log, d_skip)


def ssd_scan_bwd(xs, bm, cm, dt, a_log, d_skip, ck, dy):
    t = xs.shape[0]
    q = min(SSD_CHUNK, t)
    nc, gw = t // q, SSD_WIDTH // 2
    rev = lambda ci: nc - 1 - ci

    def body(xs_ref, bm_ref, cm_ref, dt_ref, al_ref, d_ref, ck_ref, dy_ref,
             dxs_ref, dbm_ref, dcm_ref, ddt_ref, dal_ref, dd_ref, dh_ref):
        g, ci = pl.program_id(0), pl.program_id(1)

        @pl.when(ci == 0)
        def _():
            dh_ref[...] = jnp.zeros_like(dh_ref)

        @pl.when((ci == 0) & (g == 0))
        def _():
            dal_ref[...] = jnp.zeros_like(dal_ref)
            dd_ref[...] = jnp.zeros_like(dd_ref)

        args = (ck_ref[...], xs_ref[...], bm_ref[...], cm_ref[...], dt_ref[...], al_ref[...], d_ref[...])

        def run(group):
            _, vjp = jax.vjp(functools.partial(ssd_chunk_fn, group), *args)
            dh0, dxs, dbm, dcm, ddt, dal, dd = vjp((dy_ref[...].astype(F32), dh_ref[...]))
            dh_ref[...] = dh0
            dxs_ref[...] = dxs.astype(BF16)
            dbm_ref[...] = dbm.astype(BF16)
            dcm_ref[...] = dcm.astype(BF16)
            ddt_ref[...] = ddt
            dal_ref[...] += dal
            dd_ref[...] += dd

        pl.when(g == 0)(lambda: run(0))
        pl.when(g == 1)(lambda: run(1))

    in_specs = _ssd_specs(q, rev) + [pl.BlockSpec((None, gw, SSD_STATE), lambda g, ci: (rev(ci), g, 0)),
                                     pl.BlockSpec((q, gw), lambda g, ci: (rev(ci), g))]
    return pl.pallas_call(
        body, name="ssd_scan_bwd", grid=(2, nc), in_specs=in_specs,
        out_specs=[pl.BlockSpec((q, gw), lambda g, ci: (rev(ci), g)),
                   pl.BlockSpec((q, SSD_STATE), lambda g, ci: (rev(ci), g)),
                   pl.BlockSpec((q, SSD_STATE), lambda g, ci: (rev(ci), g)),
                   pl.BlockSpec((None, q, LANES), lambda g, ci: (g, rev(ci), 0)),
                   pl.BlockSpec((1, LANES), lambda g, ci: (0, 0)),
                   pl.BlockSpec((1, LANES), lambda g, ci: (0, 0))],
        out_shape=[jax.ShapeDtypeStruct((t, SSD_WIDTH), BF16), jax.ShapeDtypeStruct((t, 2 * SSD_STATE), BF16),
                   jax.ShapeDtypeStruct((t, 2 * SSD_STATE), BF16), jax.ShapeDtypeStruct((2, t, LANES), F32),
                   jax.ShapeDtypeStruct((1, LANES), F32), jax.ShapeDtypeStruct((1, LANES), F32)],
        scratch_shapes=[pltpu.VMEM((gw, SSD_STATE), F32)], compiler_params=_params(),
    )(xs, bm, cm, dt, a_log, d_skip, ck, dy)


def loss_and_grad(x, tgt, g, tm):
    t, d = x.shape
    tm = min(tm, t)
    nb = t // tm

    def body(x_ref, t_ref, g_ref, loss_ref, dx_ref, dg_ref):
        @pl.when(pl.program_id(0) == 0)
        def _():
            loss_ref[...] = jnp.zeros_like(loss_ref)
            dg_ref[...] = jnp.zeros_like(dg_ref)

        val, vjp = jax.vjp(loss_fn, x_ref[...], t_ref[...], g_ref[...])
        dx, _, dg = vjp(jnp.ones((1, 1), F32))
        loss_ref[...] += jnp.broadcast_to(val, loss_ref.shape)
        dx_ref[...] = dx
        dg_ref[...] += dg

    row = pl.BlockSpec((tm, d), lambda i: (i, 0))
    one = pl.BlockSpec((1, d), lambda i: (0, 0))
    return pl.pallas_call(
        body, name="loss_and_grad", grid=(nb,), in_specs=[row, row, one],
        out_specs=[pl.BlockSpec((SUBLANES, LANES), lambda i: (0, 0)), row, one],
        out_shape=[jax.ShapeDtypeStruct((SUBLANES, LANES), F32), jax.ShapeDtypeStruct((t, d), F32),
                   jax.ShapeDtypeStruct((1, d), F32)],
        compiler_params=_params(),
    )(x, tgt, g)


def adamw(name, recv, w, m, v, dep=None):
    rows, cols = w.shape
    recv_block_bytes = 4 * 1024 * 1024
    tm = _pick(rows, [c for c in (256, 128, 64, 32, 16, 8) if N_DEV * c * cols * 4 <= recv_block_bytes])
    c1 = 1.0 / (1.0 - ADAM_B1 ** ADAM_STEP)
    c2 = 1.0 / (1.0 - ADAM_B2 ** ADAM_STEP)

    n_dep = 0 if dep is None else 1

    def body(recv_ref, w_ref, m_ref, v_ref, *rest):
        g_ref, d_ref, nm_ref, nv_ref = rest[n_dep:]
        g = recv_ref[0].astype(F32)
        for p in range(1, N_DEV):
            g = g + recv_ref[p].astype(F32)
        nm = ADAM_B1 * m_ref[...] + (1.0 - ADAM_B1) * g
        nv = ADAM_B2 * v_ref[...] + (1.0 - ADAM_B2) * jnp.square(g)
        g_ref[...] = g
        nm_ref[...] = nm
        nv_ref[...] = nv
        d_ref[...] = -ADAM_LR * ((nm * c1) / (jnp.sqrt(nv * c2) + ADAM_EPS) + ADAM_WD * w_ref[...])

    blk = pl.BlockSpec((tm, cols), lambda i: (i, 0))
    return pl.pallas_call(
        body, name=name, grid=(rows // tm,),
        in_specs=[pl.BlockSpec((N_DEV, tm, cols), lambda i: (0, i, 0)), blk, blk, blk]
        + [pl.BlockSpec(memory_space=pl.ANY)] * n_dep,
        out_specs=[blk] * 4, out_shape=[jax.ShapeDtypeStruct((rows, cols), F32)] * 4,
        compiler_params=_params(),
    )(recv, w, m, v, *([] if dep is None else [dep]))


def _mesh_pos():
    return lax.axis_index("x"), lax.axis_index("y"), lax.axis_index("c")


def _peer(pos, mask):
    x, y, c = pos
    return (1 - x if mask & 4 else x, 1 - y if mask & 2 else y, 1 - c if mask & 1 else c)


def _linear(pos):
    return 4 * pos[0] + 2 * pos[1] + pos[2]


class Exchange:
    MASKS = {"gather": (1, 2, 3, 4, 5, 6, 7), "scatter": (1, 2, 3, 4, 5, 6, 7), "gather_chips": (1, 2, 4, 6),
             "forward": (2, 4, 6)}

    def __init__(self, xs, kind, lands=None):
        self.kind, self.masks = kind, self.MASKS[kind]
        self.xs = [] if kind == "forward" else list(xs)
        if kind == "forward":
            self.land_shape = [jax.ShapeDtypeStruct(l.shape, l.dtype) for l in lands]
        elif kind == "scatter":
            self.land_shape = [jax.ShapeDtypeStruct(x.shape, x.dtype) for x in xs]
        else:
            self.land_shape = [jax.ShapeDtypeStruct((N_DEV,) + x.shape, x.dtype) for x in xs]
        self.n = len(self.land_shape)
        copies = self.n * len(self.masks)
        self.sems = [pltpu.SemaphoreType.DMA((copies,)), pltpu.SemaphoreType.DMA((copies,)),
                     pltpu.SemaphoreType.DMA((self.n,))]

    def _copies(self, ins, outs, sems, landing):
        send_sems, recv_sems, local_sems = sems
        me = _mesh_pos()
        me_lin = _linear(me)
        local, remote = [], []
        for ti in range(self.n):
            if self.kind != "forward":
                src_mine = ins[ti].at[me_lin] if self.kind == "scatter" else ins[ti]
                local.append(pltpu.make_async_copy(src_mine, outs[ti].at[me_lin], local_sems.at[ti]))
            for j, mask in enumerate(self.masks):
                if self.kind == "forward":
                    peer = _peer(me, 1)
                    src = outs[ti].at[_linear(_peer(me, mask))]
                    dst = outs[ti].at[_linear(_peer(me, mask ^ 1 if landing else mask))]
                else:
                    peer = _peer(me, mask)
                    src = ins[ti].at[_linear(peer)] if self.kind == "scatter" else ins[ti]
                    dst = outs[ti].at[_linear(peer) if landing else me_lin]
                sem_index = ti * len(self.masks) + j
                remote.append(pltpu.make_async_remote_copy(
                    src_ref=src, dst_ref=dst, send_sem=send_sems.at[sem_index], recv_sem=recv_sems.at[sem_index],
                    device_id=peer, device_id_type=pl.DeviceIdType.MESH))
        return local, remote

    def start(self, ins, outs, sems):
        local, remote = self._copies(ins, outs, sems, landing=False)
        for cp in local + remote:
            cp.start()

    def finish(self, ins, outs, sems):
        local, remote = self._copies(ins, outs, sems, landing=True)
        for cp in remote:
            cp.wait_recv()
        for cp in remote:
            cp.wait_send()
        for cp in local:
            cp.wait()


def exchange_start(name, xs, kind, dep=None, lands=None):
    ex = Exchange(xs, kind, lands)
    hbm = pl.BlockSpec(memory_space=pltpu.HBM)
    sem = pl.BlockSpec(memory_space=pltpu.SEMAPHORE)
    if lands is None:
        lands = [lax.empty(s.shape, s.dtype) for s in ex.land_shape]
    n_src, n = len(ex.xs), ex.n
    n_inputs = n_src + n + (0 if dep is None else 1)

    def body(*refs):
        ins, lnd, sems, token = refs[:n_src], refs[n_src:n_src + n], refs[n_inputs:n_inputs + 3], refs[-1]
        ex.start(ins, lnd, sems)
        token[...] = jnp.zeros_like(token)

    res = pl.pallas_call(
        body, name=name, in_specs=[hbm] * (n_src + n) + ([] if dep is None else [pl.BlockSpec(memory_space=pl.ANY)]),
        out_specs=[sem] * 3 + [hbm] * (n_src + n) + [pl.BlockSpec(memory_space=pltpu.VMEM)],
        out_shape=ex.sems + [pltpu.HBM(x.shape, x.dtype) for x in ex.xs]
        + [pltpu.HBM(s.shape, s.dtype) for s in ex.land_shape] + [jax.ShapeDtypeStruct((SUBLANES, LANES), F32)],
        input_output_aliases={i: 3 + i for i in range(n_src + n)},
        compiler_params=pltpu.CompilerParams(has_side_effects=pltpu.SideEffectType.DATAFLOW_SIDE_EFFECTING),
    )(*[pltpu.with_memory_space_constraint(x, pltpu.HBM) for x in ex.xs + list(lands)],
      *([] if dep is None else [dep]))
    return (ex, res[:3], res[3:3 + n_src], res[3 + n_src:3 + n_src + n]), res[-1]


def exchange_wait(name, handles, after):
    ex, sems, srcs, lands = handles
    n_src, n = len(srcs), len(lands)
    hbm = pl.BlockSpec(memory_space=pltpu.HBM)
    sem = pl.BlockSpec(memory_space=pltpu.SEMAPHORE)

    def body(*refs):
        ins, lnd, sem_refs = refs[:n_src], refs[n_src:n_src + n], refs[n_src + n:n_src + n + 3]
        ex.finish(ins, lnd, sem_refs)

    res = pl.pallas_call(
        body, name=name, in_specs=[hbm] * (n_src + n) + [sem] * 3 + [pl.BlockSpec(memory_space=pl.ANY)],
        out_specs=[hbm] * (n_src + n),
        out_shape=[pltpu.HBM(x.shape, x.dtype) for x in srcs] + [pltpu.HBM(x.shape, x.dtype) for x in lands],
        input_output_aliases={i: i for i in range(n_src + n)},
        compiler_params=pltpu.CompilerParams(has_side_effects=pltpu.SideEffectType.DATAFLOW_SIDE_EFFECTING),
    )(*srcs, *lands, *sems, after)
    return res[n_src:]


def forward_start(name, chip_gather, after):
    lands = exchange_wait(name + "_wait", chip_gather, after)
    return exchange_start(name + "_forward_start", [], "forward", lands=lands)


_Z = (0, 1024)
_XBC = (1024, 2560)
_DT = (2560, 2576)
_RKV = (2576, 5648)
_PW = (5648, 5744)
_PA = (5744, 5840)
_PG = (5840, 6096)
D_IN = 6096

_SMALL = ("norm_mix_g", "ssd_conv_b", "ssd_dt_bias", "ssd_a_log", "ssd_d", "ssd_norm_g", "rwkv_mu", "rwkv_w0",
          "rwkv_a0", "rwkv_k_k", "rwkv_k_a", "rwkv_r_k", "rwkv_ln_w", "rwkv_ln_b", "norm_x_g", "norm_mem_g",
          "norm_ffn_g", "final_norm_g")
_WEIGHTS = ("norm_mix_g", "w_in", "ssd_conv_w", "ssd_conv_b", "ssd_dt_bias", "ssd_a_log", "ssd_d", "ssd_norm_g",
            "rwkv_mu", "rwkv_w0", "rwkv_w2", "rwkv_a0", "rwkv_a2", "rwkv_g2", "rwkv_k_k", "rwkv_k_a", "rwkv_r_k",
            "rwkv_ln_w", "rwkv_ln_b", "w_out", "norm_x_g", "norm_mem_g", "xattn_wq", "xattn_wk", "xattn_wv",
            "xattn_wo", "norm_ffn_g", "ffn_w1", "ffn_w2", "final_norm_g")


def _pad_lanes(x, width=LANES):
    return jnp.pad(x, ((0, 0), (0, width - x.shape[1])))


def _pack_small(vals):
    flat = jnp.concatenate([vals[n].reshape(-1) for n in _SMALL])
    rows = -(-flat.shape[0] // (LANES * SUBLANES)) * SUBLANES
    return jnp.pad(flat, (0, rows * LANES - flat.shape[0])).reshape(rows, LANES)


def _unpack_small(packed, shapes):
    flat = packed.reshape(-1)
    out, pos = {}, 0
    for n in _SMALL:
        size = 1
        for s in shapes[n]:
            size *= s
        out[n] = flat[pos:pos + size].reshape(shapes[n])
        pos += size
    return out


def _cols(w, rng):
    return w[:, rng[0]:rng[1]]


def kernel(x, mem, norm_mix_g, w_in, ssd_conv_w, ssd_conv_b, ssd_dt_bias, ssd_a_log, ssd_d, ssd_norm_g, rwkv_mu, rwkv_w0, rwkv_w2, rwkv_a0, rwkv_a2, rwkv_g2, rwkv_k_k, rwkv_k_a, rwkv_r_k, rwkv_ln_w, rwkv_ln_b, w_out, norm_x_g, norm_mem_g, xattn_wq, xattn_wk, xattn_wv, xattn_wo, norm_ffn_g, ffn_w1, ffn_w2, final_norm_g, loss_target, m_norm_mix_g, m_w_in, m_ssd_conv_w, m_ssd_conv_b, m_ssd_dt_bias, m_ssd_a_log, m_ssd_d, m_ssd_norm_g, m_rwkv_mu, m_rwkv_w0, m_rwkv_w2, m_rwkv_a0, m_rwkv_a2, m_rwkv_g2, m_rwkv_k_k, m_rwkv_k_a, m_rwkv_r_k, m_rwkv_ln_w, m_rwkv_ln_b, m_w_out, m_norm_x_g, m_norm_mem_g, m_xattn_wq, m_xattn_wk, m_xattn_wv, m_xattn_wo, m_norm_ffn_g, m_ffn_w1, m_ffn_w2, m_final_norm_g, v_norm_mix_g, v_w_in, v_ssd_conv_w, v_ssd_conv_b, v_ssd_dt_bias, v_ssd_a_log, v_ssd_d, v_ssd_norm_g, v_rwkv_mu, v_rwkv_w0, v_rwkv_w2, v_rwkv_a0, v_rwkv_a2, v_rwkv_g2, v_rwkv_k_k, v_rwkv_k_a, v_rwkv_r_k, v_rwkv_ln_w, v_rwkv_ln_b, v_w_out, v_norm_x_g, v_norm_mem_g, v_xattn_wq, v_xattn_wk, v_xattn_wv, v_xattn_wo, v_norm_ffn_g, v_ffn_w1, v_ffn_w2, v_final_norm_g):
    given = dict(locals())
    wts = {n: given[n] for n in _WEIGHTS}
    mom_m = {n: given["m_" + n] for n in _WEIGHTS}
    mom_v = {n: given["v_" + n] for n in _WEIGHTS}
    d = D_MODEL
    xt, memt, tgt = x[0], mem[0], loss_target[0]
    tm = 256

    big = {"w_in": w_in[0], "w_out": w_out[0], "xattn_wq": xattn_wq[0], "xattn_wk": xattn_wk[0],
           "xattn_wv": xattn_wv[0], "xattn_wo": xattn_wo[0], "ffn_w1": ffn_w1[0], "ffn_w2": ffn_w2[0]}
    small_sh = {"ssd_conv_w": ssd_conv_w.reshape(4, -1), "rwkv_w2": rwkv_w2[0], "rwkv_a2": rwkv_a2[0],
                "rwkv_g2": rwkv_g2[0]}
    cast_one = lambda n, deps=(): rowwise_fwd("cast_" + n, cast_fn, [big[n]], [], [(big[n].shape[1], BF16)], 256,
                                              deps=deps)[0]
    gather_in, token_in = exchange_start("gather_in_start", [cast_one("w_in")] + list(small_sh.values()), "gather_chips")
    cast = {n: cast_one(n, deps=[token_in]) for n in big if n != "w_in"}
    late_a = ("w_out", "xattn_wq", "xattn_wk", "xattn_wv", "xattn_wo")
    late_b = ("ffn_w1", "ffn_w2")
    gather_a, token_a = exchange_start("gather_attn_start", [cast[n] for n in late_a], "gather_chips", dep=token_in)
    gather_b, token_b = exchange_start("gather_ffn_start", [cast[n] for n in late_b], "gather_chips", dep=token_a)
    (h1,) = rowwise_fwd("norm_mix", rmsnorm_fn, [xt], [norm_mix_g], [(d, BF16)], tm, deps=[token_b])
    forward_in, token_in = forward_start("gather_in", gather_in, after=h1)
    gathered = exchange_wait("gather_in_forward_wait", forward_in, after=token_in)
    g_big = {"w_in": gathered[0]}
    g_small = dict(zip(small_sh, gathered[1:]))

    w_in_full = jnp.transpose(g_big["w_in"], (1, 0, 2)).reshape(d, D_IN)
    w_z, w_xbc, w_rkv, w_pg = (_cols(w_in_full, r) for r in (_Z, _XBC, _RKV, _PG))
    w_sm = jnp.concatenate([_pad_lanes(_cols(w_in_full, r)) for r in (_PW, _PA, _DT)], axis=1)
    unshard_cols = lambda g: jnp.transpose(g, (1, 0, 2)).reshape(g.shape[1], -1)
    conv_w_f = unshard_cols(g_small["ssd_conv_w"])
    pad_rows = lambda a: jnp.pad(a, ((0, LANES - a.shape[0]), (0, 0)))
    w2p, a2p = pad_rows(unshard_cols(g_small["rwkv_w2"])), pad_rows(unshard_cols(g_small["rwkv_a2"]))
    g2_f = unshard_cols(g_small["rwkv_g2"])

    mu = rwkv_mu
    mu_rkv, mu_pg = mu[:, :3072], mu[:, 3264:3520]
    mu_pwa = jnp.concatenate([_pad_lanes(mu[:, 3072:3168]), _pad_lanes(mu[:, 3168:3264])], axis=1)
    dt_bias_p, a_log_p, d_p = _pad_lanes(ssd_dt_bias), _pad_lanes(ssd_a_log), _pad_lanes(ssd_d)
    r_k_row = rwkv_r_k.reshape(1, RWKV_WIDTH)
    g_final = final_norm_g.reshape(1, d)

    u_z = mm("in_z", h1, w_z, "nn")
    u_xbc = mm("in_xbc", h1, w_xbc, "nn")
    u_rkv = mm("in_rkv", h1, w_rkv, "nn")
    u_pg = mm("in_pg", h1, w_pg, "nn")
    u_sm = mm("in_small", h1, w_sm, "nn")

    ssd_pre_rows = lambda: [Rows(u_xbc, shifts=(1, 2, 3)), Rows(u_sm, LANES, 2)]
    ssd_pre_params = [conv_w_f, ssd_conv_b, dt_bias_p]
    xs, bm, cm, dt = rowwise_fwd("ssd_pre", ssd_pre_fn, ssd_pre_rows(), ssd_pre_params,
                                 [(SSD_WIDTH, F32), (256, F32), (256, F32), (LANES, F32)], tm)
    y_scan, ssd_ck = ssd_scan_fwd(xs, bm, cm, dt, a_log_p, d_p)
    (y_ssd,) = rowwise_fwd("ssd_post", ssd_post_fn, [y_scan, u_z], [ssd_norm_g], [(SSD_WIDTH, BF16)], tm)

    rwkv_pre_rows = lambda: [Rows(u_rkv, shifts=(1,)), Rows(u_pg, shifts=(1,)), Rows(u_sm, 2 * LANES, 0, shifts=(1,))]
    rwkv_pre_params = [mu_rkv, mu_pg, mu_pwa, rwkv_w0, w2p, rwkv_a0, a2p, g2_f, rwkv_k_k, rwkv_k_a]
    forward_a, token_a = forward_start("gather_attn", gather_a, after=y_ssd)
    r_, lw_, k_, v_, kap_, b_, gate_ = rowwise_fwd("rwkv_pre", rwkv_pre_fn, rwkv_pre_rows(), rwkv_pre_params,
                                                   [(RWKV_WIDTH, F32)] * 7, 128, deps=[token_a])
    ys_r, rwkv_ck = rwkv_scan_fwd(r_, lw_, k_, v_, kap_, b_)
    forward_b, token_b = forward_start("gather_ffn", gather_b, after=ys_r)
    g_big.update(zip(late_a, exchange_wait("gather_attn_forward_wait", forward_a, after=token_b)))
    w_out_f = g_big["w_out"].reshape(d, d)
    wq_f, wk_f, wv_f, wo_f = (g_big[n].reshape(d, d) for n in ("xattn_wq", "xattn_wk", "xattn_wv", "xattn_wo"))
    rwkv_post_params = [rwkv_ln_w, rwkv_ln_b, r_k_row]
    (y_rwkv,) = rowwise_fwd("rwkv_post", rwkv_post_fn, [ys_r, r_, k_, v_, gate_], rwkv_post_params,
                            [(RWKV_WIDTH, BF16)], tm)
    ycat = jnp.concatenate([y_ssd, y_rwkv], axis=1)
    x1 = mm("out_proj", ycat, w_out_f, "nn", res=xt)

    (h2,) = rowwise_fwd("norm_x", rmsnorm_fn, [x1], [norm_x_g], [(d, BF16)], tm)
    (mn,) = rowwise_fwd("norm_mem", rmsnorm_fn, [memt], [norm_mem_g], [(d, BF16)], tm)
    q = mm("xattn_q", h2, wq_f, "nn", out_dtype=BF16)
    kx = mm("xattn_k", mn, wk_f, "nn")
    vx = mm("xattn_v", mn, wv_f, "nn")
    (o,) = rowwise_fwd("xattn", attn_fn, [q], [kx, vx], [(d, BF16)], tm)
    x2 = mm("xattn_o", o, wo_f, "nn", res=x1)

    (h3,) = rowwise_fwd("norm_ffn", rmsnorm_fn, [x2], [norm_ffn_g], [(d, BF16)], tm)
    w1_s, w2_g = exchange_wait("gather_ffn_forward_wait", forward_b, after=h3)
    w2_f = w2_g.reshape(D_FF, d)
    relu2_epi = lambda acc: (jnp.square(jnp.maximum(acc, 0.0)), jnp.maximum(acc, 0.0))
    hid, relu_a = mm("ffn_1", h3, w1_s, "nn", b_slabs=N_DEV, epi=relu2_epi, out_dtypes=[BF16, BF16])
    x3 = mm("ffn_2", hid, w2_f, "nn", res=x2)

    loss_blk, dx3, dg_final = loss_and_grad(x3, tgt, g_final, tm)

    grads = {}
    grads["ffn_w2"] = mm("d_ffn_w2", hid, dx3, "tn", out_dtype=BF16).reshape(N_DEV, D_FF // N_DEV, d)
    sc_w2, tok = exchange_start("scatter_ffn_w2_start", [grads["ffn_w2"]], "scatter")
    da = mm("d_hid", dx3, w2_f, "nt", dep=tok, epi=lambda acc, ra: (2.0 * acc * ra,), extras=[relu_a],
            out_dtypes=[BF16])
    grads["ffn_w1"] = mm("d_ffn_w1", h3, da, "tn", out_dtype=BF16, out_slabs=N_DEV)
    sc_w1, tok = exchange_start("scatter_ffn_w1_start", [grads["ffn_w1"]], "scatter")
    dh3 = mm("d_h3", da, w1_s, "nt", out_dtype=BF16, b_slabs=N_DEV, dep=tok)
    (dx2,), (dg_ffn,) = rowwise_bwd("norm_ffn_bwd", rmsnorm_fn, [x2], [norm_ffn_g], [[dh3]], tm, [F32], row_add=[dx3])

    grads["xattn_wo"] = mm("d_wo", o, dx2, "tn", out_dtype=BF16).reshape(N_DEV, d // N_DEV, d)
    sc_wo, tok = exchange_start("scatter_wo_start", [grads["xattn_wo"]], "scatter")
    d_o = mm("d_o", dx2, wo_f, "nt", out_dtype=BF16, dep=tok)
    (dq,), (dkx, dvx) = rowwise_bwd("xattn_bwd", attn_fn, [q], [kx, vx], [[d_o]], tm, [BF16])
    grads["xattn_wq"] = mm("d_wq", h2, dq, "tn", out_dtype=BF16).reshape(N_DEV, d // N_DEV, d)
    grads["xattn_wk"] = mm("d_wk", mn, dkx, "tn", out_dtype=BF16).reshape(N_DEV, d // N_DEV, d)
    grads["xattn_wv"] = mm("d_wv", mn, dvx, "tn", out_dtype=BF16).reshape(N_DEV, d // N_DEV, d)
    qkv = ("xattn_wq", "xattn_wk", "xattn_wv")
    sc_qkv, tok = exchange_start("scatter_qkv_start", [grads[n] for n in qkv], "scatter")
    dmn = mm("d_mn_v", dvx, wv_f, "nt", res=mm("d_mn_k", dkx, wk_f, "nt", dep=tok))
    _, (dg_mem,) = rowwise_bwd("norm_mem_bwd", rmsnorm_fn, [memt], [norm_mem_g], [[dmn]], tm, [None])
    dh2 = mm("d_h2", dq, wq_f, "nt", out_dtype=BF16, dep=dg_mem)
    (dx1,), (dg_x,) = rowwise_bwd("norm_x_bwd", rmsnorm_fn, [x1], [norm_x_g], [[dh2]], tm, [F32], row_add=[dx2])

    grads["w_out"] = mm("d_w_out", ycat, dx1, "tn", out_dtype=BF16).reshape(N_DEV, d // N_DEV, d)
    sc_wout, tok = exchange_start("scatter_w_out_start", [grads["w_out"]], "scatter")
    d_ycat = mm("d_ycat", dx1, w_out_f, "nt", out_dtype=BF16, dep=tok)

    (d_ys, d_r1, d_k1, d_v1, d_gate), (dln_w, dln_b, dr_k) = rowwise_bwd(
        "rwkv_post_bwd", rwkv_post_fn, [ys_r, r_, k_, v_, gate_], rwkv_post_params,
        [[Rows(d_ycat, RWKV_WIDTH, 1)]], tm, [BF16] * 5)
    d_r2, d_lw, d_k2, d_v2, d_kap, d_b = rwkv_scan_bwd(r_, lw_, k_, v_, kap_, b_, rwkv_ck, d_ys)
    (du_rkv, du_pg, du_pwa), rwkv_pg = rowwise_bwd(
        "rwkv_pre_bwd", rwkv_pre_fn, rwkv_pre_rows(), rwkv_pre_params,
        [[d_r1, d_r2], [d_lw], [d_k1, d_k2], [d_v1, d_v2], [d_kap], [d_b], [d_gate]], 128, [BF16] * 3)
    dmu_rkv, dmu_pg, dmu_pwa, dw0, dw2p, da0, da2p, dg2, dk_k, dk_a = rwkv_pg

    (d_yscan, du_z), (dssd_norm_g,) = rowwise_bwd("ssd_post_bwd", ssd_post_fn, [y_scan, u_z], [ssd_norm_g],
                                                  [[Rows(d_ycat, SSD_WIDTH, 0)]], tm, [BF16, BF16])
    dxs, dbm, dcm, ddt2, da_log_p, dd_p = ssd_scan_bwd(xs, bm, cm, dt, a_log_p, d_p, ssd_ck, d_yscan)
    (du_xbc, du_dt), (dconv_w, dconv_b, ddt_bias_p) = rowwise_bwd(
        "ssd_pre_bwd", ssd_pre_fn, ssd_pre_rows(), ssd_pre_params,
        [[dxs], [dbm], [dcm], [ddt2[0], ddt2[1]]], tm, [BF16, BF16])
    du_sm = jnp.concatenate([du_pwa, du_dt], axis=1)

    dw_z = mm("d_w_z", h1, du_z, "tn", out_dtype=BF16)
    dw_xbc = mm("d_w_xbc", h1, du_xbc, "tn", out_dtype=BF16)
    dw_rkv = mm("d_w_rkv", h1, du_rkv, "tn", out_dtype=BF16)
    dw_pg = mm("d_w_pg", h1, du_pg, "tn", out_dtype=BF16)
    dw_sm = mm("d_w_small", h1, du_sm, "tn", out_dtype=BF16)
    dw_in_full = jnp.concatenate([dw_z, dw_xbc, dw_sm[:, 256:272], dw_rkv, dw_sm[:, 0:96], dw_sm[:, 128:224], dw_pg], axis=1)
    to_slabs = lambda g: jnp.transpose(g.reshape(g.shape[0], N_DEV, -1), (1, 0, 2))
    grads["w_in"] = to_slabs(dw_in_full)
    grads["ssd_conv_w"] = to_slabs(dconv_w)
    grads["rwkv_w2"] = to_slabs(dw2p[:96])
    grads["rwkv_a2"] = to_slabs(da2p[:96])
    grads["rwkv_g2"] = to_slabs(dg2)
    tail = ("w_in", "ssd_conv_w", "rwkv_w2", "rwkv_a2", "rwkv_g2")
    sc_tail, tok = exchange_start("scatter_tail_start", [grads[n] for n in tail], "scatter")
    dh1 = mm("d_h1_z", du_z, w_z, "nt", dep=tok)
    dh1 = mm("d_h1_xbc", du_xbc, w_xbc, "nt", res=dh1)
    dh1 = mm("d_h1_rkv", du_rkv, w_rkv, "nt", res=dh1)
    dh1 = mm("d_h1_pg", du_pg, w_pg, "nt", res=dh1)
    dh1 = mm("d_h1_small", du_sm, w_sm, "nt", res=dh1)
    (dx,), (dg_mix,) = rowwise_bwd("norm_mix_bwd", rmsnorm_fn, [xt], [norm_mix_g], [[dh1]], tm, [F32], row_add=[dx1])

    dmu =jnp.concatenate([dmu_rkv, dmu_pwa[:, 0:96], dmu_pwa[:, 128:224], dmu_pg], axis=1)
    small_grads = {
        "norm_mix_g": dg_mix, "ssd_conv_b": dconv_b, "ssd_dt_bias": ddt_bias_p[:, :16], "ssd_a_log": da_log_p[:, :16],
        "ssd_d": dd_p[:, :16], "ssd_norm_g": dssd_norm_g, "rwkv_mu": dmu, "rwkv_w0": dw0, "rwkv_a0": da0,
        "rwkv_k_k": dk_k, "rwkv_k_a": dk_a, "rwkv_r_k": dr_k, "rwkv_ln_w": dln_w, "rwkv_ln_b": dln_b,
        "norm_x_g": dg_x, "norm_mem_g": dg_mem, "norm_ffn_g": dg_ffn, "final_norm_g": dg_final}

    gather_small, tok = exchange_start("gather_small_start", [_pack_small(small_grads)], "gather")
    received = {}
    for names, handle in ((("ffn_w2",), sc_w2), (("ffn_w1",), sc_w1), (("xattn_wo",), sc_wo), (qkv, sc_qkv),
                          (("w_out",), sc_wout)):
        received.update(zip(names, exchange_wait("scatter_" + names[0] + "_wait", handle, after=tok)))

    out_g, out_d, out_m, out_v = {}, {}, {}, {}

    def run_adamw(n, dep):
        shape = wts[n].shape
        two_d = lambda a: a.reshape(-1, shape[-1])
        res = adamw("adamw_" + n, received[n].reshape(N_DEV, -1, shape[-1]), two_d(wts[n]), two_d(mom_m[n]),
                    two_d(mom_v[n]), dep=dep)
        out_g[n], out_d[n], out_m[n], out_v[n] = (r.reshape(shape) for r in res)
        return res[0]

    last = None
    for n in ("ffn_w2", "ffn_w1", "xattn_wo") + qkv + ("w_out",):
        last = run_adamw(n, last)
    received.update(zip(tail, exchange_wait("scatter_tail_wait", sc_tail, after=last)))
    for n in tail:
        last = run_adamw(n, last)
    (small_all,) = exchange_wait("gather_small_wait", gather_small, after=last)
    res = adamw("adamw_small", small_all, _pack_small(wts), _pack_small(mom_m), _pack_small(mom_v))
    shapes = {n: wts[n].shape for n in _SMALL}
    for dst, packed in zip((out_g, out_d, out_m, out_v), res):
        dst.update(_unpack_small(packed, shapes))

    loss = lax.psum(loss_blk[0, 0], ("x", "y", "c"))
    return (loss, dx[None], *[out_g[n] for n in _WEIGHTS], *[out_d[n] for n in _WEIGHTS],
            *[out_m[n] for n in _WEIGHTS], *[out_v[n] for n in _WEIGHTS])
```

```python
import functools

import jax
import jax.numpy as jnp
from jax import lax
from jax.experimental import pallas as pl
from jax.experimental.pallas import tpu as pltpu

F32 = jnp.float32
BF16 = jnp.bfloat16
HIGHEST = lax.Precision.HIGHEST

N_DEV = 8
D_MODEL = 2048
NORM_EPS = 1e-6
SSD_WIDTH = 1024
SSD_CONV_DIM = 1536
SSD_HEADS = 16
SSD_HEAD_DIM = 64
SSD_STATE = 128
SSD_CHUNK = 128
SSD_HEADS_PER_GROUP = 8
RWKV_WIDTH = 1024
RWKV_HEADS = 16
RWKV_HEAD_DIM = 64
RWKV_LN_EPS = 64e-5
RWKV_CHUNK = 64
RWKV_HEADS_PER_STEP = 16
XATTN_HEADS = 4
XATTN_HEAD_DIM = 512
D_FF = 8192
LANES = 128
SUBLANES = 8
VMEM_LIMIT = 56 * 1024 * 1024

ADAM_LR = 0.001
ADAM_B1 = 0.9
ADAM_B2 = 0.999
ADAM_EPS = 1e-08
ADAM_WD = 0.01
ADAM_STEP = 10

_DN = {"nn": ((1,), (0,)), "nt": ((1,), (1,)), "tn": ((0,), (0,))}


def _dg(a, b, mode, precision=None):
    (ca,), (cb,) = _DN[mode]
    dn = (((ca + 1,), (cb + 1,)), ((0,), (0,))) if a.ndim == 3 else (((ca,), (cb,)), ((), ()))
    return lax.dot_general(a, b, dn, precision=precision, preferred_element_type=F32)


@functools.partial(jax.custom_vjp, nondiff_argnums=(2,))
def bdot(a, b, mode):
    return _dg(a.astype(BF16), b.astype(BF16), mode)


def _bdot_fwd(a, b, mode):
    return bdot(a, b, mode), (a, b)


def _bdot_bwd(mode, res, g):
    a, b = res
    ab, bb, gb = a.astype(BF16), b.astype(BF16), g.astype(BF16)
    if mode == "nn":
        da, db = _dg(gb, bb, "nt"), _dg(ab, gb, "tn")
    elif mode == "nt":
        da, db = _dg(gb, bb, "nn"), _dg(gb, ab, "tn")
    else:
        da, db = _dg(bb, gb, "nt"), _dg(ab, gb, "nn")
    return da.astype(a.dtype), db.astype(b.dtype)


bdot.defvjp(_bdot_fwd, _bdot_bwd)


def fdot(a, b, mode):
    return _dg(a, b, mode, precision=HIGHEST)


def _split3(x):
    hi = x.astype(BF16)
    r1 = x - hi.astype(F32)
    mid = r1.astype(BF16)
    lo = (r1 - mid.astype(F32)).astype(BF16)
    return hi, mid, lo


def _dot01(x, m01):
    hi, mid, _ = _split3(x)
    return _dg(hi, m01, "nn") + _dg(mid, m01, "nn")


def _exact_dot_impl(a, b, mode, exact):
    if exact == "a":
        ae = a.astype(BF16)
        return sum(_dg(ae, part, mode) for part in _split3(b))
    be = b.astype(BF16)
    return sum(_dg(part, be, mode) for part in _split3(a))


@functools.partial(jax.custom_vjp, nondiff_argnums=(2, 3))
def exact_dot(a, b, mode, exact):
    return _exact_dot_impl(a, b, mode, exact)


def _exact_dot_fwd(a, b, mode, exact):
    return _exact_dot_impl(a, b, mode, exact), (a, b)


def _exact_dot_bwd(mode, exact, res, g):
    a, b = res
    if exact == "a":
        db = {"nn": lambda: _exact_dot_impl(a, g, "tn", "a"), "nt": lambda: _exact_dot_impl(g, a, "tn", "b"),
              "tn": lambda: _exact_dot_impl(a, g, "nn", "a")}[mode]()
        return jnp.zeros_like(a), db
    da = {"nn": lambda: _exact_dot_impl(g, b, "nt", "b"), "nt": lambda: _exact_dot_impl(g, b, "nn", "b"),
          "tn": lambda: _exact_dot_impl(b, g, "nt", "a")}[mode]()
    return da, jnp.zeros_like(b)


exact_dot.defvjp(_exact_dot_fwd, _exact_dot_bwd)


def _head_indicator(width, heads, transpose):
    hd = width // heads
    shape = (LANES, width) if transpose else (width, LANES)
    lane = lax.broadcasted_iota(jnp.int32, shape, 1 if not transpose else 0)
    pos = lax.broadcasted_iota(jnp.int32, shape, 0 if not transpose else 1)
    return ((pos >= lane * hd) & (pos < lane * hd + hd)).astype(BF16)


@jax.custom_vjp
def head_sum(x):
    w = x.shape[-1]
    e = _head_indicator(w, w // RWKV_HEAD_DIM, False)
    et = _head_indicator(w, w // RWKV_HEAD_DIM, True)
    return _dot01(_dot01(x, e), et)


head_sum.defvjp(lambda x: (head_sum(x), None), lambda _, g: (head_sum(g),))


def rmsnorm_fn(x, g):
    y = x * lax.rsqrt(jnp.mean(x * x, axis=-1, keepdims=True) + NORM_EPS)
    return ((y * g).astype(BF16),)


def cast_fn(x):
    return (x.astype(BF16),)


def ssd_pre_fn(xbc, xbc1, xbc2, xbc3, dt_raw, conv_w, conv_b, dt_bias):
    c = conv_w[3:4] * xbc + conv_w[2:3] * xbc1 + conv_w[1:2] * xbc2 + conv_w[0:1] * xbc3 + conv_b
    act = c * jax.nn.sigmoid(c)
    dt = jax.nn.softplus(dt_raw + dt_bias)
    return act[:, :SSD_WIDTH], act[:, SSD_WIDTH:SSD_WIDTH + 256], act[:, SSD_WIDTH + 256:], dt


def ssd_post_fn(yscan, z, norm_g):
    y = yscan * (z * jax.nn.sigmoid(z))
    half = SSD_WIDTH // 2
    parts = []
    for g in range(2):
        yg = y[:, g * half:(g + 1) * half]
        parts.append(yg * lax.rsqrt(jnp.mean(yg * yg, axis=-1, keepdims=True) + NORM_EPS))
    return ((jnp.concatenate(parts, axis=-1) * norm_g).astype(BF16),)


def rwkv_pre_fn(rkv, rkv_p, pg, pg_p, pwa, pwa_p, mu_rkv, mu_pg, mu_pwa, w0, w2p, a0, a2p, g2, k_k, k_a):
    w = RWKV_WIDTH
    rkv = rkv + (rkv_p - rkv) * mu_rkv
    pg = pg + (pg_p - pg) * mu_pg
    pwa = pwa + (pwa_p - pwa) * mu_pwa
    r, k, v = rkv[:, :w], rkv[:, w:2 * w], rkv[:, 2 * w:]
    pw, pa = pwa[:, :LANES], pwa[:, LANES:]
    w_log = -jax.nn.softplus(-(w0 + bdot(jnp.tanh(pw), w2p, "nn"))) - 0.5
    lw = -jnp.exp(w_log)
    iclr = jax.nn.sigmoid(a0 + bdot(pa, a2p, "nn"))
    gate = bdot(jax.nn.sigmoid(pg), g2, "nn")
    kk = k * k_k
    kap = kk / jnp.maximum(jnp.sqrt(head_sum(kk * kk)), 1e-12)
    k_mod = k * (1.0 + (iclr - 1.0) * k_a)
    return r, lw, k_mod, v, kap, kap * iclr, gate


def rwkv_post_fn(ys, r, k_mod, v, gate, ln_w, ln_b, r_k):
    inv_n = 1.0 / RWKV_HEAD_DIM
    mean = head_sum(ys) * inv_n
    yc = ys - mean
    var = head_sum(yc * yc) * inv_n
    yn = yc * lax.rsqrt(var + RWKV_LN_EPS) * ln_w + ln_b
    bonus = head_sum(r * k_mod * r_k) * v
    return (((yn + bonus) * gate).astype(BF16),)


def attn_fn(q, kx, vx):
    outs = []
    for h in range(XATTN_HEADS):
        sl = slice(h * XATTN_HEAD_DIM, (h + 1) * XATTN_HEAD_DIM)
        s = bdot(q[:, sl], kx[:, sl], "nt") * (XATTN_HEAD_DIM ** -0.5)
        s = s - jnp.max(s, axis=-1, keepdims=True)
        p = jnp.exp(s)
        p = p / jnp.sum(p, axis=-1, keepdims=True)
        outs.append(bdot(p, vx[:, sl], "nn"))
    return (jnp.concatenate(outs, axis=-1).astype(BF16),)


def loss_fn(x, tgt, g):
    y = x * lax.rsqrt(jnp.mean(x * x, axis=-1, keepdims=True) + NORM_EPS) * g
    err = jnp.square(y - tgt)
    return 0.5 * jnp.sum(jnp.mean(err, axis=-1, keepdims=True), axis=0, keepdims=True)


def _tri_masks(n):
    row = lax.broadcasted_iota(jnp.int32, (n, n), 0)
    col = lax.broadcasted_iota(jnp.int32, (n, n), 1)
    return col <= row, col < row, row == col


def rwkv_chunk_fn(st0, r, lw, k, v, kap, b):
    h, c = r.shape[0], r.shape[1]
    incl, strict, diag = _tri_masks(c)
    cum = exact_dot(jnp.broadcast_to(incl.astype(F32), (h, c, c)), lw, "nn", "a")
    g_in = jnp.exp(cum)
    g_prev = jnp.exp(cum - lw)
    g_inv = jnp.exp(-cum)
    g_end = jnp.exp(cum[:, c - 1:c, :] - cum)
    kap_t, k_t, b_t, r_t = kap * g_prev, k * g_inv, b * g_inv, r * g_in
    a_ub = jnp.where(strict, bdot(kap_t, b_t, "nt"), 0.0)
    a_vk = jnp.where(strict, bdot(kap_t, k_t, "nt"), 0.0)
    rhs = -(bdot(kap_t, st0, "nn") + bdot(a_vk, v, "nn"))
    eye = diag.astype(F32)
    m = -a_ub
    inv = eye + m
    n = 1
    while n * 2 < c:
        m = bdot(m, m, "nn")
        inv = bdot(inv, eye + m, "nn")
        n *= 2
    u = bdot(inv, rhs, "nn")
    y = (bdot(r_t, st0, "nn")
         + bdot(jnp.where(incl, bdot(r_t, k_t, "nt"), 0.0), v, "nn")
         + bdot(jnp.where(incl, bdot(r_t, b_t, "nt"), 0.0), u, "nn"))
    g_full = jnp.exp(exact_dot(lw, jnp.ones((h, c, st0.shape[2]), F32), "tn", "b"))
    st1 = g_full * st0 + bdot(k * g_end, v, "tn") + bdot(b * g_end, u, "tn")
    return y, st1


def ssd_chunk_fn(group, h0, xs, bm, cm, dt, a_log, d_skip):
    q, nh = xs.shape[0], SSD_HEADS_PER_GROUP
    causal, _, _ = _tri_masks(q)
    a_row = -jnp.exp(a_log)
    cs_all = exact_dot(causal.astype(F32), dt * a_row, "nn", "a")
    cs_t = cs_all.T
    lanes = range(group * nh, (group + 1) * nh)
    cs = jnp.stack([cs_all[:, hl:hl + 1] for hl in lanes])
    cs_row = jnp.stack([cs_t[hl:hl + 1, :] for hl in lanes])
    dt_h = jnp.stack([dt[:, hl:hl + 1] for hl in lanes])
    d_h = jnp.stack([d_skip[:, hl:hl + 1] for hl in lanes])
    x = _stack_lanes(xs, nh)
    h0s = _stack_rows(h0, nh)
    lmat = jnp.where(causal, jnp.exp(jnp.where(causal, cs - cs_row, 0.0)), 0.0)
    cb = bdot(cm, bm, "nt")
    xdt = x * dt_h
    cl = cs[:, q - 1:q, :]
    cm_b = jnp.broadcast_to(cm, (nh,) + cm.shape)
    bm_b = jnp.broadcast_to(bm, (nh,) + bm.shape)
    y = bdot(cb * lmat, xdt, "nn") + bdot(cm_b, h0s, "nt") * jnp.exp(cs) + x * d_h
    h1 = h0s * jnp.exp(cl) + bdot(xdt * jnp.exp(cl - cs), bm_b, "tn")
    return jnp.concatenate([y[e] for e in range(nh)], axis=-1), jnp.concatenate([h1[e] for e in range(nh)], axis=0)


class Rows:
    def __init__(self, arr, w=None, cb=0, shifts=()):
        self.arr, self.w, self.cb, self.shifts = arr, (arr.shape[1] if w is None else w), cb, tuple(shifts)


def _as_rows(x):
    return x if isinstance(x, Rows) else Rows(x)


def _shift_down(x, halo, k):
    rolled = pltpu.roll(x, k, 0)
    first = rolled[0:SUBLANES]
    rid = lax.broadcasted_iota(jnp.int32, first.shape, 0)
    patched = jnp.where(rid < k, pltpu.roll(halo, k, 0), first)
    return jnp.concatenate([patched, rolled[SUBLANES:]], axis=0)


def _shift_up(g, carry, k):
    tm = g.shape[0]
    rolled = pltpu.roll(g, tm - k, 0)
    last = rolled[tm - SUBLANES:]
    rid = lax.broadcasted_iota(jnp.int32, last.shape, 0)
    patched = jnp.where(rid >= SUBLANES - k, pltpu.roll(carry, SUBLANES - k, 0), last)
    return jnp.concatenate([rolled[:tm - SUBLANES], patched], axis=0)


def _params():
    return pltpu.CompilerParams(vmem_limit_bytes=VMEM_LIMIT)


def _load_rows(refs, pos, rins, first_block):
    vals = []
    for r in rins:
        x = refs[pos][...].astype(F32) if refs[pos].dtype != F32 else refs[pos][...]
        pos += 1
        vals.append(x)
        if r.shifts:
            halo = refs[pos][...]
            pos += 1
            halo = jnp.where(first_block, jnp.zeros_like(halo), halo)
            for k in r.shifts:
                vals.append(_shift_down(x, halo, k))
    return vals, pos


def _row_specs(rins, tm, blk):
    specs, args = [], []
    for r in rins:
        specs.append(pl.BlockSpec((tm, r.w), lambda i, cb=r.cb: (blk(i), cb)))
        args.append(r.arr)
        if r.shifts:
            per = tm // SUBLANES
            specs.append(pl.BlockSpec((SUBLANES, r.w), lambda i, cb=r.cb: (jnp.maximum(blk(i) * per - 1, 0), cb)))
            args.append(r.arr)
    return specs, args


def rowwise_fwd(name, fn, rins, params, outs, tm, deps=()):
    rins = [_as_rows(r) for r in rins]
    t = rins[0].arr.shape[0]
    tm = min(tm, t)
    nb = t // tm
    specs, args = _row_specs(rins, tm, lambda i: i)
    for p in params:
        specs.append(pl.BlockSpec(p.shape, lambda i: (0, 0)))
        args.append(p)
    for dep in deps:
        specs.append(pl.BlockSpec(memory_space=pl.ANY))
        args.append(dep)
    n_in = len(args)

    def body(*refs):
        vals, pos = _load_rows(refs, 0, rins, pl.program_id(0) == 0)
        pv = [refs[pos + j][...] for j in range(len(params))]
        res = fn(*vals, *pv)
        for o_ref, o in zip(refs[n_in:], res):
            o_ref[...] = o.astype(o_ref.dtype)

    return pl.pallas_call(
        body, name=name, grid=(nb,), in_specs=specs,
        out_specs=[pl.BlockSpec((tm, w), lambda i: (i, 0)) for w, _ in outs],
        out_shape=[jax.ShapeDtypeStruct((t, w), dt) for w, dt in outs],
        compiler_params=_params(),
    )(*args)


def rowwise_bwd(name, fn, rins, params, cts, tm, grad_dtypes, row_add=None):
    rins = [_as_rows(r) for r in rins]
    cts = [[_as_rows(c) for c in lst] for lst in cts]
    row_add = [_as_rows(a) for a in (row_add or [])]
    t = rins[0].arr.shape[0]
    tm = min(tm, t)
    nb = t // tm
    rev = lambda i: nb - 1 - i
    specs, args = _row_specs(rins, tm, rev)
    for p in params:
        specs.append(pl.BlockSpec(p.shape, lambda i: (0, 0)))
        args.append(p)
    flat_cts = [c for lst in cts for c in lst] + row_add
    for c in flat_cts:
        specs.append(pl.BlockSpec((tm, c.w), lambda i, cb=c.cb: (rev(i), cb)))
        args.append(c.arr)
    n_in = len(args)
    want = [i for i, d in enumerate(grad_dtypes) if d is not None]
    out_specs = [pl.BlockSpec((tm, rins[i].w), lambda i_: (rev(i_), 0)) for i in want]
    out_shape = [jax.ShapeDtypeStruct((t, rins[i].w), grad_dtypes[i]) for i in want]
    out_specs += [pl.BlockSpec(p.shape, lambda i: (0, 0)) for p in params]
    out_shape += [jax.ShapeDtypeStruct(p.shape, F32) for p in params]
    n_out = len(out_shape)
    scratch = [pltpu.VMEM((SUBLANES, r.w), F32) for r in rins for _ in r.shifts]

    def body(*refs):
        i = pl.program_id(0)
        vals, pos = _load_rows(refs, 0, rins, rev(i) == 0)
        pv = [refs[pos + j][...] for j in range(len(params))]
        pos += len(params)
        outs, vjp = jax.vjp(fn, *vals, *pv)
        ct_vals = []
        for o, lst in zip(outs, cts):
            acc = None
            for _ in lst:
                cv = refs[pos][...].astype(F32)
                pos += 1
                acc = cv if acc is None else acc + cv
            ct_vals.append(acc.astype(o.dtype))
        adds = [refs[pos + j][...].astype(F32) for j in range(len(row_add))]
        grads = vjp(tuple(ct_vals))
        out_refs = refs[n_in:n_in + n_out]
        carry_refs = refs[n_in + n_out:]

        @pl.when(i == 0)
        def _():
            for cr in carry_refs:
                cr[...] = jnp.zeros_like(cr)
            for pr in out_refs[len(want):]:
                pr[...] = jnp.zeros_like(pr)

        gi, ci, oi = 0, 0, 0
        for idx, r in enumerate(rins):
            d = grads[gi]
            gi += 1
            for k in r.shifts:
                dk = grads[gi]
                gi += 1
                d = d + _shift_up(dk, carry_refs[ci][...], k)
                carry_refs[ci][...] = dk[0:SUBLANES]
                ci += 1
            if idx == 0:
                for a in adds:
                    d = d + a
            if grad_dtypes[idx] is not None:
                out_refs[oi][...] = d.astype(out_refs[oi].dtype)
                oi += 1
        for pr, gp in zip(out_refs[len(want):], grads[gi:]):
            pr[...] += gp

    res = pl.pallas_call(
        body, name=name, grid=(nb,), in_specs=specs, out_specs=out_specs, out_shape=out_shape,
        scratch_shapes=scratch, compiler_params=_params(),
    )(*args)
    return res[:len(want)], res[len(want):]


def _pick(n, pref):
    for c in pref:
        if n % c == 0:
            return c
    return n


MM_VMEM_BUDGET = 40 * 1024 * 1024
MM_PEAK_FLOPS = 0.9e15
MM_HBM_BYTES_PER_S = 3.0e12
MM_STEP_SECONDS = 0.35e-6


def _mm_tiles(m, n, k, size_a, size_b, size_out, size_res, single_k):
    best = None
    for tk in sorted({c for c in (k, 2048, 1024, 512, 256, 128) if c <= 2048 and k % c == 0}, reverse=True):
        for tm in (1024, 512, 256, 128):
            if m % tm:
                continue
            for tn in (1024, 768, 512, 384, 256, 128):
                if n % tn:
                    continue
                nk = k // tk
                vmem = 2 * (tm * tk * size_a + tk * tn * size_b + tm * tn * (size_out + size_res))
                vmem += tm * tn * 4 * (2 if nk > 1 or not single_k else 1)
                vmem += (tm * tk * 2 if size_a > 2 else 0) + (tk * tn * 2 if size_b > 2 else 0)
                if vmem > MM_VMEM_BUDGET:
                    continue
                steps = (m // tm) * (n // tn) * nk
                a_reads = 1 if (nk == 1 and single_k) else n // tn
                traffic = m * k * size_a * a_reads + k * n * size_b * (m // tm) + m * n * (size_out + size_res)
                cost = max(2.0 * m * n * k / MM_PEAK_FLOPS, traffic / MM_HBM_BYTES_PER_S) + steps * MM_STEP_SECONDS
                if best is None or cost < best[0]:
                    best = (cost, tm, tn, tk)
    return best[1:]


def mm(name, a, b, mode, out_dtype=F32, res=None, b_slabs=None, out_slabs=None, dep=None, epi=None, extras=(),
       out_dtypes=None):
    if mode == "tn":
        k_dim, m_dim = a.shape
    else:
        m_dim, k_dim = a.shape
    if b_slabs:
        n_dim = b.shape[0] * b.shape[2] if mode == "nn" else b.shape[1]
    else:
        n_dim = b.shape[0] if mode == "nt" else b.shape[1]
    n_slabs = out_slabs or (b_slabs if (b_slabs and mode == "nn") else 1)
    k_slabs = b_slabs if (b_slabs and mode == "nt") else 1
    if epi is None:
        out_dtypes = [out_dtype]
        if res is None:
            epi = lambda acc: (acc,)
        else:
            extras, epi = [res], lambda acc, r: (acc + r,)
    tm, tn, tk = _mm_tiles(m_dim, n_dim // n_slabs, k_dim // k_slabs, a.dtype.itemsize, b.dtype.itemsize,
                           sum(jnp.dtype(dt).itemsize for dt in out_dtypes), sum(e.dtype.itemsize for e in extras),
                           single_k=(k_slabs == 1))
    nji = n_dim // n_slabs // tn
    nki = k_dim // k_slabs // tk
    nblk = lambda js, j: js * nji + j
    kblk = lambda ks, k: ks * nki + k
    if mode == "tn":
        a_spec = pl.BlockSpec((tk, tm), lambda i, js, j, ks, k: (kblk(ks, k), i))
    else:
        a_spec = pl.BlockSpec((tm, tk), lambda i, js, j, ks, k: (i, kblk(ks, k)))
    if b_slabs and mode == "nn":
        b_spec = pl.BlockSpec((None, tk, tn), lambda i, js, j, ks, k: (js, k, j))
    elif b_slabs and mode == "nt":
        b_spec = pl.BlockSpec((None, tn, tk), lambda i, js, j, ks, k: (ks, nblk(js, j), k))
    elif mode == "nt":
        b_spec = pl.BlockSpec((tn, tk), lambda i, js, j, ks, k: (nblk(js, j), kblk(ks, k)))
    else:
        b_spec = pl.BlockSpec((tk, tn), lambda i, js, j, ks, k: (kblk(ks, k), nblk(js, j)))
    specs, args = [a_spec, b_spec], [a, b]
    for e in extras:
        specs.append(pl.BlockSpec((tm, tn), lambda i, js, j, ks, k: (i, nblk(js, j))))
        args.append(e)
    if dep is not None:
        specs.append(pl.BlockSpec(memory_space=pl.ANY))
        args.append(dep)
    if out_slabs:
        o_specs = [pl.BlockSpec((None, tm, tn), lambda i, js, j, ks, k: (js, i, j))]
        o_shapes = [jax.ShapeDtypeStruct((out_slabs, m_dim, n_dim // out_slabs), out_dtypes[0])]
    else:
        o_specs = [pl.BlockSpec((tm, tn), lambda i, js, j, ks, k: (i, nblk(js, j))) for _ in out_dtypes]
        o_shapes = [jax.ShapeDtypeStruct((m_dim, n_dim), dt) for dt in out_dtypes]

    one_k_step = k_slabs * nki == 1
    n_in, n_out = len(args), len(out_dtypes)

    def body(*refs):
        a_ref, b_ref = refs[0], refs[1]
        part = _dg(a_ref[...].astype(BF16), b_ref[...].astype(BF16), mode)

        def finish(acc):
            outs = epi(acc, *[refs[2 + j][...].astype(F32) for j in range(len(extras))])
            for o_ref, o in zip(refs[n_in:n_in + n_out], outs):
                o_ref[...] = o.astype(o_ref.dtype)

        if one_k_step:
            finish(part)
            return
        acc_ref = refs[n_in + n_out]
        ks, kk = pl.program_id(3), pl.program_id(4)

        @pl.when((ks == 0) & (kk == 0))
        def _():
            acc_ref[...] = part

        @pl.when((ks > 0) | (kk > 0))
        def _():
            acc_ref[...] += part

        pl.when((ks == k_slabs - 1) & (kk == nki - 1))(lambda: finish(acc_ref[...]))

    grid = (m_dim // tm, n_slabs, nji, k_slabs, nki)
    scratch = [] if one_k_step else [pltpu.VMEM((tm, tn), F32)]
    out = pl.pallas_call(
        body, name=name, grid=grid, in_specs=specs, out_specs=o_specs, out_shape=o_shapes, scratch_shapes=scratch,
        compiler_params=pltpu.CompilerParams(
            dimension_semantics=("parallel", "parallel", "parallel", "arbitrary", "arbitrary"),
            vmem_limit_bytes=VMEM_LIMIT),
    )(*args)
    return out[0] if n_out == 1 else out


def _stack_lanes(x, n):
    w = x.shape[1] // n
    return jnp.stack([x[:, i * w:(i + 1) * w] for i in range(n)])


def _stack_rows(x, n):
    w = x.shape[0] // n
    return jnp.stack([x[i * w:(i + 1) * w, :] for i in range(n)])


def rwkv_scan_fwd(r, lw, k, v, kap, b):
    t = r.shape[0]
    c, hps, hd = min(RWKV_CHUNK, t), RWKV_HEADS_PER_STEP, RWKV_HEAD_DIM
    nc, ng, wl = t // c, RWKV_HEADS // hps, hps * hd
    spec = pl.BlockSpec((c, wl), lambda g, ci: (ci, g))

    def body(r_ref, lw_ref, k_ref, v_ref, kap_ref, b_ref, y_ref, ck_ref, st_ref):
        @pl.when(pl.program_id(1) == 0)
        def _():
            st_ref[...] = jnp.zeros_like(st_ref)

        st = st_ref[...]
        ck_ref[...] = st
        ins = [x[...] for x in (r_ref, lw_ref, k_ref, v_ref, kap_ref, b_ref)]
        y, st1 = rwkv_chunk_fn(_stack_rows(st, hps), *[_stack_lanes(x, hps) for x in ins])
        y_ref[...] = jnp.concatenate([y[h] for h in range(hps)], axis=-1)
        st_ref[...] = jnp.concatenate([st1[h] for h in range(hps)], axis=0)

    return pl.pallas_call(
        body, name="rwkv_scan_fwd", grid=(ng, nc), in_specs=[spec] * 6,
        out_specs=[spec, pl.BlockSpec((None, wl, hd), lambda g, ci: (ci, g, 0))],
        out_shape=[jax.ShapeDtypeStruct((t, RWKV_WIDTH), F32), jax.ShapeDtypeStruct((nc, RWKV_WIDTH, hd), F32)],
        scratch_shapes=[pltpu.VMEM((wl, hd), F32)], compiler_params=_params(),
    )(r, lw, k, v, kap, b)


def rwkv_scan_bwd(r, lw, k, v, kap, b, ck, dy):
    t = r.shape[0]
    c, hps, hd = min(RWKV_CHUNK, t), RWKV_HEADS_PER_STEP, RWKV_HEAD_DIM
    nc, ng, wl = t // c, RWKV_HEADS // hps, hps * hd
    spec = pl.BlockSpec((c, wl), lambda g, ci: (nc - 1 - ci, g))

    def body(r_ref, lw_ref, k_ref, v_ref, kap_ref, b_ref, ck_ref, dy_ref, *rest):
        out_refs, dst_ref = rest[:6], rest[6]

        @pl.when(pl.program_id(1) == 0)
        def _():
            dst_ref[...] = jnp.zeros_like(dst_ref)

        ins = [x[...] for x in (r_ref, lw_ref, k_ref, v_ref, kap_ref, b_ref)]
        dyv, ck, dst = dy_ref[...].astype(F32), ck_ref[...], dst_ref[...]
        _, vjp = jax.vjp(rwkv_chunk_fn, _stack_rows(ck, hps), *[_stack_lanes(x, hps) for x in ins])
        grads = vjp((_stack_lanes(dyv, hps), _stack_rows(dst, hps)))
        dst_ref[...] = jnp.concatenate([grads[0][h] for h in range(hps)], axis=0)
        for j in range(6):
            out_refs[j][...] = jnp.concatenate([grads[1 + j][h] for h in range(hps)], axis=-1).astype(BF16)

    return pl.pallas_call(
        body, name="rwkv_scan_bwd", grid=(ng, nc),
        in_specs=[spec] * 6 + [pl.BlockSpec((None, wl, hd), lambda g, ci: (nc - 1 - ci, g, 0)), spec],
        out_specs=[spec] * 6, out_shape=[jax.ShapeDtypeStruct((t, RWKV_WIDTH), BF16)] * 6,
        scratch_shapes=[pltpu.VMEM((wl, hd), F32)], compiler_params=_params(),
    )(r, lw, k, v, kap, b, ck, dy)


def _ssd_specs(q, blk):
    gw = SSD_WIDTH // 2
    return [pl.BlockSpec((q, gw), lambda g, ci: (blk(ci), g)),
            pl.BlockSpec((q, SSD_STATE), lambda g, ci: (blk(ci), g)),
            pl.BlockSpec((q, SSD_STATE), lambda g, ci: (blk(ci), g)),
            pl.BlockSpec((q, LANES), lambda g, ci: (blk(ci), 0)),
            pl.BlockSpec((1, LANES), lambda g, ci: (0, 0)),
            pl.BlockSpec((1, LANES), lambda g, ci: (0, 0))]


def ssd_scan_fwd(xs, bm, cm, dt, a_log, d_skip):
    t = xs.shape[0]
    q = min(SSD_CHUNK, t)
    nc, gw = t // q, SSD_WIDTH // 2

    def body(xs_ref, bm_ref, cm_ref, dt_ref, al_ref, d_ref, y_ref, ck_ref, h_ref):
        @pl.when(pl.program_id(1) == 0)
        def _():
            h_ref[...] = jnp.zeros_like(h_ref)

        ck_ref[...] = h_ref[...]
        args = (h_ref[...], xs_ref[...], bm_ref[...], cm_ref[...], dt_ref[...], al_ref[...], d_ref[...])
        g = pl.program_id(0)

        @pl.when(g == 0)
        def _():
            y, h1 = ssd_chunk_fn(0, *args)
            y_ref[...] = y
            h_ref[...] = h1

        @pl.when(g == 1)
        def _():
            y, h1 = ssd_chunk_fn(1, *args)
            y_ref[...] = y
            h_ref[...] = h1

    return pl.pallas_call(
        body, name="ssd_scan_fwd", grid=(2, nc), in_specs=_ssd_specs(q, lambda ci: ci),
        out_specs=[pl.BlockSpec((q, gw), lambda g, ci: (ci, g)),
                   pl.BlockSpec((None, gw, SSD_STATE), lambda g, ci: (ci, g, 0))],
        out_shape=[jax.ShapeDtypeStruct((t, SSD_WIDTH), F32), jax.ShapeDtypeStruct((nc, SSD_WIDTH, SSD_STATE), F32)],
        scratch_shapes=[pltpu.VMEM((gw, SSD_STATE), F32)], compiler_params=_params(),
    )(xs, bm, cm, dt, a_log, d_skip)


def ssd_scan_bwd(xs, bm, cm, dt, a_log, d_skip, ck, dy):
    t = xs.shape[0]
    q = min(SSD_CHUNK, t)
    nc, gw = t // q, SSD_WIDTH // 2
    rev = lambda ci: nc - 1 - ci

    def body(xs_ref, bm_ref, cm_ref, dt_ref, al_ref, d_ref, ck_ref, dy_ref,
             dxs_ref, dbm_ref, dcm_ref, ddt_ref, dal_ref, dd_ref, dh_ref):
        g, ci = pl.program_id(0), pl.program_id(1)

        @pl.when(ci == 0)
        def _():
            dh_ref[...] = jnp.zeros_like(dh_ref)

        @pl.when((ci == 0) & (g == 0))
        def _():
            dal_ref[...] = jnp.zeros_like(dal_ref)
            dd_ref[...] = jnp.zeros_like(dd_ref)

        args = (ck_ref[...], xs_ref[...], bm_ref[...], cm_ref[...], dt_ref[...], al_ref[...], d_ref[...])

        def run(group):
            _, vjp = jax.vjp(functools.partial(ssd_chunk_fn, group), *args)
            dh0, dxs, dbm, dcm, ddt, dal, dd = vjp((dy_ref[...].astype(F32), dh_ref[...]))
            dh_ref[...] = dh0
            dxs_ref[...] = dxs.astype(BF16)
            dbm_ref[...] = dbm.astype(BF16)
            dcm_ref[...] = dcm.astype(BF16)
            ddt_ref[...] = ddt
            dal_ref[...] += dal
            dd_ref[...] += dd

        pl.when(g == 0)(lambda: run(0))
        pl.when(g == 1)(lambda: run(1))

    in_specs = _ssd_specs(q, rev) + [pl.BlockSpec((None, gw, SSD_STATE), lambda g, ci: (rev(ci), g, 0)),
                                     pl.BlockSpec((q, gw), lambda g, ci: (rev(ci), g))]
    return pl.pallas_call(
        body, name="ssd_scan_bwd", grid=(2, nc), in_specs=in_specs,
        out_specs=[pl.BlockSpec((q, gw), lambda g, ci: (rev(ci), g)),
                   pl.BlockSpec((q, SSD_STATE), lambda g, ci: (rev(ci), g)),
                   pl.BlockSpec((q, SSD_STATE), lambda g, ci: (rev(ci), g)),
                   pl.BlockSpec((None, q, LANES), lambda g, ci: (g, rev(ci), 0)),
                   pl.BlockSpec((1, LANES), lambda g, ci: (0, 0)),
                   pl.BlockSpec((1, LANES), lambda g, ci: (0, 0))],
        out_shape=[jax.ShapeDtypeStruct((t, SSD_WIDTH), BF16), jax.ShapeDtypeStruct((t, 2 * SSD_STATE), BF16),
                   jax.ShapeDtypeStruct((t, 2 * SSD_STATE), BF16), jax.ShapeDtypeStruct((2, t, LANES), F32),
                   jax.ShapeDtypeStruct((1, LANES), F32), jax.ShapeDtypeStruct((1, LANES), F32)],
        scratch_shapes=[pltpu.VMEM((gw, SSD_STATE), F32)], compiler_params=_params(),
    )(xs, bm, cm, dt, a_log, d_skip, ck, dy)


def loss_and_grad(x, tgt, g, tm):
    t, d = x.shape
    tm = min(tm, t)
    nb = t // tm

    def body(x_ref, t_ref, g_ref, loss_ref, dx_ref, dg_ref):
        @pl.when(pl.program_id(0) == 0)
        def _():
            loss_ref[...] = jnp.zeros_like(loss_ref)
            dg_ref[...] = jnp.zeros_like(dg_ref)

        val, vjp = jax.vjp(loss_fn, x_ref[...], t_ref[...], g_ref[...])
        dx, _, dg = vjp(jnp.ones((1, 1), F32))
        loss_ref[...] += jnp.broadcast_to(val, loss_ref.shape)
        dx_ref[...] = dx
        dg_ref[...] += dg

    row = pl.BlockSpec((tm, d), lambda i: (i, 0))
    one = pl.BlockSpec((1, d), lambda i: (0, 0))
    return pl.pallas_call(
        body, name="loss_and_grad", grid=(nb,), in_specs=[row, row, one],
        out_specs=[pl.BlockSpec((SUBLANES, LANES), lambda i: (0, 0)), row, one],
        out_shape=[jax.ShapeDtypeStruct((SUBLANES, LANES), F32), jax.ShapeDtypeStruct((t, d), F32),
                   jax.ShapeDtypeStruct((1, d), F32)],
        compiler_params=_params(),
    )(x, tgt, g)


def adamw(name, recv, w, m, v, dep=None):
    rows, cols = w.shape
    n_slabs = recv.shape[0]
    recv_block_bytes = 4 * 1024 * 1024
    tm = _pick(rows, [c for c in (256, 128, 64, 32, 16, 8) if n_slabs * c * cols * 4 <= recv_block_bytes])
    c1 = 1.0 / (1.0 - ADAM_B1 ** ADAM_STEP)
    c2 = 1.0 / (1.0 - ADAM_B2 ** ADAM_STEP)

    n_dep = 0 if dep is None else 1

    def body(recv_ref, w_ref, m_ref, v_ref, *rest):
        g_ref, d_ref, nm_ref, nv_ref = rest[n_dep:]
        g = recv_ref[0].astype(F32)
        for p in range(1, n_slabs):
            g = g + recv_ref[p].astype(F32)
        nm =ADAM_B1 * m_ref[...] + (1.0 - ADAM_B1) * g
        nv = ADAM_B2 * v_ref[...] + (1.0 - ADAM_B2) * jnp.square(g)
        g_ref[...] = g
        nm_ref[...] = nm
        nv_ref[...] = nv
        d_ref[...] = -ADAM_LR * ((nm * c1) / (jnp.sqrt(nv * c2) + ADAM_EPS) + ADAM_WD * w_ref[...])

    blk = pl.BlockSpec((tm, cols), lambda i: (i, 0))
    return pl.pallas_call(
        body, name=name, grid=(rows // tm,),
        in_specs=[pl.BlockSpec((n_slabs, tm, cols), lambda i: (0, i, 0)), blk, blk, blk]
        + [pl.BlockSpec(memory_space=pl.ANY)] * n_dep,
        out_specs=[blk] * 4, out_shape=[jax.ShapeDtypeStruct((rows, cols), F32)] * 4,
        compiler_params=_params(),
    )(recv, w, m, v, *([] if dep is None else [dep]))


def _mesh_pos():
    return lax.axis_index("x"), lax.axis_index("y"), lax.axis_index("c")


def _peer(pos, mask):
    x, y, c = pos
    return (1 - x if mask & 4 else x, 1 - y if mask & 2 else y, 1 - c if mask & 1 else c)


def _linear(pos):
    return 4 * pos[0] + 2 * pos[1] + pos[2]


class Exchange:
    MASKS = {"gather": (1, 2, 3, 4, 5, 6, 7), "scatter": (1, 2, 3, 4, 5, 6, 7), "gather_chips": (1, 2, 4, 6),
             "forward": (2, 4, 6)}

    def __init__(self, xs, kind, lands=None):
        self.kind, self.masks = kind, self.MASKS[kind]
        self.xs = [] if kind == "forward" else list(xs)
        if kind == "forward":
            self.land_shape = [jax.ShapeDtypeStruct(l.shape, l.dtype) for l in lands]
        elif kind == "scatter":
            self.land_shape = [jax.ShapeDtypeStruct(x.shape, x.dtype) for x in xs]
        else:
            self.land_shape = [jax.ShapeDtypeStruct((N_DEV,) + x.shape, x.dtype) for x in xs]
        self.n = len(self.land_shape)
        copies = self.n * len(self.masks)
        self.sems = [pltpu.SemaphoreType.DMA((copies,)), pltpu.SemaphoreType.DMA((copies,)),
                     pltpu.SemaphoreType.DMA((self.n,))]

    def _copies(self, ins, outs, sems, landing):
        send_sems, recv_sems, local_sems = sems
        me = _mesh_pos()
        me_lin = _linear(me)
        local, remote = [], []
        for ti in range(self.n):
            if self.kind != "forward":
                src_mine = ins[ti].at[me_lin] if self.kind == "scatter" else ins[ti]
                local.append(pltpu.make_async_copy(src_mine, outs[ti].at[me_lin], local_sems.at[ti]))
            for j, mask in enumerate(self.masks):
                if self.kind == "forward":
                    peer = _peer(me, 1)
                    src = outs[ti].at[_linear(_peer(me, mask))]
                    dst = outs[ti].at[_linear(_peer(me, mask ^ 1 if landing else mask))]
                else:
                    peer = _peer(me, mask)
                    src = ins[ti].at[_linear(peer)] if self.kind == "scatter" else ins[ti]
                    dst = outs[ti].at[_linear(peer) if landing else me_lin]
                sem_index = ti * len(self.masks) + j
                remote.append(pltpu.make_async_remote_copy(
                    src_ref=src, dst_ref=dst, send_sem=send_sems.at[sem_index], recv_sem=recv_sems.at[sem_index],
                    device_id=peer, device_id_type=pl.DeviceIdType.MESH))
        return local, remote

    def start(self, ins, outs, sems):
        local, remote = self._copies(ins, outs, sems, landing=False)
        for cp in local + remote:
            cp.start()

    def finish(self, ins, outs, sems):
        local, remote = self._copies(ins, outs, sems, landing=True)
        for cp in remote:
            cp.wait_recv()
        for cp in remote:
            cp.wait_send()
        for cp in local:
            cp.wait()


def exchange_start(name, xs, kind, dep=None, lands=None):
    ex = Exchange(xs, kind, lands)
    hbm = pl.BlockSpec(memory_space=pltpu.HBM)
    sem = pl.BlockSpec(memory_space=pltpu.SEMAPHORE)
    if lands is None:
        lands = [lax.empty(s.shape, s.dtype) for s in ex.land_shape]
    n_src, n = len(ex.xs), ex.n
    n_inputs = n_src + n + (0 if dep is None else 1)

    def body(*refs):
        ins, lnd, sems, token = refs[:n_src], refs[n_src:n_src + n], refs[n_inputs:n_inputs + 3], refs[-1]
        ex.start(ins, lnd, sems)
        token[...] = jnp.zeros_like(token)

    res = pl.pallas_call(
        body, name=name, in_specs=[hbm] * (n_src + n) + ([] if dep is None else [pl.BlockSpec(memory_space=pl.ANY)]),
        out_specs=[sem] * 3 + [hbm] * (n_src + n) + [pl.BlockSpec(memory_space=pltpu.VMEM)],
        out_shape=ex.sems + [pltpu.HBM(x.shape, x.dtype) for x in ex.xs]
        + [pltpu.HBM(s.shape, s.dtype) for s in ex.land_shape] + [jax.ShapeDtypeStruct((SUBLANES, LANES), F32)],
        input_output_aliases={i: 3 + i for i in range(n_src + n)},
        compiler_params=pltpu.CompilerParams(has_side_effects=pltpu.SideEffectType.DATAFLOW_SIDE_EFFECTING),
    )(*[pltpu.with_memory_space_constraint(x, pltpu.HBM) for x in ex.xs + list(lands)],
      *([] if dep is None else [dep]))
    return (ex, res[:3], res[3:3 + n_src], res[3 + n_src:3 + n_src + n]), res[-1]


def exchange_wait(name, handles, after):
    ex, sems, srcs, lands = handles
    n_src, n = len(srcs), len(lands)
    hbm = pl.BlockSpec(memory_space=pltpu.HBM)
    sem = pl.BlockSpec(memory_space=pltpu.SEMAPHORE)

    def body(*refs):
        ins, lnd, sem_refs = refs[:n_src], refs[n_src:n_src + n], refs[n_src + n:n_src + n + 3]
        ex.finish(ins, lnd, sem_refs)

    res = pl.pallas_call(
        body, name=name, in_specs=[hbm] * (n_src + n) + [sem] * 3 + [pl.BlockSpec(memory_space=pl.ANY)],
        out_specs=[hbm] * (n_src + n),
        out_shape=[pltpu.HBM(x.shape, x.dtype) for x in srcs] + [pltpu.HBM(x.shape, x.dtype) for x in lands],
        input_output_aliases={i: i for i in range(n_src + n)},
        compiler_params=pltpu.CompilerParams(has_side_effects=pltpu.SideEffectType.DATAFLOW_SIDE_EFFECTING),
    )(*srcs, *lands, *sems, after)
    return res[n_src:]


def forward_start(name, chip_gather, after):
    lands = exchange_wait(name + "_wait", chip_gather, after)
    return exchange_start(name + "_forward_start", [], "forward", lands=lands)


_Z = (0, 1024)
_XBC = (1024, 2560)
_DT = (2560, 2576)
_RKV = (2576, 5648)
_PW = (5648, 5744)
_PA = (5744, 5840)
_PG = (5840, 6096)
D_IN = 6096

_SMALL = ("norm_mix_g", "ssd_conv_b", "ssd_dt_bias", "ssd_a_log", "ssd_d", "ssd_norm_g", "rwkv_mu", "rwkv_w0",
          "rwkv_a0", "rwkv_k_k", "rwkv_k_a", "rwkv_r_k", "rwkv_ln_w", "rwkv_ln_b", "norm_x_g", "norm_mem_g",
          "norm_ffn_g", "final_norm_g")
_WEIGHTS = ("norm_mix_g", "w_in", "ssd_conv_w", "ssd_conv_b", "ssd_dt_bias", "ssd_a_log", "ssd_d", "ssd_norm_g",
            "rwkv_mu", "rwkv_w0", "rwkv_w2", "rwkv_a0", "rwkv_a2", "rwkv_g2", "rwkv_k_k", "rwkv_k_a", "rwkv_r_k",
            "rwkv_ln_w", "rwkv_ln_b", "w_out", "norm_x_g", "norm_mem_g", "xattn_wq", "xattn_wk", "xattn_wv",
            "xattn_wo", "norm_ffn_g", "ffn_w1", "ffn_w2", "final_norm_g")


def _pad_lanes(x, width=LANES):
    return jnp.pad(x, ((0, 0), (0, width - x.shape[1])))


def _pack_small(vals):
    flat = jnp.concatenate([vals[n].reshape(-1) for n in _SMALL])
    rows = -(-flat.shape[0] // (LANES * SUBLANES)) * SUBLANES
    return jnp.pad(flat, (0, rows * LANES - flat.shape[0])).reshape(rows, LANES)


def _unpack_small(packed, shapes):
    flat = packed.reshape(-1)
    out, pos = {}, 0
    for n in _SMALL:
        size = 1
        for s in shapes[n]:
            size *= s
        out[n] = flat[pos:pos + size].reshape(shapes[n])
        pos += size
    return out


def _rows(w, rng):
    return w[rng[0]:rng[1]]


def sum_slabs(name, recv):
    n, rows, cols = recv.shape
    tc = _pick(cols, (256, 128))

    def body(r_ref, o_ref):
        acc = r_ref[0].astype(F32)
        for p in range(1, n):
            acc = acc + r_ref[p].astype(F32)
        o_ref[...] = acc

    return pl.pallas_call(
        body, name=name, grid=(cols // tc,), in_specs=[pl.BlockSpec((n, rows, tc), lambda j: (0, 0, j))],
        out_specs=pl.BlockSpec((rows, tc), lambda j: (0, j)), out_shape=jax.ShapeDtypeStruct((rows, cols), F32),
        compiler_params=_params(),
    )(recv)


def kernel(x, mem, norm_mix_g, w_in, ssd_conv_w, ssd_conv_b, ssd_dt_bias, ssd_a_log, ssd_d, ssd_norm_g, rwkv_mu, rwkv_w0, rwkv_w2, rwkv_a0, rwkv_a2, rwkv_g2, rwkv_k_k, rwkv_k_a, rwkv_r_k, rwkv_ln_w, rwkv_ln_b, w_out, norm_x_g, norm_mem_g, xattn_wq, xattn_wk, xattn_wv, xattn_wo, norm_ffn_g, ffn_w1, ffn_w2, final_norm_g, loss_target, m_norm_mix_g, m_w_in, m_ssd_conv_w, m_ssd_conv_b, m_ssd_dt_bias, m_ssd_a_log, m_ssd_d, m_ssd_norm_g, m_rwkv_mu, m_rwkv_w0, m_rwkv_w2, m_rwkv_a0, m_rwkv_a2, m_rwkv_g2, m_rwkv_k_k, m_rwkv_k_a, m_rwkv_r_k, m_rwkv_ln_w, m_rwkv_ln_b, m_w_out, m_norm_x_g, m_norm_mem_g, m_xattn_wq, m_xattn_wk, m_xattn_wv, m_xattn_wo, m_norm_ffn_g, m_ffn_w1, m_ffn_w2, m_final_norm_g, v_norm_mix_g, v_w_in, v_ssd_conv_w, v_ssd_conv_b, v_ssd_dt_bias, v_ssd_a_log, v_ssd_d, v_ssd_norm_g, v_rwkv_mu, v_rwkv_w0, v_rwkv_w2, v_rwkv_a0, v_rwkv_a2, v_rwkv_g2, v_rwkv_k_k, v_rwkv_k_a, v_rwkv_r_k, v_rwkv_ln_w, v_rwkv_ln_b, v_w_out, v_norm_x_g, v_norm_mem_g, v_xattn_wq, v_xattn_wk, v_xattn_wv, v_xattn_wo, v_norm_ffn_g, v_ffn_w1, v_ffn_w2, v_final_norm_g):
    given = dict(locals())
    wts = {n: given[n] for n in _WEIGHTS}
    mom_m = {n: given["m_" + n] for n in _WEIGHTS}
    mom_v = {n: given["v_" + n] for n in _WEIGHTS}
    d = D_MODEL
    xt, memt, tgt = x[0], mem[0], loss_target[0]
    tm = 256

    big = {"w_in": jnp.transpose(w_in[0]), "w_out": w_out[0], "xattn_wq": xattn_wq[0], "xattn_wk": xattn_wk[0],
           "xattn_wv": xattn_wv[0], "xattn_wo": xattn_wo[0], "ffn_w1": ffn_w1[0], "ffn_w2": ffn_w2[0]}
    small_sh = {"ssd_conv_w": ssd_conv_w.reshape(4, -1), "rwkv_w2": rwkv_w2[0], "rwkv_a2": rwkv_a2[0],
                "rwkv_g2": rwkv_g2[0]}
    cast_one = lambda n, deps=(): rowwise_fwd("cast_" + n, cast_fn, [big[n]], [], [(big[n].shape[1], BF16)],
                                              256 if big[n].shape[0] % 256 == 0 else big[n].shape[0], deps=deps)[0]
    gather_in, token_in = exchange_start("gather_in_start", [cast_one("w_in")] + list(small_sh.values()), "gather_chips")
    cast = {n: cast_one(n, deps=[token_in]) for n in big if n != "w_in"}
    late_a = ("w_out", "xattn_wq", "xattn_wk", "xattn_wv", "xattn_wo")
    late_b = ("ffn_w1", "ffn_w2")
    gather_a, token_a = exchange_start("gather_attn_start", [cast[n] for n in late_a], "gather_chips", dep=token_in)
    gather_b, token_b = exchange_start("gather_ffn_start", [cast[n] for n in late_b], "gather_chips", dep=token_a)
    (h1,) = rowwise_fwd("norm_mix", rmsnorm_fn, [xt], [norm_mix_g], [(d, BF16)], tm, deps=[token_b])
    forward_in, token_in = forward_start("gather_in", gather_in, after=h1)
    gathered = exchange_wait("gather_in_forward_wait", forward_in, after=token_in)
    g_big = {"w_in": gathered[0]}
    g_small = dict(zip(small_sh, gathered[1:]))

    pad_rows = lambda a: jnp.pad(a, ((0, LANES - a.shape[0]), (0, 0)))
    w_in_t = g_big["w_in"].reshape(D_IN, d)
    wt_z, wt_xbc, wt_rkv, wt_pg = (_rows(w_in_t, r) for r in (_Z, _XBC, _RKV, _PG))
    wt_sm = jnp.concatenate([pad_rows(_rows(w_in_t, r)) for r in (_PW, _PA, _DT)], axis=0)
    unshard_cols = lambda g: jnp.transpose(g, (1, 0, 2)).reshape(g.shape[1], -1)
    conv_w_f = unshard_cols(g_small["ssd_conv_w"])
    w2p, a2p = pad_rows(unshard_cols(g_small["rwkv_w2"])), pad_rows(unshard_cols(g_small["rwkv_a2"]))
    g2_f = unshard_cols(g_small["rwkv_g2"])

    mu = rwkv_mu
    mu_rkv, mu_pg = mu[:, :3072], mu[:, 3264:3520]
    mu_pwa = jnp.concatenate([_pad_lanes(mu[:, 3072:3168]), _pad_lanes(mu[:, 3168:3264])], axis=1)
    dt_bias_p, a_log_p, d_p = _pad_lanes(ssd_dt_bias), _pad_lanes(ssd_a_log), _pad_lanes(ssd_d)
    r_k_row = rwkv_r_k.reshape(1, RWKV_WIDTH)
    g_final = final_norm_g.reshape(1, d)

    u_z = mm("in_z", h1, wt_z, "nt")
    u_xbc = mm("in_xbc", h1, wt_xbc, "nt")
    u_rkv = mm("in_rkv", h1, wt_rkv, "nt")
    u_pg = mm("in_pg", h1, wt_pg, "nt")
    u_sm = mm("in_small", h1, wt_sm, "nt")

    ssd_pre_rows = lambda: [Rows(u_xbc, shifts=(1, 2, 3)), Rows(u_sm, LANES, 2)]
    ssd_pre_params = [conv_w_f, ssd_conv_b, dt_bias_p]
    xs, bm, cm, dt = rowwise_fwd("ssd_pre", ssd_pre_fn, ssd_pre_rows(), ssd_pre_params,
                                 [(SSD_WIDTH, F32), (256, F32), (256, F32), (LANES, F32)], tm)
    y_scan, ssd_ck = ssd_scan_fwd(xs, bm, cm, dt, a_log_p, d_p)
    (y_ssd,) = rowwise_fwd("ssd_post", ssd_post_fn, [y_scan, u_z], [ssd_norm_g], [(SSD_WIDTH, BF16)], tm)

    rwkv_pre_rows = lambda: [Rows(u_rkv, shifts=(1,)), Rows(u_pg, shifts=(1,)), Rows(u_sm, 2 * LANES, 0, shifts=(1,))]
    rwkv_pre_params = [mu_rkv, mu_pg, mu_pwa, rwkv_w0, w2p, rwkv_a0, a2p, g2_f, rwkv_k_k, rwkv_k_a]
    forward_a, token_a = forward_start("gather_attn", gather_a, after=y_ssd)
    r_, lw_, k_, v_, kap_, b_, gate_ = rowwise_fwd("rwkv_pre", rwkv_pre_fn, rwkv_pre_rows(), rwkv_pre_params,
                                                   [(RWKV_WIDTH, F32)] * 7, 128, deps=[token_a])
    ys_r, rwkv_ck = rwkv_scan_fwd(r_, lw_, k_, v_, kap_, b_)
    forward_b, token_b = forward_start("gather_ffn", gather_b, after=ys_r)
    g_big.update(zip(late_a, exchange_wait("gather_attn_forward_wait", forward_a, after=token_b)))
    w_out_f = g_big["w_out"].reshape(d, d)
    wq_f, wk_f, wv_f, wo_f = (g_big[n].reshape(d, d) for n in ("xattn_wq", "xattn_wk", "xattn_wv", "xattn_wo"))
    rwkv_post_params = [rwkv_ln_w, rwkv_ln_b, r_k_row]
    (y_rwkv,) = rowwise_fwd("rwkv_post", rwkv_post_fn, [ys_r, r_, k_, v_, gate_], rwkv_post_params,
                            [(RWKV_WIDTH, BF16)], tm)
    ycat = jnp.concatenate([y_ssd, y_rwkv], axis=1)
    x1 = mm("out_proj", ycat, w_out_f, "nn", res=xt)

    (h2,) = rowwise_fwd("norm_x", rmsnorm_fn, [x1], [norm_x_g], [(d, BF16)], tm)
    (mn,) = rowwise_fwd("norm_mem", rmsnorm_fn, [memt], [norm_mem_g], [(d, BF16)], tm)
    q = mm("xattn_q", h2, wq_f, "nn", out_dtype=BF16)
    kx = mm("xattn_k", mn, wk_f, "nn")
    vx = mm("xattn_v", mn, wv_f, "nn")
    (o,) = rowwise_fwd("xattn", attn_fn, [q], [kx, vx], [(d, BF16)], tm)
    x2 = mm("xattn_o", o, wo_f, "nn", res=x1)

    (h3,) = rowwise_fwd("norm_ffn", rmsnorm_fn, [x2], [norm_ffn_g], [(d, BF16)], tm)
    w1_s, w2_g = exchange_wait("gather_ffn_forward_wait", forward_b, after=h3)
    w2_f = w2_g.reshape(D_FF, d)
    relu2_epi = lambda acc: (jnp.square(jnp.maximum(acc, 0.0)), jnp.maximum(acc, 0.0))
    hid, relu_a = mm("ffn_1", h3, w1_s, "nn", b_slabs=N_DEV, epi=relu2_epi, out_dtypes=[BF16, BF16])
    x3 = mm("ffn_2", hid, w2_f, "nn", res=x2)

    loss_blk, dx3, dg_final = loss_and_grad(x3, tgt, g_final, tm)

    grads = {}
    grads["ffn_w2"] = mm("d_ffn_w2", hid, dx3, "tn", out_dtype=BF16).reshape(N_DEV, D_FF // N_DEV, d)
    sc_w2, tok = exchange_start("scatter_ffn_w2_start", [grads["ffn_w2"]], "scatter")
    da = mm("d_hid", dx3, w2_f, "nt", dep=tok, epi=lambda acc, ra: (2.0 * acc * ra,), extras=[relu_a],
            out_dtypes=[BF16])
    grads["ffn_w1"] = mm("d_ffn_w1", h3, da, "tn", out_dtype=BF16, out_slabs=N_DEV)
    sc_w1, tok = exchange_start("scatter_ffn_w1_start", [grads["ffn_w1"]], "scatter")
    dh3 = mm("d_h3", da, w1_s, "nt", out_dtype=BF16, b_slabs=N_DEV, dep=tok)
    (dx2,), (dg_ffn,) = rowwise_bwd("norm_ffn_bwd", rmsnorm_fn, [x2], [norm_ffn_g], [[dh3]], tm, [F32], row_add=[dx3])

    grads["xattn_wo"] = mm("d_wo", o, dx2, "tn", out_dtype=BF16).reshape(N_DEV, d // N_DEV, d)
    sc_wo, tok = exchange_start("scatter_wo_start", [grads["xattn_wo"]], "scatter")
    d_o = mm("d_o", dx2, wo_f, "nt", out_dtype=BF16, dep=tok)
    (dq,), (dkx, dvx) = rowwise_bwd("xattn_bwd", attn_fn, [q], [kx, vx], [[d_o]], tm, [BF16])
    grads["xattn_wq"] = mm("d_wq", h2, dq, "tn", out_dtype=BF16).reshape(N_DEV, d // N_DEV, d)
    grads["xattn_wk"] = mm("d_wk", mn, dkx, "tn", out_dtype=BF16).reshape(N_DEV, d // N_DEV, d)
    grads["xattn_wv"] = mm("d_wv", mn, dvx, "tn", out_dtype=BF16).reshape(N_DEV, d // N_DEV, d)
    qkv = ("xattn_wq", "xattn_wk", "xattn_wv")
    sc_qkv, tok = exchange_start("scatter_qkv_start", [grads[n] for n in qkv], "scatter")
    dmn = mm("d_mn_v", dvx, wv_f, "nt", res=mm("d_mn_k", dkx, wk_f, "nt", dep=tok))
    _, (dg_mem,) = rowwise_bwd("norm_mem_bwd", rmsnorm_fn, [memt], [norm_mem_g], [[dmn]], tm, [None])
    dh2 = mm("d_h2", dq, wq_f, "nt", out_dtype=BF16, dep=dg_mem)
    (dx1,), (dg_x,) = rowwise_bwd("norm_x_bwd", rmsnorm_fn, [x1], [norm_x_g], [[dh2]], tm, [F32], row_add=[dx2])

    grads["w_out"] = mm("d_w_out", ycat, dx1, "tn", out_dtype=BF16).reshape(N_DEV, d // N_DEV, d)
    sc_wout, tok = exchange_start("scatter_w_out_start", [grads["w_out"]], "scatter")
    d_ycat = mm("d_ycat", dx1, w_out_f, "nt", out_dtype=BF16, dep=tok)

    (d_ys, d_r1, d_k1, d_v1, d_gate), (dln_w, dln_b, dr_k) = rowwise_bwd(
        "rwkv_post_bwd", rwkv_post_fn, [ys_r, r_, k_, v_, gate_], rwkv_post_params,
        [[Rows(d_ycat, RWKV_WIDTH, 1)]], tm, [BF16] * 5)
    d_r2, d_lw, d_k2, d_v2, d_kap, d_b = rwkv_scan_bwd(r_, lw_, k_, v_, kap_, b_, rwkv_ck, d_ys)
    (du_rkv, du_pg, du_pwa), rwkv_pg = rowwise_bwd(
        "rwkv_pre_bwd", rwkv_pre_fn, rwkv_pre_rows(), rwkv_pre_params,
        [[d_r1, d_r2], [d_lw], [d_k1, d_k2], [d_v1, d_v2], [d_kap], [d_b], [d_gate]], 128, [BF16] * 3)
    dmu_rkv, dmu_pg, dmu_pwa, dw0, dw2p, da0, da2p, dg2, dk_k, dk_a = rwkv_pg

    (d_yscan, du_z), (dssd_norm_g,) = rowwise_bwd("ssd_post_bwd", ssd_post_fn, [y_scan, u_z], [ssd_norm_g],
                                                  [[Rows(d_ycat, SSD_WIDTH, 0)]], tm, [BF16, BF16])
    dxs, dbm, dcm, ddt2, da_log_p, dd_p = ssd_scan_bwd(xs, bm, cm, dt, a_log_p, d_p, ssd_ck, d_yscan)
    (du_xbc, du_dt), (dconv_w, dconv_b, ddt_bias_p) = rowwise_bwd(
        "ssd_pre_bwd", ssd_pre_fn, ssd_pre_rows(), ssd_pre_params,
        [[dxs], [dbm], [dcm], [ddt2[0], ddt2[1]]], tm, [BF16, BF16])
    du_sm = jnp.concatenate([du_pwa, du_dt], axis=1)

    dwt_z = mm("d_w_z", du_z, h1, "tn", out_dtype=BF16)
    dwt_xbc = mm("d_w_xbc", du_xbc, h1, "tn", out_dtype=BF16)
    dwt_rkv = mm("d_w_rkv", du_rkv, h1, "tn", out_dtype=BF16)
    dwt_pg = mm("d_w_pg", du_pg, h1, "tn", out_dtype=BF16)
    dwt_sm = mm("d_w_small", du_sm, h1, "tn", out_dtype=BF16)
    dwt_full = jnp.concatenate([dwt_z, dwt_xbc, dwt_sm[256:272], dwt_rkv, dwt_sm[0:96], dwt_sm[128:224], dwt_pg], axis=0)
    to_slabs = lambda g: jnp.transpose(g.reshape(g.shape[0], N_DEV, -1), (1, 0, 2))
    grads["w_in"] = dwt_full.reshape(N_DEV, D_IN // N_DEV, d)
    grads["ssd_conv_w"] = to_slabs(dconv_w)
    grads["rwkv_w2"] = to_slabs(dw2p[:96])
    grads["rwkv_a2"] = to_slabs(da2p[:96])
    grads["rwkv_g2"] = to_slabs(dg2)
    tail = ("w_in", "ssd_conv_w", "rwkv_w2", "rwkv_a2", "rwkv_g2")
    sc_tail, tok = exchange_start("scatter_tail_start", [grads[n] for n in tail], "scatter")
    dh1 = mm("d_h1_z", du_z, wt_z, "nn", dep=tok)
    dh1 = mm("d_h1_xbc", du_xbc, wt_xbc, "nn", res=dh1)
    dh1 = mm("d_h1_rkv", du_rkv, wt_rkv, "nn", res=dh1)
    dh1 = mm("d_h1_pg", du_pg, wt_pg, "nn", res=dh1)
    dh1 = mm("d_h1_small", du_sm, wt_sm, "nn", res=dh1)
    (dx,), (dg_mix,) = rowwise_bwd("norm_mix_bwd", rmsnorm_fn, [xt], [norm_mix_g], [[dh1]], tm, [F32], row_add=[dx1])

    dmu =jnp.concatenate([dmu_rkv, dmu_pwa[:, 0:96], dmu_pwa[:, 128:224], dmu_pg], axis=1)
    small_grads = {
        "norm_mix_g": dg_mix, "ssd_conv_b": dconv_b, "ssd_dt_bias": ddt_bias_p[:, :16], "ssd_a_log": da_log_p[:, :16],
        "ssd_d": dd_p[:, :16], "ssd_norm_g": dssd_norm_g, "rwkv_mu": dmu, "rwkv_w0": dw0, "rwkv_a0": da0,
        "rwkv_k_k": dk_k, "rwkv_k_a": dk_a, "rwkv_r_k": dr_k, "rwkv_ln_w": dln_w, "rwkv_ln_b": dln_b,
        "norm_x_g": dg_x, "norm_mem_g": dg_mem, "norm_ffn_g": dg_ffn, "final_norm_g": dg_final}

    gather_small, tok = exchange_start("gather_small_start", [_pack_small(small_grads)], "gather")
    received = {}
    for names, handle in ((("ffn_w2",), sc_w2), (("ffn_w1",), sc_w1), (("xattn_wo",), sc_wo), (qkv, sc_qkv),
                          (("w_out",), sc_wout)):
        received.update(zip(names, exchange_wait("scatter_" + names[0] + "_wait", handle, after=tok)))

    out_g, out_d, out_m, out_v = {}, {}, {}, {}

    def run_adamw(n, dep):
        shape = wts[n].shape
        two_d = lambda a: a.reshape(-1, shape[-1])
        if n == "w_in":
            recv = jnp.transpose(sum_slabs("sum_w_in", received[n]))[None]
        else:
            recv = received[n].reshape(N_DEV, -1, shape[-1])
        res = adamw("adamw_" + n, recv, two_d(wts[n]), two_d(mom_m[n]), two_d(mom_v[n]), dep=dep)
        out_g[n], out_d[n], out_m[n], out_v[n] = (r.reshape(shape) for r in res)
        return res[0]

    last = None
    for n in ("ffn_w2", "ffn_w1", "xattn_wo") + qkv + ("w_out",):
        last = run_adamw(n, last)
    received.update(zip(tail, exchange_wait("scatter_tail_wait", sc_tail, after=last)))
    for n in tail:
        last = run_adamw(n, last)
    (small_all,) = exchange_wait("gather_small_wait", gather_small, after=last)
    res = adamw("adamw_small", small_all, _pack_small(wts), _pack_small(mom_m), _pack_small(mom_v))
    shapes = {n: wts[n].shape for n in _SMALL}
    for dst, packed in zip((out_g, out_d, out_m, out_v), res):
        dst.update(_unpack_small(packed, shapes))

    loss = lax.psum(loss_blk[0, 0], ("x", "y", "c"))
    return (loss, dx[None], *[out_g[n] for n in _WEIGHTS], *[out_d[n] for n in _WEIGHTS],
            *[out_m[n] for n in _WEIGHTS], *[out_v[n] for n in _WEIGHTS])
```

```python
import functools

import jax
import jax.numpy as jnp
from jax import lax
from jax.experimental import pallas as pl
from jax.experimental.pallas import tpu as pltpu

F32 = jnp.float32
BF16 = jnp.bfloat16
HIGHEST = lax.Precision.HIGHEST

N_DEV = 8
D_MODEL = 2048
NORM_EPS = 1e-6
SSD_WIDTH = 1024
SSD_CONV_DIM = 1536
SSD_HEADS = 16
SSD_HEAD_DIM = 64
SSD_STATE = 128
SSD_CHUNK = 128
SSD_HEADS_PER_GROUP = 8
RWKV_WIDTH = 1024
RWKV_HEADS = 16
RWKV_HEAD_DIM = 64
RWKV_LN_EPS = 64e-5
RWKV_CHUNK = 64
RWKV_HEADS_PER_STEP = 16
XATTN_HEADS = 4
XATTN_HEAD_DIM = 512
D_FF = 8192
LANES = 128
SUBLANES = 8
VMEM_LIMIT = 56 * 1024 * 1024

ADAM_LR = 0.001
ADAM_B1 = 0.9
ADAM_B2 = 0.999
ADAM_EPS = 1e-08
ADAM_WD = 0.01
ADAM_STEP = 10

_DN = {"nn": ((1,), (0,)), "nt": ((1,), (1,)), "tn": ((0,), (0,))}


def _dg(a, b, mode, precision=None):
    (ca,), (cb,) = _DN[mode]
    dn = (((ca + 1,), (cb + 1,)), ((0,), (0,))) if a.ndim == 3 else (((ca,), (cb,)), ((), ()))
    return lax.dot_general(a, b, dn, precision=precision, preferred_element_type=F32)


@functools.partial(jax.custom_vjp, nondiff_argnums=(2,))
def bdot(a, b, mode):
    return _dg(a.astype(BF16), b.astype(BF16), mode)


def _bdot_fwd(a, b, mode):
    return bdot(a, b, mode), (a, b)


def _bdot_bwd(mode, res, g):
    a, b = res
    ab, bb, gb = a.astype(BF16), b.astype(BF16), g.astype(BF16)
    if mode == "nn":
        da, db = _dg(gb, bb, "nt"), _dg(ab, gb, "tn")
    elif mode == "nt":
        da, db = _dg(gb, bb, "nn"), _dg(gb, ab, "tn")
    else:
        da, db = _dg(bb, gb, "nt"), _dg(ab, gb, "nn")
    return da.astype(a.dtype), db.astype(b.dtype)


bdot.defvjp(_bdot_fwd, _bdot_bwd)


def fdot(a, b, mode):
    return _dg(a, b, mode, precision=HIGHEST)


def _split2(x):
    hi = x.astype(BF16)
    return hi, (x - hi.astype(F32)).astype(BF16)


def _dot01(x, m01):
    hi, lo = _split2(x)
    return _dg(hi, m01, "nn") + _dg(lo, m01, "nn")


def _exact_dot_impl(a, b, mode, exact):
    if exact == "a":
        ae = a.astype(BF16)
        return sum(_dg(ae, part, mode) for part in _split2(b))
    be = b.astype(BF16)
    return sum(_dg(part, be, mode) for part in _split2(a))


@functools.partial(jax.custom_vjp, nondiff_argnums=(2, 3))
def exact_dot(a, b, mode, exact):
    return _exact_dot_impl(a, b, mode, exact)


def _exact_dot_fwd(a, b, mode, exact):
    return _exact_dot_impl(a, b, mode, exact), (a, b)


def _exact_dot_bwd(mode, exact, res, g):
    a, b = res
    if exact == "a":
        db = {"nn": lambda: _exact_dot_impl(a, g, "tn", "a"), "nt": lambda: _exact_dot_impl(g, a, "tn", "b"),
              "tn": lambda: _exact_dot_impl(a, g, "nn", "a")}[mode]()
        return jnp.zeros_like(a), db
    da = {"nn": lambda: _exact_dot_impl(g, b, "nt", "b"), "nt": lambda: _exact_dot_impl(g, b, "nn", "b"),
          "tn": lambda: _exact_dot_impl(b, g, "nt", "a")}[mode]()
    return da, jnp.zeros_like(b)


exact_dot.defvjp(_exact_dot_fwd, _exact_dot_bwd)


def _head_indicator(width, heads, transpose):
    hd = width // heads
    shape = (LANES, width) if transpose else (width, LANES)
    lane = lax.broadcasted_iota(jnp.int32, shape, 1 if not transpose else 0)
    pos = lax.broadcasted_iota(jnp.int32, shape, 0 if not transpose else 1)
    return ((pos >= lane * hd) & (pos < lane * hd + hd)).astype(BF16)


@jax.custom_vjp
def head_sum(x):
    w = x.shape[-1]
    e = _head_indicator(w, w // RWKV_HEAD_DIM, False)
    et = _head_indicator(w, w // RWKV_HEAD_DIM, True)
    return _dot01(_dot01(x, e), et)


head_sum.defvjp(lambda x: (head_sum(x), None), lambda _, g: (head_sum(g),))


def rmsnorm_fn(x, g):
    y = x * lax.rsqrt(jnp.mean(x * x, axis=-1, keepdims=True) + NORM_EPS)
    return ((y * g).astype(BF16),)


def cast_fn(x):
    return (x.astype(BF16),)


def ssd_pre_fn(xbc, xbc1, xbc2, xbc3, dt_raw, conv_w, conv_b, dt_bias):
    c = conv_w[3:4] * xbc + conv_w[2:3] * xbc1 + conv_w[1:2] * xbc2 + conv_w[0:1] * xbc3 + conv_b
    act = c * jax.nn.sigmoid(c)
    dt = jax.nn.softplus(dt_raw + dt_bias)
    return act[:, :SSD_WIDTH], act[:, SSD_WIDTH:SSD_WIDTH + 256], act[:, SSD_WIDTH + 256:], dt


def ssd_post_fn(yscan, z, norm_g):
    y = yscan * (z * jax.nn.sigmoid(z))
    half = SSD_WIDTH // 2
    parts = []
    for g in range(2):
        yg = y[:, g * half:(g + 1) * half]
        parts.append(yg * lax.rsqrt(jnp.mean(yg * yg, axis=-1, keepdims=True) + NORM_EPS))
    return ((jnp.concatenate(parts, axis=-1) * norm_g).astype(BF16),)


def rwkv_pre_fn(rkv, rkv_p, pg, pg_p, pwa, pwa_p, mu_rkv, mu_pg, mu_pwa, w0, w2p, a0, a2p, g2, k_k, k_a):
    w = RWKV_WIDTH
    rkv = rkv + (rkv_p - rkv) * mu_rkv
    pg = pg + (pg_p - pg) * mu_pg
    pwa = pwa + (pwa_p - pwa) * mu_pwa
    r, k, v = rkv[:, :w], rkv[:, w:2 * w], rkv[:, 2 * w:]
    pw, pa = pwa[:, :LANES], pwa[:, LANES:]
    w_log = -jax.nn.softplus(-(w0 + bdot(jnp.tanh(pw), w2p, "nn"))) - 0.5
    lw = -jnp.exp(w_log)
    iclr = jax.nn.sigmoid(a0 + bdot(pa, a2p, "nn"))
    gate = bdot(jax.nn.sigmoid(pg), g2, "nn")
    kk = k * k_k
    kap = kk / jnp.maximum(jnp.sqrt(head_sum(kk * kk)), 1e-12)
    k_mod = k * (1.0 + (iclr - 1.0) * k_a)
    return r, lw, k_mod, v, kap, kap * iclr, gate


def rwkv_post_fn(ys, r, k_mod, v, gate, ln_w, ln_b, r_k):
    inv_n = 1.0 / RWKV_HEAD_DIM
    mean = head_sum(ys) * inv_n
    yc = ys - mean
    var = head_sum(yc * yc) * inv_n
    yn = yc * lax.rsqrt(var + RWKV_LN_EPS) * ln_w + ln_b
    bonus = head_sum(r * k_mod * r_k) * v
    return (((yn + bonus) * gate).astype(BF16),)


def attn_fn(q, kx, vx):
    outs = []
    for h in range(XATTN_HEADS):
        sl = slice(h * XATTN_HEAD_DIM, (h + 1) * XATTN_HEAD_DIM)
        s = bdot(q[:, sl], kx[:, sl], "nt") * (XATTN_HEAD_DIM ** -0.5)
        s = s - jnp.max(s, axis=-1, keepdims=True)
        p = jnp.exp(s)
        p = p / jnp.sum(p, axis=-1, keepdims=True)
        outs.append(bdot(p, vx[:, sl], "nn"))
    return (jnp.concatenate(outs, axis=-1).astype(BF16),)


def loss_fn(x, tgt, g):
    y = x * lax.rsqrt(jnp.mean(x * x, axis=-1, keepdims=True) + NORM_EPS) * g
    err = jnp.square(y - tgt)
    return 0.5 * jnp.sum(jnp.mean(err, axis=-1, keepdims=True), axis=0, keepdims=True)


def _tri_masks(n):
    row = lax.broadcasted_iota(jnp.int32, (n, n), 0)
    col = lax.broadcasted_iota(jnp.int32, (n, n), 1)
    return col <= row, col < row, row == col


def rwkv_chunk_fn(st0, r, lw, k, v, kap, b):
    h, c = r.shape[0], r.shape[1]
    incl, strict, diag = _tri_masks(c)
    cum = exact_dot(jnp.broadcast_to(incl.astype(F32), (h, c, c)), lw, "nn", "a")
    g_in = jnp.exp(cum)
    g_prev = jnp.exp(cum - lw)
    g_inv = jnp.exp(-cum)
    g_end = jnp.exp(cum[:, c - 1:c, :] - cum)
    kap_t, k_t, b_t, r_t = kap * g_prev, k * g_inv, b * g_inv, r * g_in
    a_ub = jnp.where(strict, bdot(kap_t, b_t, "nt"), 0.0)
    a_vk = jnp.where(strict, bdot(kap_t, k_t, "nt"), 0.0)
    rhs = -(bdot(kap_t, st0, "nt") + bdot(a_vk, v, "nn"))
    eye = diag.astype(F32)
    m = -a_ub
    inv = eye + m
    n = 1
    while n * 2 < c:
        m = bdot(m, m, "nn")
        inv = bdot(inv, eye + m, "nn")
        n *= 2
    u = bdot(inv, rhs, "nn")
    y = (bdot(r_t, st0, "nt")
         + bdot(jnp.where(incl, bdot(r_t, k_t, "nt"), 0.0), v, "nn")
         + bdot(jnp.where(incl, bdot(r_t, b_t, "nt"), 0.0), u, "nn"))
    st1 = jnp.exp(cum[:, c - 1:c, :]) * st0 + bdot(v, k * g_end, "tn") + bdot(u, b * g_end, "tn")
    return y, st1


def ssd_chunk_fn(group, h0, xs, bm, cm, dt, a_log, d_skip):
    q, nh = xs.shape[0], SSD_HEADS_PER_GROUP
    causal, _, _ = _tri_masks(q)
    a_row = -jnp.exp(a_log)
    cs_all = exact_dot(causal.astype(F32), dt * a_row, "nn", "a")
    cs_t = cs_all.T
    lanes = range(group * nh, (group + 1) * nh)
    cs = jnp.stack([cs_all[:, hl:hl + 1] for hl in lanes])
    cs_row = jnp.stack([cs_t[hl:hl + 1, :] for hl in lanes])
    dt_h = jnp.stack([dt[:, hl:hl + 1] for hl in lanes])
    d_h = jnp.stack([d_skip[:, hl:hl + 1] for hl in lanes])
    x = _stack_lanes(xs, nh)
    h0s = _stack_rows(h0, nh)
    lmat = jnp.where(causal, jnp.exp(jnp.where(causal, cs - cs_row, 0.0)), 0.0)
    cb = bdot(cm, bm, "nt")
    xdt = x * dt_h
    cl = cs[:, q - 1:q, :]
    cm_b = jnp.broadcast_to(cm, (nh,) + cm.shape)
    bm_b = jnp.broadcast_to(bm, (nh,) + bm.shape)
    y = bdot(cb * lmat, xdt, "nn") + bdot(cm_b, h0s, "nt") * jnp.exp(cs) + x * d_h
    h1 = h0s * jnp.exp(cl) + bdot(xdt * jnp.exp(cl - cs), bm_b, "tn")
    return jnp.concatenate([y[e] for e in range(nh)], axis=-1), jnp.concatenate([h1[e] for e in range(nh)], axis=0)


class Rows:
    def __init__(self, arr, w=None, cb=0, shifts=()):
        self.arr, self.w, self.cb, self.shifts = arr, (arr.shape[1] if w is None else w), cb, tuple(shifts)


def _as_rows(x):
    return x if isinstance(x, Rows) else Rows(x)


def _shift_down(x, halo, k):
    rolled = pltpu.roll(x, k, 0)
    first = rolled[0:SUBLANES]
    rid = lax.broadcasted_iota(jnp.int32, first.shape, 0)
    patched = jnp.where(rid < k, pltpu.roll(halo, k, 0), first)
    return jnp.concatenate([patched, rolled[SUBLANES:]], axis=0)


def _shift_up(g, carry, k):
    tm = g.shape[0]
    rolled = pltpu.roll(g, tm - k, 0)
    last = rolled[tm - SUBLANES:]
    rid = lax.broadcasted_iota(jnp.int32, last.shape, 0)
    patched = jnp.where(rid >= SUBLANES - k, pltpu.roll(carry, SUBLANES - k, 0), last)
    return jnp.concatenate([rolled[:tm - SUBLANES], patched], axis=0)


def _params():
    return pltpu.CompilerParams(vmem_limit_bytes=VMEM_LIMIT)


def _load_rows(refs, pos, rins, first_block):
    vals = []
    for r in rins:
        x = refs[pos][...].astype(F32) if refs[pos].dtype != F32 else refs[pos][...]
        pos += 1
        vals.append(x)
        if r.shifts:
            halo = refs[pos][...]
            pos += 1
            halo = jnp.where(first_block, jnp.zeros_like(halo), halo)
            for k in r.shifts:
                vals.append(_shift_down(x, halo, k))
    return vals, pos


def _row_specs(rins, tm, blk):
    specs, args = [], []
    for r in rins:
        specs.append(pl.BlockSpec((tm, r.w), lambda i, cb=r.cb: (blk(i), cb)))
        args.append(r.arr)
        if r.shifts:
            per = tm // SUBLANES
            specs.append(pl.BlockSpec((SUBLANES, r.w), lambda i, cb=r.cb: (jnp.maximum(blk(i) * per - 1, 0), cb)))
            args.append(r.arr)
    return specs, args


def rowwise_fwd(name, fn, rins, params, outs, tm, deps=()):
    rins = [_as_rows(r) for r in rins]
    t = rins[0].arr.shape[0]
    tm = min(tm, t)
    nb = t // tm
    specs, args = _row_specs(rins, tm, lambda i: i)
    for p in params:
        specs.append(pl.BlockSpec(p.shape, lambda i: (0, 0)))
        args.append(p)
    for dep in deps:
        specs.append(pl.BlockSpec(memory_space=pl.ANY))
        args.append(dep)
    n_in = len(args)

    def body(*refs):
        vals, pos = _load_rows(refs, 0, rins, pl.program_id(0) == 0)
        pv = [refs[pos + j][...] for j in range(len(params))]
        res = fn(*vals, *pv)
        for o_ref, o in zip(refs[n_in:], res):
            o_ref[...] = o.astype(o_ref.dtype)

    return pl.pallas_call(
        body, name=name, grid=(nb,), in_specs=specs,
        out_specs=[pl.BlockSpec((tm, w), lambda i: (i, 0)) for w, _ in outs],
        out_shape=[jax.ShapeDtypeStruct((t, w), dt) for w, dt in outs],
        compiler_params=_params(),
    )(*args)


def rowwise_bwd(name, fn, rins, params, cts, tm, grad_dtypes, row_add=None):
    rins = [_as_rows(r) for r in rins]
    cts = [[_as_rows(c) for c in lst] for lst in cts]
    row_add = [_as_rows(a) for a in (row_add or [])]
    t = rins[0].arr.shape[0]
    tm = min(tm, t)
    nb = t // tm
    rev = lambda i: nb - 1 - i
    specs, args = _row_specs(rins, tm, rev)
    for p in params:
        specs.append(pl.BlockSpec(p.shape, lambda i: (0, 0)))
        args.append(p)
    flat_cts = [c for lst in cts for c in lst] + row_add
    for c in flat_cts:
        specs.append(pl.BlockSpec((tm, c.w), lambda i, cb=c.cb: (rev(i), cb)))
        args.append(c.arr)
    n_in = len(args)
    want = [i for i, d in enumerate(grad_dtypes) if d is not None]
    out_specs = [pl.BlockSpec((tm, rins[i].w), lambda i_: (rev(i_), 0)) for i in want]
    out_shape = [jax.ShapeDtypeStruct((t, rins[i].w), grad_dtypes[i]) for i in want]
    out_specs += [pl.BlockSpec(p.shape, lambda i: (0, 0)) for p in params]
    out_shape += [jax.ShapeDtypeStruct(p.shape, F32) for p in params]
    n_out = len(out_shape)
    scratch = [pltpu.VMEM((SUBLANES, r.w), F32) for r in rins for _ in r.shifts]

    def body(*refs):
        i = pl.program_id(0)
        vals, pos = _load_rows(refs, 0, rins, rev(i) == 0)
        pv = [refs[pos + j][...] for j in range(len(params))]
        pos += len(params)
        outs, vjp = jax.vjp(fn, *vals, *pv)
        ct_vals = []
        for o, lst in zip(outs, cts):
            acc = None
            for _ in lst:
                cv = refs[pos][...].astype(F32)
                pos += 1
                acc = cv if acc is None else acc + cv
            ct_vals.append(acc.astype(o.dtype))
        adds = [refs[pos + j][...].astype(F32) for j in range(len(row_add))]
        grads = vjp(tuple(ct_vals))
        out_refs = refs[n_in:n_in + n_out]
        carry_refs = refs[n_in + n_out:]

        @pl.when(i == 0)
        def _():
            for cr in carry_refs:
                cr[...] = jnp.zeros_like(cr)
            for pr in out_refs[len(want):]:
                pr[...] = jnp.zeros_like(pr)

        gi, ci, oi = 0, 0, 0
        for idx, r in enumerate(rins):
            d = grads[gi]
            gi += 1
            for k in r.shifts:
                dk = grads[gi]
                gi += 1
                d = d + _shift_up(dk, carry_refs[ci][...], k)
                carry_refs[ci][...] = dk[0:SUBLANES]
                ci += 1
            if idx == 0:
                for a in adds:
                    d = d + a
            if grad_dtypes[idx] is not None:
                out_refs[oi][...] = d.astype(out_refs[oi].dtype)
                oi += 1
        for pr, gp in zip(out_refs[len(want):], grads[gi:]):
            pr[...] += gp

    res = pl.pallas_call(
        body, name=name, grid=(nb,), in_specs=specs, out_specs=out_specs, out_shape=out_shape,
        scratch_shapes=scratch, compiler_params=_params(),
    )(*args)
    return res[:len(want)], res[len(want):]


def _pick(n, pref):
    for c in pref:
        if n % c == 0:
            return c
    return n


MM_VMEM_BUDGET = 40 * 1024 * 1024
MM_PEAK_FLOPS = 0.9e15
MM_HBM_BYTES_PER_S = 3.0e12
MM_STEP_SECONDS = 0.35e-6


def _mm_tiles(m, n, k, size_a, size_b, size_out, size_res, single_k):
    best = None
    for tk in sorted({c for c in (k, 2048, 1024, 512, 256, 128) if c <= 2048 and k % c == 0}, reverse=True):
        for tm in (1024, 512, 256, 128):
            if m % tm:
                continue
            for tn in (1024, 768, 512, 384, 256, 128):
                if n % tn:
                    continue
                nk = k // tk
                vmem = 2 * (tm * tk * size_a + tk * tn * size_b + tm * tn * (size_out + size_res))
                vmem += tm * tn * 4 * (2 if nk > 1 or not single_k else 1)
                vmem += (tm * tk * 2 if size_a > 2 else 0) + (tk * tn * 2 if size_b > 2 else 0)
                if vmem > MM_VMEM_BUDGET:
                    continue
                steps = (m // tm) * (n // tn) * nk
                a_reads = 1 if (nk == 1 and single_k) else n // tn
                traffic = m * k * size_a * a_reads + k * n * size_b * (m // tm) + m * n * (size_out + size_res)
                cost = max(2.0 * m * n * k / MM_PEAK_FLOPS, traffic / MM_HBM_BYTES_PER_S) + steps * MM_STEP_SECONDS
                if best is None or cost < best[0]:
                    best = (cost, tm, tn, tk)
    return best[1:]


def mm(name, a, b, mode, out_dtype=F32, res=None, b_slabs=None, out_slabs=None, dep=None, epi=None, extras=(),
       out_dtypes=None):
    if mode == "tn":
        k_dim, m_dim = a.shape
    else:
        m_dim, k_dim = a.shape
    if b_slabs:
        n_dim = b.shape[0] * b.shape[2] if mode == "nn" else b.shape[1]
    else:
        n_dim = b.shape[0] if mode == "nt" else b.shape[1]
    n_slabs = out_slabs or (b_slabs if (b_slabs and mode == "nn") else 1)
    k_slabs = b_slabs if (b_slabs and mode == "nt") else 1
    if epi is None:
        out_dtypes = [out_dtype]
        if res is None:
            epi = lambda acc: (acc,)
        else:
            extras, epi = [res], lambda acc, r: (acc + r,)
    tm, tn, tk = _mm_tiles(m_dim, n_dim // n_slabs, k_dim // k_slabs, a.dtype.itemsize, b.dtype.itemsize,
                           sum(jnp.dtype(dt).itemsize for dt in out_dtypes), sum(e.dtype.itemsize for e in extras),
                           single_k=(k_slabs == 1))
    nji = n_dim // n_slabs // tn
    nki = k_dim // k_slabs // tk
    nblk = lambda js, j: js * nji + j
    kblk = lambda ks, k: ks * nki + k
    if mode == "tn":
        a_spec = pl.BlockSpec((tk, tm), lambda i, js, j, ks, k: (kblk(ks, k), i))
    else:
        a_spec = pl.BlockSpec((tm, tk), lambda i, js, j, ks, k: (i, kblk(ks, k)))
    if b_slabs and mode == "nn":
        b_spec = pl.BlockSpec((None, tk, tn), lambda i, js, j, ks, k: (js, k, j))
    elif b_slabs and mode == "nt":
        b_spec = pl.BlockSpec((None, tn, tk), lambda i, js, j, ks, k: (ks, nblk(js, j), k))
    elif mode == "nt":
        b_spec = pl.BlockSpec((tn, tk), lambda i, js, j, ks, k: (nblk(js, j), kblk(ks, k)))
    else:
        b_spec = pl.BlockSpec((tk, tn), lambda i, js, j, ks, k: (kblk(ks, k), nblk(js, j)))
    specs, args = [a_spec, b_spec], [a, b]
    for e in extras:
        specs.append(pl.BlockSpec((tm, tn), lambda i, js, j, ks, k: (i, nblk(js, j))))
        args.append(e)
    if dep is not None:
        specs.append(pl.BlockSpec(memory_space=pl.ANY))
        args.append(dep)
    if out_slabs:
        o_specs = [pl.BlockSpec((None, tm, tn), lambda i, js, j, ks, k: (js, i, j))]
        o_shapes = [jax.ShapeDtypeStruct((out_slabs, m_dim, n_dim // out_slabs), out_dtypes[0])]
    else:
        o_specs = [pl.BlockSpec((tm, tn), lambda i, js, j, ks, k: (i, nblk(js, j))) for _ in out_dtypes]
        o_shapes = [jax.ShapeDtypeStruct((m_dim, n_dim), dt) for dt in out_dtypes]

    one_k_step = k_slabs * nki == 1
    n_in, n_out = len(args), len(out_dtypes)

    def body(*refs):
        a_ref, b_ref = refs[0], refs[1]
        part = _dg(a_ref[...].astype(BF16), b_ref[...].astype(BF16), mode)

        def finish(acc):
            outs = epi(acc, *[refs[2 + j][...].astype(F32) for j in range(len(extras))])
            for o_ref, o in zip(refs[n_in:n_in + n_out], outs):
                o_ref[...] = o.astype(o_ref.dtype)

        if one_k_step:
            finish(part)
            return
        acc_ref = refs[n_in + n_out]
        ks, kk = pl.program_id(3), pl.program_id(4)

        @pl.when((ks == 0) & (kk == 0))
        def _():
            acc_ref[...] = part

        @pl.when((ks > 0) | (kk > 0))
        def _():
            acc_ref[...] += part

        pl.when((ks == k_slabs - 1) & (kk == nki - 1))(lambda: finish(acc_ref[...]))

    grid = (m_dim // tm, n_slabs, nji, k_slabs, nki)
    scratch = [] if one_k_step else [pltpu.VMEM((tm, tn), F32)]
    out = pl.pallas_call(
        body, name=name, grid=grid, in_specs=specs, out_specs=o_specs, out_shape=o_shapes, scratch_shapes=scratch,
        compiler_params=pltpu.CompilerParams(
            dimension_semantics=("parallel", "parallel", "parallel", "arbitrary", "arbitrary"),
            vmem_limit_bytes=VMEM_LIMIT),
    )(*args)
    return out[0] if n_out == 1 else out


def _stack_lanes(x, n):
    w = x.shape[1] // n
    return jnp.stack([x[:, i * w:(i + 1) * w] for i in range(n)])


def _stack_rows(x, n):
    w = x.shape[0] // n
    return jnp.stack([x[i * w:(i + 1) * w, :] for i in range(n)])


def rwkv_scan_fwd(r, lw, k, v, kap, b):
    t = r.shape[0]
    c, hps, hd = min(RWKV_CHUNK, t), RWKV_HEADS_PER_STEP, RWKV_HEAD_DIM
    nc, ng, wl = t // c, RWKV_HEADS // hps, hps * hd
    spec = pl.BlockSpec((c, wl), lambda g, ci: (ci, g))

    def body(r_ref, lw_ref, k_ref, v_ref, kap_ref, b_ref, y_ref, ck_ref, st_ref):
        @pl.when(pl.program_id(1) == 0)
        def _():
            st_ref[...] = jnp.zeros_like(st_ref)

        st = st_ref[...]
        ck_ref[...] = st
        ins = [x[...] for x in (r_ref, lw_ref, k_ref, v_ref, kap_ref, b_ref)]
        y, st1 = rwkv_chunk_fn(_stack_rows(st, hps), *[_stack_lanes(x, hps) for x in ins])
        y_ref[...] = jnp.concatenate([y[h] for h in range(hps)], axis=-1)
        st_ref[...] = jnp.concatenate([st1[h] for h in range(hps)], axis=0)

    return pl.pallas_call(
        body, name="rwkv_scan_fwd", grid=(ng, nc), in_specs=[spec] * 6,
        out_specs=[spec, pl.BlockSpec((None, wl, hd), lambda g, ci: (ci, g, 0))],
        out_shape=[jax.ShapeDtypeStruct((t, RWKV_WIDTH), F32), jax.ShapeDtypeStruct((nc, RWKV_WIDTH, hd), F32)],
        scratch_shapes=[pltpu.VMEM((wl, hd), F32)], compiler_params=_params(),
    )(r, lw, k, v, kap, b)


def rwkv_scan_bwd(r, lw, k, v, kap, b, ck, dy):
    t = r.shape[0]
    c, hps, hd = min(RWKV_CHUNK, t), RWKV_HEADS_PER_STEP, RWKV_HEAD_DIM
    nc, ng, wl = t // c, RWKV_HEADS // hps, hps * hd
    spec = pl.BlockSpec((c, wl), lambda g, ci: (nc - 1 - ci, g))

    def body(r_ref, lw_ref, k_ref, v_ref, kap_ref, b_ref, ck_ref, dy_ref, *rest):
        out_refs, dst_ref = rest[:6], rest[6]

        @pl.when(pl.program_id(1) == 0)
        def _():
            dst_ref[...] = jnp.zeros_like(dst_ref)

        ins = [x[...] for x in (r_ref, lw_ref, k_ref, v_ref, kap_ref, b_ref)]
        dyv, ck, dst = dy_ref[...].astype(F32), ck_ref[...], dst_ref[...]
        _, vjp = jax.vjp(rwkv_chunk_fn, _stack_rows(ck, hps), *[_stack_lanes(x, hps) for x in ins])
        grads = vjp((_stack_lanes(dyv, hps), _stack_rows(dst, hps)))
        dst_ref[...] = jnp.concatenate([grads[0][h] for h in range(hps)], axis=0)
        for j in range(6):
            out_refs[j][...] = jnp.concatenate([grads[1 + j][h] for h in range(hps)], axis=-1).astype(BF16)

    return pl.pallas_call(
        body, name="rwkv_scan_bwd", grid=(ng, nc),
        in_specs=[spec] * 6 + [pl.BlockSpec((None, wl, hd), lambda g, ci: (nc - 1 - ci, g, 0)), spec],
        out_specs=[spec] * 6, out_shape=[jax.ShapeDtypeStruct((t, RWKV_WIDTH), BF16)] * 6,
        scratch_shapes=[pltpu.VMEM((wl, hd), F32)], compiler_params=_params(),
    )(r, lw, k, v, kap, b, ck, dy)


def _ssd_specs(q, blk):
    gw = SSD_WIDTH // 2
    return [pl.BlockSpec((q, gw), lambda g, ci: (blk(ci), g)),
            pl.BlockSpec((q, SSD_STATE), lambda g, ci: (blk(ci), g)),
            pl.BlockSpec((q, SSD_STATE), lambda g, ci: (blk(ci), g)),
            pl.BlockSpec((q, LANES), lambda g, ci: (blk(ci), 0)),
            pl.BlockSpec((1, LANES), lambda g, ci: (0, 0)),
            pl.BlockSpec((1, LANES), lambda g, ci: (0, 0))]


def ssd_scan_fwd(xs, bm, cm, dt, a_log, d_skip):
    t = xs.shape[0]
    q = min(SSD_CHUNK, t)
    nc, gw = t // q, SSD_WIDTH // 2

    def body(xs_ref, bm_ref, cm_ref, dt_ref, al_ref, d_ref, y_ref, ck_ref, h_ref):
        @pl.when(pl.program_id(1) == 0)
        def _():
            h_ref[...] = jnp.zeros_like(h_ref)

        ck_ref[...] = h_ref[...]
        args = (h_ref[...], xs_ref[...], bm_ref[...], cm_ref[...], dt_ref[...], al_ref[...], d_ref[...])
        g = pl.program_id(0)

        @pl.when(g == 0)
        def _():
            y, h1 = ssd_chunk_fn(0, *args)
            y_ref[...] = y
            h_ref[...] = h1

        @pl.when(g == 1)
        def _():
            y, h1 = ssd_chunk_fn(1, *args)
            y_ref[...] = y
            h_ref[...] = h1

    return pl.pallas_call(
        body, name="ssd_scan_fwd", grid=(2, nc), in_specs=_ssd_specs(q, lambda ci: ci),
        out_specs=[pl.BlockSpec((q, gw), lambda g, ci: (ci, g)),
                   pl.BlockSpec((None, gw, SSD_STATE), lambda g, ci: (ci, g, 0))],
        out_shape=[jax.ShapeDtypeStruct((t, SSD_WIDTH), F32), jax.ShapeDtypeStruct((nc, SSD_WIDTH, SSD_STATE), F32)],
        scratch_shapes=[pltpu.VMEM((gw, SSD_STATE), F32)], compiler_params=_params(),
    )(xs, bm, cm, dt, a_log, d_skip)


def ssd_scan_bwd(xs, bm, cm, dt, a_log, d_skip, ck, dy):
    t = xs.shape[0]
    q = min(SSD_CHUNK, t)
    nc, gw = t // q, SSD_WIDTH // 2
    rev = lambda ci: nc - 1 - ci

    def body(xs_ref, bm_ref, cm_ref, dt_ref, al_ref, d_ref, ck_ref, dy_ref,
             dxs_ref, dbm_ref, dcm_ref, ddt_ref, dal_ref, dd_ref, dh_ref):
        g, ci = pl.program_id(0), pl.program_id(1)

        @pl.when(ci == 0)
        def _():
            dh_ref[...] = jnp.zeros_like(dh_ref)

        @pl.when((ci == 0) & (g == 0))
        def _():
            dal_ref[...] = jnp.zeros_like(dal_ref)
            dd_ref[...] = jnp.zeros_like(dd_ref)

        args = (ck_ref[...], xs_ref[...], bm_ref[...], cm_ref[...], dt_ref[...], al_ref[...], d_ref[...])

        def run(group):
            _, vjp = jax.vjp(functools.partial(ssd_chunk_fn, group), *args)
            dh0, dxs, dbm, dcm, ddt, dal, dd = vjp((dy_ref[...].astype(F32), dh_ref[...]))
            dh_ref[...] = dh0
            dxs_ref[...] = dxs.astype(BF16)
            dbm_ref[...] = dbm.astype(BF16)
            dcm_ref[...] = dcm.astype(BF16)
            ddt_ref[...] = ddt
            dal_ref[...] += dal
            dd_ref[...] += dd

        pl.when(g == 0)(lambda: run(0))
        pl.when(g == 1)(lambda: run(1))

    in_specs = _ssd_specs(q, rev) + [pl.BlockSpec((None, gw, SSD_STATE), lambda g, ci: (rev(ci), g, 0)),
                                     pl.BlockSpec((q, gw), lambda g, ci: (rev(ci), g))]
    return pl.pallas_call(
        body, name="ssd_scan_bwd", grid=(2, nc), in_specs=in_specs,
        out_specs=[pl.BlockSpec((q, gw), lambda g, ci: (rev(ci), g)),
                   pl.BlockSpec((q, SSD_STATE), lambda g, ci: (rev(ci), g)),
                   pl.BlockSpec((q, SSD_STATE), lambda g, ci: (rev(ci), g)),
                   pl.BlockSpec((None, q, LANES), lambda g, ci: (g, rev(ci), 0)),
                   pl.BlockSpec((1, LANES), lambda g, ci: (0, 0)),
                   pl.BlockSpec((1, LANES), lambda g, ci: (0, 0))],
        out_shape=[jax.ShapeDtypeStruct((t, SSD_WIDTH), BF16), jax.ShapeDtypeStruct((t, 2 * SSD_STATE), BF16),
                   jax.ShapeDtypeStruct((t, 2 * SSD_STATE), BF16), jax.ShapeDtypeStruct((2, t, LANES), F32),
                   jax.ShapeDtypeStruct((1, LANES), F32), jax.ShapeDtypeStruct((1, LANES), F32)],
        scratch_shapes=[pltpu.VMEM((gw, SSD_STATE), F32)], compiler_params=_params(),
    )(xs, bm, cm, dt, a_log, d_skip, ck, dy)


def loss_and_grad(x, tgt, g, tm):
    t, d = x.shape
    tm = min(tm, t)
    nb = t // tm

    def body(x_ref, t_ref, g_ref, loss_ref, dx_ref, dg_ref):
        @pl.when(pl.program_id(0) == 0)
        def _():
            loss_ref[...] = jnp.zeros_like(loss_ref)
            dg_ref[...] = jnp.zeros_like(dg_ref)

        val, vjp = jax.vjp(loss_fn, x_ref[...], t_ref[...], g_ref[...])
        dx, _, dg = vjp(jnp.ones((1, 1), F32))
        loss_ref[...] += jnp.broadcast_to(val, loss_ref.shape)
        dx_ref[...] = dx
        dg_ref[...] += dg

    row = pl.BlockSpec((tm, d), lambda i: (i, 0))
    one = pl.BlockSpec((1, d), lambda i: (0, 0))
    return pl.pallas_call(
        body, name="loss_and_grad", grid=(nb,), in_specs=[row, row, one],
        out_specs=[pl.BlockSpec((SUBLANES, LANES), lambda i: (0, 0)), row, one],
        out_shape=[jax.ShapeDtypeStruct((SUBLANES, LANES), F32), jax.ShapeDtypeStruct((t, d), F32),
                   jax.ShapeDtypeStruct((1, d), F32)],
        compiler_params=_params(),
    )(x, tgt, g)


def adamw(name, recv, w, m, v, dep=None):
    rows, cols = w.shape
    n_slabs = recv.shape[0]
    recv_block_bytes = 4 * 1024 * 1024
    tm = _pick(rows, [c for c in (256, 128, 64, 32, 16, 8) if n_slabs * c * cols * 4 <= recv_block_bytes])
    c1 = 1.0 / (1.0 - ADAM_B1 ** ADAM_STEP)
    c2 = 1.0 / (1.0 - ADAM_B2 ** ADAM_STEP)

    n_dep = 0 if dep is None else 1

    def body(recv_ref, w_ref, m_ref, v_ref, *rest):
        g_ref, d_ref, nm_ref, nv_ref = rest[n_dep:]
        g = recv_ref[0].astype(F32)
        for p in range(1, n_slabs):
            g = g + recv_ref[p].astype(F32)
        nm =ADAM_B1 * m_ref[...] + (1.0 - ADAM_B1) * g
        nv = ADAM_B2 * v_ref[...] + (1.0 - ADAM_B2) * jnp.square(g)
        g_ref[...] = g
        nm_ref[...] = nm
        nv_ref[...] = nv
        d_ref[...] = -ADAM_LR * ((nm * c1) / (jnp.sqrt(nv * c2) + ADAM_EPS) + ADAM_WD * w_ref[...])

    blk = pl.BlockSpec((tm, cols), lambda i: (i, 0))
    return pl.pallas_call(
        body, name=name, grid=(rows // tm,),
        in_specs=[pl.BlockSpec((n_slabs, tm, cols), lambda i: (0, i, 0)), blk, blk, blk]
        + [pl.BlockSpec(memory_space=pl.ANY)] * n_dep,
        out_specs=[blk] * 4, out_shape=[jax.ShapeDtypeStruct((rows, cols), F32)] * 4,
        compiler_params=_params(),
    )(recv, w, m, v, *([] if dep is None else [dep]))


def _mesh_pos():
    return lax.axis_index("x"), lax.axis_index("y"), lax.axis_index("c")


def _peer(pos, mask):
    x, y, c = pos
    return (1 - x if mask & 4 else x, 1 - y if mask & 2 else y, 1 - c if mask & 1 else c)


def _linear(pos):
    return 4 * pos[0] + 2 * pos[1] + pos[2]


class Exchange:
    MASKS = {"gather": (1, 2, 3, 4, 5, 6, 7), "scatter": (1, 2, 3, 4, 5, 6, 7), "gather_chips": (1, 2, 4, 6),
             "forward": (2, 4, 6)}

    def __init__(self, xs, kind, lands=None):
        self.kind, self.masks = kind, self.MASKS[kind]
        self.xs = [] if kind == "forward" else list(xs)
        if kind == "forward":
            self.land_shape = [jax.ShapeDtypeStruct(l.shape, l.dtype) for l in lands]
        elif kind == "scatter":
            self.land_shape = [jax.ShapeDtypeStruct(x.shape, x.dtype) for x in xs]
        else:
            self.land_shape = [jax.ShapeDtypeStruct((N_DEV,) + x.shape, x.dtype) for x in xs]
        self.n = len(self.land_shape)
        copies = self.n * len(self.masks)
        self.sems = [pltpu.SemaphoreType.DMA((copies,)), pltpu.SemaphoreType.DMA((copies,)),
                     pltpu.SemaphoreType.DMA((self.n,))]

    def _copies(self, ins, outs, sems, landing):
        send_sems, recv_sems, local_sems = sems
        me = _mesh_pos()
        me_lin = _linear(me)
        local, remote = [], []
        for ti in range(self.n):
            if self.kind != "forward":
                src_mine = ins[ti].at[me_lin] if self.kind == "scatter" else ins[ti]
                local.append(pltpu.make_async_copy(src_mine, outs[ti].at[me_lin], local_sems.at[ti]))
            for j, mask in enumerate(self.masks):
                if self.kind == "forward":
                    peer = _peer(me, 1)
                    src = outs[ti].at[_linear(_peer(me, mask))]
                    dst = outs[ti].at[_linear(_peer(me, mask ^ 1 if landing else mask))]
                else:
                    peer = _peer(me, mask)
                    src = ins[ti].at[_linear(peer)] if self.kind == "scatter" else ins[ti]
                    dst = outs[ti].at[_linear(peer) if landing else me_lin]
                sem_index = ti * len(self.masks) + j
                remote.append(pltpu.make_async_remote_copy(
                    src_ref=src, dst_ref=dst, send_sem=send_sems.at[sem_index], recv_sem=recv_sems.at[sem_index],
                    device_id=peer, device_id_type=pl.DeviceIdType.MESH))
        return local, remote

    def start(self, ins, outs, sems):
        local, remote = self._copies(ins, outs, sems, landing=False)
        for cp in local + remote:
            cp.start()

    def finish(self, ins, outs, sems):
        local, remote = self._copies(ins, outs, sems, landing=True)
        for cp in remote:
            cp.wait_recv()
        for cp in remote:
            cp.wait_send()
        for cp in local:
            cp.wait()


def exchange_start(name, xs, kind, dep=None, lands=None):
    ex = Exchange(xs, kind, lands)
    hbm = pl.BlockSpec(memory_space=pltpu.HBM)
    sem = pl.BlockSpec(memory_space=pltpu.SEMAPHORE)
    if lands is None:
        lands = [lax.empty(s.shape, s.dtype) for s in ex.land_shape]
    n_src, n = len(ex.xs), ex.n
    n_inputs = n_src + n + (0 if dep is None else 1)

    def body(*refs):
        ins, lnd, sems, token = refs[:n_src], refs[n_src:n_src + n], refs[n_inputs:n_inputs + 3], refs[-1]
        ex.start(ins, lnd, sems)
        token[...] = jnp.zeros_like(token)

    res = pl.pallas_call(
        body, name=name, in_specs=[hbm] * (n_src + n) + ([] if dep is None else [pl.BlockSpec(memory_space=pl.ANY)]),
        out_specs=[sem] * 3 + [hbm] * (n_src + n) + [pl.BlockSpec(memory_space=pltpu.VMEM)],
        out_shape=ex.sems + [pltpu.HBM(x.shape, x.dtype) for x in ex.xs]
        + [pltpu.HBM(s.shape, s.dtype) for s in ex.land_shape] + [jax.ShapeDtypeStruct((SUBLANES, LANES), F32)],
        input_output_aliases={i: 3 + i for i in range(n_src + n)},
        compiler_params=pltpu.CompilerParams(has_side_effects=pltpu.SideEffectType.DATAFLOW_SIDE_EFFECTING),
    )(*[pltpu.with_memory_space_constraint(x, pltpu.HBM) for x in ex.xs + list(lands)],
      *([] if dep is None else [dep]))
    return (ex, res[:3], res[3:3 + n_src], res[3 + n_src:3 + n_src + n]), res[-1]


def exchange_wait(name, handles, after):
    ex, sems, srcs, lands = handles
    n_src, n = len(srcs), len(lands)
    hbm = pl.BlockSpec(memory_space=pltpu.HBM)
    sem = pl.BlockSpec(memory_space=pltpu.SEMAPHORE)

    def body(*refs):
        ins, lnd, sem_refs = refs[:n_src], refs[n_src:n_src + n], refs[n_src + n:n_src + n + 3]
        ex.finish(ins, lnd, sem_refs)

    res = pl.pallas_call(
        body, name=name, in_specs=[hbm] * (n_src + n) + [sem] * 3 + [pl.BlockSpec(memory_space=pl.ANY)],
        out_specs=[hbm] * (n_src + n),
        out_shape=[pltpu.HBM(x.shape, x.dtype) for x in srcs] + [pltpu.HBM(x.shape, x.dtype) for x in lands],
        input_output_aliases={i: i for i in range(n_src + n)},
        compiler_params=pltpu.CompilerParams(has_side_effects=pltpu.SideEffectType.DATAFLOW_SIDE_EFFECTING),
    )(*srcs, *lands, *sems, after)
    return res[n_src:]


def forward_start(name, chip_gather, after):
    lands = exchange_wait(name + "_wait", chip_gather, after)
    return exchange_start(name + "_forward_start", [], "forward", lands=lands)


_Z = (0, 1024)
_XBC = (1024, 2560)
_DT = (2560, 2576)
_RKV = (2576, 5648)
_PW = (5648, 5744)
_PA = (5744, 5840)
_PG = (5840, 6096)
D_IN = 6096

_SMALL = ("norm_mix_g", "ssd_conv_b", "ssd_dt_bias", "ssd_a_log", "ssd_d", "ssd_norm_g", "rwkv_mu", "rwkv_w0",
          "rwkv_a0", "rwkv_k_k", "rwkv_k_a", "rwkv_r_k", "rwkv_ln_w", "rwkv_ln_b", "norm_x_g", "norm_mem_g",
          "norm_ffn_g", "final_norm_g")
_WEIGHTS = ("norm_mix_g", "w_in", "ssd_conv_w", "ssd_conv_b", "ssd_dt_bias", "ssd_a_log", "ssd_d", "ssd_norm_g",
            "rwkv_mu", "rwkv_w0", "rwkv_w2", "rwkv_a0", "rwkv_a2", "rwkv_g2", "rwkv_k_k", "rwkv_k_a", "rwkv_r_k",
            "rwkv_ln_w", "rwkv_ln_b", "w_out", "norm_x_g", "norm_mem_g", "xattn_wq", "xattn_wk", "xattn_wv",
            "xattn_wo", "norm_ffn_g", "ffn_w1", "ffn_w2", "final_norm_g")


def _pad_lanes(x, width=LANES):
    return jnp.pad(x, ((0, 0), (0, width - x.shape[1])))


def _pack_small(vals):
    flat = jnp.concatenate([vals[n].reshape(-1) for n in _SMALL])
    rows = -(-flat.shape[0] // (LANES * SUBLANES)) * SUBLANES
    return jnp.pad(flat, (0, rows * LANES - flat.shape[0])).reshape(rows, LANES)


def _unpack_small(packed, shapes):
    flat = packed.reshape(-1)
    out, pos = {}, 0
    for n in _SMALL:
        size = 1
        for s in shapes[n]:
            size *= s
        out[n] = flat[pos:pos + size].reshape(shapes[n])
        pos += size
    return out


def _rows(w, rng):
    return w[rng[0]:rng[1]]


def sum_slabs(name, recv):
    n, rows, cols = recv.shape
    tc = _pick(cols, (256, 128))

    def body(r_ref, o_ref):
        acc = r_ref[0].astype(F32)
        for p in range(1, n):
            acc = acc + r_ref[p].astype(F32)
        o_ref[...] = acc

    return pl.pallas_call(
        body, name=name, grid=(cols // tc,), in_specs=[pl.BlockSpec((n, rows, tc), lambda j: (0, 0, j))],
        out_specs=pl.BlockSpec((rows, tc), lambda j: (0, j)), out_shape=jax.ShapeDtypeStruct((rows, cols), F32),
        compiler_params=_params(),
    )(recv)


def kernel(x, mem, norm_mix_g, w_in, ssd_conv_w, ssd_conv_b, ssd_dt_bias, ssd_a_log, ssd_d, ssd_norm_g, rwkv_mu, rwkv_w0, rwkv_w2, rwkv_a0, rwkv_a2, rwkv_g2, rwkv_k_k, rwkv_k_a, rwkv_r_k, rwkv_ln_w, rwkv_ln_b, w_out, norm_x_g, norm_mem_g, xattn_wq, xattn_wk, xattn_wv, xattn_wo, norm_ffn_g, ffn_w1, ffn_w2, final_norm_g, loss_target, m_norm_mix_g, m_w_in, m_ssd_conv_w, m_ssd_conv_b, m_ssd_dt_bias, m_ssd_a_log, m_ssd_d, m_ssd_norm_g, m_rwkv_mu, m_rwkv_w0, m_rwkv_w2, m_rwkv_a0, m_rwkv_a2, m_rwkv_g2, m_rwkv_k_k, m_rwkv_k_a, m_rwkv_r_k, m_rwkv_ln_w, m_rwkv_ln_b, m_w_out, m_norm_x_g, m_norm_mem_g, m_xattn_wq, m_xattn_wk, m_xattn_wv, m_xattn_wo, m_norm_ffn_g, m_ffn_w1, m_ffn_w2, m_final_norm_g, v_norm_mix_g, v_w_in, v_ssd_conv_w, v_ssd_conv_b, v_ssd_dt_bias, v_ssd_a_log, v_ssd_d, v_ssd_norm_g, v_rwkv_mu, v_rwkv_w0, v_rwkv_w2, v_rwkv_a0, v_rwkv_a2, v_rwkv_g2, v_rwkv_k_k, v_rwkv_k_a, v_rwkv_r_k, v_rwkv_ln_w, v_rwkv_ln_b, v_w_out, v_norm_x_g, v_norm_mem_g, v_xattn_wq, v_xattn_wk, v_xattn_wv, v_xattn_wo, v_norm_ffn_g, v_ffn_w1, v_ffn_w2, v_final_norm_g):
    given = dict(locals())
    wts = {n: given[n] for n in _WEIGHTS}
    mom_m = {n: given["m_" + n] for n in _WEIGHTS}
    mom_v = {n: given["v_" + n] for n in _WEIGHTS}
    d = D_MODEL
    xt, memt, tgt = x[0], mem[0], loss_target[0]
    tm = 256

    big = {"w_in": jnp.transpose(w_in[0]), "w_out": w_out[0], "xattn_wq": xattn_wq[0], "xattn_wk": xattn_wk[0],
           "xattn_wv": xattn_wv[0], "xattn_wo": xattn_wo[0], "ffn_w1": ffn_w1[0], "ffn_w2": ffn_w2[0]}
    small_sh = {"ssd_conv_w": ssd_conv_w.reshape(4, -1), "rwkv_w2": rwkv_w2[0], "rwkv_a2": rwkv_a2[0],
                "rwkv_g2": rwkv_g2[0]}
    cast_one = lambda n, deps=(): rowwise_fwd("cast_" + n, cast_fn, [big[n]], [], [(big[n].shape[1], BF16)],
                                              256 if big[n].shape[0] % 256 == 0 else big[n].shape[0], deps=deps)[0]
    gather_in, token_in = exchange_start("gather_in_start", [cast_one("w_in")] + list(small_sh.values()), "gather_chips")
    cast = {n: cast_one(n, deps=[token_in]) for n in big if n != "w_in"}
    late_a = ("w_out", "xattn_wq", "xattn_wk", "xattn_wv", "xattn_wo")
    late_b = ("ffn_w1", "ffn_w2")
    gather_a, token_a = exchange_start("gather_attn_start", [cast[n] for n in late_a], "gather_chips", dep=token_in)
    gather_b, token_b = exchange_start("gather_ffn_start", [cast[n] for n in late_b], "gather_chips", dep=token_a)
    (h1,) = rowwise_fwd("norm_mix", rmsnorm_fn, [xt], [norm_mix_g], [(d, BF16)], tm, deps=[token_b])
    forward_in, token_in = forward_start("gather_in", gather_in, after=h1)
    gathered = exchange_wait("gather_in_forward_wait", forward_in, after=token_in)
    g_big = {"w_in": gathered[0]}
    g_small = dict(zip(small_sh, gathered[1:]))

    pad_rows = lambda a: jnp.pad(a, ((0, LANES - a.shape[0]), (0, 0)))
    w_in_t = g_big["w_in"].reshape(D_IN, d)
    wt_z, wt_xbc, wt_rkv, wt_pg = (_rows(w_in_t, r) for r in (_Z, _XBC, _RKV, _PG))
    wt_sm = jnp.concatenate([pad_rows(_rows(w_in_t, r)) for r in (_PW, _PA, _DT)], axis=0)
    unshard_cols = lambda g: jnp.transpose(g, (1, 0, 2)).reshape(g.shape[1], -1)
    conv_w_f = unshard_cols(g_small["ssd_conv_w"])
    w2p, a2p = pad_rows(unshard_cols(g_small["rwkv_w2"])), pad_rows(unshard_cols(g_small["rwkv_a2"]))
    g2_f = unshard_cols(g_small["rwkv_g2"])

    mu = rwkv_mu
    mu_rkv, mu_pg = mu[:, :3072], mu[:, 3264:3520]
    mu_pwa = jnp.concatenate([_pad_lanes(mu[:, 3072:3168]), _pad_lanes(mu[:, 3168:3264])], axis=1)
    dt_bias_p, a_log_p, d_p = _pad_lanes(ssd_dt_bias), _pad_lanes(ssd_a_log), _pad_lanes(ssd_d)
    r_k_row = rwkv_r_k.reshape(1, RWKV_WIDTH)
    g_final = final_norm_g.reshape(1, d)

    u_z = mm("in_z", h1, wt_z, "nt")
    u_xbc = mm("in_xbc", h1, wt_xbc, "nt")
    u_rkv = mm("in_rkv", h1, wt_rkv, "nt")
    u_pg = mm("in_pg", h1, wt_pg, "nt")
    u_sm = mm("in_small", h1, wt_sm, "nt")

    ssd_pre_rows = lambda: [Rows(u_xbc, shifts=(1, 2, 3)), Rows(u_sm, LANES, 2)]
    ssd_pre_params = [conv_w_f, ssd_conv_b, dt_bias_p]
    xs, bm, cm, dt = rowwise_fwd("ssd_pre", ssd_pre_fn, ssd_pre_rows(), ssd_pre_params,
                                 [(SSD_WIDTH, F32), (256, F32), (256, F32), (LANES, F32)], tm)
    y_scan, ssd_ck = ssd_scan_fwd(xs, bm, cm, dt, a_log_p, d_p)
    (y_ssd,) = rowwise_fwd("ssd_post", ssd_post_fn, [y_scan, u_z], [ssd_norm_g], [(SSD_WIDTH, BF16)], tm)

    rwkv_pre_rows = lambda: [Rows(u_rkv, shifts=(1,)), Rows(u_pg, shifts=(1,)), Rows(u_sm, 2 * LANES, 0, shifts=(1,))]
    rwkv_pre_params = [mu_rkv, mu_pg, mu_pwa, rwkv_w0, w2p, rwkv_a0, a2p, g2_f, rwkv_k_k, rwkv_k_a]
    forward_a, token_a = forward_start("gather_attn", gather_a, after=y_ssd)
    r_, lw_, k_, v_, kap_, b_, gate_ = rowwise_fwd("rwkv_pre", rwkv_pre_fn, rwkv_pre_rows(), rwkv_pre_params,
                                                   [(RWKV_WIDTH, F32)] * 7, 128, deps=[token_a])
    ys_r, rwkv_ck = rwkv_scan_fwd(r_, lw_, k_, v_, kap_, b_)
    forward_b, token_b = forward_start("gather_ffn", gather_b, after=ys_r)
    g_big.update(zip(late_a, exchange_wait("gather_attn_forward_wait", forward_a, after=token_b)))
    w_out_f = g_big["w_out"].reshape(d, d)
    wq_f, wk_f, wv_f, wo_f = (g_big[n].reshape(d, d) for n in ("xattn_wq", "xattn_wk", "xattn_wv", "xattn_wo"))
    rwkv_post_params = [rwkv_ln_w, rwkv_ln_b, r_k_row]
    (y_rwkv,) = rowwise_fwd("rwkv_post", rwkv_post_fn, [ys_r, r_, k_, v_, gate_], rwkv_post_params,
                            [(RWKV_WIDTH, BF16)], tm)
    ycat = jnp.concatenate([y_ssd, y_rwkv], axis=1)
    x1 = mm("out_proj", ycat, w_out_f, "nn", res=xt)

    (h2,) = rowwise_fwd("norm_x", rmsnorm_fn, [x1], [norm_x_g], [(d, BF16)], tm)
    (mn,) = rowwise_fwd("norm_mem", rmsnorm_fn, [memt], [norm_mem_g], [(d, BF16)], tm)
    q = mm("xattn_q", h2, wq_f, "nn", out_dtype=BF16)
    kx = mm("xattn_k", mn, wk_f, "nn")
    vx = mm("xattn_v", mn, wv_f, "nn")
    (o,) = rowwise_fwd("xattn", attn_fn, [q], [kx, vx], [(d, BF16)], tm)
    x2 = mm("xattn_o", o, wo_f, "nn", res=x1)

    (h3,) = rowwise_fwd("norm_ffn", rmsnorm_fn, [x2], [norm_ffn_g], [(d, BF16)], tm)
    w1_s, w2_g = exchange_wait("gather_ffn_forward_wait", forward_b, after=h3)
    w2_f = w2_g.reshape(D_FF, d)
    relu2_epi = lambda acc: (jnp.square(jnp.maximum(acc, 0.0)), jnp.maximum(acc, 0.0))
    hid, relu_a = mm("ffn_1", h3, w1_s, "nn", b_slabs=N_DEV, epi=relu2_epi, out_dtypes=[BF16, BF16])
    x3 = mm("ffn_2", hid, w2_f, "nn", res=x2)

    loss_blk, dx3, dg_final = loss_and_grad(x3, tgt, g_final, tm)

    grads = {}
    grads["ffn_w2"] = mm("d_ffn_w2", hid, dx3, "tn", out_dtype=BF16).reshape(N_DEV, D_FF // N_DEV, d)
    sc_w2, tok = exchange_start("scatter_ffn_w2_start", [grads["ffn_w2"]], "scatter")
    da = mm("d_hid", dx3, w2_f, "nt", dep=tok, epi=lambda acc, ra: (2.0 * acc * ra,), extras=[relu_a],
            out_dtypes=[BF16])
    grads["ffn_w1"] = mm("d_ffn_w1", h3, da, "tn", out_dtype=BF16, out_slabs=N_DEV)
    sc_w1, tok = exchange_start("scatter_ffn_w1_start", [grads["ffn_w1"]], "scatter")
    dh3 = mm("d_h3", da, w1_s, "nt", out_dtype=BF16, b_slabs=N_DEV, dep=tok)
    (dx2,), (dg_ffn,) = rowwise_bwd("norm_ffn_bwd", rmsnorm_fn, [x2], [norm_ffn_g], [[dh3]], tm, [F32], row_add=[dx3])

    grads["xattn_wo"] = mm("d_wo", o, dx2, "tn", out_dtype=BF16).reshape(N_DEV, d // N_DEV, d)
    sc_wo, tok = exchange_start("scatter_wo_start", [grads["xattn_wo"]], "scatter")
    d_o = mm("d_o", dx2, wo_f, "nt", out_dtype=BF16, dep=tok)
    (dq,), (dkx, dvx) = rowwise_bwd("xattn_bwd", attn_fn, [q], [kx, vx], [[d_o]], tm, [BF16])
    grads["xattn_wq"] = mm("d_wq", h2, dq, "tn", out_dtype=BF16).reshape(N_DEV, d // N_DEV, d)
    grads["xattn_wk"] = mm("d_wk", mn, dkx, "tn", out_dtype=BF16).reshape(N_DEV, d // N_DEV, d)
    grads["xattn_wv"] = mm("d_wv", mn, dvx, "tn", out_dtype=BF16).reshape(N_DEV, d // N_DEV, d)
    qkv = ("xattn_wq", "xattn_wk", "xattn_wv")
    sc_qkv, tok = exchange_start("scatter_qkv_start", [grads[n] for n in qkv], "scatter")
    dmn = mm("d_mn_v", dvx, wv_f, "nt", res=mm("d_mn_k", dkx, wk_f, "nt", dep=tok))
    _, (dg_mem,) = rowwise_bwd("norm_mem_bwd", rmsnorm_fn, [memt], [norm_mem_g], [[dmn]], tm, [None])
    dh2 = mm("d_h2", dq, wq_f, "nt", out_dtype=BF16, dep=dg_mem)
    (dx1,), (dg_x,) = rowwise_bwd("norm_x_bwd", rmsnorm_fn, [x1], [norm_x_g], [[dh2]], tm, [F32], row_add=[dx2])

    grads["w_out"] = mm("d_w_out", ycat, dx1, "tn", out_dtype=BF16).reshape(N_DEV, d // N_DEV, d)
    sc_wout, tok = exchange_start("scatter_w_out_start", [grads["w_out"]], "scatter")
    d_ycat = mm("d_ycat", dx1, w_out_f, "nt", out_dtype=BF16, dep=tok)

    (d_ys, d_r1, d_k1, d_v1, d_gate), (dln_w, dln_b, dr_k) = rowwise_bwd(
        "rwkv_post_bwd", rwkv_post_fn, [ys_r, r_, k_, v_, gate_], rwkv_post_params,
        [[Rows(d_ycat, RWKV_WIDTH, 1)]], tm, [BF16] * 5)
    d_r2, d_lw, d_k2, d_v2, d_kap, d_b = rwkv_scan_bwd(r_, lw_, k_, v_, kap_, b_, rwkv_ck, d_ys)
    (du_rkv, du_pg, du_pwa), rwkv_pg = rowwise_bwd(
        "rwkv_pre_bwd", rwkv_pre_fn, rwkv_pre_rows(), rwkv_pre_params,
        [[d_r1, d_r2], [d_lw], [d_k1, d_k2], [d_v1, d_v2], [d_kap], [d_b], [d_gate]], 128, [BF16] * 3)
    dmu_rkv, dmu_pg, dmu_pwa, dw0, dw2p, da0, da2p, dg2, dk_k, dk_a = rwkv_pg

    (d_yscan, du_z), (dssd_norm_g,) = rowwise_bwd("ssd_post_bwd", ssd_post_fn, [y_scan, u_z], [ssd_norm_g],
                                                  [[Rows(d_ycat, SSD_WIDTH, 0)]], tm, [BF16, BF16])
    dxs, dbm, dcm, ddt2, da_log_p, dd_p = ssd_scan_bwd(xs, bm, cm, dt, a_log_p, d_p, ssd_ck, d_yscan)
    (du_xbc, du_dt), (dconv_w, dconv_b, ddt_bias_p) = rowwise_bwd(
        "ssd_pre_bwd", ssd_pre_fn, ssd_pre_rows(), ssd_pre_params,
        [[dxs], [dbm], [dcm], [ddt2[0], ddt2[1]]], tm, [BF16, BF16])
    du_sm = jnp.concatenate([du_pwa, du_dt], axis=1)

    dwt_z = mm("d_w_z", du_z, h1, "tn", out_dtype=BF16)
    dwt_xbc = mm("d_w_xbc", du_xbc, h1, "tn", out_dtype=BF16)
    dwt_rkv = mm("d_w_rkv", du_rkv, h1, "tn", out_dtype=BF16)
    dwt_pg = mm("d_w_pg", du_pg, h1, "tn", out_dtype=BF16)
    dwt_sm = mm("d_w_small", du_sm, h1, "tn", out_dtype=BF16)
    dwt_full = jnp.concatenate([dwt_z, dwt_xbc, dwt_sm[256:272], dwt_rkv, dwt_sm[0:96], dwt_sm[128:224], dwt_pg], axis=0)
    to_slabs = lambda g: jnp.transpose(g.reshape(g.shape[0], N_DEV, -1), (1, 0, 2))
    grads["w_in"] = dwt_full.reshape(N_DEV, D_IN // N_DEV, d)
    grads["ssd_conv_w"] = to_slabs(dconv_w)
    grads["rwkv_w2"] = to_slabs(dw2p[:96])
    grads["rwkv_a2"] = to_slabs(da2p[:96])
    grads["rwkv_g2"] = to_slabs(dg2)
    tail = ("w_in", "ssd_conv_w", "rwkv_w2", "rwkv_a2", "rwkv_g2")
    sc_tail, tok = exchange_start("scatter_tail_start", [grads[n] for n in tail], "scatter")
    dh1 = mm("d_h1_z", du_z, wt_z, "nn", dep=tok)
    dh1 = mm("d_h1_xbc", du_xbc, wt_xbc, "nn", res=dh1)
    dh1 = mm("d_h1_rkv", du_rkv, wt_rkv, "nn", res=dh1)
    dh1 = mm("d_h1_pg", du_pg, wt_pg, "nn", res=dh1)
    dh1 = mm("d_h1_small", du_sm, wt_sm, "nn", res=dh1)
    (dx,), (dg_mix,) = rowwise_bwd("norm_mix_bwd", rmsnorm_fn, [xt], [norm_mix_g], [[dh1]], tm, [F32], row_add=[dx1])

    dmu =jnp.concatenate([dmu_rkv, dmu_pwa[:, 0:96], dmu_pwa[:, 128:224], dmu_pg], axis=1)
    small_grads = {
        "norm_mix_g": dg_mix, "ssd_conv_b": dconv_b, "ssd_dt_bias": ddt_bias_p[:, :16], "ssd_a_log": da_log_p[:, :16],
        "ssd_d": dd_p[:, :16], "ssd_norm_g": dssd_norm_g, "rwkv_mu": dmu, "rwkv_w0": dw0, "rwkv_a0": da0,
        "rwkv_k_k": dk_k, "rwkv_k_a": dk_a, "rwkv_r_k": dr_k, "rwkv_ln_w": dln_w, "rwkv_ln_b": dln_b,
        "norm_x_g": dg_x, "norm_mem_g": dg_mem, "norm_ffn_g": dg_ffn, "final_norm_g": dg_final}

    gather_small, tok = exchange_start("gather_small_start", [_pack_small(small_grads)], "gather")
    received = {}
    for names, handle in ((("ffn_w2",), sc_w2), (("ffn_w1",), sc_w1), (("xattn_wo",), sc_wo), (qkv, sc_qkv),
                          (("w_out",), sc_wout)):
        received.update(zip(names, exchange_wait("scatter_" + names[0] + "_wait", handle, after=tok)))

    out_g, out_d, out_m, out_v = {}, {}, {}, {}

    def run_adamw(n, dep):
        shape = wts[n].shape
        two_d = lambda a: a.reshape(-1, shape[-1])
        if n == "w_in":
            recv = jnp.transpose(sum_slabs("sum_w_in", received[n]))[None]
        else:
            recv = received[n].reshape(N_DEV, -1, shape[-1])
        res = adamw("adamw_" + n, recv, two_d(wts[n]), two_d(mom_m[n]), two_d(mom_v[n]), dep=dep)
        out_g[n], out_d[n], out_m[n], out_v[n] = (r.reshape(shape) for r in res)
        return res[0]

    last = None
    for n in ("ffn_w2", "ffn_w1", "xattn_wo") + qkv + ("w_out",):
        last = run_adamw(n, last)
    received.update(zip(tail, exchange_wait("scatter_tail_wait", sc_tail, after=last)))
    for n in tail:
        last = run_adamw(n, last)
    (small_all,) = exchange_wait("gather_small_wait", gather_small, after=last)
    res = adamw("adamw_small", small_all, _pack_small(wts), _pack_small(mom_m), _pack_small(mom_v))
    shapes = {n: wts[n].shape for n in _SMALL}
    for dst, packed in zip((out_g, out_d, out_m, out_v), res):
        dst.update(_unpack_small(packed, shapes))

    loss = lax.psum(loss_blk[0, 0], ("x", "y", "c"))
    return (loss, dx[None], *[out_g[n] for n in _WEIGHTS], *[out_d[n] for n in _WEIGHTS],
            *[out_m[n] for n in _WEIGHTS], *[out_v[n] for n in _WEIGHTS])
```

```python
import functools

import jax
import jax.numpy as jnp
from jax import lax
from jax.experimental import pallas as pl
from jax.experimental.pallas import tpu as pltpu

F32 = jnp.float32
BF16 = jnp.bfloat16
HIGHEST = lax.Precision.HIGHEST

N_DEV = 8
D_MODEL = 2048
NORM_EPS = 1e-6
SSD_WIDTH = 1024
SSD_CONV_DIM = 1536
SSD_HEADS = 16
SSD_HEAD_DIM = 64
SSD_STATE = 128
SSD_CHUNK = 128
SSD_HEADS_PER_GROUP = 8
RWKV_WIDTH = 1024
RWKV_HEADS = 16
RWKV_HEAD_DIM = 64
RWKV_LN_EPS = 64e-5
RWKV_CHUNK = 64
RWKV_HEADS_PER_STEP = 16
XATTN_HEADS = 4
XATTN_HEAD_DIM = 512
D_FF = 8192
LANES = 128
SUBLANES = 8
VMEM_LIMIT = 56 * 1024 * 1024

ADAM_LR = 0.001
ADAM_B1 = 0.9
ADAM_B2 = 0.999
ADAM_EPS = 1e-08
ADAM_WD = 0.01
ADAM_STEP = 10

_DN = {"nn": ((1,), (0,)), "nt": ((1,), (1,)), "tn": ((0,), (0,))}


def _dg(a, b, mode, precision=None):
    (ca,), (cb,) = _DN[mode]
    dn = (((ca + 1,), (cb + 1,)), ((0,), (0,))) if a.ndim == 3 else (((ca,), (cb,)), ((), ()))
    return lax.dot_general(a, b, dn, precision=precision, preferred_element_type=F32)


@functools.partial(jax.custom_vjp, nondiff_argnums=(2,))
def bdot(a, b, mode):
    return _dg(a.astype(BF16), b.astype(BF16), mode)


def _bdot_fwd(a, b, mode):
    return bdot(a, b, mode), (a, b)


def _bdot_bwd(mode, res, g):
    a, b = res
    ab, bb, gb = a.astype(BF16), b.astype(BF16), g.astype(BF16)
    if mode == "nn":
        da, db = _dg(gb, bb, "nt"), _dg(ab, gb, "tn")
    elif mode == "nt":
        da, db = _dg(gb, bb, "nn"), _dg(gb, ab, "tn")
    else:
        da, db = _dg(bb, gb, "nt"), _dg(ab, gb, "nn")
    return da.astype(a.dtype), db.astype(b.dtype)


bdot.defvjp(_bdot_fwd, _bdot_bwd)


def fdot(a, b, mode):
    return _dg(a, b, mode, precision=HIGHEST)


def _split2(x):
    hi = x.astype(BF16)
    return hi, (x - hi.astype(F32)).astype(BF16)


def _dot01(x, m01):
    hi, lo = _split2(x)
    return _dg(hi, m01, "nn") + _dg(lo, m01, "nn")


def _exact_dot_impl(a, b, mode, exact):
    if exact == "a":
        ae = a.astype(BF16)
        return sum(_dg(ae, part, mode) for part in _split2(b))
    be = b.astype(BF16)
    return sum(_dg(part, be, mode) for part in _split2(a))


@functools.partial(jax.custom_vjp, nondiff_argnums=(2, 3))
def exact_dot(a, b, mode, exact):
    return _exact_dot_impl(a, b, mode, exact)


def _exact_dot_fwd(a, b, mode, exact):
    return _exact_dot_impl(a, b, mode, exact), (a, b)


def _exact_dot_bwd(mode, exact, res, g):
    a, b = res
    if exact == "a":
        db = {"nn": lambda: _exact_dot_impl(a, g, "tn", "a"), "nt": lambda: _exact_dot_impl(g, a, "tn", "b"),
              "tn": lambda: _exact_dot_impl(a, g, "nn", "a")}[mode]()
        return jnp.zeros_like(a), db
    da = {"nn": lambda: _exact_dot_impl(g, b, "nt", "b"), "nt": lambda: _exact_dot_impl(g, b, "nn", "b"),
          "tn": lambda: _exact_dot_impl(b, g, "nt", "a")}[mode]()
    return da, jnp.zeros_like(b)


exact_dot.defvjp(_exact_dot_fwd, _exact_dot_bwd)


def _head_indicator(width, heads, transpose):
    hd = width // heads
    shape = (LANES, width) if transpose else (width, LANES)
    lane = lax.broadcasted_iota(jnp.int32, shape, 1 if not transpose else 0)
    pos = lax.broadcasted_iota(jnp.int32, shape, 0 if not transpose else 1)
    return ((pos >= lane * hd) & (pos < lane * hd + hd)).astype(BF16)


@jax.custom_vjp
def head_sum(x):
    w = x.shape[-1]
    e = _head_indicator(w, w // RWKV_HEAD_DIM, False)
    et = _head_indicator(w, w // RWKV_HEAD_DIM, True)
    return _dot01(_dot01(x, e), et)


head_sum.defvjp(lambda x: (head_sum(x), None), lambda _, g: (head_sum(g),))


def rmsnorm_fn(x, g):
    y = x * lax.rsqrt(jnp.mean(x * x, axis=-1, keepdims=True) + NORM_EPS)
    return ((y * g).astype(BF16),)


def cast_fn(x):
    return (x.astype(BF16),)


def ssd_pre_fn(xbc, xbc1, xbc2, xbc3, dt_raw, conv_w, conv_b, dt_bias):
    c = conv_w[3:4] * xbc + conv_w[2:3] * xbc1 + conv_w[1:2] * xbc2 + conv_w[0:1] * xbc3 + conv_b
    act = c * jax.nn.sigmoid(c)
    dt = jax.nn.softplus(dt_raw + dt_bias)
    return act[:, :SSD_WIDTH], act[:, SSD_WIDTH:SSD_WIDTH + 256], act[:, SSD_WIDTH + 256:], dt


def ssd_post_fn(yscan, z, norm_g):
    y = yscan * (z * jax.nn.sigmoid(z))
    half = SSD_WIDTH // 2
    parts = []
    for g in range(2):
        yg = y[:, g * half:(g + 1) * half]
        parts.append(yg * lax.rsqrt(jnp.mean(yg * yg, axis=-1, keepdims=True) + NORM_EPS))
    return ((jnp.concatenate(parts, axis=-1) * norm_g).astype(BF16),)


def rwkv_pre_fn(rkv, rkv_p, pg, pg_p, pwa, pwa_p, mu_rkv, mu_pg, mu_pwa, w0, w2p, a0, a2p, g2, k_k, k_a):
    w = RWKV_WIDTH
    rkv = rkv + (rkv_p - rkv) * mu_rkv
    pg = pg + (pg_p - pg) * mu_pg
    pwa = pwa + (pwa_p - pwa) * mu_pwa
    r, k, v = rkv[:, :w], rkv[:, w:2 * w], rkv[:, 2 * w:]
    pw, pa = pwa[:, :LANES], pwa[:, LANES:]
    w_log = -jax.nn.softplus(-(w0 + bdot(jnp.tanh(pw), w2p, "nn"))) - 0.5
    lw = -jnp.exp(w_log)
    iclr = jax.nn.sigmoid(a0 + bdot(pa, a2p, "nn"))
    gate = bdot(jax.nn.sigmoid(pg), g2, "nn")
    kk = k * k_k
    kap = kk / jnp.maximum(jnp.sqrt(head_sum(kk * kk)), 1e-12)
    k_mod = k * (1.0 + (iclr - 1.0) * k_a)
    return r, lw, k_mod, v, kap, kap * iclr, gate


def rwkv_post_fn(ys, r, k_mod, v, gate, ln_w, ln_b, r_k):
    inv_n = 1.0 / RWKV_HEAD_DIM
    mean = head_sum(ys) * inv_n
    yc = ys - mean
    var = head_sum(yc * yc) * inv_n
    yn = yc * lax.rsqrt(var + RWKV_LN_EPS) * ln_w + ln_b
    bonus = head_sum(r * k_mod * r_k) * v
    return (((yn + bonus) * gate).astype(BF16),)


def attn_fn(q, kx, vx):
    outs = []
    for h in range(XATTN_HEADS):
        sl = slice(h * XATTN_HEAD_DIM, (h + 1) * XATTN_HEAD_DIM)
        s = bdot(q[:, sl], kx[:, sl], "nt") * (XATTN_HEAD_DIM ** -0.5)
        s = s - jnp.max(s, axis=-1, keepdims=True)
        p = jnp.exp(s)
        p = p / jnp.sum(p, axis=-1, keepdims=True)
        outs.append(bdot(p, vx[:, sl], "nn"))
    return (jnp.concatenate(outs, axis=-1).astype(BF16),)


def loss_fn(x, tgt, g):
    y = x * lax.rsqrt(jnp.mean(x * x, axis=-1, keepdims=True) + NORM_EPS) * g
    err = jnp.square(y - tgt)
    return 0.5 * jnp.sum(jnp.mean(err, axis=-1, keepdims=True), axis=0, keepdims=True)


def _tri_masks(n):
    row = lax.broadcasted_iota(jnp.int32, (n, n), 0)
    col = lax.broadcasted_iota(jnp.int32, (n, n), 1)
    return col <= row, col < row, row == col


@jax.custom_vjp
def unit_lower_inverse(a):
    c = a.shape[-1]
    eye = _tri_masks(c)[2].astype(F32)
    m = -a
    inv = eye + m
    n = 1
    while n * 2 < c:
        m = bdot(m, m, "nn")
        inv = bdot(inv, eye + m, "nn")
        n *= 2
    return inv


def _unit_lower_inverse_fwd(a):
    inv = unit_lower_inverse(a)
    return inv, inv


def _unit_lower_inverse_bwd(inv, g):
    return (-bdot(bdot(inv, g, "tn"), inv, "nt"),)


unit_lower_inverse.defvjp(_unit_lower_inverse_fwd, _unit_lower_inverse_bwd)


@jax.custom_vjp
def known_inverse(a, inv):
    return inv


known_inverse.defvjp(lambda a, inv: (inv, inv),
                     lambda inv, g: (_unit_lower_inverse_bwd(inv, g)[0], jnp.zeros_like(inv)))


def rwkv_chunk_fn(st0, r, lw, k, v, kap, b, inv=None):
    h, c = r.shape[0], r.shape[1]
    incl, strict, _ = _tri_masks(c)
    cum = exact_dot(jnp.broadcast_to(incl.astype(F32), (h, c, c)), lw, "nn", "a")
    g_in = jnp.exp(cum)
    g_prev = jnp.exp(cum - lw)
    g_inv = jnp.exp(-cum)
    g_end = jnp.exp(cum[:, c - 1:c, :] - cum)
    kap_t, k_t, b_t, r_t = kap * g_prev, k * g_inv, b * g_inv, r * g_in
    a_ub = jnp.where(strict, bdot(kap_t, b_t, "nt"), 0.0)
    a_vk = jnp.where(strict, bdot(kap_t, k_t, "nt"), 0.0)
    rhs = -(bdot(kap_t, st0, "nt") + bdot(a_vk, v, "nn"))
    inv = unit_lower_inverse(a_ub) if inv is None else known_inverse(a_ub, inv)
    u = bdot(inv, rhs, "nn")
    y = (bdot(r_t, st0, "nt")
         + bdot(jnp.where(incl, bdot(r_t, k_t, "nt"), 0.0), v, "nn")
         + bdot(jnp.where(incl, bdot(r_t, b_t, "nt"), 0.0), u, "nn"))
    st1 = jnp.exp(cum[:, c - 1:c, :]) * st0 + bdot(v, k * g_end, "tn") + bdot(u, b * g_end, "tn")
    return y, st1, inv


def ssd_chunk_fn(group, h0, xs, bm, cm, dt, a_log, d_skip):
    q, nh = xs.shape[0], SSD_HEADS_PER_GROUP
    causal, _, _ = _tri_masks(q)
    a_row = -jnp.exp(a_log)
    cs_all = exact_dot(causal.astype(F32), dt * a_row, "nn", "a")
    cs_t = cs_all.T
    lanes = range(group * nh, (group + 1) * nh)
    cs = jnp.stack([cs_all[:, hl:hl + 1] for hl in lanes])
    cs_row = jnp.stack([cs_t[hl:hl + 1, :] for hl in lanes])
    dt_h = jnp.stack([dt[:, hl:hl + 1] for hl in lanes])
    d_h = jnp.stack([d_skip[:, hl:hl + 1] for hl in lanes])
    x = _stack_lanes(xs, nh)
    h0s = _stack_rows(h0, nh)
    lmat = jnp.where(causal, jnp.exp(jnp.where(causal, cs - cs_row, 0.0)), 0.0)
    cb = bdot(cm, bm, "nt")
    xdt = x * dt_h
    cl = cs[:, q - 1:q, :]
    cm_b = jnp.broadcast_to(cm, (nh,) + cm.shape)
    bm_b = jnp.broadcast_to(bm, (nh,) + bm.shape)
    y = bdot(cb * lmat, xdt, "nn") + bdot(cm_b, h0s, "nt") * jnp.exp(cs) + x * d_h
    h1 = h0s * jnp.exp(cl) + bdot(xdt * jnp.exp(cl - cs), bm_b, "tn")
    return jnp.concatenate([y[e] for e in range(nh)], axis=-1), jnp.concatenate([h1[e] for e in range(nh)], axis=0)


class Rows:
    def __init__(self, arr, w=None, cb=0, shifts=()):
        self.arr, self.w, self.cb, self.shifts = arr, (arr.shape[1] if w is None else w), cb, tuple(shifts)


def _as_rows(x):
    return x if isinstance(x, Rows) else Rows(x)


def _shift_down(x, halo, k):
    rolled = pltpu.roll(x, k, 0)
    first = rolled[0:SUBLANES]
    rid = lax.broadcasted_iota(jnp.int32, first.shape, 0)
    patched = jnp.where(rid < k, pltpu.roll(halo, k, 0), first)
    return jnp.concatenate([patched, rolled[SUBLANES:]], axis=0)


def _shift_up(g, carry, k):
    tm = g.shape[0]
    rolled = pltpu.roll(g, tm - k, 0)
    last = rolled[tm - SUBLANES:]
    rid = lax.broadcasted_iota(jnp.int32, last.shape, 0)
    patched = jnp.where(rid >= SUBLANES - k, pltpu.roll(carry, SUBLANES - k, 0), last)
    return jnp.concatenate([rolled[:tm - SUBLANES], patched], axis=0)


def _params():
    return pltpu.CompilerParams(vmem_limit_bytes=VMEM_LIMIT)


def _load_rows(refs, pos, rins, first_block):
    vals = []
    for r in rins:
        x = refs[pos][...].astype(F32) if refs[pos].dtype != F32 else refs[pos][...]
        pos += 1
        vals.append(x)
        if r.shifts:
            halo = refs[pos][...]
            pos += 1
            halo = jnp.where(first_block, jnp.zeros_like(halo), halo)
            for k in r.shifts:
                vals.append(_shift_down(x, halo, k))
    return vals, pos


def _row_specs(rins, tm, blk):
    specs, args = [], []
    for r in rins:
        specs.append(pl.BlockSpec((tm, r.w), lambda i, cb=r.cb: (blk(i), cb)))
        args.append(r.arr)
        if r.shifts:
            per = tm // SUBLANES
            specs.append(pl.BlockSpec((SUBLANES, r.w), lambda i, cb=r.cb: (jnp.maximum(blk(i) * per - 1, 0), cb)))
            args.append(r.arr)
    return specs, args


def rowwise_fwd(name, fn, rins, params, outs, tm, deps=()):
    rins = [_as_rows(r) for r in rins]
    t = rins[0].arr.shape[0]
    tm = min(tm, t)
    nb = t // tm
    specs, args = _row_specs(rins, tm, lambda i: i)
    for p in params:
        specs.append(pl.BlockSpec(p.shape, lambda i: (0, 0)))
        args.append(p)
    for dep in deps:
        specs.append(pl.BlockSpec(memory_space=pl.ANY))
        args.append(dep)
    n_in = len(args)

    def body(*refs):
        vals, pos = _load_rows(refs, 0, rins, pl.program_id(0) == 0)
        pv = [refs[pos + j][...] for j in range(len(params))]
        res = fn(*vals, *pv)
        for o_ref, o in zip(refs[n_in:], res):
            o_ref[...] = o.astype(o_ref.dtype)

    return pl.pallas_call(
        body, name=name, grid=(nb,), in_specs=specs,
        out_specs=[pl.BlockSpec((tm, w), lambda i: (i, 0)) for w, _ in outs],
        out_shape=[jax.ShapeDtypeStruct((t, w), dt) for w, dt in outs],
        compiler_params=_params(),
    )(*args)


def rowwise_bwd(name, fn, rins, params, cts, tm, grad_dtypes, row_add=None):
    rins = [_as_rows(r) for r in rins]
    cts = [[_as_rows(c) for c in lst] for lst in cts]
    row_add = [_as_rows(a) for a in (row_add or [])]
    t = rins[0].arr.shape[0]
    tm = min(tm, t)
    nb = t // tm
    rev = lambda i: nb - 1 - i
    specs, args = _row_specs(rins, tm, rev)
    for p in params:
        specs.append(pl.BlockSpec(p.shape, lambda i: (0, 0)))
        args.append(p)
    flat_cts = [c for lst in cts for c in lst] + row_add
    for c in flat_cts:
        specs.append(pl.BlockSpec((tm, c.w), lambda i, cb=c.cb: (rev(i), cb)))
        args.append(c.arr)
    n_in = len(args)
    want = [i for i, d in enumerate(grad_dtypes) if d is not None]
    out_specs = [pl.BlockSpec((tm, rins[i].w), lambda i_: (rev(i_), 0)) for i in want]
    out_shape = [jax.ShapeDtypeStruct((t, rins[i].w), grad_dtypes[i]) for i in want]
    out_specs += [pl.BlockSpec(p.shape, lambda i: (0, 0)) for p in params]
    out_shape += [jax.ShapeDtypeStruct(p.shape, F32) for p in params]
    n_out = len(out_shape)
    scratch = [pltpu.VMEM((SUBLANES, r.w), F32) for r in rins for _ in r.shifts]

    def body(*refs):
        i = pl.program_id(0)
        vals, pos = _load_rows(refs, 0, rins, rev(i) == 0)
        pv = [refs[pos + j][...] for j in range(len(params))]
        pos += len(params)
        outs, vjp = jax.vjp(fn, *vals, *pv)
        ct_vals = []
        for o, lst in zip(outs, cts):
            acc = None
            for _ in lst:
                cv = refs[pos][...].astype(F32)
                pos += 1
                acc = cv if acc is None else acc + cv
            ct_vals.append(acc.astype(o.dtype))
        adds = [refs[pos + j][...].astype(F32) for j in range(len(row_add))]
        grads = vjp(tuple(ct_vals))
        out_refs = refs[n_in:n_in + n_out]
        carry_refs = refs[n_in + n_out:]

        @pl.when(i == 0)
        def _():
            for cr in carry_refs:
                cr[...] = jnp.zeros_like(cr)
            for pr in out_refs[len(want):]:
                pr[...] = jnp.zeros_like(pr)

        gi, ci, oi = 0, 0, 0
        for idx, r in enumerate(rins):
            d = grads[gi]
            gi += 1
            for k in r.shifts:
                dk = grads[gi]
                gi += 1
                d = d + _shift_up(dk, carry_refs[ci][...], k)
                carry_refs[ci][...] = dk[0:SUBLANES]
                ci += 1
            if idx == 0:
                for a in adds:
                    d = d + a
            if grad_dtypes[idx] is not None:
                out_refs[oi][...] = d.astype(out_refs[oi].dtype)
                oi += 1
        for pr, gp in zip(out_refs[len(want):], grads[gi:]):
            pr[...] += gp

    res = pl.pallas_call(
        body, name=name, grid=(nb,), in_specs=specs, out_specs=out_specs, out_shape=out_shape,
        scratch_shapes=scratch, compiler_params=_params(),
    )(*args)
    return res[:len(want)], res[len(want):]


def _pick(n, pref):
    for c in pref:
        if n % c == 0:
            return c
    return n


MM_VMEM_BUDGET = 40 * 1024 * 1024
MM_PEAK_FLOPS = 0.9e15
MM_HBM_BYTES_PER_S = 3.0e12
MM_STEP_SECONDS = 0.35e-6


def _mm_tiles(m, n, k, size_a, size_b, size_out, size_res, single_k):
    best = None
    for tk in sorted({c for c in (k, 2048, 1024, 512, 256, 128) if c <= 2048 and k % c == 0}, reverse=True):
        for tm in (1024, 512, 256, 128):
            if m % tm:
                continue
            for tn in (1024, 768, 512, 384, 256, 128):
                if n % tn:
                    continue
                nk = k // tk
                vmem = 2 * (tm * tk * size_a + tk * tn * size_b + tm * tn * (size_out + size_res))
                vmem += tm * tn * 4 * (2 if nk > 1 or not single_k else 1)
                vmem += (tm * tk * 2 if size_a > 2 else 0) + (tk * tn * 2 if size_b > 2 else 0)
                if vmem > MM_VMEM_BUDGET:
                    continue
                steps = (m // tm) * (n // tn) * nk
                a_reads = 1 if (nk == 1 and single_k) else n // tn
                traffic = m * k * size_a * a_reads + k * n * size_b * (m // tm) + m * n * (size_out + size_res)
                cost = max(2.0 * m * n * k / MM_PEAK_FLOPS, traffic / MM_HBM_BYTES_PER_S) + steps * MM_STEP_SECONDS
                if best is None or cost < best[0]:
                    best = (cost, tm, tn, tk)
    return best[1:]


def mm(name, a, b, mode, out_dtype=F32, res=None, b_slabs=None, out_slabs=None, dep=None, epi=None, extras=(),
       out_dtypes=None):
    if mode == "tn":
        k_dim, m_dim = a.shape
    else:
        m_dim, k_dim = a.shape
    if b_slabs:
        n_dim = b.shape[0] * b.shape[2] if mode == "nn" else b.shape[1]
    else:
        n_dim = b.shape[0] if mode == "nt" else b.shape[1]
    n_slabs = out_slabs or (b_slabs if (b_slabs and mode == "nn") else 1)
    k_slabs = b_slabs if (b_slabs and mode == "nt") else 1
    if epi is None:
        out_dtypes = [out_dtype]
        if res is None:
            epi = lambda acc: (acc,)
        else:
            extras, epi = [res], lambda acc, r: (acc + r,)
    tm, tn, tk = _mm_tiles(m_dim, n_dim // n_slabs, k_dim // k_slabs, a.dtype.itemsize, b.dtype.itemsize,
                           sum(jnp.dtype(dt).itemsize for dt in out_dtypes), sum(e.dtype.itemsize for e in extras),
                           single_k=(k_slabs == 1))
    nji = n_dim // n_slabs // tn
    nki = k_dim // k_slabs // tk
    nblk = lambda js, j: js * nji + j
    kblk = lambda ks, k: ks * nki + k
    if mode == "tn":
        a_spec = pl.BlockSpec((tk, tm), lambda i, js, j, ks, k: (kblk(ks, k), i))
    else:
        a_spec = pl.BlockSpec((tm, tk), lambda i, js, j, ks, k: (i, kblk(ks, k)))
    if b_slabs and mode == "nn":
        b_spec = pl.BlockSpec((None, tk, tn), lambda i, js, j, ks, k: (js, k, j))
    elif b_slabs and mode == "nt":
        b_spec = pl.BlockSpec((None, tn, tk), lambda i, js, j, ks, k: (ks, nblk(js, j), k))
    elif mode == "nt":
        b_spec = pl.BlockSpec((tn, tk), lambda i, js, j, ks, k: (nblk(js, j), kblk(ks, k)))
    else:
        b_spec = pl.BlockSpec((tk, tn), lambda i, js, j, ks, k: (kblk(ks, k), nblk(js, j)))
    specs, args = [a_spec, b_spec], [a, b]
    for e in extras:
        specs.append(pl.BlockSpec((tm, tn), lambda i, js, j, ks, k: (i, nblk(js, j))))
        args.append(e)
    if dep is not None:
        specs.append(pl.BlockSpec(memory_space=pl.ANY))
        args.append(dep)
    if out_slabs:
        o_specs = [pl.BlockSpec((None, tm, tn), lambda i, js, j, ks, k: (js, i, j))]
        o_shapes = [jax.ShapeDtypeStruct((out_slabs, m_dim, n_dim // out_slabs), out_dtypes[0])]
    else:
        o_specs = [pl.BlockSpec((tm, tn), lambda i, js, j, ks, k: (i, nblk(js, j))) for _ in out_dtypes]
        o_shapes = [jax.ShapeDtypeStruct((m_dim, n_dim), dt) for dt in out_dtypes]

    one_k_step = k_slabs * nki == 1
    n_in, n_out = len(args), len(out_dtypes)

    def body(*refs):
        a_ref, b_ref = refs[0], refs[1]
        part = _dg(a_ref[...].astype(BF16), b_ref[...].astype(BF16), mode)

        def finish(acc):
            outs = epi(acc, *[refs[2 + j][...].astype(F32) for j in range(len(extras))])
            for o_ref, o in zip(refs[n_in:n_in + n_out], outs):
                o_ref[...] = o.astype(o_ref.dtype)

        if one_k_step:
            finish(part)
            return
        acc_ref = refs[n_in + n_out]
        ks, kk = pl.program_id(3), pl.program_id(4)

        @pl.when((ks == 0) & (kk == 0))
        def _():
            acc_ref[...] = part

        @pl.when((ks > 0) | (kk > 0))
        def _():
            acc_ref[...] += part

        pl.when((ks == k_slabs - 1) & (kk == nki - 1))(lambda: finish(acc_ref[...]))

    grid = (m_dim // tm, n_slabs, nji, k_slabs, nki)
    scratch = [] if one_k_step else [pltpu.VMEM((tm, tn), F32)]
    out = pl.pallas_call(
        body, name=name, grid=grid, in_specs=specs, out_specs=o_specs, out_shape=o_shapes, scratch_shapes=scratch,
        compiler_params=pltpu.CompilerParams(
            dimension_semantics=("parallel", "parallel", "parallel", "arbitrary", "arbitrary"),
            vmem_limit_bytes=VMEM_LIMIT),
    )(*args)
    return out[0] if n_out == 1 else out


def _stack_lanes(x, n):
    w = x.shape[1] // n
    return jnp.stack([x[:, i * w:(i + 1) * w] for i in range(n)])


def _stack_rows(x, n):
    w = x.shape[0] // n
    return jnp.stack([x[i * w:(i + 1) * w, :] for i in range(n)])


def rwkv_scan_fwd(r, lw, k, v, kap, b):
    t = r.shape[0]
    c, hps, hd = min(RWKV_CHUNK, t), RWKV_HEADS_PER_STEP, RWKV_HEAD_DIM
    nc, ng, wl = t // c, RWKV_HEADS // hps, hps * hd
    spec = pl.BlockSpec((c, wl), lambda g, ci: (ci, g))

    def body(r_ref, lw_ref, k_ref, v_ref, kap_ref, b_ref, y_ref, ck_ref, inv_ref, st_ref):
        @pl.when(pl.program_id(1) == 0)
        def _():
            st_ref[...] = jnp.zeros_like(st_ref)

        st = st_ref[...]
        ck_ref[...] = st
        ins = [x[...] for x in (r_ref, lw_ref, k_ref, v_ref, kap_ref, b_ref)]
        y, st1, inv = rwkv_chunk_fn(_stack_rows(st, hps), *[_stack_lanes(x, hps) for x in ins])
        y_ref[...] = jnp.concatenate([y[h] for h in range(hps)], axis=-1)
        st_ref[...] = jnp.concatenate([st1[h] for h in range(hps)], axis=0)
        inv_ref[...] = jnp.concatenate([inv[h] for h in range(hps)], axis=0)

    return pl.pallas_call(
        body, name="rwkv_scan_fwd", grid=(ng, nc), in_specs=[spec] * 6,
        out_specs=[spec, pl.BlockSpec((None, wl, hd), lambda g, ci: (ci, g, 0)),
                   pl.BlockSpec((None, hps * c, c), lambda g, ci: (ci, g, 0))],
        out_shape=[jax.ShapeDtypeStruct((t, RWKV_WIDTH), F32), jax.ShapeDtypeStruct((nc, RWKV_WIDTH, hd), F32),
                   jax.ShapeDtypeStruct((nc, RWKV_HEADS * c, c), F32)],
        scratch_shapes=[pltpu.VMEM((wl, hd), F32)], compiler_params=_params(),
    )(r, lw, k, v, kap, b)


def rwkv_scan_bwd(r, lw, k, v, kap, b, ck, inv_ck, dy):
    t = r.shape[0]
    c, hps, hd = min(RWKV_CHUNK, t), RWKV_HEADS_PER_STEP, RWKV_HEAD_DIM
    nc, ng, wl = t // c, RWKV_HEADS // hps, hps * hd
    spec = pl.BlockSpec((c, wl), lambda g, ci: (nc - 1 - ci, g))

    def body(r_ref, lw_ref, k_ref, v_ref, kap_ref, b_ref, ck_ref, inv_ref, dy_ref, *rest):
        out_refs, dst_ref = rest[:6], rest[6]

        @pl.when(pl.program_id(1) == 0)
        def _():
            dst_ref[...] = jnp.zeros_like(dst_ref)

        ins = [x[...] for x in (r_ref, lw_ref, k_ref, v_ref, kap_ref, b_ref)]
        dyv, ck, dst = dy_ref[...].astype(F32), ck_ref[...], dst_ref[...]
        chunk = lambda *a: rwkv_chunk_fn(*a, inv=_stack_rows(inv_ref[...], hps))[:2]
        _, vjp = jax.vjp(chunk, _stack_rows(ck, hps), *[_stack_lanes(x, hps) for x in ins])
        grads = vjp((_stack_lanes(dyv, hps), _stack_rows(dst, hps)))
        dst_ref[...] = jnp.concatenate([grads[0][h] for h in range(hps)], axis=0)
        for j in range(6):
            out_refs[j][...] = jnp.concatenate([grads[1 + j][h] for h in range(hps)], axis=-1).astype(BF16)

    return pl.pallas_call(
        body, name="rwkv_scan_bwd", grid=(ng, nc),
        in_specs=[spec] * 6 + [pl.BlockSpec((None, wl, hd), lambda g, ci: (nc - 1 - ci, g, 0)),
                               pl.BlockSpec((None, hps * c, c), lambda g, ci: (nc - 1 - ci, g, 0)), spec],
        out_specs=[spec] * 6, out_shape=[jax.ShapeDtypeStruct((t, RWKV_WIDTH), BF16)] * 6,
        scratch_shapes=[pltpu.VMEM((wl, hd), F32)], compiler_params=_params(),
    )(r, lw, k, v, kap, b, ck, inv_ck, dy)


def _ssd_specs(q, blk):
    gw = SSD_WIDTH // 2
    return [pl.BlockSpec((q, gw), lambda g, ci: (blk(ci), g)),
            pl.BlockSpec((q, SSD_STATE), lambda g, ci: (blk(ci), g)),
            pl.BlockSpec((q, SSD_STATE), lambda g, ci: (blk(ci), g)),
            pl.BlockSpec((q, LANES), lambda g, ci: (blk(ci), 0)),
            pl.BlockSpec((1, LANES), lambda g, ci: (0, 0)),
            pl.BlockSpec((1, LANES), lambda g, ci: (0, 0))]


def ssd_scan_fwd(xs, bm, cm, dt, a_log, d_skip):
    t = xs.shape[0]
    q = min(SSD_CHUNK, t)
    nc, gw = t // q, SSD_WIDTH // 2

    def body(xs_ref, bm_ref, cm_ref, dt_ref, al_ref, d_ref, y_ref, ck_ref, h_ref):
        @pl.when(pl.program_id(1) == 0)
        def _():
            h_ref[...] = jnp.zeros_like(h_ref)

        ck_ref[...] = h_ref[...]
        args = (h_ref[...], xs_ref[...], bm_ref[...], cm_ref[...], dt_ref[...], al_ref[...], d_ref[...])
        g = pl.program_id(0)

        @pl.when(g == 0)
        def _():
            y, h1 = ssd_chunk_fn(0, *args)
            y_ref[...] = y
            h_ref[...] = h1

        @pl.when(g == 1)
        def _():
            y, h1 = ssd_chunk_fn(1, *args)
            y_ref[...] = y
            h_ref[...] = h1

    return pl.pallas_call(
        body, name="ssd_scan_fwd", grid=(2, nc), in_specs=_ssd_specs(q, lambda ci: ci),
        out_specs=[pl.BlockSpec((q, gw), lambda g, ci: (ci, g)),
                   pl.BlockSpec((None, gw, SSD_STATE), lambda g, ci: (ci, g, 0))],
        out_shape=[jax.ShapeDtypeStruct((t, SSD_WIDTH), F32), jax.ShapeDtypeStruct((nc, SSD_WIDTH, SSD_STATE), F32)],
        scratch_shapes=[pltpu.VMEM((gw, SSD_STATE), F32)], compiler_params=_params(),
    )(xs, bm, cm, dt, a_log, d_skip)


def ssd_scan_bwd(xs, bm, cm, dt, a_log, d_skip, ck, dy):
    t = xs.shape[0]
    q = min(SSD_CHUNK, t)
    nc, gw = t // q, SSD_WIDTH // 2
    rev = lambda ci: nc - 1 - ci

    def body(xs_ref, bm_ref, cm_ref, dt_ref, al_ref, d_ref, ck_ref, dy_ref,
             dxs_ref, dbm_ref, dcm_ref, ddt_ref, dal_ref, dd_ref, dh_ref):
        g, ci = pl.program_id(0), pl.program_id(1)

        @pl.when(ci == 0)
        def _():
            dh_ref[...] = jnp.zeros_like(dh_ref)

        @pl.when((ci == 0) & (g == 0))
        def _():
            dal_ref[...] = jnp.zeros_like(dal_ref)
            dd_ref[...] = jnp.zeros_like(dd_ref)

        args = (ck_ref[...], xs_ref[...], bm_ref[...], cm_ref[...], dt_ref[...], al_ref[...], d_ref[...])

        def run(group):
            _, vjp = jax.vjp(functools.partial(ssd_chunk_fn, group), *args)
            dh0, dxs, dbm, dcm, ddt, dal, dd = vjp((dy_ref[...].astype(F32), dh_ref[...]))
            dh_ref[...] = dh0
            dxs_ref[...] = dxs.astype(BF16)
            dbm_ref[...] = dbm.astype(BF16)
            dcm_ref[...] = dcm.astype(BF16)
            ddt_ref[...] = ddt
            dal_ref[...] += dal
            dd_ref[...] += dd

        pl.when(g == 0)(lambda: run(0))
        pl.when(g == 1)(lambda: run(1))

    in_specs = _ssd_specs(q, rev) + [pl.BlockSpec((None, gw, SSD_STATE), lambda g, ci: (rev(ci), g, 0)),
                                     pl.BlockSpec((q, gw), lambda g, ci: (rev(ci), g))]
    return pl.pallas_call(
        body, name="ssd_scan_bwd", grid=(2, nc), in_specs=in_specs,
        out_specs=[pl.BlockSpec((q, gw), lambda g, ci: (rev(ci), g)),
                   pl.BlockSpec((q, SSD_STATE), lambda g, ci: (rev(ci), g)),
                   pl.BlockSpec((q, SSD_STATE), lambda g, ci: (rev(ci), g)),
                   pl.BlockSpec((None, q, LANES), lambda g, ci: (g, rev(ci), 0)),
                   pl.BlockSpec((1, LANES), lambda g, ci: (0, 0)),
                   pl.BlockSpec((1, LANES), lambda g, ci: (0, 0))],
        out_shape=[jax.ShapeDtypeStruct((t, SSD_WIDTH), BF16), jax.ShapeDtypeStruct((t, 2 * SSD_STATE), BF16),
                   jax.ShapeDtypeStruct((t, 2 * SSD_STATE), BF16), jax.ShapeDtypeStruct((2, t, LANES), F32),
                   jax.ShapeDtypeStruct((1, LANES), F32), jax.ShapeDtypeStruct((1, LANES), F32)],
        scratch_shapes=[pltpu.VMEM((gw, SSD_STATE), F32)], compiler_params=_params(),
    )(xs, bm, cm, dt, a_log, d_skip, ck, dy)


def loss_and_grad(x, tgt, g, tm):
    t, d = x.shape
    tm = min(tm, t)
    nb = t // tm

    def body(x_ref, t_ref, g_ref, loss_ref, dx_ref, dg_ref):
        @pl.when(pl.program_id(0) == 0)
        def _():
            loss_ref[...] = jnp.zeros_like(loss_ref)
            dg_ref[...] = jnp.zeros_like(dg_ref)

        val, vjp = jax.vjp(loss_fn, x_ref[...], t_ref[...], g_ref[...])
        dx, _, dg = vjp(jnp.ones((1, 1), F32))
        loss_ref[...] += jnp.broadcast_to(val, loss_ref.shape)
        dx_ref[...] = dx
        dg_ref[...] += dg

    row = pl.BlockSpec((tm, d), lambda i: (i, 0))
    one = pl.BlockSpec((1, d), lambda i: (0, 0))
    return pl.pallas_call(
        body, name="loss_and_grad", grid=(nb,), in_specs=[row, row, one],
        out_specs=[pl.BlockSpec((SUBLANES, LANES), lambda i: (0, 0)), row, one],
        out_shape=[jax.ShapeDtypeStruct((SUBLANES, LANES), F32), jax.ShapeDtypeStruct((t, d), F32),
                   jax.ShapeDtypeStruct((1, d), F32)],
        compiler_params=_params(),
    )(x, tgt, g)


def adamw(name, recv, w, m, v, dep=None):
    rows, cols = w.shape
    n_slabs = recv.shape[0]
    recv_block_bytes = 4 * 1024 * 1024
    tm = _pick(rows, [c for c in (256, 128, 64, 32, 16, 8) if n_slabs * c * cols * 4 <= recv_block_bytes])
    c1 = 1.0 / (1.0 - ADAM_B1 ** ADAM_STEP)
    c2 = 1.0 / (1.0 - ADAM_B2 ** ADAM_STEP)

    n_dep = 0 if dep is None else 1

    def body(recv_ref, w_ref, m_ref, v_ref, *rest):
        g_ref, d_ref, nm_ref, nv_ref = rest[n_dep:]
        g = recv_ref[0].astype(F32)
        for p in range(1, n_slabs):
            g = g + recv_ref[p].astype(F32)
        nm =ADAM_B1 * m_ref[...] + (1.0 - ADAM_B1) * g
        nv = ADAM_B2 * v_ref[...] + (1.0 - ADAM_B2) * jnp.square(g)
        g_ref[...] = g
        nm_ref[...] = nm
        nv_ref[...] = nv
        d_ref[...] = -ADAM_LR * ((nm * c1) / (jnp.sqrt(nv * c2) + ADAM_EPS) + ADAM_WD * w_ref[...])

    blk = pl.BlockSpec((tm, cols), lambda i: (i, 0))
    return pl.pallas_call(
        body, name=name, grid=(rows // tm,),
        in_specs=[pl.BlockSpec((n_slabs, tm, cols), lambda i: (0, i, 0)), blk, blk, blk]
        + [pl.BlockSpec(memory_space=pl.ANY)] * n_dep,
        out_specs=[blk] * 4, out_shape=[jax.ShapeDtypeStruct((rows, cols), F32)] * 4,
        compiler_params=_params(),
    )(recv, w, m, v, *([] if dep is None else [dep]))


def _mesh_pos():
    return lax.axis_index("x"), lax.axis_index("y"), lax.axis_index("c")


def _peer(pos, mask):
    x, y, c = pos
    return (1 - x if mask & 4 else x, 1 - y if mask & 2 else y, 1 - c if mask & 1 else c)


def _linear(pos):
    return 4 * pos[0] + 2 * pos[1] + pos[2]


class Exchange:
    MASKS = {"gather": (1, 2, 3, 4, 5, 6, 7), "scatter": (1, 2, 3, 4, 5, 6, 7), "gather_chips": (1, 2, 4, 6),
             "forward": (2, 4, 6)}

    def __init__(self, xs, kind, lands=None):
        self.kind, self.masks = kind, self.MASKS[kind]
        self.xs = [] if kind == "forward" else list(xs)
        if kind == "forward":
            self.land_shape = [jax.ShapeDtypeStruct(l.shape, l.dtype) for l in lands]
        elif kind == "scatter":
            self.land_shape = [jax.ShapeDtypeStruct(x.shape, x.dtype) for x in xs]
        else:
            self.land_shape = [jax.ShapeDtypeStruct((N_DEV,) + x.shape, x.dtype) for x in xs]
        self.n = len(self.land_shape)
        copies = self.n * len(self.masks)
        self.sems = [pltpu.SemaphoreType.DMA((copies,)), pltpu.SemaphoreType.DMA((copies,)),
                     pltpu.SemaphoreType.DMA((self.n,))]

    def _copies(self, ins, outs, sems, landing):
        send_sems, recv_sems, local_sems = sems
        me = _mesh_pos()
        me_lin = _linear(me)
        local, remote = [], []
        for ti in range(self.n):
            if self.kind != "forward":
                src_mine = ins[ti].at[me_lin] if self.kind == "scatter" else ins[ti]
                local.append(pltpu.make_async_copy(src_mine, outs[ti].at[me_lin], local_sems.at[ti]))
            for j, mask in enumerate(self.masks):
                if self.kind == "forward":
                    peer = _peer(me, 1)
                    src = outs[ti].at[_linear(_peer(me, mask))]
                    dst = outs[ti].at[_linear(_peer(me, mask ^ 1 if landing else mask))]
                else:
                    peer = _peer(me, mask)
                    src = ins[ti].at[_linear(peer)] if self.kind == "scatter" else ins[ti]
                    dst = outs[ti].at[_linear(peer) if landing else me_lin]
                sem_index = ti * len(self.masks) + j
                remote.append(pltpu.make_async_remote_copy(
                    src_ref=src, dst_ref=dst, send_sem=send_sems.at[sem_index], recv_sem=recv_sems.at[sem_index],
                    device_id=peer, device_id_type=pl.DeviceIdType.MESH))
        return local, remote

    def start(self, ins, outs, sems):
        local, remote = self._copies(ins, outs, sems, landing=False)
        for cp in local + remote:
            cp.start()

    def finish(self, ins, outs, sems):
        local, remote = self._copies(ins, outs, sems, landing=True)
        for cp in remote:
            cp.wait_recv()
        for cp in remote:
            cp.wait_send()
        for cp in local:
            cp.wait()


def exchange_start(name, xs, kind, dep=None, lands=None):
    ex = Exchange(xs, kind, lands)
    hbm = pl.BlockSpec(memory_space=pltpu.HBM)
    sem = pl.BlockSpec(memory_space=pltpu.SEMAPHORE)
    if lands is None:
        lands = [lax.empty(s.shape, s.dtype) for s in ex.land_shape]
    n_src, n = len(ex.xs), ex.n
    n_inputs = n_src + n + (0 if dep is None else 1)

    def body(*refs):
        ins, lnd, sems, token = refs[:n_src], refs[n_src:n_src + n], refs[n_inputs:n_inputs + 3], refs[-1]
        ex.start(ins, lnd, sems)
        token[...] = jnp.zeros_like(token)

    res = pl.pallas_call(
        body, name=name, in_specs=[hbm] * (n_src + n) + ([] if dep is None else [pl.BlockSpec(memory_space=pl.ANY)]),
        out_specs=[sem] * 3 + [hbm] * (n_src + n) + [pl.BlockSpec(memory_space=pltpu.VMEM)],
        out_shape=ex.sems + [pltpu.HBM(x.shape, x.dtype) for x in ex.xs]
        + [pltpu.HBM(s.shape, s.dtype) for s in ex.land_shape] + [jax.ShapeDtypeStruct((SUBLANES, LANES), F32)],
        input_output_aliases={i: 3 + i for i in range(n_src + n)},
        compiler_params=pltpu.CompilerParams(has_side_effects=pltpu.SideEffectType.DATAFLOW_SIDE_EFFECTING),
    )(*[pltpu.with_memory_space_constraint(x, pltpu.HBM) for x in ex.xs + list(lands)],
      *([] if dep is None else [dep]))
    return (ex, res[:3], res[3:3 + n_src], res[3 + n_src:3 + n_src + n]), res[-1]


def exchange_wait(name, handles, after):
    ex, sems, srcs, lands = handles
    n_src, n = len(srcs), len(lands)
    hbm = pl.BlockSpec(memory_space=pltpu.HBM)
    sem = pl.BlockSpec(memory_space=pltpu.SEMAPHORE)

    def body(*refs):
        ins, lnd, sem_refs = refs[:n_src], refs[n_src:n_src + n], refs[n_src + n:n_src + n + 3]
        ex.finish(ins, lnd, sem_refs)

    res = pl.pallas_call(
        body, name=name, in_specs=[hbm] * (n_src + n) + [sem] * 3 + [pl.BlockSpec(memory_space=pl.ANY)],
        out_specs=[hbm] * (n_src + n),
        out_shape=[pltpu.HBM(x.shape, x.dtype) for x in srcs] + [pltpu.HBM(x.shape, x.dtype) for x in lands],
        input_output_aliases={i: i for i in range(n_src + n)},
        compiler_params=pltpu.CompilerParams(has_side_effects=pltpu.SideEffectType.DATAFLOW_SIDE_EFFECTING),
    )(*srcs, *lands, *sems, after)
    return res[n_src:]


def forward_start(name, chip_gather, after):
    lands = exchange_wait(name + "_wait", chip_gather, after)
    return exchange_start(name + "_forward_start", [], "forward", lands=lands)


_Z = (0, 1024)
_XBC = (1024, 2560)
_DT = (2560, 2576)
_RKV = (2576, 5648)
_PW = (5648, 5744)
_PA = (5744, 5840)
_PG = (5840, 6096)
D_IN = 6096

_SMALL = ("norm_mix_g", "ssd_conv_b", "ssd_dt_bias", "ssd_a_log", "ssd_d", "ssd_norm_g", "rwkv_mu", "rwkv_w0",
          "rwkv_a0", "rwkv_k_k", "rwkv_k_a", "rwkv_r_k", "rwkv_ln_w", "rwkv_ln_b", "norm_x_g", "norm_mem_g",
          "norm_ffn_g", "final_norm_g")
_WEIGHTS = ("norm_mix_g", "w_in", "ssd_conv_w", "ssd_conv_b", "ssd_dt_bias", "ssd_a_log", "ssd_d", "ssd_norm_g",
            "rwkv_mu", "rwkv_w0", "rwkv_w2", "rwkv_a0", "rwkv_a2", "rwkv_g2", "rwkv_k_k", "rwkv_k_a", "rwkv_r_k",
            "rwkv_ln_w", "rwkv_ln_b", "w_out", "norm_x_g", "norm_mem_g", "xattn_wq", "xattn_wk", "xattn_wv",
            "xattn_wo", "norm_ffn_g", "ffn_w1", "ffn_w2", "final_norm_g")


def _pad_lanes(x, width=LANES):
    return jnp.pad(x, ((0, 0), (0, width - x.shape[1])))


def _pack_small(vals):
    flat = jnp.concatenate([vals[n].reshape(-1) for n in _SMALL])
    rows = -(-flat.shape[0] // (LANES * SUBLANES)) * SUBLANES
    return jnp.pad(flat, (0, rows * LANES - flat.shape[0])).reshape(rows, LANES)


def _unpack_small(packed, shapes):
    flat = packed.reshape(-1)
    out, pos = {}, 0
    for n in _SMALL:
        size = 1
        for s in shapes[n]:
            size *= s
        out[n] = flat[pos:pos + size].reshape(shapes[n])
        pos += size
    return out


def _rows(w, rng):
    return w[rng[0]:rng[1]]


def sum_slabs(name, recv):
    n, rows, cols = recv.shape
    tc = _pick(cols, (256, 128))

    def body(r_ref, o_ref):
        acc = r_ref[0].astype(F32)
        for p in range(1, n):
            acc = acc + r_ref[p].astype(F32)
        o_ref[...] = acc

    return pl.pallas_call(
        body, name=name, grid=(cols // tc,), in_specs=[pl.BlockSpec((n, rows, tc), lambda j: (0, 0, j))],
        out_specs=pl.BlockSpec((rows, tc), lambda j: (0, j)), out_shape=jax.ShapeDtypeStruct((rows, cols), F32),
        compiler_params=_params(),
    )(recv)


def kernel(x, mem, norm_mix_g, w_in, ssd_conv_w, ssd_conv_b, ssd_dt_bias, ssd_a_log, ssd_d, ssd_norm_g, rwkv_mu, rwkv_w0, rwkv_w2, rwkv_a0, rwkv_a2, rwkv_g2, rwkv_k_k, rwkv_k_a, rwkv_r_k, rwkv_ln_w, rwkv_ln_b, w_out, norm_x_g, norm_mem_g, xattn_wq, xattn_wk, xattn_wv, xattn_wo, norm_ffn_g, ffn_w1, ffn_w2, final_norm_g, loss_target, m_norm_mix_g, m_w_in, m_ssd_conv_w, m_ssd_conv_b, m_ssd_dt_bias, m_ssd_a_log, m_ssd_d, m_ssd_norm_g, m_rwkv_mu, m_rwkv_w0, m_rwkv_w2, m_rwkv_a0, m_rwkv_a2, m_rwkv_g2, m_rwkv_k_k, m_rwkv_k_a, m_rwkv_r_k, m_rwkv_ln_w, m_rwkv_ln_b, m_w_out, m_norm_x_g, m_norm_mem_g, m_xattn_wq, m_xattn_wk, m_xattn_wv, m_xattn_wo, m_norm_ffn_g, m_ffn_w1, m_ffn_w2, m_final_norm_g, v_norm_mix_g, v_w_in, v_ssd_conv_w, v_ssd_conv_b, v_ssd_dt_bias, v_ssd_a_log, v_ssd_d, v_ssd_norm_g, v_rwkv_mu, v_rwkv_w0, v_rwkv_w2, v_rwkv_a0, v_rwkv_a2, v_rwkv_g2, v_rwkv_k_k, v_rwkv_k_a, v_rwkv_r_k, v_rwkv_ln_w, v_rwkv_ln_b, v_w_out, v_norm_x_g, v_norm_mem_g, v_xattn_wq, v_xattn_wk, v_xattn_wv, v_xattn_wo, v_norm_ffn_g, v_ffn_w1, v_ffn_w2, v_final_norm_g):
    given = dict(locals())
    wts = {n: given[n] for n in _WEIGHTS}
    mom_m = {n: given["m_" + n] for n in _WEIGHTS}
    mom_v = {n: given["v_" + n] for n in _WEIGHTS}
    d = D_MODEL
    xt, memt, tgt = x[0], mem[0], loss_target[0]
    tm = 256

    big = {"w_in": jnp.transpose(w_in[0]), "w_out": w_out[0], "xattn_wq": xattn_wq[0], "xattn_wk": xattn_wk[0],
           "xattn_wv": xattn_wv[0], "xattn_wo": xattn_wo[0], "ffn_w1": ffn_w1[0], "ffn_w2": ffn_w2[0]}
    small_sh = {"ssd_conv_w": ssd_conv_w.reshape(4, -1), "rwkv_w2": rwkv_w2[0], "rwkv_a2": rwkv_a2[0],
                "rwkv_g2": rwkv_g2[0]}
    cast_one = lambda n, deps=(): rowwise_fwd("cast_" + n, cast_fn, [big[n]], [], [(big[n].shape[1], BF16)],
                                              256 if big[n].shape[0] % 256 == 0 else big[n].shape[0], deps=deps)[0]
    gather_in, token_in = exchange_start("gather_in_start", [cast_one("w_in")] + list(small_sh.values()), "gather_chips")
    cast = {n: cast_one(n, deps=[token_in]) for n in big if n != "w_in"}
    late_a = ("w_out", "xattn_wq", "xattn_wk", "xattn_wv", "xattn_wo")
    late_b = ("ffn_w1", "ffn_w2")
    gather_a, token_a = exchange_start("gather_attn_start", [cast[n] for n in late_a], "gather_chips", dep=token_in)
    gather_b, token_b = exchange_start("gather_ffn_start", [cast[n] for n in late_b], "gather_chips", dep=token_a)
    (h1,) = rowwise_fwd("norm_mix", rmsnorm_fn, [xt], [norm_mix_g], [(d, BF16)], tm, deps=[token_b])
    forward_in, token_in = forward_start("gather_in", gather_in, after=h1)
    gathered = exchange_wait("gather_in_forward_wait", forward_in, after=token_in)
    g_big = {"w_in": gathered[0]}
    g_small = dict(zip(small_sh, gathered[1:]))

    pad_rows = lambda a: jnp.pad(a, ((0, LANES - a.shape[0]), (0, 0)))
    w_in_t = g_big["w_in"].reshape(D_IN, d)
    wt_z, wt_xbc, wt_rkv, wt_pg = (_rows(w_in_t, r) for r in (_Z, _XBC, _RKV, _PG))
    wt_sm = jnp.concatenate([pad_rows(_rows(w_in_t, r)) for r in (_PW, _PA, _DT)], axis=0)
    unshard_cols = lambda g: jnp.transpose(g, (1, 0, 2)).reshape(g.shape[1], -1)
    conv_w_f = unshard_cols(g_small["ssd_conv_w"])
    w2p, a2p = pad_rows(unshard_cols(g_small["rwkv_w2"])), pad_rows(unshard_cols(g_small["rwkv_a2"]))
    g2_f = unshard_cols(g_small["rwkv_g2"])

    mu = rwkv_mu
    mu_rkv, mu_pg = mu[:, :3072], mu[:, 3264:3520]
    mu_pwa = jnp.concatenate([_pad_lanes(mu[:, 3072:3168]), _pad_lanes(mu[:, 3168:3264])], axis=1)
    dt_bias_p, a_log_p, d_p = _pad_lanes(ssd_dt_bias), _pad_lanes(ssd_a_log), _pad_lanes(ssd_d)
    r_k_row = rwkv_r_k.reshape(1, RWKV_WIDTH)
    g_final = final_norm_g.reshape(1, d)

    u_z = mm("in_z", h1, wt_z, "nt")
    u_xbc = mm("in_xbc", h1, wt_xbc, "nt")
    u_rkv = mm("in_rkv", h1, wt_rkv, "nt")
    u_pg = mm("in_pg", h1, wt_pg, "nt")
    u_sm = mm("in_small", h1, wt_sm, "nt")

    ssd_pre_rows = lambda: [Rows(u_xbc, shifts=(1, 2, 3)), Rows(u_sm, LANES, 2)]
    ssd_pre_params = [conv_w_f, ssd_conv_b, dt_bias_p]
    xs, bm, cm, dt = rowwise_fwd("ssd_pre", ssd_pre_fn, ssd_pre_rows(), ssd_pre_params,
                                 [(SSD_WIDTH, F32), (256, F32), (256, F32), (LANES, F32)], tm)
    y_scan, ssd_ck = ssd_scan_fwd(xs, bm, cm, dt, a_log_p, d_p)
    (y_ssd,) = rowwise_fwd("ssd_post", ssd_post_fn, [y_scan, u_z], [ssd_norm_g], [(SSD_WIDTH, BF16)], tm)

    rwkv_pre_rows = lambda: [Rows(u_rkv, shifts=(1,)), Rows(u_pg, shifts=(1,)), Rows(u_sm, 2 * LANES, 0, shifts=(1,))]
    rwkv_pre_params = [mu_rkv, mu_pg, mu_pwa, rwkv_w0, w2p, rwkv_a0, a2p, g2_f, rwkv_k_k, rwkv_k_a]
    forward_a, token_a = forward_start("gather_attn", gather_a, after=y_ssd)
    r_, lw_, k_, v_, kap_, b_, gate_ = rowwise_fwd("rwkv_pre", rwkv_pre_fn, rwkv_pre_rows(), rwkv_pre_params,
                                                   [(RWKV_WIDTH, F32)] * 7, 128, deps=[token_a])
    ys_r, rwkv_ck, rwkv_inv = rwkv_scan_fwd(r_, lw_, k_, v_, kap_, b_)
    forward_b, token_b = forward_start("gather_ffn", gather_b, after=ys_r)
    g_big.update(zip(late_a, exchange_wait("gather_attn_forward_wait", forward_a, after=token_b)))
    w_out_f = g_big["w_out"].reshape(d, d)
    wq_f, wk_f, wv_f, wo_f = (g_big[n].reshape(d, d) for n in ("xattn_wq", "xattn_wk", "xattn_wv", "xattn_wo"))
    rwkv_post_params = [rwkv_ln_w, rwkv_ln_b, r_k_row]
    (y_rwkv,) = rowwise_fwd("rwkv_post", rwkv_post_fn, [ys_r, r_, k_, v_, gate_], rwkv_post_params,
                            [(RWKV_WIDTH, BF16)], tm)
    ycat = jnp.concatenate([y_ssd, y_rwkv], axis=1)
    x1 = mm("out_proj", ycat, w_out_f, "nn", res=xt)

    (h2,) = rowwise_fwd("norm_x", rmsnorm_fn, [x1], [norm_x_g], [(d, BF16)], tm)
    (mn,) = rowwise_fwd("norm_mem", rmsnorm_fn, [memt], [norm_mem_g], [(d, BF16)], tm)
    q = mm("xattn_q", h2, wq_f, "nn", out_dtype=BF16)
    kx = mm("xattn_k", mn, wk_f, "nn")
    vx = mm("xattn_v", mn, wv_f, "nn")
    (o,) = rowwise_fwd("xattn", attn_fn, [q], [kx, vx], [(d, BF16)], tm)
    x2 = mm("xattn_o", o, wo_f, "nn", res=x1)

    (h3,) = rowwise_fwd("norm_ffn", rmsnorm_fn, [x2], [norm_ffn_g], [(d, BF16)], tm)
    w1_s, w2_g = exchange_wait("gather_ffn_forward_wait", forward_b, after=h3)
    w2_f = w2_g.reshape(D_FF, d)
    relu2_epi = lambda acc: (jnp.square(jnp.maximum(acc, 0.0)), jnp.maximum(acc, 0.0))
    hid, relu_a = mm("ffn_1", h3, w1_s, "nn", b_slabs=N_DEV, epi=relu2_epi, out_dtypes=[BF16, BF16])
    x3 = mm("ffn_2", hid, w2_f, "nn", res=x2)

    loss_blk, dx3, dg_final = loss_and_grad(x3, tgt, g_final, tm)

    grads = {}
    grads["ffn_w2"] = mm("d_ffn_w2", hid, dx3, "tn", out_dtype=BF16).reshape(N_DEV, D_FF // N_DEV, d)
    sc_w2, tok = exchange_start("scatter_ffn_w2_start", [grads["ffn_w2"]], "scatter")
    da = mm("d_hid", dx3, w2_f, "nt", dep=tok, epi=lambda acc, ra: (2.0 * acc * ra,), extras=[relu_a],
            out_dtypes=[BF16])
    grads["ffn_w1"] = mm("d_ffn_w1", h3, da, "tn", out_dtype=BF16, out_slabs=N_DEV)
    sc_w1, tok = exchange_start("scatter_ffn_w1_start", [grads["ffn_w1"]], "scatter")
    dh3 = mm("d_h3", da, w1_s, "nt", out_dtype=BF16, b_slabs=N_DEV, dep=tok)
    (dx2,), (dg_ffn,) = rowwise_bwd("norm_ffn_bwd", rmsnorm_fn, [x2], [norm_ffn_g], [[dh3]], tm, [F32], row_add=[dx3])

    grads["xattn_wo"] = mm("d_wo", o, dx2, "tn", out_dtype=BF16).reshape(N_DEV, d // N_DEV, d)
    sc_wo, tok = exchange_start("scatter_wo_start", [grads["xattn_wo"]], "scatter")
    d_o = mm("d_o", dx2, wo_f, "nt", out_dtype=BF16, dep=tok)
    (dq,), (dkx, dvx) = rowwise_bwd("xattn_bwd", attn_fn, [q], [kx, vx], [[d_o]], tm, [BF16])
    grads["xattn_wq"] = mm("d_wq", h2, dq, "tn", out_dtype=BF16).reshape(N_DEV, d // N_DEV, d)
    grads["xattn_wk"] = mm("d_wk", mn, dkx, "tn", out_dtype=BF16).reshape(N_DEV, d // N_DEV, d)
    grads["xattn_wv"] = mm("d_wv", mn, dvx, "tn", out_dtype=BF16).reshape(N_DEV, d // N_DEV, d)
    qkv = ("xattn_wq", "xattn_wk", "xattn_wv")
    sc_qkv, tok = exchange_start("scatter_qkv_start", [grads[n] for n in qkv], "scatter")
    dmn = mm("d_mn_v", dvx, wv_f, "nt", res=mm("d_mn_k", dkx, wk_f, "nt", dep=tok))
    _, (dg_mem,) = rowwise_bwd("norm_mem_bwd", rmsnorm_fn, [memt], [norm_mem_g], [[dmn]], tm, [None])
    dh2 = mm("d_h2", dq, wq_f, "nt", out_dtype=BF16, dep=dg_mem)
    (dx1,), (dg_x,) = rowwise_bwd("norm_x_bwd", rmsnorm_fn, [x1], [norm_x_g], [[dh2]], tm, [F32], row_add=[dx2])

    grads["w_out"] = mm("d_w_out", ycat, dx1, "tn", out_dtype=BF16).reshape(N_DEV, d // N_DEV, d)
    sc_wout, tok = exchange_start("scatter_w_out_start", [grads["w_out"]], "scatter")
    d_ycat = mm("d_ycat", dx1, w_out_f, "nt", out_dtype=BF16, dep=tok)

    (d_ys, d_r1, d_k1, d_v1, d_gate), (dln_w, dln_b, dr_k) = rowwise_bwd(
        "rwkv_post_bwd", rwkv_post_fn, [ys_r, r_, k_, v_, gate_], rwkv_post_params,
        [[Rows(d_ycat, RWKV_WIDTH, 1)]], tm, [BF16] * 5)
    d_r2, d_lw, d_k2, d_v2, d_kap, d_b = rwkv_scan_bwd(r_, lw_, k_, v_, kap_, b_, rwkv_ck, rwkv_inv, d_ys)
    (du_rkv, du_pg, du_pwa), rwkv_pg = rowwise_bwd(
        "rwkv_pre_bwd", rwkv_pre_fn, rwkv_pre_rows(), rwkv_pre_params,
        [[d_r1, d_r2], [d_lw], [d_k1, d_k2], [d_v1, d_v2], [d_kap], [d_b], [d_gate]], 128, [BF16] * 3)
    dmu_rkv, dmu_pg, dmu_pwa, dw0, dw2p, da0, da2p, dg2, dk_k, dk_a = rwkv_pg

    (d_yscan, du_z), (dssd_norm_g,) = rowwise_bwd("ssd_post_bwd", ssd_post_fn, [y_scan, u_z], [ssd_norm_g],
                                                  [[Rows(d_ycat, SSD_WIDTH, 0)]], tm, [BF16, BF16])
    dxs, dbm, dcm, ddt2, da_log_p, dd_p = ssd_scan_bwd(xs, bm, cm, dt, a_log_p, d_p, ssd_ck, d_yscan)
    (du_xbc, du_dt), (dconv_w, dconv_b, ddt_bias_p) = rowwise_bwd(
        "ssd_pre_bwd", ssd_pre_fn, ssd_pre_rows(), ssd_pre_params,
        [[dxs], [dbm], [dcm], [ddt2[0], ddt2[1]]], tm, [BF16, BF16])
    du_sm = jnp.concatenate([du_pwa, du_dt], axis=1)

    dwt_z = mm("d_w_z", du_z, h1, "tn", out_dtype=BF16)
    dwt_xbc = mm("d_w_xbc", du_xbc, h1, "tn", out_dtype=BF16)
    dwt_rkv = mm("d_w_rkv", du_rkv, h1, "tn", out_dtype=BF16)
    dwt_pg = mm("d_w_pg", du_pg, h1, "tn", out_dtype=BF16)
    dwt_sm = mm("d_w_small", du_sm, h1, "tn", out_dtype=BF16)
    dwt_full = jnp.concatenate([dwt_z, dwt_xbc, dwt_sm[256:272], dwt_rkv, dwt_sm[0:96], dwt_sm[128:224], dwt_pg], axis=0)
    to_slabs = lambda g: jnp.transpose(g.reshape(g.shape[0], N_DEV, -1), (1, 0, 2))
    grads["w_in"] = dwt_full.reshape(N_DEV, D_IN // N_DEV, d)
    grads["ssd_conv_w"] = to_slabs(dconv_w)
    grads["rwkv_w2"] = to_slabs(dw2p[:96])
    grads["rwkv_a2"] = to_slabs(da2p[:96])
    grads["rwkv_g2"] = to_slabs(dg2)
    tail = ("w_in", "ssd_conv_w", "rwkv_w2", "rwkv_a2", "rwkv_g2")
    sc_tail, tok = exchange_start("scatter_tail_start", [grads[n] for n in tail], "scatter")
    dh1 = mm("d_h1_z", du_z, wt_z, "nn", dep=tok)
    dh1 = mm("d_h1_xbc", du_xbc, wt_xbc, "nn", res=dh1)
    dh1 = mm("d_h1_rkv", du_rkv, wt_rkv, "nn", res=dh1)
    dh1 = mm("d_h1_pg", du_pg, wt_pg, "nn", res=dh1)
    dh1 = mm("d_h1_small", du_sm, wt_sm, "nn", res=dh1)
    (dx,), (dg_mix,) = rowwise_bwd("norm_mix_bwd", rmsnorm_fn, [xt], [norm_mix_g], [[dh1]], tm, [F32], row_add=[dx1])

    dmu =jnp.concatenate([dmu_rkv, dmu_pwa[:, 0:96], dmu_pwa[:, 128:224], dmu_pg], axis=1)
    small_grads = {
        "norm_mix_g": dg_mix, "ssd_conv_b": dconv_b, "ssd_dt_bias": ddt_bias_p[:, :16], "ssd_a_log": da_log_p[:, :16],
        "ssd_d": dd_p[:, :16], "ssd_norm_g": dssd_norm_g, "rwkv_mu": dmu, "rwkv_w0": dw0, "rwkv_a0": da0,
        "rwkv_k_k": dk_k, "rwkv_k_a": dk_a, "rwkv_r_k": dr_k, "rwkv_ln_w": dln_w, "rwkv_ln_b": dln_b,
        "norm_x_g": dg_x, "norm_mem_g": dg_mem, "norm_ffn_g": dg_ffn, "final_norm_g": dg_final}

    gather_small, tok = exchange_start("gather_small_start", [_pack_small(small_grads)], "gather")
    received = {}
    for names, handle in ((("ffn_w2",), sc_w2), (("ffn_w1",), sc_w1), (("xattn_wo",), sc_wo), (qkv, sc_qkv),
                          (("w_out",), sc_wout)):
        received.update(zip(names, exchange_wait("scatter_" + names[0] + "_wait", handle, after=tok)))

    out_g, out_d, out_m, out_v = {}, {}, {}, {}

    def run_adamw(n, dep):
        shape = wts[n].shape
        two_d = lambda a: a.reshape(-1, shape[-1])
        if n == "w_in":
            recv = jnp.transpose(sum_slabs("sum_w_in", received[n]))[None]
        else:
            recv = received[n].reshape(N_DEV, -1, shape[-1])
        res = adamw("adamw_" + n, recv, two_d(wts[n]), two_d(mom_m[n]), two_d(mom_v[n]), dep=dep)
        out_g[n], out_d[n], out_m[n], out_v[n] = (r.reshape(shape) for r in res)
        return res[0]

    last = None
    for n in ("ffn_w2", "ffn_w1", "xattn_wo") + qkv + ("w_out",):
        last = run_adamw(n, last)
    received.update(zip(tail, exchange_wait("scatter_tail_wait", sc_tail, after=last)))
    for n in tail:
        last = run_adamw(n, last)
    (small_all,) = exchange_wait("gather_small_wait", gather_small, after=last)
    res = adamw("adamw_small", small_all, _pack_small(wts), _pack_small(mom_m), _pack_small(mom_v))
    shapes = {n: wts[n].shape for n in _SMALL}
    for dst, packed in zip((out_g, out_d, out_m, out_v), res):
        dst.update(_unpack_small(packed, shapes))

    loss = lax.psum(loss_blk[0, 0], ("x", "y", "c"))
    return (loss, dx[None], *[out_g[n] for n in _WEIGHTS], *[out_d[n] for n in _WEIGHTS],
            *[out_m[n] for n in _WEIGHTS], *[out_v[n] for n in _WEIGHTS])
```

```python
import functools

import jax
import jax.numpy as jnp
from jax import lax
from jax.experimental import pallas as pl
from jax.experimental.pallas import tpu as pltpu

F32 = jnp.float32
BF16 = jnp.bfloat16
HIGHEST = lax.Precision.HIGHEST

N_DEV = 8
D_MODEL = 2048
NORM_EPS = 1e-6
SSD_WIDTH = 1024
SSD_CONV_DIM = 1536
SSD_HEADS = 16
SSD_HEAD_DIM = 64
SSD_STATE = 128
SSD_CHUNK = 128
SSD_HEADS_PER_GROUP = 8
RWKV_WIDTH = 1024
RWKV_HEADS = 16
RWKV_HEAD_DIM = 64
RWKV_LN_EPS = 64e-5
RWKV_CHUNK = 64
RWKV_HEADS_PER_STEP = 16
XATTN_HEADS = 4
XATTN_HEAD_DIM = 512
D_FF = 8192
LANES = 128
SUBLANES = 8
VMEM_LIMIT = 56 * 1024 * 1024

ADAM_LR = 0.001
ADAM_B1 = 0.9
ADAM_B2 = 0.999
ADAM_EPS = 1e-08
ADAM_WD = 0.01
ADAM_STEP = 10

_DN = {"nn": ((1,), (0,)), "nt": ((1,), (1,)), "tn": ((0,), (0,))}


def _dg(a, b, mode, precision=None):
    (ca,), (cb,) = _DN[mode]
    dn = (((ca + 1,), (cb + 1,)), ((0,), (0,))) if a.ndim == 3 else (((ca,), (cb,)), ((), ()))
    return lax.dot_general(a, b, dn, precision=precision, preferred_element_type=F32)


@functools.partial(jax.custom_vjp, nondiff_argnums=(2,))
def bdot(a, b, mode):
    return _dg(a.astype(BF16), b.astype(BF16), mode)


def _bdot_fwd(a, b, mode):
    return bdot(a, b, mode), (a, b)


def _bdot_bwd(mode, res, g):
    a, b = res
    ab, bb, gb = a.astype(BF16), b.astype(BF16), g.astype(BF16)
    if mode == "nn":
        da, db = _dg(gb, bb, "nt"), _dg(ab, gb, "tn")
    elif mode == "nt":
        da, db = _dg(gb, bb, "nn"), _dg(gb, ab, "tn")
    else:
        da, db = _dg(bb, gb, "nt"), _dg(ab, gb, "nn")
    return da.astype(a.dtype), db.astype(b.dtype)


bdot.defvjp(_bdot_fwd, _bdot_bwd)


def fdot(a, b, mode):
    return _dg(a, b, mode, precision=HIGHEST)


def _split2(x):
    hi = x.astype(BF16)
    return hi, (x - hi.astype(F32)).astype(BF16)


def _dot01(x, m01):
    hi, lo = _split2(x)
    return _dg(hi, m01, "nn") + _dg(lo, m01, "nn")


def _exact_dot_impl(a, b, mode, exact):
    if exact == "a":
        ae = a.astype(BF16)
        return sum(_dg(ae, part, mode) for part in _split2(b))
    be = b.astype(BF16)
    return sum(_dg(part, be, mode) for part in _split2(a))


@functools.partial(jax.custom_vjp, nondiff_argnums=(2, 3))
def exact_dot(a, b, mode, exact):
    return _exact_dot_impl(a, b, mode, exact)


def _exact_dot_fwd(a, b, mode, exact):
    return _exact_dot_impl(a, b, mode, exact), (a, b)


def _exact_dot_bwd(mode, exact, res, g):
    a, b = res
    if exact == "a":
        db = {"nn": lambda: _exact_dot_impl(a, g, "tn", "a"), "nt": lambda: _exact_dot_impl(g, a, "tn", "b"),
              "tn": lambda: _exact_dot_impl(a, g, "nn", "a")}[mode]()
        return jnp.zeros_like(a), db
    da = {"nn": lambda: _exact_dot_impl(g, b, "nt", "b"), "nt": lambda: _exact_dot_impl(g, b, "nn", "b"),
          "tn": lambda: _exact_dot_impl(b, g, "nt", "a")}[mode]()
    return da, jnp.zeros_like(b)


exact_dot.defvjp(_exact_dot_fwd, _exact_dot_bwd)


def _head_indicator(width, heads, transpose):
    hd = width // heads
    shape = (LANES, width) if transpose else (width, LANES)
    lane = lax.broadcasted_iota(jnp.int32, shape, 1 if not transpose else 0)
    pos = lax.broadcasted_iota(jnp.int32, shape, 0 if not transpose else 1)
    return ((pos >= lane * hd) & (pos < lane * hd + hd)).astype(BF16)


@jax.custom_vjp
def head_sum(x):
    w = x.shape[-1]
    e = _head_indicator(w, w // RWKV_HEAD_DIM, False)
    et = _head_indicator(w, w // RWKV_HEAD_DIM, True)
    return _dot01(_dot01(x, e), et)


head_sum.defvjp(lambda x: (head_sum(x), None), lambda _, g: (head_sum(g),))


def rmsnorm_fn(x, g):
    y = x * lax.rsqrt(jnp.mean(x * x, axis=-1, keepdims=True) + NORM_EPS)
    return ((y * g).astype(BF16),)


def cast_fn(x):
    return (x.astype(BF16),)


def ssd_pre_fn(xbc, xbc1, xbc2, xbc3, dt_raw, conv_w, conv_b, dt_bias):
    c = conv_w[3:4] * xbc + conv_w[2:3] * xbc1 + conv_w[1:2] * xbc2 + conv_w[0:1] * xbc3 + conv_b
    act = c * jax.nn.sigmoid(c)
    dt = jax.nn.softplus(dt_raw + dt_bias)
    return act[:, :SSD_WIDTH], act[:, SSD_WIDTH:SSD_WIDTH + 256], act[:, SSD_WIDTH + 256:], dt


def ssd_post_fn(yscan, z, norm_g):
    y = yscan * (z * jax.nn.sigmoid(z))
    half = SSD_WIDTH // 2
    parts = []
    for g in range(2):
        yg = y[:, g * half:(g + 1) * half]
        parts.append(yg * lax.rsqrt(jnp.mean(yg * yg, axis=-1, keepdims=True) + NORM_EPS))
    return ((jnp.concatenate(parts, axis=-1) * norm_g).astype(BF16),)


def rwkv_pre_fn(rkv, rkv_p, pg, pg_p, pwa, pwa_p, mu_rkv, mu_pg, mu_pwa, w0, w2p, a0, a2p, g2, k_k, k_a):
    w = RWKV_WIDTH
    rkv = rkv + (rkv_p - rkv) * mu_rkv
    pg = pg + (pg_p - pg) * mu_pg
    pwa = pwa + (pwa_p - pwa) * mu_pwa
    r, k, v = rkv[:, :w], rkv[:, w:2 * w], rkv[:, 2 * w:]
    pw, pa = pwa[:, :LANES], pwa[:, LANES:]
    w_log = -jax.nn.softplus(-(w0 + bdot(jnp.tanh(pw), w2p, "nn"))) - 0.5
    lw = -jnp.exp(w_log)
    iclr = jax.nn.sigmoid(a0 + bdot(pa, a2p, "nn"))
    gate = bdot(jax.nn.sigmoid(pg), g2, "nn")
    kk = k * k_k
    kap = kk / jnp.maximum(jnp.sqrt(head_sum(kk * kk)), 1e-12)
    k_mod = k * (1.0 + (iclr - 1.0) * k_a)
    return r, lw, k_mod, v, kap, kap * iclr, gate


def rwkv_post_fn(ys, r, k_mod, v, gate, ln_w, ln_b, r_k):
    inv_n = 1.0 / RWKV_HEAD_DIM
    mean = head_sum(ys) * inv_n
    yc = ys - mean
    var = head_sum(yc * yc) * inv_n
    yn = yc * lax.rsqrt(var + RWKV_LN_EPS) * ln_w + ln_b
    bonus = head_sum(r * k_mod * r_k) * v
    return (((yn + bonus) * gate).astype(BF16),)


def attn_fn(q, kx, vx):
    outs = []
    for h in range(XATTN_HEADS):
        sl = slice(h * XATTN_HEAD_DIM, (h + 1) * XATTN_HEAD_DIM)
        s = bdot(q[:, sl], kx[:, sl], "nt") * (XATTN_HEAD_DIM ** -0.5)
        s = s - jnp.max(s, axis=-1, keepdims=True)
        p = jnp.exp(s)
        p = p / jnp.sum(p, axis=-1, keepdims=True)
        outs.append(bdot(p, vx[:, sl], "nn"))
    return (jnp.concatenate(outs, axis=-1).astype(BF16),)


def loss_fn(x, tgt, g):
    y = x * lax.rsqrt(jnp.mean(x * x, axis=-1, keepdims=True) + NORM_EPS) * g
    err = jnp.square(y - tgt)
    return 0.5 * jnp.sum(jnp.mean(err, axis=-1, keepdims=True), axis=0, keepdims=True)


def _tri_masks(n):
    row = lax.broadcasted_iota(jnp.int32, (n, n), 0)
    col = lax.broadcasted_iota(jnp.int32, (n, n), 1)
    return col <= row, col < row, row == col


@jax.custom_vjp
def unit_lower_inverse(a):
    c = a.shape[-1]
    eye = _tri_masks(c)[2].astype(F32)
    m = -a
    inv = eye + m
    n = 1
    while n * 2 < c:
        m = bdot(m, m, "nn")
        inv = bdot(inv, eye + m, "nn")
        n *= 2
    return inv


def _unit_lower_inverse_fwd(a):
    inv = unit_lower_inverse(a)
    return inv, inv


def _unit_lower_inverse_bwd(inv, g):
    return (-bdot(bdot(inv, g, "tn"), inv, "nt"),)


unit_lower_inverse.defvjp(_unit_lower_inverse_fwd, _unit_lower_inverse_bwd)


@jax.custom_vjp
def known_inverse(a, inv):
    return inv


known_inverse.defvjp(lambda a, inv: (inv, inv),
                     lambda inv, g: (_unit_lower_inverse_bwd(inv, g)[0], jnp.zeros_like(inv)))


def rwkv_chunk_fn(st0, r, lw, k, v, kap, b, inv=None):
    h, c = r.shape[0], r.shape[1]
    incl, strict, _ = _tri_masks(c)
    cum = exact_dot(jnp.broadcast_to(incl.astype(F32), (h, c, c)), lw, "nn", "a")
    g_in = jnp.exp(cum)
    g_prev = jnp.exp(cum - lw)
    g_inv = jnp.exp(-cum)
    g_end = jnp.exp(cum[:, c - 1:c, :] - cum)
    kap_t, k_t, b_t, r_t = kap * g_prev, k * g_inv, b * g_inv, r * g_in
    a_ub = jnp.where(strict, bdot(kap_t, b_t, "nt"), 0.0)
    a_vk = jnp.where(strict, bdot(kap_t, k_t, "nt"), 0.0)
    rhs = -(bdot(kap_t, st0, "nt") + bdot(a_vk, v, "nn"))
    inv = unit_lower_inverse(a_ub) if inv is None else known_inverse(a_ub, inv)
    u = bdot(inv, rhs, "nn")
    y = (bdot(r_t, st0, "nt")
         + bdot(jnp.where(incl, bdot(r_t, k_t, "nt"), 0.0), v, "nn")
         + bdot(jnp.where(incl, bdot(r_t, b_t, "nt"), 0.0), u, "nn"))
    st1 = jnp.exp(cum[:, c - 1:c, :]) * st0 + bdot(v, k * g_end, "tn") + bdot(u, b * g_end, "tn")
    return y, st1, inv


def ssd_chunk_fn(group, h0, xs, bm, cm, dt, a_log, d_skip):
    q, nh = xs.shape[0], SSD_HEADS_PER_GROUP
    causal, _, _ = _tri_masks(q)
    a_row = -jnp.exp(a_log)
    cs_all = exact_dot(causal.astype(F32), dt * a_row, "nn", "a")
    cs_t = cs_all.T
    lanes = range(group * nh, (group + 1) * nh)
    cs = jnp.stack([cs_all[:, hl:hl + 1] for hl in lanes])
    cs_row = jnp.stack([cs_t[hl:hl + 1, :] for hl in lanes])
    dt_h = jnp.stack([dt[:, hl:hl + 1] for hl in lanes])
    d_h = jnp.stack([d_skip[:, hl:hl + 1] for hl in lanes])
    x = _stack_lanes(xs, nh)
    h0s = _stack_rows(h0, nh)
    lmat = jnp.where(causal, jnp.exp(jnp.where(causal, cs - cs_row, 0.0)), 0.0)
    cb = bdot(cm, bm, "nt")
    xdt = x * dt_h
    cl = cs[:, q - 1:q, :]
    cm_b = jnp.broadcast_to(cm, (nh,) + cm.shape)
    bm_b = jnp.broadcast_to(bm, (nh,) + bm.shape)
    y = bdot(cb * lmat, xdt, "nn") + bdot(cm_b, h0s, "nt") * jnp.exp(cs) + x * d_h
    h1 = h0s * jnp.exp(cl) + bdot(xdt * jnp.exp(cl - cs), bm_b, "tn")
    return jnp.concatenate([y[e] for e in range(nh)], axis=-1), jnp.concatenate([h1[e] for e in range(nh)], axis=0)


class Rows:
    def __init__(self, arr, w=None, cb=0, shifts=()):
        self.arr, self.w, self.cb, self.shifts = arr, (arr.shape[1] if w is None else w), cb, tuple(shifts)


def _as_rows(x):
    return x if isinstance(x, Rows) else Rows(x)


def _shift_down(x, halo, k):
    rolled = pltpu.roll(x, k, 0)
    first = rolled[0:SUBLANES]
    rid = lax.broadcasted_iota(jnp.int32, first.shape, 0)
    patched = jnp.where(rid < k, pltpu.roll(halo, k, 0), first)
    return jnp.concatenate([patched, rolled[SUBLANES:]], axis=0)


def _shift_up(g, carry, k):
    tm = g.shape[0]
    rolled = pltpu.roll(g, tm - k, 0)
    last = rolled[tm - SUBLANES:]
    rid = lax.broadcasted_iota(jnp.int32, last.shape, 0)
    patched = jnp.where(rid >= SUBLANES - k, pltpu.roll(carry, SUBLANES - k, 0), last)
    return jnp.concatenate([rolled[:tm - SUBLANES], patched], axis=0)


def _params():
    return pltpu.CompilerParams(vmem_limit_bytes=VMEM_LIMIT)


def _load_rows(refs, pos, rins, first_block):
    vals = []
    for r in rins:
        x = refs[pos][...].astype(F32) if refs[pos].dtype != F32 else refs[pos][...]
        pos += 1
        vals.append(x)
        if r.shifts:
            halo = refs[pos][...]
            pos += 1
            halo = jnp.where(first_block, jnp.zeros_like(halo), halo)
            for k in r.shifts:
                vals.append(_shift_down(x, halo, k))
    return vals, pos


def _row_specs(rins, tm, blk):
    specs, args = [], []
    for r in rins:
        specs.append(pl.BlockSpec((tm, r.w), lambda i, cb=r.cb: (blk(i), cb)))
        args.append(r.arr)
        if r.shifts:
            per = tm // SUBLANES
            specs.append(pl.BlockSpec((SUBLANES, r.w), lambda i, cb=r.cb: (jnp.maximum(blk(i) * per - 1, 0), cb)))
            args.append(r.arr)
    return specs, args


def rowwise_fwd(name, fn, rins, params, outs, tm, deps=()):
    rins = [_as_rows(r) for r in rins]
    t = rins[0].arr.shape[0]
    tm = min(tm, t)
    nb = t // tm
    specs, args = _row_specs(rins, tm, lambda i: i)
    for p in params:
        specs.append(pl.BlockSpec(p.shape, lambda i: (0, 0)))
        args.append(p)
    for dep in deps:
        specs.append(pl.BlockSpec(memory_space=pl.ANY))
        args.append(dep)
    n_in = len(args)

    def body(*refs):
        vals, pos = _load_rows(refs, 0, rins, pl.program_id(0) == 0)
        pv = [refs[pos + j][...] for j in range(len(params))]
        res = fn(*vals, *pv)
        for o_ref, o in zip(refs[n_in:], res):
            o_ref[...] = o.astype(o_ref.dtype)

    return pl.pallas_call(
        body, name=name, grid=(nb,), in_specs=specs,
        out_specs=[pl.BlockSpec((tm, w), lambda i: (i, 0)) for w, _ in outs],
        out_shape=[jax.ShapeDtypeStruct((t, w), dt) for w, dt in outs],
        compiler_params=_params(),
    )(*args)


def rowwise_bwd(name, fn, rins, params, cts, tm, grad_dtypes, row_add=None):
    rins = [_as_rows(r) for r in rins]
    cts = [[_as_rows(c) for c in lst] for lst in cts]
    row_add = [_as_rows(a) for a in (row_add or [])]
    t = rins[0].arr.shape[0]
    tm = min(tm, t)
    nb = t // tm
    rev = lambda i: nb - 1 - i
    specs, args = _row_specs(rins, tm, rev)
    for p in params:
        specs.append(pl.BlockSpec(p.shape, lambda i: (0, 0)))
        args.append(p)
    flat_cts = [c for lst in cts for c in lst] + row_add
    for c in flat_cts:
        specs.append(pl.BlockSpec((tm, c.w), lambda i, cb=c.cb: (rev(i), cb)))
        args.append(c.arr)
    n_in = len(args)
    want = [i for i, d in enumerate(grad_dtypes) if d is not None]
    out_specs = [pl.BlockSpec((tm, rins[i].w), lambda i_: (rev(i_), 0)) for i in want]
    out_shape = [jax.ShapeDtypeStruct((t, rins[i].w), grad_dtypes[i]) for i in want]
    out_specs += [pl.BlockSpec(p.shape, lambda i: (0, 0)) for p in params]
    out_shape += [jax.ShapeDtypeStruct(p.shape, F32) for p in params]
    n_out = len(out_shape)
    scratch = [pltpu.VMEM((SUBLANES, r.w), F32) for r in rins for _ in r.shifts]

    def body(*refs):
        i = pl.program_id(0)
        vals, pos = _load_rows(refs, 0, rins, rev(i) == 0)
        pv = [refs[pos + j][...] for j in range(len(params))]
        pos += len(params)
        outs, vjp = jax.vjp(fn, *vals, *pv)
        ct_vals = []
        for o, lst in zip(outs, cts):
            acc = None
            for _ in lst:
                cv = refs[pos][...].astype(F32)
                pos += 1
                acc = cv if acc is None else acc + cv
            ct_vals.append(acc.astype(o.dtype))
        adds = [refs[pos + j][...].astype(F32) for j in range(len(row_add))]
        grads = vjp(tuple(ct_vals))
        out_refs = refs[n_in:n_in + n_out]
        carry_refs = refs[n_in + n_out:]

        @pl.when(i == 0)
        def _():
            for cr in carry_refs:
                cr[...] = jnp.zeros_like(cr)
            for pr in out_refs[len(want):]:
                pr[...] = jnp.zeros_like(pr)

        gi, ci, oi = 0, 0, 0
        for idx, r in enumerate(rins):
            d = grads[gi]
            gi += 1
            for k in r.shifts:
                dk = grads[gi]
                gi += 1
                d = d + _shift_up(dk, carry_refs[ci][...], k)
                carry_refs[ci][...] = dk[0:SUBLANES]
                ci += 1
            if idx == 0:
                for a in adds:
                    d = d + a
            if grad_dtypes[idx] is not None:
                out_refs[oi][...] = d.astype(out_refs[oi].dtype)
                oi += 1
        for pr, gp in zip(out_refs[len(want):], grads[gi:]):
            pr[...] += gp

    res = pl.pallas_call(
        body, name=name, grid=(nb,), in_specs=specs, out_specs=out_specs, out_shape=out_shape,
        scratch_shapes=scratch, compiler_params=_params(),
    )(*args)
    return res[:len(want)], res[len(want):]


def _pick(n, pref):
    for c in pref:
        if n % c == 0:
            return c
    return n


MM_VMEM_BUDGET = 40 * 1024 * 1024
MM_PEAK_FLOPS = 0.9e15
MM_HBM_BYTES_PER_S = 3.0e12
MM_STEP_SECONDS = 0.35e-6


def _mm_tiles(m, n, k, size_a, size_b, size_out, size_res, single_k):
    best = None
    for tk in sorted({c for c in (k, 2048, 1024, 512, 256, 128) if c <= 2048 and k % c == 0}, reverse=True):
        for tm in (1024, 512, 256, 128):
            if m % tm:
                continue
            for tn in (2048, 1536, 1024, 768, 512, 384, 256, 128):
                if n % tn:
                    continue
                nk = k // tk
                vmem = 2 * (tm * tk * size_a + tk * tn * size_b + tm * tn * (size_out + size_res))
                vmem += tm * tn * 4 * (2 if nk > 1 or not single_k else 1)
                vmem += (tm * tk * 2 if size_a > 2 else 0) + (tk * tn * 2 if size_b > 2 else 0)
                if vmem > MM_VMEM_BUDGET:
                    continue
                steps = (m // tm) * (n // tn) * nk
                a_reads = 1 if (nk == 1 and single_k) else n // tn
                traffic = m * k * size_a * a_reads + k * n * size_b * (m // tm) + m * n * (size_out + size_res)
                cost = max(2.0 * m * n * k / MM_PEAK_FLOPS, traffic / MM_HBM_BYTES_PER_S) + steps * MM_STEP_SECONDS
                if best is None or cost < best[0]:
                    best = (cost, tm, tn, tk)
    return best[1:]


def mm(name, a, b, mode, out_dtype=F32, res=None, b_slabs=None, out_slabs=None, dep=None, epi=None, extras=(),
       out_dtypes=None):
    if mode == "tn":
        k_dim, m_dim = a.shape
    else:
        m_dim, k_dim = a.shape
    if b_slabs:
        n_dim = b.shape[0] * b.shape[2] if mode == "nn" else b.shape[1]
    else:
        n_dim = b.shape[0] if mode == "nt" else b.shape[1]
    n_slabs = out_slabs or (b_slabs if (b_slabs and mode == "nn") else 1)
    k_slabs = b_slabs if (b_slabs and mode == "nt") else 1
    if epi is None:
        out_dtypes = [out_dtype]
        if res is None:
            epi = lambda acc: (acc,)
        else:
            extras, epi = [res], lambda acc, r: (acc + r,)
    tm, tn, tk = _mm_tiles(m_dim, n_dim // n_slabs, k_dim // k_slabs, a.dtype.itemsize, b.dtype.itemsize,
                           sum(jnp.dtype(dt).itemsize for dt in out_dtypes), sum(e.dtype.itemsize for e in extras),
                           single_k=(k_slabs == 1))
    nji = n_dim // n_slabs // tn
    nki = k_dim // k_slabs // tk
    nblk = lambda js, j: js * nji + j
    kblk = lambda ks, k: ks * nki + k
    if mode == "tn":
        a_spec = pl.BlockSpec((tk, tm), lambda i, js, j, ks, k: (kblk(ks, k), i))
    else:
        a_spec = pl.BlockSpec((tm, tk), lambda i, js, j, ks, k: (i, kblk(ks, k)))
    if b_slabs and mode == "nn":
        b_spec = pl.BlockSpec((None, tk, tn), lambda i, js, j, ks, k: (js, k, j))
    elif b_slabs and mode == "nt":
        b_spec = pl.BlockSpec((None, tn, tk), lambda i, js, j, ks, k: (ks, nblk(js, j), k))
    elif mode == "nt":
        b_spec = pl.BlockSpec((tn, tk), lambda i, js, j, ks, k: (nblk(js, j), kblk(ks, k)))
    else:
        b_spec = pl.BlockSpec((tk, tn), lambda i, js, j, ks, k: (kblk(ks, k), nblk(js, j)))
    specs, args = [a_spec, b_spec], [a, b]
    for e in extras:
        specs.append(pl.BlockSpec((tm, tn), lambda i, js, j, ks, k: (i, nblk(js, j))))
        args.append(e)
    if dep is not None:
        specs.append(pl.BlockSpec(memory_space=pl.ANY))
        args.append(dep)
    if out_slabs:
        o_specs = [pl.BlockSpec((None, tm, tn), lambda i, js, j, ks, k: (js, i, j))]
        o_shapes = [jax.ShapeDtypeStruct((out_slabs, m_dim, n_dim // out_slabs), out_dtypes[0])]
    else:
        o_specs = [pl.BlockSpec((tm, tn), lambda i, js, j, ks, k: (i, nblk(js, j))) for _ in out_dtypes]
        o_shapes = [jax.ShapeDtypeStruct((m_dim, n_dim), dt) for dt in out_dtypes]

    one_k_step = k_slabs * nki == 1
    n_in, n_out = len(args), len(out_dtypes)

    def body(*refs):
        a_ref, b_ref = refs[0], refs[1]
        part = _dg(a_ref[...].astype(BF16), b_ref[...].astype(BF16), mode)

        def finish(acc):
            outs = epi(acc, *[refs[2 + j][...].astype(F32) for j in range(len(extras))])
            for o_ref, o in zip(refs[n_in:n_in + n_out], outs):
                o_ref[...] = o.astype(o_ref.dtype)

        if one_k_step:
            finish(part)
            return
        acc_ref = refs[n_in + n_out]
        ks, kk = pl.program_id(3), pl.program_id(4)

        @pl.when((ks == 0) & (kk == 0))
        def _():
            acc_ref[...] = part

        @pl.when((ks > 0) | (kk > 0))
        def _():
            acc_ref[...] += part

        pl.when((ks == k_slabs - 1) & (kk == nki - 1))(lambda: finish(acc_ref[...]))

    grid = (m_dim // tm, n_slabs, nji, k_slabs, nki)
    scratch = [] if one_k_step else [pltpu.VMEM((tm, tn), F32)]
    out = pl.pallas_call(
        body, name=name, grid=grid, in_specs=specs, out_specs=o_specs, out_shape=o_shapes, scratch_shapes=scratch,
        compiler_params=pltpu.CompilerParams(
            dimension_semantics=("parallel", "parallel", "parallel", "arbitrary", "arbitrary"),
            vmem_limit_bytes=VMEM_LIMIT),
    )(*args)
    return out[0] if n_out == 1 else out


def _stack_lanes(x, n):
    w = x.shape[1] // n
    return jnp.stack([x[:, i * w:(i + 1) * w] for i in range(n)])


def _stack_rows(x, n):
    w = x.shape[0] // n
    return jnp.stack([x[i * w:(i + 1) * w, :] for i in range(n)])


def rwkv_scan_fwd(r, lw, k, v, kap, b):
    t = r.shape[0]
    c, hps, hd = min(RWKV_CHUNK, t), RWKV_HEADS_PER_STEP, RWKV_HEAD_DIM
    nc, ng, wl = t // c, RWKV_HEADS // hps, hps * hd
    spec = pl.BlockSpec((c, wl), lambda g, ci: (ci, g))

    def body(r_ref, lw_ref, k_ref, v_ref, kap_ref, b_ref, y_ref, ck_ref, inv_ref, st_ref):
        @pl.when(pl.program_id(1) == 0)
        def _():
            st_ref[...] = jnp.zeros_like(st_ref)

        st = st_ref[...]
        ck_ref[...] = st
        ins = [x[...] for x in (r_ref, lw_ref, k_ref, v_ref, kap_ref, b_ref)]
        y, st1, inv = rwkv_chunk_fn(_stack_rows(st, hps), *[_stack_lanes(x, hps) for x in ins])
        y_ref[...] = jnp.concatenate([y[h] for h in range(hps)], axis=-1)
        st_ref[...] = jnp.concatenate([st1[h] for h in range(hps)], axis=0)
        inv_ref[...] = jnp.concatenate([inv[h] for h in range(hps)], axis=0)

    return pl.pallas_call(
        body, name="rwkv_scan_fwd", grid=(ng, nc), in_specs=[spec] * 6,
        out_specs=[spec, pl.BlockSpec((None, wl, hd), lambda g, ci: (ci, g, 0)),
                   pl.BlockSpec((None, hps * c, c), lambda g, ci: (ci, g, 0))],
        out_shape=[jax.ShapeDtypeStruct((t, RWKV_WIDTH), F32), jax.ShapeDtypeStruct((nc, RWKV_WIDTH, hd), F32),
                   jax.ShapeDtypeStruct((nc, RWKV_HEADS * c, c), F32)],
        scratch_shapes=[pltpu.VMEM((wl, hd), F32)], compiler_params=_params(),
    )(r, lw, k, v, kap, b)


def rwkv_scan_bwd(r, lw, k, v, kap, b, ck, inv_ck, dy):
    t = r.shape[0]
    c, hps, hd = min(RWKV_CHUNK, t), RWKV_HEADS_PER_STEP, RWKV_HEAD_DIM
    nc, ng, wl = t // c, RWKV_HEADS // hps, hps * hd
    spec = pl.BlockSpec((c, wl), lambda g, ci: (nc - 1 - ci, g))

    def body(r_ref, lw_ref, k_ref, v_ref, kap_ref, b_ref, ck_ref, inv_ref, dy_ref, *rest):
        out_refs, dst_ref = rest[:6], rest[6]

        @pl.when(pl.program_id(1) == 0)
        def _():
            dst_ref[...] = jnp.zeros_like(dst_ref)

        ins = [x[...] for x in (r_ref, lw_ref, k_ref, v_ref, kap_ref, b_ref)]
        dyv, ck, dst = dy_ref[...].astype(F32), ck_ref[...], dst_ref[...]
        chunk = lambda *a: rwkv_chunk_fn(*a, inv=_stack_rows(inv_ref[...], hps))[:2]
        _, vjp = jax.vjp(chunk, _stack_rows(ck, hps), *[_stack_lanes(x, hps) for x in ins])
        grads = vjp((_stack_lanes(dyv, hps), _stack_rows(dst, hps)))
        dst_ref[...] = jnp.concatenate([grads[0][h] for h in range(hps)], axis=0)
        for j in range(6):
            out_refs[j][...] = jnp.concatenate([grads[1 + j][h] for h in range(hps)], axis=-1).astype(BF16)

    return pl.pallas_call(
        body, name="rwkv_scan_bwd", grid=(ng, nc),
        in_specs=[spec] * 6 + [pl.BlockSpec((None, wl, hd), lambda g, ci: (nc - 1 - ci, g, 0)),
                               pl.BlockSpec((None, hps * c, c), lambda g, ci: (nc - 1 - ci, g, 0)), spec],
        out_specs=[spec] * 6, out_shape=[jax.ShapeDtypeStruct((t, RWKV_WIDTH), BF16)] * 6,
        scratch_shapes=[pltpu.VMEM((wl, hd), F32)], compiler_params=_params(),
    )(r, lw, k, v, kap, b, ck, inv_ck, dy)


def _ssd_specs(q, blk):
    gw = SSD_WIDTH // 2
    return [pl.BlockSpec((q, gw), lambda g, ci: (blk(ci), g)),
            pl.BlockSpec((q, SSD_STATE), lambda g, ci: (blk(ci), g)),
            pl.BlockSpec((q, SSD_STATE), lambda g, ci: (blk(ci), g)),
            pl.BlockSpec((q, LANES), lambda g, ci: (blk(ci), 0)),
            pl.BlockSpec((1, LANES), lambda g, ci: (0, 0)),
            pl.BlockSpec((1, LANES), lambda g, ci: (0, 0))]


def ssd_scan_fwd(xs, bm, cm, dt, a_log, d_skip):
    t = xs.shape[0]
    q = min(SSD_CHUNK, t)
    nc, gw = t // q, SSD_WIDTH // 2

    def body(xs_ref, bm_ref, cm_ref, dt_ref, al_ref, d_ref, y_ref, ck_ref, h_ref):
        @pl.when(pl.program_id(1) == 0)
        def _():
            h_ref[...] = jnp.zeros_like(h_ref)

        ck_ref[...] = h_ref[...]
        args = (h_ref[...], xs_ref[...], bm_ref[...], cm_ref[...], dt_ref[...], al_ref[...], d_ref[...])
        g = pl.program_id(0)

        @pl.when(g == 0)
        def _():
            y, h1 = ssd_chunk_fn(0, *args)
            y_ref[...] = y
            h_ref[...] = h1

        @pl.when(g == 1)
        def _():
            y, h1 = ssd_chunk_fn(1, *args)
            y_ref[...] = y
            h_ref[...] = h1

    return pl.pallas_call(
        body, name="ssd_scan_fwd", grid=(2, nc), in_specs=_ssd_specs(q, lambda ci: ci),
        out_specs=[pl.BlockSpec((q, gw), lambda g, ci: (ci, g)),
                   pl.BlockSpec((None, gw, SSD_STATE), lambda g, ci: (ci, g, 0))],
        out_shape=[jax.ShapeDtypeStruct((t, SSD_WIDTH), F32), jax.ShapeDtypeStruct((nc, SSD_WIDTH, SSD_STATE), F32)],
        scratch_shapes=[pltpu.VMEM((gw, SSD_STATE), F32)], compiler_params=_params(),
    )(xs, bm, cm, dt, a_log, d_skip)


def ssd_scan_bwd(xs, bm, cm, dt, a_log, d_skip, ck, dy):
    t = xs.shape[0]
    q = min(SSD_CHUNK, t)
    nc, gw = t // q, SSD_WIDTH // 2
    rev = lambda ci: nc - 1 - ci

    def body(xs_ref, bm_ref, cm_ref, dt_ref, al_ref, d_ref, ck_ref, dy_ref,
             dxs_ref, dbm_ref, dcm_ref, ddt_ref, dal_ref, dd_ref, dh_ref):
        g, ci = pl.program_id(0), pl.program_id(1)

        @pl.when(ci == 0)
        def _():
            dh_ref[...] = jnp.zeros_like(dh_ref)

        @pl.when((ci == 0) & (g == 0))
        def _():
            dal_ref[...] = jnp.zeros_like(dal_ref)
            dd_ref[...] = jnp.zeros_like(dd_ref)

        args = (ck_ref[...], xs_ref[...], bm_ref[...], cm_ref[...], dt_ref[...], al_ref[...], d_ref[...])

        def run(group):
            _, vjp = jax.vjp(functools.partial(ssd_chunk_fn, group), *args)
            dh0, dxs, dbm, dcm, ddt, dal, dd = vjp((dy_ref[...].astype(F32), dh_ref[...]))
            dh_ref[...] = dh0
            dxs_ref[...] = dxs.astype(BF16)
            dbm_ref[...] = dbm.astype(BF16)
            dcm_ref[...] = dcm.astype(BF16)
            ddt_ref[...] = ddt
            dal_ref[...] += dal
            dd_ref[...] += dd

        pl.when(g == 0)(lambda: run(0))
        pl.when(g == 1)(lambda: run(1))

    in_specs = _ssd_specs(q, rev) + [pl.BlockSpec((None, gw, SSD_STATE), lambda g, ci: (rev(ci), g, 0)),
                                     pl.BlockSpec((q, gw), lambda g, ci: (rev(ci), g))]
    return pl.pallas_call(
        body, name="ssd_scan_bwd", grid=(2, nc), in_specs=in_specs,
        out_specs=[pl.BlockSpec((q, gw), lambda g, ci: (rev(ci), g)),
                   pl.BlockSpec((q, SSD_STATE), lambda g, ci: (rev(ci), g)),
                   pl.BlockSpec((q, SSD_STATE), lambda g, ci: (rev(ci), g)),
                   pl.BlockSpec((None, q, LANES), lambda g, ci: (g, rev(ci), 0)),
                   pl.BlockSpec((1, LANES), lambda g, ci: (0, 0)),
                   pl.BlockSpec((1, LANES), lambda g, ci: (0, 0))],
        out_shape=[jax.ShapeDtypeStruct((t, SSD_WIDTH), BF16), jax.ShapeDtypeStruct((t, 2 * SSD_STATE), BF16),
                   jax.ShapeDtypeStruct((t, 2 * SSD_STATE), BF16), jax.ShapeDtypeStruct((2, t, LANES), F32),
                   jax.ShapeDtypeStruct((1, LANES), F32), jax.ShapeDtypeStruct((1, LANES), F32)],
        scratch_shapes=[pltpu.VMEM((gw, SSD_STATE), F32)], compiler_params=_params(),
    )(xs, bm, cm, dt, a_log, d_skip, ck, dy)


def loss_and_grad(x, tgt, g, tm):
    t, d = x.shape
    tm = min(tm, t)
    nb = t // tm

    def body(x_ref, t_ref, g_ref, loss_ref, dx_ref, dxb_ref, dg_ref):
        @pl.when(pl.program_id(0) == 0)
        def _():
            loss_ref[...] = jnp.zeros_like(loss_ref)
            dg_ref[...] = jnp.zeros_like(dg_ref)

        val, vjp = jax.vjp(loss_fn, x_ref[...], t_ref[...], g_ref[...])
        dx, _, dg = vjp(jnp.ones((1, 1), F32))
        loss_ref[...] += jnp.broadcast_to(val, loss_ref.shape)
        dx_ref[...] = dx
        dxb_ref[...] = dx.astype(BF16)
        dg_ref[...] += dg

    row = pl.BlockSpec((tm, d), lambda i: (i, 0))
    one = pl.BlockSpec((1, d), lambda i: (0, 0))
    return pl.pallas_call(
        body, name="loss_and_grad", grid=(nb,), in_specs=[row, row, one],
        out_specs=[pl.BlockSpec((SUBLANES, LANES), lambda i: (0, 0)), row, row, one],
        out_shape=[jax.ShapeDtypeStruct((SUBLANES, LANES), F32), jax.ShapeDtypeStruct((t, d), F32),
                   jax.ShapeDtypeStruct((t, d), BF16), jax.ShapeDtypeStruct((1, d), F32)],
        compiler_params=_params(),
    )(x, tgt, g)


def adamw(name, recv, w, m, v, dep=None):
    rows, cols = w.shape
    n_slabs = recv.shape[0]
    recv_block_bytes = 4 * 1024 * 1024
    tm = _pick(rows, [c for c in (256, 128, 64, 32, 16, 8) if n_slabs * c * cols * 4 <= recv_block_bytes])
    c1 = 1.0 / (1.0 - ADAM_B1 ** ADAM_STEP)
    c2 = 1.0 / (1.0 - ADAM_B2 ** ADAM_STEP)

    n_dep = 0 if dep is None else 1

    def body(recv_ref, w_ref, m_ref, v_ref, *rest):
        g_ref, d_ref, nm_ref, nv_ref = rest[n_dep:]
        g = recv_ref[0].astype(F32)
        for p in range(1, n_slabs):
            g = g + recv_ref[p].astype(F32)
        nm =ADAM_B1 * m_ref[...] + (1.0 - ADAM_B1) * g
        nv = ADAM_B2 * v_ref[...] + (1.0 - ADAM_B2) * jnp.square(g)
        g_ref[...] = g
        nm_ref[...] = nm
        nv_ref[...] = nv
        d_ref[...] = -ADAM_LR * ((nm * c1) / (jnp.sqrt(nv * c2) + ADAM_EPS) + ADAM_WD * w_ref[...])

    blk = pl.BlockSpec((tm, cols), lambda i: (i, 0))
    return pl.pallas_call(
        body, name=name, grid=(rows // tm,),
        in_specs=[pl.BlockSpec((n_slabs, tm, cols), lambda i: (0, i, 0)), blk, blk, blk]
        + [pl.BlockSpec(memory_space=pl.ANY)] * n_dep,
        out_specs=[blk] * 4, out_shape=[jax.ShapeDtypeStruct((rows, cols), F32)] * 4,
        compiler_params=_params(),
    )(recv, w, m, v, *([] if dep is None else [dep]))


def _mesh_pos():
    return lax.axis_index("x"), lax.axis_index("y"), lax.axis_index("c")


def _peer(pos, mask):
    x, y, c = pos
    return (1 - x if mask & 4 else x, 1 - y if mask & 2 else y, 1 - c if mask & 1 else c)


def _linear(pos):
    return 4 * pos[0] + 2 * pos[1] + pos[2]


class Exchange:
    MASKS = {"gather": (1, 2, 3, 4, 5, 6, 7), "scatter": (1, 2, 3, 4, 5, 6, 7), "gather_chips": (1, 2, 4, 6),
             "forward": (2, 4, 6)}

    def __init__(self, xs, kind, lands=None):
        self.kind, self.masks = kind, self.MASKS[kind]
        self.xs = [] if kind == "forward" else list(xs)
        if kind == "forward":
            self.land_shape = [jax.ShapeDtypeStruct(l.shape, l.dtype) for l in lands]
        elif kind == "scatter":
            self.land_shape = [jax.ShapeDtypeStruct(x.shape, x.dtype) for x in xs]
        else:
            self.land_shape = [jax.ShapeDtypeStruct((N_DEV,) + x.shape, x.dtype) for x in xs]
        self.n = len(self.land_shape)
        copies = self.n * len(self.masks)
        self.sems = [pltpu.SemaphoreType.DMA((copies,)), pltpu.SemaphoreType.DMA((copies,)),
                     pltpu.SemaphoreType.DMA((self.n,))]

    def _copies(self, ins, outs, sems, landing):
        send_sems, recv_sems, local_sems = sems
        me = _mesh_pos()
        me_lin = _linear(me)
        local, remote = [], []
        for ti in range(self.n):
            if self.kind != "forward":
                src_mine = ins[ti].at[me_lin] if self.kind == "scatter" else ins[ti]
                local.append(pltpu.make_async_copy(src_mine, outs[ti].at[me_lin], local_sems.at[ti]))
            for j, mask in enumerate(self.masks):
                if self.kind == "forward":
                    peer = _peer(me, 1)
                    src = outs[ti].at[_linear(_peer(me, mask))]
                    dst = outs[ti].at[_linear(_peer(me, mask ^ 1 if landing else mask))]
                else:
                    peer = _peer(me, mask)
                    src = ins[ti].at[_linear(peer)] if self.kind == "scatter" else ins[ti]
                    dst = outs[ti].at[_linear(peer) if landing else me_lin]
                sem_index = ti * len(self.masks) + j
                remote.append(pltpu.make_async_remote_copy(
                    src_ref=src, dst_ref=dst, send_sem=send_sems.at[sem_index], recv_sem=recv_sems.at[sem_index],
                    device_id=peer, device_id_type=pl.DeviceIdType.MESH))
        return local, remote

    def start(self, ins, outs, sems):
        local, remote = self._copies(ins, outs, sems, landing=False)
        for cp in local + remote:
            cp.start()

    def finish(self, ins, outs, sems):
        local, remote = self._copies(ins, outs, sems, landing=True)
        for cp in remote:
            cp.wait_recv()
        for cp in remote:
            cp.wait_send()
        for cp in local:
            cp.wait()


def exchange_start(name, xs, kind, dep=None, lands=None):
    ex = Exchange(xs, kind, lands)
    hbm = pl.BlockSpec(memory_space=pltpu.HBM)
    sem = pl.BlockSpec(memory_space=pltpu.SEMAPHORE)
    if lands is None:
        lands = [lax.empty(s.shape, s.dtype) for s in ex.land_shape]
    n_src, n = len(ex.xs), ex.n
    n_inputs = n_src + n + (0 if dep is None else 1)

    def body(*refs):
        ins, lnd, sems, token = refs[:n_src], refs[n_src:n_src + n], refs[n_inputs:n_inputs + 3], refs[-1]
        ex.start(ins, lnd, sems)
        token[...] = jnp.zeros_like(token)

    res = pl.pallas_call(
        body, name=name, in_specs=[hbm] * (n_src + n) + ([] if dep is None else [pl.BlockSpec(memory_space=pl.ANY)]),
        out_specs=[sem] * 3 + [hbm] * (n_src + n) + [pl.BlockSpec(memory_space=pltpu.VMEM)],
        out_shape=ex.sems + [pltpu.HBM(x.shape, x.dtype) for x in ex.xs]
        + [pltpu.HBM(s.shape, s.dtype) for s in ex.land_shape] + [jax.ShapeDtypeStruct((SUBLANES, LANES), F32)],
        input_output_aliases={i: 3 + i for i in range(n_src + n)},
        compiler_params=pltpu.CompilerParams(has_side_effects=pltpu.SideEffectType.DATAFLOW_SIDE_EFFECTING),
    )(*[pltpu.with_memory_space_constraint(x, pltpu.HBM) for x in ex.xs + list(lands)],
      *([] if dep is None else [dep]))
    return (ex, res[:3], res[3:3 + n_src], res[3 + n_src:3 + n_src + n]), res[-1]


def exchange_wait(name, handles, after):
    ex, sems, srcs, lands = handles
    n_src, n = len(srcs), len(lands)
    hbm = pl.BlockSpec(memory_space=pltpu.HBM)
    sem = pl.BlockSpec(memory_space=pltpu.SEMAPHORE)

    def body(*refs):
        ins, lnd, sem_refs = refs[:n_src], refs[n_src:n_src + n], refs[n_src + n:n_src + n + 3]
        ex.finish(ins, lnd, sem_refs)

    res = pl.pallas_call(
        body, name=name, in_specs=[hbm] * (n_src + n) + [sem] * 3 + [pl.BlockSpec(memory_space=pl.ANY)],
        out_specs=[hbm] * (n_src + n),
        out_shape=[pltpu.HBM(x.shape, x.dtype) for x in srcs] + [pltpu.HBM(x.shape, x.dtype) for x in lands],
        input_output_aliases={i: i for i in range(n_src + n)},
        compiler_params=pltpu.CompilerParams(has_side_effects=pltpu.SideEffectType.DATAFLOW_SIDE_EFFECTING),
    )(*srcs, *lands, *sems, after)
    return res[n_src:]


def forward_start(name, chip_gather, after):
    lands = exchange_wait(name + "_wait", chip_gather, after)
    return exchange_start(name + "_forward_start", [], "forward", lands=lands)


_Z = (0, 1024)
_XBC = (1024, 2560)
_DT = (2560, 2576)
_RKV = (2576, 5648)
_PW = (5648, 5744)
_PA = (5744, 5840)
_PG = (5840, 6096)
D_IN = 6096

_SMALL = ("norm_mix_g", "ssd_conv_b", "ssd_dt_bias", "ssd_a_log", "ssd_d", "ssd_norm_g", "rwkv_mu", "rwkv_w0",
          "rwkv_a0", "rwkv_k_k", "rwkv_k_a", "rwkv_r_k", "rwkv_ln_w", "rwkv_ln_b", "norm_x_g", "norm_mem_g",
          "norm_ffn_g", "final_norm_g")
_WEIGHTS = ("norm_mix_g", "w_in", "ssd_conv_w", "ssd_conv_b", "ssd_dt_bias", "ssd_a_log", "ssd_d", "ssd_norm_g",
            "rwkv_mu", "rwkv_w0", "rwkv_w2", "rwkv_a0", "rwkv_a2", "rwkv_g2", "rwkv_k_k", "rwkv_k_a", "rwkv_r_k",
            "rwkv_ln_w", "rwkv_ln_b", "w_out", "norm_x_g", "norm_mem_g", "xattn_wq", "xattn_wk", "xattn_wv",
            "xattn_wo", "norm_ffn_g", "ffn_w1", "ffn_w2", "final_norm_g")


def _pad_lanes(x, width=LANES):
    return jnp.pad(x, ((0, 0), (0, width - x.shape[1])))


def _pack_small(vals):
    flat = jnp.concatenate([vals[n].reshape(-1) for n in _SMALL])
    rows = -(-flat.shape[0] // (LANES * SUBLANES)) * SUBLANES
    return jnp.pad(flat, (0, rows * LANES - flat.shape[0])).reshape(rows, LANES)


def _unpack_small(packed, shapes):
    flat = packed.reshape(-1)
    out, pos = {}, 0
    for n in _SMALL:
        size = 1
        for s in shapes[n]:
            size *= s
        out[n] = flat[pos:pos + size].reshape(shapes[n])
        pos += size
    return out


def _rows(w, rng):
    return w[rng[0]:rng[1]]


def sum_slabs(name, recv):
    n, rows, cols = recv.shape
    tc = _pick(cols, (256, 128))

    def body(r_ref, o_ref):
        acc = r_ref[0].astype(F32)
        for p in range(1, n):
            acc = acc + r_ref[p].astype(F32)
        o_ref[...] = acc

    return pl.pallas_call(
        body, name=name, grid=(cols // tc,), in_specs=[pl.BlockSpec((n, rows, tc), lambda j: (0, 0, j))],
        out_specs=pl.BlockSpec((rows, tc), lambda j: (0, j)), out_shape=jax.ShapeDtypeStruct((rows, cols), F32),
        compiler_params=_params(),
    )(recv)


def kernel(x, mem, norm_mix_g, w_in, ssd_conv_w, ssd_conv_b, ssd_dt_bias, ssd_a_log, ssd_d, ssd_norm_g, rwkv_mu, rwkv_w0, rwkv_w2, rwkv_a0, rwkv_a2, rwkv_g2, rwkv_k_k, rwkv_k_a, rwkv_r_k, rwkv_ln_w, rwkv_ln_b, w_out, norm_x_g, norm_mem_g, xattn_wq, xattn_wk, xattn_wv, xattn_wo, norm_ffn_g, ffn_w1, ffn_w2, final_norm_g, loss_target, m_norm_mix_g, m_w_in, m_ssd_conv_w, m_ssd_conv_b, m_ssd_dt_bias, m_ssd_a_log, m_ssd_d, m_ssd_norm_g, m_rwkv_mu, m_rwkv_w0, m_rwkv_w2, m_rwkv_a0, m_rwkv_a2, m_rwkv_g2, m_rwkv_k_k, m_rwkv_k_a, m_rwkv_r_k, m_rwkv_ln_w, m_rwkv_ln_b, m_w_out, m_norm_x_g, m_norm_mem_g, m_xattn_wq, m_xattn_wk, m_xattn_wv, m_xattn_wo, m_norm_ffn_g, m_ffn_w1, m_ffn_w2, m_final_norm_g, v_norm_mix_g, v_w_in, v_ssd_conv_w, v_ssd_conv_b, v_ssd_dt_bias, v_ssd_a_log, v_ssd_d, v_ssd_norm_g, v_rwkv_mu, v_rwkv_w0, v_rwkv_w2, v_rwkv_a0, v_rwkv_a2, v_rwkv_g2, v_rwkv_k_k, v_rwkv_k_a, v_rwkv_r_k, v_rwkv_ln_w, v_rwkv_ln_b, v_w_out, v_norm_x_g, v_norm_mem_g, v_xattn_wq, v_xattn_wk, v_xattn_wv, v_xattn_wo, v_norm_ffn_g, v_ffn_w1, v_ffn_w2, v_final_norm_g):
    given = dict(locals())
    wts = {n: given[n] for n in _WEIGHTS}
    mom_m = {n: given["m_" + n] for n in _WEIGHTS}
    mom_v = {n: given["v_" + n] for n in _WEIGHTS}
    d = D_MODEL
    xt, memt, tgt = x[0], mem[0], loss_target[0]
    tm = 256

    big = {"w_in": jnp.transpose(w_in[0]), "w_out": w_out[0], "xattn_wq": xattn_wq[0], "xattn_wk": xattn_wk[0],
           "xattn_wv": xattn_wv[0], "xattn_wo": xattn_wo[0], "ffn_w1": ffn_w1[0], "ffn_w2": ffn_w2[0]}
    small_sh = {"ssd_conv_w": ssd_conv_w.reshape(4, -1), "rwkv_w2": rwkv_w2[0], "rwkv_a2": rwkv_a2[0],
                "rwkv_g2": rwkv_g2[0]}
    cast_one = lambda n, deps=(): rowwise_fwd("cast_" + n, cast_fn, [big[n]], [], [(big[n].shape[1], BF16)],
                                              256 if big[n].shape[0] % 256 == 0 else big[n].shape[0], deps=deps)[0]
    gather_in, token_in = exchange_start("gather_in_start", [cast_one("w_in")] + list(small_sh.values()), "gather_chips")
    cast = {n: cast_one(n, deps=[token_in]) for n in big if n != "w_in"}
    late_a = ("w_out", "xattn_wq", "xattn_wk", "xattn_wv", "xattn_wo")
    late_b = ("ffn_w1", "ffn_w2")
    gather_a, token_a = exchange_start("gather_attn_start", [cast[n] for n in late_a], "gather_chips", dep=token_in)
    gather_b, token_b = exchange_start("gather_ffn_start", [cast[n] for n in late_b], "gather_chips", dep=token_a)
    (h1,) = rowwise_fwd("norm_mix", rmsnorm_fn, [xt], [norm_mix_g], [(d, BF16)], tm, deps=[token_b])
    forward_in, token_in = forward_start("gather_in", gather_in, after=h1)
    gathered = exchange_wait("gather_in_forward_wait", forward_in, after=token_in)
    g_big = {"w_in": gathered[0]}
    g_small = dict(zip(small_sh, gathered[1:]))

    pad_rows = lambda a: jnp.pad(a, ((0, LANES - a.shape[0]), (0, 0)))
    w_in_t = g_big["w_in"].reshape(D_IN, d)
    wt_z, wt_xbc, wt_rkv, wt_pg = (_rows(w_in_t, r) for r in (_Z, _XBC, _RKV, _PG))
    wt_sm = jnp.concatenate([pad_rows(_rows(w_in_t, r)) for r in (_PW, _PA, _DT)], axis=0)
    unshard_cols = lambda g: jnp.transpose(g, (1, 0, 2)).reshape(g.shape[1], -1)
    conv_w_f = unshard_cols(g_small["ssd_conv_w"])
    w2p, a2p = pad_rows(unshard_cols(g_small["rwkv_w2"])), pad_rows(unshard_cols(g_small["rwkv_a2"]))
    g2_f = unshard_cols(g_small["rwkv_g2"])

    mu = rwkv_mu
    mu_rkv, mu_pg = mu[:, :3072], mu[:, 3264:3520]
    mu_pwa = jnp.concatenate([_pad_lanes(mu[:, 3072:3168]), _pad_lanes(mu[:, 3168:3264])], axis=1)
    dt_bias_p, a_log_p, d_p = _pad_lanes(ssd_dt_bias), _pad_lanes(ssd_a_log), _pad_lanes(ssd_d)
    r_k_row = rwkv_r_k.reshape(1, RWKV_WIDTH)
    g_final = final_norm_g.reshape(1, d)

    u_z = mm("in_z", h1, wt_z, "nt")
    u_xbc = mm("in_xbc", h1, wt_xbc, "nt")
    u_rkv = mm("in_rkv", h1, wt_rkv, "nt")
    u_pg = mm("in_pg", h1, wt_pg, "nt")
    u_sm = mm("in_small", h1, wt_sm, "nt")

    ssd_pre_rows = lambda: [Rows(u_xbc, shifts=(1, 2, 3)), Rows(u_sm, LANES, 2)]
    ssd_pre_params = [conv_w_f, ssd_conv_b, dt_bias_p]
    xs, bm, cm, dt = rowwise_fwd("ssd_pre", ssd_pre_fn, ssd_pre_rows(), ssd_pre_params,
                                 [(SSD_WIDTH, F32), (256, F32), (256, F32), (LANES, F32)], tm)
    y_scan, ssd_ck = ssd_scan_fwd(xs, bm, cm, dt, a_log_p, d_p)
    (y_ssd,) = rowwise_fwd("ssd_post", ssd_post_fn, [y_scan, u_z], [ssd_norm_g], [(SSD_WIDTH, BF16)], tm)

    rwkv_pre_rows = lambda: [Rows(u_rkv, shifts=(1,)), Rows(u_pg, shifts=(1,)), Rows(u_sm, 2 * LANES, 0, shifts=(1,))]
    rwkv_pre_params = [mu_rkv, mu_pg, mu_pwa, rwkv_w0, w2p, rwkv_a0, a2p, g2_f, rwkv_k_k, rwkv_k_a]
    forward_a, token_a = forward_start("gather_attn", gather_a, after=y_ssd)
    r_, lw_, k_, v_, kap_, b_, gate_ = rowwise_fwd("rwkv_pre", rwkv_pre_fn, rwkv_pre_rows(), rwkv_pre_params,
                                                   [(RWKV_WIDTH, F32)] * 7, 128, deps=[token_a])
    ys_r, rwkv_ck, rwkv_inv = rwkv_scan_fwd(r_, lw_, k_, v_, kap_, b_)
    forward_b, token_b = forward_start("gather_ffn", gather_b, after=ys_r)
    g_big.update(zip(late_a, exchange_wait("gather_attn_forward_wait", forward_a, after=token_b)))
    w_out_f = g_big["w_out"].reshape(d, d)
    wq_f, wk_f, wv_f, wo_f = (g_big[n].reshape(d, d) for n in ("xattn_wq", "xattn_wk", "xattn_wv", "xattn_wo"))
    rwkv_post_params = [rwkv_ln_w, rwkv_ln_b, r_k_row]
    (y_rwkv,) = rowwise_fwd("rwkv_post", rwkv_post_fn, [ys_r, r_, k_, v_, gate_], rwkv_post_params,
                            [(RWKV_WIDTH, BF16)], tm)
    ycat = jnp.concatenate([y_ssd, y_rwkv], axis=1)
    x1 = mm("out_proj", ycat, w_out_f, "nn", res=xt)

    (h2,) = rowwise_fwd("norm_x", rmsnorm_fn, [x1], [norm_x_g], [(d, BF16)], tm)
    (mn,) = rowwise_fwd("norm_mem", rmsnorm_fn, [memt], [norm_mem_g], [(d, BF16)], tm)
    q = mm("xattn_q", h2, wq_f, "nn", out_dtype=BF16)
    kx = mm("xattn_k", mn, wk_f, "nn")
    vx = mm("xattn_v", mn, wv_f, "nn")
    (o,) = rowwise_fwd("xattn", attn_fn, [q], [kx, vx], [(d, BF16)], tm)
    x2 = mm("xattn_o", o, wo_f, "nn", res=x1)

    (h3,) = rowwise_fwd("norm_ffn", rmsnorm_fn, [x2], [norm_ffn_g], [(d, BF16)], tm)
    w1_s, w2_g = exchange_wait("gather_ffn_forward_wait", forward_b, after=h3)
    w2_f = w2_g.reshape(D_FF, d)
    relu2_epi = lambda acc: (jnp.square(jnp.maximum(acc, 0.0)), jnp.maximum(acc, 0.0))
    hid, relu_a = mm("ffn_1", h3, w1_s, "nn", b_slabs=N_DEV, epi=relu2_epi, out_dtypes=[BF16, BF16])
    x3 = mm("ffn_2", hid, w2_f, "nn", res=x2)

    loss_blk, dx3, dx3_b, dg_final = loss_and_grad(x3, tgt, g_final, tm)

    grads = {}
    grads["ffn_w2"] = mm("d_ffn_w2", hid, dx3_b, "tn", out_dtype=BF16).reshape(N_DEV, D_FF // N_DEV, d)
    sc_w2, tok = exchange_start("scatter_ffn_w2_start", [grads["ffn_w2"]], "scatter")
    da = mm("d_hid", dx3_b, w2_f, "nt", dep=tok, epi=lambda acc, ra: (2.0 * acc * ra,), extras=[relu_a],
            out_dtypes=[BF16])
    grads["ffn_w1"] = mm("d_ffn_w1", h3, da, "tn", out_dtype=BF16, out_slabs=N_DEV)
    sc_w1, tok = exchange_start("scatter_ffn_w1_start", [grads["ffn_w1"]], "scatter")
    dh3 = mm("d_h3", da, w1_s, "nt", out_dtype=BF16, b_slabs=N_DEV, dep=tok)
    (dx2,), (dg_ffn,) = rowwise_bwd("norm_ffn_bwd", rmsnorm_fn, [x2], [norm_ffn_g], [[dh3]], tm, [F32], row_add=[dx3])

    grads["xattn_wo"] = mm("d_wo", o, dx2, "tn", out_dtype=BF16).reshape(N_DEV, d // N_DEV, d)
    sc_wo, tok = exchange_start("scatter_wo_start", [grads["xattn_wo"]], "scatter")
    d_o = mm("d_o", dx2, wo_f, "nt", out_dtype=BF16, dep=tok)
    (dq,), (dkx, dvx) = rowwise_bwd("xattn_bwd", attn_fn, [q], [kx, vx], [[d_o]], tm, [BF16])
    grads["xattn_wq"] = mm("d_wq", h2, dq, "tn", out_dtype=BF16).reshape(N_DEV, d // N_DEV, d)
    grads["xattn_wk"] = mm("d_wk", mn, dkx, "tn", out_dtype=BF16).reshape(N_DEV, d // N_DEV, d)
    grads["xattn_wv"] = mm("d_wv", mn, dvx, "tn", out_dtype=BF16).reshape(N_DEV, d // N_DEV, d)
    qkv = ("xattn_wq", "xattn_wk", "xattn_wv")
    sc_qkv, tok = exchange_start("scatter_qkv_start", [grads[n] for n in qkv], "scatter")
    dmn = mm("d_mn_v", dvx, wv_f, "nt", res=mm("d_mn_k", dkx, wk_f, "nt", dep=tok))
    _, (dg_mem,) = rowwise_bwd("norm_mem_bwd", rmsnorm_fn, [memt], [norm_mem_g], [[dmn]], tm, [None])
    dh2 = mm("d_h2", dq, wq_f, "nt", out_dtype=BF16, dep=dg_mem)
    (dx1,), (dg_x,) = rowwise_bwd("norm_x_bwd", rmsnorm_fn, [x1], [norm_x_g], [[dh2]], tm, [F32], row_add=[dx2])

    grads["w_out"] = mm("d_w_out", ycat, dx1, "tn", out_dtype=BF16).reshape(N_DEV, d // N_DEV, d)
    sc_wout, tok = exchange_start("scatter_w_out_start", [grads["w_out"]], "scatter")
    d_ycat = mm("d_ycat", dx1, w_out_f, "nt", out_dtype=BF16, dep=tok)

    (d_ys, d_r1, d_k1, d_v1, d_gate), (dln_w, dln_b, dr_k) = rowwise_bwd(
        "rwkv_post_bwd", rwkv_post_fn, [ys_r, r_, k_, v_, gate_], rwkv_post_params,
        [[Rows(d_ycat, RWKV_WIDTH, 1)]], tm, [BF16] * 5)
    d_r2, d_lw, d_k2, d_v2, d_kap, d_b = rwkv_scan_bwd(r_, lw_, k_, v_, kap_, b_, rwkv_ck, rwkv_inv, d_ys)
    (du_rkv, du_pg, du_pwa), rwkv_pg = rowwise_bwd(
        "rwkv_pre_bwd", rwkv_pre_fn, rwkv_pre_rows(), rwkv_pre_params,
        [[d_r1, d_r2], [d_lw], [d_k1, d_k2], [d_v1, d_v2], [d_kap], [d_b], [d_gate]], 128, [BF16] * 3)
    dmu_rkv, dmu_pg, dmu_pwa, dw0, dw2p, da0, da2p, dg2, dk_k, dk_a = rwkv_pg

    (d_yscan, du_z), (dssd_norm_g,) = rowwise_bwd("ssd_post_bwd", ssd_post_fn, [y_scan, u_z], [ssd_norm_g],
                                                  [[Rows(d_ycat, SSD_WIDTH, 0)]], tm, [BF16, BF16])
    dxs, dbm, dcm, ddt2, da_log_p, dd_p = ssd_scan_bwd(xs, bm, cm, dt, a_log_p, d_p, ssd_ck, d_yscan)
    (du_xbc, du_dt), (dconv_w, dconv_b, ddt_bias_p) = rowwise_bwd(
        "ssd_pre_bwd", ssd_pre_fn, ssd_pre_rows(), ssd_pre_params,
        [[dxs], [dbm], [dcm], [ddt2[0], ddt2[1]]], tm, [BF16, BF16])
    du_sm = jnp.concatenate([du_pwa, du_dt], axis=1)

    dwt_z = mm("d_w_z", du_z, h1, "tn", out_dtype=BF16)
    dwt_xbc = mm("d_w_xbc", du_xbc, h1, "tn", out_dtype=BF16)
    dwt_rkv = mm("d_w_rkv", du_rkv, h1, "tn", out_dtype=BF16)
    dwt_pg = mm("d_w_pg", du_pg, h1, "tn", out_dtype=BF16)
    dwt_sm = mm("d_w_small", du_sm, h1, "tn", out_dtype=BF16)
    dwt_full = jnp.concatenate([dwt_z, dwt_xbc, dwt_sm[256:272], dwt_rkv, dwt_sm[0:96], dwt_sm[128:224], dwt_pg], axis=0)
    to_slabs = lambda g: jnp.transpose(g.reshape(g.shape[0], N_DEV, -1), (1, 0, 2))
    grads["w_in"] = dwt_full.reshape(N_DEV, D_IN // N_DEV, d)
    grads["ssd_conv_w"] = to_slabs(dconv_w)
    grads["rwkv_w2"] = to_slabs(dw2p[:96])
    grads["rwkv_a2"] = to_slabs(da2p[:96])
    grads["rwkv_g2"] = to_slabs(dg2)
    tail = ("w_in", "ssd_conv_w", "rwkv_w2", "rwkv_a2", "rwkv_g2")
    sc_tail, tok = exchange_start("scatter_tail_start", [grads[n] for n in tail], "scatter")
    dh1 = mm("d_h1_z", du_z, wt_z, "nn", dep=tok)
    dh1 = mm("d_h1_xbc", du_xbc, wt_xbc, "nn", res=dh1)
    dh1 = mm("d_h1_rkv", du_rkv, wt_rkv, "nn", res=dh1)
    dh1 = mm("d_h1_pg", du_pg, wt_pg, "nn", res=dh1)
    dh1 = mm("d_h1_small", du_sm, wt_sm, "nn", res=dh1)
    (dx,), (dg_mix,) = rowwise_bwd("norm_mix_bwd", rmsnorm_fn, [xt], [norm_mix_g], [[dh1]], tm, [F32], row_add=[dx1])

    dmu =jnp.concatenate([dmu_rkv, dmu_pwa[:, 0:96], dmu_pwa[:, 128:224], dmu_pg], axis=1)
    small_grads = {
        "norm_mix_g": dg_mix, "ssd_conv_b": dconv_b, "ssd_dt_bias": ddt_bias_p[:, :16], "ssd_a_log": da_log_p[:, :16],
        "ssd_d": dd_p[:, :16], "ssd_norm_g": dssd_norm_g, "rwkv_mu": dmu, "rwkv_w0": dw0, "rwkv_a0": da0,
        "rwkv_k_k": dk_k, "rwkv_k_a": dk_a, "rwkv_r_k": dr_k, "rwkv_ln_w": dln_w, "rwkv_ln_b": dln_b,
        "norm_x_g": dg_x, "norm_mem_g": dg_mem, "norm_ffn_g": dg_ffn, "final_norm_g": dg_final}

    gather_small, tok = exchange_start("gather_small_start", [_pack_small(small_grads)], "gather")
    received = {}
    for names, handle in ((("ffn_w2",), sc_w2), (("ffn_w1",), sc_w1), (("xattn_wo",), sc_wo), (qkv, sc_qkv),
                          (("w_out",), sc_wout)):
        received.update(zip(names, exchange_wait("scatter_" + names[0] + "_wait", handle, after=tok)))

    out_g, out_d, out_m, out_v = {}, {}, {}, {}

    def run_adamw(n, dep):
        shape = wts[n].shape
        two_d = lambda a: a.reshape(-1, shape[-1])
        if n == "w_in":
            recv = jnp.transpose(sum_slabs("sum_w_in", received[n]))[None]
        else:
            recv = received[n].reshape(N_DEV, -1, shape[-1])
        res = adamw("adamw_" + n, recv, two_d(wts[n]), two_d(mom_m[n]), two_d(mom_v[n]), dep=dep)
        out_g[n], out_d[n], out_m[n], out_v[n] = (r.reshape(shape) for r in res)
        return res[0]

    last = None
    for n in ("ffn_w2", "ffn_w1", "xattn_wo") + qkv + ("w_out",):
        last = run_adamw(n, last)
    received.update(zip(tail, exchange_wait("scatter_tail_wait", sc_tail, after=last)))
    for n in tail:
        last = run_adamw(n, last)
    (small_all,) = exchange_wait("gather_small_wait", gather_small, after=last)
    res = adamw("adamw_small", small_all, _pack_small(wts), _pack_small(mom_m), _pack_small(mom_v))
    shapes = {n: wts[n].shape for n in _SMALL}
    for dst, packed in zip((out_g, out_d, out_m, out_v), res):
        dst.update(_unpack_small(packed, shapes))

    loss = lax.psum(loss_blk[0, 0], ("x", "y", "c"))
    return (loss, dx[None], *[out_g[n] for n in _WEIGHTS], *[out_d[n] for n in _WEIGHTS],
            *[out_m[n] for n in _WEIGHTS], *[out_v[n] for n in _WEIGHTS])
```

```python
import functools

import jax
import jax.numpy as jnp
from jax import lax
from jax.experimental import pallas as pl
from jax.experimental.pallas import tpu as pltpu

F32 = jnp.float32
BF16 = jnp.bfloat16

N_DEV = 8
D_MODEL = 2048
NORM_EPS = 1e-6
SSD_WIDTH = 1024
SSD_HEAD_DIM = 64
SSD_STATE = 128
SSD_CHUNK = 128
SSD_HEADS_PER_GROUP = 8
RWKV_WIDTH = 1024
RWKV_HEADS = 16
RWKV_HEAD_DIM = 64
RWKV_LN_EPS = 64e-5
RWKV_CHUNK = 64
RWKV_HEADS_PER_STEP = 16
XATTN_HEADS = 4
XATTN_HEAD_DIM = 512
D_FF = 8192
LANES = 128
SUBLANES = 8
VMEM_LIMIT = 56 * 1024 * 1024

ADAM_LR = 0.001
ADAM_B1 = 0.9
ADAM_B2 = 0.999
ADAM_EPS = 1e-08
ADAM_WD = 0.01
ADAM_STEP = 10

_DN = {"nn": ((1,), (0,)), "nt": ((1,), (1,)), "tn": ((0,), (0,))}


def _dg(a, b, mode):
    (ca,), (cb,) = _DN[mode]
    dn = (((ca + 1,), (cb + 1,)), ((0,), (0,))) if a.ndim == 3 else (((ca,), (cb,)), ((), ()))
    return lax.dot_general(a, b, dn, preferred_element_type=F32)


@functools.partial(jax.custom_vjp, nondiff_argnums=(2,))
def bdot(a, b, mode):
    return _dg(a.astype(BF16), b.astype(BF16), mode)


def _bdot_fwd(a, b, mode):
    return bdot(a, b, mode), (a, b)


def _bdot_bwd(mode, res, g):
    a, b = res
    ab, bb, gb = a.astype(BF16), b.astype(BF16), g.astype(BF16)
    if mode == "nn":
        da, db = _dg(gb, bb, "nt"), _dg(ab, gb, "tn")
    elif mode == "nt":
        da, db = _dg(gb, bb, "nn"), _dg(gb, ab, "tn")
    else:
        da, db = _dg(bb, gb, "nt"), _dg(ab, gb, "nn")
    return da.astype(a.dtype), db.astype(b.dtype)


bdot.defvjp(_bdot_fwd, _bdot_bwd)


def _split2(x):
    hi = x.astype(BF16)
    return hi, (x - hi.astype(F32)).astype(BF16)


def _dot01(x, m01):
    hi, lo = _split2(x)
    return _dg(hi, m01, "nn") + _dg(lo, m01, "nn")


def _exact_dot_impl(a, b, mode, exact):
    if exact == "a":
        ae = a.astype(BF16)
        return sum(_dg(ae, part, mode) for part in _split2(b))
    be = b.astype(BF16)
    return sum(_dg(part, be, mode) for part in _split2(a))


@functools.partial(jax.custom_vjp, nondiff_argnums=(2, 3))
def exact_dot(a, b, mode, exact):
    return _exact_dot_impl(a, b, mode, exact)


def _exact_dot_fwd(a, b, mode, exact):
    return _exact_dot_impl(a, b, mode, exact), (a, b)


def _exact_dot_bwd(mode, exact, res, g):
    a, b = res
    if exact == "a":
        db = {"nn": lambda: _exact_dot_impl(a, g, "tn", "a"), "nt": lambda: _exact_dot_impl(g, a, "tn", "b"),
              "tn": lambda: _exact_dot_impl(a, g, "nn", "a")}[mode]()
        return jnp.zeros_like(a), db
    da = {"nn": lambda: _exact_dot_impl(g, b, "nt", "b"), "nt": lambda: _exact_dot_impl(g, b, "nn", "b"),
          "tn": lambda: _exact_dot_impl(b, g, "nt", "a")}[mode]()
    return da, jnp.zeros_like(b)


exact_dot.defvjp(_exact_dot_fwd, _exact_dot_bwd)


def _head_indicator(width, heads, transpose):
    hd = width // heads
    shape = (LANES, width) if transpose else (width, LANES)
    lane = lax.broadcasted_iota(jnp.int32, shape, 1 if not transpose else 0)
    pos = lax.broadcasted_iota(jnp.int32, shape, 0 if not transpose else 1)
    return ((pos >= lane * hd) & (pos < lane * hd + hd)).astype(BF16)


@jax.custom_vjp
def head_sum(x):
    w = x.shape[-1]
    e = _head_indicator(w, w // RWKV_HEAD_DIM, False)
    et = _head_indicator(w, w // RWKV_HEAD_DIM, True)
    return _dot01(_dot01(x, e), et)


head_sum.defvjp(lambda x: (head_sum(x), None), lambda _, g: (head_sum(g),))


def rmsnorm_fn(x, g):
    y = x * lax.rsqrt(jnp.mean(x * x, axis=-1, keepdims=True) + NORM_EPS)
    return ((y * g).astype(BF16),)


def cast_fn(x):
    return (x.astype(BF16),)


def ssd_pre_fn(xbc, xbc1, xbc2, xbc3, dt_raw, conv_w, conv_b, dt_bias):
    c = conv_w[3:4] * xbc + conv_w[2:3] * xbc1 + conv_w[1:2] * xbc2 + conv_w[0:1] * xbc3 + conv_b
    act = c * jax.nn.sigmoid(c)
    dt = jax.nn.softplus(dt_raw + dt_bias)
    return act[:, :SSD_WIDTH], act[:, SSD_WIDTH:SSD_WIDTH + 256], act[:, SSD_WIDTH + 256:], dt


def ssd_post_fn(yscan, z, norm_g):
    y = yscan * (z * jax.nn.sigmoid(z))
    half = SSD_WIDTH // 2
    parts = []
    for g in range(2):
        yg = y[:, g * half:(g + 1) * half]
        parts.append(yg * lax.rsqrt(jnp.mean(yg * yg, axis=-1, keepdims=True) + NORM_EPS))
    return ((jnp.concatenate(parts, axis=-1) * norm_g).astype(BF16),)


def rwkv_pre_fn(rkv, rkv_p, pg, pg_p, pwa, pwa_p, mu_rkv, mu_pg, mu_pwa, w0, w2p, a0, a2p, g2, k_k, k_a):
    w = RWKV_WIDTH
    rkv = rkv + (rkv_p - rkv) * mu_rkv
    pg = pg + (pg_p - pg) * mu_pg
    pwa = pwa + (pwa_p - pwa) * mu_pwa
    r, k, v = rkv[:, :w], rkv[:, w:2 * w], rkv[:, 2 * w:]
    pw, pa = pwa[:, :LANES], pwa[:, LANES:]
    w_log = -jax.nn.softplus(-(w0 + bdot(jnp.tanh(pw), w2p, "nn"))) - 0.5
    lw = -jnp.exp(w_log)
    iclr = jax.nn.sigmoid(a0 + bdot(pa, a2p, "nn"))
    gate = bdot(jax.nn.sigmoid(pg), g2, "nn")
    kk = k * k_k
    kap = kk * lax.rsqrt(jnp.maximum(head_sum(kk * kk), 1e-24))
    k_mod = k * (1.0 + (iclr - 1.0) * k_a)
    return r, lw, k_mod, v, kap, kap * iclr, gate


def rwkv_post_fn(ys, r, k_mod, v, gate, ln_w, ln_b, r_k):
    inv_n = 1.0 / RWKV_HEAD_DIM
    mean = head_sum(ys) * inv_n
    yc = ys - mean
    var = head_sum(yc * yc) * inv_n
    yn = yc * lax.rsqrt(var + RWKV_LN_EPS) * ln_w + ln_b
    bonus = head_sum(r * k_mod * r_k) * v
    return (((yn + bonus) * gate).astype(BF16),)


def attn_fn(q, kx, vx):
    outs = []
    for h in range(XATTN_HEADS):
        sl = slice(h * XATTN_HEAD_DIM, (h + 1) * XATTN_HEAD_DIM)
        s = bdot(q[:, sl], kx[:, sl], "nt") * (XATTN_HEAD_DIM ** -0.5)
        s = s - jnp.max(s, axis=-1, keepdims=True)
        p = jnp.exp(s)
        p = p / jnp.sum(p, axis=-1, keepdims=True)
        outs.append(bdot(p, vx[:, sl], "nn"))
    return (jnp.concatenate(outs, axis=-1).astype(BF16),)


def loss_fn(x, tgt, g):
    y = x * lax.rsqrt(jnp.mean(x * x, axis=-1, keepdims=True) + NORM_EPS) * g
    err = jnp.square(y - tgt)
    return 0.5 * jnp.sum(jnp.mean(err, axis=-1, keepdims=True), axis=0, keepdims=True)


def _tri_masks(n):
    row = lax.broadcasted_iota(jnp.int32, (n, n), 0)
    col = lax.broadcasted_iota(jnp.int32, (n, n), 1)
    return col <= row, col < row, row == col


@jax.custom_vjp
def unit_lower_inverse(a):
    c = a.shape[-1]
    eye = _tri_masks(c)[2].astype(F32)
    m = -a
    inv = eye + m
    n = 1
    while n * 2 < c:
        m = bdot(m, m, "nn")
        inv = bdot(inv, eye + m, "nn")
        n *= 2
    return inv


def _unit_lower_inverse_fwd(a):
    inv = unit_lower_inverse(a)
    return inv, inv


def _unit_lower_inverse_bwd(inv, g):
    return (-bdot(bdot(inv, g, "tn"), inv, "nt"),)


unit_lower_inverse.defvjp(_unit_lower_inverse_fwd, _unit_lower_inverse_bwd)


@jax.custom_vjp
def known_inverse(a, inv):
    return inv


known_inverse.defvjp(lambda a, inv: (inv, inv),
                     lambda inv, g: (_unit_lower_inverse_bwd(inv, g)[0], jnp.zeros_like(inv)))


def rwkv_chunk_fn(st0, r, lw, k, v, kap, b, inv=None):
    h, c = r.shape[0], r.shape[1]
    incl, strict, _ = _tri_masks(c)
    cum = exact_dot(jnp.broadcast_to(incl.astype(F32), (h, c, c)), lw, "nn", "a")
    g_in = jnp.exp(cum)
    g_prev = jnp.exp(cum - lw)
    g_inv = jnp.exp(-cum)
    g_end = jnp.exp(cum[:, c - 1:c, :] - cum)
    kap_t, k_t, b_t, r_t = kap * g_prev, k * g_inv, b * g_inv, r * g_in
    a_ub = jnp.where(strict, bdot(kap_t, b_t, "nt"), 0.0)
    a_vk = jnp.where(strict, bdot(kap_t, k_t, "nt"), 0.0)
    rhs = -(bdot(kap_t, st0, "nt") + bdot(a_vk, v, "nn"))
    inv = unit_lower_inverse(a_ub) if inv is None else known_inverse(a_ub, inv)
    u = bdot(inv, rhs, "nn")
    y = (bdot(r_t, st0, "nt")
         + bdot(jnp.where(incl, bdot(r_t, k_t, "nt"), 0.0), v, "nn")
         + bdot(jnp.where(incl, bdot(r_t, b_t, "nt"), 0.0), u, "nn"))
    st1 = jnp.exp(cum[:, c - 1:c, :]) * st0 + bdot(v, k * g_end, "tn") + bdot(u, b * g_end, "tn")
    return y, st1, inv


def ssd_chunk_fn(group, h0, xs, bm, cm, dt, a_log, d_skip):
    q, nh = xs.shape[0], SSD_HEADS_PER_GROUP
    causal, _, _ = _tri_masks(q)
    a_row = -jnp.exp(a_log)
    cs_all = exact_dot(causal.astype(F32), dt * a_row, "nn", "a")
    cs_t = cs_all.T
    lanes = range(group * nh, (group + 1) * nh)
    cs = jnp.stack([cs_all[:, hl:hl + 1] for hl in lanes])
    cs_row = jnp.stack([cs_t[hl:hl + 1, :] for hl in lanes])
    dt_h = jnp.stack([dt[:, hl:hl + 1] for hl in lanes])
    d_h = jnp.stack([d_skip[:, hl:hl + 1] for hl in lanes])
    x = _stack_lanes(xs, nh)
    h0s = _stack_rows(h0, nh)
    lmat = jnp.where(causal, jnp.exp(jnp.where(causal, cs - cs_row, 0.0)), 0.0)
    cb = bdot(cm, bm, "nt")
    xdt = x * dt_h
    cl = cs[:, q - 1:q, :]
    cm_b = jnp.broadcast_to(cm, (nh,) + cm.shape)
    bm_b = jnp.broadcast_to(bm, (nh,) + bm.shape)
    y = bdot(cb * lmat, xdt, "nn") + bdot(cm_b, h0s, "nt") * jnp.exp(cs) + x * d_h
    h1 = h0s * jnp.exp(cl) + bdot(xdt * jnp.exp(cl - cs), bm_b, "tn")
    return jnp.concatenate([y[e] for e in range(nh)], axis=-1), jnp.concatenate([h1[e] for e in range(nh)], axis=0)


class Rows:
    def __init__(self, arr, w=None, cb=0, shifts=()):
        self.arr, self.w, self.cb, self.shifts = arr, (arr.shape[1] if w is None else w), cb, tuple(shifts)


def _as_rows(x):
    return x if isinstance(x, Rows) else Rows(x)


def _shift_down(x, halo, k):
    rolled = pltpu.roll(x, k, 0)
    first = rolled[0:SUBLANES]
    rid = lax.broadcasted_iota(jnp.int32, first.shape, 0)
    patched = jnp.where(rid < k, pltpu.roll(halo, k, 0), first)
    return jnp.concatenate([patched, rolled[SUBLANES:]], axis=0)


def _shift_up(g, carry, k):
    tm = g.shape[0]
    rolled = pltpu.roll(g, tm - k, 0)
    last = rolled[tm - SUBLANES:]
    rid = lax.broadcasted_iota(jnp.int32, last.shape, 0)
    patched = jnp.where(rid >= SUBLANES - k, pltpu.roll(carry, SUBLANES - k, 0), last)
    return jnp.concatenate([rolled[:tm - SUBLANES], patched], axis=0)


def _params():
    return pltpu.CompilerParams(vmem_limit_bytes=VMEM_LIMIT)


def _load_rows(refs, pos, rins, first_block):
    vals = []
    for r in rins:
        x = refs[pos][...].astype(F32) if refs[pos].dtype != F32 else refs[pos][...]
        pos += 1
        vals.append(x)
        if r.shifts:
            halo = refs[pos][...]
            pos += 1
            halo = jnp.where(first_block, jnp.zeros_like(halo), halo)
            for k in r.shifts:
                vals.append(_shift_down(x, halo, k))
    return vals, pos


def _row_specs(rins, tm, blk):
    specs, args = [], []
    for r in rins:
        specs.append(pl.BlockSpec((tm, r.w), lambda i, cb=r.cb: (blk(i), cb)))
        args.append(r.arr)
        if r.shifts:
            per = tm // SUBLANES
            specs.append(pl.BlockSpec((SUBLANES, r.w), lambda i, cb=r.cb: (jnp.maximum(blk(i) * per - 1, 0), cb)))
            args.append(r.arr)
    return specs, args


def rowwise_fwd(name, fn, rins, params, outs, tm, deps=()):
    rins = [_as_rows(r) for r in rins]
    t = rins[0].arr.shape[0]
    tm = min(tm, t)
    nb = t // tm
    specs, args = _row_specs(rins, tm, lambda i: i)
    for p in params:
        specs.append(pl.BlockSpec(p.shape, lambda i: (0, 0)))
        args.append(p)
    for dep in deps:
        specs.append(pl.BlockSpec(memory_space=pl.ANY))
        args.append(dep)
    n_in = len(args)

    def body(*refs):
        vals, pos = _load_rows(refs, 0, rins, pl.program_id(0) == 0)
        pv = [refs[pos + j][...] for j in range(len(params))]
        res = fn(*vals, *pv)
        for o_ref, o in zip(refs[n_in:], res):
            o_ref[...] = o.astype(o_ref.dtype)

    return pl.pallas_call(
        body, name=name, grid=(nb,), in_specs=specs,
        out_specs=[pl.BlockSpec((tm, w), lambda i: (i, 0)) for w, _ in outs],
        out_shape=[jax.ShapeDtypeStruct((t, w), dt) for w, dt in outs],
        compiler_params=_params(),
    )(*args)


def rowwise_bwd(name, fn, rins, params, cts, tm, grad_dtypes, row_add=None):
    rins = [_as_rows(r) for r in rins]
    cts = [[_as_rows(c) for c in lst] for lst in cts]
    row_add = [_as_rows(a) for a in (row_add or [])]
    t = rins[0].arr.shape[0]
    tm = min(tm, t)
    nb = t // tm
    rev = lambda i: nb - 1 - i
    specs, args = _row_specs(rins, tm, rev)
    for p in params:
        specs.append(pl.BlockSpec(p.shape, lambda i: (0, 0)))
        args.append(p)
    flat_cts = [c for lst in cts for c in lst] + row_add
    for c in flat_cts:
        specs.append(pl.BlockSpec((tm, c.w), lambda i, cb=c.cb: (rev(i), cb)))
        args.append(c.arr)
    n_in = len(args)
    want = [i for i, d in enumerate(grad_dtypes) if d is not None]
    out_specs = [pl.BlockSpec((tm, rins[i].w), lambda i_: (rev(i_), 0)) for i in want]
    out_shape = [jax.ShapeDtypeStruct((t, rins[i].w), grad_dtypes[i]) for i in want]
    out_specs += [pl.BlockSpec(p.shape, lambda i: (0, 0)) for p in params]
    out_shape += [jax.ShapeDtypeStruct(p.shape, F32) for p in params]
    n_out = len(out_shape)
    scratch = [pltpu.VMEM((SUBLANES, r.w), F32) for r in rins for _ in r.shifts]

    def body(*refs):
        i = pl.program_id(0)
        vals, pos = _load_rows(refs, 0, rins, rev(i) == 0)
        pv = [refs[pos + j][...] for j in range(len(params))]
        pos += len(params)
        outs, vjp = jax.vjp(fn, *vals, *pv)
        ct_vals = []
        for o, lst in zip(outs, cts):
            acc = None
            for _ in lst:
                cv = refs[pos][...].astype(F32)
                pos += 1
                acc = cv if acc is None else acc + cv
            ct_vals.append(acc.astype(o.dtype))
        adds = [refs[pos + j][...].astype(F32) for j in range(len(row_add))]
        grads = vjp(tuple(ct_vals))
        out_refs = refs[n_in:n_in + n_out]
        carry_refs = refs[n_in + n_out:]

        @pl.when(i == 0)
        def _():
            for cr in carry_refs:
                cr[...] = jnp.zeros_like(cr)
            for pr in out_refs[len(want):]:
                pr[...] = jnp.zeros_like(pr)

        gi, ci, oi = 0, 0, 0
        for idx, r in enumerate(rins):
            d = grads[gi]
            gi += 1
            for k in r.shifts:
                dk = grads[gi]
                gi += 1
                d = d + _shift_up(dk, carry_refs[ci][...], k)
                carry_refs[ci][...] = dk[0:SUBLANES]
                ci += 1
            if idx == 0:
                for a in adds:
                    d = d + a
            if grad_dtypes[idx] is not None:
                out_refs[oi][...] = d.astype(out_refs[oi].dtype)
                oi += 1
        for pr, gp in zip(out_refs[len(want):], grads[gi:]):
            pr[...] += gp

    res = pl.pallas_call(
        body, name=name, grid=(nb,), in_specs=specs, out_specs=out_specs, out_shape=out_shape,
        scratch_shapes=scratch, compiler_params=_params(),
    )(*args)
    return res[:len(want)], res[len(want):]


def _pick(n, pref):
    for c in pref:
        if n % c == 0:
            return c
    return n


MM_VMEM_BUDGET = 40 * 1024 * 1024
MM_PEAK_FLOPS = 0.9e15
MM_HBM_BYTES_PER_S = 3.0e12
MM_STEP_SECONDS = 0.35e-6


def _mm_tiles(m, n, k, size_a, size_b, size_out, size_res, single_k):
    best = None
    for tk in sorted({c for c in (k, 2048, 1024, 512, 256, 128) if c <= 2048 and k % c == 0}, reverse=True):
        for tm in (1024, 512, 256, 128):
            if m % tm:
                continue
            for tn in (2048, 1536, 1024, 768, 512, 384, 256, 128):
                if n % tn:
                    continue
                nk = k // tk
                vmem = 2 * (tm * tk * size_a + tk * tn * size_b + tm * tn * (size_out + size_res))
                vmem += tm * tn * 4 * (2 if nk > 1 or not single_k else 1)
                vmem += (tm * tk * 2 if size_a > 2 else 0) + (tk * tn * 2 if size_b > 2 else 0)
                if vmem > MM_VMEM_BUDGET:
                    continue
                steps = (m // tm) * (n // tn) * nk
                a_reads = 1 if (nk == 1 and single_k) else n // tn
                traffic = m * k * size_a * a_reads + k * n * size_b * (m // tm) + m * n * (size_out + size_res)
                cost = max(2.0 * m * n * k / MM_PEAK_FLOPS, traffic / MM_HBM_BYTES_PER_S) + steps * MM_STEP_SECONDS
                if best is None or cost < best[0]:
                    best = (cost, tm, tn, tk)
    return best[1:]


def mm(name, a, b, mode, out_dtype=F32, res=None, b_slabs=None, out_slabs=None, dep=None, epi=None, extras=(),
       out_dtypes=None):
    if mode == "tn":
        k_dim, m_dim = a.shape
    else:
        m_dim, k_dim = a.shape
    if b_slabs:
        n_dim = b.shape[0] * b.shape[2] if mode == "nn" else b.shape[1]
    else:
        n_dim = b.shape[0] if mode == "nt" else b.shape[1]
    n_slabs = out_slabs or (b_slabs if (b_slabs and mode == "nn") else 1)
    k_slabs = b_slabs if (b_slabs and mode == "nt") else 1
    if epi is None:
        out_dtypes = [out_dtype]
        if res is None:
            epi = lambda acc: (acc,)
        else:
            extras, epi = [res], lambda acc, r: (acc + r,)
    tm, tn, tk = _mm_tiles(m_dim, n_dim // n_slabs, k_dim // k_slabs, a.dtype.itemsize, b.dtype.itemsize,
                           sum(jnp.dtype(dt).itemsize for dt in out_dtypes), sum(e.dtype.itemsize for e in extras),
                           single_k=(k_slabs == 1))
    nji = n_dim // n_slabs // tn
    nki = k_dim // k_slabs // tk
    nblk = lambda js, j: js * nji + j
    kblk = lambda ks, k: ks * nki + k
    if mode == "tn":
        a_spec = pl.BlockSpec((tk, tm), lambda i, js, j, ks, k: (kblk(ks, k), i))
    else:
        a_spec = pl.BlockSpec((tm, tk), lambda i, js, j, ks, k: (i, kblk(ks, k)))
    if b_slabs and mode == "nn":
        b_spec = pl.BlockSpec((None, tk, tn), lambda i, js, j, ks, k: (js, k, j))
    elif b_slabs and mode == "nt":
        b_spec = pl.BlockSpec((None, tn, tk), lambda i, js, j, ks, k: (ks, nblk(js, j), k))
    elif mode == "nt":
        b_spec = pl.BlockSpec((tn, tk), lambda i, js, j, ks, k: (nblk(js, j), kblk(ks, k)))
    else:
        b_spec = pl.BlockSpec((tk, tn), lambda i, js, j, ks, k: (kblk(ks, k), nblk(js, j)))
    specs, args = [a_spec, b_spec], [a, b]
    for e in extras:
        specs.append(pl.BlockSpec((tm, tn), lambda i, js, j, ks, k: (i, nblk(js, j))))
        args.append(e)
    if dep is not None:
        specs.append(pl.BlockSpec(memory_space=pl.ANY))
        args.append(dep)
    if out_slabs:
        o_specs = [pl.BlockSpec((None, tm, tn), lambda i, js, j, ks, k: (js, i, j))]
        o_shapes = [jax.ShapeDtypeStruct((out_slabs, m_dim, n_dim // out_slabs), out_dtypes[0])]
    else:
        o_specs = [pl.BlockSpec((tm, tn), lambda i, js, j, ks, k: (i, nblk(js, j))) for _ in out_dtypes]
        o_shapes = [jax.ShapeDtypeStruct((m_dim, n_dim), dt) for dt in out_dtypes]

    one_k_step = k_slabs * nki == 1
    n_in, n_out = len(args), len(out_dtypes)

    def body(*refs):
        a_ref, b_ref = refs[0], refs[1]
        part = _dg(a_ref[...].astype(BF16), b_ref[...].astype(BF16), mode)

        def finish(acc):
            outs = epi(acc, *[refs[2 + j][...].astype(F32) for j in range(len(extras))])
            for o_ref, o in zip(refs[n_in:n_in + n_out], outs):
                o_ref[...] = o.astype(o_ref.dtype)

        if one_k_step:
            finish(part)
            return
        acc_ref = refs[n_in + n_out]
        ks, kk = pl.program_id(3), pl.program_id(4)

        @pl.when((ks == 0) & (kk == 0))
        def _():
            acc_ref[...] = part

        @pl.when((ks > 0) | (kk > 0))
        def _():
            acc_ref[...] += part

        pl.when((ks == k_slabs - 1) & (kk == nki - 1))(lambda: finish(acc_ref[...]))

    grid = (m_dim // tm, n_slabs, nji, k_slabs, nki)
    scratch = [] if one_k_step else [pltpu.VMEM((tm, tn), F32)]
    out = pl.pallas_call(
        body, name=name, grid=grid, in_specs=specs, out_specs=o_specs, out_shape=o_shapes, scratch_shapes=scratch,
        compiler_params=pltpu.CompilerParams(
            dimension_semantics=("parallel", "parallel", "parallel", "arbitrary", "arbitrary"),
            vmem_limit_bytes=VMEM_LIMIT),
    )(*args)
    return out[0] if n_out == 1 else out


def _stack_lanes(x, n):
    w = x.shape[1] // n
    return jnp.stack([x[:, i * w:(i + 1) * w] for i in range(n)])


def _stack_rows(x, n):
    w = x.shape[0] // n
    return jnp.stack([x[i * w:(i + 1) * w, :] for i in range(n)])


def rwkv_scan_fwd(r, lw, k, v, kap, b):
    t = r.shape[0]
    c, hps, hd = min(RWKV_CHUNK, t), RWKV_HEADS_PER_STEP, RWKV_HEAD_DIM
    nc, ng, wl = t // c, RWKV_HEADS // hps, hps * hd
    spec = pl.BlockSpec((c, wl), lambda g, ci: (ci, g))

    def body(r_ref, lw_ref, k_ref, v_ref, kap_ref, b_ref, y_ref, ck_ref, inv_ref, st_ref):
        @pl.when(pl.program_id(1) == 0)
        def _():
            st_ref[...] = jnp.zeros_like(st_ref)

        st = st_ref[...]
        ck_ref[...] = st
        ins = [x[...] for x in (r_ref, lw_ref, k_ref, v_ref, kap_ref, b_ref)]
        y, st1, inv = rwkv_chunk_fn(_stack_rows(st, hps), *[_stack_lanes(x, hps) for x in ins])
        y_ref[...] = jnp.concatenate([y[h] for h in range(hps)], axis=-1)
        st_ref[...] = jnp.concatenate([st1[h] for h in range(hps)], axis=0)
        inv_ref[...] = jnp.concatenate([inv[h] for h in range(hps)], axis=0)

    return pl.pallas_call(
        body, name="rwkv_scan_fwd", grid=(ng, nc), in_specs=[spec] * 6,
        out_specs=[spec, pl.BlockSpec((None, wl, hd), lambda g, ci: (ci, g, 0)),
                   pl.BlockSpec((None, hps * c, c), lambda g, ci: (ci, g, 0))],
        out_shape=[jax.ShapeDtypeStruct((t, RWKV_WIDTH), F32), jax.ShapeDtypeStruct((nc, RWKV_WIDTH, hd), F32),
                   jax.ShapeDtypeStruct((nc, RWKV_HEADS * c, c), F32)],
        scratch_shapes=[pltpu.VMEM((wl, hd), F32)], compiler_params=_params(),
    )(r, lw, k, v, kap, b)


def rwkv_scan_bwd(r, lw, k, v, kap, b, ck, inv_ck, dy):
    t = r.shape[0]
    c, hps, hd = min(RWKV_CHUNK, t), RWKV_HEADS_PER_STEP, RWKV_HEAD_DIM
    nc, ng, wl = t // c, RWKV_HEADS // hps, hps * hd
    spec = pl.BlockSpec((c, wl), lambda g, ci: (nc - 1 - ci, g))

    def body(r_ref, lw_ref, k_ref, v_ref, kap_ref, b_ref, ck_ref, inv_ref, dy_ref, *rest):
        out_refs, dst_ref = rest[:6], rest[6]

        @pl.when(pl.program_id(1) == 0)
        def _():
            dst_ref[...] = jnp.zeros_like(dst_ref)

        ins = [x[...] for x in (r_ref, lw_ref, k_ref, v_ref, kap_ref, b_ref)]
        dyv, ck, dst = dy_ref[...].astype(F32), ck_ref[...], dst_ref[...]
        chunk = lambda *a: rwkv_chunk_fn(*a, inv=_stack_rows(inv_ref[...], hps))[:2]
        _, vjp = jax.vjp(chunk, _stack_rows(ck, hps), *[_stack_lanes(x, hps) for x in ins])
        grads = vjp((_stack_lanes(dyv, hps), _stack_rows(dst, hps)))
        dst_ref[...] = jnp.concatenate([grads[0][h] for h in range(hps)], axis=0)
        for j in range(6):
            out_refs[j][...] = jnp.concatenate([grads[1 + j][h] for h in range(hps)], axis=-1).astype(BF16)

    return pl.pallas_call(
        body, name="rwkv_scan_bwd", grid=(ng, nc),
        in_specs=[spec] * 6 + [pl.BlockSpec((None, wl, hd), lambda g, ci: (nc - 1 - ci, g, 0)),
                               pl.BlockSpec((None, hps * c, c), lambda g, ci: (nc - 1 - ci, g, 0)), spec],
        out_specs=[spec] * 6, out_shape=[jax.ShapeDtypeStruct((t, RWKV_WIDTH), BF16)] * 6,
        scratch_shapes=[pltpu.VMEM((wl, hd), F32)], compiler_params=_params(),
    )(r, lw, k, v, kap, b, ck, inv_ck, dy)


def _ssd_specs(q, blk):
    gw = SSD_WIDTH // 2
    return [pl.BlockSpec((q, gw), lambda g, ci: (blk(ci), g)),
            pl.BlockSpec((q, SSD_STATE), lambda g, ci: (blk(ci), g)),
            pl.BlockSpec((q, SSD_STATE), lambda g, ci: (blk(ci), g)),
            pl.BlockSpec((q, LANES), lambda g, ci: (blk(ci), 0)),
            pl.BlockSpec((1, LANES), lambda g, ci: (0, 0)),
            pl.BlockSpec((1, LANES), lambda g, ci: (0, 0))]


def ssd_scan_fwd(xs, bm, cm, dt, a_log, d_skip):
    t = xs.shape[0]
    q = min(SSD_CHUNK, t)
    nc, gw = t // q, SSD_WIDTH // 2

    def body(xs_ref, bm_ref, cm_ref, dt_ref, al_ref, d_ref, y_ref, ck_ref, h_ref):
        @pl.when(pl.program_id(1) == 0)
        def _():
            h_ref[...] = jnp.zeros_like(h_ref)

        ck_ref[...] = h_ref[...]
        args = (h_ref[...], xs_ref[...], bm_ref[...], cm_ref[...], dt_ref[...], al_ref[...], d_ref[...])
        g = pl.program_id(0)

        @pl.when(g == 0)
        def _():
            y, h1 = ssd_chunk_fn(0, *args)
            y_ref[...] = y
            h_ref[...] = h1

        @pl.when(g == 1)
        def _():
            y, h1 = ssd_chunk_fn(1, *args)
            y_ref[...] = y
            h_ref[...] = h1

    return pl.pallas_call(
        body, name="ssd_scan_fwd", grid=(2, nc), in_specs=_ssd_specs(q, lambda ci: ci),
        out_specs=[pl.BlockSpec((q, gw), lambda g, ci: (ci, g)),
                   pl.BlockSpec((None, gw, SSD_STATE), lambda g, ci: (ci, g, 0))],
        out_shape=[jax.ShapeDtypeStruct((t, SSD_WIDTH), F32), jax.ShapeDtypeStruct((nc, SSD_WIDTH, SSD_STATE), F32)],
        scratch_shapes=[pltpu.VMEM((gw, SSD_STATE), F32)], compiler_params=_params(),
    )(xs, bm, cm, dt, a_log, d_skip)


def ssd_scan_bwd(xs, bm, cm, dt, a_log, d_skip, ck, dy):
    t = xs.shape[0]
    q = min(SSD_CHUNK, t)
    nc, gw = t // q, SSD_WIDTH // 2
    rev = lambda ci: nc - 1 - ci

    def body(xs_ref, bm_ref, cm_ref, dt_ref, al_ref, d_ref, ck_ref, dy_ref,
             dxs_ref, dbm_ref, dcm_ref, ddt_ref, dal_ref, dd_ref, dh_ref):
        g, ci = pl.program_id(0), pl.program_id(1)

        @pl.when(ci == 0)
        def _():
            dh_ref[...] = jnp.zeros_like(dh_ref)

        @pl.when((ci == 0) & (g == 0))
        def _():
            dal_ref[...] = jnp.zeros_like(dal_ref)
            dd_ref[...] = jnp.zeros_like(dd_ref)

        args = (ck_ref[...], xs_ref[...], bm_ref[...], cm_ref[...], dt_ref[...], al_ref[...], d_ref[...])

        def run(group):
            _, vjp = jax.vjp(functools.partial(ssd_chunk_fn, group), *args)
            dh0, dxs, dbm, dcm, ddt, dal, dd = vjp((dy_ref[...].astype(F32), dh_ref[...]))
            dh_ref[...] = dh0
            dxs_ref[...] = dxs.astype(BF16)
            dbm_ref[...] = dbm.astype(BF16)
            dcm_ref[...] = dcm.astype(BF16)
            ddt_ref[...] = ddt
            dal_ref[...] += dal
            dd_ref[...] += dd

        pl.when(g == 0)(lambda: run(0))
        pl.when(g == 1)(lambda: run(1))

    in_specs = _ssd_specs(q, rev) + [pl.BlockSpec((None, gw, SSD_STATE), lambda g, ci: (rev(ci), g, 0)),
                                     pl.BlockSpec((q, gw), lambda g, ci: (rev(ci), g))]
    return pl.pallas_call(
        body, name="ssd_scan_bwd", grid=(2, nc), in_specs=in_specs,
        out_specs=[pl.BlockSpec((q, gw), lambda g, ci: (rev(ci), g)),
                   pl.BlockSpec((q, SSD_STATE), lambda g, ci: (rev(ci), g)),
                   pl.BlockSpec((q, SSD_STATE), lambda g, ci: (rev(ci), g)),
                   pl.BlockSpec((None, q, LANES), lambda g, ci: (g, rev(ci), 0)),
                   pl.BlockSpec((1, LANES), lambda g, ci: (0, 0)),
                   pl.BlockSpec((1, LANES), lambda g, ci: (0, 0))],
        out_shape=[jax.ShapeDtypeStruct((t, SSD_WIDTH), BF16), jax.ShapeDtypeStruct((t, 2 * SSD_STATE), BF16),
                   jax.ShapeDtypeStruct((t, 2 * SSD_STATE), BF16), jax.ShapeDtypeStruct((2, t, LANES), F32),
                   jax.ShapeDtypeStruct((1, LANES), F32), jax.ShapeDtypeStruct((1, LANES), F32)],
        scratch_shapes=[pltpu.VMEM((gw, SSD_STATE), F32)], compiler_params=_params(),
    )(xs, bm, cm, dt, a_log, d_skip, ck, dy)


def loss_and_grad(x, tgt, g, tm):
    t, d = x.shape
    tm = min(tm, t)
    nb = t // tm

    def body(x_ref, t_ref, g_ref, loss_ref, dx_ref, dxb_ref, dg_ref):
        @pl.when(pl.program_id(0) == 0)
        def _():
            loss_ref[...] = jnp.zeros_like(loss_ref)
            dg_ref[...] = jnp.zeros_like(dg_ref)

        val, vjp = jax.vjp(loss_fn, x_ref[...], t_ref[...], g_ref[...])
        dx, _, dg = vjp(jnp.ones((1, 1), F32))
        loss_ref[...] += jnp.broadcast_to(val, loss_ref.shape)
        dx_ref[...] = dx
        dxb_ref[...] = dx.astype(BF16)
        dg_ref[...] += dg

    row = pl.BlockSpec((tm, d), lambda i: (i, 0))
    one = pl.BlockSpec((1, d), lambda i: (0, 0))
    return pl.pallas_call(
        body, name="loss_and_grad", grid=(nb,), in_specs=[row, row, one],
        out_specs=[pl.BlockSpec((SUBLANES, LANES), lambda i: (0, 0)), row, row, one],
        out_shape=[jax.ShapeDtypeStruct((SUBLANES, LANES), F32), jax.ShapeDtypeStruct((t, d), F32),
                   jax.ShapeDtypeStruct((t, d), BF16), jax.ShapeDtypeStruct((1, d), F32)],
        compiler_params=_params(),
    )(x, tgt, g)


def adamw(name, recv, w, m, v, dep=None):
    rows, cols = w.shape
    n_slabs = recv.shape[0]
    recv_block_bytes = 4 * 1024 * 1024
    tm = _pick(rows, [c for c in (256, 128, 64, 32, 16, 8) if n_slabs * c * cols * 4 <= recv_block_bytes])
    c1 = 1.0 / (1.0 - ADAM_B1 ** ADAM_STEP)
    c2 = 1.0 / (1.0 - ADAM_B2 ** ADAM_STEP)

    n_dep = 0 if dep is None else 1

    def body(recv_ref, w_ref, m_ref, v_ref, *rest):
        g_ref, d_ref, nm_ref, nv_ref = rest[n_dep:]
        g = recv_ref[0].astype(F32)
        for p in range(1, n_slabs):
            g = g + recv_ref[p].astype(F32)
        nm =ADAM_B1 * m_ref[...] + (1.0 - ADAM_B1) * g
        nv = ADAM_B2 * v_ref[...] + (1.0 - ADAM_B2) * jnp.square(g)
        g_ref[...] = g
        nm_ref[...] = nm
        nv_ref[...] = nv
        d_ref[...] = -ADAM_LR * ((nm * c1) / (jnp.sqrt(nv * c2) + ADAM_EPS) + ADAM_WD * w_ref[...])

    blk = pl.BlockSpec((tm, cols), lambda i: (i, 0))
    return pl.pallas_call(
        body, name=name, grid=(rows // tm,),
        in_specs=[pl.BlockSpec((n_slabs, tm, cols), lambda i: (0, i, 0)), blk, blk, blk]
        + [pl.BlockSpec(memory_space=pl.ANY)] * n_dep,
        out_specs=[blk] * 4, out_shape=[jax.ShapeDtypeStruct((rows, cols), F32)] * 4,
        compiler_params=_params(),
    )(recv, w, m, v, *([] if dep is None else [dep]))


def _mesh_pos():
    return lax.axis_index("x"), lax.axis_index("y"), lax.axis_index("c")


def _peer(pos, mask):
    x, y, c = pos
    return (1 - x if mask & 4 else x, 1 - y if mask & 2 else y, 1 - c if mask & 1 else c)


def _linear(pos):
    return 4 * pos[0] + 2 * pos[1] + pos[2]


class Exchange:
    MASKS = {"gather": (1, 2, 3, 4, 5, 6, 7), "scatter": (1, 2, 3, 4, 5, 6, 7), "gather_chips": (1, 2, 4, 6),
             "forward": (2, 4, 6)}

    def __init__(self, xs, kind, lands=None):
        self.kind, self.masks = kind, self.MASKS[kind]
        self.xs = [] if kind == "forward" else list(xs)
        if kind == "forward":
            self.land_shape = [jax.ShapeDtypeStruct(l.shape, l.dtype) for l in lands]
        elif kind == "scatter":
            self.land_shape = [jax.ShapeDtypeStruct(x.shape, x.dtype) for x in xs]
        else:
            self.land_shape = [jax.ShapeDtypeStruct((N_DEV,) + x.shape, x.dtype) for x in xs]
        self.n = len(self.land_shape)
        copies = self.n * len(self.masks)
        self.sems = [pltpu.SemaphoreType.DMA((copies,)), pltpu.SemaphoreType.DMA((copies,)),
                     pltpu.SemaphoreType.DMA((self.n,))]

    def _copies(self, ins, outs, sems, landing):
        send_sems, recv_sems, local_sems = sems
        me = _mesh_pos()
        me_lin = _linear(me)
        local, remote = [], []
        for ti in range(self.n):
            if self.kind != "forward":
                src_mine = ins[ti].at[me_lin] if self.kind == "scatter" else ins[ti]
                local.append(pltpu.make_async_copy(src_mine, outs[ti].at[me_lin], local_sems.at[ti]))
            for j, mask in enumerate(self.masks):
                if self.kind == "forward":
                    peer = _peer(me, 1)
                    src = outs[ti].at[_linear(_peer(me, mask))]
                    dst = outs[ti].at[_linear(_peer(me, mask ^ 1 if landing else mask))]
                else:
                    peer = _peer(me, mask)
                    src = ins[ti].at[_linear(peer)] if self.kind == "scatter" else ins[ti]
                    dst = outs[ti].at[_linear(peer) if landing else me_lin]
                sem_index = ti * len(self.masks) + j
                remote.append(pltpu.make_async_remote_copy(
                    src_ref=src, dst_ref=dst, send_sem=send_sems.at[sem_index], recv_sem=recv_sems.at[sem_index],
                    device_id=peer, device_id_type=pl.DeviceIdType.MESH))
        return local, remote

    def start(self, ins, outs, sems):
        local, remote = self._copies(ins, outs, sems, landing=False)
        for cp in local + remote:
            cp.start()

    def finish(self, ins, outs, sems):
        local, remote = self._copies(ins, outs, sems, landing=True)
        for cp in remote:
            cp.wait_recv()
        for cp in remote:
            cp.wait_send()
        for cp in local:
            cp.wait()


def exchange_start(name, xs, kind, dep=None, lands=None):
    ex = Exchange(xs, kind, lands)
    hbm = pl.BlockSpec(memory_space=pltpu.HBM)
    sem = pl.BlockSpec(memory_space=pltpu.SEMAPHORE)
    if lands is None:
        lands = [lax.empty(s.shape, s.dtype) for s in ex.land_shape]
    n_src, n = len(ex.xs), ex.n
    n_inputs = n_src + n + (0 if dep is None else 1)

    def body(*refs):
        ins, lnd, sems, token = refs[:n_src], refs[n_src:n_src + n], refs[n_inputs:n_inputs + 3], refs[-1]
        ex.start(ins, lnd, sems)
        token[...] = jnp.zeros_like(token)

    res = pl.pallas_call(
        body, name=name, in_specs=[hbm] * (n_src + n) + ([] if dep is None else [pl.BlockSpec(memory_space=pl.ANY)]),
        out_specs=[sem] * 3 + [hbm] * (n_src + n) + [pl.BlockSpec(memory_space=pltpu.VMEM)],
        out_shape=ex.sems + [pltpu.HBM(x.shape, x.dtype) for x in ex.xs]
        + [pltpu.HBM(s.shape, s.dtype) for s in ex.land_shape] + [jax.ShapeDtypeStruct((SUBLANES, LANES), F32)],
        input_output_aliases={i: 3 + i for i in range(n_src + n)},
        compiler_params=pltpu.CompilerParams(has_side_effects=pltpu.SideEffectType.DATAFLOW_SIDE_EFFECTING),
    )(*[pltpu.with_memory_space_constraint(x, pltpu.HBM) for x in ex.xs + list(lands)],
      *([] if dep is None else [dep]))
    return (ex, res[:3], res[3:3 + n_src], res[3 + n_src:3 + n_src + n]), res[-1]


def exchange_wait(name, handles, after):
    ex, sems, srcs, lands = handles
    n_src, n = len(srcs), len(lands)
    hbm = pl.BlockSpec(memory_space=pltpu.HBM)
    sem = pl.BlockSpec(memory_space=pltpu.SEMAPHORE)

    def body(*refs):
        ins, lnd, sem_refs = refs[:n_src], refs[n_src:n_src + n], refs[n_src + n:n_src + n + 3]
        ex.finish(ins, lnd, sem_refs)

    res = pl.pallas_call(
        body, name=name, in_specs=[hbm] * (n_src + n) + [sem] * 3 + [pl.BlockSpec(memory_space=pl.ANY)],
        out_specs=[hbm] * (n_src + n),
        out_shape=[pltpu.HBM(x.shape, x.dtype) for x in srcs] + [pltpu.HBM(x.shape, x.dtype) for x in lands],
        input_output_aliases={i: i for i in range(n_src + n)},
        compiler_params=pltpu.CompilerParams(has_side_effects=pltpu.SideEffectType.DATAFLOW_SIDE_EFFECTING),
    )(*srcs, *lands, *sems, after)
    return res[n_src:]


def forward_start(name, chip_gather, after):
    lands = exchange_wait(name + "_wait", chip_gather, after)
    return exchange_start(name + "_forward_start", [], "forward", lands=lands)


_Z = (0, 1024)
_XBC = (1024, 2560)
_DT = (2560, 2576)
_RKV = (2576, 5648)
_PW = (5648, 5744)
_PA = (5744, 5840)
_PG = (5840, 6096)
D_IN = 6096

_SMALL = ("norm_mix_g", "ssd_conv_b", "ssd_dt_bias", "ssd_a_log", "ssd_d", "ssd_norm_g", "rwkv_mu", "rwkv_w0",
          "rwkv_a0", "rwkv_k_k", "rwkv_k_a", "rwkv_r_k", "rwkv_ln_w", "rwkv_ln_b", "norm_x_g", "norm_mem_g",
          "norm_ffn_g", "final_norm_g")
_WEIGHTS = ("norm_mix_g", "w_in", "ssd_conv_w", "ssd_conv_b", "ssd_dt_bias", "ssd_a_log", "ssd_d", "ssd_norm_g",
            "rwkv_mu", "rwkv_w0", "rwkv_w2", "rwkv_a0", "rwkv_a2", "rwkv_g2", "rwkv_k_k", "rwkv_k_a", "rwkv_r_k",
            "rwkv_ln_w", "rwkv_ln_b", "w_out", "norm_x_g", "norm_mem_g", "xattn_wq", "xattn_wk", "xattn_wv",
            "xattn_wo", "norm_ffn_g", "ffn_w1", "ffn_w2", "final_norm_g")


def _pad_lanes(x, width=LANES):
    return jnp.pad(x, ((0, 0), (0, width - x.shape[1])))


def _pack_small(vals):
    flat = jnp.concatenate([vals[n].reshape(-1) for n in _SMALL])
    rows = -(-flat.shape[0] // (LANES * SUBLANES)) * SUBLANES
    return jnp.pad(flat, (0, rows * LANES - flat.shape[0])).reshape(rows, LANES)


def _unpack_small(packed, shapes):
    flat = packed.reshape(-1)
    out, pos = {}, 0
    for n in _SMALL:
        size = 1
        for s in shapes[n]:
            size *= s
        out[n] = flat[pos:pos + size].reshape(shapes[n])
        pos += size
    return out


def _rows(w, rng):
    return w[rng[0]:rng[1]]


def sum_slabs(name, recv):
    n, rows, cols = recv.shape
    tc = _pick(cols, (256, 128))

    def body(r_ref, o_ref):
        acc = r_ref[0].astype(F32)
        for p in range(1, n):
            acc = acc + r_ref[p].astype(F32)
        o_ref[...] = acc

    return pl.pallas_call(
        body, name=name, grid=(cols // tc,), in_specs=[pl.BlockSpec((n, rows, tc), lambda j: (0, 0, j))],
        out_specs=pl.BlockSpec((rows, tc), lambda j: (0, j)), out_shape=jax.ShapeDtypeStruct((rows, cols), F32),
        compiler_params=_params(),
    )(recv)


def kernel(x, mem, norm_mix_g, w_in, ssd_conv_w, ssd_conv_b, ssd_dt_bias, ssd_a_log, ssd_d, ssd_norm_g, rwkv_mu, rwkv_w0, rwkv_w2, rwkv_a0, rwkv_a2, rwkv_g2, rwkv_k_k, rwkv_k_a, rwkv_r_k, rwkv_ln_w, rwkv_ln_b, w_out, norm_x_g, norm_mem_g, xattn_wq, xattn_wk, xattn_wv, xattn_wo, norm_ffn_g, ffn_w1, ffn_w2, final_norm_g, loss_target, m_norm_mix_g, m_w_in, m_ssd_conv_w, m_ssd_conv_b, m_ssd_dt_bias, m_ssd_a_log, m_ssd_d, m_ssd_norm_g, m_rwkv_mu, m_rwkv_w0, m_rwkv_w2, m_rwkv_a0, m_rwkv_a2, m_rwkv_g2, m_rwkv_k_k, m_rwkv_k_a, m_rwkv_r_k, m_rwkv_ln_w, m_rwkv_ln_b, m_w_out, m_norm_x_g, m_norm_mem_g, m_xattn_wq, m_xattn_wk, m_xattn_wv, m_xattn_wo, m_norm_ffn_g, m_ffn_w1, m_ffn_w2, m_final_norm_g, v_norm_mix_g, v_w_in, v_ssd_conv_w, v_ssd_conv_b, v_ssd_dt_bias, v_ssd_a_log, v_ssd_d, v_ssd_norm_g, v_rwkv_mu, v_rwkv_w0, v_rwkv_w2, v_rwkv_a0, v_rwkv_a2, v_rwkv_g2, v_rwkv_k_k, v_rwkv_k_a, v_rwkv_r_k, v_rwkv_ln_w, v_rwkv_ln_b, v_w_out, v_norm_x_g, v_norm_mem_g, v_xattn_wq, v_xattn_wk, v_xattn_wv, v_xattn_wo, v_norm_ffn_g, v_ffn_w1, v_ffn_w2, v_final_norm_g):
    given = dict(locals())
    wts = {n: given[n] for n in _WEIGHTS}
    mom_m = {n: given["m_" + n] for n in _WEIGHTS}
    mom_v = {n: given["v_" + n] for n in _WEIGHTS}
    d = D_MODEL
    xt, memt, tgt = x[0], mem[0], loss_target[0]
    tm = 256
    tm_rwkv = 128

    big = {"w_in": jnp.transpose(w_in[0]), "w_out": w_out[0], "xattn_wq": xattn_wq[0], "xattn_wk": xattn_wk[0],
           "xattn_wv": xattn_wv[0], "xattn_wo": xattn_wo[0], "ffn_w1": ffn_w1[0], "ffn_w2": ffn_w2[0]}
    small_sh = {"ssd_conv_w": ssd_conv_w.reshape(4, -1), "rwkv_w2": rwkv_w2[0], "rwkv_a2": rwkv_a2[0],
                "rwkv_g2": rwkv_g2[0]}
    cast_one = lambda n, deps=(): rowwise_fwd("cast_" + n, cast_fn, [big[n]], [], [(big[n].shape[1], BF16)],
                                              256 if big[n].shape[0] % 256 == 0 else big[n].shape[0], deps=deps)[0]
    gather_in, token_in = exchange_start("gather_in_start", [cast_one("w_in")] + list(small_sh.values()), "gather_chips")
    cast = {n: cast_one(n, deps=[token_in]) for n in big if n != "w_in"}
    late_a = ("w_out", "xattn_wq", "xattn_wk", "xattn_wv", "xattn_wo")
    late_b = ("ffn_w1", "ffn_w2")
    gather_a, token_a = exchange_start("gather_attn_start", [cast[n] for n in late_a], "gather_chips", dep=token_in)
    gather_b, token_b = exchange_start("gather_ffn_start", [cast[n] for n in late_b], "gather_chips", dep=token_a)
    (h1,) = rowwise_fwd("norm_mix", rmsnorm_fn, [xt], [norm_mix_g], [(d, BF16)], tm, deps=[token_b])
    forward_in, token_in = forward_start("gather_in", gather_in, after=h1)
    gathered = exchange_wait("gather_in_forward_wait", forward_in, after=token_in)
    g_big = {"w_in": gathered[0]}
    g_small = dict(zip(small_sh, gathered[1:]))

    pad_rows = lambda a: jnp.pad(a, ((0, LANES - a.shape[0]), (0, 0)))
    w_in_t = g_big["w_in"].reshape(D_IN, d)
    wt_z, wt_xbc, wt_rkv, wt_pg = (_rows(w_in_t, r) for r in (_Z, _XBC, _RKV, _PG))
    wt_sm = jnp.concatenate([pad_rows(_rows(w_in_t, r)) for r in (_PW, _PA, _DT)], axis=0)
    unshard_cols = lambda g: jnp.transpose(g, (1, 0, 2)).reshape(g.shape[1], -1)
    conv_w_f = unshard_cols(g_small["ssd_conv_w"])
    w2p, a2p = pad_rows(unshard_cols(g_small["rwkv_w2"])), pad_rows(unshard_cols(g_small["rwkv_a2"]))
    g2_f = unshard_cols(g_small["rwkv_g2"])

    mu = rwkv_mu
    mu_rkv, mu_pg = mu[:, :3072], mu[:, 3264:3520]
    mu_pwa = jnp.concatenate([_pad_lanes(mu[:, 3072:3168]), _pad_lanes(mu[:, 3168:3264])], axis=1)
    dt_bias_p, a_log_p, d_p = _pad_lanes(ssd_dt_bias), _pad_lanes(ssd_a_log), _pad_lanes(ssd_d)
    r_k_row = rwkv_r_k.reshape(1, RWKV_WIDTH)
    g_final = final_norm_g.reshape(1, d)

    u_z = mm("in_z", h1, wt_z, "nt")
    u_xbc = mm("in_xbc", h1, wt_xbc, "nt")
    u_rkv = mm("in_rkv", h1, wt_rkv, "nt")
    u_pg = mm("in_pg", h1, wt_pg, "nt")
    u_sm = mm("in_small", h1, wt_sm, "nt")

    ssd_pre_rows = lambda: [Rows(u_xbc, shifts=(1, 2, 3)), Rows(u_sm, LANES, 2)]
    ssd_pre_params = [conv_w_f, ssd_conv_b, dt_bias_p]
    xs, bm, cm, dt = rowwise_fwd("ssd_pre", ssd_pre_fn, ssd_pre_rows(), ssd_pre_params,
                                 [(SSD_WIDTH, F32), (256, F32), (256, F32), (LANES, F32)], tm)
    y_scan, ssd_ck = ssd_scan_fwd(xs, bm, cm, dt, a_log_p, d_p)
    (y_ssd,) = rowwise_fwd("ssd_post", ssd_post_fn, [y_scan, u_z], [ssd_norm_g], [(SSD_WIDTH, BF16)], tm)

    rwkv_pre_rows = lambda: [Rows(u_rkv, shifts=(1,)), Rows(u_pg, shifts=(1,)), Rows(u_sm, 2 * LANES, 0, shifts=(1,))]
    rwkv_pre_params = [mu_rkv, mu_pg, mu_pwa, rwkv_w0, w2p, rwkv_a0, a2p, g2_f, rwkv_k_k, rwkv_k_a]
    forward_a, token_a = forward_start("gather_attn", gather_a, after=y_ssd)
    r_, lw_, k_, v_, kap_, b_, gate_ = rowwise_fwd("rwkv_pre", rwkv_pre_fn, rwkv_pre_rows(), rwkv_pre_params,
                                                   [(RWKV_WIDTH, F32)] * 7, tm_rwkv, deps=[token_a])
    ys_r, rwkv_ck, rwkv_inv = rwkv_scan_fwd(r_, lw_, k_, v_, kap_, b_)
    forward_b, token_b = forward_start("gather_ffn", gather_b, after=ys_r)
    g_big.update(zip(late_a, exchange_wait("gather_attn_forward_wait", forward_a, after=token_b)))
    w_out_f = g_big["w_out"].reshape(d, d)
    wq_f, wk_f, wv_f, wo_f = (g_big[n].reshape(d, d) for n in ("xattn_wq", "xattn_wk", "xattn_wv", "xattn_wo"))
    rwkv_post_params = [rwkv_ln_w, rwkv_ln_b, r_k_row]
    (y_rwkv,) = rowwise_fwd("rwkv_post", rwkv_post_fn, [ys_r, r_, k_, v_, gate_], rwkv_post_params,
                            [(RWKV_WIDTH, BF16)], tm)
    ycat = jnp.concatenate([y_ssd, y_rwkv], axis=1)
    x1 = mm("out_proj", ycat, w_out_f, "nn", res=xt)

    (h2,) = rowwise_fwd("norm_x", rmsnorm_fn, [x1], [norm_x_g], [(d, BF16)], tm)
    (mn,) = rowwise_fwd("norm_mem", rmsnorm_fn, [memt], [norm_mem_g], [(d, BF16)], tm)
    q = mm("xattn_q", h2, wq_f, "nn", out_dtype=BF16)
    kx = mm("xattn_k", mn, wk_f, "nn")
    vx = mm("xattn_v", mn, wv_f, "nn")
    (o,) = rowwise_fwd("xattn", attn_fn, [q], [kx, vx], [(d, BF16)], tm)
    x2 = mm("xattn_o", o, wo_f, "nn", res=x1)

    (h3,) = rowwise_fwd("norm_ffn", rmsnorm_fn, [x2], [norm_ffn_g], [(d, BF16)], tm)
    w1_s, w2_g = exchange_wait("gather_ffn_forward_wait", forward_b, after=h3)
    w2_f = w2_g.reshape(D_FF, d)
    relu2_epi = lambda acc: (jnp.square(jnp.maximum(acc, 0.0)), jnp.maximum(acc, 0.0))
    hid, relu_a = mm("ffn_1", h3, w1_s, "nn", b_slabs=N_DEV, epi=relu2_epi, out_dtypes=[BF16, BF16])
    x3 = mm("ffn_2", hid, w2_f, "nn", res=x2)

    loss_blk, dx3, dx3_b, dg_final = loss_and_grad(x3, tgt, g_final, tm)

    grads = {}
    grads["ffn_w2"] = mm("d_ffn_w2", hid, dx3_b, "tn", out_dtype=BF16).reshape(N_DEV, D_FF // N_DEV, d)
    sc_w2, tok = exchange_start("scatter_ffn_w2_start", [grads["ffn_w2"]], "scatter")
    da = mm("d_hid", dx3_b, w2_f, "nt", dep=tok, epi=lambda acc, ra: (2.0 * acc * ra,), extras=[relu_a],
            out_dtypes=[BF16])
    grads["ffn_w1"] = mm("d_ffn_w1", h3, da, "tn", out_dtype=BF16, out_slabs=N_DEV)
    sc_w1, tok = exchange_start("scatter_ffn_w1_start", [grads["ffn_w1"]], "scatter")
    dh3 = mm("d_h3", da, w1_s, "nt", out_dtype=BF16, b_slabs=N_DEV, dep=tok)
    (dx2,), (dg_ffn,) = rowwise_bwd("norm_ffn_bwd", rmsnorm_fn, [x2], [norm_ffn_g], [[dh3]], tm, [F32], row_add=[dx3])

    grads["xattn_wo"] = mm("d_wo", o, dx2, "tn", out_dtype=BF16).reshape(N_DEV, d // N_DEV, d)
    sc_wo, tok = exchange_start("scatter_wo_start", [grads["xattn_wo"]], "scatter")
    d_o = mm("d_o", dx2, wo_f, "nt", out_dtype=BF16, dep=tok)
    (dq,), (dkx, dvx) = rowwise_bwd("xattn_bwd", attn_fn, [q], [kx, vx], [[d_o]], tm, [BF16])
    grads["xattn_wq"] = mm("d_wq", h2, dq, "tn", out_dtype=BF16).reshape(N_DEV, d // N_DEV, d)
    grads["xattn_wk"] = mm("d_wk", mn, dkx, "tn", out_dtype=BF16).reshape(N_DEV, d // N_DEV, d)
    grads["xattn_wv"] = mm("d_wv", mn, dvx, "tn", out_dtype=BF16).reshape(N_DEV, d // N_DEV, d)
    qkv = ("xattn_wq", "xattn_wk", "xattn_wv")
    sc_qkv, tok = exchange_start("scatter_qkv_start", [grads[n] for n in qkv], "scatter")
    dmn = mm("d_mn_v", dvx, wv_f, "nt", res=mm("d_mn_k", dkx, wk_f, "nt", dep=tok))
    _, (dg_mem,) = rowwise_bwd("norm_mem_bwd", rmsnorm_fn, [memt], [norm_mem_g], [[dmn]], tm, [None])
    dh2 = mm("d_h2", dq, wq_f, "nt", out_dtype=BF16, dep=dg_mem)
    (dx1,), (dg_x,) = rowwise_bwd("norm_x_bwd", rmsnorm_fn, [x1], [norm_x_g], [[dh2]], tm, [F32], row_add=[dx2])

    grads["w_out"] = mm("d_w_out", ycat, dx1, "tn", out_dtype=BF16).reshape(N_DEV, d // N_DEV, d)
    sc_wout, tok = exchange_start("scatter_w_out_start", [grads["w_out"]], "scatter")
    d_ycat = mm("d_ycat", dx1, w_out_f, "nt", out_dtype=BF16, dep=tok)

    (d_ys, d_r1, d_k1, d_v1, d_gate), (dln_w, dln_b, dr_k) = rowwise_bwd(
        "rwkv_post_bwd", rwkv_post_fn, [ys_r, r_, k_, v_, gate_], rwkv_post_params,
        [[Rows(d_ycat, RWKV_WIDTH, 1)]], tm, [BF16] * 5)
    d_r2, d_lw, d_k2, d_v2, d_kap, d_b = rwkv_scan_bwd(r_, lw_, k_, v_, kap_, b_, rwkv_ck, rwkv_inv, d_ys)
    (du_rkv, du_pg, du_pwa), rwkv_pg = rowwise_bwd(
        "rwkv_pre_bwd", rwkv_pre_fn, rwkv_pre_rows(), rwkv_pre_params,
        [[d_r1, d_r2], [d_lw], [d_k1, d_k2], [d_v1, d_v2], [d_kap], [d_b], [d_gate]], tm_rwkv, [BF16] * 3)
    dmu_rkv, dmu_pg, dmu_pwa, dw0, dw2p, da0, da2p, dg2, dk_k, dk_a = rwkv_pg

    (d_yscan, du_z), (dssd_norm_g,) = rowwise_bwd("ssd_post_bwd", ssd_post_fn, [y_scan, u_z], [ssd_norm_g],
                                                  [[Rows(d_ycat, SSD_WIDTH, 0)]], tm, [BF16, BF16])
    dxs, dbm, dcm, ddt2, da_log_p, dd_p = ssd_scan_bwd(xs, bm, cm, dt, a_log_p, d_p, ssd_ck, d_yscan)
    (du_xbc, du_dt), (dconv_w, dconv_b, ddt_bias_p) = rowwise_bwd(
        "ssd_pre_bwd", ssd_pre_fn, ssd_pre_rows(), ssd_pre_params,
        [[dxs], [dbm], [dcm], [ddt2[0], ddt2[1]]], tm, [BF16, BF16])
    du_sm = jnp.concatenate([du_pwa, du_dt], axis=1)

    dwt_z = mm("d_w_z", du_z, h1, "tn", out_dtype=BF16)
    dwt_xbc = mm("d_w_xbc", du_xbc, h1, "tn", out_dtype=BF16)
    dwt_rkv = mm("d_w_rkv", du_rkv, h1, "tn", out_dtype=BF16)
    dwt_pg = mm("d_w_pg", du_pg, h1, "tn", out_dtype=BF16)
    dwt_sm = mm("d_w_small", du_sm, h1, "tn", out_dtype=BF16)
    dwt_full = jnp.concatenate([dwt_z, dwt_xbc, dwt_sm[256:272], dwt_rkv, dwt_sm[0:96], dwt_sm[128:224], dwt_pg], axis=0)
    to_slabs = lambda g: jnp.transpose(g.reshape(g.shape[0], N_DEV, -1), (1, 0, 2))
    grads["w_in"] = dwt_full.reshape(N_DEV, D_IN // N_DEV, d)
    grads["ssd_conv_w"] = to_slabs(dconv_w)
    grads["rwkv_w2"] = to_slabs(dw2p[:96])
    grads["rwkv_a2"] = to_slabs(da2p[:96])
    grads["rwkv_g2"] = to_slabs(dg2)
    tail = ("w_in", "ssd_conv_w", "rwkv_w2", "rwkv_a2", "rwkv_g2")
    sc_tail, tok = exchange_start("scatter_tail_start", [grads[n] for n in tail], "scatter")
    dh1 = mm("d_h1_z", du_z, wt_z, "nn", dep=tok)
    dh1 = mm("d_h1_xbc", du_xbc, wt_xbc, "nn", res=dh1)
    dh1 = mm("d_h1_rkv", du_rkv, wt_rkv, "nn", res=dh1)
    dh1 = mm("d_h1_pg", du_pg, wt_pg, "nn", res=dh1)
    dh1 = mm("d_h1_small", du_sm, wt_sm, "nn", res=dh1)
    (dx,), (dg_mix,) = rowwise_bwd("norm_mix_bwd", rmsnorm_fn, [xt], [norm_mix_g], [[dh1]], tm, [F32], row_add=[dx1])

    dmu =jnp.concatenate([dmu_rkv, dmu_pwa[:, 0:96], dmu_pwa[:, 128:224], dmu_pg], axis=1)
    small_grads = {
        "norm_mix_g": dg_mix, "ssd_conv_b": dconv_b, "ssd_dt_bias": ddt_bias_p[:, :16], "ssd_a_log": da_log_p[:, :16],
        "ssd_d": dd_p[:, :16], "ssd_norm_g": dssd_norm_g, "rwkv_mu": dmu, "rwkv_w0": dw0, "rwkv_a0": da0,
        "rwkv_k_k": dk_k, "rwkv_k_a": dk_a, "rwkv_r_k": dr_k, "rwkv_ln_w": dln_w, "rwkv_ln_b": dln_b,
        "norm_x_g": dg_x, "norm_mem_g": dg_mem, "norm_ffn_g": dg_ffn, "final_norm_g": dg_final}

    gather_small, tok = exchange_start("gather_small_start", [_pack_small(small_grads)], "gather")
    received = {}
    for names, handle in ((("ffn_w2",), sc_w2), (("ffn_w1",), sc_w1), (("xattn_wo",), sc_wo), (qkv, sc_qkv),
                          (("w_out",), sc_wout)):
        received.update(zip(names, exchange_wait("scatter_" + names[0] + "_wait", handle, after=tok)))

    out_g, out_d, out_m, out_v = {}, {}, {}, {}

    def run_adamw(n, dep):
        shape = wts[n].shape
        two_d = lambda a: a.reshape(-1, shape[-1])
        if n == "w_in":
            recv = jnp.transpose(sum_slabs("sum_w_in", received[n]))[None]
        else:
            recv = received[n].reshape(N_DEV, -1, shape[-1])
        res = adamw("adamw_" + n, recv, two_d(wts[n]), two_d(mom_m[n]), two_d(mom_v[n]), dep=dep)
        out_g[n], out_d[n], out_m[n], out_v[n] = (r.reshape(shape) for r in res)
        return res[0]

    last = None
    for n in ("ffn_w2", "ffn_w1", "xattn_wo") + qkv + ("w_out",):
        last = run_adamw(n, last)
    received.update(zip(tail, exchange_wait("scatter_tail_wait", sc_tail, after=last)))
    for n in tail:
        last = run_adamw(n, last)
    (small_all,) = exchange_wait("gather_small_wait", gather_small, after=last)
    res = adamw("adamw_small", small_all, _pack_small(wts), _pack_small(mom_m), _pack_small(mom_v))
    shapes = {n: wts[n].shape for n in _SMALL}
    for dst, packed in zip((out_g, out_d, out_m, out_v), res):
        dst.update(_unpack_small(packed, shapes))

    loss = lax.psum(loss_blk[0, 0], ("x", "y", "c"))
    return (loss, dx[None], *[out_g[n] for n in _WEIGHTS], *[out_d[n] for n in _WEIGHTS],
            *[out_m[n] for n in _WEIGHTS], *[out_v[n] for n in _WEIGHTS])
```

```python
import functools

import jax
import jax.numpy as jnp
from jax import lax
from jax.experimental import pallas as pl
from jax.experimental.pallas import tpu as pltpu

F32 = jnp.float32
BF16 = jnp.bfloat16

N_DEV = 8
D_MODEL = 2048
NORM_EPS = 1e-6
SSD_WIDTH = 1024
SSD_HEAD_DIM = 64
SSD_STATE = 128
SSD_CHUNK = 128
SSD_HEADS_PER_GROUP = 8
RWKV_WIDTH = 1024
RWKV_HEADS = 16
RWKV_HEAD_DIM = 64
RWKV_LN_EPS = 64e-5
RWKV_CHUNK = 64
RWKV_HEADS_PER_STEP = 16
XATTN_HEADS = 4
XATTN_HEAD_DIM = 512
D_FF = 8192
LANES = 128
SUBLANES = 8
VMEM_LIMIT = 56 * 1024 * 1024

ADAM_LR = 0.001
ADAM_B1 = 0.9
ADAM_B2 = 0.999
ADAM_EPS = 1e-08
ADAM_WD = 0.01
ADAM_STEP = 10

_DN = {"nn": ((1,), (0,)), "nt": ((1,), (1,)), "tn": ((0,), (0,))}


def _dg(a, b, mode):
    (ca,), (cb,) = _DN[mode]
    dn = (((ca + 1,), (cb + 1,)), ((0,), (0,))) if a.ndim == 3 else (((ca,), (cb,)), ((), ()))
    return lax.dot_general(a, b, dn, preferred_element_type=F32)


@functools.partial(jax.custom_vjp, nondiff_argnums=(2,))
def bdot(a, b, mode):
    return _dg(a.astype(BF16), b.astype(BF16), mode)


def _bdot_fwd(a, b, mode):
    return bdot(a, b, mode), (a, b)


def _bdot_bwd(mode, res, g):
    a, b = res
    ab, bb, gb = a.astype(BF16), b.astype(BF16), g.astype(BF16)
    if mode == "nn":
        da, db = _dg(gb, bb, "nt"), _dg(ab, gb, "tn")
    elif mode == "nt":
        da, db = _dg(gb, bb, "nn"), _dg(gb, ab, "tn")
    else:
        da, db = _dg(bb, gb, "nt"), _dg(ab, gb, "nn")
    return da.astype(a.dtype), db.astype(b.dtype)


bdot.defvjp(_bdot_fwd, _bdot_bwd)


def _split2(x):
    hi = x.astype(BF16)
    return hi, (x - hi.astype(F32)).astype(BF16)


def _dot01(x, m01):
    hi, lo = _split2(x)
    return _dg(hi, m01, "nn") + _dg(lo, m01, "nn")


def _exact_dot_impl(a, b, mode, exact):
    if exact == "a":
        ae = a.astype(BF16)
        return sum(_dg(ae, part, mode) for part in _split2(b))
    be = b.astype(BF16)
    return sum(_dg(part, be, mode) for part in _split2(a))


@functools.partial(jax.custom_vjp, nondiff_argnums=(2, 3))
def exact_dot(a, b, mode, exact):
    return _exact_dot_impl(a, b, mode, exact)


def _exact_dot_fwd(a, b, mode, exact):
    return _exact_dot_impl(a, b, mode, exact), (a, b)


def _exact_dot_bwd(mode, exact, res, g):
    a, b = res
    if exact == "a":
        db = {"nn": lambda: _exact_dot_impl(a, g, "tn", "a"), "nt": lambda: _exact_dot_impl(g, a, "tn", "b"),
              "tn": lambda: _exact_dot_impl(a, g, "nn", "a")}[mode]()
        return jnp.zeros_like(a), db
    da = {"nn": lambda: _exact_dot_impl(g, b, "nt", "b"), "nt": lambda: _exact_dot_impl(g, b, "nn", "b"),
          "tn": lambda: _exact_dot_impl(b, g, "nt", "a")}[mode]()
    return da, jnp.zeros_like(b)


exact_dot.defvjp(_exact_dot_fwd, _exact_dot_bwd)


def _head_indicator(width, heads, transpose):
    hd = width // heads
    shape = (LANES, width) if transpose else (width, LANES)
    lane = lax.broadcasted_iota(jnp.int32, shape, 1 if not transpose else 0)
    pos = lax.broadcasted_iota(jnp.int32, shape, 0 if not transpose else 1)
    return ((pos >= lane * hd) & (pos < lane * hd + hd)).astype(BF16)


@jax.custom_vjp
def head_sum(x):
    w = x.shape[-1]
    e = _head_indicator(w, w // RWKV_HEAD_DIM, False)
    et = _head_indicator(w, w // RWKV_HEAD_DIM, True)
    return _dot01(_dot01(x, e), et)


head_sum.defvjp(lambda x: (head_sum(x), None), lambda _, g: (head_sum(g),))


def rmsnorm_fn(x, g):
    y = x * lax.rsqrt(jnp.mean(x * x, axis=-1, keepdims=True) + NORM_EPS)
    return ((y * g).astype(BF16),)


def cast_fn(x):
    return (x.astype(BF16),)


def ssd_pre_fn(xbc, xbc1, xbc2, xbc3, dt_raw, conv_w, conv_b, dt_bias):
    c = conv_w[3:4] * xbc + conv_w[2:3] * xbc1 + conv_w[1:2] * xbc2 + conv_w[0:1] * xbc3 + conv_b
    act = c * jax.nn.sigmoid(c)
    dt = jax.nn.softplus(dt_raw + dt_bias)
    return act[:, :SSD_WIDTH], act[:, SSD_WIDTH:SSD_WIDTH + 256], act[:, SSD_WIDTH + 256:], dt


def ssd_post_fn(yscan, z, norm_g):
    y = yscan * (z * jax.nn.sigmoid(z))
    half = SSD_WIDTH // 2
    parts = []
    for g in range(2):
        yg = y[:, g * half:(g + 1) * half]
        parts.append(yg * lax.rsqrt(jnp.mean(yg * yg, axis=-1, keepdims=True) + NORM_EPS))
    return ((jnp.concatenate(parts, axis=-1) * norm_g).astype(BF16),)


def rwkv_pre_fn(rkv, rkv_p, pg, pg_p, pwa, pwa_p, mu_rkv, mu_pg, mu_pwa, w0, w2p, a0, a2p, g2, k_k, k_a):
    w = RWKV_WIDTH
    rkv = rkv + (rkv_p - rkv) * mu_rkv
    pg = pg + (pg_p - pg) * mu_pg
    pwa = pwa + (pwa_p - pwa) * mu_pwa
    r, k, v = rkv[:, :w], rkv[:, w:2 * w], rkv[:, 2 * w:]
    pw, pa = pwa[:, :LANES], pwa[:, LANES:]
    w_log = -jax.nn.softplus(-(w0 + bdot(jnp.tanh(pw), w2p, "nn"))) - 0.5
    lw = -jnp.exp(w_log)
    iclr = jax.nn.sigmoid(a0 + bdot(pa, a2p, "nn"))
    gate = bdot(jax.nn.sigmoid(pg), g2, "nn")
    kk = k * k_k
    kap = kk * lax.rsqrt(jnp.maximum(head_sum(kk * kk), 1e-24))
    k_mod = k * (1.0 + (iclr - 1.0) * k_a)
    return r, lw, k_mod, v, kap, kap * iclr, gate


def rwkv_post_fn(ys, r, k_mod, v, gate, ln_w, ln_b, r_k):
    inv_n = 1.0 / RWKV_HEAD_DIM
    mean = head_sum(ys) * inv_n
    yc = ys - mean
    var = head_sum(yc * yc) * inv_n
    yn = yc * lax.rsqrt(var + RWKV_LN_EPS) * ln_w + ln_b
    bonus = head_sum(r * k_mod * r_k) * v
    return (((yn + bonus) * gate).astype(BF16),)


def attn_fn(q, kx, vx):
    outs = []
    for h in range(XATTN_HEADS):
        sl = slice(h * XATTN_HEAD_DIM, (h + 1) * XATTN_HEAD_DIM)
        s = bdot(q[:, sl], kx[:, sl], "nt") * (XATTN_HEAD_DIM ** -0.5)
        s = s - jnp.max(s, axis=-1, keepdims=True)
        p = jnp.exp(s)
        p = p / jnp.sum(p, axis=-1, keepdims=True)
        outs.append(bdot(p, vx[:, sl], "nn"))
    return (jnp.concatenate(outs, axis=-1).astype(BF16),)


def loss_fn(x, tgt, g):
    y = x * lax.rsqrt(jnp.mean(x * x, axis=-1, keepdims=True) + NORM_EPS) * g
    err = jnp.square(y - tgt)
    return 0.5 * jnp.sum(jnp.mean(err, axis=-1, keepdims=True), axis=0, keepdims=True)


def _tri_masks(n):
    row = lax.broadcasted_iota(jnp.int32, (n, n), 0)
    col = lax.broadcasted_iota(jnp.int32, (n, n), 1)
    return col <= row, col < row, row == col


@jax.custom_vjp
def unit_lower_inverse(a):
    c = a.shape[-1]
    eye = _tri_masks(c)[2].astype(F32)
    m = -a
    inv = eye + m
    n = 1
    while n * 2 < c:
        m = bdot(m, m, "nn")
        inv = bdot(inv, eye + m, "nn")
        n *= 2
    return inv


def _unit_lower_inverse_fwd(a):
    inv = unit_lower_inverse(a)
    return inv, inv


def _unit_lower_inverse_bwd(inv, g):
    return (-bdot(bdot(inv, g, "tn"), inv, "nt"),)


unit_lower_inverse.defvjp(_unit_lower_inverse_fwd, _unit_lower_inverse_bwd)


@jax.custom_vjp
def known_inverse(a, inv):
    return inv


known_inverse.defvjp(lambda a, inv: (inv, inv),
                     lambda inv, g: (_unit_lower_inverse_bwd(inv, g)[0], jnp.zeros_like(inv)))


def rwkv_chunk_fn(st0, r, lw, k, v, kap, b, inv=None):
    h, c = r.shape[0], r.shape[1]
    incl, strict, _ = _tri_masks(c)
    cum = exact_dot(jnp.broadcast_to(incl.astype(F32), (h, c, c)), lw, "nn", "a")
    g_in = jnp.exp(cum)
    g_prev = jnp.exp(cum - lw)
    g_inv = jnp.exp(-cum)
    g_end = jnp.exp(cum[:, c - 1:c, :] - cum)
    kap_t, k_t, b_t, r_t = kap * g_prev, k * g_inv, b * g_inv, r * g_in
    a_ub = jnp.where(strict, bdot(kap_t, b_t, "nt"), 0.0)
    a_vk = jnp.where(strict, bdot(kap_t, k_t, "nt"), 0.0)
    rhs = -(bdot(kap_t, st0, "nt") + bdot(a_vk, v, "nn"))
    inv = unit_lower_inverse(a_ub) if inv is None else known_inverse(a_ub, inv)
    u = bdot(inv, rhs, "nn")
    y = (bdot(r_t, st0, "nt")
         + bdot(jnp.where(incl, bdot(r_t, k_t, "nt"), 0.0), v, "nn")
         + bdot(jnp.where(incl, bdot(r_t, b_t, "nt"), 0.0), u, "nn"))
    st1 = jnp.exp(cum[:, c - 1:c, :]) * st0 + bdot(v, k * g_end, "tn") + bdot(u, b * g_end, "tn")
    return y, st1, inv


def ssd_chunk_fn(group, h0, xs, bm, cm, dt, a_log, d_skip):
    q, nh = xs.shape[0], SSD_HEADS_PER_GROUP
    causal, _, _ = _tri_masks(q)
    a_row = -jnp.exp(a_log)
    cs_all = exact_dot(causal.astype(F32), dt * a_row, "nn", "a")
    cs_t = cs_all.T
    lanes = range(group * nh, (group + 1) * nh)
    cs = jnp.stack([cs_all[:, hl:hl + 1] for hl in lanes])
    cs_row = jnp.stack([cs_t[hl:hl + 1, :] for hl in lanes])
    dt_h = jnp.stack([dt[:, hl:hl + 1] for hl in lanes])
    d_h = jnp.stack([d_skip[:, hl:hl + 1] for hl in lanes])
    x = _stack_lanes(xs, nh)
    h0s = _stack_rows(h0, nh)
    lmat = jnp.where(causal, jnp.exp(jnp.where(causal, cs - cs_row, 0.0)), 0.0)
    cb = bdot(cm, bm, "nt")
    xdt = x * dt_h
    cl = cs[:, q - 1:q, :]
    cm_b = jnp.broadcast_to(cm, (nh,) + cm.shape)
    bm_b = jnp.broadcast_to(bm, (nh,) + bm.shape)
    y = bdot(cb * lmat, xdt, "nn") + bdot(cm_b, h0s, "nt") * jnp.exp(cs) + x * d_h
    h1 = h0s * jnp.exp(cl) + bdot(xdt * jnp.exp(cl - cs), bm_b, "tn")
    return jnp.concatenate([y[e] for e in range(nh)], axis=-1), jnp.concatenate([h1[e] for e in range(nh)], axis=0)


class Rows:
    def __init__(self, arr, w=None, cb=0, shifts=()):
        self.arr, self.w, self.cb, self.shifts = arr, (arr.shape[1] if w is None else w), cb, tuple(shifts)


def _as_rows(x):
    return x if isinstance(x, Rows) else Rows(x)


def _shift_down(x, halo, k):
    rolled = pltpu.roll(x, k, 0)
    first = rolled[0:SUBLANES]
    rid = lax.broadcasted_iota(jnp.int32, first.shape, 0)
    patched = jnp.where(rid < k, pltpu.roll(halo, k, 0), first)
    return jnp.concatenate([patched, rolled[SUBLANES:]], axis=0)


def _shift_up(g, carry, k):
    tm = g.shape[0]
    rolled = pltpu.roll(g, tm - k, 0)
    last = rolled[tm - SUBLANES:]
    rid = lax.broadcasted_iota(jnp.int32, last.shape, 0)
    patched = jnp.where(rid >= SUBLANES - k, pltpu.roll(carry, SUBLANES - k, 0), last)
    return jnp.concatenate([rolled[:tm - SUBLANES], patched], axis=0)


def _params():
    return pltpu.CompilerParams(vmem_limit_bytes=VMEM_LIMIT)


def _load_rows(refs, pos, rins, first_block):
    vals = []
    for r in rins:
        x = refs[pos][...].astype(F32) if refs[pos].dtype != F32 else refs[pos][...]
        pos += 1
        vals.append(x)
        if r.shifts:
            halo = refs[pos][...]
            pos += 1
            halo = jnp.where(first_block, jnp.zeros_like(halo), halo)
            for k in r.shifts:
                vals.append(_shift_down(x, halo, k))
    return vals, pos


def _row_specs(rins, tm, blk):
    specs, args = [], []
    for r in rins:
        specs.append(pl.BlockSpec((tm, r.w), lambda i, cb=r.cb: (blk(i), cb)))
        args.append(r.arr)
        if r.shifts:
            per = tm // SUBLANES
            specs.append(pl.BlockSpec((SUBLANES, r.w), lambda i, cb=r.cb: (jnp.maximum(blk(i) * per - 1, 0), cb)))
            args.append(r.arr)
    return specs, args


def rowwise_fwd(name, fn, rins, params, outs, tm, deps=()):
    rins = [_as_rows(r) for r in rins]
    t = rins[0].arr.shape[0]
    tm = min(tm, t)
    nb = t // tm
    specs, args = _row_specs(rins, tm, lambda i: i)
    for p in params:
        specs.append(pl.BlockSpec(p.shape, lambda i: (0, 0)))
        args.append(p)
    for dep in deps:
        specs.append(pl.BlockSpec(memory_space=pl.ANY))
        args.append(dep)
    n_in = len(args)

    def body(*refs):
        vals, pos = _load_rows(refs, 0, rins, pl.program_id(0) == 0)
        pv = [refs[pos + j][...] for j in range(len(params))]
        res = fn(*vals, *pv)
        for o_ref, o in zip(refs[n_in:], res):
            o_ref[...] = o.astype(o_ref.dtype)

    return pl.pallas_call(
        body, name=name, grid=(nb,), in_specs=specs,
        out_specs=[pl.BlockSpec((tm, w), lambda i: (i, 0)) for w, _ in outs],
        out_shape=[jax.ShapeDtypeStruct((t, w), dt) for w, dt in outs],
        compiler_params=_params(),
    )(*args)


def rowwise_bwd(name, fn, rins, params, cts, tm, grad_dtypes, row_add=None):
    rins = [_as_rows(r) for r in rins]
    cts = [[_as_rows(c) for c in lst] for lst in cts]
    row_add = [_as_rows(a) for a in (row_add or [])]
    t = rins[0].arr.shape[0]
    tm = min(tm, t)
    nb = t // tm
    rev = lambda i: nb - 1 - i
    specs, args = _row_specs(rins, tm, rev)
    for p in params:
        specs.append(pl.BlockSpec(p.shape, lambda i: (0, 0)))
        args.append(p)
    flat_cts = [c for lst in cts for c in lst] + row_add
    for c in flat_cts:
        specs.append(pl.BlockSpec((tm, c.w), lambda i, cb=c.cb: (rev(i), cb)))
        args.append(c.arr)
    n_in = len(args)
    want = [i for i, d in enumerate(grad_dtypes) if d is not None]
    out_specs = [pl.BlockSpec((tm, rins[i].w), lambda i_: (rev(i_), 0)) for i in want]
    out_shape = [jax.ShapeDtypeStruct((t, rins[i].w), grad_dtypes[i]) for i in want]
    out_specs += [pl.BlockSpec(p.shape, lambda i: (0, 0)) for p in params]
    out_shape += [jax.ShapeDtypeStruct(p.shape, F32) for p in params]
    n_out = len(out_shape)
    scratch = [pltpu.VMEM((SUBLANES, r.w), F32) for r in rins for _ in r.shifts]

    def body(*refs):
        i = pl.program_id(0)
        vals, pos = _load_rows(refs, 0, rins, rev(i) == 0)
        pv = [refs[pos + j][...] for j in range(len(params))]
        pos += len(params)
        outs, vjp = jax.vjp(fn, *vals, *pv)
        ct_vals = []
        for o, lst in zip(outs, cts):
            acc = None
            for _ in lst:
                cv = refs[pos][...].astype(F32)
                pos += 1
                acc = cv if acc is None else acc + cv
            ct_vals.append(acc.astype(o.dtype))
        adds = [refs[pos + j][...].astype(F32) for j in range(len(row_add))]
        grads = vjp(tuple(ct_vals))
        out_refs = refs[n_in:n_in + n_out]
        carry_refs = refs[n_in + n_out:]

        @pl.when(i == 0)
        def _():
            for cr in carry_refs:
                cr[...] = jnp.zeros_like(cr)
            for pr in out_refs[len(want):]:
                pr[...] = jnp.zeros_like(pr)

        gi, ci, oi = 0, 0, 0
        for idx, r in enumerate(rins):
            d = grads[gi]
            gi += 1
            for k in r.shifts:
                dk = grads[gi]
                gi += 1
                d = d + _shift_up(dk, carry_refs[ci][...], k)
                carry_refs[ci][...] = dk[0:SUBLANES]
                ci += 1
            if idx == 0:
                for a in adds:
                    d = d + a
            if grad_dtypes[idx] is not None:
                out_refs[oi][...] = d.astype(out_refs[oi].dtype)
                oi += 1
        for pr, gp in zip(out_refs[len(want):], grads[gi:]):
            pr[...] += gp

    res = pl.pallas_call(
        body, name=name, grid=(nb,), in_specs=specs, out_specs=out_specs, out_shape=out_shape,
        scratch_shapes=scratch, compiler_params=_params(),
    )(*args)
    return res[:len(want)], res[len(want):]


def _pick(n, pref):
    for c in pref:
        if n % c == 0:
            return c
    return n


MM_VMEM_BUDGET = 40 * 1024 * 1024
MM_PEAK_FLOPS = 0.9e15
MM_HBM_BYTES_PER_S = 3.0e12
MM_STEP_SECONDS = 0.35e-6


def _mm_tiles(m, n, k, size_a, size_b, size_out, size_res, single_k):
    best = None
    for tk in sorted({c for c in (k, 2048, 1024, 512, 256, 128) if c <= 2048 and k % c == 0}, reverse=True):
        for tm in (1024, 512, 256, 128):
            if m % tm:
                continue
            for tn in (2048, 1536, 1024, 768, 512, 384, 256, 128):
                if n % tn:
                    continue
                nk = k // tk
                vmem = 2 * (tm * tk * size_a + tk * tn * size_b + tm * tn * (size_out + size_res))
                vmem += tm * tn * 4 * (2 if nk > 1 or not single_k else 1)
                vmem += (tm * tk * 2 if size_a > 2 else 0) + (tk * tn * 2 if size_b > 2 else 0)
                if vmem > MM_VMEM_BUDGET:
                    continue
                steps = (m // tm) * (n // tn) * nk
                a_reads = 1 if (nk == 1 and single_k) else n // tn
                traffic = m * k * size_a * a_reads + k * n * size_b * (m // tm) + m * n * (size_out + size_res)
                cost = max(2.0 * m * n * k / MM_PEAK_FLOPS, traffic / MM_HBM_BYTES_PER_S) + steps * MM_STEP_SECONDS
                if best is None or cost < best[0]:
                    best = (cost, tm, tn, tk)
    return best[1:]


def mm(name, a, b, mode, out_dtype=F32, res=None, b_slabs=None, out_slabs=None, dep=None, epi=None, extras=(),
       out_dtypes=None):
    if mode == "tn":
        k_dim, m_dim = a.shape
    else:
        m_dim, k_dim = a.shape
    if b_slabs:
        n_dim = b.shape[0] * b.shape[2] if mode == "nn" else b.shape[1]
    else:
        n_dim = b.shape[0] if mode == "nt" else b.shape[1]
    n_slabs = out_slabs or (b_slabs if (b_slabs and mode == "nn") else 1)
    k_slabs = b_slabs if (b_slabs and mode == "nt") else 1
    if epi is None:
        out_dtypes = [out_dtype]
        if res is None:
            epi = lambda acc: (acc,)
        else:
            extras, epi = [res], lambda acc, r: (acc + r,)
    tm, tn, tk = _mm_tiles(m_dim, n_dim // n_slabs, k_dim // k_slabs, a.dtype.itemsize, b.dtype.itemsize,
                           sum(jnp.dtype(dt).itemsize for dt in out_dtypes), sum(e.dtype.itemsize for e in extras),
                           single_k=(k_slabs == 1))
    nji = n_dim // n_slabs // tn
    nki = k_dim // k_slabs // tk
    nblk = lambda js, j: js * nji + j
    kblk = lambda ks, k: ks * nki + k
    if mode == "tn":
        a_spec = pl.BlockSpec((tk, tm), lambda i, js, j, ks, k: (kblk(ks, k), i))
    else:
        a_spec = pl.BlockSpec((tm, tk), lambda i, js, j, ks, k: (i, kblk(ks, k)))
    if b_slabs and mode == "nn":
        b_spec = pl.BlockSpec((None, tk, tn), lambda i, js, j, ks, k: (js, k, j))
    elif b_slabs and mode == "nt":
        b_spec = pl.BlockSpec((None, tn, tk), lambda i, js, j, ks, k: (ks, nblk(js, j), k))
    elif mode == "nt":
        b_spec = pl.BlockSpec((tn, tk), lambda i, js, j, ks, k: (nblk(js, j), kblk(ks, k)))
    else:
        b_spec = pl.BlockSpec((tk, tn), lambda i, js, j, ks, k: (kblk(ks, k), nblk(js, j)))
    specs, args = [a_spec, b_spec], [a, b]
    for e in extras:
        specs.append(pl.BlockSpec((tm, tn), lambda i, js, j, ks, k: (i, nblk(js, j))))
        args.append(e)
    if dep is not None:
        specs.append(pl.BlockSpec(memory_space=pl.ANY))
        args.append(dep)
    if out_slabs:
        o_specs = [pl.BlockSpec((None, tm, tn), lambda i, js, j, ks, k: (js, i, j))]
        o_shapes = [jax.ShapeDtypeStruct((out_slabs, m_dim, n_dim // out_slabs), out_dtypes[0])]
    else:
        o_specs = [pl.BlockSpec((tm, tn), lambda i, js, j, ks, k: (i, nblk(js, j))) for _ in out_dtypes]
        o_shapes = [jax.ShapeDtypeStruct((m_dim, n_dim), dt) for dt in out_dtypes]

    one_k_step = k_slabs * nki == 1
    n_in, n_out = len(args), len(out_dtypes)

    def body(*refs):
        a_ref, b_ref = refs[0], refs[1]
        part = _dg(a_ref[...].astype(BF16), b_ref[...].astype(BF16), mode)

        def finish(acc):
            outs = epi(acc, *[refs[2 + j][...].astype(F32) for j in range(len(extras))])
            for o_ref, o in zip(refs[n_in:n_in + n_out], outs):
                o_ref[...] = o.astype(o_ref.dtype)

        if one_k_step:
            finish(part)
            return
        acc_ref = refs[n_in + n_out]
        ks, kk = pl.program_id(3), pl.program_id(4)

        @pl.when((ks == 0) & (kk == 0))
        def _():
            acc_ref[...] = part

        @pl.when((ks > 0) | (kk > 0))
        def _():
            acc_ref[...] += part

        pl.when((ks == k_slabs - 1) & (kk == nki - 1))(lambda: finish(acc_ref[...]))

    grid = (m_dim // tm, n_slabs, nji, k_slabs, nki)
    scratch = [] if one_k_step else [pltpu.VMEM((tm, tn), F32)]
    out = pl.pallas_call(
        body, name=name, grid=grid, in_specs=specs, out_specs=o_specs, out_shape=o_shapes, scratch_shapes=scratch,
        compiler_params=pltpu.CompilerParams(
            dimension_semantics=("parallel", "parallel", "parallel", "arbitrary", "arbitrary"),
            vmem_limit_bytes=VMEM_LIMIT),
    )(*args)
    return out[0] if n_out == 1 else out


def _stack_lanes(x, n):
    w = x.shape[1] // n
    return jnp.stack([x[:, i * w:(i + 1) * w] for i in range(n)])


def _stack_rows(x, n):
    w = x.shape[0] // n
    return jnp.stack([x[i * w:(i + 1) * w, :] for i in range(n)])


def rwkv_scan_fwd(r, lw, k, v, kap, b):
    t = r.shape[0]
    c, hps, hd = min(RWKV_CHUNK, t), RWKV_HEADS_PER_STEP, RWKV_HEAD_DIM
    nc, ng, wl = t // c, RWKV_HEADS // hps, hps * hd
    spec = pl.BlockSpec((c, wl), lambda g, ci: (ci, g))

    def body(r_ref, lw_ref, k_ref, v_ref, kap_ref, b_ref, y_ref, ck_ref, inv_ref, st_ref):
        @pl.when(pl.program_id(1) == 0)
        def _():
            st_ref[...] = jnp.zeros_like(st_ref)

        st = st_ref[...]
        ck_ref[...] = st
        ins = [x[...] for x in (r_ref, lw_ref, k_ref, v_ref, kap_ref, b_ref)]
        y, st1, inv = rwkv_chunk_fn(_stack_rows(st, hps), *[_stack_lanes(x, hps) for x in ins])
        y_ref[...] = jnp.concatenate([y[h] for h in range(hps)], axis=-1)
        st_ref[...] = jnp.concatenate([st1[h] for h in range(hps)], axis=0)
        inv_ref[...] = jnp.concatenate([inv[h] for h in range(hps)], axis=0)

    return pl.pallas_call(
        body, name="rwkv_scan_fwd", grid=(ng, nc), in_specs=[spec] * 6,
        out_specs=[spec, pl.BlockSpec((None, wl, hd), lambda g, ci: (ci, g, 0)),
                   pl.BlockSpec((None, hps * c, c), lambda g, ci: (ci, g, 0))],
        out_shape=[jax.ShapeDtypeStruct((t, RWKV_WIDTH), F32), jax.ShapeDtypeStruct((nc, RWKV_WIDTH, hd), F32),
                   jax.ShapeDtypeStruct((nc, RWKV_HEADS * c, c), F32)],
        scratch_shapes=[pltpu.VMEM((wl, hd), F32)], compiler_params=_params(),
    )(r, lw, k, v, kap, b)


def rwkv_scan_bwd(r, lw, k, v, kap, b, ck, inv_ck, dy):
    t = r.shape[0]
    c, hps, hd = min(RWKV_CHUNK, t), RWKV_HEADS_PER_STEP, RWKV_HEAD_DIM
    nc, ng, wl = t // c, RWKV_HEADS // hps, hps * hd
    spec = pl.BlockSpec((c, wl), lambda g, ci: (nc - 1 - ci, g))

    def body(r_ref, lw_ref, k_ref, v_ref, kap_ref, b_ref, ck_ref, inv_ref, dy_ref, *rest):
        out_refs, dst_ref = rest[:6], rest[6]

        @pl.when(pl.program_id(1) == 0)
        def _():
            dst_ref[...] = jnp.zeros_like(dst_ref)

        ins = [x[...] for x in (r_ref, lw_ref, k_ref, v_ref, kap_ref, b_ref)]
        dyv, ck, dst = dy_ref[...].astype(F32), ck_ref[...], dst_ref[...]
        chunk = lambda *a: rwkv_chunk_fn(*a, inv=_stack_rows(inv_ref[...], hps))[:2]
        _, vjp = jax.vjp(chunk, _stack_rows(ck, hps), *[_stack_lanes(x, hps) for x in ins])
        grads = vjp((_stack_lanes(dyv, hps), _stack_rows(dst, hps)))
        dst_ref[...] = jnp.concatenate([grads[0][h] for h in range(hps)], axis=0)
        for j in range(6):
            out_refs[j][...] = jnp.concatenate([grads[1 + j][h] for h in range(hps)], axis=-1).astype(BF16)

    return pl.pallas_call(
        body, name="rwkv_scan_bwd", grid=(ng, nc),
        in_specs=[spec] * 6 + [pl.BlockSpec((None, wl, hd), lambda g, ci: (nc - 1 - ci, g, 0)),
                               pl.BlockSpec((None, hps * c, c), lambda g, ci: (nc - 1 - ci, g, 0)), spec],
        out_specs=[spec] * 6, out_shape=[jax.ShapeDtypeStruct((t, RWKV_WIDTH), BF16)] * 6,
        scratch_shapes=[pltpu.VMEM((wl, hd), F32)], compiler_params=_params(),
    )(r, lw, k, v, kap, b, ck, inv_ck, dy)


def _ssd_specs(q, blk):
    gw = SSD_WIDTH // 2
    return [pl.BlockSpec((q, gw), lambda g, ci: (blk(ci), g)),
            pl.BlockSpec((q, SSD_STATE), lambda g, ci: (blk(ci), g)),
            pl.BlockSpec((q, SSD_STATE), lambda g, ci: (blk(ci), g)),
            pl.BlockSpec((q, LANES), lambda g, ci: (blk(ci), 0)),
            pl.BlockSpec((1, LANES), lambda g, ci: (0, 0)),
            pl.BlockSpec((1, LANES), lambda g, ci: (0, 0))]


def ssd_scan_fwd(xs, bm, cm, dt, a_log, d_skip):
    t = xs.shape[0]
    q = min(SSD_CHUNK, t)
    nc, gw = t // q, SSD_WIDTH // 2

    def body(xs_ref, bm_ref, cm_ref, dt_ref, al_ref, d_ref, y_ref, ck_ref, h_ref):
        @pl.when(pl.program_id(1) == 0)
        def _():
            h_ref[...] = jnp.zeros_like(h_ref)

        ck_ref[...] = h_ref[...]
        args = (h_ref[...], xs_ref[...], bm_ref[...], cm_ref[...], dt_ref[...], al_ref[...], d_ref[...])
        g = pl.program_id(0)

        @pl.when(g == 0)
        def _():
            y, h1 = ssd_chunk_fn(0, *args)
            y_ref[...] = y
            h_ref[...] = h1

        @pl.when(g == 1)
        def _():
            y, h1 = ssd_chunk_fn(1, *args)
            y_ref[...] = y
            h_ref[...] = h1

    return pl.pallas_call(
        body, name="ssd_scan_fwd", grid=(2, nc), in_specs=_ssd_specs(q, lambda ci: ci),
        out_specs=[pl.BlockSpec((q, gw), lambda g, ci: (ci, g)),
                   pl.BlockSpec((None, gw, SSD_STATE), lambda g, ci: (ci, g, 0))],
        out_shape=[jax.ShapeDtypeStruct((t, SSD_WIDTH), F32), jax.ShapeDtypeStruct((nc, SSD_WIDTH, SSD_STATE), F32)],
        scratch_shapes=[pltpu.VMEM((gw, SSD_STATE), F32)], compiler_params=_params(),
    )(xs, bm, cm, dt, a_log, d_skip)


def ssd_scan_bwd(xs, bm, cm, dt, a_log, d_skip, ck, dy):
    t = xs.shape[0]
    q = min(SSD_CHUNK, t)
    nc, gw = t // q, SSD_WIDTH // 2
    rev = lambda ci: nc - 1 - ci

    def body(xs_ref, bm_ref, cm_ref, dt_ref, al_ref, d_ref, ck_ref, dy_ref,
             dxs_ref, dbm_ref, dcm_ref, ddt_ref, dal_ref, dd_ref, dh_ref):
        g, ci = pl.program_id(0), pl.program_id(1)

        @pl.when(ci == 0)
        def _():
            dh_ref[...] = jnp.zeros_like(dh_ref)

        @pl.when((ci == 0) & (g == 0))
        def _():
            dal_ref[...] = jnp.zeros_like(dal_ref)
            dd_ref[...] = jnp.zeros_like(dd_ref)

        args = (ck_ref[...], xs_ref[...], bm_ref[...], cm_ref[...], dt_ref[...], al_ref[...], d_ref[...])

        def run(group):
            _, vjp = jax.vjp(functools.partial(ssd_chunk_fn, group), *args)
            dh0, dxs, dbm, dcm, ddt, dal, dd = vjp((dy_ref[...].astype(F32), dh_ref[...]))
            dh_ref[...] = dh0
            dxs_ref[...] = dxs.astype(BF16)
            dbm_ref[...] = dbm.astype(BF16)
            dcm_ref[...] = dcm.astype(BF16)
            ddt_ref[...] = ddt
            dal_ref[...] += dal
            dd_ref[...] += dd

        pl.when(g == 0)(lambda: run(0))
        pl.when(g == 1)(lambda: run(1))

    in_specs = _ssd_specs(q, rev) + [pl.BlockSpec((None, gw, SSD_STATE), lambda g, ci: (rev(ci), g, 0)),
                                     pl.BlockSpec((q, gw), lambda g, ci: (rev(ci), g))]
    return pl.pallas_call(
        body, name="ssd_scan_bwd", grid=(2, nc), in_specs=in_specs,
        out_specs=[pl.BlockSpec((q, gw), lambda g, ci: (rev(ci), g)),
                   pl.BlockSpec((q, SSD_STATE), lambda g, ci: (rev(ci), g)),
                   pl.BlockSpec((q, SSD_STATE), lambda g, ci: (rev(ci), g)),
                   pl.BlockSpec((None, q, LANES), lambda g, ci: (g, rev(ci), 0)),
                   pl.BlockSpec((1, LANES), lambda g, ci: (0, 0)),
                   pl.BlockSpec((1, LANES), lambda g, ci: (0, 0))],
        out_shape=[jax.ShapeDtypeStruct((t, SSD_WIDTH), BF16), jax.ShapeDtypeStruct((t, 2 * SSD_STATE), BF16),
                   jax.ShapeDtypeStruct((t, 2 * SSD_STATE), BF16), jax.ShapeDtypeStruct((2, t, LANES), F32),
                   jax.ShapeDtypeStruct((1, LANES), F32), jax.ShapeDtypeStruct((1, LANES), F32)],
        scratch_shapes=[pltpu.VMEM((gw, SSD_STATE), F32)], compiler_params=_params(),
    )(xs, bm, cm, dt, a_log, d_skip, ck, dy)


def loss_and_grad(x, tgt, g, tm):
    t, d = x.shape
    tm = min(tm, t)
    nb = t // tm

    def body(x_ref, t_ref, g_ref, loss_ref, dx_ref, dxb_ref, dg_ref):
        @pl.when(pl.program_id(0) == 0)
        def _():
            loss_ref[...] = jnp.zeros_like(loss_ref)
            dg_ref[...] = jnp.zeros_like(dg_ref)

        val, vjp = jax.vjp(loss_fn, x_ref[...], t_ref[...], g_ref[...])
        dx, _, dg = vjp(jnp.ones((1, 1), F32))
        loss_ref[...] += jnp.broadcast_to(val, loss_ref.shape)
        dx_ref[...] = dx
        dxb_ref[...] = dx.astype(BF16)
        dg_ref[...] += dg

    row = pl.BlockSpec((tm, d), lambda i: (i, 0))
    one = pl.BlockSpec((1, d), lambda i: (0, 0))
    return pl.pallas_call(
        body, name="loss_and_grad", grid=(nb,), in_specs=[row, row, one],
        out_specs=[pl.BlockSpec((SUBLANES, LANES), lambda i: (0, 0)), row, row, one],
        out_shape=[jax.ShapeDtypeStruct((SUBLANES, LANES), F32), jax.ShapeDtypeStruct((t, d), F32),
                   jax.ShapeDtypeStruct((t, d), BF16), jax.ShapeDtypeStruct((1, d), F32)],
        compiler_params=_params(),
    )(x, tgt, g)


def adamw(name, recv, w, m, v, dep=None):
    rows, cols = w.shape
    n_slabs = recv.shape[0]
    recv_block_bytes = 4 * 1024 * 1024
    tm = _pick(rows, [c for c in (256, 128, 64, 32, 16, 8) if n_slabs * c * cols * 4 <= recv_block_bytes])
    c1 = 1.0 / (1.0 - ADAM_B1 ** ADAM_STEP)
    c2 = 1.0 / (1.0 - ADAM_B2 ** ADAM_STEP)

    n_dep = 0 if dep is None else 1

    def body(recv_ref, w_ref, m_ref, v_ref, *rest):
        g_ref, d_ref, nm_ref, nv_ref = rest[n_dep:]
        g = recv_ref[0].astype(F32)
        for p in range(1, n_slabs):
            g = g + recv_ref[p].astype(F32)
        nm =ADAM_B1 * m_ref[...] + (1.0 - ADAM_B1) * g
        nv = ADAM_B2 * v_ref[...] + (1.0 - ADAM_B2) * jnp.square(g)
        g_ref[...] = g
        nm_ref[...] = nm
        nv_ref[...] = nv
        d_ref[...] = -ADAM_LR * ((nm * c1) / (jnp.sqrt(nv * c2) + ADAM_EPS) + ADAM_WD * w_ref[...])

    blk = pl.BlockSpec((tm, cols), lambda i: (i, 0))
    return pl.pallas_call(
        body, name=name, grid=(rows // tm,),
        in_specs=[pl.BlockSpec((n_slabs, tm, cols), lambda i: (0, i, 0)), blk, blk, blk]
        + [pl.BlockSpec(memory_space=pl.ANY)] * n_dep,
        out_specs=[blk] * 4, out_shape=[jax.ShapeDtypeStruct((rows, cols), F32)] * 4,
        compiler_params=_params(),
    )(recv, w, m, v, *([] if dep is None else [dep]))


def _mesh_pos():
    return lax.axis_index("x"), lax.axis_index("y"), lax.axis_index("c")


def _peer(pos, mask):
    x, y, c = pos
    return (1 - x if mask & 4 else x, 1 - y if mask & 2 else y, 1 - c if mask & 1 else c)


def _linear(pos):
    return 4 * pos[0] + 2 * pos[1] + pos[2]


class Exchange:
    MASKS = {"gather": (1, 2, 3, 4, 5, 6, 7), "scatter": (1, 2, 3, 4, 5, 6, 7), "gather_chips": (1, 2, 4, 6),
             "forward": (2, 4, 6)}

    def __init__(self, xs, kind, lands=None):
        self.kind, self.masks = kind, self.MASKS[kind]
        self.xs = [] if kind == "forward" else list(xs)
        if kind == "forward":
            self.land_shape = [jax.ShapeDtypeStruct(l.shape, l.dtype) for l in lands]
        elif kind == "scatter":
            self.land_shape = [jax.ShapeDtypeStruct(x.shape, x.dtype) for x in xs]
        else:
            self.land_shape = [jax.ShapeDtypeStruct((N_DEV,) + x.shape, x.dtype) for x in xs]
        self.n = len(self.land_shape)
        copies = self.n * len(self.masks)
        self.sems = [pltpu.SemaphoreType.DMA((copies,)), pltpu.SemaphoreType.DMA((copies,)),
                     pltpu.SemaphoreType.DMA((self.n,))]

    def _copies(self, ins, outs, sems, landing):
        send_sems, recv_sems, local_sems = sems
        me = _mesh_pos()
        me_lin = _linear(me)
        local, remote = [], []
        for ti in range(self.n):
            if self.kind != "forward":
                src_mine = ins[ti].at[me_lin] if self.kind == "scatter" else ins[ti]
                local.append(pltpu.make_async_copy(src_mine, outs[ti].at[me_lin], local_sems.at[ti]))
            for j, mask in enumerate(self.masks):
                if self.kind == "forward":
                    peer = _peer(me, 1)
                    src = outs[ti].at[_linear(_peer(me, mask))]
                    dst = outs[ti].at[_linear(_peer(me, mask ^ 1 if landing else mask))]
                else:
                    peer = _peer(me, mask)
                    src = ins[ti].at[_linear(peer)] if self.kind == "scatter" else ins[ti]
                    dst = outs[ti].at[_linear(peer) if landing else me_lin]
                sem_index = ti * len(self.masks) + j
                remote.append(pltpu.make_async_remote_copy(
                    src_ref=src, dst_ref=dst, send_sem=send_sems.at[sem_index], recv_sem=recv_sems.at[sem_index],
                    device_id=peer, device_id_type=pl.DeviceIdType.MESH))
        return local, remote

    def start(self, ins, outs, sems):
        local, remote = self._copies(ins, outs, sems, landing=False)
        for cp in local + remote:
            cp.start()

    def finish(self, ins, outs, sems):
        local, remote = self._copies(ins, outs, sems, landing=True)
        for cp in remote:
            cp.wait_recv()
        for cp in remote:
            cp.wait_send()
        for cp in local:
            cp.wait()


def exchange_start(name, xs, kind, dep=None, lands=None):
    ex = Exchange(xs, kind, lands)
    hbm = pl.BlockSpec(memory_space=pltpu.HBM)
    sem = pl.BlockSpec(memory_space=pltpu.SEMAPHORE)
    if lands is None:
        lands = [lax.empty(s.shape, s.dtype) for s in ex.land_shape]
    n_src, n = len(ex.xs), ex.n
    n_inputs = n_src + n + (0 if dep is None else 1)

    def body(*refs):
        ins, lnd, sems, token = refs[:n_src], refs[n_src:n_src + n], refs[n_inputs:n_inputs + 3], refs[-1]
        ex.start(ins, lnd, sems)
        token[...] = jnp.zeros_like(token)

    res = pl.pallas_call(
        body, name=name, in_specs=[hbm] * (n_src + n) + ([] if dep is None else [pl.BlockSpec(memory_space=pl.ANY)]),
        out_specs=[sem] * 3 + [hbm] * (n_src + n) + [pl.BlockSpec(memory_space=pltpu.VMEM)],
        out_shape=ex.sems + [pltpu.HBM(x.shape, x.dtype) for x in ex.xs]
        + [pltpu.HBM(s.shape, s.dtype) for s in ex.land_shape] + [jax.ShapeDtypeStruct((SUBLANES, LANES), F32)],
        input_output_aliases={i: 3 + i for i in range(n_src + n)},
        compiler_params=pltpu.CompilerParams(has_side_effects=pltpu.SideEffectType.DATAFLOW_SIDE_EFFECTING),
    )(*[pltpu.with_memory_space_constraint(x, pltpu.HBM) for x in ex.xs + list(lands)],
      *([] if dep is None else [dep]))
    return (ex, res[:3], res[3:3 + n_src], res[3 + n_src:3 + n_src + n]), res[-1]


def exchange_wait(name, handles, after):
    ex, sems, srcs, lands = handles
    n_src, n = len(srcs), len(lands)
    hbm = pl.BlockSpec(memory_space=pltpu.HBM)
    sem = pl.BlockSpec(memory_space=pltpu.SEMAPHORE)

    def body(*refs):
        ins, lnd, sem_refs = refs[:n_src], refs[n_src:n_src + n], refs[n_src + n:n_src + n + 3]
        ex.finish(ins, lnd, sem_refs)

    res = pl.pallas_call(
        body, name=name, in_specs=[hbm] * (n_src + n) + [sem] * 3 + [pl.BlockSpec(memory_space=pl.ANY)],
        out_specs=[hbm] * (n_src + n),
        out_shape=[pltpu.HBM(x.shape, x.dtype) for x in srcs] + [pltpu.HBM(x.shape, x.dtype) for x in lands],
        input_output_aliases={i: i for i in range(n_src + n)},
        compiler_params=pltpu.CompilerParams(has_side_effects=pltpu.SideEffectType.DATAFLOW_SIDE_EFFECTING),
    )(*srcs, *lands, *sems, after)
    return res[n_src:]


def forward_start(name, chip_gather, after):
    lands = exchange_wait(name + "_wait", chip_gather, after)
    return exchange_start(name + "_forward_start", [], "forward", lands=lands)


_Z = (0, 1024)
_XBC = (1024, 2560)
_DT = (2560, 2576)
_RKV = (2576, 5648)
_PW = (5648, 5744)
_PA = (5744, 5840)
_PG = (5840, 6096)
D_IN = 6096

_SMALL = ("norm_mix_g", "ssd_conv_b", "ssd_dt_bias", "ssd_a_log", "ssd_d", "ssd_norm_g", "rwkv_mu", "rwkv_w0",
          "rwkv_a0", "rwkv_k_k", "rwkv_k_a", "rwkv_r_k", "rwkv_ln_w", "rwkv_ln_b", "norm_x_g", "norm_mem_g",
          "norm_ffn_g", "final_norm_g")
_WEIGHTS = ("norm_mix_g", "w_in", "ssd_conv_w", "ssd_conv_b", "ssd_dt_bias", "ssd_a_log", "ssd_d", "ssd_norm_g",
            "rwkv_mu", "rwkv_w0", "rwkv_w2", "rwkv_a0", "rwkv_a2", "rwkv_g2", "rwkv_k_k", "rwkv_k_a", "rwkv_r_k",
            "rwkv_ln_w", "rwkv_ln_b", "w_out", "norm_x_g", "norm_mem_g", "xattn_wq", "xattn_wk", "xattn_wv",
            "xattn_wo", "norm_ffn_g", "ffn_w1", "ffn_w2", "final_norm_g")


def _pad_lanes(x, width=LANES):
    return jnp.pad(x, ((0, 0), (0, width - x.shape[1])))


def _pack_small(vals):
    flat = jnp.concatenate([vals[n].reshape(-1) for n in _SMALL])
    rows = -(-flat.shape[0] // (LANES * SUBLANES)) * SUBLANES
    return jnp.pad(flat, (0, rows * LANES - flat.shape[0])).reshape(rows, LANES)


def _unpack_small(packed, shapes):
    flat = packed.reshape(-1)
    out, pos = {}, 0
    for n in _SMALL:
        size = 1
        for s in shapes[n]:
            size *= s
        out[n] = flat[pos:pos + size].reshape(shapes[n])
        pos += size
    return out


def _rows(w, rng):
    return w[rng[0]:rng[1]]


def sum_slabs(name, recv):
    n, rows, cols = recv.shape
    tc = _pick(cols, (256, 128))

    def body(r_ref, o_ref):
        acc = r_ref[0].astype(F32)
        for p in range(1, n):
            acc = acc + r_ref[p].astype(F32)
        o_ref[...] = acc

    return pl.pallas_call(
        body, name=name, grid=(cols // tc,), in_specs=[pl.BlockSpec((n, rows, tc), lambda j: (0, 0, j))],
        out_specs=pl.BlockSpec((rows, tc), lambda j: (0, j)), out_shape=jax.ShapeDtypeStruct((rows, cols), F32),
        compiler_params=_params(),
    )(recv)


def kernel(x, mem, norm_mix_g, w_in, ssd_conv_w, ssd_conv_b, ssd_dt_bias, ssd_a_log, ssd_d, ssd_norm_g, rwkv_mu, rwkv_w0, rwkv_w2, rwkv_a0, rwkv_a2, rwkv_g2, rwkv_k_k, rwkv_k_a, rwkv_r_k, rwkv_ln_w, rwkv_ln_b, w_out, norm_x_g, norm_mem_g, xattn_wq, xattn_wk, xattn_wv, xattn_wo, norm_ffn_g, ffn_w1, ffn_w2, final_norm_g, loss_target, m_norm_mix_g, m_w_in, m_ssd_conv_w, m_ssd_conv_b, m_ssd_dt_bias, m_ssd_a_log, m_ssd_d, m_ssd_norm_g, m_rwkv_mu, m_rwkv_w0, m_rwkv_w2, m_rwkv_a0, m_rwkv_a2, m_rwkv_g2, m_rwkv_k_k, m_rwkv_k_a, m_rwkv_r_k, m_rwkv_ln_w, m_rwkv_ln_b, m_w_out, m_norm_x_g, m_norm_mem_g, m_xattn_wq, m_xattn_wk, m_xattn_wv, m_xattn_wo, m_norm_ffn_g, m_ffn_w1, m_ffn_w2, m_final_norm_g, v_norm_mix_g, v_w_in, v_ssd_conv_w, v_ssd_conv_b, v_ssd_dt_bias, v_ssd_a_log, v_ssd_d, v_ssd_norm_g, v_rwkv_mu, v_rwkv_w0, v_rwkv_w2, v_rwkv_a0, v_rwkv_a2, v_rwkv_g2, v_rwkv_k_k, v_rwkv_k_a, v_rwkv_r_k, v_rwkv_ln_w, v_rwkv_ln_b, v_w_out, v_norm_x_g, v_norm_mem_g, v_xattn_wq, v_xattn_wk, v_xattn_wv, v_xattn_wo, v_norm_ffn_g, v_ffn_w1, v_ffn_w2, v_final_norm_g):
    given = dict(locals())
    wts = {n: given[n] for n in _WEIGHTS}
    mom_m = {n: given["m_" + n] for n in _WEIGHTS}
    mom_v = {n: given["v_" + n] for n in _WEIGHTS}
    d = D_MODEL
    xt, memt, tgt = x[0], mem[0], loss_target[0]
    tm = 256
    tm_rwkv = 128

    big = {"w_in": jnp.transpose(w_in[0]), "w_out": w_out[0], "xattn_wq": xattn_wq[0], "xattn_wk": xattn_wk[0],
           "xattn_wv": xattn_wv[0], "xattn_wo": xattn_wo[0], "ffn_w1": ffn_w1[0], "ffn_w2": ffn_w2[0]}
    small_sh = {"ssd_conv_w": ssd_conv_w.reshape(4, -1), "rwkv_w2": rwkv_w2[0], "rwkv_a2": rwkv_a2[0],
                "rwkv_g2": rwkv_g2[0]}
    cast_one = lambda n, deps=(): rowwise_fwd("cast_" + n, cast_fn, [big[n]], [], [(big[n].shape[1], BF16)],
                                              256 if big[n].shape[0] % 256 == 0 else big[n].shape[0], deps=deps)[0]
    gather_in, token_in = exchange_start("gather_in_start", [cast_one("w_in")] + list(small_sh.values()), "gather_chips")
    cast = {n: cast_one(n, deps=[token_in]) for n in big if n != "w_in"}
    late_a = ("w_out", "xattn_wq", "xattn_wk", "xattn_wv", "xattn_wo")
    late_b = ("ffn_w1", "ffn_w2")
    gather_a, token_a = exchange_start("gather_attn_start", [cast[n] for n in late_a], "gather_chips", dep=token_in)
    gather_b, token_b = exchange_start("gather_ffn_start", [cast[n] for n in late_b], "gather_chips", dep=token_a)
    (h1,) = rowwise_fwd("norm_mix", rmsnorm_fn, [xt], [norm_mix_g], [(d, BF16)], tm, deps=[token_b])
    forward_in, token_in = forward_start("gather_in", gather_in, after=h1)
    gathered = exchange_wait("gather_in_forward_wait", forward_in, after=token_in)
    g_big = {"w_in": gathered[0]}
    g_small = dict(zip(small_sh, gathered[1:]))

    pad_rows = lambda a: jnp.pad(a, ((0, LANES - a.shape[0]), (0, 0)))
    w_in_t = g_big["w_in"].reshape(D_IN, d)
    wt_z, wt_xbc, wt_rkv, wt_pg = (_rows(w_in_t, r) for r in (_Z, _XBC, _RKV, _PG))
    wt_sm = jnp.concatenate([pad_rows(_rows(w_in_t, r)) for r in (_PW, _PA, _DT)], axis=0)
    unshard_cols = lambda g: jnp.transpose(g, (1, 0, 2)).reshape(g.shape[1], -1)
    conv_w_f = unshard_cols(g_small["ssd_conv_w"])
    w2p, a2p = pad_rows(unshard_cols(g_small["rwkv_w2"])), pad_rows(unshard_cols(g_small["rwkv_a2"]))
    g2_f = unshard_cols(g_small["rwkv_g2"])

    mu = rwkv_mu
    mu_rkv, mu_pg = mu[:, :3072], mu[:, 3264:3520]
    mu_pwa = jnp.concatenate([_pad_lanes(mu[:, 3072:3168]), _pad_lanes(mu[:, 3168:3264])], axis=1)
    dt_bias_p, a_log_p, d_p = _pad_lanes(ssd_dt_bias), _pad_lanes(ssd_a_log), _pad_lanes(ssd_d)
    r_k_row = rwkv_r_k.reshape(1, RWKV_WIDTH)
    g_final = final_norm_g.reshape(1, d)

    u_z = mm("in_z", h1, wt_z, "nt")
    u_xbc = mm("in_xbc", h1, wt_xbc, "nt")
    u_rkv = mm("in_rkv", h1, wt_rkv, "nt")
    u_pg = mm("in_pg", h1, wt_pg, "nt")
    u_sm = mm("in_small", h1, wt_sm, "nt")

    ssd_pre_rows = lambda: [Rows(u_xbc, shifts=(1, 2, 3)), Rows(u_sm, LANES, 2)]
    ssd_pre_params = [conv_w_f, ssd_conv_b, dt_bias_p]
    xs, bm, cm, dt = rowwise_fwd("ssd_pre", ssd_pre_fn, ssd_pre_rows(), ssd_pre_params,
                                 [(SSD_WIDTH, F32), (256, F32), (256, F32), (LANES, F32)], tm)
    y_scan, ssd_ck = ssd_scan_fwd(xs, bm, cm, dt, a_log_p, d_p)
    (y_ssd,) = rowwise_fwd("ssd_post", ssd_post_fn, [y_scan, u_z], [ssd_norm_g], [(SSD_WIDTH, BF16)], tm)

    rwkv_pre_rows = lambda: [Rows(u_rkv, shifts=(1,)), Rows(u_pg, shifts=(1,)), Rows(u_sm, 2 * LANES, 0, shifts=(1,))]
    rwkv_pre_params = [mu_rkv, mu_pg, mu_pwa, rwkv_w0, w2p, rwkv_a0, a2p, g2_f, rwkv_k_k, rwkv_k_a]
    forward_a, token_a = forward_start("gather_attn", gather_a, after=y_ssd)
    r_, lw_, k_, v_, kap_, b_, gate_ = rowwise_fwd("rwkv_pre", rwkv_pre_fn, rwkv_pre_rows(), rwkv_pre_params,
                                                   [(RWKV_WIDTH, F32)] * 7, tm_rwkv, deps=[token_a])
    ys_r, rwkv_ck, rwkv_inv = rwkv_scan_fwd(r_, lw_, k_, v_, kap_, b_)
    forward_b, token_b = forward_start("gather_ffn", gather_b, after=ys_r)
    g_big.update(zip(late_a, exchange_wait("gather_attn_forward_wait", forward_a, after=token_b)))
    w_out_f = g_big["w_out"].reshape(d, d)
    wq_f, wk_f, wv_f, wo_f = (g_big[n].reshape(d, d) for n in ("xattn_wq", "xattn_wk", "xattn_wv", "xattn_wo"))
    rwkv_post_params = [rwkv_ln_w, rwkv_ln_b, r_k_row]
    (y_rwkv,) = rowwise_fwd("rwkv_post", rwkv_post_fn, [ys_r, r_, k_, v_, gate_], rwkv_post_params,
                            [(RWKV_WIDTH, BF16)], tm)
    ycat = jnp.concatenate([y_ssd, y_rwkv], axis=1)
    x1 = mm("out_proj", ycat, w_out_f, "nn", res=xt)

    (h2,) = rowwise_fwd("norm_x", rmsnorm_fn, [x1], [norm_x_g], [(d, BF16)], tm)
    (mn,) = rowwise_fwd("norm_mem", rmsnorm_fn, [memt], [norm_mem_g], [(d, BF16)], tm)
    q = mm("xattn_q", h2, wq_f, "nn", out_dtype=BF16)
    kx = mm("xattn_k", mn, wk_f, "nn")
    vx = mm("xattn_v", mn, wv_f, "nn")
    (o,) = rowwise_fwd("xattn", attn_fn, [q], [kx, vx], [(d, BF16)], tm)
    x2 = mm("xattn_o", o, wo_f, "nn", res=x1)

    (h3,) = rowwise_fwd("norm_ffn", rmsnorm_fn, [x2], [norm_ffn_g], [(d, BF16)], tm)
    w1_s, w2_g = exchange_wait("gather_ffn_forward_wait", forward_b, after=h3)
    w2_f = w2_g.reshape(D_FF, d)
    relu2_epi = lambda acc: (jnp.square(jnp.maximum(acc, 0.0)), jnp.maximum(acc, 0.0))
    hid, relu_a = mm("ffn_1", h3, w1_s, "nn", b_slabs=N_DEV, epi=relu2_epi, out_dtypes=[BF16, BF16])
    x3 = mm("ffn_2", hid, w2_f, "nn", res=x2)

    loss_blk, dx3, dx3_b, dg_final = loss_and_grad(x3, tgt, g_final, tm)

    grads = {}
    grads["ffn_w2"] = mm("d_ffn_w2", hid, dx3_b, "tn", out_dtype=BF16).reshape(N_DEV, D_FF // N_DEV, d)
    sc_w2, tok = exchange_start("scatter_ffn_w2_start", [grads["ffn_w2"]], "scatter")
    da = mm("d_hid", dx3_b, w2_f, "nt", dep=tok, epi=lambda acc, ra: (2.0 * acc * ra,), extras=[relu_a],
            out_dtypes=[BF16])
    grads["ffn_w1"] = mm("d_ffn_w1", h3, da, "tn", out_dtype=BF16, out_slabs=N_DEV)
    sc_w1, tok = exchange_start("scatter_ffn_w1_start", [grads["ffn_w1"]], "scatter")
    dh3 = mm("d_h3", da, w1_s, "nt", out_dtype=BF16, b_slabs=N_DEV, dep=tok)
    (dx2,), (dg_ffn,) = rowwise_bwd("norm_ffn_bwd", rmsnorm_fn, [x2], [norm_ffn_g], [[dh3]], tm, [F32], row_add=[dx3])

    grads["xattn_wo"] = mm("d_wo", o, dx2, "tn", out_dtype=BF16).reshape(N_DEV, d // N_DEV, d)
    sc_wo, tok = exchange_start("scatter_wo_start", [grads["xattn_wo"]], "scatter")
    d_o = mm("d_o", dx2, wo_f, "nt", out_dtype=BF16, dep=tok)
    (dq,), (dkx, dvx) = rowwise_bwd("xattn_bwd", attn_fn, [q], [kx, vx], [[d_o]], tm, [BF16])
    grads["xattn_wq"] = mm("d_wq", h2, dq, "tn", out_dtype=BF16).reshape(N_DEV, d // N_DEV, d)
    grads["xattn_wk"] = mm("d_wk", mn, dkx, "tn", out_dtype=BF16).reshape(N_DEV, d // N_DEV, d)
    grads["xattn_wv"] = mm("d_wv", mn, dvx, "tn", out_dtype=BF16).reshape(N_DEV, d // N_DEV, d)
    qkv = ("xattn_wq", "xattn_wk", "xattn_wv")
    sc_qkv, tok = exchange_start("scatter_qkv_start", [grads[n] for n in qkv], "scatter")
    dmn = mm("d_mn_v", dvx, wv_f, "nt", res=mm("d_mn_k", dkx, wk_f, "nt", dep=tok))
    _, (dg_mem,) = rowwise_bwd("norm_mem_bwd", rmsnorm_fn, [memt], [norm_mem_g], [[dmn]], tm, [None])
    dh2 = mm("d_h2", dq, wq_f, "nt", out_dtype=BF16, dep=dg_mem)
    (dx1,), (dg_x,) = rowwise_bwd("norm_x_bwd", rmsnorm_fn, [x1], [norm_x_g], [[dh2]], tm, [F32], row_add=[dx2])

    grads["w_out"] = mm("d_w_out", ycat, dx1, "tn", out_dtype=BF16).reshape(N_DEV, d // N_DEV, d)
    sc_wout, tok = exchange_start("scatter_w_out_start", [grads["w_out"]], "scatter")
    d_ycat = mm("d_ycat", dx1, w_out_f, "nt", out_dtype=BF16, dep=tok)

    (d_ys, d_r1, d_k1, d_v1, d_gate), (dln_w, dln_b, dr_k) = rowwise_bwd(
        "rwkv_post_bwd", rwkv_post_fn, [ys_r, r_, k_, v_, gate_], rwkv_post_params,
        [[Rows(d_ycat, RWKV_WIDTH, 1)]], tm, [BF16] * 5)
    d_r2, d_lw, d_k2, d_v2, d_kap, d_b = rwkv_scan_bwd(r_, lw_, k_, v_, kap_, b_, rwkv_ck, rwkv_inv, d_ys)
    (du_rkv, du_pg, du_pwa), rwkv_pg = rowwise_bwd(
        "rwkv_pre_bwd", rwkv_pre_fn, rwkv_pre_rows(), rwkv_pre_params,
        [[d_r1, d_r2], [d_lw], [d_k1, d_k2], [d_v1, d_v2], [d_kap], [d_b], [d_gate]], tm_rwkv, [BF16] * 3)
    dmu_rkv, dmu_pg, dmu_pwa, dw0, dw2p, da0, da2p, dg2, dk_k, dk_a = rwkv_pg

    (d_yscan, du_z), (dssd_norm_g,) = rowwise_bwd("ssd_post_bwd", ssd_post_fn, [y_scan, u_z], [ssd_norm_g],
                                                  [[Rows(d_ycat, SSD_WIDTH, 0)]], tm, [BF16, BF16])
    dxs, dbm, dcm, ddt2, da_log_p, dd_p = ssd_scan_bwd(xs, bm, cm, dt, a_log_p, d_p, ssd_ck, d_yscan)
    (du_xbc, du_dt), (dconv_w, dconv_b, ddt_bias_p) = rowwise_bwd(
        "ssd_pre_bwd", ssd_pre_fn, ssd_pre_rows(), ssd_pre_params,
        [[dxs], [dbm], [dcm], [ddt2[0], ddt2[1]]], tm, [BF16, BF16])
    du_sm = jnp.concatenate([du_pwa, du_dt], axis=1)

    dwt_z = mm("d_w_z", du_z, h1, "tn", out_dtype=BF16)
    dwt_xbc = mm("d_w_xbc", du_xbc, h1, "tn", out_dtype=BF16)
    dwt_rkv = mm("d_w_rkv", du_rkv, h1, "tn", out_dtype=BF16)
    dwt_pg = mm("d_w_pg", du_pg, h1, "tn", out_dtype=BF16)
    dwt_sm = mm("d_w_small", du_sm, h1, "tn", out_dtype=BF16)
    dwt_full = jnp.concatenate([dwt_z, dwt_xbc, dwt_sm[256:272], dwt_rkv, dwt_sm[0:96], dwt_sm[128:224], dwt_pg], axis=0)
    to_slabs = lambda g: jnp.transpose(g.reshape(g.shape[0], N_DEV, -1), (1, 0, 2))
    grads["w_in"] = dwt_full.reshape(N_DEV, D_IN // N_DEV, d)
    grads["ssd_conv_w"] = to_slabs(dconv_w)
    grads["rwkv_w2"] = to_slabs(dw2p[:96])
    grads["rwkv_a2"] = to_slabs(da2p[:96])
    grads["rwkv_g2"] = to_slabs(dg2)
    tail = ("w_in", "ssd_conv_w", "rwkv_w2", "rwkv_a2", "rwkv_g2")
    sc_tail, tok = exchange_start("scatter_tail_start", [grads[n] for n in tail], "scatter")
    dh1 = mm("d_h1_z", du_z, wt_z, "nn", dep=tok)
    dh1 = mm("d_h1_xbc", du_xbc, wt_xbc, "nn", res=dh1)
    dh1 = mm("d_h1_rkv", du_rkv, wt_rkv, "nn", res=dh1)
    dh1 = mm("d_h1_pg", du_pg, wt_pg, "nn", res=dh1)
    dh1 = mm("d_h1_small", du_sm, wt_sm, "nn", res=dh1)
    (dx,), (dg_mix,) = rowwise_bwd("norm_mix_bwd", rmsnorm_fn, [xt], [norm_mix_g], [[dh1]], tm, [F32], row_add=[dx1])

    dmu =jnp.concatenate([dmu_rkv, dmu_pwa[:, 0:96], dmu_pwa[:, 128:224], dmu_pg], axis=1)
    small_grads = {
        "norm_mix_g": dg_mix, "ssd_conv_b": dconv_b, "ssd_dt_bias": ddt_bias_p[:, :16], "ssd_a_log": da_log_p[:, :16],
        "ssd_d": dd_p[:, :16], "ssd_norm_g": dssd_norm_g, "rwkv_mu": dmu, "rwkv_w0": dw0, "rwkv_a0": da0,
        "rwkv_k_k": dk_k, "rwkv_k_a": dk_a, "rwkv_r_k": dr_k, "rwkv_ln_w": dln_w, "rwkv_ln_b": dln_b,
        "norm_x_g": dg_x, "norm_mem_g": dg_mem, "norm_ffn_g": dg_ffn, "final_norm_g": dg_final}

    packed = _pack_small(small_grads)
    assert sum(g.size for g in small_grads.values()) < packed.size, "no padding word left for the loss"
    packed = packed.at[-1, -1].set(loss_blk[0, 0])
    gather_small, tok = exchange_start("gather_small_start", [packed], "gather")
    received = {}
    for names, handle in ((("ffn_w2",), sc_w2), (("ffn_w1",), sc_w1), (("xattn_wo",), sc_wo), (qkv, sc_qkv),
                          (("w_out",), sc_wout)):
        received.update(zip(names, exchange_wait("scatter_" + names[0] + "_wait", handle, after=tok)))

    out_g, out_d, out_m, out_v = {}, {}, {}, {}

    def run_adamw(n, dep):
        shape = wts[n].shape
        two_d = lambda a: a.reshape(-1, shape[-1])
        if n == "w_in":
            recv = jnp.transpose(sum_slabs("sum_w_in", received[n]))[None]
        else:
            recv = received[n].reshape(N_DEV, -1, shape[-1])
        res = adamw("adamw_" + n, recv, two_d(wts[n]), two_d(mom_m[n]), two_d(mom_v[n]), dep=dep)
        out_g[n], out_d[n], out_m[n], out_v[n] = (r.reshape(shape) for r in res)
        return res[0]

    last = None
    for n in ("ffn_w2", "ffn_w1", "xattn_wo") + qkv + ("w_out",):
        last = run_adamw(n, last)
    received.update(zip(tail, exchange_wait("scatter_tail_wait", sc_tail, after=last)))
    for n in tail:
        last = run_adamw(n, last)
    (small_all,) = exchange_wait("gather_small_wait", gather_small, after=last)
    res = adamw("adamw_small", small_all, _pack_small(wts), _pack_small(mom_m), _pack_small(mom_v))
    shapes = {n: wts[n].shape for n in _SMALL}
    for dst, packed in zip((out_g, out_d, out_m, out_v), res):
        dst.update(_unpack_small(packed, shapes))

    loss = jnp.sum(small_all[:, -1, -1])
    return (loss, dx[None], *[out_g[n] for n in _WEIGHTS], *[out_d[n] for n in _WEIGHTS],
            *[out_m[n] for n in _WEIGHTS], *[out_v[n] for n in _WEIGHTS])
```

```python
import functools

import jax
import jax.numpy as jnp
from jax import lax
from jax.experimental import pallas as pl
from jax.experimental.pallas import tpu as pltpu

F32 = jnp.float32
BF16 = jnp.bfloat16

N_DEV = 8
D_MODEL = 2048
NORM_EPS = 1e-6
SSD_WIDTH = 1024
SSD_HEAD_DIM = 64
SSD_STATE = 128
SSD_CHUNK = 128
SSD_HEADS_PER_GROUP = 8
RWKV_WIDTH = 1024
RWKV_HEADS = 16
RWKV_HEAD_DIM = 64
RWKV_LN_EPS = 64e-5
RWKV_CHUNK = 64
RWKV_HEADS_PER_STEP = 16
XATTN_HEADS = 4
XATTN_HEAD_DIM = 512
D_FF = 8192
LANES = 128
SUBLANES = 8
VMEM_LIMIT = 56 * 1024 * 1024

ADAM_LR = 0.001
ADAM_B1 = 0.9
ADAM_B2 = 0.999
ADAM_EPS = 1e-08
ADAM_WD = 0.01
ADAM_STEP = 10

_DN = {"nn": ((1,), (0,)), "nt": ((1,), (1,)), "tn": ((0,), (0,))}


def _dg(a, b, mode):
    (ca,), (cb,) = _DN[mode]
    dn = (((ca + 1,), (cb + 1,)), ((0,), (0,))) if a.ndim == 3 else (((ca,), (cb,)), ((), ()))
    return lax.dot_general(a, b, dn, preferred_element_type=F32)


@functools.partial(jax.custom_vjp, nondiff_argnums=(2,))
def bdot(a, b, mode):
    return _dg(a.astype(BF16), b.astype(BF16), mode)


def _bdot_fwd(a, b, mode):
    return bdot(a, b, mode), (a, b)


def _bdot_bwd(mode, res, g):
    a, b = res
    ab, bb, gb = a.astype(BF16), b.astype(BF16), g.astype(BF16)
    if mode == "nn":
        da, db = _dg(gb, bb, "nt"), _dg(ab, gb, "tn")
    elif mode == "nt":
        da, db = _dg(gb, bb, "nn"), _dg(gb, ab, "tn")
    else:
        da, db = _dg(bb, gb, "nt"), _dg(ab, gb, "nn")
    return da.astype(a.dtype), db.astype(b.dtype)


bdot.defvjp(_bdot_fwd, _bdot_bwd)


def _split2(x):
    hi = x.astype(BF16)
    return hi, (x - hi.astype(F32)).astype(BF16)


def _dot01(x, m01):
    hi, lo = _split2(x)
    return _dg(hi, m01, "nn") + _dg(lo, m01, "nn")


def _exact_dot_impl(a, b, mode, exact):
    if exact == "a":
        ae = a.astype(BF16)
        return sum(_dg(ae, part, mode) for part in _split2(b))
    be = b.astype(BF16)
    return sum(_dg(part, be, mode) for part in _split2(a))


@functools.partial(jax.custom_vjp, nondiff_argnums=(2, 3))
def exact_dot(a, b, mode, exact):
    return _exact_dot_impl(a, b, mode, exact)


def _exact_dot_fwd(a, b, mode, exact):
    return _exact_dot_impl(a, b, mode, exact), (a, b)


def _exact_dot_bwd(mode, exact, res, g):
    a, b = res
    if exact == "a":
        db = {"nn": lambda: _exact_dot_impl(a, g, "tn", "a"), "nt": lambda: _exact_dot_impl(g, a, "tn", "b"),
              "tn": lambda: _exact_dot_impl(a, g, "nn", "a")}[mode]()
        return jnp.zeros_like(a), db
    da = {"nn": lambda: _exact_dot_impl(g, b, "nt", "b"), "nt": lambda: _exact_dot_impl(g, b, "nn", "b"),
          "tn": lambda: _exact_dot_impl(b, g, "nt", "a")}[mode]()
    return da, jnp.zeros_like(b)


exact_dot.defvjp(_exact_dot_fwd, _exact_dot_bwd)


def _head_indicator(width, heads, transpose):
    hd = width // heads
    shape = (LANES, width) if transpose else (width, LANES)
    lane = lax.broadcasted_iota(jnp.int32, shape, 1 if not transpose else 0)
    pos = lax.broadcasted_iota(jnp.int32, shape, 0 if not transpose else 1)
    return ((pos >= lane * hd) & (pos < lane * hd + hd)).astype(BF16)


@jax.custom_vjp
def head_sum(x):
    w = x.shape[-1]
    e = _head_indicator(w, w // RWKV_HEAD_DIM, False)
    et = _head_indicator(w, w // RWKV_HEAD_DIM, True)
    return _dot01(_dot01(x, e), et)


head_sum.defvjp(lambda x: (head_sum(x), None), lambda _, g: (head_sum(g),))


def rmsnorm_fn(x, g):
    y = x * lax.rsqrt(jnp.mean(x * x, axis=-1, keepdims=True) + NORM_EPS)
    return ((y * g).astype(BF16),)


def cast_fn(x):
    return (x.astype(BF16),)


def ssd_pre_fn(xbc, xbc1, xbc2, xbc3, dt_raw, conv_w, conv_b, dt_bias):
    c = conv_w[3:4] * xbc + conv_w[2:3] * xbc1 + conv_w[1:2] * xbc2 + conv_w[0:1] * xbc3 + conv_b
    act = c * jax.nn.sigmoid(c)
    dt = jax.nn.softplus(dt_raw + dt_bias)
    return act[:, :SSD_WIDTH], act[:, SSD_WIDTH:SSD_WIDTH + 256], act[:, SSD_WIDTH + 256:], dt


def ssd_post_fn(yscan, z, norm_g):
    y = yscan * (z * jax.nn.sigmoid(z))
    half = SSD_WIDTH // 2
    parts = []
    for g in range(2):
        yg = y[:, g * half:(g + 1) * half]
        parts.append(yg * lax.rsqrt(jnp.mean(yg * yg, axis=-1, keepdims=True) + NORM_EPS))
    return ((jnp.concatenate(parts, axis=-1) * norm_g).astype(BF16),)


def rwkv_pre_fn(rkv, rkv_p, pg, pg_p, pwa, pwa_p, mu_rkv, mu_pg, mu_pwa, w0, w2p, a0, a2p, g2, k_k, k_a):
    w = RWKV_WIDTH
    rkv = rkv + (rkv_p - rkv) * mu_rkv
    pg = pg + (pg_p - pg) * mu_pg
    pwa = pwa + (pwa_p - pwa) * mu_pwa
    r, k, v = rkv[:, :w], rkv[:, w:2 * w], rkv[:, 2 * w:]
    pw, pa = pwa[:, :LANES], pwa[:, LANES:]
    w_log = -jax.nn.softplus(-(w0 + bdot(jnp.tanh(pw), w2p, "nn"))) - 0.5
    lw = -jnp.exp(w_log)
    iclr = jax.nn.sigmoid(a0 + bdot(pa, a2p, "nn"))
    gate = bdot(jax.nn.sigmoid(pg), g2, "nn")
    kk = k * k_k
    kap = kk * lax.rsqrt(jnp.maximum(head_sum(kk * kk), 1e-24))
    k_mod = k * (1.0 + (iclr - 1.0) * k_a)
    return r, lw, k_mod, v, kap, kap * iclr, gate


def rwkv_post_fn(ys, r, k_mod, v, gate, ln_w, ln_b, r_k):
    inv_n = 1.0 / RWKV_HEAD_DIM
    mean = head_sum(ys) * inv_n
    yc = ys - mean
    var = head_sum(yc * yc) * inv_n
    yn = yc * lax.rsqrt(var + RWKV_LN_EPS) * ln_w + ln_b
    bonus = head_sum(r * k_mod * r_k) * v
    return (((yn + bonus) * gate).astype(BF16),)


def attn_fn(q, kx, vx):
    outs = []
    for h in range(XATTN_HEADS):
        sl = slice(h * XATTN_HEAD_DIM, (h + 1) * XATTN_HEAD_DIM)
        s = bdot(q[:, sl], kx[:, sl], "nt") * (XATTN_HEAD_DIM ** -0.5)
        s = s - jnp.max(s, axis=-1, keepdims=True)
        p = jnp.exp(s)
        p = p / jnp.sum(p, axis=-1, keepdims=True)
        outs.append(bdot(p, vx[:, sl], "nn"))
    return (jnp.concatenate(outs, axis=-1).astype(BF16),)


def loss_fn(x, tgt, g):
    y = x * lax.rsqrt(jnp.mean(x * x, axis=-1, keepdims=True) + NORM_EPS) * g
    err = jnp.square(y - tgt)
    return 0.5 * jnp.sum(jnp.mean(err, axis=-1, keepdims=True), axis=0, keepdims=True)


def _tri_masks(n):
    row = lax.broadcasted_iota(jnp.int32, (n, n), 0)
    col = lax.broadcasted_iota(jnp.int32, (n, n), 1)
    return col <= row, col < row, row == col


@jax.custom_vjp
def unit_lower_inverse(a):
    c = a.shape[-1]
    eye = _tri_masks(c)[2].astype(F32)
    m = -a
    inv = eye + m
    n = 1
    while n * 2 < c:
        m = bdot(m, m, "nn")
        inv = bdot(inv, eye + m, "nn")
        n *= 2
    return inv


def _unit_lower_inverse_fwd(a):
    inv = unit_lower_inverse(a)
    return inv, inv


def _unit_lower_inverse_bwd(inv, g):
    return (-bdot(bdot(inv, g, "tn"), inv, "nt"),)


unit_lower_inverse.defvjp(_unit_lower_inverse_fwd, _unit_lower_inverse_bwd)


@jax.custom_vjp
def known_inverse(a, inv):
    return inv


known_inverse.defvjp(lambda a, inv: (inv, inv),
                     lambda inv, g: (_unit_lower_inverse_bwd(inv, g)[0], jnp.zeros_like(inv)))


def rwkv_chunk_fn(st0, r, lw, k, v, kap, b, inv=None):
    h, c = r.shape[0], r.shape[1]
    incl, strict, _ = _tri_masks(c)
    cum = exact_dot(jnp.broadcast_to(incl.astype(F32), (h, c, c)), lw, "nn", "a")
    g_in = jnp.exp(cum)
    g_prev = jnp.exp(cum - lw)
    g_inv = jnp.exp(-cum)
    g_end = jnp.exp(cum[:, c - 1:c, :] - cum)
    kap_t, k_t, b_t, r_t = kap * g_prev, k * g_inv, b * g_inv, r * g_in
    a_ub = jnp.where(strict, bdot(kap_t, b_t, "nt"), 0.0)
    a_vk = jnp.where(strict, bdot(kap_t, k_t, "nt"), 0.0)
    rhs = -(bdot(kap_t, st0, "nt") + bdot(a_vk, v, "nn"))
    inv = unit_lower_inverse(a_ub) if inv is None else known_inverse(a_ub, inv)
    u = bdot(inv, rhs, "nn")
    y = (bdot(r_t, st0, "nt")
         + bdot(jnp.where(incl, bdot(r_t, k_t, "nt"), 0.0), v, "nn")
         + bdot(jnp.where(incl, bdot(r_t, b_t, "nt"), 0.0), u, "nn"))
    st1 = jnp.exp(cum[:, c - 1:c, :]) * st0 + bdot(v, k * g_end, "tn") + bdot(u, b * g_end, "tn")
    return y, st1, inv


def ssd_chunk_fn(group, h0, xs, bm, cm, dt, a_log, d_skip):
    q, nh = xs.shape[0], SSD_HEADS_PER_GROUP
    causal, _, _ = _tri_masks(q)
    a_row = -jnp.exp(a_log)
    cs_all = exact_dot(causal.astype(F32), dt * a_row, "nn", "a")
    cs_t = cs_all.T
    lanes = range(group * nh, (group + 1) * nh)
    cs = jnp.stack([cs_all[:, hl:hl + 1] for hl in lanes])
    cs_row = jnp.stack([cs_t[hl:hl + 1, :] for hl in lanes])
    dt_h = jnp.stack([dt[:, hl:hl + 1] for hl in lanes])
    d_h = jnp.stack([d_skip[:, hl:hl + 1] for hl in lanes])
    x = _stack_lanes(xs, nh)
    h0s = _stack_rows(h0, nh)
    lmat = jnp.where(causal, jnp.exp(jnp.where(causal, cs - cs_row, 0.0)), 0.0)
    cb = bdot(cm, bm, "nt")
    xdt = x * dt_h
    cl = cs[:, q - 1:q, :]
    cm_b = jnp.broadcast_to(cm, (nh,) + cm.shape)
    bm_b = jnp.broadcast_to(bm, (nh,) + bm.shape)
    y = bdot(cb * lmat, xdt, "nn") + bdot(cm_b, h0s, "nt") * jnp.exp(cs) + x * d_h
    h1 = h0s * jnp.exp(cl) + bdot(xdt * jnp.exp(cl - cs), bm_b, "tn")
    return jnp.concatenate([y[e] for e in range(nh)], axis=-1), jnp.concatenate([h1[e] for e in range(nh)], axis=0)


class Rows:
    def __init__(self, arr, w=None, cb=0, shifts=()):
        self.arr, self.w, self.cb, self.shifts = arr, (arr.shape[1] if w is None else w), cb, tuple(shifts)


def _as_rows(x):
    return x if isinstance(x, Rows) else Rows(x)


def _shift_down(x, halo, k):
    rolled = pltpu.roll(x, k, 0)
    first = rolled[0:SUBLANES]
    rid = lax.broadcasted_iota(jnp.int32, first.shape, 0)
    patched = jnp.where(rid < k, pltpu.roll(halo, k, 0), first)
    return jnp.concatenate([patched, rolled[SUBLANES:]], axis=0)


def _shift_up(g, carry, k):
    tm = g.shape[0]
    rolled = pltpu.roll(g, tm - k, 0)
    last = rolled[tm - SUBLANES:]
    rid = lax.broadcasted_iota(jnp.int32, last.shape, 0)
    patched = jnp.where(rid >= SUBLANES - k, pltpu.roll(carry, SUBLANES - k, 0), last)
    return jnp.concatenate([rolled[:tm - SUBLANES], patched], axis=0)


def _params():
    return pltpu.CompilerParams(vmem_limit_bytes=VMEM_LIMIT)


def _load_rows(refs, pos, rins, first_block):
    vals = []
    for r in rins:
        x = refs[pos][...].astype(F32) if refs[pos].dtype != F32 else refs[pos][...]
        pos += 1
        vals.append(x)
        if r.shifts:
            halo = refs[pos][...]
            pos += 1
            halo = jnp.where(first_block, jnp.zeros_like(halo), halo)
            for k in r.shifts:
                vals.append(_shift_down(x, halo, k))
    return vals, pos


def _row_specs(rins, tm, blk):
    specs, args = [], []
    for r in rins:
        specs.append(pl.BlockSpec((tm, r.w), lambda i, cb=r.cb: (blk(i), cb)))
        args.append(r.arr)
        if r.shifts:
            per = tm // SUBLANES
            specs.append(pl.BlockSpec((SUBLANES, r.w), lambda i, cb=r.cb: (jnp.maximum(blk(i) * per - 1, 0), cb)))
            args.append(r.arr)
    return specs, args


def rowwise_fwd(name, fn, rins, params, outs, tm, deps=()):
    rins = [_as_rows(r) for r in rins]
    t = rins[0].arr.shape[0]
    tm = min(tm, t)
    nb = t // tm
    specs, args = _row_specs(rins, tm, lambda i: i)
    for p in params:
        specs.append(pl.BlockSpec(p.shape, lambda i: (0, 0)))
        args.append(p)
    for dep in deps:
        specs.append(pl.BlockSpec(memory_space=pl.ANY))
        args.append(dep)
    n_in = len(args)

    def body(*refs):
        vals, pos = _load_rows(refs, 0, rins, pl.program_id(0) == 0)
        pv = [refs[pos + j][...] for j in range(len(params))]
        res = fn(*vals, *pv)
        for o_ref, o in zip(refs[n_in:], res):
            o_ref[...] = o.astype(o_ref.dtype)

    return pl.pallas_call(
        body, name=name, grid=(nb,), in_specs=specs,
        out_specs=[pl.BlockSpec((tm, w), lambda i: (i, 0)) for w, _ in outs],
        out_shape=[jax.ShapeDtypeStruct((t, w), dt) for w, dt in outs],
        compiler_params=_params(),
    )(*args)


def rowwise_bwd(name, fn, rins, params, cts, tm, grad_dtypes, row_add=None):
    rins = [_as_rows(r) for r in rins]
    cts = [[_as_rows(c) for c in lst] for lst in cts]
    row_add = [_as_rows(a) for a in (row_add or [])]
    t = rins[0].arr.shape[0]
    tm = min(tm, t)
    nb = t // tm
    rev = lambda i: nb - 1 - i
    specs, args = _row_specs(rins, tm, rev)
    for p in params:
        specs.append(pl.BlockSpec(p.shape, lambda i: (0, 0)))
        args.append(p)
    flat_cts = [c for lst in cts for c in lst] + row_add
    for c in flat_cts:
        specs.append(pl.BlockSpec((tm, c.w), lambda i, cb=c.cb: (rev(i), cb)))
        args.append(c.arr)
    n_in = len(args)
    want = [i for i, d in enumerate(grad_dtypes) if d is not None]
    out_specs = [pl.BlockSpec((tm, rins[i].w), lambda i_: (rev(i_), 0)) for i in want]
    out_shape = [jax.ShapeDtypeStruct((t, rins[i].w), grad_dtypes[i]) for i in want]
    out_specs += [pl.BlockSpec(p.shape, lambda i: (0, 0)) for p in params]
    out_shape += [jax.ShapeDtypeStruct(p.shape, F32) for p in params]
    n_out = len(out_shape)
    scratch = [pltpu.VMEM((SUBLANES, r.w), F32) for r in rins for _ in r.shifts]

    def body(*refs):
        i = pl.program_id(0)
        vals, pos = _load_rows(refs, 0, rins, rev(i) == 0)
        pv = [refs[pos + j][...] for j in range(len(params))]
        pos += len(params)
        outs, vjp = jax.vjp(fn, *vals, *pv)
        ct_vals = []
        for o, lst in zip(outs, cts):
            acc = None
            for _ in lst:
                cv = refs[pos][...].astype(F32)
                pos += 1
                acc = cv if acc is None else acc + cv
            ct_vals.append(acc.astype(o.dtype))
        adds = [refs[pos + j][...].astype(F32) for j in range(len(row_add))]
        grads = vjp(tuple(ct_vals))
        out_refs = refs[n_in:n_in + n_out]
        carry_refs = refs[n_in + n_out:]

        @pl.when(i == 0)
        def _():
            for cr in carry_refs:
                cr[...] = jnp.zeros_like(cr)
            for pr in out_refs[len(want):]:
                pr[...] = jnp.zeros_like(pr)

        gi, ci, oi = 0, 0, 0
        for idx, r in enumerate(rins):
            d = grads[gi]
            gi += 1
            for k in r.shifts:
                dk = grads[gi]
                gi += 1
                d = d + _shift_up(dk, carry_refs[ci][...], k)
                carry_refs[ci][...] = dk[0:SUBLANES]
                ci += 1
            if idx == 0:
                for a in adds:
                    d = d + a
            if grad_dtypes[idx] is not None:
                out_refs[oi][...] = d.astype(out_refs[oi].dtype)
                oi += 1
        for pr, gp in zip(out_refs[len(want):], grads[gi:]):
            pr[...] += gp

    res = pl.pallas_call(
        body, name=name, grid=(nb,), in_specs=specs, out_specs=out_specs, out_shape=out_shape,
        scratch_shapes=scratch, compiler_params=_params(),
    )(*args)
    return res[:len(want)], res[len(want):]


def _pick(n, pref):
    for c in pref:
        if n % c == 0:
            return c
    return n


MM_VMEM_BUDGET = 40 * 1024 * 1024
MM_PEAK_FLOPS = 0.9e15
MM_HBM_BYTES_PER_S = 3.0e12
MM_STEP_SECONDS = 0.35e-6


def _mm_tiles(m, n, k, size_a, size_b, size_out, size_res, single_k):
    best = None
    for tk in sorted({c for c in (k, 2048, 1024, 512, 256, 128) if c <= 2048 and k % c == 0}, reverse=True):
        for tm in sorted({c for c in (m, 1024, 512, 256, 128) if c <= 1024 and m % c == 0}, reverse=True):
            for tn in sorted({c for c in (n, 2048, 1536, 1024, 768, 512, 384, 256, 128) if c <= 2048 and n % c == 0},
                             reverse=True):
                nk = k // tk
                vmem = 2 * (tm * tk * size_a + tk * tn * size_b + tm * tn * (size_out + size_res))
                vmem += tm * tn * 4 * (2 if nk > 1 or not single_k else 1)
                vmem += (tm * tk * 2 if size_a > 2 else 0) + (tk * tn * 2 if size_b > 2 else 0)
                if vmem > MM_VMEM_BUDGET:
                    continue
                steps = (m // tm) * (n // tn) * nk
                a_reads = 1 if (nk == 1 and single_k) else n // tn
                traffic = m * k * size_a * a_reads + k * n * size_b * (m // tm) + m * n * (size_out + size_res)
                cost = max(2.0 * m * n * k / MM_PEAK_FLOPS, traffic / MM_HBM_BYTES_PER_S) + steps * MM_STEP_SECONDS
                if best is None or cost < best[0]:
                    best = (cost, tm, tn, tk)
    return best[1:]


def mm(name, a, b, mode, out_dtype=F32, res=None, b_slabs=None, out_slabs=None, dep=None, epi=None, extras=(),
       out_dtypes=None):
    if mode == "tn":
        k_dim, m_dim = a.shape
    else:
        m_dim, k_dim = a.shape
    if b_slabs:
        n_dim = b.shape[0] * b.shape[2] if mode == "nn" else b.shape[1]
    else:
        n_dim = b.shape[0] if mode == "nt" else b.shape[1]
    n_slabs = out_slabs or (b_slabs if (b_slabs and mode == "nn") else 1)
    k_slabs = b_slabs if (b_slabs and mode == "nt") else 1
    if epi is None:
        out_dtypes = [out_dtype]
        if res is None:
            epi = lambda acc: (acc,)
        else:
            extras, epi = [res], lambda acc, r: (acc + r,)
    tm, tn, tk = _mm_tiles(m_dim, n_dim // n_slabs, k_dim // k_slabs, a.dtype.itemsize, b.dtype.itemsize,
                           sum(jnp.dtype(dt).itemsize for dt in out_dtypes), sum(e.dtype.itemsize for e in extras),
                           single_k=(k_slabs == 1))
    nji = n_dim // n_slabs // tn
    nki = k_dim // k_slabs // tk
    nblk = lambda js, j: js * nji + j
    kblk = lambda ks, k: ks * nki + k
    if mode == "tn":
        a_spec = pl.BlockSpec((tk, tm), lambda i, js, j, ks, k: (kblk(ks, k), i))
    else:
        a_spec = pl.BlockSpec((tm, tk), lambda i, js, j, ks, k: (i, kblk(ks, k)))
    if b_slabs and mode == "nn":
        b_spec = pl.BlockSpec((None, tk, tn), lambda i, js, j, ks, k: (js, k, j))
    elif b_slabs and mode == "nt":
        b_spec = pl.BlockSpec((None, tn, tk), lambda i, js, j, ks, k: (ks, nblk(js, j), k))
    elif mode == "nt":
        b_spec = pl.BlockSpec((tn, tk), lambda i, js, j, ks, k: (nblk(js, j), kblk(ks, k)))
    else:
        b_spec = pl.BlockSpec((tk, tn), lambda i, js, j, ks, k: (kblk(ks, k), nblk(js, j)))
    specs, args = [a_spec, b_spec], [a, b]
    for e in extras:
        specs.append(pl.BlockSpec((tm, tn), lambda i, js, j, ks, k: (i, nblk(js, j))))
        args.append(e)
    if dep is not None:
        specs.append(pl.BlockSpec(memory_space=pl.ANY))
        args.append(dep)
    if out_slabs:
        o_specs = [pl.BlockSpec((None, tm, tn), lambda i, js, j, ks, k: (js, i, j))]
        o_shapes = [jax.ShapeDtypeStruct((out_slabs, m_dim, n_dim // out_slabs), out_dtypes[0])]
    else:
        o_specs = [pl.BlockSpec((tm, tn), lambda i, js, j, ks, k: (i, nblk(js, j))) for _ in out_dtypes]
        o_shapes = [jax.ShapeDtypeStruct((m_dim, n_dim), dt) for dt in out_dtypes]

    one_k_step = k_slabs * nki == 1
    n_in, n_out = len(args), len(out_dtypes)

    def body(*refs):
        a_ref, b_ref = refs[0], refs[1]
        part = _dg(a_ref[...].astype(BF16), b_ref[...].astype(BF16), mode)

        def finish(acc):
            outs = epi(acc, *[refs[2 + j][...].astype(F32) for j in range(len(extras))])
            for o_ref, o in zip(refs[n_in:n_in + n_out], outs):
                o_ref[...] = o.astype(o_ref.dtype)

        if one_k_step:
            finish(part)
            return
        acc_ref = refs[n_in + n_out]
        ks, kk = pl.program_id(3), pl.program_id(4)

        @pl.when((ks == 0) & (kk == 0))
        def _():
            acc_ref[...] = part

        @pl.when((ks > 0) | (kk > 0))
        def _():
            acc_ref[...] += part

        pl.when((ks == k_slabs - 1) & (kk == nki - 1))(lambda: finish(acc_ref[...]))

    grid = (m_dim // tm, n_slabs, nji, k_slabs, nki)
    scratch = [] if one_k_step else [pltpu.VMEM((tm, tn), F32)]
    out = pl.pallas_call(
        body, name=name, grid=grid, in_specs=specs, out_specs=o_specs, out_shape=o_shapes, scratch_shapes=scratch,
        compiler_params=pltpu.CompilerParams(
            dimension_semantics=("parallel", "parallel", "parallel", "arbitrary", "arbitrary"),
            vmem_limit_bytes=VMEM_LIMIT),
    )(*args)
    return out[0] if n_out == 1 else out


def _stack_lanes(x, n):
    w = x.shape[1] // n
    return jnp.stack([x[:, i * w:(i + 1) * w] for i in range(n)])


def _stack_rows(x, n):
    w = x.shape[0] // n
    return jnp.stack([x[i * w:(i + 1) * w, :] for i in range(n)])


def rwkv_scan_fwd(r, lw, k, v, kap, b):
    t = r.shape[0]
    c, hps, hd = min(RWKV_CHUNK, t), RWKV_HEADS_PER_STEP, RWKV_HEAD_DIM
    nc, ng, wl = t // c, RWKV_HEADS // hps, hps * hd
    spec = pl.BlockSpec((c, wl), lambda g, ci: (ci, g))

    def body(r_ref, lw_ref, k_ref, v_ref, kap_ref, b_ref, y_ref, ck_ref, inv_ref, st_ref):
        @pl.when(pl.program_id(1) == 0)
        def _():
            st_ref[...] = jnp.zeros_like(st_ref)

        st = st_ref[...]
        ck_ref[...] = st
        ins = [x[...] for x in (r_ref, lw_ref, k_ref, v_ref, kap_ref, b_ref)]
        y, st1, inv = rwkv_chunk_fn(_stack_rows(st, hps), *[_stack_lanes(x, hps) for x in ins])
        y_ref[...] = jnp.concatenate([y[h] for h in range(hps)], axis=-1)
        st_ref[...] = jnp.concatenate([st1[h] for h in range(hps)], axis=0)
        inv_ref[...] = jnp.concatenate([inv[h] for h in range(hps)], axis=0)

    return pl.pallas_call(
        body, name="rwkv_scan_fwd", grid=(ng, nc), in_specs=[spec] * 6,
        out_specs=[spec, pl.BlockSpec((None, wl, hd), lambda g, ci: (ci, g, 0)),
                   pl.BlockSpec((None, hps * c, c), lambda g, ci: (ci, g, 0))],
        out_shape=[jax.ShapeDtypeStruct((t, RWKV_WIDTH), F32), jax.ShapeDtypeStruct((nc, RWKV_WIDTH, hd), F32),
                   jax.ShapeDtypeStruct((nc, RWKV_HEADS * c, c), F32)],
        scratch_shapes=[pltpu.VMEM((wl, hd), F32)], compiler_params=_params(),
    )(r, lw, k, v, kap, b)


def rwkv_scan_bwd(r, lw, k, v, kap, b, ck, inv_ck, dy):
    t = r.shape[0]
    c, hps, hd = min(RWKV_CHUNK, t), RWKV_HEADS_PER_STEP, RWKV_HEAD_DIM
    nc, ng, wl = t // c, RWKV_HEADS // hps, hps * hd
    spec = pl.BlockSpec((c, wl), lambda g, ci: (nc - 1 - ci, g))

    def body(r_ref, lw_ref, k_ref, v_ref, kap_ref, b_ref, ck_ref, inv_ref, dy_ref, *rest):
        out_refs, dst_ref = rest[:6], rest[6]

        @pl.when(pl.program_id(1) == 0)
        def _():
            dst_ref[...] = jnp.zeros_like(dst_ref)

        ins = [x[...] for x in (r_ref, lw_ref, k_ref, v_ref, kap_ref, b_ref)]
        dyv, ck, dst = dy_ref[...].astype(F32), ck_ref[...], dst_ref[...]
        chunk = lambda *a: rwkv_chunk_fn(*a, inv=_stack_rows(inv_ref[...], hps))[:2]
        _, vjp = jax.vjp(chunk, _stack_rows(ck, hps), *[_stack_lanes(x, hps) for x in ins])
        grads = vjp((_stack_lanes(dyv, hps), _stack_rows(dst, hps)))
        dst_ref[...] = jnp.concatenate([grads[0][h] for h in range(hps)], axis=0)
        for j in range(6):
            out_refs[j][...] = jnp.concatenate([grads[1 + j][h] for h in range(hps)], axis=-1).astype(BF16)

    return pl.pallas_call(
        body, name="rwkv_scan_bwd", grid=(ng, nc),
        in_specs=[spec] * 6 + [pl.BlockSpec((None, wl, hd), lambda g, ci: (nc - 1 - ci, g, 0)),
                               pl.BlockSpec((None, hps * c, c), lambda g, ci: (nc - 1 - ci, g, 0)), spec],
        out_specs=[spec] * 6, out_shape=[jax.ShapeDtypeStruct((t, RWKV_WIDTH), BF16)] * 6,
        scratch_shapes=[pltpu.VMEM((wl, hd), F32)], compiler_params=_params(),
    )(r, lw, k, v, kap, b, ck, inv_ck, dy)


def _ssd_specs(q, blk):
    gw = SSD_WIDTH // 2
    return [pl.BlockSpec((q, gw), lambda g, ci: (blk(ci), g)),
            pl.BlockSpec((q, SSD_STATE), lambda g, ci: (blk(ci), g)),
            pl.BlockSpec((q, SSD_STATE), lambda g, ci: (blk(ci), g)),
            pl.BlockSpec((q, LANES), lambda g, ci: (blk(ci), 0)),
            pl.BlockSpec((1, LANES), lambda g, ci: (0, 0)),
            pl.BlockSpec((1, LANES), lambda g, ci: (0, 0))]


def ssd_scan_fwd(xs, bm, cm, dt, a_log, d_skip):
    t = xs.shape[0]
    q = min(SSD_CHUNK, t)
    nc, gw = t // q, SSD_WIDTH // 2

    def body(xs_ref, bm_ref, cm_ref, dt_ref, al_ref, d_ref, y_ref, ck_ref, h_ref):
        @pl.when(pl.program_id(1) == 0)
        def _():
            h_ref[...] = jnp.zeros_like(h_ref)

        ck_ref[...] = h_ref[...]
        args = (h_ref[...], xs_ref[...], bm_ref[...], cm_ref[...], dt_ref[...], al_ref[...], d_ref[...])
        g = pl.program_id(0)

        @pl.when(g == 0)
        def _():
            y, h1 = ssd_chunk_fn(0, *args)
            y_ref[...] = y
            h_ref[...] = h1

        @pl.when(g == 1)
        def _():
            y, h1 = ssd_chunk_fn(1, *args)
            y_ref[...] = y
            h_ref[...] = h1

    return pl.pallas_call(
        body, name="ssd_scan_fwd", grid=(2, nc), in_specs=_ssd_specs(q, lambda ci: ci),
        out_specs=[pl.BlockSpec((q, gw), lambda g, ci: (ci, g)),
                   pl.BlockSpec((None, gw, SSD_STATE), lambda g, ci: (ci, g, 0))],
        out_shape=[jax.ShapeDtypeStruct((t, SSD_WIDTH), F32), jax.ShapeDtypeStruct((nc, SSD_WIDTH, SSD_STATE), F32)],
        scratch_shapes=[pltpu.VMEM((gw, SSD_STATE), F32)], compiler_params=_params(),
    )(xs, bm, cm, dt, a_log, d_skip)


def ssd_scan_bwd(xs, bm, cm, dt, a_log, d_skip, ck, dy):
    t = xs.shape[0]
    q = min(SSD_CHUNK, t)
    nc, gw = t // q, SSD_WIDTH // 2
    rev = lambda ci: nc - 1 - ci

    def body(xs_ref, bm_ref, cm_ref, dt_ref, al_ref, d_ref, ck_ref, dy_ref,
             dxs_ref, dbm_ref, dcm_ref, ddt_ref, dal_ref, dd_ref, dh_ref):
        g, ci = pl.program_id(0), pl.program_id(1)

        @pl.when(ci == 0)
        def _():
            dh_ref[...] = jnp.zeros_like(dh_ref)

        @pl.when((ci == 0) & (g == 0))
        def _():
            dal_ref[...] = jnp.zeros_like(dal_ref)
            dd_ref[...] = jnp.zeros_like(dd_ref)

        args = (ck_ref[...], xs_ref[...], bm_ref[...], cm_ref[...], dt_ref[...], al_ref[...], d_ref[...])

        def run(group):
            _, vjp = jax.vjp(functools.partial(ssd_chunk_fn, group), *args)
            dh0, dxs, dbm, dcm, ddt, dal, dd = vjp((dy_ref[...].astype(F32), dh_ref[...]))
            dh_ref[...] = dh0
            dxs_ref[...] = dxs.astype(BF16)
            dbm_ref[...] = dbm.astype(BF16)
            dcm_ref[...] = dcm.astype(BF16)
            ddt_ref[...] = ddt
            dal_ref[...] += dal
            dd_ref[...] += dd

        pl.when(g == 0)(lambda: run(0))
        pl.when(g == 1)(lambda: run(1))

    in_specs = _ssd_specs(q, rev) + [pl.BlockSpec((None, gw, SSD_STATE), lambda g, ci: (rev(ci), g, 0)),
                                     pl.BlockSpec((q, gw), lambda g, ci: (rev(ci), g))]
    return pl.pallas_call(
        body, name="ssd_scan_bwd", grid=(2, nc), in_specs=in_specs,
        out_specs=[pl.BlockSpec((q, gw), lambda g, ci: (rev(ci), g)),
                   pl.BlockSpec((q, SSD_STATE), lambda g, ci: (rev(ci), g)),
                   pl.BlockSpec((q, SSD_STATE), lambda g, ci: (rev(ci), g)),
                   pl.BlockSpec((None, q, LANES), lambda g, ci: (g, rev(ci), 0)),
                   pl.BlockSpec((1, LANES), lambda g, ci: (0, 0)),
                   pl.BlockSpec((1, LANES), lambda g, ci: (0, 0))],
        out_shape=[jax.ShapeDtypeStruct((t, SSD_WIDTH), BF16), jax.ShapeDtypeStruct((t, 2 * SSD_STATE), BF16),
                   jax.ShapeDtypeStruct((t, 2 * SSD_STATE), BF16), jax.ShapeDtypeStruct((2, t, LANES), F32),
                   jax.ShapeDtypeStruct((1, LANES), F32), jax.ShapeDtypeStruct((1, LANES), F32)],
        scratch_shapes=[pltpu.VMEM((gw, SSD_STATE), F32)], compiler_params=_params(),
    )(xs, bm, cm, dt, a_log, d_skip, ck, dy)


def loss_and_grad(x, tgt, g, tm):
    t, d = x.shape
    tm = min(tm, t)
    nb = t // tm

    def body(x_ref, t_ref, g_ref, loss_ref, dx_ref, dxb_ref, dg_ref):
        @pl.when(pl.program_id(0) == 0)
        def _():
            loss_ref[...] = jnp.zeros_like(loss_ref)
            dg_ref[...] = jnp.zeros_like(dg_ref)

        val, vjp = jax.vjp(loss_fn, x_ref[...], t_ref[...], g_ref[...])
        dx, _, dg = vjp(jnp.ones((1, 1), F32))
        loss_ref[...] += jnp.broadcast_to(val, loss_ref.shape)
        dx_ref[...] = dx
        dxb_ref[...] = dx.astype(BF16)
        dg_ref[...] += dg

    row = pl.BlockSpec((tm, d), lambda i: (i, 0))
    one = pl.BlockSpec((1, d), lambda i: (0, 0))
    return pl.pallas_call(
        body, name="loss_and_grad", grid=(nb,), in_specs=[row, row, one],
        out_specs=[pl.BlockSpec((SUBLANES, LANES), lambda i: (0, 0)), row, row, one],
        out_shape=[jax.ShapeDtypeStruct((SUBLANES, LANES), F32), jax.ShapeDtypeStruct((t, d), F32),
                   jax.ShapeDtypeStruct((t, d), BF16), jax.ShapeDtypeStruct((1, d), F32)],
        compiler_params=_params(),
    )(x, tgt, g)


def adamw(name, recv, w, m, v, dep=None):
    rows, cols = w.shape
    n_slabs = recv.shape[0]
    recv_block_bytes = 4 * 1024 * 1024
    tm = _pick(rows, [c for c in (256, 128, 64, 32, 16, 8) if n_slabs * c * cols * 4 <= recv_block_bytes])
    c1 = 1.0 / (1.0 - ADAM_B1 ** ADAM_STEP)
    c2 = 1.0 / (1.0 - ADAM_B2 ** ADAM_STEP)

    n_dep = 0 if dep is None else 1

    def body(recv_ref, w_ref, m_ref, v_ref, *rest):
        g_ref, d_ref, nm_ref, nv_ref = rest[n_dep:]
        g = recv_ref[0].astype(F32)
        for p in range(1, n_slabs):
            g = g + recv_ref[p].astype(F32)
        nm =ADAM_B1 * m_ref[...] + (1.0 - ADAM_B1) * g
        nv = ADAM_B2 * v_ref[...] + (1.0 - ADAM_B2) * jnp.square(g)
        g_ref[...] = g
        nm_ref[...] = nm
        nv_ref[...] = nv
        d_ref[...] = -ADAM_LR * ((nm * c1) / (jnp.sqrt(nv * c2) + ADAM_EPS) + ADAM_WD * w_ref[...])

    blk = pl.BlockSpec((tm, cols), lambda i: (i, 0))
    return pl.pallas_call(
        body, name=name, grid=(rows // tm,),
        in_specs=[pl.BlockSpec((n_slabs, tm, cols), lambda i: (0, i, 0)), blk, blk, blk]
        + [pl.BlockSpec(memory_space=pl.ANY)] * n_dep,
        out_specs=[blk] * 4, out_shape=[jax.ShapeDtypeStruct((rows, cols), F32)] * 4,
        compiler_params=_params(),
    )(recv, w, m, v, *([] if dep is None else [dep]))


def _mesh_pos():
    return lax.axis_index("x"), lax.axis_index("y"), lax.axis_index("c")


def _peer(pos, mask):
    x, y, c = pos
    return (1 - x if mask & 4 else x, 1 - y if mask & 2 else y, 1 - c if mask & 1 else c)


def _linear(pos):
    return 4 * pos[0] + 2 * pos[1] + pos[2]


class Exchange:
    MASKS = {"gather": (1, 2, 3, 4, 5, 6, 7), "scatter": (1, 2, 3, 4, 5, 6, 7), "gather_chips": (1, 2, 4, 6),
             "forward": (2, 4, 6)}

    def __init__(self, xs, kind, lands=None):
        self.kind, self.masks = kind, self.MASKS[kind]
        self.xs = [] if kind == "forward" else list(xs)
        if kind == "forward":
            self.land_shape = [jax.ShapeDtypeStruct(l.shape, l.dtype) for l in lands]
        elif kind == "scatter":
            self.land_shape = [jax.ShapeDtypeStruct(x.shape, x.dtype) for x in xs]
        else:
            self.land_shape = [jax.ShapeDtypeStruct((N_DEV,) + x.shape, x.dtype) for x in xs]
        self.n = len(self.land_shape)
        copies = self.n * len(self.masks)
        self.sems = [pltpu.SemaphoreType.DMA((copies,)), pltpu.SemaphoreType.DMA((copies,)),
                     pltpu.SemaphoreType.DMA((self.n,))]

    def _copies(self, ins, outs, sems, landing):
        send_sems, recv_sems, local_sems = sems
        me = _mesh_pos()
        me_lin = _linear(me)
        local, remote = [], []
        for ti in range(self.n):
            if self.kind != "forward":
                src_mine = ins[ti].at[me_lin] if self.kind == "scatter" else ins[ti]
                local.append(pltpu.make_async_copy(src_mine, outs[ti].at[me_lin], local_sems.at[ti]))
            for j, mask in enumerate(self.masks):
                if self.kind == "forward":
                    peer = _peer(me, 1)
                    src = outs[ti].at[_linear(_peer(me, mask))]
                    dst = outs[ti].at[_linear(_peer(me, mask ^ 1 if landing else mask))]
                else:
                    peer = _peer(me, mask)
                    src = ins[ti].at[_linear(peer)] if self.kind == "scatter" else ins[ti]
                    dst = outs[ti].at[_linear(peer) if landing else me_lin]
                sem_index = ti * len(self.masks) + j
                remote.append(pltpu.make_async_remote_copy(
                    src_ref=src, dst_ref=dst, send_sem=send_sems.at[sem_index], recv_sem=recv_sems.at[sem_index],
                    device_id=peer, device_id_type=pl.DeviceIdType.MESH))
        return local, remote

    def start(self, ins, outs, sems):
        local, remote = self._copies(ins, outs, sems, landing=False)
        for cp in local + remote:
            cp.start()

    def finish(self, ins, outs, sems):
        local, remote = self._copies(ins, outs, sems, landing=True)
        for cp in remote:
            cp.wait_recv()
        for cp in remote:
            cp.wait_send()
        for cp in local:
            cp.wait()


def exchange_start(name, xs, kind, dep=None, lands=None):
    ex = Exchange(xs, kind, lands)
    hbm = pl.BlockSpec(memory_space=pltpu.HBM)
    sem = pl.BlockSpec(memory_space=pltpu.SEMAPHORE)
    if lands is None:
        lands = [lax.empty(s.shape, s.dtype) for s in ex.land_shape]
    n_src, n = len(ex.xs), ex.n
    n_inputs = n_src + n + (0 if dep is None else 1)

    def body(*refs):
        ins, lnd, sems, token = refs[:n_src], refs[n_src:n_src + n], refs[n_inputs:n_inputs + 3], refs[-1]
        ex.start(ins, lnd, sems)
        token[...] = jnp.zeros_like(token)

    res = pl.pallas_call(
        body, name=name, in_specs=[hbm] * (n_src + n) + ([] if dep is None else [pl.BlockSpec(memory_space=pl.ANY)]),
        out_specs=[sem] * 3 + [hbm] * (n_src + n) + [pl.BlockSpec(memory_space=pltpu.VMEM)],
        out_shape=ex.sems + [pltpu.HBM(x.shape, x.dtype) for x in ex.xs]
        + [pltpu.HBM(s.shape, s.dtype) for s in ex.land_shape] + [jax.ShapeDtypeStruct((SUBLANES, LANES), F32)],
        input_output_aliases={i: 3 + i for i in range(n_src + n)},
        compiler_params=pltpu.CompilerParams(has_side_effects=pltpu.SideEffectType.DATAFLOW_SIDE_EFFECTING),
    )(*[pltpu.with_memory_space_constraint(x, pltpu.HBM) for x in ex.xs + list(lands)],
      *([] if dep is None else [dep]))
    return (ex, res[:3], res[3:3 + n_src], res[3 + n_src:3 + n_src + n]), res[-1]


def exchange_wait(name, handles, after):
    ex, sems, srcs, lands = handles
    n_src, n = len(srcs), len(lands)
    hbm = pl.BlockSpec(memory_space=pltpu.HBM)
    sem = pl.BlockSpec(memory_space=pltpu.SEMAPHORE)

    def body(*refs):
        ins, lnd, sem_refs = refs[:n_src], refs[n_src:n_src + n], refs[n_src + n:n_src + n + 3]
        ex.finish(ins, lnd, sem_refs)

    res = pl.pallas_call(
        body, name=name, in_specs=[hbm] * (n_src + n) + [sem] * 3 + [pl.BlockSpec(memory_space=pl.ANY)],
        out_specs=[hbm] * (n_src + n),
        out_shape=[pltpu.HBM(x.shape, x.dtype) for x in srcs] + [pltpu.HBM(x.shape, x.dtype) for x in lands],
        input_output_aliases={i: i for i in range(n_src + n)},
        compiler_params=pltpu.CompilerParams(has_side_effects=pltpu.SideEffectType.DATAFLOW_SIDE_EFFECTING),
    )(*srcs, *lands, *sems, after)
    return res[n_src:]


def forward_start(name, chip_gather, after):
    lands = exchange_wait(name + "_wait", chip_gather, after)
    return exchange_start(name + "_forward_start", [], "forward", lands=lands)


_Z = (0, 1024)
_XBC = (1024, 2560)
_DT = (2560, 2576)
_RKV = (2576, 5648)
_PW = (5648, 5744)
_PA = (5744, 5840)
_PG = (5840, 6096)
D_IN = 6096

_SMALL = ("norm_mix_g", "ssd_conv_b", "ssd_dt_bias", "ssd_a_log", "ssd_d", "ssd_norm_g", "rwkv_mu", "rwkv_w0",
          "rwkv_a0", "rwkv_k_k", "rwkv_k_a", "rwkv_r_k", "rwkv_ln_w", "rwkv_ln_b", "norm_x_g", "norm_mem_g",
          "norm_ffn_g", "final_norm_g")
_WEIGHTS = ("norm_mix_g", "w_in", "ssd_conv_w", "ssd_conv_b", "ssd_dt_bias", "ssd_a_log", "ssd_d", "ssd_norm_g",
            "rwkv_mu", "rwkv_w0", "rwkv_w2", "rwkv_a0", "rwkv_a2", "rwkv_g2", "rwkv_k_k", "rwkv_k_a", "rwkv_r_k",
            "rwkv_ln_w", "rwkv_ln_b", "w_out", "norm_x_g", "norm_mem_g", "xattn_wq", "xattn_wk", "xattn_wv",
            "xattn_wo", "norm_ffn_g", "ffn_w1", "ffn_w2", "final_norm_g")


def _pad_lanes(x, width=LANES):
    return jnp.pad(x, ((0, 0), (0, width - x.shape[1])))


def _pack_small(vals):
    flat = jnp.concatenate([vals[n].reshape(-1) for n in _SMALL])
    rows = -(-flat.shape[0] // (LANES * SUBLANES)) * SUBLANES
    return jnp.pad(flat, (0, rows * LANES - flat.shape[0])).reshape(rows, LANES)


def _unpack_small(packed, shapes):
    flat = packed.reshape(-1)
    out, pos = {}, 0
    for n in _SMALL:
        size = 1
        for s in shapes[n]:
            size *= s
        out[n] = flat[pos:pos + size].reshape(shapes[n])
        pos += size
    return out


def _rows(w, rng):
    return w[rng[0]:rng[1]]


def sum_slabs(name, recv):
    n, rows, cols = recv.shape
    tc = _pick(cols, (256, 128))

    def body(r_ref, o_ref):
        acc = r_ref[0].astype(F32)
        for p in range(1, n):
            acc = acc + r_ref[p].astype(F32)
        o_ref[...] = acc

    return pl.pallas_call(
        body, name=name, grid=(cols // tc,), in_specs=[pl.BlockSpec((n, rows, tc), lambda j: (0, 0, j))],
        out_specs=pl.BlockSpec((rows, tc), lambda j: (0, j)), out_shape=jax.ShapeDtypeStruct((rows, cols), F32),
        compiler_params=_params(),
    )(recv)


def kernel(x, mem, norm_mix_g, w_in, ssd_conv_w, ssd_conv_b, ssd_dt_bias, ssd_a_log, ssd_d, ssd_norm_g, rwkv_mu, rwkv_w0, rwkv_w2, rwkv_a0, rwkv_a2, rwkv_g2, rwkv_k_k, rwkv_k_a, rwkv_r_k, rwkv_ln_w, rwkv_ln_b, w_out, norm_x_g, norm_mem_g, xattn_wq, xattn_wk, xattn_wv, xattn_wo, norm_ffn_g, ffn_w1, ffn_w2, final_norm_g, loss_target, m_norm_mix_g, m_w_in, m_ssd_conv_w, m_ssd_conv_b, m_ssd_dt_bias, m_ssd_a_log, m_ssd_d, m_ssd_norm_g, m_rwkv_mu, m_rwkv_w0, m_rwkv_w2, m_rwkv_a0, m_rwkv_a2, m_rwkv_g2, m_rwkv_k_k, m_rwkv_k_a, m_rwkv_r_k, m_rwkv_ln_w, m_rwkv_ln_b, m_w_out, m_norm_x_g, m_norm_mem_g, m_xattn_wq, m_xattn_wk, m_xattn_wv, m_xattn_wo, m_norm_ffn_g, m_ffn_w1, m_ffn_w2, m_final_norm_g, v_norm_mix_g, v_w_in, v_ssd_conv_w, v_ssd_conv_b, v_ssd_dt_bias, v_ssd_a_log, v_ssd_d, v_ssd_norm_g, v_rwkv_mu, v_rwkv_w0, v_rwkv_w2, v_rwkv_a0, v_rwkv_a2, v_rwkv_g2, v_rwkv_k_k, v_rwkv_k_a, v_rwkv_r_k, v_rwkv_ln_w, v_rwkv_ln_b, v_w_out, v_norm_x_g, v_norm_mem_g, v_xattn_wq, v_xattn_wk, v_xattn_wv, v_xattn_wo, v_norm_ffn_g, v_ffn_w1, v_ffn_w2, v_final_norm_g):
    given = dict(locals())
    wts = {n: given[n] for n in _WEIGHTS}
    mom_m = {n: given["m_" + n] for n in _WEIGHTS}
    mom_v = {n: given["v_" + n] for n in _WEIGHTS}
    d = D_MODEL
    xt, memt, tgt = x[0], mem[0], loss_target[0]
    tm = 256
    tm_rwkv = 128

    big = {"w_in": jnp.transpose(w_in[0]), "w_out": w_out[0], "xattn_wq": xattn_wq[0], "xattn_wk": xattn_wk[0],
           "xattn_wv": xattn_wv[0], "xattn_wo": xattn_wo[0], "ffn_w1": ffn_w1[0], "ffn_w2": ffn_w2[0]}
    small_sh = {"ssd_conv_w": ssd_conv_w.reshape(4, -1), "rwkv_w2": rwkv_w2[0], "rwkv_a2": rwkv_a2[0],
                "rwkv_g2": rwkv_g2[0]}
    cast_one = lambda n, deps=(): rowwise_fwd("cast_" + n, cast_fn, [big[n]], [], [(big[n].shape[1], BF16)],
                                              256 if big[n].shape[0] % 256 == 0 else big[n].shape[0], deps=deps)[0]
    gather_in, token_in = exchange_start("gather_in_start", [cast_one("w_in")] + list(small_sh.values()), "gather_chips")
    cast = {n: cast_one(n, deps=[token_in]) for n in big if n != "w_in"}
    late_a = ("w_out", "xattn_wq", "xattn_wk", "xattn_wv", "xattn_wo")
    late_b = ("ffn_w1", "ffn_w2")
    gather_a, token_a = exchange_start("gather_attn_start", [cast[n] for n in late_a], "gather_chips", dep=token_in)
    gather_b, token_b = exchange_start("gather_ffn_start", [cast[n] for n in late_b], "gather_chips", dep=token_a)
    (h1,) = rowwise_fwd("norm_mix", rmsnorm_fn, [xt], [norm_mix_g], [(d, BF16)], tm, deps=[token_b])
    forward_in, token_in = forward_start("gather_in", gather_in, after=h1)
    gathered = exchange_wait("gather_in_forward_wait", forward_in, after=token_in)
    g_big = {"w_in": gathered[0]}
    g_small = dict(zip(small_sh, gathered[1:]))

    pad_rows = lambda a: jnp.pad(a, ((0, LANES - a.shape[0]), (0, 0)))
    w_in_t = g_big["w_in"].reshape(D_IN, d)
    wt_z, wt_xbc, wt_rkv = (_rows(w_in_t, r) for r in (_Z, _XBC, _RKV))
    wt_ps = jnp.concatenate([_rows(w_in_t, _PG)] + [pad_rows(_rows(w_in_t, r)) for r in (_PW, _PA, _DT)], axis=0)
    unshard_cols = lambda g: jnp.transpose(g, (1, 0, 2)).reshape(g.shape[1], -1)
    conv_w_f = unshard_cols(g_small["ssd_conv_w"])
    w2p, a2p = pad_rows(unshard_cols(g_small["rwkv_w2"])), pad_rows(unshard_cols(g_small["rwkv_a2"]))
    g2_f = unshard_cols(g_small["rwkv_g2"])

    mu = rwkv_mu
    mu_rkv, mu_pg = mu[:, :3072], mu[:, 3264:3520]
    mu_pwa = jnp.concatenate([_pad_lanes(mu[:, 3072:3168]), _pad_lanes(mu[:, 3168:3264])], axis=1)
    dt_bias_p, a_log_p, d_p = _pad_lanes(ssd_dt_bias), _pad_lanes(ssd_a_log), _pad_lanes(ssd_d)
    r_k_row = rwkv_r_k.reshape(1, RWKV_WIDTH)
    g_final = final_norm_g.reshape(1, d)

    u_z = mm("in_z", h1, wt_z, "nt")
    u_xbc = mm("in_xbc", h1, wt_xbc, "nt")
    u_rkv = mm("in_rkv", h1, wt_rkv, "nt")
    u_ps = mm("in_narrow", h1, wt_ps, "nt")

    ssd_pre_rows = lambda: [Rows(u_xbc, shifts=(1, 2, 3)), Rows(u_ps, LANES, 4)]
    ssd_pre_params = [conv_w_f, ssd_conv_b, dt_bias_p]
    xs, bm, cm, dt = rowwise_fwd("ssd_pre", ssd_pre_fn, ssd_pre_rows(), ssd_pre_params,
                                 [(SSD_WIDTH, F32), (256, F32), (256, F32), (LANES, F32)], tm)
    y_scan, ssd_ck = ssd_scan_fwd(xs, bm, cm, dt, a_log_p, d_p)
    (y_ssd,) = rowwise_fwd("ssd_post", ssd_post_fn, [y_scan, u_z], [ssd_norm_g], [(SSD_WIDTH, BF16)], tm)

    rwkv_pre_rows = lambda: [Rows(u_rkv, shifts=(1,)), Rows(u_ps, 2 * LANES, 0, shifts=(1,)), Rows(u_ps, 2 * LANES, 1, shifts=(1,))]
    rwkv_pre_params = [mu_rkv, mu_pg, mu_pwa, rwkv_w0, w2p, rwkv_a0, a2p, g2_f, rwkv_k_k, rwkv_k_a]
    forward_a, token_a = forward_start("gather_attn", gather_a, after=y_ssd)
    r_, lw_, k_, v_, kap_, b_, gate_ = rowwise_fwd("rwkv_pre", rwkv_pre_fn, rwkv_pre_rows(), rwkv_pre_params,
                                                   [(RWKV_WIDTH, F32)] * 7, tm_rwkv, deps=[token_a])
    ys_r, rwkv_ck, rwkv_inv = rwkv_scan_fwd(r_, lw_, k_, v_, kap_, b_)
    forward_b, token_b = forward_start("gather_ffn", gather_b, after=ys_r)
    g_big.update(zip(late_a, exchange_wait("gather_attn_forward_wait", forward_a, after=token_b)))
    w_out_f = g_big["w_out"].reshape(d, d)
    wq_f, wk_f, wv_f, wo_f = (g_big[n].reshape(d, d) for n in ("xattn_wq", "xattn_wk", "xattn_wv", "xattn_wo"))
    rwkv_post_params = [rwkv_ln_w, rwkv_ln_b, r_k_row]
    (y_rwkv,) = rowwise_fwd("rwkv_post", rwkv_post_fn, [ys_r, r_, k_, v_, gate_], rwkv_post_params,
                            [(RWKV_WIDTH, BF16)], tm)
    ycat = jnp.concatenate([y_ssd, y_rwkv], axis=1)
    x1 = mm("out_proj", ycat, w_out_f, "nn", res=xt)

    (h2,) = rowwise_fwd("norm_x", rmsnorm_fn, [x1], [norm_x_g], [(d, BF16)], tm)
    (mn,) = rowwise_fwd("norm_mem", rmsnorm_fn, [memt], [norm_mem_g], [(d, BF16)], tm)
    q = mm("xattn_q", h2, wq_f, "nn", out_dtype=BF16)
    kx = mm("xattn_k", mn, wk_f, "nn")
    vx = mm("xattn_v", mn, wv_f, "nn")
    (o,) = rowwise_fwd("xattn", attn_fn, [q], [kx, vx], [(d, BF16)], tm)
    x2 = mm("xattn_o", o, wo_f, "nn", res=x1)

    (h3,) = rowwise_fwd("norm_ffn", rmsnorm_fn, [x2], [norm_ffn_g], [(d, BF16)], tm)
    w1_s, w2_g = exchange_wait("gather_ffn_forward_wait", forward_b, after=h3)
    w2_f = w2_g.reshape(D_FF, d)
    relu2_epi = lambda acc: (jnp.square(jnp.maximum(acc, 0.0)), jnp.maximum(acc, 0.0))
    hid, relu_a = mm("ffn_1", h3, w1_s, "nn", b_slabs=N_DEV, epi=relu2_epi, out_dtypes=[BF16, BF16])
    x3 = mm("ffn_2", hid, w2_f, "nn", res=x2)

    loss_blk, dx3, dx3_b, dg_final = loss_and_grad(x3, tgt, g_final, tm)

    grads = {}
    grads["ffn_w2"] = mm("d_ffn_w2", hid, dx3_b, "tn", out_dtype=BF16).reshape(N_DEV, D_FF // N_DEV, d)
    sc_w2, tok = exchange_start("scatter_ffn_w2_start", [grads["ffn_w2"]], "scatter")
    da = mm("d_hid", dx3_b, w2_f, "nt", dep=tok, epi=lambda acc, ra: (2.0 * acc * ra,), extras=[relu_a],
            out_dtypes=[BF16])
    grads["ffn_w1"] = mm("d_ffn_w1", h3, da, "tn", out_dtype=BF16, out_slabs=N_DEV)
    sc_w1, tok = exchange_start("scatter_ffn_w1_start", [grads["ffn_w1"]], "scatter")
    dh3 = mm("d_h3", da, w1_s, "nt", out_dtype=BF16, b_slabs=N_DEV, dep=tok)
    (dx2,), (dg_ffn,) = rowwise_bwd("norm_ffn_bwd", rmsnorm_fn, [x2], [norm_ffn_g], [[dh3]], tm, [F32], row_add=[dx3])

    grads["xattn_wo"] = mm("d_wo", o, dx2, "tn", out_dtype=BF16).reshape(N_DEV, d // N_DEV, d)
    sc_wo, tok = exchange_start("scatter_wo_start", [grads["xattn_wo"]], "scatter")
    d_o = mm("d_o", dx2, wo_f, "nt", out_dtype=BF16, dep=tok)
    (dq,), (dkx, dvx) = rowwise_bwd("xattn_bwd", attn_fn, [q], [kx, vx], [[d_o]], tm, [BF16])
    grads["xattn_wq"] = mm("d_wq", h2, dq, "tn", out_dtype=BF16).reshape(N_DEV, d // N_DEV, d)
    grads["xattn_wk"] = mm("d_wk", mn, dkx, "tn", out_dtype=BF16).reshape(N_DEV, d // N_DEV, d)
    grads["xattn_wv"] = mm("d_wv", mn, dvx, "tn", out_dtype=BF16).reshape(N_DEV, d // N_DEV, d)
    qkv = ("xattn_wq", "xattn_wk", "xattn_wv")
    sc_qkv, tok = exchange_start("scatter_qkv_start", [grads[n] for n in qkv], "scatter")
    dmn = mm("d_mn_v", dvx, wv_f, "nt", res=mm("d_mn_k", dkx, wk_f, "nt", dep=tok))
    _, (dg_mem,) = rowwise_bwd("norm_mem_bwd", rmsnorm_fn, [memt], [norm_mem_g], [[dmn]], tm, [None])
    dh2 = mm("d_h2", dq, wq_f, "nt", out_dtype=BF16, dep=dg_mem)
    (dx1,), (dg_x,) = rowwise_bwd("norm_x_bwd", rmsnorm_fn, [x1], [norm_x_g], [[dh2]], tm, [F32], row_add=[dx2])

    grads["w_out"] = mm("d_w_out", ycat, dx1, "tn", out_dtype=BF16).reshape(N_DEV, d // N_DEV, d)
    sc_wout, tok = exchange_start("scatter_w_out_start", [grads["w_out"]], "scatter")
    d_ycat = mm("d_ycat", dx1, w_out_f, "nt", out_dtype=BF16, dep=tok)

    (d_ys, d_r1, d_k1, d_v1, d_gate), (dln_w, dln_b, dr_k) = rowwise_bwd(
        "rwkv_post_bwd", rwkv_post_fn, [ys_r, r_, k_, v_, gate_], rwkv_post_params,
        [[Rows(d_ycat, RWKV_WIDTH, 1)]], tm, [BF16] * 5)
    d_r2, d_lw, d_k2, d_v2, d_kap, d_b = rwkv_scan_bwd(r_, lw_, k_, v_, kap_, b_, rwkv_ck, rwkv_inv, d_ys)
    (du_rkv, du_pg, du_pwa), rwkv_pg = rowwise_bwd(
        "rwkv_pre_bwd", rwkv_pre_fn, rwkv_pre_rows(), rwkv_pre_params,
        [[d_r1, d_r2], [d_lw], [d_k1, d_k2], [d_v1, d_v2], [d_kap], [d_b], [d_gate]], tm_rwkv, [BF16] * 3)
    dmu_rkv, dmu_pg, dmu_pwa, dw0, dw2p, da0, da2p, dg2, dk_k, dk_a = rwkv_pg

    (d_yscan, du_z), (dssd_norm_g,) = rowwise_bwd("ssd_post_bwd", ssd_post_fn, [y_scan, u_z], [ssd_norm_g],
                                                  [[Rows(d_ycat, SSD_WIDTH, 0)]], tm, [BF16, BF16])
    dxs, dbm, dcm, ddt2, da_log_p, dd_p = ssd_scan_bwd(xs, bm, cm, dt, a_log_p, d_p, ssd_ck, d_yscan)
    (du_xbc, du_dt), (dconv_w, dconv_b, ddt_bias_p) = rowwise_bwd(
        "ssd_pre_bwd", ssd_pre_fn, ssd_pre_rows(), ssd_pre_params,
        [[dxs], [dbm], [dcm], [ddt2[0], ddt2[1]]], tm, [BF16, BF16])
    du_ps = jnp.concatenate([du_pg, du_pwa, du_dt], axis=1)

    dwt_z = mm("d_w_z", du_z, h1, "tn", out_dtype=BF16)
    dwt_xbc = mm("d_w_xbc", du_xbc, h1, "tn", out_dtype=BF16)
    dwt_rkv = mm("d_w_rkv", du_rkv, h1, "tn", out_dtype=BF16)
    dwt_ps = mm("d_w_narrow", du_ps, h1, "tn", out_dtype=BF16)
    dwt_full = jnp.concatenate([dwt_z, dwt_xbc, dwt_ps[512:528], dwt_rkv, dwt_ps[256:352], dwt_ps[384:480], dwt_ps[0:256]],
                               axis=0)
    to_slabs = lambda g: jnp.transpose(g.reshape(g.shape[0], N_DEV, -1), (1, 0, 2))
    grads["w_in"] = dwt_full.reshape(N_DEV, D_IN // N_DEV, d)
    grads["ssd_conv_w"] = to_slabs(dconv_w)
    grads["rwkv_w2"] = to_slabs(dw2p[:96])
    grads["rwkv_a2"] = to_slabs(da2p[:96])
    grads["rwkv_g2"] = to_slabs(dg2)
    tail = ("w_in", "ssd_conv_w", "rwkv_w2", "rwkv_a2", "rwkv_g2")
    sc_tail, tok = exchange_start("scatter_tail_start", [grads[n] for n in tail], "scatter")
    dh1 = mm("d_h1_z", du_z, wt_z, "nn", dep=tok)
    dh1 = mm("d_h1_xbc", du_xbc, wt_xbc, "nn", res=dh1)
    dh1 = mm("d_h1_rkv", du_rkv, wt_rkv, "nn", res=dh1)
    dh1 = mm("d_h1_narrow", du_ps, wt_ps, "nn", res=dh1)
    (dx,), (dg_mix,) = rowwise_bwd("norm_mix_bwd", rmsnorm_fn, [xt], [norm_mix_g], [[dh1]], tm, [F32], row_add=[dx1])

    dmu =jnp.concatenate([dmu_rkv, dmu_pwa[:, 0:96], dmu_pwa[:, 128:224], dmu_pg], axis=1)
    small_grads = {
        "norm_mix_g": dg_mix, "ssd_conv_b": dconv_b, "ssd_dt_bias": ddt_bias_p[:, :16], "ssd_a_log": da_log_p[:, :16],
        "ssd_d": dd_p[:, :16], "ssd_norm_g": dssd_norm_g, "rwkv_mu": dmu, "rwkv_w0": dw0, "rwkv_a0": da0,
        "rwkv_k_k": dk_k, "rwkv_k_a": dk_a, "rwkv_r_k": dr_k, "rwkv_ln_w": dln_w, "rwkv_ln_b": dln_b,
        "norm_x_g": dg_x, "norm_mem_g": dg_mem, "norm_ffn_g": dg_ffn, "final_norm_g": dg_final}

    gather_small, tok = exchange_start("gather_small_start", [_pack_small(small_grads)], "gather")
    received = {}
    for names, handle in ((("ffn_w2",), sc_w2), (("ffn_w1",), sc_w1), (("xattn_wo",), sc_wo), (qkv, sc_qkv),
                          (("w_out",), sc_wout)):
        received.update(zip(names, exchange_wait("scatter_" + names[0] + "_wait", handle, after=tok)))

    out_g, out_d, out_m, out_v = {}, {}, {}, {}

    def run_adamw(n, dep):
        shape = wts[n].shape
        two_d = lambda a: a.reshape(-1, shape[-1])
        if n == "w_in":
            recv = jnp.transpose(sum_slabs("sum_w_in", received[n]))[None]
        else:
            recv = received[n].reshape(N_DEV, -1, shape[-1])
        res = adamw("adamw_" + n, recv, two_d(wts[n]), two_d(mom_m[n]), two_d(mom_v[n]), dep=dep)
        out_g[n], out_d[n], out_m[n], out_v[n] = (r.reshape(shape) for r in res)
        return res[0]

    last = None
    for n in ("ffn_w2", "ffn_w1", "xattn_wo") + qkv + ("w_out",):
        last = run_adamw(n, last)
    received.update(zip(tail, exchange_wait("scatter_tail_wait", sc_tail, after=last)))
    for n in tail:
        last = run_adamw(n, last)
    (small_all,) = exchange_wait("gather_small_wait", gather_small, after=last)
    res = adamw("adamw_small", small_all, _pack_small(wts), _pack_small(mom_m), _pack_small(mom_v))
    shapes = {n: wts[n].shape for n in _SMALL}
    for dst, packed in zip((out_g, out_d, out_m, out_v), res):
        dst.update(_unpack_small(packed, shapes))

    loss = lax.psum(loss_blk[0, 0], ("x", "y", "c"))
    return (loss, dx[None], *[out_g[n] for n in _WEIGHTS], *[out_d[n] for n in _WEIGHTS],
            *[out_m[n] for n in _WEIGHTS], *[out_v[n] for n in _WEIGHTS])
```

```python
import functools

import jax
import jax.numpy as jnp
from jax import lax
from jax.experimental import pallas as pl
from jax.experimental.pallas import tpu as pltpu

F32 = jnp.float32
BF16 = jnp.bfloat16

N_DEV = 8
D_MODEL = 2048
NORM_EPS = 1e-6
SSD_WIDTH = 1024
SSD_HEAD_DIM = 64
SSD_STATE = 128
SSD_CHUNK = 128
SSD_HEADS_PER_GROUP = 8
RWKV_WIDTH = 1024
RWKV_HEADS = 16
RWKV_HEAD_DIM = 64
RWKV_LN_EPS = 64e-5
RWKV_CHUNK = 64
RWKV_HEADS_PER_STEP = 16
XATTN_HEADS = 4
XATTN_HEAD_DIM = 512
D_FF = 8192
LANES = 128
SUBLANES = 8
VMEM_LIMIT = 56 * 1024 * 1024

ADAM_LR = 0.001
ADAM_B1 = 0.9
ADAM_B2 = 0.999
ADAM_EPS = 1e-08
ADAM_WD = 0.01
ADAM_STEP = 10

_DN = {"nn": ((1,), (0,)), "nt": ((1,), (1,)), "tn": ((0,), (0,))}


def _dg(a, b, mode):
    (ca,), (cb,) = _DN[mode]
    dn = (((ca + 1,), (cb + 1,)), ((0,), (0,))) if a.ndim == 3 else (((ca,), (cb,)), ((), ()))
    return lax.dot_general(a, b, dn, preferred_element_type=F32)


@functools.partial(jax.custom_vjp, nondiff_argnums=(2,))
def bdot(a, b, mode):
    return _dg(a.astype(BF16), b.astype(BF16), mode)


def _bdot_fwd(a, b, mode):
    return bdot(a, b, mode), (a, b)


def _bdot_bwd(mode, res, g):
    a, b = res
    ab, bb, gb = a.astype(BF16), b.astype(BF16), g.astype(BF16)
    if mode == "nn":
        da, db = _dg(gb, bb, "nt"), _dg(ab, gb, "tn")
    elif mode == "nt":
        da, db = _dg(gb, bb, "nn"), _dg(gb, ab, "tn")
    else:
        da, db = _dg(bb, gb, "nt"), _dg(ab, gb, "nn")
    return da.astype(a.dtype), db.astype(b.dtype)


bdot.defvjp(_bdot_fwd, _bdot_bwd)


def _split2(x):
    hi = x.astype(BF16)
    return hi, (x - hi.astype(F32)).astype(BF16)


def _dot01(x, m01):
    hi, lo = _split2(x)
    return _dg(hi, m01, "nn") + _dg(lo, m01, "nn")


def _exact_dot_impl(a, b, mode, exact):
    if exact == "a":
        ae = a.astype(BF16)
        return sum(_dg(ae, part, mode) for part in _split2(b))
    be = b.astype(BF16)
    return sum(_dg(part, be, mode) for part in _split2(a))


@functools.partial(jax.custom_vjp, nondiff_argnums=(2, 3))
def exact_dot(a, b, mode, exact):
    return _exact_dot_impl(a, b, mode, exact)


def _exact_dot_fwd(a, b, mode, exact):
    return _exact_dot_impl(a, b, mode, exact), (a, b)


def _exact_dot_bwd(mode, exact, res, g):
    a, b = res
    if exact == "a":
        db = {"nn": lambda: _exact_dot_impl(a, g, "tn", "a"), "nt": lambda: _exact_dot_impl(g, a, "tn", "b"),
              "tn": lambda: _exact_dot_impl(a, g, "nn", "a")}[mode]()
        return jnp.zeros_like(a), db
    da = {"nn": lambda: _exact_dot_impl(g, b, "nt", "b"), "nt": lambda: _exact_dot_impl(g, b, "nn", "b"),
          "tn": lambda: _exact_dot_impl(b, g, "nt", "a")}[mode]()
    return da, jnp.zeros_like(b)


exact_dot.defvjp(_exact_dot_fwd, _exact_dot_bwd)


def _head_indicator(width, heads, transpose):
    hd = width // heads
    shape = (LANES, width) if transpose else (width, LANES)
    lane = lax.broadcasted_iota(jnp.int32, shape, 1 if not transpose else 0)
    pos = lax.broadcasted_iota(jnp.int32, shape, 0 if not transpose else 1)
    return ((pos >= lane * hd) & (pos < lane * hd + hd)).astype(BF16)


@jax.custom_vjp
def head_sum(x):
    w = x.shape[-1]
    e = _head_indicator(w, w // RWKV_HEAD_DIM, False)
    et = _head_indicator(w, w // RWKV_HEAD_DIM, True)
    return _dot01(_dot01(x, e), et)


head_sum.defvjp(lambda x: (head_sum(x), None), lambda _, g: (head_sum(g),))


def rmsnorm_fn(x, g):
    y = x * lax.rsqrt(jnp.mean(x * x, axis=-1, keepdims=True) + NORM_EPS)
    return ((y * g).astype(BF16),)


def cast_fn(x):
    return (x.astype(BF16),)


def ssd_pre_fn(xbc, xbc1, xbc2, xbc3, dt_raw, conv_w, conv_b, dt_bias):
    c = conv_w[3:4] * xbc + conv_w[2:3] * xbc1 + conv_w[1:2] * xbc2 + conv_w[0:1] * xbc3 + conv_b
    act = c * jax.nn.sigmoid(c)
    dt = jax.nn.softplus(dt_raw + dt_bias)
    return act[:, :SSD_WIDTH], act[:, SSD_WIDTH:SSD_WIDTH + 256], act[:, SSD_WIDTH + 256:], dt


def ssd_post_fn(yscan, z, norm_g):
    y = yscan * (z * jax.nn.sigmoid(z))
    half = SSD_WIDTH // 2
    parts = []
    for g in range(2):
        yg = y[:, g * half:(g + 1) * half]
        parts.append(yg * lax.rsqrt(jnp.mean(yg * yg, axis=-1, keepdims=True) + NORM_EPS))
    return ((jnp.concatenate(parts, axis=-1) * norm_g).astype(BF16),)


def rwkv_pre_fn(rkv, rkv_p, pg, pg_p, pwa, pwa_p, mu_rkv, mu_pg, mu_pwa, w0, w2p, a0, a2p, g2, k_k, k_a):
    w = RWKV_WIDTH
    rkv = rkv + (rkv_p - rkv) * mu_rkv
    pg = pg + (pg_p - pg) * mu_pg
    pwa = pwa + (pwa_p - pwa) * mu_pwa
    r, k, v = rkv[:, :w], rkv[:, w:2 * w], rkv[:, 2 * w:]
    pw, pa = pwa[:, :LANES], pwa[:, LANES:]
    w_log = -jax.nn.softplus(-(w0 + bdot(jnp.tanh(pw), w2p, "nn"))) - 0.5
    lw = -jnp.exp(w_log)
    iclr = jax.nn.sigmoid(a0 + bdot(pa, a2p, "nn"))
    gate = bdot(jax.nn.sigmoid(pg), g2, "nn")
    kk = k * k_k
    kap = kk * lax.rsqrt(jnp.maximum(head_sum(kk * kk), 1e-24))
    k_mod = k * (1.0 + (iclr - 1.0) * k_a)
    return r, lw, k_mod, v, kap, kap * iclr, gate


def rwkv_post_fn(ys, r, k_mod, v, gate, ln_w, ln_b, r_k):
    inv_n = 1.0 / RWKV_HEAD_DIM
    mean = head_sum(ys) * inv_n
    yc = ys - mean
    var = head_sum(yc * yc) * inv_n
    yn = yc * lax.rsqrt(var + RWKV_LN_EPS) * ln_w + ln_b
    bonus = head_sum(r * k_mod * r_k) * v
    return (((yn + bonus) * gate).astype(BF16),)


def attn_fn(q, kx, vx):
    outs = []
    for h in range(XATTN_HEADS):
        sl = slice(h * XATTN_HEAD_DIM, (h + 1) * XATTN_HEAD_DIM)
        s = bdot(q[:, sl], kx[:, sl], "nt") * (XATTN_HEAD_DIM ** -0.5)
        s = s - jnp.max(s, axis=-1, keepdims=True)
        p = jnp.exp(s)
        p = p / jnp.sum(p, axis=-1, keepdims=True)
        outs.append(bdot(p, vx[:, sl], "nn"))
    return (jnp.concatenate(outs, axis=-1).astype(BF16),)


def loss_fn(x, tgt, g):
    y = x * lax.rsqrt(jnp.mean(x * x, axis=-1, keepdims=True) + NORM_EPS) * g
    err = jnp.square(y - tgt)
    return 0.5 * jnp.sum(jnp.mean(err, axis=-1, keepdims=True), axis=0, keepdims=True)


def _tri_masks(n):
    row = lax.broadcasted_iota(jnp.int32, (n, n), 0)
    col = lax.broadcasted_iota(jnp.int32, (n, n), 1)
    return col <= row, col < row, row == col


@jax.custom_vjp
def unit_lower_inverse(a):
    c = a.shape[-1]
    eye = _tri_masks(c)[2].astype(F32)
    m = -a
    inv = eye + m
    n = 1
    while n * 2 < c:
        m = bdot(m, m, "nn")
        inv = bdot(inv, eye + m, "nn")
        n *= 2
    return inv


def _unit_lower_inverse_fwd(a):
    inv = unit_lower_inverse(a)
    return inv, inv


def _unit_lower_inverse_bwd(inv, g):
    return (-bdot(bdot(inv, g, "tn"), inv, "nt"),)


unit_lower_inverse.defvjp(_unit_lower_inverse_fwd, _unit_lower_inverse_bwd)


@jax.custom_vjp
def known_inverse(a, inv):
    return inv


known_inverse.defvjp(lambda a, inv: (inv, inv),
                     lambda inv, g: (_unit_lower_inverse_bwd(inv, g)[0], jnp.zeros_like(inv)))


def rwkv_chunk_fn(st0, r, lw, k, v, kap, b, inv=None):
    h, c = r.shape[0], r.shape[1]
    incl, strict, _ = _tri_masks(c)
    cum = exact_dot(jnp.broadcast_to(incl.astype(F32), (h, c, c)), lw, "nn", "a")
    g_in = jnp.exp(cum)
    g_prev = jnp.exp(cum - lw)
    g_inv = jnp.exp(-cum)
    g_end = jnp.exp(cum[:, c - 1:c, :] - cum)
    kap_t, k_t, b_t, r_t = kap * g_prev, k * g_inv, b * g_inv, r * g_in
    a_ub = jnp.where(strict, bdot(kap_t, b_t, "nt"), 0.0)
    a_vk = jnp.where(strict, bdot(kap_t, k_t, "nt"), 0.0)
    rhs = -(bdot(kap_t, st0, "nt") + bdot(a_vk, v, "nn"))
    inv = unit_lower_inverse(a_ub) if inv is None else known_inverse(a_ub, inv)
    u = bdot(inv, rhs, "nn")
    y = (bdot(r_t, st0, "nt")
         + bdot(jnp.where(incl, bdot(r_t, k_t, "nt"), 0.0), v, "nn")
         + bdot(jnp.where(incl, bdot(r_t, b_t, "nt"), 0.0), u, "nn"))
    st1 = jnp.exp(cum[:, c - 1:c, :]) * st0 + bdot(v, k * g_end, "tn") + bdot(u, b * g_end, "tn")
    return y, st1, inv


def ssd_chunk_fn(group, h0, xs, bm, cm, dt, a_log, d_skip):
    q, nh = xs.shape[0], SSD_HEADS_PER_GROUP
    causal, _, _ = _tri_masks(q)
    a_row = -jnp.exp(a_log)
    cs_all = exact_dot(causal.astype(F32), dt * a_row, "nn", "a")
    cs_t = cs_all.T
    lanes = range(group * nh, (group + 1) * nh)
    cs = jnp.stack([cs_all[:, hl:hl + 1] for hl in lanes])
    cs_row = jnp.stack([cs_t[hl:hl + 1, :] for hl in lanes])
    dt_h = jnp.stack([dt[:, hl:hl + 1] for hl in lanes])
    d_h = jnp.stack([d_skip[:, hl:hl + 1] for hl in lanes])
    x = _stack_lanes(xs, nh)
    h0s = _stack_rows(h0, nh)
    lmat = jnp.where(causal, jnp.exp(jnp.where(causal, cs - cs_row, 0.0)), 0.0)
    cb = bdot(cm, bm, "nt")
    xdt = x * dt_h
    cl = cs[:, q - 1:q, :]
    cm_b = jnp.broadcast_to(cm, (nh,) + cm.shape)
    bm_b = jnp.broadcast_to(bm, (nh,) + bm.shape)
    y = bdot(cb * lmat, xdt, "nn") + bdot(cm_b, h0s, "nt") * jnp.exp(cs) + x * d_h
    h1 = h0s * jnp.exp(cl) + bdot(xdt * jnp.exp(cl - cs), bm_b, "tn")
    return jnp.concatenate([y[e] for e in range(nh)], axis=-1), jnp.concatenate([h1[e] for e in range(nh)], axis=0)


class Rows:
    def __init__(self, arr, w=None, cb=0, shifts=()):
        self.arr, self.w, self.cb, self.shifts = arr, (arr.shape[1] if w is None else w), cb, tuple(shifts)


def _as_rows(x):
    return x if isinstance(x, Rows) else Rows(x)


def _shift_down(x, halo, k):
    rolled = pltpu.roll(x, k, 0)
    first = rolled[0:SUBLANES]
    rid = lax.broadcasted_iota(jnp.int32, first.shape, 0)
    patched = jnp.where(rid < k, pltpu.roll(halo, k, 0), first)
    return jnp.concatenate([patched, rolled[SUBLANES:]], axis=0)


def _shift_up(g, carry, k):
    tm = g.shape[0]
    rolled = pltpu.roll(g, tm - k, 0)
    last = rolled[tm - SUBLANES:]
    rid = lax.broadcasted_iota(jnp.int32, last.shape, 0)
    patched = jnp.where(rid >= SUBLANES - k, pltpu.roll(carry, SUBLANES - k, 0), last)
    return jnp.concatenate([rolled[:tm - SUBLANES], patched], axis=0)


def _params():
    return pltpu.CompilerParams(vmem_limit_bytes=VMEM_LIMIT)


def _load_rows(refs, pos, rins, first_block):
    vals = []
    for r in rins:
        x = refs[pos][...].astype(F32) if refs[pos].dtype != F32 else refs[pos][...]
        pos += 1
        vals.append(x)
        if r.shifts:
            halo = refs[pos][...]
            pos += 1
            halo = jnp.where(first_block, jnp.zeros_like(halo), halo)
            for k in r.shifts:
                vals.append(_shift_down(x, halo, k))
    return vals, pos


def _row_specs(rins, tm, blk):
    specs, args = [], []
    for r in rins:
        specs.append(pl.BlockSpec((tm, r.w), lambda i, cb=r.cb: (blk(i), cb)))
        args.append(r.arr)
        if r.shifts:
            per = tm // SUBLANES
            specs.append(pl.BlockSpec((SUBLANES, r.w), lambda i, cb=r.cb: (jnp.maximum(blk(i) * per - 1, 0), cb)))
            args.append(r.arr)
    return specs, args


def rowwise_fwd(name, fn, rins, params, outs, tm, deps=()):
    rins = [_as_rows(r) for r in rins]
    t = rins[0].arr.shape[0]
    tm = min(tm, t)
    nb = t // tm
    specs, args = _row_specs(rins, tm, lambda i: i)
    for p in params:
        specs.append(pl.BlockSpec(p.shape, lambda i: (0, 0)))
        args.append(p)
    for dep in deps:
        specs.append(pl.BlockSpec(memory_space=pl.ANY))
        args.append(dep)
    n_in = len(args)

    def body(*refs):
        vals, pos = _load_rows(refs, 0, rins, pl.program_id(0) == 0)
        pv = [refs[pos + j][...] for j in range(len(params))]
        res = fn(*vals, *pv)
        for o_ref, o in zip(refs[n_in:], res):
            o_ref[...] = o.astype(o_ref.dtype)

    return pl.pallas_call(
        body, name=name, grid=(nb,), in_specs=specs,
        out_specs=[pl.BlockSpec((tm, w), lambda i: (i, 0)) for w, _ in outs],
        out_shape=[jax.ShapeDtypeStruct((t, w), dt) for w, dt in outs],
        compiler_params=_params(),
    )(*args)


def rowwise_bwd(name, fn, rins, params, cts, tm, grad_dtypes, row_add=None):
    rins = [_as_rows(r) for r in rins]
    cts = [[_as_rows(c) for c in lst] for lst in cts]
    row_add = [_as_rows(a) for a in (row_add or [])]
    t = rins[0].arr.shape[0]
    tm = min(tm, t)
    nb = t // tm
    rev = lambda i: nb - 1 - i
    specs, args = _row_specs(rins, tm, rev)
    for p in params:
        specs.append(pl.BlockSpec(p.shape, lambda i: (0, 0)))
        args.append(p)
    flat_cts = [c for lst in cts for c in lst] + row_add
    for c in flat_cts:
        specs.append(pl.BlockSpec((tm, c.w), lambda i, cb=c.cb: (rev(i), cb)))
        args.append(c.arr)
    n_in = len(args)
    want = [i for i, d in enumerate(grad_dtypes) if d is not None]
    out_specs = [pl.BlockSpec((tm, rins[i].w), lambda i_: (rev(i_), 0)) for i in want]
    out_shape = [jax.ShapeDtypeStruct((t, rins[i].w), grad_dtypes[i]) for i in want]
    out_specs += [pl.BlockSpec(p.shape, lambda i: (0, 0)) for p in params]
    out_shape += [jax.ShapeDtypeStruct(p.shape, F32) for p in params]
    n_out = len(out_shape)
    scratch = [pltpu.VMEM((SUBLANES, r.w), F32) for r in rins for _ in r.shifts]

    def body(*refs):
        i = pl.program_id(0)
        vals, pos = _load_rows(refs, 0, rins, rev(i) == 0)
        pv = [refs[pos + j][...] for j in range(len(params))]
        pos += len(params)
        outs, vjp = jax.vjp(fn, *vals, *pv)
        ct_vals = []
        for o, lst in zip(outs, cts):
            acc = None
            for _ in lst:
                cv = refs[pos][...].astype(F32)
                pos += 1
                acc = cv if acc is None else acc + cv
            ct_vals.append(acc.astype(o.dtype))
        adds = [refs[pos + j][...].astype(F32) for j in range(len(row_add))]
        grads = vjp(tuple(ct_vals))
        out_refs = refs[n_in:n_in + n_out]
        carry_refs = refs[n_in + n_out:]

        @pl.when(i == 0)
        def _():
            for cr in carry_refs:
                cr[...] = jnp.zeros_like(cr)
            for pr in out_refs[len(want):]:
                pr[...] = jnp.zeros_like(pr)

        gi, ci, oi = 0, 0, 0
        for idx, r in enumerate(rins):
            d = grads[gi]
            gi += 1
            for k in r.shifts:
                dk = grads[gi]
                gi += 1
                d = d + _shift_up(dk, carry_refs[ci][...], k)
                carry_refs[ci][...] = dk[0:SUBLANES]
                ci += 1
            if idx == 0:
                for a in adds:
                    d = d + a
            if grad_dtypes[idx] is not None:
                out_refs[oi][...] = d.astype(out_refs[oi].dtype)
                oi += 1
        for pr, gp in zip(out_refs[len(want):], grads[gi:]):
            pr[...] += gp

    res = pl.pallas_call(
        body, name=name, grid=(nb,), in_specs=specs, out_specs=out_specs, out_shape=out_shape,
        scratch_shapes=scratch, compiler_params=_params(),
    )(*args)
    return res[:len(want)], res[len(want):]


def _pick(n, pref):
    for c in pref:
        if n % c == 0:
            return c
    return n


MM_VMEM_BUDGET = 40 * 1024 * 1024
MM_PEAK_FLOPS = 0.9e15
MM_HBM_BYTES_PER_S = 3.0e12
MM_STEP_SECONDS = 0.35e-6


def _mm_tiles(m, n, k, size_a, size_b, size_out, size_res, single_k):
    best = None
    for tk in sorted({c for c in (k, 2048, 1024, 512, 256, 128) if c <= 2048 and k % c == 0}, reverse=True):
        for tm in sorted({c for c in (m, 1024, 512, 256, 128) if c <= 1024 and m % c == 0}, reverse=True):
            for tn in sorted({c for c in (n, 2048, 1536, 1024, 768, 512, 384, 256, 128) if c <= 2048 and n % c == 0},
                             reverse=True):
                nk = k // tk
                vmem = 2 * (tm * tk * size_a + tk * tn * size_b + tm * tn * (size_out + size_res))
                vmem += tm * tn * 4 * (2 if nk > 1 or not single_k else 1)
                vmem += (tm * tk * 2 if size_a > 2 else 0) + (tk * tn * 2 if size_b > 2 else 0)
                if vmem > MM_VMEM_BUDGET:
                    continue
                steps = (m // tm) * (n // tn) * nk
                a_reads = 1 if (nk == 1 and single_k) else n // tn
                traffic = m * k * size_a * a_reads + k * n * size_b * (m // tm) + m * n * (size_out + size_res)
                cost = max(2.0 * m * n * k / MM_PEAK_FLOPS, traffic / MM_HBM_BYTES_PER_S) + steps * MM_STEP_SECONDS
                if best is None or cost < best[0]:
                    best = (cost, tm, tn, tk)
    return best[1:]


def mm(name, a, b, mode, out_dtype=F32, res=None, b_slabs=None, out_slabs=None, dep=None, epi=None, extras=(),
       out_dtypes=None):
    if mode == "tn":
        k_dim, m_dim = a.shape
    else:
        m_dim, k_dim = a.shape
    if b_slabs:
        n_dim = b.shape[0] * b.shape[2] if mode == "nn" else b.shape[1]
    else:
        n_dim = b.shape[0] if mode == "nt" else b.shape[1]
    n_slabs = out_slabs or (b_slabs if (b_slabs and mode == "nn") else 1)
    k_slabs = b_slabs if (b_slabs and mode == "nt") else 1
    if epi is None:
        out_dtypes = [out_dtype]
        if res is None:
            epi = lambda acc: (acc,)
        else:
            extras, epi = [res], lambda acc, r: (acc + r,)
    tm, tn, tk = _mm_tiles(m_dim, n_dim // n_slabs, k_dim // k_slabs, a.dtype.itemsize, b.dtype.itemsize,
                           sum(jnp.dtype(dt).itemsize for dt in out_dtypes), sum(e.dtype.itemsize for e in extras),
                           single_k=(k_slabs == 1))
    nji = n_dim // n_slabs // tn
    nki = k_dim // k_slabs // tk
    nblk = lambda js, j: js * nji + j
    kblk = lambda ks, k: ks * nki + k
    if mode == "tn":
        a_spec = pl.BlockSpec((tk, tm), lambda i, js, j, ks, k: (kblk(ks, k), i))
    else:
        a_spec = pl.BlockSpec((tm, tk), lambda i, js, j, ks, k: (i, kblk(ks, k)))
    if b_slabs and mode == "nn":
        b_spec = pl.BlockSpec((None, tk, tn), lambda i, js, j, ks, k: (js, k, j))
    elif b_slabs and mode == "nt":
        b_spec = pl.BlockSpec((None, tn, tk), lambda i, js, j, ks, k: (ks, nblk(js, j), k))
    elif mode == "nt":
        b_spec = pl.BlockSpec((tn, tk), lambda i, js, j, ks, k: (nblk(js, j), kblk(ks, k)))
    else:
        b_spec = pl.BlockSpec((tk, tn), lambda i, js, j, ks, k: (kblk(ks, k), nblk(js, j)))
    specs, args = [a_spec, b_spec], [a, b]
    for e in extras:
        specs.append(pl.BlockSpec((tm, tn), lambda i, js, j, ks, k: (i, nblk(js, j))))
        args.append(e)
    if dep is not None:
        specs.append(pl.BlockSpec(memory_space=pl.ANY))
        args.append(dep)
    if out_slabs:
        o_specs = [pl.BlockSpec((None, tm, tn), lambda i, js, j, ks, k: (js, i, j))]
        o_shapes = [jax.ShapeDtypeStruct((out_slabs, m_dim, n_dim // out_slabs), out_dtypes[0])]
    else:
        o_specs = [pl.BlockSpec((tm, tn), lambda i, js, j, ks, k: (i, nblk(js, j))) for _ in out_dtypes]
        o_shapes = [jax.ShapeDtypeStruct((m_dim, n_dim), dt) for dt in out_dtypes]

    one_k_step = k_slabs * nki == 1
    n_in, n_out = len(args), len(out_dtypes)

    def body(*refs):
        a_ref, b_ref = refs[0], refs[1]
        part = _dg(a_ref[...].astype(BF16), b_ref[...].astype(BF16), mode)

        def finish(acc):
            outs = epi(acc, *[refs[2 + j][...].astype(F32) for j in range(len(extras))])
            for o_ref, o in zip(refs[n_in:n_in + n_out], outs):
                o_ref[...] = o.astype(o_ref.dtype)

        if one_k_step:
            finish(part)
            return
        acc_ref = refs[n_in + n_out]
        ks, kk = pl.program_id(3), pl.program_id(4)

        @pl.when((ks == 0) & (kk == 0))
        def _():
            acc_ref[...] = part

        @pl.when((ks > 0) | (kk > 0))
        def _():
            acc_ref[...] += part

        pl.when((ks == k_slabs - 1) & (kk == nki - 1))(lambda: finish(acc_ref[...]))

    grid = (m_dim // tm, n_slabs, nji, k_slabs, nki)
    scratch = [] if one_k_step else [pltpu.VMEM((tm, tn), F32)]
    out = pl.pallas_call(
        body, name=name, grid=grid, in_specs=specs, out_specs=o_specs, out_shape=o_shapes, scratch_shapes=scratch,
        compiler_params=pltpu.CompilerParams(
            dimension_semantics=("parallel", "parallel", "parallel", "arbitrary", "arbitrary"),
            vmem_limit_bytes=VMEM_LIMIT),
    )(*args)
    return out[0] if n_out == 1 else out


def _stack_lanes(x, n):
    w = x.shape[1] // n
    return jnp.stack([x[:, i * w:(i + 1) * w] for i in range(n)])


def _stack_rows(x, n):
    w = x.shape[0] // n
    return jnp.stack([x[i * w:(i + 1) * w, :] for i in range(n)])


def rwkv_scan_fwd(r, lw, k, v, kap, b):
    t = r.shape[0]
    c, hps, hd = min(RWKV_CHUNK, t), RWKV_HEADS_PER_STEP, RWKV_HEAD_DIM
    nc, ng, wl = t // c, RWKV_HEADS // hps, hps * hd
    spec = pl.BlockSpec((c, wl), lambda g, ci: (ci, g))

    def body(r_ref, lw_ref, k_ref, v_ref, kap_ref, b_ref, y_ref, ck_ref, inv_ref, st_ref):
        @pl.when(pl.program_id(1) == 0)
        def _():
            st_ref[...] = jnp.zeros_like(st_ref)

        st = st_ref[...]
        ck_ref[...] = st
        ins = [x[...] for x in (r_ref, lw_ref, k_ref, v_ref, kap_ref, b_ref)]
        y, st1, inv = rwkv_chunk_fn(_stack_rows(st, hps), *[_stack_lanes(x, hps) for x in ins])
        y_ref[...] = jnp.concatenate([y[h] for h in range(hps)], axis=-1)
        st_ref[...] = jnp.concatenate([st1[h] for h in range(hps)], axis=0)
        inv_ref[...] = jnp.concatenate([inv[h] for h in range(hps)], axis=0)

    return pl.pallas_call(
        body, name="rwkv_scan_fwd", grid=(ng, nc), in_specs=[spec] * 6,
        out_specs=[spec, pl.BlockSpec((None, wl, hd), lambda g, ci: (ci, g, 0)),
                   pl.BlockSpec((None, hps * c, c), lambda g, ci: (ci, g, 0))],
        out_shape=[jax.ShapeDtypeStruct((t, RWKV_WIDTH), F32), jax.ShapeDtypeStruct((nc, RWKV_WIDTH, hd), F32),
                   jax.ShapeDtypeStruct((nc, RWKV_HEADS * c, c), F32)],
        scratch_shapes=[pltpu.VMEM((wl, hd), F32)], compiler_params=_params(),
    )(r, lw, k, v, kap, b)


def rwkv_scan_bwd(r, lw, k, v, kap, b, ck, inv_ck, dy):
    t = r.shape[0]
    c, hps, hd = min(RWKV_CHUNK, t), RWKV_HEADS_PER_STEP, RWKV_HEAD_DIM
    nc, ng, wl = t // c, RWKV_HEADS // hps, hps * hd
    spec = pl.BlockSpec((c, wl), lambda g, ci: (nc - 1 - ci, g))

    def body(r_ref, lw_ref, k_ref, v_ref, kap_ref, b_ref, ck_ref, inv_ref, dy_ref, *rest):
        out_refs, dst_ref = rest[:6], rest[6]

        @pl.when(pl.program_id(1) == 0)
        def _():
            dst_ref[...] = jnp.zeros_like(dst_ref)

        ins = [x[...] for x in (r_ref, lw_ref, k_ref, v_ref, kap_ref, b_ref)]
        dyv, ck, dst = dy_ref[...].astype(F32), ck_ref[...], dst_ref[...]
        chunk = lambda *a: rwkv_chunk_fn(*a, inv=_stack_rows(inv_ref[...], hps))[:2]
        _, vjp = jax.vjp(chunk, _stack_rows(ck, hps), *[_stack_lanes(x, hps) for x in ins])
        grads = vjp((_stack_lanes(dyv, hps), _stack_rows(dst, hps)))
        dst_ref[...] = jnp.concatenate([grads[0][h] for h in range(hps)], axis=0)
        for j in range(6):
            out_refs[j][...] = jnp.concatenate([grads[1 + j][h] for h in range(hps)], axis=-1).astype(BF16)

    return pl.pallas_call(
        body, name="rwkv_scan_bwd", grid=(ng, nc),
        in_specs=[spec] * 6 + [pl.BlockSpec((None, wl, hd), lambda g, ci: (nc - 1 - ci, g, 0)),
                               pl.BlockSpec((None, hps * c, c), lambda g, ci: (nc - 1 - ci, g, 0)), spec],
        out_specs=[spec] * 6, out_shape=[jax.ShapeDtypeStruct((t, RWKV_WIDTH), BF16)] * 6,
        scratch_shapes=[pltpu.VMEM((wl, hd), F32)], compiler_params=_params(),
    )(r, lw, k, v, kap, b, ck, inv_ck, dy)


def _ssd_specs(q, blk):
    gw = SSD_WIDTH // 2
    return [pl.BlockSpec((q, gw), lambda g, ci: (blk(ci), g)),
            pl.BlockSpec((q, SSD_STATE), lambda g, ci: (blk(ci), g)),
            pl.BlockSpec((q, SSD_STATE), lambda g, ci: (blk(ci), g)),
            pl.BlockSpec((q, LANES), lambda g, ci: (blk(ci), 0)),
            pl.BlockSpec((1, LANES), lambda g, ci: (0, 0)),
            pl.BlockSpec((1, LANES), lambda g, ci: (0, 0))]


def ssd_scan_fwd(xs, bm, cm, dt, a_log, d_skip):
    t = xs.shape[0]
    q = min(SSD_CHUNK, t)
    nc, gw = t // q, SSD_WIDTH // 2

    def body(xs_ref, bm_ref, cm_ref, dt_ref, al_ref, d_ref, y_ref, ck_ref, h_ref):
        @pl.when(pl.program_id(1) == 0)
        def _():
            h_ref[...] = jnp.zeros_like(h_ref)

        ck_ref[...] = h_ref[...]
        args = (h_ref[...], xs_ref[...], bm_ref[...], cm_ref[...], dt_ref[...], al_ref[...], d_ref[...])
        g = pl.program_id(0)

        @pl.when(g == 0)
        def _():
            y, h1 = ssd_chunk_fn(0, *args)
            y_ref[...] = y
            h_ref[...] = h1

        @pl.when(g == 1)
        def _():
            y, h1 = ssd_chunk_fn(1, *args)
            y_ref[...] = y
            h_ref[...] = h1

    return pl.pallas_call(
        body, name="ssd_scan_fwd", grid=(2, nc), in_specs=_ssd_specs(q, lambda ci: ci),
        out_specs=[pl.BlockSpec((q, gw), lambda g, ci: (ci, g)),
                   pl.BlockSpec((None, gw, SSD_STATE), lambda g, ci: (ci, g, 0))],
        out_shape=[jax.ShapeDtypeStruct((t, SSD_WIDTH), F32), jax.ShapeDtypeStruct((nc, SSD_WIDTH, SSD_STATE), F32)],
        scratch_shapes=[pltpu.VMEM((gw, SSD_STATE), F32)], compiler_params=_params(),
    )(xs, bm, cm, dt, a_log, d_skip)


def ssd_scan_bwd(xs, bm, cm, dt, a_log, d_skip, ck, dy):
    t = xs.shape[0]
    q = min(SSD_CHUNK, t)
    nc, gw = t // q, SSD_WIDTH // 2
    rev = lambda ci: nc - 1 - ci

    def body(xs_ref, bm_ref, cm_ref, dt_ref, al_ref, d_ref, ck_ref, dy_ref,
             dxs_ref, dbm_ref, dcm_ref, ddt_ref, dal_ref, dd_ref, dh_ref):
        g, ci = pl.program_id(0), pl.program_id(1)

        @pl.when(ci == 0)
        def _():
            dh_ref[...] = jnp.zeros_like(dh_ref)

        @pl.when((ci == 0) & (g == 0))
        def _():
            dal_ref[...] = jnp.zeros_like(dal_ref)
            dd_ref[...] = jnp.zeros_like(dd_ref)

        args = (ck_ref[...], xs_ref[...], bm_ref[...], cm_ref[...], dt_ref[...], al_ref[...], d_ref[...])

        def run(group):
            _, vjp = jax.vjp(functools.partial(ssd_chunk_fn, group), *args)
            dh0, dxs, dbm, dcm, ddt, dal, dd = vjp((dy_ref[...].astype(F32), dh_ref[...]))
            dh_ref[...] = dh0
            dxs_ref[...] = dxs.astype(BF16)
            dbm_ref[...] = dbm.astype(BF16)
            dcm_ref[...] = dcm.astype(BF16)
            ddt_ref[...] = ddt
            dal_ref[...] += dal
            dd_ref[...] += dd

        pl.when(g == 0)(lambda: run(0))
        pl.when(g == 1)(lambda: run(1))

    in_specs = _ssd_specs(q, rev) + [pl.BlockSpec((None, gw, SSD_STATE), lambda g, ci: (rev(ci), g, 0)),
                                     pl.BlockSpec((q, gw), lambda g, ci: (rev(ci), g))]
    return pl.pallas_call(
        body, name="ssd_scan_bwd", grid=(2, nc), in_specs=in_specs,
        out_specs=[pl.BlockSpec((q, gw), lambda g, ci: (rev(ci), g)),
                   pl.BlockSpec((q, SSD_STATE), lambda g, ci: (rev(ci), g)),
                   pl.BlockSpec((q, SSD_STATE), lambda g, ci: (rev(ci), g)),
                   pl.BlockSpec((None, q, LANES), lambda g, ci: (g, rev(ci), 0)),
                   pl.BlockSpec((1, LANES), lambda g, ci: (0, 0)),
                   pl.BlockSpec((1, LANES), lambda g, ci: (0, 0))],
        out_shape=[jax.ShapeDtypeStruct((t, SSD_WIDTH), BF16), jax.ShapeDtypeStruct((t, 2 * SSD_STATE), BF16),
                   jax.ShapeDtypeStruct((t, 2 * SSD_STATE), BF16), jax.ShapeDtypeStruct((2, t, LANES), F32),
                   jax.ShapeDtypeStruct((1, LANES), F32), jax.ShapeDtypeStruct((1, LANES), F32)],
        scratch_shapes=[pltpu.VMEM((gw, SSD_STATE), F32)], compiler_params=_params(),
    )(xs, bm, cm, dt, a_log, d_skip, ck, dy)


def loss_and_grad(x, tgt, g, tm):
    t, d = x.shape
    tm = min(tm, t)
    nb = t // tm

    def body(x_ref, t_ref, g_ref, loss_ref, dx_ref, dxb_ref, dg_ref):
        @pl.when(pl.program_id(0) == 0)
        def _():
            loss_ref[...] = jnp.zeros_like(loss_ref)
            dg_ref[...] = jnp.zeros_like(dg_ref)

        val, vjp = jax.vjp(loss_fn, x_ref[...], t_ref[...], g_ref[...])
        dx, _, dg = vjp(jnp.ones((1, 1), F32))
        loss_ref[...] += jnp.broadcast_to(val, loss_ref.shape)
        dx_ref[...] = dx
        dxb_ref[...] = dx.astype(BF16)
        dg_ref[...] += dg

    row = pl.BlockSpec((tm, d), lambda i: (i, 0))
    one = pl.BlockSpec((1, d), lambda i: (0, 0))
    return pl.pallas_call(
        body, name="loss_and_grad", grid=(nb,), in_specs=[row, row, one],
        out_specs=[pl.BlockSpec((SUBLANES, LANES), lambda i: (0, 0)), row, row, one],
        out_shape=[jax.ShapeDtypeStruct((SUBLANES, LANES), F32), jax.ShapeDtypeStruct((t, d), F32),
                   jax.ShapeDtypeStruct((t, d), BF16), jax.ShapeDtypeStruct((1, d), F32)],
        compiler_params=_params(),
    )(x, tgt, g)


def adamw(name, recv, w, m, v, dep=None):
    rows, cols = w.shape
    n_slabs = recv.shape[0]
    recv_block_bytes = 4 * 1024 * 1024
    tm = _pick(rows, [c for c in (256, 128, 64, 32, 16, 8) if n_slabs * c * cols * 4 <= recv_block_bytes])
    c1 = 1.0 / (1.0 - ADAM_B1 ** ADAM_STEP)
    c2 = 1.0 / (1.0 - ADAM_B2 ** ADAM_STEP)

    n_dep = 0 if dep is None else 1

    def body(recv_ref, w_ref, m_ref, v_ref, *rest):
        g_ref, d_ref, nm_ref, nv_ref = rest[n_dep:]
        g = recv_ref[0].astype(F32)
        for p in range(1, n_slabs):
            g = g + recv_ref[p].astype(F32)
        nm =ADAM_B1 * m_ref[...] + (1.0 - ADAM_B1) * g
        nv = ADAM_B2 * v_ref[...] + (1.0 - ADAM_B2) * jnp.square(g)
        g_ref[...] = g
        nm_ref[...] = nm
        nv_ref[...] = nv
        d_ref[...] = -ADAM_LR * ((nm * c1) / (jnp.sqrt(nv * c2) + ADAM_EPS) + ADAM_WD * w_ref[...])

    blk = pl.BlockSpec((tm, cols), lambda i: (i, 0))
    return pl.pallas_call(
        body, name=name, grid=(rows // tm,),
        in_specs=[pl.BlockSpec((n_slabs, tm, cols), lambda i: (0, i, 0)), blk, blk, blk]
        + [pl.BlockSpec(memory_space=pl.ANY)] * n_dep,
        out_specs=[blk] * 4, out_shape=[jax.ShapeDtypeStruct((rows, cols), F32)] * 4,
        compiler_params=_params(),
    )(recv, w, m, v, *([] if dep is None else [dep]))


def _mesh_pos():
    return lax.axis_index("x"), lax.axis_index("y"), lax.axis_index("c")


def _peer(pos, mask):
    x, y, c = pos
    return (1 - x if mask & 4 else x, 1 - y if mask & 2 else y, 1 - c if mask & 1 else c)


def _linear(pos):
    return 4 * pos[0] + 2 * pos[1] + pos[2]


class Exchange:
    MASKS = {"gather": (1, 2, 3, 4, 5, 6, 7), "scatter": (1, 2, 3, 4, 5, 6, 7), "gather_chips": (1, 2, 4, 6),
             "forward": (2, 4, 6)}

    def __init__(self, xs, kind, lands=None):
        self.kind, self.masks = kind, self.MASKS[kind]
        self.xs = [] if kind == "forward" else list(xs)
        if kind == "forward":
            self.land_shape = [jax.ShapeDtypeStruct(l.shape, l.dtype) for l in lands]
        elif kind == "scatter":
            self.land_shape = [jax.ShapeDtypeStruct(x.shape, x.dtype) for x in xs]
        else:
            self.land_shape = [jax.ShapeDtypeStruct((N_DEV,) + x.shape, x.dtype) for x in xs]
        self.n = len(self.land_shape)
        copies = self.n * len(self.masks)
        self.sems = [pltpu.SemaphoreType.DMA((copies,)), pltpu.SemaphoreType.DMA((copies,)),
                     pltpu.SemaphoreType.DMA((self.n,))]

    def _copies(self, ins, outs, sems, landing):
        send_sems, recv_sems, local_sems = sems
        me = _mesh_pos()
        me_lin = _linear(me)
        local, remote = [], []
        for ti in range(self.n):
            if self.kind != "forward":
                src_mine = ins[ti].at[me_lin] if self.kind == "scatter" else ins[ti]
                local.append(pltpu.make_async_copy(src_mine, outs[ti].at[me_lin], local_sems.at[ti]))
            for j, mask in enumerate(self.masks):
                if self.kind == "forward":
                    peer = _peer(me, 1)
                    src = outs[ti].at[_linear(_peer(me, mask))]
                    dst = outs[ti].at[_linear(_peer(me, mask ^ 1 if landing else mask))]
                else:
                    peer = _peer(me, mask)
                    src = ins[ti].at[_linear(peer)] if self.kind == "scatter" else ins[ti]
                    dst = outs[ti].at[_linear(peer) if landing else me_lin]
                sem_index = ti * len(self.masks) + j
                remote.append(pltpu.make_async_remote_copy(
                    src_ref=src, dst_ref=dst, send_sem=send_sems.at[sem_index], recv_sem=recv_sems.at[sem_index],
                    device_id=peer, device_id_type=pl.DeviceIdType.MESH))
        return local, remote

    def start(self, ins, outs, sems):
        local, remote = self._copies(ins, outs, sems, landing=False)
        for cp in local + remote:
            cp.start()

    def finish(self, ins, outs, sems):
        local, remote = self._copies(ins, outs, sems, landing=True)
        for cp in remote:
            cp.wait_recv()
        for cp in remote:
            cp.wait_send()
        for cp in local:
            cp.wait()


def exchange_start(name, xs, kind, dep=None, lands=None):
    ex = Exchange(xs, kind, lands)
    hbm = pl.BlockSpec(memory_space=pltpu.HBM)
    sem = pl.BlockSpec(memory_space=pltpu.SEMAPHORE)
    if lands is None:
        lands = [lax.empty(s.shape, s.dtype) for s in ex.land_shape]
    n_src, n = len(ex.xs), ex.n
    n_inputs = n_src + n + (0 if dep is None else 1)

    def body(*refs):
        ins, lnd, sems, token = refs[:n_src], refs[n_src:n_src + n], refs[n_inputs:n_inputs + 3], refs[-1]
        ex.start(ins, lnd, sems)
        token[...] = jnp.zeros_like(token)

    res = pl.pallas_call(
        body, name=name, in_specs=[hbm] * (n_src + n) + ([] if dep is None else [pl.BlockSpec(memory_space=pl.ANY)]),
        out_specs=[sem] * 3 + [hbm] * (n_src + n) + [pl.BlockSpec(memory_space=pltpu.VMEM)],
        out_shape=ex.sems + [pltpu.HBM(x.shape, x.dtype) for x in ex.xs]
        + [pltpu.HBM(s.shape, s.dtype) for s in ex.land_shape] + [jax.ShapeDtypeStruct((SUBLANES, LANES), F32)],
        input_output_aliases={i: 3 + i for i in range(n_src + n)},
        compiler_params=pltpu.CompilerParams(has_side_effects=pltpu.SideEffectType.DATAFLOW_SIDE_EFFECTING),
    )(*[pltpu.with_memory_space_constraint(x, pltpu.HBM) for x in ex.xs + list(lands)],
      *([] if dep is None else [dep]))
    return (ex, res[:3], res[3:3 + n_src], res[3 + n_src:3 + n_src + n]), res[-1]


def exchange_wait(name, handles, after):
    ex, sems, srcs, lands = handles
    n_src, n = len(srcs), len(lands)
    hbm = pl.BlockSpec(memory_space=pltpu.HBM)
    sem = pl.BlockSpec(memory_space=pltpu.SEMAPHORE)

    def body(*refs):
        ins, lnd, sem_refs = refs[:n_src], refs[n_src:n_src + n], refs[n_src + n:n_src + n + 3]
        ex.finish(ins, lnd, sem_refs)

    res = pl.pallas_call(
        body, name=name, in_specs=[hbm] * (n_src + n) + [sem] * 3 + [pl.BlockSpec(memory_space=pl.ANY)],
        out_specs=[hbm] * (n_src + n),
        out_shape=[pltpu.HBM(x.shape, x.dtype) for x in srcs] + [pltpu.HBM(x.shape, x.dtype) for x in lands],
        input_output_aliases={i: i for i in range(n_src + n)},
        compiler_params=pltpu.CompilerParams(has_side_effects=pltpu.SideEffectType.DATAFLOW_SIDE_EFFECTING),
    )(*srcs, *lands, *sems, after)
    return res[n_src:]


def forward_start(name, chip_gather, after):
    lands = exchange_wait(name + "_wait", chip_gather, after)
    return exchange_start(name + "_forward_start", [], "forward", lands=lands)


_Z = (0, 1024)
_XBC = (1024, 2560)
_DT = (2560, 2576)
_RKV = (2576, 5648)
_PW = (5648, 5744)
_PA = (5744, 5840)
_PG = (5840, 6096)
D_IN = 6096

_SMALL = ("norm_mix_g", "ssd_conv_b", "ssd_dt_bias", "ssd_a_log", "ssd_d", "ssd_norm_g", "rwkv_mu", "rwkv_w0",
          "rwkv_a0", "rwkv_k_k", "rwkv_k_a", "rwkv_r_k", "rwkv_ln_w", "rwkv_ln_b", "norm_x_g", "norm_mem_g",
          "norm_ffn_g", "final_norm_g")
_WEIGHTS = ("norm_mix_g", "w_in", "ssd_conv_w", "ssd_conv_b", "ssd_dt_bias", "ssd_a_log", "ssd_d", "ssd_norm_g",
            "rwkv_mu", "rwkv_w0", "rwkv_w2", "rwkv_a0", "rwkv_a2", "rwkv_g2", "rwkv_k_k", "rwkv_k_a", "rwkv_r_k",
            "rwkv_ln_w", "rwkv_ln_b", "w_out", "norm_x_g", "norm_mem_g", "xattn_wq", "xattn_wk", "xattn_wv",
            "xattn_wo", "norm_ffn_g", "ffn_w1", "ffn_w2", "final_norm_g")


def _pad_lanes(x, width=LANES):
    return jnp.pad(x, ((0, 0), (0, width - x.shape[1])))


def _pack_small(vals):
    flat = jnp.concatenate([vals[n].reshape(-1) for n in _SMALL])
    rows = -(-flat.shape[0] // (LANES * SUBLANES)) * SUBLANES
    return jnp.pad(flat, (0, rows * LANES - flat.shape[0])).reshape(rows, LANES)


def _unpack_small(packed, shapes):
    flat = packed.reshape(-1)
    out, pos = {}, 0
    for n in _SMALL:
        size = 1
        for s in shapes[n]:
            size *= s
        out[n] = flat[pos:pos + size].reshape(shapes[n])
        pos += size
    return out


def _rows(w, rng):
    return w[rng[0]:rng[1]]


def sum_slabs(name, recv):
    n, rows, cols = recv.shape
    tc = _pick(cols, (256, 128))

    def body(r_ref, o_ref):
        acc = r_ref[0].astype(F32)
        for p in range(1, n):
            acc = acc + r_ref[p].astype(F32)
        o_ref[...] = acc

    return pl.pallas_call(
        body, name=name, grid=(cols // tc,), in_specs=[pl.BlockSpec((n, rows, tc), lambda j: (0, 0, j))],
        out_specs=pl.BlockSpec((rows, tc), lambda j: (0, j)), out_shape=jax.ShapeDtypeStruct((rows, cols), F32),
        compiler_params=_params(),
    )(recv)


def kernel(x, mem, norm_mix_g, w_in, ssd_conv_w, ssd_conv_b, ssd_dt_bias, ssd_a_log, ssd_d, ssd_norm_g, rwkv_mu, rwkv_w0, rwkv_w2, rwkv_a0, rwkv_a2, rwkv_g2, rwkv_k_k, rwkv_k_a, rwkv_r_k, rwkv_ln_w, rwkv_ln_b, w_out, norm_x_g, norm_mem_g, xattn_wq, xattn_wk, xattn_wv, xattn_wo, norm_ffn_g, ffn_w1, ffn_w2, final_norm_g, loss_target, m_norm_mix_g, m_w_in, m_ssd_conv_w, m_ssd_conv_b, m_ssd_dt_bias, m_ssd_a_log, m_ssd_d, m_ssd_norm_g, m_rwkv_mu, m_rwkv_w0, m_rwkv_w2, m_rwkv_a0, m_rwkv_a2, m_rwkv_g2, m_rwkv_k_k, m_rwkv_k_a, m_rwkv_r_k, m_rwkv_ln_w, m_rwkv_ln_b, m_w_out, m_norm_x_g, m_norm_mem_g, m_xattn_wq, m_xattn_wk, m_xattn_wv, m_xattn_wo, m_norm_ffn_g, m_ffn_w1, m_ffn_w2, m_final_norm_g, v_norm_mix_g, v_w_in, v_ssd_conv_w, v_ssd_conv_b, v_ssd_dt_bias, v_ssd_a_log, v_ssd_d, v_ssd_norm_g, v_rwkv_mu, v_rwkv_w0, v_rwkv_w2, v_rwkv_a0, v_rwkv_a2, v_rwkv_g2, v_rwkv_k_k, v_rwkv_k_a, v_rwkv_r_k, v_rwkv_ln_w, v_rwkv_ln_b, v_w_out, v_norm_x_g, v_norm_mem_g, v_xattn_wq, v_xattn_wk, v_xattn_wv, v_xattn_wo, v_norm_ffn_g, v_ffn_w1, v_ffn_w2, v_final_norm_g):
    given = dict(locals())
    wts = {n: given[n] for n in _WEIGHTS}
    mom_m = {n: given["m_" + n] for n in _WEIGHTS}
    mom_v = {n: given["v_" + n] for n in _WEIGHTS}
    d = D_MODEL
    xt, memt, tgt = x[0], mem[0], loss_target[0]
    tm = 256
    tm_light = 512
    tm_rwkv = 128

    big = {"w_in": jnp.transpose(w_in[0]), "w_out": w_out[0], "xattn_wq": xattn_wq[0], "xattn_wk": xattn_wk[0],
           "xattn_wv": xattn_wv[0], "xattn_wo": xattn_wo[0], "ffn_w1": ffn_w1[0], "ffn_w2": ffn_w2[0]}
    small_sh = {"ssd_conv_w": ssd_conv_w.reshape(4, -1), "rwkv_w2": rwkv_w2[0], "rwkv_a2": rwkv_a2[0],
                "rwkv_g2": rwkv_g2[0]}
    cast_one = lambda n, deps=(): rowwise_fwd("cast_" + n, cast_fn, [big[n]], [], [(big[n].shape[1], BF16)],
                                              256 if big[n].shape[0] % 256 == 0 else big[n].shape[0], deps=deps)[0]
    gather_in, token_in = exchange_start("gather_in_start", [cast_one("w_in")] + list(small_sh.values()), "gather_chips")
    cast = {n: cast_one(n, deps=[token_in]) for n in big if n != "w_in"}
    late_a = ("w_out", "xattn_wq", "xattn_wk", "xattn_wv", "xattn_wo")
    late_b = ("ffn_w1", "ffn_w2")
    gather_a, token_a = exchange_start("gather_attn_start", [cast[n] for n in late_a], "gather_chips", dep=token_in)
    gather_b, token_b = exchange_start("gather_ffn_start", [cast[n] for n in late_b], "gather_chips", dep=token_a)
    (h1,) = rowwise_fwd("norm_mix", rmsnorm_fn, [xt], [norm_mix_g], [(d, BF16)], tm_light, deps=[token_b])
    forward_in, token_in = forward_start("gather_in", gather_in, after=h1)
    gathered = exchange_wait("gather_in_forward_wait", forward_in, after=token_in)
    g_big = {"w_in": gathered[0]}
    g_small = dict(zip(small_sh, gathered[1:]))

    pad_rows = lambda a: jnp.pad(a, ((0, LANES - a.shape[0]), (0, 0)))
    w_in_t = g_big["w_in"].reshape(D_IN, d)
    wt_z, wt_xbc, wt_rkv = (_rows(w_in_t, r) for r in (_Z, _XBC, _RKV))
    wt_ps = jnp.concatenate([_rows(w_in_t, _PG)] + [pad_rows(_rows(w_in_t, r)) for r in (_PW, _PA, _DT)], axis=0)
    unshard_cols = lambda g: jnp.transpose(g, (1, 0, 2)).reshape(g.shape[1], -1)
    conv_w_f = unshard_cols(g_small["ssd_conv_w"])
    w2p, a2p = pad_rows(unshard_cols(g_small["rwkv_w2"])), pad_rows(unshard_cols(g_small["rwkv_a2"]))
    g2_f = unshard_cols(g_small["rwkv_g2"])

    mu = rwkv_mu
    mu_rkv, mu_pg = mu[:, :3072], mu[:, 3264:3520]
    mu_pwa = jnp.concatenate([_pad_lanes(mu[:, 3072:3168]), _pad_lanes(mu[:, 3168:3264])], axis=1)
    dt_bias_p, a_log_p, d_p = _pad_lanes(ssd_dt_bias), _pad_lanes(ssd_a_log), _pad_lanes(ssd_d)
    r_k_row = rwkv_r_k.reshape(1, RWKV_WIDTH)
    g_final = final_norm_g.reshape(1, d)

    u_z = mm("in_z", h1, wt_z, "nt")
    u_xbc = mm("in_xbc", h1, wt_xbc, "nt")
    u_rkv = mm("in_rkv", h1, wt_rkv, "nt")
    u_ps = mm("in_narrow", h1, wt_ps, "nt")

    ssd_pre_rows = lambda: [Rows(u_xbc, shifts=(1, 2, 3)), Rows(u_ps, LANES, 4)]
    ssd_pre_params = [conv_w_f, ssd_conv_b, dt_bias_p]
    xs, bm, cm, dt = rowwise_fwd("ssd_pre", ssd_pre_fn, ssd_pre_rows(), ssd_pre_params,
                                 [(SSD_WIDTH, F32), (256, F32), (256, F32), (LANES, F32)], tm)
    y_scan, ssd_ck = ssd_scan_fwd(xs, bm, cm, dt, a_log_p, d_p)
    (y_ssd,) = rowwise_fwd("ssd_post", ssd_post_fn, [y_scan, u_z], [ssd_norm_g], [(SSD_WIDTH, BF16)], tm_light)

    rwkv_pre_rows = lambda: [Rows(u_rkv, shifts=(1,)), Rows(u_ps, 2 * LANES, 0, shifts=(1,)), Rows(u_ps, 2 * LANES, 1, shifts=(1,))]
    rwkv_pre_params = [mu_rkv, mu_pg, mu_pwa, rwkv_w0, w2p, rwkv_a0, a2p, g2_f, rwkv_k_k, rwkv_k_a]
    forward_a, token_a = forward_start("gather_attn", gather_a, after=y_ssd)
    r_, lw_, k_, v_, kap_, b_, gate_ = rowwise_fwd("rwkv_pre", rwkv_pre_fn, rwkv_pre_rows(), rwkv_pre_params,
                                                   [(RWKV_WIDTH, F32)] * 7, tm_rwkv, deps=[token_a])
    ys_r, rwkv_ck, rwkv_inv = rwkv_scan_fwd(r_, lw_, k_, v_, kap_, b_)
    forward_b, token_b = forward_start("gather_ffn", gather_b, after=ys_r)
    g_big.update(zip(late_a, exchange_wait("gather_attn_forward_wait", forward_a, after=token_b)))
    w_out_f = g_big["w_out"].reshape(d, d)
    wq_f, wk_f, wv_f, wo_f = (g_big[n].reshape(d, d) for n in ("xattn_wq", "xattn_wk", "xattn_wv", "xattn_wo"))
    rwkv_post_params = [rwkv_ln_w, rwkv_ln_b, r_k_row]
    (y_rwkv,) = rowwise_fwd("rwkv_post", rwkv_post_fn, [ys_r, r_, k_, v_, gate_], rwkv_post_params,
                            [(RWKV_WIDTH, BF16)], tm)
    ycat = jnp.concatenate([y_ssd, y_rwkv], axis=1)
    x1 = mm("out_proj", ycat, w_out_f, "nn", res=xt)

    (h2,) = rowwise_fwd("norm_x", rmsnorm_fn, [x1], [norm_x_g], [(d, BF16)], tm_light)
    (mn,) = rowwise_fwd("norm_mem", rmsnorm_fn, [memt], [norm_mem_g], [(d, BF16)], tm)
    q = mm("xattn_q", h2, wq_f, "nn", out_dtype=BF16)
    kx = mm("xattn_k", mn, wk_f, "nn")
    vx = mm("xattn_v", mn, wv_f, "nn")
    (o,) = rowwise_fwd("xattn", attn_fn, [q], [kx, vx], [(d, BF16)], tm)
    x2 = mm("xattn_o", o, wo_f, "nn", res=x1)

    (h3,) = rowwise_fwd("norm_ffn", rmsnorm_fn, [x2], [norm_ffn_g], [(d, BF16)], tm_light)
    w1_s, w2_g = exchange_wait("gather_ffn_forward_wait", forward_b, after=h3)
    w2_f = w2_g.reshape(D_FF, d)
    relu2_epi = lambda acc: (jnp.square(jnp.maximum(acc, 0.0)), jnp.maximum(acc, 0.0))
    hid, relu_a = mm("ffn_1", h3, w1_s, "nn", b_slabs=N_DEV, epi=relu2_epi, out_dtypes=[BF16, BF16])
    x3 = mm("ffn_2", hid, w2_f, "nn", res=x2)

    loss_blk, dx3, dx3_b, dg_final = loss_and_grad(x3, tgt, g_final, tm_light)

    grads = {}
    grads["ffn_w2"] = mm("d_ffn_w2", hid, dx3_b, "tn", out_dtype=BF16).reshape(N_DEV, D_FF // N_DEV, d)
    sc_w2, tok = exchange_start("scatter_ffn_w2_start", [grads["ffn_w2"]], "scatter")
    da = mm("d_hid", dx3_b, w2_f, "nt", dep=tok, epi=lambda acc, ra: (2.0 * acc * ra,), extras=[relu_a],
            out_dtypes=[BF16])
    grads["ffn_w1"] = mm("d_ffn_w1", h3, da, "tn", out_dtype=BF16, out_slabs=N_DEV)
    sc_w1, tok = exchange_start("scatter_ffn_w1_start", [grads["ffn_w1"]], "scatter")
    dh3 = mm("d_h3", da, w1_s, "nt", out_dtype=BF16, b_slabs=N_DEV, dep=tok)
    (dx2,), (dg_ffn,) = rowwise_bwd("norm_ffn_bwd", rmsnorm_fn, [x2], [norm_ffn_g], [[dh3]], tm_light, [F32], row_add=[dx3])

    grads["xattn_wo"] = mm("d_wo", o, dx2, "tn", out_dtype=BF16).reshape(N_DEV, d // N_DEV, d)
    sc_wo, tok = exchange_start("scatter_wo_start", [grads["xattn_wo"]], "scatter")
    d_o = mm("d_o", dx2, wo_f, "nt", out_dtype=BF16, dep=tok)
    (dq,), (dkx, dvx) = rowwise_bwd("xattn_bwd", attn_fn, [q], [kx, vx], [[d_o]], tm, [BF16])
    grads["xattn_wq"] = mm("d_wq", h2, dq, "tn", out_dtype=BF16).reshape(N_DEV, d // N_DEV, d)
    grads["xattn_wk"] = mm("d_wk", mn, dkx, "tn", out_dtype=BF16).reshape(N_DEV, d // N_DEV, d)
    grads["xattn_wv"] = mm("d_wv", mn, dvx, "tn", out_dtype=BF16).reshape(N_DEV, d // N_DEV, d)
    qkv = ("xattn_wq", "xattn_wk", "xattn_wv")
    sc_qkv, tok = exchange_start("scatter_qkv_start", [grads[n] for n in qkv], "scatter")
    dmn = mm("d_mn_v", dvx, wv_f, "nt", res=mm("d_mn_k", dkx, wk_f, "nt", dep=tok))
    _, (dg_mem,) = rowwise_bwd("norm_mem_bwd", rmsnorm_fn, [memt], [norm_mem_g], [[dmn]], tm, [None])
    dh2 = mm("d_h2", dq, wq_f, "nt", out_dtype=BF16, dep=dg_mem)
    (dx1,), (dg_x,) = rowwise_bwd("norm_x_bwd", rmsnorm_fn, [x1], [norm_x_g], [[dh2]], tm_light, [F32], row_add=[dx2])

    grads["w_out"] = mm("d_w_out", ycat, dx1, "tn", out_dtype=BF16).reshape(N_DEV, d // N_DEV, d)
    sc_wout, tok = exchange_start("scatter_w_out_start", [grads["w_out"]], "scatter")
    d_ycat = mm("d_ycat", dx1, w_out_f, "nt", out_dtype=BF16, dep=tok)

    (d_ys, d_r1, d_k1, d_v1, d_gate), (dln_w, dln_b, dr_k) = rowwise_bwd(
        "rwkv_post_bwd", rwkv_post_fn, [ys_r, r_, k_, v_, gate_], rwkv_post_params,
        [[Rows(d_ycat, RWKV_WIDTH, 1)]], tm, [BF16] * 5)
    d_r2, d_lw, d_k2, d_v2, d_kap, d_b = rwkv_scan_bwd(r_, lw_, k_, v_, kap_, b_, rwkv_ck, rwkv_inv, d_ys)
    (du_rkv, du_pg, du_pwa), rwkv_pg = rowwise_bwd(
        "rwkv_pre_bwd", rwkv_pre_fn, rwkv_pre_rows(), rwkv_pre_params,
        [[d_r1, d_r2], [d_lw], [d_k1, d_k2], [d_v1, d_v2], [d_kap], [d_b], [d_gate]], tm_rwkv, [BF16] * 3)
    dmu_rkv, dmu_pg, dmu_pwa, dw0, dw2p, da0, da2p, dg2, dk_k, dk_a = rwkv_pg

    (d_yscan, du_z), (dssd_norm_g,) = rowwise_bwd("ssd_post_bwd", ssd_post_fn, [y_scan, u_z], [ssd_norm_g],
                                                  [[Rows(d_ycat, SSD_WIDTH, 0)]], tm_light, [BF16, BF16])
    dxs, dbm, dcm, ddt2, da_log_p, dd_p = ssd_scan_bwd(xs, bm, cm, dt, a_log_p, d_p, ssd_ck, d_yscan)
    (du_xbc, du_dt), (dconv_w, dconv_b, ddt_bias_p) = rowwise_bwd(
        "ssd_pre_bwd", ssd_pre_fn, ssd_pre_rows(), ssd_pre_params,
        [[dxs], [dbm], [dcm], [ddt2[0], ddt2[1]]], tm, [BF16, BF16])
    du_ps = jnp.concatenate([du_pg, du_pwa, du_dt], axis=1)

    dwt_z = mm("d_w_z", du_z, h1, "tn", out_dtype=BF16)
    dwt_xbc = mm("d_w_xbc", du_xbc, h1, "tn", out_dtype=BF16)
    dwt_rkv = mm("d_w_rkv", du_rkv, h1, "tn", out_dtype=BF16)
    dwt_ps = mm("d_w_narrow", du_ps, h1, "tn", out_dtype=BF16)
    dwt_full = jnp.concatenate([dwt_z, dwt_xbc, dwt_ps[512:528], dwt_rkv, dwt_ps[256:352], dwt_ps[384:480], dwt_ps[0:256]],
                               axis=0)
    to_slabs = lambda g: jnp.transpose(g.reshape(g.shape[0], N_DEV, -1), (1, 0, 2))
    grads["w_in"] = dwt_full.reshape(N_DEV, D_IN // N_DEV, d)
    grads["ssd_conv_w"] = to_slabs(dconv_w)
    grads["rwkv_w2"] = to_slabs(dw2p[:96])
    grads["rwkv_a2"] = to_slabs(da2p[:96])
    grads["rwkv_g2"] = to_slabs(dg2)
    tail = ("w_in", "ssd_conv_w", "rwkv_w2", "rwkv_a2", "rwkv_g2")
    sc_tail, tok = exchange_start("scatter_tail_start", [grads[n] for n in tail], "scatter")
    dh1 = mm("d_h1_z", du_z, wt_z, "nn", dep=tok)
    dh1 = mm("d_h1_xbc", du_xbc, wt_xbc, "nn", res=dh1)
    dh1 = mm("d_h1_rkv", du_rkv, wt_rkv, "nn", res=dh1)
    dh1 = mm("d_h1_narrow", du_ps, wt_ps, "nn", res=dh1)
    (dx,), (dg_mix,) = rowwise_bwd("norm_mix_bwd", rmsnorm_fn, [xt], [norm_mix_g], [[dh1]], tm_light, [F32], row_add=[dx1])

    dmu =jnp.concatenate([dmu_rkv, dmu_pwa[:, 0:96], dmu_pwa[:, 128:224], dmu_pg], axis=1)
    small_grads = {
        "norm_mix_g": dg_mix, "ssd_conv_b": dconv_b, "ssd_dt_bias": ddt_bias_p[:, :16], "ssd_a_log": da_log_p[:, :16],
        "ssd_d": dd_p[:, :16], "ssd_norm_g": dssd_norm_g, "rwkv_mu": dmu, "rwkv_w0": dw0, "rwkv_a0": da0,
        "rwkv_k_k": dk_k, "rwkv_k_a": dk_a, "rwkv_r_k": dr_k, "rwkv_ln_w": dln_w, "rwkv_ln_b": dln_b,
        "norm_x_g": dg_x, "norm_mem_g": dg_mem, "norm_ffn_g": dg_ffn, "final_norm_g": dg_final}

    gather_small, tok = exchange_start("gather_small_start", [_pack_small(small_grads)], "gather")
    received = {}
    for names, handle in ((("ffn_w2",), sc_w2), (("ffn_w1",), sc_w1), (("xattn_wo",), sc_wo), (qkv, sc_qkv),
                          (("w_out",), sc_wout)):
        received.update(zip(names, exchange_wait("scatter_" + names[0] + "_wait", handle, after=tok)))

    out_g, out_d, out_m, out_v = {}, {}, {}, {}

    def run_adamw(n, dep):
        shape = wts[n].shape
        two_d = lambda a: a.reshape(-1, shape[-1])
        if n == "w_in":
            recv = jnp.transpose(sum_slabs("sum_w_in", received[n]))[None]
        else:
            recv = received[n].reshape(N_DEV, -1, shape[-1])
        res = adamw("adamw_" + n, recv, two_d(wts[n]), two_d(mom_m[n]), two_d(mom_v[n]), dep=dep)
        out_g[n], out_d[n], out_m[n], out_v[n] = (r.reshape(shape) for r in res)
        return res[0]

    last = None
    for n in ("ffn_w2", "ffn_w1", "xattn_wo") + qkv + ("w_out",):
        last = run_adamw(n, last)
    received.update(zip(tail, exchange_wait("scatter_tail_wait", sc_tail, after=last)))
    for n in tail:
        last = run_adamw(n, last)
    (small_all,) = exchange_wait("gather_small_wait", gather_small, after=last)
    res = adamw("adamw_small", small_all, _pack_small(wts), _pack_small(mom_m), _pack_small(mom_v))
    shapes = {n: wts[n].shape for n in _SMALL}
    for dst, packed in zip((out_g, out_d, out_m, out_v), res):
        dst.update(_unpack_small(packed, shapes))

    loss = lax.psum(loss_blk[0, 0], ("x", "y", "c"))
    return (loss, dx[None], *[out_g[n] for n in _WEIGHTS], *[out_d[n] for n in _WEIGHTS],
            *[out_m[n] for n in _WEIGHTS], *[out_v[n] for n in _WEIGHTS])
```

```python
import functools

import jax
import jax.numpy as jnp
from jax import lax
from jax.experimental import pallas as pl
from jax.experimental.pallas import tpu as pltpu

F32 = jnp.float32
BF16 = jnp.bfloat16

N_DEV = 8
D_MODEL = 2048
NORM_EPS = 1e-6
SSD_WIDTH = 1024
SSD_HEAD_DIM = 64
SSD_STATE = 128
SSD_CHUNK = 128
SSD_HEADS_PER_GROUP = 8
RWKV_WIDTH = 1024
RWKV_HEADS = 16
RWKV_HEAD_DIM = 64
RWKV_LN_EPS = 64e-5
RWKV_CHUNK = 64
RWKV_HEADS_PER_STEP = 16
XATTN_HEADS = 4
XATTN_HEAD_DIM = 512
D_FF = 8192
LANES = 128
SUBLANES = 8
VMEM_LIMIT = 56 * 1024 * 1024

ADAM_LR = 0.001
ADAM_B1 = 0.9
ADAM_B2 = 0.999
ADAM_EPS = 1e-08
ADAM_WD = 0.01
ADAM_STEP = 10

_DN = {"nn": ((1,), (0,)), "nt": ((1,), (1,)), "tn": ((0,), (0,))}


def _dg(a, b, mode):
    (ca,), (cb,) = _DN[mode]
    dn = (((ca + 1,), (cb + 1,)), ((0,), (0,))) if a.ndim == 3 else (((ca,), (cb,)), ((), ()))
    return lax.dot_general(a, b, dn, preferred_element_type=F32)


@functools.partial(jax.custom_vjp, nondiff_argnums=(2,))
def bdot(a, b, mode):
    return _dg(a.astype(BF16), b.astype(BF16), mode)


def _bdot_fwd(a, b, mode):
    return bdot(a, b, mode), (a, b)


def _bdot_bwd(mode, res, g):
    a, b = res
    ab, bb, gb = a.astype(BF16), b.astype(BF16), g.astype(BF16)
    if mode == "nn":
        da, db = _dg(gb, bb, "nt"), _dg(ab, gb, "tn")
    elif mode == "nt":
        da, db = _dg(gb, bb, "nn"), _dg(gb, ab, "tn")
    else:
        da, db = _dg(bb, gb, "nt"), _dg(ab, gb, "nn")
    return da.astype(a.dtype), db.astype(b.dtype)


bdot.defvjp(_bdot_fwd, _bdot_bwd)


def _split2(x):
    hi = x.astype(BF16)
    return hi, (x - hi.astype(F32)).astype(BF16)


def _dot01(x, m01):
    hi, lo = _split2(x)
    return _dg(hi, m01, "nn") + _dg(lo, m01, "nn")


def _exact_dot_impl(a, b, mode, exact):
    if exact == "a":
        ae = a.astype(BF16)
        return sum(_dg(ae, part, mode) for part in _split2(b))
    be = b.astype(BF16)
    return sum(_dg(part, be, mode) for part in _split2(a))


@functools.partial(jax.custom_vjp, nondiff_argnums=(2, 3))
def exact_dot(a, b, mode, exact):
    return _exact_dot_impl(a, b, mode, exact)


def _exact_dot_fwd(a, b, mode, exact):
    return _exact_dot_impl(a, b, mode, exact), (a, b)


def _exact_dot_bwd(mode, exact, res, g):
    a, b = res
    if exact == "a":
        db = {"nn": lambda: _exact_dot_impl(a, g, "tn", "a"), "nt": lambda: _exact_dot_impl(g, a, "tn", "b"),
              "tn": lambda: _exact_dot_impl(a, g, "nn", "a")}[mode]()
        return jnp.zeros_like(a), db
    da = {"nn": lambda: _exact_dot_impl(g, b, "nt", "b"), "nt": lambda: _exact_dot_impl(g, b, "nn", "b"),
          "tn": lambda: _exact_dot_impl(b, g, "nt", "a")}[mode]()
    return da, jnp.zeros_like(b)


exact_dot.defvjp(_exact_dot_fwd, _exact_dot_bwd)


def _head_indicator(width, heads, transpose):
    hd = width // heads
    shape = (LANES, width) if transpose else (width, LANES)
    lane = lax.broadcasted_iota(jnp.int32, shape, 1 if not transpose else 0)
    pos = lax.broadcasted_iota(jnp.int32, shape, 0 if not transpose else 1)
    return ((pos >= lane * hd) & (pos < lane * hd + hd)).astype(BF16)


@jax.custom_vjp
def head_sum(x):
    w = x.shape[-1]
    e = _head_indicator(w, w // RWKV_HEAD_DIM, False)
    et = _head_indicator(w, w // RWKV_HEAD_DIM, True)
    return _dot01(_dot01(x, e), et)


head_sum.defvjp(lambda x: (head_sum(x), None), lambda _, g: (head_sum(g),))


def rmsnorm_fn(x, g):
    y = x * lax.rsqrt(jnp.mean(x * x, axis=-1, keepdims=True) + NORM_EPS)
    return ((y * g).astype(BF16),)


def cast_fn(x):
    return (x.astype(BF16),)


def ssd_pre_fn(xbc, xbc1, xbc2, xbc3, dt_raw, conv_w, conv_b, dt_bias):
    c = conv_w[3:4] * xbc + conv_w[2:3] * xbc1 + conv_w[1:2] * xbc2 + conv_w[0:1] * xbc3 + conv_b
    act = c * jax.nn.sigmoid(c)
    dt = jax.nn.softplus(dt_raw + dt_bias)
    return act[:, :SSD_WIDTH], act[:, SSD_WIDTH:SSD_WIDTH + 256], act[:, SSD_WIDTH + 256:], dt


def ssd_post_fn(yscan, z, norm_g):
    y = yscan * (z * jax.nn.sigmoid(z))
    half = SSD_WIDTH // 2
    parts = []
    for g in range(2):
        yg = y[:, g * half:(g + 1) * half]
        parts.append(yg * lax.rsqrt(jnp.mean(yg * yg, axis=-1, keepdims=True) + NORM_EPS))
    return ((jnp.concatenate(parts, axis=-1) * norm_g).astype(BF16),)


def rwkv_pre_fn(rkv, rkv_p, pg, pg_p, pwa, pwa_p, mu_rkv, mu_pg, mu_pwa, w0, w2p, a0, a2p, g2, k_k, k_a):
    w = RWKV_WIDTH
    rkv = rkv + (rkv_p - rkv) * mu_rkv
    pg = pg + (pg_p - pg) * mu_pg
    pwa = pwa + (pwa_p - pwa) * mu_pwa
    r, k, v = rkv[:, :w], rkv[:, w:2 * w], rkv[:, 2 * w:]
    pw, pa = pwa[:, :LANES], pwa[:, LANES:]
    w_log = -jax.nn.softplus(-(w0 + bdot(jnp.tanh(pw), w2p, "nn"))) - 0.5
    lw = -jnp.exp(w_log)
    iclr = jax.nn.sigmoid(a0 + bdot(pa, a2p, "nn"))
    gate = bdot(jax.nn.sigmoid(pg), g2, "nn")
    kk = k * k_k
    kap = kk * lax.rsqrt(jnp.maximum(head_sum(kk * kk), 1e-24))
    k_mod = k * (1.0 + (iclr - 1.0) * k_a)
    return r, lw, k_mod, v, kap, kap * iclr, gate


def rwkv_post_fn(ys, r, k_mod, v, gate, ln_w, ln_b, r_k):
    inv_n = 1.0 / RWKV_HEAD_DIM
    mean = head_sum(ys) * inv_n
    yc = ys - mean
    var = head_sum(yc * yc) * inv_n
    yn = yc * lax.rsqrt(var + RWKV_LN_EPS) * ln_w + ln_b
    bonus = head_sum(r * k_mod * r_k) * v
    return (((yn + bonus) * gate).astype(BF16),)


def attn_fn(q, kx, vx):
    outs = []
    for h in range(XATTN_HEADS):
        sl = slice(h * XATTN_HEAD_DIM, (h + 1) * XATTN_HEAD_DIM)
        s = bdot(q[:, sl], kx[:, sl], "nt") * (XATTN_HEAD_DIM ** -0.5)
        s = s - jnp.max(s, axis=-1, keepdims=True)
        p = jnp.exp(s)
        p = p / jnp.sum(p, axis=-1, keepdims=True)
        outs.append(bdot(p, vx[:, sl], "nn"))
    return (jnp.concatenate(outs, axis=-1).astype(BF16),)


def loss_fn(x, tgt, g):
    y = x * lax.rsqrt(jnp.mean(x * x, axis=-1, keepdims=True) + NORM_EPS) * g
    err = jnp.square(y - tgt)
    return 0.5 * jnp.sum(jnp.mean(err, axis=-1, keepdims=True), axis=0, keepdims=True)


def _tri_masks(n):
    row = lax.broadcasted_iota(jnp.int32, (n, n), 0)
    col = lax.broadcasted_iota(jnp.int32, (n, n), 1)
    return col <= row, col < row, row == col


@jax.custom_vjp
def unit_lower_inverse(a):
    c = a.shape[-1]
    eye = _tri_masks(c)[2].astype(F32)
    m = -a
    inv = eye + m
    n = 1
    while n * 2 < c:
        m = bdot(m, m, "nn")
        inv = bdot(inv, eye + m, "nn")
        n *= 2
    return inv


def _unit_lower_inverse_fwd(a):
    inv = unit_lower_inverse(a)
    return inv, inv


def _unit_lower_inverse_bwd(inv, g):
    return (-bdot(bdot(inv, g, "tn"), inv, "nt"),)


unit_lower_inverse.defvjp(_unit_lower_inverse_fwd, _unit_lower_inverse_bwd)


@jax.custom_vjp
def known_inverse(a, inv):
    return inv


known_inverse.defvjp(lambda a, inv: (inv, inv),
                     lambda inv, g: (_unit_lower_inverse_bwd(inv, g)[0], jnp.zeros_like(inv)))


def rwkv_chunk_fn(st0, r, lw, k, v, kap, b, inv=None):
    h, c = r.shape[0], r.shape[1]
    incl, strict, _ = _tri_masks(c)
    cum = exact_dot(jnp.broadcast_to(incl.astype(F32), (h, c, c)), lw, "nn", "a")
    g_in = jnp.exp(cum)
    g_prev = jnp.exp(cum - lw)
    g_inv = jnp.exp(-cum)
    g_end = jnp.exp(cum[:, c - 1:c, :] - cum)
    kap_t, k_t, b_t, r_t = kap * g_prev, k * g_inv, b * g_inv, r * g_in
    a_ub = jnp.where(strict, bdot(kap_t, b_t, "nt"), 0.0)
    a_vk = jnp.where(strict, bdot(kap_t, k_t, "nt"), 0.0)
    rhs = -(bdot(kap_t, st0, "nt") + bdot(a_vk, v, "nn"))
    inv = unit_lower_inverse(a_ub) if inv is None else known_inverse(a_ub, inv)
    u = bdot(inv, rhs, "nn")
    y = (bdot(r_t, st0, "nt")
         + bdot(jnp.where(incl, bdot(r_t, k_t, "nt"), 0.0), v, "nn")
         + bdot(jnp.where(incl, bdot(r_t, b_t, "nt"), 0.0), u, "nn"))
    st1 = jnp.exp(cum[:, c - 1:c, :]) * st0 + bdot(v, k * g_end, "tn") + bdot(u, b * g_end, "tn")
    return y, st1, inv


def ssd_chunk_fn(group, h0, xs, bm, cm, dt, a_log, d_skip):
    q, nh = xs.shape[0], SSD_HEADS_PER_GROUP
    causal, _, _ = _tri_masks(q)
    a_row = -jnp.exp(a_log)
    cs_all = exact_dot(causal.astype(F32), dt * a_row, "nn", "a")
    cs_t = cs_all.T
    lanes = range(group * nh, (group + 1) * nh)
    cs = jnp.stack([cs_all[:, hl:hl + 1] for hl in lanes])
    cs_row = jnp.stack([cs_t[hl:hl + 1, :] for hl in lanes])
    dt_h = jnp.stack([dt[:, hl:hl + 1] for hl in lanes])
    d_h = jnp.stack([d_skip[:, hl:hl + 1] for hl in lanes])
    x = _stack_lanes(xs, nh)
    h0s = _stack_rows(h0, nh)
    lmat = jnp.where(causal, jnp.exp(jnp.where(causal, cs - cs_row, 0.0)), 0.0)
    cb = bdot(cm, bm, "nt")
    xdt = x * dt_h
    cl = cs[:, q - 1:q, :]
    cm_b = jnp.broadcast_to(cm, (nh,) + cm.shape)
    bm_b = jnp.broadcast_to(bm, (nh,) + bm.shape)
    y = bdot(cb * lmat, xdt, "nn") + bdot(cm_b, h0s, "nt") * jnp.exp(cs) + x * d_h
    h1 = h0s * jnp.exp(cl) + bdot(xdt * jnp.exp(cl - cs), bm_b, "tn")
    return jnp.concatenate([y[e] for e in range(nh)], axis=-1), jnp.concatenate([h1[e] for e in range(nh)], axis=0)


class Rows:
    def __init__(self, arr, w=None, cb=0, shifts=()):
        self.arr, self.w, self.cb, self.shifts = arr, (arr.shape[1] if w is None else w), cb, tuple(shifts)


def _as_rows(x):
    return x if isinstance(x, Rows) else Rows(x)


def _shift_down(x, halo, k):
    rolled = pltpu.roll(x, k, 0)
    first = rolled[0:SUBLANES]
    rid = lax.broadcasted_iota(jnp.int32, first.shape, 0)
    patched = jnp.where(rid < k, pltpu.roll(halo, k, 0), first)
    return jnp.concatenate([patched, rolled[SUBLANES:]], axis=0)


def _shift_up(g, carry, k):
    tm = g.shape[0]
    rolled = pltpu.roll(g, tm - k, 0)
    last = rolled[tm - SUBLANES:]
    rid = lax.broadcasted_iota(jnp.int32, last.shape, 0)
    patched = jnp.where(rid >= SUBLANES - k, pltpu.roll(carry, SUBLANES - k, 0), last)
    return jnp.concatenate([rolled[:tm - SUBLANES], patched], axis=0)


def _params():
    return pltpu.CompilerParams(vmem_limit_bytes=VMEM_LIMIT)


def _load_rows(refs, pos, rins, first_block):
    vals = []
    for r in rins:
        x = refs[pos][...].astype(F32) if refs[pos].dtype != F32 else refs[pos][...]
        pos += 1
        vals.append(x)
        if r.shifts:
            halo = refs[pos][...]
            pos += 1
            halo = jnp.where(first_block, jnp.zeros_like(halo), halo)
            for k in r.shifts:
                vals.append(_shift_down(x, halo, k))
    return vals, pos


def _row_specs(rins, tm, blk):
    specs, args = [], []
    for r in rins:
        specs.append(pl.BlockSpec((tm, r.w), lambda i, cb=r.cb: (blk(i), cb)))
        args.append(r.arr)
        if r.shifts:
            per = tm // SUBLANES
            specs.append(pl.BlockSpec((SUBLANES, r.w), lambda i, cb=r.cb: (jnp.maximum(blk(i) * per - 1, 0), cb)))
            args.append(r.arr)
    return specs, args


def rowwise_fwd(name, fn, rins, params, outs, tm, deps=(), into=None):
    rins = [_as_rows(r) for r in rins]
    t = rins[0].arr.shape[0]
    tm = min(tm, t)
    nb = t // tm
    specs, args = _row_specs(rins, tm, lambda i: i)
    for p in params:
        specs.append(pl.BlockSpec(p.shape, lambda i: (0, 0)))
        args.append(p)
    for dep in deps:
        specs.append(pl.BlockSpec(memory_space=pl.ANY))
        args.append(dep)
    out_specs = [pl.BlockSpec((tm, w), lambda i: (i, 0)) for w, _ in outs]
    out_shape = [jax.ShapeDtypeStruct((t, w), dt) for w, dt in outs]
    aliases = {}
    if into is not None:
        target, total_width, col_block = into
        out_specs = [pl.BlockSpec((tm, outs[0][0]), lambda i: (i, col_block))]
        out_shape = [jax.ShapeDtypeStruct((t, total_width), outs[0][1])]
        if target is not None:
            aliases = {len(args): 0}
            specs.append(pl.BlockSpec(memory_space=pl.ANY))
            args.append(target)
    n_in = len(args)

    def body(*refs):
        vals, pos = _load_rows(refs, 0, rins, pl.program_id(0) == 0)
        pv = [refs[pos + j][...] for j in range(len(params))]
        res = fn(*vals, *pv)
        for o_ref, o in zip(refs[n_in:], res):
            o_ref[...] = o.astype(o_ref.dtype)

    return pl.pallas_call(
        body, name=name, grid=(nb,), in_specs=specs, out_specs=out_specs, out_shape=out_shape,
        input_output_aliases=aliases, compiler_params=_params(),
    )(*args)


def rowwise_bwd(name, fn, rins, params, cts, tm, grad_dtypes, row_add=None):
    rins = [_as_rows(r) for r in rins]
    cts = [[_as_rows(c) for c in lst] for lst in cts]
    row_add = [_as_rows(a) for a in (row_add or [])]
    t = rins[0].arr.shape[0]
    tm = min(tm, t)
    nb = t // tm
    rev = lambda i: nb - 1 - i
    specs, args = _row_specs(rins, tm, rev)
    for p in params:
        specs.append(pl.BlockSpec(p.shape, lambda i: (0, 0)))
        args.append(p)
    flat_cts = [c for lst in cts for c in lst] + row_add
    for c in flat_cts:
        specs.append(pl.BlockSpec((tm, c.w), lambda i, cb=c.cb: (rev(i), cb)))
        args.append(c.arr)
    n_in = len(args)
    want = [i for i, d in enumerate(grad_dtypes) if d is not None]
    out_specs = [pl.BlockSpec((tm, rins[i].w), lambda i_: (rev(i_), 0)) for i in want]
    out_shape = [jax.ShapeDtypeStruct((t, rins[i].w), grad_dtypes[i]) for i in want]
    out_specs += [pl.BlockSpec(p.shape, lambda i: (0, 0)) for p in params]
    out_shape += [jax.ShapeDtypeStruct(p.shape, F32) for p in params]
    n_out = len(out_shape)
    scratch = [pltpu.VMEM((SUBLANES, r.w), F32) for r in rins for _ in r.shifts]

    def body(*refs):
        i = pl.program_id(0)
        vals, pos = _load_rows(refs, 0, rins, rev(i) == 0)
        pv = [refs[pos + j][...] for j in range(len(params))]
        pos += len(params)
        outs, vjp = jax.vjp(fn, *vals, *pv)
        ct_vals = []
        for o, lst in zip(outs, cts):
            acc = None
            for _ in lst:
                cv = refs[pos][...].astype(F32)
                pos += 1
                acc = cv if acc is None else acc + cv
            ct_vals.append(acc.astype(o.dtype))
        adds = [refs[pos + j][...].astype(F32) for j in range(len(row_add))]
        grads = vjp(tuple(ct_vals))
        out_refs = refs[n_in:n_in + n_out]
        carry_refs = refs[n_in + n_out:]

        @pl.when(i == 0)
        def _():
            for cr in carry_refs:
                cr[...] = jnp.zeros_like(cr)
            for pr in out_refs[len(want):]:
                pr[...] = jnp.zeros_like(pr)

        gi, ci, oi = 0, 0, 0
        for idx, r in enumerate(rins):
            d = grads[gi]
            gi += 1
            for k in r.shifts:
                dk = grads[gi]
                gi += 1
                d = d + _shift_up(dk, carry_refs[ci][...], k)
                carry_refs[ci][...] = dk[0:SUBLANES]
                ci += 1
            if idx == 0:
                for a in adds:
                    d = d + a
            if grad_dtypes[idx] is not None:
                out_refs[oi][...] = d.astype(out_refs[oi].dtype)
                oi += 1
        for pr, gp in zip(out_refs[len(want):], grads[gi:]):
            pr[...] += gp

    res = pl.pallas_call(
        body, name=name, grid=(nb,), in_specs=specs, out_specs=out_specs, out_shape=out_shape,
        scratch_shapes=scratch, compiler_params=_params(),
    )(*args)
    return res[:len(want)], res[len(want):]


def _pick(n, pref):
    for c in pref:
        if n % c == 0:
            return c
    return n


MM_VMEM_BUDGET = 40 * 1024 * 1024
MM_PEAK_FLOPS = 0.9e15
MM_HBM_BYTES_PER_S = 3.0e12
MM_STEP_SECONDS = 0.35e-6


def _mm_tiles(m, n, k, size_a, size_b, size_out, size_res, single_k):
    best = None
    for tk in sorted({c for c in (k, 2048, 1024, 512, 256, 128) if c <= 2048 and k % c == 0}, reverse=True):
        for tm in sorted({c for c in (m, 1024, 512, 256, 128) if c <= 1024 and m % c == 0}, reverse=True):
            for tn in sorted({c for c in (n, 2048, 1536, 1024, 768, 512, 384, 256, 128) if c <= 2048 and n % c == 0},
                             reverse=True):
                nk = k // tk
                vmem = 2 * (tm * tk * size_a + tk * tn * size_b + tm * tn * (size_out + size_res))
                vmem += tm * tn * 4 * (2 if nk > 1 or not single_k else 1)
                vmem += (tm * tk * 2 if size_a > 2 else 0) + (tk * tn * 2 if size_b > 2 else 0)
                if vmem > MM_VMEM_BUDGET:
                    continue
                steps = (m // tm) * (n // tn) * nk
                a_reads = 1 if (nk == 1 and single_k) else n // tn
                traffic = m * k * size_a * a_reads + k * n * size_b * (m // tm) + m * n * (size_out + size_res)
                cost = max(2.0 * m * n * k / MM_PEAK_FLOPS, traffic / MM_HBM_BYTES_PER_S) + steps * MM_STEP_SECONDS
                if best is None or cost < best[0]:
                    best = (cost, tm, tn, tk)
    return best[1:]


def mm(name, a, b, mode, out_dtype=F32, res=None, b_slabs=None, out_slabs=None, dep=None, epi=None, extras=(),
       out_dtypes=None):
    if mode == "tn":
        k_dim, m_dim = a.shape
    else:
        m_dim, k_dim = a.shape
    if b_slabs:
        n_dim = b.shape[0] * b.shape[2] if mode == "nn" else b.shape[1]
    else:
        n_dim = b.shape[0] if mode == "nt" else b.shape[1]
    n_slabs = out_slabs or (b_slabs if (b_slabs and mode == "nn") else 1)
    k_slabs = b_slabs if (b_slabs and mode == "nt") else 1
    if epi is None:
        out_dtypes = [out_dtype]
        if res is None:
            epi = lambda acc: (acc,)
        else:
            extras, epi = [res], lambda acc, r: (acc + r,)
    tm, tn, tk = _mm_tiles(m_dim, n_dim // n_slabs, k_dim // k_slabs, a.dtype.itemsize, b.dtype.itemsize,
                           sum(jnp.dtype(dt).itemsize for dt in out_dtypes), sum(e.dtype.itemsize for e in extras),
                           single_k=(k_slabs == 1))
    nji = n_dim // n_slabs // tn
    nki = k_dim // k_slabs // tk
    nblk = lambda js, j: js * nji + j
    kblk = lambda ks, k: ks * nki + k
    if mode == "tn":
        a_spec = pl.BlockSpec((tk, tm), lambda i, js, j, ks, k: (kblk(ks, k), i))
    else:
        a_spec = pl.BlockSpec((tm, tk), lambda i, js, j, ks, k: (i, kblk(ks, k)))
    if b_slabs and mode == "nn":
        b_spec = pl.BlockSpec((None, tk, tn), lambda i, js, j, ks, k: (js, k, j))
    elif b_slabs and mode == "nt":
        b_spec = pl.BlockSpec((None, tn, tk), lambda i, js, j, ks, k: (ks, nblk(js, j), k))
    elif mode == "nt":
        b_spec = pl.BlockSpec((tn, tk), lambda i, js, j, ks, k: (nblk(js, j), kblk(ks, k)))
    else:
        b_spec = pl.BlockSpec((tk, tn), lambda i, js, j, ks, k: (kblk(ks, k), nblk(js, j)))
    specs, args = [a_spec, b_spec], [a, b]
    for e in extras:
        specs.append(pl.BlockSpec((tm, tn), lambda i, js, j, ks, k: (i, nblk(js, j))))
        args.append(e)
    if dep is not None:
        specs.append(pl.BlockSpec(memory_space=pl.ANY))
        args.append(dep)
    if out_slabs:
        o_specs = [pl.BlockSpec((None, tm, tn), lambda i, js, j, ks, k: (js, i, j))]
        o_shapes = [jax.ShapeDtypeStruct((out_slabs, m_dim, n_dim // out_slabs), out_dtypes[0])]
    else:
        o_specs = [pl.BlockSpec((tm, tn), lambda i, js, j, ks, k: (i, nblk(js, j))) for _ in out_dtypes]
        o_shapes = [jax.ShapeDtypeStruct((m_dim, n_dim), dt) for dt in out_dtypes]

    one_k_step = k_slabs * nki == 1
    n_in, n_out = len(args), len(out_dtypes)

    def body(*refs):
        a_ref, b_ref = refs[0], refs[1]
        part = _dg(a_ref[...].astype(BF16), b_ref[...].astype(BF16), mode)

        def finish(acc):
            outs = epi(acc, *[refs[2 + j][...].astype(F32) for j in range(len(extras))])
            for o_ref, o in zip(refs[n_in:n_in + n_out], outs):
                o_ref[...] = o.astype(o_ref.dtype)

        if one_k_step:
            finish(part)
            return
        acc_ref = refs[n_in + n_out]
        ks, kk = pl.program_id(3), pl.program_id(4)

        @pl.when((ks == 0) & (kk == 0))
        def _():
            acc_ref[...] = part

        @pl.when((ks > 0) | (kk > 0))
        def _():
            acc_ref[...] += part

        pl.when((ks == k_slabs - 1) & (kk == nki - 1))(lambda: finish(acc_ref[...]))

    grid = (m_dim // tm, n_slabs, nji, k_slabs, nki)
    scratch = [] if one_k_step else [pltpu.VMEM((tm, tn), F32)]
    out = pl.pallas_call(
        body, name=name, grid=grid, in_specs=specs, out_specs=o_specs, out_shape=o_shapes, scratch_shapes=scratch,
        compiler_params=pltpu.CompilerParams(
            dimension_semantics=("parallel", "parallel", "parallel", "arbitrary", "arbitrary"),
            vmem_limit_bytes=VMEM_LIMIT),
    )(*args)
    return out[0] if n_out == 1 else out


def _stack_lanes(x, n):
    w = x.shape[1] // n
    return jnp.stack([x[:, i * w:(i + 1) * w] for i in range(n)])


def _stack_rows(x, n):
    w = x.shape[0] // n
    return jnp.stack([x[i * w:(i + 1) * w, :] for i in range(n)])


def rwkv_scan_fwd(r, lw, k, v, kap, b):
    t = r.shape[0]
    c, hps, hd = min(RWKV_CHUNK, t), RWKV_HEADS_PER_STEP, RWKV_HEAD_DIM
    nc, ng, wl = t // c, RWKV_HEADS // hps, hps * hd
    spec = pl.BlockSpec((c, wl), lambda g, ci: (ci, g))

    def body(r_ref, lw_ref, k_ref, v_ref, kap_ref, b_ref, y_ref, ck_ref, inv_ref, st_ref):
        @pl.when(pl.program_id(1) == 0)
        def _():
            st_ref[...] = jnp.zeros_like(st_ref)

        st = st_ref[...]
        ck_ref[...] = st
        ins = [x[...] for x in (r_ref, lw_ref, k_ref, v_ref, kap_ref, b_ref)]
        y, st1, inv = rwkv_chunk_fn(_stack_rows(st, hps), *[_stack_lanes(x, hps) for x in ins])
        y_ref[...] = jnp.concatenate([y[h] for h in range(hps)], axis=-1)
        st_ref[...] = jnp.concatenate([st1[h] for h in range(hps)], axis=0)
        inv_ref[...] = jnp.concatenate([inv[h] for h in range(hps)], axis=0)

    return pl.pallas_call(
        body, name="rwkv_scan_fwd", grid=(ng, nc), in_specs=[spec] * 6,
        out_specs=[spec, pl.BlockSpec((None, wl, hd), lambda g, ci: (ci, g, 0)),
                   pl.BlockSpec((None, hps * c, c), lambda g, ci: (ci, g, 0))],
        out_shape=[jax.ShapeDtypeStruct((t, RWKV_WIDTH), F32), jax.ShapeDtypeStruct((nc, RWKV_WIDTH, hd), F32),
                   jax.ShapeDtypeStruct((nc, RWKV_HEADS * c, c), F32)],
        scratch_shapes=[pltpu.VMEM((wl, hd), F32)], compiler_params=_params(),
    )(r, lw, k, v, kap, b)


def rwkv_scan_bwd(r, lw, k, v, kap, b, ck, inv_ck, dy):
    t = r.shape[0]
    c, hps, hd = min(RWKV_CHUNK, t), RWKV_HEADS_PER_STEP, RWKV_HEAD_DIM
    nc, ng, wl = t // c, RWKV_HEADS // hps, hps * hd
    spec = pl.BlockSpec((c, wl), lambda g, ci: (nc - 1 - ci, g))

    def body(r_ref, lw_ref, k_ref, v_ref, kap_ref, b_ref, ck_ref, inv_ref, dy_ref, *rest):
        out_refs, dst_ref = rest[:6], rest[6]

        @pl.when(pl.program_id(1) == 0)
        def _():
            dst_ref[...] = jnp.zeros_like(dst_ref)

        ins = [x[...] for x in (r_ref, lw_ref, k_ref, v_ref, kap_ref, b_ref)]
        dyv, ck, dst = dy_ref[...].astype(F32), ck_ref[...], dst_ref[...]
        chunk = lambda *a: rwkv_chunk_fn(*a, inv=_stack_rows(inv_ref[...], hps))[:2]
        _, vjp = jax.vjp(chunk, _stack_rows(ck, hps), *[_stack_lanes(x, hps) for x in ins])
        grads = vjp((_stack_lanes(dyv, hps), _stack_rows(dst, hps)))
        dst_ref[...] = jnp.concatenate([grads[0][h] for h in range(hps)], axis=0)
        for j in range(6):
            out_refs[j][...] = jnp.concatenate([grads[1 + j][h] for h in range(hps)], axis=-1).astype(BF16)

    return pl.pallas_call(
        body, name="rwkv_scan_bwd", grid=(ng, nc),
        in_specs=[spec] * 6 + [pl.BlockSpec((None, wl, hd), lambda g, ci: (nc - 1 - ci, g, 0)),
                               pl.BlockSpec((None, hps * c, c), lambda g, ci: (nc - 1 - ci, g, 0)), spec],
        out_specs=[spec] * 6, out_shape=[jax.ShapeDtypeStruct((t, RWKV_WIDTH), BF16)] * 6,
        scratch_shapes=[pltpu.VMEM((wl, hd), F32)], compiler_params=_params(),
    )(r, lw, k, v, kap, b, ck, inv_ck, dy)


def _ssd_specs(q, blk):
    gw = SSD_WIDTH // 2
    return [pl.BlockSpec((q, gw), lambda g, ci: (blk(ci), g)),
            pl.BlockSpec((q, SSD_STATE), lambda g, ci: (blk(ci), g)),
            pl.BlockSpec((q, SSD_STATE), lambda g, ci: (blk(ci), g)),
            pl.BlockSpec((q, LANES), lambda g, ci: (blk(ci), 0)),
            pl.BlockSpec((1, LANES), lambda g, ci: (0, 0)),
            pl.BlockSpec((1, LANES), lambda g, ci: (0, 0))]


def ssd_scan_fwd(xs, bm, cm, dt, a_log, d_skip):
    t = xs.shape[0]
    q = min(SSD_CHUNK, t)
    nc, gw = t // q, SSD_WIDTH // 2

    def body(xs_ref, bm_ref, cm_ref, dt_ref, al_ref, d_ref, y_ref, ck_ref, h_ref):
        @pl.when(pl.program_id(1) == 0)
        def _():
            h_ref[...] = jnp.zeros_like(h_ref)

        ck_ref[...] = h_ref[...]
        args = (h_ref[...], xs_ref[...], bm_ref[...], cm_ref[...], dt_ref[...], al_ref[...], d_ref[...])
        g = pl.program_id(0)

        @pl.when(g == 0)
        def _():
            y, h1 = ssd_chunk_fn(0, *args)
            y_ref[...] = y
            h_ref[...] = h1

        @pl.when(g == 1)
        def _():
            y, h1 = ssd_chunk_fn(1, *args)
            y_ref[...] = y
            h_ref[...] = h1

    return pl.pallas_call(
        body, name="ssd_scan_fwd", grid=(2, nc), in_specs=_ssd_specs(q, lambda ci: ci),
        out_specs=[pl.BlockSpec((q, gw), lambda g, ci: (ci, g)),
                   pl.BlockSpec((None, gw, SSD_STATE), lambda g, ci: (ci, g, 0))],
        out_shape=[jax.ShapeDtypeStruct((t, SSD_WIDTH), F32), jax.ShapeDtypeStruct((nc, SSD_WIDTH, SSD_STATE), F32)],
        scratch_shapes=[pltpu.VMEM((gw, SSD_STATE), F32)], compiler_params=_params(),
    )(xs, bm, cm, dt, a_log, d_skip)


def ssd_scan_bwd(xs, bm, cm, dt, a_log, d_skip, ck, dy):
    t = xs.shape[0]
    q = min(SSD_CHUNK, t)
    nc, gw = t // q, SSD_WIDTH // 2
    rev = lambda ci: nc - 1 - ci

    def body(xs_ref, bm_ref, cm_ref, dt_ref, al_ref, d_ref, ck_ref, dy_ref,
             dxs_ref, dbm_ref, dcm_ref, ddt_ref, dal_ref, dd_ref, dh_ref):
        g, ci = pl.program_id(0), pl.program_id(1)

        @pl.when(ci == 0)
        def _():
            dh_ref[...] = jnp.zeros_like(dh_ref)

        @pl.when((ci == 0) & (g == 0))
        def _():
            dal_ref[...] = jnp.zeros_like(dal_ref)
            dd_ref[...] = jnp.zeros_like(dd_ref)

        args = (ck_ref[...], xs_ref[...], bm_ref[...], cm_ref[...], dt_ref[...], al_ref[...], d_ref[...])

        def run(group):
            _, vjp = jax.vjp(functools.partial(ssd_chunk_fn, group), *args)
            dh0, dxs, dbm, dcm, ddt, dal, dd = vjp((dy_ref[...].astype(F32), dh_ref[...]))
            dh_ref[...] = dh0
            dxs_ref[...] = dxs.astype(BF16)
            dbm_ref[...] = dbm.astype(BF16)
            dcm_ref[...] = dcm.astype(BF16)
            ddt_ref[...] = ddt
            dal_ref[...] += dal
            dd_ref[...] += dd

        pl.when(g == 0)(lambda: run(0))
        pl.when(g == 1)(lambda: run(1))

    in_specs = _ssd_specs(q, rev) + [pl.BlockSpec((None, gw, SSD_STATE), lambda g, ci: (rev(ci), g, 0)),
                                     pl.BlockSpec((q, gw), lambda g, ci: (rev(ci), g))]
    return pl.pallas_call(
        body, name="ssd_scan_bwd", grid=(2, nc), in_specs=in_specs,
        out_specs=[pl.BlockSpec((q, gw), lambda g, ci: (rev(ci), g)),
                   pl.BlockSpec((q, SSD_STATE), lambda g, ci: (rev(ci), g)),
                   pl.BlockSpec((q, SSD_STATE), lambda g, ci: (rev(ci), g)),
                   pl.BlockSpec((None, q, LANES), lambda g, ci: (g, rev(ci), 0)),
                   pl.BlockSpec((1, LANES), lambda g, ci: (0, 0)),
                   pl.BlockSpec((1, LANES), lambda g, ci: (0, 0))],
        out_shape=[jax.ShapeDtypeStruct((t, SSD_WIDTH), BF16), jax.ShapeDtypeStruct((t, 2 * SSD_STATE), BF16),
                   jax.ShapeDtypeStruct((t, 2 * SSD_STATE), BF16), jax.ShapeDtypeStruct((2, t, LANES), F32),
                   jax.ShapeDtypeStruct((1, LANES), F32), jax.ShapeDtypeStruct((1, LANES), F32)],
        scratch_shapes=[pltpu.VMEM((gw, SSD_STATE), F32)], compiler_params=_params(),
    )(xs, bm, cm, dt, a_log, d_skip, ck, dy)


def loss_and_grad(x, tgt, g, tm):
    t, d = x.shape
    tm = min(tm, t)
    nb = t // tm

    def body(x_ref, t_ref, g_ref, loss_ref, dx_ref, dxb_ref, dg_ref):
        @pl.when(pl.program_id(0) == 0)
        def _():
            loss_ref[...] = jnp.zeros_like(loss_ref)
            dg_ref[...] = jnp.zeros_like(dg_ref)

        val, vjp = jax.vjp(loss_fn, x_ref[...], t_ref[...], g_ref[...])
        dx, _, dg = vjp(jnp.ones((1, 1), F32))
        loss_ref[...] += jnp.broadcast_to(val, loss_ref.shape)
        dx_ref[...] = dx
        dxb_ref[...] = dx.astype(BF16)
        dg_ref[...] += dg

    row = pl.BlockSpec((tm, d), lambda i: (i, 0))
    one = pl.BlockSpec((1, d), lambda i: (0, 0))
    return pl.pallas_call(
        body, name="loss_and_grad", grid=(nb,), in_specs=[row, row, one],
        out_specs=[pl.BlockSpec((SUBLANES, LANES), lambda i: (0, 0)), row, row, one],
        out_shape=[jax.ShapeDtypeStruct((SUBLANES, LANES), F32), jax.ShapeDtypeStruct((t, d), F32),
                   jax.ShapeDtypeStruct((t, d), BF16), jax.ShapeDtypeStruct((1, d), F32)],
        compiler_params=_params(),
    )(x, tgt, g)


def adamw(name, recv, w, m, v, dep=None):
    rows, cols = w.shape
    n_slabs = recv.shape[0]
    recv_block_bytes = 4 * 1024 * 1024
    tm = _pick(rows, [c for c in (256, 128, 64, 32, 16, 8) if n_slabs * c * cols * 4 <= recv_block_bytes])
    c1 = 1.0 / (1.0 - ADAM_B1 ** ADAM_STEP)
    c2 = 1.0 / (1.0 - ADAM_B2 ** ADAM_STEP)

    n_dep = 0 if dep is None else 1

    def body(recv_ref, w_ref, m_ref, v_ref, *rest):
        g_ref, d_ref, nm_ref, nv_ref = rest[n_dep:]
        g = recv_ref[0].astype(F32)
        for p in range(1, n_slabs):
            g = g + recv_ref[p].astype(F32)
        nm =ADAM_B1 * m_ref[...] + (1.0 - ADAM_B1) * g
        nv = ADAM_B2 * v_ref[...] + (1.0 - ADAM_B2) * jnp.square(g)
        g_ref[...] = g
        nm_ref[...] = nm
        nv_ref[...] = nv
        d_ref[...] = -ADAM_LR * ((nm * c1) / (jnp.sqrt(nv * c2) + ADAM_EPS) + ADAM_WD * w_ref[...])

    blk = pl.BlockSpec((tm, cols), lambda i: (i, 0))
    return pl.pallas_call(
        body, name=name, grid=(rows // tm,),
        in_specs=[pl.BlockSpec((n_slabs, tm, cols), lambda i: (0, i, 0)), blk, blk, blk]
        + [pl.BlockSpec(memory_space=pl.ANY)] * n_dep,
        out_specs=[blk] * 4, out_shape=[jax.ShapeDtypeStruct((rows, cols), F32)] * 4,
        compiler_params=_params(),
    )(recv, w, m, v, *([] if dep is None else [dep]))


def _mesh_pos():
    return lax.axis_index("x"), lax.axis_index("y"), lax.axis_index("c")


def _peer(pos, mask):
    x, y, c = pos
    return (1 - x if mask & 4 else x, 1 - y if mask & 2 else y, 1 - c if mask & 1 else c)


def _linear(pos):
    return 4 * pos[0] + 2 * pos[1] + pos[2]


class Exchange:
    MASKS = {"gather": (1, 2, 3, 4, 5, 6, 7), "scatter": (1, 2, 3, 4, 5, 6, 7), "gather_chips": (1, 2, 4, 6),
             "forward": (2, 4, 6)}

    def __init__(self, xs, kind, lands=None):
        self.kind, self.masks = kind, self.MASKS[kind]
        self.xs = [] if kind == "forward" else list(xs)
        if kind == "forward":
            self.land_shape = [jax.ShapeDtypeStruct(l.shape, l.dtype) for l in lands]
        elif kind == "scatter":
            self.land_shape = [jax.ShapeDtypeStruct(x.shape, x.dtype) for x in xs]
        else:
            self.land_shape = [jax.ShapeDtypeStruct((N_DEV,) + x.shape, x.dtype) for x in xs]
        self.n = len(self.land_shape)
        copies = self.n * len(self.masks)
        self.sems = [pltpu.SemaphoreType.DMA((copies,)), pltpu.SemaphoreType.DMA((copies,)),
                     pltpu.SemaphoreType.DMA((self.n,))]

    def _copies(self, ins, outs, sems, landing):
        send_sems, recv_sems, local_sems = sems
        me = _mesh_pos()
        me_lin = _linear(me)
        local, remote = [], []
        for ti in range(self.n):
            if self.kind != "forward":
                src_mine = ins[ti].at[me_lin] if self.kind == "scatter" else ins[ti]
                local.append(pltpu.make_async_copy(src_mine, outs[ti].at[me_lin], local_sems.at[ti]))
            for j, mask in enumerate(self.masks):
                if self.kind == "forward":
                    peer = _peer(me, 1)
                    src = outs[ti].at[_linear(_peer(me, mask))]
                    dst = outs[ti].at[_linear(_peer(me, mask ^ 1 if landing else mask))]
                else:
                    peer = _peer(me, mask)
                    src = ins[ti].at[_linear(peer)] if self.kind == "scatter" else ins[ti]
                    dst = outs[ti].at[_linear(peer) if landing else me_lin]
                sem_index = ti * len(self.masks) + j
                remote.append(pltpu.make_async_remote_copy(
                    src_ref=src, dst_ref=dst, send_sem=send_sems.at[sem_index], recv_sem=recv_sems.at[sem_index],
                    device_id=peer, device_id_type=pl.DeviceIdType.MESH))
        return local, remote

    def start(self, ins, outs, sems):
        local, remote = self._copies(ins, outs, sems, landing=False)
        for cp in local + remote:
            cp.start()

    def finish(self, ins, outs, sems):
        local, remote = self._copies(ins, outs, sems, landing=True)
        for cp in remote:
            cp.wait_recv()
        for cp in remote:
            cp.wait_send()
        for cp in local:
            cp.wait()


def exchange_start(name, xs, kind, dep=None, lands=None):
    ex = Exchange(xs, kind, lands)
    hbm = pl.BlockSpec(memory_space=pltpu.HBM)
    sem = pl.BlockSpec(memory_space=pltpu.SEMAPHORE)
    if lands is None:
        lands = [lax.empty(s.shape, s.dtype) for s in ex.land_shape]
    n_src, n = len(ex.xs), ex.n
    n_inputs = n_src + n + (0 if dep is None else 1)

    def body(*refs):
        ins, lnd, sems, token = refs[:n_src], refs[n_src:n_src + n], refs[n_inputs:n_inputs + 3], refs[-1]
        ex.start(ins, lnd, sems)
        token[...] = jnp.zeros_like(token)

    res = pl.pallas_call(
        body, name=name, in_specs=[hbm] * (n_src + n) + ([] if dep is None else [pl.BlockSpec(memory_space=pl.ANY)]),
        out_specs=[sem] * 3 + [hbm] * (n_src + n) + [pl.BlockSpec(memory_space=pltpu.VMEM)],
        out_shape=ex.sems + [pltpu.HBM(x.shape, x.dtype) for x in ex.xs]
        + [pltpu.HBM(s.shape, s.dtype) for s in ex.land_shape] + [jax.ShapeDtypeStruct((SUBLANES, LANES), F32)],
        input_output_aliases={i: 3 + i for i in range(n_src + n)},
        compiler_params=pltpu.CompilerParams(has_side_effects=pltpu.SideEffectType.DATAFLOW_SIDE_EFFECTING),
    )(*[pltpu.with_memory_space_constraint(x, pltpu.HBM) for x in ex.xs + list(lands)],
      *([] if dep is None else [dep]))
    return (ex, res[:3], res[3:3 + n_src], res[3 + n_src:3 + n_src + n]), res[-1]


def exchange_wait(name, handles, after):
    ex, sems, srcs, lands = handles
    n_src, n = len(srcs), len(lands)
    hbm = pl.BlockSpec(memory_space=pltpu.HBM)
    sem = pl.BlockSpec(memory_space=pltpu.SEMAPHORE)

    def body(*refs):
        ins, lnd, sem_refs = refs[:n_src], refs[n_src:n_src + n], refs[n_src + n:n_src + n + 3]
        ex.finish(ins, lnd, sem_refs)

    res = pl.pallas_call(
        body, name=name, in_specs=[hbm] * (n_src + n) + [sem] * 3 + [pl.BlockSpec(memory_space=pl.ANY)],
        out_specs=[hbm] * (n_src + n),
        out_shape=[pltpu.HBM(x.shape, x.dtype) for x in srcs] + [pltpu.HBM(x.shape, x.dtype) for x in lands],
        input_output_aliases={i: i for i in range(n_src + n)},
        compiler_params=pltpu.CompilerParams(has_side_effects=pltpu.SideEffectType.DATAFLOW_SIDE_EFFECTING),
    )(*srcs, *lands, *sems, after)
    return res[n_src:]


def forward_start(name, chip_gather, after):
    lands = exchange_wait(name + "_wait", chip_gather, after)
    return exchange_start(name + "_forward_start", [], "forward", lands=lands)


_Z = (0, 1024)
_XBC = (1024, 2560)
_DT = (2560, 2576)
_RKV = (2576, 5648)
_PW = (5648, 5744)
_PA = (5744, 5840)
_PG = (5840, 6096)
D_IN = 6096

_SMALL = ("norm_mix_g", "ssd_conv_b", "ssd_dt_bias", "ssd_a_log", "ssd_d", "ssd_norm_g", "rwkv_mu", "rwkv_w0",
          "rwkv_a0", "rwkv_k_k", "rwkv_k_a", "rwkv_r_k", "rwkv_ln_w", "rwkv_ln_b", "norm_x_g", "norm_mem_g",
          "norm_ffn_g", "final_norm_g")
_WEIGHTS = ("norm_mix_g", "w_in", "ssd_conv_w", "ssd_conv_b", "ssd_dt_bias", "ssd_a_log", "ssd_d", "ssd_norm_g",
            "rwkv_mu", "rwkv_w0", "rwkv_w2", "rwkv_a0", "rwkv_a2", "rwkv_g2", "rwkv_k_k", "rwkv_k_a", "rwkv_r_k",
            "rwkv_ln_w", "rwkv_ln_b", "w_out", "norm_x_g", "norm_mem_g", "xattn_wq", "xattn_wk", "xattn_wv",
            "xattn_wo", "norm_ffn_g", "ffn_w1", "ffn_w2", "final_norm_g")


def _pad_lanes(x, width=LANES):
    return jnp.pad(x, ((0, 0), (0, width - x.shape[1])))


def _pack_small(vals):
    flat = jnp.concatenate([vals[n].reshape(-1) for n in _SMALL])
    rows = -(-flat.shape[0] // (LANES * SUBLANES)) * SUBLANES
    return jnp.pad(flat, (0, rows * LANES - flat.shape[0])).reshape(rows, LANES)


def _unpack_small(packed, shapes):
    flat = packed.reshape(-1)
    out, pos = {}, 0
    for n in _SMALL:
        size = 1
        for s in shapes[n]:
            size *= s
        out[n] = flat[pos:pos + size].reshape(shapes[n])
        pos += size
    return out


def _rows(w, rng):
    return w[rng[0]:rng[1]]


def sum_slabs(name, recv):
    n, rows, cols = recv.shape
    tc = _pick(cols, (256, 128))

    def body(r_ref, o_ref):
        acc = r_ref[0].astype(F32)
        for p in range(1, n):
            acc = acc + r_ref[p].astype(F32)
        o_ref[...] = acc

    return pl.pallas_call(
        body, name=name, grid=(cols // tc,), in_specs=[pl.BlockSpec((n, rows, tc), lambda j: (0, 0, j))],
        out_specs=pl.BlockSpec((rows, tc), lambda j: (0, j)), out_shape=jax.ShapeDtypeStruct((rows, cols), F32),
        compiler_params=_params(),
    )(recv)


def kernel(x, mem, norm_mix_g, w_in, ssd_conv_w, ssd_conv_b, ssd_dt_bias, ssd_a_log, ssd_d, ssd_norm_g, rwkv_mu, rwkv_w0, rwkv_w2, rwkv_a0, rwkv_a2, rwkv_g2, rwkv_k_k, rwkv_k_a, rwkv_r_k, rwkv_ln_w, rwkv_ln_b, w_out, norm_x_g, norm_mem_g, xattn_wq, xattn_wk, xattn_wv, xattn_wo, norm_ffn_g, ffn_w1, ffn_w2, final_norm_g, loss_target, m_norm_mix_g, m_w_in, m_ssd_conv_w, m_ssd_conv_b, m_ssd_dt_bias, m_ssd_a_log, m_ssd_d, m_ssd_norm_g, m_rwkv_mu, m_rwkv_w0, m_rwkv_w2, m_rwkv_a0, m_rwkv_a2, m_rwkv_g2, m_rwkv_k_k, m_rwkv_k_a, m_rwkv_r_k, m_rwkv_ln_w, m_rwkv_ln_b, m_w_out, m_norm_x_g, m_norm_mem_g, m_xattn_wq, m_xattn_wk, m_xattn_wv, m_xattn_wo, m_norm_ffn_g, m_ffn_w1, m_ffn_w2, m_final_norm_g, v_norm_mix_g, v_w_in, v_ssd_conv_w, v_ssd_conv_b, v_ssd_dt_bias, v_ssd_a_log, v_ssd_d, v_ssd_norm_g, v_rwkv_mu, v_rwkv_w0, v_rwkv_w2, v_rwkv_a0, v_rwkv_a2, v_rwkv_g2, v_rwkv_k_k, v_rwkv_k_a, v_rwkv_r_k, v_rwkv_ln_w, v_rwkv_ln_b, v_w_out, v_norm_x_g, v_norm_mem_g, v_xattn_wq, v_xattn_wk, v_xattn_wv, v_xattn_wo, v_norm_ffn_g, v_ffn_w1, v_ffn_w2, v_final_norm_g):
    given = dict(locals())
    wts = {n: given[n] for n in _WEIGHTS}
    mom_m = {n: given["m_" + n] for n in _WEIGHTS}
    mom_v = {n: given["v_" + n] for n in _WEIGHTS}
    d = D_MODEL
    xt, memt, tgt = x[0], mem[0], loss_target[0]
    tm = 256
    tm_light = 512
    tm_rwkv = 128

    big = {"w_in": jnp.transpose(w_in[0]), "w_out": w_out[0], "xattn_wq": xattn_wq[0], "xattn_wk": xattn_wk[0],
           "xattn_wv": xattn_wv[0], "xattn_wo": xattn_wo[0], "ffn_w1": ffn_w1[0], "ffn_w2": ffn_w2[0]}
    small_sh = {"ssd_conv_w": ssd_conv_w.reshape(4, -1), "rwkv_w2": rwkv_w2[0], "rwkv_a2": rwkv_a2[0],
                "rwkv_g2": rwkv_g2[0]}
    cast_one = lambda n, deps=(): rowwise_fwd("cast_" + n, cast_fn, [big[n]], [], [(big[n].shape[1], BF16)],
                                              256 if big[n].shape[0] % 256 == 0 else big[n].shape[0], deps=deps)[0]
    gather_in, token_in = exchange_start("gather_in_start", [cast_one("w_in")] + list(small_sh.values()), "gather_chips")
    cast = {n: cast_one(n, deps=[token_in]) for n in big if n != "w_in"}
    late_a = ("w_out", "xattn_wq", "xattn_wk", "xattn_wv", "xattn_wo")
    late_b = ("ffn_w1", "ffn_w2")
    gather_a, token_a = exchange_start("gather_attn_start", [cast[n] for n in late_a], "gather_chips", dep=token_in)
    gather_b, token_b = exchange_start("gather_ffn_start", [cast[n] for n in late_b], "gather_chips", dep=token_a)
    (h1,) = rowwise_fwd("norm_mix", rmsnorm_fn, [xt], [norm_mix_g], [(d, BF16)], tm_light, deps=[token_b])
    forward_in, token_in = forward_start("gather_in", gather_in, after=h1)
    gathered = exchange_wait("gather_in_forward_wait", forward_in, after=token_in)
    g_big = {"w_in": gathered[0]}
    g_small = dict(zip(small_sh, gathered[1:]))

    pad_rows = lambda a: jnp.pad(a, ((0, LANES - a.shape[0]), (0, 0)))
    w_in_t = g_big["w_in"].reshape(D_IN, d)
    wt_z, wt_xbc, wt_rkv = (_rows(w_in_t, r) for r in (_Z, _XBC, _RKV))
    wt_ps = jnp.concatenate([_rows(w_in_t, _PG)] + [pad_rows(_rows(w_in_t, r)) for r in (_PW, _PA, _DT)], axis=0)
    unshard_cols = lambda g: jnp.transpose(g, (1, 0, 2)).reshape(g.shape[1], -1)
    conv_w_f = unshard_cols(g_small["ssd_conv_w"])
    w2p, a2p = pad_rows(unshard_cols(g_small["rwkv_w2"])), pad_rows(unshard_cols(g_small["rwkv_a2"]))
    g2_f = unshard_cols(g_small["rwkv_g2"])

    mu = rwkv_mu
    mu_rkv, mu_pg = mu[:, :3072], mu[:, 3264:3520]
    mu_pwa = jnp.concatenate([_pad_lanes(mu[:, 3072:3168]), _pad_lanes(mu[:, 3168:3264])], axis=1)
    dt_bias_p, a_log_p, d_p = _pad_lanes(ssd_dt_bias), _pad_lanes(ssd_a_log), _pad_lanes(ssd_d)
    r_k_row = rwkv_r_k.reshape(1, RWKV_WIDTH)
    g_final = final_norm_g.reshape(1, d)

    u_z = mm("in_z", h1, wt_z, "nt")
    u_xbc = mm("in_xbc", h1, wt_xbc, "nt")
    u_rkv = mm("in_rkv", h1, wt_rkv, "nt")
    u_ps = mm("in_narrow", h1, wt_ps, "nt")

    ssd_pre_rows = lambda: [Rows(u_xbc, shifts=(1, 2, 3)), Rows(u_ps, LANES, 4)]
    ssd_pre_params = [conv_w_f, ssd_conv_b, dt_bias_p]
    xs, bm, cm, dt = rowwise_fwd("ssd_pre", ssd_pre_fn, ssd_pre_rows(), ssd_pre_params,
                                 [(SSD_WIDTH, F32), (256, F32), (256, F32), (LANES, F32)], tm)
    y_scan, ssd_ck = ssd_scan_fwd(xs, bm, cm, dt, a_log_p, d_p)
    (y_ssd,) = rowwise_fwd("ssd_post", ssd_post_fn, [y_scan, u_z], [ssd_norm_g], [(SSD_WIDTH, BF16)], tm_light,
                           into=(None, d, 0))

    rwkv_pre_rows = lambda: [Rows(u_rkv, shifts=(1,)), Rows(u_ps, 2 * LANES, 0, shifts=(1,)), Rows(u_ps, 2 * LANES, 1, shifts=(1,))]
    rwkv_pre_params = [mu_rkv, mu_pg, mu_pwa, rwkv_w0, w2p, rwkv_a0, a2p, g2_f, rwkv_k_k, rwkv_k_a]
    forward_a, token_a = forward_start("gather_attn", gather_a, after=y_scan)
    r_, lw_, k_, v_, kap_, b_, gate_ = rowwise_fwd("rwkv_pre", rwkv_pre_fn, rwkv_pre_rows(), rwkv_pre_params,
                                                   [(RWKV_WIDTH, F32)] * 7, tm_rwkv, deps=[token_a])
    ys_r, rwkv_ck, rwkv_inv = rwkv_scan_fwd(r_, lw_, k_, v_, kap_, b_)
    forward_b, token_b = forward_start("gather_ffn", gather_b, after=ys_r)
    g_big.update(zip(late_a, exchange_wait("gather_attn_forward_wait", forward_a, after=token_b)))
    w_out_f = g_big["w_out"].reshape(d, d)
    wq_f, wk_f, wv_f, wo_f = (g_big[n].reshape(d, d) for n in ("xattn_wq", "xattn_wk", "xattn_wv", "xattn_wo"))
    rwkv_post_params = [rwkv_ln_w, rwkv_ln_b, r_k_row]
    (ycat,) = rowwise_fwd("rwkv_post", rwkv_post_fn, [ys_r, r_, k_, v_, gate_], rwkv_post_params,
                          [(RWKV_WIDTH, BF16)], tm, into=(y_ssd, d, 1))
    x1 = mm("out_proj", ycat, w_out_f, "nn", res=xt)

    (h2,) = rowwise_fwd("norm_x", rmsnorm_fn, [x1], [norm_x_g], [(d, BF16)], tm_light)
    (mn,) = rowwise_fwd("norm_mem", rmsnorm_fn, [memt], [norm_mem_g], [(d, BF16)], tm)
    q = mm("xattn_q", h2, wq_f, "nn", out_dtype=BF16)
    kx = mm("xattn_k", mn, wk_f, "nn")
    vx = mm("xattn_v", mn, wv_f, "nn")
    (o,) = rowwise_fwd("xattn", attn_fn, [q], [kx, vx], [(d, BF16)], tm)
    x2 = mm("xattn_o", o, wo_f, "nn", res=x1)

    (h3,) = rowwise_fwd("norm_ffn", rmsnorm_fn, [x2], [norm_ffn_g], [(d, BF16)], tm_light)
    w1_s, w2_g = exchange_wait("gather_ffn_forward_wait", forward_b, after=h3)
    w2_f = w2_g.reshape(D_FF, d)
    relu2_epi = lambda acc: (jnp.square(jnp.maximum(acc, 0.0)), jnp.maximum(acc, 0.0))
    hid, relu_a = mm("ffn_1", h3, w1_s, "nn", b_slabs=N_DEV, epi=relu2_epi, out_dtypes=[BF16, BF16])
    x3 = mm("ffn_2", hid, w2_f, "nn", res=x2)

    loss_blk, dx3, dx3_b, dg_final = loss_and_grad(x3, tgt, g_final, tm_light)

    grads = {}
    grads["ffn_w2"] = mm("d_ffn_w2", hid, dx3_b, "tn", out_dtype=BF16).reshape(N_DEV, D_FF // N_DEV, d)
    sc_w2, tok = exchange_start("scatter_ffn_w2_start", [grads["ffn_w2"]], "scatter")
    da = mm("d_hid", dx3_b, w2_f, "nt", dep=tok, epi=lambda acc, ra: (2.0 * acc * ra,), extras=[relu_a],
            out_dtypes=[BF16])
    grads["ffn_w1"] = mm("d_ffn_w1", h3, da, "tn", out_dtype=BF16, out_slabs=N_DEV)
    sc_w1, tok = exchange_start("scatter_ffn_w1_start", [grads["ffn_w1"]], "scatter")
    dh3 = mm("d_h3", da, w1_s, "nt", out_dtype=BF16, b_slabs=N_DEV, dep=tok)
    (dx2,), (dg_ffn,) = rowwise_bwd("norm_ffn_bwd", rmsnorm_fn, [x2], [norm_ffn_g], [[dh3]], tm_light, [F32], row_add=[dx3])

    grads["xattn_wo"] = mm("d_wo", o, dx2, "tn", out_dtype=BF16).reshape(N_DEV, d // N_DEV, d)
    sc_wo, tok = exchange_start("scatter_wo_start", [grads["xattn_wo"]], "scatter")
    d_o = mm("d_o", dx2, wo_f, "nt", out_dtype=BF16, dep=tok)
    (dq,), (dkx, dvx) = rowwise_bwd("xattn_bwd", attn_fn, [q], [kx, vx], [[d_o]], tm, [BF16])
    grads["xattn_wq"] = mm("d_wq", h2, dq, "tn", out_dtype=BF16).reshape(N_DEV, d // N_DEV, d)
    grads["xattn_wk"] = mm("d_wk", mn, dkx, "tn", out_dtype=BF16).reshape(N_DEV, d // N_DEV, d)
    grads["xattn_wv"] = mm("d_wv", mn, dvx, "tn", out_dtype=BF16).reshape(N_DEV, d // N_DEV, d)
    qkv = ("xattn_wq", "xattn_wk", "xattn_wv")
    sc_qkv, tok = exchange_start("scatter_qkv_start", [grads[n] for n in qkv], "scatter")
    dmn = mm("d_mn_v", dvx, wv_f, "nt", res=mm("d_mn_k", dkx, wk_f, "nt", dep=tok))
    _, (dg_mem,) = rowwise_bwd("norm_mem_bwd", rmsnorm_fn, [memt], [norm_mem_g], [[dmn]], tm, [None])
    dh2 = mm("d_h2", dq, wq_f, "nt", out_dtype=BF16, dep=dg_mem)
    (dx1,), (dg_x,) = rowwise_bwd("norm_x_bwd", rmsnorm_fn, [x1], [norm_x_g], [[dh2]], tm_light, [F32], row_add=[dx2])

    grads["w_out"] = mm("d_w_out", ycat, dx1, "tn", out_dtype=BF16).reshape(N_DEV, d // N_DEV, d)
    sc_wout, tok = exchange_start("scatter_w_out_start", [grads["w_out"]], "scatter")
    d_ycat = mm("d_ycat", dx1, w_out_f, "nt", out_dtype=BF16, dep=tok)

    (d_ys, d_r1, d_k1, d_v1, d_gate), (dln_w, dln_b, dr_k) = rowwise_bwd(
        "rwkv_post_bwd", rwkv_post_fn, [ys_r, r_, k_, v_, gate_], rwkv_post_params,
        [[Rows(d_ycat, RWKV_WIDTH, 1)]], tm, [BF16] * 5)
    d_r2, d_lw, d_k2, d_v2, d_kap, d_b = rwkv_scan_bwd(r_, lw_, k_, v_, kap_, b_, rwkv_ck, rwkv_inv, d_ys)
    (du_rkv, du_pg, du_pwa), rwkv_pg = rowwise_bwd(
        "rwkv_pre_bwd", rwkv_pre_fn, rwkv_pre_rows(), rwkv_pre_params,
        [[d_r1, d_r2], [d_lw], [d_k1, d_k2], [d_v1, d_v2], [d_kap], [d_b], [d_gate]], tm_rwkv, [BF16] * 3)
    dmu_rkv, dmu_pg, dmu_pwa, dw0, dw2p, da0, da2p, dg2, dk_k, dk_a = rwkv_pg

    (d_yscan, du_z), (dssd_norm_g,) = rowwise_bwd("ssd_post_bwd", ssd_post_fn, [y_scan, u_z], [ssd_norm_g],
                                                  [[Rows(d_ycat, SSD_WIDTH, 0)]], tm_light, [BF16, BF16])
    dxs, dbm, dcm, ddt2, da_log_p, dd_p = ssd_scan_bwd(xs, bm, cm, dt, a_log_p, d_p, ssd_ck, d_yscan)
    (du_xbc, du_dt), (dconv_w, dconv_b, ddt_bias_p) = rowwise_bwd(
        "ssd_pre_bwd", ssd_pre_fn, ssd_pre_rows(), ssd_pre_params,
        [[dxs], [dbm], [dcm], [ddt2[0], ddt2[1]]], tm, [BF16, BF16])
    du_ps = jnp.concatenate([du_pg, du_pwa, du_dt], axis=1)

    dwt_z = mm("d_w_z", du_z, h1, "tn", out_dtype=BF16)
    dwt_xbc = mm("d_w_xbc", du_xbc, h1, "tn", out_dtype=BF16)
    dwt_rkv = mm("d_w_rkv", du_rkv, h1, "tn", out_dtype=BF16)
    dwt_ps = mm("d_w_narrow", du_ps, h1, "tn", out_dtype=BF16)
    dwt_full = jnp.concatenate([dwt_z, dwt_xbc, dwt_ps[512:528], dwt_rkv, dwt_ps[256:352], dwt_ps[384:480], dwt_ps[0:256]],
                               axis=0)
    to_slabs = lambda g: jnp.transpose(g.reshape(g.shape[0], N_DEV, -1), (1, 0, 2))
    grads["w_in"] = dwt_full.reshape(N_DEV, D_IN // N_DEV, d)
    grads["ssd_conv_w"] = to_slabs(dconv_w)
    grads["rwkv_w2"] = to_slabs(dw2p[:96])
    grads["rwkv_a2"] = to_slabs(da2p[:96])
    grads["rwkv_g2"] = to_slabs(dg2)
    tail = ("w_in", "ssd_conv_w", "rwkv_w2", "rwkv_a2", "rwkv_g2")
    sc_tail, tok = exchange_start("scatter_tail_start", [grads[n] for n in tail], "scatter")
    dh1 = mm("d_h1_z", du_z, wt_z, "nn", dep=tok)
    dh1 = mm("d_h1_xbc", du_xbc, wt_xbc, "nn", res=dh1)
    dh1 = mm("d_h1_rkv", du_rkv, wt_rkv, "nn", res=dh1)
    dh1 = mm("d_h1_narrow", du_ps, wt_ps, "nn", res=dh1)
    (dx,), (dg_mix,) = rowwise_bwd("norm_mix_bwd", rmsnorm_fn, [xt], [norm_mix_g], [[dh1]], tm_light, [F32], row_add=[dx1])

    dmu =jnp.concatenate([dmu_rkv, dmu_pwa[:, 0:96], dmu_pwa[:, 128:224], dmu_pg], axis=1)
    small_grads = {
        "norm_mix_g": dg_mix, "ssd_conv_b": dconv_b, "ssd_dt_bias": ddt_bias_p[:, :16], "ssd_a_log": da_log_p[:, :16],
        "ssd_d": dd_p[:, :16], "ssd_norm_g": dssd_norm_g, "rwkv_mu": dmu, "rwkv_w0": dw0, "rwkv_a0": da0,
        "rwkv_k_k": dk_k, "rwkv_k_a": dk_a, "rwkv_r_k": dr_k, "rwkv_ln_w": dln_w, "rwkv_ln_b": dln_b,
        "norm_x_g": dg_x, "norm_mem_g": dg_mem, "norm_ffn_g": dg_ffn, "final_norm_g": dg_final}

    gather_small, tok = exchange_start("gather_small_start", [_pack_small(small_grads)], "gather")
    received = {}
    for names, handle in ((("ffn_w2",), sc_w2), (("ffn_w1",), sc_w1), (("xattn_wo",), sc_wo), (qkv, sc_qkv),
                          (("w_out",), sc_wout)):
        received.update(zip(names, exchange_wait("scatter_" + names[0] + "_wait", handle, after=tok)))

    out_g, out_d, out_m, out_v = {}, {}, {}, {}

    def run_adamw(n, dep):
        shape = wts[n].shape
        two_d = lambda a: a.reshape(-1, shape[-1])
        if n == "w_in":
            recv = jnp.transpose(sum_slabs("sum_w_in", received[n]))[None]
        else:
            recv = received[n].reshape(N_DEV, -1, shape[-1])
        res = adamw("adamw_" + n, recv, two_d(wts[n]), two_d(mom_m[n]), two_d(mom_v[n]), dep=dep)
        out_g[n], out_d[n], out_m[n], out_v[n] = (r.reshape(shape) for r in res)
        return res[0]

    last = None
    for n in ("ffn_w2", "ffn_w1", "xattn_wo") + qkv + ("w_out",):
        last = run_adamw(n, last)
    received.update(zip(tail, exchange_wait("scatter_tail_wait", sc_tail, after=last)))
    for n in tail:
        last = run_adamw(n, last)
    (small_all,) = exchange_wait("gather_small_wait", gather_small, after=last)
    res = adamw("adamw_small", small_all, _pack_small(wts), _pack_small(mom_m), _pack_small(mom_v))
    shapes = {n: wts[n].shape for n in _SMALL}
    for dst, packed in zip((out_g, out_d, out_m, out_v), res):
        dst.update(_unpack_small(packed, shapes))

    loss = lax.psum(loss_blk[0, 0], ("x", "y", "c"))
    return (loss, dx[None], *[out_g[n] for n in _WEIGHTS], *[out_d[n] for n in _WEIGHTS],
            *[out_m[n] for n in _WEIGHTS], *[out_v[n] for n in _WEIGHTS])
```

```python
import functools

import jax
import jax.numpy as jnp
from jax import lax
from jax.experimental import pallas as pl
from jax.experimental.pallas import tpu as pltpu

F32 = jnp.float32
BF16 = jnp.bfloat16

N_DEV = 8
D_MODEL = 2048
NORM_EPS = 1e-6
SSD_WIDTH = 1024
SSD_HEAD_DIM = 64
SSD_STATE = 128
SSD_CHUNK = 128
SSD_HEADS_PER_GROUP = 8
RWKV_WIDTH = 1024
RWKV_HEADS = 16
RWKV_HEAD_DIM = 64
RWKV_LN_EPS = 64e-5
RWKV_CHUNK = 128
RWKV_HEADS_PER_STEP = 16
XATTN_HEADS = 4
XATTN_HEAD_DIM = 512
D_FF = 8192
LANES = 128
SUBLANES = 8
VMEM_LIMIT = 56 * 1024 * 1024

ADAM_LR = 0.001
ADAM_B1 = 0.9
ADAM_B2 = 0.999
ADAM_EPS = 1e-08
ADAM_WD = 0.01
ADAM_STEP = 10

_DN = {"nn": ((1,), (0,)), "nt": ((1,), (1,)), "tn": ((0,), (0,))}


def _dg(a, b, mode):
    (ca,), (cb,) = _DN[mode]
    dn = (((ca + 1,), (cb + 1,)), ((0,), (0,))) if a.ndim == 3 else (((ca,), (cb,)), ((), ()))
    return lax.dot_general(a, b, dn, preferred_element_type=F32)


@functools.partial(jax.custom_vjp, nondiff_argnums=(2,))
def bdot(a, b, mode):
    return _dg(a.astype(BF16), b.astype(BF16), mode)


def _bdot_fwd(a, b, mode):
    return bdot(a, b, mode), (a, b)


def _bdot_bwd(mode, res, g):
    a, b = res
    ab, bb, gb = a.astype(BF16), b.astype(BF16), g.astype(BF16)
    if mode == "nn":
        da, db = _dg(gb, bb, "nt"), _dg(ab, gb, "tn")
    elif mode == "nt":
        da, db = _dg(gb, bb, "nn"), _dg(gb, ab, "tn")
    else:
        da, db = _dg(bb, gb, "nt"), _dg(ab, gb, "nn")
    return da.astype(a.dtype), db.astype(b.dtype)


bdot.defvjp(_bdot_fwd, _bdot_bwd)


def _split2(x):
    hi = x.astype(BF16)
    return hi, (x - hi.astype(F32)).astype(BF16)


def _dot01(x, m01):
    hi, lo = _split2(x)
    return _dg(hi, m01, "nn") + _dg(lo, m01, "nn")


def _exact_dot_impl(a, b, mode, exact):
    if exact == "a":
        ae = a.astype(BF16)
        return sum(_dg(ae, part, mode) for part in _split2(b))
    be = b.astype(BF16)
    return sum(_dg(part, be, mode) for part in _split2(a))


@functools.partial(jax.custom_vjp, nondiff_argnums=(2, 3))
def exact_dot(a, b, mode, exact):
    return _exact_dot_impl(a, b, mode, exact)


def _exact_dot_fwd(a, b, mode, exact):
    return _exact_dot_impl(a, b, mode, exact), (a, b)


def _exact_dot_bwd(mode, exact, res, g):
    a, b = res
    if exact == "a":
        db = {"nn": lambda: _exact_dot_impl(a, g, "tn", "a"), "nt": lambda: _exact_dot_impl(g, a, "tn", "b"),
              "tn": lambda: _exact_dot_impl(a, g, "nn", "a")}[mode]()
        return jnp.zeros_like(a), db
    da = {"nn": lambda: _exact_dot_impl(g, b, "nt", "b"), "nt": lambda: _exact_dot_impl(g, b, "nn", "b"),
          "tn": lambda: _exact_dot_impl(b, g, "nt", "a")}[mode]()
    return da, jnp.zeros_like(b)


exact_dot.defvjp(_exact_dot_fwd, _exact_dot_bwd)


def _head_indicator(width, heads, transpose):
    hd = width // heads
    shape = (LANES, width) if transpose else (width, LANES)
    lane = lax.broadcasted_iota(jnp.int32, shape, 1 if not transpose else 0)
    pos = lax.broadcasted_iota(jnp.int32, shape, 0 if not transpose else 1)
    return ((pos >= lane * hd) & (pos < lane * hd + hd)).astype(BF16)


@jax.custom_vjp
def head_sum(x):
    w = x.shape[-1]
    e = _head_indicator(w, w // RWKV_HEAD_DIM, False)
    et = _head_indicator(w, w // RWKV_HEAD_DIM, True)
    return _dot01(_dot01(x, e), et)


head_sum.defvjp(lambda x: (head_sum(x), None), lambda _, g: (head_sum(g),))


def rmsnorm_fn(x, g):
    y = x * lax.rsqrt(jnp.mean(x * x, axis=-1, keepdims=True) + NORM_EPS)
    return ((y * g).astype(BF16),)


def cast_fn(x):
    return (x.astype(BF16),)


def ssd_pre_fn(xbc, xbc1, xbc2, xbc3, dt_raw, conv_w, conv_b, dt_bias):
    c = conv_w[3:4] * xbc + conv_w[2:3] * xbc1 + conv_w[1:2] * xbc2 + conv_w[0:1] * xbc3 + conv_b
    act = c * jax.nn.sigmoid(c)
    dt = jax.nn.softplus(dt_raw + dt_bias)
    return act[:, :SSD_WIDTH], act[:, SSD_WIDTH:SSD_WIDTH + 256], act[:, SSD_WIDTH + 256:], dt


def ssd_post_fn(yscan, z, norm_g):
    y = yscan * (z * jax.nn.sigmoid(z))
    half = SSD_WIDTH // 2
    parts = []
    for g in range(2):
        yg = y[:, g * half:(g + 1) * half]
        parts.append(yg * lax.rsqrt(jnp.mean(yg * yg, axis=-1, keepdims=True) + NORM_EPS))
    return ((jnp.concatenate(parts, axis=-1) * norm_g).astype(BF16),)


def rwkv_pre_fn(rkv, rkv_p, pg, pg_p, pwa, pwa_p, mu_rkv, mu_pg, mu_pwa, w0, w2p, a0, a2p, g2, k_k, k_a):
    w = RWKV_WIDTH
    rkv = rkv + (rkv_p - rkv) * mu_rkv
    pg = pg + (pg_p - pg) * mu_pg
    pwa = pwa + (pwa_p - pwa) * mu_pwa
    r, k, v = rkv[:, :w], rkv[:, w:2 * w], rkv[:, 2 * w:]
    pw, pa = pwa[:, :LANES], pwa[:, LANES:]
    w_log = -jax.nn.softplus(-(w0 + bdot(jnp.tanh(pw), w2p, "nn"))) - 0.5
    lw = -jnp.exp(w_log)
    iclr = jax.nn.sigmoid(a0 + bdot(pa, a2p, "nn"))
    gate = bdot(jax.nn.sigmoid(pg), g2, "nn")
    kk = k * k_k
    kap = kk * lax.rsqrt(jnp.maximum(head_sum(kk * kk), 1e-24))
    k_mod = k * (1.0 + (iclr - 1.0) * k_a)
    return r, lw, k_mod, v, kap, kap * iclr, gate


def rwkv_post_fn(ys, r, k_mod, v, gate, ln_w, ln_b, r_k):
    inv_n = 1.0 / RWKV_HEAD_DIM
    mean = head_sum(ys) * inv_n
    yc = ys - mean
    var = head_sum(yc * yc) * inv_n
    yn = yc * lax.rsqrt(var + RWKV_LN_EPS) * ln_w + ln_b
    bonus = head_sum(r * k_mod * r_k) * v
    return (((yn + bonus) * gate).astype(BF16),)


def attn_fn(q, kx, vx):
    outs = []
    for h in range(XATTN_HEADS):
        sl = slice(h * XATTN_HEAD_DIM, (h + 1) * XATTN_HEAD_DIM)
        s = bdot(q[:, sl], kx[:, sl], "nt") * (XATTN_HEAD_DIM ** -0.5)
        s = s - jnp.max(s, axis=-1, keepdims=True)
        p = jnp.exp(s)
        p = p / jnp.sum(p, axis=-1, keepdims=True)
        outs.append(bdot(p, vx[:, sl], "nn"))
    return (jnp.concatenate(outs, axis=-1).astype(BF16),)


def loss_fn(x, tgt, g):
    y = x * lax.rsqrt(jnp.mean(x * x, axis=-1, keepdims=True) + NORM_EPS) * g
    err = jnp.square(y - tgt)
    return 0.5 * jnp.sum(jnp.mean(err, axis=-1, keepdims=True), axis=0, keepdims=True)


def _tri_masks(n):
    row = lax.broadcasted_iota(jnp.int32, (n, n), 0)
    col = lax.broadcasted_iota(jnp.int32, (n, n), 1)
    return col <= row, col < row, row == col


@jax.custom_vjp
def unit_lower_inverse(a):
    c = a.shape[-1]
    eye = _tri_masks(c)[2].astype(F32)
    m = -a
    inv = eye + m
    n = 1
    while n * 2 < c:
        m = bdot(m, m, "nn")
        inv = bdot(inv, eye + m, "nn")
        n *= 2
    return inv


def _unit_lower_inverse_fwd(a):
    inv = unit_lower_inverse(a)
    return inv, inv


def _unit_lower_inverse_bwd(inv, g):
    return (-bdot(bdot(inv, g, "tn"), inv, "nt"),)


unit_lower_inverse.defvjp(_unit_lower_inverse_fwd, _unit_lower_inverse_bwd)


@jax.custom_vjp
def known_inverse(a, inv):
    return inv


known_inverse.defvjp(lambda a, inv: (inv, inv),
                     lambda inv, g: (_unit_lower_inverse_bwd(inv, g)[0], jnp.zeros_like(inv)))


def rwkv_chunk_fn(st0, r, lw, k, v, kap, b, inv=None):
    h, c = r.shape[0], r.shape[1]
    incl, strict, _ = _tri_masks(c)
    cum = exact_dot(jnp.broadcast_to(incl.astype(F32), (h, c, c)), lw, "nn", "a")
    g_in = jnp.exp(cum)
    g_prev = jnp.exp(cum - lw)
    g_inv = jnp.exp(-cum)
    g_end = jnp.exp(cum[:, c - 1:c, :] - cum)
    kap_t, k_t, b_t, r_t = kap * g_prev, k * g_inv, b * g_inv, r * g_in
    a_ub = jnp.where(strict, bdot(kap_t, b_t, "nt"), 0.0)
    a_vk = jnp.where(strict, bdot(kap_t, k_t, "nt"), 0.0)
    rhs = -(bdot(kap_t, st0, "nt") + bdot(a_vk, v, "nn"))
    inv = unit_lower_inverse(a_ub) if inv is None else known_inverse(a_ub, inv)
    u = bdot(inv, rhs, "nn")
    y = (bdot(r_t, st0, "nt")
         + bdot(jnp.where(incl, bdot(r_t, k_t, "nt"), 0.0), v, "nn")
         + bdot(jnp.where(incl, bdot(r_t, b_t, "nt"), 0.0), u, "nn"))
    st1 = jnp.exp(cum[:, c - 1:c, :]) * st0 + bdot(v, k * g_end, "tn") + bdot(u, b * g_end, "tn")
    return y, st1, inv


def ssd_chunk_fn(group, h0, xs, bm, cm, dt, a_log, d_skip):
    q, nh = xs.shape[0], SSD_HEADS_PER_GROUP
    causal, _, _ = _tri_masks(q)
    a_row = -jnp.exp(a_log)
    cs_all = exact_dot(causal.astype(F32), dt * a_row, "nn", "a")
    cs_t = cs_all.T
    lanes = range(group * nh, (group + 1) * nh)
    cs = jnp.stack([cs_all[:, hl:hl + 1] for hl in lanes])
    cs_row = jnp.stack([cs_t[hl:hl + 1, :] for hl in lanes])
    dt_h = jnp.stack([dt[:, hl:hl + 1] for hl in lanes])
    d_h = jnp.stack([d_skip[:, hl:hl + 1] for hl in lanes])
    x = _stack_lanes(xs, nh)
    h0s = _stack_rows(h0, nh)
    lmat = jnp.where(causal, jnp.exp(jnp.where(causal, cs - cs_row, 0.0)), 0.0)
    cb = bdot(cm, bm, "nt")
    xdt = x * dt_h
    cl = cs[:, q - 1:q, :]
    cm_b = jnp.broadcast_to(cm, (nh,) + cm.shape)
    bm_b = jnp.broadcast_to(bm, (nh,) + bm.shape)
    y = bdot(cb * lmat, xdt, "nn") + bdot(cm_b, h0s, "nt") * jnp.exp(cs) + x * d_h
    h1 = h0s * jnp.exp(cl) + bdot(xdt * jnp.exp(cl - cs), bm_b, "tn")
    return jnp.concatenate([y[e] for e in range(nh)], axis=-1), jnp.concatenate([h1[e] for e in range(nh)], axis=0)


class Rows:
    def __init__(self, arr, w=None, cb=0, shifts=()):
        self.arr, self.w, self.cb, self.shifts = arr, (arr.shape[1] if w is None else w), cb, tuple(shifts)


def _as_rows(x):
    return x if isinstance(x, Rows) else Rows(x)


def _shift_down(x, halo, k):
    rolled = pltpu.roll(x, k, 0)
    first = rolled[0:SUBLANES]
    rid = lax.broadcasted_iota(jnp.int32, first.shape, 0)
    patched = jnp.where(rid < k, pltpu.roll(halo, k, 0), first)
    return jnp.concatenate([patched, rolled[SUBLANES:]], axis=0)


def _shift_up(g, carry, k):
    tm = g.shape[0]
    rolled = pltpu.roll(g, tm - k, 0)
    last = rolled[tm - SUBLANES:]
    rid = lax.broadcasted_iota(jnp.int32, last.shape, 0)
    patched = jnp.where(rid >= SUBLANES - k, pltpu.roll(carry, SUBLANES - k, 0), last)
    return jnp.concatenate([rolled[:tm - SUBLANES], patched], axis=0)


def _params():
    return pltpu.CompilerParams(vmem_limit_bytes=VMEM_LIMIT)


def _load_rows(refs, pos, rins, first_block):
    vals = []
    for r in rins:
        x = refs[pos][...].astype(F32) if refs[pos].dtype != F32 else refs[pos][...]
        pos += 1
        vals.append(x)
        if r.shifts:
            halo = refs[pos][...]
            pos += 1
            halo = jnp.where(first_block, jnp.zeros_like(halo), halo)
            for k in r.shifts:
                vals.append(_shift_down(x, halo, k))
    return vals, pos


def _row_specs(rins, tm, blk):
    specs, args = [], []
    for r in rins:
        specs.append(pl.BlockSpec((tm, r.w), lambda i, cb=r.cb: (blk(i), cb)))
        args.append(r.arr)
        if r.shifts:
            per = tm // SUBLANES
            specs.append(pl.BlockSpec((SUBLANES, r.w), lambda i, cb=r.cb: (jnp.maximum(blk(i) * per - 1, 0), cb)))
            args.append(r.arr)
    return specs, args


def rowwise_fwd(name, fn, rins, params, outs, tm, deps=(), into=None):
    rins = [_as_rows(r) for r in rins]
    t = rins[0].arr.shape[0]
    tm = min(tm, t)
    nb = t // tm
    specs, args = _row_specs(rins, tm, lambda i: i)
    for p in params:
        specs.append(pl.BlockSpec(p.shape, lambda i: (0, 0)))
        args.append(p)
    for dep in deps:
        specs.append(pl.BlockSpec(memory_space=pl.ANY))
        args.append(dep)
    out_specs = [pl.BlockSpec((tm, w), lambda i: (i, 0)) for w, _ in outs]
    out_shape = [jax.ShapeDtypeStruct((t, w), dt) for w, dt in outs]
    aliases = {}
    if into is not None:
        target, total_width, col_block = into
        out_specs = [pl.BlockSpec((tm, outs[0][0]), lambda i: (i, col_block))]
        out_shape = [jax.ShapeDtypeStruct((t, total_width), outs[0][1])]
        if target is not None:
            aliases = {len(args): 0}
            specs.append(pl.BlockSpec(memory_space=pl.ANY))
            args.append(target)
    n_in = len(args)

    def body(*refs):
        vals, pos = _load_rows(refs, 0, rins, pl.program_id(0) == 0)
        pv = [refs[pos + j][...] for j in range(len(params))]
        res = fn(*vals, *pv)
        for o_ref, o in zip(refs[n_in:], res):
            o_ref[...] = o.astype(o_ref.dtype)

    return pl.pallas_call(
        body, name=name, grid=(nb,), in_specs=specs, out_specs=out_specs, out_shape=out_shape,
        input_output_aliases=aliases, compiler_params=_params(),
    )(*args)


def rowwise_bwd(name, fn, rins, params, cts, tm, grad_dtypes, row_add=None):
    rins = [_as_rows(r) for r in rins]
    cts = [[_as_rows(c) for c in lst] for lst in cts]
    row_add = [_as_rows(a) for a in (row_add or [])]
    t = rins[0].arr.shape[0]
    tm = min(tm, t)
    nb = t // tm
    rev = lambda i: nb - 1 - i
    specs, args = _row_specs(rins, tm, rev)
    for p in params:
        specs.append(pl.BlockSpec(p.shape, lambda i: (0, 0)))
        args.append(p)
    flat_cts = [c for lst in cts for c in lst] + row_add
    for c in flat_cts:
        specs.append(pl.BlockSpec((tm, c.w), lambda i, cb=c.cb: (rev(i), cb)))
        args.append(c.arr)
    n_in = len(args)
    want = [i for i, d in enumerate(grad_dtypes) if d is not None]
    out_specs = [pl.BlockSpec((tm, rins[i].w), lambda i_: (rev(i_), 0)) for i in want]
    out_shape = [jax.ShapeDtypeStruct((t, rins[i].w), grad_dtypes[i]) for i in want]
    out_specs += [pl.BlockSpec(p.shape, lambda i: (0, 0)) for p in params]
    out_shape += [jax.ShapeDtypeStruct(p.shape, F32) for p in params]
    n_out = len(out_shape)
    scratch = [pltpu.VMEM((SUBLANES, r.w), F32) for r in rins for _ in r.shifts]

    def body(*refs):
        i = pl.program_id(0)
        vals, pos = _load_rows(refs, 0, rins, rev(i) == 0)
        pv = [refs[pos + j][...] for j in range(len(params))]
        pos += len(params)
        outs, vjp = jax.vjp(fn, *vals, *pv)
        ct_vals = []
        for o, lst in zip(outs, cts):
            acc = None
            for _ in lst:
                cv = refs[pos][...].astype(F32)
                pos += 1
                acc = cv if acc is None else acc + cv
            ct_vals.append(acc.astype(o.dtype))
        adds = [refs[pos + j][...].astype(F32) for j in range(len(row_add))]
        grads = vjp(tuple(ct_vals))
        out_refs = refs[n_in:n_in + n_out]
        carry_refs = refs[n_in + n_out:]

        @pl.when(i == 0)
        def _():
            for cr in carry_refs:
                cr[...] = jnp.zeros_like(cr)
            for pr in out_refs[len(want):]:
                pr[...] = jnp.zeros_like(pr)

        gi, ci, oi = 0, 0, 0
        for idx, r in enumerate(rins):
            d = grads[gi]
            gi += 1
            for k in r.shifts:
                dk = grads[gi]
                gi += 1
                d = d + _shift_up(dk, carry_refs[ci][...], k)
                carry_refs[ci][...] = dk[0:SUBLANES]
                ci += 1
            if idx == 0:
                for a in adds:
                    d = d + a
            if grad_dtypes[idx] is not None:
                out_refs[oi][...] = d.astype(out_refs[oi].dtype)
                oi += 1
        for pr, gp in zip(out_refs[len(want):], grads[gi:]):
            pr[...] += gp

    res = pl.pallas_call(
        body, name=name, grid=(nb,), in_specs=specs, out_specs=out_specs, out_shape=out_shape,
        scratch_shapes=scratch, compiler_params=_params(),
    )(*args)
    return res[:len(want)], res[len(want):]


def _pick(n, pref):
    for c in pref:
        if n % c == 0:
            return c
    return n


MM_VMEM_BUDGET = 40 * 1024 * 1024
MM_PEAK_FLOPS = 0.9e15
MM_HBM_BYTES_PER_S = 3.0e12
MM_STEP_SECONDS = 0.35e-6


def _mm_tiles(m, n, k, size_a, size_b, size_out, size_res, single_k):
    best = None
    for tk in sorted({c for c in (k, 2048, 1024, 512, 256, 128) if c <= 2048 and k % c == 0}, reverse=True):
        for tm in sorted({c for c in (m, 1024, 512, 256, 128) if c <= 1024 and m % c == 0}, reverse=True):
            for tn in sorted({c for c in (n, 2048, 1536, 1024, 768, 512, 384, 256, 128) if c <= 2048 and n % c == 0},
                             reverse=True):
                nk = k // tk
                vmem = 2 * (tm * tk * size_a + tk * tn * size_b + tm * tn * (size_out + size_res))
                vmem += tm * tn * 4 * (2 if nk > 1 or not single_k else 1)
                vmem += (tm * tk * 2 if size_a > 2 else 0) + (tk * tn * 2 if size_b > 2 else 0)
                if vmem > MM_VMEM_BUDGET:
                    continue
                steps = (m // tm) * (n // tn) * nk
                a_reads = 1 if (nk == 1 and single_k) else n // tn
                traffic = m * k * size_a * a_reads + k * n * size_b * (m // tm) + m * n * (size_out + size_res)
                cost = max(2.0 * m * n * k / MM_PEAK_FLOPS, traffic / MM_HBM_BYTES_PER_S) + steps * MM_STEP_SECONDS
                if best is None or cost < best[0]:
                    best = (cost, tm, tn, tk)
    return best[1:]


def mm(name, a, b, mode, out_dtype=F32, res=None, b_slabs=None, out_slabs=None, dep=None, epi=None, extras=(),
       out_dtypes=None):
    if mode == "tn":
        k_dim, m_dim = a.shape
    else:
        m_dim, k_dim = a.shape
    if b_slabs:
        n_dim = b.shape[0] * b.shape[2] if mode == "nn" else b.shape[1]
    else:
        n_dim = b.shape[0] if mode == "nt" else b.shape[1]
    n_slabs = out_slabs or (b_slabs if (b_slabs and mode == "nn") else 1)
    k_slabs = b_slabs if (b_slabs and mode == "nt") else 1
    if epi is None:
        out_dtypes = [out_dtype]
        if res is None:
            epi = lambda acc: (acc,)
        else:
            extras, epi = [res], lambda acc, r: (acc + r,)
    tm, tn, tk = _mm_tiles(m_dim, n_dim // n_slabs, k_dim // k_slabs, a.dtype.itemsize, b.dtype.itemsize,
                           sum(jnp.dtype(dt).itemsize for dt in out_dtypes), sum(e.dtype.itemsize for e in extras),
                           single_k=(k_slabs == 1))
    nji = n_dim // n_slabs // tn
    nki = k_dim // k_slabs // tk
    nblk = lambda js, j: js * nji + j
    kblk = lambda ks, k: ks * nki + k
    if mode == "tn":
        a_spec = pl.BlockSpec((tk, tm), lambda i, js, j, ks, k: (kblk(ks, k), i))
    else:
        a_spec = pl.BlockSpec((tm, tk), lambda i, js, j, ks, k: (i, kblk(ks, k)))
    if b_slabs and mode == "nn":
        b_spec = pl.BlockSpec((None, tk, tn), lambda i, js, j, ks, k: (js, k, j))
    elif b_slabs and mode == "nt":
        b_spec = pl.BlockSpec((None, tn, tk), lambda i, js, j, ks, k: (ks, nblk(js, j), k))
    elif mode == "nt":
        b_spec = pl.BlockSpec((tn, tk), lambda i, js, j, ks, k: (nblk(js, j), kblk(ks, k)))
    else:
        b_spec = pl.BlockSpec((tk, tn), lambda i, js, j, ks, k: (kblk(ks, k), nblk(js, j)))
    specs, args = [a_spec, b_spec], [a, b]
    for e in extras:
        specs.append(pl.BlockSpec((tm, tn), lambda i, js, j, ks, k: (i, nblk(js, j))))
        args.append(e)
    if dep is not None:
        specs.append(pl.BlockSpec(memory_space=pl.ANY))
        args.append(dep)
    if out_slabs:
        o_specs = [pl.BlockSpec((None, tm, tn), lambda i, js, j, ks, k: (js, i, j))]
        o_shapes = [jax.ShapeDtypeStruct((out_slabs, m_dim, n_dim // out_slabs), out_dtypes[0])]
    else:
        o_specs = [pl.BlockSpec((tm, tn), lambda i, js, j, ks, k: (i, nblk(js, j))) for _ in out_dtypes]
        o_shapes = [jax.ShapeDtypeStruct((m_dim, n_dim), dt) for dt in out_dtypes]

    one_k_step = k_slabs * nki == 1
    n_in, n_out = len(args), len(out_dtypes)

    def body(*refs):
        a_ref, b_ref = refs[0], refs[1]
        part = _dg(a_ref[...].astype(BF16), b_ref[...].astype(BF16), mode)

        def finish(acc):
            outs = epi(acc, *[refs[2 + j][...].astype(F32) for j in range(len(extras))])
            for o_ref, o in zip(refs[n_in:n_in + n_out], outs):
                o_ref[...] = o.astype(o_ref.dtype)

        if one_k_step:
            finish(part)
            return
        acc_ref = refs[n_in + n_out]
        ks, kk = pl.program_id(3), pl.program_id(4)

        @pl.when((ks == 0) & (kk == 0))
        def _():
            acc_ref[...] = part

        @pl.when((ks > 0) | (kk > 0))
        def _():
            acc_ref[...] += part

        pl.when((ks == k_slabs - 1) & (kk == nki - 1))(lambda: finish(acc_ref[...]))

    grid = (m_dim // tm, n_slabs, nji, k_slabs, nki)
    scratch = [] if one_k_step else [pltpu.VMEM((tm, tn), F32)]
    out = pl.pallas_call(
        body, name=name, grid=grid, in_specs=specs, out_specs=o_specs, out_shape=o_shapes, scratch_shapes=scratch,
        compiler_params=pltpu.CompilerParams(
            dimension_semantics=("parallel", "parallel", "parallel", "arbitrary", "arbitrary"),
            vmem_limit_bytes=VMEM_LIMIT),
    )(*args)
    return out[0] if n_out == 1 else out


def _stack_lanes(x, n):
    w = x.shape[1] // n
    return jnp.stack([x[:, i * w:(i + 1) * w] for i in range(n)])


def _stack_rows(x, n):
    w = x.shape[0] // n
    return jnp.stack([x[i * w:(i + 1) * w, :] for i in range(n)])


def rwkv_scan_fwd(r, lw, k, v, kap, b):
    t = r.shape[0]
    c, hps, hd = min(RWKV_CHUNK, t), RWKV_HEADS_PER_STEP, RWKV_HEAD_DIM
    nc, ng, wl = t // c, RWKV_HEADS // hps, hps * hd
    spec = pl.BlockSpec((c, wl), lambda g, ci: (ci, g))

    def body(r_ref, lw_ref, k_ref, v_ref, kap_ref, b_ref, y_ref, ck_ref, inv_ref, st_ref):
        @pl.when(pl.program_id(1) == 0)
        def _():
            st_ref[...] = jnp.zeros_like(st_ref)

        st = st_ref[...]
        ck_ref[...] = st
        ins = [x[...] for x in (r_ref, lw_ref, k_ref, v_ref, kap_ref, b_ref)]
        y, st1, inv = rwkv_chunk_fn(_stack_rows(st, hps), *[_stack_lanes(x, hps) for x in ins])
        y_ref[...] = jnp.concatenate([y[h] for h in range(hps)], axis=-1)
        st_ref[...] = jnp.concatenate([st1[h] for h in range(hps)], axis=0)
        inv_ref[...] = jnp.concatenate([inv[h] for h in range(hps)], axis=0)

    return pl.pallas_call(
        body, name="rwkv_scan_fwd", grid=(ng, nc), in_specs=[spec] * 6,
        out_specs=[spec, pl.BlockSpec((None, wl, hd), lambda g, ci: (ci, g, 0)),
                   pl.BlockSpec((None, hps * c, c), lambda g, ci: (ci, g, 0))],
        out_shape=[jax.ShapeDtypeStruct((t, RWKV_WIDTH), F32), jax.ShapeDtypeStruct((nc, RWKV_WIDTH, hd), F32),
                   jax.ShapeDtypeStruct((nc, RWKV_HEADS * c, c), F32)],
        scratch_shapes=[pltpu.VMEM((wl, hd), F32)], compiler_params=_params(),
    )(r, lw, k, v, kap, b)


def rwkv_scan_bwd(r, lw, k, v, kap, b, ck, inv_ck, dy):
    t = r.shape[0]
    c, hps, hd = min(RWKV_CHUNK, t), RWKV_HEADS_PER_STEP, RWKV_HEAD_DIM
    nc, ng, wl = t // c, RWKV_HEADS // hps, hps * hd
    spec = pl.BlockSpec((c, wl), lambda g, ci: (nc - 1 - ci, g))

    def body(r_ref, lw_ref, k_ref, v_ref, kap_ref, b_ref, ck_ref, inv_ref, dy_ref, *rest):
        out_refs, dst_ref = rest[:6], rest[6]

        @pl.when(pl.program_id(1) == 0)
        def _():
            dst_ref[...] = jnp.zeros_like(dst_ref)

        ins = [x[...] for x in (r_ref, lw_ref, k_ref, v_ref, kap_ref, b_ref)]
        dyv, ck, dst = dy_ref[...].astype(F32), ck_ref[...], dst_ref[...]
        chunk = lambda *a: rwkv_chunk_fn(*a, inv=_stack_rows(inv_ref[...], hps))[:2]
        _, vjp = jax.vjp(chunk, _stack_rows(ck, hps), *[_stack_lanes(x, hps) for x in ins])
        grads = vjp((_stack_lanes(dyv, hps), _stack_rows(dst, hps)))
        dst_ref[...] = jnp.concatenate([grads[0][h] for h in range(hps)], axis=0)
        for j in range(6):
            out_refs[j][...] = jnp.concatenate([grads[1 + j][h] for h in range(hps)], axis=-1).astype(BF16)

    return pl.pallas_call(
        body, name="rwkv_scan_bwd", grid=(ng, nc),
        in_specs=[spec] * 6 + [pl.BlockSpec((None, wl, hd), lambda g, ci: (nc - 1 - ci, g, 0)),
                               pl.BlockSpec((None, hps * c, c), lambda g, ci: (nc - 1 - ci, g, 0)), spec],
        out_specs=[spec] * 6, out_shape=[jax.ShapeDtypeStruct((t, RWKV_WIDTH), BF16)] * 6,
        scratch_shapes=[pltpu.VMEM((wl, hd), F32)], compiler_params=_params(),
    )(r, lw, k, v, kap, b, ck, inv_ck, dy)


def _ssd_specs(q, blk):
    gw = SSD_WIDTH // 2
    return [pl.BlockSpec((q, gw), lambda g, ci: (blk(ci), g)),
            pl.BlockSpec((q, SSD_STATE), lambda g, ci: (blk(ci), g)),
            pl.BlockSpec((q, SSD_STATE), lambda g, ci: (blk(ci), g)),
            pl.BlockSpec((q, LANES), lambda g, ci: (blk(ci), 0)),
            pl.BlockSpec((1, LANES), lambda g, ci: (0, 0)),
            pl.BlockSpec((1, LANES), lambda g, ci: (0, 0))]


def ssd_scan_fwd(xs, bm, cm, dt, a_log, d_skip):
    t = xs.shape[0]
    q = min(SSD_CHUNK, t)
    nc, gw = t // q, SSD_WIDTH // 2

    def body(xs_ref, bm_ref, cm_ref, dt_ref, al_ref, d_ref, y_ref, ck_ref, h_ref):
        @pl.when(pl.program_id(1) == 0)
        def _():
            h_ref[...] = jnp.zeros_like(h_ref)

        ck_ref[...] = h_ref[...]
        args = (h_ref[...], xs_ref[...], bm_ref[...], cm_ref[...], dt_ref[...], al_ref[...], d_ref[...])
        g = pl.program_id(0)

        @pl.when(g == 0)
        def _():
            y, h1 = ssd_chunk_fn(0, *args)
            y_ref[...] = y
            h_ref[...] = h1

        @pl.when(g == 1)
        def _():
            y, h1 = ssd_chunk_fn(1, *args)
            y_ref[...] = y
            h_ref[...] = h1

    return pl.pallas_call(
        body, name="ssd_scan_fwd", grid=(2, nc), in_specs=_ssd_specs(q, lambda ci: ci),
        out_specs=[pl.BlockSpec((q, gw), lambda g, ci: (ci, g)),
                   pl.BlockSpec((None, gw, SSD_STATE), lambda g, ci: (ci, g, 0))],
        out_shape=[jax.ShapeDtypeStruct((t, SSD_WIDTH), F32), jax.ShapeDtypeStruct((nc, SSD_WIDTH, SSD_STATE), F32)],
        scratch_shapes=[pltpu.VMEM((gw, SSD_STATE), F32)], compiler_params=_params(),
    )(xs, bm, cm, dt, a_log, d_skip)


def ssd_scan_bwd(xs, bm, cm, dt, a_log, d_skip, ck, dy):
    t = xs.shape[0]
    q = min(SSD_CHUNK, t)
    nc, gw = t // q, SSD_WIDTH // 2
    rev = lambda ci: nc - 1 - ci

    def body(xs_ref, bm_ref, cm_ref, dt_ref, al_ref, d_ref, ck_ref, dy_ref,
             dxs_ref, dbm_ref, dcm_ref, ddt_ref, dal_ref, dd_ref, dh_ref):
        g, ci = pl.program_id(0), pl.program_id(1)

        @pl.when(ci == 0)
        def _():
            dh_ref[...] = jnp.zeros_like(dh_ref)

        @pl.when((ci == 0) & (g == 0))
        def _():
            dal_ref[...] = jnp.zeros_like(dal_ref)
            dd_ref[...] = jnp.zeros_like(dd_ref)

        args = (ck_ref[...], xs_ref[...], bm_ref[...], cm_ref[...], dt_ref[...], al_ref[...], d_ref[...])

        def run(group):
            _, vjp = jax.vjp(functools.partial(ssd_chunk_fn, group), *args)
            dh0, dxs, dbm, dcm, ddt, dal, dd = vjp((dy_ref[...].astype(F32), dh_ref[...]))
            dh_ref[...] = dh0
            dxs_ref[...] = dxs.astype(BF16)
            dbm_ref[...] = dbm.astype(BF16)
            dcm_ref[...] = dcm.astype(BF16)
            ddt_ref[...] = ddt
            dal_ref[...] += dal
            dd_ref[...] += dd

        pl.when(g == 0)(lambda: run(0))
        pl.when(g == 1)(lambda: run(1))

    in_specs = _ssd_specs(q, rev) + [pl.BlockSpec((None, gw, SSD_STATE), lambda g, ci: (rev(ci), g, 0)),
                                     pl.BlockSpec((q, gw), lambda g, ci: (rev(ci), g))]
    return pl.pallas_call(
        body, name="ssd_scan_bwd", grid=(2, nc), in_specs=in_specs,
        out_specs=[pl.BlockSpec((q, gw), lambda g, ci: (rev(ci), g)),
                   pl.BlockSpec((q, SSD_STATE), lambda g, ci: (rev(ci), g)),
                   pl.BlockSpec((q, SSD_STATE), lambda g, ci: (rev(ci), g)),
                   pl.BlockSpec((None, q, LANES), lambda g, ci: (g, rev(ci), 0)),
                   pl.BlockSpec((1, LANES), lambda g, ci: (0, 0)),
                   pl.BlockSpec((1, LANES), lambda g, ci: (0, 0))],
        out_shape=[jax.ShapeDtypeStruct((t, SSD_WIDTH), BF16), jax.ShapeDtypeStruct((t, 2 * SSD_STATE), BF16),
                   jax.ShapeDtypeStruct((t, 2 * SSD_STATE), BF16), jax.ShapeDtypeStruct((2, t, LANES), F32),
                   jax.ShapeDtypeStruct((1, LANES), F32), jax.ShapeDtypeStruct((1, LANES), F32)],
        scratch_shapes=[pltpu.VMEM((gw, SSD_STATE), F32)], compiler_params=_params(),
    )(xs, bm, cm, dt, a_log, d_skip, ck, dy)


def loss_and_grad(x, tgt, g, tm):
    t, d = x.shape
    tm = min(tm, t)
    nb = t // tm

    def body(x_ref, t_ref, g_ref, loss_ref, dx_ref, dxb_ref, dg_ref):
        @pl.when(pl.program_id(0) == 0)
        def _():
            loss_ref[...] = jnp.zeros_like(loss_ref)
            dg_ref[...] = jnp.zeros_like(dg_ref)

        val, vjp = jax.vjp(loss_fn, x_ref[...], t_ref[...], g_ref[...])
        dx, _, dg = vjp(jnp.ones((1, 1), F32))
        loss_ref[...] += jnp.broadcast_to(val, loss_ref.shape)
        dx_ref[...] = dx
        dxb_ref[...] = dx.astype(BF16)
        dg_ref[...] += dg

    row = pl.BlockSpec((tm, d), lambda i: (i, 0))
    one = pl.BlockSpec((1, d), lambda i: (0, 0))
    return pl.pallas_call(
        body, name="loss_and_grad", grid=(nb,), in_specs=[row, row, one],
        out_specs=[pl.BlockSpec((SUBLANES, LANES), lambda i: (0, 0)), row, row, one],
        out_shape=[jax.ShapeDtypeStruct((SUBLANES, LANES), F32), jax.ShapeDtypeStruct((t, d), F32),
                   jax.ShapeDtypeStruct((t, d), BF16), jax.ShapeDtypeStruct((1, d), F32)],
        compiler_params=_params(),
    )(x, tgt, g)


def adamw(name, recv, w, m, v, dep=None):
    rows, cols = w.shape
    n_slabs = recv.shape[0]
    recv_block_bytes = 4 * 1024 * 1024
    tm = _pick(rows, [c for c in (256, 128, 64, 32, 16, 8) if n_slabs * c * cols * 4 <= recv_block_bytes])
    c1 = 1.0 / (1.0 - ADAM_B1 ** ADAM_STEP)
    c2 = 1.0 / (1.0 - ADAM_B2 ** ADAM_STEP)

    n_dep = 0 if dep is None else 1

    def body(recv_ref, w_ref, m_ref, v_ref, *rest):
        g_ref, d_ref, nm_ref, nv_ref = rest[n_dep:]
        g = recv_ref[0].astype(F32)
        for p in range(1, n_slabs):
            g = g + recv_ref[p].astype(F32)
        nm =ADAM_B1 * m_ref[...] + (1.0 - ADAM_B1) * g
        nv = ADAM_B2 * v_ref[...] + (1.0 - ADAM_B2) * jnp.square(g)
        g_ref[...] = g
        nm_ref[...] = nm
        nv_ref[...] = nv
        d_ref[...] = -ADAM_LR * ((nm * c1) / (jnp.sqrt(nv * c2) + ADAM_EPS) + ADAM_WD * w_ref[...])

    blk = pl.BlockSpec((tm, cols), lambda i: (i, 0))
    return pl.pallas_call(
        body, name=name, grid=(rows // tm,),
        in_specs=[pl.BlockSpec((n_slabs, tm, cols), lambda i: (0, i, 0)), blk, blk, blk]
        + [pl.BlockSpec(memory_space=pl.ANY)] * n_dep,
        out_specs=[blk] * 4, out_shape=[jax.ShapeDtypeStruct((rows, cols), F32)] * 4,
        compiler_params=_params(),
    )(recv, w, m, v, *([] if dep is None else [dep]))


def _mesh_pos():
    return lax.axis_index("x"), lax.axis_index("y"), lax.axis_index("c")


def _peer(pos, mask):
    x, y, c = pos
    return (1 - x if mask & 4 else x, 1 - y if mask & 2 else y, 1 - c if mask & 1 else c)


def _linear(pos):
    return 4 * pos[0] + 2 * pos[1] + pos[2]


class Exchange:
    MASKS = {"gather": (1, 2, 3, 4, 5, 6, 7), "scatter": (1, 2, 3, 4, 5, 6, 7), "gather_chips": (1, 2, 4, 6),
             "forward": (2, 4, 6)}

    def __init__(self, xs, kind, lands=None):
        self.kind, self.masks = kind, self.MASKS[kind]
        self.xs = [] if kind == "forward" else list(xs)
        if kind == "forward":
            self.land_shape = [jax.ShapeDtypeStruct(l.shape, l.dtype) for l in lands]
        elif kind == "scatter":
            self.land_shape = [jax.ShapeDtypeStruct(x.shape, x.dtype) for x in xs]
        else:
            self.land_shape = [jax.ShapeDtypeStruct((N_DEV,) + x.shape, x.dtype) for x in xs]
        self.n = len(self.land_shape)
        copies = self.n * len(self.masks)
        self.sems = [pltpu.SemaphoreType.DMA((copies,)), pltpu.SemaphoreType.DMA((copies,)),
                     pltpu.SemaphoreType.DMA((self.n,))]

    def _copies(self, ins, outs, sems, landing):
        send_sems, recv_sems, local_sems = sems
        me = _mesh_pos()
        me_lin = _linear(me)
        local, remote = [], []
        for ti in range(self.n):
            if self.kind != "forward":
                src_mine = ins[ti].at[me_lin] if self.kind == "scatter" else ins[ti]
                local.append(pltpu.make_async_copy(src_mine, outs[ti].at[me_lin], local_sems.at[ti]))
            for j, mask in enumerate(self.masks):
                if self.kind == "forward":
                    peer = _peer(me, 1)
                    src = outs[ti].at[_linear(_peer(me, mask))]
                    dst = outs[ti].at[_linear(_peer(me, mask ^ 1 if landing else mask))]
                else:
                    peer = _peer(me, mask)
                    src = ins[ti].at[_linear(peer)] if self.kind == "scatter" else ins[ti]
                    dst = outs[ti].at[_linear(peer) if landing else me_lin]
                sem_index = ti * len(self.masks) + j
                remote.append(pltpu.make_async_remote_copy(
                    src_ref=src, dst_ref=dst, send_sem=send_sems.at[sem_index], recv_sem=recv_sems.at[sem_index],
                    device_id=peer, device_id_type=pl.DeviceIdType.MESH))
        return local, remote

    def start(self, ins, outs, sems):
        local, remote = self._copies(ins, outs, sems, landing=False)
        for cp in local + remote:
            cp.start()

    def finish(self, ins, outs, sems):
        local, remote = self._copies(ins, outs, sems, landing=True)
        for cp in remote:
            cp.wait_recv()
        for cp in remote:
            cp.wait_send()
        for cp in local:
            cp.wait()


def exchange_start(name, xs, kind, dep=None, lands=None):
    ex = Exchange(xs, kind, lands)
    hbm = pl.BlockSpec(memory_space=pltpu.HBM)
    sem = pl.BlockSpec(memory_space=pltpu.SEMAPHORE)
    if lands is None:
        lands = [lax.empty(s.shape, s.dtype) for s in ex.land_shape]
    n_src, n = len(ex.xs), ex.n
    n_inputs = n_src + n + (0 if dep is None else 1)

    def body(*refs):
        ins, lnd, sems, token = refs[:n_src], refs[n_src:n_src + n], refs[n_inputs:n_inputs + 3], refs[-1]
        ex.start(ins, lnd, sems)
        token[...] = jnp.zeros_like(token)

    res = pl.pallas_call(
        body, name=name, in_specs=[hbm] * (n_src + n) + ([] if dep is None else [pl.BlockSpec(memory_space=pl.ANY)]),
        out_specs=[sem] * 3 + [hbm] * (n_src + n) + [pl.BlockSpec(memory_space=pltpu.VMEM)],
        out_shape=ex.sems + [pltpu.HBM(x.shape, x.dtype) for x in ex.xs]
        + [pltpu.HBM(s.shape, s.dtype) for s in ex.land_shape] + [jax.ShapeDtypeStruct((SUBLANES, LANES), F32)],
        input_output_aliases={i: 3 + i for i in range(n_src + n)},
        compiler_params=pltpu.CompilerParams(has_side_effects=pltpu.SideEffectType.DATAFLOW_SIDE_EFFECTING),
    )(*[pltpu.with_memory_space_constraint(x, pltpu.HBM) for x in ex.xs + list(lands)],
      *([] if dep is None else [dep]))
    return (ex, res[:3], res[3:3 + n_src], res[3 + n_src:3 + n_src + n]), res[-1]


def exchange_wait(name, handles, after):
    ex, sems, srcs, lands = handles
    n_src, n = len(srcs), len(lands)
    hbm = pl.BlockSpec(memory_space=pltpu.HBM)
    sem = pl.BlockSpec(memory_space=pltpu.SEMAPHORE)

    def body(*refs):
        ins, lnd, sem_refs = refs[:n_src], refs[n_src:n_src + n], refs[n_src + n:n_src + n + 3]
        ex.finish(ins, lnd, sem_refs)

    res = pl.pallas_call(
        body, name=name, in_specs=[hbm] * (n_src + n) + [sem] * 3 + [pl.BlockSpec(memory_space=pl.ANY)],
        out_specs=[hbm] * (n_src + n),
        out_shape=[pltpu.HBM(x.shape, x.dtype) for x in srcs] + [pltpu.HBM(x.shape, x.dtype) for x in lands],
        input_output_aliases={i: i for i in range(n_src + n)},
        compiler_params=pltpu.CompilerParams(has_side_effects=pltpu.SideEffectType.DATAFLOW_SIDE_EFFECTING),
    )(*srcs, *lands, *sems, after)
    return res[n_src:]


def forward_start(name, chip_gather, after):
    lands = exchange_wait(name + "_wait", chip_gather, after)
    return exchange_start(name + "_forward_start", [], "forward", lands=lands)


_Z = (0, 1024)
_XBC = (1024, 2560)
_DT = (2560, 2576)
_RKV = (2576, 5648)
_PW = (5648, 5744)
_PA = (5744, 5840)
_PG = (5840, 6096)
D_IN = 6096

_SMALL = ("norm_mix_g", "ssd_conv_b", "ssd_dt_bias", "ssd_a_log", "ssd_d", "ssd_norm_g", "rwkv_mu", "rwkv_w0",
          "rwkv_a0", "rwkv_k_k", "rwkv_k_a", "rwkv_r_k", "rwkv_ln_w", "rwkv_ln_b", "norm_x_g", "norm_mem_g",
          "norm_ffn_g", "final_norm_g")
_WEIGHTS = ("norm_mix_g", "w_in", "ssd_conv_w", "ssd_conv_b", "ssd_dt_bias", "ssd_a_log", "ssd_d", "ssd_norm_g",
            "rwkv_mu", "rwkv_w0", "rwkv_w2", "rwkv_a0", "rwkv_a2", "rwkv_g2", "rwkv_k_k", "rwkv_k_a", "rwkv_r_k",
            "rwkv_ln_w", "rwkv_ln_b", "w_out", "norm_x_g", "norm_mem_g", "xattn_wq", "xattn_wk", "xattn_wv",
            "xattn_wo", "norm_ffn_g", "ffn_w1", "ffn_w2", "final_norm_g")


def _pad_lanes(x, width=LANES):
    return jnp.pad(x, ((0, 0), (0, width - x.shape[1])))


def _pack_small(vals):
    flat = jnp.concatenate([vals[n].reshape(-1) for n in _SMALL])
    rows = -(-flat.shape[0] // (LANES * SUBLANES)) * SUBLANES
    return jnp.pad(flat, (0, rows * LANES - flat.shape[0])).reshape(rows, LANES)


def _unpack_small(packed, shapes):
    flat = packed.reshape(-1)
    out, pos = {}, 0
    for n in _SMALL:
        size = 1
        for s in shapes[n]:
            size *= s
        out[n] = flat[pos:pos + size].reshape(shapes[n])
        pos += size
    return out


def _rows(w, rng):
    return w[rng[0]:rng[1]]


def sum_slabs(name, recv):
    n, rows, cols = recv.shape
    tc = _pick(cols, (256, 128))

    def body(r_ref, o_ref):
        acc = r_ref[0].astype(F32)
        for p in range(1, n):
            acc = acc + r_ref[p].astype(F32)
        o_ref[...] = acc

    return pl.pallas_call(
        body, name=name, grid=(cols // tc,), in_specs=[pl.BlockSpec((n, rows, tc), lambda j: (0, 0, j))],
        out_specs=pl.BlockSpec((rows, tc), lambda j: (0, j)), out_shape=jax.ShapeDtypeStruct((rows, cols), F32),
        compiler_params=_params(),
    )(recv)


def kernel(x, mem, norm_mix_g, w_in, ssd_conv_w, ssd_conv_b, ssd_dt_bias, ssd_a_log, ssd_d, ssd_norm_g, rwkv_mu, rwkv_w0, rwkv_w2, rwkv_a0, rwkv_a2, rwkv_g2, rwkv_k_k, rwkv_k_a, rwkv_r_k, rwkv_ln_w, rwkv_ln_b, w_out, norm_x_g, norm_mem_g, xattn_wq, xattn_wk, xattn_wv, xattn_wo, norm_ffn_g, ffn_w1, ffn_w2, final_norm_g, loss_target, m_norm_mix_g, m_w_in, m_ssd_conv_w, m_ssd_conv_b, m_ssd_dt_bias, m_ssd_a_log, m_ssd_d, m_ssd_norm_g, m_rwkv_mu, m_rwkv_w0, m_rwkv_w2, m_rwkv_a0, m_rwkv_a2, m_rwkv_g2, m_rwkv_k_k, m_rwkv_k_a, m_rwkv_r_k, m_rwkv_ln_w, m_rwkv_ln_b, m_w_out, m_norm_x_g, m_norm_mem_g, m_xattn_wq, m_xattn_wk, m_xattn_wv, m_xattn_wo, m_norm_ffn_g, m_ffn_w1, m_ffn_w2, m_final_norm_g, v_norm_mix_g, v_w_in, v_ssd_conv_w, v_ssd_conv_b, v_ssd_dt_bias, v_ssd_a_log, v_ssd_d, v_ssd_norm_g, v_rwkv_mu, v_rwkv_w0, v_rwkv_w2, v_rwkv_a0, v_rwkv_a2, v_rwkv_g2, v_rwkv_k_k, v_rwkv_k_a, v_rwkv_r_k, v_rwkv_ln_w, v_rwkv_ln_b, v_w_out, v_norm_x_g, v_norm_mem_g, v_xattn_wq, v_xattn_wk, v_xattn_wv, v_xattn_wo, v_norm_ffn_g, v_ffn_w1, v_ffn_w2, v_final_norm_g):
    given = dict(locals())
    wts = {n: given[n] for n in _WEIGHTS}
    mom_m = {n: given["m_" + n] for n in _WEIGHTS}
    mom_v = {n: given["v_" + n] for n in _WEIGHTS}
    d = D_MODEL
    xt, memt, tgt = x[0], mem[0], loss_target[0]
    tm = 256
    tm_light = 512
    tm_rwkv = 128

    big = {"w_in": jnp.transpose(w_in[0]), "w_out": w_out[0], "xattn_wq": xattn_wq[0], "xattn_wk": xattn_wk[0],
           "xattn_wv": xattn_wv[0], "xattn_wo": xattn_wo[0], "ffn_w1": ffn_w1[0], "ffn_w2": ffn_w2[0]}
    small_sh = {"ssd_conv_w": ssd_conv_w.reshape(4, -1), "rwkv_w2": rwkv_w2[0], "rwkv_a2": rwkv_a2[0],
                "rwkv_g2": rwkv_g2[0]}
    cast_one = lambda n, deps=(): rowwise_fwd("cast_" + n, cast_fn, [big[n]], [], [(big[n].shape[1], BF16)],
                                              256 if big[n].shape[0] % 256 == 0 else big[n].shape[0], deps=deps)[0]
    gather_in, token_in = exchange_start("gather_in_start", [cast_one("w_in")] + list(small_sh.values()), "gather_chips")
    cast = {n: cast_one(n, deps=[token_in]) for n in big if n != "w_in"}
    late_a = ("w_out", "xattn_wq", "xattn_wk", "xattn_wv", "xattn_wo")
    late_b = ("ffn_w1", "ffn_w2")
    gather_a, token_a = exchange_start("gather_attn_start", [cast[n] for n in late_a], "gather_chips", dep=token_in)
    gather_b, token_b = exchange_start("gather_ffn_start", [cast[n] for n in late_b], "gather_chips", dep=token_a)
    (h1,) = rowwise_fwd("norm_mix", rmsnorm_fn, [xt], [norm_mix_g], [(d, BF16)], tm_light, deps=[token_b])
    forward_in, token_in = forward_start("gather_in", gather_in, after=h1)
    gathered = exchange_wait("gather_in_forward_wait", forward_in, after=token_in)
    g_big = {"w_in": gathered[0]}
    g_small = dict(zip(small_sh, gathered[1:]))

    pad_rows = lambda a: jnp.pad(a, ((0, LANES - a.shape[0]), (0, 0)))
    w_in_t = g_big["w_in"].reshape(D_IN, d)
    wt_z, wt_xbc, wt_rkv = (_rows(w_in_t, r) for r in (_Z, _XBC, _RKV))
    wt_ps = jnp.concatenate([_rows(w_in_t, _PG)] + [pad_rows(_rows(w_in_t, r)) for r in (_PW, _PA, _DT)], axis=0)
    unshard_cols = lambda g: jnp.transpose(g, (1, 0, 2)).reshape(g.shape[1], -1)
    conv_w_f = unshard_cols(g_small["ssd_conv_w"])
    w2p, a2p = pad_rows(unshard_cols(g_small["rwkv_w2"])), pad_rows(unshard_cols(g_small["rwkv_a2"]))
    g2_f = unshard_cols(g_small["rwkv_g2"])

    mu = rwkv_mu
    mu_rkv, mu_pg = mu[:, :3072], mu[:, 3264:3520]
    mu_pwa = jnp.concatenate([_pad_lanes(mu[:, 3072:3168]), _pad_lanes(mu[:, 3168:3264])], axis=1)
    dt_bias_p, a_log_p, d_p = _pad_lanes(ssd_dt_bias), _pad_lanes(ssd_a_log), _pad_lanes(ssd_d)
    r_k_row = rwkv_r_k.reshape(1, RWKV_WIDTH)
    g_final = final_norm_g.reshape(1, d)

    u_z = mm("in_z", h1, wt_z, "nt")
    u_xbc = mm("in_xbc", h1, wt_xbc, "nt")
    u_rkv = mm("in_rkv", h1, wt_rkv, "nt")
    u_ps = mm("in_narrow", h1, wt_ps, "nt")

    ssd_pre_rows = lambda: [Rows(u_xbc, shifts=(1, 2, 3)), Rows(u_ps, LANES, 4)]
    ssd_pre_params = [conv_w_f, ssd_conv_b, dt_bias_p]
    xs, bm, cm, dt = rowwise_fwd("ssd_pre", ssd_pre_fn, ssd_pre_rows(), ssd_pre_params,
                                 [(SSD_WIDTH, F32), (256, F32), (256, F32), (LANES, F32)], tm)
    y_scan, ssd_ck = ssd_scan_fwd(xs, bm, cm, dt, a_log_p, d_p)
    (y_ssd,) = rowwise_fwd("ssd_post", ssd_post_fn, [y_scan, u_z], [ssd_norm_g], [(SSD_WIDTH, BF16)], tm_light,
                           into=(None, d, 0))

    rwkv_pre_rows = lambda: [Rows(u_rkv, shifts=(1,)), Rows(u_ps, 2 * LANES, 0, shifts=(1,)), Rows(u_ps, 2 * LANES, 1, shifts=(1,))]
    rwkv_pre_params = [mu_rkv, mu_pg, mu_pwa, rwkv_w0, w2p, rwkv_a0, a2p, g2_f, rwkv_k_k, rwkv_k_a]
    forward_a, token_a = forward_start("gather_attn", gather_a, after=y_scan)
    r_, lw_, k_, v_, kap_, b_, gate_ = rowwise_fwd("rwkv_pre", rwkv_pre_fn, rwkv_pre_rows(), rwkv_pre_params,
                                                   [(RWKV_WIDTH, F32)] * 7, tm_rwkv, deps=[token_a])
    ys_r, rwkv_ck, rwkv_inv = rwkv_scan_fwd(r_, lw_, k_, v_, kap_, b_)
    forward_b, token_b = forward_start("gather_ffn", gather_b, after=ys_r)
    g_big.update(zip(late_a, exchange_wait("gather_attn_forward_wait", forward_a, after=token_b)))
    w_out_f = g_big["w_out"].reshape(d, d)
    wq_f, wk_f, wv_f, wo_f = (g_big[n].reshape(d, d) for n in ("xattn_wq", "xattn_wk", "xattn_wv", "xattn_wo"))
    rwkv_post_params = [rwkv_ln_w, rwkv_ln_b, r_k_row]
    (ycat,) = rowwise_fwd("rwkv_post", rwkv_post_fn, [ys_r, r_, k_, v_, gate_], rwkv_post_params,
                          [(RWKV_WIDTH, BF16)], tm, into=(y_ssd, d, 1))
    x1 = mm("out_proj", ycat, w_out_f, "nn", res=xt)

    (h2,) = rowwise_fwd("norm_x", rmsnorm_fn, [x1], [norm_x_g], [(d, BF16)], tm_light)
    (mn,) = rowwise_fwd("norm_mem", rmsnorm_fn, [memt], [norm_mem_g], [(d, BF16)], tm)
    q = mm("xattn_q", h2, wq_f, "nn", out_dtype=BF16)
    kx = mm("xattn_k", mn, wk_f, "nn")
    vx = mm("xattn_v", mn, wv_f, "nn")
    (o,) = rowwise_fwd("xattn", attn_fn, [q], [kx, vx], [(d, BF16)], tm)
    x2 = mm("xattn_o", o, wo_f, "nn", res=x1)

    (h3,) = rowwise_fwd("norm_ffn", rmsnorm_fn, [x2], [norm_ffn_g], [(d, BF16)], tm_light)
    w1_s, w2_g = exchange_wait("gather_ffn_forward_wait", forward_b, after=h3)
    w2_f = w2_g.reshape(D_FF, d)
    relu2_epi = lambda acc: (jnp.square(jnp.maximum(acc, 0.0)), jnp.maximum(acc, 0.0))
    hid, relu_a = mm("ffn_1", h3, w1_s, "nn", b_slabs=N_DEV, epi=relu2_epi, out_dtypes=[BF16, BF16])
    x3 = mm("ffn_2", hid, w2_f, "nn", res=x2)

    loss_blk, dx3, dx3_b, dg_final = loss_and_grad(x3, tgt, g_final, tm_light)

    grads = {}
    grads["ffn_w2"] = mm("d_ffn_w2", hid, dx3_b, "tn", out_dtype=BF16).reshape(N_DEV, D_FF // N_DEV, d)
    sc_w2, tok = exchange_start("scatter_ffn_w2_start", [grads["ffn_w2"]], "scatter")
    da = mm("d_hid", dx3_b, w2_f, "nt", dep=tok, epi=lambda acc, ra: (2.0 * acc * ra,), extras=[relu_a],
            out_dtypes=[BF16])
    grads["ffn_w1"] = mm("d_ffn_w1", h3, da, "tn", out_dtype=BF16, out_slabs=N_DEV)
    sc_w1, tok = exchange_start("scatter_ffn_w1_start", [grads["ffn_w1"]], "scatter")
    dh3 = mm("d_h3", da, w1_s, "nt", out_dtype=BF16, b_slabs=N_DEV, dep=tok)
    (dx2,), (dg_ffn,) = rowwise_bwd("norm_ffn_bwd", rmsnorm_fn, [x2], [norm_ffn_g], [[dh3]], tm_light, [F32], row_add=[dx3])

    grads["xattn_wo"] = mm("d_wo", o, dx2, "tn", out_dtype=BF16).reshape(N_DEV, d // N_DEV, d)
    sc_wo, tok = exchange_start("scatter_wo_start", [grads["xattn_wo"]], "scatter")
    d_o = mm("d_o", dx2, wo_f, "nt", out_dtype=BF16, dep=tok)
    (dq,), (dkx, dvx) = rowwise_bwd("xattn_bwd", attn_fn, [q], [kx, vx], [[d_o]], tm, [BF16])
    grads["xattn_wq"] = mm("d_wq", h2, dq, "tn", out_dtype=BF16).reshape(N_DEV, d // N_DEV, d)
    grads["xattn_wk"] = mm("d_wk", mn, dkx, "tn", out_dtype=BF16).reshape(N_DEV, d // N_DEV, d)
    grads["xattn_wv"] = mm("d_wv", mn, dvx, "tn", out_dtype=BF16).reshape(N_DEV, d // N_DEV, d)
    qkv = ("xattn_wq", "xattn_wk", "xattn_wv")
    sc_qkv, tok = exchange_start("scatter_qkv_start", [grads[n] for n in qkv], "scatter")
    dmn = mm("d_mn_v", dvx, wv_f, "nt", res=mm("d_mn_k", dkx, wk_f, "nt", dep=tok))
    _, (dg_mem,) = rowwise_bwd("norm_mem_bwd", rmsnorm_fn, [memt], [norm_mem_g], [[dmn]], tm, [None])
    dh2 = mm("d_h2", dq, wq_f, "nt", out_dtype=BF16, dep=dg_mem)
    (dx1,), (dg_x,) = rowwise_bwd("norm_x_bwd", rmsnorm_fn, [x1], [norm_x_g], [[dh2]], tm_light, [F32], row_add=[dx2])

    grads["w_out"] = mm("d_w_out", ycat, dx1, "tn", out_dtype=BF16).reshape(N_DEV, d // N_DEV, d)
    sc_wout, tok = exchange_start("scatter_w_out_start", [grads["w_out"]], "scatter")
    d_ycat = mm("d_ycat", dx1, w_out_f, "nt", out_dtype=BF16, dep=tok)

    (d_ys, d_r1, d_k1, d_v1, d_gate), (dln_w, dln_b, dr_k) = rowwise_bwd(
        "rwkv_post_bwd", rwkv_post_fn, [ys_r, r_, k_, v_, gate_], rwkv_post_params,
        [[Rows(d_ycat, RWKV_WIDTH, 1)]], tm, [BF16] * 5)
    d_r2, d_lw, d_k2, d_v2, d_kap, d_b = rwkv_scan_bwd(r_, lw_, k_, v_, kap_, b_, rwkv_ck, rwkv_inv, d_ys)
    (du_rkv, du_pg, du_pwa), rwkv_pg = rowwise_bwd(
        "rwkv_pre_bwd", rwkv_pre_fn, rwkv_pre_rows(), rwkv_pre_params,
        [[d_r1, d_r2], [d_lw], [d_k1, d_k2], [d_v1, d_v2], [d_kap], [d_b], [d_gate]], tm_rwkv, [BF16] * 3)
    dmu_rkv, dmu_pg, dmu_pwa, dw0, dw2p, da0, da2p, dg2, dk_k, dk_a = rwkv_pg

    (d_yscan, du_z), (dssd_norm_g,) = rowwise_bwd("ssd_post_bwd", ssd_post_fn, [y_scan, u_z], [ssd_norm_g],
                                                  [[Rows(d_ycat, SSD_WIDTH, 0)]], tm_light, [BF16, BF16])
    dxs, dbm, dcm, ddt2, da_log_p, dd_p = ssd_scan_bwd(xs, bm, cm, dt, a_log_p, d_p, ssd_ck, d_yscan)
    (du_xbc, du_dt), (dconv_w, dconv_b, ddt_bias_p) = rowwise_bwd(
        "ssd_pre_bwd", ssd_pre_fn, ssd_pre_rows(), ssd_pre_params,
        [[dxs], [dbm], [dcm], [ddt2[0], ddt2[1]]], tm, [BF16, BF16])
    du_ps = jnp.concatenate([du_pg, du_pwa, du_dt], axis=1)

    dwt_z = mm("d_w_z", du_z, h1, "tn", out_dtype=BF16)
    dwt_xbc = mm("d_w_xbc", du_xbc, h1, "tn", out_dtype=BF16)
    dwt_rkv = mm("d_w_rkv", du_rkv, h1, "tn", out_dtype=BF16)
    dwt_ps = mm("d_w_narrow", du_ps, h1, "tn", out_dtype=BF16)
    dwt_full = jnp.concatenate([dwt_z, dwt_xbc, dwt_ps[512:528], dwt_rkv, dwt_ps[256:352], dwt_ps[384:480], dwt_ps[0:256]],
                               axis=0)
    to_slabs = lambda g: jnp.transpose(g.reshape(g.shape[0], N_DEV, -1), (1, 0, 2))
    grads["w_in"] = dwt_full.reshape(N_DEV, D_IN // N_DEV, d)
    grads["ssd_conv_w"] = to_slabs(dconv_w)
    grads["rwkv_w2"] = to_slabs(dw2p[:96])
    grads["rwkv_a2"] = to_slabs(da2p[:96])
    grads["rwkv_g2"] = to_slabs(dg2)
    tail = ("w_in", "ssd_conv_w", "rwkv_w2", "rwkv_a2", "rwkv_g2")
    sc_tail, tok = exchange_start("scatter_tail_start", [grads[n] for n in tail], "scatter")
    dh1 = mm("d_h1_z", du_z, wt_z, "nn", dep=tok)
    dh1 = mm("d_h1_xbc", du_xbc, wt_xbc, "nn", res=dh1)
    dh1 = mm("d_h1_rkv", du_rkv, wt_rkv, "nn", res=dh1)
    dh1 = mm("d_h1_narrow", du_ps, wt_ps, "nn", res=dh1)
    (dx,), (dg_mix,) = rowwise_bwd("norm_mix_bwd", rmsnorm_fn, [xt], [norm_mix_g], [[dh1]], tm_light, [F32], row_add=[dx1])

    dmu =jnp.concatenate([dmu_rkv, dmu_pwa[:, 0:96], dmu_pwa[:, 128:224], dmu_pg], axis=1)
    small_grads = {
        "norm_mix_g": dg_mix, "ssd_conv_b": dconv_b, "ssd_dt_bias": ddt_bias_p[:, :16], "ssd_a_log": da_log_p[:, :16],
        "ssd_d": dd_p[:, :16], "ssd_norm_g": dssd_norm_g, "rwkv_mu": dmu, "rwkv_w0": dw0, "rwkv_a0": da0,
        "rwkv_k_k": dk_k, "rwkv_k_a": dk_a, "rwkv_r_k": dr_k, "rwkv_ln_w": dln_w, "rwkv_ln_b": dln_b,
        "norm_x_g": dg_x, "norm_mem_g": dg_mem, "norm_ffn_g": dg_ffn, "final_norm_g": dg_final}

    gather_small, tok = exchange_start("gather_small_start", [_pack_small(small_grads)], "gather")
    received = {}
    for names, handle in ((("ffn_w2",), sc_w2), (("ffn_w1",), sc_w1), (("xattn_wo",), sc_wo), (qkv, sc_qkv),
                          (("w_out",), sc_wout)):
        received.update(zip(names, exchange_wait("scatter_" + names[0] + "_wait", handle, after=tok)))

    out_g, out_d, out_m, out_v = {}, {}, {}, {}

    def run_adamw(n, dep):
        shape = wts[n].shape
        two_d = lambda a: a.reshape(-1, shape[-1])
        if n == "w_in":
            recv = jnp.transpose(sum_slabs("sum_w_in", received[n]))[None]
        else:
            recv = received[n].reshape(N_DEV, -1, shape[-1])
        res = adamw("adamw_" + n, recv, two_d(wts[n]), two_d(mom_m[n]), two_d(mom_v[n]), dep=dep)
        out_g[n], out_d[n], out_m[n], out_v[n] = (r.reshape(shape) for r in res)
        return res[0]

    last = None
    for n in ("ffn_w2", "ffn_w1", "xattn_wo") + qkv + ("w_out",):
        last = run_adamw(n, last)
    received.update(zip(tail, exchange_wait("scatter_tail_wait", sc_tail, after=last)))
    for n in tail:
        last = run_adamw(n, last)
    (small_all,) = exchange_wait("gather_small_wait", gather_small, after=last)
    res = adamw("adamw_small", small_all, _pack_small(wts), _pack_small(mom_m), _pack_small(mom_v))
    shapes = {n: wts[n].shape for n in _SMALL}
    for dst, packed in zip((out_g, out_d, out_m, out_v), res):
        dst.update(_unpack_small(packed, shapes))

    loss = lax.psum(loss_blk[0, 0], ("x", "y", "c"))
    return (loss, dx[None], *[out_g[n] for n in _WEIGHTS], *[out_d[n] for n in _WEIGHTS],
            *[out_m[n] for n in _WEIGHTS], *[out_v[n] for n in _WEIGHTS])
```

```python
import functools

import jax
import jax.numpy as jnp
from jax import lax
from jax.experimental import pallas as pl
from jax.experimental.pallas import tpu as pltpu

F32 = jnp.float32
BF16 = jnp.bfloat16

N_DEV = 8
D_MODEL = 2048
NORM_EPS = 1e-6
SSD_WIDTH = 1024
SSD_HEAD_DIM = 64
SSD_STATE = 128
SSD_CHUNK = 128
SSD_HEADS_PER_GROUP = 8
RWKV_WIDTH = 1024
RWKV_HEADS = 16
RWKV_HEAD_DIM = 64
RWKV_LN_EPS = 64e-5
RWKV_CHUNK = 128
RWKV_HEADS_PER_STEP = 16
XATTN_HEADS = 4
XATTN_HEAD_DIM = 512
D_FF = 8192
LANES = 128
SUBLANES = 8
VMEM_LIMIT = 56 * 1024 * 1024

ADAM_LR = 0.001
ADAM_B1 = 0.9
ADAM_B2 = 0.999
ADAM_EPS = 1e-08
ADAM_WD = 0.01
ADAM_STEP = 10

_DN = {"nn": ((1,), (0,)), "nt": ((1,), (1,)), "tn": ((0,), (0,))}


def _dg(a, b, mode):
    (ca,), (cb,) = _DN[mode]
    dn = (((ca + 1,), (cb + 1,)), ((0,), (0,))) if a.ndim == 3 else (((ca,), (cb,)), ((), ()))
    return lax.dot_general(a, b, dn, preferred_element_type=F32)


@functools.partial(jax.custom_vjp, nondiff_argnums=(2,))
def bdot(a, b, mode):
    return _dg(a.astype(BF16), b.astype(BF16), mode)


def _bdot_fwd(a, b, mode):
    return bdot(a, b, mode), (a, b)


def _bdot_bwd(mode, res, g):
    a, b = res
    ab, bb, gb = a.astype(BF16), b.astype(BF16), g.astype(BF16)
    if mode == "nn":
        da, db = _dg(gb, bb, "nt"), _dg(ab, gb, "tn")
    elif mode == "nt":
        da, db = _dg(gb, bb, "nn"), _dg(gb, ab, "tn")
    else:
        da, db = _dg(bb, gb, "nt"), _dg(ab, gb, "nn")
    return da.astype(a.dtype), db.astype(b.dtype)


bdot.defvjp(_bdot_fwd, _bdot_bwd)


def _split2(x):
    hi = x.astype(BF16)
    return hi, (x - hi.astype(F32)).astype(BF16)


def _dot01(x, m01):
    hi, lo = _split2(x)
    return _dg(hi, m01, "nn") + _dg(lo, m01, "nn")


def _exact_dot_impl(a, b, mode, exact):
    if exact == "a":
        ae = a.astype(BF16)
        return sum(_dg(ae, part, mode) for part in _split2(b))
    be = b.astype(BF16)
    return sum(_dg(part, be, mode) for part in _split2(a))


@functools.partial(jax.custom_vjp, nondiff_argnums=(2, 3))
def exact_dot(a, b, mode, exact):
    return _exact_dot_impl(a, b, mode, exact)


def _exact_dot_fwd(a, b, mode, exact):
    return _exact_dot_impl(a, b, mode, exact), (a, b)


def _exact_dot_bwd(mode, exact, res, g):
    a, b = res
    if exact == "a":
        db = {"nn": lambda: _exact_dot_impl(a, g, "tn", "a"), "nt": lambda: _exact_dot_impl(g, a, "tn", "b"),
              "tn": lambda: _exact_dot_impl(a, g, "nn", "a")}[mode]()
        return jnp.zeros_like(a), db
    da = {"nn": lambda: _exact_dot_impl(g, b, "nt", "b"), "nt": lambda: _exact_dot_impl(g, b, "nn", "b"),
          "tn": lambda: _exact_dot_impl(b, g, "nt", "a")}[mode]()
    return da, jnp.zeros_like(b)


exact_dot.defvjp(_exact_dot_fwd, _exact_dot_bwd)


def _head_indicator(width, heads, transpose):
    hd = width // heads
    shape = (LANES, width) if transpose else (width, LANES)
    lane = lax.broadcasted_iota(jnp.int32, shape, 1 if not transpose else 0)
    pos = lax.broadcasted_iota(jnp.int32, shape, 0 if not transpose else 1)
    return ((pos >= lane * hd) & (pos < lane * hd + hd)).astype(BF16)


@jax.custom_vjp
def head_sum(x):
    w = x.shape[-1]
    e = _head_indicator(w, w // RWKV_HEAD_DIM, False)
    et = _head_indicator(w, w // RWKV_HEAD_DIM, True)
    return _dot01(_dot01(x, e), et)


head_sum.defvjp(lambda x: (head_sum(x), None), lambda _, g: (head_sum(g),))


def rmsnorm_fn(x, g):
    y = x * lax.rsqrt(jnp.mean(x * x, axis=-1, keepdims=True) + NORM_EPS)
    return ((y * g).astype(BF16),)


def cast_fn(x):
    return (x.astype(BF16),)


def ssd_pre_fn(xbc, xbc1, xbc2, xbc3, dt_raw, conv_w, conv_b, dt_bias):
    c = conv_w[3:4] * xbc + conv_w[2:3] * xbc1 + conv_w[1:2] * xbc2 + conv_w[0:1] * xbc3 + conv_b
    act = c * jax.nn.sigmoid(c)
    dt = jax.nn.softplus(dt_raw + dt_bias)
    return act[:, :SSD_WIDTH], act[:, SSD_WIDTH:SSD_WIDTH + 256], act[:, SSD_WIDTH + 256:], dt


def ssd_post_fn(yscan, z, norm_g):
    y = yscan * (z * jax.nn.sigmoid(z))
    half = SSD_WIDTH // 2
    parts = []
    for g in range(2):
        yg = y[:, g * half:(g + 1) * half]
        parts.append(yg * lax.rsqrt(jnp.mean(yg * yg, axis=-1, keepdims=True) + NORM_EPS))
    return ((jnp.concatenate(parts, axis=-1) * norm_g).astype(BF16),)


def rwkv_pre_fn(rkv, rkv_p, pg, pg_p, pwa, pwa_p, mu_rkv, mu_pg, mu_pwa, w0, w2p, a0, a2p, g2, k_k, k_a):
    w = RWKV_WIDTH
    rkv = rkv + (rkv_p - rkv) * mu_rkv
    pg = pg + (pg_p - pg) * mu_pg
    pwa = pwa + (pwa_p - pwa) * mu_pwa
    r, k, v = rkv[:, :w], rkv[:, w:2 * w], rkv[:, 2 * w:]
    pw, pa = pwa[:, :LANES], pwa[:, LANES:]
    w_log = -jax.nn.softplus(-(w0 + bdot(jnp.tanh(pw), w2p, "nn"))) - 0.5
    lw = -jnp.exp(w_log)
    iclr = jax.nn.sigmoid(a0 + bdot(pa, a2p, "nn"))
    gate = bdot(jax.nn.sigmoid(pg), g2, "nn")
    kk = k * k_k
    kap = kk * lax.rsqrt(jnp.maximum(head_sum(kk * kk), 1e-24))
    k_mod = k * (1.0 + (iclr - 1.0) * k_a)
    return r, lw, k_mod, v, kap, kap * iclr, gate


def rwkv_post_fn(ys, r, k_mod, v, gate, ln_w, ln_b, r_k):
    inv_n = 1.0 / RWKV_HEAD_DIM
    mean = head_sum(ys) * inv_n
    yc = ys - mean
    var = head_sum(yc * yc) * inv_n
    yn = yc * lax.rsqrt(var + RWKV_LN_EPS) * ln_w + ln_b
    bonus = head_sum(r * k_mod * r_k) * v
    return (((yn + bonus) * gate).astype(BF16),)


def attn_fn(q, kx, vx):
    outs = []
    for h in range(XATTN_HEADS):
        sl = slice(h * XATTN_HEAD_DIM, (h + 1) * XATTN_HEAD_DIM)
        s = bdot(q[:, sl], kx[:, sl], "nt") * (XATTN_HEAD_DIM ** -0.5)
        s = s - jnp.max(s, axis=-1, keepdims=True)
        p = jnp.exp(s)
        p = p / jnp.sum(p, axis=-1, keepdims=True)
        outs.append(bdot(p, vx[:, sl], "nn"))
    return (jnp.concatenate(outs, axis=-1).astype(BF16),)


def loss_fn(x, tgt, g):
    y = x * lax.rsqrt(jnp.mean(x * x, axis=-1, keepdims=True) + NORM_EPS) * g
    err = jnp.square(y - tgt)
    return 0.5 * jnp.sum(jnp.mean(err, axis=-1, keepdims=True), axis=0, keepdims=True)


def _tri_masks(n):
    row = lax.broadcasted_iota(jnp.int32, (n, n), 0)
    col = lax.broadcasted_iota(jnp.int32, (n, n), 1)
    return col <= row, col < row, row == col


@jax.custom_vjp
def unit_lower_inverse(a):
    c = a.shape[-1]
    eye = _tri_masks(c)[2].astype(F32)
    m = -a
    inv = eye + m
    n = 1
    while n * 2 < c:
        m = bdot(m, m, "nn")
        inv = bdot(inv, eye + m, "nn")
        n *= 2
    return inv


def _unit_lower_inverse_fwd(a):
    inv = unit_lower_inverse(a)
    return inv, inv


def _unit_lower_inverse_bwd(inv, g):
    return (-bdot(bdot(inv, g, "tn"), inv, "nt"),)


unit_lower_inverse.defvjp(_unit_lower_inverse_fwd, _unit_lower_inverse_bwd)


@jax.custom_vjp
def known_inverse(a, inv):
    return inv


known_inverse.defvjp(lambda a, inv: (inv, inv),
                     lambda inv, g: (_unit_lower_inverse_bwd(inv, g)[0], jnp.zeros_like(inv)))


def rwkv_chunk_fn(st0, r, lw, k, v, kap, b, inv=None):
    h, c = r.shape[0], r.shape[1]
    incl, strict, _ = _tri_masks(c)
    cum = exact_dot(jnp.broadcast_to(incl.astype(F32), (h, c, c)), lw, "nn", "a")
    g_in = jnp.exp(cum)
    g_prev = jnp.exp(cum - lw)
    g_inv = jnp.exp(-cum)
    g_end = jnp.exp(cum[:, c - 1:c, :] - cum)
    kap_t, k_t, b_t, r_t = kap * g_prev, k * g_inv, b * g_inv, r * g_in
    a_ub = jnp.where(strict, bdot(kap_t, b_t, "nt"), 0.0)
    a_vk = jnp.where(strict, bdot(kap_t, k_t, "nt"), 0.0)
    rhs = -(bdot(kap_t, st0, "nt") + bdot(a_vk, v, "nn"))
    inv = unit_lower_inverse(a_ub) if inv is None else known_inverse(a_ub, inv)
    u = bdot(inv, rhs, "nn")
    y = (bdot(r_t, st0, "nt")
         + bdot(jnp.where(incl, bdot(r_t, k_t, "nt"), 0.0), v, "nn")
         + bdot(jnp.where(incl, bdot(r_t, b_t, "nt"), 0.0), u, "nn"))
    st1 = jnp.exp(cum[:, c - 1:c, :]) * st0 + bdot(v, k * g_end, "tn") + bdot(u, b * g_end, "tn")
    return y, st1, inv


def ssd_chunk_fn(group, h0, xs, bm, cm, dt, a_log, d_skip):
    q, nh = xs.shape[0], SSD_HEADS_PER_GROUP
    causal, _, _ = _tri_masks(q)
    a_row = -jnp.exp(a_log)
    cs_all = exact_dot(causal.astype(F32), dt * a_row, "nn", "a")
    cs_t = cs_all.T
    lanes = range(group * nh, (group + 1) * nh)
    cs = jnp.stack([cs_all[:, hl:hl + 1] for hl in lanes])
    cs_row = jnp.stack([cs_t[hl:hl + 1, :] for hl in lanes])
    dt_h = jnp.stack([dt[:, hl:hl + 1] for hl in lanes])
    d_h = jnp.stack([d_skip[:, hl:hl + 1] for hl in lanes])
    x = _stack_lanes(xs, nh)
    h0s = _stack_rows(h0, nh)
    lmat = jnp.where(causal, jnp.exp(jnp.where(causal, cs - cs_row, 0.0)), 0.0)
    cb = bdot(cm, bm, "nt")
    xdt = x * dt_h
    cl = cs[:, q - 1:q, :]
    cm_b = jnp.broadcast_to(cm, (nh,) + cm.shape)
    bm_b = jnp.broadcast_to(bm, (nh,) + bm.shape)
    y = bdot(cb * lmat, xdt, "nn") + bdot(cm_b, h0s, "nt") * jnp.exp(cs) + x * d_h
    h1 = h0s * jnp.exp(cl) + bdot(xdt * jnp.exp(cl - cs), bm_b, "tn")
    return jnp.concatenate([y[e] for e in range(nh)], axis=-1), jnp.concatenate([h1[e] for e in range(nh)], axis=0)


class Rows:
    def __init__(self, arr, w=None, cb=0, shifts=()):
        self.arr, self.w, self.cb, self.shifts = arr, (arr.shape[1] if w is None else w), cb, tuple(shifts)


def _as_rows(x):
    return x if isinstance(x, Rows) else Rows(x)


def _shift_down(x, halo, k):
    rolled = pltpu.roll(x, k, 0)
    first = rolled[0:SUBLANES]
    rid = lax.broadcasted_iota(jnp.int32, first.shape, 0)
    patched = jnp.where(rid < k, pltpu.roll(halo, k, 0), first)
    return jnp.concatenate([patched, rolled[SUBLANES:]], axis=0)


def _shift_up(g, carry, k):
    tm = g.shape[0]
    rolled = pltpu.roll(g, tm - k, 0)
    last = rolled[tm - SUBLANES:]
    rid = lax.broadcasted_iota(jnp.int32, last.shape, 0)
    patched = jnp.where(rid >= SUBLANES - k, pltpu.roll(carry, SUBLANES - k, 0), last)
    return jnp.concatenate([rolled[:tm - SUBLANES], patched], axis=0)


def _params():
    return pltpu.CompilerParams(vmem_limit_bytes=VMEM_LIMIT)


def _load_rows(refs, pos, rins, first_block):
    vals = []
    for r in rins:
        x = refs[pos][...].astype(F32) if refs[pos].dtype != F32 else refs[pos][...]
        pos += 1
        vals.append(x)
        if r.shifts:
            halo = refs[pos][...]
            pos += 1
            halo = jnp.where(first_block, jnp.zeros_like(halo), halo)
            for k in r.shifts:
                vals.append(_shift_down(x, halo, k))
    return vals, pos


def _row_specs(rins, tm, blk):
    specs, args = [], []
    for r in rins:
        specs.append(pl.BlockSpec((tm, r.w), lambda i, cb=r.cb: (blk(i), cb)))
        args.append(r.arr)
        if r.shifts:
            per = tm // SUBLANES
            specs.append(pl.BlockSpec((SUBLANES, r.w), lambda i, cb=r.cb: (jnp.maximum(blk(i) * per - 1, 0), cb)))
            args.append(r.arr)
    return specs, args


def rowwise_fwd(name, fn, rins, params, outs, tm, deps=(), into=None):
    rins = [_as_rows(r) for r in rins]
    t = rins[0].arr.shape[0]
    tm = min(tm, t)
    nb = t // tm
    specs, args = _row_specs(rins, tm, lambda i: i)
    for p in params:
        specs.append(pl.BlockSpec(p.shape, lambda i: (0, 0)))
        args.append(p)
    for dep in deps:
        specs.append(pl.BlockSpec(memory_space=pl.ANY))
        args.append(dep)
    out_specs = [pl.BlockSpec((tm, w), lambda i: (i, 0)) for w, _ in outs]
    out_shape = [jax.ShapeDtypeStruct((t, w), dt) for w, dt in outs]
    aliases = {}
    if into is not None:
        target, total_width, col_block = into
        out_specs = [pl.BlockSpec((tm, outs[0][0]), lambda i: (i, col_block))]
        out_shape = [jax.ShapeDtypeStruct((t, total_width), outs[0][1])]
        if target is not None:
            aliases = {len(args): 0}
            specs.append(pl.BlockSpec(memory_space=pl.ANY))
            args.append(target)
    n_in = len(args)

    def body(*refs):
        vals, pos = _load_rows(refs, 0, rins, pl.program_id(0) == 0)
        pv = [refs[pos + j][...] for j in range(len(params))]
        res = fn(*vals, *pv)
        for o_ref, o in zip(refs[n_in:], res):
            o_ref[...] = o.astype(o_ref.dtype)

    return pl.pallas_call(
        body, name=name, grid=(nb,), in_specs=specs, out_specs=out_specs, out_shape=out_shape,
        input_output_aliases=aliases, compiler_params=_params(),
    )(*args)


def rowwise_bwd(name, fn, rins, params, cts, tm, grad_dtypes, row_add=None):
    rins = [_as_rows(r) for r in rins]
    cts = [[_as_rows(c) for c in lst] for lst in cts]
    row_add = [_as_rows(a) for a in (row_add or [])]
    t = rins[0].arr.shape[0]
    tm = min(tm, t)
    nb = t // tm
    rev = lambda i: nb - 1 - i
    specs, args = _row_specs(rins, tm, rev)
    for p in params:
        specs.append(pl.BlockSpec(p.shape, lambda i: (0, 0)))
        args.append(p)
    flat_cts = [c for lst in cts for c in lst] + row_add
    for c in flat_cts:
        specs.append(pl.BlockSpec((tm, c.w), lambda i, cb=c.cb: (rev(i), cb)))
        args.append(c.arr)
    n_in = len(args)
    want = [i for i, d in enumerate(grad_dtypes) if d is not None]
    out_specs = [pl.BlockSpec((tm, rins[i].w), lambda i_: (rev(i_), 0)) for i in want]
    out_shape = [jax.ShapeDtypeStruct((t, rins[i].w), grad_dtypes[i]) for i in want]
    out_specs += [pl.BlockSpec(p.shape, lambda i: (0, 0)) for p in params]
    out_shape += [jax.ShapeDtypeStruct(p.shape, F32) for p in params]
    n_out = len(out_shape)
    scratch = [pltpu.VMEM((SUBLANES, r.w), F32) for r in rins for _ in r.shifts]

    def body(*refs):
        i = pl.program_id(0)
        vals, pos = _load_rows(refs, 0, rins, rev(i) == 0)
        pv = [refs[pos + j][...] for j in range(len(params))]
        pos += len(params)
        outs, vjp = jax.vjp(fn, *vals, *pv)
        ct_vals = []
        for o, lst in zip(outs, cts):
            acc = None
            for _ in lst:
                cv = refs[pos][...].astype(F32)
                pos += 1
                acc = cv if acc is None else acc + cv
            ct_vals.append(acc.astype(o.dtype))
        adds = [refs[pos + j][...].astype(F32) for j in range(len(row_add))]
        grads = vjp(tuple(ct_vals))
        out_refs = refs[n_in:n_in + n_out]
        carry_refs = refs[n_in + n_out:]

        @pl.when(i == 0)
        def _():
            for cr in carry_refs:
                cr[...] = jnp.zeros_like(cr)
            for pr in out_refs[len(want):]:
                pr[...] = jnp.zeros_like(pr)

        gi, ci, oi = 0, 0, 0
        for idx, r in enumerate(rins):
            d = grads[gi]
            gi += 1
            for k in r.shifts:
                dk = grads[gi]
                gi += 1
                d = d + _shift_up(dk, carry_refs[ci][...], k)
                carry_refs[ci][...] = dk[0:SUBLANES]
                ci += 1
            if idx == 0:
                for a in adds:
                    d = d + a
            if grad_dtypes[idx] is not None:
                out_refs[oi][...] = d.astype(out_refs[oi].dtype)
                oi += 1
        for pr, gp in zip(out_refs[len(want):], grads[gi:]):
            pr[...] += gp

    res = pl.pallas_call(
        body, name=name, grid=(nb,), in_specs=specs, out_specs=out_specs, out_shape=out_shape,
        scratch_shapes=scratch, compiler_params=_params(),
    )(*args)
    return res[:len(want)], res[len(want):]


def _pick(n, pref):
    for c in pref:
        if n % c == 0:
            return c
    return n


MM_VMEM_BUDGET = 40 * 1024 * 1024
MM_PEAK_FLOPS = 0.9e15
MM_HBM_BYTES_PER_S = 3.0e12
MM_STEP_SECONDS = 0.35e-6


def _mm_tiles(m, n, k, size_a, size_b, size_out, size_res, single_k):
    best = None
    for tk in sorted({c for c in (k, 2048, 1024, 512, 256, 128) if c <= 2048 and k % c == 0}, reverse=True):
        for tm in sorted({c for c in (m, 1024, 512, 256, 128) if c <= 1024 and m % c == 0}, reverse=True):
            for tn in sorted({c for c in (n, 2048, 1536, 1024, 768, 512, 384, 256, 128) if c <= 2048 and n % c == 0},
                             reverse=True):
                nk = k // tk
                vmem = 2 * (tm * tk * size_a + tk * tn * size_b + tm * tn * (size_out + size_res))
                vmem += tm * tn * 4 * (2 if nk > 1 or not single_k else 1)
                vmem += (tm * tk * 2 if size_a > 2 else 0) + (tk * tn * 2 if size_b > 2 else 0)
                if vmem > MM_VMEM_BUDGET:
                    continue
                steps = (m // tm) * (n // tn) * nk
                a_reads = 1 if (nk == 1 and single_k) else n // tn
                traffic = m * k * size_a * a_reads + k * n * size_b * (m // tm) + m * n * (size_out + size_res)
                cost = max(2.0 * m * n * k / MM_PEAK_FLOPS, traffic / MM_HBM_BYTES_PER_S) + steps * MM_STEP_SECONDS
                if best is None or cost < best[0]:
                    best = (cost, tm, tn, tk)
    return best[1:]


def mm(name, a, b, mode, out_dtype=F32, res=None, b_slabs=None, out_slabs=None, dep=None, epi=None, extras=(),
       out_dtypes=None):
    if mode == "tn":
        k_dim, m_dim = a.shape
    else:
        m_dim, k_dim = a.shape
    if b_slabs:
        n_dim = b.shape[0] * b.shape[2] if mode == "nn" else b.shape[1]
    else:
        n_dim = b.shape[0] if mode == "nt" else b.shape[1]
    n_slabs = out_slabs or (b_slabs if (b_slabs and mode == "nn") else 1)
    k_slabs = b_slabs if (b_slabs and mode == "nt") else 1
    if epi is None:
        out_dtypes = [out_dtype]
        if res is None:
            epi = lambda acc: (acc,)
        else:
            extras, epi = [res], lambda acc, r: (acc + r,)
    tm, tn, tk = _mm_tiles(m_dim, n_dim // n_slabs, k_dim // k_slabs, a.dtype.itemsize, b.dtype.itemsize,
                           sum(jnp.dtype(dt).itemsize for dt in out_dtypes), sum(e.dtype.itemsize for e in extras),
                           single_k=(k_slabs == 1))
    nji = n_dim // n_slabs // tn
    nki = k_dim // k_slabs // tk
    nblk = lambda js, j: js * nji + j
    kblk = lambda ks, k: ks * nki + k
    if mode == "tn":
        a_spec = pl.BlockSpec((tk, tm), lambda i, js, j, ks, k: (kblk(ks, k), i))
    else:
        a_spec = pl.BlockSpec((tm, tk), lambda i, js, j, ks, k: (i, kblk(ks, k)))
    if b_slabs and mode == "nn":
        b_spec = pl.BlockSpec((None, tk, tn), lambda i, js, j, ks, k: (js, k, j))
    elif b_slabs and mode == "nt":
        b_spec = pl.BlockSpec((None, tn, tk), lambda i, js, j, ks, k: (ks, nblk(js, j), k))
    elif mode == "nt":
        b_spec = pl.BlockSpec((tn, tk), lambda i, js, j, ks, k: (nblk(js, j), kblk(ks, k)))
    else:
        b_spec = pl.BlockSpec((tk, tn), lambda i, js, j, ks, k: (kblk(ks, k), nblk(js, j)))
    specs, args = [a_spec, b_spec], [a, b]
    for e in extras:
        specs.append(pl.BlockSpec((tm, tn), lambda i, js, j, ks, k: (i, nblk(js, j))))
        args.append(e)
    if dep is not None:
        specs.append(pl.BlockSpec(memory_space=pl.ANY))
        args.append(dep)
    if out_slabs:
        o_specs = [pl.BlockSpec((None, tm, tn), lambda i, js, j, ks, k: (js, i, j))]
        o_shapes = [jax.ShapeDtypeStruct((out_slabs, m_dim, n_dim // out_slabs), out_dtypes[0])]
    else:
        o_specs = [pl.BlockSpec((tm, tn), lambda i, js, j, ks, k: (i, nblk(js, j))) for _ in out_dtypes]
        o_shapes = [jax.ShapeDtypeStruct((m_dim, n_dim), dt) for dt in out_dtypes]

    one_k_step = k_slabs * nki == 1
    n_in, n_out = len(args), len(out_dtypes)

    def body(*refs):
        a_ref, b_ref = refs[0], refs[1]
        part = _dg(a_ref[...].astype(BF16), b_ref[...].astype(BF16), mode)

        def finish(acc):
            outs = epi(acc, *[refs[2 + j][...].astype(F32) for j in range(len(extras))])
            for o_ref, o in zip(refs[n_in:n_in + n_out], outs):
                o_ref[...] = o.astype(o_ref.dtype)

        if one_k_step:
            finish(part)
            return
        acc_ref = refs[n_in + n_out]
        ks, kk = pl.program_id(3), pl.program_id(4)

        @pl.when((ks == 0) & (kk == 0))
        def _():
            acc_ref[...] = part

        @pl.when((ks > 0) | (kk > 0))
        def _():
            acc_ref[...] += part

        pl.when((ks == k_slabs - 1) & (kk == nki - 1))(lambda: finish(acc_ref[...]))

    grid = (m_dim // tm, n_slabs, nji, k_slabs, nki)
    scratch = [] if one_k_step else [pltpu.VMEM((tm, tn), F32)]
    out = pl.pallas_call(
        body, name=name, grid=grid, in_specs=specs, out_specs=o_specs, out_shape=o_shapes, scratch_shapes=scratch,
        compiler_params=pltpu.CompilerParams(
            dimension_semantics=("parallel", "parallel", "parallel", "arbitrary", "arbitrary"),
            vmem_limit_bytes=VMEM_LIMIT),
    )(*args)
    return out[0] if n_out == 1 else out


def _stack_lanes(x, n):
    w = x.shape[1] // n
    return jnp.stack([x[:, i * w:(i + 1) * w] for i in range(n)])


def _stack_rows(x, n):
    w = x.shape[0] // n
    return jnp.stack([x[i * w:(i + 1) * w, :] for i in range(n)])


def rwkv_scan_fwd(r, lw, k, v, kap, b):
    t = r.shape[0]
    c, hps, hd = min(RWKV_CHUNK, t), RWKV_HEADS_PER_STEP, RWKV_HEAD_DIM
    nc, ng, wl = t // c, RWKV_HEADS // hps, hps * hd
    spec = pl.BlockSpec((c, wl), lambda g, ci: (ci, g))

    def body(r_ref, lw_ref, k_ref, v_ref, kap_ref, b_ref, y_ref, ck_ref, inv_ref, st_ref):
        @pl.when(pl.program_id(1) == 0)
        def _():
            st_ref[...] = jnp.zeros_like(st_ref)

        st = st_ref[...]
        ck_ref[...] = st
        ins = [x[...] for x in (r_ref, lw_ref, k_ref, v_ref, kap_ref, b_ref)]
        y, st1, inv = rwkv_chunk_fn(_stack_rows(st, hps), *[_stack_lanes(x, hps) for x in ins])
        y_ref[...] = jnp.concatenate([y[h] for h in range(hps)], axis=-1)
        st_ref[...] = jnp.concatenate([st1[h] for h in range(hps)], axis=0)
        inv_ref[...] = jnp.concatenate([inv[h] for h in range(hps)], axis=0)

    return pl.pallas_call(
        body, name="rwkv_scan_fwd", grid=(ng, nc), in_specs=[spec] * 6,
        out_specs=[spec, pl.BlockSpec((None, wl, hd), lambda g, ci: (ci, g, 0)),
                   pl.BlockSpec((None, hps * c, c), lambda g, ci: (ci, g, 0))],
        out_shape=[jax.ShapeDtypeStruct((t, RWKV_WIDTH), F32), jax.ShapeDtypeStruct((nc, RWKV_WIDTH, hd), F32),
                   jax.ShapeDtypeStruct((nc, RWKV_HEADS * c, c), F32)],
        scratch_shapes=[pltpu.VMEM((wl, hd), F32)], compiler_params=_params(),
    )(r, lw, k, v, kap, b)


def rwkv_scan_bwd(r, lw, k, v, kap, b, ck, inv_ck, dy):
    t = r.shape[0]
    c, hps, hd = min(RWKV_CHUNK, t), RWKV_HEADS_PER_STEP, RWKV_HEAD_DIM
    nc, ng, wl = t // c, RWKV_HEADS // hps, hps * hd
    spec = pl.BlockSpec((c, wl), lambda g, ci: (nc - 1 - ci, g))

    def body(r_ref, lw_ref, k_ref, v_ref, kap_ref, b_ref, ck_ref, inv_ref, dy_ref, *rest):
        out_refs, dst_ref = rest[:6], rest[6]

        @pl.when(pl.program_id(1) == 0)
        def _():
            dst_ref[...] = jnp.zeros_like(dst_ref)

        ins = [x[...] for x in (r_ref, lw_ref, k_ref, v_ref, kap_ref, b_ref)]
        dyv, ck, dst = dy_ref[...].astype(F32), ck_ref[...], dst_ref[...]
        chunk = lambda *a: rwkv_chunk_fn(*a, inv=_stack_rows(inv_ref[...], hps))[:2]
        _, vjp = jax.vjp(chunk, _stack_rows(ck, hps), *[_stack_lanes(x, hps) for x in ins])
        grads = vjp((_stack_lanes(dyv, hps), _stack_rows(dst, hps)))
        dst_ref[...] = jnp.concatenate([grads[0][h] for h in range(hps)], axis=0)
        for j in range(6):
            out_refs[j][...] = jnp.concatenate([grads[1 + j][h] for h in range(hps)], axis=-1).astype(BF16)

    return pl.pallas_call(
        body, name="rwkv_scan_bwd", grid=(ng, nc),
        in_specs=[spec] * 6 + [pl.BlockSpec((None, wl, hd), lambda g, ci: (nc - 1 - ci, g, 0)),
                               pl.BlockSpec((None, hps * c, c), lambda g, ci: (nc - 1 - ci, g, 0)), spec],
        out_specs=[spec] * 6, out_shape=[jax.ShapeDtypeStruct((t, RWKV_WIDTH), BF16)] * 6,
        scratch_shapes=[pltpu.VMEM((wl, hd), F32)], compiler_params=_params(),
    )(r, lw, k, v, kap, b, ck, inv_ck, dy)


def _ssd_specs(q, blk):
    gw = SSD_WIDTH // 2
    return [pl.BlockSpec((q, gw), lambda g, ci: (blk(ci), g)),
            pl.BlockSpec((q, SSD_STATE), lambda g, ci: (blk(ci), g)),
            pl.BlockSpec((q, SSD_STATE), lambda g, ci: (blk(ci), g)),
            pl.BlockSpec((q, LANES), lambda g, ci: (blk(ci), 0)),
            pl.BlockSpec((1, LANES), lambda g, ci: (0, 0)),
            pl.BlockSpec((1, LANES), lambda g, ci: (0, 0))]


def ssd_scan_fwd(xs, bm, cm, dt, a_log, d_skip):
    t = xs.shape[0]
    q = min(SSD_CHUNK, t)
    nc, gw = t // q, SSD_WIDTH // 2

    def body(xs_ref, bm_ref, cm_ref, dt_ref, al_ref, d_ref, y_ref, ck_ref, h_ref):
        @pl.when(pl.program_id(1) == 0)
        def _():
            h_ref[...] = jnp.zeros_like(h_ref)

        ck_ref[...] = h_ref[...]
        args = (h_ref[...], xs_ref[...], bm_ref[...], cm_ref[...], dt_ref[...], al_ref[...], d_ref[...])
        g = pl.program_id(0)

        @pl.when(g == 0)
        def _():
            y, h1 = ssd_chunk_fn(0, *args)
            y_ref[...] = y
            h_ref[...] = h1

        @pl.when(g == 1)
        def _():
            y, h1 = ssd_chunk_fn(1, *args)
            y_ref[...] = y
            h_ref[...] = h1

    return pl.pallas_call(
        body, name="ssd_scan_fwd", grid=(2, nc), in_specs=_ssd_specs(q, lambda ci: ci),
        out_specs=[pl.BlockSpec((q, gw), lambda g, ci: (ci, g)),
                   pl.BlockSpec((None, gw, SSD_STATE), lambda g, ci: (ci, g, 0))],
        out_shape=[jax.ShapeDtypeStruct((t, SSD_WIDTH), F32), jax.ShapeDtypeStruct((nc, SSD_WIDTH, SSD_STATE), F32)],
        scratch_shapes=[pltpu.VMEM((gw, SSD_STATE), F32)], compiler_params=_params(),
    )(xs, bm, cm, dt, a_log, d_skip)


def ssd_scan_bwd(xs, bm, cm, dt, a_log, d_skip, ck, dy):
    t = xs.shape[0]
    q = min(SSD_CHUNK, t)
    nc, gw = t // q, SSD_WIDTH // 2
    rev = lambda ci: nc - 1 - ci

    def body(xs_ref, bm_ref, cm_ref, dt_ref, al_ref, d_ref, ck_ref, dy_ref,
             dxs_ref, dbm_ref, dcm_ref, ddt_ref, dal_ref, dd_ref, dh_ref):
        g, ci = pl.program_id(0), pl.program_id(1)

        @pl.when(ci == 0)
        def _():
            dh_ref[...] = jnp.zeros_like(dh_ref)

        @pl.when((ci == 0) & (g == 0))
        def _():
            dal_ref[...] = jnp.zeros_like(dal_ref)
            dd_ref[...] = jnp.zeros_like(dd_ref)

        args = (ck_ref[...], xs_ref[...], bm_ref[...], cm_ref[...], dt_ref[...], al_ref[...], d_ref[...])

        def run(group):
            _, vjp = jax.vjp(functools.partial(ssd_chunk_fn, group), *args)
            dh0, dxs, dbm, dcm, ddt, dal, dd = vjp((dy_ref[...].astype(F32), dh_ref[...]))
            dh_ref[...] = dh0
            dxs_ref[...] = dxs.astype(BF16)
            dbm_ref[...] = dbm.astype(BF16)
            dcm_ref[...] = dcm.astype(BF16)
            ddt_ref[...] = ddt
            dal_ref[...] += dal
            dd_ref[...] += dd

        pl.when(g == 0)(lambda: run(0))
        pl.when(g == 1)(lambda: run(1))

    in_specs = _ssd_specs(q, rev) + [pl.BlockSpec((None, gw, SSD_STATE), lambda g, ci: (rev(ci), g, 0)),
                                     pl.BlockSpec((q, gw), lambda g, ci: (rev(ci), g))]
    return pl.pallas_call(
        body, name="ssd_scan_bwd", grid=(2, nc), in_specs=in_specs,
        out_specs=[pl.BlockSpec((q, gw), lambda g, ci: (rev(ci), g)),
                   pl.BlockSpec((q, SSD_STATE), lambda g, ci: (rev(ci), g)),
                   pl.BlockSpec((q, SSD_STATE), lambda g, ci: (rev(ci), g)),
                   pl.BlockSpec((None, q, LANES), lambda g, ci: (g, rev(ci), 0)),
                   pl.BlockSpec((1, LANES), lambda g, ci: (0, 0)),
                   pl.BlockSpec((1, LANES), lambda g, ci: (0, 0))],
        out_shape=[jax.ShapeDtypeStruct((t, SSD_WIDTH), BF16), jax.ShapeDtypeStruct((t, 2 * SSD_STATE), BF16),
                   jax.ShapeDtypeStruct((t, 2 * SSD_STATE), BF16), jax.ShapeDtypeStruct((2, t, LANES), F32),
                   jax.ShapeDtypeStruct((1, LANES), F32), jax.ShapeDtypeStruct((1, LANES), F32)],
        scratch_shapes=[pltpu.VMEM((gw, SSD_STATE), F32)], compiler_params=_params(),
    )(xs, bm, cm, dt, a_log, d_skip, ck, dy)


def loss_and_grad(x, tgt, g, tm):
    t, d = x.shape
    tm = min(tm, t)
    nb = t // tm

    def body(x_ref, t_ref, g_ref, loss_ref, dx_ref, dxb_ref, dg_ref):
        @pl.when(pl.program_id(0) == 0)
        def _():
            loss_ref[...] = jnp.zeros_like(loss_ref)
            dg_ref[...] = jnp.zeros_like(dg_ref)

        val, vjp = jax.vjp(loss_fn, x_ref[...], t_ref[...], g_ref[...])
        dx, _, dg = vjp(jnp.ones((1, 1), F32))
        loss_ref[...] += jnp.broadcast_to(val, loss_ref.shape)
        dx_ref[...] = dx
        dxb_ref[...] = dx.astype(BF16)
        dg_ref[...] += dg

    row = pl.BlockSpec((tm, d), lambda i: (i, 0))
    one = pl.BlockSpec((1, d), lambda i: (0, 0))
    return pl.pallas_call(
        body, name="loss_and_grad", grid=(nb,), in_specs=[row, row, one],
        out_specs=[pl.BlockSpec((SUBLANES, LANES), lambda i: (0, 0)), row, row, one],
        out_shape=[jax.ShapeDtypeStruct((SUBLANES, LANES), F32), jax.ShapeDtypeStruct((t, d), F32),
                   jax.ShapeDtypeStruct((t, d), BF16), jax.ShapeDtypeStruct((1, d), F32)],
        compiler_params=_params(),
    )(x, tgt, g)


def adamw(name, recv, w, m, v, dep=None):
    rows, cols = w.shape
    n_slabs = recv.shape[0]
    recv_block_bytes = 4 * 1024 * 1024
    tm = _pick(rows, [c for c in (256, 128, 64, 32, 16, 8) if n_slabs * c * cols * 4 <= recv_block_bytes])
    c1 = 1.0 / (1.0 - ADAM_B1 ** ADAM_STEP)
    c2 = 1.0 / (1.0 - ADAM_B2 ** ADAM_STEP)

    n_dep = 0 if dep is None else 1

    def body(recv_ref, w_ref, m_ref, v_ref, *rest):
        g_ref, d_ref, nm_ref, nv_ref = rest[n_dep:]
        g = recv_ref[0].astype(F32)
        for p in range(1, n_slabs):
            g = g + recv_ref[p].astype(F32)
        nm =ADAM_B1 * m_ref[...] + (1.0 - ADAM_B1) * g
        nv = ADAM_B2 * v_ref[...] + (1.0 - ADAM_B2) * jnp.square(g)
        g_ref[...] = g
        nm_ref[...] = nm
        nv_ref[...] = nv
        d_ref[...] = -ADAM_LR * ((nm * c1) / (jnp.sqrt(nv * c2) + ADAM_EPS) + ADAM_WD * w_ref[...])

    blk = pl.BlockSpec((tm, cols), lambda i: (i, 0))
    return pl.pallas_call(
        body, name=name, grid=(rows // tm,),
        in_specs=[pl.BlockSpec((n_slabs, tm, cols), lambda i: (0, i, 0)), blk, blk, blk]
        + [pl.BlockSpec(memory_space=pl.ANY)] * n_dep,
        out_specs=[blk] * 4, out_shape=[jax.ShapeDtypeStruct((rows, cols), F32)] * 4,
        compiler_params=_params(),
    )(recv, w, m, v, *([] if dep is None else [dep]))


def _mesh_pos():
    return lax.axis_index("x"), lax.axis_index("y"), lax.axis_index("c")


def _peer(pos, mask):
    x, y, c = pos
    return (1 - x if mask & 4 else x, 1 - y if mask & 2 else y, 1 - c if mask & 1 else c)


def _linear(pos):
    return 4 * pos[0] + 2 * pos[1] + pos[2]


class Exchange:
    MASKS = {"gather": (1, 2, 3, 4, 5, 6, 7), "scatter": (1, 2, 3, 4, 5, 6, 7), "gather_chips": (1, 2, 4, 6),
             "forward": (2, 4, 6)}

    def __init__(self, xs, kind, lands=None):
        self.kind, self.masks = kind, self.MASKS[kind]
        self.xs = [] if kind == "forward" else list(xs)
        if kind == "forward":
            self.land_shape = [jax.ShapeDtypeStruct(l.shape, l.dtype) for l in lands]
        elif kind == "scatter":
            self.land_shape = [jax.ShapeDtypeStruct(x.shape, x.dtype) for x in xs]
        else:
            self.land_shape = [jax.ShapeDtypeStruct((N_DEV,) + x.shape, x.dtype) for x in xs]
        self.n = len(self.land_shape)
        copies = self.n * len(self.masks)
        self.sems = [pltpu.SemaphoreType.DMA((copies,)), pltpu.SemaphoreType.DMA((copies,)),
                     pltpu.SemaphoreType.DMA((self.n,))]

    def _copies(self, ins, outs, sems, landing):
        send_sems, recv_sems, local_sems = sems
        me = _mesh_pos()
        me_lin = _linear(me)
        local, remote = [], []
        for ti in range(self.n):
            if self.kind != "forward":
                src_mine = ins[ti].at[me_lin] if self.kind == "scatter" else ins[ti]
                local.append(pltpu.make_async_copy(src_mine, outs[ti].at[me_lin], local_sems.at[ti]))
            for j, mask in enumerate(self.masks):
                if self.kind == "forward":
                    peer = _peer(me, 1)
                    src = outs[ti].at[_linear(_peer(me, mask))]
                    dst = outs[ti].at[_linear(_peer(me, mask ^ 1 if landing else mask))]
                else:
                    peer = _peer(me, mask)
                    src = ins[ti].at[_linear(peer)] if self.kind == "scatter" else ins[ti]
                    dst = outs[ti].at[_linear(peer) if landing else me_lin]
                sem_index = ti * len(self.masks) + j
                remote.append(pltpu.make_async_remote_copy(
                    src_ref=src, dst_ref=dst, send_sem=send_sems.at[sem_index], recv_sem=recv_sems.at[sem_index],
                    device_id=peer, device_id_type=pl.DeviceIdType.MESH))
        return local, remote

    def start(self, ins, outs, sems):
        local, remote = self._copies(ins, outs, sems, landing=False)
        for cp in local + remote:
            cp.start()

    def finish(self, ins, outs, sems):
        local, remote = self._copies(ins, outs, sems, landing=True)
        for cp in remote:
            cp.wait_recv()
        for cp in remote:
            cp.wait_send()
        for cp in local:
            cp.wait()


def exchange_start(name, xs, kind, dep=None, lands=None):
    ex = Exchange(xs, kind, lands)
    hbm = pl.BlockSpec(memory_space=pltpu.HBM)
    sem = pl.BlockSpec(memory_space=pltpu.SEMAPHORE)
    if lands is None:
        lands = [lax.empty(s.shape, s.dtype) for s in ex.land_shape]
    n_src, n = len(ex.xs), ex.n
    n_inputs = n_src + n + (0 if dep is None else 1)

    def body(*refs):
        ins, lnd, sems, token = refs[:n_src], refs[n_src:n_src + n], refs[n_inputs:n_inputs + 3], refs[-1]
        ex.start(ins, lnd, sems)
        token[...] = jnp.zeros_like(token)

    res = pl.pallas_call(
        body, name=name, in_specs=[hbm] * (n_src + n) + ([] if dep is None else [pl.BlockSpec(memory_space=pl.ANY)]),
        out_specs=[sem] * 3 + [hbm] * (n_src + n) + [pl.BlockSpec(memory_space=pltpu.VMEM)],
        out_shape=ex.sems + [pltpu.HBM(x.shape, x.dtype) for x in ex.xs]
        + [pltpu.HBM(s.shape, s.dtype) for s in ex.land_shape] + [jax.ShapeDtypeStruct((SUBLANES, LANES), F32)],
        input_output_aliases={i: 3 + i for i in range(n_src + n)},
        compiler_params=pltpu.CompilerParams(has_side_effects=pltpu.SideEffectType.DATAFLOW_SIDE_EFFECTING),
    )(*[pltpu.with_memory_space_constraint(x, pltpu.HBM) for x in ex.xs + list(lands)],
      *([] if dep is None else [dep]))
    return (ex, res[:3], res[3:3 + n_src], res[3 + n_src:3 + n_src + n]), res[-1]


def exchange_wait(name, handles, after):
    ex, sems, srcs, lands = handles
    n_src, n = len(srcs), len(lands)
    hbm = pl.BlockSpec(memory_space=pltpu.HBM)
    sem = pl.BlockSpec(memory_space=pltpu.SEMAPHORE)

    def body(*refs):
        ins, lnd, sem_refs = refs[:n_src], refs[n_src:n_src + n], refs[n_src + n:n_src + n + 3]
        ex.finish(ins, lnd, sem_refs)

    res = pl.pallas_call(
        body, name=name, in_specs=[hbm] * (n_src + n) + [sem] * 3 + [pl.BlockSpec(memory_space=pl.ANY)],
        out_specs=[hbm] * (n_src + n),
        out_shape=[pltpu.HBM(x.shape, x.dtype) for x in srcs] + [pltpu.HBM(x.shape, x.dtype) for x in lands],
        input_output_aliases={i: i for i in range(n_src + n)},
        compiler_params=pltpu.CompilerParams(has_side_effects=pltpu.SideEffectType.DATAFLOW_SIDE_EFFECTING),
    )(*srcs, *lands, *sems, after)
    return res[n_src:]


def forward_start(name, chip_gather, after):
    lands = exchange_wait(name + "_wait", chip_gather, after)
    return exchange_start(name + "_forward_start", [], "forward", lands=lands)


_Z = (0, 1024)
_XBC = (1024, 2560)
_DT = (2560, 2576)
_RKV = (2576, 5648)
_PW = (5648, 5744)
_PA = (5744, 5840)
_PG = (5840, 6096)
D_IN = 6096

_SMALL = ("norm_mix_g", "ssd_conv_b", "ssd_dt_bias", "ssd_a_log", "ssd_d", "ssd_norm_g", "rwkv_mu", "rwkv_w0",
          "rwkv_a0", "rwkv_k_k", "rwkv_k_a", "rwkv_r_k", "rwkv_ln_w", "rwkv_ln_b", "norm_x_g", "norm_mem_g",
          "norm_ffn_g", "final_norm_g")
_WEIGHTS = ("norm_mix_g", "w_in", "ssd_conv_w", "ssd_conv_b", "ssd_dt_bias", "ssd_a_log", "ssd_d", "ssd_norm_g",
            "rwkv_mu", "rwkv_w0", "rwkv_w2", "rwkv_a0", "rwkv_a2", "rwkv_g2", "rwkv_k_k", "rwkv_k_a", "rwkv_r_k",
            "rwkv_ln_w", "rwkv_ln_b", "w_out", "norm_x_g", "norm_mem_g", "xattn_wq", "xattn_wk", "xattn_wv",
            "xattn_wo", "norm_ffn_g", "ffn_w1", "ffn_w2", "final_norm_g")


def _pad_lanes(x, width=LANES):
    return jnp.pad(x, ((0, 0), (0, width - x.shape[1])))


def _pack_small(vals):
    flat = jnp.concatenate([vals[n].reshape(-1) for n in _SMALL])
    rows = -(-flat.shape[0] // (LANES * SUBLANES)) * SUBLANES
    return jnp.pad(flat, (0, rows * LANES - flat.shape[0])).reshape(rows, LANES)


def _unpack_small(packed, shapes):
    flat = packed.reshape(-1)
    out, pos = {}, 0
    for n in _SMALL:
        size = 1
        for s in shapes[n]:
            size *= s
        out[n] = flat[pos:pos + size].reshape(shapes[n])
        pos += size
    return out


def _rows(w, rng):
    return w[rng[0]:rng[1]]


def sum_slabs(name, recv):
    n, rows, cols = recv.shape
    tc = _pick(cols, (256, 128))

    def body(r_ref, o_ref):
        acc = r_ref[0].astype(F32)
        for p in range(1, n):
            acc = acc + r_ref[p].astype(F32)
        o_ref[...] = acc

    return pl.pallas_call(
        body, name=name, grid=(cols // tc,), in_specs=[pl.BlockSpec((n, rows, tc), lambda j: (0, 0, j))],
        out_specs=pl.BlockSpec((rows, tc), lambda j: (0, j)), out_shape=jax.ShapeDtypeStruct((rows, cols), F32),
        compiler_params=_params(),
    )(recv)


def kernel(x, mem, norm_mix_g, w_in, ssd_conv_w, ssd_conv_b, ssd_dt_bias, ssd_a_log, ssd_d, ssd_norm_g, rwkv_mu, rwkv_w0, rwkv_w2, rwkv_a0, rwkv_a2, rwkv_g2, rwkv_k_k, rwkv_k_a, rwkv_r_k, rwkv_ln_w, rwkv_ln_b, w_out, norm_x_g, norm_mem_g, xattn_wq, xattn_wk, xattn_wv, xattn_wo, norm_ffn_g, ffn_w1, ffn_w2, final_norm_g, loss_target, m_norm_mix_g, m_w_in, m_ssd_conv_w, m_ssd_conv_b, m_ssd_dt_bias, m_ssd_a_log, m_ssd_d, m_ssd_norm_g, m_rwkv_mu, m_rwkv_w0, m_rwkv_w2, m_rwkv_a0, m_rwkv_a2, m_rwkv_g2, m_rwkv_k_k, m_rwkv_k_a, m_rwkv_r_k, m_rwkv_ln_w, m_rwkv_ln_b, m_w_out, m_norm_x_g, m_norm_mem_g, m_xattn_wq, m_xattn_wk, m_xattn_wv, m_xattn_wo, m_norm_ffn_g, m_ffn_w1, m_ffn_w2, m_final_norm_g, v_norm_mix_g, v_w_in, v_ssd_conv_w, v_ssd_conv_b, v_ssd_dt_bias, v_ssd_a_log, v_ssd_d, v_ssd_norm_g, v_rwkv_mu, v_rwkv_w0, v_rwkv_w2, v_rwkv_a0, v_rwkv_a2, v_rwkv_g2, v_rwkv_k_k, v_rwkv_k_a, v_rwkv_r_k, v_rwkv_ln_w, v_rwkv_ln_b, v_w_out, v_norm_x_g, v_norm_mem_g, v_xattn_wq, v_xattn_wk, v_xattn_wv, v_xattn_wo, v_norm_ffn_g, v_ffn_w1, v_ffn_w2, v_final_norm_g):
    given = dict(locals())
    wts = {n: given[n] for n in _WEIGHTS}
    mom_m = {n: given["m_" + n] for n in _WEIGHTS}
    mom_v = {n: given["v_" + n] for n in _WEIGHTS}
    d = D_MODEL
    xt, memt, tgt = x[0], mem[0], loss_target[0]
    tm = 256
    tm_light = 512

    big = {"w_in": jnp.transpose(w_in[0]), "w_out": w_out[0], "xattn_wq": xattn_wq[0], "xattn_wk": xattn_wk[0],
           "xattn_wv": xattn_wv[0], "xattn_wo": xattn_wo[0], "ffn_w1": ffn_w1[0], "ffn_w2": ffn_w2[0]}
    small_sh = {"ssd_conv_w": ssd_conv_w.reshape(4, -1), "rwkv_w2": rwkv_w2[0], "rwkv_a2": rwkv_a2[0],
                "rwkv_g2": rwkv_g2[0]}
    cast_one = lambda n, deps=(): rowwise_fwd("cast_" + n, cast_fn, [big[n]], [], [(big[n].shape[1], BF16)],
                                              256 if big[n].shape[0] % 256 == 0 else big[n].shape[0], deps=deps)[0]
    gather_in, token_in = exchange_start("gather_in_start", [cast_one("w_in")] + list(small_sh.values()), "gather_chips")
    cast = {n: cast_one(n, deps=[token_in]) for n in big if n != "w_in"}
    late_a = ("w_out", "xattn_wq", "xattn_wk", "xattn_wv", "xattn_wo")
    late_b = ("ffn_w1", "ffn_w2")
    gather_a, token_a = exchange_start("gather_attn_start", [cast[n] for n in late_a], "gather_chips", dep=token_in)
    gather_b, token_b = exchange_start("gather_ffn_start", [cast[n] for n in late_b], "gather_chips", dep=token_a)
    (h1,) = rowwise_fwd("norm_mix", rmsnorm_fn, [xt], [norm_mix_g], [(d, BF16)], tm_light, deps=[token_b])
    forward_in, token_in = forward_start("gather_in", gather_in, after=h1)
    gathered = exchange_wait("gather_in_forward_wait", forward_in, after=token_in)
    g_big = {"w_in": gathered[0]}
    g_small = dict(zip(small_sh, gathered[1:]))

    pad_rows = lambda a: jnp.pad(a, ((0, LANES - a.shape[0]), (0, 0)))
    w_in_t = g_big["w_in"].reshape(D_IN, d)
    wt_z, wt_xbc, wt_rkv = (_rows(w_in_t, r) for r in (_Z, _XBC, _RKV))
    wt_ps = jnp.concatenate([_rows(w_in_t, _PG)] + [pad_rows(_rows(w_in_t, r)) for r in (_PW, _PA, _DT)], axis=0)
    unshard_cols = lambda g: jnp.transpose(g, (1, 0, 2)).reshape(g.shape[1], -1)
    conv_w_f = unshard_cols(g_small["ssd_conv_w"])
    w2p, a2p = pad_rows(unshard_cols(g_small["rwkv_w2"])), pad_rows(unshard_cols(g_small["rwkv_a2"]))
    g2_f = unshard_cols(g_small["rwkv_g2"])

    mu = rwkv_mu
    mu_rkv, mu_pg = mu[:, :3072], mu[:, 3264:3520]
    mu_pwa = jnp.concatenate([_pad_lanes(mu[:, 3072:3168]), _pad_lanes(mu[:, 3168:3264])], axis=1)
    dt_bias_p, a_log_p, d_p = _pad_lanes(ssd_dt_bias), _pad_lanes(ssd_a_log), _pad_lanes(ssd_d)
    r_k_row = rwkv_r_k.reshape(1, RWKV_WIDTH)
    g_final = final_norm_g.reshape(1, d)

    u_z = mm("in_z", h1, wt_z, "nt")
    u_xbc = mm("in_xbc", h1, wt_xbc, "nt")
    u_rkv = mm("in_rkv", h1, wt_rkv, "nt")
    u_ps = mm("in_narrow", h1, wt_ps, "nt")

    ssd_pre_rows = lambda: [Rows(u_xbc, shifts=(1, 2, 3)), Rows(u_ps, LANES, 4)]
    ssd_pre_params = [conv_w_f, ssd_conv_b, dt_bias_p]
    xs, bm, cm, dt = rowwise_fwd("ssd_pre", ssd_pre_fn, ssd_pre_rows(), ssd_pre_params,
                                 [(SSD_WIDTH, F32), (256, F32), (256, F32), (LANES, F32)], tm)
    y_scan, ssd_ck = ssd_scan_fwd(xs, bm, cm, dt, a_log_p, d_p)
    (y_ssd,) = rowwise_fwd("ssd_post", ssd_post_fn, [y_scan, u_z], [ssd_norm_g], [(SSD_WIDTH, BF16)], tm_light,
                           into=(None, d, 0))

    rwkv_pre_rows = lambda: [Rows(u_rkv, shifts=(1,)), Rows(u_ps, 2 * LANES, 0, shifts=(1,)), Rows(u_ps, 2 * LANES, 1, shifts=(1,))]
    rwkv_pre_params = [mu_rkv, mu_pg, mu_pwa, rwkv_w0, w2p, rwkv_a0, a2p, g2_f, rwkv_k_k, rwkv_k_a]
    forward_a, token_a = forward_start("gather_attn", gather_a, after=y_scan)
    r_, lw_, k_, v_, kap_, b_, gate_ = rowwise_fwd("rwkv_pre", rwkv_pre_fn, rwkv_pre_rows(), rwkv_pre_params,
                                                   [(RWKV_WIDTH, F32)] * 7, tm, deps=[token_a])
    ys_r, rwkv_ck, rwkv_inv = rwkv_scan_fwd(r_, lw_, k_, v_, kap_, b_)
    forward_b, token_b = forward_start("gather_ffn", gather_b, after=ys_r)
    g_big.update(zip(late_a, exchange_wait("gather_attn_forward_wait", forward_a, after=token_b)))
    w_out_f = g_big["w_out"].reshape(d, d)
    wq_f, wk_f, wv_f, wo_f = (g_big[n].reshape(d, d) for n in ("xattn_wq", "xattn_wk", "xattn_wv", "xattn_wo"))
    rwkv_post_params = [rwkv_ln_w, rwkv_ln_b, r_k_row]
    (ycat,) = rowwise_fwd("rwkv_post", rwkv_post_fn, [ys_r, r_, k_, v_, gate_], rwkv_post_params,
                          [(RWKV_WIDTH, BF16)], tm, into=(y_ssd, d, 1))
    x1 = mm("out_proj", ycat, w_out_f, "nn", res=xt)

    (h2,) = rowwise_fwd("norm_x", rmsnorm_fn, [x1], [norm_x_g], [(d, BF16)], tm_light)
    (mn,) = rowwise_fwd("norm_mem", rmsnorm_fn, [memt], [norm_mem_g], [(d, BF16)], tm)
    q = mm("xattn_q", h2, wq_f, "nn", out_dtype=BF16)
    kx = mm("xattn_k", mn, wk_f, "nn")
    vx = mm("xattn_v", mn, wv_f, "nn")
    (o,) = rowwise_fwd("xattn", attn_fn, [q], [kx, vx], [(d, BF16)], tm_light)
    x2 = mm("xattn_o", o, wo_f, "nn", res=x1)

    (h3,) = rowwise_fwd("norm_ffn", rmsnorm_fn, [x2], [norm_ffn_g], [(d, BF16)], tm_light)
    w1_s, w2_g = exchange_wait("gather_ffn_forward_wait", forward_b, after=h3)
    w2_f = w2_g.reshape(D_FF, d)
    relu2_epi = lambda acc: (jnp.square(jnp.maximum(acc, 0.0)), jnp.maximum(acc, 0.0))
    hid, relu_a = mm("ffn_1", h3, w1_s, "nn", b_slabs=N_DEV, epi=relu2_epi, out_dtypes=[BF16, BF16])
    x3 = mm("ffn_2", hid, w2_f, "nn", res=x2)

    loss_blk, dx3, dx3_b, dg_final = loss_and_grad(x3, tgt, g_final, tm_light)

    grads = {}
    grads["ffn_w2"] = mm("d_ffn_w2", hid, dx3_b, "tn", out_dtype=BF16).reshape(N_DEV, D_FF // N_DEV, d)
    sc_w2, tok = exchange_start("scatter_ffn_w2_start", [grads["ffn_w2"]], "scatter")
    da = mm("d_hid", dx3_b, w2_f, "nt", dep=tok, epi=lambda acc, ra: (2.0 * acc * ra,), extras=[relu_a],
            out_dtypes=[BF16])
    grads["ffn_w1"] = mm("d_ffn_w1", h3, da, "tn", out_dtype=BF16, out_slabs=N_DEV)
    sc_w1, tok = exchange_start("scatter_ffn_w1_start", [grads["ffn_w1"]], "scatter")
    dh3 = mm("d_h3", da, w1_s, "nt", out_dtype=BF16, b_slabs=N_DEV, dep=tok)
    (dx2,), (dg_ffn,) = rowwise_bwd("norm_ffn_bwd", rmsnorm_fn, [x2], [norm_ffn_g], [[dh3]], tm_light, [F32], row_add=[dx3])

    grads["xattn_wo"] = mm("d_wo", o, dx2, "tn", out_dtype=BF16).reshape(N_DEV, d // N_DEV, d)
    sc_wo, tok = exchange_start("scatter_wo_start", [grads["xattn_wo"]], "scatter")
    d_o = mm("d_o", dx2, wo_f, "nt", out_dtype=BF16, dep=tok)
    (dq,), (dkx, dvx) = rowwise_bwd("xattn_bwd", attn_fn, [q], [kx, vx], [[d_o]], tm_light, [BF16])
    grads["xattn_wq"] = mm("d_wq", h2, dq, "tn", out_dtype=BF16).reshape(N_DEV, d // N_DEV, d)
    grads["xattn_wk"] = mm("d_wk", mn, dkx, "tn", out_dtype=BF16).reshape(N_DEV, d // N_DEV, d)
    grads["xattn_wv"] = mm("d_wv", mn, dvx, "tn", out_dtype=BF16).reshape(N_DEV, d // N_DEV, d)
    qkv = ("xattn_wq", "xattn_wk", "xattn_wv")
    sc_qkv, tok = exchange_start("scatter_qkv_start", [grads[n] for n in qkv], "scatter")
    dmn = mm("d_mn_v", dvx, wv_f, "nt", res=mm("d_mn_k", dkx, wk_f, "nt", dep=tok))
    _, (dg_mem,) = rowwise_bwd("norm_mem_bwd", rmsnorm_fn, [memt], [norm_mem_g], [[dmn]], tm, [None])
    dh2 = mm("d_h2", dq, wq_f, "nt", out_dtype=BF16, dep=dg_mem)
    (dx1,), (dg_x,) = rowwise_bwd("norm_x_bwd", rmsnorm_fn, [x1], [norm_x_g], [[dh2]], tm_light, [F32], row_add=[dx2])

    grads["w_out"] = mm("d_w_out", ycat, dx1, "tn", out_dtype=BF16).reshape(N_DEV, d // N_DEV, d)
    sc_wout, tok = exchange_start("scatter_w_out_start", [grads["w_out"]], "scatter")
    d_ycat = mm("d_ycat", dx1, w_out_f, "nt", out_dtype=BF16, dep=tok)

    (d_ys, d_r1, d_k1, d_v1, d_gate), (dln_w, dln_b, dr_k) = rowwise_bwd(
        "rwkv_post_bwd", rwkv_post_fn, [ys_r, r_, k_, v_, gate_], rwkv_post_params,
        [[Rows(d_ycat, RWKV_WIDTH, 1)]], tm, [BF16] * 5)
    d_r2, d_lw, d_k2, d_v2, d_kap, d_b = rwkv_scan_bwd(r_, lw_, k_, v_, kap_, b_, rwkv_ck, rwkv_inv, d_ys)
    (du_rkv, du_pg, du_pwa), rwkv_pg = rowwise_bwd(
        "rwkv_pre_bwd", rwkv_pre_fn, rwkv_pre_rows(), rwkv_pre_params,
        [[d_r1, d_r2], [d_lw], [d_k1, d_k2], [d_v1, d_v2], [d_kap], [d_b], [d_gate]], tm, [BF16] * 3)
    dmu_rkv, dmu_pg, dmu_pwa, dw0, dw2p, da0, da2p, dg2, dk_k, dk_a = rwkv_pg

    (d_yscan, du_z), (dssd_norm_g,) = rowwise_bwd("ssd_post_bwd", ssd_post_fn, [y_scan, u_z], [ssd_norm_g],
                                                  [[Rows(d_ycat, SSD_WIDTH, 0)]], tm_light, [BF16, BF16])
    dxs, dbm, dcm, ddt2, da_log_p, dd_p = ssd_scan_bwd(xs, bm, cm, dt, a_log_p, d_p, ssd_ck, d_yscan)
    (du_xbc, du_dt), (dconv_w, dconv_b, ddt_bias_p) = rowwise_bwd(
        "ssd_pre_bwd", ssd_pre_fn, ssd_pre_rows(), ssd_pre_params,
        [[dxs], [dbm], [dcm], [ddt2[0], ddt2[1]]], tm, [BF16, BF16])
    du_ps = jnp.concatenate([du_pg, du_pwa, du_dt], axis=1)

    dwt_z = mm("d_w_z", du_z, h1, "tn", out_dtype=BF16)
    dwt_xbc = mm("d_w_xbc", du_xbc, h1, "tn", out_dtype=BF16)
    dwt_rkv = mm("d_w_rkv", du_rkv, h1, "tn", out_dtype=BF16)
    dwt_ps = mm("d_w_narrow", du_ps, h1, "tn", out_dtype=BF16)
    dwt_full = jnp.concatenate([dwt_z, dwt_xbc, dwt_ps[512:528], dwt_rkv, dwt_ps[256:352], dwt_ps[384:480], dwt_ps[0:256]],
                               axis=0)
    to_slabs = lambda g: jnp.transpose(g.reshape(g.shape[0], N_DEV, -1), (1, 0, 2))
    grads["w_in"] = dwt_full.reshape(N_DEV, D_IN // N_DEV, d)
    grads["ssd_conv_w"] = to_slabs(dconv_w)
    grads["rwkv_w2"] = to_slabs(dw2p[:96])
    grads["rwkv_a2"] = to_slabs(da2p[:96])
    grads["rwkv_g2"] = to_slabs(dg2)
    tail = ("w_in", "ssd_conv_w", "rwkv_w2", "rwkv_a2", "rwkv_g2")
    sc_tail, tok = exchange_start("scatter_tail_start", [grads[n] for n in tail], "scatter")
    dh1 = mm("d_h1_z", du_z, wt_z, "nn", dep=tok)
    dh1 = mm("d_h1_xbc", du_xbc, wt_xbc, "nn", res=dh1)
    dh1 = mm("d_h1_rkv", du_rkv, wt_rkv, "nn", res=dh1)
    dh1 = mm("d_h1_narrow", du_ps, wt_ps, "nn", res=dh1)
    (dx,), (dg_mix,) = rowwise_bwd("norm_mix_bwd", rmsnorm_fn, [xt], [norm_mix_g], [[dh1]], tm_light, [F32], row_add=[dx1])

    dmu =jnp.concatenate([dmu_rkv, dmu_pwa[:, 0:96], dmu_pwa[:, 128:224], dmu_pg], axis=1)
    small_grads = {
        "norm_mix_g": dg_mix, "ssd_conv_b": dconv_b, "ssd_dt_bias": ddt_bias_p[:, :16], "ssd_a_log": da_log_p[:, :16],
        "ssd_d": dd_p[:, :16], "ssd_norm_g": dssd_norm_g, "rwkv_mu": dmu, "rwkv_w0": dw0, "rwkv_a0": da0,
        "rwkv_k_k": dk_k, "rwkv_k_a": dk_a, "rwkv_r_k": dr_k, "rwkv_ln_w": dln_w, "rwkv_ln_b": dln_b,
        "norm_x_g": dg_x, "norm_mem_g": dg_mem, "norm_ffn_g": dg_ffn, "final_norm_g": dg_final}

    gather_small, tok = exchange_start("gather_small_start", [_pack_small(small_grads)], "gather")
    received = {}
    for names, handle in ((("ffn_w2",), sc_w2), (("ffn_w1",), sc_w1), (("xattn_wo",), sc_wo), (qkv, sc_qkv),
                          (("w_out",), sc_wout)):
        received.update(zip(names, exchange_wait("scatter_" + names[0] + "_wait", handle, after=tok)))

    out_g, out_d, out_m, out_v = {}, {}, {}, {}

    def run_adamw(n, dep):
        shape = wts[n].shape
        two_d = lambda a: a.reshape(-1, shape[-1])
        if n == "w_in":
            recv = jnp.transpose(sum_slabs("sum_w_in", received[n]))[None]
        else:
            recv = received[n].reshape(N_DEV, -1, shape[-1])
        res = adamw("adamw_" + n, recv, two_d(wts[n]), two_d(mom_m[n]), two_d(mom_v[n]), dep=dep)
        out_g[n], out_d[n], out_m[n], out_v[n] = (r.reshape(shape) for r in res)
        return res[0]

    last = None
    for n in ("ffn_w2", "ffn_w1", "xattn_wo") + qkv + ("w_out",):
        last = run_adamw(n, last)
    received.update(zip(tail, exchange_wait("scatter_tail_wait", sc_tail, after=last)))
    for n in tail:
        last = run_adamw(n, last)
    (small_all,) = exchange_wait("gather_small_wait", gather_small, after=last)
    res = adamw("adamw_small", small_all, _pack_small(wts), _pack_small(mom_m), _pack_small(mom_v))
    shapes = {n: wts[n].shape for n in _SMALL}
    for dst, packed in zip((out_g, out_d, out_m, out_v), res):
        dst.update(_unpack_small(packed, shapes))

    loss = lax.psum(loss_blk[0, 0], ("x", "y", "c"))
    return (loss, dx[None], *[out_g[n] for n in _WEIGHTS], *[out_d[n] for n in _WEIGHTS],
            *[out_m[n] for n in _WEIGHTS], *[out_v[n] for n in _WEIGHTS])
```

```python
import functools

import jax
import jax.numpy as jnp
from jax import lax
from jax.experimental import pallas as pl
from jax.experimental.pallas import tpu as pltpu

F32 = jnp.float32
BF16 = jnp.bfloat16

N_DEV = 8
D_MODEL = 2048
NORM_EPS = 1e-6
SSD_WIDTH = 1024
SSD_HEAD_DIM = 64
SSD_STATE = 128
SSD_CHUNK = 128
SSD_HEADS_PER_GROUP = 8
RWKV_WIDTH = 1024
RWKV_HEADS = 16
RWKV_HEAD_DIM = 64
RWKV_LN_EPS = 64e-5
RWKV_CHUNK = 128
RWKV_HEADS_PER_STEP = 16
XATTN_HEADS = 4
XATTN_HEAD_DIM = 512
D_FF = 8192
LANES = 128
SUBLANES = 8
VMEM_LIMIT = 56 * 1024 * 1024

ADAM_LR = 0.001
ADAM_B1 = 0.9
ADAM_B2 = 0.999
ADAM_EPS = 1e-08
ADAM_WD = 0.01
ADAM_STEP = 10

_DN = {"nn": ((1,), (0,)), "nt": ((1,), (1,)), "tn": ((0,), (0,))}


def _dg(a, b, mode):
    (ca,), (cb,) = _DN[mode]
    dn = (((ca + 1,), (cb + 1,)), ((0,), (0,))) if a.ndim == 3 else (((ca,), (cb,)), ((), ()))
    return lax.dot_general(a, b, dn, preferred_element_type=F32)


@functools.partial(jax.custom_vjp, nondiff_argnums=(2,))
def bdot(a, b, mode):
    return _dg(a.astype(BF16), b.astype(BF16), mode)


def _bdot_fwd(a, b, mode):
    return bdot(a, b, mode), (a, b)


def _bdot_bwd(mode, res, g):
    a, b = res
    ab, bb, gb = a.astype(BF16), b.astype(BF16), g.astype(BF16)
    if mode == "nn":
        da, db = _dg(gb, bb, "nt"), _dg(ab, gb, "tn")
    elif mode == "nt":
        da, db = _dg(gb, bb, "nn"), _dg(gb, ab, "tn")
    else:
        da, db = _dg(bb, gb, "nt"), _dg(ab, gb, "nn")
    return da.astype(a.dtype), db.astype(b.dtype)


bdot.defvjp(_bdot_fwd, _bdot_bwd)


def _split2(x):
    hi = x.astype(BF16)
    return hi, (x - hi.astype(F32)).astype(BF16)


def _dot01(x, m01):
    hi, lo = _split2(x)
    return _dg(hi, m01, "nn") + _dg(lo, m01, "nn")


def _exact_dot_impl(a, b, mode, exact):
    if exact == "a":
        ae = a.astype(BF16)
        return sum(_dg(ae, part, mode) for part in _split2(b))
    be = b.astype(BF16)
    return sum(_dg(part, be, mode) for part in _split2(a))


@functools.partial(jax.custom_vjp, nondiff_argnums=(2, 3))
def exact_dot(a, b, mode, exact):
    return _exact_dot_impl(a, b, mode, exact)


def _exact_dot_fwd(a, b, mode, exact):
    return _exact_dot_impl(a, b, mode, exact), (a, b)


def _exact_dot_bwd(mode, exact, res, g):
    a, b = res
    if exact == "a":
        db = {"nn": lambda: _exact_dot_impl(a, g, "tn", "a"), "nt": lambda: _exact_dot_impl(g, a, "tn", "b"),
              "tn": lambda: _exact_dot_impl(a, g, "nn", "a")}[mode]()
        return jnp.zeros_like(a), db
    da = {"nn": lambda: _exact_dot_impl(g, b, "nt", "b"), "nt": lambda: _exact_dot_impl(g, b, "nn", "b"),
          "tn": lambda: _exact_dot_impl(b, g, "nt", "a")}[mode]()
    return da, jnp.zeros_like(b)


exact_dot.defvjp(_exact_dot_fwd, _exact_dot_bwd)


def _head_indicator(width, heads, transpose):
    hd = width // heads
    shape = (LANES, width) if transpose else (width, LANES)
    lane = lax.broadcasted_iota(jnp.int32, shape, 1 if not transpose else 0)
    pos = lax.broadcasted_iota(jnp.int32, shape, 0 if not transpose else 1)
    return ((pos >= lane * hd) & (pos < lane * hd + hd)).astype(BF16)


@jax.custom_vjp
def head_sum(x):
    w = x.shape[-1]
    e = _head_indicator(w, w // RWKV_HEAD_DIM, False)
    et = _head_indicator(w, w // RWKV_HEAD_DIM, True)
    return _dot01(_dot01(x, e), et)


head_sum.defvjp(lambda x: (head_sum(x), None), lambda _, g: (head_sum(g),))


def rmsnorm_fn(x, g):
    y = x * lax.rsqrt(jnp.mean(x * x, axis=-1, keepdims=True) + NORM_EPS)
    return ((y * g).astype(BF16),)


def cast_fn(x):
    return (x.astype(BF16),)


def ssd_pre_fn(xbc, xbc1, xbc2, xbc3, dt_raw, conv_w, conv_b, dt_bias):
    c = conv_w[3:4] * xbc + conv_w[2:3] * xbc1 + conv_w[1:2] * xbc2 + conv_w[0:1] * xbc3 + conv_b
    act = c * jax.nn.sigmoid(c)
    dt = jax.nn.softplus(dt_raw + dt_bias)
    return act[:, :SSD_WIDTH], act[:, SSD_WIDTH:SSD_WIDTH + 256], act[:, SSD_WIDTH + 256:], dt


def ssd_post_fn(yscan, z, norm_g):
    y = yscan * (z * jax.nn.sigmoid(z))
    half = SSD_WIDTH // 2
    parts = []
    for g in range(2):
        yg = y[:, g * half:(g + 1) * half]
        parts.append(yg * lax.rsqrt(jnp.mean(yg * yg, axis=-1, keepdims=True) + NORM_EPS))
    return ((jnp.concatenate(parts, axis=-1) * norm_g).astype(BF16),)


def rwkv_pre_fn(rkv, rkv_p, pg, pg_p, pwa, pwa_p, mu_rkv, mu_pg, mu_pwa, w0, w2p, a0, a2p, g2, k_k, k_a):
    w = RWKV_WIDTH
    rkv = rkv + (rkv_p - rkv) * mu_rkv
    pg = pg + (pg_p - pg) * mu_pg
    pwa = pwa + (pwa_p - pwa) * mu_pwa
    r, k, v = rkv[:, :w], rkv[:, w:2 * w], rkv[:, 2 * w:]
    pw, pa = pwa[:, :LANES], pwa[:, LANES:]
    w_log = -jax.nn.softplus(-(w0 + bdot(jnp.tanh(pw), w2p, "nn"))) - 0.5
    lw = -jnp.exp(w_log)
    iclr = jax.nn.sigmoid(a0 + bdot(pa, a2p, "nn"))
    gate = bdot(jax.nn.sigmoid(pg), g2, "nn")
    kk = k * k_k
    kap = kk * lax.rsqrt(jnp.maximum(head_sum(kk * kk), 1e-24))
    k_mod = k * (1.0 + (iclr - 1.0) * k_a)
    return r, lw, k_mod, v, kap, kap * iclr, gate


def rwkv_post_fn(ys, r, k_mod, v, gate, ln_w, ln_b, r_k):
    inv_n = 1.0 / RWKV_HEAD_DIM
    mean = head_sum(ys) * inv_n
    yc = ys - mean
    var = head_sum(yc * yc) * inv_n
    yn = yc * lax.rsqrt(var + RWKV_LN_EPS) * ln_w + ln_b
    bonus = head_sum(r * k_mod * r_k) * v
    return (((yn + bonus) * gate).astype(BF16),)


def attn_fn(q, kx, vx):
    outs = []
    for h in range(XATTN_HEADS):
        sl = slice(h * XATTN_HEAD_DIM, (h + 1) * XATTN_HEAD_DIM)
        s = bdot(q[:, sl], kx[:, sl], "nt") * (XATTN_HEAD_DIM ** -0.5)
        s = s - jnp.max(s, axis=-1, keepdims=True)
        p = jnp.exp(s)
        p = p / jnp.sum(p, axis=-1, keepdims=True)
        outs.append(bdot(p, vx[:, sl], "nn"))
    return (jnp.concatenate(outs, axis=-1).astype(BF16),)


def loss_fn(x, tgt, g):
    y = x * lax.rsqrt(jnp.mean(x * x, axis=-1, keepdims=True) + NORM_EPS) * g
    err = jnp.square(y - tgt)
    return 0.5 * jnp.sum(jnp.mean(err, axis=-1, keepdims=True), axis=0, keepdims=True)


def _tri_masks(n):
    row = lax.broadcasted_iota(jnp.int32, (n, n), 0)
    col = lax.broadcasted_iota(jnp.int32, (n, n), 1)
    return col <= row, col < row, row == col


@jax.custom_vjp
def unit_lower_inverse(a):
    c = a.shape[-1]
    eye = _tri_masks(c)[2].astype(F32)
    m = -a
    inv = eye + m
    n = 1
    while n * 2 < c:
        m = bdot(m, m, "nn")
        inv = bdot(inv, eye + m, "nn")
        n *= 2
    return inv


def _unit_lower_inverse_fwd(a):
    inv = unit_lower_inverse(a)
    return inv, inv


def _unit_lower_inverse_bwd(inv, g):
    return (-bdot(bdot(inv, g, "tn"), inv, "nt"),)


unit_lower_inverse.defvjp(_unit_lower_inverse_fwd, _unit_lower_inverse_bwd)


@jax.custom_vjp
def known_inverse(a, inv):
    return inv


known_inverse.defvjp(lambda a, inv: (inv, inv),
                     lambda inv, g: (_unit_lower_inverse_bwd(inv, g)[0], jnp.zeros_like(inv)))


def rwkv_chunk_fn(st0, r, lw, k, v, kap, b, inv=None):
    h, c = r.shape[0], r.shape[1]
    incl, strict, _ = _tri_masks(c)
    cum = exact_dot(jnp.broadcast_to(incl.astype(F32), (h, c, c)), lw, "nn", "a")
    g_in = jnp.exp(cum)
    g_prev = jnp.exp(cum - lw)
    g_inv = jnp.exp(-cum)
    g_end = jnp.exp(cum[:, c - 1:c, :] - cum)
    kap_t, k_t, b_t, r_t = kap * g_prev, k * g_inv, b * g_inv, r * g_in
    a_ub = jnp.where(strict, bdot(kap_t, b_t, "nt"), 0.0)
    a_vk = jnp.where(strict, bdot(kap_t, k_t, "nt"), 0.0)
    rhs = -(bdot(kap_t, st0, "nt") + bdot(a_vk, v, "nn"))
    inv = unit_lower_inverse(a_ub) if inv is None else known_inverse(a_ub, inv)
    u = bdot(inv, rhs, "nn")
    y = (bdot(r_t, st0, "nt")
         + bdot(jnp.where(incl, bdot(r_t, k_t, "nt"), 0.0), v, "nn")
         + bdot(jnp.where(incl, bdot(r_t, b_t, "nt"), 0.0), u, "nn"))
    st1 = jnp.exp(cum[:, c - 1:c, :]) * st0 + bdot(v, k * g_end, "tn") + bdot(u, b * g_end, "tn")
    return y, st1, inv


def ssd_chunk_fn(group, h0, xs, bm, cm, dt, a_log, d_skip):
    q, nh = xs.shape[0], SSD_HEADS_PER_GROUP
    causal, _, _ = _tri_masks(q)
    a_row = -jnp.exp(a_log)
    cs_all = exact_dot(causal.astype(F32), dt * a_row, "nn", "a")
    cs_t = cs_all.T
    lanes = range(group * nh, (group + 1) * nh)
    cs = jnp.stack([cs_all[:, hl:hl + 1] for hl in lanes])
    cs_row = jnp.stack([cs_t[hl:hl + 1, :] for hl in lanes])
    dt_h = jnp.stack([dt[:, hl:hl + 1] for hl in lanes])
    d_h = jnp.stack([d_skip[:, hl:hl + 1] for hl in lanes])
    x = _stack_lanes(xs, nh)
    h0s = _stack_rows(h0, nh)
    lmat = jnp.where(causal, jnp.exp(jnp.where(causal, cs - cs_row, 0.0)), 0.0)
    cb = bdot(cm, bm, "nt")
    xdt = x * dt_h
    cl = cs[:, q - 1:q, :]
    cm_b = jnp.broadcast_to(cm, (nh,) + cm.shape)
    bm_b = jnp.broadcast_to(bm, (nh,) + bm.shape)
    y = bdot(cb * lmat, xdt, "nn") + bdot(cm_b, h0s, "nt") * jnp.exp(cs) + x * d_h
    h1 = h0s * jnp.exp(cl) + bdot(xdt * jnp.exp(cl - cs), bm_b, "tn")
    return jnp.concatenate([y[e] for e in range(nh)], axis=-1), jnp.concatenate([h1[e] for e in range(nh)], axis=0)


class Rows:
    def __init__(self, arr, w=None, cb=0, shifts=()):
        self.arr, self.w, self.cb, self.shifts = arr, (arr.shape[1] if w is None else w), cb, tuple(shifts)


def _as_rows(x):
    return x if isinstance(x, Rows) else Rows(x)


def _shift_down(x, halo, k):
    rolled = pltpu.roll(x, k, 0)
    first = rolled[0:SUBLANES]
    rid = lax.broadcasted_iota(jnp.int32, first.shape, 0)
    patched = jnp.where(rid < k, pltpu.roll(halo, k, 0), first)
    return jnp.concatenate([patched, rolled[SUBLANES:]], axis=0)


def _shift_up(g, carry, k):
    tm = g.shape[0]
    rolled = pltpu.roll(g, tm - k, 0)
    last = rolled[tm - SUBLANES:]
    rid = lax.broadcasted_iota(jnp.int32, last.shape, 0)
    patched = jnp.where(rid >= SUBLANES - k, pltpu.roll(carry, SUBLANES - k, 0), last)
    return jnp.concatenate([rolled[:tm - SUBLANES], patched], axis=0)


def _params():
    return pltpu.CompilerParams(vmem_limit_bytes=VMEM_LIMIT)


def _load_rows(refs, pos, rins, first_block):
    vals = []
    for r in rins:
        x = refs[pos][...].astype(F32) if refs[pos].dtype != F32 else refs[pos][...]
        pos += 1
        vals.append(x)
        if r.shifts:
            halo = refs[pos][...]
            pos += 1
            halo = jnp.where(first_block, jnp.zeros_like(halo), halo)
            for k in r.shifts:
                vals.append(_shift_down(x, halo, k))
    return vals, pos


def _row_specs(rins, tm, blk):
    specs, args = [], []
    for r in rins:
        specs.append(pl.BlockSpec((tm, r.w), lambda i, cb=r.cb: (blk(i), cb)))
        args.append(r.arr)
        if r.shifts:
            per = tm // SUBLANES
            specs.append(pl.BlockSpec((SUBLANES, r.w), lambda i, cb=r.cb: (jnp.maximum(blk(i) * per - 1, 0), cb)))
            args.append(r.arr)
    return specs, args


def rowwise_fwd(name, fn, rins, params, outs, tm, deps=(), into=None):
    rins = [_as_rows(r) for r in rins]
    t = rins[0].arr.shape[0]
    tm = min(tm, t)
    nb = t // tm
    specs, args = _row_specs(rins, tm, lambda i: i)
    for p in params:
        specs.append(pl.BlockSpec(p.shape, lambda i: (0, 0)))
        args.append(p)
    for dep in deps:
        specs.append(pl.BlockSpec(memory_space=pl.ANY))
        args.append(dep)
    out_specs = [pl.BlockSpec((tm, w), lambda i: (i, 0)) for w, _ in outs]
    out_shape = [jax.ShapeDtypeStruct((t, w), dt) for w, dt in outs]
    aliases = {}
    if into is not None:
        target, total_width, col_block = into
        out_specs = [pl.BlockSpec((tm, outs[0][0]), lambda i: (i, col_block))]
        out_shape = [jax.ShapeDtypeStruct((t, total_width), outs[0][1])]
        if target is not None:
            aliases = {len(args): 0}
            specs.append(pl.BlockSpec(memory_space=pl.ANY))
            args.append(target)
    n_in = len(args)

    def body(*refs):
        vals, pos = _load_rows(refs, 0, rins, pl.program_id(0) == 0)
        pv = [refs[pos + j][...] for j in range(len(params))]
        res = fn(*vals, *pv)
        for o_ref, o in zip(refs[n_in:], res):
            o_ref[...] = o.astype(o_ref.dtype)

    return pl.pallas_call(
        body, name=name, grid=(nb,), in_specs=specs, out_specs=out_specs, out_shape=out_shape,
        input_output_aliases=aliases, compiler_params=_params(),
    )(*args)


def rowwise_bwd(name, fn, rins, params, cts, tm, grad_dtypes, row_add=None):
    rins = [_as_rows(r) for r in rins]
    cts = [[_as_rows(c) for c in lst] for lst in cts]
    row_add = [_as_rows(a) for a in (row_add or [])]
    t = rins[0].arr.shape[0]
    tm = min(tm, t)
    nb = t // tm
    rev = lambda i: nb - 1 - i
    specs, args = _row_specs(rins, tm, rev)
    for p in params:
        specs.append(pl.BlockSpec(p.shape, lambda i: (0, 0)))
        args.append(p)
    flat_cts = [c for lst in cts for c in lst] + row_add
    for c in flat_cts:
        specs.append(pl.BlockSpec((tm, c.w), lambda i, cb=c.cb: (rev(i), cb)))
        args.append(c.arr)
    n_in = len(args)
    want = [i for i, d in enumerate(grad_dtypes) if d is not None]
    out_specs = [pl.BlockSpec((tm, rins[i].w), lambda i_: (rev(i_), 0)) for i in want]
    out_shape = [jax.ShapeDtypeStruct((t, rins[i].w), grad_dtypes[i]) for i in want]
    out_specs += [pl.BlockSpec(p.shape, lambda i: (0, 0)) for p in params]
    out_shape += [jax.ShapeDtypeStruct(p.shape, F32) for p in params]
    n_out = len(out_shape)
    scratch = [pltpu.VMEM((SUBLANES, r.w), F32) for r in rins for _ in r.shifts]

    def body(*refs):
        i = pl.program_id(0)
        vals, pos = _load_rows(refs, 0, rins, rev(i) == 0)
        pv = [refs[pos + j][...] for j in range(len(params))]
        pos += len(params)
        outs, vjp = jax.vjp(fn, *vals, *pv)
        ct_vals = []
        for o, lst in zip(outs, cts):
            acc = None
            for _ in lst:
                cv = refs[pos][...].astype(F32)
                pos += 1
                acc = cv if acc is None else acc + cv
            ct_vals.append(acc.astype(o.dtype))
        adds = [refs[pos + j][...].astype(F32) for j in range(len(row_add))]
        grads = vjp(tuple(ct_vals))
        out_refs = refs[n_in:n_in + n_out]
        carry_refs = refs[n_in + n_out:]

        @pl.when(i == 0)
        def _():
            for cr in carry_refs:
                cr[...] = jnp.zeros_like(cr)
            for pr in out_refs[len(want):]:
                pr[...] = jnp.zeros_like(pr)

        gi, ci, oi = 0, 0, 0
        for idx, r in enumerate(rins):
            d = grads[gi]
            gi += 1
            for k in r.shifts:
                dk = grads[gi]
                gi += 1
                d = d + _shift_up(dk, carry_refs[ci][...], k)
                carry_refs[ci][...] = dk[0:SUBLANES]
                ci += 1
            if idx == 0:
                for a in adds:
                    d = d + a
            if grad_dtypes[idx] is not None:
                out_refs[oi][...] = d.astype(out_refs[oi].dtype)
                oi += 1
        for pr, gp in zip(out_refs[len(want):], grads[gi:]):
            pr[...] += gp

    res = pl.pallas_call(
        body, name=name, grid=(nb,), in_specs=specs, out_specs=out_specs, out_shape=out_shape,
        scratch_shapes=scratch, compiler_params=_params(),
    )(*args)
    return res[:len(want)], res[len(want):]


def _pick(n, pref):
    for c in pref:
        if n % c == 0:
            return c
    return n


MM_VMEM_BUDGET = 40 * 1024 * 1024
MM_PEAK_FLOPS = 0.9e15
MM_HBM_BYTES_PER_S = 3.0e12
MM_STEP_SECONDS = 0.35e-6


def _mm_tiles(m, n, k, size_a, size_b, size_out, size_res, single_k):
    best = None
    for tk in sorted({c for c in (k, 2048, 1024, 512, 256, 128) if c <= 2048 and k % c == 0}, reverse=True):
        for tm in sorted({c for c in (m, 1024, 512, 256, 128) if c <= 1024 and m % c == 0}, reverse=True):
            for tn in sorted({c for c in (n, 2048, 1536, 1024, 768, 512, 384, 256, 128) if c <= 2048 and n % c == 0},
                             reverse=True):
                nk = k // tk
                vmem = 2 * (tm * tk * size_a + tk * tn * size_b + tm * tn * (size_out + size_res))
                vmem += tm * tn * 4 * (2 if nk > 1 or not single_k else 1)
                vmem += (tm * tk * 2 if size_a > 2 else 0) + (tk * tn * 2 if size_b > 2 else 0)
                if vmem > MM_VMEM_BUDGET:
                    continue
                steps = (m // tm) * (n // tn) * nk
                a_reads = 1 if (nk == 1 and single_k) else n // tn
                traffic = m * k * size_a * a_reads + k * n * size_b * (m // tm) + m * n * (size_out + size_res)
                cost = max(2.0 * m * n * k / MM_PEAK_FLOPS, traffic / MM_HBM_BYTES_PER_S) + steps * MM_STEP_SECONDS
                if best is None or cost < best[0]:
                    best = (cost, tm, tn, tk)
    return best[1:]


def mm(name, a, b, mode, out_dtype=F32, res=None, b_slabs=None, out_slabs=None, dep=None, epi=None, extras=(),
       out_dtypes=None):
    if mode == "tn":
        k_dim, m_dim = a.shape
    else:
        m_dim, k_dim = a.shape
    if b_slabs:
        n_dim = b.shape[0] * b.shape[2] if mode == "nn" else b.shape[1]
    else:
        n_dim = b.shape[0] if mode == "nt" else b.shape[1]
    n_slabs = out_slabs or (b_slabs if (b_slabs and mode == "nn") else 1)
    k_slabs = b_slabs if (b_slabs and mode == "nt") else 1
    if epi is None:
        out_dtypes = [out_dtype]
        if res is None:
            epi = lambda acc: (acc,)
        else:
            extras, epi = [res], lambda acc, r: (acc + r,)
    tm, tn, tk = _mm_tiles(m_dim, n_dim // n_slabs, k_dim // k_slabs, a.dtype.itemsize, b.dtype.itemsize,
                           sum(jnp.dtype(dt).itemsize for dt in out_dtypes), sum(e.dtype.itemsize for e in extras),
                           single_k=(k_slabs == 1))
    nji = n_dim // n_slabs // tn
    nki = k_dim // k_slabs // tk
    nblk = lambda js, j: js * nji + j
    kblk = lambda ks, k: ks * nki + k
    if mode == "tn":
        a_spec = pl.BlockSpec((tk, tm), lambda i, js, j, ks, k: (kblk(ks, k), i))
    else:
        a_spec = pl.BlockSpec((tm, tk), lambda i, js, j, ks, k: (i, kblk(ks, k)))
    if b_slabs and mode == "nn":
        b_spec = pl.BlockSpec((None, tk, tn), lambda i, js, j, ks, k: (js, k, j))
    elif b_slabs and mode == "nt":
        b_spec = pl.BlockSpec((None, tn, tk), lambda i, js, j, ks, k: (ks, nblk(js, j), k))
    elif mode == "nt":
        b_spec = pl.BlockSpec((tn, tk), lambda i, js, j, ks, k: (nblk(js, j), kblk(ks, k)))
    else:
        b_spec = pl.BlockSpec((tk, tn), lambda i, js, j, ks, k: (kblk(ks, k), nblk(js, j)))
    specs, args = [a_spec, b_spec], [a, b]
    for e in extras:
        specs.append(pl.BlockSpec((tm, tn), lambda i, js, j, ks, k: (i, nblk(js, j))))
        args.append(e)
    if dep is not None:
        specs.append(pl.BlockSpec(memory_space=pl.ANY))
        args.append(dep)
    if out_slabs:
        o_specs = [pl.BlockSpec((None, tm, tn), lambda i, js, j, ks, k: (js, i, j))]
        o_shapes = [jax.ShapeDtypeStruct((out_slabs, m_dim, n_dim // out_slabs), out_dtypes[0])]
    else:
        o_specs = [pl.BlockSpec((tm, tn), lambda i, js, j, ks, k: (i, nblk(js, j))) for _ in out_dtypes]
        o_shapes = [jax.ShapeDtypeStruct((m_dim, n_dim), dt) for dt in out_dtypes]

    one_k_step = k_slabs * nki == 1
    n_in, n_out = len(args), len(out_dtypes)

    def body(*refs):
        a_ref, b_ref = refs[0], refs[1]
        part = _dg(a_ref[...].astype(BF16), b_ref[...].astype(BF16), mode)

        def finish(acc):
            outs = epi(acc, *[refs[2 + j][...].astype(F32) for j in range(len(extras))])
            for o_ref, o in zip(refs[n_in:n_in + n_out], outs):
                o_ref[...] = o.astype(o_ref.dtype)

        if one_k_step:
            finish(part)
            return
        acc_ref = refs[n_in + n_out]
        ks, kk = pl.program_id(3), pl.program_id(4)

        @pl.when((ks == 0) & (kk == 0))
        def _():
            acc_ref[...] = part

        @pl.when((ks > 0) | (kk > 0))
        def _():
            acc_ref[...] += part

        pl.when((ks == k_slabs - 1) & (kk == nki - 1))(lambda: finish(acc_ref[...]))

    grid = (m_dim // tm, n_slabs, nji, k_slabs, nki)
    scratch = [] if one_k_step else [pltpu.VMEM((tm, tn), F32)]
    out = pl.pallas_call(
        body, name=name, grid=grid, in_specs=specs, out_specs=o_specs, out_shape=o_shapes, scratch_shapes=scratch,
        compiler_params=pltpu.CompilerParams(
            dimension_semantics=("parallel", "parallel", "parallel", "arbitrary", "arbitrary"),
            vmem_limit_bytes=VMEM_LIMIT),
    )(*args)
    return out[0] if n_out == 1 else out


def _stack_lanes(x, n):
    w = x.shape[1] // n
    return jnp.stack([x[:, i * w:(i + 1) * w] for i in range(n)])


def _stack_rows(x, n):
    w = x.shape[0] // n
    return jnp.stack([x[i * w:(i + 1) * w, :] for i in range(n)])


def rwkv_scan_fwd(r, lw, k, v, kap, b):
    t = r.shape[0]
    c, hps, hd = min(RWKV_CHUNK, t), RWKV_HEADS_PER_STEP, RWKV_HEAD_DIM
    nc, ng, wl = t // c, RWKV_HEADS // hps, hps * hd
    spec = pl.BlockSpec((c, wl), lambda g, ci: (ci, g))

    def body(r_ref, lw_ref, k_ref, v_ref, kap_ref, b_ref, y_ref, ck_ref, inv_ref, st_ref):
        @pl.when(pl.program_id(1) == 0)
        def _():
            st_ref[...] = jnp.zeros_like(st_ref)

        st = st_ref[...]
        ck_ref[...] = st
        ins = [x[...] for x in (r_ref, lw_ref, k_ref, v_ref, kap_ref, b_ref)]
        y, st1, inv = rwkv_chunk_fn(_stack_rows(st, hps), *[_stack_lanes(x, hps) for x in ins])
        y_ref[...] = jnp.concatenate([y[h] for h in range(hps)], axis=-1)
        st_ref[...] = jnp.concatenate([st1[h] for h in range(hps)], axis=0)
        inv_ref[...] = jnp.concatenate([inv[h] for h in range(hps)], axis=0)

    return pl.pallas_call(
        body, name="rwkv_scan_fwd", grid=(ng, nc), in_specs=[spec] * 6,
        out_specs=[spec, pl.BlockSpec((None, wl, hd), lambda g, ci: (ci, g, 0)),
                   pl.BlockSpec((None, hps * c, c), lambda g, ci: (ci, g, 0))],
        out_shape=[jax.ShapeDtypeStruct((t, RWKV_WIDTH), F32), jax.ShapeDtypeStruct((nc, RWKV_WIDTH, hd), F32),
                   jax.ShapeDtypeStruct((nc, RWKV_HEADS * c, c), F32)],
        scratch_shapes=[pltpu.VMEM((wl, hd), F32)], compiler_params=_params(),
    )(r, lw, k, v, kap, b)


def rwkv_scan_bwd(r, lw, k, v, kap, b, ck, inv_ck, dy):
    t = r.shape[0]
    c, hps, hd = min(RWKV_CHUNK, t), RWKV_HEADS_PER_STEP, RWKV_HEAD_DIM
    nc, ng, wl = t // c, RWKV_HEADS // hps, hps * hd
    spec = pl.BlockSpec((c, wl), lambda g, ci: (nc - 1 - ci, g))

    def body(r_ref, lw_ref, k_ref, v_ref, kap_ref, b_ref, ck_ref, inv_ref, dy_ref, *rest):
        out_refs, dst_ref = rest[:6], rest[6]

        @pl.when(pl.program_id(1) == 0)
        def _():
            dst_ref[...] = jnp.zeros_like(dst_ref)

        ins = [x[...] for x in (r_ref, lw_ref, k_ref, v_ref, kap_ref, b_ref)]
        dyv, ck, dst = dy_ref[...].astype(F32), ck_ref[...], dst_ref[...]
        chunk = lambda *a: rwkv_chunk_fn(*a, inv=_stack_rows(inv_ref[...], hps))[:2]
        _, vjp = jax.vjp(chunk, _stack_rows(ck, hps), *[_stack_lanes(x, hps) for x in ins])
        grads = vjp((_stack_lanes(dyv, hps), _stack_rows(dst, hps)))
        dst_ref[...] = jnp.concatenate([grads[0][h] for h in range(hps)], axis=0)
        for j in range(6):
            out_refs[j][...] = jnp.concatenate([grads[1 + j][h] for h in range(hps)], axis=-1).astype(BF16)

    return pl.pallas_call(
        body, name="rwkv_scan_bwd", grid=(ng, nc),
        in_specs=[spec] * 6 + [pl.BlockSpec((None, wl, hd), lambda g, ci: (nc - 1 - ci, g, 0)),
                               pl.BlockSpec((None, hps * c, c), lambda g, ci: (nc - 1 - ci, g, 0)), spec],
        out_specs=[spec] * 6, out_shape=[jax.ShapeDtypeStruct((t, RWKV_WIDTH), BF16)] * 6,
        scratch_shapes=[pltpu.VMEM((wl, hd), F32)], compiler_params=_params(),
    )(r, lw, k, v, kap, b, ck, inv_ck, dy)


def _ssd_specs(q, blk):
    gw = SSD_WIDTH // 2
    return [pl.BlockSpec((q, gw), lambda g, ci: (blk(ci), g)),
            pl.BlockSpec((q, SSD_STATE), lambda g, ci: (blk(ci), g)),
            pl.BlockSpec((q, SSD_STATE), lambda g, ci: (blk(ci), g)),
            pl.BlockSpec((q, LANES), lambda g, ci: (blk(ci), 0)),
            pl.BlockSpec((1, LANES), lambda g, ci: (0, 0)),
            pl.BlockSpec((1, LANES), lambda g, ci: (0, 0))]


def ssd_scan_fwd(xs, bm, cm, dt, a_log, d_skip):
    t = xs.shape[0]
    q = min(SSD_CHUNK, t)
    nc, gw = t // q, SSD_WIDTH // 2

    def body(xs_ref, bm_ref, cm_ref, dt_ref, al_ref, d_ref, y_ref, ck_ref, h_ref):
        @pl.when(pl.program_id(1) == 0)
        def _():
            h_ref[...] = jnp.zeros_like(h_ref)

        ck_ref[...] = h_ref[...]
        args = (h_ref[...], xs_ref[...], bm_ref[...], cm_ref[...], dt_ref[...], al_ref[...], d_ref[...])
        g = pl.program_id(0)

        @pl.when(g == 0)
        def _():
            y, h1 = ssd_chunk_fn(0, *args)
            y_ref[...] = y
            h_ref[...] = h1

        @pl.when(g == 1)
        def _():
            y, h1 = ssd_chunk_fn(1, *args)
            y_ref[...] = y
            h_ref[...] = h1

    return pl.pallas_call(
        body, name="ssd_scan_fwd", grid=(2, nc), in_specs=_ssd_specs(q, lambda ci: ci),
        out_specs=[pl.BlockSpec((q, gw), lambda g, ci: (ci, g)),
                   pl.BlockSpec((None, gw, SSD_STATE), lambda g, ci: (ci, g, 0))],
        out_shape=[jax.ShapeDtypeStruct((t, SSD_WIDTH), F32), jax.ShapeDtypeStruct((nc, SSD_WIDTH, SSD_STATE), F32)],
        scratch_shapes=[pltpu.VMEM((gw, SSD_STATE), F32)], compiler_params=_params(),
    )(xs, bm, cm, dt, a_log, d_skip)


def ssd_scan_bwd(xs, bm, cm, dt, a_log, d_skip, ck, dy):
    t = xs.shape[0]
    q = min(SSD_CHUNK, t)
    nc, gw = t // q, SSD_WIDTH // 2
    rev = lambda ci: nc - 1 - ci

    def body(xs_ref, bm_ref, cm_ref, dt_ref, al_ref, d_ref, ck_ref, dy_ref,
             dxs_ref, dbm_ref, dcm_ref, ddt_ref, dal_ref, dd_ref, dh_ref):
        g, ci = pl.program_id(0), pl.program_id(1)

        @pl.when(ci == 0)
        def _():
            dh_ref[...] = jnp.zeros_like(dh_ref)

        @pl.when((ci == 0) & (g == 0))
        def _():
            dal_ref[...] = jnp.zeros_like(dal_ref)
            dd_ref[...] = jnp.zeros_like(dd_ref)

        args = (ck_ref[...], xs_ref[...], bm_ref[...], cm_ref[...], dt_ref[...], al_ref[...], d_ref[...])

        def run(group):
            _, vjp = jax.vjp(functools.partial(ssd_chunk_fn, group), *args)
            dh0, dxs, dbm, dcm, ddt, dal, dd = vjp((dy_ref[...].astype(F32), dh_ref[...]))
            dh_ref[...] = dh0
            dxs_ref[...] = dxs.astype(BF16)
            dbm_ref[...] = dbm.astype(BF16)
            dcm_ref[...] = dcm.astype(BF16)
            ddt_ref[...] = ddt
            dal_ref[...] += dal
            dd_ref[...] += dd

        pl.when(g == 0)(lambda: run(0))
        pl.when(g == 1)(lambda: run(1))

    in_specs = _ssd_specs(q, rev) + [pl.BlockSpec((None, gw, SSD_STATE), lambda g, ci: (rev(ci), g, 0)),
                                     pl.BlockSpec((q, gw), lambda g, ci: (rev(ci), g))]
    return pl.pallas_call(
        body, name="ssd_scan_bwd", grid=(2, nc), in_specs=in_specs,
        out_specs=[pl.BlockSpec((q, gw), lambda g, ci: (rev(ci), g)),
                   pl.BlockSpec((q, SSD_STATE), lambda g, ci: (rev(ci), g)),
                   pl.BlockSpec((q, SSD_STATE), lambda g, ci: (rev(ci), g)),
                   pl.BlockSpec((None, q, LANES), lambda g, ci: (g, rev(ci), 0)),
                   pl.BlockSpec((1, LANES), lambda g, ci: (0, 0)),
                   pl.BlockSpec((1, LANES), lambda g, ci: (0, 0))],
        out_shape=[jax.ShapeDtypeStruct((t, SSD_WIDTH), BF16), jax.ShapeDtypeStruct((t, 2 * SSD_STATE), BF16),
                   jax.ShapeDtypeStruct((t, 2 * SSD_STATE), BF16), jax.ShapeDtypeStruct((2, t, LANES), F32),
                   jax.ShapeDtypeStruct((1, LANES), F32), jax.ShapeDtypeStruct((1, LANES), F32)],
        scratch_shapes=[pltpu.VMEM((gw, SSD_STATE), F32)], compiler_params=_params(),
    )(xs, bm, cm, dt, a_log, d_skip, ck, dy)


def loss_and_grad(x, tgt, g, tm):
    t, d = x.shape
    tm = min(tm, t)
    nb = t // tm

    def body(x_ref, t_ref, g_ref, loss_ref, dx_ref, dxb_ref, dg_ref):
        @pl.when(pl.program_id(0) == 0)
        def _():
            loss_ref[...] = jnp.zeros_like(loss_ref)
            dg_ref[...] = jnp.zeros_like(dg_ref)

        val, vjp = jax.vjp(loss_fn, x_ref[...], t_ref[...], g_ref[...])
        dx, _, dg = vjp(jnp.ones((1, 1), F32))
        loss_ref[...] += jnp.broadcast_to(val, loss_ref.shape)
        dx_ref[...] = dx
        dxb_ref[...] = dx.astype(BF16)
        dg_ref[...] += dg

    row = pl.BlockSpec((tm, d), lambda i: (i, 0))
    one = pl.BlockSpec((1, d), lambda i: (0, 0))
    return pl.pallas_call(
        body, name="loss_and_grad", grid=(nb,), in_specs=[row, row, one],
        out_specs=[pl.BlockSpec((SUBLANES, LANES), lambda i: (0, 0)), row, row, one],
        out_shape=[jax.ShapeDtypeStruct((SUBLANES, LANES), F32), jax.ShapeDtypeStruct((t, d), F32),
                   jax.ShapeDtypeStruct((t, d), BF16), jax.ShapeDtypeStruct((1, d), F32)],
        compiler_params=_params(),
    )(x, tgt, g)


def adamw(name, recv, w, m, v, dep=None):
    rows, cols = w.shape
    n_slabs = recv.shape[0]
    recv_block_bytes = 8 * 1024 * 1024
    tm = _pick(rows, [c for c in (256, 128, 64, 32, 16, 8)
                      if n_slabs * c * cols * recv.dtype.itemsize <= recv_block_bytes])
    c1 = 1.0 / (1.0 - ADAM_B1 ** ADAM_STEP)
    c2 = 1.0 / (1.0 - ADAM_B2 ** ADAM_STEP)

    n_dep = 0 if dep is None else 1

    def body(recv_ref, w_ref, m_ref, v_ref, *rest):
        g_ref, d_ref, nm_ref, nv_ref = rest[n_dep:]
        g = recv_ref[0].astype(F32)
        for p in range(1, n_slabs):
            g = g + recv_ref[p].astype(F32)
        nm =ADAM_B1 * m_ref[...] + (1.0 - ADAM_B1) * g
        nv = ADAM_B2 * v_ref[...] + (1.0 - ADAM_B2) * jnp.square(g)
        g_ref[...] = g
        nm_ref[...] = nm
        nv_ref[...] = nv
        d_ref[...] = -ADAM_LR * ((nm * c1) / (jnp.sqrt(nv * c2) + ADAM_EPS) + ADAM_WD * w_ref[...])

    blk = pl.BlockSpec((tm, cols), lambda i: (i, 0))
    return pl.pallas_call(
        body, name=name, grid=(rows // tm,),
        in_specs=[pl.BlockSpec((n_slabs, tm, cols), lambda i: (0, i, 0)), blk, blk, blk]
        + [pl.BlockSpec(memory_space=pl.ANY)] * n_dep,
        out_specs=[blk] * 4, out_shape=[jax.ShapeDtypeStruct((rows, cols), F32)] * 4,
        compiler_params=_params(),
    )(recv, w, m, v, *([] if dep is None else [dep]))


def _mesh_pos():
    return lax.axis_index("x"), lax.axis_index("y"), lax.axis_index("c")


def _peer(pos, mask):
    x, y, c = pos
    return (1 - x if mask & 4 else x, 1 - y if mask & 2 else y, 1 - c if mask & 1 else c)


def _linear(pos):
    return 4 * pos[0] + 2 * pos[1] + pos[2]


class Exchange:
    MASKS = {"gather": (1, 2, 3, 4, 5, 6, 7), "scatter": (1, 2, 3, 4, 5, 6, 7), "gather_chips": (1, 2, 4, 6),
             "forward": (2, 4, 6)}

    def __init__(self, xs, kind, lands=None):
        self.kind, self.masks = kind, self.MASKS[kind]
        self.xs = [] if kind == "forward" else list(xs)
        if kind == "forward":
            self.land_shape = [jax.ShapeDtypeStruct(l.shape, l.dtype) for l in lands]
        elif kind == "scatter":
            self.land_shape = [jax.ShapeDtypeStruct(x.shape, x.dtype) for x in xs]
        else:
            self.land_shape = [jax.ShapeDtypeStruct((N_DEV,) + x.shape, x.dtype) for x in xs]
        self.n = len(self.land_shape)
        copies = self.n * len(self.masks)
        self.sems = [pltpu.SemaphoreType.DMA((copies,)), pltpu.SemaphoreType.DMA((copies,)),
                     pltpu.SemaphoreType.DMA((self.n,))]

    def _copies(self, ins, outs, sems, landing):
        send_sems, recv_sems, local_sems = sems
        me = _mesh_pos()
        me_lin = _linear(me)
        local, remote = [], []
        for ti in range(self.n):
            if self.kind != "forward":
                src_mine = ins[ti].at[me_lin] if self.kind == "scatter" else ins[ti]
                local.append(pltpu.make_async_copy(src_mine, outs[ti].at[me_lin], local_sems.at[ti]))
            for j, mask in enumerate(self.masks):
                if self.kind == "forward":
                    peer = _peer(me, 1)
                    src = outs[ti].at[_linear(_peer(me, mask))]
                    dst = outs[ti].at[_linear(_peer(me, mask ^ 1 if landing else mask))]
                else:
                    peer = _peer(me, mask)
                    src = ins[ti].at[_linear(peer)] if self.kind == "scatter" else ins[ti]
                    dst = outs[ti].at[_linear(peer) if landing else me_lin]
                sem_index = ti * len(self.masks) + j
                remote.append(pltpu.make_async_remote_copy(
                    src_ref=src, dst_ref=dst, send_sem=send_sems.at[sem_index], recv_sem=recv_sems.at[sem_index],
                    device_id=peer, device_id_type=pl.DeviceIdType.MESH))
        return local, remote

    def start(self, ins, outs, sems):
        local, remote = self._copies(ins, outs, sems, landing=False)
        for cp in local + remote:
            cp.start()

    def finish(self, ins, outs, sems):
        local, remote = self._copies(ins, outs, sems, landing=True)
        for cp in remote:
            cp.wait_recv()
        for cp in remote:
            cp.wait_send()
        for cp in local:
            cp.wait()


def exchange_start(name, xs, kind, dep=None, lands=None):
    ex = Exchange(xs, kind, lands)
    hbm = pl.BlockSpec(memory_space=pltpu.HBM)
    sem = pl.BlockSpec(memory_space=pltpu.SEMAPHORE)
    if lands is None:
        lands = [lax.empty(s.shape, s.dtype) for s in ex.land_shape]
    n_src, n = len(ex.xs), ex.n
    n_inputs = n_src + n + (0 if dep is None else 1)

    def body(*refs):
        ins, lnd, sems, token = refs[:n_src], refs[n_src:n_src + n], refs[n_inputs:n_inputs + 3], refs[-1]
        ex.start(ins, lnd, sems)
        token[...] = jnp.zeros_like(token)

    res = pl.pallas_call(
        body, name=name, in_specs=[hbm] * (n_src + n) + ([] if dep is None else [pl.BlockSpec(memory_space=pl.ANY)]),
        out_specs=[sem] * 3 + [hbm] * (n_src + n) + [pl.BlockSpec(memory_space=pltpu.VMEM)],
        out_shape=ex.sems + [pltpu.HBM(x.shape, x.dtype) for x in ex.xs]
        + [pltpu.HBM(s.shape, s.dtype) for s in ex.land_shape] + [jax.ShapeDtypeStruct((SUBLANES, LANES), F32)],
        input_output_aliases={i: 3 + i for i in range(n_src + n)},
        compiler_params=pltpu.CompilerParams(has_side_effects=pltpu.SideEffectType.DATAFLOW_SIDE_EFFECTING),
    )(*[pltpu.with_memory_space_constraint(x, pltpu.HBM) for x in ex.xs + list(lands)],
      *([] if dep is None else [dep]))
    return (ex, res[:3], res[3:3 + n_src], res[3 + n_src:3 + n_src + n]), res[-1]


def exchange_wait(name, handles, after):
    ex, sems, srcs, lands = handles
    n_src, n = len(srcs), len(lands)
    hbm = pl.BlockSpec(memory_space=pltpu.HBM)
    sem = pl.BlockSpec(memory_space=pltpu.SEMAPHORE)

    def body(*refs):
        ins, lnd, sem_refs = refs[:n_src], refs[n_src:n_src + n], refs[n_src + n:n_src + n + 3]
        ex.finish(ins, lnd, sem_refs)

    res = pl.pallas_call(
        body, name=name, in_specs=[hbm] * (n_src + n) + [sem] * 3 + [pl.BlockSpec(memory_space=pl.ANY)],
        out_specs=[hbm] * (n_src + n),
        out_shape=[pltpu.HBM(x.shape, x.dtype) for x in srcs] + [pltpu.HBM(x.shape, x.dtype) for x in lands],
        input_output_aliases={i: i for i in range(n_src + n)},
        compiler_params=pltpu.CompilerParams(has_side_effects=pltpu.SideEffectType.DATAFLOW_SIDE_EFFECTING),
    )(*srcs, *lands, *sems, after)
    return res[n_src:]


def forward_start(name, chip_gather, after):
    lands = exchange_wait(name + "_wait", chip_gather, after)
    return exchange_start(name + "_forward_start", [], "forward", lands=lands)


_Z = (0, 1024)
_XBC = (1024, 2560)
_DT = (2560, 2576)
_RKV = (2576, 5648)
_PW = (5648, 5744)
_PA = (5744, 5840)
_PG = (5840, 6096)
D_IN = 6096

_SMALL = ("norm_mix_g", "ssd_conv_b", "ssd_dt_bias", "ssd_a_log", "ssd_d", "ssd_norm_g", "rwkv_mu", "rwkv_w0",
          "rwkv_a0", "rwkv_k_k", "rwkv_k_a", "rwkv_r_k", "rwkv_ln_w", "rwkv_ln_b", "norm_x_g", "norm_mem_g",
          "norm_ffn_g", "final_norm_g")
_WEIGHTS = ("norm_mix_g", "w_in", "ssd_conv_w", "ssd_conv_b", "ssd_dt_bias", "ssd_a_log", "ssd_d", "ssd_norm_g",
            "rwkv_mu", "rwkv_w0", "rwkv_w2", "rwkv_a0", "rwkv_a2", "rwkv_g2", "rwkv_k_k", "rwkv_k_a", "rwkv_r_k",
            "rwkv_ln_w", "rwkv_ln_b", "w_out", "norm_x_g", "norm_mem_g", "xattn_wq", "xattn_wk", "xattn_wv",
            "xattn_wo", "norm_ffn_g", "ffn_w1", "ffn_w2", "final_norm_g")


def _pad_lanes(x, width=LANES):
    return jnp.pad(x, ((0, 0), (0, width - x.shape[1])))


def _pack_small(vals):
    flat = jnp.concatenate([vals[n].reshape(-1) for n in _SMALL])
    rows = -(-flat.shape[0] // (LANES * SUBLANES)) * SUBLANES
    return jnp.pad(flat, (0, rows * LANES - flat.shape[0])).reshape(rows, LANES)


def _unpack_small(packed, shapes):
    flat = packed.reshape(-1)
    out, pos = {}, 0
    for n in _SMALL:
        size = 1
        for s in shapes[n]:
            size *= s
        out[n] = flat[pos:pos + size].reshape(shapes[n])
        pos += size
    return out


def _rows(w, rng):
    return w[rng[0]:rng[1]]


def sum_slabs(name, recv):
    n, rows, cols = recv.shape
    tc = _pick(cols, (256, 128))

    def body(r_ref, o_ref):
        acc = r_ref[0].astype(F32)
        for p in range(1, n):
            acc = acc + r_ref[p].astype(F32)
        o_ref[...] = acc

    return pl.pallas_call(
        body, name=name, grid=(cols // tc,), in_specs=[pl.BlockSpec((n, rows, tc), lambda j: (0, 0, j))],
        out_specs=pl.BlockSpec((rows, tc), lambda j: (0, j)), out_shape=jax.ShapeDtypeStruct((rows, cols), F32),
        compiler_params=_params(),
    )(recv)


def kernel(x, mem, norm_mix_g, w_in, ssd_conv_w, ssd_conv_b, ssd_dt_bias, ssd_a_log, ssd_d, ssd_norm_g, rwkv_mu, rwkv_w0, rwkv_w2, rwkv_a0, rwkv_a2, rwkv_g2, rwkv_k_k, rwkv_k_a, rwkv_r_k, rwkv_ln_w, rwkv_ln_b, w_out, norm_x_g, norm_mem_g, xattn_wq, xattn_wk, xattn_wv, xattn_wo, norm_ffn_g, ffn_w1, ffn_w2, final_norm_g, loss_target, m_norm_mix_g, m_w_in, m_ssd_conv_w, m_ssd_conv_b, m_ssd_dt_bias, m_ssd_a_log, m_ssd_d, m_ssd_norm_g, m_rwkv_mu, m_rwkv_w0, m_rwkv_w2, m_rwkv_a0, m_rwkv_a2, m_rwkv_g2, m_rwkv_k_k, m_rwkv_k_a, m_rwkv_r_k, m_rwkv_ln_w, m_rwkv_ln_b, m_w_out, m_norm_x_g, m_norm_mem_g, m_xattn_wq, m_xattn_wk, m_xattn_wv, m_xattn_wo, m_norm_ffn_g, m_ffn_w1, m_ffn_w2, m_final_norm_g, v_norm_mix_g, v_w_in, v_ssd_conv_w, v_ssd_conv_b, v_ssd_dt_bias, v_ssd_a_log, v_ssd_d, v_ssd_norm_g, v_rwkv_mu, v_rwkv_w0, v_rwkv_w2, v_rwkv_a0, v_rwkv_a2, v_rwkv_g2, v_rwkv_k_k, v_rwkv_k_a, v_rwkv_r_k, v_rwkv_ln_w, v_rwkv_ln_b, v_w_out, v_norm_x_g, v_norm_mem_g, v_xattn_wq, v_xattn_wk, v_xattn_wv, v_xattn_wo, v_norm_ffn_g, v_ffn_w1, v_ffn_w2, v_final_norm_g):
    given = dict(locals())
    wts = {n: given[n] for n in _WEIGHTS}
    mom_m = {n: given["m_" + n] for n in _WEIGHTS}
    mom_v = {n: given["v_" + n] for n in _WEIGHTS}
    d = D_MODEL
    xt, memt, tgt = x[0], mem[0], loss_target[0]
    tm = 256
    tm_light = 512

    big = {"w_in": jnp.transpose(w_in[0]), "w_out": w_out[0], "xattn_wq": xattn_wq[0], "xattn_wk": xattn_wk[0],
           "xattn_wv": xattn_wv[0], "xattn_wo": xattn_wo[0], "ffn_w1": ffn_w1[0], "ffn_w2": ffn_w2[0]}
    small_sh = {"ssd_conv_w": ssd_conv_w.reshape(4, -1), "rwkv_w2": rwkv_w2[0], "rwkv_a2": rwkv_a2[0],
                "rwkv_g2": rwkv_g2[0]}
    cast_one = lambda n, deps=(): rowwise_fwd("cast_" + n, cast_fn, [big[n]], [], [(big[n].shape[1], BF16)],
                                              256 if big[n].shape[0] % 256 == 0 else big[n].shape[0], deps=deps)[0]
    gather_in, token_in = exchange_start("gather_in_start", [cast_one("w_in")] + list(small_sh.values()), "gather_chips")
    cast = {n: cast_one(n, deps=[token_in]) for n in big if n != "w_in"}
    late_a = ("w_out", "xattn_wq", "xattn_wk", "xattn_wv", "xattn_wo")
    late_b = ("ffn_w1", "ffn_w2")
    gather_a, token_a = exchange_start("gather_attn_start", [cast[n] for n in late_a], "gather_chips", dep=token_in)
    gather_b, token_b = exchange_start("gather_ffn_start", [cast[n] for n in late_b], "gather_chips", dep=token_a)
    (h1,) = rowwise_fwd("norm_mix", rmsnorm_fn, [xt], [norm_mix_g], [(d, BF16)], tm_light, deps=[token_b])
    forward_in, token_in = forward_start("gather_in", gather_in, after=h1)
    gathered = exchange_wait("gather_in_forward_wait", forward_in, after=token_in)
    g_big = {"w_in": gathered[0]}
    g_small = dict(zip(small_sh, gathered[1:]))

    pad_rows = lambda a: jnp.pad(a, ((0, LANES - a.shape[0]), (0, 0)))
    w_in_t = g_big["w_in"].reshape(D_IN, d)
    wt_z, wt_xbc, wt_rkv = (_rows(w_in_t, r) for r in (_Z, _XBC, _RKV))
    wt_ps = jnp.concatenate([_rows(w_in_t, _PG)] + [pad_rows(_rows(w_in_t, r)) for r in (_PW, _PA, _DT)], axis=0)
    unshard_cols = lambda g: jnp.transpose(g, (1, 0, 2)).reshape(g.shape[1], -1)
    conv_w_f = unshard_cols(g_small["ssd_conv_w"])
    w2p, a2p = pad_rows(unshard_cols(g_small["rwkv_w2"])), pad_rows(unshard_cols(g_small["rwkv_a2"]))
    g2_f = unshard_cols(g_small["rwkv_g2"])

    mu = rwkv_mu
    mu_rkv, mu_pg = mu[:, :3072], mu[:, 3264:3520]
    mu_pwa = jnp.concatenate([_pad_lanes(mu[:, 3072:3168]), _pad_lanes(mu[:, 3168:3264])], axis=1)
    dt_bias_p, a_log_p, d_p = _pad_lanes(ssd_dt_bias), _pad_lanes(ssd_a_log), _pad_lanes(ssd_d)
    r_k_row = rwkv_r_k.reshape(1, RWKV_WIDTH)
    g_final = final_norm_g.reshape(1, d)

    u_z = mm("in_z", h1, wt_z, "nt")
    u_xbc = mm("in_xbc", h1, wt_xbc, "nt")
    u_rkv = mm("in_rkv", h1, wt_rkv, "nt")
    u_ps = mm("in_narrow", h1, wt_ps, "nt")

    ssd_pre_rows = lambda: [Rows(u_xbc, shifts=(1, 2, 3)), Rows(u_ps, LANES, 4)]
    ssd_pre_params = [conv_w_f, ssd_conv_b, dt_bias_p]
    xs, bm, cm, dt = rowwise_fwd("ssd_pre", ssd_pre_fn, ssd_pre_rows(), ssd_pre_params,
                                 [(SSD_WIDTH, F32), (256, F32), (256, F32), (LANES, F32)], tm)
    y_scan, ssd_ck = ssd_scan_fwd(xs, bm, cm, dt, a_log_p, d_p)
    (y_ssd,) = rowwise_fwd("ssd_post", ssd_post_fn, [y_scan, u_z], [ssd_norm_g], [(SSD_WIDTH, BF16)], tm_light,
                           into=(None, d, 0))

    rwkv_pre_rows = lambda: [Rows(u_rkv, shifts=(1,)), Rows(u_ps, 2 * LANES, 0, shifts=(1,)), Rows(u_ps, 2 * LANES, 1, shifts=(1,))]
    rwkv_pre_params = [mu_rkv, mu_pg, mu_pwa, rwkv_w0, w2p, rwkv_a0, a2p, g2_f, rwkv_k_k, rwkv_k_a]
    forward_a, token_a = forward_start("gather_attn", gather_a, after=y_scan)
    r_, lw_, k_, v_, kap_, b_, gate_ = rowwise_fwd("rwkv_pre", rwkv_pre_fn, rwkv_pre_rows(), rwkv_pre_params,
                                                   [(RWKV_WIDTH, F32)] * 7, tm, deps=[token_a])
    ys_r, rwkv_ck, rwkv_inv = rwkv_scan_fwd(r_, lw_, k_, v_, kap_, b_)
    forward_b, token_b = forward_start("gather_ffn", gather_b, after=ys_r)
    g_big.update(zip(late_a, exchange_wait("gather_attn_forward_wait", forward_a, after=token_b)))
    w_out_f = g_big["w_out"].reshape(d, d)
    wq_f, wk_f, wv_f, wo_f = (g_big[n].reshape(d, d) for n in ("xattn_wq", "xattn_wk", "xattn_wv", "xattn_wo"))
    rwkv_post_params = [rwkv_ln_w, rwkv_ln_b, r_k_row]
    (ycat,) = rowwise_fwd("rwkv_post", rwkv_post_fn, [ys_r, r_, k_, v_, gate_], rwkv_post_params,
                          [(RWKV_WIDTH, BF16)], tm, into=(y_ssd, d, 1))
    x1 = mm("out_proj", ycat, w_out_f, "nn", res=xt)

    (h2,) = rowwise_fwd("norm_x", rmsnorm_fn, [x1], [norm_x_g], [(d, BF16)], tm_light)
    (mn,) = rowwise_fwd("norm_mem", rmsnorm_fn, [memt], [norm_mem_g], [(d, BF16)], tm)
    q = mm("xattn_q", h2, wq_f, "nn", out_dtype=BF16)
    kx = mm("xattn_k", mn, wk_f, "nn")
    vx = mm("xattn_v", mn, wv_f, "nn")
    (o,) = rowwise_fwd("xattn", attn_fn, [q], [kx, vx], [(d, BF16)], tm_light)
    x2 = mm("xattn_o", o, wo_f, "nn", res=x1)

    (h3,) = rowwise_fwd("norm_ffn", rmsnorm_fn, [x2], [norm_ffn_g], [(d, BF16)], tm_light)
    w1_s, w2_g = exchange_wait("gather_ffn_forward_wait", forward_b, after=h3)
    w2_f = w2_g.reshape(D_FF, d)
    relu2_epi = lambda acc: (jnp.square(jnp.maximum(acc, 0.0)), jnp.maximum(acc, 0.0))
    hid, relu_a = mm("ffn_1", h3, w1_s, "nn", b_slabs=N_DEV, epi=relu2_epi, out_dtypes=[BF16, BF16])
    x3 = mm("ffn_2", hid, w2_f, "nn", res=x2)

    loss_blk, dx3, dx3_b, dg_final = loss_and_grad(x3, tgt, g_final, tm_light)

    grads = {}
    grads["ffn_w2"] = mm("d_ffn_w2", hid, dx3_b, "tn", out_dtype=BF16).reshape(N_DEV, D_FF // N_DEV, d)
    sc_w2, tok = exchange_start("scatter_ffn_w2_start", [grads["ffn_w2"]], "scatter")
    da = mm("d_hid", dx3_b, w2_f, "nt", dep=tok, epi=lambda acc, ra: (2.0 * acc * ra,), extras=[relu_a],
            out_dtypes=[BF16])
    grads["ffn_w1"] = mm("d_ffn_w1", h3, da, "tn", out_dtype=BF16, out_slabs=N_DEV)
    sc_w1, tok = exchange_start("scatter_ffn_w1_start", [grads["ffn_w1"]], "scatter")
    dh3 = mm("d_h3", da, w1_s, "nt", out_dtype=BF16, b_slabs=N_DEV, dep=tok)
    (dx2,), (dg_ffn,) = rowwise_bwd("norm_ffn_bwd", rmsnorm_fn, [x2], [norm_ffn_g], [[dh3]], tm_light, [F32], row_add=[dx3])

    grads["xattn_wo"] = mm("d_wo", o, dx2, "tn", out_dtype=BF16).reshape(N_DEV, d // N_DEV, d)
    sc_wo, tok = exchange_start("scatter_wo_start", [grads["xattn_wo"]], "scatter")
    d_o = mm("d_o", dx2, wo_f, "nt", out_dtype=BF16, dep=tok)
    (dq,), (dkx, dvx) = rowwise_bwd("xattn_bwd", attn_fn, [q], [kx, vx], [[d_o]], tm_light, [BF16])
    grads["xattn_wq"] = mm("d_wq", h2, dq, "tn", out_dtype=BF16).reshape(N_DEV, d // N_DEV, d)
    grads["xattn_wk"] = mm("d_wk", mn, dkx, "tn", out_dtype=BF16).reshape(N_DEV, d // N_DEV, d)
    grads["xattn_wv"] = mm("d_wv", mn, dvx, "tn", out_dtype=BF16).reshape(N_DEV, d // N_DEV, d)
    qkv = ("xattn_wq", "xattn_wk", "xattn_wv")
    sc_qkv, tok = exchange_start("scatter_qkv_start", [grads[n] for n in qkv], "scatter")
    dmn = mm("d_mn_v", dvx, wv_f, "nt", res=mm("d_mn_k", dkx, wk_f, "nt", dep=tok))
    _, (dg_mem,) = rowwise_bwd("norm_mem_bwd", rmsnorm_fn, [memt], [norm_mem_g], [[dmn]], tm, [None])
    dh2 = mm("d_h2", dq, wq_f, "nt", out_dtype=BF16, dep=dg_mem)
    (dx1,), (dg_x,) = rowwise_bwd("norm_x_bwd", rmsnorm_fn, [x1], [norm_x_g], [[dh2]], tm_light, [F32], row_add=[dx2])

    grads["w_out"] = mm("d_w_out", ycat, dx1, "tn", out_dtype=BF16).reshape(N_DEV, d // N_DEV, d)
    sc_wout, tok = exchange_start("scatter_w_out_start", [grads["w_out"]], "scatter")
    d_ycat = mm("d_ycat", dx1, w_out_f, "nt", out_dtype=BF16, dep=tok)

    (d_ys, d_r1, d_k1, d_v1, d_gate), (dln_w, dln_b, dr_k) = rowwise_bwd(
        "rwkv_post_bwd", rwkv_post_fn, [ys_r, r_, k_, v_, gate_], rwkv_post_params,
        [[Rows(d_ycat, RWKV_WIDTH, 1)]], tm, [BF16] * 5)
    d_r2, d_lw, d_k2, d_v2, d_kap, d_b = rwkv_scan_bwd(r_, lw_, k_, v_, kap_, b_, rwkv_ck, rwkv_inv, d_ys)
    (du_rkv, du_pg, du_pwa), rwkv_pg = rowwise_bwd(
        "rwkv_pre_bwd", rwkv_pre_fn, rwkv_pre_rows(), rwkv_pre_params,
        [[d_r1, d_r2], [d_lw], [d_k1, d_k2], [d_v1, d_v2], [d_kap], [d_b], [d_gate]], tm, [BF16] * 3)
    dmu_rkv, dmu_pg, dmu_pwa, dw0, dw2p, da0, da2p, dg2, dk_k, dk_a = rwkv_pg

    (d_yscan, du_z), (dssd_norm_g,) = rowwise_bwd("ssd_post_bwd", ssd_post_fn, [y_scan, u_z], [ssd_norm_g],
                                                  [[Rows(d_ycat, SSD_WIDTH, 0)]], tm_light, [BF16, BF16])
    dxs, dbm, dcm, ddt2, da_log_p, dd_p = ssd_scan_bwd(xs, bm, cm, dt, a_log_p, d_p, ssd_ck, d_yscan)
    (du_xbc, du_dt), (dconv_w, dconv_b, ddt_bias_p) = rowwise_bwd(
        "ssd_pre_bwd", ssd_pre_fn, ssd_pre_rows(), ssd_pre_params,
        [[dxs], [dbm], [dcm], [ddt2[0], ddt2[1]]], tm, [BF16, BF16])
    du_ps = jnp.concatenate([du_pg, du_pwa, du_dt], axis=1)

    dwt_z = mm("d_w_z", du_z, h1, "tn", out_dtype=BF16)
    dwt_xbc = mm("d_w_xbc", du_xbc, h1, "tn", out_dtype=BF16)
    dwt_rkv = mm("d_w_rkv", du_rkv, h1, "tn", out_dtype=BF16)
    dwt_ps = mm("d_w_narrow", du_ps, h1, "tn", out_dtype=BF16)
    dwt_full = jnp.concatenate([dwt_z, dwt_xbc, dwt_ps[512:528], dwt_rkv, dwt_ps[256:352], dwt_ps[384:480], dwt_ps[0:256]],
                               axis=0)
    to_slabs = lambda g: jnp.transpose(g.reshape(g.shape[0], N_DEV, -1), (1, 0, 2))
    grads["w_in"] = dwt_full.reshape(N_DEV, D_IN // N_DEV, d)
    grads["ssd_conv_w"] = to_slabs(dconv_w)
    grads["rwkv_w2"] = to_slabs(dw2p[:96])
    grads["rwkv_a2"] = to_slabs(da2p[:96])
    grads["rwkv_g2"] = to_slabs(dg2)
    tail = ("w_in", "ssd_conv_w", "rwkv_w2", "rwkv_a2", "rwkv_g2")
    sc_tail, tok = exchange_start("scatter_tail_start", [grads[n] for n in tail], "scatter")
    dh1 = mm("d_h1_z", du_z, wt_z, "nn", dep=tok)
    dh1 = mm("d_h1_xbc", du_xbc, wt_xbc, "nn", res=dh1)
    dh1 = mm("d_h1_rkv", du_rkv, wt_rkv, "nn", res=dh1)
    dh1 = mm("d_h1_narrow", du_ps, wt_ps, "nn", res=dh1)
    (dx,), (dg_mix,) = rowwise_bwd("norm_mix_bwd", rmsnorm_fn, [xt], [norm_mix_g], [[dh1]], tm_light, [F32], row_add=[dx1])

    dmu =jnp.concatenate([dmu_rkv, dmu_pwa[:, 0:96], dmu_pwa[:, 128:224], dmu_pg], axis=1)
    small_grads = {
        "norm_mix_g": dg_mix, "ssd_conv_b": dconv_b, "ssd_dt_bias": ddt_bias_p[:, :16], "ssd_a_log": da_log_p[:, :16],
        "ssd_d": dd_p[:, :16], "ssd_norm_g": dssd_norm_g, "rwkv_mu": dmu, "rwkv_w0": dw0, "rwkv_a0": da0,
        "rwkv_k_k": dk_k, "rwkv_k_a": dk_a, "rwkv_r_k": dr_k, "rwkv_ln_w": dln_w, "rwkv_ln_b": dln_b,
        "norm_x_g": dg_x, "norm_mem_g": dg_mem, "norm_ffn_g": dg_ffn, "final_norm_g": dg_final}

    gather_small, tok = exchange_start("gather_small_start", [_pack_small(small_grads)], "gather")
    received = {}
    for names, handle in ((("ffn_w2",), sc_w2), (("ffn_w1",), sc_w1), (("xattn_wo",), sc_wo), (qkv, sc_qkv),
                          (("w_out",), sc_wout)):
        received.update(zip(names, exchange_wait("scatter_" + names[0] + "_wait", handle, after=tok)))

    out_g, out_d, out_m, out_v = {}, {}, {}, {}

    def run_adamw(n, dep):
        shape = wts[n].shape
        two_d = lambda a: a.reshape(-1, shape[-1])
        if n == "w_in":
            recv = jnp.transpose(sum_slabs("sum_w_in", received[n]))[None]
        else:
            recv = received[n].reshape(N_DEV, -1, shape[-1])
        res = adamw("adamw_" + n, recv, two_d(wts[n]), two_d(mom_m[n]), two_d(mom_v[n]), dep=dep)
        out_g[n], out_d[n], out_m[n], out_v[n] = (r.reshape(shape) for r in res)
        return res[0]

    last = None
    for n in ("ffn_w2", "ffn_w1", "xattn_wo") + qkv + ("w_out",):
        last = run_adamw(n, last)
    received.update(zip(tail, exchange_wait("scatter_tail_wait", sc_tail, after=last)))
    for n in tail:
        last = run_adamw(n, last)
    (small_all,) = exchange_wait("gather_small_wait", gather_small, after=last)
    res = adamw("adamw_small", small_all, _pack_small(wts), _pack_small(mom_m), _pack_small(mom_v))
    shapes = {n: wts[n].shape for n in _SMALL}
    for dst, packed in zip((out_g, out_d, out_m, out_v), res):
        dst.update(_unpack_small(packed, shapes))

    loss = lax.psum(loss_blk[0, 0], ("x", "y", "c"))
    return (loss, dx[None], *[out_g[n] for n in _WEIGHTS], *[out_d[n] for n in _WEIGHTS],
            *[out_m[n] for n in _WEIGHTS], *[out_v[n] for n in _WEIGHTS])
```

```python
import functools

import jax
import jax.numpy as jnp
from jax import lax
from jax.experimental import pallas as pl
from jax.experimental.pallas import tpu as pltpu

F32 = jnp.float32
BF16 = jnp.bfloat16

N_DEV = 8
D_MODEL = 2048
NORM_EPS = 1e-6
SSD_WIDTH = 1024
SSD_HEAD_DIM = 64
SSD_STATE = 128
SSD_CHUNK = 128
SSD_HEADS_PER_GROUP = 8
RWKV_WIDTH = 1024
RWKV_HEADS = 16
RWKV_HEAD_DIM = 64
RWKV_LN_EPS = 64e-5
RWKV_CHUNK = 128
RWKV_HEADS_PER_STEP = 16
XATTN_HEADS = 4
XATTN_HEAD_DIM = 512
D_FF = 8192
LANES = 128
SUBLANES = 8
VMEM_LIMIT = 56 * 1024 * 1024

ADAM_LR = 0.001
ADAM_B1 = 0.9
ADAM_B2 = 0.999
ADAM_EPS = 1e-08
ADAM_WD = 0.01
ADAM_STEP = 10

_DN = {"nn": ((1,), (0,)), "nt": ((1,), (1,)), "tn": ((0,), (0,))}


def _dg(a, b, mode):
    (ca,), (cb,) = _DN[mode]
    dn = (((ca + 1,), (cb + 1,)), ((0,), (0,))) if a.ndim == 3 else (((ca,), (cb,)), ((), ()))
    return lax.dot_general(a, b, dn, preferred_element_type=F32)


@functools.partial(jax.custom_vjp, nondiff_argnums=(2,))
def bdot(a, b, mode):
    return _dg(a.astype(BF16), b.astype(BF16), mode)


def _bdot_fwd(a, b, mode):
    return bdot(a, b, mode), (a, b)


def _bdot_bwd(mode, res, g):
    a, b = res
    ab, bb, gb = a.astype(BF16), b.astype(BF16), g.astype(BF16)
    if mode == "nn":
        da, db = _dg(gb, bb, "nt"), _dg(ab, gb, "tn")
    elif mode == "nt":
        da, db = _dg(gb, bb, "nn"), _dg(gb, ab, "tn")
    else:
        da, db = _dg(bb, gb, "nt"), _dg(ab, gb, "nn")
    return da.astype(a.dtype), db.astype(b.dtype)


bdot.defvjp(_bdot_fwd, _bdot_bwd)


def _split2(x):
    hi = x.astype(BF16)
    return hi, (x - hi.astype(F32)).astype(BF16)


def _dot01(x, m01):
    hi, lo = _split2(x)
    return _dg(hi, m01, "nn") + _dg(lo, m01, "nn")


def _exact_dot_impl(a, b, mode, exact):
    if exact == "a":
        ae = a.astype(BF16)
        return sum(_dg(ae, part, mode) for part in _split2(b))
    be = b.astype(BF16)
    return sum(_dg(part, be, mode) for part in _split2(a))


@functools.partial(jax.custom_vjp, nondiff_argnums=(2, 3))
def exact_dot(a, b, mode, exact):
    return _exact_dot_impl(a, b, mode, exact)


def _exact_dot_fwd(a, b, mode, exact):
    return _exact_dot_impl(a, b, mode, exact), (a, b)


def _exact_dot_bwd(mode, exact, res, g):
    a, b = res
    if exact == "a":
        db = {"nn": lambda: _exact_dot_impl(a, g, "tn", "a"), "nt": lambda: _exact_dot_impl(g, a, "tn", "b"),
              "tn": lambda: _exact_dot_impl(a, g, "nn", "a")}[mode]()
        return jnp.zeros_like(a), db
    da = {"nn": lambda: _exact_dot_impl(g, b, "nt", "b"), "nt": lambda: _exact_dot_impl(g, b, "nn", "b"),
          "tn": lambda: _exact_dot_impl(b, g, "nt", "a")}[mode]()
    return da, jnp.zeros_like(b)


exact_dot.defvjp(_exact_dot_fwd, _exact_dot_bwd)


def _head_indicator(width, heads, transpose):
    hd = width // heads
    shape = (LANES, width) if transpose else (width, LANES)
    lane = lax.broadcasted_iota(jnp.int32, shape, 1 if not transpose else 0)
    pos = lax.broadcasted_iota(jnp.int32, shape, 0 if not transpose else 1)
    return ((pos >= lane * hd) & (pos < lane * hd + hd)).astype(BF16)


@jax.custom_vjp
def head_sum(x):
    w = x.shape[-1]
    e = _head_indicator(w, w // RWKV_HEAD_DIM, False)
    et = _head_indicator(w, w // RWKV_HEAD_DIM, True)
    return _dot01(_dot01(x, e), et)


head_sum.defvjp(lambda x: (head_sum(x), None), lambda _, g: (head_sum(g),))


def rmsnorm_fn(x, g):
    y = x * lax.rsqrt(jnp.mean(x * x, axis=-1, keepdims=True) + NORM_EPS)
    return ((y * g).astype(BF16),)


def cast_fn(x):
    return (x.astype(BF16),)


def ssd_pre_fn(xbc, xbc1, xbc2, xbc3, dt_raw, conv_w, conv_b, dt_bias):
    c = conv_w[3:4] * xbc + conv_w[2:3] * xbc1 + conv_w[1:2] * xbc2 + conv_w[0:1] * xbc3 + conv_b
    act = c * jax.nn.sigmoid(c)
    dt = jax.nn.softplus(dt_raw + dt_bias)
    return act[:, :SSD_WIDTH], act[:, SSD_WIDTH:SSD_WIDTH + 256], act[:, SSD_WIDTH + 256:], dt


def ssd_post_fn(yscan, z, norm_g):
    y = yscan * (z * jax.nn.sigmoid(z))
    half = SSD_WIDTH // 2
    parts = []
    for g in range(2):
        yg = y[:, g * half:(g + 1) * half]
        parts.append(yg * lax.rsqrt(jnp.mean(yg * yg, axis=-1, keepdims=True) + NORM_EPS))
    return ((jnp.concatenate(parts, axis=-1) * norm_g).astype(BF16),)


def rwkv_pre_fn(rkv, rkv_p, pg, pg_p, pwa, pwa_p, mu_rkv, mu_pg, mu_pwa, w0, w2p, a0, a2p, g2, k_k, k_a):
    w = RWKV_WIDTH
    rkv = rkv + (rkv_p - rkv) * mu_rkv
    pg = pg + (pg_p - pg) * mu_pg
    pwa = pwa + (pwa_p - pwa) * mu_pwa
    r, k, v = rkv[:, :w], rkv[:, w:2 * w], rkv[:, 2 * w:]
    pw, pa = pwa[:, :LANES], pwa[:, LANES:]
    w_log = -jax.nn.softplus(-(w0 + bdot(jnp.tanh(pw), w2p, "nn"))) - 0.5
    lw = -jnp.exp(w_log)
    iclr = jax.nn.sigmoid(a0 + bdot(pa, a2p, "nn"))
    gate = bdot(jax.nn.sigmoid(pg), g2, "nn")
    kk = k * k_k
    kap = kk * lax.rsqrt(jnp.maximum(head_sum(kk * kk), 1e-24))
    k_mod = k * (1.0 + (iclr - 1.0) * k_a)
    return r, lw, k_mod, v, kap, kap * iclr, gate


def rwkv_post_fn(ys, r, k_mod, v, gate, ln_w, ln_b, r_k):
    inv_n = 1.0 / RWKV_HEAD_DIM
    mean = head_sum(ys) * inv_n
    yc = ys - mean
    var = head_sum(yc * yc) * inv_n
    yn = yc * lax.rsqrt(var + RWKV_LN_EPS) * ln_w + ln_b
    bonus = head_sum(r * k_mod * r_k) * v
    return (((yn + bonus) * gate).astype(BF16),)


def attn_fn(q, kx, vx):
    outs = []
    for h in range(XATTN_HEADS):
        sl = slice(h * XATTN_HEAD_DIM, (h + 1) * XATTN_HEAD_DIM)
        s = bdot(q[:, sl], kx[:, sl], "nt") * (XATTN_HEAD_DIM ** -0.5)
        s = s - jnp.max(s, axis=-1, keepdims=True)
        p = jnp.exp(s)
        p = p / jnp.sum(p, axis=-1, keepdims=True)
        outs.append(bdot(p, vx[:, sl], "nn"))
    return (jnp.concatenate(outs, axis=-1).astype(BF16),)


def loss_fn(x, tgt, g):
    y = x * lax.rsqrt(jnp.mean(x * x, axis=-1, keepdims=True) + NORM_EPS) * g
    err = jnp.square(y - tgt)
    return 0.5 * jnp.sum(jnp.mean(err, axis=-1, keepdims=True), axis=0, keepdims=True)


def _tri_masks(n):
    row = lax.broadcasted_iota(jnp.int32, (n, n), 0)
    col = lax.broadcasted_iota(jnp.int32, (n, n), 1)
    return col <= row, col < row, row == col


@jax.custom_vjp
def unit_lower_inverse(a):
    c = a.shape[-1]
    eye = _tri_masks(c)[2].astype(F32)
    m = -a
    inv = eye + m
    n = 1
    while n * 2 < c:
        m = bdot(m, m, "nn")
        inv = bdot(inv, eye + m, "nn")
        n *= 2
    return inv


def _unit_lower_inverse_fwd(a):
    inv = unit_lower_inverse(a)
    return inv, inv


def _unit_lower_inverse_bwd(inv, g):
    return (-bdot(bdot(inv, g, "tn"), inv, "nt"),)


unit_lower_inverse.defvjp(_unit_lower_inverse_fwd, _unit_lower_inverse_bwd)


@jax.custom_vjp
def known_inverse(a, inv):
    return inv


known_inverse.defvjp(lambda a, inv: (inv, inv),
                     lambda inv, g: (_unit_lower_inverse_bwd(inv, g)[0], jnp.zeros_like(inv)))


def rwkv_chunk_fn(st0, r, lw, k, v, kap, b, inv=None):
    h, c = r.shape[0], r.shape[1]
    incl, strict, _ = _tri_masks(c)
    cum = exact_dot(jnp.broadcast_to(incl.astype(F32), (h, c, c)), lw, "nn", "a")
    g_in = jnp.exp(cum)
    g_prev = jnp.exp(cum - lw)
    g_inv = jnp.exp(-cum)
    g_end = jnp.exp(cum[:, c - 1:c, :] - cum)
    kap_t, k_t, b_t, r_t = kap * g_prev, k * g_inv, b * g_inv, r * g_in
    a_ub = jnp.where(strict, bdot(kap_t, b_t, "nt"), 0.0)
    a_vk = jnp.where(strict, bdot(kap_t, k_t, "nt"), 0.0)
    rhs = -(bdot(kap_t, st0, "nt") + bdot(a_vk, v, "nn"))
    inv = unit_lower_inverse(a_ub) if inv is None else known_inverse(a_ub, inv)
    u = bdot(inv, rhs, "nn")
    y = (bdot(r_t, st0, "nt")
         + bdot(jnp.where(incl, bdot(r_t, k_t, "nt"), 0.0), v, "nn")
         + bdot(jnp.where(incl, bdot(r_t, b_t, "nt"), 0.0), u, "nn"))
    st1 = jnp.exp(cum[:, c - 1:c, :]) * st0 + bdot(v, k * g_end, "tn") + bdot(u, b * g_end, "tn")
    return y, st1, inv


def ssd_chunk_fn(group, h0, xs, bm, cm, dt, a_log, d_skip):
    q, nh = xs.shape[0], SSD_HEADS_PER_GROUP
    causal, _, _ = _tri_masks(q)
    a_row = -jnp.exp(a_log)
    cs_all = exact_dot(causal.astype(F32), dt * a_row, "nn", "a")
    cs_t = cs_all.T
    lanes = range(group * nh, (group + 1) * nh)
    cs = jnp.stack([cs_all[:, hl:hl + 1] for hl in lanes])
    cs_row = jnp.stack([cs_t[hl:hl + 1, :] for hl in lanes])
    dt_h = jnp.stack([dt[:, hl:hl + 1] for hl in lanes])
    d_h = jnp.stack([d_skip[:, hl:hl + 1] for hl in lanes])
    x = _stack_lanes(xs, nh)
    h0s = _stack_rows(h0, nh)
    lmat = jnp.where(causal, jnp.exp(jnp.where(causal, cs - cs_row, 0.0)), 0.0)
    cb = bdot(cm, bm, "nt")
    xdt = x * dt_h
    cl = cs[:, q - 1:q, :]
    cm_b = jnp.broadcast_to(cm, (nh,) + cm.shape)
    bm_b = jnp.broadcast_to(bm, (nh,) + bm.shape)
    y = bdot(cb * lmat, xdt, "nn") + bdot(cm_b, h0s, "nt") * jnp.exp(cs) + x * d_h
    h1 = h0s * jnp.exp(cl) + bdot(xdt * jnp.exp(cl - cs), bm_b, "tn")
    return jnp.concatenate([y[e] for e in range(nh)], axis=-1), jnp.concatenate([h1[e] for e in range(nh)], axis=0)


class Rows:
    def __init__(self, arr, w=None, cb=0, shifts=()):
        self.arr, self.w, self.cb, self.shifts = arr, (arr.shape[1] if w is None else w), cb, tuple(shifts)


def _as_rows(x):
    return x if isinstance(x, Rows) else Rows(x)


def _shift_down(x, halo, k):
    rolled = pltpu.roll(x, k, 0)
    first = rolled[0:SUBLANES]
    rid = lax.broadcasted_iota(jnp.int32, first.shape, 0)
    patched = jnp.where(rid < k, pltpu.roll(halo, k, 0), first)
    return jnp.concatenate([patched, rolled[SUBLANES:]], axis=0)


def _shift_up(g, carry, k):
    tm = g.shape[0]
    rolled = pltpu.roll(g, tm - k, 0)
    last = rolled[tm - SUBLANES:]
    rid = lax.broadcasted_iota(jnp.int32, last.shape, 0)
    patched = jnp.where(rid >= SUBLANES - k, pltpu.roll(carry, SUBLANES - k, 0), last)
    return jnp.concatenate([rolled[:tm - SUBLANES], patched], axis=0)


def _params():
    return pltpu.CompilerParams(vmem_limit_bytes=VMEM_LIMIT)


def _load_rows(refs, pos, rins, first_block):
    vals = []
    for r in rins:
        x = refs[pos][...].astype(F32) if refs[pos].dtype != F32 else refs[pos][...]
        pos += 1
        vals.append(x)
        if r.shifts:
            halo = refs[pos][...]
            pos += 1
            halo = jnp.where(first_block, jnp.zeros_like(halo), halo)
            for k in r.shifts:
                vals.append(_shift_down(x, halo, k))
    return vals, pos


def _row_specs(rins, tm, blk):
    specs, args = [], []
    for r in rins:
        specs.append(pl.BlockSpec((tm, r.w), lambda i, cb=r.cb: (blk(i), cb)))
        args.append(r.arr)
        if r.shifts:
            per = tm // SUBLANES
            specs.append(pl.BlockSpec((SUBLANES, r.w), lambda i, cb=r.cb: (jnp.maximum(blk(i) * per - 1, 0), cb)))
            args.append(r.arr)
    return specs, args


def rowwise_fwd(name, fn, rins, params, outs, tm, deps=(), into=None):
    rins = [_as_rows(r) for r in rins]
    t = rins[0].arr.shape[0]
    tm = min(tm, t)
    nb = t // tm
    specs, args = _row_specs(rins, tm, lambda i: i)
    for p in params:
        specs.append(pl.BlockSpec(p.shape, lambda i: (0, 0)))
        args.append(p)
    for dep in deps:
        specs.append(pl.BlockSpec(memory_space=pl.ANY))
        args.append(dep)
    out_specs = [pl.BlockSpec((tm, w), lambda i: (i, 0)) for w, _ in outs]
    out_shape = [jax.ShapeDtypeStruct((t, w), dt) for w, dt in outs]
    aliases = {}
    if into is not None:
        target, total_width, col_block = into
        out_specs = [pl.BlockSpec((tm, outs[0][0]), lambda i: (i, col_block))]
        out_shape = [jax.ShapeDtypeStruct((t, total_width), outs[0][1])]
        if target is not None:
            aliases = {len(args): 0}
            specs.append(pl.BlockSpec(memory_space=pl.ANY))
            args.append(target)
    n_in = len(args)

    def body(*refs):
        vals, pos = _load_rows(refs, 0, rins, pl.program_id(0) == 0)
        pv = [refs[pos + j][...] for j in range(len(params))]
        res = fn(*vals, *pv)
        for o_ref, o in zip(refs[n_in:], res):
            o_ref[...] = o.astype(o_ref.dtype)

    return pl.pallas_call(
        body, name=name, grid=(nb,), in_specs=specs, out_specs=out_specs, out_shape=out_shape,
        input_output_aliases=aliases, compiler_params=_params(),
    )(*args)


def rowwise_bwd(name, fn, rins, params, cts, tm, grad_dtypes, row_add=None):
    rins = [_as_rows(r) for r in rins]
    cts = [[_as_rows(c) for c in lst] for lst in cts]
    row_add = [_as_rows(a) for a in (row_add or [])]
    t = rins[0].arr.shape[0]
    tm = min(tm, t)
    nb = t // tm
    rev = lambda i: nb - 1 - i
    specs, args = _row_specs(rins, tm, rev)
    for p in params:
        specs.append(pl.BlockSpec(p.shape, lambda i: (0, 0)))
        args.append(p)
    flat_cts = [c for lst in cts for c in lst] + row_add
    for c in flat_cts:
        specs.append(pl.BlockSpec((tm, c.w), lambda i, cb=c.cb: (rev(i), cb)))
        args.append(c.arr)
    n_in = len(args)
    as_tuple = lambda d: () if d is None else (d if isinstance(d, tuple) else (d,))
    want = [(i, dt) for i, d in enumerate(grad_dtypes) for dt in as_tuple(d)]
    out_specs = [pl.BlockSpec((tm, rins[i].w), lambda i_: (rev(i_), 0)) for i, _ in want]
    out_shape = [jax.ShapeDtypeStruct((t, rins[i].w), dt) for i, dt in want]
    out_specs += [pl.BlockSpec(p.shape, lambda i: (0, 0)) for p in params]
    out_shape += [jax.ShapeDtypeStruct(p.shape, F32) for p in params]
    n_out = len(out_shape)
    scratch = [pltpu.VMEM((SUBLANES, r.w), F32) for r in rins for _ in r.shifts]

    def body(*refs):
        i = pl.program_id(0)
        vals, pos = _load_rows(refs, 0, rins, rev(i) == 0)
        pv = [refs[pos + j][...] for j in range(len(params))]
        pos += len(params)
        outs, vjp = jax.vjp(fn, *vals, *pv)
        ct_vals = []
        for o, lst in zip(outs, cts):
            acc = None
            for _ in lst:
                cv = refs[pos][...].astype(F32)
                pos += 1
                acc = cv if acc is None else acc + cv
            ct_vals.append(acc.astype(o.dtype))
        adds = [refs[pos + j][...].astype(F32) for j in range(len(row_add))]
        grads = vjp(tuple(ct_vals))
        out_refs = refs[n_in:n_in + n_out]
        carry_refs = refs[n_in + n_out:]

        @pl.when(i == 0)
        def _():
            for cr in carry_refs:
                cr[...] = jnp.zeros_like(cr)
            for pr in out_refs[len(want):]:
                pr[...] = jnp.zeros_like(pr)

        gi, ci, oi = 0, 0, 0
        for idx, r in enumerate(rins):
            d = grads[gi]
            gi += 1
            for k in r.shifts:
                dk = grads[gi]
                gi += 1
                d = d + _shift_up(dk, carry_refs[ci][...], k)
                carry_refs[ci][...] = dk[0:SUBLANES]
                ci += 1
            if idx == 0:
                for a in adds:
                    d = d + a
            for _ in as_tuple(grad_dtypes[idx]):
                out_refs[oi][...] = d.astype(out_refs[oi].dtype)
                oi += 1
        for pr, gp in zip(out_refs[len(want):], grads[gi:]):
            pr[...] += gp

    res = pl.pallas_call(
        body, name=name, grid=(nb,), in_specs=specs, out_specs=out_specs, out_shape=out_shape,
        scratch_shapes=scratch, compiler_params=_params(),
    )(*args)
    return res[:len(want)], res[len(want):]


def _pick(n, pref):
    for c in pref:
        if n % c == 0:
            return c
    return n


MM_VMEM_BUDGET = 40 * 1024 * 1024
MM_PEAK_FLOPS = 0.9e15
MM_HBM_BYTES_PER_S = 3.0e12
MM_STEP_SECONDS = 0.35e-6


def _mm_tiles(m, n, k, size_a, size_b, size_out, size_res, single_k):
    best = None
    for tk in sorted({c for c in (k, 2048, 1024, 512, 256, 128) if c <= 2048 and k % c == 0}, reverse=True):
        for tm in sorted({c for c in (m, 1024, 512, 256, 128) if c <= 1024 and m % c == 0}, reverse=True):
            for tn in sorted({c for c in (n, 2048, 1536, 1024, 768, 512, 384, 256, 128) if c <= 2048 and n % c == 0},
                             reverse=True):
                nk = k // tk
                vmem = 2 * (tm * tk * size_a + tk * tn * size_b + tm * tn * (size_out + size_res))
                vmem += tm * tn * 4 * (2 if nk > 1 or not single_k else 1)
                vmem += (tm * tk * 2 if size_a > 2 else 0) + (tk * tn * 2 if size_b > 2 else 0)
                if vmem > MM_VMEM_BUDGET:
                    continue
                steps = (m // tm) * (n // tn) * nk
                a_reads = 1 if (nk == 1 and single_k) else n // tn
                traffic = m * k * size_a * a_reads + k * n * size_b * (m // tm) + m * n * (size_out + size_res)
                cost = max(2.0 * m * n * k / MM_PEAK_FLOPS, traffic / MM_HBM_BYTES_PER_S) + steps * MM_STEP_SECONDS
                if best is None or cost < best[0]:
                    best = (cost, tm, tn, tk)
    return best[1:]


def mm(name, a, b, mode, out_dtype=F32, res=None, b_slabs=None, out_slabs=None, dep=None, epi=None, extras=(),
       out_dtypes=None):
    if mode == "tn":
        k_dim, m_dim = a.shape
    else:
        m_dim, k_dim = a.shape
    if b_slabs:
        n_dim = b.shape[0] * b.shape[2] if mode == "nn" else b.shape[1]
    else:
        n_dim = b.shape[0] if mode == "nt" else b.shape[1]
    n_slabs = out_slabs or (b_slabs if (b_slabs and mode == "nn") else 1)
    k_slabs = b_slabs if (b_slabs and mode == "nt") else 1
    if epi is None:
        out_dtypes = [out_dtype]
        if res is None:
            epi = lambda acc: (acc,)
        else:
            extras, epi = [res], lambda acc, r: (acc + r,)
    tm, tn, tk = _mm_tiles(m_dim, n_dim // n_slabs, k_dim // k_slabs, a.dtype.itemsize, b.dtype.itemsize,
                           sum(jnp.dtype(dt).itemsize for dt in out_dtypes), sum(e.dtype.itemsize for e in extras),
                           single_k=(k_slabs == 1))
    nji = n_dim // n_slabs // tn
    nki = k_dim // k_slabs // tk
    nblk = lambda js, j: js * nji + j
    kblk = lambda ks, k: ks * nki + k
    if mode == "tn":
        a_spec = pl.BlockSpec((tk, tm), lambda i, js, j, ks, k: (kblk(ks, k), i))
    else:
        a_spec = pl.BlockSpec((tm, tk), lambda i, js, j, ks, k: (i, kblk(ks, k)))
    if b_slabs and mode == "nn":
        b_spec = pl.BlockSpec((None, tk, tn), lambda i, js, j, ks, k: (js, k, j))
    elif b_slabs and mode == "nt":
        b_spec = pl.BlockSpec((None, tn, tk), lambda i, js, j, ks, k: (ks, nblk(js, j), k))
    elif mode == "nt":
        b_spec = pl.BlockSpec((tn, tk), lambda i, js, j, ks, k: (nblk(js, j), kblk(ks, k)))
    else:
        b_spec = pl.BlockSpec((tk, tn), lambda i, js, j, ks, k: (kblk(ks, k), nblk(js, j)))
    specs, args = [a_spec, b_spec], [a, b]
    for e in extras:
        specs.append(pl.BlockSpec((tm, tn), lambda i, js, j, ks, k: (i, nblk(js, j))))
        args.append(e)
    if dep is not None:
        specs.append(pl.BlockSpec(memory_space=pl.ANY))
        args.append(dep)
    if out_slabs:
        o_specs = [pl.BlockSpec((None, tm, tn), lambda i, js, j, ks, k: (js, i, j))]
        o_shapes = [jax.ShapeDtypeStruct((out_slabs, m_dim, n_dim // out_slabs), out_dtypes[0])]
    else:
        o_specs = [pl.BlockSpec((tm, tn), lambda i, js, j, ks, k: (i, nblk(js, j))) for _ in out_dtypes]
        o_shapes = [jax.ShapeDtypeStruct((m_dim, n_dim), dt) for dt in out_dtypes]

    one_k_step = k_slabs * nki == 1
    n_in, n_out = len(args), len(out_dtypes)

    def body(*refs):
        a_ref, b_ref = refs[0], refs[1]
        part = _dg(a_ref[...].astype(BF16), b_ref[...].astype(BF16), mode)

        def finish(acc):
            outs = epi(acc, *[refs[2 + j][...].astype(F32) for j in range(len(extras))])
            for o_ref, o in zip(refs[n_in:n_in + n_out], outs):
                o_ref[...] = o.astype(o_ref.dtype)

        if one_k_step:
            finish(part)
            return
        acc_ref = refs[n_in + n_out]
        ks, kk = pl.program_id(3), pl.program_id(4)

        @pl.when((ks == 0) & (kk == 0))
        def _():
            acc_ref[...] = part

        @pl.when((ks > 0) | (kk > 0))
        def _():
            acc_ref[...] += part

        pl.when((ks == k_slabs - 1) & (kk == nki - 1))(lambda: finish(acc_ref[...]))

    grid = (m_dim // tm, n_slabs, nji, k_slabs, nki)
    scratch = [] if one_k_step else [pltpu.VMEM((tm, tn), F32)]
    out = pl.pallas_call(
        body, name=name, grid=grid, in_specs=specs, out_specs=o_specs, out_shape=o_shapes, scratch_shapes=scratch,
        compiler_params=pltpu.CompilerParams(
            dimension_semantics=("parallel", "parallel", "parallel", "arbitrary", "arbitrary"),
            vmem_limit_bytes=VMEM_LIMIT),
    )(*args)
    return out[0] if n_out == 1 else out


def _stack_lanes(x, n):
    w = x.shape[1] // n
    return jnp.stack([x[:, i * w:(i + 1) * w] for i in range(n)])


def _stack_rows(x, n):
    w = x.shape[0] // n
    return jnp.stack([x[i * w:(i + 1) * w, :] for i in range(n)])


def rwkv_scan_fwd(r, lw, k, v, kap, b):
    t = r.shape[0]
    c, hps, hd = min(RWKV_CHUNK, t), RWKV_HEADS_PER_STEP, RWKV_HEAD_DIM
    nc, ng, wl = t // c, RWKV_HEADS // hps, hps * hd
    spec = pl.BlockSpec((c, wl), lambda g, ci: (ci, g))

    def body(r_ref, lw_ref, k_ref, v_ref, kap_ref, b_ref, y_ref, ck_ref, inv_ref, st_ref):
        @pl.when(pl.program_id(1) == 0)
        def _():
            st_ref[...] = jnp.zeros_like(st_ref)

        st = st_ref[...]
        ck_ref[...] = st
        ins = [x[...] for x in (r_ref, lw_ref, k_ref, v_ref, kap_ref, b_ref)]
        y, st1, inv = rwkv_chunk_fn(_stack_rows(st, hps), *[_stack_lanes(x, hps) for x in ins])
        y_ref[...] = jnp.concatenate([y[h] for h in range(hps)], axis=-1)
        st_ref[...] = jnp.concatenate([st1[h] for h in range(hps)], axis=0)
        inv_ref[...] = jnp.concatenate([inv[h] for h in range(hps)], axis=0)

    return pl.pallas_call(
        body, name="rwkv_scan_fwd", grid=(ng, nc), in_specs=[spec] * 6,
        out_specs=[spec, pl.BlockSpec((None, wl, hd), lambda g, ci: (ci, g, 0)),
                   pl.BlockSpec((None, hps * c, c), lambda g, ci: (ci, g, 0))],
        out_shape=[jax.ShapeDtypeStruct((t, RWKV_WIDTH), F32), jax.ShapeDtypeStruct((nc, RWKV_WIDTH, hd), F32),
                   jax.ShapeDtypeStruct((nc, RWKV_HEADS * c, c), F32)],
        scratch_shapes=[pltpu.VMEM((wl, hd), F32)], compiler_params=_params(),
    )(r, lw, k, v, kap, b)


def rwkv_scan_bwd(r, lw, k, v, kap, b, ck, inv_ck, dy):
    t = r.shape[0]
    c, hps, hd = min(RWKV_CHUNK, t), RWKV_HEADS_PER_STEP, RWKV_HEAD_DIM
    nc, ng, wl = t // c, RWKV_HEADS // hps, hps * hd
    spec = pl.BlockSpec((c, wl), lambda g, ci: (nc - 1 - ci, g))

    def body(r_ref, lw_ref, k_ref, v_ref, kap_ref, b_ref, ck_ref, inv_ref, dy_ref, *rest):
        out_refs, dst_ref = rest[:6], rest[6]

        @pl.when(pl.program_id(1) == 0)
        def _():
            dst_ref[...] = jnp.zeros_like(dst_ref)

        ins = [x[...] for x in (r_ref, lw_ref, k_ref, v_ref, kap_ref, b_ref)]
        dyv, ck, dst = dy_ref[...].astype(F32), ck_ref[...], dst_ref[...]
        chunk = lambda *a: rwkv_chunk_fn(*a, inv=_stack_rows(inv_ref[...], hps))[:2]
        _, vjp = jax.vjp(chunk, _stack_rows(ck, hps), *[_stack_lanes(x, hps) for x in ins])
        grads = vjp((_stack_lanes(dyv, hps), _stack_rows(dst, hps)))
        dst_ref[...] = jnp.concatenate([grads[0][h] for h in range(hps)], axis=0)
        for j in range(6):
            out_refs[j][...] = jnp.concatenate([grads[1 + j][h] for h in range(hps)], axis=-1).astype(BF16)

    return pl.pallas_call(
        body, name="rwkv_scan_bwd", grid=(ng, nc),
        in_specs=[spec] * 6 + [pl.BlockSpec((None, wl, hd), lambda g, ci: (nc - 1 - ci, g, 0)),
                               pl.BlockSpec((None, hps * c, c), lambda g, ci: (nc - 1 - ci, g, 0)), spec],
        out_specs=[spec] * 6, out_shape=[jax.ShapeDtypeStruct((t, RWKV_WIDTH), BF16)] * 6,
        scratch_shapes=[pltpu.VMEM((wl, hd), F32)], compiler_params=_params(),
    )(r, lw, k, v, kap, b, ck, inv_ck, dy)


def _ssd_specs(q, blk):
    gw = SSD_WIDTH // 2
    return [pl.BlockSpec((q, gw), lambda g, ci: (blk(ci), g)),
            pl.BlockSpec((q, SSD_STATE), lambda g, ci: (blk(ci), g)),
            pl.BlockSpec((q, SSD_STATE), lambda g, ci: (blk(ci), g)),
            pl.BlockSpec((q, LANES), lambda g, ci: (blk(ci), 0)),
            pl.BlockSpec((1, LANES), lambda g, ci: (0, 0)),
            pl.BlockSpec((1, LANES), lambda g, ci: (0, 0))]


def ssd_scan_fwd(xs, bm, cm, dt, a_log, d_skip):
    t = xs.shape[0]
    q = min(SSD_CHUNK, t)
    nc, gw = t // q, SSD_WIDTH // 2

    def body(xs_ref, bm_ref, cm_ref, dt_ref, al_ref, d_ref, y_ref, ck_ref, h_ref):
        @pl.when(pl.program_id(1) == 0)
        def _():
            h_ref[...] = jnp.zeros_like(h_ref)

        ck_ref[...] = h_ref[...]
        args = (h_ref[...], xs_ref[...], bm_ref[...], cm_ref[...], dt_ref[...], al_ref[...], d_ref[...])
        g = pl.program_id(0)

        @pl.when(g == 0)
        def _():
            y, h1 = ssd_chunk_fn(0, *args)
            y_ref[...] = y
            h_ref[...] = h1

        @pl.when(g == 1)
        def _():
            y, h1 = ssd_chunk_fn(1, *args)
            y_ref[...] = y
            h_ref[...] = h1

    return pl.pallas_call(
        body, name="ssd_scan_fwd", grid=(2, nc), in_specs=_ssd_specs(q, lambda ci: ci),
        out_specs=[pl.BlockSpec((q, gw), lambda g, ci: (ci, g)),
                   pl.BlockSpec((None, gw, SSD_STATE), lambda g, ci: (ci, g, 0))],
        out_shape=[jax.ShapeDtypeStruct((t, SSD_WIDTH), F32), jax.ShapeDtypeStruct((nc, SSD_WIDTH, SSD_STATE), F32)],
        scratch_shapes=[pltpu.VMEM((gw, SSD_STATE), F32)], compiler_params=_params(),
    )(xs, bm, cm, dt, a_log, d_skip)


def ssd_scan_bwd(xs, bm, cm, dt, a_log, d_skip, ck, dy):
    t = xs.shape[0]
    q = min(SSD_CHUNK, t)
    nc, gw = t // q, SSD_WIDTH // 2
    rev = lambda ci: nc - 1 - ci

    def body(xs_ref, bm_ref, cm_ref, dt_ref, al_ref, d_ref, ck_ref, dy_ref,
             dxs_ref, dbm_ref, dcm_ref, ddt_ref, dal_ref, dd_ref, dh_ref):
        g, ci = pl.program_id(0), pl.program_id(1)

        @pl.when(ci == 0)
        def _():
            dh_ref[...] = jnp.zeros_like(dh_ref)

        @pl.when((ci == 0) & (g == 0))
        def _():
            dal_ref[...] = jnp.zeros_like(dal_ref)
            dd_ref[...] = jnp.zeros_like(dd_ref)

        args = (ck_ref[...], xs_ref[...], bm_ref[...], cm_ref[...], dt_ref[...], al_ref[...], d_ref[...])

        def run(group):
            _, vjp = jax.vjp(functools.partial(ssd_chunk_fn, group), *args)
            dh0, dxs, dbm, dcm, ddt, dal, dd = vjp((dy_ref[...].astype(F32), dh_ref[...]))
            dh_ref[...] = dh0
            dxs_ref[...] = dxs.astype(BF16)
            dbm_ref[...] = dbm.astype(BF16)
            dcm_ref[...] = dcm.astype(BF16)
            ddt_ref[...] = ddt
            dal_ref[...] += dal
            dd_ref[...] += dd

        pl.when(g == 0)(lambda: run(0))
        pl.when(g == 1)(lambda: run(1))

    in_specs = _ssd_specs(q, rev) + [pl.BlockSpec((None, gw, SSD_STATE), lambda g, ci: (rev(ci), g, 0)),
                                     pl.BlockSpec((q, gw), lambda g, ci: (rev(ci), g))]
    return pl.pallas_call(
        body, name="ssd_scan_bwd", grid=(2, nc), in_specs=in_specs,
        out_specs=[pl.BlockSpec((q, gw), lambda g, ci: (rev(ci), g)),
                   pl.BlockSpec((q, SSD_STATE), lambda g, ci: (rev(ci), g)),
                   pl.BlockSpec((q, SSD_STATE), lambda g, ci: (rev(ci), g)),
                   pl.BlockSpec((None, q, LANES), lambda g, ci: (g, rev(ci), 0)),
                   pl.BlockSpec((1, LANES), lambda g, ci: (0, 0)),
                   pl.BlockSpec((1, LANES), lambda g, ci: (0, 0))],
        out_shape=[jax.ShapeDtypeStruct((t, SSD_WIDTH), BF16), jax.ShapeDtypeStruct((t, 2 * SSD_STATE), BF16),
                   jax.ShapeDtypeStruct((t, 2 * SSD_STATE), BF16), jax.ShapeDtypeStruct((2, t, LANES), F32),
                   jax.ShapeDtypeStruct((1, LANES), F32), jax.ShapeDtypeStruct((1, LANES), F32)],
        scratch_shapes=[pltpu.VMEM((gw, SSD_STATE), F32)], compiler_params=_params(),
    )(xs, bm, cm, dt, a_log, d_skip, ck, dy)


def loss_and_grad(x, tgt, g, tm):
    t, d = x.shape
    tm = min(tm, t)
    nb = t // tm

    def body(x_ref, t_ref, g_ref, loss_ref, dx_ref, dxb_ref, dg_ref):
        @pl.when(pl.program_id(0) == 0)
        def _():
            loss_ref[...] = jnp.zeros_like(loss_ref)
            dg_ref[...] = jnp.zeros_like(dg_ref)

        val, vjp = jax.vjp(loss_fn, x_ref[...], t_ref[...], g_ref[...])
        dx, _, dg = vjp(jnp.ones((1, 1), F32))
        loss_ref[...] += jnp.broadcast_to(val, loss_ref.shape)
        dx_ref[...] = dx
        dxb_ref[...] = dx.astype(BF16)
        dg_ref[...] += dg

    row = pl.BlockSpec((tm, d), lambda i: (i, 0))
    one = pl.BlockSpec((1, d), lambda i: (0, 0))
    return pl.pallas_call(
        body, name="loss_and_grad", grid=(nb,), in_specs=[row, row, one],
        out_specs=[pl.BlockSpec((SUBLANES, LANES), lambda i: (0, 0)), row, row, one],
        out_shape=[jax.ShapeDtypeStruct((SUBLANES, LANES), F32), jax.ShapeDtypeStruct((t, d), F32),
                   jax.ShapeDtypeStruct((t, d), BF16), jax.ShapeDtypeStruct((1, d), F32)],
        compiler_params=_params(),
    )(x, tgt, g)


def adamw(name, recv, w, m, v, dep=None):
    rows, cols = w.shape
    n_slabs = recv.shape[0]
    recv_block_bytes = 8 * 1024 * 1024
    tm = _pick(rows, [c for c in (256, 128, 64, 32, 16, 8)
                      if n_slabs * c * cols * recv.dtype.itemsize <= recv_block_bytes])
    c1 = 1.0 / (1.0 - ADAM_B1 ** ADAM_STEP)
    c2 = 1.0 / (1.0 - ADAM_B2 ** ADAM_STEP)

    n_dep = 0 if dep is None else 1

    def body(recv_ref, w_ref, m_ref, v_ref, *rest):
        g_ref, d_ref, nm_ref, nv_ref = rest[n_dep:]
        g = recv_ref[0].astype(F32)
        for p in range(1, n_slabs):
            g = g + recv_ref[p].astype(F32)
        nm =ADAM_B1 * m_ref[...] + (1.0 - ADAM_B1) * g
        nv = ADAM_B2 * v_ref[...] + (1.0 - ADAM_B2) * jnp.square(g)
        g_ref[...] = g
        nm_ref[...] = nm
        nv_ref[...] = nv
        d_ref[...] = -ADAM_LR * ((nm * c1) / (jnp.sqrt(nv * c2) + ADAM_EPS) + ADAM_WD * w_ref[...])

    blk = pl.BlockSpec((tm, cols), lambda i: (i, 0))
    return pl.pallas_call(
        body, name=name, grid=(rows // tm,),
        in_specs=[pl.BlockSpec((n_slabs, tm, cols), lambda i: (0, i, 0)), blk, blk, blk]
        + [pl.BlockSpec(memory_space=pl.ANY)] * n_dep,
        out_specs=[blk] * 4, out_shape=[jax.ShapeDtypeStruct((rows, cols), F32)] * 4,
        compiler_params=_params(),
    )(recv, w, m, v, *([] if dep is None else [dep]))


def _mesh_pos():
    return lax.axis_index("x"), lax.axis_index("y"), lax.axis_index("c")


def _peer(pos, mask):
    x, y, c = pos
    return (1 - x if mask & 4 else x, 1 - y if mask & 2 else y, 1 - c if mask & 1 else c)


def _linear(pos):
    return 4 * pos[0] + 2 * pos[1] + pos[2]


class Exchange:
    MASKS = {"gather": (1, 2, 3, 4, 5, 6, 7), "scatter": (1, 2, 3, 4, 5, 6, 7), "gather_chips": (1, 2, 4, 6),
             "forward": (2, 4, 6)}

    def __init__(self, xs, kind, lands=None):
        self.kind, self.masks = kind, self.MASKS[kind]
        self.xs = [] if kind == "forward" else list(xs)
        if kind == "forward":
            self.land_shape = [jax.ShapeDtypeStruct(l.shape, l.dtype) for l in lands]
        elif kind == "scatter":
            self.land_shape = [jax.ShapeDtypeStruct(x.shape, x.dtype) for x in xs]
        else:
            self.land_shape = [jax.ShapeDtypeStruct((N_DEV,) + x.shape, x.dtype) for x in xs]
        self.n = len(self.land_shape)
        copies = self.n * len(self.masks)
        self.sems = [pltpu.SemaphoreType.DMA((copies,)), pltpu.SemaphoreType.DMA((copies,)),
                     pltpu.SemaphoreType.DMA((self.n,))]

    def _copies(self, ins, outs, sems, landing):
        send_sems, recv_sems, local_sems = sems
        me = _mesh_pos()
        me_lin = _linear(me)
        local, remote = [], []
        for ti in range(self.n):
            if self.kind != "forward":
                src_mine = ins[ti].at[me_lin] if self.kind == "scatter" else ins[ti]
                local.append(pltpu.make_async_copy(src_mine, outs[ti].at[me_lin], local_sems.at[ti]))
            for j, mask in enumerate(self.masks):
                if self.kind == "forward":
                    peer = _peer(me, 1)
                    src = outs[ti].at[_linear(_peer(me, mask))]
                    dst = outs[ti].at[_linear(_peer(me, mask ^ 1 if landing else mask))]
                else:
                    peer = _peer(me, mask)
                    src = ins[ti].at[_linear(peer)] if self.kind == "scatter" else ins[ti]
                    dst = outs[ti].at[_linear(peer) if landing else me_lin]
                sem_index = ti * len(self.masks) + j
                remote.append(pltpu.make_async_remote_copy(
                    src_ref=src, dst_ref=dst, send_sem=send_sems.at[sem_index], recv_sem=recv_sems.at[sem_index],
                    device_id=peer, device_id_type=pl.DeviceIdType.MESH))
        return local, remote

    def start(self, ins, outs, sems):
        local, remote = self._copies(ins, outs, sems, landing=False)
        for cp in local + remote:
            cp.start()

    def finish(self, ins, outs, sems):
        local, remote = self._copies(ins, outs, sems, landing=True)
        for cp in remote:
            cp.wait_recv()
        for cp in remote:
            cp.wait_send()
        for cp in local:
            cp.wait()


def exchange_start(name, xs, kind, dep=None, lands=None):
    ex = Exchange(xs, kind, lands)
    hbm = pl.BlockSpec(memory_space=pltpu.HBM)
    sem = pl.BlockSpec(memory_space=pltpu.SEMAPHORE)
    if lands is None:
        lands = [lax.empty(s.shape, s.dtype) for s in ex.land_shape]
    n_src, n = len(ex.xs), ex.n
    n_inputs = n_src + n + (0 if dep is None else 1)

    def body(*refs):
        ins, lnd, sems, token = refs[:n_src], refs[n_src:n_src + n], refs[n_inputs:n_inputs + 3], refs[-1]
        ex.start(ins, lnd, sems)
        token[...] = jnp.zeros_like(token)

    res = pl.pallas_call(
        body, name=name, in_specs=[hbm] * (n_src + n) + ([] if dep is None else [pl.BlockSpec(memory_space=pl.ANY)]),
        out_specs=[sem] * 3 + [hbm] * (n_src + n) + [pl.BlockSpec(memory_space=pltpu.VMEM)],
        out_shape=ex.sems + [pltpu.HBM(x.shape, x.dtype) for x in ex.xs]
        + [pltpu.HBM(s.shape, s.dtype) for s in ex.land_shape] + [jax.ShapeDtypeStruct((SUBLANES, LANES), F32)],
        input_output_aliases={i: 3 + i for i in range(n_src + n)},
        compiler_params=pltpu.CompilerParams(has_side_effects=pltpu.SideEffectType.DATAFLOW_SIDE_EFFECTING),
    )(*[pltpu.with_memory_space_constraint(x, pltpu.HBM) for x in ex.xs + list(lands)],
      *([] if dep is None else [dep]))
    return (ex, res[:3], res[3:3 + n_src], res[3 + n_src:3 + n_src + n]), res[-1]


def exchange_wait(name, handles, after):
    ex, sems, srcs, lands = handles
    n_src, n = len(srcs), len(lands)
    hbm = pl.BlockSpec(memory_space=pltpu.HBM)
    sem = pl.BlockSpec(memory_space=pltpu.SEMAPHORE)

    def body(*refs):
        ins, lnd, sem_refs = refs[:n_src], refs[n_src:n_src + n], refs[n_src + n:n_src + n + 3]
        ex.finish(ins, lnd, sem_refs)

    res = pl.pallas_call(
        body, name=name, in_specs=[hbm] * (n_src + n) + [sem] * 3 + [pl.BlockSpec(memory_space=pl.ANY)],
        out_specs=[hbm] * (n_src + n),
        out_shape=[pltpu.HBM(x.shape, x.dtype) for x in srcs] + [pltpu.HBM(x.shape, x.dtype) for x in lands],
        input_output_aliases={i: i for i in range(n_src + n)},
        compiler_params=pltpu.CompilerParams(has_side_effects=pltpu.SideEffectType.DATAFLOW_SIDE_EFFECTING),
    )(*srcs, *lands, *sems, after)
    return res[n_src:]


def forward_start(name, chip_gather, after):
    lands = exchange_wait(name + "_wait", chip_gather, after)
    return exchange_start(name + "_forward_start", [], "forward", lands=lands)


_Z = (0, 1024)
_XBC = (1024, 2560)
_DT = (2560, 2576)
_RKV = (2576, 5648)
_PW = (5648, 5744)
_PA = (5744, 5840)
_PG = (5840, 6096)
D_IN = 6096

_SMALL = ("norm_mix_g", "ssd_conv_b", "ssd_dt_bias", "ssd_a_log", "ssd_d", "ssd_norm_g", "rwkv_mu", "rwkv_w0",
          "rwkv_a0", "rwkv_k_k", "rwkv_k_a", "rwkv_r_k", "rwkv_ln_w", "rwkv_ln_b", "norm_x_g", "norm_mem_g",
          "norm_ffn_g", "final_norm_g")
_WEIGHTS = ("norm_mix_g", "w_in", "ssd_conv_w", "ssd_conv_b", "ssd_dt_bias", "ssd_a_log", "ssd_d", "ssd_norm_g",
            "rwkv_mu", "rwkv_w0", "rwkv_w2", "rwkv_a0", "rwkv_a2", "rwkv_g2", "rwkv_k_k", "rwkv_k_a", "rwkv_r_k",
            "rwkv_ln_w", "rwkv_ln_b", "w_out", "norm_x_g", "norm_mem_g", "xattn_wq", "xattn_wk", "xattn_wv",
            "xattn_wo", "norm_ffn_g", "ffn_w1", "ffn_w2", "final_norm_g")


def _pad_lanes(x, width=LANES):
    return jnp.pad(x, ((0, 0), (0, width - x.shape[1])))


def _pack_small(vals):
    flat = jnp.concatenate([vals[n].reshape(-1) for n in _SMALL])
    rows = -(-flat.shape[0] // (LANES * SUBLANES)) * SUBLANES
    return jnp.pad(flat, (0, rows * LANES - flat.shape[0])).reshape(rows, LANES)


def _unpack_small(packed, shapes):
    flat = packed.reshape(-1)
    out, pos = {}, 0
    for n in _SMALL:
        size = 1
        for s in shapes[n]:
            size *= s
        out[n] = flat[pos:pos + size].reshape(shapes[n])
        pos += size
    return out


def _rows(w, rng):
    return w[rng[0]:rng[1]]


def sum_slabs(name, recv):
    n, rows, cols = recv.shape
    tc = _pick(cols, (256, 128))

    def body(r_ref, o_ref):
        acc = r_ref[0].astype(F32)
        for p in range(1, n):
            acc = acc + r_ref[p].astype(F32)
        o_ref[...] = acc

    return pl.pallas_call(
        body, name=name, grid=(cols // tc,), in_specs=[pl.BlockSpec((n, rows, tc), lambda j: (0, 0, j))],
        out_specs=pl.BlockSpec((rows, tc), lambda j: (0, j)), out_shape=jax.ShapeDtypeStruct((rows, cols), F32),
        compiler_params=_params(),
    )(recv)


def kernel(x, mem, norm_mix_g, w_in, ssd_conv_w, ssd_conv_b, ssd_dt_bias, ssd_a_log, ssd_d, ssd_norm_g, rwkv_mu, rwkv_w0, rwkv_w2, rwkv_a0, rwkv_a2, rwkv_g2, rwkv_k_k, rwkv_k_a, rwkv_r_k, rwkv_ln_w, rwkv_ln_b, w_out, norm_x_g, norm_mem_g, xattn_wq, xattn_wk, xattn_wv, xattn_wo, norm_ffn_g, ffn_w1, ffn_w2, final_norm_g, loss_target, m_norm_mix_g, m_w_in, m_ssd_conv_w, m_ssd_conv_b, m_ssd_dt_bias, m_ssd_a_log, m_ssd_d, m_ssd_norm_g, m_rwkv_mu, m_rwkv_w0, m_rwkv_w2, m_rwkv_a0, m_rwkv_a2, m_rwkv_g2, m_rwkv_k_k, m_rwkv_k_a, m_rwkv_r_k, m_rwkv_ln_w, m_rwkv_ln_b, m_w_out, m_norm_x_g, m_norm_mem_g, m_xattn_wq, m_xattn_wk, m_xattn_wv, m_xattn_wo, m_norm_ffn_g, m_ffn_w1, m_ffn_w2, m_final_norm_g, v_norm_mix_g, v_w_in, v_ssd_conv_w, v_ssd_conv_b, v_ssd_dt_bias, v_ssd_a_log, v_ssd_d, v_ssd_norm_g, v_rwkv_mu, v_rwkv_w0, v_rwkv_w2, v_rwkv_a0, v_rwkv_a2, v_rwkv_g2, v_rwkv_k_k, v_rwkv_k_a, v_rwkv_r_k, v_rwkv_ln_w, v_rwkv_ln_b, v_w_out, v_norm_x_g, v_norm_mem_g, v_xattn_wq, v_xattn_wk, v_xattn_wv, v_xattn_wo, v_norm_ffn_g, v_ffn_w1, v_ffn_w2, v_final_norm_g):
    given = dict(locals())
    wts = {n: given[n] for n in _WEIGHTS}
    mom_m = {n: given["m_" + n] for n in _WEIGHTS}
    mom_v = {n: given["v_" + n] for n in _WEIGHTS}
    d = D_MODEL
    xt, memt, tgt = x[0], mem[0], loss_target[0]
    tm = 256
    tm_light = 512

    big = {"w_in": jnp.transpose(w_in[0]), "w_out": w_out[0], "xattn_wq": xattn_wq[0], "xattn_wk": xattn_wk[0],
           "xattn_wv": xattn_wv[0], "xattn_wo": xattn_wo[0], "ffn_w1": ffn_w1[0], "ffn_w2": ffn_w2[0]}
    small_sh = {"ssd_conv_w": ssd_conv_w.reshape(4, -1), "rwkv_w2": rwkv_w2[0], "rwkv_a2": rwkv_a2[0],
                "rwkv_g2": rwkv_g2[0]}
    cast_one = lambda n, deps=(): rowwise_fwd("cast_" + n, cast_fn, [big[n]], [], [(big[n].shape[1], BF16)],
                                              256 if big[n].shape[0] % 256 == 0 else big[n].shape[0], deps=deps)[0]
    gather_in, token_in = exchange_start("gather_in_start", [cast_one("w_in")] + list(small_sh.values()), "gather_chips")
    cast = {n: cast_one(n, deps=[token_in]) for n in big if n != "w_in"}
    late_a = ("w_out", "xattn_wq", "xattn_wk", "xattn_wv", "xattn_wo")
    late_b = ("ffn_w1", "ffn_w2")
    gather_a, token_a = exchange_start("gather_attn_start", [cast[n] for n in late_a], "gather_chips", dep=token_in)
    gather_b, token_b = exchange_start("gather_ffn_start", [cast[n] for n in late_b], "gather_chips", dep=token_a)
    (h1,) = rowwise_fwd("norm_mix", rmsnorm_fn, [xt], [norm_mix_g], [(d, BF16)], tm_light, deps=[token_b])
    forward_in, token_in = forward_start("gather_in", gather_in, after=h1)
    gathered = exchange_wait("gather_in_forward_wait", forward_in, after=token_in)
    g_big = {"w_in": gathered[0]}
    g_small = dict(zip(small_sh, gathered[1:]))

    pad_rows = lambda a: jnp.pad(a, ((0, LANES - a.shape[0]), (0, 0)))
    w_in_t = g_big["w_in"].reshape(D_IN, d)
    wt_z, wt_xbc, wt_rkv = (_rows(w_in_t, r) for r in (_Z, _XBC, _RKV))
    wt_ps = jnp.concatenate([_rows(w_in_t, _PG)] + [pad_rows(_rows(w_in_t, r)) for r in (_PW, _PA, _DT)], axis=0)
    unshard_cols = lambda g: jnp.transpose(g, (1, 0, 2)).reshape(g.shape[1], -1)
    conv_w_f = unshard_cols(g_small["ssd_conv_w"])
    w2p, a2p = pad_rows(unshard_cols(g_small["rwkv_w2"])), pad_rows(unshard_cols(g_small["rwkv_a2"]))
    g2_f = unshard_cols(g_small["rwkv_g2"])

    mu = rwkv_mu
    mu_rkv, mu_pg = mu[:, :3072], mu[:, 3264:3520]
    mu_pwa = jnp.concatenate([_pad_lanes(mu[:, 3072:3168]), _pad_lanes(mu[:, 3168:3264])], axis=1)
    dt_bias_p, a_log_p, d_p = _pad_lanes(ssd_dt_bias), _pad_lanes(ssd_a_log), _pad_lanes(ssd_d)
    r_k_row = rwkv_r_k.reshape(1, RWKV_WIDTH)
    g_final = final_norm_g.reshape(1, d)

    u_z = mm("in_z", h1, wt_z, "nt")
    u_xbc = mm("in_xbc", h1, wt_xbc, "nt")
    u_rkv = mm("in_rkv", h1, wt_rkv, "nt")
    u_ps = mm("in_narrow", h1, wt_ps, "nt")

    ssd_pre_rows = lambda: [Rows(u_xbc, shifts=(1, 2, 3)), Rows(u_ps, LANES, 4)]
    ssd_pre_params = [conv_w_f, ssd_conv_b, dt_bias_p]
    xs, bm, cm, dt = rowwise_fwd("ssd_pre", ssd_pre_fn, ssd_pre_rows(), ssd_pre_params,
                                 [(SSD_WIDTH, F32), (256, F32), (256, F32), (LANES, F32)], tm)
    y_scan, ssd_ck = ssd_scan_fwd(xs, bm, cm, dt, a_log_p, d_p)
    (y_ssd,) = rowwise_fwd("ssd_post", ssd_post_fn, [y_scan, u_z], [ssd_norm_g], [(SSD_WIDTH, BF16)], tm_light,
                           into=(None, d, 0))

    rwkv_pre_rows = lambda: [Rows(u_rkv, shifts=(1,)), Rows(u_ps, 2 * LANES, 0, shifts=(1,)), Rows(u_ps, 2 * LANES, 1, shifts=(1,))]
    rwkv_pre_params = [mu_rkv, mu_pg, mu_pwa, rwkv_w0, w2p, rwkv_a0, a2p, g2_f, rwkv_k_k, rwkv_k_a]
    forward_a, token_a = forward_start("gather_attn", gather_a, after=y_scan)
    r_, lw_, k_, v_, kap_, b_, gate_ = rowwise_fwd("rwkv_pre", rwkv_pre_fn, rwkv_pre_rows(), rwkv_pre_params,
                                                   [(RWKV_WIDTH, F32)] * 7, tm, deps=[token_a])
    ys_r, rwkv_ck, rwkv_inv = rwkv_scan_fwd(r_, lw_, k_, v_, kap_, b_)
    forward_b, token_b = forward_start("gather_ffn", gather_b, after=ys_r)
    g_big.update(zip(late_a, exchange_wait("gather_attn_forward_wait", forward_a, after=token_b)))
    w_out_f = g_big["w_out"].reshape(d, d)
    wq_f, wk_f, wv_f, wo_f = (g_big[n].reshape(d, d) for n in ("xattn_wq", "xattn_wk", "xattn_wv", "xattn_wo"))
    rwkv_post_params = [rwkv_ln_w, rwkv_ln_b, r_k_row]
    (ycat,) = rowwise_fwd("rwkv_post", rwkv_post_fn, [ys_r, r_, k_, v_, gate_], rwkv_post_params,
                          [(RWKV_WIDTH, BF16)], tm, into=(y_ssd, d, 1))
    x1 = mm("out_proj", ycat, w_out_f, "nn", res=xt)

    (h2,) = rowwise_fwd("norm_x", rmsnorm_fn, [x1], [norm_x_g], [(d, BF16)], tm_light)
    (mn,) = rowwise_fwd("norm_mem", rmsnorm_fn, [memt], [norm_mem_g], [(d, BF16)], tm)
    q = mm("xattn_q", h2, wq_f, "nn", out_dtype=BF16)
    kx = mm("xattn_k", mn, wk_f, "nn")
    vx = mm("xattn_v", mn, wv_f, "nn")
    (o,) = rowwise_fwd("xattn", attn_fn, [q], [kx, vx], [(d, BF16)], tm_light)
    x2 = mm("xattn_o", o, wo_f, "nn", res=x1)

    (h3,) = rowwise_fwd("norm_ffn", rmsnorm_fn, [x2], [norm_ffn_g], [(d, BF16)], tm_light)
    w1_s, w2_g = exchange_wait("gather_ffn_forward_wait", forward_b, after=h3)
    w2_f = w2_g.reshape(D_FF, d)
    relu2_epi = lambda acc: (jnp.square(jnp.maximum(acc, 0.0)), jnp.maximum(acc, 0.0))
    hid, relu_a = mm("ffn_1", h3, w1_s, "nn", b_slabs=N_DEV, epi=relu2_epi, out_dtypes=[BF16, BF16])
    x3 = mm("ffn_2", hid, w2_f, "nn", res=x2)

    loss_blk, dx3, dx3_b, dg_final = loss_and_grad(x3, tgt, g_final, tm_light)

    grads = {}
    grads["ffn_w2"] = mm("d_ffn_w2", hid, dx3_b, "tn", out_dtype=BF16).reshape(N_DEV, D_FF // N_DEV, d)
    sc_w2, tok = exchange_start("scatter_ffn_w2_start", [grads["ffn_w2"]], "scatter")
    da = mm("d_hid", dx3_b, w2_f, "nt", dep=tok, epi=lambda acc, ra: (2.0 * acc * ra,), extras=[relu_a],
            out_dtypes=[BF16])
    grads["ffn_w1"] = mm("d_ffn_w1", h3, da, "tn", out_dtype=BF16, out_slabs=N_DEV)
    sc_w1, tok = exchange_start("scatter_ffn_w1_start", [grads["ffn_w1"]], "scatter")
    dh3 = mm("d_h3", da, w1_s, "nt", out_dtype=BF16, b_slabs=N_DEV, dep=tok)
    (dx2, dx2_b), (dg_ffn,) = rowwise_bwd("norm_ffn_bwd", rmsnorm_fn, [x2], [norm_ffn_g], [[dh3]], tm_light,
                                          [(F32, BF16)], row_add=[dx3])

    grads["xattn_wo"] = mm("d_wo", o, dx2_b, "tn", out_dtype=BF16).reshape(N_DEV, d // N_DEV, d)
    sc_wo, tok = exchange_start("scatter_wo_start", [grads["xattn_wo"]], "scatter")
    d_o = mm("d_o", dx2_b, wo_f, "nt", out_dtype=BF16, dep=tok)
    (dq,), (dkx, dvx) = rowwise_bwd("xattn_bwd", attn_fn, [q], [kx, vx], [[d_o]], tm_light, [BF16])
    grads["xattn_wq"] = mm("d_wq", h2, dq, "tn", out_dtype=BF16).reshape(N_DEV, d // N_DEV, d)
    grads["xattn_wk"] = mm("d_wk", mn, dkx, "tn", out_dtype=BF16).reshape(N_DEV, d // N_DEV, d)
    grads["xattn_wv"] = mm("d_wv", mn, dvx, "tn", out_dtype=BF16).reshape(N_DEV, d // N_DEV, d)
    qkv = ("xattn_wq", "xattn_wk", "xattn_wv")
    sc_qkv, tok = exchange_start("scatter_qkv_start", [grads[n] for n in qkv], "scatter")
    dmn = mm("d_mn_v", dvx, wv_f, "nt", res=mm("d_mn_k", dkx, wk_f, "nt", dep=tok))
    _, (dg_mem,) = rowwise_bwd("norm_mem_bwd", rmsnorm_fn, [memt], [norm_mem_g], [[dmn]], tm, [None])
    dh2 = mm("d_h2", dq, wq_f, "nt", out_dtype=BF16, dep=dg_mem)
    (dx1, dx1_b), (dg_x,) = rowwise_bwd("norm_x_bwd", rmsnorm_fn, [x1], [norm_x_g], [[dh2]], tm_light,
                                        [(F32, BF16)], row_add=[dx2])

    grads["w_out"] = mm("d_w_out", ycat, dx1_b, "tn", out_dtype=BF16).reshape(N_DEV, d // N_DEV, d)
    sc_wout, tok = exchange_start("scatter_w_out_start", [grads["w_out"]], "scatter")
    d_ycat = mm("d_ycat", dx1_b, w_out_f, "nt", out_dtype=BF16, dep=tok)

    (d_ys, d_r1, d_k1, d_v1, d_gate), (dln_w, dln_b, dr_k) = rowwise_bwd(
        "rwkv_post_bwd", rwkv_post_fn, [ys_r, r_, k_, v_, gate_], rwkv_post_params,
        [[Rows(d_ycat, RWKV_WIDTH, 1)]], tm, [BF16] * 5)
    d_r2, d_lw, d_k2, d_v2, d_kap, d_b = rwkv_scan_bwd(r_, lw_, k_, v_, kap_, b_, rwkv_ck, rwkv_inv, d_ys)
    (du_rkv, du_pg, du_pwa), rwkv_pg = rowwise_bwd(
        "rwkv_pre_bwd", rwkv_pre_fn, rwkv_pre_rows(), rwkv_pre_params,
        [[d_r1, d_r2], [d_lw], [d_k1, d_k2], [d_v1, d_v2], [d_kap], [d_b], [d_gate]], tm, [BF16] * 3)
    dmu_rkv, dmu_pg, dmu_pwa, dw0, dw2p, da0, da2p, dg2, dk_k, dk_a = rwkv_pg

    (d_yscan, du_z), (dssd_norm_g,) = rowwise_bwd("ssd_post_bwd", ssd_post_fn, [y_scan, u_z], [ssd_norm_g],
                                                  [[Rows(d_ycat, SSD_WIDTH, 0)]], tm_light, [BF16, BF16])
    dxs, dbm, dcm, ddt2, da_log_p, dd_p = ssd_scan_bwd(xs, bm, cm, dt, a_log_p, d_p, ssd_ck, d_yscan)
    (du_xbc, du_dt), (dconv_w, dconv_b, ddt_bias_p) = rowwise_bwd(
        "ssd_pre_bwd", ssd_pre_fn, ssd_pre_rows(), ssd_pre_params,
        [[dxs], [dbm], [dcm], [ddt2[0], ddt2[1]]], tm, [BF16, BF16])
    du_ps = jnp.concatenate([du_pg, du_pwa, du_dt], axis=1)

    dwt_z = mm("d_w_z", du_z, h1, "tn", out_dtype=BF16)
    dwt_xbc = mm("d_w_xbc", du_xbc, h1, "tn", out_dtype=BF16)
    dwt_rkv = mm("d_w_rkv", du_rkv, h1, "tn", out_dtype=BF16)
    dwt_ps = mm("d_w_narrow", du_ps, h1, "tn", out_dtype=BF16)
    dwt_full = jnp.concatenate([dwt_z, dwt_xbc, dwt_ps[512:528], dwt_rkv, dwt_ps[256:352], dwt_ps[384:480], dwt_ps[0:256]],
                               axis=0)
    to_slabs = lambda g: jnp.transpose(g.reshape(g.shape[0], N_DEV, -1), (1, 0, 2))
    grads["w_in"] = dwt_full.reshape(N_DEV, D_IN // N_DEV, d)
    grads["ssd_conv_w"] = to_slabs(dconv_w)
    grads["rwkv_w2"] = to_slabs(dw2p[:96])
    grads["rwkv_a2"] = to_slabs(da2p[:96])
    grads["rwkv_g2"] = to_slabs(dg2)
    tail = ("w_in", "ssd_conv_w", "rwkv_w2", "rwkv_a2", "rwkv_g2")
    sc_tail, tok = exchange_start("scatter_tail_start", [grads[n] for n in tail], "scatter")
    dh1 = mm("d_h1_z", du_z, wt_z, "nn", dep=tok)
    dh1 = mm("d_h1_xbc", du_xbc, wt_xbc, "nn", res=dh1)
    dh1 = mm("d_h1_rkv", du_rkv, wt_rkv, "nn", res=dh1)
    dh1 = mm("d_h1_narrow", du_ps, wt_ps, "nn", res=dh1)
    (dx,), (dg_mix,) = rowwise_bwd("norm_mix_bwd", rmsnorm_fn, [xt], [norm_mix_g], [[dh1]], tm_light, [F32], row_add=[dx1])

    dmu =jnp.concatenate([dmu_rkv, dmu_pwa[:, 0:96], dmu_pwa[:, 128:224], dmu_pg], axis=1)
    small_grads = {
        "norm_mix_g": dg_mix, "ssd_conv_b": dconv_b, "ssd_dt_bias": ddt_bias_p[:, :16], "ssd_a_log": da_log_p[:, :16],
        "ssd_d": dd_p[:, :16], "ssd_norm_g": dssd_norm_g, "rwkv_mu": dmu, "rwkv_w0": dw0, "rwkv_a0": da0,
        "rwkv_k_k": dk_k, "rwkv_k_a": dk_a, "rwkv_r_k": dr_k, "rwkv_ln_w": dln_w, "rwkv_ln_b": dln_b,
        "norm_x_g": dg_x, "norm_mem_g": dg_mem, "norm_ffn_g": dg_ffn, "final_norm_g": dg_final}

    gather_small, tok = exchange_start("gather_small_start", [_pack_small(small_grads)], "gather")
    received = {}
    for names, handle in ((("ffn_w2",), sc_w2), (("ffn_w1",), sc_w1), (("xattn_wo",), sc_wo), (qkv, sc_qkv),
                          (("w_out",), sc_wout)):
        received.update(zip(names, exchange_wait("scatter_" + names[0] + "_wait", handle, after=tok)))

    out_g, out_d, out_m, out_v = {}, {}, {}, {}

    def run_adamw(n, dep):
        shape = wts[n].shape
        two_d = lambda a: a.reshape(-1, shape[-1])
        if n == "w_in":
            recv = jnp.transpose(sum_slabs("sum_w_in", received[n]))[None]
        else:
            recv = received[n].reshape(N_DEV, -1, shape[-1])
        res = adamw("adamw_" + n, recv, two_d(wts[n]), two_d(mom_m[n]), two_d(mom_v[n]), dep=dep)
        out_g[n], out_d[n], out_m[n], out_v[n] = (r.reshape(shape) for r in res)
        return res[0]

    last = None
    for n in ("ffn_w2", "ffn_w1", "xattn_wo") + qkv + ("w_out",):
        last = run_adamw(n, last)
    received.update(zip(tail, exchange_wait("scatter_tail_wait", sc_tail, after=last)))
    for n in tail:
        last = run_adamw(n, last)
    (small_all,) = exchange_wait("gather_small_wait", gather_small, after=last)
    res = adamw("adamw_small", small_all, _pack_small(wts), _pack_small(mom_m), _pack_small(mom_v))
    shapes = {n: wts[n].shape for n in _SMALL}
    for dst, packed in zip((out_g, out_d, out_m, out_v), res):
        dst.update(_unpack_small(packed, shapes))

    loss = lax.psum(loss_blk[0, 0], ("x", "y", "c"))
    return (loss, dx[None], *[out_g[n] for n in _WEIGHTS], *[out_d[n] for n in _WEIGHTS],
            *[out_m[n] for n in _WEIGHTS], *[out_v[n] for n in _WEIGHTS])
```

```python
import functools

import jax
import jax.numpy as jnp
from jax import lax
from jax.experimental import pallas as pl
from jax.experimental.pallas import tpu as pltpu

F32 = jnp.float32
BF16 = jnp.bfloat16

N_DEV = 8
D_MODEL = 2048
NORM_EPS = 1e-6
SSD_WIDTH = 1024
SSD_HEAD_DIM = 64
SSD_STATE = 128
SSD_CHUNK = 128
SSD_HEADS_PER_GROUP = 8
RWKV_WIDTH = 1024
RWKV_HEADS = 16
RWKV_HEAD_DIM = 64
RWKV_LN_EPS = 64e-5
RWKV_CHUNK = 128
RWKV_HEADS_PER_STEP = 16
XATTN_HEADS = 4
XATTN_HEAD_DIM = 512
D_FF = 8192
LANES = 128
SUBLANES = 8
VMEM_LIMIT = 56 * 1024 * 1024

ADAM_LR = 0.001
ADAM_B1 = 0.9
ADAM_B2 = 0.999
ADAM_EPS = 1e-08
ADAM_WD = 0.01
ADAM_STEP = 10

_DN = {"nn": ((1,), (0,)), "nt": ((1,), (1,)), "tn": ((0,), (0,))}


def _dg(a, b, mode):
    (ca,), (cb,) = _DN[mode]
    dn = (((ca + 1,), (cb + 1,)), ((0,), (0,))) if a.ndim == 3 else (((ca,), (cb,)), ((), ()))
    return lax.dot_general(a, b, dn, preferred_element_type=F32)


@functools.partial(jax.custom_vjp, nondiff_argnums=(2,))
def bdot(a, b, mode):
    return _dg(a.astype(BF16), b.astype(BF16), mode)


def _bdot_fwd(a, b, mode):
    return bdot(a, b, mode), (a, b)


def _bdot_bwd(mode, res, g):
    a, b = res
    ab, bb, gb = a.astype(BF16), b.astype(BF16), g.astype(BF16)
    if mode == "nn":
        da, db = _dg(gb, bb, "nt"), _dg(ab, gb, "tn")
    elif mode == "nt":
        da, db = _dg(gb, bb, "nn"), _dg(gb, ab, "tn")
    else:
        da, db = _dg(bb, gb, "nt"), _dg(ab, gb, "nn")
    return da.astype(a.dtype), db.astype(b.dtype)


bdot.defvjp(_bdot_fwd, _bdot_bwd)


def _split2(x):
    hi = x.astype(BF16)
    return hi, (x - hi.astype(F32)).astype(BF16)


def _dot01(x, m01):
    hi, lo = _split2(x)
    return _dg(hi, m01, "nn") + _dg(lo, m01, "nn")


def _exact_dot_impl(a, b, mode, exact):
    if exact == "a":
        ae = a.astype(BF16)
        return sum(_dg(ae, part, mode) for part in _split2(b))
    be = b.astype(BF16)
    return sum(_dg(part, be, mode) for part in _split2(a))


@functools.partial(jax.custom_vjp, nondiff_argnums=(2, 3))
def exact_dot(a, b, mode, exact):
    return _exact_dot_impl(a, b, mode, exact)


def _exact_dot_fwd(a, b, mode, exact):
    return _exact_dot_impl(a, b, mode, exact), (a, b)


def _exact_dot_bwd(mode, exact, res, g):
    a, b = res
    if exact == "a":
        db = {"nn": lambda: _exact_dot_impl(a, g, "tn", "a"), "nt": lambda: _exact_dot_impl(g, a, "tn", "b"),
              "tn": lambda: _exact_dot_impl(a, g, "nn", "a")}[mode]()
        return jnp.zeros_like(a), db
    da = {"nn": lambda: _exact_dot_impl(g, b, "nt", "b"), "nt": lambda: _exact_dot_impl(g, b, "nn", "b"),
          "tn": lambda: _exact_dot_impl(b, g, "nt", "a")}[mode]()
    return da, jnp.zeros_like(b)


exact_dot.defvjp(_exact_dot_fwd, _exact_dot_bwd)


def _head_indicator(width, heads, transpose):
    hd = width // heads
    shape = (LANES, width) if transpose else (width, LANES)
    lane = lax.broadcasted_iota(jnp.int32, shape, 1 if not transpose else 0)
    pos = lax.broadcasted_iota(jnp.int32, shape, 0 if not transpose else 1)
    return ((pos >= lane * hd) & (pos < lane * hd + hd)).astype(BF16)


@jax.custom_vjp
def head_sum(x):
    w = x.shape[-1]
    e = _head_indicator(w, w // RWKV_HEAD_DIM, False)
    et = _head_indicator(w, w // RWKV_HEAD_DIM, True)
    return _dot01(_dot01(x, e), et)


head_sum.defvjp(lambda x: (head_sum(x), None), lambda _, g: (head_sum(g),))


def rmsnorm_fn(x, g):
    y = x * lax.rsqrt(jnp.mean(x * x, axis=-1, keepdims=True) + NORM_EPS)
    return ((y * g).astype(BF16),)


def cast_fn(x):
    return (x.astype(BF16),)


def ssd_pre_fn(xbc, xbc1, xbc2, xbc3, dt_raw, conv_w, conv_b, dt_bias):
    c = conv_w[3:4] * xbc + conv_w[2:3] * xbc1 + conv_w[1:2] * xbc2 + conv_w[0:1] * xbc3 + conv_b
    act = c * jax.nn.sigmoid(c)
    dt = jax.nn.softplus(dt_raw + dt_bias)
    return act[:, :SSD_WIDTH], act[:, SSD_WIDTH:SSD_WIDTH + 256], act[:, SSD_WIDTH + 256:], dt


def ssd_post_fn(yscan, z, norm_g):
    y = yscan * (z * jax.nn.sigmoid(z))
    half = SSD_WIDTH // 2
    parts = []
    for g in range(2):
        yg = y[:, g * half:(g + 1) * half]
        parts.append(yg * lax.rsqrt(jnp.mean(yg * yg, axis=-1, keepdims=True) + NORM_EPS))
    return ((jnp.concatenate(parts, axis=-1) * norm_g).astype(BF16),)


def rwkv_pre_fn(rkv, rkv_p, pg, pg_p, pwa, pwa_p, mu_rkv, mu_pg, mu_pwa, w0, w2p, a0, a2p, g2, k_k, k_a):
    w = RWKV_WIDTH
    rkv = rkv + (rkv_p - rkv) * mu_rkv
    pg = pg + (pg_p - pg) * mu_pg
    pwa = pwa + (pwa_p - pwa) * mu_pwa
    r, k, v = rkv[:, :w], rkv[:, w:2 * w], rkv[:, 2 * w:]
    pw, pa = pwa[:, :LANES], pwa[:, LANES:]
    w_log = -jax.nn.softplus(-(w0 + bdot(jnp.tanh(pw), w2p, "nn"))) - 0.5
    lw = -jnp.exp(w_log)
    iclr = jax.nn.sigmoid(a0 + bdot(pa, a2p, "nn"))
    gate = bdot(jax.nn.sigmoid(pg), g2, "nn")
    kk = k * k_k
    kap = kk * lax.rsqrt(jnp.maximum(head_sum(kk * kk), 1e-24))
    k_mod = k * (1.0 + (iclr - 1.0) * k_a)
    return r, lw, k_mod, v, kap, kap * iclr, gate


def rwkv_post_fn(ys, r, k_mod, v, gate, ln_w, ln_b, r_k):
    inv_n = 1.0 / RWKV_HEAD_DIM
    mean = head_sum(ys) * inv_n
    yc = ys - mean
    var = head_sum(yc * yc) * inv_n
    yn = yc * lax.rsqrt(var + RWKV_LN_EPS) * ln_w + ln_b
    bonus = head_sum(r * k_mod * r_k) * v
    return (((yn + bonus) * gate).astype(BF16),)


def attn_fn(q, kx, vx):
    outs = []
    for h in range(XATTN_HEADS):
        sl = slice(h * XATTN_HEAD_DIM, (h + 1) * XATTN_HEAD_DIM)
        s = bdot(q[:, sl], kx[:, sl], "nt") * (XATTN_HEAD_DIM ** -0.5)
        s = s - jnp.max(s, axis=-1, keepdims=True)
        p = jnp.exp(s)
        p = p / jnp.sum(p, axis=-1, keepdims=True)
        outs.append(bdot(p, vx[:, sl], "nn"))
    return (jnp.concatenate(outs, axis=-1).astype(BF16),)


def loss_fn(x, tgt, g):
    y = x * lax.rsqrt(jnp.mean(x * x, axis=-1, keepdims=True) + NORM_EPS) * g
    err = jnp.square(y - tgt)
    return 0.5 * jnp.sum(jnp.mean(err, axis=-1, keepdims=True), axis=0, keepdims=True)


def _tri_masks(n):
    row = lax.broadcasted_iota(jnp.int32, (n, n), 0)
    col = lax.broadcasted_iota(jnp.int32, (n, n), 1)
    return col <= row, col < row, row == col


@jax.custom_vjp
def unit_lower_inverse(a):
    c = a.shape[-1]
    eye = _tri_masks(c)[2].astype(F32)
    m = -a
    inv = eye + m
    n = 1
    while n * 2 < c:
        m = bdot(m, m, "nn")
        inv = bdot(inv, eye + m, "nn")
        n *= 2
    return inv


def _unit_lower_inverse_fwd(a):
    inv = unit_lower_inverse(a)
    return inv, inv


def _unit_lower_inverse_bwd(inv, g):
    return (-bdot(bdot(inv, g, "tn"), inv, "nt"),)


unit_lower_inverse.defvjp(_unit_lower_inverse_fwd, _unit_lower_inverse_bwd)


@jax.custom_vjp
def known_inverse(a, inv):
    return inv


known_inverse.defvjp(lambda a, inv: (inv, inv),
                     lambda inv, g: (_unit_lower_inverse_bwd(inv, g)[0], jnp.zeros_like(inv)))


def rwkv_chunk_fn(st0, r, lw, k, v, kap, b, inv=None):
    h, c = r.shape[0], r.shape[1]
    incl, strict, _ = _tri_masks(c)
    cum = exact_dot(jnp.broadcast_to(incl.astype(F32), (h, c, c)), lw, "nn", "a")
    g_in = jnp.exp(cum)
    g_prev = jnp.exp(cum - lw)
    g_inv = jnp.exp(-cum)
    g_end = jnp.exp(cum[:, c - 1:c, :] - cum)
    kap_t, k_t, b_t, r_t = kap * g_prev, k * g_inv, b * g_inv, r * g_in
    a_ub = jnp.where(strict, bdot(kap_t, b_t, "nt"), 0.0)
    a_vk = jnp.where(strict, bdot(kap_t, k_t, "nt"), 0.0)
    rhs = -(bdot(kap_t, st0, "nt") + bdot(a_vk, v, "nn"))
    inv = unit_lower_inverse(a_ub) if inv is None else known_inverse(a_ub, inv)
    u = bdot(inv, rhs, "nn")
    y = (bdot(r_t, st0, "nt")
         + bdot(jnp.where(incl, bdot(r_t, k_t, "nt"), 0.0), v, "nn")
         + bdot(jnp.where(incl, bdot(r_t, b_t, "nt"), 0.0), u, "nn"))
    st1 = jnp.exp(cum[:, c - 1:c, :]) * st0 + bdot(v, k * g_end, "tn") + bdot(u, b * g_end, "tn")
    return y, st1, inv


def ssd_chunk_fn(group, h0, xs, bm, cm, dt, a_log, d_skip):
    q, nh = xs.shape[0], SSD_HEADS_PER_GROUP
    causal, _, _ = _tri_masks(q)
    a_row = -jnp.exp(a_log)
    cs_all = exact_dot(causal.astype(F32), dt * a_row, "nn", "a")
    cs_t = cs_all.T
    lanes = range(group * nh, (group + 1) * nh)
    cs = jnp.stack([cs_all[:, hl:hl + 1] for hl in lanes])
    cs_row = jnp.stack([cs_t[hl:hl + 1, :] for hl in lanes])
    dt_h = jnp.stack([dt[:, hl:hl + 1] for hl in lanes])
    d_h = jnp.stack([d_skip[:, hl:hl + 1] for hl in lanes])
    x = _stack_lanes(xs, nh)
    h0s = _stack_rows(h0, nh)
    lmat = jnp.where(causal, jnp.exp(jnp.where(causal, cs - cs_row, 0.0)), 0.0)
    cb = bdot(cm, bm, "nt")
    xdt = x * dt_h
    cl = cs[:, q - 1:q, :]
    cm_b = jnp.broadcast_to(cm, (nh,) + cm.shape)
    bm_b = jnp.broadcast_to(bm, (nh,) + bm.shape)
    y = bdot(cb * lmat, xdt, "nn") + bdot(cm_b, h0s, "nt") * jnp.exp(cs) + x * d_h
    h1 = h0s * jnp.exp(cl) + bdot(xdt * jnp.exp(cl - cs), bm_b, "tn")
    return jnp.concatenate([y[e] for e in range(nh)], axis=-1), jnp.concatenate([h1[e] for e in range(nh)], axis=0)


class Rows:
    def __init__(self, arr, w=None, cb=0, shifts=()):
        self.arr, self.w, self.cb, self.shifts = arr, (arr.shape[1] if w is None else w), cb, tuple(shifts)


def _as_rows(x):
    return x if isinstance(x, Rows) else Rows(x)


def _shift_down(x, halo, k):
    rolled = pltpu.roll(x, k, 0)
    first = rolled[0:SUBLANES]
    rid = lax.broadcasted_iota(jnp.int32, first.shape, 0)
    patched = jnp.where(rid < k, pltpu.roll(halo, k, 0), first)
    return jnp.concatenate([patched, rolled[SUBLANES:]], axis=0)


def _shift_up(g, carry, k):
    tm = g.shape[0]
    rolled = pltpu.roll(g, tm - k, 0)
    last = rolled[tm - SUBLANES:]
    rid = lax.broadcasted_iota(jnp.int32, last.shape, 0)
    patched = jnp.where(rid >= SUBLANES - k, pltpu.roll(carry, SUBLANES - k, 0), last)
    return jnp.concatenate([rolled[:tm - SUBLANES], patched], axis=0)


def _params():
    return pltpu.CompilerParams(vmem_limit_bytes=VMEM_LIMIT)


def _load_rows(refs, pos, rins, first_block):
    vals = []
    for r in rins:
        x = refs[pos][...].astype(F32) if refs[pos].dtype != F32 else refs[pos][...]
        pos += 1
        vals.append(x)
        if r.shifts:
            halo = refs[pos][...]
            pos += 1
            halo = jnp.where(first_block, jnp.zeros_like(halo), halo)
            for k in r.shifts:
                vals.append(_shift_down(x, halo, k))
    return vals, pos


def _row_specs(rins, tm, blk):
    specs, args = [], []
    for r in rins:
        specs.append(pl.BlockSpec((tm, r.w), lambda i, cb=r.cb: (blk(i), cb)))
        args.append(r.arr)
        if r.shifts:
            per = tm // SUBLANES
            specs.append(pl.BlockSpec((SUBLANES, r.w), lambda i, cb=r.cb: (jnp.maximum(blk(i) * per - 1, 0), cb)))
            args.append(r.arr)
    return specs, args


def rowwise_fwd(name, fn, rins, params, outs, tm, deps=(), into=None):
    rins = [_as_rows(r) for r in rins]
    t = rins[0].arr.shape[0]
    tm = min(tm, t)
    nb = t // tm
    specs, args = _row_specs(rins, tm, lambda i: i)
    for p in params:
        specs.append(pl.BlockSpec(p.shape, lambda i: (0, 0)))
        args.append(p)
    for dep in deps:
        specs.append(pl.BlockSpec(memory_space=pl.ANY))
        args.append(dep)
    out_specs = [pl.BlockSpec((tm, w), lambda i: (i, 0)) for w, _ in outs]
    out_shape = [jax.ShapeDtypeStruct((t, w), dt) for w, dt in outs]
    aliases = {}
    if into is not None:
        target, total_width, col_block = into
        out_specs = [pl.BlockSpec((tm, outs[0][0]), lambda i: (i, col_block))]
        out_shape = [jax.ShapeDtypeStruct((t, total_width), outs[0][1])]
        if target is not None:
            aliases = {len(args): 0}
            specs.append(pl.BlockSpec(memory_space=pl.ANY))
            args.append(target)
    n_in = len(args)

    def body(*refs):
        vals, pos = _load_rows(refs, 0, rins, pl.program_id(0) == 0)
        pv = [refs[pos + j][...] for j in range(len(params))]
        res = fn(*vals, *pv)
        for o_ref, o in zip(refs[n_in:], res):
            o_ref[...] = o.astype(o_ref.dtype)

    return pl.pallas_call(
        body, name=name, grid=(nb,), in_specs=specs, out_specs=out_specs, out_shape=out_shape,
        input_output_aliases=aliases, compiler_params=_params(),
    )(*args)


def rowwise_bwd(name, fn, rins, params, cts, tm, grad_dtypes, row_add=None):
    rins = [_as_rows(r) for r in rins]
    cts = [[_as_rows(c) for c in lst] for lst in cts]
    row_add = [_as_rows(a) for a in (row_add or [])]
    t = rins[0].arr.shape[0]
    tm = min(tm, t)
    nb = t // tm
    rev = lambda i: nb - 1 - i
    specs, args = _row_specs(rins, tm, rev)
    for p in params:
        specs.append(pl.BlockSpec(p.shape, lambda i: (0, 0)))
        args.append(p)
    flat_cts = [c for lst in cts for c in lst] + row_add
    for c in flat_cts:
        specs.append(pl.BlockSpec((tm, c.w), lambda i, cb=c.cb: (rev(i), cb)))
        args.append(c.arr)
    n_in = len(args)
    want = [i for i, d in enumerate(grad_dtypes) if d is not None]
    out_specs = [pl.BlockSpec((tm, rins[i].w), lambda i_: (rev(i_), 0)) for i in want]
    out_shape = [jax.ShapeDtypeStruct((t, rins[i].w), grad_dtypes[i]) for i in want]
    out_specs += [pl.BlockSpec(p.shape, lambda i: (0, 0)) for p in params]
    out_shape += [jax.ShapeDtypeStruct(p.shape, F32) for p in params]
    n_out = len(out_shape)
    scratch = [pltpu.VMEM((SUBLANES, r.w), F32) for r in rins for _ in r.shifts]

    def body(*refs):
        i = pl.program_id(0)
        vals, pos = _load_rows(refs, 0, rins, rev(i) == 0)
        pv = [refs[pos + j][...] for j in range(len(params))]
        pos += len(params)
        outs, vjp = jax.vjp(fn, *vals, *pv)
        ct_vals = []
        for o, lst in zip(outs, cts):
            acc = None
            for _ in lst:
                cv = refs[pos][...].astype(F32)
                pos += 1
                acc = cv if acc is None else acc + cv
            ct_vals.append(acc.astype(o.dtype))
        adds = [refs[pos + j][...].astype(F32) for j in range(len(row_add))]
        grads = vjp(tuple(ct_vals))
        out_refs = refs[n_in:n_in + n_out]
        carry_refs = refs[n_in + n_out:]

        @pl.when(i == 0)
        def _():
            for cr in carry_refs:
                cr[...] = jnp.zeros_like(cr)
            for pr in out_refs[len(want):]:
                pr[...] = jnp.zeros_like(pr)

        gi, ci, oi = 0, 0, 0
        for idx, r in enumerate(rins):
            d = grads[gi]
            gi += 1
            for k in r.shifts:
                dk = grads[gi]
                gi += 1
                d = d + _shift_up(dk, carry_refs[ci][...], k)
                carry_refs[ci][...] = dk[0:SUBLANES]
                ci += 1
            if idx == 0:
                for a in adds:
                    d = d + a
            if grad_dtypes[idx] is not None:
                out_refs[oi][...] = d.astype(out_refs[oi].dtype)
                oi += 1
        for pr, gp in zip(out_refs[len(want):], grads[gi:]):
            pr[...] += gp

    res = pl.pallas_call(
        body, name=name, grid=(nb,), in_specs=specs, out_specs=out_specs, out_shape=out_shape,
        scratch_shapes=scratch, compiler_params=_params(),
    )(*args)
    return res[:len(want)], res[len(want):]


def _pick(n, pref):
    for c in pref:
        if n % c == 0:
            return c
    return n


MM_VMEM_BUDGET = 40 * 1024 * 1024
MM_PEAK_FLOPS = 0.9e15
MM_HBM_BYTES_PER_S = 3.0e12
MM_STEP_SECONDS = 0.35e-6


def _mm_tiles(m, n, k, size_a, size_b, size_out, size_res, single_k):
    best = None
    for tk in sorted({c for c in (k, 2048, 1024, 512, 256, 128) if c <= 2048 and k % c == 0}, reverse=True):
        for tm in sorted({c for c in (m, 1024, 512, 256, 128) if c <= 1024 and m % c == 0}, reverse=True):
            for tn in sorted({c for c in (n, 2048, 1536, 1024, 768, 512, 384, 256, 128) if c <= 2048 and n % c == 0},
                             reverse=True):
                nk = k // tk
                vmem = 2 * (tm * tk * size_a + tk * tn * size_b + tm * tn * (size_out + size_res))
                vmem += tm * tn * 4 * (2 if nk > 1 or not single_k else 1)
                vmem += (tm * tk * 2 if size_a > 2 else 0) + (tk * tn * 2 if size_b > 2 else 0)
                if vmem > MM_VMEM_BUDGET:
                    continue
                steps = (m // tm) * (n // tn) * nk
                a_reads = 1 if (nk == 1 and single_k) else n // tn
                traffic = m * k * size_a * a_reads + k * n * size_b * (m // tm) + m * n * (size_out + size_res)
                cost = max(2.0 * m * n * k / MM_PEAK_FLOPS, traffic / MM_HBM_BYTES_PER_S) + steps * MM_STEP_SECONDS
                if best is None or cost < best[0]:
                    best = (cost, tm, tn, tk)
    return best[1:]


def mm(name, a, b, mode, out_dtype=F32, res=None, b_slabs=None, out_slabs=None, dep=None, epi=None, extras=(),
       out_dtypes=None):
    if mode == "tn":
        k_dim, m_dim = a.shape
    else:
        m_dim, k_dim = a.shape
    if b_slabs:
        n_dim = b.shape[0] * b.shape[2] if mode == "nn" else b.shape[1]
    else:
        n_dim = b.shape[0] if mode == "nt" else b.shape[1]
    n_slabs = out_slabs or (b_slabs if (b_slabs and mode == "nn") else 1)
    k_slabs = b_slabs if (b_slabs and mode == "nt") else 1
    if epi is None:
        out_dtypes = [out_dtype]
        if res is None:
            epi = lambda acc: (acc,)
        else:
            extras, epi = [res], lambda acc, r: (acc + r,)
    tm, tn, tk = _mm_tiles(m_dim, n_dim // n_slabs, k_dim // k_slabs, a.dtype.itemsize, b.dtype.itemsize,
                           sum(jnp.dtype(dt).itemsize for dt in out_dtypes), sum(e.dtype.itemsize for e in extras),
                           single_k=(k_slabs == 1))
    nji = n_dim // n_slabs // tn
    nki = k_dim // k_slabs // tk
    nblk = lambda js, j: js * nji + j
    kblk = lambda ks, k: ks * nki + k
    if mode == "tn":
        a_spec = pl.BlockSpec((tk, tm), lambda i, js, j, ks, k: (kblk(ks, k), i))
    else:
        a_spec = pl.BlockSpec((tm, tk), lambda i, js, j, ks, k: (i, kblk(ks, k)))
    if b_slabs and mode == "nn":
        b_spec = pl.BlockSpec((None, tk, tn), lambda i, js, j, ks, k: (js, k, j))
    elif b_slabs and mode == "nt":
        b_spec = pl.BlockSpec((None, tn, tk), lambda i, js, j, ks, k: (ks, nblk(js, j), k))
    elif mode == "nt":
        b_spec = pl.BlockSpec((tn, tk), lambda i, js, j, ks, k: (nblk(js, j), kblk(ks, k)))
    else:
        b_spec = pl.BlockSpec((tk, tn), lambda i, js, j, ks, k: (kblk(ks, k), nblk(js, j)))
    specs, args = [a_spec, b_spec], [a, b]
    for e in extras:
        specs.append(pl.BlockSpec((tm, tn), lambda i, js, j, ks, k: (i, nblk(js, j))))
        args.append(e)
    if dep is not None:
        specs.append(pl.BlockSpec(memory_space=pl.ANY))
        args.append(dep)
    if out_slabs:
        o_specs = [pl.BlockSpec((None, tm, tn), lambda i, js, j, ks, k: (js, i, j))]
        o_shapes = [jax.ShapeDtypeStruct((out_slabs, m_dim, n_dim // out_slabs), out_dtypes[0])]
    else:
        o_specs = [pl.BlockSpec((tm, tn), lambda i, js, j, ks, k: (i, nblk(js, j))) for _ in out_dtypes]
        o_shapes = [jax.ShapeDtypeStruct((m_dim, n_dim), dt) for dt in out_dtypes]

    one_k_step = k_slabs * nki == 1
    n_in, n_out = len(args), len(out_dtypes)

    def body(*refs):
        a_ref, b_ref = refs[0], refs[1]
        part = _dg(a_ref[...].astype(BF16), b_ref[...].astype(BF16), mode)

        def finish(acc):
            outs = epi(acc, *[refs[2 + j][...].astype(F32) for j in range(len(extras))])
            for o_ref, o in zip(refs[n_in:n_in + n_out], outs):
                o_ref[...] = o.astype(o_ref.dtype)

        if one_k_step:
            finish(part)
            return
        acc_ref = refs[n_in + n_out]
        ks, kk = pl.program_id(3), pl.program_id(4)

        @pl.when((ks == 0) & (kk == 0))
        def _():
            acc_ref[...] = part

        @pl.when((ks > 0) | (kk > 0))
        def _():
            acc_ref[...] += part

        pl.when((ks == k_slabs - 1) & (kk == nki - 1))(lambda: finish(acc_ref[...]))

    grid = (m_dim // tm, n_slabs, nji, k_slabs, nki)
    scratch = [] if one_k_step else [pltpu.VMEM((tm, tn), F32)]
    out = pl.pallas_call(
        body, name=name, grid=grid, in_specs=specs, out_specs=o_specs, out_shape=o_shapes, scratch_shapes=scratch,
        compiler_params=pltpu.CompilerParams(
            dimension_semantics=("parallel", "parallel", "parallel", "arbitrary", "arbitrary"),
            vmem_limit_bytes=VMEM_LIMIT),
    )(*args)
    return out[0] if n_out == 1 else out


def _stack_lanes(x, n):
    w = x.shape[1] // n
    return jnp.stack([x[:, i * w:(i + 1) * w] for i in range(n)])


def _stack_rows(x, n):
    w = x.shape[0] // n
    return jnp.stack([x[i * w:(i + 1) * w, :] for i in range(n)])


def rwkv_scan_fwd(r, lw, k, v, kap, b):
    t = r.shape[0]
    c, hps, hd = min(RWKV_CHUNK, t), RWKV_HEADS_PER_STEP, RWKV_HEAD_DIM
    nc, ng, wl = t // c, RWKV_HEADS // hps, hps * hd
    spec = pl.BlockSpec((c, wl), lambda g, ci: (ci, g))

    def body(r_ref, lw_ref, k_ref, v_ref, kap_ref, b_ref, y_ref, ck_ref, inv_ref, st_ref):
        @pl.when(pl.program_id(1) == 0)
        def _():
            st_ref[...] = jnp.zeros_like(st_ref)

        st = st_ref[...]
        ck_ref[...] = st
        ins = [x[...] for x in (r_ref, lw_ref, k_ref, v_ref, kap_ref, b_ref)]
        y, st1, inv = rwkv_chunk_fn(_stack_rows(st, hps), *[_stack_lanes(x, hps) for x in ins])
        y_ref[...] = jnp.concatenate([y[h] for h in range(hps)], axis=-1)
        st_ref[...] = jnp.concatenate([st1[h] for h in range(hps)], axis=0)
        inv_ref[...] = jnp.concatenate([inv[h] for h in range(hps)], axis=0)

    return pl.pallas_call(
        body, name="rwkv_scan_fwd", grid=(ng, nc), in_specs=[spec] * 6,
        out_specs=[spec, pl.BlockSpec((None, wl, hd), lambda g, ci: (ci, g, 0)),
                   pl.BlockSpec((None, hps * c, c), lambda g, ci: (ci, g, 0))],
        out_shape=[jax.ShapeDtypeStruct((t, RWKV_WIDTH), F32), jax.ShapeDtypeStruct((nc, RWKV_WIDTH, hd), F32),
                   jax.ShapeDtypeStruct((nc, RWKV_HEADS * c, c), F32)],
        scratch_shapes=[pltpu.VMEM((wl, hd), F32)], compiler_params=_params(),
    )(r, lw, k, v, kap, b)


def rwkv_scan_bwd(r, lw, k, v, kap, b, ck, inv_ck, dy):
    t = r.shape[0]
    c, hps, hd = min(RWKV_CHUNK, t), RWKV_HEADS_PER_STEP, RWKV_HEAD_DIM
    nc, ng, wl = t // c, RWKV_HEADS // hps, hps * hd
    spec = pl.BlockSpec((c, wl), lambda g, ci: (nc - 1 - ci, g))

    def body(r_ref, lw_ref, k_ref, v_ref, kap_ref, b_ref, ck_ref, inv_ref, dy_ref, *rest):
        out_refs, dst_ref = rest[:6], rest[6]

        @pl.when(pl.program_id(1) == 0)
        def _():
            dst_ref[...] = jnp.zeros_like(dst_ref)

        ins = [x[...] for x in (r_ref, lw_ref, k_ref, v_ref, kap_ref, b_ref)]
        dyv, ck, dst = dy_ref[...].astype(F32), ck_ref[...], dst_ref[...]
        chunk = lambda *a: rwkv_chunk_fn(*a, inv=_stack_rows(inv_ref[...], hps))[:2]
        _, vjp = jax.vjp(chunk, _stack_rows(ck, hps), *[_stack_lanes(x, hps) for x in ins])
        grads = vjp((_stack_lanes(dyv, hps), _stack_rows(dst, hps)))
        dst_ref[...] = jnp.concatenate([grads[0][h] for h in range(hps)], axis=0)
        for j in range(6):
            out_refs[j][...] = jnp.concatenate([grads[1 + j][h] for h in range(hps)], axis=-1).astype(BF16)

    return pl.pallas_call(
        body, name="rwkv_scan_bwd", grid=(ng, nc),
        in_specs=[spec] * 6 + [pl.BlockSpec((None, wl, hd), lambda g, ci: (nc - 1 - ci, g, 0)),
                               pl.BlockSpec((None, hps * c, c), lambda g, ci: (nc - 1 - ci, g, 0)), spec],
        out_specs=[spec] * 6, out_shape=[jax.ShapeDtypeStruct((t, RWKV_WIDTH), BF16)] * 6,
        scratch_shapes=[pltpu.VMEM((wl, hd), F32)], compiler_params=_params(),
    )(r, lw, k, v, kap, b, ck, inv_ck, dy)


def _ssd_specs(q, blk):
    gw = SSD_WIDTH // 2
    return [pl.BlockSpec((q, gw), lambda g, ci: (blk(ci), g)),
            pl.BlockSpec((q, SSD_STATE), lambda g, ci: (blk(ci), g)),
            pl.BlockSpec((q, SSD_STATE), lambda g, ci: (blk(ci), g)),
            pl.BlockSpec((q, LANES), lambda g, ci: (blk(ci), 0)),
            pl.BlockSpec((1, LANES), lambda g, ci: (0, 0)),
            pl.BlockSpec((1, LANES), lambda g, ci: (0, 0))]


def ssd_scan_fwd(xs, bm, cm, dt, a_log, d_skip):
    t = xs.shape[0]
    q = min(SSD_CHUNK, t)
    nc, gw = t // q, SSD_WIDTH // 2

    def body(xs_ref, bm_ref, cm_ref, dt_ref, al_ref, d_ref, y_ref, ck_ref, h_ref):
        @pl.when(pl.program_id(1) == 0)
        def _():
            h_ref[...] = jnp.zeros_like(h_ref)

        ck_ref[...] = h_ref[...]
        args = (h_ref[...], xs_ref[...], bm_ref[...], cm_ref[...], dt_ref[...], al_ref[...], d_ref[...])
        g = pl.program_id(0)

        @pl.when(g == 0)
        def _():
            y, h1 = ssd_chunk_fn(0, *args)
            y_ref[...] = y
            h_ref[...] = h1

        @pl.when(g == 1)
        def _():
            y, h1 = ssd_chunk_fn(1, *args)
            y_ref[...] = y
            h_ref[...] = h1

    return pl.pallas_call(
        body, name="ssd_scan_fwd", grid=(2, nc), in_specs=_ssd_specs(q, lambda ci: ci),
        out_specs=[pl.BlockSpec((q, gw), lambda g, ci: (ci, g)),
                   pl.BlockSpec((None, gw, SSD_STATE), lambda g, ci: (ci, g, 0))],
        out_shape=[jax.ShapeDtypeStruct((t, SSD_WIDTH), F32), jax.ShapeDtypeStruct((nc, SSD_WIDTH, SSD_STATE), F32)],
        scratch_shapes=[pltpu.VMEM((gw, SSD_STATE), F32)], compiler_params=_params(),
    )(xs, bm, cm, dt, a_log, d_skip)


def ssd_scan_bwd(xs, bm, cm, dt, a_log, d_skip, ck, dy):
    t = xs.shape[0]
    q = min(SSD_CHUNK, t)
    nc, gw = t // q, SSD_WIDTH // 2
    rev = lambda ci: nc - 1 - ci

    def body(xs_ref, bm_ref, cm_ref, dt_ref, al_ref, d_ref, ck_ref, dy_ref,
             dxs_ref, dbm_ref, dcm_ref, ddt_ref, dal_ref, dd_ref, dh_ref):
        g, ci = pl.program_id(0), pl.program_id(1)

        @pl.when(ci == 0)
        def _():
            dh_ref[...] = jnp.zeros_like(dh_ref)

        @pl.when((ci == 0) & (g == 0))
        def _():
            dal_ref[...] = jnp.zeros_like(dal_ref)
            dd_ref[...] = jnp.zeros_like(dd_ref)

        args = (ck_ref[...], xs_ref[...], bm_ref[...], cm_ref[...], dt_ref[...], al_ref[...], d_ref[...])

        def run(group):
            _, vjp = jax.vjp(functools.partial(ssd_chunk_fn, group), *args)
            dh0, dxs, dbm, dcm, ddt, dal, dd = vjp((dy_ref[...].astype(F32), dh_ref[...]))
            dh_ref[...] = dh0
            dxs_ref[...] = dxs.astype(BF16)
            dbm_ref[...] = dbm.astype(BF16)
            dcm_ref[...] = dcm.astype(BF16)
            ddt_ref[...] = ddt
            dal_ref[...] += dal
            dd_ref[...] += dd

        pl.when(g == 0)(lambda: run(0))
        pl.when(g == 1)(lambda: run(1))

    in_specs = _ssd_specs(q, rev) + [pl.BlockSpec((None, gw, SSD_STATE), lambda g, ci: (rev(ci), g, 0)),
                                     pl.BlockSpec((q, gw), lambda g, ci: (rev(ci), g))]
    return pl.pallas_call(
        body, name="ssd_scan_bwd", grid=(2, nc), in_specs=in_specs,
        out_specs=[pl.BlockSpec((q, gw), lambda g, ci: (rev(ci), g)),
                   pl.BlockSpec((q, SSD_STATE), lambda g, ci: (rev(ci), g)),
                   pl.BlockSpec((q, SSD_STATE), lambda g, ci: (rev(ci), g)),
                   pl.BlockSpec((None, q, LANES), lambda g, ci: (g, rev(ci), 0)),
                   pl.BlockSpec((1, LANES), lambda g, ci: (0, 0)),
                   pl.BlockSpec((1, LANES), lambda g, ci: (0, 0))],
        out_shape=[jax.ShapeDtypeStruct((t, SSD_WIDTH), BF16), jax.ShapeDtypeStruct((t, 2 * SSD_STATE), BF16),
                   jax.ShapeDtypeStruct((t, 2 * SSD_STATE), BF16), jax.ShapeDtypeStruct((2, t, LANES), F32),
                   jax.ShapeDtypeStruct((1, LANES), F32), jax.ShapeDtypeStruct((1, LANES), F32)],
        scratch_shapes=[pltpu.VMEM((gw, SSD_STATE), F32)], compiler_params=_params(),
    )(xs, bm, cm, dt, a_log, d_skip, ck, dy)


def loss_and_grad(x, tgt, g, tm):
    t, d = x.shape
    tm = min(tm, t)
    nb = t // tm

    def body(x_ref, t_ref, g_ref, loss_ref, dx_ref, dxb_ref, dg_ref):
        @pl.when(pl.program_id(0) == 0)
        def _():
            loss_ref[...] = jnp.zeros_like(loss_ref)
            dg_ref[...] = jnp.zeros_like(dg_ref)

        val, vjp = jax.vjp(loss_fn, x_ref[...], t_ref[...], g_ref[...])
        dx, _, dg = vjp(jnp.ones((1, 1), F32))
        loss_ref[...] += jnp.broadcast_to(val, loss_ref.shape)
        dx_ref[...] = dx
        dxb_ref[...] = dx.astype(BF16)
        dg_ref[...] += dg

    row = pl.BlockSpec((tm, d), lambda i: (i, 0))
    one = pl.BlockSpec((1, d), lambda i: (0, 0))
    return pl.pallas_call(
        body, name="loss_and_grad", grid=(nb,), in_specs=[row, row, one],
        out_specs=[pl.BlockSpec((SUBLANES, LANES), lambda i: (0, 0)), row, row, one],
        out_shape=[jax.ShapeDtypeStruct((SUBLANES, LANES), F32), jax.ShapeDtypeStruct((t, d), F32),
                   jax.ShapeDtypeStruct((t, d), BF16), jax.ShapeDtypeStruct((1, d), F32)],
        compiler_params=_params(),
    )(x, tgt, g)


def adamw(name, recv, w, m, v, dep=None):
    rows, cols = w.shape
    n_slabs = recv.shape[0]
    recv_block_bytes = 8 * 1024 * 1024
    tm = _pick(rows, [c for c in (256, 128, 64, 32, 16, 8)
                      if n_slabs * c * cols * recv.dtype.itemsize <= recv_block_bytes and 2 * c <= rows])
    c1 = 1.0 / (1.0 - ADAM_B1 ** ADAM_STEP)
    c2 = 1.0 / (1.0 - ADAM_B2 ** ADAM_STEP)

    n_dep = 0 if dep is None else 1

    def body(recv_ref, w_ref, m_ref, v_ref, *rest):
        g_ref, d_ref, nm_ref, nv_ref = rest[n_dep:]
        g = recv_ref[0].astype(F32)
        for p in range(1, n_slabs):
            g = g + recv_ref[p].astype(F32)
        nm =ADAM_B1 * m_ref[...] + (1.0 - ADAM_B1) * g
        nv = ADAM_B2 * v_ref[...] + (1.0 - ADAM_B2) * jnp.square(g)
        g_ref[...] = g
        nm_ref[...] = nm
        nv_ref[...] = nv
        d_ref[...] = -ADAM_LR * ((nm * c1) / (jnp.sqrt(nv * c2) + ADAM_EPS) + ADAM_WD * w_ref[...])

    blk = pl.BlockSpec((tm, cols), lambda i: (i, 0))
    return pl.pallas_call(
        body, name=name, grid=(rows // tm,),
        in_specs=[pl.BlockSpec((n_slabs, tm, cols), lambda i: (0, i, 0)), blk, blk, blk]
        + [pl.BlockSpec(memory_space=pl.ANY)] * n_dep,
        out_specs=[blk] * 4, out_shape=[jax.ShapeDtypeStruct((rows, cols), F32)] * 4,
        compiler_params=_params(),
    )(recv, w, m, v, *([] if dep is None else [dep]))


def _mesh_pos():
    return lax.axis_index("x"), lax.axis_index("y"), lax.axis_index("c")


def _peer(pos, mask):
    x, y, c = pos
    return (1 - x if mask & 4 else x, 1 - y if mask & 2 else y, 1 - c if mask & 1 else c)


def _linear(pos):
    return 4 * pos[0] + 2 * pos[1] + pos[2]


class Exchange:
    MASKS = {"gather": (1, 2, 3, 4, 5, 6, 7), "scatter": (1, 2, 3, 4, 5, 6, 7), "gather_chips": (1, 2, 4, 6),
             "forward": (2, 4, 6)}

    def __init__(self, xs, kind, lands=None):
        self.kind, self.masks = kind, self.MASKS[kind]
        self.xs = [] if kind == "forward" else list(xs)
        if kind == "forward":
            self.land_shape = [jax.ShapeDtypeStruct(l.shape, l.dtype) for l in lands]
        elif kind == "scatter":
            self.land_shape = [jax.ShapeDtypeStruct(x.shape, x.dtype) for x in xs]
        else:
            self.land_shape = [jax.ShapeDtypeStruct((N_DEV,) + x.shape, x.dtype) for x in xs]
        self.n = len(self.land_shape)
        copies = self.n * len(self.masks)
        self.sems = [pltpu.SemaphoreType.DMA((copies,)), pltpu.SemaphoreType.DMA((copies,)),
                     pltpu.SemaphoreType.DMA((self.n,))]

    def _copies(self, ins, outs, sems, landing):
        send_sems, recv_sems, local_sems = sems
        me = _mesh_pos()
        me_lin = _linear(me)
        local, remote = [], []
        for ti in range(self.n):
            if self.kind != "forward":
                src_mine = ins[ti].at[me_lin] if self.kind == "scatter" else ins[ti]
                local.append(pltpu.make_async_copy(src_mine, outs[ti].at[me_lin], local_sems.at[ti]))
            for j, mask in enumerate(self.masks):
                if self.kind == "forward":
                    peer = _peer(me, 1)
                    src = outs[ti].at[_linear(_peer(me, mask))]
                    dst = outs[ti].at[_linear(_peer(me, mask ^ 1 if landing else mask))]
                else:
                    peer = _peer(me, mask)
                    src = ins[ti].at[_linear(peer)] if self.kind == "scatter" else ins[ti]
                    dst = outs[ti].at[_linear(peer) if landing else me_lin]
                sem_index = ti * len(self.masks) + j
                remote.append(pltpu.make_async_remote_copy(
                    src_ref=src, dst_ref=dst, send_sem=send_sems.at[sem_index], recv_sem=recv_sems.at[sem_index],
                    device_id=peer, device_id_type=pl.DeviceIdType.MESH))
        return local, remote

    def start(self, ins, outs, sems):
        local, remote = self._copies(ins, outs, sems, landing=False)
        for cp in local + remote:
            cp.start()

    def finish(self, ins, outs, sems):
        local, remote = self._copies(ins, outs, sems, landing=True)
        for cp in remote:
            cp.wait_recv()
        for cp in remote:
            cp.wait_send()
        for cp in local:
            cp.wait()


def exchange_start(name, xs, kind, dep=None, lands=None):
    ex = Exchange(xs, kind, lands)
    hbm = pl.BlockSpec(memory_space=pltpu.HBM)
    sem = pl.BlockSpec(memory_space=pltpu.SEMAPHORE)
    if lands is None:
        lands = [lax.empty(s.shape, s.dtype) for s in ex.land_shape]
    n_src, n = len(ex.xs), ex.n
    n_inputs = n_src + n + (0 if dep is None else 1)

    def body(*refs):
        ins, lnd, sems, token = refs[:n_src], refs[n_src:n_src + n], refs[n_inputs:n_inputs + 3], refs[-1]
        ex.start(ins, lnd, sems)
        token[...] = jnp.zeros_like(token)

    res = pl.pallas_call(
        body, name=name, in_specs=[hbm] * (n_src + n) + ([] if dep is None else [pl.BlockSpec(memory_space=pl.ANY)]),
        out_specs=[sem] * 3 + [hbm] * (n_src + n) + [pl.BlockSpec(memory_space=pltpu.VMEM)],
        out_shape=ex.sems + [pltpu.HBM(x.shape, x.dtype) for x in ex.xs]
        + [pltpu.HBM(s.shape, s.dtype) for s in ex.land_shape] + [jax.ShapeDtypeStruct((SUBLANES, LANES), F32)],
        input_output_aliases={i: 3 + i for i in range(n_src + n)},
        compiler_params=pltpu.CompilerParams(has_side_effects=pltpu.SideEffectType.DATAFLOW_SIDE_EFFECTING),
    )(*[pltpu.with_memory_space_constraint(x, pltpu.HBM) for x in ex.xs + list(lands)],
      *([] if dep is None else [dep]))
    return (ex, res[:3], res[3:3 + n_src], res[3 + n_src:3 + n_src + n]), res[-1]


def exchange_wait(name, handles, after):
    ex, sems, srcs, lands = handles
    n_src, n = len(srcs), len(lands)
    hbm = pl.BlockSpec(memory_space=pltpu.HBM)
    sem = pl.BlockSpec(memory_space=pltpu.SEMAPHORE)

    def body(*refs):
        ins, lnd, sem_refs = refs[:n_src], refs[n_src:n_src + n], refs[n_src + n:n_src + n + 3]
        ex.finish(ins, lnd, sem_refs)

    res = pl.pallas_call(
        body, name=name, in_specs=[hbm] * (n_src + n) + [sem] * 3 + [pl.BlockSpec(memory_space=pl.ANY)],
        out_specs=[hbm] * (n_src + n),
        out_shape=[pltpu.HBM(x.shape, x.dtype) for x in srcs] + [pltpu.HBM(x.shape, x.dtype) for x in lands],
        input_output_aliases={i: i for i in range(n_src + n)},
        compiler_params=pltpu.CompilerParams(has_side_effects=pltpu.SideEffectType.DATAFLOW_SIDE_EFFECTING),
    )(*srcs, *lands, *sems, after)
    return res[n_src:]


def forward_start(name, chip_gather, after):
    lands = exchange_wait(name + "_wait", chip_gather, after)
    return exchange_start(name + "_forward_start", [], "forward", lands=lands)


_Z = (0, 1024)
_XBC = (1024, 2560)
_DT = (2560, 2576)
_RKV = (2576, 5648)
_PW = (5648, 5744)
_PA = (5744, 5840)
_PG = (5840, 6096)
D_IN = 6096

_SMALL = ("norm_mix_g", "ssd_conv_b", "ssd_dt_bias", "ssd_a_log", "ssd_d", "ssd_norm_g", "rwkv_mu", "rwkv_w0",
          "rwkv_a0", "rwkv_k_k", "rwkv_k_a", "rwkv_r_k", "rwkv_ln_w", "rwkv_ln_b", "norm_x_g", "norm_mem_g",
          "norm_ffn_g", "final_norm_g")
_WEIGHTS = ("norm_mix_g", "w_in", "ssd_conv_w", "ssd_conv_b", "ssd_dt_bias", "ssd_a_log", "ssd_d", "ssd_norm_g",
            "rwkv_mu", "rwkv_w0", "rwkv_w2", "rwkv_a0", "rwkv_a2", "rwkv_g2", "rwkv_k_k", "rwkv_k_a", "rwkv_r_k",
            "rwkv_ln_w", "rwkv_ln_b", "w_out", "norm_x_g", "norm_mem_g", "xattn_wq", "xattn_wk", "xattn_wv",
            "xattn_wo", "norm_ffn_g", "ffn_w1", "ffn_w2", "final_norm_g")


def _pad_lanes(x, width=LANES):
    return jnp.pad(x, ((0, 0), (0, width - x.shape[1])))


def _pack_small(vals):
    flat = jnp.concatenate([vals[n].reshape(-1) for n in _SMALL])
    rows = -(-flat.shape[0] // (LANES * SUBLANES)) * SUBLANES
    return jnp.pad(flat, (0, rows * LANES - flat.shape[0])).reshape(rows, LANES)


def _unpack_small(packed, shapes):
    flat = packed.reshape(-1)
    out, pos = {}, 0
    for n in _SMALL:
        size = 1
        for s in shapes[n]:
            size *= s
        out[n] = flat[pos:pos + size].reshape(shapes[n])
        pos += size
    return out


def _rows(w, rng):
    return w[rng[0]:rng[1]]


def sum_slabs(name, recv):
    n, rows, cols = recv.shape
    tc = _pick(cols, (256, 128))

    def body(r_ref, o_ref):
        acc = r_ref[0].astype(F32)
        for p in range(1, n):
            acc = acc + r_ref[p].astype(F32)
        o_ref[...] = acc

    return pl.pallas_call(
        body, name=name, grid=(cols // tc,), in_specs=[pl.BlockSpec((n, rows, tc), lambda j: (0, 0, j))],
        out_specs=pl.BlockSpec((rows, tc), lambda j: (0, j)), out_shape=jax.ShapeDtypeStruct((rows, cols), F32),
        compiler_params=_params(),
    )(recv)


def kernel(x, mem, norm_mix_g, w_in, ssd_conv_w, ssd_conv_b, ssd_dt_bias, ssd_a_log, ssd_d, ssd_norm_g, rwkv_mu, rwkv_w0, rwkv_w2, rwkv_a0, rwkv_a2, rwkv_g2, rwkv_k_k, rwkv_k_a, rwkv_r_k, rwkv_ln_w, rwkv_ln_b, w_out, norm_x_g, norm_mem_g, xattn_wq, xattn_wk, xattn_wv, xattn_wo, norm_ffn_g, ffn_w1, ffn_w2, final_norm_g, loss_target, m_norm_mix_g, m_w_in, m_ssd_conv_w, m_ssd_conv_b, m_ssd_dt_bias, m_ssd_a_log, m_ssd_d, m_ssd_norm_g, m_rwkv_mu, m_rwkv_w0, m_rwkv_w2, m_rwkv_a0, m_rwkv_a2, m_rwkv_g2, m_rwkv_k_k, m_rwkv_k_a, m_rwkv_r_k, m_rwkv_ln_w, m_rwkv_ln_b, m_w_out, m_norm_x_g, m_norm_mem_g, m_xattn_wq, m_xattn_wk, m_xattn_wv, m_xattn_wo, m_norm_ffn_g, m_ffn_w1, m_ffn_w2, m_final_norm_g, v_norm_mix_g, v_w_in, v_ssd_conv_w, v_ssd_conv_b, v_ssd_dt_bias, v_ssd_a_log, v_ssd_d, v_ssd_norm_g, v_rwkv_mu, v_rwkv_w0, v_rwkv_w2, v_rwkv_a0, v_rwkv_a2, v_rwkv_g2, v_rwkv_k_k, v_rwkv_k_a, v_rwkv_r_k, v_rwkv_ln_w, v_rwkv_ln_b, v_w_out, v_norm_x_g, v_norm_mem_g, v_xattn_wq, v_xattn_wk, v_xattn_wv, v_xattn_wo, v_norm_ffn_g, v_ffn_w1, v_ffn_w2, v_final_norm_g):
    given = dict(locals())
    wts = {n: given[n] for n in _WEIGHTS}
    mom_m = {n: given["m_" + n] for n in _WEIGHTS}
    mom_v = {n: given["v_" + n] for n in _WEIGHTS}
    d = D_MODEL
    xt, memt, tgt = x[0], mem[0], loss_target[0]
    tm = 256
    tm_light = 512

    big = {"w_in": jnp.transpose(w_in[0]), "w_out": w_out[0], "xattn_wq": xattn_wq[0], "xattn_wk": xattn_wk[0],
           "xattn_wv": xattn_wv[0], "xattn_wo": xattn_wo[0], "ffn_w1": ffn_w1[0], "ffn_w2": ffn_w2[0]}
    small_sh = {"ssd_conv_w": ssd_conv_w.reshape(4, -1), "rwkv_w2": rwkv_w2[0], "rwkv_a2": rwkv_a2[0],
                "rwkv_g2": rwkv_g2[0]}
    cast_one = lambda n, deps=(): rowwise_fwd("cast_" + n, cast_fn, [big[n]], [], [(big[n].shape[1], BF16)],
                                              256 if big[n].shape[0] % 256 == 0 else big[n].shape[0], deps=deps)[0]
    gather_in, token_in = exchange_start("gather_in_start", [cast_one("w_in")] + list(small_sh.values()), "gather_chips")
    cast = {n: cast_one(n, deps=[token_in]) for n in big if n != "w_in"}
    late_a = ("w_out", "xattn_wq", "xattn_wk", "xattn_wv", "xattn_wo")
    late_b = ("ffn_w1", "ffn_w2")
    gather_a, token_a = exchange_start("gather_attn_start", [cast[n] for n in late_a], "gather_chips", dep=token_in)
    gather_b, token_b = exchange_start("gather_ffn_start", [cast[n] for n in late_b], "gather_chips", dep=token_a)
    (h1,) = rowwise_fwd("norm_mix", rmsnorm_fn, [xt], [norm_mix_g], [(d, BF16)], tm_light, deps=[token_b])
    forward_in, token_in = forward_start("gather_in", gather_in, after=h1)
    gathered = exchange_wait("gather_in_forward_wait", forward_in, after=token_in)
    g_big = {"w_in": gathered[0]}
    g_small = dict(zip(small_sh, gathered[1:]))

    pad_rows = lambda a: jnp.pad(a, ((0, LANES - a.shape[0]), (0, 0)))
    w_in_t = g_big["w_in"].reshape(D_IN, d)
    wt_z, wt_xbc, wt_rkv = (_rows(w_in_t, r) for r in (_Z, _XBC, _RKV))
    wt_ps = jnp.concatenate([_rows(w_in_t, _PG)] + [pad_rows(_rows(w_in_t, r)) for r in (_PW, _PA, _DT)], axis=0)
    unshard_cols = lambda g: jnp.transpose(g, (1, 0, 2)).reshape(g.shape[1], -1)
    conv_w_f = unshard_cols(g_small["ssd_conv_w"])
    w2p, a2p = pad_rows(unshard_cols(g_small["rwkv_w2"])), pad_rows(unshard_cols(g_small["rwkv_a2"]))
    g2_f = unshard_cols(g_small["rwkv_g2"])

    mu = rwkv_mu
    mu_rkv, mu_pg = mu[:, :3072], mu[:, 3264:3520]
    mu_pwa = jnp.concatenate([_pad_lanes(mu[:, 3072:3168]), _pad_lanes(mu[:, 3168:3264])], axis=1)
    dt_bias_p, a_log_p, d_p = _pad_lanes(ssd_dt_bias), _pad_lanes(ssd_a_log), _pad_lanes(ssd_d)
    r_k_row = rwkv_r_k.reshape(1, RWKV_WIDTH)
    g_final = final_norm_g.reshape(1, d)

    u_z = mm("in_z", h1, wt_z, "nt")
    u_xbc = mm("in_xbc", h1, wt_xbc, "nt")
    u_rkv = mm("in_rkv", h1, wt_rkv, "nt")
    u_ps = mm("in_narrow", h1, wt_ps, "nt")

    ssd_pre_rows = lambda: [Rows(u_xbc, shifts=(1, 2, 3)), Rows(u_ps, LANES, 4)]
    ssd_pre_params = [conv_w_f, ssd_conv_b, dt_bias_p]
    xs, bm, cm, dt = rowwise_fwd("ssd_pre", ssd_pre_fn, ssd_pre_rows(), ssd_pre_params,
                                 [(SSD_WIDTH, F32), (256, F32), (256, F32), (LANES, F32)], tm)
    y_scan, ssd_ck = ssd_scan_fwd(xs, bm, cm, dt, a_log_p, d_p)
    (y_ssd,) = rowwise_fwd("ssd_post", ssd_post_fn, [y_scan, u_z], [ssd_norm_g], [(SSD_WIDTH, BF16)], tm_light,
                           into=(None, d, 0))

    rwkv_pre_rows = lambda: [Rows(u_rkv, shifts=(1,)), Rows(u_ps, 2 * LANES, 0, shifts=(1,)), Rows(u_ps, 2 * LANES, 1, shifts=(1,))]
    rwkv_pre_params = [mu_rkv, mu_pg, mu_pwa, rwkv_w0, w2p, rwkv_a0, a2p, g2_f, rwkv_k_k, rwkv_k_a]
    forward_a, token_a = forward_start("gather_attn", gather_a, after=y_scan)
    r_, lw_, k_, v_, kap_, b_, gate_ = rowwise_fwd("rwkv_pre", rwkv_pre_fn, rwkv_pre_rows(), rwkv_pre_params,
                                                   [(RWKV_WIDTH, F32)] * 7, tm, deps=[token_a])
    ys_r, rwkv_ck, rwkv_inv = rwkv_scan_fwd(r_, lw_, k_, v_, kap_, b_)
    forward_b, token_b = forward_start("gather_ffn", gather_b, after=ys_r)
    g_big.update(zip(late_a, exchange_wait("gather_attn_forward_wait", forward_a, after=token_b)))
    w_out_f = g_big["w_out"].reshape(d, d)
    wq_f, wk_f, wv_f, wo_f = (g_big[n].reshape(d, d) for n in ("xattn_wq", "xattn_wk", "xattn_wv", "xattn_wo"))
    rwkv_post_params = [rwkv_ln_w, rwkv_ln_b, r_k_row]
    (ycat,) = rowwise_fwd("rwkv_post", rwkv_post_fn, [ys_r, r_, k_, v_, gate_], rwkv_post_params,
                          [(RWKV_WIDTH, BF16)], tm, into=(y_ssd, d, 1))
    x1 = mm("out_proj", ycat, w_out_f, "nn", res=xt)

    (h2,) = rowwise_fwd("norm_x", rmsnorm_fn, [x1], [norm_x_g], [(d, BF16)], tm_light)
    (mn,) = rowwise_fwd("norm_mem", rmsnorm_fn, [memt], [norm_mem_g], [(d, BF16)], tm)
    q = mm("xattn_q", h2, wq_f, "nn", out_dtype=BF16)
    kx = mm("xattn_k", mn, wk_f, "nn")
    vx = mm("xattn_v", mn, wv_f, "nn")
    (o,) = rowwise_fwd("xattn", attn_fn, [q], [kx, vx], [(d, BF16)], tm_light)
    x2 = mm("xattn_o", o, wo_f, "nn", res=x1)

    (h3,) = rowwise_fwd("norm_ffn", rmsnorm_fn, [x2], [norm_ffn_g], [(d, BF16)], tm_light)
    w1_s, w2_g = exchange_wait("gather_ffn_forward_wait", forward_b, after=h3)
    w2_f = w2_g.reshape(D_FF, d)
    relu2_epi = lambda acc: (jnp.square(jnp.maximum(acc, 0.0)), jnp.maximum(acc, 0.0))
    hid, relu_a = mm("ffn_1", h3, w1_s, "nn", b_slabs=N_DEV, epi=relu2_epi, out_dtypes=[BF16, BF16])
    x3 = mm("ffn_2", hid, w2_f, "nn", res=x2)

    loss_blk, dx3, dx3_b, dg_final = loss_and_grad(x3, tgt, g_final, tm_light)

    grads = {}
    grads["ffn_w2"] = mm("d_ffn_w2", hid, dx3_b, "tn", out_dtype=BF16).reshape(N_DEV, D_FF // N_DEV, d)
    sc_w2, tok = exchange_start("scatter_ffn_w2_start", [grads["ffn_w2"]], "scatter")
    da = mm("d_hid", dx3_b, w2_f, "nt", dep=tok, epi=lambda acc, ra: (2.0 * acc * ra,), extras=[relu_a],
            out_dtypes=[BF16])
    grads["ffn_w1"] = mm("d_ffn_w1", h3, da, "tn", out_dtype=BF16, out_slabs=N_DEV)
    sc_w1, tok = exchange_start("scatter_ffn_w1_start", [grads["ffn_w1"]], "scatter")
    dh3 = mm("d_h3", da, w1_s, "nt", out_dtype=BF16, b_slabs=N_DEV, dep=tok)
    (dx2,), (dg_ffn,) = rowwise_bwd("norm_ffn_bwd", rmsnorm_fn, [x2], [norm_ffn_g], [[dh3]], tm_light, [F32], row_add=[dx3])

    grads["xattn_wo"] = mm("d_wo", o, dx2, "tn", out_dtype=BF16).reshape(N_DEV, d // N_DEV, d)
    sc_wo, tok = exchange_start("scatter_wo_start", [grads["xattn_wo"]], "scatter")
    d_o = mm("d_o", dx2, wo_f, "nt", out_dtype=BF16, dep=tok)
    (dq,), (dkx, dvx) = rowwise_bwd("xattn_bwd", attn_fn, [q], [kx, vx], [[d_o]], tm_light, [BF16])
    grads["xattn_wq"] = mm("d_wq", h2, dq, "tn", out_dtype=BF16).reshape(N_DEV, d // N_DEV, d)
    grads["xattn_wk"] = mm("d_wk", mn, dkx, "tn", out_dtype=BF16).reshape(N_DEV, d // N_DEV, d)
    grads["xattn_wv"] = mm("d_wv", mn, dvx, "tn", out_dtype=BF16).reshape(N_DEV, d // N_DEV, d)
    qkv = ("xattn_wq", "xattn_wk", "xattn_wv")
    sc_qkv, tok = exchange_start("scatter_qkv_start", [grads[n] for n in qkv], "scatter")
    dmn = mm("d_mn_v", dvx, wv_f, "nt", res=mm("d_mn_k", dkx, wk_f, "nt", dep=tok))
    _, (dg_mem,) = rowwise_bwd("norm_mem_bwd", rmsnorm_fn, [memt], [norm_mem_g], [[dmn]], tm, [None])
    dh2 = mm("d_h2", dq, wq_f, "nt", out_dtype=BF16, dep=dg_mem)
    (dx1,), (dg_x,) = rowwise_bwd("norm_x_bwd", rmsnorm_fn, [x1], [norm_x_g], [[dh2]], tm_light, [F32], row_add=[dx2])

    grads["w_out"] = mm("d_w_out", ycat, dx1, "tn", out_dtype=BF16).reshape(N_DEV, d // N_DEV, d)
    sc_wout, tok = exchange_start("scatter_w_out_start", [grads["w_out"]], "scatter")
    d_ycat = mm("d_ycat", dx1, w_out_f, "nt", out_dtype=BF16, dep=tok)

    (d_ys, d_r1, d_k1, d_v1, d_gate), (dln_w, dln_b, dr_k) = rowwise_bwd(
        "rwkv_post_bwd", rwkv_post_fn, [ys_r, r_, k_, v_, gate_], rwkv_post_params,
        [[Rows(d_ycat, RWKV_WIDTH, 1)]], tm, [BF16] * 5)
    d_r2, d_lw, d_k2, d_v2, d_kap, d_b = rwkv_scan_bwd(r_, lw_, k_, v_, kap_, b_, rwkv_ck, rwkv_inv, d_ys)
    (du_rkv, du_pg, du_pwa), rwkv_pg = rowwise_bwd(
        "rwkv_pre_bwd", rwkv_pre_fn, rwkv_pre_rows(), rwkv_pre_params,
        [[d_r1, d_r2], [d_lw], [d_k1, d_k2], [d_v1, d_v2], [d_kap], [d_b], [d_gate]], tm, [BF16] * 3)
    dmu_rkv, dmu_pg, dmu_pwa, dw0, dw2p, da0, da2p, dg2, dk_k, dk_a = rwkv_pg

    (d_yscan, du_z), (dssd_norm_g,) = rowwise_bwd("ssd_post_bwd", ssd_post_fn, [y_scan, u_z], [ssd_norm_g],
                                                  [[Rows(d_ycat, SSD_WIDTH, 0)]], tm_light, [BF16, BF16])
    dxs, dbm, dcm, ddt2, da_log_p, dd_p = ssd_scan_bwd(xs, bm, cm, dt, a_log_p, d_p, ssd_ck, d_yscan)
    (du_xbc, du_dt), (dconv_w, dconv_b, ddt_bias_p) = rowwise_bwd(
        "ssd_pre_bwd", ssd_pre_fn, ssd_pre_rows(), ssd_pre_params,
        [[dxs], [dbm], [dcm], [ddt2[0], ddt2[1]]], tm, [BF16, BF16])
    du_ps = jnp.concatenate([du_pg, du_pwa, du_dt], axis=1)

    dwt_z = mm("d_w_z", du_z, h1, "tn", out_dtype=BF16)
    dwt_xbc = mm("d_w_xbc", du_xbc, h1, "tn", out_dtype=BF16)
    dwt_rkv = mm("d_w_rkv", du_rkv, h1, "tn", out_dtype=BF16)
    dwt_ps = mm("d_w_narrow", du_ps, h1, "tn", out_dtype=BF16)
    dwt_full = jnp.concatenate([dwt_z, dwt_xbc, dwt_ps[512:528], dwt_rkv, dwt_ps[256:352], dwt_ps[384:480], dwt_ps[0:256]],
                               axis=0)
    to_slabs = lambda g: jnp.transpose(g.reshape(g.shape[0], N_DEV, -1), (1, 0, 2))
    grads["w_in"] = dwt_full.reshape(N_DEV, D_IN // N_DEV, d)
    grads["ssd_conv_w"] = to_slabs(dconv_w)
    grads["rwkv_w2"] = to_slabs(dw2p[:96])
    grads["rwkv_a2"] = to_slabs(da2p[:96])
    grads["rwkv_g2"] = to_slabs(dg2)
    tail = ("w_in", "ssd_conv_w", "rwkv_w2", "rwkv_a2", "rwkv_g2")
    sc_tail, tok = exchange_start("scatter_tail_start", [grads[n] for n in tail], "scatter")
    dh1 = mm("d_h1_z", du_z, wt_z, "nn", dep=tok)
    dh1 = mm("d_h1_xbc", du_xbc, wt_xbc, "nn", res=dh1)
    dh1 = mm("d_h1_rkv", du_rkv, wt_rkv, "nn", res=dh1)
    dh1 = mm("d_h1_narrow", du_ps, wt_ps, "nn", res=dh1)
    (dx,), (dg_mix,) = rowwise_bwd("norm_mix_bwd", rmsnorm_fn, [xt], [norm_mix_g], [[dh1]], tm_light, [F32], row_add=[dx1])

    dmu =jnp.concatenate([dmu_rkv, dmu_pwa[:, 0:96], dmu_pwa[:, 128:224], dmu_pg], axis=1)
    small_grads = {
        "norm_mix_g": dg_mix, "ssd_conv_b": dconv_b, "ssd_dt_bias": ddt_bias_p[:, :16], "ssd_a_log": da_log_p[:, :16],
        "ssd_d": dd_p[:, :16], "ssd_norm_g": dssd_norm_g, "rwkv_mu": dmu, "rwkv_w0": dw0, "rwkv_a0": da0,
        "rwkv_k_k": dk_k, "rwkv_k_a": dk_a, "rwkv_r_k": dr_k, "rwkv_ln_w": dln_w, "rwkv_ln_b": dln_b,
        "norm_x_g": dg_x, "norm_mem_g": dg_mem, "norm_ffn_g": dg_ffn, "final_norm_g": dg_final}

    gather_small, tok = exchange_start("gather_small_start", [_pack_small(small_grads)], "gather")
    received = {}
    for names, handle in ((("ffn_w2",), sc_w2), (("ffn_w1",), sc_w1), (("xattn_wo",), sc_wo), (qkv, sc_qkv),
                          (("w_out",), sc_wout)):
        received.update(zip(names, exchange_wait("scatter_" + names[0] + "_wait", handle, after=tok)))

    out_g, out_d, out_m, out_v = {}, {}, {}, {}

    def run_adamw(n, dep):
        shape = wts[n].shape
        two_d = lambda a: a.reshape(-1, shape[-1])
        if n == "w_in":
            recv = jnp.transpose(sum_slabs("sum_w_in", received[n]))[None]
        else:
            recv = received[n].reshape(N_DEV, -1, shape[-1])
        res = adamw("adamw_" + n, recv, two_d(wts[n]), two_d(mom_m[n]), two_d(mom_v[n]), dep=dep)
        out_g[n], out_d[n], out_m[n], out_v[n] = (r.reshape(shape) for r in res)
        return res[0]

    last = None
    for n in ("ffn_w2", "ffn_w1", "xattn_wo") + qkv + ("w_out",):
        last = run_adamw(n, last)
    received.update(zip(tail, exchange_wait("scatter_tail_wait", sc_tail, after=last)))
    for n in tail:
        last = run_adamw(n, last)
    (small_all,) = exchange_wait("gather_small_wait", gather_small, after=last)
    res = adamw("adamw_small", small_all, _pack_small(wts), _pack_small(mom_m), _pack_small(mom_v))
    shapes = {n: wts[n].shape for n in _SMALL}
    for dst, packed in zip((out_g, out_d, out_m, out_v), res):
        dst.update(_unpack_small(packed, shapes))

    loss = lax.psum(loss_blk[0, 0], ("x", "y", "c"))
    return (loss, dx[None], *[out_g[n] for n in _WEIGHTS], *[out_d[n] for n in _WEIGHTS],
            *[out_m[n] for n in _WEIGHTS], *[out_v[n] for n in _WEIGHTS])
```

```python
import functools

import jax
import jax.numpy as jnp
from jax import lax
from jax.experimental import pallas as pl
from jax.experimental.pallas import tpu as pltpu

F32 = jnp.float32
BF16 = jnp.bfloat16

N_DEV = 8
D_MODEL = 2048
NORM_EPS = 1e-6
SSD_WIDTH = 1024
SSD_HEAD_DIM = 64
SSD_STATE = 128
SSD_CHUNK = 128
SSD_HEADS_PER_GROUP = 8
RWKV_WIDTH = 1024
RWKV_HEADS = 16
RWKV_HEAD_DIM = 64
RWKV_LN_EPS = 64e-5
RWKV_CHUNK = 128
RWKV_HEADS_PER_STEP = 16
XATTN_HEADS = 4
XATTN_HEAD_DIM = 512
D_FF = 8192
LANES = 128
SUBLANES = 8
VMEM_LIMIT = 56 * 1024 * 1024

ADAM_LR = 0.001
ADAM_B1 = 0.9
ADAM_B2 = 0.999
ADAM_EPS = 1e-08
ADAM_WD = 0.01
ADAM_STEP = 10

_DN = {"nn": ((1,), (0,)), "nt": ((1,), (1,)), "tn": ((0,), (0,))}


def _dg(a, b, mode):
    (ca,), (cb,) = _DN[mode]
    dn = (((ca + 1,), (cb + 1,)), ((0,), (0,))) if a.ndim == 3 else (((ca,), (cb,)), ((), ()))
    return lax.dot_general(a, b, dn, preferred_element_type=F32)


@functools.partial(jax.custom_vjp, nondiff_argnums=(2,))
def bdot(a, b, mode):
    return _dg(a.astype(BF16), b.astype(BF16), mode)


def _bdot_fwd(a, b, mode):
    return bdot(a, b, mode), (a, b)


def _bdot_bwd(mode, res, g):
    a, b = res
    ab, bb, gb = a.astype(BF16), b.astype(BF16), g.astype(BF16)
    if mode == "nn":
        da, db = _dg(gb, bb, "nt"), _dg(ab, gb, "tn")
    elif mode == "nt":
        da, db = _dg(gb, bb, "nn"), _dg(gb, ab, "tn")
    else:
        da, db = _dg(bb, gb, "nt"), _dg(ab, gb, "nn")
    return da.astype(a.dtype), db.astype(b.dtype)


bdot.defvjp(_bdot_fwd, _bdot_bwd)


def _split2(x):
    hi = x.astype(BF16)
    return hi, (x - hi.astype(F32)).astype(BF16)


def _dot01(x, m01):
    hi, lo = _split2(x)
    return _dg(hi, m01, "nn") + _dg(lo, m01, "nn")


def _exact_dot_impl(a, b, mode, exact):
    if exact == "a":
        ae = a.astype(BF16)
        return sum(_dg(ae, part, mode) for part in _split2(b))
    be = b.astype(BF16)
    return sum(_dg(part, be, mode) for part in _split2(a))


@functools.partial(jax.custom_vjp, nondiff_argnums=(2, 3))
def exact_dot(a, b, mode, exact):
    return _exact_dot_impl(a, b, mode, exact)


def _exact_dot_fwd(a, b, mode, exact):
    return _exact_dot_impl(a, b, mode, exact), (a, b)


def _exact_dot_bwd(mode, exact, res, g):
    a, b = res
    if exact == "a":
        db = {"nn": lambda: _exact_dot_impl(a, g, "tn", "a"), "nt": lambda: _exact_dot_impl(g, a, "tn", "b"),
              "tn": lambda: _exact_dot_impl(a, g, "nn", "a")}[mode]()
        return jnp.zeros_like(a), db
    da = {"nn": lambda: _exact_dot_impl(g, b, "nt", "b"), "nt": lambda: _exact_dot_impl(g, b, "nn", "b"),
          "tn": lambda: _exact_dot_impl(b, g, "nt", "a")}[mode]()
    return da, jnp.zeros_like(b)


exact_dot.defvjp(_exact_dot_fwd, _exact_dot_bwd)


def _head_indicator(width, heads, transpose):
    hd = width // heads
    shape = (LANES, width) if transpose else (width, LANES)
    lane = lax.broadcasted_iota(jnp.int32, shape, 1 if not transpose else 0)
    pos = lax.broadcasted_iota(jnp.int32, shape, 0 if not transpose else 1)
    return ((pos >= lane * hd) & (pos < lane * hd + hd)).astype(BF16)


@jax.custom_vjp
def head_sum(x):
    w = x.shape[-1]
    e = _head_indicator(w, w // RWKV_HEAD_DIM, False)
    et = _head_indicator(w, w // RWKV_HEAD_DIM, True)
    return _dot01(_dot01(x, e), et)


head_sum.defvjp(lambda x: (head_sum(x), None), lambda _, g: (head_sum(g),))


def rmsnorm_fn(x, g):
    y = x * lax.rsqrt(jnp.mean(x * x, axis=-1, keepdims=True) + NORM_EPS)
    return ((y * g).astype(BF16),)


def cast_fn(x):
    return (x.astype(BF16),)


def ssd_pre_fn(xbc, xbc1, xbc2, xbc3, dt_raw, conv_w, conv_b, dt_bias):
    c = conv_w[3:4] * xbc + conv_w[2:3] * xbc1 + conv_w[1:2] * xbc2 + conv_w[0:1] * xbc3 + conv_b
    act = c * jax.nn.sigmoid(c)
    dt = jax.nn.softplus(dt_raw + dt_bias)
    return act[:, :SSD_WIDTH], act[:, SSD_WIDTH:SSD_WIDTH + 256], act[:, SSD_WIDTH + 256:], dt


def ssd_post_fn(yscan, z, norm_g):
    y = yscan * (z * jax.nn.sigmoid(z))
    half = SSD_WIDTH // 2
    parts = []
    for g in range(2):
        yg = y[:, g * half:(g + 1) * half]
        parts.append(yg * lax.rsqrt(jnp.mean(yg * yg, axis=-1, keepdims=True) + NORM_EPS))
    return ((jnp.concatenate(parts, axis=-1) * norm_g).astype(BF16),)


def rwkv_pre_fn(rkv, rkv_p, pg, pg_p, pwa, pwa_p, mu_rkv, mu_pg, mu_pwa, w0, w2p, a0, a2p, g2, k_k, k_a):
    w = RWKV_WIDTH
    rkv = rkv + (rkv_p - rkv) * mu_rkv
    pg = pg + (pg_p - pg) * mu_pg
    pwa = pwa + (pwa_p - pwa) * mu_pwa
    r, k, v = rkv[:, :w], rkv[:, w:2 * w], rkv[:, 2 * w:]
    pw, pa = pwa[:, :LANES], pwa[:, LANES:]
    w_log = -jax.nn.softplus(-(w0 + bdot(jnp.tanh(pw), w2p, "nn"))) - 0.5
    lw = -jnp.exp(w_log)
    iclr = jax.nn.sigmoid(a0 + bdot(pa, a2p, "nn"))
    gate = bdot(jax.nn.sigmoid(pg), g2, "nn")
    kk = k * k_k
    kap = kk * lax.rsqrt(jnp.maximum(head_sum(kk * kk), 1e-24))
    k_mod = k * (1.0 + (iclr - 1.0) * k_a)
    return r, lw, k_mod, v, kap, kap * iclr, gate


def rwkv_post_fn(ys, r, k_mod, v, gate, ln_w, ln_b, r_k):
    inv_n = 1.0 / RWKV_HEAD_DIM
    mean = head_sum(ys) * inv_n
    yc = ys - mean
    var = head_sum(yc * yc) * inv_n
    yn = yc * lax.rsqrt(var + RWKV_LN_EPS) * ln_w + ln_b
    bonus = head_sum(r * k_mod * r_k) * v
    return (((yn + bonus) * gate).astype(BF16),)


def attn_fn(q, kx, vx):
    outs = []
    for h in range(XATTN_HEADS):
        sl = slice(h * XATTN_HEAD_DIM, (h + 1) * XATTN_HEAD_DIM)
        s = bdot(q[:, sl], kx[:, sl], "nt") * (XATTN_HEAD_DIM ** -0.5)
        s = s - jnp.max(s, axis=-1, keepdims=True)
        p = jnp.exp(s)
        p = p / jnp.sum(p, axis=-1, keepdims=True)
        outs.append(bdot(p, vx[:, sl], "nn"))
    return (jnp.concatenate(outs, axis=-1).astype(BF16),)


def loss_fn(x, tgt, g):
    y = x * lax.rsqrt(jnp.mean(x * x, axis=-1, keepdims=True) + NORM_EPS) * g
    err = jnp.square(y - tgt)
    return 0.5 * jnp.sum(jnp.mean(err, axis=-1, keepdims=True), axis=0, keepdims=True)


def _tri_masks(n):
    row = lax.broadcasted_iota(jnp.int32, (n, n), 0)
    col = lax.broadcasted_iota(jnp.int32, (n, n), 1)
    return col <= row, col < row, row == col


@jax.custom_vjp
def unit_lower_inverse(a):
    c = a.shape[-1]
    eye = _tri_masks(c)[2].astype(F32)
    m = -a
    inv = eye + m
    n = 1
    while n * 2 < c:
        m = bdot(m, m, "nn")
        inv = bdot(inv, eye + m, "nn")
        n *= 2
    return inv


def _unit_lower_inverse_fwd(a):
    inv = unit_lower_inverse(a)
    return inv, inv


def _unit_lower_inverse_bwd(inv, g):
    return (-bdot(bdot(inv, g, "tn"), inv, "nt"),)


unit_lower_inverse.defvjp(_unit_lower_inverse_fwd, _unit_lower_inverse_bwd)


@jax.custom_vjp
def known_inverse(a, inv):
    return inv


known_inverse.defvjp(lambda a, inv: (inv, inv),
                     lambda inv, g: (_unit_lower_inverse_bwd(inv, g)[0], jnp.zeros_like(inv)))


def rwkv_chunk_fn(st0, r, lw, k, v, kap, b, inv=None):
    h, c = r.shape[0], r.shape[1]
    incl, strict, _ = _tri_masks(c)
    cum = exact_dot(jnp.broadcast_to(incl.astype(F32), (h, c, c)), lw, "nn", "a")
    g_in = jnp.exp(cum)
    g_prev = jnp.exp(cum - lw)
    g_inv = jnp.exp(-cum)
    g_end = jnp.exp(cum[:, c - 1:c, :] - cum)
    kap_t, k_t, b_t, r_t = kap * g_prev, k * g_inv, b * g_inv, r * g_in
    a_ub = jnp.where(strict, bdot(kap_t, b_t, "nt"), 0.0)
    a_vk = jnp.where(strict, bdot(kap_t, k_t, "nt"), 0.0)
    rhs = -(bdot(kap_t, st0, "nt") + bdot(a_vk, v, "nn"))
    inv = unit_lower_inverse(a_ub) if inv is None else known_inverse(a_ub, inv)
    u = bdot(inv, rhs, "nn")
    y = (bdot(r_t, st0, "nt")
         + bdot(jnp.where(incl, bdot(r_t, k_t, "nt"), 0.0), v, "nn")
         + bdot(jnp.where(incl, bdot(r_t, b_t, "nt"), 0.0), u, "nn"))
    st1 = jnp.exp(cum[:, c - 1:c, :]) * st0 + bdot(v, k * g_end, "tn") + bdot(u, b * g_end, "tn")
    return y, st1, inv


def ssd_chunk_fn(group, h0, xs, bm, cm, dt, a_log, d_skip):
    q, nh = xs.shape[0], SSD_HEADS_PER_GROUP
    causal, _, _ = _tri_masks(q)
    a_row = -jnp.exp(a_log)
    cs_all = exact_dot(causal.astype(F32), dt * a_row, "nn", "a")
    cs_t = cs_all.T
    lanes = range(group * nh, (group + 1) * nh)
    cs = jnp.stack([cs_all[:, hl:hl + 1] for hl in lanes])
    cs_row = jnp.stack([cs_t[hl:hl + 1, :] for hl in lanes])
    dt_h = jnp.stack([dt[:, hl:hl + 1] for hl in lanes])
    d_h = jnp.stack([d_skip[:, hl:hl + 1] for hl in lanes])
    x = _stack_lanes(xs, nh)
    h0s = _stack_rows(h0, nh)
    lmat = jnp.where(causal, jnp.exp(jnp.where(causal, cs - cs_row, 0.0)), 0.0)
    cb = bdot(cm, bm, "nt")
    xdt = x * dt_h
    cl = cs[:, q - 1:q, :]
    cm_b = jnp.broadcast_to(cm, (nh,) + cm.shape)
    bm_b = jnp.broadcast_to(bm, (nh,) + bm.shape)
    y = bdot(cb * lmat, xdt, "nn") + bdot(cm_b, h0s, "nt") * jnp.exp(cs) + x * d_h
    h1 = h0s * jnp.exp(cl) + bdot(xdt * jnp.exp(cl - cs), bm_b, "tn")
    return jnp.concatenate([y[e] for e in range(nh)], axis=-1), jnp.concatenate([h1[e] for e in range(nh)], axis=0)


class Rows:
    def __init__(self, arr, w=None, cb=0, shifts=()):
        self.arr, self.w, self.cb, self.shifts = arr, (arr.shape[1] if w is None else w), cb, tuple(shifts)


def _as_rows(x):
    return x if isinstance(x, Rows) else Rows(x)


def _shift_down(x, halo, k):
    rolled = pltpu.roll(x, k, 0)
    first = rolled[0:SUBLANES]
    rid = lax.broadcasted_iota(jnp.int32, first.shape, 0)
    patched = jnp.where(rid < k, pltpu.roll(halo, k, 0), first)
    return jnp.concatenate([patched, rolled[SUBLANES:]], axis=0)


def _shift_up(g, carry, k):
    tm = g.shape[0]
    rolled = pltpu.roll(g, tm - k, 0)
    last = rolled[tm - SUBLANES:]
    rid = lax.broadcasted_iota(jnp.int32, last.shape, 0)
    patched = jnp.where(rid >= SUBLANES - k, pltpu.roll(carry, SUBLANES - k, 0), last)
    return jnp.concatenate([rolled[:tm - SUBLANES], patched], axis=0)


def _params():
    return pltpu.CompilerParams(vmem_limit_bytes=VMEM_LIMIT)


def _load_rows(refs, pos, rins, first_block):
    vals = []
    for r in rins:
        x = refs[pos][...].astype(F32) if refs[pos].dtype != F32 else refs[pos][...]
        pos += 1
        vals.append(x)
        if r.shifts:
            halo = refs[pos][...]
            pos += 1
            halo = jnp.where(first_block, jnp.zeros_like(halo), halo)
            for k in r.shifts:
                vals.append(_shift_down(x, halo, k))
    return vals, pos


def _row_specs(rins, tm, blk):
    specs, args = [], []
    for r in rins:
        specs.append(pl.BlockSpec((tm, r.w), lambda i, cb=r.cb: (blk(i), cb)))
        args.append(r.arr)
        if r.shifts:
            per = tm // SUBLANES
            specs.append(pl.BlockSpec((SUBLANES, r.w), lambda i, cb=r.cb: (jnp.maximum(blk(i) * per - 1, 0), cb)))
            args.append(r.arr)
    return specs, args


def rowwise_fwd(name, fn, rins, params, outs, tm, deps=(), into=None):
    rins = [_as_rows(r) for r in rins]
    t = rins[0].arr.shape[0]
    tm = min(tm, t)
    nb = t // tm
    specs, args = _row_specs(rins, tm, lambda i: i)
    for p in params:
        specs.append(pl.BlockSpec(p.shape, lambda i: (0, 0)))
        args.append(p)
    for dep in deps:
        specs.append(pl.BlockSpec(memory_space=pl.ANY))
        args.append(dep)
    out_specs = [pl.BlockSpec((tm, w), lambda i: (i, 0)) for w, _ in outs]
    out_shape = [jax.ShapeDtypeStruct((t, w), dt) for w, dt in outs]
    aliases = {}
    if into is not None:
        target, total_width, col_block = into
        out_specs = [pl.BlockSpec((tm, outs[0][0]), lambda i: (i, col_block))]
        out_shape = [jax.ShapeDtypeStruct((t, total_width), outs[0][1])]
        if target is not None:
            aliases = {len(args): 0}
            specs.append(pl.BlockSpec(memory_space=pl.ANY))
            args.append(target)
    n_in = len(args)

    def body(*refs):
        vals, pos = _load_rows(refs, 0, rins, pl.program_id(0) == 0)
        pv = [refs[pos + j][...] for j in range(len(params))]
        res = fn(*vals, *pv)
        for o_ref, o in zip(refs[n_in:], res):
            o_ref[...] = o.astype(o_ref.dtype)

    return pl.pallas_call(
        body, name=name, grid=(nb,), in_specs=specs, out_specs=out_specs, out_shape=out_shape,
        input_output_aliases=aliases, compiler_params=_params(),
    )(*args)


def rowwise_bwd(name, fn, rins, params, cts, tm, grad_dtypes, row_add=None):
    rins = [_as_rows(r) for r in rins]
    cts = [[_as_rows(c) for c in lst] for lst in cts]
    row_add = [_as_rows(a) for a in (row_add or [])]
    t = rins[0].arr.shape[0]
    tm = min(tm, t)
    nb = t // tm
    rev = lambda i: nb - 1 - i
    specs, args = _row_specs(rins, tm, rev)
    for p in params:
        specs.append(pl.BlockSpec(p.shape, lambda i: (0, 0)))
        args.append(p)
    flat_cts = [c for lst in cts for c in lst] + row_add
    for c in flat_cts:
        specs.append(pl.BlockSpec((tm, c.w), lambda i, cb=c.cb: (rev(i), cb)))
        args.append(c.arr)
    n_in = len(args)
    want = [i for i, d in enumerate(grad_dtypes) if d is not None]
    out_specs = [pl.BlockSpec((tm, rins[i].w), lambda i_: (rev(i_), 0)) for i in want]
    out_shape = [jax.ShapeDtypeStruct((t, rins[i].w), grad_dtypes[i]) for i in want]
    out_specs += [pl.BlockSpec(p.shape, lambda i: (0, 0)) for p in params]
    out_shape += [jax.ShapeDtypeStruct(p.shape, F32) for p in params]
    n_out = len(out_shape)
    scratch = [pltpu.VMEM((SUBLANES, r.w), F32) for r in rins for _ in r.shifts]

    def body(*refs):
        i = pl.program_id(0)
        vals, pos = _load_rows(refs, 0, rins, rev(i) == 0)
        pv = [refs[pos + j][...] for j in range(len(params))]
        pos += len(params)
        outs, vjp = jax.vjp(fn, *vals, *pv)
        ct_vals = []
        for o, lst in zip(outs, cts):
            acc = None
            for _ in lst:
                cv = refs[pos][...].astype(F32)
                pos += 1
                acc = cv if acc is None else acc + cv
            ct_vals.append(acc.astype(o.dtype))
        adds = [refs[pos + j][...].astype(F32) for j in range(len(row_add))]
        grads = vjp(tuple(ct_vals))
        out_refs = refs[n_in:n_in + n_out]
        carry_refs = refs[n_in + n_out:]

        @pl.when(i == 0)
        def _():
            for cr in carry_refs:
                cr[...] = jnp.zeros_like(cr)
            for pr in out_refs[len(want):]:
                pr[...] = jnp.zeros_like(pr)

        gi, ci, oi = 0, 0, 0
        for idx, r in enumerate(rins):
            d = grads[gi]
            gi += 1
            for k in r.shifts:
                dk = grads[gi]
                gi += 1
                d = d + _shift_up(dk, carry_refs[ci][...], k)
                carry_refs[ci][...] = dk[0:SUBLANES]
                ci += 1
            if idx == 0:
                for a in adds:
                    d = d + a
            if grad_dtypes[idx] is not None:
                out_refs[oi][...] = d.astype(out_refs[oi].dtype)
                oi += 1
        for pr, gp in zip(out_refs[len(want):], grads[gi:]):
            pr[...] += gp

    res = pl.pallas_call(
        body, name=name, grid=(nb,), in_specs=specs, out_specs=out_specs, out_shape=out_shape,
        scratch_shapes=scratch, compiler_params=_params(),
    )(*args)
    return res[:len(want)], res[len(want):]


def _pick(n, pref):
    for c in pref:
        if n % c == 0:
            return c
    return n


MM_VMEM_BUDGET = 40 * 1024 * 1024
MM_PEAK_FLOPS = 0.9e15
MM_HBM_BYTES_PER_S = 3.0e12
MM_STEP_SECONDS = 0.35e-6


def _mm_tiles(m, n, k, size_a, size_b, size_out, size_res, single_k):
    best = None
    for tk in sorted({c for c in (k, 2048, 1024, 512, 256, 128) if c <= 2048 and k % c == 0}, reverse=True):
        for tm in sorted({c for c in (m, 1024, 512, 256, 128) if c <= 1024 and m % c == 0}, reverse=True):
            for tn in sorted({c for c in (n, 2048, 1536, 1024, 768, 512, 384, 256, 128) if c <= 2048 and n % c == 0},
                             reverse=True):
                nk = k // tk
                vmem = 2 * (tm * tk * size_a + tk * tn * size_b + tm * tn * (size_out + size_res))
                vmem += tm * tn * 4 * (2 if nk > 1 or not single_k else 1)
                vmem += (tm * tk * 2 if size_a > 2 else 0) + (tk * tn * 2 if size_b > 2 else 0)
                if vmem > MM_VMEM_BUDGET:
                    continue
                steps = (m // tm) * (n // tn) * nk
                a_reads = 1 if (nk == 1 and single_k) else n // tn
                traffic = m * k * size_a * a_reads + k * n * size_b * (m // tm) + m * n * (size_out + size_res)
                t_mxu, t_hbm = 2.0 * m * n * k / MM_PEAK_FLOPS, traffic / MM_HBM_BYTES_PER_S
                exposed = min(t_mxu, t_hbm) / steps if steps <= 2 else 0.0
                cost = max(t_mxu, t_hbm) + exposed + steps * MM_STEP_SECONDS
                if best is None or cost < best[0]:
                    best = (cost, tm, tn, tk)
    return best[1:]


def mm(name, a, b, mode, out_dtype=F32, res=None, b_slabs=None, out_slabs=None, dep=None, epi=None, extras=(),
       out_dtypes=None):
    if mode == "tn":
        k_dim, m_dim = a.shape
    else:
        m_dim, k_dim = a.shape
    if b_slabs:
        n_dim = b.shape[0] * b.shape[2] if mode == "nn" else b.shape[1]
    else:
        n_dim = b.shape[0] if mode == "nt" else b.shape[1]
    n_slabs = out_slabs or (b_slabs if (b_slabs and mode == "nn") else 1)
    k_slabs = b_slabs if (b_slabs and mode == "nt") else 1
    if epi is None:
        out_dtypes = [out_dtype]
        if res is None:
            epi = lambda acc: (acc,)
        else:
            extras, epi = [res], lambda acc, r: (acc + r,)
    tm, tn, tk = _mm_tiles(m_dim, n_dim // n_slabs, k_dim // k_slabs, a.dtype.itemsize, b.dtype.itemsize,
                           sum(jnp.dtype(dt).itemsize for dt in out_dtypes), sum(e.dtype.itemsize for e in extras),
                           single_k=(k_slabs == 1))
    nji = n_dim // n_slabs // tn
    nki = k_dim // k_slabs // tk
    nblk = lambda js, j: js * nji + j
    kblk = lambda ks, k: ks * nki + k
    if mode == "tn":
        a_spec = pl.BlockSpec((tk, tm), lambda i, js, j, ks, k: (kblk(ks, k), i))
    else:
        a_spec = pl.BlockSpec((tm, tk), lambda i, js, j, ks, k: (i, kblk(ks, k)))
    if b_slabs and mode == "nn":
        b_spec = pl.BlockSpec((None, tk, tn), lambda i, js, j, ks, k: (js, k, j))
    elif b_slabs and mode == "nt":
        b_spec = pl.BlockSpec((None, tn, tk), lambda i, js, j, ks, k: (ks, nblk(js, j), k))
    elif mode == "nt":
        b_spec = pl.BlockSpec((tn, tk), lambda i, js, j, ks, k: (nblk(js, j), kblk(ks, k)))
    else:
        b_spec = pl.BlockSpec((tk, tn), lambda i, js, j, ks, k: (kblk(ks, k), nblk(js, j)))
    specs, args = [a_spec, b_spec], [a, b]
    for e in extras:
        specs.append(pl.BlockSpec((tm, tn), lambda i, js, j, ks, k: (i, nblk(js, j))))
        args.append(e)
    if dep is not None:
        specs.append(pl.BlockSpec(memory_space=pl.ANY))
        args.append(dep)
    if out_slabs:
        o_specs = [pl.BlockSpec((None, tm, tn), lambda i, js, j, ks, k: (js, i, j))]
        o_shapes = [jax.ShapeDtypeStruct((out_slabs, m_dim, n_dim // out_slabs), out_dtypes[0])]
    else:
        o_specs = [pl.BlockSpec((tm, tn), lambda i, js, j, ks, k: (i, nblk(js, j))) for _ in out_dtypes]
        o_shapes = [jax.ShapeDtypeStruct((m_dim, n_dim), dt) for dt in out_dtypes]

    one_k_step = k_slabs * nki == 1
    n_in, n_out = len(args), len(out_dtypes)

    def body(*refs):
        a_ref, b_ref = refs[0], refs[1]
        part = _dg(a_ref[...].astype(BF16), b_ref[...].astype(BF16), mode)

        def finish(acc):
            outs = epi(acc, *[refs[2 + j][...].astype(F32) for j in range(len(extras))])
            for o_ref, o in zip(refs[n_in:n_in + n_out], outs):
                o_ref[...] = o.astype(o_ref.dtype)

        if one_k_step:
            finish(part)
            return
        acc_ref = refs[n_in + n_out]
        ks, kk = pl.program_id(3), pl.program_id(4)

        @pl.when((ks == 0) & (kk == 0))
        def _():
            acc_ref[...] = part

        @pl.when((ks > 0) | (kk > 0))
        def _():
            acc_ref[...] += part

        pl.when((ks == k_slabs - 1) & (kk == nki - 1))(lambda: finish(acc_ref[...]))

    grid = (m_dim // tm, n_slabs, nji, k_slabs, nki)
    scratch = [] if one_k_step else [pltpu.VMEM((tm, tn), F32)]
    out = pl.pallas_call(
        body, name=name, grid=grid, in_specs=specs, out_specs=o_specs, out_shape=o_shapes, scratch_shapes=scratch,
        compiler_params=pltpu.CompilerParams(
            dimension_semantics=("parallel", "parallel", "parallel", "arbitrary", "arbitrary"),
            vmem_limit_bytes=VMEM_LIMIT),
    )(*args)
    return out[0] if n_out == 1 else out


def _stack_lanes(x, n):
    w = x.shape[1] // n
    return jnp.stack([x[:, i * w:(i + 1) * w] for i in range(n)])


def _stack_rows(x, n):
    w = x.shape[0] // n
    return jnp.stack([x[i * w:(i + 1) * w, :] for i in range(n)])


def rwkv_scan_fwd(r, lw, k, v, kap, b):
    t = r.shape[0]
    c, hps, hd = min(RWKV_CHUNK, t), RWKV_HEADS_PER_STEP, RWKV_HEAD_DIM
    nc, ng, wl = t // c, RWKV_HEADS // hps, hps * hd
    spec = pl.BlockSpec((c, wl), lambda g, ci: (ci, g))

    def body(r_ref, lw_ref, k_ref, v_ref, kap_ref, b_ref, y_ref, ck_ref, inv_ref, st_ref):
        @pl.when(pl.program_id(1) == 0)
        def _():
            st_ref[...] = jnp.zeros_like(st_ref)

        st = st_ref[...]
        ck_ref[...] = st
        ins = [x[...] for x in (r_ref, lw_ref, k_ref, v_ref, kap_ref, b_ref)]
        y, st1, inv = rwkv_chunk_fn(_stack_rows(st, hps), *[_stack_lanes(x, hps) for x in ins])
        y_ref[...] = jnp.concatenate([y[h] for h in range(hps)], axis=-1)
        st_ref[...] = jnp.concatenate([st1[h] for h in range(hps)], axis=0)
        inv_ref[...] = jnp.concatenate([inv[h] for h in range(hps)], axis=0)

    return pl.pallas_call(
        body, name="rwkv_scan_fwd", grid=(ng, nc), in_specs=[spec] * 6,
        out_specs=[spec, pl.BlockSpec((None, wl, hd), lambda g, ci: (ci, g, 0)),
                   pl.BlockSpec((None, hps * c, c), lambda g, ci: (ci, g, 0))],
        out_shape=[jax.ShapeDtypeStruct((t, RWKV_WIDTH), F32), jax.ShapeDtypeStruct((nc, RWKV_WIDTH, hd), F32),
                   jax.ShapeDtypeStruct((nc, RWKV_HEADS * c, c), F32)],
        scratch_shapes=[pltpu.VMEM((wl, hd), F32)], compiler_params=_params(),
    )(r, lw, k, v, kap, b)


def rwkv_scan_bwd(r, lw, k, v, kap, b, ck, inv_ck, dy):
    t = r.shape[0]
    c, hps, hd = min(RWKV_CHUNK, t), RWKV_HEADS_PER_STEP, RWKV_HEAD_DIM
    nc, ng, wl = t // c, RWKV_HEADS // hps, hps * hd
    spec = pl.BlockSpec((c, wl), lambda g, ci: (nc - 1 - ci, g))

    def body(r_ref, lw_ref, k_ref, v_ref, kap_ref, b_ref, ck_ref, inv_ref, dy_ref, *rest):
        out_refs, dst_ref = rest[:6], rest[6]

        @pl.when(pl.program_id(1) == 0)
        def _():
            dst_ref[...] = jnp.zeros_like(dst_ref)

        ins = [x[...] for x in (r_ref, lw_ref, k_ref, v_ref, kap_ref, b_ref)]
        dyv, ck, dst = dy_ref[...].astype(F32), ck_ref[...], dst_ref[...]
        chunk = lambda *a: rwkv_chunk_fn(*a, inv=_stack_rows(inv_ref[...], hps))[:2]
        _, vjp = jax.vjp(chunk, _stack_rows(ck, hps), *[_stack_lanes(x, hps) for x in ins])
        grads = vjp((_stack_lanes(dyv, hps), _stack_rows(dst, hps)))
        dst_ref[...] = jnp.concatenate([grads[0][h] for h in range(hps)], axis=0)
        for j in range(6):
            out_refs[j][...] = jnp.concatenate([grads[1 + j][h] for h in range(hps)], axis=-1).astype(BF16)

    return pl.pallas_call(
        body, name="rwkv_scan_bwd", grid=(ng, nc),
        in_specs=[spec] * 6 + [pl.BlockSpec((None, wl, hd), lambda g, ci: (nc - 1 - ci, g, 0)),
                               pl.BlockSpec((None, hps * c, c), lambda g, ci: (nc - 1 - ci, g, 0)), spec],
        out_specs=[spec] * 6, out_shape=[jax.ShapeDtypeStruct((t, RWKV_WIDTH), BF16)] * 6,
        scratch_shapes=[pltpu.VMEM((wl, hd), F32)], compiler_params=_params(),
    )(r, lw, k, v, kap, b, ck, inv_ck, dy)


def _ssd_specs(q, blk):
    gw = SSD_WIDTH // 2
    return [pl.BlockSpec((q, gw), lambda g, ci: (blk(ci), g)),
            pl.BlockSpec((q, SSD_STATE), lambda g, ci: (blk(ci), g)),
            pl.BlockSpec((q, SSD_STATE), lambda g, ci: (blk(ci), g)),
            pl.BlockSpec((q, LANES), lambda g, ci: (blk(ci), 0)),
            pl.BlockSpec((1, LANES), lambda g, ci: (0, 0)),
            pl.BlockSpec((1, LANES), lambda g, ci: (0, 0))]


def ssd_scan_fwd(xs, bm, cm, dt, a_log, d_skip):
    t = xs.shape[0]
    q = min(SSD_CHUNK, t)
    nc, gw = t // q, SSD_WIDTH // 2

    def body(xs_ref, bm_ref, cm_ref, dt_ref, al_ref, d_ref, y_ref, ck_ref, h_ref):
        @pl.when(pl.program_id(1) == 0)
        def _():
            h_ref[...] = jnp.zeros_like(h_ref)

        ck_ref[...] = h_ref[...]
        args = (h_ref[...], xs_ref[...], bm_ref[...], cm_ref[...], dt_ref[...], al_ref[...], d_ref[...])
        g = pl.program_id(0)

        @pl.when(g == 0)
        def _():
            y, h1 = ssd_chunk_fn(0, *args)
            y_ref[...] = y
            h_ref[...] = h1

        @pl.when(g == 1)
        def _():
            y, h1 = ssd_chunk_fn(1, *args)
            y_ref[...] = y
            h_ref[...] = h1

    return pl.pallas_call(
        body, name="ssd_scan_fwd", grid=(2, nc), in_specs=_ssd_specs(q, lambda ci: ci),
        out_specs=[pl.BlockSpec((q, gw), lambda g, ci: (ci, g)),
                   pl.BlockSpec((None, gw, SSD_STATE), lambda g, ci: (ci, g, 0))],
        out_shape=[jax.ShapeDtypeStruct((t, SSD_WIDTH), F32), jax.ShapeDtypeStruct((nc, SSD_WIDTH, SSD_STATE), F32)],
        scratch_shapes=[pltpu.VMEM((gw, SSD_STATE), F32)], compiler_params=_params(),
    )(xs, bm, cm, dt, a_log, d_skip)


def ssd_scan_bwd(xs, bm, cm, dt, a_log, d_skip, ck, dy):
    t = xs.shape[0]
    q = min(SSD_CHUNK, t)
    nc, gw = t // q, SSD_WIDTH // 2
    rev = lambda ci: nc - 1 - ci

    def body(xs_ref, bm_ref, cm_ref, dt_ref, al_ref, d_ref, ck_ref, dy_ref,
             dxs_ref, dbm_ref, dcm_ref, ddt_ref, dal_ref, dd_ref, dh_ref):
        g, ci = pl.program_id(0), pl.program_id(1)

        @pl.when(ci == 0)
        def _():
            dh_ref[...] = jnp.zeros_like(dh_ref)

        @pl.when((ci == 0) & (g == 0))
        def _():
            dal_ref[...] = jnp.zeros_like(dal_ref)
            dd_ref[...] = jnp.zeros_like(dd_ref)

        args = (ck_ref[...], xs_ref[...], bm_ref[...], cm_ref[...], dt_ref[...], al_ref[...], d_ref[...])

        def run(group):
            _, vjp = jax.vjp(functools.partial(ssd_chunk_fn, group), *args)
            dh0, dxs, dbm, dcm, ddt, dal, dd = vjp((dy_ref[...].astype(F32), dh_ref[...]))
            dh_ref[...] = dh0
            dxs_ref[...] = dxs.astype(BF16)
            dbm_ref[...] = dbm.astype(BF16)
            dcm_ref[...] = dcm.astype(BF16)
            ddt_ref[...] = ddt
            dal_ref[...] += dal
            dd_ref[...] += dd

        pl.when(g == 0)(lambda: run(0))
        pl.when(g == 1)(lambda: run(1))

    in_specs = _ssd_specs(q, rev) + [pl.BlockSpec((None, gw, SSD_STATE), lambda g, ci: (rev(ci), g, 0)),
                                     pl.BlockSpec((q, gw), lambda g, ci: (rev(ci), g))]
    return pl.pallas_call(
        body, name="ssd_scan_bwd", grid=(2, nc), in_specs=in_specs,
        out_specs=[pl.BlockSpec((q, gw), lambda g, ci: (rev(ci), g)),
                   pl.BlockSpec((q, SSD_STATE), lambda g, ci: (rev(ci), g)),
                   pl.BlockSpec((q, SSD_STATE), lambda g, ci: (rev(ci), g)),
                   pl.BlockSpec((None, q, LANES), lambda g, ci: (g, rev(ci), 0)),
                   pl.BlockSpec((1, LANES), lambda g, ci: (0, 0)),
                   pl.BlockSpec((1, LANES), lambda g, ci: (0, 0))],
        out_shape=[jax.ShapeDtypeStruct((t, SSD_WIDTH), BF16), jax.ShapeDtypeStruct((t, 2 * SSD_STATE), BF16),
                   jax.ShapeDtypeStruct((t, 2 * SSD_STATE), BF16), jax.ShapeDtypeStruct((2, t, LANES), F32),
                   jax.ShapeDtypeStruct((1, LANES), F32), jax.ShapeDtypeStruct((1, LANES), F32)],
        scratch_shapes=[pltpu.VMEM((gw, SSD_STATE), F32)], compiler_params=_params(),
    )(xs, bm, cm, dt, a_log, d_skip, ck, dy)


def loss_and_grad(x, tgt, g, tm):
    t, d = x.shape
    tm = min(tm, t)
    nb = t // tm

    def body(x_ref, t_ref, g_ref, loss_ref, dx_ref, dxb_ref, dg_ref):
        @pl.when(pl.program_id(0) == 0)
        def _():
            loss_ref[...] = jnp.zeros_like(loss_ref)
            dg_ref[...] = jnp.zeros_like(dg_ref)

        val, vjp = jax.vjp(loss_fn, x_ref[...], t_ref[...], g_ref[...])
        dx, _, dg = vjp(jnp.ones((1, 1), F32))
        loss_ref[...] += jnp.broadcast_to(val, loss_ref.shape)
        dx_ref[...] = dx
        dxb_ref[...] = dx.astype(BF16)
        dg_ref[...] += dg

    row = pl.BlockSpec((tm, d), lambda i: (i, 0))
    one = pl.BlockSpec((1, d), lambda i: (0, 0))
    return pl.pallas_call(
        body, name="loss_and_grad", grid=(nb,), in_specs=[row, row, one],
        out_specs=[pl.BlockSpec((SUBLANES, LANES), lambda i: (0, 0)), row, row, one],
        out_shape=[jax.ShapeDtypeStruct((SUBLANES, LANES), F32), jax.ShapeDtypeStruct((t, d), F32),
                   jax.ShapeDtypeStruct((t, d), BF16), jax.ShapeDtypeStruct((1, d), F32)],
        compiler_params=_params(),
    )(x, tgt, g)


def adamw(name, recv, w, m, v, dep=None):
    rows, cols = w.shape
    n_slabs = recv.shape[0]
    recv_block_bytes = 8 * 1024 * 1024
    tm = _pick(rows, [c for c in (256, 128, 64, 32, 16, 8)
                      if n_slabs * c * cols * recv.dtype.itemsize <= recv_block_bytes and 2 * c <= rows])
    c1 = 1.0 / (1.0 - ADAM_B1 ** ADAM_STEP)
    c2 = 1.0 / (1.0 - ADAM_B2 ** ADAM_STEP)

    n_dep = 0 if dep is None else 1

    def body(recv_ref, w_ref, m_ref, v_ref, *rest):
        g_ref, d_ref, nm_ref, nv_ref = rest[n_dep:]
        g = recv_ref[0].astype(F32)
        for p in range(1, n_slabs):
            g = g + recv_ref[p].astype(F32)
        nm =ADAM_B1 * m_ref[...] + (1.0 - ADAM_B1) * g
        nv = ADAM_B2 * v_ref[...] + (1.0 - ADAM_B2) * jnp.square(g)
        g_ref[...] = g
        nm_ref[...] = nm
        nv_ref[...] = nv
        d_ref[...] = -ADAM_LR * ((nm * c1) / (jnp.sqrt(nv * c2) + ADAM_EPS) + ADAM_WD * w_ref[...])

    blk = pl.BlockSpec((tm, cols), lambda i: (i, 0))
    return pl.pallas_call(
        body, name=name, grid=(rows // tm,),
        in_specs=[pl.BlockSpec((n_slabs, tm, cols), lambda i: (0, i, 0)), blk, blk, blk]
        + [pl.BlockSpec(memory_space=pl.ANY)] * n_dep,
        out_specs=[blk] * 4, out_shape=[jax.ShapeDtypeStruct((rows, cols), F32)] * 4,
        compiler_params=_params(),
    )(recv, w, m, v, *([] if dep is None else [dep]))


def _mesh_pos():
    return lax.axis_index("x"), lax.axis_index("y"), lax.axis_index("c")


def _peer(pos, mask):
    x, y, c = pos
    return (1 - x if mask & 4 else x, 1 - y if mask & 2 else y, 1 - c if mask & 1 else c)


def _linear(pos):
    return 4 * pos[0] + 2 * pos[1] + pos[2]


class Exchange:
    MASKS = {"gather": (1, 2, 3, 4, 5, 6, 7), "scatter": (1, 2, 3, 4, 5, 6, 7), "gather_chips": (1, 2, 4, 6),
             "forward": (2, 4, 6)}

    def __init__(self, xs, kind, lands=None):
        self.kind, self.masks = kind, self.MASKS[kind]
        self.xs = [] if kind == "forward" else list(xs)
        if kind == "forward":
            self.land_shape = [jax.ShapeDtypeStruct(l.shape, l.dtype) for l in lands]
        elif kind == "scatter":
            self.land_shape = [jax.ShapeDtypeStruct(x.shape, x.dtype) for x in xs]
        else:
            self.land_shape = [jax.ShapeDtypeStruct((N_DEV,) + x.shape, x.dtype) for x in xs]
        self.n = len(self.land_shape)
        copies = self.n * len(self.masks)
        self.sems = [pltpu.SemaphoreType.DMA((copies,)), pltpu.SemaphoreType.DMA((copies,)),
                     pltpu.SemaphoreType.DMA((self.n,))]

    def _copies(self, ins, outs, sems, landing):
        send_sems, recv_sems, local_sems = sems
        me = _mesh_pos()
        me_lin = _linear(me)
        local, remote = [], []
        for ti in range(self.n):
            if self.kind != "forward":
                src_mine = ins[ti].at[me_lin] if self.kind == "scatter" else ins[ti]
                local.append(pltpu.make_async_copy(src_mine, outs[ti].at[me_lin], local_sems.at[ti]))
            for j, mask in enumerate(self.masks):
                if self.kind == "forward":
                    peer = _peer(me, 1)
                    src = outs[ti].at[_linear(_peer(me, mask))]
                    dst = outs[ti].at[_linear(_peer(me, mask ^ 1 if landing else mask))]
                else:
                    peer = _peer(me, mask)
                    src = ins[ti].at[_linear(peer)] if self.kind == "scatter" else ins[ti]
                    dst = outs[ti].at[_linear(peer) if landing else me_lin]
                sem_index = ti * len(self.masks) + j
                remote.append(pltpu.make_async_remote_copy(
                    src_ref=src, dst_ref=dst, send_sem=send_sems.at[sem_index], recv_sem=recv_sems.at[sem_index],
                    device_id=peer, device_id_type=pl.DeviceIdType.MESH))
        return local, remote

    def start(self, ins, outs, sems):
        local, remote = self._copies(ins, outs, sems, landing=False)
        for cp in local + remote:
            cp.start()

    def finish(self, ins, outs, sems):
        local, remote = self._copies(ins, outs, sems, landing=True)
        for cp in remote:
            cp.wait_recv()
        for cp in remote:
            cp.wait_send()
        for cp in local:
            cp.wait()


def exchange_start(name, xs, kind, dep=None, lands=None):
    ex = Exchange(xs, kind, lands)
    hbm = pl.BlockSpec(memory_space=pltpu.HBM)
    sem = pl.BlockSpec(memory_space=pltpu.SEMAPHORE)
    if lands is None:
        lands = [lax.empty(s.shape, s.dtype) for s in ex.land_shape]
    n_src, n = len(ex.xs), ex.n
    n_inputs = n_src + n + (0 if dep is None else 1)

    def body(*refs):
        ins, lnd, sems, token = refs[:n_src], refs[n_src:n_src + n], refs[n_inputs:n_inputs + 3], refs[-1]
        ex.start(ins, lnd, sems)
        token[...] = jnp.zeros_like(token)

    res = pl.pallas_call(
        body, name=name, in_specs=[hbm] * (n_src + n) + ([] if dep is None else [pl.BlockSpec(memory_space=pl.ANY)]),
        out_specs=[sem] * 3 + [hbm] * (n_src + n) + [pl.BlockSpec(memory_space=pltpu.VMEM)],
        out_shape=ex.sems + [pltpu.HBM(x.shape, x.dtype) for x in ex.xs]
        + [pltpu.HBM(s.shape, s.dtype) for s in ex.land_shape] + [jax.ShapeDtypeStruct((SUBLANES, LANES), F32)],
        input_output_aliases={i: 3 + i for i in range(n_src + n)},
        compiler_params=pltpu.CompilerParams(has_side_effects=pltpu.SideEffectType.DATAFLOW_SIDE_EFFECTING),
    )(*[pltpu.with_memory_space_constraint(x, pltpu.HBM) for x in ex.xs + list(lands)],
      *([] if dep is None else [dep]))
    return (ex, res[:3], res[3:3 + n_src], res[3 + n_src:3 + n_src + n]), res[-1]


def exchange_wait(name, handles, after):
    ex, sems, srcs, lands = handles
    n_src, n = len(srcs), len(lands)
    hbm = pl.BlockSpec(memory_space=pltpu.HBM)
    sem = pl.BlockSpec(memory_space=pltpu.SEMAPHORE)

    def body(*refs):
        ins, lnd, sem_refs = refs[:n_src], refs[n_src:n_src + n], refs[n_src + n:n_src + n + 3]
        ex.finish(ins, lnd, sem_refs)

    res = pl.pallas_call(
        body, name=name, in_specs=[hbm] * (n_src + n) + [sem] * 3 + [pl.BlockSpec(memory_space=pl.ANY)],
        out_specs=[hbm] * (n_src + n),
        out_shape=[pltpu.HBM(x.shape, x.dtype) for x in srcs] + [pltpu.HBM(x.shape, x.dtype) for x in lands],
        input_output_aliases={i: i for i in range(n_src + n)},
        compiler_params=pltpu.CompilerParams(has_side_effects=pltpu.SideEffectType.DATAFLOW_SIDE_EFFECTING),
    )(*srcs, *lands, *sems, after)
    return res[n_src:]


def forward_start(name, chip_gather, after):
    lands = exchange_wait(name + "_wait", chip_gather, after)
    return exchange_start(name + "_forward_start", [], "forward", lands=lands)


_Z = (0, 1024)
_XBC = (1024, 2560)
_DT = (2560, 2576)
_RKV = (2576, 5648)
_PW = (5648, 5744)
_PA = (5744, 5840)
_PG = (5840, 6096)
D_IN = 6096

_SMALL = ("norm_mix_g", "ssd_conv_b", "ssd_dt_bias", "ssd_a_log", "ssd_d", "ssd_norm_g", "rwkv_mu", "rwkv_w0",
          "rwkv_a0", "rwkv_k_k", "rwkv_k_a", "rwkv_r_k", "rwkv_ln_w", "rwkv_ln_b", "norm_x_g", "norm_mem_g",
          "norm_ffn_g", "final_norm_g")
_WEIGHTS = ("norm_mix_g", "w_in", "ssd_conv_w", "ssd_conv_b", "ssd_dt_bias", "ssd_a_log", "ssd_d", "ssd_norm_g",
            "rwkv_mu", "rwkv_w0", "rwkv_w2", "rwkv_a0", "rwkv_a2", "rwkv_g2", "rwkv_k_k", "rwkv_k_a", "rwkv_r_k",
            "rwkv_ln_w", "rwkv_ln_b", "w_out", "norm_x_g", "norm_mem_g", "xattn_wq", "xattn_wk", "xattn_wv",
            "xattn_wo", "norm_ffn_g", "ffn_w1", "ffn_w2", "final_norm_g")


def _pad_lanes(x, width=LANES):
    return jnp.pad(x, ((0, 0), (0, width - x.shape[1])))


def _pack_small(vals):
    flat = jnp.concatenate([vals[n].reshape(-1) for n in _SMALL])
    rows = -(-flat.shape[0] // (LANES * SUBLANES)) * SUBLANES
    return jnp.pad(flat, (0, rows * LANES - flat.shape[0])).reshape(rows, LANES)


def _unpack_small(packed, shapes):
    flat = packed.reshape(-1)
    out, pos = {}, 0
    for n in _SMALL:
        size = 1
        for s in shapes[n]:
            size *= s
        out[n] = flat[pos:pos + size].reshape(shapes[n])
        pos += size
    return out


def _rows(w, rng):
    return w[rng[0]:rng[1]]


def sum_slabs(name, recv):
    n, rows, cols = recv.shape
    tc = _pick(cols, (256, 128))

    def body(r_ref, o_ref):
        acc = r_ref[0].astype(F32)
        for p in range(1, n):
            acc = acc + r_ref[p].astype(F32)
        o_ref[...] = acc

    return pl.pallas_call(
        body, name=name, grid=(cols // tc,), in_specs=[pl.BlockSpec((n, rows, tc), lambda j: (0, 0, j))],
        out_specs=pl.BlockSpec((rows, tc), lambda j: (0, j)), out_shape=jax.ShapeDtypeStruct((rows, cols), F32),
        compiler_params=_params(),
    )(recv)


def kernel(x, mem, norm_mix_g, w_in, ssd_conv_w, ssd_conv_b, ssd_dt_bias, ssd_a_log, ssd_d, ssd_norm_g, rwkv_mu, rwkv_w0, rwkv_w2, rwkv_a0, rwkv_a2, rwkv_g2, rwkv_k_k, rwkv_k_a, rwkv_r_k, rwkv_ln_w, rwkv_ln_b, w_out, norm_x_g, norm_mem_g, xattn_wq, xattn_wk, xattn_wv, xattn_wo, norm_ffn_g, ffn_w1, ffn_w2, final_norm_g, loss_target, m_norm_mix_g, m_w_in, m_ssd_conv_w, m_ssd_conv_b, m_ssd_dt_bias, m_ssd_a_log, m_ssd_d, m_ssd_norm_g, m_rwkv_mu, m_rwkv_w0, m_rwkv_w2, m_rwkv_a0, m_rwkv_a2, m_rwkv_g2, m_rwkv_k_k, m_rwkv_k_a, m_rwkv_r_k, m_rwkv_ln_w, m_rwkv_ln_b, m_w_out, m_norm_x_g, m_norm_mem_g, m_xattn_wq, m_xattn_wk, m_xattn_wv, m_xattn_wo, m_norm_ffn_g, m_ffn_w1, m_ffn_w2, m_final_norm_g, v_norm_mix_g, v_w_in, v_ssd_conv_w, v_ssd_conv_b, v_ssd_dt_bias, v_ssd_a_log, v_ssd_d, v_ssd_norm_g, v_rwkv_mu, v_rwkv_w0, v_rwkv_w2, v_rwkv_a0, v_rwkv_a2, v_rwkv_g2, v_rwkv_k_k, v_rwkv_k_a, v_rwkv_r_k, v_rwkv_ln_w, v_rwkv_ln_b, v_w_out, v_norm_x_g, v_norm_mem_g, v_xattn_wq, v_xattn_wk, v_xattn_wv, v_xattn_wo, v_norm_ffn_g, v_ffn_w1, v_ffn_w2, v_final_norm_g):
    given = dict(locals())
    wts = {n: given[n] for n in _WEIGHTS}
    mom_m = {n: given["m_" + n] for n in _WEIGHTS}
    mom_v = {n: given["v_" + n] for n in _WEIGHTS}
    d = D_MODEL
    xt, memt, tgt = x[0], mem[0], loss_target[0]
    tm = 256
    tm_light = 512

    big = {"w_in": jnp.transpose(w_in[0]), "w_out": w_out[0], "xattn_wq": xattn_wq[0], "xattn_wk": xattn_wk[0],
           "xattn_wv": xattn_wv[0], "xattn_wo": xattn_wo[0], "ffn_w1": ffn_w1[0], "ffn_w2": ffn_w2[0]}
    small_sh = {"ssd_conv_w": ssd_conv_w.reshape(4, -1), "rwkv_w2": rwkv_w2[0], "rwkv_a2": rwkv_a2[0],
                "rwkv_g2": rwkv_g2[0]}
    cast_one = lambda n, deps=(): rowwise_fwd("cast_" + n, cast_fn, [big[n]], [], [(big[n].shape[1], BF16)],
                                              256 if big[n].shape[0] % 256 == 0 else big[n].shape[0], deps=deps)[0]
    gather_in, token_in = exchange_start("gather_in_start", [cast_one("w_in")] + list(small_sh.values()), "gather_chips")
    cast = {n: cast_one(n, deps=[token_in]) for n in big if n != "w_in"}
    late_a = ("w_out", "xattn_wq", "xattn_wk", "xattn_wv", "xattn_wo")
    late_b = ("ffn_w1", "ffn_w2")
    gather_a, token_a = exchange_start("gather_attn_start", [cast[n] for n in late_a], "gather_chips", dep=token_in)
    gather_b, token_b = exchange_start("gather_ffn_start", [cast[n] for n in late_b], "gather_chips", dep=token_a)
    (h1,) = rowwise_fwd("norm_mix", rmsnorm_fn, [xt], [norm_mix_g], [(d, BF16)], tm_light, deps=[token_b])
    forward_in, token_in = forward_start("gather_in", gather_in, after=h1)
    gathered = exchange_wait("gather_in_forward_wait", forward_in, after=token_in)
    g_big = {"w_in": gathered[0]}
    g_small = dict(zip(small_sh, gathered[1:]))

    pad_rows = lambda a: jnp.pad(a, ((0, LANES - a.shape[0]), (0, 0)))
    w_in_t = g_big["w_in"].reshape(D_IN, d)
    wt_z, wt_xbc, wt_rkv = (_rows(w_in_t, r) for r in (_Z, _XBC, _RKV))
    wt_ps = jnp.concatenate([_rows(w_in_t, _PG)] + [pad_rows(_rows(w_in_t, r)) for r in (_PW, _PA, _DT)], axis=0)
    unshard_cols = lambda g: jnp.transpose(g, (1, 0, 2)).reshape(g.shape[1], -1)
    conv_w_f = unshard_cols(g_small["ssd_conv_w"])
    w2p, a2p = pad_rows(unshard_cols(g_small["rwkv_w2"])), pad_rows(unshard_cols(g_small["rwkv_a2"]))
    g2_f = unshard_cols(g_small["rwkv_g2"])

    mu = rwkv_mu
    mu_rkv, mu_pg = mu[:, :3072], mu[:, 3264:3520]
    mu_pwa = jnp.concatenate([_pad_lanes(mu[:, 3072:3168]), _pad_lanes(mu[:, 3168:3264])], axis=1)
    dt_bias_p, a_log_p, d_p = _pad_lanes(ssd_dt_bias), _pad_lanes(ssd_a_log), _pad_lanes(ssd_d)
    r_k_row = rwkv_r_k.reshape(1, RWKV_WIDTH)
    g_final = final_norm_g.reshape(1, d)

    u_z = mm("in_z", h1, wt_z, "nt")
    u_xbc = mm("in_xbc", h1, wt_xbc, "nt")
    u_rkv = mm("in_rkv", h1, wt_rkv, "nt")
    u_ps = mm("in_narrow", h1, wt_ps, "nt")

    ssd_pre_rows = lambda: [Rows(u_xbc, shifts=(1, 2, 3)), Rows(u_ps, LANES, 4)]
    ssd_pre_params = [conv_w_f, ssd_conv_b, dt_bias_p]
    xs, bm, cm, dt = rowwise_fwd("ssd_pre", ssd_pre_fn, ssd_pre_rows(), ssd_pre_params,
                                 [(SSD_WIDTH, F32), (256, F32), (256, F32), (LANES, F32)], tm)
    y_scan, ssd_ck = ssd_scan_fwd(xs, bm, cm, dt, a_log_p, d_p)
    (y_ssd,) = rowwise_fwd("ssd_post", ssd_post_fn, [y_scan, u_z], [ssd_norm_g], [(SSD_WIDTH, BF16)], tm_light,
                           into=(None, d, 0))

    rwkv_pre_rows = lambda: [Rows(u_rkv, shifts=(1,)), Rows(u_ps, 2 * LANES, 0, shifts=(1,)), Rows(u_ps, 2 * LANES, 1, shifts=(1,))]
    rwkv_pre_params = [mu_rkv, mu_pg, mu_pwa, rwkv_w0, w2p, rwkv_a0, a2p, g2_f, rwkv_k_k, rwkv_k_a]
    forward_a, token_a = forward_start("gather_attn", gather_a, after=y_scan)
    r_, lw_, k_, v_, kap_, b_, gate_ = rowwise_fwd("rwkv_pre", rwkv_pre_fn, rwkv_pre_rows(), rwkv_pre_params,
                                                   [(RWKV_WIDTH, F32)] * 7, tm, deps=[token_a])
    ys_r, rwkv_ck, rwkv_inv = rwkv_scan_fwd(r_, lw_, k_, v_, kap_, b_)
    forward_b, token_b = forward_start("gather_ffn", gather_b, after=ys_r)
    g_big.update(zip(late_a, exchange_wait("gather_attn_forward_wait", forward_a, after=token_b)))
    w_out_f = g_big["w_out"].reshape(d, d)
    wq_f, wk_f, wv_f, wo_f = (g_big[n].reshape(d, d) for n in ("xattn_wq", "xattn_wk", "xattn_wv", "xattn_wo"))
    rwkv_post_params = [rwkv_ln_w, rwkv_ln_b, r_k_row]
    (ycat,) = rowwise_fwd("rwkv_post", rwkv_post_fn, [ys_r, r_, k_, v_, gate_], rwkv_post_params,
                          [(RWKV_WIDTH, BF16)], tm, into=(y_ssd, d, 1))
    x1 = mm("out_proj", ycat, w_out_f, "nn", res=xt)

    (h2,) = rowwise_fwd("norm_x", rmsnorm_fn, [x1], [norm_x_g], [(d, BF16)], tm_light)
    (mn,) = rowwise_fwd("norm_mem", rmsnorm_fn, [memt], [norm_mem_g], [(d, BF16)], tm)
    q = mm("xattn_q", h2, wq_f, "nn", out_dtype=BF16)
    kx = mm("xattn_k", mn, wk_f, "nn")
    vx = mm("xattn_v", mn, wv_f, "nn")
    (o,) = rowwise_fwd("xattn", attn_fn, [q], [kx, vx], [(d, BF16)], tm_light)
    x2 = mm("xattn_o", o, wo_f, "nn", res=x1)

    (h3,) = rowwise_fwd("norm_ffn", rmsnorm_fn, [x2], [norm_ffn_g], [(d, BF16)], tm_light)
    w1_s, w2_g = exchange_wait("gather_ffn_forward_wait", forward_b, after=h3)
    w2_f = w2_g.reshape(D_FF, d)
    relu2_epi = lambda acc: (jnp.square(jnp.maximum(acc, 0.0)), jnp.maximum(acc, 0.0))
    hid, relu_a = mm("ffn_1", h3, w1_s, "nn", b_slabs=N_DEV, epi=relu2_epi, out_dtypes=[BF16, BF16])
    x3 = mm("ffn_2", hid, w2_f, "nn", res=x2)

    loss_blk, dx3, dx3_b, dg_final = loss_and_grad(x3, tgt, g_final, tm_light)

    grads = {}
    grads["ffn_w2"] = mm("d_ffn_w2", hid, dx3_b, "tn", out_dtype=BF16).reshape(N_DEV, D_FF // N_DEV, d)
    sc_w2, tok = exchange_start("scatter_ffn_w2_start", [grads["ffn_w2"]], "scatter")
    da = mm("d_hid", dx3_b, w2_f, "nt", dep=tok, epi=lambda acc, ra: (2.0 * acc * ra,), extras=[relu_a],
            out_dtypes=[BF16])
    grads["ffn_w1"] = mm("d_ffn_w1", h3, da, "tn", out_dtype=BF16, out_slabs=N_DEV)
    sc_w1, tok = exchange_start("scatter_ffn_w1_start", [grads["ffn_w1"]], "scatter")
    dh3 = mm("d_h3", da, w1_s, "nt", out_dtype=BF16, b_slabs=N_DEV, dep=tok)
    (dx2,), (dg_ffn,) = rowwise_bwd("norm_ffn_bwd", rmsnorm_fn, [x2], [norm_ffn_g], [[dh3]], tm_light, [F32], row_add=[dx3])

    grads["xattn_wo"] = mm("d_wo", o, dx2, "tn", out_dtype=BF16).reshape(N_DEV, d // N_DEV, d)
    sc_wo, tok = exchange_start("scatter_wo_start", [grads["xattn_wo"]], "scatter")
    d_o = mm("d_o", dx2, wo_f, "nt", out_dtype=BF16, dep=tok)
    (dq,), (dkx, dvx) = rowwise_bwd("xattn_bwd", attn_fn, [q], [kx, vx], [[d_o]], tm_light, [BF16])
    grads["xattn_wq"] = mm("d_wq", h2, dq, "tn", out_dtype=BF16).reshape(N_DEV, d // N_DEV, d)
    grads["xattn_wk"] = mm("d_wk", mn, dkx, "tn", out_dtype=BF16).reshape(N_DEV, d // N_DEV, d)
    grads["xattn_wv"] = mm("d_wv", mn, dvx, "tn", out_dtype=BF16).reshape(N_DEV, d // N_DEV, d)
    qkv = ("xattn_wq", "xattn_wk", "xattn_wv")
    sc_qkv, tok = exchange_start("scatter_qkv_start", [grads[n] for n in qkv], "scatter")
    dmn = mm("d_mn_v", dvx, wv_f, "nt", res=mm("d_mn_k", dkx, wk_f, "nt", dep=tok))
    _, (dg_mem,) = rowwise_bwd("norm_mem_bwd", rmsnorm_fn, [memt], [norm_mem_g], [[dmn]], tm, [None])
    dh2 = mm("d_h2", dq, wq_f, "nt", out_dtype=BF16, dep=dg_mem)
    (dx1,), (dg_x,) = rowwise_bwd("norm_x_bwd", rmsnorm_fn, [x1], [norm_x_g], [[dh2]], tm_light, [F32], row_add=[dx2])

    grads["w_out"] = mm("d_w_out", ycat, dx1, "tn", out_dtype=BF16).reshape(N_DEV, d // N_DEV, d)
    sc_wout, tok = exchange_start("scatter_w_out_start", [grads["w_out"]], "scatter")
    d_ycat = mm("d_ycat", dx1, w_out_f, "nt", out_dtype=BF16, dep=tok)

    (d_ys, d_r1, d_k1, d_v1, d_gate), (dln_w, dln_b, dr_k) = rowwise_bwd(
        "rwkv_post_bwd", rwkv_post_fn, [ys_r, r_, k_, v_, gate_], rwkv_post_params,
        [[Rows(d_ycat, RWKV_WIDTH, 1)]], tm, [BF16] * 5)
    d_r2, d_lw, d_k2, d_v2, d_kap, d_b = rwkv_scan_bwd(r_, lw_, k_, v_, kap_, b_, rwkv_ck, rwkv_inv, d_ys)
    (du_rkv, du_pg, du_pwa), rwkv_pg = rowwise_bwd(
        "rwkv_pre_bwd", rwkv_pre_fn, rwkv_pre_rows(), rwkv_pre_params,
        [[d_r1, d_r2], [d_lw], [d_k1, d_k2], [d_v1, d_v2], [d_kap], [d_b], [d_gate]], tm, [BF16] * 3)
    dmu_rkv, dmu_pg, dmu_pwa, dw0, dw2p, da0, da2p, dg2, dk_k, dk_a = rwkv_pg

    (d_yscan, du_z), (dssd_norm_g,) = rowwise_bwd("ssd_post_bwd", ssd_post_fn, [y_scan, u_z], [ssd_norm_g],
                                                  [[Rows(d_ycat, SSD_WIDTH, 0)]], tm_light, [BF16, BF16])
    dxs, dbm, dcm, ddt2, da_log_p, dd_p = ssd_scan_bwd(xs, bm, cm, dt, a_log_p, d_p, ssd_ck, d_yscan)
    (du_xbc, du_dt), (dconv_w, dconv_b, ddt_bias_p) = rowwise_bwd(
        "ssd_pre_bwd", ssd_pre_fn, ssd_pre_rows(), ssd_pre_params,
        [[dxs], [dbm], [dcm], [ddt2[0], ddt2[1]]], tm, [BF16, BF16])
    du_ps = jnp.concatenate([du_pg, du_pwa, du_dt], axis=1)

    dwt_z = mm("d_w_z", du_z, h1, "tn", out_dtype=BF16)
    dwt_xbc = mm("d_w_xbc", du_xbc, h1, "tn", out_dtype=BF16)
    dwt_rkv = mm("d_w_rkv", du_rkv, h1, "tn", out_dtype=BF16)
    dwt_ps = mm("d_w_narrow", du_ps, h1, "tn", out_dtype=BF16)
    dwt_full = jnp.concatenate([dwt_z, dwt_xbc, dwt_ps[512:528], dwt_rkv, dwt_ps[256:352], dwt_ps[384:480], dwt_ps[0:256]],
                               axis=0)
    to_slabs = lambda g: jnp.transpose(g.reshape(g.shape[0], N_DEV, -1), (1, 0, 2))
    grads["w_in"] = dwt_full.reshape(N_DEV, D_IN // N_DEV, d)
    grads["ssd_conv_w"] = to_slabs(dconv_w)
    grads["rwkv_w2"] = to_slabs(dw2p[:96])
    grads["rwkv_a2"] = to_slabs(da2p[:96])
    grads["rwkv_g2"] = to_slabs(dg2)
    tail = ("w_in", "ssd_conv_w", "rwkv_w2", "rwkv_a2", "rwkv_g2")
    sc_tail, tok = exchange_start("scatter_tail_start", [grads[n] for n in tail], "scatter")
    dh1 = mm("d_h1_z", du_z, wt_z, "nn", dep=tok)
    dh1 = mm("d_h1_xbc", du_xbc, wt_xbc, "nn", res=dh1)
    dh1 = mm("d_h1_rkv", du_rkv, wt_rkv, "nn", res=dh1)
    dh1 = mm("d_h1_narrow", du_ps, wt_ps, "nn", res=dh1)
    (dx,), (dg_mix,) = rowwise_bwd("norm_mix_bwd", rmsnorm_fn, [xt], [norm_mix_g], [[dh1]], tm_light, [F32], row_add=[dx1])

    dmu =jnp.concatenate([dmu_rkv, dmu_pwa[:, 0:96], dmu_pwa[:, 128:224], dmu_pg], axis=1)
    small_grads = {
        "norm_mix_g": dg_mix, "ssd_conv_b": dconv_b, "ssd_dt_bias": ddt_bias_p[:, :16], "ssd_a_log": da_log_p[:, :16],
        "ssd_d": dd_p[:, :16], "ssd_norm_g": dssd_norm_g, "rwkv_mu": dmu, "rwkv_w0": dw0, "rwkv_a0": da0,
        "rwkv_k_k": dk_k, "rwkv_k_a": dk_a, "rwkv_r_k": dr_k, "rwkv_ln_w": dln_w, "rwkv_ln_b": dln_b,
        "norm_x_g": dg_x, "norm_mem_g": dg_mem, "norm_ffn_g": dg_ffn, "final_norm_g": dg_final}

    gather_small, tok = exchange_start("gather_small_start", [_pack_small(small_grads)], "gather")
    received = {}
    for names, handle in ((("ffn_w2",), sc_w2), (("ffn_w1",), sc_w1), (("xattn_wo",), sc_wo), (qkv, sc_qkv),
                          (("w_out",), sc_wout)):
        received.update(zip(names, exchange_wait("scatter_" + names[0] + "_wait", handle, after=tok)))

    out_g, out_d, out_m, out_v = {}, {}, {}, {}

    def run_adamw(n, dep):
        shape = wts[n].shape
        two_d = lambda a: a.reshape(-1, shape[-1])
        if n == "w_in":
            recv = jnp.transpose(sum_slabs("sum_w_in", received[n]))[None]
        else:
            recv = received[n].reshape(N_DEV, -1, shape[-1])
        res = adamw("adamw_" + n, recv, two_d(wts[n]), two_d(mom_m[n]), two_d(mom_v[n]), dep=dep)
        out_g[n], out_d[n], out_m[n], out_v[n] = (r.reshape(shape) for r in res)
        return res[0]

    last = None
    for n in ("ffn_w2", "ffn_w1", "xattn_wo") + qkv + ("w_out",):
        last = run_adamw(n, last)
    received.update(zip(tail, exchange_wait("scatter_tail_wait", sc_tail, after=last)))
    for n in tail:
        last = run_adamw(n, last)
    (small_all,) = exchange_wait("gather_small_wait", gather_small, after=last)
    res = adamw("adamw_small", small_all, _pack_small(wts), _pack_small(mom_m), _pack_small(mom_v))
    shapes = {n: wts[n].shape for n in _SMALL}
    for dst, packed in zip((out_g, out_d, out_m, out_v), res):
        dst.update(_unpack_small(packed, shapes))

    loss = lax.psum(loss_blk[0, 0], ("x", "y", "c"))
    return (loss, dx[None], *[out_g[n] for n in _WEIGHTS], *[out_d[n] for n in _WEIGHTS],
            *[out_m[n] for n in _WEIGHTS], *[out_v[n] for n in _WEIGHTS])
```

```python
import functools

import jax
import jax.numpy as jnp
from jax import lax
from jax.experimental import pallas as pl
from jax.experimental.pallas import tpu as pltpu

F32 = jnp.float32
BF16 = jnp.bfloat16

N_DEV = 8
D_MODEL = 2048
NORM_EPS = 1e-6
SSD_WIDTH = 1024
SSD_HEAD_DIM = 64
SSD_STATE = 128
SSD_CHUNK = 128
SSD_HEADS_PER_GROUP = 8
RWKV_WIDTH = 1024
RWKV_HEADS = 16
RWKV_HEAD_DIM = 64
RWKV_LN_EPS = 64e-5
RWKV_CHUNK = 128
RWKV_HEADS_PER_STEP = 16
XATTN_HEADS = 4
XATTN_HEAD_DIM = 512
D_FF = 8192
LANES = 128
SUBLANES = 8
VMEM_LIMIT = 56 * 1024 * 1024

ADAM_LR = 0.001
ADAM_B1 = 0.9
ADAM_B2 = 0.999
ADAM_EPS = 1e-08
ADAM_WD = 0.01
ADAM_STEP = 10

_DN = {"nn": ((1,), (0,)), "nt": ((1,), (1,)), "tn": ((0,), (0,))}


def _dg(a, b, mode):
    (ca,), (cb,) = _DN[mode]
    dn = (((ca + 1,), (cb + 1,)), ((0,), (0,))) if a.ndim == 3 else (((ca,), (cb,)), ((), ()))
    return lax.dot_general(a, b, dn, preferred_element_type=F32)


@functools.partial(jax.custom_vjp, nondiff_argnums=(2,))
def bdot(a, b, mode):
    return _dg(a.astype(BF16), b.astype(BF16), mode)


def _bdot_fwd(a, b, mode):
    return bdot(a, b, mode), (a, b)


def _bdot_bwd(mode, res, g):
    a, b = res
    ab, bb, gb = a.astype(BF16), b.astype(BF16), g.astype(BF16)
    if mode == "nn":
        da, db = _dg(gb, bb, "nt"), _dg(ab, gb, "tn")
    elif mode == "nt":
        da, db = _dg(gb, bb, "nn"), _dg(gb, ab, "tn")
    else:
        da, db = _dg(bb, gb, "nt"), _dg(ab, gb, "nn")
    return da.astype(a.dtype), db.astype(b.dtype)


bdot.defvjp(_bdot_fwd, _bdot_bwd)


def _split2(x):
    hi = x.astype(BF16)
    return hi, (x - hi.astype(F32)).astype(BF16)


def _dot01(x, m01):
    hi, lo = _split2(x)
    return _dg(hi, m01, "nn") + _dg(lo, m01, "nn")


def _exact_dot_impl(a, b, mode, exact):
    if exact == "a":
        ae = a.astype(BF16)
        return sum(_dg(ae, part, mode) for part in _split2(b))
    be = b.astype(BF16)
    return sum(_dg(part, be, mode) for part in _split2(a))


@functools.partial(jax.custom_vjp, nondiff_argnums=(2, 3))
def exact_dot(a, b, mode, exact):
    return _exact_dot_impl(a, b, mode, exact)


def _exact_dot_fwd(a, b, mode, exact):
    return _exact_dot_impl(a, b, mode, exact), (a, b)


def _exact_dot_bwd(mode, exact, res, g):
    a, b = res
    if exact == "a":
        db = {"nn": lambda: _exact_dot_impl(a, g, "tn", "a"), "nt": lambda: _exact_dot_impl(g, a, "tn", "b"),
              "tn": lambda: _exact_dot_impl(a, g, "nn", "a")}[mode]()
        return jnp.zeros_like(a), db
    da = {"nn": lambda: _exact_dot_impl(g, b, "nt", "b"), "nt": lambda: _exact_dot_impl(g, b, "nn", "b"),
          "tn": lambda: _exact_dot_impl(b, g, "nt", "a")}[mode]()
    return da, jnp.zeros_like(b)


exact_dot.defvjp(_exact_dot_fwd, _exact_dot_bwd)


def _head_indicator(width, heads, transpose):
    hd = width // heads
    shape = (LANES, width) if transpose else (width, LANES)
    lane = lax.broadcasted_iota(jnp.int32, shape, 1 if not transpose else 0)
    pos = lax.broadcasted_iota(jnp.int32, shape, 0 if not transpose else 1)
    return ((pos >= lane * hd) & (pos < lane * hd + hd)).astype(BF16)


@jax.custom_vjp
def head_sum(x):
    w = x.shape[-1]
    e = _head_indicator(w, w // RWKV_HEAD_DIM, False)
    et = _head_indicator(w, w // RWKV_HEAD_DIM, True)
    return _dot01(_dot01(x, e), et)


head_sum.defvjp(lambda x: (head_sum(x), None), lambda _, g: (head_sum(g),))


def rmsnorm_fn(x, g):
    y = x * lax.rsqrt(jnp.mean(x * x, axis=-1, keepdims=True) + NORM_EPS)
    return ((y * g).astype(BF16),)


def cast_fn(x):
    return (x.astype(BF16),)


def ssd_pre_fn(xbc, xbc1, xbc2, xbc3, dt_raw, conv_w, conv_b, dt_bias):
    c = conv_w[3:4] * xbc + conv_w[2:3] * xbc1 + conv_w[1:2] * xbc2 + conv_w[0:1] * xbc3 + conv_b
    act = c * jax.nn.sigmoid(c)
    dt = jax.nn.softplus(dt_raw + dt_bias)
    return act[:, :SSD_WIDTH], act[:, SSD_WIDTH:SSD_WIDTH + 256], act[:, SSD_WIDTH + 256:], dt


def ssd_post_fn(yscan, z, norm_g):
    y = yscan * (z * jax.nn.sigmoid(z))
    half = SSD_WIDTH // 2
    parts = []
    for g in range(2):
        yg = y[:, g * half:(g + 1) * half]
        parts.append(yg * lax.rsqrt(jnp.mean(yg * yg, axis=-1, keepdims=True) + NORM_EPS))
    return ((jnp.concatenate(parts, axis=-1) * norm_g).astype(BF16),)


def rwkv_pre_fn(rkv, rkv_p, pg, pg_p, pwa, pwa_p, mu_rkv, mu_pg, mu_pwa, w0, w2p, a0, a2p, g2, k_k, k_a):
    w = RWKV_WIDTH
    rkv = rkv + (rkv_p - rkv) * mu_rkv
    pg = pg + (pg_p - pg) * mu_pg
    pwa = pwa + (pwa_p - pwa) * mu_pwa
    r, k, v = rkv[:, :w], rkv[:, w:2 * w], rkv[:, 2 * w:]
    pw, pa = pwa[:, :LANES], pwa[:, LANES:]
    w_log = -jax.nn.softplus(-(w0 + bdot(jnp.tanh(pw), w2p, "nn"))) - 0.5
    lw = -jnp.exp(w_log)
    iclr = jax.nn.sigmoid(a0 + bdot(pa, a2p, "nn"))
    gate = bdot(jax.nn.sigmoid(pg), g2, "nn")
    kk = k * k_k
    kap = kk * lax.rsqrt(jnp.maximum(head_sum(kk * kk), 1e-24))
    k_mod = k * (1.0 + (iclr - 1.0) * k_a)
    return r, lw, k_mod, v, kap, kap * iclr, gate


def rwkv_post_fn(ys, r, k_mod, v, gate, ln_w, ln_b, r_k):
    inv_n = 1.0 / RWKV_HEAD_DIM
    mean = head_sum(ys) * inv_n
    yc = ys - mean
    var = head_sum(yc * yc) * inv_n
    yn = yc * lax.rsqrt(var + RWKV_LN_EPS) * ln_w + ln_b
    bonus = head_sum(r * k_mod * r_k) * v
    return (((yn + bonus) * gate).astype(BF16),)


def attn_fn(q, kx, vx):
    outs = []
    for h in range(XATTN_HEADS):
        sl = slice(h * XATTN_HEAD_DIM, (h + 1) * XATTN_HEAD_DIM)
        s = bdot(q[:, sl], kx[:, sl], "nt") * (XATTN_HEAD_DIM ** -0.5)
        s = s - jnp.max(s, axis=-1, keepdims=True)
        p = jnp.exp(s)
        p = p / jnp.sum(p, axis=-1, keepdims=True)
        outs.append(bdot(p, vx[:, sl], "nn"))
    return (jnp.concatenate(outs, axis=-1).astype(BF16),)


def loss_fn(x, tgt, g):
    y = x * lax.rsqrt(jnp.mean(x * x, axis=-1, keepdims=True) + NORM_EPS) * g
    err = jnp.square(y - tgt)
    return 0.5 * jnp.sum(jnp.mean(err, axis=-1, keepdims=True), axis=0, keepdims=True)


def _tri_masks(n):
    row = lax.broadcasted_iota(jnp.int32, (n, n), 0)
    col = lax.broadcasted_iota(jnp.int32, (n, n), 1)
    return col <= row, col < row, row == col


@jax.custom_vjp
def unit_lower_inverse(a):
    c = a.shape[-1]
    eye = _tri_masks(c)[2].astype(F32)
    m = -a
    inv = eye + m
    n = 1
    while n * 2 < c:
        m = bdot(m, m, "nn")
        inv = bdot(inv, eye + m, "nn")
        n *= 2
    return inv


def _unit_lower_inverse_fwd(a):
    inv = unit_lower_inverse(a)
    return inv, inv


def _unit_lower_inverse_bwd(inv, g):
    return (-bdot(bdot(inv, g, "tn"), inv, "nt"),)


unit_lower_inverse.defvjp(_unit_lower_inverse_fwd, _unit_lower_inverse_bwd)


@jax.custom_vjp
def known_inverse(a, inv):
    return inv


known_inverse.defvjp(lambda a, inv: (inv, inv),
                     lambda inv, g: (_unit_lower_inverse_bwd(inv, g)[0], jnp.zeros_like(inv)))


def rwkv_chunk_fn(st0, r, lw, k, v, kap, b, inv=None):
    h, c = r.shape[0], r.shape[1]
    incl, strict, _ = _tri_masks(c)
    cum = exact_dot(jnp.broadcast_to(incl.astype(F32), (h, c, c)), lw, "nn", "a")
    g_in = jnp.exp(cum)
    g_prev = jnp.exp(cum - lw)
    g_inv = jnp.exp(-cum)
    g_end = jnp.exp(cum[:, c - 1:c, :] - cum)
    kap_t, k_t, b_t, r_t = kap * g_prev, k * g_inv, b * g_inv, r * g_in
    a_ub = jnp.where(strict, bdot(kap_t, b_t, "nt"), 0.0)
    a_vk = jnp.where(strict, bdot(kap_t, k_t, "nt"), 0.0)
    rhs = -(bdot(kap_t, st0, "nt") + bdot(a_vk, v, "nn"))
    inv = unit_lower_inverse(a_ub) if inv is None else known_inverse(a_ub, inv)
    u = bdot(inv, rhs, "nn")
    y = (bdot(r_t, st0, "nt")
         + bdot(jnp.where(incl, bdot(r_t, k_t, "nt"), 0.0), v, "nn")
         + bdot(jnp.where(incl, bdot(r_t, b_t, "nt"), 0.0), u, "nn"))
    st1 = jnp.exp(cum[:, c - 1:c, :]) * st0 + bdot(v, k * g_end, "tn") + bdot(u, b * g_end, "tn")
    return y, st1, inv


def ssd_chunk_fn(group, h0, xs, bm, cm, dt, a_log, d_skip):
    q, nh = xs.shape[0], SSD_HEADS_PER_GROUP
    causal, _, _ = _tri_masks(q)
    a_row = -jnp.exp(a_log)
    cs_all = exact_dot(causal.astype(F32), dt * a_row, "nn", "a")
    cs_t = cs_all.T
    lanes = range(group * nh, (group + 1) * nh)
    cs = jnp.stack([cs_all[:, hl:hl + 1] for hl in lanes])
    cs_row = jnp.stack([cs_t[hl:hl + 1, :] for hl in lanes])
    dt_h = jnp.stack([dt[:, hl:hl + 1] for hl in lanes])
    d_h = jnp.stack([d_skip[:, hl:hl + 1] for hl in lanes])
    x = _stack_lanes(xs, nh)
    h0s = _stack_rows(h0, nh)
    lmat = jnp.where(causal, jnp.exp(jnp.where(causal, cs - cs_row, 0.0)), 0.0)
    cb = bdot(cm, bm, "nt")
    xdt = x * dt_h
    cl = cs[:, q - 1:q, :]
    cm_b = jnp.broadcast_to(cm, (nh,) + cm.shape)
    bm_b = jnp.broadcast_to(bm, (nh,) + bm.shape)
    y = bdot(cb * lmat, xdt, "nn") + bdot(cm_b, h0s, "nt") * jnp.exp(cs) + x * d_h
    h1 = h0s * jnp.exp(cl) + bdot(xdt * jnp.exp(cl - cs), bm_b, "tn")
    return jnp.concatenate([y[e] for e in range(nh)], axis=-1), jnp.concatenate([h1[e] for e in range(nh)], axis=0)


class Rows:
    def __init__(self, arr, w=None, cb=0, shifts=()):
        self.arr, self.w, self.cb, self.shifts = arr, (arr.shape[1] if w is None else w), cb, tuple(shifts)


def _as_rows(x):
    return x if isinstance(x, Rows) else Rows(x)


def _shift_down(x, halo, k):
    rolled = pltpu.roll(x, k, 0)
    first = rolled[0:SUBLANES]
    rid = lax.broadcasted_iota(jnp.int32, first.shape, 0)
    patched = jnp.where(rid < k, pltpu.roll(halo, k, 0), first)
    return jnp.concatenate([patched, rolled[SUBLANES:]], axis=0)


def _shift_up(g, carry, k):
    tm = g.shape[0]
    rolled = pltpu.roll(g, tm - k, 0)
    last = rolled[tm - SUBLANES:]
    rid = lax.broadcasted_iota(jnp.int32, last.shape, 0)
    patched = jnp.where(rid >= SUBLANES - k, pltpu.roll(carry, SUBLANES - k, 0), last)
    return jnp.concatenate([rolled[:tm - SUBLANES], patched], axis=0)


def _params():
    return pltpu.CompilerParams(vmem_limit_bytes=VMEM_LIMIT)


def _load_rows(refs, pos, rins, first_block):
    vals = []
    for r in rins:
        x = refs[pos][...].astype(F32) if refs[pos].dtype != F32 else refs[pos][...]
        pos += 1
        vals.append(x)
        if r.shifts:
            halo = refs[pos][...]
            pos += 1
            halo = jnp.where(first_block, jnp.zeros_like(halo), halo)
            for k in r.shifts:
                vals.append(_shift_down(x, halo, k))
    return vals, pos


def _row_specs(rins, tm, blk):
    specs, args = [], []
    for r in rins:
        specs.append(pl.BlockSpec((tm, r.w), lambda i, cb=r.cb: (blk(i), cb)))
        args.append(r.arr)
        if r.shifts:
            per = tm // SUBLANES
            specs.append(pl.BlockSpec((SUBLANES, r.w), lambda i, cb=r.cb: (jnp.maximum(blk(i) * per - 1, 0), cb)))
            args.append(r.arr)
    return specs, args


def rowwise_fwd(name, fn, rins, params, outs, tm, deps=(), into=None):
    rins = [_as_rows(r) for r in rins]
    t = rins[0].arr.shape[0]
    tm = min(tm, t)
    nb = t // tm
    specs, args = _row_specs(rins, tm, lambda i: i)
    for p in params:
        specs.append(pl.BlockSpec(p.shape, lambda i: (0, 0)))
        args.append(p)
    for dep in deps:
        specs.append(pl.BlockSpec(memory_space=pl.ANY))
        args.append(dep)
    out_specs = [pl.BlockSpec((tm, w), lambda i: (i, 0)) for w, _ in outs]
    out_shape = [jax.ShapeDtypeStruct((t, w), dt) for w, dt in outs]
    aliases = {}
    if into is not None:
        target, total_width, col_block = into
        out_specs = [pl.BlockSpec((tm, outs[0][0]), lambda i: (i, col_block))]
        out_shape = [jax.ShapeDtypeStruct((t, total_width), outs[0][1])]
        if target is not None:
            aliases = {len(args): 0}
            specs.append(pl.BlockSpec(memory_space=pl.ANY))
            args.append(target)
    n_in = len(args)

    def body(*refs):
        vals, pos = _load_rows(refs, 0, rins, pl.program_id(0) == 0)
        pv = [refs[pos + j][...] for j in range(len(params))]
        res = fn(*vals, *pv)
        for o_ref, o in zip(refs[n_in:], res):
            o_ref[...] = o.astype(o_ref.dtype)

    return pl.pallas_call(
        body, name=name, grid=(nb,), in_specs=specs, out_specs=out_specs, out_shape=out_shape,
        input_output_aliases=aliases, compiler_params=_params(),
    )(*args)


def rowwise_bwd(name, fn, rins, params, cts, tm, grad_dtypes, row_add=None):
    rins = [_as_rows(r) for r in rins]
    cts = [[_as_rows(c) for c in lst] for lst in cts]
    row_add = [_as_rows(a) for a in (row_add or [])]
    t = rins[0].arr.shape[0]
    tm = min(tm, t)
    nb = t // tm
    rev = lambda i: nb - 1 - i
    specs, args = _row_specs(rins, tm, rev)
    for p in params:
        specs.append(pl.BlockSpec(p.shape, lambda i: (0, 0)))
        args.append(p)
    flat_cts = [c for lst in cts for c in lst] + row_add
    for c in flat_cts:
        specs.append(pl.BlockSpec((tm, c.w), lambda i, cb=c.cb: (rev(i), cb)))
        args.append(c.arr)
    n_in = len(args)
    want = [i for i, d in enumerate(grad_dtypes) if d is not None]
    out_specs = [pl.BlockSpec((tm, rins[i].w), lambda i_: (rev(i_), 0)) for i in want]
    out_shape = [jax.ShapeDtypeStruct((t, rins[i].w), grad_dtypes[i]) for i in want]
    out_specs += [pl.BlockSpec(p.shape, lambda i: (0, 0)) for p in params]
    out_shape += [jax.ShapeDtypeStruct(p.shape, F32) for p in params]
    n_out = len(out_shape)
    scratch = [pltpu.VMEM((SUBLANES, r.w), F32) for r in rins for _ in r.shifts]

    def body(*refs):
        i = pl.program_id(0)
        vals, pos = _load_rows(refs, 0, rins, rev(i) == 0)
        pv = [refs[pos + j][...] for j in range(len(params))]
        pos += len(params)
        outs, vjp = jax.vjp(fn, *vals, *pv)
        ct_vals = []
        for o, lst in zip(outs, cts):
            acc = None
            for _ in lst:
                cv = refs[pos][...].astype(F32)
                pos += 1
                acc = cv if acc is None else acc + cv
            ct_vals.append(acc.astype(o.dtype))
        adds = [refs[pos + j][...].astype(F32) for j in range(len(row_add))]
        grads = vjp(tuple(ct_vals))
        out_refs = refs[n_in:n_in + n_out]
        carry_refs = refs[n_in + n_out:]

        @pl.when(i == 0)
        def _():
            for cr in carry_refs:
                cr[...] = jnp.zeros_like(cr)
            for pr in out_refs[len(want):]:
                pr[...] = jnp.zeros_like(pr)

        gi, ci, oi = 0, 0, 0
        for idx, r in enumerate(rins):
            d = grads[gi]
            gi += 1
            for k in r.shifts:
                dk = grads[gi]
                gi += 1
                d = d + _shift_up(dk, carry_refs[ci][...], k)
                carry_refs[ci][...] = dk[0:SUBLANES]
                ci += 1
            if idx == 0:
                for a in adds:
                    d = d + a
            if grad_dtypes[idx] is not None:
                out_refs[oi][...] = d.astype(out_refs[oi].dtype)
                oi += 1
        for pr, gp in zip(out_refs[len(want):], grads[gi:]):
            pr[...] += gp

    res = pl.pallas_call(
        body, name=name, grid=(nb,), in_specs=specs, out_specs=out_specs, out_shape=out_shape,
        scratch_shapes=scratch, compiler_params=_params(),
    )(*args)
    return res[:len(want)], res[len(want):]


def _pick(n, pref):
    for c in pref:
        if n % c == 0:
            return c
    return n


MM_VMEM_BUDGET = 40 * 1024 * 1024
MM_PEAK_FLOPS = 0.9e15
MM_HBM_BYTES_PER_S = 3.0e12
MM_STEP_SECONDS = 0.35e-6


def _mm_tiles(m, n, k, size_a, size_b, size_out, size_res, single_k):
    best = None
    for tk in sorted({c for c in (k, 2048, 1024, 512, 256, 128) if c <= 2048 and k % c == 0}, reverse=True):
        for tm in sorted({c for c in (m, 1024, 512, 256, 128) if c <= 1024 and m % c == 0}, reverse=True):
            for tn in sorted({c for c in (n, 2048, 1536, 1024, 768, 512, 384, 256, 128) if c <= 2048 and n % c == 0},
                             reverse=True):
                nk = k // tk
                vmem = 2 * (tm * tk * size_a + tk * tn * size_b + tm * tn * (size_out + size_res))
                vmem += tm * tn * 4 * (2 if nk > 1 or not single_k else 1)
                vmem += (tm * tk * 2 if size_a > 2 else 0) + (tk * tn * 2 if size_b > 2 else 0)
                if vmem > MM_VMEM_BUDGET:
                    continue
                steps = (m // tm) * (n // tn) * nk
                a_reads = 1 if (nk == 1 and single_k) else n // tn
                traffic = m * k * size_a * a_reads + k * n * size_b * (m // tm) + m * n * (size_out + size_res)
                cost = max(2.0 * m * n * k / MM_PEAK_FLOPS, traffic / MM_HBM_BYTES_PER_S) + steps * MM_STEP_SECONDS
                if best is None or cost < best[0]:
                    best = (cost, tm, tn, tk)
    return best[1:]


def mm(name, a, b, mode, out_dtype=F32, res=None, b_slabs=None, out_slabs=None, dep=None, epi=None, extras=(),
       out_dtypes=None):
    if mode == "tn":
        k_dim, m_dim = a.shape
    else:
        m_dim, k_dim = a.shape
    if b_slabs:
        n_dim = b.shape[0] * b.shape[2] if mode == "nn" else b.shape[1]
    else:
        n_dim = b.shape[0] if mode == "nt" else b.shape[1]
    n_slabs = out_slabs or (b_slabs if (b_slabs and mode == "nn") else 1)
    k_slabs = b_slabs if (b_slabs and mode == "nt") else 1
    if epi is None:
        out_dtypes = [out_dtype]
        if res is None:
            epi = lambda acc: (acc,)
        else:
            extras, epi = [res], lambda acc, r: (acc + r,)
    tm, tn, tk = _mm_tiles(m_dim, n_dim // n_slabs, k_dim // k_slabs, a.dtype.itemsize, b.dtype.itemsize,
                           sum(jnp.dtype(dt).itemsize for dt in out_dtypes), sum(e.dtype.itemsize for e in extras),
                           single_k=(k_slabs == 1))
    nji = n_dim // n_slabs // tn
    nki = k_dim // k_slabs // tk
    nblk = lambda js, j: js * nji + j
    kblk = lambda ks, k: ks * nki + k
    if mode == "tn":
        a_spec = pl.BlockSpec((tk, tm), lambda i, js, j, ks, k: (kblk(ks, k), i))
    else:
        a_spec = pl.BlockSpec((tm, tk), lambda i, js, j, ks, k: (i, kblk(ks, k)))
    if b_slabs and mode == "nn":
        b_spec = pl.BlockSpec((None, tk, tn), lambda i, js, j, ks, k: (js, k, j))
    elif b_slabs and mode == "nt":
        b_spec = pl.BlockSpec((None, tn, tk), lambda i, js, j, ks, k: (ks, nblk(js, j), k))
    elif mode == "nt":
        b_spec = pl.BlockSpec((tn, tk), lambda i, js, j, ks, k: (nblk(js, j), kblk(ks, k)))
    else:
        b_spec = pl.BlockSpec((tk, tn), lambda i, js, j, ks, k: (kblk(ks, k), nblk(js, j)))
    specs, args = [a_spec, b_spec], [a, b]
    for e in extras:
        specs.append(pl.BlockSpec((tm, tn), lambda i, js, j, ks, k: (i, nblk(js, j))))
        args.append(e)
    if dep is not None:
        specs.append(pl.BlockSpec(memory_space=pl.ANY))
        args.append(dep)
    if out_slabs:
        o_specs = [pl.BlockSpec((None, tm, tn), lambda i, js, j, ks, k: (js, i, j))]
        o_shapes = [jax.ShapeDtypeStruct((out_slabs, m_dim, n_dim // out_slabs), out_dtypes[0])]
    else:
        o_specs = [pl.BlockSpec((tm, tn), lambda i, js, j, ks, k: (i, nblk(js, j))) for _ in out_dtypes]
        o_shapes = [jax.ShapeDtypeStruct((m_dim, n_dim), dt) for dt in out_dtypes]

    one_k_step = k_slabs * nki == 1
    n_in, n_out = len(args), len(out_dtypes)

    def body(*refs):
        a_ref, b_ref = refs[0], refs[1]
        part = _dg(a_ref[...].astype(BF16), b_ref[...].astype(BF16), mode)

        def finish(acc):
            outs = epi(acc, *[refs[2 + j][...].astype(F32) for j in range(len(extras))])
            for o_ref, o in zip(refs[n_in:n_in + n_out], outs):
                o_ref[...] = o.astype(o_ref.dtype)

        if one_k_step:
            finish(part)
            return
        acc_ref = refs[n_in + n_out]
        ks, kk = pl.program_id(3), pl.program_id(4)

        @pl.when((ks == 0) & (kk == 0))
        def _():
            acc_ref[...] = part

        @pl.when((ks > 0) | (kk > 0))
        def _():
            acc_ref[...] += part

        pl.when((ks == k_slabs - 1) & (kk == nki - 1))(lambda: finish(acc_ref[...]))

    grid = (m_dim // tm, n_slabs, nji, k_slabs, nki)
    scratch = [] if one_k_step else [pltpu.VMEM((tm, tn), F32)]
    out = pl.pallas_call(
        body, name=name, grid=grid, in_specs=specs, out_specs=o_specs, out_shape=o_shapes, scratch_shapes=scratch,
        compiler_params=pltpu.CompilerParams(
            dimension_semantics=("parallel", "parallel", "parallel", "arbitrary", "arbitrary"),
            vmem_limit_bytes=VMEM_LIMIT),
    )(*args)
    return out[0] if n_out == 1 else out


def _stack_lanes(x, n):
    w = x.shape[1] // n
    return jnp.stack([x[:, i * w:(i + 1) * w] for i in range(n)])


def _stack_rows(x, n):
    w = x.shape[0] // n
    return jnp.stack([x[i * w:(i + 1) * w, :] for i in range(n)])


def rwkv_scan_fwd(r, lw, k, v, kap, b):
    t = r.shape[0]
    c, hps, hd = min(RWKV_CHUNK, t), RWKV_HEADS_PER_STEP, RWKV_HEAD_DIM
    nc, ng, wl = t // c, RWKV_HEADS // hps, hps * hd
    spec = pl.BlockSpec((c, wl), lambda g, ci: (ci, g))

    def body(r_ref, lw_ref, k_ref, v_ref, kap_ref, b_ref, y_ref, ck_ref, inv_ref, st_ref):
        @pl.when(pl.program_id(1) == 0)
        def _():
            st_ref[...] = jnp.zeros_like(st_ref)

        st = st_ref[...]
        ck_ref[...] = st
        ins = [x[...] for x in (r_ref, lw_ref, k_ref, v_ref, kap_ref, b_ref)]
        y, st1, inv = rwkv_chunk_fn(_stack_rows(st, hps), *[_stack_lanes(x, hps) for x in ins])
        y_ref[...] = jnp.concatenate([y[h] for h in range(hps)], axis=-1)
        st_ref[...] = jnp.concatenate([st1[h] for h in range(hps)], axis=0)
        inv_ref[...] = jnp.concatenate([inv[h] for h in range(hps)], axis=0)

    return pl.pallas_call(
        body, name="rwkv_scan_fwd", grid=(ng, nc), in_specs=[spec] * 6,
        out_specs=[spec, pl.BlockSpec((None, wl, hd), lambda g, ci: (ci, g, 0)),
                   pl.BlockSpec((None, hps * c, c), lambda g, ci: (ci, g, 0))],
        out_shape=[jax.ShapeDtypeStruct((t, RWKV_WIDTH), F32), jax.ShapeDtypeStruct((nc, RWKV_WIDTH, hd), F32),
                   jax.ShapeDtypeStruct((nc, RWKV_HEADS * c, c), F32)],
        scratch_shapes=[pltpu.VMEM((wl, hd), F32)], compiler_params=_params(),
    )(r, lw, k, v, kap, b)


def rwkv_scan_bwd(r, lw, k, v, kap, b, ck, inv_ck, dy):
    t = r.shape[0]
    c, hps, hd = min(RWKV_CHUNK, t), RWKV_HEADS_PER_STEP, RWKV_HEAD_DIM
    nc, ng, wl = t // c, RWKV_HEADS // hps, hps * hd
    spec = pl.BlockSpec((c, wl), lambda g, ci: (nc - 1 - ci, g))

    def body(r_ref, lw_ref, k_ref, v_ref, kap_ref, b_ref, ck_ref, inv_ref, dy_ref, *rest):
        out_refs, dst_ref = rest[:6], rest[6]

        @pl.when(pl.program_id(1) == 0)
        def _():
            dst_ref[...] = jnp.zeros_like(dst_ref)

        ins = [x[...] for x in (r_ref, lw_ref, k_ref, v_ref, kap_ref, b_ref)]
        dyv, ck, dst = dy_ref[...].astype(F32), ck_ref[...], dst_ref[...]
        chunk = lambda *a: rwkv_chunk_fn(*a, inv=_stack_rows(inv_ref[...], hps))[:2]
        _, vjp = jax.vjp(chunk, _stack_rows(ck, hps), *[_stack_lanes(x, hps) for x in ins])
        grads = vjp((_stack_lanes(dyv, hps), _stack_rows(dst, hps)))
        dst_ref[...] = jnp.concatenate([grads[0][h] for h in range(hps)], axis=0)
        for j in range(6):
            out_refs[j][...] = jnp.concatenate([grads[1 + j][h] for h in range(hps)], axis=-1).astype(BF16)

    return pl.pallas_call(
        body, name="rwkv_scan_bwd", grid=(ng, nc),
        in_specs=[spec] * 6 + [pl.BlockSpec((None, wl, hd), lambda g, ci: (nc - 1 - ci, g, 0)),
                               pl.BlockSpec((None, hps * c, c), lambda g, ci: (nc - 1 - ci, g, 0)), spec],
        out_specs=[spec] * 6, out_shape=[jax.ShapeDtypeStruct((t, RWKV_WIDTH), BF16)] * 6,
        scratch_shapes=[pltpu.VMEM((wl, hd), F32)], compiler_params=_params(),
    )(r, lw, k, v, kap, b, ck, inv_ck, dy)


def _ssd_specs(q, blk):
    gw = SSD_WIDTH // 2
    return [pl.BlockSpec((q, gw), lambda g, ci: (blk(ci), g)),
            pl.BlockSpec((q, SSD_STATE), lambda g, ci: (blk(ci), g)),
            pl.BlockSpec((q, SSD_STATE), lambda g, ci: (blk(ci), g)),
            pl.BlockSpec((q, LANES), lambda g, ci: (blk(ci), 0)),
            pl.BlockSpec((1, LANES), lambda g, ci: (0, 0)),
            pl.BlockSpec((1, LANES), lambda g, ci: (0, 0))]


def ssd_scan_fwd(xs, bm, cm, dt, a_log, d_skip):
    t = xs.shape[0]
    q = min(SSD_CHUNK, t)
    nc, gw = t // q, SSD_WIDTH // 2

    def body(xs_ref, bm_ref, cm_ref, dt_ref, al_ref, d_ref, y_ref, ck_ref, h_ref):
        @pl.when(pl.program_id(1) == 0)
        def _():
            h_ref[...] = jnp.zeros_like(h_ref)

        ck_ref[...] = h_ref[...]
        args = (h_ref[...], xs_ref[...], bm_ref[...], cm_ref[...], dt_ref[...], al_ref[...], d_ref[...])
        g = pl.program_id(0)

        @pl.when(g == 0)
        def _():
            y, h1 = ssd_chunk_fn(0, *args)
            y_ref[...] = y
            h_ref[...] = h1

        @pl.when(g == 1)
        def _():
            y, h1 = ssd_chunk_fn(1, *args)
            y_ref[...] = y
            h_ref[...] = h1

    return pl.pallas_call(
        body, name="ssd_scan_fwd", grid=(2, nc), in_specs=_ssd_specs(q, lambda ci: ci),
        out_specs=[pl.BlockSpec((q, gw), lambda g, ci: (ci, g)),
                   pl.BlockSpec((None, gw, SSD_STATE), lambda g, ci: (ci, g, 0))],
        out_shape=[jax.ShapeDtypeStruct((t, SSD_WIDTH), F32), jax.ShapeDtypeStruct((nc, SSD_WIDTH, SSD_STATE), F32)],
        scratch_shapes=[pltpu.VMEM((gw, SSD_STATE), F32)], compiler_params=_params(),
    )(xs, bm, cm, dt, a_log, d_skip)


def ssd_scan_bwd(xs, bm, cm, dt, a_log, d_skip, ck, dy):
    t = xs.shape[0]
    q = min(SSD_CHUNK, t)
    nc, gw = t // q, SSD_WIDTH // 2
    rev = lambda ci: nc - 1 - ci

    def body(xs_ref, bm_ref, cm_ref, dt_ref, al_ref, d_ref, ck_ref, dy_ref,
             dxs_ref, dbm_ref, dcm_ref, ddt_ref, dal_ref, dd_ref, dh_ref):
        g, ci = pl.program_id(0), pl.program_id(1)

        @pl.when(ci == 0)
        def _():
            dh_ref[...] = jnp.zeros_like(dh_ref)

        @pl.when((ci == 0) & (g == 0))
        def _():
            dal_ref[...] = jnp.zeros_like(dal_ref)
            dd_ref[...] = jnp.zeros_like(dd_ref)

        args = (ck_ref[...], xs_ref[...], bm_ref[...], cm_ref[...], dt_ref[...], al_ref[...], d_ref[...])

        def run(group):
            _, vjp = jax.vjp(functools.partial(ssd_chunk_fn, group), *args)
            dh0, dxs, dbm, dcm, ddt, dal, dd = vjp((dy_ref[...].astype(F32), dh_ref[...]))
            dh_ref[...] = dh0
            dxs_ref[...] = dxs.astype(BF16)
            dbm_ref[...] = dbm.astype(BF16)
            dcm_ref[...] = dcm.astype(BF16)
            ddt_ref[...] = ddt
            dal_ref[...] += dal
            dd_ref[...] += dd

        pl.when(g == 0)(lambda: run(0))
        pl.when(g == 1)(lambda: run(1))

    in_specs = _ssd_specs(q, rev) + [pl.BlockSpec((None, gw, SSD_STATE), lambda g, ci: (rev(ci), g, 0)),
                                     pl.BlockSpec((q, gw), lambda g, ci: (rev(ci), g))]
    return pl.pallas_call(
        body, name="ssd_scan_bwd", grid=(2, nc), in_specs=in_specs,
        out_specs=[pl.BlockSpec((q, gw), lambda g, ci: (rev(ci), g)),
                   pl.BlockSpec((q, SSD_STATE), lambda g, ci: (rev(ci), g)),
                   pl.BlockSpec((q, SSD_STATE), lambda g, ci: (rev(ci), g)),
                   pl.BlockSpec((None, q, LANES), lambda g, ci: (g, rev(ci), 0)),
                   pl.BlockSpec((1, LANES), lambda g, ci: (0, 0)),
                   pl.BlockSpec((1, LANES), lambda g, ci: (0, 0))],
        out_shape=[jax.ShapeDtypeStruct((t, SSD_WIDTH), BF16), jax.ShapeDtypeStruct((t, 2 * SSD_STATE), BF16),
                   jax.ShapeDtypeStruct((t, 2 * SSD_STATE), BF16), jax.ShapeDtypeStruct((2, t, LANES), F32),
                   jax.ShapeDtypeStruct((1, LANES), F32), jax.ShapeDtypeStruct((1, LANES), F32)],
        scratch_shapes=[pltpu.VMEM((gw, SSD_STATE), F32)], compiler_params=_params(),
    )(xs, bm, cm, dt, a_log, d_skip, ck, dy)


def loss_and_grad(x, tgt, g, tm):
    t, d = x.shape
    tm = min(tm, t)
    nb = t // tm

    def body(x_ref, t_ref, g_ref, loss_ref, dx_ref, dxb_ref, dg_ref):
        @pl.when(pl.program_id(0) == 0)
        def _():
            loss_ref[...] = jnp.zeros_like(loss_ref)
            dg_ref[...] = jnp.zeros_like(dg_ref)

        val, vjp = jax.vjp(loss_fn, x_ref[...], t_ref[...], g_ref[...])
        dx, _, dg = vjp(jnp.ones((1, 1), F32))
        loss_ref[...] += jnp.broadcast_to(val, loss_ref.shape)
        dx_ref[...] = dx
        dxb_ref[...] = dx.astype(BF16)
        dg_ref[...] += dg

    row = pl.BlockSpec((tm, d), lambda i: (i, 0))
    one = pl.BlockSpec((1, d), lambda i: (0, 0))
    return pl.pallas_call(
        body, name="loss_and_grad", grid=(nb,), in_specs=[row, row, one],
        out_specs=[pl.BlockSpec((SUBLANES, LANES), lambda i: (0, 0)), row, row, one],
        out_shape=[jax.ShapeDtypeStruct((SUBLANES, LANES), F32), jax.ShapeDtypeStruct((t, d), F32),
                   jax.ShapeDtypeStruct((t, d), BF16), jax.ShapeDtypeStruct((1, d), F32)],
        compiler_params=_params(),
    )(x, tgt, g)


def adamw(name, recv, w, m, v, dep=None):
    rows, cols = w.shape
    n_slabs = recv.shape[0]
    recv_block_bytes = 8 * 1024 * 1024
    tm = _pick(rows, [c for c in (256, 128, 64, 32, 16, 8)
                      if n_slabs * c * cols * recv.dtype.itemsize <= recv_block_bytes and 2 * c <= rows])
    c1 = 1.0 / (1.0 - ADAM_B1 ** ADAM_STEP)
    c2 = 1.0 / (1.0 - ADAM_B2 ** ADAM_STEP)

    n_dep = 0 if dep is None else 1

    def body(recv_ref, w_ref, m_ref, v_ref, *rest):
        g_ref, d_ref, nm_ref, nv_ref = rest[n_dep:]
        g = recv_ref[0].astype(F32)
        for p in range(1, n_slabs):
            g = g + recv_ref[p].astype(F32)
        nm =ADAM_B1 * m_ref[...] + (1.0 - ADAM_B1) * g
        nv = ADAM_B2 * v_ref[...] + (1.0 - ADAM_B2) * jnp.square(g)
        g_ref[...] = g
        nm_ref[...] = nm
        nv_ref[...] = nv
        d_ref[...] = -ADAM_LR * ((nm * c1) / (jnp.sqrt(nv * c2) + ADAM_EPS) + ADAM_WD * w_ref[...])

    blk = pl.BlockSpec((tm, cols), lambda i: (i, 0))
    return pl.pallas_call(
        body, name=name, grid=(rows // tm,),
        in_specs=[pl.BlockSpec((n_slabs, tm, cols), lambda i: (0, i, 0)), blk, blk, blk]
        + [pl.BlockSpec(memory_space=pl.ANY)] * n_dep,
        out_specs=[blk] * 4, out_shape=[jax.ShapeDtypeStruct((rows, cols), F32)] * 4,
        compiler_params=_params(),
    )(recv, w, m, v, *([] if dep is None else [dep]))


def _mesh_pos():
    return lax.axis_index("x"), lax.axis_index("y"), lax.axis_index("c")


def _peer(pos, mask):
    x, y, c = pos
    return (1 - x if mask & 4 else x, 1 - y if mask & 2 else y, 1 - c if mask & 1 else c)


def _linear(pos):
    return 4 * pos[0] + 2 * pos[1] + pos[2]


class Exchange:
    MASKS = {"gather": (1, 2, 3, 4, 5, 6, 7), "scatter": (1, 2, 3, 4, 5, 6, 7), "gather_chips": (1, 2, 4, 6),
             "forward": (2, 4, 6)}

    def __init__(self, xs, kind, lands=None):
        self.kind, self.masks = kind, self.MASKS[kind]
        self.xs = [] if kind == "forward" else list(xs)
        if kind == "forward":
            self.land_shape = [jax.ShapeDtypeStruct(l.shape, l.dtype) for l in lands]
        elif kind == "scatter":
            self.land_shape = [jax.ShapeDtypeStruct(x.shape, x.dtype) for x in xs]
        else:
            self.land_shape = [jax.ShapeDtypeStruct((N_DEV,) + x.shape, x.dtype) for x in xs]
        self.n = len(self.land_shape)
        copies = self.n * len(self.masks)
        self.sems = [pltpu.SemaphoreType.DMA((copies,)), pltpu.SemaphoreType.DMA((copies,)),
                     pltpu.SemaphoreType.DMA((self.n,))]

    def _copies(self, ins, outs, sems, landing):
        send_sems, recv_sems, local_sems = sems
        me = _mesh_pos()
        me_lin = _linear(me)
        local, remote = [], []
        for ti in range(self.n):
            if self.kind != "forward":
                src_mine = ins[ti].at[me_lin] if self.kind == "scatter" else ins[ti]
                local.append(pltpu.make_async_copy(src_mine, outs[ti].at[me_lin], local_sems.at[ti]))
            for j, mask in enumerate(self.masks):
                if self.kind == "forward":
                    peer = _peer(me, 1)
                    src = outs[ti].at[_linear(_peer(me, mask))]
                    dst = outs[ti].at[_linear(_peer(me, mask ^ 1 if landing else mask))]
                else:
                    peer = _peer(me, mask)
                    src = ins[ti].at[_linear(peer)] if self.kind == "scatter" else ins[ti]
                    dst = outs[ti].at[_linear(peer) if landing else me_lin]
                sem_index = ti * len(self.masks) + j
                remote.append(pltpu.make_async_remote_copy(
                    src_ref=src, dst_ref=dst, send_sem=send_sems.at[sem_index], recv_sem=recv_sems.at[sem_index],
                    device_id=peer, device_id_type=pl.DeviceIdType.MESH))
        return local, remote

    def start(self, ins, outs, sems):
        local, remote = self._copies(ins, outs, sems, landing=False)
        for cp in local + remote:
            cp.start()

    def finish(self, ins, outs, sems):
        local, remote = self._copies(ins, outs, sems, landing=True)
        for cp in remote:
            cp.wait_recv()
        for cp in remote:
            cp.wait_send()
        for cp in local:
            cp.wait()


def exchange_start(name, xs, kind, dep=None, lands=None):
    ex = Exchange(xs, kind, lands)
    hbm = pl.BlockSpec(memory_space=pltpu.HBM)
    sem = pl.BlockSpec(memory_space=pltpu.SEMAPHORE)
    if lands is None:
        lands = [lax.empty(s.shape, s.dtype) for s in ex.land_shape]
    n_src, n = len(ex.xs), ex.n
    n_inputs = n_src + n + (0 if dep is None else 1)

    def body(*refs):
        ins, lnd, sems, token = refs[:n_src], refs[n_src:n_src + n], refs[n_inputs:n_inputs + 3], refs[-1]
        ex.start(ins, lnd, sems)
        token[...] = jnp.zeros_like(token)

    res = pl.pallas_call(
        body, name=name, in_specs=[hbm] * (n_src + n) + ([] if dep is None else [pl.BlockSpec(memory_space=pl.ANY)]),
        out_specs=[sem] * 3 + [hbm] * (n_src + n) + [pl.BlockSpec(memory_space=pltpu.VMEM)],
        out_shape=ex.sems + [pltpu.HBM(x.shape, x.dtype) for x in ex.xs]
        + [pltpu.HBM(s.shape, s.dtype) for s in ex.land_shape] + [jax.ShapeDtypeStruct((SUBLANES, LANES), F32)],
        input_output_aliases={i: 3 + i for i in range(n_src + n)},
        compiler_params=pltpu.CompilerParams(has_side_effects=pltpu.SideEffectType.DATAFLOW_SIDE_EFFECTING),
    )(*[pltpu.with_memory_space_constraint(x, pltpu.HBM) for x in ex.xs + list(lands)],
      *([] if dep is None else [dep]))
    return (ex, res[:3], res[3:3 + n_src], res[3 + n_src:3 + n_src + n]), res[-1]


def exchange_wait(name, handles, after):
    ex, sems, srcs, lands = handles
    n_src, n = len(srcs), len(lands)
    hbm = pl.BlockSpec(memory_space=pltpu.HBM)
    sem = pl.BlockSpec(memory_space=pltpu.SEMAPHORE)

    def body(*refs):
        ins, lnd, sem_refs = refs[:n_src], refs[n_src:n_src + n], refs[n_src + n:n_src + n + 3]
        ex.finish(ins, lnd, sem_refs)

    res = pl.pallas_call(
        body, name=name, in_specs=[hbm] * (n_src + n) + [sem] * 3 + [pl.BlockSpec(memory_space=pl.ANY)],
        out_specs=[hbm] * (n_src + n),
        out_shape=[pltpu.HBM(x.shape, x.dtype) for x in srcs] + [pltpu.HBM(x.shape, x.dtype) for x in lands],
        input_output_aliases={i: i for i in range(n_src + n)},
        compiler_params=pltpu.CompilerParams(has_side_effects=pltpu.SideEffectType.DATAFLOW_SIDE_EFFECTING),
    )(*srcs, *lands, *sems, after)
    return res[n_src:]


def forward_start(name, chip_gather, after):
    lands = exchange_wait(name + "_wait", chip_gather, after)
    return exchange_start(name + "_forward_start", [], "forward", lands=lands)


_Z = (0, 1024)
_XBC = (1024, 2560)
_DT = (2560, 2576)
_RKV = (2576, 5648)
_PW = (5648, 5744)
_PA = (5744, 5840)
_PG = (5840, 6096)
D_IN = 6096

_SMALL = ("norm_mix_g", "ssd_conv_b", "ssd_dt_bias", "ssd_a_log", "ssd_d", "ssd_norm_g", "rwkv_mu", "rwkv_w0",
          "rwkv_a0", "rwkv_k_k", "rwkv_k_a", "rwkv_r_k", "rwkv_ln_w", "rwkv_ln_b", "norm_x_g", "norm_mem_g",
          "norm_ffn_g", "final_norm_g")
_WEIGHTS = ("norm_mix_g", "w_in", "ssd_conv_w", "ssd_conv_b", "ssd_dt_bias", "ssd_a_log", "ssd_d", "ssd_norm_g",
            "rwkv_mu", "rwkv_w0", "rwkv_w2", "rwkv_a0", "rwkv_a2", "rwkv_g2", "rwkv_k_k", "rwkv_k_a", "rwkv_r_k",
            "rwkv_ln_w", "rwkv_ln_b", "w_out", "norm_x_g", "norm_mem_g", "xattn_wq", "xattn_wk", "xattn_wv",
            "xattn_wo", "norm_ffn_g", "ffn_w1", "ffn_w2", "final_norm_g")


def _pad_lanes(x, width=LANES):
    return jnp.pad(x, ((0, 0), (0, width - x.shape[1])))


def _pack_small(vals):
    flat = jnp.concatenate([vals[n].reshape(-1) for n in _SMALL])
    rows = -(-flat.shape[0] // (LANES * SUBLANES)) * SUBLANES
    return jnp.pad(flat, (0, rows * LANES - flat.shape[0])).reshape(rows, LANES)


def _unpack_small(packed, shapes):
    flat = packed.reshape(-1)
    out, pos = {}, 0
    for n in _SMALL:
        size = 1
        for s in shapes[n]:
            size *= s
        out[n] = flat[pos:pos + size].reshape(shapes[n])
        pos += size
    return out


def _rows(w, rng):
    return w[rng[0]:rng[1]]


def sum_slabs(name, recv):
    n, rows, cols = recv.shape
    tc = _pick(cols, (256, 128))

    def body(r_ref, o_ref):
        acc = r_ref[0].astype(F32)
        for p in range(1, n):
            acc = acc + r_ref[p].astype(F32)
        o_ref[...] = acc

    return pl.pallas_call(
        body, name=name, grid=(cols // tc,), in_specs=[pl.BlockSpec((n, rows, tc), lambda j: (0, 0, j))],
        out_specs=pl.BlockSpec((rows, tc), lambda j: (0, j)), out_shape=jax.ShapeDtypeStruct((rows, cols), F32),
        compiler_params=_params(),
    )(recv)


def kernel(x, mem, norm_mix_g, w_in, ssd_conv_w, ssd_conv_b, ssd_dt_bias, ssd_a_log, ssd_d, ssd_norm_g, rwkv_mu, rwkv_w0, rwkv_w2, rwkv_a0, rwkv_a2, rwkv_g2, rwkv_k_k, rwkv_k_a, rwkv_r_k, rwkv_ln_w, rwkv_ln_b, w_out, norm_x_g, norm_mem_g, xattn_wq, xattn_wk, xattn_wv, xattn_wo, norm_ffn_g, ffn_w1, ffn_w2, final_norm_g, loss_target, m_norm_mix_g, m_w_in, m_ssd_conv_w, m_ssd_conv_b, m_ssd_dt_bias, m_ssd_a_log, m_ssd_d, m_ssd_norm_g, m_rwkv_mu, m_rwkv_w0, m_rwkv_w2, m_rwkv_a0, m_rwkv_a2, m_rwkv_g2, m_rwkv_k_k, m_rwkv_k_a, m_rwkv_r_k, m_rwkv_ln_w, m_rwkv_ln_b, m_w_out, m_norm_x_g, m_norm_mem_g, m_xattn_wq, m_xattn_wk, m_xattn_wv, m_xattn_wo, m_norm_ffn_g, m_ffn_w1, m_ffn_w2, m_final_norm_g, v_norm_mix_g, v_w_in, v_ssd_conv_w, v_ssd_conv_b, v_ssd_dt_bias, v_ssd_a_log, v_ssd_d, v_ssd_norm_g, v_rwkv_mu, v_rwkv_w0, v_rwkv_w2, v_rwkv_a0, v_rwkv_a2, v_rwkv_g2, v_rwkv_k_k, v_rwkv_k_a, v_rwkv_r_k, v_rwkv_ln_w, v_rwkv_ln_b, v_w_out, v_norm_x_g, v_norm_mem_g, v_xattn_wq, v_xattn_wk, v_xattn_wv, v_xattn_wo, v_norm_ffn_g, v_ffn_w1, v_ffn_w2, v_final_norm_g):
    given = dict(locals())
    wts = {n: given[n] for n in _WEIGHTS}
    mom_m = {n: given["m_" + n] for n in _WEIGHTS}
    mom_v = {n: given["v_" + n] for n in _WEIGHTS}
    d = D_MODEL
    xt, memt, tgt = x[0], mem[0], loss_target[0]
    tm = 256
    tm_light = 512

    big = {"w_in": jnp.transpose(w_in[0]), "w_out": w_out[0], "xattn_wq": xattn_wq[0], "xattn_wk": xattn_wk[0],
           "xattn_wv": xattn_wv[0], "xattn_wo": xattn_wo[0], "ffn_w1": ffn_w1[0], "ffn_w2": ffn_w2[0]}
    small_sh = {"ssd_conv_w": ssd_conv_w.reshape(4, -1), "rwkv_w2": rwkv_w2[0], "rwkv_a2": rwkv_a2[0],
                "rwkv_g2": rwkv_g2[0]}
    cast_one = lambda n, deps=(): rowwise_fwd("cast_" + n, cast_fn, [big[n]], [], [(big[n].shape[1], BF16)],
                                              256 if big[n].shape[0] % 256 == 0 else big[n].shape[0], deps=deps)[0]
    gather_in, token_in = exchange_start("gather_in_start", [cast_one("w_in")] + list(small_sh.values()), "gather_chips")
    cast = {n: cast_one(n, deps=[token_in]) for n in big if n != "w_in"}
    late_a = ("w_out", "xattn_wq", "xattn_wk", "xattn_wv", "xattn_wo")
    late_b = ("ffn_w1", "ffn_w2")
    gather_a, token_a = exchange_start("gather_attn_start", [cast[n] for n in late_a], "gather_chips", dep=token_in)
    gather_b, token_b = exchange_start("gather_ffn_start", [cast[n] for n in late_b], "gather_chips", dep=token_a)
    (h1,) = rowwise_fwd("norm_mix", rmsnorm_fn, [xt], [norm_mix_g], [(d, BF16)], 2 * tm_light, deps=[token_b])
    forward_in, token_in = forward_start("gather_in", gather_in, after=h1)
    gathered = exchange_wait("gather_in_forward_wait", forward_in, after=token_in)
    g_big = {"w_in": gathered[0]}
    g_small = dict(zip(small_sh, gathered[1:]))

    pad_rows = lambda a: jnp.pad(a, ((0, LANES - a.shape[0]), (0, 0)))
    w_in_t = g_big["w_in"].reshape(D_IN, d)
    wt_z, wt_xbc, wt_rkv = (_rows(w_in_t, r) for r in (_Z, _XBC, _RKV))
    wt_ps = jnp.concatenate([_rows(w_in_t, _PG)] + [pad_rows(_rows(w_in_t, r)) for r in (_PW, _PA, _DT)], axis=0)
    unshard_cols = lambda g: jnp.transpose(g, (1, 0, 2)).reshape(g.shape[1], -1)
    conv_w_f = unshard_cols(g_small["ssd_conv_w"])
    w2p, a2p = pad_rows(unshard_cols(g_small["rwkv_w2"])), pad_rows(unshard_cols(g_small["rwkv_a2"]))
    g2_f = unshard_cols(g_small["rwkv_g2"])

    mu = rwkv_mu
    mu_rkv, mu_pg = mu[:, :3072], mu[:, 3264:3520]
    mu_pwa = jnp.concatenate([_pad_lanes(mu[:, 3072:3168]), _pad_lanes(mu[:, 3168:3264])], axis=1)
    dt_bias_p, a_log_p, d_p = _pad_lanes(ssd_dt_bias), _pad_lanes(ssd_a_log), _pad_lanes(ssd_d)
    r_k_row = rwkv_r_k.reshape(1, RWKV_WIDTH)
    g_final = final_norm_g.reshape(1, d)

    u_z = mm("in_z", h1, wt_z, "nt")
    u_xbc = mm("in_xbc", h1, wt_xbc, "nt")
    u_rkv = mm("in_rkv", h1, wt_rkv, "nt")
    u_ps = mm("in_narrow", h1, wt_ps, "nt")

    ssd_pre_rows = lambda: [Rows(u_xbc, shifts=(1, 2, 3)), Rows(u_ps, LANES, 4)]
    ssd_pre_params = [conv_w_f, ssd_conv_b, dt_bias_p]
    xs, bm, cm, dt = rowwise_fwd("ssd_pre", ssd_pre_fn, ssd_pre_rows(), ssd_pre_params,
                                 [(SSD_WIDTH, F32), (256, F32), (256, F32), (LANES, F32)], tm)
    y_scan, ssd_ck = ssd_scan_fwd(xs, bm, cm, dt, a_log_p, d_p)
    (y_ssd,) = rowwise_fwd("ssd_post", ssd_post_fn, [y_scan, u_z], [ssd_norm_g], [(SSD_WIDTH, BF16)], tm_light,
                           into=(None, d, 0))

    rwkv_pre_rows = lambda: [Rows(u_rkv, shifts=(1,)), Rows(u_ps, 2 * LANES, 0, shifts=(1,)), Rows(u_ps, 2 * LANES, 1, shifts=(1,))]
    rwkv_pre_params = [mu_rkv, mu_pg, mu_pwa, rwkv_w0, w2p, rwkv_a0, a2p, g2_f, rwkv_k_k, rwkv_k_a]
    forward_a, token_a = forward_start("gather_attn", gather_a, after=y_scan)
    r_, lw_, k_, v_, kap_, b_, gate_ = rowwise_fwd("rwkv_pre", rwkv_pre_fn, rwkv_pre_rows(), rwkv_pre_params,
                                                   [(RWKV_WIDTH, F32)] * 7, tm, deps=[token_a])
    ys_r, rwkv_ck, rwkv_inv = rwkv_scan_fwd(r_, lw_, k_, v_, kap_, b_)
    forward_b, token_b = forward_start("gather_ffn", gather_b, after=ys_r)
    g_big.update(zip(late_a, exchange_wait("gather_attn_forward_wait", forward_a, after=token_b)))
    w_out_f = g_big["w_out"].reshape(d, d)
    wq_f, wk_f, wv_f, wo_f = (g_big[n].reshape(d, d) for n in ("xattn_wq", "xattn_wk", "xattn_wv", "xattn_wo"))
    rwkv_post_params = [rwkv_ln_w, rwkv_ln_b, r_k_row]
    (ycat,) = rowwise_fwd("rwkv_post", rwkv_post_fn, [ys_r, r_, k_, v_, gate_], rwkv_post_params,
                          [(RWKV_WIDTH, BF16)], tm, into=(y_ssd, d, 1))
    x1 = mm("out_proj", ycat, w_out_f, "nn", res=xt)

    (h2,) = rowwise_fwd("norm_x", rmsnorm_fn, [x1], [norm_x_g], [(d, BF16)], 2 * tm_light)
    (mn,) = rowwise_fwd("norm_mem", rmsnorm_fn, [memt], [norm_mem_g], [(d, BF16)], tm)
    q = mm("xattn_q", h2, wq_f, "nn", out_dtype=BF16)
    kx = mm("xattn_k", mn, wk_f, "nn")
    vx = mm("xattn_v", mn, wv_f, "nn")
    (o,) = rowwise_fwd("xattn", attn_fn, [q], [kx, vx], [(d, BF16)], tm_light)
    x2 = mm("xattn_o", o, wo_f, "nn", res=x1)

    (h3,) = rowwise_fwd("norm_ffn", rmsnorm_fn, [x2], [norm_ffn_g], [(d, BF16)], 2 * tm_light)
    w1_s, w2_g = exchange_wait("gather_ffn_forward_wait", forward_b, after=h3)
    w2_f = w2_g.reshape(D_FF, d)
    relu2_epi = lambda acc: (jnp.square(jnp.maximum(acc, 0.0)), jnp.maximum(acc, 0.0))
    hid, relu_a = mm("ffn_1", h3, w1_s, "nn", b_slabs=N_DEV, epi=relu2_epi, out_dtypes=[BF16, BF16])
    x3 = mm("ffn_2", hid, w2_f, "nn", res=x2)

    loss_blk, dx3, dx3_b, dg_final = loss_and_grad(x3, tgt, g_final, tm_light)

    grads = {}
    grads["ffn_w2"] = mm("d_ffn_w2", hid, dx3_b, "tn", out_dtype=BF16).reshape(N_DEV, D_FF // N_DEV, d)
    sc_w2, tok = exchange_start("scatter_ffn_w2_start", [grads["ffn_w2"]], "scatter")
    da = mm("d_hid", dx3_b, w2_f, "nt", dep=tok, epi=lambda acc, ra: (2.0 * acc * ra,), extras=[relu_a],
            out_dtypes=[BF16])
    grads["ffn_w1"] = mm("d_ffn_w1", h3, da, "tn", out_dtype=BF16, out_slabs=N_DEV)
    sc_w1, tok = exchange_start("scatter_ffn_w1_start", [grads["ffn_w1"]], "scatter")
    dh3 = mm("d_h3", da, w1_s, "nt", out_dtype=BF16, b_slabs=N_DEV, dep=tok)
    (dx2,), (dg_ffn,) = rowwise_bwd("norm_ffn_bwd", rmsnorm_fn, [x2], [norm_ffn_g], [[dh3]], tm_light, [F32], row_add=[dx3])

    grads["xattn_wo"] = mm("d_wo", o, dx2, "tn", out_dtype=BF16).reshape(N_DEV, d // N_DEV, d)
    sc_wo, tok = exchange_start("scatter_wo_start", [grads["xattn_wo"]], "scatter")
    d_o = mm("d_o", dx2, wo_f, "nt", out_dtype=BF16, dep=tok)
    (dq,), (dkx, dvx) = rowwise_bwd("xattn_bwd", attn_fn, [q], [kx, vx], [[d_o]], tm_light, [BF16])
    grads["xattn_wq"] = mm("d_wq", h2, dq, "tn", out_dtype=BF16).reshape(N_DEV, d // N_DEV, d)
    grads["xattn_wk"] = mm("d_wk", mn, dkx, "tn", out_dtype=BF16).reshape(N_DEV, d // N_DEV, d)
    grads["xattn_wv"] = mm("d_wv", mn, dvx, "tn", out_dtype=BF16).reshape(N_DEV, d // N_DEV, d)
    qkv = ("xattn_wq", "xattn_wk", "xattn_wv")
    sc_qkv, tok = exchange_start("scatter_qkv_start", [grads[n] for n in qkv], "scatter")
    dmn = mm("d_mn_v", dvx, wv_f, "nt", res=mm("d_mn_k", dkx, wk_f, "nt", dep=tok))
    _, (dg_mem,) = rowwise_bwd("norm_mem_bwd", rmsnorm_fn, [memt], [norm_mem_g], [[dmn]], tm, [None])
    dh2 = mm("d_h2", dq, wq_f, "nt", out_dtype=BF16, dep=dg_mem)
    (dx1,), (dg_x,) = rowwise_bwd("norm_x_bwd", rmsnorm_fn, [x1], [norm_x_g], [[dh2]], tm_light, [F32], row_add=[dx2])

    grads["w_out"] = mm("d_w_out", ycat, dx1, "tn", out_dtype=BF16).reshape(N_DEV, d // N_DEV, d)
    sc_wout, tok = exchange_start("scatter_w_out_start", [grads["w_out"]], "scatter")
    d_ycat = mm("d_ycat", dx1, w_out_f, "nt", out_dtype=BF16, dep=tok)

    (d_ys, d_r1, d_k1, d_v1, d_gate), (dln_w, dln_b, dr_k) = rowwise_bwd(
        "rwkv_post_bwd", rwkv_post_fn, [ys_r, r_, k_, v_, gate_], rwkv_post_params,
        [[Rows(d_ycat, RWKV_WIDTH, 1)]], tm, [BF16] * 5)
    d_r2, d_lw, d_k2, d_v2, d_kap, d_b = rwkv_scan_bwd(r_, lw_, k_, v_, kap_, b_, rwkv_ck, rwkv_inv, d_ys)
    (du_rkv, du_pg, du_pwa), rwkv_pg = rowwise_bwd(
        "rwkv_pre_bwd", rwkv_pre_fn, rwkv_pre_rows(), rwkv_pre_params,
        [[d_r1, d_r2], [d_lw], [d_k1, d_k2], [d_v1, d_v2], [d_kap], [d_b], [d_gate]], tm, [BF16] * 3)
    dmu_rkv, dmu_pg, dmu_pwa, dw0, dw2p, da0, da2p, dg2, dk_k, dk_a = rwkv_pg

    (d_yscan, du_z), (dssd_norm_g,) = rowwise_bwd("ssd_post_bwd", ssd_post_fn, [y_scan, u_z], [ssd_norm_g],
                                                  [[Rows(d_ycat, SSD_WIDTH, 0)]], tm_light, [BF16, BF16])
    dxs, dbm, dcm, ddt2, da_log_p, dd_p = ssd_scan_bwd(xs, bm, cm, dt, a_log_p, d_p, ssd_ck, d_yscan)
    (du_xbc, du_dt), (dconv_w, dconv_b, ddt_bias_p) = rowwise_bwd(
        "ssd_pre_bwd", ssd_pre_fn, ssd_pre_rows(), ssd_pre_params,
        [[dxs], [dbm], [dcm], [ddt2[0], ddt2[1]]], tm, [BF16, BF16])
    du_ps = jnp.concatenate([du_pg, du_pwa, du_dt], axis=1)

    dwt_z = mm("d_w_z", du_z, h1, "tn", out_dtype=BF16)
    dwt_xbc = mm("d_w_xbc", du_xbc, h1, "tn", out_dtype=BF16)
    dwt_rkv = mm("d_w_rkv", du_rkv, h1, "tn", out_dtype=BF16)
    dwt_ps = mm("d_w_narrow", du_ps, h1, "tn", out_dtype=BF16)
    dwt_full = jnp.concatenate([dwt_z, dwt_xbc, dwt_ps[512:528], dwt_rkv, dwt_ps[256:352], dwt_ps[384:480], dwt_ps[0:256]],
                               axis=0)
    to_slabs = lambda g: jnp.transpose(g.reshape(g.shape[0], N_DEV, -1), (1, 0, 2))
    grads["w_in"] = dwt_full.reshape(N_DEV, D_IN // N_DEV, d)
    grads["ssd_conv_w"] = to_slabs(dconv_w)
    grads["rwkv_w2"] = to_slabs(dw2p[:96])
    grads["rwkv_a2"] = to_slabs(da2p[:96])
    grads["rwkv_g2"] = to_slabs(dg2)
    tail = ("w_in", "ssd_conv_w", "rwkv_w2", "rwkv_a2", "rwkv_g2")
    sc_tail, tok = exchange_start("scatter_tail_start", [grads[n] for n in tail], "scatter")
    dh1 = mm("d_h1_z", du_z, wt_z, "nn", dep=tok)
    dh1 = mm("d_h1_xbc", du_xbc, wt_xbc, "nn", res=dh1)
    dh1 = mm("d_h1_rkv", du_rkv, wt_rkv, "nn", res=dh1)
    dh1 = mm("d_h1_narrow", du_ps, wt_ps, "nn", res=dh1)
    (dx,), (dg_mix,) = rowwise_bwd("norm_mix_bwd", rmsnorm_fn, [xt], [norm_mix_g], [[dh1]], tm_light, [F32], row_add=[dx1])

    dmu =jnp.concatenate([dmu_rkv, dmu_pwa[:, 0:96], dmu_pwa[:, 128:224], dmu_pg], axis=1)
    small_grads = {
        "norm_mix_g": dg_mix, "ssd_conv_b": dconv_b, "ssd_dt_bias": ddt_bias_p[:, :16], "ssd_a_log": da_log_p[:, :16],
        "ssd_d": dd_p[:, :16], "ssd_norm_g": dssd_norm_g, "rwkv_mu": dmu, "rwkv_w0": dw0, "rwkv_a0": da0,
        "rwkv_k_k": dk_k, "rwkv_k_a": dk_a, "rwkv_r_k": dr_k, "rwkv_ln_w": dln_w, "rwkv_ln_b": dln_b,
        "norm_x_g": dg_x, "norm_mem_g": dg_mem, "norm_ffn_g": dg_ffn, "final_norm_g": dg_final}

    gather_small, tok = exchange_start("gather_small_start", [_pack_small(small_grads)], "gather")
    received = {}
    for names, handle in ((("ffn_w2",), sc_w2), (("ffn_w1",), sc_w1), (("xattn_wo",), sc_wo), (qkv, sc_qkv),
                          (("w_out",), sc_wout)):
        received.update(zip(names, exchange_wait("scatter_" + names[0] + "_wait", handle, after=tok)))

    out_g, out_d, out_m, out_v = {}, {}, {}, {}

    def run_adamw(n, dep):
        shape = wts[n].shape
        two_d = lambda a: a.reshape(-1, shape[-1])
        if n == "w_in":
            recv = jnp.transpose(sum_slabs("sum_w_in", received[n]))[None]
        else:
            recv = received[n].reshape(N_DEV, -1, shape[-1])
        res = adamw("adamw_" + n, recv, two_d(wts[n]), two_d(mom_m[n]), two_d(mom_v[n]), dep=dep)
        out_g[n], out_d[n], out_m[n], out_v[n] = (r.reshape(shape) for r in res)
        return res[0]

    last = None
    for n in ("ffn_w2", "ffn_w1", "xattn_wo") + qkv + ("w_out",):
        last = run_adamw(n, last)
    received.update(zip(tail, exchange_wait("scatter_tail_wait", sc_tail, after=last)))
    for n in tail:
        last = run_adamw(n, last)
    (small_all,) = exchange_wait("gather_small_wait", gather_small, after=last)
    res = adamw("adamw_small", small_all, _pack_small(wts), _pack_small(mom_m), _pack_small(mom_v))
    shapes = {n: wts[n].shape for n in _SMALL}
    for dst, packed in zip((out_g, out_d, out_m, out_v), res):
        dst.update(_unpack_small(packed, shapes))

    loss = lax.psum(loss_blk[0, 0], ("x", "y", "c"))
    return (loss, dx[None], *[out_g[n] for n in _WEIGHTS], *[out_d[n] for n in _WEIGHTS],
            *[out_m[n] for n in _WEIGHTS], *[out_v[n] for n in _WEIGHTS])
```
